```python
import jax, jax.numpy as jnp
from jax import lax
import numpy as np

D_MODEL = 1024
BATCH = 8
SEQ = 8192
DEPTH = 1

D_INNER = 2 * D_MODEL
SSD_HEAD_DIM = 64
SSD_HEADS = D_INNER // SSD_HEAD_DIM
SSD_GROUPS = 4
SSD_HEADS_PER_GROUP = SSD_HEADS // SSD_GROUPS
SSD_STATE = 128
CONV_WIDTH = 4
CHUNK = 128
XBC_WIDTH = D_INNER + 2 * SSD_GROUPS * SSD_STATE
DT_MIN = 0.001
DT_MAX = 0.1
POOL_WINDOWS = (2, 4, 8, 16)
POOL_GROUPS = len(POOL_WINDOWS)
POOL_WIDTH = D_MODEL
POOL_GROUP_WIDTH = POOL_WIDTH // POOL_GROUPS
N_BRANCHES = 2
D_FF = 4 * D_MODEL
N_MOD = 6
NORM_EPS = 1e-5
IN_SPLITS = (D_INNER,
             D_INNER + XBC_WIDTH,
             D_INNER + XBC_WIDTH + SSD_HEADS,
             D_INNER + XBC_WIDTH + SSD_HEADS + POOL_WIDTH)
IN_COLS = IN_SPLITS[-1] + N_BRANCHES * D_MODEL

kernel_name = "hybrid_ssd_pool_gated_block"


def rmsnorm(x, w, eps=NORM_EPS):
    x32 = x.astype(jnp.float32)
    y = x32 * lax.rsqrt(jnp.mean(x32 * x32, axis=-1, keepdims=True) + eps)
    return y.astype(x.dtype) * w


def causal_depthwise_conv(u, w, b):
    k_width = w.shape[0]
    seq = u.shape[1]
    up = jnp.pad(u, ((0, 0), (k_width - 1, 0), (0, 0)))
    out = b
    for k in range(k_width):
        out = out + up[:, k:k + seq] * w[k]
    return out


def segsum_decay(a_cs):
    q = a_cs.shape[-1]
    seg = a_cs[..., :, None] - a_cs[..., None, :]
    mask = jnp.tril(jnp.ones((q, q), dtype=bool))
    return jnp.exp(jnp.where(mask, seg, -jnp.inf))


def ssd_chunked_scan(xdt, dtA, bmat, cmat):
    b, seq, _, p = xdt.shape
    g, hg, n, q = SSD_GROUPS, SSD_HEADS_PER_GROUP, SSD_STATE, CHUNK
    nc = seq // q
    x = xdt.astype(jnp.float32).reshape(b, nc, q, g, hg, p)
    a = dtA.astype(jnp.float32).reshape(b, nc, q, g, hg).transpose(0, 3, 4, 1, 2)
    bc = bmat.astype(jnp.float32).reshape(b, nc, q, g, n)
    cc = cmat.astype(jnp.float32).reshape(b, nc, q, g, n)
    a_cs = jnp.cumsum(a, axis=-1)
    scores = jnp.einsum('bclgn,bcsgn->bgcls', cc, bc)
    mmat = scores[:, :, None] * segsum_decay(a_cs)
    y_diag = jnp.einsum('bghcls,bcsghp->bclghp', mmat, x)
    decay_states = jnp.exp(a_cs[..., -1:] - a_cs)
    states = jnp.einsum('bcsgn,bghcs,bcsghp->bcghpn', bc, decay_states, x)
    chunk_decay = jnp.exp(a_cs[..., -1])

    def step(h, inp):
        s_c, d_c = inp
        return h * d_c[..., None, None] + s_c, h

    h0 = jnp.zeros((b, g, hg, p, n), jnp.float32)
    _, prev = lax.scan(step, h0, (jnp.moveaxis(states, 1, 0), jnp.moveaxis(chunk_decay, 3, 0)))
    prev = jnp.moveaxis(prev, 0, 1)
    y_off = jnp.einsum('bclgn,bcghpn,bghcl->bclghp', cc, prev, jnp.exp(a_cs))
    return (y_diag + y_off).reshape(b, seq, SSD_HEADS, p)


def causal_multiscale_pool(u, pool_w, pool_scale):
    b, seq, _ = u.shape
    gw = POOL_GROUP_WIDTH
    u32 = u.astype(jnp.float32)
    cs = jnp.concatenate([jnp.zeros((b, 1, POOL_WIDTH), jnp.float32), jnp.cumsum(u32, axis=1)], axis=1)
    count = jnp.arange(1, seq + 1, dtype=jnp.float32)[:, None]
    outs = []
    for gi, win in enumerate(POOL_WINDOWS):
        csg = cs[:, :, gi * gw:(gi + 1) * gw]
        start = jnp.concatenate([jnp.zeros((b, win - 1, gw), jnp.float32), csg[:, :seq - win + 1]], axis=1)
        mean = (csg[:, 1:] - start) / jnp.minimum(count, float(win))
        outs.append(mean - u32[..., gi * gw:(gi + 1) * gw])
    pooled = jnp.stack(outs, axis=2).astype(u.dtype)
    y = jnp.einsum('blgc,gcd->blgd', pooled, pool_w).reshape(b, seq, POOL_WIDTH)
    return y * pool_scale


def hybrid_mixer(h, w_in, conv_w, conv_b, dt_bias, a_log, d_skip, ssd_norm_w,
                 w_branch_ssd, pool_w, pool_scale, w_branch_pool, w_out):
    b, seq, _ = h.shape
    proj = h @ w_in
    z, xbc, dt_raw, u_pool, gate_logits = jnp.split(proj, IN_SPLITS, axis=-1)
    xbc = jax.nn.silu(causal_depthwise_conv(xbc, conv_w, conv_b))
    xs, bmat, cmat = jnp.split(xbc, (D_INNER, D_INNER + SSD_GROUPS * SSD_STATE), axis=-1)
    xs = xs.reshape(b, seq, SSD_HEADS, SSD_HEAD_DIM)
    bmat = bmat.reshape(b, seq, SSD_GROUPS, SSD_STATE)
    cmat = cmat.reshape(b, seq, SSD_GROUPS, SSD_STATE)
    dt = jax.nn.softplus(dt_raw.astype(jnp.float32) + dt_bias.astype(jnp.float32))
    a_cont = -jnp.exp(a_log.astype(jnp.float32))
    x32 = xs.astype(jnp.float32)
    y = ssd_chunked_scan(x32 * dt[..., None], dt * a_cont, bmat, cmat)
    y = y + d_skip.astype(jnp.float32)[:, None] * x32
    y = y.reshape(b, seq, D_INNER).astype(h.dtype) * jax.nn.silu(z)
    y = rmsnorm(y.reshape(b, seq, SSD_GROUPS, D_INNER // SSD_GROUPS),
                ssd_norm_w.reshape(SSD_GROUPS, D_INNER // SSD_GROUPS))
    y_ssd = y.reshape(b, seq, D_INNER) @ w_branch_ssd
    y_pool = causal_multiscale_pool(u_pool, pool_w, pool_scale) @ w_branch_pool
    g_ssd, g_pool = jnp.split(jax.nn.sigmoid(gate_logits), N_BRANCHES, axis=-1)
    return (g_ssd * y_ssd + g_pool * y_pool) @ w_out


def _fwd_setup_inputs(seed: int = 0) -> dict:
    key = jax.random.key(seed)
    ks = jax.random.split(key, 24)
    f32 = jnp.float32
    nrm = lambda k, shape, s: jax.random.normal(k, shape, f32) * s
    dt0 = jnp.exp(jax.random.uniform(ks[6], (DEPTH, SSD_HEADS), f32)
                  * (np.log(DT_MAX) - np.log(DT_MIN)) + np.log(DT_MIN))
    return {
        "x": nrm(ks[0], (BATCH, SEQ, D_MODEL), 1.0),
        "c": nrm(ks[1], (BATCH, D_MODEL), 1.0),
        "w_ada": nrm(ks[2], (DEPTH, D_MODEL, N_MOD * D_MODEL), D_MODEL ** -0.5),
        "b_ada": nrm(ks[3], (DEPTH, N_MOD * D_MODEL), 0.01),
        "norm_mix_w": 1.0 + nrm(ks[4], (DEPTH, D_MODEL), 0.05),
        "w_in": nrm(ks[5], (DEPTH, D_MODEL, IN_COLS), D_MODEL ** -0.5),
        "conv_w": nrm(ks[7], (DEPTH, CONV_WIDTH, XBC_WIDTH), CONV_WIDTH ** -0.5),
        "conv_b": nrm(ks[8], (DEPTH, XBC_WIDTH), 0.01),
        "dt_bias": dt0 + jnp.log(-jnp.expm1(-dt0)),
        "a_log": jnp.log(jax.random.uniform(ks[9], (DEPTH, SSD_HEADS), f32, 1.0, 16.0)),
        "d_skip": 1.0 + nrm(ks[10], (DEPTH, SSD_HEADS), 0.1),
        "ssd_norm_w": 1.0 + nrm(ks[11], (DEPTH, D_INNER), 0.05),
        "w_branch_ssd": nrm(ks[12], (DEPTH, D_INNER, D_MODEL), D_INNER ** -0.5),
        "pool_w": nrm(ks[13], (DEPTH, POOL_GROUPS, POOL_GROUP_WIDTH, POOL_GROUP_WIDTH), POOL_GROUP_WIDTH ** -0.5),
        "pool_scale": 1.0 + nrm(ks[14], (DEPTH, POOL_WIDTH), 0.1),
        "w_branch_pool": nrm(ks[15], (DEPTH, POOL_WIDTH, D_MODEL), POOL_WIDTH ** -0.5),
        "w_out": nrm(ks[16], (DEPTH, D_MODEL, D_MODEL), D_MODEL ** -0.5),
        "norm_mlp_w": 1.0 + nrm(ks[17], (DEPTH, D_MODEL), 0.05),
        "w_up": nrm(ks[18], (DEPTH, D_MODEL, D_FF), D_MODEL ** -0.5),
        "w_down": nrm(ks[19], (DEPTH, D_FF, D_MODEL), D_FF ** -0.5),
        "norm_final_w": 1.0 + nrm(ks[20], (D_MODEL,), 0.05),
    }


def _fwd_reference(x, c, w_ada, b_ada, norm_mix_w, w_in, conv_w, conv_b, dt_bias, a_log, d_skip,
              ssd_norm_w, w_branch_ssd, pool_w, pool_scale, w_branch_pool, w_out,
              norm_mlp_w, w_up, w_down, norm_final_w):
    for i in range(DEPTH):
        mod = (jax.nn.silu(c) @ w_ada[i] + b_ada[i])[:, None, :]
        shift_m, scale_m, gate_m, shift_f, scale_f, gate_f = jnp.split(mod, N_MOD, axis=-1)
        h = rmsnorm(x, norm_mix_w[i]) * (1.0 + scale_m) + shift_m
        x = x + gate_m * hybrid_mixer(h, w_in[i], conv_w[i], conv_b[i], dt_bias[i], a_log[i],
                                      d_skip[i], ssd_norm_w[i], w_branch_ssd[i], pool_w[i],
                                      pool_scale[i], w_branch_pool[i], w_out[i])
        h = rmsnorm(x, norm_mlp_w[i]) * (1.0 + scale_f) + shift_f
        x = x + gate_f * (jnp.square(jax.nn.relu(h @ w_up[i])) @ w_down[i])
    return rmsnorm(x, norm_final_w)


import jax as _jax
import jax.numpy as _jnp

TWIN_FORMAT = 'train_step'
FWD_PARAMS = ['x', 'c', 'w_ada', 'b_ada', 'norm_mix_w', 'w_in', 'conv_w', 'conv_b', 'dt_bias', 'a_log', 'd_skip', 'ssd_norm_w', 'w_branch_ssd', 'pool_w', 'pool_scale', 'w_branch_pool', 'w_out', 'norm_mlp_w', 'w_up', 'w_down', 'norm_final_w']
TWIN_WEIGHTS = ['w_ada', 'b_ada', 'norm_mix_w', 'w_in', 'conv_w', 'conv_b', 'dt_bias', 'a_log', 'd_skip', 'ssd_norm_w', 'w_branch_ssd', 'pool_w', 'pool_scale', 'w_branch_pool', 'w_out', 'norm_mlp_w', 'w_up', 'w_down', 'norm_final_w']
TWIN_DIFF_INPUT = 'x'
TWIN_INPUTS = ['x', 'c', 'w_ada', 'b_ada', 'norm_mix_w', 'w_in', 'conv_w', 'conv_b', 'dt_bias', 'a_log', 'd_skip', 'ssd_norm_w', 'w_branch_ssd', 'pool_w', 'pool_scale', 'w_branch_pool', 'w_out', 'norm_mlp_w', 'w_up', 'w_down', 'norm_final_w', 'loss_target', 'm_w_ada', 'm_b_ada', 'm_norm_mix_w', 'm_w_in', 'm_conv_w', 'm_conv_b', 'm_dt_bias', 'm_a_log', 'm_d_skip', 'm_ssd_norm_w', 'm_w_branch_ssd', 'm_pool_w', 'm_pool_scale', 'm_w_branch_pool', 'm_w_out', 'm_norm_mlp_w', 'm_w_up', 'm_w_down', 'm_norm_final_w', 'v_w_ada', 'v_b_ada', 'v_norm_mix_w', 'v_w_in', 'v_conv_w', 'v_conv_b', 'v_dt_bias', 'v_a_log', 'v_d_skip', 'v_ssd_norm_w', 'v_w_branch_ssd', 'v_pool_w', 'v_pool_scale', 'v_w_branch_pool', 'v_w_out', 'v_norm_mlp_w', 'v_w_up', 'v_w_down', 'v_norm_final_w']
TWIN_OUTPUTS = ['loss', 'grad_x', 'grad_w_ada', 'grad_b_ada', 'grad_norm_mix_w', 'grad_w_in', 'grad_conv_w', 'grad_conv_b', 'grad_dt_bias', 'grad_a_log', 'grad_d_skip', 'grad_ssd_norm_w', 'grad_w_branch_ssd', 'grad_pool_w', 'grad_pool_scale', 'grad_w_branch_pool', 'grad_w_out', 'grad_norm_mlp_w', 'grad_w_up', 'grad_w_down', 'grad_norm_final_w', 'delta_w_ada', 'delta_b_ada', 'delta_norm_mix_w', 'delta_w_in', 'delta_conv_w', 'delta_conv_b', 'delta_dt_bias', 'delta_a_log', 'delta_d_skip', 'delta_ssd_norm_w', 'delta_w_branch_ssd', 'delta_pool_w', 'delta_pool_scale', 'delta_w_branch_pool', 'delta_w_out', 'delta_norm_mlp_w', 'delta_w_up', 'delta_w_down', 'delta_norm_final_w', 'new_m_w_ada', 'new_m_b_ada', 'new_m_norm_mix_w', 'new_m_w_in', 'new_m_conv_w', 'new_m_conv_b', 'new_m_dt_bias', 'new_m_a_log', 'new_m_d_skip', 'new_m_ssd_norm_w', 'new_m_w_branch_ssd', 'new_m_pool_w', 'new_m_pool_scale', 'new_m_w_branch_pool', 'new_m_w_out', 'new_m_norm_mlp_w', 'new_m_w_up', 'new_m_w_down', 'new_m_norm_final_w', 'new_v_w_ada', 'new_v_b_ada', 'new_v_norm_mix_w', 'new_v_w_in', 'new_v_conv_w', 'new_v_conv_b', 'new_v_dt_bias', 'new_v_a_log', 'new_v_d_skip', 'new_v_ssd_norm_w', 'new_v_w_branch_ssd', 'new_v_pool_w', 'new_v_pool_scale', 'new_v_w_branch_pool', 'new_v_w_out', 'new_v_norm_mlp_w', 'new_v_w_up', 'new_v_w_down', 'new_v_norm_final_w']
TWIN_LEAF_KINDS = {'loss': 'loss', 'grad_x': 'grad_x', 'grad_w_ada': 'grad_w', 'grad_b_ada': 'grad_w', 'grad_norm_mix_w': 'grad_w', 'grad_w_in': 'grad_w', 'grad_conv_w': 'grad_w', 'grad_conv_b': 'grad_w', 'grad_dt_bias': 'grad_w', 'grad_a_log': 'grad_w', 'grad_d_skip': 'grad_w', 'grad_ssd_norm_w': 'grad_w', 'grad_w_branch_ssd': 'grad_w', 'grad_pool_w': 'grad_w', 'grad_pool_scale': 'grad_w', 'grad_w_branch_pool': 'grad_w', 'grad_w_out': 'grad_w', 'grad_norm_mlp_w': 'grad_w', 'grad_w_up': 'grad_w', 'grad_w_down': 'grad_w', 'grad_norm_final_w': 'grad_w', 'delta_w_ada': 'delta_w', 'delta_b_ada': 'delta_w', 'delta_norm_mix_w': 'delta_w', 'delta_w_in': 'delta_w', 'delta_conv_w': 'delta_w', 'delta_conv_b': 'delta_w', 'delta_dt_bias': 'delta_w', 'delta_a_log': 'delta_w', 'delta_d_skip': 'delta_w', 'delta_ssd_norm_w': 'delta_w', 'delta_w_branch_ssd': 'delta_w', 'delta_pool_w': 'delta_w', 'delta_pool_scale': 'delta_w', 'delta_w_branch_pool': 'delta_w', 'delta_w_out': 'delta_w', 'delta_norm_mlp_w': 'delta_w', 'delta_w_up': 'delta_w', 'delta_w_down': 'delta_w', 'delta_norm_final_w': 'delta_w', 'new_m_w_ada': 'new_m', 'new_m_b_ada': 'new_m', 'new_m_norm_mix_w': 'new_m', 'new_m_w_in': 'new_m', 'new_m_conv_w': 'new_m', 'new_m_conv_b': 'new_m', 'new_m_dt_bias': 'new_m', 'new_m_a_log': 'new_m', 'new_m_d_skip': 'new_m', 'new_m_ssd_norm_w': 'new_m', 'new_m_w_branch_ssd': 'new_m', 'new_m_pool_w': 'new_m', 'new_m_pool_scale': 'new_m', 'new_m_w_branch_pool': 'new_m', 'new_m_w_out': 'new_m', 'new_m_norm_mlp_w': 'new_m', 'new_m_w_up': 'new_m', 'new_m_w_down': 'new_m', 'new_m_norm_final_w': 'new_m', 'new_v_w_ada': 'new_v', 'new_v_b_ada': 'new_v', 'new_v_norm_mix_w': 'new_v', 'new_v_w_in': 'new_v', 'new_v_conv_w': 'new_v', 'new_v_conv_b': 'new_v', 'new_v_dt_bias': 'new_v', 'new_v_a_log': 'new_v', 'new_v_d_skip': 'new_v', 'new_v_ssd_norm_w': 'new_v', 'new_v_w_branch_ssd': 'new_v', 'new_v_pool_w': 'new_v', 'new_v_pool_scale': 'new_v', 'new_v_w_branch_pool': 'new_v', 'new_v_w_out': 'new_v', 'new_v_norm_mlp_w': 'new_v', 'new_v_w_up': 'new_v', 'new_v_w_down': 'new_v', 'new_v_norm_final_w': 'new_v'}


def _forward(args):
    return _fwd_reference(*[args[k] for k in FWD_PARAMS])


def _output_shape():
    def fwd():
        inp = _fwd_setup_inputs(0)
        return _fwd_reference(*[inp[k] for k in FWD_PARAMS])
    out = _jax.eval_shape(fwd)
    return out.shape, out.dtype

N_MICROBATCH = 1
ADAM_LR = 0.001
ADAM_B1 = 0.9
ADAM_B2 = 0.999
ADAM_EPS = 1e-08
ADAM_WD = 0.01
ADAM_STEP = 10
PER_EXAMPLE_BATCH_AXIS = {'x': 0, 'c': 0, 'loss_target': 0}
SHARED_INPUTS = []
_WEIGHT_DTYPES = {'w_ada': _jnp.float32, 'b_ada': _jnp.float32, 'norm_mix_w': _jnp.float32, 'w_in': _jnp.float32, 'conv_w': _jnp.float32, 'conv_b': _jnp.float32, 'dt_bias': _jnp.float32, 'a_log': _jnp.float32, 'd_skip': _jnp.float32, 'ssd_norm_w': _jnp.float32, 'w_branch_ssd': _jnp.float32, 'pool_w': _jnp.float32, 'pool_scale': _jnp.float32, 'w_branch_pool': _jnp.float32, 'w_out': _jnp.float32, 'norm_mlp_w': _jnp.float32, 'w_up': _jnp.float32, 'w_down': _jnp.float32, 'norm_final_w': _jnp.float32}
MOMENT_SCALE = {'w_ada': 4.666255e-01, 'b_ada': 1.000011e+00, 'norm_mix_w': 1.490926e-01, 'w_in': 5.944310e-02, 'conv_w': 5.860613e-02, 'conv_b': 6.294130e-02, 'dt_bias': 1.723339e-01, 'a_log': 1.756847e-01, 'd_skip': 2.217905e-01, 'ssd_norm_w': 7.301470e-02, 'w_branch_ssd': 9.977388e-02, 'pool_w': 8.184483e-02, 'pool_scale': 8.191187e-02, 'w_branch_pool': 8.263643e-02, 'w_out': 1.308863e-01, 'norm_mlp_w': 2.112796e-01, 'w_up': 1.697291e-01, 'w_down': 6.316476e-01, 'norm_final_w': 6.872228e+01}


def _to_microbatches(a, axis):
    t = _jnp.moveaxis(a, axis, 0)
    t = t.reshape((N_MICROBATCH, t.shape[0] // N_MICROBATCH) + t.shape[1:])
    return _jnp.moveaxis(t, 1, axis + 1)


def setup_inputs(seed: int = 0) -> dict:
    inp = _fwd_setup_inputs(seed)
    key = _jax.random.fold_in(_jax.random.key(seed), 7919)
    shape, _ = _output_shape()
    out = dict(inp)
    out["loss_target"] = _jax.random.normal(_jax.random.fold_in(key, 0), shape, _jnp.float32)
    for i, name in enumerate(TWIN_WEIGHTS):
        w = inp[name].astype(_jnp.float32)
        if MOMENT_SCALE is None:
            s = _jnp.sqrt(_jnp.mean(_jnp.square(w)) + 1e-30)
        else:
            s = MOMENT_SCALE[name]
        km, kv = _jax.random.split(_jax.random.fold_in(key, i + 1))
        out[name] = w
        out["m_" + name] = s * _jax.random.normal(km, w.shape, _jnp.float32)
        out["v_" + name] = (s * s) * _jax.random.uniform(kv, w.shape, _jnp.float32, 0.5, 1.5)
    if N_MICROBATCH > 1:
        for name, axis in PER_EXAMPLE_BATCH_AXIS.items():
            out[name] = _to_microbatches(out[name], axis)
    return {'x': out['x'], 'c': out['c'], 'w_ada': out['w_ada'], 'b_ada': out['b_ada'], 'norm_mix_w': out['norm_mix_w'], 'w_in': out['w_in'], 'conv_w': out['conv_w'], 'conv_b': out['conv_b'], 'dt_bias': out['dt_bias'], 'a_log': out['a_log'], 'd_skip': out['d_skip'], 'ssd_norm_w': out['ssd_norm_w'], 'w_branch_ssd': out['w_branch_ssd'], 'pool_w': out['pool_w'], 'pool_scale': out['pool_scale'], 'w_branch_pool': out['w_branch_pool'], 'w_out': out['w_out'], 'norm_mlp_w': out['norm_mlp_w'], 'w_up': out['w_up'], 'w_down': out['w_down'], 'norm_final_w': out['norm_final_w'], 'loss_target': out['loss_target'], 'm_w_ada': out['m_w_ada'], 'm_b_ada': out['m_b_ada'], 'm_norm_mix_w': out['m_norm_mix_w'], 'm_w_in': out['m_w_in'], 'm_conv_w': out['m_conv_w'], 'm_conv_b': out['m_conv_b'], 'm_dt_bias': out['m_dt_bias'], 'm_a_log': out['m_a_log'], 'm_d_skip': out['m_d_skip'], 'm_ssd_norm_w': out['m_ssd_norm_w'], 'm_w_branch_ssd': out['m_w_branch_ssd'], 'm_pool_w': out['m_pool_w'], 'm_pool_scale': out['m_pool_scale'], 'm_w_branch_pool': out['m_w_branch_pool'], 'm_w_out': out['m_w_out'], 'm_norm_mlp_w': out['m_norm_mlp_w'], 'm_w_up': out['m_w_up'], 'm_w_down': out['m_w_down'], 'm_norm_final_w': out['m_norm_final_w'], 'v_w_ada': out['v_w_ada'], 'v_b_ada': out['v_b_ada'], 'v_norm_mix_w': out['v_norm_mix_w'], 'v_w_in': out['v_w_in'], 'v_conv_w': out['v_conv_w'], 'v_conv_b': out['v_conv_b'], 'v_dt_bias': out['v_dt_bias'], 'v_a_log': out['v_a_log'], 'v_d_skip': out['v_d_skip'], 'v_ssd_norm_w': out['v_ssd_norm_w'], 'v_w_branch_ssd': out['v_w_branch_ssd'], 'v_pool_w': out['v_pool_w'], 'v_pool_scale': out['v_pool_scale'], 'v_w_branch_pool': out['v_w_branch_pool'], 'v_w_out': out['v_w_out'], 'v_norm_mlp_w': out['v_norm_mlp_w'], 'v_w_up': out['v_w_up'], 'v_w_down': out['v_w_down'], 'v_norm_final_w': out['v_norm_final_w']}


def _loss(weights, diff, rest, loss_target):
    with _jax.named_scope("forward"):
        args = {**rest, TWIN_DIFF_INPUT: diff, **{k: w.astype(_WEIGHT_DTYPES[k]) for k, w in weights.items()}}
        y = _forward(args)
    with _jax.named_scope("loss_head"):
        err = _jnp.square(y.astype(_jnp.float32) - loss_target)
        return 0.5 * _jnp.sum(_jnp.mean(err, axis=-1)) if err.ndim else 0.5 * err


def _adamw(w, g, m, v):
    m = ADAM_B1 * m + (1.0 - ADAM_B1) * g
    v = ADAM_B2 * v + (1.0 - ADAM_B2) * _jnp.square(g)
    m_hat = m / (1.0 - ADAM_B1 ** ADAM_STEP)
    v_hat = v / (1.0 - ADAM_B2 ** ADAM_STEP)
    delta = -ADAM_LR * (m_hat / (_jnp.sqrt(v_hat) + ADAM_EPS) + ADAM_WD * w)
    return delta, m, v


def reference(x, c, w_ada, b_ada, norm_mix_w, w_in, conv_w, conv_b, dt_bias, a_log, d_skip, ssd_norm_w, w_branch_ssd, pool_w, pool_scale, w_branch_pool, w_out, norm_mlp_w, w_up, w_down, norm_final_w, loss_target, m_w_ada, m_b_ada, m_norm_mix_w, m_w_in, m_conv_w, m_conv_b, m_dt_bias, m_a_log, m_d_skip, m_ssd_norm_w, m_w_branch_ssd, m_pool_w, m_pool_scale, m_w_branch_pool, m_w_out, m_norm_mlp_w, m_w_up, m_w_down, m_norm_final_w, v_w_ada, v_b_ada, v_norm_mix_w, v_w_in, v_conv_w, v_conv_b, v_dt_bias, v_a_log, v_d_skip, v_ssd_norm_w, v_w_branch_ssd, v_pool_w, v_pool_scale, v_w_branch_pool, v_w_out, v_norm_mlp_w, v_w_up, v_w_down, v_norm_final_w):
    given = dict(x=x, c=c, w_ada=w_ada, b_ada=b_ada, norm_mix_w=norm_mix_w, w_in=w_in, conv_w=conv_w, conv_b=conv_b, dt_bias=dt_bias, a_log=a_log, d_skip=d_skip, ssd_norm_w=ssd_norm_w, w_branch_ssd=w_branch_ssd, pool_w=pool_w, pool_scale=pool_scale, w_branch_pool=w_branch_pool, w_out=w_out, norm_mlp_w=norm_mlp_w, w_up=w_up, w_down=w_down, norm_final_w=norm_final_w, loss_target=loss_target, m_w_ada=m_w_ada, m_b_ada=m_b_ada, m_norm_mix_w=m_norm_mix_w, m_w_in=m_w_in, m_conv_w=m_conv_w, m_conv_b=m_conv_b, m_dt_bias=m_dt_bias, m_a_log=m_a_log, m_d_skip=m_d_skip, m_ssd_norm_w=m_ssd_norm_w, m_w_branch_ssd=m_w_branch_ssd, m_pool_w=m_pool_w, m_pool_scale=m_pool_scale, m_w_branch_pool=m_w_branch_pool, m_w_out=m_w_out, m_norm_mlp_w=m_norm_mlp_w, m_w_up=m_w_up, m_w_down=m_w_down, m_norm_final_w=m_norm_final_w, v_w_ada=v_w_ada, v_b_ada=v_b_ada, v_norm_mix_w=v_norm_mix_w, v_w_in=v_w_in, v_conv_w=v_conv_w, v_conv_b=v_conv_b, v_dt_bias=v_dt_bias, v_a_log=v_a_log, v_d_skip=v_d_skip, v_ssd_norm_w=v_ssd_norm_w, v_w_branch_ssd=v_w_branch_ssd, v_pool_w=v_pool_w, v_pool_scale=v_pool_scale, v_w_branch_pool=v_w_branch_pool, v_w_out=v_w_out, v_norm_mlp_w=v_norm_mlp_w, v_w_up=v_w_up, v_w_down=v_w_down, v_norm_final_w=v_norm_final_w)
    weights = {n: given[n] for n in TWIN_WEIGHTS}
    shared = {n: given[n] for n in SHARED_INPUTS}
    per_example = {n: given[n] for n in ['x', 'c']}
    grad_fn = _jax.value_and_grad(_loss, argnums=(0, 1))

    def one_microbatch(ex, loss_target):
        ex = dict(ex)
        diff = ex.pop(TWIN_DIFF_INPUT)
        return grad_fn(weights, diff, {**shared, **ex}, loss_target)

    if N_MICROBATCH == 1:
        loss, (grad_w, grad_x) = one_microbatch(per_example, given["loss_target"])
    else:
        def body(carry, xs):
            loss_sum, grad_sum = carry
            l_k, (gw_k, gx_k) = one_microbatch(xs[0], xs[1])
            with _jax.named_scope("update"):
                return (loss_sum + l_k, _jax.tree.map(_jnp.add, grad_sum, gw_k)), gx_k

        init = (_jnp.zeros((), _jnp.float32), _jax.tree.map(_jnp.zeros_like, weights))
        (loss, grad_w), grad_x = _jax.lax.scan(body, init, (per_example, given["loss_target"]))
    with _jax.named_scope("update"):
        delta_w, new_m, new_v = {}, {}, {}
        for n in TWIN_WEIGHTS:
            delta_w[n], new_m[n], new_v[n] = _adamw(weights[n], grad_w[n], given["m_" + n], given["v_" + n])
    return (loss, grad_x, *[grad_w[n] for n in TWIN_WEIGHTS], *[delta_w[n] for n in TWIN_WEIGHTS],
            *[new_m[n] for n in TWIN_WEIGHTS], *[new_v[n] for n in TWIN_WEIGHTS])
```

```python
import functools

import jax
import jax.numpy as jnp
from jax import lax
from jax.experimental import pallas as pl
from jax.experimental.pallas import tpu as pltpu

F32, BF16 = jnp.float32, jnp.bfloat16
HIGH = lax.Precision.HIGHEST
MESH = pl.DeviceIdType.MESH

D = 1024
DI = 2048
HEADS, HEAD_DIM = 32, 64
GROUPS, STATE = 4, 128
Q = 128
XBC = DI + 2 * GROUPS * STATE
POOL_WINDOWS = (2, 4, 8, 16)
GW = 256
DFF = 4096
EPS = 1e-5
IN_COLS = 8224
OFF_Z, OFF_XBC, OFF_POOL, OFF_GATE, OFF_DT, NP = 0, 2048, 5120, 6144, 8192, 8448
N_CHIPS = 4
ADAM_LR, ADAM_B1, ADAM_B2, ADAM_EPS, ADAM_WD, ADAM_STEP = 0.001, 0.9, 0.999, 1e-08, 0.01, 10
VMEM_LIMIT = 56 * 2 ** 20
NEG = -1e30


def _cp(sem=None, **kw):
    if sem is not None:
        kw["dimension_semantics"] = sem
    return pltpu.CompilerParams(vmem_limit_bytes=VMEM_LIMIT, **kw)


def _sigmoid(v):
    return 1.0 / (1.0 + jnp.exp(-v))


def _softplus(v):
    return jnp.maximum(v, 0.0) + jnp.log1p(jnp.exp(-jnp.abs(v)))


def _dot(a, b, dims, **kw):
    return lax.dot_general(a, b, (dims, ((), ())), preferred_element_type=F32, **kw)


def _nn(a, b, **kw):
    return _dot(a, b, ((1,), (0,)), **kw)


def _nt(a, b, **kw):
    return _dot(a, b, ((1,), (1,)), **kw)


def _tn(a, b, **kw):
    return _dot(a, b, ((0,), (0,)), **kw)


def _perm_cols(w):
    pad = jnp.zeros(w.shape[:-1] + (NP - IN_COLS,), w.dtype)
    return jnp.concatenate([w[..., :5120], w[..., 5152:], w[..., 5120:5152], pad], axis=-1)


def _unperm_cols(g):
    return jnp.concatenate([g[..., :5120], g[..., OFF_DT:OFF_DT + 32], g[..., 5120:OFF_DT]], axis=-1)


def _matmul(a, b, *, mode, tm, tn, tk, out_dtypes, name, epi=None, tile_extras=(), row_extras=()):
    M, K = (a.shape[1], a.shape[0]) if mode == "tn" else a.shape
    N = b.shape[0] if mode == "nt" else b.shape[1]
    tm, tn, tk = min(tm, M), min(tn, N), min(tk, K)
    assert M % tm == 0 and N % tn == 0 and K % tk == 0, (name, M, N, K, tm, tn, tk)
    if mode == "nn":
        a_spec = pl.BlockSpec((tm, tk), lambda i, j, k: (i, k))
        b_spec = pl.BlockSpec((tk, tn), lambda i, j, k: (k, j))
        dims = ((1,), (0,))
    elif mode == "nt":
        a_spec = pl.BlockSpec((tm, tk), lambda i, j, k: (i, k))
        b_spec = pl.BlockSpec((tn, tk), lambda i, j, k: (j, k))
        dims = ((1,), (1,))
    else:
        a_spec = pl.BlockSpec((tk, tm), lambda i, j, k: (k, i))
        b_spec = pl.BlockSpec((tk, tn), lambda i, j, k: (k, j))
        dims = ((0,), (0,))
    nk = K // tk
    n_te, n_re, n_out = len(tile_extras), len(row_extras), len(out_dtypes)
    if epi is None:
        epi = lambda acc: (acc,)

    def body(a_ref, b_ref, *rest):
        extras = rest[:n_te + n_re]
        outs = rest[n_te + n_re:n_te + n_re + n_out]
        p = _dot(a_ref[...], b_ref[...], dims)

        def finish(acc):
            vals = epi(acc, *[e[...] for e in extras])
            for o, v in zip(outs, vals):
                o[...] = v.astype(o.dtype)

        if nk == 1:
            finish(p)
        else:
            acc_ref = rest[-1]
            k = pl.program_id(2)

            @pl.when(k == 0)
            def _():
                acc_ref[...] = p

            @pl.when(k > 0)
            def _():
                acc_ref[...] += p

            @pl.when(k == nk - 1)
            def _():
                finish(acc_ref[...])

    tile_spec = pl.BlockSpec((tm, tn), lambda i, j, k: (i, j))
    row_spec = pl.BlockSpec((1, tn), lambda i, j, k: (0, j))
    outs = pl.pallas_call(
        body, name=name, grid=(M // tm, N // tn, nk),
        in_specs=[a_spec, b_spec] + [tile_spec] * n_te + [row_spec] * n_re,
        out_specs=[tile_spec] * n_out,
        out_shape=[jax.ShapeDtypeStruct((M, N), dt) for dt in out_dtypes],
        scratch_shapes=[pltpu.VMEM((tm, tn), F32)] if nk > 1 else [],
        compiler_params=_cp(("parallel", "parallel", "arbitrary")),
    )(a, b, *tile_extras, *row_extras)
    return outs


def _row_tile(T):
    return min(512, T)


def _norm_mod(x, nw, scale, shift, name):
    T = x.shape[0]
    tr = _row_tile(T)

    def body(x_ref, nw_ref, sc_ref, sh_ref, o_ref):
        xv = x_ref[...]
        r = lax.rsqrt(jnp.mean(xv * xv, axis=-1, keepdims=True) + EPS)
        o_ref[...] = ((xv * r) * nw_ref[...] * (1.0 + sc_ref[...]) + sh_ref[...]).astype(BF16)

    tile = pl.BlockSpec((tr, D), lambda i: (i, 0))
    row = pl.BlockSpec((1, D), lambda i: (0, 0))
    return pl.pallas_call(
        body, name=name, grid=(T // tr,), in_specs=[tile, row, row, row], out_specs=tile,
        out_shape=jax.ShapeDtypeStruct((T, D), BF16), compiler_params=_cp(("parallel",)),
    )(x, nw, scale, shift)


def _norm_mod_bwd(x, dh, dres, nw, scale, name, branch=None, gate=None):
    T = x.shape[0]
    tr = _row_tile(T)
    with_branch = branch is not None

    def body(x_ref, dh_ref, dr_ref, nw_ref, sc_ref, *rest):
        if with_branch:
            br_ref, g_ref, dx_ref, sums_ref, db_ref = rest
        else:
            dx_ref, sums_ref = rest
        i = pl.program_id(0)

        @pl.when(i == 0)
        def _():
            sums_ref[...] = jnp.zeros_like(sums_ref)

        xv, dhv = x_ref[...], dh_ref[...]
        r = lax.rsqrt(jnp.mean(xv * xv, axis=-1, keepdims=True) + EPS)
        xn = xv * r
        g1 = dhv * (1.0 + sc_ref[...])
        dxn = g1 * nw_ref[...]
        dx = dr_ref[...] + r * (dxn - xn * jnp.mean(dxn * xn, axis=-1, keepdims=True))
        dx_ref[...] = dx
        sums_ref[0:1, :] += jnp.sum(dhv, axis=0, keepdims=True)
        sums_ref[1:2, :] += jnp.sum(dhv * (xn * nw_ref[...]), axis=0, keepdims=True)
        sums_ref[2:3, :] += jnp.sum(g1 * xn, axis=0, keepdims=True)
        if with_branch:
            db_ref[...] = (dx * g_ref[...]).astype(BF16)
            sums_ref[3:4, :] += jnp.sum(dx * br_ref[...], axis=0, keepdims=True)

    tile = pl.BlockSpec((tr, D), lambda i: (i, 0))
    row = pl.BlockSpec((1, D), lambda i: (0, 0))
    sums = pl.BlockSpec((8, D), lambda i: (0, 0))
    ins = [x, dh, dres, nw, scale] + ([branch, gate] if with_branch else [])
    in_specs = [tile, tile, tile, row, row] + ([tile, row] if with_branch else [])
    out_specs = [tile, sums] + ([tile] if with_branch else [])
    out_shape = [jax.ShapeDtypeStruct((T, D), F32), jax.ShapeDtypeStruct((8, D), F32)]
    if with_branch:
        out_shape.append(jax.ShapeDtypeStruct((T, D), BF16))
    return pl.pallas_call(
        body, name=name, grid=(T // tr,), in_specs=in_specs, out_specs=out_specs, out_shape=out_shape,
        compiler_params=_cp(("arbitrary",)),
    )(*ins)


def _final_loss_bwd(x3, target, wf, down, gate_f):
    T = x3.shape[0]
    tr = _row_tile(T)
    n_steps = T // tr

    def body(x_ref, t_ref, w_ref, dn_ref, g_ref, dx_ref, dd_ref, sums_ref):
        i = pl.program_id(0)

        @pl.when(i == 0)
        def _():
            sums_ref[...] = jnp.zeros_like(sums_ref)

        xv = x_ref[...]
        r = lax.rsqrt(jnp.mean(xv * xv, axis=-1, keepdims=True) + EPS)
        xn = xv * r
        err = xn * w_ref[...] - t_ref[...]
        dy = err * (1.0 / D)
        dxn = dy * w_ref[...]
        dx = r * (dxn - xn * jnp.mean(dxn * xn, axis=-1, keepdims=True))
        dx_ref[...] = dx
        dd_ref[...] = (dx * g_ref[...]).astype(BF16)
        sums_ref[0:1, :] += jnp.sum(dy * xn, axis=0, keepdims=True)
        sums_ref[1:2, :] += jnp.sum(dx * dn_ref[...], axis=0, keepdims=True)
        sums_ref[2:3, :] += jnp.sum(err * err, axis=0, keepdims=True) * (0.5 / D)

        @pl.when(i == n_steps - 1)
        def _():
            sums_ref[3:4, :] = jnp.broadcast_to(jnp.sum(sums_ref[2:3, :], axis=1, keepdims=True), (1, D))

    tile = pl.BlockSpec((tr, D), lambda i: (i, 0))
    row = pl.BlockSpec((1, D), lambda i: (0, 0))
    sums = pl.BlockSpec((8, D), lambda i: (0, 0))
    return pl.pallas_call(
        body, name="final_loss_bwd", grid=(n_steps,), in_specs=[tile, tile, row, tile, row],
        out_specs=[tile, tile, sums],
        out_shape=[jax.ShapeDtypeStruct((T, D), F32), jax.ShapeDtypeStruct((T, D), BF16),
                   jax.ShapeDtypeStruct((8, D), F32)],
        compiler_params=_cp(("arbitrary",)),
    )(x3, target, wf, down, gate_f)


CONV_TC = 512


def _conv_taps(xp, w, b):
    acc = b + w[3:4, :] * xp
    for k in range(3):
        acc = acc + w[k:k + 1, :] * pltpu.roll(xp, 3 - k, 0)
    return acc


def _conv_fwd(proj, conv_w, conv_b):
    T = proj.shape[0]
    tr = _row_tile(T)
    nb, offb = tr // 8, OFF_XBC // CONV_TC

    def body(x_ref, h_ref, w_ref, b_ref, o_ref):
        halo = jnp.where(pl.program_id(0) > 0, h_ref[...], 0.0)
        xp = jnp.concatenate([halo, x_ref[...]], axis=0)
        acc = _conv_taps(xp, w_ref[...], b_ref[...])[8:]
        o_ref[...] = acc * _sigmoid(acc)

    return pl.pallas_call(
        body, name="conv_fwd", grid=(T // tr, XBC // CONV_TC),
        in_specs=[pl.BlockSpec((tr, CONV_TC), lambda i, j: (i, j + offb)),
                  pl.BlockSpec((8, CONV_TC), lambda i, j: (jnp.maximum(i * nb - 1, 0), j + offb)),
                  pl.BlockSpec((4, CONV_TC), lambda i, j: (0, j)),
                  pl.BlockSpec((1, CONV_TC), lambda i, j: (0, j))],
        out_specs=pl.BlockSpec((tr, CONV_TC), lambda i, j: (i, j)),
        out_shape=jax.ShapeDtypeStruct((T, XBC), F32), compiler_params=_cp(("parallel", "parallel")),
    )(proj, proj, conv_w, conv_b)


def _conv_bwd_a(dxa, proj, conv_w, conv_b):
    T = proj.shape[0]
    tr = _row_tile(T)
    nb, offb = tr // 8, OFF_XBC // CONV_TC

    def body(d_ref, x_ref, h_ref, w_ref, b_ref, o_ref, sums_ref):
        i = pl.program_id(1)

        @pl.when(i == 0)
        def _():
            sums_ref[...] = jnp.zeros_like(sums_ref)

        halo = jnp.where(i > 0, h_ref[...], 0.0)
        xp = jnp.concatenate([halo, x_ref[...]], axis=0)
        acc = _conv_taps(xp, w_ref[...], b_ref[...])[8:]
        s = _sigmoid(acc)
        dxc = d_ref[...] * (s * (1.0 + acc * (1.0 - s)))
        o_ref[...] = dxc
        sums_ref[3:4, :] += jnp.sum(dxc * x_ref[...], axis=0, keepdims=True)
        for k in range(3):
            sums_ref[k:k + 1, :] += jnp.sum(dxc * pltpu.roll(xp, 3 - k, 0)[8:], axis=0, keepdims=True)
        sums_ref[4:5, :] += jnp.sum(dxc, axis=0, keepdims=True)

    return pl.pallas_call(
        body, name="conv_bwd_a", grid=(XBC // CONV_TC, T // tr),
        in_specs=[pl.BlockSpec((tr, CONV_TC), lambda j, i: (i, j)),
                  pl.BlockSpec((tr, CONV_TC), lambda j, i: (i, j + offb)),
                  pl.BlockSpec((8, CONV_TC), lambda j, i: (jnp.maximum(i * nb - 1, 0), j + offb)),
                  pl.BlockSpec((4, CONV_TC), lambda j, i: (0, j)),
                  pl.BlockSpec((1, CONV_TC), lambda j, i: (0, j))],
        out_specs=[pl.BlockSpec((tr, CONV_TC), lambda j, i: (i, j)), pl.BlockSpec((8, CONV_TC), lambda j, i: (0, j))],
        out_shape=[jax.ShapeDtypeStruct((T, XBC), F32), jax.ShapeDtypeStruct((8, XBC), F32)],
        compiler_params=_cp(("parallel", "arbitrary")),
    )(dxa, proj, proj, conv_w, conv_b)


def _conv_bwd_b(dxc, conv_w, dproj):
    T = dxc.shape[0]
    tr = _row_tile(T)
    nb, offb, last = tr // 8, OFF_XBC // CONV_TC, T // tr - 1

    def body(d_ref, h_ref, w_ref, dp_in, o_ref):
        del dp_in
        halo = jnp.where(pl.program_id(0) < last, h_ref[...], 0.0)
        xp = jnp.concatenate([d_ref[...], halo], axis=0)
        n = xp.shape[0]
        w = w_ref[...]
        acc = w[3:4, :] * xp
        for k in range(3):
            acc = acc + w[k:k + 1, :] * pltpu.roll(xp, n - (3 - k), 0)
        o_ref[...] = acc[:tr].astype(BF16)

    return pl.pallas_call(
        body, name="conv_bwd_b", grid=(T // tr, XBC // CONV_TC),
        in_specs=[pl.BlockSpec((tr, CONV_TC), lambda i, j: (i, j)),
                  pl.BlockSpec((8, CONV_TC), lambda i, j: (jnp.minimum((i + 1) * nb, T // 8 - 1), j)),
                  pl.BlockSpec((4, CONV_TC), lambda i, j: (0, j)),
                  pl.BlockSpec(memory_space=pl.ANY)],
        out_specs=pl.BlockSpec((tr, CONV_TC), lambda i, j: (i, j + offb)),
        out_shape=jax.ShapeDtypeStruct(dproj.shape, BF16), input_output_aliases={3: 0},
        compiler_params=_cp(("parallel", "parallel")),
    )(dxc, dxc, conv_w, dproj)


def _expand_pairs(v, j0, n_pairs, lo):
    R = v.shape[0]
    parts = []
    for j in range(j0, j0 + n_pairs):
        va = jnp.broadcast_to(v[:, 2 * j:2 * j + 1], (R, 128))
        vb = jnp.broadcast_to(v[:, 2 * j + 1:2 * j + 2], (R, 128))
        parts.append(jnp.where(lo, va, vb))
    return jnp.concatenate(parts, axis=1)


def _ssd_common(dtp_ref, dtT_ref, dtb_r, alog_r, dtb_c, alog_c):
    rows = lax.broadcasted_iota(jnp.int32, (Q, Q), 0)
    cols = lax.broadcasted_iota(jnp.int32, (Q, Q), 1)
    causal = cols <= rows
    tri = causal.astype(F32)
    raw = dtp_ref[:, 0:HEADS] + dtb_r[...]
    dt = _softplus(raw)
    a_r = -jnp.exp(alog_r[...])
    cs = _nn(tri, dt * a_r, precision=HIGH)
    aT = _softplus(dtT_ref[...] + dtb_c[...]) * (-jnp.exp(alog_c[...]))
    csT = _nt(aT, tri, precision=HIGH)
    return causal, raw, dt, a_r, cs, csT


def _ssd_fwd(xbc_a, proj, dtT, dtb_r, alog_r, dtb_c, alog_c, dsk_exp):
    T = xbc_a.shape[0]
    nc = T // Q

    def body(xbc_ref, dtp_ref, dtT_ref, dtb_r_ref, alog_r_ref, dtb_c_ref, alog_c_ref, dsk_ref, y_ref, hin_ref, h_scr):
        @pl.when(pl.program_id(0) == 0)
        def _():
            h_scr[...] = jnp.zeros_like(h_scr)

        causal, _, dt, _, cs, csT = _ssd_common(dtp_ref, dtT_ref, dtb_r_ref, alog_r_ref, dtb_c_ref, alog_c_ref)
        lo = lax.broadcasted_iota(jnp.int32, (1, 128), 1) < HEAD_DIM
        cs_last = cs[Q - 1:Q, :]
        ecs, dec, cd = jnp.exp(cs), jnp.exp(cs_last - cs), jnp.exp(cs_last)
        for g in range(GROUPS):
            gs = slice(512 * g, 512 * (g + 1))
            xs_g = xbc_ref[:, gs]
            b_g = xbc_ref[:, DI + STATE * g:DI + STATE * (g + 1)].astype(BF16)
            c_g = xbc_ref[:, DI + 512 + STATE * g:DI + 512 + STATE * (g + 1)].astype(BF16)
            xdt = xs_g * _expand_pairs(dt, 4 * g, 4, lo)
            xdt_b = xdt.astype(BF16)
            s_mat = _nt(c_g, b_g)
            ys = []
            for jj in range(4):
                xp = xdt_b[:, 128 * jj:128 * (jj + 1)]
                acc = None
                for h, sel in ((8 * g + 2 * jj, lo), (8 * g + 2 * jj + 1, jnp.logical_not(lo))):
                    l_mat = jnp.exp(jnp.where(causal, cs[:, h:h + 1] - csT[h:h + 1, :], NEG))
                    part = _nn((s_mat * l_mat).astype(BF16), jnp.where(sel, xp, jnp.zeros_like(xp)))
                    acc = part if acc is None else acc + part
                ys.append(acc)
            h_g = h_scr[128 * g:128 * (g + 1), :]
            hin_ref[0, 128 * g:128 * (g + 1), :] = h_g
            y_off = _nn(c_g, h_g.astype(BF16)) * _expand_pairs(ecs, 4 * g, 4, lo)
            y_ref[:, gs] = jnp.concatenate(ys, axis=1) + y_off + dsk_ref[:, gs] * xs_g
            xdec = (xdt * _expand_pairs(dec, 4 * g, 4, lo)).astype(BF16)
            h_scr[128 * g:128 * (g + 1), :] = h_g * _expand_pairs(cd, 4 * g, 4, lo) + _tn(b_g, xdec)

    small_r = pl.BlockSpec((1, HEADS), lambda c: (0, 0))
    small_c = pl.BlockSpec((HEADS, 1), lambda c: (0, 0))
    return pl.pallas_call(
        body, name="ssd_fwd", grid=(nc,),
        in_specs=[pl.BlockSpec((Q, XBC), lambda c: (c, 0)),
                  pl.BlockSpec((Q, 128), lambda c: (c, OFF_DT // 128)),
                  pl.BlockSpec((HEADS, Q), lambda c: (0, c)),
                  small_r, small_r, small_c, small_c,
                  pl.BlockSpec((1, DI), lambda c: (0, 0))],
        out_specs=[pl.BlockSpec((Q, DI), lambda c: (c, 0)), pl.BlockSpec((1, 512, 512), lambda c: (c, 0, 0))],
        out_shape=[jax.ShapeDtypeStruct((T, DI), F32), jax.ShapeDtypeStruct((nc, 512, 512), F32)],
        scratch_shapes=[pltpu.VMEM((512, 512), F32)], compiler_params=_cp(("arbitrary",)),
    )(xbc_a, proj, dtT, dtb_r, alog_r, dtb_c, alog_c, dsk_exp)


def _ssd_bwd(dy, xbc_a, proj, dtT, hin, dtb_r, alog_r, dtb_c, alog_c, dsk_exp, dproj):
    T = xbc_a.shape[0]
    nc = T // Q

    def body(dy_ref, xbc_ref, dtp_ref, dtT_ref, hin_ref, dtb_r_ref, alog_r_ref, dtb_c_ref, alog_c_ref, dsk_ref, dp_in,
             dxa_ref, dp_ref, dsk_sum_ref, small_ref, dh_scr):
        del dp_in

        @pl.when(pl.program_id(0) == 0)
        def _():
            dh_scr[...] = jnp.zeros_like(dh_scr)
            dsk_sum_ref[...] = jnp.zeros_like(dsk_sum_ref)
            small_ref[...] = jnp.zeros_like(small_ref)

        causal, raw, dt, a_r, cs, csT = _ssd_common(dtp_ref, dtT_ref, dtb_r_ref, alog_r_ref, dtb_c_ref, alog_c_ref)
        lane = lax.broadcasted_iota(jnp.int32, (1, 128), 1)
        lo = lane < HEAD_DIM
        hi = jnp.logical_not(lo)
        lane32 = lax.broadcasted_iota(jnp.int32, (1, HEADS), 1)
        sub32 = lax.broadcasted_iota(jnp.int32, (HEADS, 1), 0)
        cs_last = cs[Q - 1:Q, :]
        ecs, dec, cd = jnp.exp(cs), jnp.exp(cs_last - cs), jnp.exp(cs_last)
        dcs_c = jnp.zeros((Q, HEADS), F32)
        dcs_r = jnp.zeros((HEADS, Q), F32)
        dcs_l = jnp.zeros((1, HEADS), F32)
        ddt_x = jnp.zeros((Q, HEADS), F32)

        def put(vec, h, val):
            return vec + jnp.where(lane32 == h, val, 0.0)

        def halves(v):
            sa = jnp.sum(jnp.where(lo, v, 0.0), axis=1, keepdims=True)
            sb = jnp.sum(jnp.where(lo, 0.0, v), axis=1, keepdims=True)
            return sa, sb

        for g in range(GROUPS):
            gs = slice(512 * g, 512 * (g + 1))
            hs = slice(128 * g, 128 * (g + 1))
            xs_g = xbc_ref[:, gs]
            b_g = xbc_ref[:, DI + STATE * g:DI + STATE * (g + 1)].astype(BF16)
            c_g = xbc_ref[:, DI + 512 + STATE * g:DI + 512 + STATE * (g + 1)].astype(BF16)
            dt_g = _expand_pairs(dt, 4 * g, 4, lo)
            ecs_g = _expand_pairs(ecs, 4 * g, 4, lo)
            dec_g = _expand_pairs(dec, 4 * g, 4, lo)
            cd_g = _expand_pairs(cd, 4 * g, 4, lo)
            xdt = xs_g * dt_g
            xdt_b = xdt.astype(BF16)
            dy_g = dy_ref[:, gs]
            dy_b = dy_g.astype(BF16)
            s_mat = _nt(c_g, b_g)
            ds_mat = jnp.zeros((Q, Q), F32)
            dx_parts = []
            for jj in range(4):
                xp = xdt_b[:, 128 * jj:128 * (jj + 1)]
                dyp = dy_b[:, 128 * jj:128 * (jj + 1)]
                dxh = []
                for h, sel in ((8 * g + 2 * jj, lo), (8 * g + 2 * jj + 1, hi)):
                    l_mat = jnp.exp(jnp.where(causal, cs[:, h:h + 1] - csT[h:h + 1, :], NEG))
                    m_mat = s_mat * l_mat
                    dm = _nt(jnp.where(sel, dyp, jnp.zeros_like(dyp)), xp)
                    w_mat = dm * m_mat
                    dcs_c = put(dcs_c, h, jnp.sum(w_mat, axis=1, keepdims=True))
                    dcs_r = dcs_r + jnp.where(sub32 == h, jnp.sum(w_mat, axis=0, keepdims=True), 0.0)
                    ds_mat = ds_mat + dm * l_mat
                    dxh.append(_tn(m_mat.astype(BF16), dyp))
                dx_parts.append(jnp.where(lo, dxh[0], dxh[1]))
            hin_g = hin_ref[0, hs, :]
            hin_b = hin_g.astype(BF16)
            dh_g = dh_scr[hs, :]
            dh_b = dh_g.astype(BF16)
            y_off = _nn(c_g, hin_b) * ecs_g
            dz = (dy_g * ecs_g).astype(BF16)
            g_mat = _nn(b_g, dh_b)
            xdec = xdt * dec_g
            v1 = dy_g * y_off - xdec * g_mat
            v2 = jnp.sum(xdec * g_mat, axis=0, keepdims=True) + jnp.sum(dh_g * hin_g, axis=0, keepdims=True) * cd_g
            dxdt = jnp.concatenate(dx_parts, axis=1) + dec_g * g_mat
            v3 = dxdt * xs_g
            for jj in range(4):
                ps = slice(128 * jj, 128 * (jj + 1))
                ha = 8 * g + 2 * jj
                for vec_name, v in (("c", v1), ("l", v2), ("x", v3)):
                    sa, sb = halves(v[:, ps])
                    if vec_name == "c":
                        dcs_c = put(put(dcs_c, ha, sa), ha + 1, sb)
                    elif vec_name == "l":
                        dcs_l = put(put(dcs_l, ha, sa), ha + 1, sb)
                    else:
                        ddt_x = put(put(ddt_x, ha, sa), ha + 1, sb)
            ds_b = ds_mat.astype(BF16)
            dxa_ref[:, gs] = dxdt * dt_g + dy_g * dsk_ref[:, gs]
            dxa_ref[:, DI + STATE * g:DI + STATE * (g + 1)] = _nt(xdec.astype(BF16), dh_b) + _tn(ds_b, c_g)
            dxa_ref[:, DI + 512 + STATE * g:DI + 512 + STATE * (g + 1)] = _nt(dz, hin_b) + _nn(ds_b, b_g)
            dh_scr[hs, :] = _tn(c_g, dz) + dh_g * cd_g
            dsk_sum_ref[0:1, gs] += jnp.sum(dy_g * xs_g, axis=0, keepdims=True)

        rows = lax.broadcasted_iota(jnp.int32, (Q, Q), 0)
        cols = lax.broadcasted_iota(jnp.int32, (Q, Q), 1)
        tri_t = (cols >= rows).astype(F32)
        last_row = lax.broadcasted_iota(jnp.int32, (Q, 1), 0) == Q - 1
        dcs = dcs_c + jnp.where(last_row, dcs_l, 0.0)
        da = _nn(tri_t, dcs, precision=HIGH) - _nt(tri_t, dcs_r, precision=HIGH)
        ddt_raw = (ddt_x + da * a_r) * _sigmoid(raw)
        small_ref[0:1, :] += jnp.sum(da * dt, axis=0, keepdims=True) * a_r
        small_ref[1:2, :] += jnp.sum(ddt_raw, axis=0, keepdims=True)
        dp_ref[...] = jnp.zeros_like(dp_ref)
        dp_ref[:, 0:HEADS] = ddt_raw.astype(BF16)

    rev = lambda c: nc - 1 - c
    small_r = pl.BlockSpec((1, HEADS), lambda c: (0, 0))
    small_c = pl.BlockSpec((HEADS, 1), lambda c: (0, 0))
    return pl.pallas_call(
        body, name="ssd_bwd", grid=(nc,),
        in_specs=[pl.BlockSpec((Q, DI), lambda c: (rev(c), 0)),
                  pl.BlockSpec((Q, XBC), lambda c: (rev(c), 0)),
                  pl.BlockSpec((Q, 128), lambda c: (rev(c), OFF_DT // 128)),
                  pl.BlockSpec((HEADS, Q), lambda c: (0, rev(c))),
                  pl.BlockSpec((1, 512, 512), lambda c: (rev(c), 0, 0)),
                  small_r, small_r, small_c, small_c,
                  pl.BlockSpec((1, DI), lambda c: (0, 0)),
                  pl.BlockSpec(memory_space=pl.ANY)],
        out_specs=[pl.BlockSpec((Q, XBC), lambda c: (rev(c), 0)),
                   pl.BlockSpec((Q, 256), lambda c: (rev(c), OFF_DT // 256)),
                   pl.BlockSpec((8, DI), lambda c: (0, 0)),
                   pl.BlockSpec((8, HEADS), lambda c: (0, 0))],
        out_shape=[jax.ShapeDtypeStruct((T, XBC), F32), jax.ShapeDtypeStruct(dproj.shape, BF16),
                   jax.ShapeDtypeStruct((8, DI), F32), jax.ShapeDtypeStruct((8, HEADS), F32)],
        input_output_aliases={10: 1},
        scratch_shapes=[pltpu.VMEM((512, 512), F32)], compiler_params=_cp(("arbitrary",)),
    )(dy, xbc_a, proj, dtT, hin, dtb_r, alog_r, dtb_c, alog_c, dsk_exp, dproj)


def _gate_norm(y, proj, w):
    T = y.shape[0]
    tr = _row_tile(T)

    def body(y_ref, z_ref, w_ref, o_ref):
        for g in range(GROUPS):
            gs = slice(512 * g, 512 * (g + 1))
            z = z_ref[:, gs]
            yg = y_ref[:, gs] * (z * _sigmoid(z))
            r = lax.rsqrt(jnp.mean(yg * yg, axis=-1, keepdims=True) + EPS)
            o_ref[:, gs] = (yg * r * w_ref[:, gs]).astype(BF16)

    tile = pl.BlockSpec((tr, DI), lambda i: (i, 0))
    return pl.pallas_call(
        body, name="gate_norm", grid=(T // tr,), in_specs=[tile, tile, pl.BlockSpec((1, DI), lambda i: (0, 0))],
        out_specs=tile, out_shape=jax.ShapeDtypeStruct((T, DI), BF16), compiler_params=_cp(("parallel",)),
    )(y, proj, w)


def _gate_norm_bwd(dyn, y, proj, w, dproj):
    T = y.shape[0]
    tr = _row_tile(T)

    def body(d_ref, y_ref, z_ref, w_ref, dp_in, dy_ref, dz_ref, sums_ref):
        del dp_in

        @pl.when(pl.program_id(0) == 0)
        def _():
            sums_ref[...] = jnp.zeros_like(sums_ref)

        for g in range(GROUPS):
            gs = slice(512 * g, 512 * (g + 1))
            z, yv, d = z_ref[:, gs], y_ref[:, gs], d_ref[:, gs]
            s = _sigmoid(z)
            silu = z * s
            yg = yv * silu
            r = lax.rsqrt(jnp.mean(yg * yg, axis=-1, keepdims=True) + EPS)
            yn = yg * r
            sums_ref[0:1, gs] += jnp.sum(d * yn, axis=0, keepdims=True)
            dn = d * w_ref[:, gs]
            dyg = r * (dn - yn * jnp.mean(dn * yn, axis=-1, keepdims=True))
            dy_ref[:, gs] = dyg * silu
            dz_ref[:, gs] = (dyg * yv * (s * (1.0 + z * (1.0 - s)))).astype(BF16)

    tile = pl.BlockSpec((tr, DI), lambda i: (i, 0))
    return pl.pallas_call(
        body, name="gate_norm_bwd", grid=(T // tr,),
        in_specs=[tile, tile, tile, pl.BlockSpec((1, DI), lambda i: (0, 0)), pl.BlockSpec(memory_space=pl.ANY)],
        out_specs=[tile, tile, pl.BlockSpec((8, DI), lambda i: (0, 0))],
        out_shape=[jax.ShapeDtypeStruct((T, DI), F32), jax.ShapeDtypeStruct(dproj.shape, BF16),
                   jax.ShapeDtypeStruct((8, DI), F32)],
        input_output_aliases={4: 1}, compiler_params=_cp(("arbitrary",)),
    )(dyn, y, proj, w, dproj)


def _pool_fwd(proj, pool_w_b, pool_scale):
    T = proj.shape[0]
    tr = _row_tile(T)
    nb = tr // 16

    def body(u_ref, h_ref, pw_ref, ps_ref, pooled_ref, pw_out_ref, yps_ref):
        i = pl.program_id(0)
        t = i * tr + lax.broadcasted_iota(jnp.int32, (tr, 1), 0)
        for g, win in enumerate(POOL_WINDOWS):
            gs = slice(GW * g, GW * (g + 1))
            u = u_ref[:, gs]
            s = jnp.concatenate([jnp.where(i > 0, h_ref[:, gs], 0.0), u], axis=0)
            sh = 1
            while sh < win:
                s = s + pltpu.roll(s, sh, 0)
                sh *= 2
            pooled = (s[16:] / jnp.minimum(t + 1, win).astype(F32) - u).astype(BF16)
            pooled_ref[:, gs] = pooled
            pwv = _nn(pooled, pw_ref[g])
            pw_out_ref[:, gs] = pwv
            yps_ref[:, gs] = (pwv * ps_ref[:, gs]).astype(BF16)

    tile = pl.BlockSpec((tr, D), lambda i: (i, 0))
    return pl.pallas_call(
        body, name="pool_fwd", grid=(T // tr,),
        in_specs=[pl.BlockSpec((tr, D), lambda i: (i, OFF_POOL // D)),
                  pl.BlockSpec((16, D), lambda i: (jnp.maximum(i * nb - 1, 0), OFF_POOL // D)),
                  pl.BlockSpec((4, GW, GW), lambda i: (0, 0, 0)),
                  pl.BlockSpec((1, D), lambda i: (0, 0))],
        out_specs=[tile, tile, tile],
        out_shape=[jax.ShapeDtypeStruct((T, D), BF16), jax.ShapeDtypeStruct((T, D), F32),
                   jax.ShapeDtypeStruct((T, D), BF16)],
        compiler_params=_cp(("parallel",)),
    )(proj, proj, pool_w_b, pool_scale)


def _pool_bwd(dyp, pw_out, pooled, pool_w_b, pool_scale, dproj):
    T = dyp.shape[0]
    tr = _row_tile(T)
    nb, last = tr // 16, T // tr - 1

    def body(d_ref, h_ref, pwo_ref, pooled_ref, pw_ref, ps_ref, dp_in, du_ref, gpw_ref, sums_ref):
        del dp_in
        i = pl.program_id(0)

        @pl.when(i == 0)
        def _():
            gpw_ref[...] = jnp.zeros_like(gpw_ref)
            sums_ref[...] = jnp.zeros_like(sums_ref)

        n = tr + 16
        t = i * tr + lax.broadcasted_iota(jnp.int32, (n, 1), 0)
        sums_ref[0:1, :] += jnp.sum(d_ref[...] * pwo_ref[...], axis=0, keepdims=True)
        for g, win in enumerate(POOL_WINDOWS):
            gs = slice(GW * g, GW * (g + 1))
            d_ext = jnp.concatenate([d_ref[:, gs], jnp.where(i < last, h_ref[:, gs], 0.0)], axis=0)
            dpw = (d_ext * ps_ref[:, gs]).astype(BF16)
            dpooled = _nt(dpw, pw_ref[g])
            s = jnp.where(t < T, dpooled / jnp.minimum(t + 1, win).astype(F32), 0.0)
            sh = 1
            while sh < win:
                s = s + pltpu.roll(s, n - sh, 0)
                sh *= 2
            du_ref[:, gs] = (s[:tr] - dpooled[:tr]).astype(BF16)
            gpw_ref[g] += _tn(pooled_ref[:, gs], dpw[:tr])

    tile = pl.BlockSpec((tr, D), lambda i: (i, 0))
    return pl.pallas_call(
        body, name="pool_bwd", grid=(T // tr,),
        in_specs=[tile, pl.BlockSpec((16, D), lambda i: (jnp.minimum((i + 1) * nb, T // 16 - 1), 0)), tile, tile,
                  pl.BlockSpec((4, GW, GW), lambda i: (0, 0, 0)), pl.BlockSpec((1, D), lambda i: (0, 0)),
                  pl.BlockSpec(memory_space=pl.ANY)],
        out_specs=[pl.BlockSpec((tr, D), lambda i: (i, OFF_POOL // D)),
                   pl.BlockSpec((4, GW, GW), lambda i: (0, 0, 0)), pl.BlockSpec((8, D), lambda i: (0, 0))],
        out_shape=[jax.ShapeDtypeStruct(dproj.shape, BF16), jax.ShapeDtypeStruct((4, GW, GW), F32),
                   jax.ShapeDtypeStruct((8, D), F32)],
        input_output_aliases={6: 0}, compiler_params=_cp(("arbitrary",)),
    )(dyp, dyp, pw_out, pooled, pool_w_b, pool_scale, dproj)


def _merge(proj, y_ssd, y_pool):
    T = proj.shape[0]
    tr = _row_tile(T)

    def body(g_ref, a_ref, b_ref, o_ref):
        o_ref[...] = (_sigmoid(g_ref[:, 0:D]) * a_ref[...] + _sigmoid(g_ref[:, D:2 * D]) * b_ref[...]).astype(BF16)

    tile = pl.BlockSpec((tr, D), lambda i: (i, 0))
    return pl.pallas_call(
        body, name="merge", grid=(T // tr,),
        in_specs=[pl.BlockSpec((tr, 2 * D), lambda i: (i, OFF_GATE // (2 * D))), tile, tile], out_specs=tile,
        out_shape=jax.ShapeDtypeStruct((T, D), BF16), compiler_params=_cp(("parallel",)),
    )(proj, y_ssd, y_pool)


def _merge_bwd(dmerged, proj, y_ssd, y_pool):
    T = proj.shape[0]
    tr = _row_tile(T)

    def body(d_ref, g_ref, a_ref, b_ref, da_ref, db_ref, dg_ref):
        d = d_ref[...]
        ga, gb = _sigmoid(g_ref[:, 0:D]), _sigmoid(g_ref[:, D:2 * D])
        da_ref[...] = (d * ga).astype(BF16)
        db_ref[...] = (d * gb).astype(BF16)
        dg_ref[:, 0:D] = (d * a_ref[...] * ga * (1.0 - ga)).astype(BF16)
        dg_ref[:, D:2 * D] = (d * b_ref[...] * gb * (1.0 - gb)).astype(BF16)

    tile = pl.BlockSpec((tr, D), lambda i: (i, 0))
    gates = pl.BlockSpec((tr, 2 * D), lambda i: (i, OFF_GATE // (2 * D)))
    return pl.pallas_call(
        body, name="merge_bwd", grid=(T // tr,), in_specs=[tile, gates, tile, tile], out_specs=[tile, tile, gates],
        out_shape=[jax.ShapeDtypeStruct((T, D), BF16), jax.ShapeDtypeStruct((T, D), BF16),
                   jax.ShapeDtypeStruct((T, NP), BF16)],
        compiler_params=_cp(("parallel",)),
    )(dmerged, proj, y_ssd, y_pool)


def _adamw(w, g, m, v, name):
    R, C = w.shape
    tr = R if R <= 128 else 128
    assert R % tr == 0

    def body(w_ref, g_ref, m_ref, v_ref, d_ref, mo_ref, vo_ref):
        gv = g_ref[...]
        mn = ADAM_B1 * m_ref[...] + (1.0 - ADAM_B1) * gv
        vn = ADAM_B2 * v_ref[...] + (1.0 - ADAM_B2) * (gv * gv)
        m_hat = mn / (1.0 - ADAM_B1 ** ADAM_STEP)
        v_hat = vn / (1.0 - ADAM_B2 ** ADAM_STEP)
        d_ref[...] = -ADAM_LR * (m_hat / (jnp.sqrt(v_hat) + ADAM_EPS) + ADAM_WD * w_ref[...])
        mo_ref[...] = mn
        vo_ref[...] = vn

    tile = pl.BlockSpec((tr, C), lambda i: (i, 0))
    sds = jax.ShapeDtypeStruct((R, C), F32)
    return pl.pallas_call(
        body, name=name, grid=(R // tr,), in_specs=[tile] * 4, out_specs=[tile] * 3, out_shape=[sds] * 3,
        compiler_params=_cp(("parallel",)),
    )(w, g, m, v)


def _me():
    return lax.axis_index("x"), lax.axis_index("y"), lax.axis_index("c")


def _xor_peer(x, y, c, p):
    return (x ^ ((p >> 2) & 1), y ^ ((p >> 1) & 1), c ^ (p & 1))


def _ada_fwd(c_row, w_ada, b_ada_mine):
    n_cols = w_ada.shape[1]

    def body(c_ref, w_ref, b_ref, mod_ref, c8_ref, csend, mpart, modbuf, send_sems, recv_sems):
        x, y, c = _me()
        me = 4 * x + 2 * y + c
        chip = 2 * x + y
        csend[...] = jnp.broadcast_to(c_ref[...], csend.shape)
        c8_ref[me] = csend[...]

        def c_copy(p):
            return pltpu.make_async_remote_copy(
                src_ref=csend, dst_ref=c8_ref.at[me], send_sem=send_sems.at[p - 1], recv_sem=recv_sems.at[p - 1],
                device_id=_xor_peer(x, y, c, p), device_id_type=MESH)

        for p in range(1, 8):
            c_copy(p).start()
        for p in range(1, 8):
            c_copy(p).wait_recv()
        cs = jnp.concatenate([c8_ref[d][0:1, :] for d in range(8)], axis=0)
        mpart[...] = _nn(cs * _sigmoid(cs), w_ref[...], precision=HIGH) + b_ref[...]
        modbuf[chip] = mpart[...]

        def m_copy(m):
            return pltpu.make_async_remote_copy(
                src_ref=mpart, dst_ref=modbuf.at[chip], send_sem=send_sems.at[6 + m], recv_sem=recv_sems.at[6 + m],
                device_id=_xor_peer(x, y, c, 2 * m), device_id_type=MESH)

        for m in range(1, 4):
            m_copy(m).start()
        for m in range(1, 4):
            m_copy(m).wait_recv()
        mine = lax.broadcasted_iota(jnp.int32, (8, 1), 0) == me
        for k in range(N_CHIPS):
            mod_ref[:, n_cols * k:n_cols * (k + 1)] = jnp.sum(jnp.where(mine, modbuf[k], 0.0), axis=0, keepdims=True)
        for p in range(1, 8):
            c_copy(p).wait_send()
        for m in range(1, 4):
            m_copy(m).wait_send()

    vmem = pl.BlockSpec(memory_space=pltpu.VMEM)
    return pl.pallas_call(
        body, name="ada_fwd", in_specs=[vmem, vmem, vmem], out_specs=[vmem, vmem],
        out_shape=[jax.ShapeDtypeStruct((1, N_CHIPS * n_cols), F32), jax.ShapeDtypeStruct((8, 8, D), F32)],
        scratch_shapes=[pltpu.VMEM((8, D), F32), pltpu.VMEM((8, n_cols), F32), pltpu.VMEM((N_CHIPS, 8, n_cols), F32),
                        pltpu.SemaphoreType.DMA((10,)), pltpu.SemaphoreType.DMA((10,))],
        compiler_params=_cp(),
    )(c_row, w_ada, b_ada_mine)


def _gather_small(vec):
    rows = vec.shape[0]

    def body(v_ref, all_ref, tot_ref, dsk_ref, send_sems, recv_sems):
        x, y, c = _me()
        me = 4 * x + 2 * y + c
        all_ref[me] = v_ref[...]

        def copy(p):
            return pltpu.make_async_remote_copy(
                src_ref=v_ref, dst_ref=all_ref.at[me], send_sem=send_sems.at[p - 1], recv_sem=recv_sems.at[p - 1],
                device_id=_xor_peer(x, y, c, p), device_id_type=MESH)

        for p in range(1, 8):
            copy(p).start()
        for p in range(1, 8):
            copy(p).wait_recv()
        tot = all_ref[0]
        for d in range(1, 8):
            tot = tot + all_ref[d]
        tot_ref[...] = tot
        seg = tot[SMALL_OFF["d_skip"] // 128:SMALL_OFF["d_skip"] // 128 + 16, :]
        lane = lax.broadcasted_iota(jnp.int32, (1, 128), 1)
        sa = jnp.sum(jnp.where(lane < HEAD_DIM, seg, 0.0), axis=1, keepdims=True)
        sb = jnp.sum(jnp.where(lane < HEAD_DIM, 0.0, seg), axis=1, keepdims=True)
        dsk_ref[...] = jnp.where(lane == 0, sa, jnp.where(lane == 1, sb, 0.0))
        for p in range(1, 8):
            copy(p).wait_send()

    vmem = pl.BlockSpec(memory_space=pltpu.VMEM)
    return pl.pallas_call(
        body, name="gather_small", in_specs=[vmem], out_specs=[vmem, vmem, vmem],
        out_shape=[jax.ShapeDtypeStruct((8, rows, 128), F32), jax.ShapeDtypeStruct((rows, 128), F32),
                   jax.ShapeDtypeStruct((16, 128), F32)],
        scratch_shapes=[pltpu.SemaphoreType.DMA((7,)), pltpu.SemaphoreType.DMA((7,))],
        compiler_params=_cp(),
    )(vec)


def _gather_weights(shards):
    n = len(shards)

    def body(*refs):
        ins, outs = refs[:n], refs[n:2 * n]
        send_sems, recv_sems, local_sems = refs[2 * n:]
        x, y, c = _me()
        chip = 2 * x + y
        local = [pltpu.make_async_copy(ins[w], outs[w].at[chip], local_sems.at[w]) for w in range(n)]
        for cp in local:
            cp.start()

        def half(w, which):
            h = shards[w].shape[0] // 2
            return pl.ds(which * h, h)

        def first(w, m):
            return pltpu.make_async_remote_copy(
                src_ref=ins[w].at[half(w, c)], dst_ref=outs[w].at[chip, half(w, c)],
                send_sem=send_sems.at[6 * w + m - 1], recv_sem=recv_sems.at[6 * w + m - 1],
                device_id=_xor_peer(x, y, c, 2 * m), device_id_type=MESH)

        def landed(w, m):
            src_chip = chip ^ m
            return pltpu.make_async_remote_copy(
                src_ref=ins[w].at[half(w, c)], dst_ref=outs[w].at[src_chip, half(w, c)],
                send_sem=send_sems.at[6 * w + m - 1], recv_sem=recv_sems.at[6 * w + m - 1],
                device_id=_xor_peer(x, y, c, 2 * m), device_id_type=MESH)

        def passed(w, m, which):
            src_chip = chip ^ m
            return pltpu.make_async_remote_copy(
                src_ref=outs[w].at[src_chip, half(w, which)], dst_ref=outs[w].at[src_chip, half(w, which)],
                send_sem=send_sems.at[6 * w + 2 + m], recv_sem=recv_sems.at[6 * w + 2 + m],
                device_id=(x, y, 1 - c), device_id_type=MESH)

        for w in range(n):
            for m in range(1, 4):
                first(w, m).start()
        for w in range(n):
            for m in range(1, 4):
                landed(w, m).wait_recv()
                passed(w, m, c).start()
        for w in range(n):
            for m in range(1, 4):
                passed(w, m, 1 - c).wait_recv()
        for w in range(n):
            for m in range(1, 4):
                first(w, m).wait_send()
                passed(w, m, c).wait_send()
        for cp in local:
            cp.wait()

    hbm = pl.BlockSpec(memory_space=pl.ANY)
    return pl.pallas_call(
        body, name="gather_weights", in_specs=[hbm] * n, out_specs=[hbm] * n,
        out_shape=[jax.ShapeDtypeStruct((N_CHIPS,) + s.shape, s.dtype) for s in shards],
        scratch_shapes=[pltpu.SemaphoreType.DMA((6 * n,)), pltpu.SemaphoreType.DMA((6 * n,)),
                        pltpu.SemaphoreType.DMA((n,))],
        compiler_params=_cp(),
    )(*shards)


def _pair_exchange(grads):
    n = len(grads)

    def body(*refs):
        ins, outs = refs[:n], refs[n:2 * n]
        send_sems, recv_sems = refs[2 * n:]
        x, y, c = _me()

        def copy(w):
            h = grads[w].shape[1] // 2
            return pltpu.make_async_remote_copy(
                src_ref=ins[w].at[:, pl.ds((1 - c) * h, h)], dst_ref=outs[w],
                send_sem=send_sems.at[w], recv_sem=recv_sems.at[w], device_id=(x, y, 1 - c), device_id_type=MESH)

        for w in range(n):
            copy(w).start()
        for w in range(n):
            copy(w).wait()

    hbm = pl.BlockSpec(memory_space=pl.ANY)
    return pl.pallas_call(
        body, name="pair_exchange", in_specs=[hbm] * n, out_specs=[hbm] * n,
        out_shape=[jax.ShapeDtypeStruct((N_CHIPS, g.shape[1] // 2, g.shape[2]), F32) for g in grads],
        scratch_shapes=[pltpu.SemaphoreType.DMA((n,)), pltpu.SemaphoreType.DMA((n,))],
        compiler_params=_cp(),
    )(*grads)


def _chip_exchange(partials):
    n = len(partials)

    def body(*refs):
        ins, outs = refs[:n], refs[n:2 * n]
        send_sems, recv_sems, local_sems = refs[2 * n:]
        x, y, c = _me()
        chip = 2 * x + y
        local = [pltpu.make_async_copy(ins[w].at[chip], outs[w].at[chip], local_sems.at[w]) for w in range(n)]
        for cp in local:
            cp.start()

        def copy(w, m):
            return pltpu.make_async_remote_copy(
                src_ref=ins[w].at[chip ^ m], dst_ref=outs[w].at[chip],
                send_sem=send_sems.at[3 * w + m - 1], recv_sem=recv_sems.at[3 * w + m - 1],
                device_id=_xor_peer(x, y, c, 2 * m), device_id_type=MESH)

        def landed(w, m):
            return pltpu.make_async_remote_copy(
                src_ref=ins[w].at[chip ^ m], dst_ref=outs[w].at[chip ^ m],
                send_sem=send_sems.at[3 * w + m - 1], recv_sem=recv_sems.at[3 * w + m - 1],
                device_id=_xor_peer(x, y, c, 2 * m), device_id_type=MESH)

        for w in range(n):
            for m in range(1, 4):
                copy(w, m).start()
        for w in range(n):
            for m in range(1, 4):
                landed(w, m).wait_recv()
        for w in range(n):
            for m in range(1, 4):
                copy(w, m).wait_send()
        for cp in local:
            cp.wait()

    hbm = pl.BlockSpec(memory_space=pl.ANY)
    return pl.pallas_call(
        body, name="chip_exchange", in_specs=[hbm] * n, out_specs=[hbm] * n,
        out_shape=[jax.ShapeDtypeStruct(p.shape, F32) for p in partials],
        scratch_shapes=[pltpu.SemaphoreType.DMA((3 * n,)), pltpu.SemaphoreType.DMA((3 * n,)),
                        pltpu.SemaphoreType.DMA((n,))],
        compiler_params=_cp(),
    )(*partials)


def _pair_share(halves):
    n = len(halves)

    def body(*refs):
        ins, outs = refs[:n], refs[n:2 * n]
        send_sems, recv_sems, local_sems = refs[2 * n:]
        x, y, c = _me()

        def rows(w, which):
            h = halves[w].shape[0]
            return pl.ds(which * h, h)

        local = [pltpu.make_async_copy(ins[w], outs[w].at[rows(w, c)], local_sems.at[w]) for w in range(n)]
        for cp in local:
            cp.start()

        def copy(w, which):
            return pltpu.make_async_remote_copy(
                src_ref=ins[w], dst_ref=outs[w].at[rows(w, which)],
                send_sem=send_sems.at[w], recv_sem=recv_sems.at[w], device_id=(x, y, 1 - c), device_id_type=MESH)

        for w in range(n):
            copy(w, c).start()
        for w in range(n):
            copy(w, 1 - c).wait_recv()
        for w in range(n):
            copy(w, c).wait_send()
        for cp in local:
            cp.wait()

    hbm = pl.BlockSpec(memory_space=pl.ANY)
    return pl.pallas_call(
        body, name="pair_share", in_specs=[hbm] * n, out_specs=[hbm] * n,
        out_shape=[jax.ShapeDtypeStruct((2 * h.shape[0], h.shape[1]), F32) for h in halves],
        scratch_shapes=[pltpu.SemaphoreType.DMA((n,)), pltpu.SemaphoreType.DMA((n,)), pltpu.SemaphoreType.DMA((n,))],
        compiler_params=_cp(),
    )(*halves)


def _add_my_half(g, part, c_idx, name):
    _, h, C = part.shape
    tr = min(128, h)
    nb = h // tr

    def body(c_ref, g_ref, p_ref, o_ref):
        del c_ref
        o_ref[...] = g_ref[...] + p_ref[...]

    return pl.pallas_call(
        body, name=name,
        grid_spec=pltpu.PrefetchScalarGridSpec(
            num_scalar_prefetch=1, grid=(N_CHIPS, nb),
            in_specs=[pl.BlockSpec((None, tr, C), lambda s, i, c_ref: (s, c_ref[0] * nb + i, 0)),
                      pl.BlockSpec((None, tr, C), lambda s, i, c_ref: (s, i, 0))],
            out_specs=pl.BlockSpec((None, tr, C), lambda s, i, c_ref: (s, i, 0))),
        out_shape=jax.ShapeDtypeStruct(part.shape, F32), compiler_params=_cp(("parallel", "parallel")),
    )(c_idx, g, part)


def _sum_slots(p, name):
    _, h, C = p.shape
    tr = min(128, h)

    def body(p_ref, o_ref):
        o_ref[...] = ((p_ref[0] + p_ref[1]) + p_ref[2]) + p_ref[3]

    return pl.pallas_call(
        body, name=name, grid=(h // tr,), in_specs=[pl.BlockSpec((N_CHIPS, tr, C), lambda i: (0, i, 0))],
        out_specs=pl.BlockSpec((tr, C), lambda i: (i, 0)), out_shape=jax.ShapeDtypeStruct((h, C), F32),
        compiler_params=_cp(("parallel",)),
    )(p)


def _w_ada_grad(c8, dmod_cols):
    n_cols = dmod_cols.shape[1]
    tn = 512

    def body(c_ref, d_ref, o_ref):
        cv = c_ref[...]
        o_ref[...] = _tn(cv * _sigmoid(cv), d_ref[...], precision=HIGH)

    return pl.pallas_call(
        body, name="w_ada_grad", grid=(n_cols // tn,),
        in_specs=[pl.BlockSpec((8, D), lambda j: (0, 0)), pl.BlockSpec((8, tn), lambda j: (0, j))],
        out_specs=pl.BlockSpec((D, tn), lambda j: (0, j)), out_shape=jax.ShapeDtypeStruct((D, n_cols), F32),
        compiler_params=_cp(("parallel",)),
    )(c8, dmod_cols)


_SMALL_SEGS = (("dmod", 6144), ("norm_mix_w", 1024), ("conv_b", 3072), ("ssd_norm_w", 2048), ("pool_scale", 1024),
               ("norm_mlp_w", 1024), ("norm_final_w", 1024), ("conv_w", 4 * XBC), ("d_skip", 2048), ("a_log", 128),
               ("dt_bias", 128), ("loss", 128))
SMALL_OFF = {}
_o = 0
for _n, _s in _SMALL_SEGS:
    SMALL_OFF[_n] = _o
    _o += _s
SMALL_LEN = -(-_o // 1024) * 1024


def _local_step(x, mod, target, p):
    T = x.shape[0]
    TM = min(1024, T)
    shift_m, scale_m, gate_m, shift_f, scale_f, gate_f = [mod[:, D * i:D * (i + 1)] for i in range(6)]

    h1 = _norm_mod(x, p["norm_mix_w"], scale_m, shift_m, "norm_mod_mix")
    (proj,) = _matmul(h1, p["w_in"], mode="nn", tm=TM, tn=768, tk=D, out_dtypes=[F32], name="mm_proj")
    xbc_a = _conv_fwd(proj, p["conv_w"], p["conv_b"])
    dtT = proj[:, OFF_DT:OFF_DT + HEADS].T
    dtb_r, alog_r = p["dt_bias"], p["a_log"]
    dtb_c, alog_c = dtb_r.reshape(HEADS, 1), alog_r.reshape(HEADS, 1)
    dsk_exp = jnp.repeat(p["d_skip"], HEAD_DIM, axis=1)
    y, hin = _ssd_fwd(xbc_a, proj, dtT, dtb_r, alog_r, dtb_c, alog_c, dsk_exp)
    yn = _gate_norm(y, proj, p["ssd_norm_w"])
    (y_ssd,) = _matmul(yn, p["w_branch_ssd"], mode="nn", tm=TM, tn=D, tk=DI, out_dtypes=[F32], name="mm_branch_ssd")
    pooled, pw_out, yps = _pool_fwd(proj, p["pool_w"], p["pool_scale"])
    (y_pool,) = _matmul(yps, p["w_branch_pool"], mode="nn", tm=TM, tn=D, tk=D, out_dtypes=[F32], name="mm_branch_pool")
    merged = _merge(proj, y_ssd, y_pool)
    resid = lambda acc, r, g: (r + g * acc, acc)
    x2, mix = _matmul(merged, p["w_out"], mode="nn", tm=512, tn=D, tk=D, out_dtypes=[F32, F32], name="mm_out",
                      epi=resid, tile_extras=(x,), row_extras=(gate_m,))
    h2 = _norm_mod(x2, p["norm_mlp_w"], scale_f, shift_f, "norm_mod_mlp")
    relu2 = lambda acc: (acc, jnp.square(jnp.maximum(acc, 0.0)))
    up, act = _matmul(h2, p["w_up"], mode="nn", tm=TM, tn=D, tk=D, out_dtypes=[F32, BF16], name="mm_up", epi=relu2)
    x3, down = _matmul(act, p["w_down"], mode="nn", tm=512, tn=D, tk=1024, out_dtypes=[F32, F32], name="mm_down",
                       epi=resid, tile_extras=(x2,), row_extras=(gate_f,))

    dx3, d_down, sums_f = _final_loss_bwd(x3, target, p["norm_final_w"], down, gate_f)
    drelu2 = lambda acc, u: (acc * (2.0 * jnp.maximum(u, 0.0)),)
    (dup,) = _matmul(d_down, p["w_down"], mode="nt", tm=TM, tn=D, tk=D, out_dtypes=[BF16], name="mm_dact",
                     epi=drelu2, tile_extras=(up,))
    (g_w_down,) = _matmul(act, d_down, mode="tn", tm=1024, tn=D, tk=512, out_dtypes=[F32], name="mm_g_down")
    (dh2,) = _matmul(dup, p["w_up"], mode="nt", tm=TM, tn=D, tk=1024, out_dtypes=[F32], name="mm_dh2")
    (g_w_up,) = _matmul(h2, dup, mode="tn", tm=1024, tn=1024, tk=512, out_dtypes=[F32], name="mm_g_up")
    dx2, sums_2, dmix = _norm_mod_bwd(x2, dh2, dx3, p["norm_mlp_w"], scale_f, "norm_mod_mlp_bwd", branch=mix, gate=gate_m)
    (dmerged,) = _matmul(dmix, p["w_out"], mode="nt", tm=TM, tn=D, tk=D, out_dtypes=[F32], name="mm_dmerged")
    (g_w_out,) = _matmul(merged, dmix, mode="tn", tm=1024, tn=D, tk=512, out_dtypes=[F32], name="mm_g_out")
    dy_ssd, dy_pool, dproj = _merge_bwd(dmerged, proj, y_ssd, y_pool)
    (dyn,) = _matmul(dy_ssd, p["w_branch_ssd"], mode="nt", tm=TM, tn=1024, tk=D, out_dtypes=[F32], name="mm_dyn")
    (g_w_bssd,) = _matmul(yn, dy_ssd, mode="tn", tm=1024, tn=D, tk=512, out_dtypes=[F32], name="mm_g_bssd")
    dy, dproj, sums_gn = _gate_norm_bwd(dyn, y, proj, p["ssd_norm_w"], dproj)
    dxa, dproj, dsk_sum, ssd_small = _ssd_bwd(dy, xbc_a, proj, dtT, hin, dtb_r, alog_r, dtb_c, alog_c, dsk_exp, dproj)
    dxc, sums_conv = _conv_bwd_a(dxa, proj, p["conv_w"], p["conv_b"])
    dproj = _conv_bwd_b(dxc, p["conv_w"], dproj)
    (dyp,) = _matmul(dy_pool, p["w_branch_pool"], mode="nt", tm=TM, tn=D, tk=D, out_dtypes=[F32], name="mm_dyp")
    (g_w_bpool,) = _matmul(yps, dy_pool, mode="tn", tm=1024, tn=D, tk=512, out_dtypes=[F32], name="mm_g_bpool")
    dproj, g_pool_w, sums_pool = _pool_bwd(dyp, pw_out, pooled, p["pool_w"], p["pool_scale"], dproj)
    (dh1,) = _matmul(dproj, p["w_in"], mode="nt", tm=TM, tn=D, tk=768, out_dtypes=[F32], name="mm_dh1")
    (g_w_in,) = _matmul(h1, dproj, mode="tn", tm=1024, tn=768, tk=512, out_dtypes=[F32], name="mm_g_in")
    grad_x, sums_1 = _norm_mod_bwd(x, dh1, dx2, p["norm_mix_w"], scale_m, "norm_mod_mix_bwd")

    big = {"w_in": g_w_in, "w_branch_ssd": g_w_bssd, "pool_w": g_pool_w, "w_branch_pool": g_w_bpool,
           "w_out": g_w_out, "w_up": g_w_up, "w_down": g_w_down}
    dmod = jnp.concatenate([sums_1[0:1], sums_1[1:2], sums_2[3:4], sums_2[0:1], sums_2[1:2], sums_f[1:2]], axis=1)
    pad96 = jnp.zeros((1, 96), F32)
    small = {"dmod": dmod, "norm_mix_w": sums_1[2:3], "conv_b": sums_conv[4:5], "ssd_norm_w": sums_gn[0:1],
             "pool_scale": sums_pool[0:1], "norm_mlp_w": sums_2[2:3], "norm_final_w": sums_f[0:1],
             "conv_w": sums_conv[0:4].reshape(1, 4 * XBC), "d_skip": dsk_sum[0:1],
             "a_log": jnp.concatenate([ssd_small[0:1], pad96], axis=1),
             "dt_bias": jnp.concatenate([ssd_small[1:2], pad96], axis=1), "loss": sums_f[3:4, 0:128]}
    return grad_x, big, small


_BIG = ("w_in", "w_branch_ssd", "pool_w", "w_branch_pool", "w_out", "w_up", "w_down")
_SMALL_REPLICATED = ("b_ada", "norm_mix_w", "conv_b", "dt_bias", "a_log", "d_skip", "ssd_norm_w", "pool_scale",
                     "norm_mlp_w", "norm_final_w")
_WEIGHTS = ("w_ada", "b_ada", "norm_mix_w", "w_in", "conv_w", "conv_b", "dt_bias", "a_log", "d_skip", "ssd_norm_w",
            "w_branch_ssd", "pool_w", "pool_scale", "w_branch_pool", "w_out", "norm_mlp_w", "w_up", "w_down",
            "norm_final_w")


def _shard_2d(name, a):
    return a.reshape(GW, GW) if name == "pool_w" else a.reshape(a.shape[-2], a.shape[-1])


def _whole_from_chips(name, g):
    if name == "w_in":
        return _perm_cols(jnp.transpose(g, (1, 0, 2)).reshape(D, IN_COLS))
    if name == "w_up":
        return jnp.transpose(g, (1, 0, 2)).reshape(D, DFF)
    if name == "pool_w":
        return jnp.transpose(g.reshape(N_CHIPS, 4, GW // N_CHIPS, GW), (1, 0, 2, 3)).reshape(4, GW, GW)
    if name == "conv_w":
        return jnp.transpose(g.reshape(N_CHIPS, 4, XBC // N_CHIPS), (1, 0, 2)).reshape(4, XBC)
    return g.reshape(N_CHIPS * g.shape[1], g.shape[2])


def _chips_from_whole(name, g):
    if name == "w_in":
        return jnp.transpose(_unperm_cols(g).reshape(D, N_CHIPS, IN_COLS // N_CHIPS), (1, 0, 2))
    if name == "w_up":
        return jnp.transpose(g.reshape(D, N_CHIPS, DFF // N_CHIPS), (1, 0, 2))
    if name == "pool_w":
        return jnp.transpose(g.reshape(4, N_CHIPS, GW // N_CHIPS, GW), (1, 0, 2, 3)).reshape(N_CHIPS, GW, GW)
    return g.reshape(N_CHIPS, g.shape[0] // N_CHIPS, g.shape[1])


def kernel(x, c, w_ada, b_ada, norm_mix_w, w_in, conv_w, conv_b, dt_bias, a_log, d_skip, ssd_norm_w, w_branch_ssd, pool_w, pool_scale, w_branch_pool, w_out, norm_mlp_w, w_up, w_down, norm_final_w, loss_target, m_w_ada, m_b_ada, m_norm_mix_w, m_w_in, m_conv_w, m_conv_b, m_dt_bias, m_a_log, m_d_skip, m_ssd_norm_w, m_w_branch_ssd, m_pool_w, m_pool_scale, m_w_branch_pool, m_w_out, m_norm_mlp_w, m_w_up, m_w_down, m_norm_final_w, v_w_ada, v_b_ada, v_norm_mix_w, v_w_in, v_conv_w, v_conv_b, v_dt_bias, v_a_log, v_d_skip, v_ssd_norm_w, v_w_branch_ssd, v_pool_w, v_pool_scale, v_w_branch_pool, v_w_out, v_norm_mlp_w, v_w_up, v_w_down, v_norm_final_w):
    args = locals()
    w = {n: args[n] for n in _WEIGHTS}
    m = {n: args["m_" + n] for n in _WEIGHTS}
    v = {n: args["v_" + n] for n in _WEIGHTS}
    xi, yi, ci = _me()
    chip = 2 * xi + yi
    ada_cols = w_ada.shape[-1]

    b_mine = lax.dynamic_slice(b_ada, (0, chip * ada_cols), (1, ada_cols))
    mod, c8 = _ada_fwd(c, w_ada[0], b_mine)
    c8 = c8[:, 0, :]

    gathered = _gather_weights([_shard_2d(n, w[n]).astype(BF16) for n in _BIG] + [conv_w.reshape(16, -1)])
    p = {n: _whole_from_chips(n, g) for n, g in zip(_BIG + ("conv_w",), gathered)}
    p.update(norm_mix_w=norm_mix_w, conv_b=conv_b, dt_bias=dt_bias, a_log=a_log, d_skip=d_skip, ssd_norm_w=ssd_norm_w,
             pool_scale=pool_scale, norm_mlp_w=norm_mlp_w, norm_final_w=norm_final_w.reshape(1, D))

    grad_x, big, small = _local_step(x[0], mod, loss_target[0], p)

    vec = jnp.concatenate([small[n] for n, _ in _SMALL_SEGS], axis=1)
    vec = jnp.pad(vec, ((0, 0), (0, SMALL_LEN - vec.shape[1]))).reshape(SMALL_LEN // 128, 128)
    every, total, dsk = _gather_small(vec)
    total = total.reshape(1, SMALL_LEN)
    seg = lambda n, size: total[:, SMALL_OFF[n]:SMALL_OFF[n] + size]
    g = {"b_ada": seg("dmod", 6 * D), "norm_mix_w": seg("norm_mix_w", D), "conv_b": seg("conv_b", XBC),
         "dt_bias": seg("dt_bias", HEADS), "a_log": seg("a_log", HEADS), "d_skip": dsk[:, 0:2].reshape(1, HEADS),
         "ssd_norm_w": seg("ssd_norm_w", DI), "pool_scale": seg("pool_scale", D), "norm_mlp_w": seg("norm_mlp_w", D),
         "norm_final_w": seg("norm_final_w", D)}
    loss = total[0, SMALL_OFF["loss"]]
    conv_cols = conv_w.shape[-1]
    g["conv_w"] = lax.dynamic_slice(seg("conv_w", 4 * XBC).reshape(4, XBC), (0, chip * conv_cols), (4, conv_cols))
    dmod8 = every.reshape(8, SMALL_LEN)[:, SMALL_OFF["dmod"]:SMALL_OFF["dmod"] + 6 * D]
    g["w_ada"] = _w_ada_grad(c8, lax.dynamic_slice(dmod8, (0, chip * ada_cols), (8, ada_cols)))

    chips = [_chips_from_whole(n, big[n]) for n in _BIG]
    c_idx = jnp.reshape(ci, (1,)).astype(jnp.int32)
    from_sibling = _pair_exchange(chips)
    pair_sums = [_add_my_half(a, b, c_idx, "pair_sum_" + n) for n, a, b in zip(_BIG, chips, from_sibling)]
    slots = _chip_exchange(pair_sums)
    mine = [_sum_slots(s, "chip_sum_" + n) for n, s in zip(_BIG, slots)]
    for n, s in zip(_BIG, _pair_share(mine)):
        g[n] = s

    delta, new_m, new_v = {}, {}, {}
    two_d = lambda n, a: a.reshape(-1, a.shape[-1]) if n != "pool_w" else a.reshape(GW, GW)
    for n in ("w_ada", "conv_w") + _BIG:
        d_, m_, v_ = _adamw(two_d(n, w[n]), two_d(n, g[n]), two_d(n, m[n]), two_d(n, v[n]), "adamw_" + n)
        delta[n], new_m[n], new_v[n] = d_, m_, v_
    sizes = [w[n].size for n in _SMALL_REPLICATED]
    n_small = -(-sum(sizes) // 1024) * 1024
    pack = lambda d: jnp.pad(jnp.concatenate([d[n].reshape(1, -1) for n in _SMALL_REPLICATED], axis=1),
                             ((0, 0), (0, n_small - sum(sizes)))).reshape(n_small // 128, 128)
    d_, m_, v_ = _adamw(pack(w), pack(g), pack(m), pack(v), "adamw_small")
    off = 0
    for n, s in zip(_SMALL_REPLICATED, sizes):
        for dst, src in ((delta, d_), (new_m, m_), (new_v, v_)):
            dst[n] = src.reshape(1, n_small)[:, off:off + s]
        off += s

    out = [loss, grad_x.reshape(x.shape)]
    for d in (g, delta, new_m, new_v):
        out += [d[n].reshape(w[n].shape) for n in _WEIGHTS]
    return tuple(out)
```

```python
import functools
import operator

import jax
import jax.numpy as jnp
from jax import lax
from jax.experimental import pallas as pl
from jax.experimental.pallas import tpu as pltpu

F32, BF16 = jnp.float32, jnp.bfloat16
HIGH = lax.Precision.HIGHEST
MESH = pl.DeviceIdType.MESH

D = 1024
DI = 2048
HEADS, HEAD_DIM = 32, 64
GROUPS, STATE = 4, 128
Q = 128
XBC = DI + 2 * GROUPS * STATE
POOL_WINDOWS = (2, 4, 8, 16)
GW = 256
DFF = 4096
EPS = 1e-5
IN_COLS = 8224
OFF_Z, OFF_XBC, OFF_POOL, OFF_GATE, OFF_DT, NP = 0, 2048, 5120, 6144, 8192, 8448
N_CHIPS = 4
ADAM_LR, ADAM_B1, ADAM_B2, ADAM_EPS, ADAM_WD, ADAM_STEP = 0.001, 0.9, 0.999, 1e-08, 0.01, 10
VMEM_LIMIT = 56 * 2 ** 20
NEG = -1e30


def _sigmoid(v):
    return 1.0 / (1.0 + jnp.exp(-v))


def _softplus(v):
    return jnp.maximum(v, 0.0) + jnp.log1p(jnp.exp(-jnp.abs(v)))


def _dot(a, b, dims, **kw):
    return lax.dot_general(a, b, (dims, ((), ())), preferred_element_type=F32, **kw)


def _nn(a, b, **kw):
    return _dot(a, b, ((1,), (0,)), **kw)


def _nt(a, b, **kw):
    return _dot(a, b, ((1,), (1,)), **kw)


def _tn(a, b, **kw):
    return _dot(a, b, ((0,), (0,)), **kw)


def _perm_cols(w):
    pad = jnp.zeros(w.shape[:-1] + (NP - IN_COLS,), w.dtype)
    return jnp.concatenate([w[..., :5120], w[..., 5152:], w[..., 5120:5152], pad], axis=-1)


def _unperm_cols(g):
    return jnp.concatenate([g[..., :5120], g[..., OFF_DT:OFF_DT + 32], g[..., 5120:OFF_DT]], axis=-1)


class _Sems:
    def __init__(self, send, recv, local, base=0):
        self._send, self._recv, self._local, self._base = send, recv, local, base

    def shift(self, n):
        return _Sems(self._send, self._recv, self._local, self._base + n)

    def send(self, i):
        return self._send.at[self._base + i]

    def recv(self, i):
        return self._recv.at[self._base + i]

    def local(self, i):
        return self._local.at[self._base + i]


class _Carry:
    def __init__(self, ins, out_shapes, n_sems, start, finish):
        self.ins, self.out_shapes, self.n_sems, self.start, self.finish = list(ins), list(out_shapes), n_sems, start, finish


def _join(*carries):
    def run(which):
        def fn(ins, outs, sems):
            i = o = s = 0
            for cy in carries:
                getattr(cy, which)(ins[i:i + len(cy.ins)], outs[o:o + len(cy.out_shapes)], sems.shift(s))
                i, o, s = i + len(cy.ins), o + len(cy.out_shapes), s + cy.n_sems
        return fn

    return _Carry([a for cy in carries for a in cy.ins], [a for cy in carries for a in cy.out_shapes],
                  sum(cy.n_sems for cy in carries), run("start"), run("finish"))


def _call(body, args, *, name, grid=(), in_specs, out_specs, out_shape, scratch_shapes=(), sem=None, aliases=None,
          carry=None):
    in_specs, out_specs, out_shape, scratch_shapes = list(in_specs), list(out_specs), list(out_shape), list(scratch_shapes)
    n_in, n_out, n_scr = len(in_specs), len(out_specs), len(scratch_shapes)
    kw = {"vmem_limit_bytes": VMEM_LIMIT}
    if carry is None:
        kernel_fn = functools.partial(body)
        if sem is not None:
            kw["dimension_semantics"] = sem
    else:
        n_ci, n_co = len(carry.ins), len(carry.out_shapes)
        hbm = pl.BlockSpec(memory_space=pl.ANY)
        in_specs += [hbm] * n_ci
        out_specs += [hbm] * n_co
        out_shape += carry.out_shapes
        n_s = max(carry.n_sems, 1)
        scratch_shapes += [pltpu.SemaphoreType.DMA((n_s,))] * 3
        args = list(args) + carry.ins
        if grid:
            kw["dimension_semantics"] = ("arbitrary",) * len(grid)

        def kernel_fn(*refs):
            a = n_in
            ins, c_ins = refs[:a], refs[a:a + n_ci]
            a += n_ci
            outs, c_outs = refs[a:a + n_out], refs[a + n_out:a + n_out + n_co]
            a += n_out + n_co
            scr, sems = refs[a:a + n_scr], _Sems(*refs[a + n_scr:a + n_scr + 3])
            if grid:
                ids = [pl.program_id(d) for d in range(len(grid))]
                first = functools.reduce(operator.and_, [i == 0 for i in ids])
                last = functools.reduce(operator.and_, [i == g - 1 for i, g in zip(ids, grid)])

                @pl.when(first)
                def _():
                    carry.start(c_ins, c_outs, sems)

                body(*ins, *outs, *scr)

                @pl.when(last)
                def _():
                    carry.finish(c_ins, c_outs, sems)
            else:
                carry.start(c_ins, c_outs, sems)
                body(*ins, *outs, *scr)
                carry.finish(c_ins, c_outs, sems)

    outs = pl.pallas_call(
        kernel_fn, name=name, grid=grid, in_specs=in_specs, out_specs=out_specs, out_shape=out_shape,
        scratch_shapes=scratch_shapes, input_output_aliases=aliases or {},
        compiler_params=pltpu.CompilerParams(**kw),
    )(*args)
    outs = list(outs)
    return outs if carry is None else (outs[:n_out], outs[n_out:])


def _run_carry(carry, name):
    _, outs = _call(lambda: None, [], name=name, in_specs=[], out_specs=[], out_shape=[], carry=carry)
    return outs


def _matmul(a, b, *, mode, tm, tn, tk, out_dtypes, name, epi=None, tile_extras=(), row_extras=(), carry=None):
    M, K = (a.shape[1], a.shape[0]) if mode == "tn" else a.shape
    N = b.shape[0] if mode == "nt" else b.shape[1]
    tm, tn, tk = min(tm, M), min(tn, N), min(tk, K)
    assert M % tm == 0 and N % tn == 0 and K % tk == 0, (name, M, N, K, tm, tn, tk)
    if mode == "nn":
        a_spec = pl.BlockSpec((tm, tk), lambda i, j, k: (i, k))
        b_spec = pl.BlockSpec((tk, tn), lambda i, j, k: (k, j))
        dims = ((1,), (0,))
    elif mode == "nt":
        a_spec = pl.BlockSpec((tm, tk), lambda i, j, k: (i, k))
        b_spec = pl.BlockSpec((tn, tk), lambda i, j, k: (j, k))
        dims = ((1,), (1,))
    else:
        a_spec = pl.BlockSpec((tk, tm), lambda i, j, k: (k, i))
        b_spec = pl.BlockSpec((tk, tn), lambda i, j, k: (k, j))
        dims = ((0,), (0,))
    nk = K // tk
    n_te, n_re, n_out = len(tile_extras), len(row_extras), len(out_dtypes)
    if epi is None:
        epi = lambda acc: (acc,)

    def body(a_ref, b_ref, *rest):
        extras = rest[:n_te + n_re]
        outs = rest[n_te + n_re:n_te + n_re + n_out]
        p = _dot(a_ref[...], b_ref[...], dims)

        def finish(acc):
            vals = epi(acc, *[e[...] for e in extras])
            for o, v in zip(outs, vals):
                o[...] = v.astype(o.dtype)

        if nk == 1:
            finish(p)
        else:
            acc_ref = rest[-1]
            k = pl.program_id(2)

            @pl.when(k == 0)
            def _():
                acc_ref[...] = p

            @pl.when(k > 0)
            def _():
                acc_ref[...] += p

            @pl.when(k == nk - 1)
            def _():
                finish(acc_ref[...])

    tile_spec = pl.BlockSpec((tm, tn), lambda i, j, k: (i, j))
    row_spec = pl.BlockSpec((1, tn), lambda i, j, k: (0, j))
    return _call(
        body, [a, b, *tile_extras, *row_extras], name=name, grid=(M // tm, N // tn, nk),
        in_specs=[a_spec, b_spec] + [tile_spec] * n_te + [row_spec] * n_re, out_specs=[tile_spec] * n_out,
        out_shape=[jax.ShapeDtypeStruct((M, N), dt) for dt in out_dtypes],
        scratch_shapes=[pltpu.VMEM((tm, tn), F32)] if nk > 1 else [],
        sem=("parallel", "parallel", "arbitrary"), carry=carry)


def _row_tile(T):
    return min(512, T)


def _norm_mod(x, nw, scale, shift, name):
    T = x.shape[0]
    tr = _row_tile(T)

    def body(x_ref, nw_ref, sc_ref, sh_ref, o_ref):
        xv = x_ref[...]
        r = lax.rsqrt(jnp.mean(xv * xv, axis=-1, keepdims=True) + EPS)
        o_ref[...] = ((xv * r) * nw_ref[...] * (1.0 + sc_ref[...]) + sh_ref[...]).astype(BF16)

    tile = pl.BlockSpec((tr, D), lambda i: (i, 0))
    row = pl.BlockSpec((1, D), lambda i: (0, 0))
    return _call(body, [x, nw, scale, shift], name=name, grid=(T // tr,), in_specs=[tile, row, row, row],
                 out_specs=[tile], out_shape=[jax.ShapeDtypeStruct((T, D), BF16)], sem=("parallel",))[0]


def _norm_mod_bwd(x, dh, dres, nw, scale, name, branch=None, gate=None, carry=None):
    T = x.shape[0]
    tr = _row_tile(T)
    with_branch = branch is not None

    def body(x_ref, dh_ref, dr_ref, nw_ref, sc_ref, *rest):
        if with_branch:
            br_ref, g_ref, dx_ref, sums_ref, db_ref = rest
        else:
            dx_ref, sums_ref = rest
        i = pl.program_id(0)

        @pl.when(i == 0)
        def _():
            sums_ref[...] = jnp.zeros_like(sums_ref)

        xv, dhv = x_ref[...], dh_ref[...]
        r = lax.rsqrt(jnp.mean(xv * xv, axis=-1, keepdims=True) + EPS)
        xn = xv * r
        g1 = dhv * (1.0 + sc_ref[...])
        dxn = g1 * nw_ref[...]
        dx = dr_ref[...] + r * (dxn - xn * jnp.mean(dxn * xn, axis=-1, keepdims=True))
        dx_ref[...] = dx
        sums_ref[0:1, :] += jnp.sum(dhv, axis=0, keepdims=True)
        sums_ref[1:2, :] += jnp.sum(dhv * (xn * nw_ref[...]), axis=0, keepdims=True)
        sums_ref[2:3, :] += jnp.sum(g1 * xn, axis=0, keepdims=True)
        if with_branch:
            db_ref[...] = (dx * g_ref[...]).astype(BF16)
            sums_ref[3:4, :] += jnp.sum(dx * br_ref[...], axis=0, keepdims=True)

    tile = pl.BlockSpec((tr, D), lambda i: (i, 0))
    row = pl.BlockSpec((1, D), lambda i: (0, 0))
    sums = pl.BlockSpec((8, D), lambda i: (0, 0))
    ins = [x, dh, dres, nw, scale] + ([branch, gate] if with_branch else [])
    in_specs = [tile, tile, tile, row, row] + ([tile, row] if with_branch else [])
    out_specs = [tile, sums] + ([tile] if with_branch else [])
    out_shape = [jax.ShapeDtypeStruct((T, D), F32), jax.ShapeDtypeStruct((8, D), F32)]
    if with_branch:
        out_shape.append(jax.ShapeDtypeStruct((T, D), BF16))
    return _call(body, ins, name=name, grid=(T // tr,), in_specs=in_specs, out_specs=out_specs, out_shape=out_shape,
                 sem=("arbitrary",), carry=carry)


def _final_loss_bwd(x3, target, wf, down, gate_f):
    T = x3.shape[0]
    tr = _row_tile(T)
    n_steps = T // tr

    def body(x_ref, t_ref, w_ref, dn_ref, g_ref, dx_ref, dd_ref, sums_ref):
        i = pl.program_id(0)

        @pl.when(i == 0)
        def _():
            sums_ref[...] = jnp.zeros_like(sums_ref)

        xv = x_ref[...]
        r = lax.rsqrt(jnp.mean(xv * xv, axis=-1, keepdims=True) + EPS)
        xn = xv * r
        err = xn * w_ref[...] - t_ref[...]
        dy = err * (1.0 / D)
        dxn = dy * w_ref[...]
        dx = r * (dxn - xn * jnp.mean(dxn * xn, axis=-1, keepdims=True))
        dx_ref[...] = dx
        dd_ref[...] = (dx * g_ref[...]).astype(BF16)
        sums_ref[0:1, :] += jnp.sum(dy * xn, axis=0, keepdims=True)
        sums_ref[1:2, :] += jnp.sum(dx * dn_ref[...], axis=0, keepdims=True)
        sums_ref[2:3, :] += jnp.sum(err * err, axis=0, keepdims=True) * (0.5 / D)

        @pl.when(i == n_steps - 1)
        def _():
            sums_ref[3:4, :] = jnp.broadcast_to(jnp.sum(sums_ref[2:3, :], axis=1, keepdims=True), (1, D))

    tile = pl.BlockSpec((tr, D), lambda i: (i, 0))
    row = pl.BlockSpec((1, D), lambda i: (0, 0))
    sums = pl.BlockSpec((8, D), lambda i: (0, 0))
    return _call(body, [x3, target, wf, down, gate_f], name="final_loss_bwd", grid=(n_steps,),
                 in_specs=[tile, tile, row, tile, row], out_specs=[tile, tile, sums],
                 out_shape=[jax.ShapeDtypeStruct((T, D), F32), jax.ShapeDtypeStruct((T, D), BF16),
                            jax.ShapeDtypeStruct((8, D), F32)], sem=("arbitrary",))


CONV_TC = 512


def _conv_taps(xp, w, b):
    acc = b + w[3:4, :] * xp
    for k in range(3):
        acc = acc + w[k:k + 1, :] * pltpu.roll(xp, 3 - k, 0)
    return acc


def _conv_fwd(proj, conv_w, conv_b):
    T = proj.shape[0]
    tr = _row_tile(T)
    nb, offb = tr // 8, OFF_XBC // CONV_TC

    def body(x_ref, h_ref, w_ref, b_ref, o_ref):
        halo = jnp.where(pl.program_id(0) > 0, h_ref[...], 0.0)
        xp = jnp.concatenate([halo, x_ref[...]], axis=0)
        acc = _conv_taps(xp, w_ref[...], b_ref[...])[8:]
        o_ref[...] = acc * _sigmoid(acc)

    return _call(
        body, [proj, proj, conv_w, conv_b], name="conv_fwd", grid=(T // tr, XBC // CONV_TC),
        in_specs=[pl.BlockSpec((tr, CONV_TC), lambda i, j: (i, j + offb)),
                  pl.BlockSpec((8, CONV_TC), lambda i, j: (jnp.maximum(i * nb - 1, 0), j + offb)),
                  pl.BlockSpec((4, CONV_TC), lambda i, j: (0, j)),
                  pl.BlockSpec((1, CONV_TC), lambda i, j: (0, j))],
        out_specs=[pl.BlockSpec((tr, CONV_TC), lambda i, j: (i, j))],
        out_shape=[jax.ShapeDtypeStruct((T, XBC), F32)], sem=("parallel", "parallel"))[0]


def _conv_bwd_a(dxa, proj, conv_w, conv_b):
    T = proj.shape[0]
    tr = _row_tile(T)
    nb, offb = tr // 8, OFF_XBC // CONV_TC

    def body(d_ref, x_ref, h_ref, w_ref, b_ref, o_ref, sums_ref):
        i = pl.program_id(1)

        @pl.when(i == 0)
        def _():
            sums_ref[...] = jnp.zeros_like(sums_ref)

        halo = jnp.where(i > 0, h_ref[...], 0.0)
        xp = jnp.concatenate([halo, x_ref[...]], axis=0)
        acc = _conv_taps(xp, w_ref[...], b_ref[...])[8:]
        s = _sigmoid(acc)
        dxc = d_ref[...] * (s * (1.0 + acc * (1.0 - s)))
        o_ref[...] = dxc
        sums_ref[3:4, :] += jnp.sum(dxc * x_ref[...], axis=0, keepdims=True)
        for k in range(3):
            sums_ref[k:k + 1, :] += jnp.sum(dxc * pltpu.roll(xp, 3 - k, 0)[8:], axis=0, keepdims=True)
        sums_ref[4:5, :] += jnp.sum(dxc, axis=0, keepdims=True)

    return _call(
        body, [dxa, proj, proj, conv_w, conv_b], name="conv_bwd_a", grid=(XBC // CONV_TC, T // tr),
        in_specs=[pl.BlockSpec((tr, CONV_TC), lambda j, i: (i, j)),
                  pl.BlockSpec((tr, CONV_TC), lambda j, i: (i, j + offb)),
                  pl.BlockSpec((8, CONV_TC), lambda j, i: (jnp.maximum(i * nb - 1, 0), j + offb)),
                  pl.BlockSpec((4, CONV_TC), lambda j, i: (0, j)),
                  pl.BlockSpec((1, CONV_TC), lambda j, i: (0, j))],
        out_specs=[pl.BlockSpec((tr, CONV_TC), lambda j, i: (i, j)), pl.BlockSpec((8, CONV_TC), lambda j, i: (0, j))],
        out_shape=[jax.ShapeDtypeStruct((T, XBC), F32), jax.ShapeDtypeStruct((8, XBC), F32)],
        sem=("parallel", "arbitrary"))


def _conv_bwd_b(dxc, conv_w, dproj):
    T = dxc.shape[0]
    tr = _row_tile(T)
    nb, offb, last = tr // 8, OFF_XBC // CONV_TC, T // tr - 1

    def body(d_ref, h_ref, w_ref, dp_in, o_ref):
        del dp_in
        halo = jnp.where(pl.program_id(0) < last, h_ref[...], 0.0)
        xp = jnp.concatenate([d_ref[...], halo], axis=0)
        n = xp.shape[0]
        w = w_ref[...]
        acc = w[3:4, :] * xp
        for k in range(3):
            acc = acc + w[k:k + 1, :] * pltpu.roll(xp, n - (3 - k), 0)
        o_ref[...] = acc[:tr].astype(BF16)

    return _call(
        body, [dxc, dxc, conv_w, dproj], name="conv_bwd_b", grid=(T // tr, XBC // CONV_TC),
        in_specs=[pl.BlockSpec((tr, CONV_TC), lambda i, j: (i, j)),
                  pl.BlockSpec((8, CONV_TC), lambda i, j: (jnp.minimum((i + 1) * nb, T // 8 - 1), j)),
                  pl.BlockSpec((4, CONV_TC), lambda i, j: (0, j)),
                  pl.BlockSpec(memory_space=pl.ANY)],
        out_specs=[pl.BlockSpec((tr, CONV_TC), lambda i, j: (i, j + offb))],
        out_shape=[jax.ShapeDtypeStruct(dproj.shape, BF16)], aliases={3: 0}, sem=("parallel", "parallel"))[0]


def _expand_pairs(v, j0, n_pairs, lo):
    R = v.shape[0]
    parts = []
    for j in range(j0, j0 + n_pairs):
        va = jnp.broadcast_to(v[:, 2 * j:2 * j + 1], (R, 128))
        vb = jnp.broadcast_to(v[:, 2 * j + 1:2 * j + 2], (R, 128))
        parts.append(jnp.where(lo, va, vb))
    return jnp.concatenate(parts, axis=1)


def _ssd_common(dtp_ref, dtT_ref, dtb_r, alog_r, dtb_c, alog_c):
    rows = lax.broadcasted_iota(jnp.int32, (Q, Q), 0)
    cols = lax.broadcasted_iota(jnp.int32, (Q, Q), 1)
    causal = cols <= rows
    tri = causal.astype(F32)
    raw = dtp_ref[:, 0:HEADS] + dtb_r[...]
    dt = _softplus(raw)
    a_r = -jnp.exp(alog_r[...])
    cs = _nn(tri, dt * a_r, precision=HIGH)
    aT = _softplus(dtT_ref[...] + dtb_c[...]) * (-jnp.exp(alog_c[...]))
    csT = _nt(aT, tri, precision=HIGH)
    return causal, raw, dt, a_r, cs, csT


def _ssd_fwd(xbc_a, proj, dtT, dtb_r, alog_r, dtb_c, alog_c, dsk_exp):
    T = xbc_a.shape[0]
    nc = T // Q

    def body(xbc_ref, dtp_ref, dtT_ref, dtb_r_ref, alog_r_ref, dtb_c_ref, alog_c_ref, dsk_ref, y_ref, hin_ref, h_scr):
        @pl.when(pl.program_id(0) == 0)
        def _():
            h_scr[...] = jnp.zeros_like(h_scr)

        causal, _, dt, _, cs, csT = _ssd_common(dtp_ref, dtT_ref, dtb_r_ref, alog_r_ref, dtb_c_ref, alog_c_ref)
        lo = lax.broadcasted_iota(jnp.int32, (1, 128), 1) < HEAD_DIM
        cs_last = cs[Q - 1:Q, :]
        ecs, dec, cd = jnp.exp(cs), jnp.exp(cs_last - cs), jnp.exp(cs_last)
        for g in range(GROUPS):
            gs = slice(512 * g, 512 * (g + 1))
            xs_g = xbc_ref[:, gs]
            b_g = xbc_ref[:, DI + STATE * g:DI + STATE * (g + 1)].astype(BF16)
            c_g = xbc_ref[:, DI + 512 + STATE * g:DI + 512 + STATE * (g + 1)].astype(BF16)
            xdt = xs_g * _expand_pairs(dt, 4 * g, 4, lo)
            xdt_b = xdt.astype(BF16)
            s_mat = _nt(c_g, b_g)
            ys = []
            for jj in range(4):
                xp = xdt_b[:, 128 * jj:128 * (jj + 1)]
                acc = None
                for h, sel in ((8 * g + 2 * jj, lo), (8 * g + 2 * jj + 1, jnp.logical_not(lo))):
                    l_mat = jnp.exp(jnp.where(causal, cs[:, h:h + 1] - csT[h:h + 1, :], NEG))
                    part = _nn((s_mat * l_mat).astype(BF16), jnp.where(sel, xp, jnp.zeros_like(xp)))
                    acc = part if acc is None else acc + part
                ys.append(acc)
            h_g = h_scr[128 * g:128 * (g + 1), :]
            hin_ref[0, 128 * g:128 * (g + 1), :] = h_g
            y_off = _nn(c_g, h_g.astype(BF16)) * _expand_pairs(ecs, 4 * g, 4, lo)
            y_ref[:, gs] = jnp.concatenate(ys, axis=1) + y_off + dsk_ref[:, gs] * xs_g
            xdec = (xdt * _expand_pairs(dec, 4 * g, 4, lo)).astype(BF16)
            h_scr[128 * g:128 * (g + 1), :] = h_g * _expand_pairs(cd, 4 * g, 4, lo) + _tn(b_g, xdec)

    small_r = pl.BlockSpec((1, HEADS), lambda c: (0, 0))
    small_c = pl.BlockSpec((HEADS, 1), lambda c: (0, 0))
    return _call(
        body, [xbc_a, proj, dtT, dtb_r, alog_r, dtb_c, alog_c, dsk_exp], name="ssd_fwd", grid=(nc,),
        in_specs=[pl.BlockSpec((Q, XBC), lambda c: (c, 0)),
                  pl.BlockSpec((Q, 128), lambda c: (c, OFF_DT // 128)),
                  pl.BlockSpec((HEADS, Q), lambda c: (0, c)),
                  small_r, small_r, small_c, small_c,
                  pl.BlockSpec((1, DI), lambda c: (0, 0))],
        out_specs=[pl.BlockSpec((Q, DI), lambda c: (c, 0)), pl.BlockSpec((1, 512, 512), lambda c: (c, 0, 0))],
        out_shape=[jax.ShapeDtypeStruct((T, DI), F32), jax.ShapeDtypeStruct((nc, 512, 512), F32)],
        scratch_shapes=[pltpu.VMEM((512, 512), F32)], sem=("arbitrary",))


def _ssd_bwd(dy, xbc_a, proj, dtT, hin, dtb_r, alog_r, dtb_c, alog_c, dsk_exp, dproj, carry=None):
    T = xbc_a.shape[0]
    nc = T // Q

    def body(dy_ref, xbc_ref, dtp_ref, dtT_ref, hin_ref, dtb_r_ref, alog_r_ref, dtb_c_ref, alog_c_ref, dsk_ref, dp_in,
             dxa_ref, dp_ref, dsk_sum_ref, small_ref, dh_scr):
        del dp_in

        @pl.when(pl.program_id(0) == 0)
        def _():
            dh_scr[...] = jnp.zeros_like(dh_scr)
            dsk_sum_ref[...] = jnp.zeros_like(dsk_sum_ref)
            small_ref[...] = jnp.zeros_like(small_ref)

        causal, raw, dt, a_r, cs, csT = _ssd_common(dtp_ref, dtT_ref, dtb_r_ref, alog_r_ref, dtb_c_ref, alog_c_ref)
        lane = lax.broadcasted_iota(jnp.int32, (1, 128), 1)
        lo = lane < HEAD_DIM
        hi = jnp.logical_not(lo)
        lane32 = lax.broadcasted_iota(jnp.int32, (1, HEADS), 1)
        sub32 = lax.broadcasted_iota(jnp.int32, (HEADS, 1), 0)
        cs_last = cs[Q - 1:Q, :]
        ecs, dec, cd = jnp.exp(cs), jnp.exp(cs_last - cs), jnp.exp(cs_last)
        dcs_c = jnp.zeros((Q, HEADS), F32)
        dcs_r = jnp.zeros((HEADS, Q), F32)
        dcs_l = jnp.zeros((1, HEADS), F32)
        ddt_x = jnp.zeros((Q, HEADS), F32)

        def put(vec, h, val):
            return vec + jnp.where(lane32 == h, val, 0.0)

        def halves(v):
            sa = jnp.sum(jnp.where(lo, v, 0.0), axis=1, keepdims=True)
            sb = jnp.sum(jnp.where(lo, 0.0, v), axis=1, keepdims=True)
            return sa, sb

        for g in range(GROUPS):
            gs = slice(512 * g, 512 * (g + 1))
            hs = slice(128 * g, 128 * (g + 1))
            xs_g = xbc_ref[:, gs]
            b_g = xbc_ref[:, DI + STATE * g:DI + STATE * (g + 1)].astype(BF16)
            c_g = xbc_ref[:, DI + 512 + STATE * g:DI + 512 + STATE * (g + 1)].astype(BF16)
            dt_g = _expand_pairs(dt, 4 * g, 4, lo)
            ecs_g = _expand_pairs(ecs, 4 * g, 4, lo)
            dec_g = _expand_pairs(dec, 4 * g, 4, lo)
            cd_g = _expand_pairs(cd, 4 * g, 4, lo)
            xdt = xs_g * dt_g
            xdt_b = xdt.astype(BF16)
            dy_g = dy_ref[:, gs]
            dy_b = dy_g.astype(BF16)
            s_mat = _nt(c_g, b_g)
            ds_mat = jnp.zeros((Q, Q), F32)
            dx_parts = []
            for jj in range(4):
                xp = xdt_b[:, 128 * jj:128 * (jj + 1)]
                dyp = dy_b[:, 128 * jj:128 * (jj + 1)]
                dxh = []
                for h, sel in ((8 * g + 2 * jj, lo), (8 * g + 2 * jj + 1, hi)):
                    l_mat = jnp.exp(jnp.where(causal, cs[:, h:h + 1] - csT[h:h + 1, :], NEG))
                    m_mat = s_mat * l_mat
                    dm = _nt(jnp.where(sel, dyp, jnp.zeros_like(dyp)), xp)
                    w_mat = dm * m_mat
                    dcs_c = put(dcs_c, h, jnp.sum(w_mat, axis=1, keepdims=True))
                    dcs_r = dcs_r + jnp.where(sub32 == h, jnp.sum(w_mat, axis=0, keepdims=True), 0.0)
                    ds_mat = ds_mat + dm * l_mat
                    dxh.append(_tn(m_mat.astype(BF16), dyp))
                dx_parts.append(jnp.where(lo, dxh[0], dxh[1]))
            hin_g = hin_ref[0, hs, :]
            hin_b = hin_g.astype(BF16)
            dh_g = dh_scr[hs, :]
            dh_b = dh_g.astype(BF16)
            y_off = _nn(c_g, hin_b) * ecs_g
            dz = (dy_g * ecs_g).astype(BF16)
            g_mat = _nn(b_g, dh_b)
            xdec = xdt * dec_g
            v1 = dy_g * y_off - xdec * g_mat
            v2 = jnp.sum(xdec * g_mat, axis=0, keepdims=True) + jnp.sum(dh_g * hin_g, axis=0, keepdims=True) * cd_g
            dxdt = jnp.concatenate(dx_parts, axis=1) + dec_g * g_mat
            v3 = dxdt * xs_g
            for jj in range(4):
                ps = slice(128 * jj, 128 * (jj + 1))
                ha = 8 * g + 2 * jj
                sa, sb = halves(v1[:, ps])
                dcs_c = put(put(dcs_c, ha, sa), ha + 1, sb)
                sa, sb = halves(v2[:, ps])
                dcs_l = put(put(dcs_l, ha, sa), ha + 1, sb)
                sa, sb = halves(v3[:, ps])
                ddt_x = put(put(ddt_x, ha, sa), ha + 1, sb)
            ds_b = ds_mat.astype(BF16)
            dxa_ref[:, gs] = dxdt * dt_g + dy_g * dsk_ref[:, gs]
            dxa_ref[:, DI + STATE * g:DI + STATE * (g + 1)] = _nt(xdec.astype(BF16), dh_b) + _tn(ds_b, c_g)
            dxa_ref[:, DI + 512 + STATE * g:DI + 512 + STATE * (g + 1)] = _nt(dz, hin_b) + _nn(ds_b, b_g)
            dh_scr[hs, :] = _tn(c_g, dz) + dh_g * cd_g
            dsk_sum_ref[0:1, gs] += jnp.sum(dy_g * xs_g, axis=0, keepdims=True)

        rows = lax.broadcasted_iota(jnp.int32, (Q, Q), 0)
        cols = lax.broadcasted_iota(jnp.int32, (Q, Q), 1)
        tri_t = (cols >= rows).astype(F32)
        last_row = lax.broadcasted_iota(jnp.int32, (Q, 1), 0) == Q - 1
        dcs = dcs_c + jnp.where(last_row, dcs_l, 0.0)
        da = _nn(tri_t, dcs, precision=HIGH) - _nt(tri_t, dcs_r, precision=HIGH)
        ddt_raw = (ddt_x + da * a_r) * _sigmoid(raw)
        small_ref[0:1, :] += jnp.sum(da * dt, axis=0, keepdims=True) * a_r
        small_ref[1:2, :] += jnp.sum(ddt_raw, axis=0, keepdims=True)
        dp_ref[...] = jnp.zeros_like(dp_ref)
        dp_ref[:, 0:HEADS] = ddt_raw.astype(BF16)

    rev = lambda c: nc - 1 - c
    small_r = pl.BlockSpec((1, HEADS), lambda c: (0, 0))
    small_c = pl.BlockSpec((HEADS, 1), lambda c: (0, 0))
    return _call(
        body, [dy, xbc_a, proj, dtT, hin, dtb_r, alog_r, dtb_c, alog_c, dsk_exp, dproj], name="ssd_bwd", grid=(nc,),
        in_specs=[pl.BlockSpec((Q, DI), lambda c: (rev(c), 0)),
                  pl.BlockSpec((Q, XBC), lambda c: (rev(c), 0)),
                  pl.BlockSpec((Q, 128), lambda c: (rev(c), OFF_DT // 128)),
                  pl.BlockSpec((HEADS, Q), lambda c: (0, rev(c))),
                  pl.BlockSpec((1, 512, 512), lambda c: (rev(c), 0, 0)),
                  small_r, small_r, small_c, small_c,
                  pl.BlockSpec((1, DI), lambda c: (0, 0)),
                  pl.BlockSpec(memory_space=pl.ANY)],
        out_specs=[pl.BlockSpec((Q, XBC), lambda c: (rev(c), 0)),
                   pl.BlockSpec((Q, 256), lambda c: (rev(c), OFF_DT // 256)),
                   pl.BlockSpec((8, DI), lambda c: (0, 0)),
                   pl.BlockSpec((8, HEADS), lambda c: (0, 0))],
        out_shape=[jax.ShapeDtypeStruct((T, XBC), F32), jax.ShapeDtypeStruct(dproj.shape, BF16),
                   jax.ShapeDtypeStruct((8, DI), F32), jax.ShapeDtypeStruct((8, HEADS), F32)],
        aliases={10: 1}, scratch_shapes=[pltpu.VMEM((512, 512), F32)], sem=("arbitrary",), carry=carry)


def _gate_norm(y, proj, w):
    T = y.shape[0]
    tr = _row_tile(T)

    def body(y_ref, z_ref, w_ref, o_ref):
        for g in range(GROUPS):
            gs = slice(512 * g, 512 * (g + 1))
            z = z_ref[:, gs]
            yg = y_ref[:, gs] * (z * _sigmoid(z))
            r = lax.rsqrt(jnp.mean(yg * yg, axis=-1, keepdims=True) + EPS)
            o_ref[:, gs] = (yg * r * w_ref[:, gs]).astype(BF16)

    tile = pl.BlockSpec((tr, DI), lambda i: (i, 0))
    return _call(body, [y, proj, w], name="gate_norm", grid=(T // tr,),
                 in_specs=[tile, tile, pl.BlockSpec((1, DI), lambda i: (0, 0))], out_specs=[tile],
                 out_shape=[jax.ShapeDtypeStruct((T, DI), BF16)], sem=("parallel",))[0]


def _gate_norm_bwd(dyn, y, proj, w, dproj):
    T = y.shape[0]
    tr = _row_tile(T)

    def body(d_ref, y_ref, z_ref, w_ref, dp_in, dy_ref, dz_ref, sums_ref):
        del dp_in

        @pl.when(pl.program_id(0) == 0)
        def _():
            sums_ref[...] = jnp.zeros_like(sums_ref)

        for g in range(GROUPS):
            gs = slice(512 * g, 512 * (g + 1))
            z, yv, d = z_ref[:, gs], y_ref[:, gs], d_ref[:, gs]
            s = _sigmoid(z)
            silu = z * s
            yg = yv * silu
            r = lax.rsqrt(jnp.mean(yg * yg, axis=-1, keepdims=True) + EPS)
            yn = yg * r
            sums_ref[0:1, gs] += jnp.sum(d * yn, axis=0, keepdims=True)
            dn = d * w_ref[:, gs]
            dyg = r * (dn - yn * jnp.mean(dn * yn, axis=-1, keepdims=True))
            dy_ref[:, gs] = dyg * silu
            dz_ref[:, gs] = (dyg * yv * (s * (1.0 + z * (1.0 - s)))).astype(BF16)

    tile = pl.BlockSpec((tr, DI), lambda i: (i, 0))
    return _call(
        body, [dyn, y, proj, w, dproj], name="gate_norm_bwd", grid=(T // tr,),
        in_specs=[tile, tile, tile, pl.BlockSpec((1, DI), lambda i: (0, 0)), pl.BlockSpec(memory_space=pl.ANY)],
        out_specs=[tile, tile, pl.BlockSpec((8, DI), lambda i: (0, 0))],
        out_shape=[jax.ShapeDtypeStruct((T, DI), F32), jax.ShapeDtypeStruct(dproj.shape, BF16),
                   jax.ShapeDtypeStruct((8, DI), F32)],
        aliases={4: 1}, sem=("arbitrary",))


def _pool_fwd(proj, pool_w_b, pool_scale):
    T = proj.shape[0]
    tr = _row_tile(T)
    nb = tr // 16

    def body(u_ref, h_ref, pw_ref, ps_ref, pooled_ref, pw_out_ref, yps_ref):
        i = pl.program_id(0)
        t = i * tr + lax.broadcasted_iota(jnp.int32, (tr, 1), 0)
        for g, win in enumerate(POOL_WINDOWS):
            gs = slice(GW * g, GW * (g + 1))
            u = u_ref[:, gs]
            s = jnp.concatenate([jnp.where(i > 0, h_ref[:, gs], 0.0), u], axis=0)
            sh = 1
            while sh < win:
                s = s + pltpu.roll(s, sh, 0)
                sh *= 2
            pooled = (s[16:] / jnp.minimum(t + 1, win).astype(F32) - u).astype(BF16)
            pooled_ref[:, gs] = pooled
            pwv = _nn(pooled, pw_ref[g])
            pw_out_ref[:, gs] = pwv
            yps_ref[:, gs] = (pwv * ps_ref[:, gs]).astype(BF16)

    tile = pl.BlockSpec((tr, D), lambda i: (i, 0))
    return _call(
        body, [proj, proj, pool_w_b, pool_scale], name="pool_fwd", grid=(T // tr,),
        in_specs=[pl.BlockSpec((tr, D), lambda i: (i, OFF_POOL // D)),
                  pl.BlockSpec((16, D), lambda i: (jnp.maximum(i * nb - 1, 0), OFF_POOL // D)),
                  pl.BlockSpec((4, GW, GW), lambda i: (0, 0, 0)),
                  pl.BlockSpec((1, D), lambda i: (0, 0))],
        out_specs=[tile, tile, tile],
        out_shape=[jax.ShapeDtypeStruct((T, D), BF16), jax.ShapeDtypeStruct((T, D), F32),
                   jax.ShapeDtypeStruct((T, D), BF16)], sem=("parallel",))


def _pool_bwd(dyp, pw_out, pooled, pool_w_b, pool_scale, dproj):
    T = dyp.shape[0]
    tr = _row_tile(T)
    nb, last = tr // 16, T // tr - 1

    def body(d_ref, h_ref, pwo_ref, pooled_ref, pw_ref, ps_ref, dp_in, du_ref, gpw_ref, sums_ref):
        del dp_in
        i = pl.program_id(0)

        @pl.when(i == 0)
        def _():
            gpw_ref[...] = jnp.zeros_like(gpw_ref)
            sums_ref[...] = jnp.zeros_like(sums_ref)

        n = tr + 16
        t = i * tr + lax.broadcasted_iota(jnp.int32, (n, 1), 0)
        sums_ref[0:1, :] += jnp.sum(d_ref[...] * pwo_ref[...], axis=0, keepdims=True)
        for g, win in enumerate(POOL_WINDOWS):
            gs = slice(GW * g, GW * (g + 1))
            d_ext = jnp.concatenate([d_ref[:, gs], jnp.where(i < last, h_ref[:, gs], 0.0)], axis=0)
            dpw = (d_ext * ps_ref[:, gs]).astype(BF16)
            dpooled = _nt(dpw, pw_ref[g])
            s = jnp.where(t < T, dpooled / jnp.minimum(t + 1, win).astype(F32), 0.0)
            sh = 1
            while sh < win:
                s = s + pltpu.roll(s, n - sh, 0)
                sh *= 2
            du_ref[:, gs] = (s[:tr] - dpooled[:tr]).astype(BF16)
            gpw_ref[g] += _tn(pooled_ref[:, gs], dpw[:tr])

    tile = pl.BlockSpec((tr, D), lambda i: (i, 0))
    return _call(
        body, [dyp, dyp, pw_out, pooled, pool_w_b, pool_scale, dproj], name="pool_bwd", grid=(T // tr,),
        in_specs=[tile, pl.BlockSpec((16, D), lambda i: (jnp.minimum((i + 1) * nb, T // 16 - 1), 0)), tile, tile,
                  pl.BlockSpec((4, GW, GW), lambda i: (0, 0, 0)), pl.BlockSpec((1, D), lambda i: (0, 0)),
                  pl.BlockSpec(memory_space=pl.ANY)],
        out_specs=[pl.BlockSpec((tr, D), lambda i: (i, OFF_POOL // D)),
                   pl.BlockSpec((4, GW, GW), lambda i: (0, 0, 0)), pl.BlockSpec((8, D), lambda i: (0, 0))],
        out_shape=[jax.ShapeDtypeStruct(dproj.shape, BF16), jax.ShapeDtypeStruct((4, GW, GW), F32),
                   jax.ShapeDtypeStruct((8, D), F32)],
        aliases={6: 0}, sem=("arbitrary",))


def _merge(proj, y_ssd, y_pool):
    T = proj.shape[0]
    tr = _row_tile(T)

    def body(g_ref, a_ref, b_ref, o_ref):
        o_ref[...] = (_sigmoid(g_ref[:, 0:D]) * a_ref[...] + _sigmoid(g_ref[:, D:2 * D]) * b_ref[...]).astype(BF16)

    tile = pl.BlockSpec((tr, D), lambda i: (i, 0))
    return _call(body, [proj, y_ssd, y_pool], name="merge", grid=(T // tr,),
                 in_specs=[pl.BlockSpec((tr, 2 * D), lambda i: (i, OFF_GATE // (2 * D))), tile, tile], out_specs=[tile],
                 out_shape=[jax.ShapeDtypeStruct((T, D), BF16)], sem=("parallel",))[0]


def _merge_bwd(dmerged, proj, y_ssd, y_pool):
    T = proj.shape[0]
    tr = _row_tile(T)

    def body(d_ref, g_ref, a_ref, b_ref, da_ref, db_ref, dg_ref):
        d = d_ref[...]
        ga, gb = _sigmoid(g_ref[:, 0:D]), _sigmoid(g_ref[:, D:2 * D])
        da_ref[...] = (d * ga).astype(BF16)
        db_ref[...] = (d * gb).astype(BF16)
        dg_ref[:, 0:D] = (d * a_ref[...] * ga * (1.0 - ga)).astype(BF16)
        dg_ref[:, D:2 * D] = (d * b_ref[...] * gb * (1.0 - gb)).astype(BF16)

    tile = pl.BlockSpec((tr, D), lambda i: (i, 0))
    gates = pl.BlockSpec((tr, 2 * D), lambda i: (i, OFF_GATE // (2 * D)))
    return _call(body, [dmerged, proj, y_ssd, y_pool], name="merge_bwd", grid=(T // tr,),
                 in_specs=[tile, gates, tile, tile], out_specs=[tile, tile, gates],
                 out_shape=[jax.ShapeDtypeStruct((T, D), BF16), jax.ShapeDtypeStruct((T, D), BF16),
                            jax.ShapeDtypeStruct((T, NP), BF16)], sem=("parallel",))


def _adamw(w, g, m, v, name, carry=None):
    R, C = w.shape
    tr = R if R <= 128 else 128
    assert R % tr == 0

    def body(w_ref, g_ref, m_ref, v_ref, d_ref, mo_ref, vo_ref):
        gv = g_ref[...]
        mn = ADAM_B1 * m_ref[...] + (1.0 - ADAM_B1) * gv
        vn = ADAM_B2 * v_ref[...] + (1.0 - ADAM_B2) * (gv * gv)
        m_hat = mn / (1.0 - ADAM_B1 ** ADAM_STEP)
        v_hat = vn / (1.0 - ADAM_B2 ** ADAM_STEP)
        d_ref[...] = -ADAM_LR * (m_hat / (jnp.sqrt(v_hat) + ADAM_EPS) + ADAM_WD * w_ref[...])
        mo_ref[...] = mn
        vo_ref[...] = vn

    tile = pl.BlockSpec((tr, C), lambda i: (i, 0))
    sds = jax.ShapeDtypeStruct((R, C), F32)
    return _call(body, [w, g, m, v], name=name, grid=(R // tr,), in_specs=[tile] * 4, out_specs=[tile] * 3,
                 out_shape=[sds] * 3, sem=("parallel",), carry=carry)


def _me():
    return lax.axis_index("x"), lax.axis_index("y"), lax.axis_index("c")


def _xor_peer(x, y, c, p):
    return (x ^ ((p >> 2) & 1), y ^ ((p >> 1) & 1), c ^ (p & 1))


def _ada_fwd(c_row, w_ada, b_ada_mine):
    n_cols = w_ada.shape[1]

    def body(c_ref, w_ref, b_ref, mod_ref, c8_ref, csend, mpart, modbuf, send_sems, recv_sems):
        x, y, c = _me()
        me = 4 * x + 2 * y + c
        chip = 2 * x + y
        csend[...] = jnp.broadcast_to(c_ref[...], csend.shape)
        c8_ref[me] = csend[...]

        def c_copy(p):
            return pltpu.make_async_remote_copy(
                src_ref=csend, dst_ref=c8_ref.at[me], send_sem=send_sems.at[p - 1], recv_sem=recv_sems.at[p - 1],
                device_id=_xor_peer(x, y, c, p), device_id_type=MESH)

        for p in range(1, 8):
            c_copy(p).start()
        for p in range(1, 8):
            c_copy(p).wait_recv()
        cs = jnp.concatenate([c8_ref[d][0:1, :] for d in range(8)], axis=0)
        mpart[...] = _nn(cs * _sigmoid(cs), w_ref[...], precision=HIGH) + b_ref[...]
        modbuf[chip] = mpart[...]

        def m_copy(m):
            return pltpu.make_async_remote_copy(
                src_ref=mpart, dst_ref=modbuf.at[chip], send_sem=send_sems.at[6 + m], recv_sem=recv_sems.at[6 + m],
                device_id=_xor_peer(x, y, c, 2 * m), device_id_type=MESH)

        for m in range(1, 4):
            m_copy(m).start()
        for m in range(1, 4):
            m_copy(m).wait_recv()
        mine = lax.broadcasted_iota(jnp.int32, (8, 1), 0) == me
        for k in range(N_CHIPS):
            mod_ref[:, n_cols * k:n_cols * (k + 1)] = jnp.sum(jnp.where(mine, modbuf[k], 0.0), axis=0, keepdims=True)
        for p in range(1, 8):
            c_copy(p).wait_send()
        for m in range(1, 4):
            m_copy(m).wait_send()

    vmem = pl.BlockSpec(memory_space=pltpu.VMEM)
    return _call(
        body, [c_row, w_ada, b_ada_mine], name="ada_fwd", in_specs=[vmem, vmem, vmem], out_specs=[vmem, vmem],
        out_shape=[jax.ShapeDtypeStruct((1, N_CHIPS * n_cols), F32), jax.ShapeDtypeStruct((8, 8, D), F32)],
        scratch_shapes=[pltpu.VMEM((8, D), F32), pltpu.VMEM((8, n_cols), F32), pltpu.VMEM((N_CHIPS, 8, n_cols), F32),
                        pltpu.SemaphoreType.DMA((10,)), pltpu.SemaphoreType.DMA((10,))])


def _gather_small(vec):
    rows = vec.shape[0]

    def body(v_ref, all_ref, tot_ref, dsk_ref, send_sems, recv_sems):
        x, y, c = _me()
        me = 4 * x + 2 * y + c
        all_ref[me] = v_ref[...]

        def copy(p):
            return pltpu.make_async_remote_copy(
                src_ref=v_ref, dst_ref=all_ref.at[me], send_sem=send_sems.at[p - 1], recv_sem=recv_sems.at[p - 1],
                device_id=_xor_peer(x, y, c, p), device_id_type=MESH)

        for p in range(1, 8):
            copy(p).start()
        for p in range(1, 8):
            copy(p).wait_recv()
        tot = all_ref[0]
        for d in range(1, 8):
            tot = tot + all_ref[d]
        tot_ref[...] = tot
        seg = tot[SMALL_OFF["d_skip"] // 128:SMALL_OFF["d_skip"] // 128 + 16, :]
        lane = lax.broadcasted_iota(jnp.int32, (1, 128), 1)
        sa = jnp.sum(jnp.where(lane < HEAD_DIM, seg, 0.0), axis=1, keepdims=True)
        sb = jnp.sum(jnp.where(lane < HEAD_DIM, 0.0, seg), axis=1, keepdims=True)
        dsk_ref[...] = jnp.where(lane == 0, sa, jnp.where(lane == 1, sb, 0.0))
        for p in range(1, 8):
            copy(p).wait_send()

    vmem = pl.BlockSpec(memory_space=pltpu.VMEM)
    return _call(
        body, [vec], name="gather_small", in_specs=[vmem], out_specs=[vmem, vmem, vmem],
        out_shape=[jax.ShapeDtypeStruct((8, rows, 128), F32), jax.ShapeDtypeStruct((rows, 128), F32),
                   jax.ShapeDtypeStruct((16, 128), F32)],
        scratch_shapes=[pltpu.SemaphoreType.DMA((7,)), pltpu.SemaphoreType.DMA((7,))])


def _gather_carry(shards):
    n = len(shards)

    def copies(ins, outs, sems):
        x, y, c = _me()
        chip = 2 * x + y

        def half(w, which):
            h = shards[w].shape[0] // 2
            return pl.ds(which * h, h)

        local = [pltpu.make_async_copy(ins[w], outs[w].at[chip], sems.local(w)) for w in range(n)]

        def first(w, m):
            return pltpu.make_async_remote_copy(
                src_ref=ins[w].at[half(w, c)], dst_ref=outs[w].at[chip, half(w, c)],
                send_sem=sems.send(6 * w + m - 1), recv_sem=sems.recv(6 * w + m - 1),
                device_id=_xor_peer(x, y, c, 2 * m), device_id_type=MESH)

        def landed(w, m):
            return pltpu.make_async_remote_copy(
                src_ref=ins[w].at[half(w, c)], dst_ref=outs[w].at[chip ^ m, half(w, c)],
                send_sem=sems.send(6 * w + m - 1), recv_sem=sems.recv(6 * w + m - 1),
                device_id=_xor_peer(x, y, c, 2 * m), device_id_type=MESH)

        def passed(w, m, which):
            part = outs[w].at[chip ^ m, half(w, which)]
            return pltpu.make_async_remote_copy(
                src_ref=part, dst_ref=part, send_sem=sems.send(6 * w + 2 + m), recv_sem=sems.recv(6 * w + 2 + m),
                device_id=(x, y, 1 - c), device_id_type=MESH)

        return c, local, first, landed, passed

    pairs = [(w, m) for w in range(n) for m in range(1, 4)]

    def start(ins, outs, sems):
        _, local, first, _, _ = copies(ins, outs, sems)
        for cp in local:
            cp.start()
        for w, m in pairs:
            first(w, m).start()

    def finish(ins, outs, sems):
        c, local, first, landed, passed = copies(ins, outs, sems)
        for w, m in pairs:
            landed(w, m).wait_recv()
            passed(w, m, c).start()
        for w, m in pairs:
            passed(w, m, 1 - c).wait_recv()
        for w, m in pairs:
            first(w, m).wait_send()
            passed(w, m, c).wait_send()
        for cp in local:
            cp.wait()

    return _Carry(shards, [jax.ShapeDtypeStruct((N_CHIPS,) + s.shape, s.dtype) for s in shards], 6 * n, start, finish)


def _pair_exchange_carry(grads):
    n = len(grads)

    def copy(ins, outs, sems, w):
        x, y, c = _me()
        h = grads[w].shape[1] // 2
        return pltpu.make_async_remote_copy(
            src_ref=ins[w].at[:, pl.ds((1 - c) * h, h)], dst_ref=outs[w],
            send_sem=sems.send(w), recv_sem=sems.recv(w), device_id=(x, y, 1 - c), device_id_type=MESH)

    def start(ins, outs, sems):
        for w in range(n):
            copy(ins, outs, sems, w).start()

    def finish(ins, outs, sems):
        for w in range(n):
            copy(ins, outs, sems, w).wait()

    return _Carry(grads, [jax.ShapeDtypeStruct((N_CHIPS, g.shape[1] // 2, g.shape[2]), g.dtype) for g in grads], n,
                  start, finish)


def _chip_exchange_carry(partials):
    n = len(partials)

    def copies(ins, outs, sems):
        x, y, c = _me()
        chip = 2 * x + y
        local = [pltpu.make_async_copy(ins[w].at[chip], outs[w].at[chip], sems.local(w)) for w in range(n)]

        def copy(w, m, landed):
            return pltpu.make_async_remote_copy(
                src_ref=ins[w].at[chip ^ m], dst_ref=outs[w].at[(chip ^ m) if landed else chip],
                send_sem=sems.send(3 * w + m - 1), recv_sem=sems.recv(3 * w + m - 1),
                device_id=_xor_peer(x, y, c, 2 * m), device_id_type=MESH)

        return local, copy

    pairs = [(w, m) for w in range(n) for m in range(1, 4)]

    def start(ins, outs, sems):
        local, copy = copies(ins, outs, sems)
        for cp in local:
            cp.start()
        for w, m in pairs:
            copy(w, m, False).start()

    def finish(ins, outs, sems):
        local, copy = copies(ins, outs, sems)
        for w, m in pairs:
            copy(w, m, True).wait_recv()
        for w, m in pairs:
            copy(w, m, False).wait_send()
        for cp in local:
            cp.wait()

    return _Carry(partials, [jax.ShapeDtypeStruct(p.shape, p.dtype) for p in partials], 3 * n, start, finish)


def _pair_share_carry(halves):
    n = len(halves)

    def copies(ins, outs, sems):
        x, y, c = _me()

        def rows(w, which):
            h = halves[w].shape[0]
            return pl.ds(which * h, h)

        local = [pltpu.make_async_copy(ins[w], outs[w].at[rows(w, c)], sems.local(w)) for w in range(n)]

        def copy(w, which):
            return pltpu.make_async_remote_copy(
                src_ref=ins[w], dst_ref=outs[w].at[rows(w, which)],
                send_sem=sems.send(w), recv_sem=sems.recv(w), device_id=(x, y, 1 - c), device_id_type=MESH)

        return c, local, copy

    def start(ins, outs, sems):
        c, local, copy = copies(ins, outs, sems)
        for cp in local:
            cp.start()
        for w in range(n):
            copy(w, c).start()

    def finish(ins, outs, sems):
        c, local, copy = copies(ins, outs, sems)
        for w in range(n):
            copy(w, 1 - c).wait_recv()
        for w in range(n):
            copy(w, c).wait_send()
        for cp in local:
            cp.wait()

    return _Carry(halves, [jax.ShapeDtypeStruct((2 * h.shape[0], h.shape[1]), h.dtype) for h in halves], n, start, finish)


def _pair_sum(g, part, idx, name):
    _, h, C = part.shape
    tr = min(128, h)
    nb = h // tr

    def body(idx_ref, g_ref, p_ref, o16_ref, own_ref):
        v = g_ref[...] + p_ref[...]
        o16_ref[...] = v.astype(BF16)

        @pl.when(pl.program_id(1) == idx_ref[1])
        def _():
            own_ref[...] = v

    return pl.pallas_call(
        body, name=name,
        grid_spec=pltpu.PrefetchScalarGridSpec(
            num_scalar_prefetch=1, grid=(nb, N_CHIPS),
            in_specs=[pl.BlockSpec((None, tr, C), lambda i, s, idx_ref: (s, idx_ref[0] * nb + i, 0)),
                      pl.BlockSpec((None, tr, C), lambda i, s, idx_ref: (s, i, 0))],
            out_specs=[pl.BlockSpec((None, tr, C), lambda i, s, idx_ref: (s, i, 0)),
                       pl.BlockSpec((tr, C), lambda i, s, idx_ref: (i, 0))]),
        out_shape=[jax.ShapeDtypeStruct(part.shape, BF16), jax.ShapeDtypeStruct((h, C), F32)],
        compiler_params=pltpu.CompilerParams(dimension_semantics=("arbitrary", "arbitrary"), vmem_limit_bytes=VMEM_LIMIT),
    )(idx, g, part)


def _chip_sum(own, slots, idx, name):
    h, C = own.shape
    tr = min(128, h)

    def body(idx_ref, own_ref, s_ref, o_ref):
        chip = idx_ref[1]
        tot = None
        for s in range(N_CHIPS):
            term = jnp.where(chip == s, own_ref[...], s_ref[s].astype(F32))
            tot = term if tot is None else tot + term
        o_ref[...] = tot

    return pl.pallas_call(
        body, name=name,
        grid_spec=pltpu.PrefetchScalarGridSpec(
            num_scalar_prefetch=1, grid=(h // tr,),
            in_specs=[pl.BlockSpec((tr, C), lambda i, idx_ref: (i, 0)),
                      pl.BlockSpec((N_CHIPS, tr, C), lambda i, idx_ref: (0, i, 0))],
            out_specs=pl.BlockSpec((tr, C), lambda i, idx_ref: (i, 0))),
        out_shape=jax.ShapeDtypeStruct((h, C), F32),
        compiler_params=pltpu.CompilerParams(dimension_semantics=("parallel",), vmem_limit_bytes=VMEM_LIMIT),
    )(idx, own, slots)


class _Reducer:
    def __init__(self, idx):
        self.idx, self.chips, self.p16, self.own, self.mine, self.final = idx, {}, {}, {}, {}, {}

    def add(self, name, whole):
        self.chips[name] = _chips_from_whole(name, whole)

    def pair(self, names):
        return _pair_exchange_carry([self.chips[n] for n in names])

    def take_pair(self, names, outs):
        for n, part in zip(names, outs):
            self.p16[n], self.own[n] = _pair_sum(self.chips.pop(n), part, self.idx, "pair_sum_" + n)

    def chip(self, names):
        return _chip_exchange_carry([self.p16[n] for n in names])

    def take_chip(self, names, outs):
        for n, slots in zip(names, outs):
            del self.p16[n]
            self.mine[n] = _chip_sum(self.own.pop(n), slots, self.idx, "chip_sum_" + n)

    def share(self, names):
        return _pair_share_carry([self.mine[n] for n in names])

    def take_share(self, names, outs):
        for n, s in zip(names, outs):
            del self.mine[n]
            self.final[n] = s


def _w_ada_grad(c8, dmod_cols):
    n_cols = dmod_cols.shape[1]
    tn = 512

    def body(c_ref, d_ref, o_ref):
        cv = c_ref[...]
        o_ref[...] = _tn(cv * _sigmoid(cv), d_ref[...], precision=HIGH)

    return _call(body, [c8, dmod_cols], name="w_ada_grad", grid=(n_cols // tn,),
                 in_specs=[pl.BlockSpec((8, D), lambda j: (0, 0)), pl.BlockSpec((8, tn), lambda j: (0, j))],
                 out_specs=[pl.BlockSpec((D, tn), lambda j: (0, j))],
                 out_shape=[jax.ShapeDtypeStruct((D, n_cols), F32)], sem=("parallel",))[0]


_SMALL_SEGS = (("dmod", 6144), ("norm_mix_w", 1024), ("conv_b", 3072), ("ssd_norm_w", 2048), ("pool_scale", 1024),
               ("norm_mlp_w", 1024), ("norm_final_w", 1024), ("conv_w", 4 * XBC), ("d_skip", 2048), ("a_log", 128),
               ("dt_bias", 128), ("loss", 128))
SMALL_OFF = {}
_o = 0
for _n, _s in _SMALL_SEGS:
    SMALL_OFF[_n] = _o
    _o += _s
SMALL_LEN = -(-_o // 1024) * 1024

_FIRST = ("w_in", "conv_w")
_LATER = ("w_branch_ssd", "pool_w", "w_branch_pool", "w_out", "w_up", "w_down")
_SMALL_REPLICATED = ("b_ada", "norm_mix_w", "conv_b", "dt_bias", "a_log", "d_skip", "ssd_norm_w", "pool_scale",
                     "norm_mlp_w", "norm_final_w")
_WEIGHTS = ("w_ada", "b_ada", "norm_mix_w", "w_in", "conv_w", "conv_b", "dt_bias", "a_log", "d_skip", "ssd_norm_w",
            "w_branch_ssd", "pool_w", "pool_scale", "w_branch_pool", "w_out", "norm_mlp_w", "w_up", "w_down",
            "norm_final_w")


def _shard_2d(name, a):
    if name == "conv_w":
        return a.reshape(16, -1)
    return (a.reshape(GW, GW) if name == "pool_w" else a.reshape(a.shape[-2], a.shape[-1])).astype(BF16)


def _whole_from_chips(name, g):
    if name == "w_in":
        return _perm_cols(jnp.transpose(g, (1, 0, 2)).reshape(D, IN_COLS))
    if name == "w_up":
        return jnp.transpose(g, (1, 0, 2)).reshape(D, DFF)
    if name == "pool_w":
        return jnp.transpose(g.reshape(N_CHIPS, 4, GW // N_CHIPS, GW), (1, 0, 2, 3)).reshape(4, GW, GW)
    if name == "conv_w":
        return jnp.transpose(g.reshape(N_CHIPS, 4, XBC // N_CHIPS), (1, 0, 2)).reshape(4, XBC)
    return g.reshape(N_CHIPS * g.shape[1], g.shape[2])


def _chips_from_whole(name, g):
    if name.startswith("w_in"):
        return jnp.transpose(_unperm_cols(g).reshape(g.shape[0], N_CHIPS, IN_COLS // N_CHIPS), (1, 0, 2))
    if name == "w_up":
        return jnp.transpose(g.reshape(D, N_CHIPS, DFF // N_CHIPS), (1, 0, 2))
    if name == "pool_w":
        return jnp.transpose(g.reshape(4, N_CHIPS, GW // N_CHIPS, GW), (1, 0, 2, 3)).reshape(N_CHIPS, GW, GW)
    return g.reshape(N_CHIPS, g.shape[0] // N_CHIPS, g.shape[1])


def kernel(x, c, w_ada, b_ada, norm_mix_w, w_in, conv_w, conv_b, dt_bias, a_log, d_skip, ssd_norm_w, w_branch_ssd, pool_w, pool_scale, w_branch_pool, w_out, norm_mlp_w, w_up, w_down, norm_final_w, loss_target, m_w_ada, m_b_ada, m_norm_mix_w, m_w_in, m_conv_w, m_conv_b, m_dt_bias, m_a_log, m_d_skip, m_ssd_norm_w, m_w_branch_ssd, m_pool_w, m_pool_scale, m_w_branch_pool, m_w_out, m_norm_mlp_w, m_w_up, m_w_down, m_norm_final_w, v_w_ada, v_b_ada, v_norm_mix_w, v_w_in, v_conv_w, v_conv_b, v_dt_bias, v_a_log, v_d_skip, v_ssd_norm_w, v_w_branch_ssd, v_pool_w, v_pool_scale, v_w_branch_pool, v_w_out, v_norm_mlp_w, v_w_up, v_w_down, v_norm_final_w):
    args = locals()
    w = {n: args[n] for n in _WEIGHTS}
    m = {n: args["m_" + n] for n in _WEIGHTS}
    v = {n: args["v_" + n] for n in _WEIGHTS}
    xi, yi, ci = _me()
    chip = 2 * xi + yi
    idx = jnp.stack([ci, chip]).astype(jnp.int32)
    ada_cols = w_ada.shape[-1]
    xs, target = x[0], loss_target[0]
    T = xs.shape[0]
    TM = min(1024, T)
    two_d = lambda n, a: a.reshape(GW, GW) if n == "pool_w" else a.reshape(-1, a.shape[-1])
    delta, new_m, new_v, g = {}, {}, {}, {}

    def adamw(n, carry=None):
        res = _adamw(two_d(n, w[n]), two_d(n, g[n]), two_d(n, m[n]), two_d(n, v[n]), "adamw_" + n, carry=carry)
        (delta[n], new_m[n], new_v[n]), extra = res if carry is not None else (res, None)
        return extra

    b_mine = lax.dynamic_slice(b_ada, (0, chip * ada_cols), (1, ada_cols))
    mod, c8 = _ada_fwd(c, w_ada[0], b_mine)
    c8 = c8[:, 0, :]
    shift_m, scale_m, gate_m, shift_f, scale_f, gate_f = [mod[:, D * i:D * (i + 1)] for i in range(6)]
    first = _run_carry(_gather_carry([_shard_2d(n, w[n]) for n in _FIRST]), "gather_first")
    p = {n: _whole_from_chips(n, a) for n, a in zip(_FIRST, first)}
    nf_w = norm_final_w.reshape(1, D)

    h1 = _norm_mod(xs, norm_mix_w, scale_m, shift_m, "norm_mod_mix")
    (proj,), later = _matmul(h1, p["w_in"], mode="nn", tm=TM, tn=768, tk=D, out_dtypes=[F32], name="mm_proj",
                             carry=_gather_carry([_shard_2d(n, w[n]) for n in _LATER]))
    p.update({n: _whole_from_chips(n, a) for n, a in zip(_LATER, later)})
    xbc_a = _conv_fwd(proj, p["conv_w"], conv_b)
    dtT = proj[:, OFF_DT:OFF_DT + HEADS].T
    dtb_c, alog_c = dt_bias.reshape(HEADS, 1), a_log.reshape(HEADS, 1)
    dsk_exp = jnp.repeat(d_skip, HEAD_DIM, axis=1)
    y, hin = _ssd_fwd(xbc_a, proj, dtT, dt_bias, a_log, dtb_c, alog_c, dsk_exp)
    yn = _gate_norm(y, proj, ssd_norm_w)
    (y_ssd,) = _matmul(yn, p["w_branch_ssd"], mode="nn", tm=TM, tn=D, tk=DI, out_dtypes=[F32], name="mm_branch_ssd")
    pooled, pw_out, yps = _pool_fwd(proj, p["pool_w"], pool_scale)
    (y_pool,) = _matmul(yps, p["w_branch_pool"], mode="nn", tm=TM, tn=D, tk=D, out_dtypes=[F32], name="mm_branch_pool")
    merged = _merge(proj, y_ssd, y_pool)
    resid = lambda acc, r, gt: (r + gt * acc, acc)
    x2, mix = _matmul(merged, p["w_out"], mode="nn", tm=512, tn=D, tk=D, out_dtypes=[F32, F32], name="mm_out",
                      epi=resid, tile_extras=(xs,), row_extras=(gate_m,))
    h2 = _norm_mod(x2, norm_mlp_w, scale_f, shift_f, "norm_mod_mlp")
    relu2 = lambda acc: (acc, jnp.square(jnp.maximum(acc, 0.0)))
    up, act = _matmul(h2, p["w_up"], mode="nn", tm=TM, tn=D, tk=D, out_dtypes=[F32, BF16], name="mm_up", epi=relu2)
    x3, down = _matmul(act, p["w_down"], mode="nn", tm=512, tn=D, tk=1024, out_dtypes=[F32, F32], name="mm_down",
                       epi=resid, tile_extras=(x2,), row_extras=(gate_f,))

    red = _Reducer(idx)
    dx3, d_down, sums_f = _final_loss_bwd(x3, target, nf_w, down, gate_f)
    drelu2 = lambda acc, u: (acc * (2.0 * jnp.maximum(u, 0.0)),)
    (dup,) = _matmul(d_down, p["w_down"], mode="nt", tm=TM, tn=D, tk=D, out_dtypes=[BF16], name="mm_dact",
                     epi=drelu2, tile_extras=(up,))
    red.add("w_down", _matmul(act, d_down, mode="tn", tm=1024, tn=D, tk=512, out_dtypes=[F32], name="mm_g_down")[0])
    (dh2,), got = _matmul(dup, p["w_up"], mode="nt", tm=TM, tn=D, tk=1024, out_dtypes=[F32], name="mm_dh2",
                          carry=red.pair(["w_down"]))
    red.take_pair(["w_down"], got)
    red.add("w_up", _matmul(h2, dup, mode="tn", tm=1024, tn=1024, tk=512, out_dtypes=[F32], name="mm_g_up")[0])
    dx2, sums_2, dmix = _norm_mod_bwd(x2, dh2, dx3, norm_mlp_w, scale_f, "norm_mod_mlp_bwd", branch=mix, gate=gate_m)
    (dmerged,), got = _matmul(dmix, p["w_out"], mode="nt", tm=TM, tn=D, tk=D, out_dtypes=[F32], name="mm_dmerged",
                              carry=red.pair(["w_up"]))
    red.take_pair(["w_up"], got)
    red.add("w_out", _matmul(merged, dmix, mode="tn", tm=1024, tn=D, tk=512, out_dtypes=[F32], name="mm_g_out")[0])
    dy_ssd, dy_pool, dproj = _merge_bwd(dmerged, proj, y_ssd, y_pool)
    (dyp,), got = _matmul(dy_pool, p["w_branch_pool"], mode="nt", tm=TM, tn=D, tk=D, out_dtypes=[F32], name="mm_dyp",
                          carry=red.pair(["w_out"]))
    red.take_pair(["w_out"], got)
    red.add("w_branch_pool", _matmul(yps, dy_pool, mode="tn", tm=1024, tn=D, tk=512, out_dtypes=[F32], name="mm_g_bpool")[0])
    dproj, g_pool_w, sums_pool = _pool_bwd(dyp, pw_out, pooled, p["pool_w"], pool_scale, dproj)
    red.add("pool_w", g_pool_w)
    red.add("w_branch_ssd", _matmul(yn, dy_ssd, mode="tn", tm=1024, tn=D, tk=512, out_dtypes=[F32], name="mm_g_bssd")[0])
    mixers = ["w_branch_pool", "pool_w", "w_branch_ssd"]
    (dyn,), got = _matmul(dy_ssd, p["w_branch_ssd"], mode="nt", tm=TM, tn=1024, tk=D, out_dtypes=[F32], name="mm_dyn",
                          carry=red.pair(mixers))
    red.take_pair(mixers, got)
    dy, dproj, sums_gn = _gate_norm_bwd(dyn, y, proj, ssd_norm_w, dproj)
    six = ["w_down", "w_up", "w_out"] + mixers
    (dxa, dproj, dsk_sum, ssd_small), got = _ssd_bwd(dy, xbc_a, proj, dtT, hin, dt_bias, a_log, dtb_c, alog_c, dsk_exp,
                                                     dproj, carry=red.chip(six))
    red.take_chip(six, got)
    dxc, sums_conv = _conv_bwd_a(dxa, proj, p["conv_w"], conv_b)
    dproj = _conv_bwd_b(dxc, p["conv_w"], dproj)
    half = D // 2
    (g_in_a,), got = _matmul(h1[:, :half], dproj, mode="tn", tm=512, tn=768, tk=512, out_dtypes=[F32], name="mm_g_in_a",
                             carry=red.share(six))
    red.take_share(six, got)
    red.add("w_in_a", g_in_a)
    (g_in_b,), got = _matmul(h1[:, half:], dproj, mode="tn", tm=512, tn=768, tk=512, out_dtypes=[F32], name="mm_g_in_b",
                             carry=red.pair(["w_in_a"]))
    red.take_pair(["w_in_a"], got)
    red.add("w_in_b", g_in_b)
    (dh1,), got = _matmul(dproj, p["w_in"], mode="nt", tm=TM, tn=D, tk=768, out_dtypes=[F32], name="mm_dh1",
                          carry=_join(red.chip(["w_in_a"]), red.pair(["w_in_b"])))
    red.take_chip(["w_in_a"], got[:1])
    red.take_pair(["w_in_b"], got[1:])
    (grad_x, sums_1), got = _norm_mod_bwd(xs, dh1, dx2, norm_mix_w, scale_m, "norm_mod_mix_bwd",
                                          carry=_join(red.chip(["w_in_b"]), red.share(["w_in_a"])))
    red.take_chip(["w_in_b"], got[:1])
    red.take_share(["w_in_a"], got[1:])

    dmod = jnp.concatenate([sums_1[0:1], sums_1[1:2], sums_2[3:4], sums_2[0:1], sums_2[1:2], sums_f[1:2]], axis=1)
    pad96 = jnp.zeros((1, 96), F32)
    small = {"dmod": dmod, "norm_mix_w": sums_1[2:3], "conv_b": sums_conv[4:5], "ssd_norm_w": sums_gn[0:1],
             "pool_scale": sums_pool[0:1], "norm_mlp_w": sums_2[2:3], "norm_final_w": sums_f[0:1],
             "conv_w": sums_conv[0:4].reshape(1, 4 * XBC), "d_skip": dsk_sum[0:1],
             "a_log": jnp.concatenate([ssd_small[0:1], pad96], axis=1),
             "dt_bias": jnp.concatenate([ssd_small[1:2], pad96], axis=1), "loss": sums_f[3:4, 0:128]}
    vec = jnp.concatenate([small[n] for n, _ in _SMALL_SEGS], axis=1)
    vec = jnp.pad(vec, ((0, 0), (0, SMALL_LEN - vec.shape[1]))).reshape(SMALL_LEN // 128, 128)
    every, total, dsk = _gather_small(vec)
    total = total.reshape(1, SMALL_LEN)
    seg = lambda n, size: total[:, SMALL_OFF[n]:SMALL_OFF[n] + size]
    g.update({"b_ada": seg("dmod", 6 * D), "norm_mix_w": seg("norm_mix_w", D), "conv_b": seg("conv_b", XBC),
              "dt_bias": seg("dt_bias", HEADS), "a_log": seg("a_log", HEADS), "d_skip": dsk[:, 0:2].reshape(1, HEADS),
              "ssd_norm_w": seg("ssd_norm_w", DI), "pool_scale": seg("pool_scale", D),
              "norm_mlp_w": seg("norm_mlp_w", D), "norm_final_w": seg("norm_final_w", D)})
    loss = total[0, SMALL_OFF["loss"]]
    conv_cols = conv_w.shape[-1]
    g["conv_w"] = lax.dynamic_slice(seg("conv_w", 4 * XBC).reshape(4, XBC), (0, chip * conv_cols), (4, conv_cols))
    dmod8 = every.reshape(8, SMALL_LEN)[:, SMALL_OFF["dmod"]:SMALL_OFF["dmod"] + 6 * D]
    g["w_ada"] = _w_ada_grad(c8, lax.dynamic_slice(dmod8, (0, chip * ada_cols), (8, ada_cols)))

    got = adamw("w_ada", carry=red.share(["w_in_b"]))
    red.take_share(["w_in_b"], got)
    for n in six:
        g[n] = red.final[n]
    g["w_in"] = jnp.concatenate([red.final["w_in_a"], red.final["w_in_b"]], axis=0)
    for n in ["conv_w", "w_in"] + six:
        adamw(n)
    sizes = [w[n].size for n in _SMALL_REPLICATED]
    n_small = -(-sum(sizes) // 1024) * 1024
    pack = lambda d: jnp.pad(jnp.concatenate([d[n].reshape(1, -1) for n in _SMALL_REPLICATED], axis=1),
                             ((0, 0), (0, n_small - sum(sizes)))).reshape(n_small // 128, 128)
    d_, m_, v_ = _adamw(pack(w), pack(g), pack(m), pack(v), "adamw_small")
    off = 0
    for n, s in zip(_SMALL_REPLICATED, sizes):
        for dst, src in ((delta, d_), (new_m, m_), (new_v, v_)):
            dst[n] = src.reshape(1, n_small)[:, off:off + s]
        off += s

    out = [loss, grad_x.reshape(x.shape)]
    for d in (g, delta, new_m, new_v):
        out += [d[n].reshape(w[n].shape) for n in _WEIGHTS]
    return tuple(out)
```

```python
import functools
import operator

import jax
import jax.numpy as jnp
from jax import lax
from jax.experimental import pallas as pl
from jax.experimental.pallas import tpu as pltpu

F32, BF16 = jnp.float32, jnp.bfloat16
HIGH = lax.Precision.HIGHEST
MESH = pl.DeviceIdType.MESH

D = 1024
DI = 2048
HEADS, HEAD_DIM = 32, 64
GROUPS, STATE = 4, 128
Q = 128
XBC = DI + 2 * GROUPS * STATE
POOL_WINDOWS = (2, 4, 8, 16)
GW = 256
DFF = 4096
EPS = 1e-5
IN_COLS = 8224
OFF_Z, OFF_XBC, OFF_POOL, OFF_GATE, OFF_DT, NP = 0, 2048, 5120, 6144, 8192, 8448
N_CHIPS = 4
ADAM_LR, ADAM_B1, ADAM_B2, ADAM_EPS, ADAM_WD, ADAM_STEP = 0.001, 0.9, 0.999, 1e-08, 0.01, 10
VMEM_LIMIT = 56 * 2 ** 20
NEG = -1e30


def _sigmoid(v):
    return 0.5 * jnp.tanh(0.5 * v) + 0.5


def _softplus(v):
    return jnp.maximum(v, 0.0) + jnp.log1p(jnp.exp(-jnp.abs(v)))


def _dot(a, b, dims, **kw):
    return lax.dot_general(a, b, (dims, ((), ())), preferred_element_type=F32, **kw)


def _nn(a, b, **kw):
    return _dot(a, b, ((1,), (0,)), **kw)


def _nt(a, b, **kw):
    return _dot(a, b, ((1,), (1,)), **kw)


def _tn(a, b, **kw):
    return _dot(a, b, ((0,), (0,)), **kw)


def _perm_cols(w):
    pad = jnp.zeros(w.shape[:-1] + (NP - IN_COLS,), w.dtype)
    return jnp.concatenate([w[..., :5120], w[..., 5152:], w[..., 5120:5152], pad], axis=-1)


def _unperm_cols(g):
    return jnp.concatenate([g[..., :5120], g[..., OFF_DT:OFF_DT + 32], g[..., 5120:OFF_DT]], axis=-1)


class _Sems:
    def __init__(self, send, recv, local, base=0):
        self._send, self._recv, self._local, self._base = send, recv, local, base

    def shift(self, n):
        return _Sems(self._send, self._recv, self._local, self._base + n)

    def send(self, i):
        return self._send.at[self._base + i]

    def recv(self, i):
        return self._recv.at[self._base + i]

    def local(self, i):
        return self._local.at[self._base + i]


class _Carry:
    def __init__(self, ins, out_shapes, n_sems, start, finish):
        self.ins, self.out_shapes, self.n_sems, self.start, self.finish = list(ins), list(out_shapes), n_sems, start, finish


def _join(*carries):
    def run(which):
        def fn(ins, outs, sems):
            i = o = s = 0
            for cy in carries:
                getattr(cy, which)(ins[i:i + len(cy.ins)], outs[o:o + len(cy.out_shapes)], sems.shift(s))
                i, o, s = i + len(cy.ins), o + len(cy.out_shapes), s + cy.n_sems
        return fn

    return _Carry([a for cy in carries for a in cy.ins], [a for cy in carries for a in cy.out_shapes],
                  sum(cy.n_sems for cy in carries), run("start"), run("finish"))


def _call(body, args, *, name, grid=(), in_specs, out_specs, out_shape, scratch_shapes=(), sem=None, aliases=None,
          carry=None):
    in_specs, out_specs, out_shape, scratch_shapes = list(in_specs), list(out_specs), list(out_shape), list(scratch_shapes)
    n_in, n_out, n_scr = len(in_specs), len(out_specs), len(scratch_shapes)
    kw = {"vmem_limit_bytes": VMEM_LIMIT}
    if carry is None:
        kernel_fn = functools.partial(body)
        if sem is not None:
            kw["dimension_semantics"] = sem
    else:
        n_ci, n_co = len(carry.ins), len(carry.out_shapes)
        hbm = pl.BlockSpec(memory_space=pl.ANY)
        in_specs += [hbm] * n_ci
        out_specs += [hbm] * n_co
        out_shape += carry.out_shapes
        n_s = max(carry.n_sems, 1)
        scratch_shapes += [pltpu.SemaphoreType.DMA((n_s,))] * 3
        args = list(args) + carry.ins
        if grid:
            kw["dimension_semantics"] = ("arbitrary",) * len(grid)

        def kernel_fn(*refs):
            a = n_in
            ins, c_ins = refs[:a], refs[a:a + n_ci]
            a += n_ci
            outs, c_outs = refs[a:a + n_out], refs[a + n_out:a + n_out + n_co]
            a += n_out + n_co
            scr, sems = refs[a:a + n_scr], _Sems(*refs[a + n_scr:a + n_scr + 3])
            if grid:
                ids = [pl.program_id(d) for d in range(len(grid))]
                first = functools.reduce(operator.and_, [i == 0 for i in ids])
                last = functools.reduce(operator.and_, [i == g - 1 for i, g in zip(ids, grid)])

                @pl.when(first)
                def _():
                    carry.start(c_ins, c_outs, sems)

                body(*ins, *outs, *scr)

                @pl.when(last)
                def _():
                    carry.finish(c_ins, c_outs, sems)
            else:
                carry.start(c_ins, c_outs, sems)
                body(*ins, *outs, *scr)
                carry.finish(c_ins, c_outs, sems)

    outs = pl.pallas_call(
        kernel_fn, name=name, grid=grid, in_specs=in_specs, out_specs=out_specs, out_shape=out_shape,
        scratch_shapes=scratch_shapes, input_output_aliases=aliases or {},
        compiler_params=pltpu.CompilerParams(**kw),
    )(*args)
    outs = list(outs)
    return outs if carry is None else (outs[:n_out], outs[n_out:])


def _run_carry(carry, name):
    _, outs = _call(lambda: None, [], name=name, in_specs=[], out_specs=[], out_shape=[], carry=carry)
    return outs


_TILES = {
    "mm_proj": (1024, 2816, 1024), "mm_branch_ssd": (1024, 1024, 2048), "mm_branch_pool": (1024, 1024, 1024),
    "mm_out": (1024, 1024, 1024), "mm_up": (1024, 1024, 1024), "mm_down": (512, 1024, 4096),
    "mm_dact": (1024, 1024, 1024), "mm_g_down": (1024, 1024, 2048), "mm_dh2": (1024, 1024, 4096),
    "mm_g_up": (1024, 1024, 2048), "mm_dmerged": (1024, 1024, 1024), "mm_g_out": (1024, 1024, 2048),
    "mm_dyp": (1024, 1024, 1024), "mm_g_bpool": (1024, 1024, 2048), "mm_g_bssd": (1024, 1024, 2048),
    "mm_dyn": (1024, 1024, 1024), "mm_g_in_a": (512, 1408, 2048), "mm_g_in_b": (512, 1408, 2048),
    "mm_dh1": (1024, 1024, 2816),
}


def _matmul(a, b, *, mode, out_dtypes, name, epi=None, tile_extras=(), row_extras=(), carry=None):
    M, K = (a.shape[1], a.shape[0]) if mode == "tn" else a.shape
    N = b.shape[0] if mode == "nt" else b.shape[1]
    tm, tn, tk = _TILES[name]
    tm, tn, tk = min(tm, M), min(tn, N), min(tk, K)
    assert M % tm == 0 and N % tn == 0 and K % tk == 0, (name, M, N, K, tm, tn, tk)
    if mode == "nn":
        a_spec = pl.BlockSpec((tm, tk), lambda i, j, k: (i, k))
        b_spec = pl.BlockSpec((tk, tn), lambda i, j, k: (k, j))
        dims = ((1,), (0,))
    elif mode == "nt":
        a_spec = pl.BlockSpec((tm, tk), lambda i, j, k: (i, k))
        b_spec = pl.BlockSpec((tn, tk), lambda i, j, k: (j, k))
        dims = ((1,), (1,))
    else:
        a_spec = pl.BlockSpec((tk, tm), lambda i, j, k: (k, i))
        b_spec = pl.BlockSpec((tk, tn), lambda i, j, k: (k, j))
        dims = ((0,), (0,))
    nk = K // tk
    n_te, n_re, n_out = len(tile_extras), len(row_extras), len(out_dtypes)
    if epi is None:
        epi = lambda acc: (acc,)

    def body(a_ref, b_ref, *rest):
        extras = rest[:n_te + n_re]
        outs = rest[n_te + n_re:n_te + n_re + n_out]
        p = _dot(a_ref[...], b_ref[...], dims)

        def finish(acc):
            vals = epi(acc, *[e[...] for e in extras])
            for o, v in zip(outs, vals):
                o[...] = v.astype(o.dtype)

        if nk == 1:
            finish(p)
        else:
            acc_ref = rest[-1]
            k = pl.program_id(2)

            @pl.when(k == 0)
            def _():
                acc_ref[...] = p

            @pl.when(k > 0)
            def _():
                acc_ref[...] += p

            @pl.when(k == nk - 1)
            def _():
                finish(acc_ref[...])

    tile_spec = pl.BlockSpec((tm, tn), lambda i, j, k: (i, j))
    row_spec = pl.BlockSpec((1, tn), lambda i, j, k: (0, j))
    return _call(
        body, [a, b, *tile_extras, *row_extras], name=name, grid=(M // tm, N // tn, nk),
        in_specs=[a_spec, b_spec] + [tile_spec] * n_te + [row_spec] * n_re, out_specs=[tile_spec] * n_out,
        out_shape=[jax.ShapeDtypeStruct((M, N), dt) for dt in out_dtypes],
        scratch_shapes=[pltpu.VMEM((tm, tn), F32)] if nk > 1 else [],
        sem=("parallel", "parallel", "arbitrary"), carry=carry)


def _row_tile(T):
    return min(512, T)


def _norm_mod(x, nw, scale, shift, name):
    T = x.shape[0]
    tr = _row_tile(T)

    def body(x_ref, nw_ref, sc_ref, sh_ref, o_ref):
        xv = x_ref[...]
        r = lax.rsqrt(jnp.mean(xv * xv, axis=-1, keepdims=True) + EPS)
        o_ref[...] = ((xv * r) * nw_ref[...] * (1.0 + sc_ref[...]) + sh_ref[...]).astype(BF16)

    tile = pl.BlockSpec((tr, D), lambda i: (i, 0))
    row = pl.BlockSpec((1, D), lambda i: (0, 0))
    return _call(body, [x, nw, scale, shift], name=name, grid=(T // tr,), in_specs=[tile, row, row, row],
                 out_specs=[tile], out_shape=[jax.ShapeDtypeStruct((T, D), BF16)], sem=("parallel",))[0]


def _norm_mod_bwd(x, dh, dres, nw, scale, name, branch=None, gate=None, carry=None):
    T = x.shape[0]
    tr = _row_tile(T)
    with_branch = branch is not None

    def body(x_ref, dh_ref, dr_ref, nw_ref, sc_ref, *rest):
        if with_branch:
            br_ref, g_ref, dx_ref, sums_ref, db_ref = rest
        else:
            dx_ref, sums_ref = rest
        i = pl.program_id(0)

        @pl.when(i == 0)
        def _():
            sums_ref[...] = jnp.zeros_like(sums_ref)

        xv, dhv = x_ref[...], dh_ref[...]
        r = lax.rsqrt(jnp.mean(xv * xv, axis=-1, keepdims=True) + EPS)
        xn = xv * r
        g1 = dhv * (1.0 + sc_ref[...])
        dxn = g1 * nw_ref[...]
        dx = dr_ref[...] + r * (dxn - xn * jnp.mean(dxn * xn, axis=-1, keepdims=True))
        dx_ref[...] = dx
        sums_ref[0:1, :] += jnp.sum(dhv, axis=0, keepdims=True)
        sums_ref[1:2, :] += jnp.sum(dhv * (xn * nw_ref[...]), axis=0, keepdims=True)
        sums_ref[2:3, :] += jnp.sum(g1 * xn, axis=0, keepdims=True)
        if with_branch:
            db_ref[...] = (dx * g_ref[...]).astype(BF16)
            sums_ref[3:4, :] += jnp.sum(dx * br_ref[...], axis=0, keepdims=True)

    tile = pl.BlockSpec((tr, D), lambda i: (i, 0))
    row = pl.BlockSpec((1, D), lambda i: (0, 0))
    sums = pl.BlockSpec((8, D), lambda i: (0, 0))
    ins = [x, dh, dres, nw, scale] + ([branch, gate] if with_branch else [])
    in_specs = [tile, tile, tile, row, row] + ([tile, row] if with_branch else [])
    out_specs = [tile, sums] + ([tile] if with_branch else [])
    out_shape = [jax.ShapeDtypeStruct((T, D), F32), jax.ShapeDtypeStruct((8, D), F32)]
    if with_branch:
        out_shape.append(jax.ShapeDtypeStruct((T, D), BF16))
    return _call(body, ins, name=name, grid=(T // tr,), in_specs=in_specs, out_specs=out_specs, out_shape=out_shape,
                 sem=("arbitrary",), carry=carry)


def _final_loss_bwd(x3, target, wf, down, gate_f):
    T = x3.shape[0]
    tr = _row_tile(T)
    n_steps = T // tr

    def body(x_ref, t_ref, w_ref, dn_ref, g_ref, dx_ref, dd_ref, sums_ref):
        i = pl.program_id(0)

        @pl.when(i == 0)
        def _():
            sums_ref[...] = jnp.zeros_like(sums_ref)

        xv = x_ref[...]
        r = lax.rsqrt(jnp.mean(xv * xv, axis=-1, keepdims=True) + EPS)
        xn = xv * r
        err = xn * w_ref[...] - t_ref[...]
        dy = err * (1.0 / D)
        dxn = dy * w_ref[...]
        dx = r * (dxn - xn * jnp.mean(dxn * xn, axis=-1, keepdims=True))
        dx_ref[...] = dx
        dd_ref[...] = (dx * g_ref[...]).astype(BF16)
        sums_ref[0:1, :] += jnp.sum(dy * xn, axis=0, keepdims=True)
        sums_ref[1:2, :] += jnp.sum(dx * dn_ref[...], axis=0, keepdims=True)
        sums_ref[2:3, :] += jnp.sum(err * err, axis=0, keepdims=True) * (0.5 / D)

        @pl.when(i == n_steps - 1)
        def _():
            sums_ref[3:4, :] = jnp.broadcast_to(jnp.sum(sums_ref[2:3, :], axis=1, keepdims=True), (1, D))

    tile = pl.BlockSpec((tr, D), lambda i: (i, 0))
    row = pl.BlockSpec((1, D), lambda i: (0, 0))
    sums = pl.BlockSpec((8, D), lambda i: (0, 0))
    return _call(body, [x3, target, wf, down, gate_f], name="final_loss_bwd", grid=(n_steps,),
                 in_specs=[tile, tile, row, tile, row], out_specs=[tile, tile, sums],
                 out_shape=[jax.ShapeDtypeStruct((T, D), F32), jax.ShapeDtypeStruct((T, D), BF16),
                            jax.ShapeDtypeStruct((8, D), F32)], sem=("arbitrary",))


CONV_TC = 512


def _conv_taps(xp, w, b):
    acc = b + w[3:4, :] * xp
    for k in range(3):
        acc = acc + w[k:k + 1, :] * pltpu.roll(xp, 3 - k, 0)
    return acc


def _conv_fwd(proj, conv_w, conv_b):
    T = proj.shape[0]
    tr = _row_tile(T)
    nb, offb = tr // 8, OFF_XBC // CONV_TC

    def body(x_ref, h_ref, w_ref, b_ref, o_ref):
        halo = jnp.where(pl.program_id(0) > 0, h_ref[...], 0.0)
        xp = jnp.concatenate([halo, x_ref[...]], axis=0)
        acc = _conv_taps(xp, w_ref[...], b_ref[...])[8:]
        o_ref[...] = acc * _sigmoid(acc)

    return _call(
        body, [proj, proj, conv_w, conv_b], name="conv_fwd", grid=(T // tr, XBC // CONV_TC),
        in_specs=[pl.BlockSpec((tr, CONV_TC), lambda i, j: (i, j + offb)),
                  pl.BlockSpec((8, CONV_TC), lambda i, j: (jnp.maximum(i * nb - 1, 0), j + offb)),
                  pl.BlockSpec((4, CONV_TC), lambda i, j: (0, j)),
                  pl.BlockSpec((1, CONV_TC), lambda i, j: (0, j))],
        out_specs=[pl.BlockSpec((tr, CONV_TC), lambda i, j: (i, j))],
        out_shape=[jax.ShapeDtypeStruct((T, XBC), F32)], sem=("parallel", "parallel"))[0]


def _conv_bwd_a(dxa, proj, conv_w, conv_b):
    T = proj.shape[0]
    tr = _row_tile(T)
    nb, offb = tr // 8, OFF_XBC // CONV_TC

    def body(d_ref, x_ref, h_ref, w_ref, b_ref, o_ref, sums_ref):
        i = pl.program_id(1)

        @pl.when(i == 0)
        def _():
            sums_ref[...] = jnp.zeros_like(sums_ref)

        halo = jnp.where(i > 0, h_ref[...], 0.0)
        xp = jnp.concatenate([halo, x_ref[...]], axis=0)
        acc = _conv_taps(xp, w_ref[...], b_ref[...])[8:]
        s = _sigmoid(acc)
        dxc = d_ref[...] * (s * (1.0 + acc * (1.0 - s)))
        o_ref[...] = dxc
        sums_ref[3:4, :] += jnp.sum(dxc * x_ref[...], axis=0, keepdims=True)
        for k in range(3):
            sums_ref[k:k + 1, :] += jnp.sum(dxc * pltpu.roll(xp, 3 - k, 0)[8:], axis=0, keepdims=True)
        sums_ref[4:5, :] += jnp.sum(dxc, axis=0, keepdims=True)

    return _call(
        body, [dxa, proj, proj, conv_w, conv_b], name="conv_bwd_a", grid=(XBC // CONV_TC, T // tr),
        in_specs=[pl.BlockSpec((tr, CONV_TC), lambda j, i: (i, j)),
                  pl.BlockSpec((tr, CONV_TC), lambda j, i: (i, j + offb)),
                  pl.BlockSpec((8, CONV_TC), lambda j, i: (jnp.maximum(i * nb - 1, 0), j + offb)),
                  pl.BlockSpec((4, CONV_TC), lambda j, i: (0, j)),
                  pl.BlockSpec((1, CONV_TC), lambda j, i: (0, j))],
        out_specs=[pl.BlockSpec((tr, CONV_TC), lambda j, i: (i, j)), pl.BlockSpec((8, CONV_TC), lambda j, i: (0, j))],
        out_shape=[jax.ShapeDtypeStruct((T, XBC), F32), jax.ShapeDtypeStruct((8, XBC), F32)],
        sem=("parallel", "arbitrary"))


def _conv_bwd_b(dxc, conv_w, dproj):
    T = dxc.shape[0]
    tr = _row_tile(T)
    nb, offb, last = tr // 8, OFF_XBC // CONV_TC, T // tr - 1

    def body(d_ref, h_ref, w_ref, dp_in, o_ref):
        del dp_in
        halo = jnp.where(pl.program_id(0) < last, h_ref[...], 0.0)
        xp = jnp.concatenate([d_ref[...], halo], axis=0)
        n = xp.shape[0]
        w = w_ref[...]
        acc = w[3:4, :] * xp
        for k in range(3):
            acc = acc + w[k:k + 1, :] * pltpu.roll(xp, n - (3 - k), 0)
        o_ref[...] = acc[:tr].astype(BF16)

    return _call(
        body, [dxc, dxc, conv_w, dproj], name="conv_bwd_b", grid=(T // tr, XBC // CONV_TC),
        in_specs=[pl.BlockSpec((tr, CONV_TC), lambda i, j: (i, j)),
                  pl.BlockSpec((8, CONV_TC), lambda i, j: (jnp.minimum((i + 1) * nb, T // 8 - 1), j)),
                  pl.BlockSpec((4, CONV_TC), lambda i, j: (0, j)),
                  pl.BlockSpec(memory_space=pl.ANY)],
        out_specs=[pl.BlockSpec((tr, CONV_TC), lambda i, j: (i, j + offb))],
        out_shape=[jax.ShapeDtypeStruct(dproj.shape, BF16)], aliases={3: 0}, sem=("parallel", "parallel"))[0]


def _expand_pairs(v, j0, n_pairs, lo):
    R = v.shape[0]
    parts = []
    for j in range(j0, j0 + n_pairs):
        va = jnp.broadcast_to(v[:, 2 * j:2 * j + 1], (R, 128))
        vb = jnp.broadcast_to(v[:, 2 * j + 1:2 * j + 2], (R, 128))
        parts.append(jnp.where(lo, va, vb))
    return jnp.concatenate(parts, axis=1)


def _ssd_common(dtp_ref, dtb_r, alog_r, dtb_c, alog_c):
    rows = lax.broadcasted_iota(jnp.int32, (Q, Q), 0)
    cols = lax.broadcasted_iota(jnp.int32, (Q, Q), 1)
    causal = cols <= rows
    tri = causal.astype(F32)
    raw = dtp_ref[:, 0:HEADS] + dtb_r[...]
    dt = _softplus(raw)
    a_r = -jnp.exp(alog_r[...])
    cs = _nn(tri, dt * a_r, precision=HIGH)
    aT = _softplus(dtp_ref[...].T[0:HEADS, :] + dtb_c[...]) * (-jnp.exp(alog_c[...]))
    csT = _nt(aT, tri, precision=HIGH)
    return causal, raw, dt, a_r, cs, csT


def _ssd_fwd(xbc_a, proj, dtb_r, alog_r, dtb_c, alog_c, dsk_exp):
    T = xbc_a.shape[0]
    nc = T // Q

    def body(xbc_ref, dtp_ref, dtb_r_ref, alog_r_ref, dtb_c_ref, alog_c_ref, dsk_ref, y_ref, hin_ref, h_scr):
        @pl.when(pl.program_id(0) == 0)
        def _():
            h_scr[...] = jnp.zeros_like(h_scr)

        causal, _, dt, _, cs, csT = _ssd_common(dtp_ref, dtb_r_ref, alog_r_ref, dtb_c_ref, alog_c_ref)
        lo = lax.broadcasted_iota(jnp.int32, (1, 128), 1) < HEAD_DIM
        cs_last = cs[Q - 1:Q, :]
        ecs, dec, cd = jnp.exp(cs), jnp.exp(cs_last - cs), jnp.exp(cs_last)
        for g in range(GROUPS):
            gs = slice(512 * g, 512 * (g + 1))
            xs_g = xbc_ref[:, gs]
            b_g = xbc_ref[:, DI + STATE * g:DI + STATE * (g + 1)].astype(BF16)
            c_g = xbc_ref[:, DI + 512 + STATE * g:DI + 512 + STATE * (g + 1)].astype(BF16)
            xdt = xs_g * _expand_pairs(dt, 4 * g, 4, lo)
            xdt_b = xdt.astype(BF16)
            s_mat = _nt(c_g, b_g)
            ys = []
            for jj in range(4):
                xp = xdt_b[:, 128 * jj:128 * (jj + 1)]
                acc = None
                for h, sel in ((8 * g + 2 * jj, lo), (8 * g + 2 * jj + 1, jnp.logical_not(lo))):
                    l_mat = jnp.exp(jnp.where(causal, cs[:, h:h + 1] - csT[h:h + 1, :], NEG))
                    part = _nn((s_mat * l_mat).astype(BF16), jnp.where(sel, xp, jnp.zeros_like(xp)))
                    acc = part if acc is None else acc + part
                ys.append(acc)
            h_g = h_scr[128 * g:128 * (g + 1), :]
            hin_ref[0, 128 * g:128 * (g + 1), :] = h_g
            y_off = _nn(c_g, h_g.astype(BF16)) * _expand_pairs(ecs, 4 * g, 4, lo)
            y_ref[:, gs] = jnp.concatenate(ys, axis=1) + y_off + dsk_ref[:, gs] * xs_g
            xdec = (xdt * _expand_pairs(dec, 4 * g, 4, lo)).astype(BF16)
            h_scr[128 * g:128 * (g + 1), :] = h_g * _expand_pairs(cd, 4 * g, 4, lo) + _tn(b_g, xdec)

    small_r = pl.BlockSpec((1, HEADS), lambda c: (0, 0))
    small_c = pl.BlockSpec((HEADS, 1), lambda c: (0, 0))
    return _call(
        body, [xbc_a, proj, dtb_r, alog_r, dtb_c, alog_c, dsk_exp], name="ssd_fwd", grid=(nc,),
        in_specs=[pl.BlockSpec((Q, XBC), lambda c: (c, 0)),
                  pl.BlockSpec((Q, 128), lambda c: (c, OFF_DT // 128)),
                  small_r, small_r, small_c, small_c,
                  pl.BlockSpec((1, DI), lambda c: (0, 0))],
        out_specs=[pl.BlockSpec((Q, DI), lambda c: (c, 0)), pl.BlockSpec((1, 512, 512), lambda c: (c, 0, 0))],
        out_shape=[jax.ShapeDtypeStruct((T, DI), F32), jax.ShapeDtypeStruct((nc, 512, 512), F32)],
        scratch_shapes=[pltpu.VMEM((512, 512), F32)], sem=("arbitrary",))


def _ssd_bwd(dy, xbc_a, proj, hin, dtb_r, alog_r, dtb_c, alog_c, dsk_exp, dproj, carry=None):
    T = xbc_a.shape[0]
    nc = T // Q

    def body(dy_ref, xbc_ref, dtp_ref, hin_ref, dtb_r_ref, alog_r_ref, dtb_c_ref, alog_c_ref, dsk_ref, dp_in,
             dxa_ref, dp_ref, dsk_sum_ref, small_ref, dh_scr):
        del dp_in

        @pl.when(pl.program_id(0) == 0)
        def _():
            dh_scr[...] = jnp.zeros_like(dh_scr)
            dsk_sum_ref[...] = jnp.zeros_like(dsk_sum_ref)
            small_ref[...] = jnp.zeros_like(small_ref)

        causal, raw, dt, a_r, cs, csT = _ssd_common(dtp_ref, dtb_r_ref, alog_r_ref, dtb_c_ref, alog_c_ref)
        lane = lax.broadcasted_iota(jnp.int32, (1, 128), 1)
        lo = lane < HEAD_DIM
        hi = jnp.logical_not(lo)
        lane32 = lax.broadcasted_iota(jnp.int32, (1, HEADS), 1)
        sub32 = lax.broadcasted_iota(jnp.int32, (HEADS, 1), 0)
        cs_last = cs[Q - 1:Q, :]
        ecs, dec, cd = jnp.exp(cs), jnp.exp(cs_last - cs), jnp.exp(cs_last)
        dcs_c = jnp.zeros((Q, HEADS), F32)
        dcs_r = jnp.zeros((HEADS, Q), F32)
        dcs_l = jnp.zeros((1, HEADS), F32)
        ddt_x = jnp.zeros((Q, HEADS), F32)

        def put(vec, h, val):
            return vec + jnp.where(lane32 == h, val, 0.0)

        def halves(v):
            sa = jnp.sum(jnp.where(lo, v, 0.0), axis=1, keepdims=True)
            sb = jnp.sum(jnp.where(lo, 0.0, v), axis=1, keepdims=True)
            return sa, sb

        for g in range(GROUPS):
            gs = slice(512 * g, 512 * (g + 1))
            hs = slice(128 * g, 128 * (g + 1))
            xs_g = xbc_ref[:, gs]
            b_g = xbc_ref[:, DI + STATE * g:DI + STATE * (g + 1)].astype(BF16)
            c_g = xbc_ref[:, DI + 512 + STATE * g:DI + 512 + STATE * (g + 1)].astype(BF16)
            dt_g = _expand_pairs(dt, 4 * g, 4, lo)
            ecs_g = _expand_pairs(ecs, 4 * g, 4, lo)
            dec_g = _expand_pairs(dec, 4 * g, 4, lo)
            cd_g = _expand_pairs(cd, 4 * g, 4, lo)
            xdt = xs_g * dt_g
            xdt_b = xdt.astype(BF16)
            dy_g = dy_ref[:, gs]
            dy_b = dy_g.astype(BF16)
            s_mat = _nt(c_g, b_g)
            ds_mat = jnp.zeros((Q, Q), F32)
            dx_parts = []
            for jj in range(4):
                xp = xdt_b[:, 128 * jj:128 * (jj + 1)]
                dyp = dy_b[:, 128 * jj:128 * (jj + 1)]
                dxh = []
                for h, sel in ((8 * g + 2 * jj, lo), (8 * g + 2 * jj + 1, hi)):
                    l_mat = jnp.exp(jnp.where(causal, cs[:, h:h + 1] - csT[h:h + 1, :], NEG))
                    m_mat = s_mat * l_mat
                    dm = _nt(jnp.where(sel, dyp, jnp.zeros_like(dyp)), xp)
                    w_mat = dm * m_mat
                    dcs_c = put(dcs_c, h, jnp.sum(w_mat, axis=1, keepdims=True))
                    dcs_r = dcs_r + jnp.where(sub32 == h, jnp.sum(w_mat, axis=0, keepdims=True), 0.0)
                    ds_mat = ds_mat + dm * l_mat
                    dxh.append(_tn(m_mat.astype(BF16), dyp))
                dx_parts.append(jnp.where(lo, dxh[0], dxh[1]))
            hin_g = hin_ref[0, hs, :]
            hin_b = hin_g.astype(BF16)
            dh_g = dh_scr[hs, :]
            dh_b = dh_g.astype(BF16)
            y_off = _nn(c_g, hin_b) * ecs_g
            dz = (dy_g * ecs_g).astype(BF16)
            g_mat = _nn(b_g, dh_b)
            xdec = xdt * dec_g
            v1 = dy_g * y_off - xdec * g_mat
            v2 = jnp.sum(xdec * g_mat, axis=0, keepdims=True) + jnp.sum(dh_g * hin_g, axis=0, keepdims=True) * cd_g
            dxdt = jnp.concatenate(dx_parts, axis=1) + dec_g * g_mat
            v3 = dxdt * xs_g
            for jj in range(4):
                ps = slice(128 * jj, 128 * (jj + 1))
                ha = 8 * g + 2 * jj
                sa, sb = halves(v1[:, ps])
                dcs_c = put(put(dcs_c, ha, sa), ha + 1, sb)
                sa, sb = halves(v2[:, ps])
                dcs_l = put(put(dcs_l, ha, sa), ha + 1, sb)
                sa, sb = halves(v3[:, ps])
                ddt_x = put(put(ddt_x, ha, sa), ha + 1, sb)
            ds_b = ds_mat.astype(BF16)
            dxa_ref[:, gs] = dxdt * dt_g + dy_g * dsk_ref[:, gs]
            dxa_ref[:, DI + STATE * g:DI + STATE * (g + 1)] = _nt(xdec.astype(BF16), dh_b) + _tn(ds_b, c_g)
            dxa_ref[:, DI + 512 + STATE * g:DI + 512 + STATE * (g + 1)] = _nt(dz, hin_b) + _nn(ds_b, b_g)
            dh_scr[hs, :] = _tn(c_g, dz) + dh_g * cd_g
            dsk_sum_ref[0:1, gs] += jnp.sum(dy_g * xs_g, axis=0, keepdims=True)

        rows = lax.broadcasted_iota(jnp.int32, (Q, Q), 0)
        cols = lax.broadcasted_iota(jnp.int32, (Q, Q), 1)
        tri_t = (cols >= rows).astype(F32)
        last_row = lax.broadcasted_iota(jnp.int32, (Q, 1), 0) == Q - 1
        dcs = dcs_c + jnp.where(last_row, dcs_l, 0.0)
        da = _nn(tri_t, dcs, precision=HIGH) - _nt(tri_t, dcs_r, precision=HIGH)
        ddt_raw = (ddt_x + da * a_r) * _sigmoid(raw)
        small_ref[0:1, :] += jnp.sum(da * dt, axis=0, keepdims=True) * a_r
        small_ref[1:2, :] += jnp.sum(ddt_raw, axis=0, keepdims=True)
        dp_ref[...] = jnp.zeros_like(dp_ref)
        dp_ref[:, 0:HEADS] = ddt_raw.astype(BF16)

    rev = lambda c: nc - 1 - c
    small_r = pl.BlockSpec((1, HEADS), lambda c: (0, 0))
    small_c = pl.BlockSpec((HEADS, 1), lambda c: (0, 0))
    return _call(
        body, [dy, xbc_a, proj, hin, dtb_r, alog_r, dtb_c, alog_c, dsk_exp, dproj], name="ssd_bwd", grid=(nc,),
        in_specs=[pl.BlockSpec((Q, DI), lambda c: (rev(c), 0)),
                  pl.BlockSpec((Q, XBC), lambda c: (rev(c), 0)),
                  pl.BlockSpec((Q, 128), lambda c: (rev(c), OFF_DT // 128)),
                  pl.BlockSpec((1, 512, 512), lambda c: (rev(c), 0, 0)),
                  small_r, small_r, small_c, small_c,
                  pl.BlockSpec((1, DI), lambda c: (0, 0)),
                  pl.BlockSpec(memory_space=pl.ANY)],
        out_specs=[pl.BlockSpec((Q, XBC), lambda c: (rev(c), 0)),
                   pl.BlockSpec((Q, 256), lambda c: (rev(c), OFF_DT // 256)),
                   pl.BlockSpec((8, DI), lambda c: (0, 0)),
                   pl.BlockSpec((8, HEADS), lambda c: (0, 0))],
        out_shape=[jax.ShapeDtypeStruct((T, XBC), F32), jax.ShapeDtypeStruct(dproj.shape, BF16),
                   jax.ShapeDtypeStruct((8, DI), F32), jax.ShapeDtypeStruct((8, HEADS), F32)],
        aliases={9: 1}, scratch_shapes=[pltpu.VMEM((512, 512), F32)], sem=("arbitrary",), carry=carry)


def _gate_norm(y, proj, w):
    T = y.shape[0]
    tr = _row_tile(T)

    def body(y_ref, z_ref, w_ref, o_ref):
        for g in range(GROUPS):
            gs = slice(512 * g, 512 * (g + 1))
            z = z_ref[:, gs]
            yg = y_ref[:, gs] * (z * _sigmoid(z))
            r = lax.rsqrt(jnp.mean(yg * yg, axis=-1, keepdims=True) + EPS)
            o_ref[:, gs] = (yg * r * w_ref[:, gs]).astype(BF16)

    tile = pl.BlockSpec((tr, DI), lambda i: (i, 0))
    return _call(body, [y, proj, w], name="gate_norm", grid=(T // tr,),
                 in_specs=[tile, tile, pl.BlockSpec((1, DI), lambda i: (0, 0))], out_specs=[tile],
                 out_shape=[jax.ShapeDtypeStruct((T, DI), BF16)], sem=("parallel",))[0]


def _gate_norm_bwd(dyn, y, proj, w, dproj):
    T = y.shape[0]
    tr = _row_tile(T)

    def body(d_ref, y_ref, z_ref, w_ref, dp_in, dy_ref, dz_ref, sums_ref):
        del dp_in

        @pl.when(pl.program_id(0) == 0)
        def _():
            sums_ref[...] = jnp.zeros_like(sums_ref)

        for g in range(GROUPS):
            gs = slice(512 * g, 512 * (g + 1))
            z, yv, d = z_ref[:, gs], y_ref[:, gs], d_ref[:, gs]
            s = _sigmoid(z)
            silu = z * s
            yg = yv * silu
            r = lax.rsqrt(jnp.mean(yg * yg, axis=-1, keepdims=True) + EPS)
            yn = yg * r
            sums_ref[0:1, gs] += jnp.sum(d * yn, axis=0, keepdims=True)
            dn = d * w_ref[:, gs]
            dyg = r * (dn - yn * jnp.mean(dn * yn, axis=-1, keepdims=True))
            dy_ref[:, gs] = dyg * silu
            dz_ref[:, gs] = (dyg * yv * (s * (1.0 + z * (1.0 - s)))).astype(BF16)

    tile = pl.BlockSpec((tr, DI), lambda i: (i, 0))
    return _call(
        body, [dyn, y, proj, w, dproj], name="gate_norm_bwd", grid=(T // tr,),
        in_specs=[tile, tile, tile, pl.BlockSpec((1, DI), lambda i: (0, 0)), pl.BlockSpec(memory_space=pl.ANY)],
        out_specs=[tile, tile, pl.BlockSpec((8, DI), lambda i: (0, 0))],
        out_shape=[jax.ShapeDtypeStruct((T, DI), F32), jax.ShapeDtypeStruct(dproj.shape, BF16),
                   jax.ShapeDtypeStruct((8, DI), F32)],
        aliases={4: 1}, sem=("arbitrary",))


def _pool_fwd(proj, pool_w_b, pool_scale):
    T = proj.shape[0]
    tr = _row_tile(T)
    nb = tr // 16

    def body(u_ref, h_ref, pw_ref, ps_ref, pooled_ref, pw_out_ref, yps_ref):
        i = pl.program_id(0)
        t = i * tr + lax.broadcasted_iota(jnp.int32, (tr, 1), 0)
        for g, win in enumerate(POOL_WINDOWS):
            gs = slice(GW * g, GW * (g + 1))
            u = u_ref[:, gs]
            s = jnp.concatenate([jnp.where(i > 0, h_ref[:, gs], 0.0), u], axis=0)
            sh = 1
            while sh < win:
                s = s + pltpu.roll(s, sh, 0)
                sh *= 2
            pooled = (s[16:] * (1.0 / jnp.minimum(t + 1, win).astype(F32)) - u).astype(BF16)
            pooled_ref[:, gs] = pooled
            pwv = _nn(pooled, pw_ref[g])
            pw_out_ref[:, gs] = pwv
            yps_ref[:, gs] = (pwv * ps_ref[:, gs]).astype(BF16)

    tile = pl.BlockSpec((tr, D), lambda i: (i, 0))
    return _call(
        body, [proj, proj, pool_w_b, pool_scale], name="pool_fwd", grid=(T // tr,),
        in_specs=[pl.BlockSpec((tr, D), lambda i: (i, OFF_POOL // D)),
                  pl.BlockSpec((16, D), lambda i: (jnp.maximum(i * nb - 1, 0), OFF_POOL // D)),
                  pl.BlockSpec((4, GW, GW), lambda i: (0, 0, 0)),
                  pl.BlockSpec((1, D), lambda i: (0, 0))],
        out_specs=[tile, tile, tile],
        out_shape=[jax.ShapeDtypeStruct((T, D), BF16), jax.ShapeDtypeStruct((T, D), F32),
                   jax.ShapeDtypeStruct((T, D), BF16)], sem=("parallel",))


def _pool_bwd(dyp, pw_out, pooled, pool_w_b, pool_scale, dproj):
    T = dyp.shape[0]
    tr = _row_tile(T)
    nb, last = tr // 16, T // tr - 1

    def body(d_ref, h_ref, pwo_ref, pooled_ref, pw_ref, ps_ref, dp_in, du_ref, gpw_ref, sums_ref):
        del dp_in
        i = pl.program_id(0)

        @pl.when(i == 0)
        def _():
            gpw_ref[...] = jnp.zeros_like(gpw_ref)
            sums_ref[...] = jnp.zeros_like(sums_ref)

        n = tr + 16
        t = i * tr + lax.broadcasted_iota(jnp.int32, (n, 1), 0)
        sums_ref[0:1, :] += jnp.sum(d_ref[...] * pwo_ref[...], axis=0, keepdims=True)
        for g, win in enumerate(POOL_WINDOWS):
            gs = slice(GW * g, GW * (g + 1))
            d_ext = jnp.concatenate([d_ref[:, gs], jnp.where(i < last, h_ref[:, gs], 0.0)], axis=0)
            dpw = (d_ext * ps_ref[:, gs]).astype(BF16)
            dpooled = _nt(dpw, pw_ref[g])
            s = jnp.where(t < T, dpooled * (1.0 / jnp.minimum(t + 1, win).astype(F32)), 0.0)
            sh = 1
            while sh < win:
                s = s + pltpu.roll(s, n - sh, 0)
                sh *= 2
            du_ref[:, gs] = (s[:tr] - dpooled[:tr]).astype(BF16)
            gpw_ref[g] += _tn(pooled_ref[:, gs], dpw[:tr])

    tile = pl.BlockSpec((tr, D), lambda i: (i, 0))
    return _call(
        body, [dyp, dyp, pw_out, pooled, pool_w_b, pool_scale, dproj], name="pool_bwd", grid=(T // tr,),
        in_specs=[tile, pl.BlockSpec((16, D), lambda i: (jnp.minimum((i + 1) * nb, T // 16 - 1), 0)), tile, tile,
                  pl.BlockSpec((4, GW, GW), lambda i: (0, 0, 0)), pl.BlockSpec((1, D), lambda i: (0, 0)),
                  pl.BlockSpec(memory_space=pl.ANY)],
        out_specs=[pl.BlockSpec((tr, D), lambda i: (i, OFF_POOL // D)),
                   pl.BlockSpec((4, GW, GW), lambda i: (0, 0, 0)), pl.BlockSpec((8, D), lambda i: (0, 0))],
        out_shape=[jax.ShapeDtypeStruct(dproj.shape, BF16), jax.ShapeDtypeStruct((4, GW, GW), F32),
                   jax.ShapeDtypeStruct((8, D), F32)],
        aliases={6: 0}, sem=("arbitrary",))


def _merge(proj, y_ssd, y_pool):
    T = proj.shape[0]
    tr = _row_tile(T)

    def body(g_ref, a_ref, b_ref, o_ref):
        o_ref[...] = (_sigmoid(g_ref[:, 0:D]) * a_ref[...] + _sigmoid(g_ref[:, D:2 * D]) * b_ref[...]).astype(BF16)

    tile = pl.BlockSpec((tr, D), lambda i: (i, 0))
    return _call(body, [proj, y_ssd, y_pool], name="merge", grid=(T // tr,),
                 in_specs=[pl.BlockSpec((tr, 2 * D), lambda i: (i, OFF_GATE // (2 * D))), tile, tile], out_specs=[tile],
                 out_shape=[jax.ShapeDtypeStruct((T, D), BF16)], sem=("parallel",))[0]


def _merge_bwd(dmerged, proj, y_ssd, y_pool):
    T = proj.shape[0]
    tr = _row_tile(T)

    def body(d_ref, g_ref, a_ref, b_ref, da_ref, db_ref, dg_ref):
        d = d_ref[...]
        ga, gb = _sigmoid(g_ref[:, 0:D]), _sigmoid(g_ref[:, D:2 * D])
        da_ref[...] = (d * ga).astype(BF16)
        db_ref[...] = (d * gb).astype(BF16)
        dg_ref[:, 0:D] = (d * a_ref[...] * ga * (1.0 - ga)).astype(BF16)
        dg_ref[:, D:2 * D] = (d * b_ref[...] * gb * (1.0 - gb)).astype(BF16)

    tile = pl.BlockSpec((tr, D), lambda i: (i, 0))
    gates = pl.BlockSpec((tr, 2 * D), lambda i: (i, OFF_GATE // (2 * D)))
    return _call(body, [dmerged, proj, y_ssd, y_pool], name="merge_bwd", grid=(T // tr,),
                 in_specs=[tile, gates, tile, tile], out_specs=[tile, tile, gates],
                 out_shape=[jax.ShapeDtypeStruct((T, D), BF16), jax.ShapeDtypeStruct((T, D), BF16),
                            jax.ShapeDtypeStruct((T, NP), BF16)], sem=("parallel",))


def _adamw(w, g, m, v, name, carry=None):
    R, C = w.shape
    tr = R if R <= 128 else 128
    assert R % tr == 0

    def body(w_ref, g_ref, m_ref, v_ref, d_ref, mo_ref, vo_ref):
        gv = g_ref[...]
        mn = ADAM_B1 * m_ref[...] + (1.0 - ADAM_B1) * gv
        vn = ADAM_B2 * v_ref[...] + (1.0 - ADAM_B2) * (gv * gv)
        m_hat = mn / (1.0 - ADAM_B1 ** ADAM_STEP)
        v_hat = vn / (1.0 - ADAM_B2 ** ADAM_STEP)
        d_ref[...] = -ADAM_LR * (m_hat / (jnp.sqrt(v_hat) + ADAM_EPS) + ADAM_WD * w_ref[...])
        mo_ref[...] = mn
        vo_ref[...] = vn

    tile = pl.BlockSpec((tr, C), lambda i: (i, 0))
    sds = jax.ShapeDtypeStruct((R, C), F32)
    return _call(body, [w, g, m, v], name=name, grid=(R // tr,), in_specs=[tile] * 4, out_specs=[tile] * 3,
                 out_shape=[sds] * 3, sem=("parallel",), carry=carry)


def _me():
    return lax.axis_index("x"), lax.axis_index("y"), lax.axis_index("c")


def _xor_peer(x, y, c, p):
    return (x ^ ((p >> 2) & 1), y ^ ((p >> 1) & 1), c ^ (p & 1))


def _ada_fwd(c_row, w_ada, b_ada_mine):
    n_cols = w_ada.shape[1]

    def body(c_ref, w_ref, b_ref, mod_ref, c8_ref, csend, mpart, modbuf, send_sems, recv_sems):
        x, y, c = _me()
        me = 4 * x + 2 * y + c
        chip = 2 * x + y
        csend[...] = jnp.broadcast_to(c_ref[...], csend.shape)
        c8_ref[me] = csend[...]

        def c_copy(p):
            return pltpu.make_async_remote_copy(
                src_ref=csend, dst_ref=c8_ref.at[me], send_sem=send_sems.at[p - 1], recv_sem=recv_sems.at[p - 1],
                device_id=_xor_peer(x, y, c, p), device_id_type=MESH)

        for p in range(1, 8):
            c_copy(p).start()
        for p in range(1, 8):
            c_copy(p).wait_recv()
        cs = jnp.concatenate([c8_ref[d][0:1, :] for d in range(8)], axis=0)
        mpart[...] = _nn(cs * _sigmoid(cs), w_ref[...], precision=HIGH) + b_ref[...]
        modbuf[chip] = mpart[...]

        def m_copy(m):
            return pltpu.make_async_remote_copy(
                src_ref=mpart, dst_ref=modbuf.at[chip], send_sem=send_sems.at[6 + m], recv_sem=recv_sems.at[6 + m],
                device_id=_xor_peer(x, y, c, 2 * m), device_id_type=MESH)

        for m in range(1, 4):
            m_copy(m).start()
        for m in range(1, 4):
            m_copy(m).wait_recv()
        mine = lax.broadcasted_iota(jnp.int32, (8, 1), 0) == me
        for k in range(N_CHIPS):
            mod_ref[:, n_cols * k:n_cols * (k + 1)] = jnp.sum(jnp.where(mine, modbuf[k], 0.0), axis=0, keepdims=True)
        for p in range(1, 8):
            c_copy(p).wait_send()
        for m in range(1, 4):
            m_copy(m).wait_send()

    vmem = pl.BlockSpec(memory_space=pltpu.VMEM)
    return _call(
        body, [c_row, w_ada, b_ada_mine], name="ada_fwd", in_specs=[vmem, vmem, vmem], out_specs=[vmem, vmem],
        out_shape=[jax.ShapeDtypeStruct((1, N_CHIPS * n_cols), F32), jax.ShapeDtypeStruct((8, 8, D), F32)],
        scratch_shapes=[pltpu.VMEM((8, D), F32), pltpu.VMEM((8, n_cols), F32), pltpu.VMEM((N_CHIPS, 8, n_cols), F32),
                        pltpu.SemaphoreType.DMA((10,)), pltpu.SemaphoreType.DMA((10,))])


def _gather_small(vec):
    rows = vec.shape[0]

    def body(v_ref, all_ref, tot_ref, dsk_ref, send_sems, recv_sems):
        x, y, c = _me()
        me = 4 * x + 2 * y + c
        all_ref[me] = v_ref[...]

        def copy(p):
            return pltpu.make_async_remote_copy(
                src_ref=v_ref, dst_ref=all_ref.at[me], send_sem=send_sems.at[p - 1], recv_sem=recv_sems.at[p - 1],
                device_id=_xor_peer(x, y, c, p), device_id_type=MESH)

        for p in range(1, 8):
            copy(p).start()
        for p in range(1, 8):
            copy(p).wait_recv()
        tot = all_ref[0]
        for d in range(1, 8):
            tot = tot + all_ref[d]
        tot_ref[...] = tot
        seg = tot[SMALL_OFF["d_skip"] // 128:SMALL_OFF["d_skip"] // 128 + 16, :]
        lane = lax.broadcasted_iota(jnp.int32, (1, 128), 1)
        sa = jnp.sum(jnp.where(lane < HEAD_DIM, seg, 0.0), axis=1, keepdims=True)
        sb = jnp.sum(jnp.where(lane < HEAD_DIM, 0.0, seg), axis=1, keepdims=True)
        dsk_ref[...] = jnp.where(lane == 0, sa, jnp.where(lane == 1, sb, 0.0))
        for p in range(1, 8):
            copy(p).wait_send()

    vmem = pl.BlockSpec(memory_space=pltpu.VMEM)
    return _call(
        body, [vec], name="gather_small", in_specs=[vmem], out_specs=[vmem, vmem, vmem],
        out_shape=[jax.ShapeDtypeStruct((8, rows, 128), F32), jax.ShapeDtypeStruct((rows, 128), F32),
                   jax.ShapeDtypeStruct((16, 128), F32)],
        scratch_shapes=[pltpu.SemaphoreType.DMA((7,)), pltpu.SemaphoreType.DMA((7,))])


def _gather_carry(shards):
    n = len(shards)

    def copies(ins, outs, sems):
        x, y, c = _me()
        chip = 2 * x + y

        def half(w, which):
            h = shards[w].shape[0] // 2
            return pl.ds(which * h, h)

        local = [pltpu.make_async_copy(ins[w], outs[w].at[chip], sems.local(w)) for w in range(n)]

        def first(w, m):
            return pltpu.make_async_remote_copy(
                src_ref=ins[w].at[half(w, c)], dst_ref=outs[w].at[chip, half(w, c)],
                send_sem=sems.send(6 * w + m - 1), recv_sem=sems.recv(6 * w + m - 1),
                device_id=_xor_peer(x, y, c, 2 * m), device_id_type=MESH)

        def landed(w, m):
            return pltpu.make_async_remote_copy(
                src_ref=ins[w].at[half(w, c)], dst_ref=outs[w].at[chip ^ m, half(w, c)],
                send_sem=sems.send(6 * w + m - 1), recv_sem=sems.recv(6 * w + m - 1),
                device_id=_xor_peer(x, y, c, 2 * m), device_id_type=MESH)

        def passed(w, m, which):
            part = outs[w].at[chip ^ m, half(w, which)]
            return pltpu.make_async_remote_copy(
                src_ref=part, dst_ref=part, send_sem=sems.send(6 * w + 2 + m), recv_sem=sems.recv(6 * w + 2 + m),
                device_id=(x, y, 1 - c), device_id_type=MESH)

        return c, local, first, landed, passed

    pairs = [(w, m) for w in range(n) for m in range(1, 4)]

    def start(ins, outs, sems):
        _, local, first, _, _ = copies(ins, outs, sems)
        for cp in local:
            cp.start()
        for w, m in pairs:
            first(w, m).start()

    def finish(ins, outs, sems):
        c, local, first, landed, passed = copies(ins, outs, sems)
        for w, m in pairs:
            landed(w, m).wait_recv()
            passed(w, m, c).start()
        for w, m in pairs:
            passed(w, m, 1 - c).wait_recv()
        for w, m in pairs:
            first(w, m).wait_send()
            passed(w, m, c).wait_send()
        for cp in local:
            cp.wait()

    return _Carry(shards, [jax.ShapeDtypeStruct((N_CHIPS,) + s.shape, s.dtype) for s in shards], 6 * n, start, finish)


def _pair_exchange_carry(grads):
    n = len(grads)

    def copy(ins, outs, sems, w):
        x, y, c = _me()
        h = grads[w].shape[1] // 2
        return pltpu.make_async_remote_copy(
            src_ref=ins[w].at[:, pl.ds((1 - c) * h, h)], dst_ref=outs[w],
            send_sem=sems.send(w), recv_sem=sems.recv(w), device_id=(x, y, 1 - c), device_id_type=MESH)

    def start(ins, outs, sems):
        for w in range(n):
            copy(ins, outs, sems, w).start()

    def finish(ins, outs, sems):
        for w in range(n):
            copy(ins, outs, sems, w).wait()

    return _Carry(grads, [jax.ShapeDtypeStruct((N_CHIPS, g.shape[1] // 2, g.shape[2]), g.dtype) for g in grads], n,
                  start, finish)


def _chip_exchange_carry(partials):
    n = len(partials)

    def copies(ins, outs, sems):
        x, y, c = _me()
        chip = 2 * x + y
        local = [pltpu.make_async_copy(ins[w].at[chip], outs[w].at[chip], sems.local(w)) for w in range(n)]

        def copy(w, m, landed):
            return pltpu.make_async_remote_copy(
                src_ref=ins[w].at[chip ^ m], dst_ref=outs[w].at[(chip ^ m) if landed else chip],
                send_sem=sems.send(3 * w + m - 1), recv_sem=sems.recv(3 * w + m - 1),
                device_id=_xor_peer(x, y, c, 2 * m), device_id_type=MESH)

        return local, copy

    pairs = [(w, m) for w in range(n) for m in range(1, 4)]

    def start(ins, outs, sems):
        local, copy = copies(ins, outs, sems)
        for cp in local:
            cp.start()
        for w, m in pairs:
            copy(w, m, False).start()

    def finish(ins, outs, sems):
        local, copy = copies(ins, outs, sems)
        for w, m in pairs:
            copy(w, m, True).wait_recv()
        for w, m in pairs:
            copy(w, m, False).wait_send()
        for cp in local:
            cp.wait()

    return _Carry(partials, [jax.ShapeDtypeStruct(p.shape, p.dtype) for p in partials], 3 * n, start, finish)


def _pair_share_carry(halves):
    n = len(halves)

    def copies(ins, outs, sems):
        x, y, c = _me()

        def rows(w, which):
            h = halves[w].shape[0]
            return pl.ds(which * h, h)

        local = [pltpu.make_async_copy(ins[w], outs[w].at[rows(w, c)], sems.local(w)) for w in range(n)]

        def copy(w, which):
            return pltpu.make_async_remote_copy(
                src_ref=ins[w], dst_ref=outs[w].at[rows(w, which)],
                send_sem=sems.send(w), recv_sem=sems.recv(w), device_id=(x, y, 1 - c), device_id_type=MESH)

        return c, local, copy

    def start(ins, outs, sems):
        c, local, copy = copies(ins, outs, sems)
        for cp in local:
            cp.start()
        for w in range(n):
            copy(w, c).start()

    def finish(ins, outs, sems):
        c, local, copy = copies(ins, outs, sems)
        for w in range(n):
            copy(w, 1 - c).wait_recv()
        for w in range(n):
            copy(w, c).wait_send()
        for cp in local:
            cp.wait()

    return _Carry(halves, [jax.ShapeDtypeStruct((2 * h.shape[0], h.shape[1]), h.dtype) for h in halves], n, start, finish)


def _pair_sum(g, part, idx, name):
    _, h, C = part.shape
    tr = min(128, h)
    nb = h // tr

    def body(idx_ref, g_ref, p_ref, o16_ref, own_ref):
        v = g_ref[...] + p_ref[...]
        o16_ref[...] = v.astype(BF16)

        @pl.when(pl.program_id(1) == idx_ref[1])
        def _():
            own_ref[...] = v

    return pl.pallas_call(
        body, name=name,
        grid_spec=pltpu.PrefetchScalarGridSpec(
            num_scalar_prefetch=1, grid=(nb, N_CHIPS),
            in_specs=[pl.BlockSpec((None, tr, C), lambda i, s, idx_ref: (s, idx_ref[0] * nb + i, 0)),
                      pl.BlockSpec((None, tr, C), lambda i, s, idx_ref: (s, i, 0))],
            out_specs=[pl.BlockSpec((None, tr, C), lambda i, s, idx_ref: (s, i, 0)),
                       pl.BlockSpec((tr, C), lambda i, s, idx_ref: (i, 0))]),
        out_shape=[jax.ShapeDtypeStruct(part.shape, BF16), jax.ShapeDtypeStruct((h, C), F32)],
        compiler_params=pltpu.CompilerParams(dimension_semantics=("arbitrary", "arbitrary"), vmem_limit_bytes=VMEM_LIMIT),
    )(idx, g, part)


def _chip_sum(own, slots, idx, name):
    h, C = own.shape
    tr = min(128, h)

    def body(idx_ref, own_ref, s_ref, o_ref):
        chip = idx_ref[1]
        tot = None
        for s in range(N_CHIPS):
            term = jnp.where(chip == s, own_ref[...], s_ref[s].astype(F32))
            tot = term if tot is None else tot + term
        o_ref[...] = tot

    return pl.pallas_call(
        body, name=name,
        grid_spec=pltpu.PrefetchScalarGridSpec(
            num_scalar_prefetch=1, grid=(h // tr,),
            in_specs=[pl.BlockSpec((tr, C), lambda i, idx_ref: (i, 0)),
                      pl.BlockSpec((N_CHIPS, tr, C), lambda i, idx_ref: (0, i, 0))],
            out_specs=pl.BlockSpec((tr, C), lambda i, idx_ref: (i, 0))),
        out_shape=jax.ShapeDtypeStruct((h, C), F32),
        compiler_params=pltpu.CompilerParams(dimension_semantics=("parallel",), vmem_limit_bytes=VMEM_LIMIT),
    )(idx, own, slots)


class _Reducer:
    def __init__(self, idx):
        self.idx, self.chips, self.p16, self.own, self.mine, self.final = idx, {}, {}, {}, {}, {}

    def add(self, name, whole):
        self.chips[name] = _chips_from_whole(name, whole)

    def pair(self, names):
        return _pair_exchange_carry([self.chips[n] for n in names])

    def take_pair(self, names, outs):
        for n, part in zip(names, outs):
            self.p16[n], self.own[n] = _pair_sum(self.chips.pop(n), part, self.idx, "pair_sum_" + n)

    def chip(self, names):
        return _chip_exchange_carry([self.p16[n] for n in names])

    def take_chip(self, names, outs):
        for n, slots in zip(names, outs):
            del self.p16[n]
            self.mine[n] = _chip_sum(self.own.pop(n), slots, self.idx, "chip_sum_" + n)

    def share(self, names):
        return _pair_share_carry([self.mine[n] for n in names])

    def take_share(self, names, outs):
        for n, s in zip(names, outs):
            del self.mine[n]
            self.final[n] = s


def _w_ada_grad(c8, dmod_cols):
    n_cols = dmod_cols.shape[1]
    tn = 512

    def body(c_ref, d_ref, o_ref):
        cv = c_ref[...]
        o_ref[...] = _tn(cv * _sigmoid(cv), d_ref[...], precision=HIGH)

    return _call(body, [c8, dmod_cols], name="w_ada_grad", grid=(n_cols // tn,),
                 in_specs=[pl.BlockSpec((8, D), lambda j: (0, 0)), pl.BlockSpec((8, tn), lambda j: (0, j))],
                 out_specs=[pl.BlockSpec((D, tn), lambda j: (0, j))],
                 out_shape=[jax.ShapeDtypeStruct((D, n_cols), F32)], sem=("parallel",))[0]


_SMALL_SEGS = (("dmod", 6144), ("norm_mix_w", 1024), ("conv_b", 3072), ("ssd_norm_w", 2048), ("pool_scale", 1024),
               ("norm_mlp_w", 1024), ("norm_final_w", 1024), ("conv_w", 4 * XBC), ("d_skip", 2048), ("a_log", 128),
               ("dt_bias", 128), ("loss", 128))
SMALL_OFF = {}
_o = 0
for _n, _s in _SMALL_SEGS:
    SMALL_OFF[_n] = _o
    _o += _s
SMALL_LEN = -(-_o // 1024) * 1024

_FIRST = ("w_in", "conv_w")
_LATER = ("w_branch_ssd", "pool_w", "w_branch_pool", "w_out", "w_up", "w_down")
_SMALL_REPLICATED = ("b_ada", "norm_mix_w", "conv_b", "dt_bias", "a_log", "d_skip", "ssd_norm_w", "pool_scale",
                     "norm_mlp_w", "norm_final_w")
_WEIGHTS = ("w_ada", "b_ada", "norm_mix_w", "w_in", "conv_w", "conv_b", "dt_bias", "a_log", "d_skip", "ssd_norm_w",
            "w_branch_ssd", "pool_w", "pool_scale", "w_branch_pool", "w_out", "norm_mlp_w", "w_up", "w_down",
            "norm_final_w")


def _shard_2d(name, a):
    if name == "conv_w":
        return a.reshape(16, -1)
    return (a.reshape(GW, GW) if name == "pool_w" else a.reshape(a.shape[-2], a.shape[-1])).astype(BF16)


def _whole_from_chips(name, g):
    if name == "w_in":
        return _perm_cols(jnp.transpose(g, (1, 0, 2)).reshape(D, IN_COLS))
    if name == "w_up":
        return jnp.transpose(g, (1, 0, 2)).reshape(D, DFF)
    if name == "pool_w":
        return jnp.transpose(g.reshape(N_CHIPS, 4, GW // N_CHIPS, GW), (1, 0, 2, 3)).reshape(4, GW, GW)
    if name == "conv_w":
        return jnp.transpose(g.reshape(N_CHIPS, 4, XBC // N_CHIPS), (1, 0, 2)).reshape(4, XBC)
    return g.reshape(N_CHIPS * g.shape[1], g.shape[2])


def _chips_from_whole(name, g):
    if name.startswith("w_in"):
        return jnp.transpose(_unperm_cols(g).reshape(g.shape[0], N_CHIPS, IN_COLS // N_CHIPS), (1, 0, 2))
    if name == "w_up":
        return jnp.transpose(g.reshape(D, N_CHIPS, DFF // N_CHIPS), (1, 0, 2))
    if name == "pool_w":
        return jnp.transpose(g.reshape(4, N_CHIPS, GW // N_CHIPS, GW), (1, 0, 2, 3)).reshape(N_CHIPS, GW, GW)
    return g.reshape(N_CHIPS, g.shape[0] // N_CHIPS, g.shape[1])


def kernel(x, c, w_ada, b_ada, norm_mix_w, w_in, conv_w, conv_b, dt_bias, a_log, d_skip, ssd_norm_w, w_branch_ssd, pool_w, pool_scale, w_branch_pool, w_out, norm_mlp_w, w_up, w_down, norm_final_w, loss_target, m_w_ada, m_b_ada, m_norm_mix_w, m_w_in, m_conv_w, m_conv_b, m_dt_bias, m_a_log, m_d_skip, m_ssd_norm_w, m_w_branch_ssd, m_pool_w, m_pool_scale, m_w_branch_pool, m_w_out, m_norm_mlp_w, m_w_up, m_w_down, m_norm_final_w, v_w_ada, v_b_ada, v_norm_mix_w, v_w_in, v_conv_w, v_conv_b, v_dt_bias, v_a_log, v_d_skip, v_ssd_norm_w, v_w_branch_ssd, v_pool_w, v_pool_scale, v_w_branch_pool, v_w_out, v_norm_mlp_w, v_w_up, v_w_down, v_norm_final_w):
    args = locals()
    w = {n: args[n] for n in _WEIGHTS}
    m = {n: args["m_" + n] for n in _WEIGHTS}
    v = {n: args["v_" + n] for n in _WEIGHTS}
    xi, yi, ci = _me()
    chip = 2 * xi + yi
    idx = jnp.stack([ci, chip]).astype(jnp.int32)
    ada_cols = w_ada.shape[-1]
    xs, target = x[0], loss_target[0]
    two_d = lambda n, a: a.reshape(GW, GW) if n == "pool_w" else a.reshape(-1, a.shape[-1])
    delta, new_m, new_v, g = {}, {}, {}, {}

    def adamw(n, carry=None):
        res = _adamw(two_d(n, w[n]), two_d(n, g[n]), two_d(n, m[n]), two_d(n, v[n]), "adamw_" + n, carry=carry)
        (delta[n], new_m[n], new_v[n]), extra = res if carry is not None else (res, None)
        return extra

    b_mine = lax.dynamic_slice(b_ada, (0, chip * ada_cols), (1, ada_cols))
    mod, c8 = _ada_fwd(c, w_ada[0], b_mine)
    c8 = c8[:, 0, :]
    shift_m, scale_m, gate_m, shift_f, scale_f, gate_f = [mod[:, D * i:D * (i + 1)] for i in range(6)]
    first = _run_carry(_gather_carry([_shard_2d(n, w[n]) for n in _FIRST]), "gather_first")
    p = {n: _whole_from_chips(n, a) for n, a in zip(_FIRST, first)}
    nf_w = norm_final_w.reshape(1, D)

    h1 = _norm_mod(xs, norm_mix_w, scale_m, shift_m, "norm_mod_mix")
    (proj,), later = _matmul(h1, p["w_in"], mode="nn", out_dtypes=[F32], name="mm_proj",
                             carry=_gather_carry([_shard_2d(n, w[n]) for n in _LATER]))
    p.update({n: _whole_from_chips(n, a) for n, a in zip(_LATER, later)})
    xbc_a = _conv_fwd(proj, p["conv_w"], conv_b)
    dtb_c, alog_c = dt_bias.reshape(HEADS, 1), a_log.reshape(HEADS, 1)
    dsk_exp = jnp.repeat(d_skip, HEAD_DIM, axis=1)
    y, hin = _ssd_fwd(xbc_a, proj, dt_bias, a_log, dtb_c, alog_c, dsk_exp)
    yn = _gate_norm(y, proj, ssd_norm_w)
    (y_ssd,) = _matmul(yn, p["w_branch_ssd"], mode="nn", out_dtypes=[F32], name="mm_branch_ssd")
    pooled, pw_out, yps = _pool_fwd(proj, p["pool_w"], pool_scale)
    (y_pool,) = _matmul(yps, p["w_branch_pool"], mode="nn", out_dtypes=[F32], name="mm_branch_pool")
    merged = _merge(proj, y_ssd, y_pool)
    resid = lambda acc, r, gt: (r + gt * acc, acc)
    x2, mix = _matmul(merged, p["w_out"], mode="nn", out_dtypes=[F32, F32], name="mm_out",
                      epi=resid, tile_extras=(xs,), row_extras=(gate_m,))
    h2 = _norm_mod(x2, norm_mlp_w, scale_f, shift_f, "norm_mod_mlp")
    relu2 = lambda acc: (acc, jnp.square(jnp.maximum(acc, 0.0)))
    up, act = _matmul(h2, p["w_up"], mode="nn", out_dtypes=[F32, BF16], name="mm_up", epi=relu2)
    x3, down = _matmul(act, p["w_down"], mode="nn", out_dtypes=[F32, F32], name="mm_down",
                       epi=resid, tile_extras=(x2,), row_extras=(gate_f,))

    red = _Reducer(idx)
    dx3, d_down, sums_f = _final_loss_bwd(x3, target, nf_w, down, gate_f)
    drelu2 = lambda acc, u: (acc * (2.0 * jnp.maximum(u, 0.0)),)
    (dup,) = _matmul(d_down, p["w_down"], mode="nt", out_dtypes=[BF16], name="mm_dact",
                     epi=drelu2, tile_extras=(up,))
    red.add("w_down", _matmul(act, d_down, mode="tn", out_dtypes=[F32], name="mm_g_down")[0])
    (dh2,), got = _matmul(dup, p["w_up"], mode="nt", out_dtypes=[F32], name="mm_dh2",
                          carry=red.pair(["w_down"]))
    red.take_pair(["w_down"], got)
    red.add("w_up", _matmul(h2, dup, mode="tn", out_dtypes=[F32], name="mm_g_up")[0])
    dx2, sums_2, dmix = _norm_mod_bwd(x2, dh2, dx3, norm_mlp_w, scale_f, "norm_mod_mlp_bwd", branch=mix, gate=gate_m)
    (dmerged,), got = _matmul(dmix, p["w_out"], mode="nt", out_dtypes=[F32], name="mm_dmerged",
                              carry=red.pair(["w_up"]))
    red.take_pair(["w_up"], got)
    red.add("w_out", _matmul(merged, dmix, mode="tn", out_dtypes=[F32], name="mm_g_out")[0])
    dy_ssd, dy_pool, dproj = _merge_bwd(dmerged, proj, y_ssd, y_pool)
    (dyp,), got = _matmul(dy_pool, p["w_branch_pool"], mode="nt", out_dtypes=[F32], name="mm_dyp",
                          carry=red.pair(["w_out"]))
    red.take_pair(["w_out"], got)
    red.add("w_branch_pool", _matmul(yps, dy_pool, mode="tn", out_dtypes=[F32], name="mm_g_bpool")[0])
    dproj, g_pool_w, sums_pool = _pool_bwd(dyp, pw_out, pooled, p["pool_w"], pool_scale, dproj)
    red.add("pool_w", g_pool_w)
    red.add("w_branch_ssd", _matmul(yn, dy_ssd, mode="tn", out_dtypes=[F32], name="mm_g_bssd")[0])
    mixers = ["w_branch_pool", "pool_w", "w_branch_ssd"]
    (dyn,), got = _matmul(dy_ssd, p["w_branch_ssd"], mode="nt", out_dtypes=[F32], name="mm_dyn",
                          carry=red.pair(mixers))
    red.take_pair(mixers, got)
    dy, dproj, sums_gn = _gate_norm_bwd(dyn, y, proj, ssd_norm_w, dproj)
    six = ["w_down", "w_up", "w_out"] + mixers
    (dxa, dproj, dsk_sum, ssd_small), got = _ssd_bwd(dy, xbc_a, proj, hin, dt_bias, a_log, dtb_c, alog_c, dsk_exp,
                                                     dproj, carry=red.chip(six))
    red.take_chip(six, got)
    dxc, sums_conv = _conv_bwd_a(dxa, proj, p["conv_w"], conv_b)
    dproj = _conv_bwd_b(dxc, p["conv_w"], dproj)
    half = D // 2
    (g_in_a,), got = _matmul(h1[:, :half], dproj, mode="tn", out_dtypes=[F32], name="mm_g_in_a",
                             carry=red.share(six))
    red.take_share(six, got)
    red.add("w_in_a", g_in_a)
    (g_in_b,), got = _matmul(h1[:, half:], dproj, mode="tn", out_dtypes=[F32], name="mm_g_in_b",
                             carry=red.pair(["w_in_a"]))
    red.take_pair(["w_in_a"], got)
    red.add("w_in_b", g_in_b)
    (dh1,), got = _matmul(dproj, p["w_in"], mode="nt", out_dtypes=[F32], name="mm_dh1",
                          carry=_join(red.chip(["w_in_a"]), red.pair(["w_in_b"])))
    red.take_chip(["w_in_a"], got[:1])
    red.take_pair(["w_in_b"], got[1:])
    (grad_x, sums_1), got = _norm_mod_bwd(xs, dh1, dx2, norm_mix_w, scale_m, "norm_mod_mix_bwd",
                                          carry=_join(red.chip(["w_in_b"]), red.share(["w_in_a"])))
    red.take_chip(["w_in_b"], got[:1])
    red.take_share(["w_in_a"], got[1:])

    dmod = jnp.concatenate([sums_1[0:1], sums_1[1:2], sums_2[3:4], sums_2[0:1], sums_2[1:2], sums_f[1:2]], axis=1)
    pad96 = jnp.zeros((1, 96), F32)
    small = {"dmod": dmod, "norm_mix_w": sums_1[2:3], "conv_b": sums_conv[4:5], "ssd_norm_w": sums_gn[0:1],
             "pool_scale": sums_pool[0:1], "norm_mlp_w": sums_2[2:3], "norm_final_w": sums_f[0:1],
             "conv_w": sums_conv[0:4].reshape(1, 4 * XBC), "d_skip": dsk_sum[0:1],
             "a_log": jnp.concatenate([ssd_small[0:1], pad96], axis=1),
             "dt_bias": jnp.concatenate([ssd_small[1:2], pad96], axis=1), "loss": sums_f[3:4, 0:128]}
    vec = jnp.concatenate([small[n] for n, _ in _SMALL_SEGS], axis=1)
    vec = jnp.pad(vec, ((0, 0), (0, SMALL_LEN - vec.shape[1]))).reshape(SMALL_LEN // 128, 128)
    every, total, dsk = _gather_small(vec)
    total = total.reshape(1, SMALL_LEN)
    seg = lambda n, size: total[:, SMALL_OFF[n]:SMALL_OFF[n] + size]
    g.update({"b_ada": seg("dmod", 6 * D), "norm_mix_w": seg("norm_mix_w", D), "conv_b": seg("conv_b", XBC),
              "dt_bias": seg("dt_bias", HEADS), "a_log": seg("a_log", HEADS), "d_skip": dsk[:, 0:2].reshape(1, HEADS),
              "ssd_norm_w": seg("ssd_norm_w", DI), "pool_scale": seg("pool_scale", D),
              "norm_mlp_w": seg("norm_mlp_w", D), "norm_final_w": seg("norm_final_w", D)})
    loss = total[0, SMALL_OFF["loss"]]
    conv_cols = conv_w.shape[-1]
    g["conv_w"] = lax.dynamic_slice(seg("conv_w", 4 * XBC).reshape(4, XBC), (0, chip * conv_cols), (4, conv_cols))
    dmod8 = every.reshape(8, SMALL_LEN)[:, SMALL_OFF["dmod"]:SMALL_OFF["dmod"] + 6 * D]
    g["w_ada"] = _w_ada_grad(c8, lax.dynamic_slice(dmod8, (0, chip * ada_cols), (8, ada_cols)))

    got = adamw("w_ada", carry=red.share(["w_in_b"]))
    red.take_share(["w_in_b"], got)
    for n in six:
        g[n] = red.final[n]
    g["w_in"] = jnp.concatenate([red.final["w_in_a"], red.final["w_in_b"]], axis=0)
    for n in ["conv_w", "w_in"] + six:
        adamw(n)
    sizes = [w[n].size for n in _SMALL_REPLICATED]
    n_small = -(-sum(sizes) // 1024) * 1024
    pack = lambda d: jnp.pad(jnp.concatenate([d[n].reshape(1, -1) for n in _SMALL_REPLICATED], axis=1),
                             ((0, 0), (0, n_small - sum(sizes)))).reshape(n_small // 128, 128)
    d_, m_, v_ = _adamw(pack(w), pack(g), pack(m), pack(v), "adamw_small")
    off = 0
    for n, s in zip(_SMALL_REPLICATED, sizes):
        for dst, src in ((delta, d_), (new_m, m_), (new_v, v_)):
            dst[n] = src.reshape(1, n_small)[:, off:off + s]
        off += s

    out = [loss, grad_x.reshape(x.shape)]
    for d in (g, delta, new_m, new_v):
        out += [d[n].reshape(w[n].shape) for n in _WEIGHTS]
    return tuple(out)
```

```python
import functools
import operator

import jax
import jax.numpy as jnp
from jax import lax
from jax.experimental import pallas as pl
from jax.experimental.pallas import tpu as pltpu

F32, BF16 = jnp.float32, jnp.bfloat16
HIGH = lax.Precision.HIGHEST
MESH = pl.DeviceIdType.MESH

D = 1024
DI = 2048
HEADS, HEAD_DIM = 32, 64
GROUPS, STATE = 4, 128
Q = 128
XBC = DI + 2 * GROUPS * STATE
POOL_WINDOWS = (2, 4, 8, 16)
GW = 256
DFF = 4096
EPS = 1e-5
IN_COLS = 8224
OFF_Z, OFF_XBC, OFF_POOL, OFF_GATE, OFF_DT, NP = 0, 2048, 5120, 6144, 8192, 8448
N_CHIPS = 4
ADAM_LR, ADAM_B1, ADAM_B2, ADAM_EPS, ADAM_WD, ADAM_STEP = 0.001, 0.9, 0.999, 1e-08, 0.01, 10
VMEM_LIMIT = 56 * 2 ** 20
NEG = -1e30


def _sigmoid(v):
    return 0.5 * jnp.tanh(0.5 * v) + 0.5


def _softplus(v):
    return jnp.maximum(v, 0.0) + jnp.log1p(jnp.exp(-jnp.abs(v)))


def _dot(a, b, dims, **kw):
    return lax.dot_general(a, b, (dims, ((), ())), preferred_element_type=F32, **kw)


def _nn(a, b, **kw):
    return _dot(a, b, ((1,), (0,)), **kw)


def _nt(a, b, **kw):
    return _dot(a, b, ((1,), (1,)), **kw)


def _tn(a, b, **kw):
    return _dot(a, b, ((0,), (0,)), **kw)


def _perm_cols(w):
    pad = jnp.zeros(w.shape[:-1] + (NP - IN_COLS,), w.dtype)
    return jnp.concatenate([w[..., :5120], w[..., 5152:], w[..., 5120:5152], pad], axis=-1)


def _unperm_cols(g):
    return jnp.concatenate([g[..., :5120], g[..., OFF_DT:OFF_DT + 32], g[..., 5120:OFF_DT]], axis=-1)


class _Sems:
    def __init__(self, send, recv, local, base=0):
        self._send, self._recv, self._local, self._base = send, recv, local, base

    def shift(self, n):
        return _Sems(self._send, self._recv, self._local, self._base + n)

    def send(self, i):
        return self._send.at[self._base + i]

    def recv(self, i):
        return self._recv.at[self._base + i]

    def local(self, i):
        return self._local.at[self._base + i]


class _Carry:
    def __init__(self, ins, out_shapes, n_sems, start, finish, aliased=()):
        self.ins, self.out_shapes, self.n_sems, self.start, self.finish = list(ins), list(out_shapes), n_sems, start, finish
        self.aliased = list(aliased)


def _join(*carries):
    def run(which):
        def fn(ins, outs, sems):
            i = o = s = 0
            for cy in carries:
                getattr(cy, which)(ins[i:i + len(cy.ins)], outs[o:o + len(cy.out_shapes)], sems.shift(s))
                i, o, s = i + len(cy.ins), o + len(cy.out_shapes), s + cy.n_sems
        return fn

    aliased, i, o = [], 0, 0
    for cy in carries:
        aliased += [(i + a, o + b) for a, b in cy.aliased]
        i, o = i + len(cy.ins), o + len(cy.out_shapes)
    return _Carry([a for cy in carries for a in cy.ins], [a for cy in carries for a in cy.out_shapes],
                  sum(cy.n_sems for cy in carries), run("start"), run("finish"), aliased)


def _call(body, args, *, name, grid=(), in_specs, out_specs, out_shape, scratch_shapes=(), sem=None, aliases=None,
          carry=None):
    in_specs, out_specs, out_shape, scratch_shapes = list(in_specs), list(out_specs), list(out_shape), list(scratch_shapes)
    n_in, n_out, n_scr = len(in_specs), len(out_specs), len(scratch_shapes)
    kw = {"vmem_limit_bytes": VMEM_LIMIT}
    if carry is None:
        kernel_fn = functools.partial(body)
        if sem is not None:
            kw["dimension_semantics"] = sem
    else:
        n_ci, n_co = len(carry.ins), len(carry.out_shapes)
        hbm = pl.BlockSpec(memory_space=pl.ANY)
        in_specs += [hbm] * n_ci
        out_specs += [hbm] * n_co
        out_shape += carry.out_shapes
        n_s = max(carry.n_sems, 1)
        scratch_shapes += [pltpu.SemaphoreType.DMA((n_s,))] * 3
        args = list(args) + carry.ins
        aliases = dict(aliases or {})
        aliases.update({n_in + i: n_out + o for i, o in carry.aliased})
        if grid:
            kw["dimension_semantics"] = ("arbitrary",) * len(grid)

        def kernel_fn(*refs):
            a = n_in
            ins, c_ins = refs[:a], refs[a:a + n_ci]
            a += n_ci
            outs, c_outs = refs[a:a + n_out], refs[a + n_out:a + n_out + n_co]
            a += n_out + n_co
            scr, sems = refs[a:a + n_scr], _Sems(*refs[a + n_scr:a + n_scr + 3])
            if grid:
                ids = [pl.program_id(d) for d in range(len(grid))]
                first = functools.reduce(operator.and_, [i == 0 for i in ids])
                last = functools.reduce(operator.and_, [i == g - 1 for i, g in zip(ids, grid)])

                @pl.when(first)
                def _():
                    carry.start(c_ins, c_outs, sems)

                body(*ins, *outs, *scr)

                @pl.when(last)
                def _():
                    carry.finish(c_ins, c_outs, sems)
            else:
                carry.start(c_ins, c_outs, sems)
                body(*ins, *outs, *scr)
                carry.finish(c_ins, c_outs, sems)

    outs = pl.pallas_call(
        kernel_fn, name=name, grid=grid, in_specs=in_specs, out_specs=out_specs, out_shape=out_shape,
        scratch_shapes=scratch_shapes, input_output_aliases=aliases or {},
        compiler_params=pltpu.CompilerParams(**kw),
    )(*args)
    outs = list(outs)
    return outs if carry is None else (outs[:n_out], outs[n_out:])


def _run_carry(carry, name):
    _, outs = _call(lambda: None, [], name=name, in_specs=[], out_specs=[], out_shape=[], carry=carry)
    return outs


_TILES = {
    "mm_proj": (1024, 2816, 1024), "mm_branch_ssd": (1024, 1024, 2048), "mm_branch_pool": (1024, 1024, 1024),
    "mm_out": (1024, 1024, 1024), "mm_up": (1024, 1024, 1024), "mm_down": (512, 1024, 4096),
    "mm_dact": (1024, 1024, 1024), "mm_g_down": (1024, 1024, 2048), "mm_dh2": (1024, 1024, 4096),
    "mm_g_up": (1024, 1024, 2048), "mm_dmerged": (1024, 1024, 1024), "mm_g_out": (1024, 1024, 2048),
    "mm_dyp": (1024, 1024, 1024), "mm_g_bpool": (1024, 1024, 2048), "mm_g_bssd": (1024, 1024, 2048),
    "mm_dyn": (1024, 1024, 1024), "mm_g_in_a": (768, 1408, 2048), "mm_g_in_b": (256, 2816, 2048),
    "mm_dh1": (1024, 1024, 2816),
}


def _matmul(a, b, *, mode, out_dtypes, name, epi=None, tile_extras=(), row_extras=(), carry=None, a_cols=None):
    M, K = (a.shape[1], a.shape[0]) if mode == "tn" else a.shape
    N = b.shape[0] if mode == "nt" else b.shape[1]
    a_start, M = a_cols if a_cols is not None else (0, M)
    tm, tn, tk = _TILES[name]
    tm, tn, tk = min(tm, M), min(tn, N), min(tk, K)
    assert M % tm == 0 and N % tn == 0 and K % tk == 0 and a_start % tm == 0, (name, M, N, K, tm, tn, tk)
    a_off = a_start // tm
    if mode == "nn":
        a_spec = pl.BlockSpec((tm, tk), lambda i, j, k: (i, k))
        b_spec = pl.BlockSpec((tk, tn), lambda i, j, k: (k, j))
        dims = ((1,), (0,))
    elif mode == "nt":
        a_spec = pl.BlockSpec((tm, tk), lambda i, j, k: (i, k))
        b_spec = pl.BlockSpec((tn, tk), lambda i, j, k: (j, k))
        dims = ((1,), (1,))
    else:
        a_spec = pl.BlockSpec((tk, tm), lambda i, j, k: (k, i + a_off))
        b_spec = pl.BlockSpec((tk, tn), lambda i, j, k: (k, j))
        dims = ((0,), (0,))
    nk = K // tk
    n_te, n_re, n_out = len(tile_extras), len(row_extras), len(out_dtypes)
    if epi is None:
        epi = lambda acc: (acc,)

    def body(a_ref, b_ref, *rest):
        extras = rest[:n_te + n_re]
        outs = rest[n_te + n_re:n_te + n_re + n_out]
        p = _dot(a_ref[...], b_ref[...], dims)

        def finish(acc):
            vals = epi(acc, *[e[...] for e in extras])
            for o, v in zip(outs, vals):
                o[...] = v.astype(o.dtype)

        if nk == 1:
            finish(p)
        else:
            acc_ref = rest[-1]
            k = pl.program_id(2)

            @pl.when(k == 0)
            def _():
                acc_ref[...] = p

            @pl.when(k > 0)
            def _():
                acc_ref[...] += p

            @pl.when(k == nk - 1)
            def _():
                finish(acc_ref[...])

    tile_spec = pl.BlockSpec((tm, tn), lambda i, j, k: (i, j))
    row_spec = pl.BlockSpec((1, tn), lambda i, j, k: (0, j))
    return _call(
        body, [a, b, *tile_extras, *row_extras], name=name, grid=(M // tm, N // tn, nk),
        in_specs=[a_spec, b_spec] + [tile_spec] * n_te + [row_spec] * n_re, out_specs=[tile_spec] * n_out,
        out_shape=[jax.ShapeDtypeStruct((M, N), dt) for dt in out_dtypes],
        scratch_shapes=[pltpu.VMEM((tm, tn), F32)] if nk > 1 else [],
        sem=("parallel", "parallel", "arbitrary"), carry=carry)


def _row_tile(T):
    return min(512, T)


def _norm_mod(x, nw, scale, shift, name, carry=None):
    T = x.shape[0]
    tr = _row_tile(T)

    def body(x_ref, nw_ref, sc_ref, sh_ref, o_ref):
        xv = x_ref[...]
        r = lax.rsqrt(jnp.mean(xv * xv, axis=-1, keepdims=True) + EPS)
        o_ref[...] = ((xv * r) * nw_ref[...] * (1.0 + sc_ref[...]) + sh_ref[...]).astype(BF16)

    tile = pl.BlockSpec((tr, D), lambda i: (i, 0))
    row = pl.BlockSpec((1, D), lambda i: (0, 0))
    res = _call(body, [x, nw, scale, shift], name=name, grid=(T // tr,), in_specs=[tile, row, row, row],
                out_specs=[tile], out_shape=[jax.ShapeDtypeStruct((T, D), BF16)], sem=("parallel",), carry=carry)
    return res[0] if carry is None else (res[0][0], res[1])


def _norm_mod_bwd(x, dh, dres, nw, scale, name, branch=None, gate=None, carry=None):
    T = x.shape[0]
    tr = _row_tile(T)
    with_branch = branch is not None

    def body(x_ref, dh_ref, dr_ref, nw_ref, sc_ref, *rest):
        if with_branch:
            br_ref, g_ref, dx_ref, sums_ref, db_ref = rest
        else:
            dx_ref, sums_ref = rest
        i = pl.program_id(0)

        @pl.when(i == 0)
        def _():
            sums_ref[...] = jnp.zeros_like(sums_ref)

        xv, dhv = x_ref[...], dh_ref[...]
        r = lax.rsqrt(jnp.mean(xv * xv, axis=-1, keepdims=True) + EPS)
        xn = xv * r
        g1 = dhv * (1.0 + sc_ref[...])
        dxn = g1 * nw_ref[...]
        dx = dr_ref[...] + r * (dxn - xn * jnp.mean(dxn * xn, axis=-1, keepdims=True))
        dx_ref[...] = dx
        sums_ref[0:1, :] += jnp.sum(dhv, axis=0, keepdims=True)
        sums_ref[1:2, :] += jnp.sum(dhv * (xn * nw_ref[...]), axis=0, keepdims=True)
        sums_ref[2:3, :] += jnp.sum(g1 * xn, axis=0, keepdims=True)
        if with_branch:
            db_ref[...] = (dx * g_ref[...]).astype(BF16)
            sums_ref[3:4, :] += jnp.sum(dx * br_ref[...], axis=0, keepdims=True)

    tile = pl.BlockSpec((tr, D), lambda i: (i, 0))
    row = pl.BlockSpec((1, D), lambda i: (0, 0))
    sums = pl.BlockSpec((8, D), lambda i: (0, 0))
    ins = [x, dh, dres, nw, scale] + ([branch, gate] if with_branch else [])
    in_specs = [tile, tile, tile, row, row] + ([tile, row] if with_branch else [])
    out_specs = [tile, sums] + ([tile] if with_branch else [])
    out_shape = [jax.ShapeDtypeStruct((T, D), F32), jax.ShapeDtypeStruct((8, D), F32)]
    if with_branch:
        out_shape.append(jax.ShapeDtypeStruct((T, D), BF16))
    return _call(body, ins, name=name, grid=(T // tr,), in_specs=in_specs, out_specs=out_specs, out_shape=out_shape,
                 sem=("arbitrary",), carry=carry)


def _final_loss_bwd(x3, target, wf, down, gate_f):
    T = x3.shape[0]
    tr = _row_tile(T)
    n_steps = T // tr

    def body(x_ref, t_ref, w_ref, dn_ref, g_ref, dx_ref, dd_ref, sums_ref):
        i = pl.program_id(0)

        @pl.when(i == 0)
        def _():
            sums_ref[...] = jnp.zeros_like(sums_ref)

        xv = x_ref[...]
        r = lax.rsqrt(jnp.mean(xv * xv, axis=-1, keepdims=True) + EPS)
        xn = xv * r
        err = xn * w_ref[...] - t_ref[...]
        dy = err * (1.0 / D)
        dxn = dy * w_ref[...]
        dx = r * (dxn - xn * jnp.mean(dxn * xn, axis=-1, keepdims=True))
        dx_ref[...] = dx
        dd_ref[...] = (dx * g_ref[...]).astype(BF16)
        sums_ref[0:1, :] += jnp.sum(dy * xn, axis=0, keepdims=True)
        sums_ref[1:2, :] += jnp.sum(dx * dn_ref[...], axis=0, keepdims=True)
        sums_ref[2:3, :] += jnp.sum(err * err, axis=0, keepdims=True) * (0.5 / D)

        @pl.when(i == n_steps - 1)
        def _():
            sums_ref[3:4, :] = jnp.broadcast_to(jnp.sum(sums_ref[2:3, :], axis=1, keepdims=True), (1, D))

    tile = pl.BlockSpec((tr, D), lambda i: (i, 0))
    row = pl.BlockSpec((1, D), lambda i: (0, 0))
    sums = pl.BlockSpec((8, D), lambda i: (0, 0))
    return _call(body, [x3, target, wf, down, gate_f], name="final_loss_bwd", grid=(n_steps,),
                 in_specs=[tile, tile, row, tile, row], out_specs=[tile, tile, sums],
                 out_shape=[jax.ShapeDtypeStruct((T, D), F32), jax.ShapeDtypeStruct((T, D), BF16),
                            jax.ShapeDtypeStruct((8, D), F32)], sem=("arbitrary",))


CONV_TC = 512


def _conv_taps(xp, w, b):
    acc = b + w[3:4, :] * xp
    for k in range(3):
        acc = acc + w[k:k + 1, :] * pltpu.roll(xp, 3 - k, 0)
    return acc


def _conv_fwd(proj, conv_w, conv_b):
    T = proj.shape[0]
    tr = _row_tile(T)
    nb, offb = tr // 8, OFF_XBC // CONV_TC

    def body(x_ref, h_ref, w_ref, b_ref, o_ref):
        halo = jnp.where(pl.program_id(0) > 0, h_ref[...], 0.0)
        xp = jnp.concatenate([halo, x_ref[...]], axis=0)
        acc = _conv_taps(xp, w_ref[...], b_ref[...])[8:]
        o_ref[...] = acc * _sigmoid(acc)

    return _call(
        body, [proj, proj, conv_w, conv_b], name="conv_fwd", grid=(T // tr, XBC // CONV_TC),
        in_specs=[pl.BlockSpec((tr, CONV_TC), lambda i, j: (i, j + offb)),
                  pl.BlockSpec((8, CONV_TC), lambda i, j: (jnp.maximum(i * nb - 1, 0), j + offb)),
                  pl.BlockSpec((4, CONV_TC), lambda i, j: (0, j)),
                  pl.BlockSpec((1, CONV_TC), lambda i, j: (0, j))],
        out_specs=[pl.BlockSpec((tr, CONV_TC), lambda i, j: (i, j))],
        out_shape=[jax.ShapeDtypeStruct((T, XBC), F32)], sem=("parallel", "parallel"))[0]


def _conv_bwd_a(dxa, proj, conv_w, conv_b):
    T = proj.shape[0]
    tr = _row_tile(T)
    nb, offb = tr // 8, OFF_XBC // CONV_TC

    def body(d_ref, x_ref, h_ref, w_ref, b_ref, o_ref, sums_ref):
        i = pl.program_id(1)

        @pl.when(i == 0)
        def _():
            sums_ref[...] = jnp.zeros_like(sums_ref)

        halo = jnp.where(i > 0, h_ref[...], 0.0)
        xp = jnp.concatenate([halo, x_ref[...]], axis=0)
        acc = _conv_taps(xp, w_ref[...], b_ref[...])[8:]
        s = _sigmoid(acc)
        dxc = d_ref[...] * (s * (1.0 + acc * (1.0 - s)))
        o_ref[...] = dxc
        sums_ref[3:4, :] += jnp.sum(dxc * x_ref[...], axis=0, keepdims=True)
        for k in range(3):
            sums_ref[k:k + 1, :] += jnp.sum(dxc * pltpu.roll(xp, 3 - k, 0)[8:], axis=0, keepdims=True)
        sums_ref[4:5, :] += jnp.sum(dxc, axis=0, keepdims=True)

    return _call(
        body, [dxa, proj, proj, conv_w, conv_b], name="conv_bwd_a", grid=(XBC // CONV_TC, T // tr),
        in_specs=[pl.BlockSpec((tr, CONV_TC), lambda j, i: (i, j)),
                  pl.BlockSpec((tr, CONV_TC), lambda j, i: (i, j + offb)),
                  pl.BlockSpec((8, CONV_TC), lambda j, i: (jnp.maximum(i * nb - 1, 0), j + offb)),
                  pl.BlockSpec((4, CONV_TC), lambda j, i: (0, j)),
                  pl.BlockSpec((1, CONV_TC), lambda j, i: (0, j))],
        out_specs=[pl.BlockSpec((tr, CONV_TC), lambda j, i: (i, j)), pl.BlockSpec((8, CONV_TC), lambda j, i: (0, j))],
        out_shape=[jax.ShapeDtypeStruct((T, XBC), F32), jax.ShapeDtypeStruct((8, XBC), F32)],
        sem=("parallel", "arbitrary"))


def _conv_bwd_b(dxc, conv_w, dproj):
    T = dxc.shape[0]
    tr = _row_tile(T)
    nb, offb, last = tr // 8, OFF_XBC // CONV_TC, T // tr - 1

    def body(d_ref, h_ref, w_ref, dp_in, o_ref):
        del dp_in
        halo = jnp.where(pl.program_id(0) < last, h_ref[...], 0.0)
        xp = jnp.concatenate([d_ref[...], halo], axis=0)
        n = xp.shape[0]
        w = w_ref[...]
        acc = w[3:4, :] * xp
        for k in range(3):
            acc = acc + w[k:k + 1, :] * pltpu.roll(xp, n - (3 - k), 0)
        o_ref[...] = acc[:tr].astype(BF16)

    return _call(
        body, [dxc, dxc, conv_w, dproj], name="conv_bwd_b", grid=(T // tr, XBC // CONV_TC),
        in_specs=[pl.BlockSpec((tr, CONV_TC), lambda i, j: (i, j)),
                  pl.BlockSpec((8, CONV_TC), lambda i, j: (jnp.minimum((i + 1) * nb, T // 8 - 1), j)),
                  pl.BlockSpec((4, CONV_TC), lambda i, j: (0, j)),
                  pl.BlockSpec(memory_space=pl.ANY)],
        out_specs=[pl.BlockSpec((tr, CONV_TC), lambda i, j: (i, j + offb))],
        out_shape=[jax.ShapeDtypeStruct(dproj.shape, BF16)], aliases={3: 0}, sem=("parallel", "parallel"))[0]


def _expand_pairs(v, j0, n_pairs, lo):
    R = v.shape[0]
    parts = []
    for j in range(j0, j0 + n_pairs):
        va = jnp.broadcast_to(v[:, 2 * j:2 * j + 1], (R, 128))
        vb = jnp.broadcast_to(v[:, 2 * j + 1:2 * j + 2], (R, 128))
        parts.append(jnp.where(lo, va, vb))
    return jnp.concatenate(parts, axis=1)


def _ssd_common(dtp_ref, dtb_r, alog_r, dtb_c, alog_c):
    rows = lax.broadcasted_iota(jnp.int32, (Q, Q), 0)
    cols = lax.broadcasted_iota(jnp.int32, (Q, Q), 1)
    causal = cols <= rows
    tri = causal.astype(F32)
    raw = dtp_ref[:, 0:HEADS] + dtb_r[...]
    dt = _softplus(raw)
    a_r = -jnp.exp(alog_r[...])
    cs = _nn(tri, dt * a_r, precision=HIGH)
    aT = _softplus(dtp_ref[...].T[0:HEADS, :] + dtb_c[...]) * (-jnp.exp(alog_c[...]))
    csT = _nt(aT, tri, precision=HIGH)
    return causal, raw, dt, a_r, cs, csT


def _ssd_fwd(xbc_a, proj, dtb_r, alog_r, dtb_c, alog_c, dsk_exp):
    T = xbc_a.shape[0]
    nc = T // Q

    def body(xbc_ref, dtp_ref, dtb_r_ref, alog_r_ref, dtb_c_ref, alog_c_ref, dsk_ref, y_ref, hin_ref, h_scr):
        @pl.when(pl.program_id(0) == 0)
        def _():
            h_scr[...] = jnp.zeros_like(h_scr)

        causal, _, dt, _, cs, csT = _ssd_common(dtp_ref, dtb_r_ref, alog_r_ref, dtb_c_ref, alog_c_ref)
        lo = lax.broadcasted_iota(jnp.int32, (1, 128), 1) < HEAD_DIM
        cs_last = cs[Q - 1:Q, :]
        ecs, dec, cd = jnp.exp(cs), jnp.exp(cs_last - cs), jnp.exp(cs_last)
        for g in range(GROUPS):
            gs = slice(512 * g, 512 * (g + 1))
            xs_g = xbc_ref[:, gs]
            b_g = xbc_ref[:, DI + STATE * g:DI + STATE * (g + 1)].astype(BF16)
            c_g = xbc_ref[:, DI + 512 + STATE * g:DI + 512 + STATE * (g + 1)].astype(BF16)
            xdt = xs_g * _expand_pairs(dt, 4 * g, 4, lo)
            xdt_b = xdt.astype(BF16)
            s_mat = _nt(c_g, b_g)
            ys = []
            for jj in range(4):
                xp = xdt_b[:, 128 * jj:128 * (jj + 1)]
                acc = None
                for h, sel in ((8 * g + 2 * jj, lo), (8 * g + 2 * jj + 1, jnp.logical_not(lo))):
                    l_mat = jnp.exp(jnp.where(causal, cs[:, h:h + 1] - csT[h:h + 1, :], NEG))
                    part = _nn((s_mat * l_mat).astype(BF16), jnp.where(sel, xp, jnp.zeros_like(xp)))
                    acc = part if acc is None else acc + part
                ys.append(acc)
            h_g = h_scr[128 * g:128 * (g + 1), :]
            hin_ref[0, 128 * g:128 * (g + 1), :] = h_g
            y_off = _nn(c_g, h_g.astype(BF16)) * _expand_pairs(ecs, 4 * g, 4, lo)
            y_ref[:, gs] = jnp.concatenate(ys, axis=1) + y_off + dsk_ref[:, gs] * xs_g
            xdec = (xdt * _expand_pairs(dec, 4 * g, 4, lo)).astype(BF16)
            h_scr[128 * g:128 * (g + 1), :] = h_g * _expand_pairs(cd, 4 * g, 4, lo) + _tn(b_g, xdec)

    small_r = pl.BlockSpec((1, HEADS), lambda c: (0, 0))
    small_c = pl.BlockSpec((HEADS, 1), lambda c: (0, 0))
    return _call(
        body, [xbc_a, proj, dtb_r, alog_r, dtb_c, alog_c, dsk_exp], name="ssd_fwd", grid=(nc,),
        in_specs=[pl.BlockSpec((Q, XBC), lambda c: (c, 0)),
                  pl.BlockSpec((Q, 128), lambda c: (c, OFF_DT // 128)),
                  small_r, small_r, small_c, small_c,
                  pl.BlockSpec((1, DI), lambda c: (0, 0))],
        out_specs=[pl.BlockSpec((Q, DI), lambda c: (c, 0)), pl.BlockSpec((1, 512, 512), lambda c: (c, 0, 0))],
        out_shape=[jax.ShapeDtypeStruct((T, DI), F32), jax.ShapeDtypeStruct((nc, 512, 512), F32)],
        scratch_shapes=[pltpu.VMEM((512, 512), F32)], sem=("arbitrary",))


def _ssd_bwd(dy, xbc_a, proj, hin, dtb_r, alog_r, dtb_c, alog_c, dsk_exp, dproj, carry=None):
    T = xbc_a.shape[0]
    nc = T // Q

    def body(dy_ref, xbc_ref, dtp_ref, hin_ref, dtb_r_ref, alog_r_ref, dtb_c_ref, alog_c_ref, dsk_ref, dp_in,
             dxa_ref, dp_ref, dsk_sum_ref, small_ref, dh_scr):
        del dp_in

        @pl.when(pl.program_id(0) == 0)
        def _():
            dh_scr[...] = jnp.zeros_like(dh_scr)
            dsk_sum_ref[...] = jnp.zeros_like(dsk_sum_ref)
            small_ref[...] = jnp.zeros_like(small_ref)

        causal, raw, dt, a_r, cs, csT = _ssd_common(dtp_ref, dtb_r_ref, alog_r_ref, dtb_c_ref, alog_c_ref)
        lane = lax.broadcasted_iota(jnp.int32, (1, 128), 1)
        lo = lane < HEAD_DIM
        hi = jnp.logical_not(lo)
        lane32 = lax.broadcasted_iota(jnp.int32, (1, HEADS), 1)
        sub32 = lax.broadcasted_iota(jnp.int32, (HEADS, 1), 0)
        cs_last = cs[Q - 1:Q, :]
        ecs, dec, cd = jnp.exp(cs), jnp.exp(cs_last - cs), jnp.exp(cs_last)
        dcs_c = jnp.zeros((Q, HEADS), F32)
        dcs_r = jnp.zeros((HEADS, Q), F32)
        dcs_l = jnp.zeros((1, HEADS), F32)
        ddt_x = jnp.zeros((Q, HEADS), F32)

        def put(vec, h, val):
            return vec + jnp.where(lane32 == h, val, 0.0)

        def halves(v):
            sa = jnp.sum(jnp.where(lo, v, 0.0), axis=1, keepdims=True)
            sb = jnp.sum(jnp.where(lo, 0.0, v), axis=1, keepdims=True)
            return sa, sb

        for g in range(GROUPS):
            gs = slice(512 * g, 512 * (g + 1))
            hs = slice(128 * g, 128 * (g + 1))
            xs_g = xbc_ref[:, gs]
            b_g = xbc_ref[:, DI + STATE * g:DI + STATE * (g + 1)].astype(BF16)
            c_g = xbc_ref[:, DI + 512 + STATE * g:DI + 512 + STATE * (g + 1)].astype(BF16)
            dt_g = _expand_pairs(dt, 4 * g, 4, lo)
            ecs_g = _expand_pairs(ecs, 4 * g, 4, lo)
            dec_g = _expand_pairs(dec, 4 * g, 4, lo)
            cd_g = _expand_pairs(cd, 4 * g, 4, lo)
            xdt = xs_g * dt_g
            xdt_b = xdt.astype(BF16)
            dy_g = dy_ref[:, gs]
            dy_b = dy_g.astype(BF16)
            s_mat = _nt(c_g, b_g)
            ds_mat = jnp.zeros((Q, Q), F32)
            dx_parts = []
            for jj in range(4):
                xp = xdt_b[:, 128 * jj:128 * (jj + 1)]
                dyp = dy_b[:, 128 * jj:128 * (jj + 1)]
                dxh = []
                for h, sel in ((8 * g + 2 * jj, lo), (8 * g + 2 * jj + 1, hi)):
                    l_mat = jnp.exp(jnp.where(causal, cs[:, h:h + 1] - csT[h:h + 1, :], NEG))
                    m_mat = s_mat * l_mat
                    dm = _nt(jnp.where(sel, dyp, jnp.zeros_like(dyp)), xp)
                    w_mat = dm * m_mat
                    dcs_c = put(dcs_c, h, jnp.sum(w_mat, axis=1, keepdims=True))
                    dcs_r = dcs_r + jnp.where(sub32 == h, jnp.sum(w_mat, axis=0, keepdims=True), 0.0)
                    ds_mat = ds_mat + dm * l_mat
                    dxh.append(_tn(m_mat.astype(BF16), dyp))
                dx_parts.append(jnp.where(lo, dxh[0], dxh[1]))
            hin_g = hin_ref[0, hs, :]
            hin_b = hin_g.astype(BF16)
            dh_g = dh_scr[hs, :]
            dh_b = dh_g.astype(BF16)
            y_off = _nn(c_g, hin_b) * ecs_g
            dz = (dy_g * ecs_g).astype(BF16)
            g_mat = _nn(b_g, dh_b)
            xdec = xdt * dec_g
            v1 = dy_g * y_off - xdec * g_mat
            v2 = jnp.sum(xdec * g_mat, axis=0, keepdims=True) + jnp.sum(dh_g * hin_g, axis=0, keepdims=True) * cd_g
            dxdt = jnp.concatenate(dx_parts, axis=1) + dec_g * g_mat
            v3 = dxdt * xs_g
            for jj in range(4):
                ps = slice(128 * jj, 128 * (jj + 1))
                ha = 8 * g + 2 * jj
                sa, sb = halves(v1[:, ps])
                dcs_c = put(put(dcs_c, ha, sa), ha + 1, sb)
                sa, sb = halves(v2[:, ps])
                dcs_l = put(put(dcs_l, ha, sa), ha + 1, sb)
                sa, sb = halves(v3[:, ps])
                ddt_x = put(put(ddt_x, ha, sa), ha + 1, sb)
            ds_b = ds_mat.astype(BF16)
            dxa_ref[:, gs] = dxdt * dt_g + dy_g * dsk_ref[:, gs]
            dxa_ref[:, DI + STATE * g:DI + STATE * (g + 1)] = _nt(xdec.astype(BF16), dh_b) + _tn(ds_b, c_g)
            dxa_ref[:, DI + 512 + STATE * g:DI + 512 + STATE * (g + 1)] = _nt(dz, hin_b) + _nn(ds_b, b_g)
            dh_scr[hs, :] = _tn(c_g, dz) + dh_g * cd_g
            dsk_sum_ref[0:1, gs] += jnp.sum(dy_g * xs_g, axis=0, keepdims=True)

        rows = lax.broadcasted_iota(jnp.int32, (Q, Q), 0)
        cols = lax.broadcasted_iota(jnp.int32, (Q, Q), 1)
        tri_t = (cols >= rows).astype(F32)
        last_row = lax.broadcasted_iota(jnp.int32, (Q, 1), 0) == Q - 1
        dcs = dcs_c + jnp.where(last_row, dcs_l, 0.0)
        da = _nn(tri_t, dcs, precision=HIGH) - _nt(tri_t, dcs_r, precision=HIGH)
        ddt_raw = (ddt_x + da * a_r) * _sigmoid(raw)
        small_ref[0:1, :] += jnp.sum(da * dt, axis=0, keepdims=True) * a_r
        small_ref[1:2, :] += jnp.sum(ddt_raw, axis=0, keepdims=True)
        dp_ref[...] = jnp.zeros_like(dp_ref)
        dp_ref[:, 0:HEADS] = ddt_raw.astype(BF16)

    rev = lambda c: nc - 1 - c
    small_r = pl.BlockSpec((1, HEADS), lambda c: (0, 0))
    small_c = pl.BlockSpec((HEADS, 1), lambda c: (0, 0))
    return _call(
        body, [dy, xbc_a, proj, hin, dtb_r, alog_r, dtb_c, alog_c, dsk_exp, dproj], name="ssd_bwd", grid=(nc,),
        in_specs=[pl.BlockSpec((Q, DI), lambda c: (rev(c), 0)),
                  pl.BlockSpec((Q, XBC), lambda c: (rev(c), 0)),
                  pl.BlockSpec((Q, 128), lambda c: (rev(c), OFF_DT // 128)),
                  pl.BlockSpec((1, 512, 512), lambda c: (rev(c), 0, 0)),
                  small_r, small_r, small_c, small_c,
                  pl.BlockSpec((1, DI), lambda c: (0, 0)),
                  pl.BlockSpec(memory_space=pl.ANY)],
        out_specs=[pl.BlockSpec((Q, XBC), lambda c: (rev(c), 0)),
                   pl.BlockSpec((Q, 256), lambda c: (rev(c), OFF_DT // 256)),
                   pl.BlockSpec((8, DI), lambda c: (0, 0)),
                   pl.BlockSpec((8, HEADS), lambda c: (0, 0))],
        out_shape=[jax.ShapeDtypeStruct((T, XBC), F32), jax.ShapeDtypeStruct(dproj.shape, BF16),
                   jax.ShapeDtypeStruct((8, DI), F32), jax.ShapeDtypeStruct((8, HEADS), F32)],
        aliases={9: 1}, scratch_shapes=[pltpu.VMEM((512, 512), F32)], sem=("arbitrary",), carry=carry)


def _gate_norm(y, proj, w):
    T = y.shape[0]
    tr = _row_tile(T)

    def body(y_ref, z_ref, w_ref, o_ref):
        for g in range(GROUPS):
            gs = slice(512 * g, 512 * (g + 1))
            z = z_ref[:, gs]
            yg = y_ref[:, gs] * (z * _sigmoid(z))
            r = lax.rsqrt(jnp.mean(yg * yg, axis=-1, keepdims=True) + EPS)
            o_ref[:, gs] = (yg * r * w_ref[:, gs]).astype(BF16)

    tile = pl.BlockSpec((tr, DI), lambda i: (i, 0))
    return _call(body, [y, proj, w], name="gate_norm", grid=(T // tr,),
                 in_specs=[tile, tile, pl.BlockSpec((1, DI), lambda i: (0, 0))], out_specs=[tile],
                 out_shape=[jax.ShapeDtypeStruct((T, DI), BF16)], sem=("parallel",))[0]


def _gate_norm_bwd(dyn, y, proj, w, dproj):
    T = y.shape[0]
    tr = _row_tile(T)

    def body(d_ref, y_ref, z_ref, w_ref, dp_in, dy_ref, dz_ref, sums_ref):
        del dp_in

        @pl.when(pl.program_id(0) == 0)
        def _():
            sums_ref[...] = jnp.zeros_like(sums_ref)

        for g in range(GROUPS):
            gs = slice(512 * g, 512 * (g + 1))
            z, yv, d = z_ref[:, gs], y_ref[:, gs], d_ref[:, gs]
            s = _sigmoid(z)
            silu = z * s
            yg = yv * silu
            r = lax.rsqrt(jnp.mean(yg * yg, axis=-1, keepdims=True) + EPS)
            yn = yg * r
            sums_ref[0:1, gs] += jnp.sum(d * yn, axis=0, keepdims=True)
            dn = d * w_ref[:, gs]
            dyg = r * (dn - yn * jnp.mean(dn * yn, axis=-1, keepdims=True))
            dy_ref[:, gs] = dyg * silu
            dz_ref[:, gs] = (dyg * yv * (s * (1.0 + z * (1.0 - s)))).astype(BF16)

    tile = pl.BlockSpec((tr, DI), lambda i: (i, 0))
    return _call(
        body, [dyn, y, proj, w, dproj], name="gate_norm_bwd", grid=(T // tr,),
        in_specs=[tile, tile, tile, pl.BlockSpec((1, DI), lambda i: (0, 0)), pl.BlockSpec(memory_space=pl.ANY)],
        out_specs=[tile, tile, pl.BlockSpec((8, DI), lambda i: (0, 0))],
        out_shape=[jax.ShapeDtypeStruct((T, DI), F32), jax.ShapeDtypeStruct(dproj.shape, BF16),
                   jax.ShapeDtypeStruct((8, DI), F32)],
        aliases={4: 1}, sem=("arbitrary",))


def _pool_fwd(proj, pool_w_b, pool_scale):
    T = proj.shape[0]
    tr = _row_tile(T)
    nb = tr // 16

    def body(u_ref, h_ref, pw_ref, ps_ref, pooled_ref, pw_out_ref, yps_ref):
        i = pl.program_id(0)
        t = i * tr + lax.broadcasted_iota(jnp.int32, (tr, 1), 0)
        for g, win in enumerate(POOL_WINDOWS):
            gs = slice(GW * g, GW * (g + 1))
            u = u_ref[:, gs]
            s = jnp.concatenate([jnp.where(i > 0, h_ref[:, gs], 0.0), u], axis=0)
            sh = 1
            while sh < win:
                s = s + pltpu.roll(s, sh, 0)
                sh *= 2
            pooled = (s[16:] * (1.0 / jnp.minimum(t + 1, win).astype(F32)) - u).astype(BF16)
            pooled_ref[:, gs] = pooled
            pwv = _nn(pooled, pw_ref[g])
            pw_out_ref[:, gs] = pwv
            yps_ref[:, gs] = (pwv * ps_ref[:, gs]).astype(BF16)

    tile = pl.BlockSpec((tr, D), lambda i: (i, 0))
    return _call(
        body, [proj, proj, pool_w_b, pool_scale], name="pool_fwd", grid=(T // tr,),
        in_specs=[pl.BlockSpec((tr, D), lambda i: (i, OFF_POOL // D)),
                  pl.BlockSpec((16, D), lambda i: (jnp.maximum(i * nb - 1, 0), OFF_POOL // D)),
                  pl.BlockSpec((4, GW, GW), lambda i: (0, 0, 0)),
                  pl.BlockSpec((1, D), lambda i: (0, 0))],
        out_specs=[tile, tile, tile],
        out_shape=[jax.ShapeDtypeStruct((T, D), BF16), jax.ShapeDtypeStruct((T, D), F32),
                   jax.ShapeDtypeStruct((T, D), BF16)], sem=("parallel",))


def _pool_bwd(dyp, pw_out, pooled, pool_w_b, pool_scale, dproj):
    T = dyp.shape[0]
    tr = _row_tile(T)
    nb, last = tr // 16, T // tr - 1

    def body(d_ref, h_ref, pwo_ref, pooled_ref, pw_ref, ps_ref, dp_in, du_ref, gpw_ref, sums_ref):
        del dp_in
        i = pl.program_id(0)

        @pl.when(i == 0)
        def _():
            gpw_ref[...] = jnp.zeros_like(gpw_ref)
            sums_ref[...] = jnp.zeros_like(sums_ref)

        n = tr + 16
        t = i * tr + lax.broadcasted_iota(jnp.int32, (n, 1), 0)
        sums_ref[0:1, :] += jnp.sum(d_ref[...] * pwo_ref[...], axis=0, keepdims=True)
        for g, win in enumerate(POOL_WINDOWS):
            gs = slice(GW * g, GW * (g + 1))
            d_ext = jnp.concatenate([d_ref[:, gs], jnp.where(i < last, h_ref[:, gs], 0.0)], axis=0)
            dpw = (d_ext * ps_ref[:, gs]).astype(BF16)
            dpooled = _nt(dpw, pw_ref[g])
            s = jnp.where(t < T, dpooled * (1.0 / jnp.minimum(t + 1, win).astype(F32)), 0.0)
            sh = 1
            while sh < win:
                s = s + pltpu.roll(s, n - sh, 0)
                sh *= 2
            du_ref[:, gs] = (s[:tr] - dpooled[:tr]).astype(BF16)
            gpw_ref[g] += _tn(pooled_ref[:, gs], dpw[:tr])

    tile = pl.BlockSpec((tr, D), lambda i: (i, 0))
    return _call(
        body, [dyp, dyp, pw_out, pooled, pool_w_b, pool_scale, dproj], name="pool_bwd", grid=(T // tr,),
        in_specs=[tile, pl.BlockSpec((16, D), lambda i: (jnp.minimum((i + 1) * nb, T // 16 - 1), 0)), tile, tile,
                  pl.BlockSpec((4, GW, GW), lambda i: (0, 0, 0)), pl.BlockSpec((1, D), lambda i: (0, 0)),
                  pl.BlockSpec(memory_space=pl.ANY)],
        out_specs=[pl.BlockSpec((tr, D), lambda i: (i, OFF_POOL // D)),
                   pl.BlockSpec((4, GW, GW), lambda i: (0, 0, 0)), pl.BlockSpec((8, D), lambda i: (0, 0))],
        out_shape=[jax.ShapeDtypeStruct(dproj.shape, BF16), jax.ShapeDtypeStruct((4, GW, GW), F32),
                   jax.ShapeDtypeStruct((8, D), F32)],
        aliases={6: 0}, sem=("arbitrary",))


def _merge(proj, y_ssd, y_pool):
    T = proj.shape[0]
    tr = _row_tile(T)

    def body(g_ref, a_ref, b_ref, o_ref):
        o_ref[...] = (_sigmoid(g_ref[:, 0:D]) * a_ref[...] + _sigmoid(g_ref[:, D:2 * D]) * b_ref[...]).astype(BF16)

    tile = pl.BlockSpec((tr, D), lambda i: (i, 0))
    return _call(body, [proj, y_ssd, y_pool], name="merge", grid=(T // tr,),
                 in_specs=[pl.BlockSpec((tr, 2 * D), lambda i: (i, OFF_GATE // (2 * D))), tile, tile], out_specs=[tile],
                 out_shape=[jax.ShapeDtypeStruct((T, D), BF16)], sem=("parallel",))[0]


def _merge_bwd(dmerged, proj, y_ssd, y_pool):
    T = proj.shape[0]
    tr = _row_tile(T)

    def body(d_ref, g_ref, a_ref, b_ref, da_ref, db_ref, dg_ref):
        d = d_ref[...]
        ga, gb = _sigmoid(g_ref[:, 0:D]), _sigmoid(g_ref[:, D:2 * D])
        da_ref[...] = (d * ga).astype(BF16)
        db_ref[...] = (d * gb).astype(BF16)
        dg_ref[:, 0:D] = (d * a_ref[...] * ga * (1.0 - ga)).astype(BF16)
        dg_ref[:, D:2 * D] = (d * b_ref[...] * gb * (1.0 - gb)).astype(BF16)

    tile = pl.BlockSpec((tr, D), lambda i: (i, 0))
    gates = pl.BlockSpec((tr, 2 * D), lambda i: (i, OFF_GATE // (2 * D)))
    return _call(body, [dmerged, proj, y_ssd, y_pool], name="merge_bwd", grid=(T // tr,),
                 in_specs=[tile, gates, tile, tile], out_specs=[tile, tile, gates],
                 out_shape=[jax.ShapeDtypeStruct((T, D), BF16), jax.ShapeDtypeStruct((T, D), BF16),
                            jax.ShapeDtypeStruct((T, NP), BF16)], sem=("parallel",))


def _adamw(w, g, m, v, name, carry=None):
    R, C = w.shape
    tr = R if R <= 128 else 128
    assert R % tr == 0

    def body(w_ref, g_ref, m_ref, v_ref, d_ref, mo_ref, vo_ref):
        gv = g_ref[...]
        mn = ADAM_B1 * m_ref[...] + (1.0 - ADAM_B1) * gv
        vn = ADAM_B2 * v_ref[...] + (1.0 - ADAM_B2) * (gv * gv)
        m_hat = mn / (1.0 - ADAM_B1 ** ADAM_STEP)
        v_hat = vn / (1.0 - ADAM_B2 ** ADAM_STEP)
        d_ref[...] = -ADAM_LR * (m_hat / (jnp.sqrt(v_hat) + ADAM_EPS) + ADAM_WD * w_ref[...])
        mo_ref[...] = mn
        vo_ref[...] = vn

    tile = pl.BlockSpec((tr, C), lambda i: (i, 0))
    sds = jax.ShapeDtypeStruct((R, C), F32)
    return _call(body, [w, g, m, v], name=name, grid=(R // tr,), in_specs=[tile] * 4, out_specs=[tile] * 3,
                 out_shape=[sds] * 3, sem=("parallel",), carry=carry)


def _me():
    return lax.axis_index("x"), lax.axis_index("y"), lax.axis_index("c")


def _xor_peer(x, y, c, p):
    return (x ^ ((p >> 2) & 1), y ^ ((p >> 1) & 1), c ^ (p & 1))


def _ada_fwd(c_row, w_ada, b_ada_mine):
    n_cols = w_ada.shape[1]

    def body(c_ref, w_ref, b_ref, mod_ref, c8_ref, csend, mpart, modbuf, send_sems, recv_sems):
        x, y, c = _me()
        me = 4 * x + 2 * y + c
        chip = 2 * x + y
        csend[...] = jnp.broadcast_to(c_ref[...], csend.shape)
        c8_ref[me] = csend[...]

        def c_copy(p):
            return pltpu.make_async_remote_copy(
                src_ref=csend, dst_ref=c8_ref.at[me], send_sem=send_sems.at[p - 1], recv_sem=recv_sems.at[p - 1],
                device_id=_xor_peer(x, y, c, p), device_id_type=MESH)

        for p in range(1, 8):
            c_copy(p).start()
        for p in range(1, 8):
            c_copy(p).wait_recv()
        cs = jnp.concatenate([c8_ref[d][0:1, :] for d in range(8)], axis=0)
        mpart[...] = _nn(cs * _sigmoid(cs), w_ref[...], precision=HIGH) + b_ref[...]
        modbuf[chip] = mpart[...]

        def m_copy(m):
            return pltpu.make_async_remote_copy(
                src_ref=mpart, dst_ref=modbuf.at[chip], send_sem=send_sems.at[6 + m], recv_sem=recv_sems.at[6 + m],
                device_id=_xor_peer(x, y, c, 2 * m), device_id_type=MESH)

        for m in range(1, 4):
            m_copy(m).start()
        for m in range(1, 4):
            m_copy(m).wait_recv()
        mine = lax.broadcasted_iota(jnp.int32, (8, 1), 0) == me
        for k in range(N_CHIPS):
            mod_ref[:, n_cols * k:n_cols * (k + 1)] = jnp.sum(jnp.where(mine, modbuf[k], 0.0), axis=0, keepdims=True)
        for p in range(1, 8):
            c_copy(p).wait_send()
        for m in range(1, 4):
            m_copy(m).wait_send()

    vmem = pl.BlockSpec(memory_space=pltpu.VMEM)
    return _call(
        body, [c_row, w_ada, b_ada_mine], name="ada_fwd", in_specs=[vmem, vmem, vmem], out_specs=[vmem, vmem],
        out_shape=[jax.ShapeDtypeStruct((1, N_CHIPS * n_cols), F32), jax.ShapeDtypeStruct((8, 8, D), F32)],
        scratch_shapes=[pltpu.VMEM((8, D), F32), pltpu.VMEM((8, n_cols), F32), pltpu.VMEM((N_CHIPS, 8, n_cols), F32),
                        pltpu.SemaphoreType.DMA((10,)), pltpu.SemaphoreType.DMA((10,))])


def _gather_small(vec):
    rows = vec.shape[0]

    def body(v_ref, all_ref, tot_ref, dsk_ref, send_sems, recv_sems):
        x, y, c = _me()
        me = 4 * x + 2 * y + c
        all_ref[me] = v_ref[...]

        def copy(p):
            return pltpu.make_async_remote_copy(
                src_ref=v_ref, dst_ref=all_ref.at[me], send_sem=send_sems.at[p - 1], recv_sem=recv_sems.at[p - 1],
                device_id=_xor_peer(x, y, c, p), device_id_type=MESH)

        for p in range(1, 8):
            copy(p).start()
        for p in range(1, 8):
            copy(p).wait_recv()
        tot = all_ref[0]
        for d in range(1, 8):
            tot = tot + all_ref[d]
        tot_ref[...] = tot
        seg = tot[SMALL_OFF["d_skip"] // 128:SMALL_OFF["d_skip"] // 128 + 16, :]
        lane = lax.broadcasted_iota(jnp.int32, (1, 128), 1)
        sa = jnp.sum(jnp.where(lane < HEAD_DIM, seg, 0.0), axis=1, keepdims=True)
        sb = jnp.sum(jnp.where(lane < HEAD_DIM, 0.0, seg), axis=1, keepdims=True)
        dsk_ref[...] = jnp.where(lane == 0, sa, jnp.where(lane == 1, sb, 0.0))
        for p in range(1, 8):
            copy(p).wait_send()

    vmem = pl.BlockSpec(memory_space=pltpu.VMEM)
    return _call(
        body, [vec], name="gather_small", in_specs=[vmem], out_specs=[vmem, vmem, vmem],
        out_shape=[jax.ShapeDtypeStruct((8, rows, 128), F32), jax.ShapeDtypeStruct((rows, 128), F32),
                   jax.ShapeDtypeStruct((16, 128), F32)],
        scratch_shapes=[pltpu.SemaphoreType.DMA((7,)), pltpu.SemaphoreType.DMA((7,))])


def _gather_carry(shards):
    n = len(shards)

    def copies(ins, outs, sems):
        x, y, c = _me()
        chip = 2 * x + y

        def half(w, which):
            h = shards[w].shape[0] // 2
            return pl.ds(which * h, h)

        def first(w, m):
            return pltpu.make_async_remote_copy(
                src_ref=ins[w].at[half(w, c)], dst_ref=outs[w].at[chip, half(w, c)],
                send_sem=sems.send(6 * w + m - 1), recv_sem=sems.recv(6 * w + m - 1),
                device_id=_xor_peer(x, y, c, 2 * m), device_id_type=MESH)

        def landed(w, m):
            return pltpu.make_async_remote_copy(
                src_ref=ins[w].at[half(w, c)], dst_ref=outs[w].at[chip ^ m, half(w, c)],
                send_sem=sems.send(6 * w + m - 1), recv_sem=sems.recv(6 * w + m - 1),
                device_id=_xor_peer(x, y, c, 2 * m), device_id_type=MESH)

        def passed(w, m, which):
            part = outs[w].at[chip ^ m, half(w, which)]
            return pltpu.make_async_remote_copy(
                src_ref=part, dst_ref=part, send_sem=sems.send(6 * w + 2 + m), recv_sem=sems.recv(6 * w + 2 + m),
                device_id=(x, y, 1 - c), device_id_type=MESH)

        return c, first, landed, passed

    pairs = [(w, m) for w in range(n) for m in range(1, 4)]

    def start(ins, outs, sems):
        _, first, _, _ = copies(ins, outs, sems)
        for w, m in pairs:
            first(w, m).start()

    def finish(ins, outs, sems):
        c, first, landed, passed = copies(ins, outs, sems)
        for w, m in pairs:
            landed(w, m).wait_recv()
            passed(w, m, c).start()
        for w, m in pairs:
            passed(w, m, 1 - c).wait_recv()
        for w, m in pairs:
            first(w, m).wait_send()
            passed(w, m, c).wait_send()

    return _Carry(shards, [jax.ShapeDtypeStruct((N_CHIPS,) + s.shape, s.dtype) for s in shards], 6 * n, start, finish)


def _pair_exchange_carry(grads):
    n = len(grads)

    def copy(ins, outs, sems, w):
        x, y, c = _me()
        h = grads[w].shape[1] // 2
        return pltpu.make_async_remote_copy(
            src_ref=ins[w].at[:, pl.ds((1 - c) * h, h)], dst_ref=outs[w],
            send_sem=sems.send(w), recv_sem=sems.recv(w), device_id=(x, y, 1 - c), device_id_type=MESH)

    def start(ins, outs, sems):
        for w in range(n):
            copy(ins, outs, sems, w).start()

    def finish(ins, outs, sems):
        for w in range(n):
            copy(ins, outs, sems, w).wait()

    return _Carry(grads, [jax.ShapeDtypeStruct((N_CHIPS, g.shape[1] // 2, g.shape[2]), g.dtype) for g in grads], n,
                  start, finish)


def _chip_exchange_carry(partials):
    n = len(partials)

    def copier(ins, outs, sems):
        x, y, c = _me()
        chip = 2 * x + y

        def copy(w, m, landed):
            return pltpu.make_async_remote_copy(
                src_ref=ins[w].at[chip ^ m], dst_ref=outs[w].at[(chip ^ m) if landed else chip],
                send_sem=sems.send(3 * w + m - 1), recv_sem=sems.recv(3 * w + m - 1),
                device_id=_xor_peer(x, y, c, 2 * m), device_id_type=MESH)

        return copy

    pairs = [(w, m) for w in range(n) for m in range(1, 4)]

    def start(ins, outs, sems):
        copy = copier(ins, outs, sems)
        for w, m in pairs:
            copy(w, m, False).start()

    def finish(ins, outs, sems):
        copy = copier(ins, outs, sems)
        for w, m in pairs:
            copy(w, m, True).wait_recv()
        for w, m in pairs:
            copy(w, m, False).wait_send()

    return _Carry(partials, [jax.ShapeDtypeStruct(p.shape, p.dtype) for p in partials], 3 * n, start, finish)


def _pair_share_carry(shards):
    n = len(shards)

    def copier(ins, outs, sems):
        x, y, c = _me()

        def copy(w, which):
            h = shards[w].shape[0] // 2
            rows = pl.ds(which * h, h)
            return pltpu.make_async_remote_copy(
                src_ref=ins[w].at[rows], dst_ref=outs[w].at[rows],
                send_sem=sems.send(w), recv_sem=sems.recv(w), device_id=(x, y, 1 - c), device_id_type=MESH)

        return c, copy

    def start(ins, outs, sems):
        c, copy = copier(ins, outs, sems)
        for w in range(n):
            copy(w, c).start()

    def finish(ins, outs, sems):
        c, copy = copier(ins, outs, sems)
        for w in range(n):
            copy(w, 1 - c).wait_recv()
        for w in range(n):
            copy(w, c).wait_send()

    return _Carry(shards, [jax.ShapeDtypeStruct(s.shape, s.dtype) for s in shards], n, start, finish,
                  aliased=[(w, w) for w in range(n)])


def _pair_sum(g, part, idx, name):
    _, h, C = part.shape
    tr = min(128, h)
    nb = h // tr

    def body(idx_ref, g_ref, p_ref, o16_ref, own_ref):
        v = g_ref[...] + p_ref[...]
        o16_ref[...] = v.astype(BF16)

        @pl.when(pl.program_id(1) == idx_ref[1])
        def _():
            own_ref[...] = v

    return pl.pallas_call(
        body, name=name,
        grid_spec=pltpu.PrefetchScalarGridSpec(
            num_scalar_prefetch=1, grid=(nb, N_CHIPS),
            in_specs=[pl.BlockSpec((None, tr, C), lambda i, s, idx_ref: (s, idx_ref[0] * nb + i, 0)),
                      pl.BlockSpec((None, tr, C), lambda i, s, idx_ref: (s, i, 0))],
            out_specs=[pl.BlockSpec((None, tr, C), lambda i, s, idx_ref: (s, i, 0)),
                       pl.BlockSpec((tr, C), lambda i, s, idx_ref: (i, 0))]),
        out_shape=[jax.ShapeDtypeStruct(part.shape, BF16), jax.ShapeDtypeStruct((h, C), F32)],
        compiler_params=pltpu.CompilerParams(dimension_semantics=("arbitrary", "arbitrary"), vmem_limit_bytes=VMEM_LIMIT),
    )(idx, g, part)


def _chip_sum(own, slots, idx, name):
    h, C = own.shape
    tr = min(128, h)
    nb = h // tr

    def body(idx_ref, own_ref, s1_ref, s2_ref, s3_ref, o_ref):
        del idx_ref
        o_ref[...] = ((own_ref[...] + s1_ref[...].astype(F32)) + s2_ref[...].astype(F32)) + s3_ref[...].astype(F32)

    def slot(m):
        return pl.BlockSpec((None, tr, C), lambda i, idx_ref: (idx_ref[1] ^ m, i, 0))

    return pl.pallas_call(
        body, name=name,
        grid_spec=pltpu.PrefetchScalarGridSpec(
            num_scalar_prefetch=1, grid=(nb,),
            in_specs=[pl.BlockSpec((tr, C), lambda i, idx_ref: (i, 0)), slot(1), slot(2), slot(3)],
            out_specs=pl.BlockSpec((tr, C), lambda i, idx_ref: (idx_ref[0] * nb + i, 0))),
        out_shape=jax.ShapeDtypeStruct((2 * h, C), F32),
        compiler_params=pltpu.CompilerParams(dimension_semantics=("parallel",), vmem_limit_bytes=VMEM_LIMIT),
    )(idx, own, slots, slots, slots)


class _Reducer:
    def __init__(self, idx):
        self.idx, self.chips, self.p16, self.own, self.mine, self.final = idx, {}, {}, {}, {}, {}

    def add(self, name, whole):
        self.chips[name] = _chips_from_whole(name, whole)

    def pair(self, names):
        return _pair_exchange_carry([self.chips[n] for n in names])

    def take_pair(self, names, outs):
        for n, part in zip(names, outs):
            self.p16[n], self.own[n] = _pair_sum(self.chips.pop(n), part, self.idx, "pair_sum_" + n)

    def chip(self, names):
        return _chip_exchange_carry([self.p16[n] for n in names])

    def take_chip(self, names, outs):
        for n, slots in zip(names, outs):
            del self.p16[n]
            self.mine[n] = _chip_sum(self.own.pop(n), slots, self.idx, "chip_sum_" + n)

    def share(self, names):
        return _pair_share_carry([self.mine[n] for n in names])

    def take_share(self, names, outs):
        for n, s in zip(names, outs):
            del self.mine[n]
            self.final[n] = s


def _w_ada_grad(c8, dmod_cols):
    n_cols = dmod_cols.shape[1]
    tn = 512

    def body(c_ref, d_ref, o_ref):
        cv = c_ref[...]
        o_ref[...] = _tn(cv * _sigmoid(cv), d_ref[...], precision=HIGH)

    return _call(body, [c8, dmod_cols], name="w_ada_grad", grid=(n_cols // tn,),
                 in_specs=[pl.BlockSpec((8, D), lambda j: (0, 0)), pl.BlockSpec((8, tn), lambda j: (0, j))],
                 out_specs=[pl.BlockSpec((D, tn), lambda j: (0, j))],
                 out_shape=[jax.ShapeDtypeStruct((D, n_cols), F32)], sem=("parallel",))[0]


_SMALL_SEGS = (("dmod", 6144), ("norm_mix_w", 1024), ("conv_b", 3072), ("ssd_norm_w", 2048), ("pool_scale", 1024),
               ("norm_mlp_w", 1024), ("norm_final_w", 1024), ("conv_w", 4 * XBC), ("d_skip", 2048), ("a_log", 128),
               ("dt_bias", 128), ("loss", 128))
SMALL_OFF = {}
_o = 0
for _n, _s in _SMALL_SEGS:
    SMALL_OFF[_n] = _o
    _o += _s
SMALL_LEN = -(-_o // 1024) * 1024

_FIRST = ("w_in", "conv_w")
_LATER = ("w_branch_ssd", "pool_w", "w_branch_pool", "w_out", "w_up", "w_down")
_SMALL_REPLICATED = ("b_ada", "norm_mix_w", "conv_b", "dt_bias", "a_log", "d_skip", "ssd_norm_w", "pool_scale",
                     "norm_mlp_w", "norm_final_w")
_WEIGHTS = ("w_ada", "b_ada", "norm_mix_w", "w_in", "conv_w", "conv_b", "dt_bias", "a_log", "d_skip", "ssd_norm_w",
            "w_branch_ssd", "pool_w", "pool_scale", "w_branch_pool", "w_out", "norm_mlp_w", "w_up", "w_down",
            "norm_final_w")


def _shard_2d(name, a):
    if name == "conv_w":
        return a.reshape(16, -1)
    return (a.reshape(GW, GW) if name == "pool_w" else a.reshape(a.shape[-2], a.shape[-1])).astype(BF16)


def _whole_from_chips(name, g, own, chip):
    g = lax.dynamic_update_slice(g, own[None], (chip, 0, 0))
    if name == "w_in":
        return _perm_cols(jnp.transpose(g, (1, 0, 2)).reshape(D, IN_COLS))
    if name == "w_up":
        return jnp.transpose(g, (1, 0, 2)).reshape(D, DFF)
    if name == "pool_w":
        return jnp.transpose(g.reshape(N_CHIPS, 4, GW // N_CHIPS, GW), (1, 0, 2, 3)).reshape(4, GW, GW)
    if name == "conv_w":
        return jnp.transpose(g.reshape(N_CHIPS, 4, XBC // N_CHIPS), (1, 0, 2)).reshape(4, XBC)
    return g.reshape(N_CHIPS * g.shape[1], g.shape[2])


def _chips_from_whole(name, g):
    if name.startswith("w_in"):
        return jnp.transpose(_unperm_cols(g).reshape(g.shape[0], N_CHIPS, IN_COLS // N_CHIPS), (1, 0, 2))
    if name == "w_up":
        return jnp.transpose(g.reshape(D, N_CHIPS, DFF // N_CHIPS), (1, 0, 2))
    if name == "pool_w":
        return jnp.transpose(g.reshape(4, N_CHIPS, GW // N_CHIPS, GW), (1, 0, 2, 3)).reshape(N_CHIPS, GW, GW)
    return g.reshape(N_CHIPS, g.shape[0] // N_CHIPS, g.shape[1])


def kernel(x, c, w_ada, b_ada, norm_mix_w, w_in, conv_w, conv_b, dt_bias, a_log, d_skip, ssd_norm_w, w_branch_ssd, pool_w, pool_scale, w_branch_pool, w_out, norm_mlp_w, w_up, w_down, norm_final_w, loss_target, m_w_ada, m_b_ada, m_norm_mix_w, m_w_in, m_conv_w, m_conv_b, m_dt_bias, m_a_log, m_d_skip, m_ssd_norm_w, m_w_branch_ssd, m_pool_w, m_pool_scale, m_w_branch_pool, m_w_out, m_norm_mlp_w, m_w_up, m_w_down, m_norm_final_w, v_w_ada, v_b_ada, v_norm_mix_w, v_w_in, v_conv_w, v_conv_b, v_dt_bias, v_a_log, v_d_skip, v_ssd_norm_w, v_w_branch_ssd, v_pool_w, v_pool_scale, v_w_branch_pool, v_w_out, v_norm_mlp_w, v_w_up, v_w_down, v_norm_final_w):
    args = locals()
    w = {n: args[n] for n in _WEIGHTS}
    m = {n: args["m_" + n] for n in _WEIGHTS}
    v = {n: args["v_" + n] for n in _WEIGHTS}
    xi, yi, ci = _me()
    chip = 2 * xi + yi
    idx = jnp.stack([ci, chip]).astype(jnp.int32)
    ada_cols = w_ada.shape[-1]
    xs, target = x[0], loss_target[0]
    two_d = lambda n, a: a.reshape(GW, GW) if n == "pool_w" else a.reshape(-1, a.shape[-1])
    delta, new_m, new_v, g = {}, {}, {}, {}

    def adamw(n, carry=None):
        res = _adamw(two_d(n, w[n]), two_d(n, g[n]), two_d(n, m[n]), two_d(n, v[n]), "adamw_" + n, carry=carry)
        (delta[n], new_m[n], new_v[n]), extra = res if carry is not None else (res, None)
        return extra

    b_mine = lax.dynamic_slice(b_ada, (0, chip * ada_cols), (1, ada_cols))
    mod, c8 = _ada_fwd(c, w_ada[0], b_mine)
    c8 = c8[:, 0, :]
    shift_m, scale_m, gate_m, shift_f, scale_f, gate_f = [mod[:, D * i:D * (i + 1)] for i in range(6)]
    shards = {n: _shard_2d(n, w[n]) for n in _FIRST + _LATER}
    nf_w = norm_final_w.reshape(1, D)

    h1, first = _norm_mod(xs, norm_mix_w, scale_m, shift_m, "norm_mod_mix",
                          carry=_gather_carry([shards[n] for n in _FIRST]))
    p = {n: _whole_from_chips(n, a, shards[n], chip) for n, a in zip(_FIRST, first)}
    (proj,), later = _matmul(h1, p["w_in"], mode="nn", out_dtypes=[F32], name="mm_proj",
                             carry=_gather_carry([shards[n] for n in _LATER]))
    p.update({n: _whole_from_chips(n, a, shards[n], chip) for n, a in zip(_LATER, later)})
    xbc_a = _conv_fwd(proj, p["conv_w"], conv_b)
    dtb_c, alog_c = dt_bias.reshape(HEADS, 1), a_log.reshape(HEADS, 1)
    dsk_exp = jnp.repeat(d_skip, HEAD_DIM, axis=1)
    y, hin = _ssd_fwd(xbc_a, proj, dt_bias, a_log, dtb_c, alog_c, dsk_exp)
    yn = _gate_norm(y, proj, ssd_norm_w)
    (y_ssd,) = _matmul(yn, p["w_branch_ssd"], mode="nn", out_dtypes=[F32], name="mm_branch_ssd")
    pooled, pw_out, yps = _pool_fwd(proj, p["pool_w"], pool_scale)
    (y_pool,) = _matmul(yps, p["w_branch_pool"], mode="nn", out_dtypes=[F32], name="mm_branch_pool")
    merged = _merge(proj, y_ssd, y_pool)
    resid = lambda acc, r, gt: (r + gt * acc, acc)
    x2, mix = _matmul(merged, p["w_out"], mode="nn", out_dtypes=[F32, F32], name="mm_out",
                      epi=resid, tile_extras=(xs,), row_extras=(gate_m,))
    h2 = _norm_mod(x2, norm_mlp_w, scale_f, shift_f, "norm_mod_mlp")
    relu2 = lambda acc: (acc, jnp.square(jnp.maximum(acc, 0.0)))
    up, act = _matmul(h2, p["w_up"], mode="nn", out_dtypes=[F32, BF16], name="mm_up", epi=relu2)
    x3, down = _matmul(act, p["w_down"], mode="nn", out_dtypes=[F32, F32], name="mm_down",
                       epi=resid, tile_extras=(x2,), row_extras=(gate_f,))

    red = _Reducer(idx)
    dx3, d_down, sums_f = _final_loss_bwd(x3, target, nf_w, down, gate_f)
    drelu2 = lambda acc, u: (acc * (2.0 * jnp.maximum(u, 0.0)),)
    (dup,) = _matmul(d_down, p["w_down"], mode="nt", out_dtypes=[BF16], name="mm_dact",
                     epi=drelu2, tile_extras=(up,))
    red.add("w_down", _matmul(act, d_down, mode="tn", out_dtypes=[F32], name="mm_g_down")[0])
    (dh2,), got = _matmul(dup, p["w_up"], mode="nt", out_dtypes=[F32], name="mm_dh2",
                          carry=red.pair(["w_down"]))
    red.take_pair(["w_down"], got)
    red.add("w_up", _matmul(h2, dup, mode="tn", out_dtypes=[F32], name="mm_g_up")[0])
    dx2, sums_2, dmix = _norm_mod_bwd(x2, dh2, dx3, norm_mlp_w, scale_f, "norm_mod_mlp_bwd", branch=mix, gate=gate_m)
    (dmerged,), got = _matmul(dmix, p["w_out"], mode="nt", out_dtypes=[F32], name="mm_dmerged",
                              carry=red.pair(["w_up"]))
    red.take_pair(["w_up"], got)
    red.add("w_out", _matmul(merged, dmix, mode="tn", out_dtypes=[F32], name="mm_g_out")[0])
    dy_ssd, dy_pool, dproj = _merge_bwd(dmerged, proj, y_ssd, y_pool)
    (dyp,), got = _matmul(dy_pool, p["w_branch_pool"], mode="nt", out_dtypes=[F32], name="mm_dyp",
                          carry=red.pair(["w_out"]))
    red.take_pair(["w_out"], got)
    red.add("w_branch_pool", _matmul(yps, dy_pool, mode="tn", out_dtypes=[F32], name="mm_g_bpool")[0])
    dproj, g_pool_w, sums_pool = _pool_bwd(dyp, pw_out, pooled, p["pool_w"], pool_scale, dproj)
    red.add("pool_w", g_pool_w)
    red.add("w_branch_ssd", _matmul(yn, dy_ssd, mode="tn", out_dtypes=[F32], name="mm_g_bssd")[0])
    mixers = ["w_branch_pool", "pool_w", "w_branch_ssd"]
    (dyn,), got = _matmul(dy_ssd, p["w_branch_ssd"], mode="nt", out_dtypes=[F32], name="mm_dyn",
                          carry=red.pair(mixers))
    red.take_pair(mixers, got)
    dy, dproj, sums_gn = _gate_norm_bwd(dyn, y, proj, ssd_norm_w, dproj)
    six = ["w_down", "w_up", "w_out"] + mixers
    (dxa, dproj, dsk_sum, ssd_small), got = _ssd_bwd(dy, xbc_a, proj, hin, dt_bias, a_log, dtb_c, alog_c, dsk_exp,
                                                     dproj, carry=red.chip(six))
    red.take_chip(six, got)
    dxc, sums_conv = _conv_bwd_a(dxa, proj, p["conv_w"], conv_b)
    dproj = _conv_bwd_b(dxc, p["conv_w"], dproj)
    rows_a = 3 * D // 4
    (g_in_a,), got = _matmul(h1, dproj, mode="tn", out_dtypes=[F32], name="mm_g_in_a", a_cols=(0, rows_a),
                             carry=red.share(six))
    red.take_share(six, got)
    red.add("w_in_a", g_in_a)
    (g_in_b,), got = _matmul(h1, dproj, mode="tn", out_dtypes=[F32], name="mm_g_in_b", a_cols=(rows_a, D - rows_a),
                             carry=red.pair(["w_in_a"]))
    red.take_pair(["w_in_a"], got)
    red.add("w_in_b", g_in_b)
    (dh1,), got = _matmul(dproj, p["w_in"], mode="nt", out_dtypes=[F32], name="mm_dh1",
                          carry=_join(red.chip(["w_in_a"]), red.pair(["w_in_b"])))
    red.take_chip(["w_in_a"], got[:1])
    red.take_pair(["w_in_b"], got[1:])
    (grad_x, sums_1), got = _norm_mod_bwd(xs, dh1, dx2, norm_mix_w, scale_m, "norm_mod_mix_bwd",
                                          carry=_join(red.chip(["w_in_b"]), red.share(["w_in_a"])))
    red.take_chip(["w_in_b"], got[:1])
    red.take_share(["w_in_a"], got[1:])

    dmod = jnp.concatenate([sums_1[0:1], sums_1[1:2], sums_2[3:4], sums_2[0:1], sums_2[1:2], sums_f[1:2]], axis=1)
    pad96 = jnp.zeros((1, 96), F32)
    small = {"dmod": dmod, "norm_mix_w": sums_1[2:3], "conv_b": sums_conv[4:5], "ssd_norm_w": sums_gn[0:1],
             "pool_scale": sums_pool[0:1], "norm_mlp_w": sums_2[2:3], "norm_final_w": sums_f[0:1],
             "conv_w": sums_conv[0:4].reshape(1, 4 * XBC), "d_skip": dsk_sum[0:1],
             "a_log": jnp.concatenate([ssd_small[0:1], pad96], axis=1),
             "dt_bias": jnp.concatenate([ssd_small[1:2], pad96], axis=1), "loss": sums_f[3:4, 0:128]}
    vec = jnp.concatenate([small[n] for n, _ in _SMALL_SEGS], axis=1)
    vec = jnp.pad(vec, ((0, 0), (0, SMALL_LEN - vec.shape[1]))).reshape(SMALL_LEN // 128, 128)
    every, total, dsk = _gather_small(vec)
    total = total.reshape(1, SMALL_LEN)
    seg = lambda n, size: total[:, SMALL_OFF[n]:SMALL_OFF[n] + size]
    g.update({"b_ada": seg("dmod", 6 * D), "norm_mix_w": seg("norm_mix_w", D), "conv_b": seg("conv_b", XBC),
              "dt_bias": seg("dt_bias", HEADS), "a_log": seg("a_log", HEADS), "d_skip": dsk[:, 0:2].reshape(1, HEADS),
              "ssd_norm_w": seg("ssd_norm_w", DI), "pool_scale": seg("pool_scale", D),
              "norm_mlp_w": seg("norm_mlp_w", D), "norm_final_w": seg("norm_final_w", D)})
    loss = total[0, SMALL_OFF["loss"]]
    conv_cols = conv_w.shape[-1]
    g["conv_w"] = lax.dynamic_slice(seg("conv_w", 4 * XBC).reshape(4, XBC), (0, chip * conv_cols), (4, conv_cols))
    dmod8 = every.reshape(8, SMALL_LEN)[:, SMALL_OFF["dmod"]:SMALL_OFF["dmod"] + 6 * D]
    g["w_ada"] = _w_ada_grad(c8, lax.dynamic_slice(dmod8, (0, chip * ada_cols), (8, ada_cols)))

    got = adamw("w_ada", carry=red.share(["w_in_b"]))
    red.take_share(["w_in_b"], got)
    for n in six:
        g[n] = red.final[n]
    g["w_in"] = jnp.concatenate([red.final["w_in_a"], red.final["w_in_b"]], axis=0)
    for n in ["conv_w", "w_in"] + six:
        adamw(n)
    sizes = [w[n].size for n in _SMALL_REPLICATED]
    n_small = -(-sum(sizes) // 1024) * 1024
    pack = lambda d: jnp.pad(jnp.concatenate([d[n].reshape(1, -1) for n in _SMALL_REPLICATED], axis=1),
                             ((0, 0), (0, n_small - sum(sizes)))).reshape(n_small // 128, 128)
    d_, m_, v_ = _adamw(pack(w), pack(g), pack(m), pack(v), "adamw_small")
    off = 0
    for n, s in zip(_SMALL_REPLICATED, sizes):
        for dst, src in ((delta, d_), (new_m, m_), (new_v, v_)):
            dst[n] = src.reshape(1, n_small)[:, off:off + s]
        off += s

    out = [loss, grad_x.reshape(x.shape)]
    for d in (g, delta, new_m, new_v):
        out += [d[n].reshape(w[n].shape) for n in _WEIGHTS]
    return tuple(out)
```

```python
import functools
import operator

import jax
import jax.numpy as jnp
import numpy as np
from jax import lax
from jax.experimental import pallas as pl
from jax.experimental.pallas import tpu as pltpu

F32, BF16 = jnp.float32, jnp.bfloat16
HIGH = lax.Precision.HIGHEST
MESH = pl.DeviceIdType.MESH

D = 1024
DI = 2048
HEADS, HEAD_DIM = 32, 64
GROUPS, STATE = 4, 128
Q = 128
XBC = DI + 2 * GROUPS * STATE
POOL_WINDOWS = (2, 4, 8, 16)
GW = 256
DFF = 4096
EPS = 1e-5
IN_COLS = 8224
OFF_Z, OFF_XBC, OFF_POOL, OFF_GATE, OFF_DT, NP = 0, 2048, 5120, 6144, 8192, 8448
N_CHIPS = 4
ADAM_LR, ADAM_B1, ADAM_B2, ADAM_EPS, ADAM_WD, ADAM_STEP = 0.001, 0.9, 0.999, 1e-08, 0.01, 10
VMEM_LIMIT = 56 * 2 ** 20
NEG = -1e30


def _sigmoid(v):
    return 0.5 * jnp.tanh(0.5 * v) + 0.5


def _softplus(v):
    return jnp.maximum(v, 0.0) + jnp.log1p(jnp.exp(-jnp.abs(v)))


def _dot(a, b, dims, **kw):
    return lax.dot_general(a, b, (dims, ((), ())), preferred_element_type=F32, **kw)


def _nn(a, b, **kw):
    return _dot(a, b, ((1,), (0,)), **kw)


def _nt(a, b, **kw):
    return _dot(a, b, ((1,), (1,)), **kw)


def _tn(a, b, **kw):
    return _dot(a, b, ((0,), (0,)), **kw)


def _perm_cols(w):
    pad = jnp.zeros(w.shape[:-1] + (NP - IN_COLS,), w.dtype)
    return jnp.concatenate([w[..., :5120], w[..., 5152:], w[..., 5120:5152], pad], axis=-1)


def _unperm_cols(g):
    return jnp.concatenate([g[..., :5120], g[..., OFF_DT:OFF_DT + 32], g[..., 5120:OFF_DT]], axis=-1)


class _Sems:
    def __init__(self, send, recv, local, base=0):
        self._send, self._recv, self._local, self._base = send, recv, local, base

    def shift(self, n):
        return _Sems(self._send, self._recv, self._local, self._base + n)

    def send(self, i):
        return self._send.at[self._base + i]

    def recv(self, i):
        return self._recv.at[self._base + i]

    def local(self, i):
        return self._local.at[self._base + i]


class _Carry:
    def __init__(self, ins, out_shapes, n_sems, start, finish, aliased=()):
        self.ins, self.out_shapes, self.n_sems, self.start, self.finish = list(ins), list(out_shapes), n_sems, start, finish
        self.aliased = list(aliased)


def _join(*carries):
    def run(which):
        def fn(ins, outs, sems):
            i = o = s = 0
            for cy in carries:
                getattr(cy, which)(ins[i:i + len(cy.ins)], outs[o:o + len(cy.out_shapes)], sems.shift(s))
                i, o, s = i + len(cy.ins), o + len(cy.out_shapes), s + cy.n_sems
        return fn

    aliased, i, o = [], 0, 0
    for cy in carries:
        aliased += [(i + a, o + b) for a, b in cy.aliased]
        i, o = i + len(cy.ins), o + len(cy.out_shapes)
    return _Carry([a for cy in carries for a in cy.ins], [a for cy in carries for a in cy.out_shapes],
                  sum(cy.n_sems for cy in carries), run("start"), run("finish"), aliased)


def _call(body, args, *, name, grid=(), in_specs, out_specs, out_shape, scratch_shapes=(), sem=None, aliases=None,
          carry=None):
    in_specs, out_specs, out_shape, scratch_shapes = list(in_specs), list(out_specs), list(out_shape), list(scratch_shapes)
    n_in, n_out, n_scr = len(in_specs), len(out_specs), len(scratch_shapes)
    kw = {"vmem_limit_bytes": VMEM_LIMIT}
    if carry is None:
        kernel_fn = functools.partial(body)
        if sem is not None:
            kw["dimension_semantics"] = sem
    else:
        n_ci, n_co = len(carry.ins), len(carry.out_shapes)
        hbm = pl.BlockSpec(memory_space=pl.ANY)
        in_specs += [hbm] * n_ci
        out_specs += [hbm] * n_co
        out_shape += carry.out_shapes
        n_s = max(carry.n_sems, 1)
        scratch_shapes += [pltpu.SemaphoreType.DMA((n_s,))] * 3
        args = list(args) + carry.ins
        aliases = dict(aliases or {})
        aliases.update({n_in + i: n_out + o for i, o in carry.aliased})
        if grid:
            kw["dimension_semantics"] = ("arbitrary",) * len(grid)

        def kernel_fn(*refs):
            a = n_in
            ins, c_ins = refs[:a], refs[a:a + n_ci]
            a += n_ci
            outs, c_outs = refs[a:a + n_out], refs[a + n_out:a + n_out + n_co]
            a += n_out + n_co
            scr, sems = refs[a:a + n_scr], _Sems(*refs[a + n_scr:a + n_scr + 3])
            if grid:
                ids = [pl.program_id(d) for d in range(len(grid))]
                first = functools.reduce(operator.and_, [i == 0 for i in ids])
                last = functools.reduce(operator.and_, [i == g - 1 for i, g in zip(ids, grid)])

                @pl.when(first)
                def _():
                    carry.start(c_ins, c_outs, sems)

                body(*ins, *outs, *scr)

                @pl.when(last)
                def _():
                    carry.finish(c_ins, c_outs, sems)
            else:
                carry.start(c_ins, c_outs, sems)
                body(*ins, *outs, *scr)
                carry.finish(c_ins, c_outs, sems)

    outs = pl.pallas_call(
        kernel_fn, name=name, grid=grid, in_specs=in_specs, out_specs=out_specs, out_shape=out_shape,
        scratch_shapes=scratch_shapes, input_output_aliases=aliases or {},
        compiler_params=pltpu.CompilerParams(**kw),
    )(*args)
    outs = list(outs)
    return outs if carry is None else (outs[:n_out], outs[n_out:])


def _run_carry(carry, name):
    _, outs = _call(lambda: None, [], name=name, in_specs=[], out_specs=[], out_shape=[], carry=carry)
    return outs


_TILES = {
    "mm_proj": (1024, 2816, 1024), "mm_branch_ssd": (1024, 1024, 2048), "mm_branch_pool": (1024, 1024, 1024),
    "mm_out": (1024, 1024, 1024), "mm_up": (1024, 1024, 1024), "mm_down": (512, 1024, 4096),
    "mm_dact": (1024, 1024, 1024), "mm_g_down": (1024, 1024, 2048), "mm_dh2": (1024, 1024, 4096),
    "mm_g_up": (1024, 1024, 2048), "mm_dmerged": (1024, 1024, 1024), "mm_g_out": (1024, 1024, 2048),
    "mm_dyp": (1024, 1024, 1024), "mm_g_bpool": (1024, 1024, 2048), "mm_g_bssd": (1024, 1024, 2048),
    "mm_dyn": (1024, 1024, 1024), "mm_g_in_a": (768, 1408, 2048), "mm_g_in_b": (256, 2816, 2048),
    "mm_dh1": (1024, 1024, 2816),
}


def _matmul(a, b, *, mode, out_dtypes, name, epi=None, tile_extras=(), row_extras=(), carry=None, a_cols=None,
            chip_blocks=False):
    M, K = (a.shape[1], a.shape[0]) if mode == "tn" else a.shape
    N = b.shape[0] if mode == "nt" else b.shape[1]
    a_start, M = a_cols if a_cols is not None else (0, M)
    tm, tn, tk = _TILES[name]
    tm, tn, tk = min(tm, M), min(tn, N), min(tk, K)
    assert M % tm == 0 and N % tn == 0 and K % tk == 0 and a_start % tm == 0, (name, M, N, K, tm, tn, tk)
    a_off = a_start // tm
    if mode == "nn":
        a_spec = pl.BlockSpec((tm, tk), lambda i, j, k: (i, k))
        b_spec = pl.BlockSpec((tk, tn), lambda i, j, k: (k, j))
        dims = ((1,), (0,))
    elif mode == "nt":
        a_spec = pl.BlockSpec((tm, tk), lambda i, j, k: (i, k))
        b_spec = pl.BlockSpec((tn, tk), lambda i, j, k: (j, k))
        dims = ((1,), (1,))
    else:
        a_spec = pl.BlockSpec((tk, tm), lambda i, j, k: (k, i + a_off))
        b_spec = pl.BlockSpec((tk, tn), lambda i, j, k: (k, j))
        dims = ((0,), (0,))
    nk = K // tk
    n_te, n_re, n_out = len(tile_extras), len(row_extras), len(out_dtypes)
    if epi is None:
        epi = lambda acc: (acc,)

    def body(a_ref, b_ref, *rest):
        extras = rest[:n_te + n_re]
        outs = rest[n_te + n_re:n_te + n_re + n_out]
        p = _dot(a_ref[...], b_ref[...], dims)

        def finish(acc):
            vals = epi(acc, *[e[...] for e in extras])
            for o, v in zip(outs, vals):
                o[...] = v.astype(o.dtype)

        if nk == 1:
            finish(p)
        else:
            acc_ref = rest[-1]
            k = pl.program_id(2)

            @pl.when(k == 0)
            def _():
                acc_ref[...] = p

            @pl.when(k > 0)
            def _():
                acc_ref[...] += p

            @pl.when(k == nk - 1)
            def _():
                finish(acc_ref[...])

    tile_spec = pl.BlockSpec((tm, tn), lambda i, j, k: (i, j))
    row_spec = pl.BlockSpec((1, tn), lambda i, j, k: (0, j))
    out_spec, out_dims = tile_spec, (M, N)
    if chip_blocks:
        assert n_te == 0 and tn * N_CHIPS == N
        out_spec, out_dims = pl.BlockSpec((None, tm, tn), lambda i, j, k: (j, i, 0)), (N_CHIPS, M, tn)
    return _call(
        body, [a, b, *tile_extras, *row_extras], name=name, grid=(M // tm, N // tn, nk),
        in_specs=[a_spec, b_spec] + [tile_spec] * n_te + [row_spec] * n_re, out_specs=[out_spec] * n_out,
        out_shape=[jax.ShapeDtypeStruct(out_dims, dt) for dt in out_dtypes],
        scratch_shapes=[pltpu.VMEM((tm, tn), F32)] if nk > 1 else [],
        sem=("parallel", "parallel", "arbitrary"), carry=carry)


def _row_tile(T):
    return min(512, T)


def _norm_mod(x, nw, scale, shift, name, carry=None):
    T = x.shape[0]
    tr = _row_tile(T)

    def body(x_ref, nw_ref, sc_ref, sh_ref, o_ref):
        xv = x_ref[...]
        r = lax.rsqrt(jnp.mean(xv * xv, axis=-1, keepdims=True) + EPS)
        o_ref[...] = ((xv * r) * nw_ref[...] * (1.0 + sc_ref[...]) + sh_ref[...]).astype(BF16)

    tile = pl.BlockSpec((tr, D), lambda i: (i, 0))
    row = pl.BlockSpec((1, D), lambda i: (0, 0))
    res = _call(body, [x, nw, scale, shift], name=name, grid=(T // tr,), in_specs=[tile, row, row, row],
                out_specs=[tile], out_shape=[jax.ShapeDtypeStruct((T, D), BF16)], sem=("parallel",), carry=carry)
    return res[0] if carry is None else (res[0][0], res[1])


def _norm_mod_bwd(x, dh, dres, nw, scale, name, branch=None, gate=None, carry=None):
    T = x.shape[0]
    tr = _row_tile(T)
    with_branch = branch is not None

    def body(x_ref, dh_ref, dr_ref, nw_ref, sc_ref, *rest):
        if with_branch:
            br_ref, g_ref, dx_ref, sums_ref, db_ref = rest
        else:
            dx_ref, sums_ref = rest
        i = pl.program_id(0)

        @pl.when(i == 0)
        def _():
            sums_ref[...] = jnp.zeros_like(sums_ref)

        xv, dhv = x_ref[...], dh_ref[...]
        r = lax.rsqrt(jnp.mean(xv * xv, axis=-1, keepdims=True) + EPS)
        xn = xv * r
        g1 = dhv * (1.0 + sc_ref[...])
        dxn = g1 * nw_ref[...]
        dx = dr_ref[...] + r * (dxn - xn * jnp.mean(dxn * xn, axis=-1, keepdims=True))
        dx_ref[...] = dx
        sums_ref[0:1, :] += jnp.sum(dhv, axis=0, keepdims=True)
        sums_ref[1:2, :] += jnp.sum(dhv * (xn * nw_ref[...]), axis=0, keepdims=True)
        sums_ref[2:3, :] += jnp.sum(g1 * xn, axis=0, keepdims=True)
        if with_branch:
            db_ref[...] = (dx * g_ref[...]).astype(BF16)
            sums_ref[3:4, :] += jnp.sum(dx * br_ref[...], axis=0, keepdims=True)

    tile = pl.BlockSpec((tr, D), lambda i: (i, 0))
    row = pl.BlockSpec((1, D), lambda i: (0, 0))
    sums = pl.BlockSpec((8, D), lambda i: (0, 0))
    ins = [x, dh, dres, nw, scale] + ([branch, gate] if with_branch else [])
    in_specs = [tile, tile, tile, row, row] + ([tile, row] if with_branch else [])
    out_specs = [tile, sums] + ([tile] if with_branch else [])
    out_shape = [jax.ShapeDtypeStruct((T, D), F32), jax.ShapeDtypeStruct((8, D), F32)]
    if with_branch:
        out_shape.append(jax.ShapeDtypeStruct((T, D), BF16))
    return _call(body, ins, name=name, grid=(T // tr,), in_specs=in_specs, out_specs=out_specs, out_shape=out_shape,
                 sem=("arbitrary",), carry=carry)


def _final_loss_bwd(x3, target, wf, down, gate_f):
    T = x3.shape[0]
    tr = _row_tile(T)
    n_steps = T // tr

    def body(x_ref, t_ref, w_ref, dn_ref, g_ref, dx_ref, dd_ref, sums_ref):
        i = pl.program_id(0)

        @pl.when(i == 0)
        def _():
            sums_ref[...] = jnp.zeros_like(sums_ref)

        xv = x_ref[...]
        r = lax.rsqrt(jnp.mean(xv * xv, axis=-1, keepdims=True) + EPS)
        xn = xv * r
        err = xn * w_ref[...] - t_ref[...]
        dy = err * (1.0 / D)
        dxn = dy * w_ref[...]
        dx = r * (dxn - xn * jnp.mean(dxn * xn, axis=-1, keepdims=True))
        dx_ref[...] = dx
        dd_ref[...] = (dx * g_ref[...]).astype(BF16)
        sums_ref[0:1, :] += jnp.sum(dy * xn, axis=0, keepdims=True)
        sums_ref[1:2, :] += jnp.sum(dx * dn_ref[...], axis=0, keepdims=True)
        sums_ref[2:3, :] += jnp.sum(err * err, axis=0, keepdims=True) * (0.5 / D)

        @pl.when(i == n_steps - 1)
        def _():
            sums_ref[3:4, :] = jnp.broadcast_to(jnp.sum(sums_ref[2:3, :], axis=1, keepdims=True), (1, D))

    tile = pl.BlockSpec((tr, D), lambda i: (i, 0))
    row = pl.BlockSpec((1, D), lambda i: (0, 0))
    sums = pl.BlockSpec((8, D), lambda i: (0, 0))
    return _call(body, [x3, target, wf, down, gate_f], name="final_loss_bwd", grid=(n_steps,),
                 in_specs=[tile, tile, row, tile, row], out_specs=[tile, tile, sums],
                 out_shape=[jax.ShapeDtypeStruct((T, D), F32), jax.ShapeDtypeStruct((T, D), BF16),
                            jax.ShapeDtypeStruct((8, D), F32)], sem=("arbitrary",))


CONV_TC = 1024


def _conv_taps(xp, w, b):
    acc = b + w[3:4, :] * xp
    for k in range(3):
        acc = acc + w[k:k + 1, :] * pltpu.roll(xp, 3 - k, 0)
    return acc


def _conv_fwd(proj, conv_w, conv_b):
    T = proj.shape[0]
    tr = _row_tile(T)
    nb, offb = tr // 8, OFF_XBC // CONV_TC

    def body(x_ref, h_ref, w_ref, b_ref, o_ref):
        halo = jnp.where(pl.program_id(0) > 0, h_ref[...], 0.0)
        xp = jnp.concatenate([halo, x_ref[...]], axis=0)
        acc = _conv_taps(xp, w_ref[...], b_ref[...])[8:]
        o_ref[...] = acc * _sigmoid(acc)

    return _call(
        body, [proj, proj, conv_w, conv_b], name="conv_fwd", grid=(T // tr, XBC // CONV_TC),
        in_specs=[pl.BlockSpec((tr, CONV_TC), lambda i, j: (i, j + offb)),
                  pl.BlockSpec((8, CONV_TC), lambda i, j: (jnp.maximum(i * nb - 1, 0), j + offb)),
                  pl.BlockSpec((4, CONV_TC), lambda i, j: (0, j)),
                  pl.BlockSpec((1, CONV_TC), lambda i, j: (0, j))],
        out_specs=[pl.BlockSpec((tr, CONV_TC), lambda i, j: (i, j))],
        out_shape=[jax.ShapeDtypeStruct((T, XBC), F32)], sem=("parallel", "parallel"))[0]


def _conv_bwd_a(dxa, proj, conv_w, conv_b):
    T = proj.shape[0]
    tr = _row_tile(T)
    nb, offb = tr // 8, OFF_XBC // CONV_TC

    def body(d_ref, x_ref, h_ref, w_ref, b_ref, o_ref, sums_ref):
        i = pl.program_id(1)

        @pl.when(i == 0)
        def _():
            sums_ref[...] = jnp.zeros_like(sums_ref)

        halo = jnp.where(i > 0, h_ref[...], 0.0)
        xp = jnp.concatenate([halo, x_ref[...]], axis=0)
        acc = _conv_taps(xp, w_ref[...], b_ref[...])[8:]
        s = _sigmoid(acc)
        dxc = d_ref[...] * (s * (1.0 + acc * (1.0 - s)))
        o_ref[...] = dxc
        sums_ref[3:4, :] += jnp.sum(dxc * x_ref[...], axis=0, keepdims=True)
        for k in range(3):
            sums_ref[k:k + 1, :] += jnp.sum(dxc * pltpu.roll(xp, 3 - k, 0)[8:], axis=0, keepdims=True)
        sums_ref[4:5, :] += jnp.sum(dxc, axis=0, keepdims=True)

    return _call(
        body, [dxa, proj, proj, conv_w, conv_b], name="conv_bwd_a", grid=(XBC // CONV_TC, T // tr),
        in_specs=[pl.BlockSpec((tr, CONV_TC), lambda j, i: (i, j)),
                  pl.BlockSpec((tr, CONV_TC), lambda j, i: (i, j + offb)),
                  pl.BlockSpec((8, CONV_TC), lambda j, i: (jnp.maximum(i * nb - 1, 0), j + offb)),
                  pl.BlockSpec((4, CONV_TC), lambda j, i: (0, j)),
                  pl.BlockSpec((1, CONV_TC), lambda j, i: (0, j))],
        out_specs=[pl.BlockSpec((tr, CONV_TC), lambda j, i: (i, j)), pl.BlockSpec((8, CONV_TC), lambda j, i: (0, j))],
        out_shape=[jax.ShapeDtypeStruct((T, XBC), F32), jax.ShapeDtypeStruct((8, XBC), F32)],
        sem=("parallel", "arbitrary"))


def _conv_bwd_b(dxc, conv_w, dproj):
    T = dxc.shape[0]
    tr = _row_tile(T)
    nb, offb, last = tr // 8, OFF_XBC // CONV_TC, T // tr - 1

    def body(d_ref, h_ref, w_ref, dp_in, o_ref):
        del dp_in
        halo = jnp.where(pl.program_id(0) < last, h_ref[...], 0.0)
        xp = jnp.concatenate([d_ref[...], halo], axis=0)
        n = xp.shape[0]
        w = w_ref[...]
        acc = w[3:4, :] * xp
        for k in range(3):
            acc = acc + w[k:k + 1, :] * pltpu.roll(xp, n - (3 - k), 0)
        o_ref[...] = acc[:tr].astype(BF16)

    return _call(
        body, [dxc, dxc, conv_w, dproj], name="conv_bwd_b", grid=(T // tr, XBC // CONV_TC),
        in_specs=[pl.BlockSpec((tr, CONV_TC), lambda i, j: (i, j)),
                  pl.BlockSpec((8, CONV_TC), lambda i, j: (jnp.minimum((i + 1) * nb, T // 8 - 1), j)),
                  pl.BlockSpec((4, CONV_TC), lambda i, j: (0, j)),
                  pl.BlockSpec(memory_space=pl.ANY)],
        out_specs=[pl.BlockSpec((tr, CONV_TC), lambda i, j: (i, j + offb))],
        out_shape=[jax.ShapeDtypeStruct(dproj.shape, BF16)], aliases={3: 0}, sem=("parallel", "parallel"))[0]


def _spread(v, sel, pieces):
    out = None
    for _ in range(pieces):
        p = v.astype(BF16)
        term = _nn(p, sel)
        out = term if out is None else out + term
        v = v - p.astype(F32)
    return out


def _ssd_selectors():
    g = np.arange(GROUPS)[:, None, None]
    h = np.arange(HEADS)[None, :, None]
    blocks = (h == 8 * g + np.arange(1024)[None, None, :] // 128)
    pairs = (h == 8 * g + np.arange(512)[None, None, :] // HEAD_DIM)
    lane = np.arange(128)[None, None, :]
    block_sum = (lane == 8 * g + np.arange(1024)[None, :, None] // 128)
    pair_sum = (lane == 8 * g + np.arange(512)[None, :, None] // HEAD_DIM)
    return [jnp.asarray(m, BF16) for m in (blocks, pairs, block_sum, pair_sum)]


def _ssd_group(g, cs, csT, dt, s_mat, causal_w, lo, blocks_ref, pairs_ref):
    csb = _spread(cs, blocks_ref[g], 3)
    row = jnp.concatenate([csT[8 * g + hh:8 * g + hh + 1, :] for hh in range(8)], axis=1)
    l_w = jnp.exp(jnp.where(causal_w, csb - row, NEG))
    m_w = jnp.concatenate([s_mat] * 8, axis=1) * l_w
    cs_g = jnp.concatenate([jnp.where(lo, csb[:, 256 * jj:256 * jj + 128], csb[:, 256 * jj + 128:256 * jj + 256])
                            for jj in range(4)], axis=1)
    cs_last = cs_g[Q - 1:Q, :]
    return m_w, l_w, _spread(dt, pairs_ref[g], 2), jnp.exp(cs_g), jnp.exp(cs_last - cs_g), jnp.exp(cs_last)


def _ssd_common(dtp_ref, dtb_r, alog_r, dtb_c, alog_c):
    rows = lax.broadcasted_iota(jnp.int32, (Q, Q), 0)
    cols = lax.broadcasted_iota(jnp.int32, (Q, Q), 1)
    causal = cols <= rows
    tri = causal.astype(F32)
    raw = dtp_ref[:, 0:HEADS] + dtb_r[...]
    dt = _softplus(raw)
    a_r = -jnp.exp(alog_r[...])
    cs = _nn(tri, dt * a_r, precision=HIGH)
    aT = _softplus(dtp_ref[...].T[0:HEADS, :] + dtb_c[...]) * (-jnp.exp(alog_c[...]))
    csT = _nt(aT, tri, precision=HIGH)
    return causal, raw, dt, a_r, cs, csT


def _ssd_fwd(xbc_a, proj, dtb_r, alog_r, dtb_c, alog_c, dsk_exp):
    T = xbc_a.shape[0]
    nc = T // Q

    def body(xbc_ref, dtp_ref, dtb_r_ref, alog_r_ref, dtb_c_ref, alog_c_ref, dsk_ref, blocks_ref, pairs_ref,
             y_ref, hin_ref, h_scr):
        @pl.when(pl.program_id(0) == 0)
        def _():
            h_scr[...] = jnp.zeros_like(h_scr)

        _, _, dt, _, cs, csT = _ssd_common(dtp_ref, dtb_r_ref, alog_r_ref, dtb_c_ref, alog_c_ref)
        lo = lax.broadcasted_iota(jnp.int32, (1, 128), 1) < HEAD_DIM
        hi = jnp.logical_not(lo)
        causal_w = (lax.broadcasted_iota(jnp.int32, (Q, 1024), 1) & (Q - 1)) <= lax.broadcasted_iota(jnp.int32, (Q, 1024), 0)
        for g in range(GROUPS):
            gs = slice(512 * g, 512 * (g + 1))
            hs = slice(128 * g, 128 * (g + 1))
            xs_g = xbc_ref[:, gs]
            b_g = xbc_ref[:, DI + STATE * g:DI + STATE * (g + 1)].astype(BF16)
            c_g = xbc_ref[:, DI + 512 + STATE * g:DI + 512 + STATE * (g + 1)].astype(BF16)
            m_w, _, dt_g, ecs_g, dec_g, cd_g = _ssd_group(g, cs, csT, dt, _nt(c_g, b_g), causal_w, lo, blocks_ref, pairs_ref)
            m_b = m_w.astype(BF16)
            xdt = xs_g * dt_g
            xdt_b = xdt.astype(BF16)
            ys = []
            for jj in range(4):
                xp = xdt_b[:, 128 * jj:128 * (jj + 1)]
                x_ab = jnp.concatenate([jnp.where(lo, xp, jnp.zeros_like(xp)), jnp.where(hi, xp, jnp.zeros_like(xp))], axis=0)
                ys.append(_nn(m_b[:, 256 * jj:256 * (jj + 1)], x_ab))
            h_g = h_scr[hs, :]
            hin_ref[0, hs, :] = h_g
            y_ref[:, gs] = jnp.concatenate(ys, axis=1) + _nn(c_g, h_g.astype(BF16)) * ecs_g + dsk_ref[:, gs] * xs_g
            h_scr[hs, :] = h_g * cd_g + _tn(b_g, (xdt * dec_g).astype(BF16))

    small_r = pl.BlockSpec((1, HEADS), lambda c: (0, 0))
    small_c = pl.BlockSpec((HEADS, 1), lambda c: (0, 0))
    blocks, pairs, _, _ = _ssd_selectors()
    whole = lambda a: pl.BlockSpec(a.shape, lambda c: (0,) * a.ndim)
    return _call(
        body, [xbc_a, proj, dtb_r, alog_r, dtb_c, alog_c, dsk_exp, blocks, pairs], name="ssd_fwd", grid=(nc,),
        in_specs=[pl.BlockSpec((Q, XBC), lambda c: (c, 0)),
                  pl.BlockSpec((Q, 128), lambda c: (c, OFF_DT // 128)),
                  small_r, small_r, small_c, small_c,
                  pl.BlockSpec((1, DI), lambda c: (0, 0)), whole(blocks), whole(pairs)],
        out_specs=[pl.BlockSpec((Q, DI), lambda c: (c, 0)), pl.BlockSpec((1, 512, 512), lambda c: (c, 0, 0))],
        out_shape=[jax.ShapeDtypeStruct((T, DI), F32), jax.ShapeDtypeStruct((nc, 512, 512), F32)],
        scratch_shapes=[pltpu.VMEM((512, 512), F32)], sem=("arbitrary",))


def _ssd_bwd(dy, xbc_a, proj, hin, dtb_r, alog_r, dtb_c, alog_c, dsk_exp, dproj, carry=None):
    T = xbc_a.shape[0]
    nc = T // Q

    def body(dy_ref, xbc_ref, dtp_ref, hin_ref, dtb_r_ref, alog_r_ref, dtb_c_ref, alog_c_ref, dsk_ref, dp_in,
             blocks_ref, pairs_ref, block_sum_ref, pair_sum_ref, dxa_ref, dp_ref, dsk_sum_ref, small_ref, dh_scr):
        del dp_in

        @pl.when(pl.program_id(0) == 0)
        def _():
            dh_scr[...] = jnp.zeros_like(dh_scr)
            dsk_sum_ref[...] = jnp.zeros_like(dsk_sum_ref)
            small_ref[...] = jnp.zeros_like(small_ref)

        _, raw, dt, a_r, cs, csT = _ssd_common(dtp_ref, dtb_r_ref, alog_r_ref, dtb_c_ref, alog_c_ref)
        lo = lax.broadcasted_iota(jnp.int32, (1, 128), 1) < HEAD_DIM
        hi = jnp.logical_not(lo)
        sub32 = lax.broadcasted_iota(jnp.int32, (HEADS, 1), 0)
        causal_w = (lax.broadcasted_iota(jnp.int32, (Q, 1024), 1) & (Q - 1)) <= lax.broadcasted_iota(jnp.int32, (Q, 1024), 0)
        dcs_c = jnp.zeros((Q, 128), F32)
        dcs_r = jnp.zeros((HEADS, Q), F32)
        dcs_l = jnp.zeros((8, 128), F32)
        ddt_x = jnp.zeros((Q, 128), F32)
        for g in range(GROUPS):
            gs = slice(512 * g, 512 * (g + 1))
            hs = slice(128 * g, 128 * (g + 1))
            xs_g, dy_g = xbc_ref[:, gs], dy_ref[:, gs]
            b_g = xbc_ref[:, DI + STATE * g:DI + STATE * (g + 1)].astype(BF16)
            c_g = xbc_ref[:, DI + 512 + STATE * g:DI + 512 + STATE * (g + 1)].astype(BF16)
            m_w, l_w, dt_g, ecs_g, dec_g, cd_g = _ssd_group(g, cs, csT, dt, _nt(c_g, b_g), causal_w, lo, blocks_ref, pairs_ref)
            m_b = m_w.astype(BF16)
            xdt = xs_g * dt_g
            xdt_b, dy_b = xdt.astype(BF16), dy_g.astype(BF16)
            dms, dxs = [], []
            for jj in range(4):
                xp, dyp = xdt_b[:, 128 * jj:128 * (jj + 1)], dy_b[:, 128 * jj:128 * (jj + 1)]
                dy_ab = jnp.concatenate([jnp.where(lo, dyp, jnp.zeros_like(dyp)), jnp.where(hi, dyp, jnp.zeros_like(dyp))], axis=0)
                dm_ab = _nt(dy_ab, xp)
                dms += [dm_ab[:Q], dm_ab[Q:]]
                dx_ab = _tn(m_b[:, 256 * jj:256 * (jj + 1)], dyp)
                dxs.append(jnp.where(lo, dx_ab[:Q], dx_ab[Q:]))
            dm_w = jnp.concatenate(dms, axis=1)
            w_w = dm_w * m_w
            dcs_c = dcs_c + _spread(w_w, block_sum_ref[g], 2)
            w_cols = jnp.sum(w_w, axis=0, keepdims=True)
            for hh in range(8):
                dcs_r = dcs_r + jnp.where(sub32 == 8 * g + hh, w_cols[:, 128 * hh:128 * (hh + 1)], 0.0)
            dl_w = dm_w * l_w
            ds_mat = dl_w[:, 0:128]
            for hh in range(1, 8):
                ds_mat = ds_mat + dl_w[:, 128 * hh:128 * (hh + 1)]
            hin_g = hin_ref[0, hs, :]
            hin_b = hin_g.astype(BF16)
            dh_g = dh_scr[hs, :]
            dh_b = dh_g.astype(BF16)
            g_mat = _nn(b_g, dh_b)
            xdec = xdt * dec_g
            xg = xdec * g_mat
            dxdt = jnp.concatenate(dxs, axis=1) + dec_g * g_mat
            sums = _spread(jnp.concatenate([dy_g * (_nn(c_g, hin_b) * ecs_g) - xg, dxdt * xs_g], axis=0), pair_sum_ref[g], 2)
            dcs_c = dcs_c + sums[:Q]
            ddt_x = ddt_x + sums[Q:]
            last = jnp.sum(xg, axis=0, keepdims=True) + jnp.sum(dh_g * hin_g, axis=0, keepdims=True) * cd_g
            dcs_l = dcs_l + _spread(jnp.broadcast_to(last, (8, 512)), pair_sum_ref[g], 2)
            dz = (dy_g * ecs_g).astype(BF16)
            ds_b = ds_mat.astype(BF16)
            dxa_ref[:, gs] = dxdt * dt_g + dy_g * dsk_ref[:, gs]
            dxa_ref[:, DI + STATE * g:DI + STATE * (g + 1)] = _nt(xdec.astype(BF16), dh_b) + _tn(ds_b, c_g)
            dxa_ref[:, DI + 512 + STATE * g:DI + 512 + STATE * (g + 1)] = _nt(dz, hin_b) + _nn(ds_b, b_g)
            dh_scr[hs, :] = _tn(c_g, dz) + dh_g * cd_g
            dsk_sum_ref[0:1, gs] += jnp.sum(dy_g * xs_g, axis=0, keepdims=True)

        rows = lax.broadcasted_iota(jnp.int32, (Q, Q), 0)
        cols = lax.broadcasted_iota(jnp.int32, (Q, Q), 1)
        tri_t = (cols >= rows).astype(F32)
        last_row = lax.broadcasted_iota(jnp.int32, (Q, 1), 0) == Q - 1
        dcs = (dcs_c + jnp.where(last_row, dcs_l[0:1, :], 0.0))[:, 0:HEADS]
        da = _nn(tri_t, dcs, precision=HIGH) - _nt(tri_t, dcs_r, precision=HIGH)
        ddt_raw = (ddt_x[:, 0:HEADS] + da * a_r) * _sigmoid(raw)
        small_ref[0:1, :] += jnp.sum(da * dt, axis=0, keepdims=True) * a_r
        small_ref[1:2, :] += jnp.sum(ddt_raw, axis=0, keepdims=True)
        dp_ref[...] = jnp.zeros_like(dp_ref)
        dp_ref[:, 0:HEADS] = ddt_raw.astype(BF16)

    rev = lambda c: nc - 1 - c
    small_r = pl.BlockSpec((1, HEADS), lambda c: (0, 0))
    small_c = pl.BlockSpec((HEADS, 1), lambda c: (0, 0))
    selectors = _ssd_selectors()
    whole = lambda a: pl.BlockSpec(a.shape, lambda c: (0,) * a.ndim)
    return _call(
        body, [dy, xbc_a, proj, hin, dtb_r, alog_r, dtb_c, alog_c, dsk_exp, dproj, *selectors], name="ssd_bwd", grid=(nc,),
        in_specs=[pl.BlockSpec((Q, DI), lambda c: (rev(c), 0)),
                  pl.BlockSpec((Q, XBC), lambda c: (rev(c), 0)),
                  pl.BlockSpec((Q, 128), lambda c: (rev(c), OFF_DT // 128)),
                  pl.BlockSpec((1, 512, 512), lambda c: (rev(c), 0, 0)),
                  small_r, small_r, small_c, small_c,
                  pl.BlockSpec((1, DI), lambda c: (0, 0)),
                  pl.BlockSpec(memory_space=pl.ANY)] + [whole(a) for a in selectors],
        out_specs=[pl.BlockSpec((Q, XBC), lambda c: (rev(c), 0)),
                   pl.BlockSpec((Q, 256), lambda c: (rev(c), OFF_DT // 256)),
                   pl.BlockSpec((8, DI), lambda c: (0, 0)),
                   pl.BlockSpec((8, HEADS), lambda c: (0, 0))],
        out_shape=[jax.ShapeDtypeStruct((T, XBC), F32), jax.ShapeDtypeStruct(dproj.shape, BF16),
                   jax.ShapeDtypeStruct((8, DI), F32), jax.ShapeDtypeStruct((8, HEADS), F32)],
        aliases={9: 1}, scratch_shapes=[pltpu.VMEM((512, 512), F32)], sem=("arbitrary",), carry=carry)


def _gate_norm(y, proj, w):
    T = y.shape[0]
    tr = _row_tile(T)

    def body(y_ref, z_ref, w_ref, o_ref):
        for g in range(GROUPS):
            gs = slice(512 * g, 512 * (g + 1))
            z = z_ref[:, gs]
            yg = y_ref[:, gs] * (z * _sigmoid(z))
            r = lax.rsqrt(jnp.mean(yg * yg, axis=-1, keepdims=True) + EPS)
            o_ref[:, gs] = (yg * r * w_ref[:, gs]).astype(BF16)

    tile = pl.BlockSpec((tr, DI), lambda i: (i, 0))
    return _call(body, [y, proj, w], name="gate_norm", grid=(T // tr,),
                 in_specs=[tile, tile, pl.BlockSpec((1, DI), lambda i: (0, 0))], out_specs=[tile],
                 out_shape=[jax.ShapeDtypeStruct((T, DI), BF16)], sem=("parallel",))[0]


def _gate_norm_bwd(dyn, y, proj, w, dproj):
    T = y.shape[0]
    tr = _row_tile(T)

    def body(d_ref, y_ref, z_ref, w_ref, dp_in, dy_ref, dz_ref, sums_ref):
        del dp_in

        @pl.when(pl.program_id(0) == 0)
        def _():
            sums_ref[...] = jnp.zeros_like(sums_ref)

        for g in range(GROUPS):
            gs = slice(512 * g, 512 * (g + 1))
            z, yv, d = z_ref[:, gs], y_ref[:, gs], d_ref[:, gs]
            s = _sigmoid(z)
            silu = z * s
            yg = yv * silu
            r = lax.rsqrt(jnp.mean(yg * yg, axis=-1, keepdims=True) + EPS)
            yn = yg * r
            sums_ref[0:1, gs] += jnp.sum(d * yn, axis=0, keepdims=True)
            dn = d * w_ref[:, gs]
            dyg = r * (dn - yn * jnp.mean(dn * yn, axis=-1, keepdims=True))
            dy_ref[:, gs] = dyg * silu
            dz_ref[:, gs] = (dyg * yv * (s * (1.0 + z * (1.0 - s)))).astype(BF16)

    tile = pl.BlockSpec((tr, DI), lambda i: (i, 0))
    return _call(
        body, [dyn, y, proj, w, dproj], name="gate_norm_bwd", grid=(T // tr,),
        in_specs=[tile, tile, tile, pl.BlockSpec((1, DI), lambda i: (0, 0)), pl.BlockSpec(memory_space=pl.ANY)],
        out_specs=[tile, tile, pl.BlockSpec((8, DI), lambda i: (0, 0))],
        out_shape=[jax.ShapeDtypeStruct((T, DI), F32), jax.ShapeDtypeStruct(dproj.shape, BF16),
                   jax.ShapeDtypeStruct((8, DI), F32)],
        aliases={4: 1}, sem=("arbitrary",))


def _pool_fwd(proj, pool_w_b, pool_scale):
    T = proj.shape[0]
    tr = _row_tile(T)
    nb = tr // 16

    def body(u_ref, h_ref, pw_ref, ps_ref, pooled_ref, pw_out_ref, yps_ref):
        i = pl.program_id(0)
        t = i * tr + lax.broadcasted_iota(jnp.int32, (tr, 1), 0)
        for g, win in enumerate(POOL_WINDOWS):
            gs = slice(GW * g, GW * (g + 1))
            u = u_ref[:, gs]
            s = jnp.concatenate([jnp.where(i > 0, h_ref[:, gs], 0.0), u], axis=0)
            sh = 1
            while sh < win:
                s = s + pltpu.roll(s, sh, 0)
                sh *= 2
            pooled = (s[16:] * (1.0 / jnp.minimum(t + 1, win).astype(F32)) - u).astype(BF16)
            pooled_ref[:, gs] = pooled
            pwv = _nn(pooled, pw_ref[g])
            pw_out_ref[:, gs] = pwv
            yps_ref[:, gs] = (pwv * ps_ref[:, gs]).astype(BF16)

    tile = pl.BlockSpec((tr, D), lambda i: (i, 0))
    return _call(
        body, [proj, proj, pool_w_b, pool_scale], name="pool_fwd", grid=(T // tr,),
        in_specs=[pl.BlockSpec((tr, D), lambda i: (i, OFF_POOL // D)),
                  pl.BlockSpec((16, D), lambda i: (jnp.maximum(i * nb - 1, 0), OFF_POOL // D)),
                  pl.BlockSpec((4, GW, GW), lambda i: (0, 0, 0)),
                  pl.BlockSpec((1, D), lambda i: (0, 0))],
        out_specs=[tile, tile, tile],
        out_shape=[jax.ShapeDtypeStruct((T, D), BF16), jax.ShapeDtypeStruct((T, D), F32),
                   jax.ShapeDtypeStruct((T, D), BF16)], sem=("parallel",))


def _pool_bwd(dyp, pw_out, pooled, pool_w_b, pool_scale, dproj):
    T = dyp.shape[0]
    tr = _row_tile(T)
    nb, last = tr // 16, T // tr - 1

    def body(d_ref, h_ref, pwo_ref, pooled_ref, pw_ref, ps_ref, dp_in, du_ref, gpw_ref, sums_ref):
        del dp_in
        i = pl.program_id(0)

        @pl.when(i == 0)
        def _():
            gpw_ref[...] = jnp.zeros_like(gpw_ref)
            sums_ref[...] = jnp.zeros_like(sums_ref)

        n = tr + 16
        t = i * tr + lax.broadcasted_iota(jnp.int32, (n, 1), 0)
        sums_ref[0:1, :] += jnp.sum(d_ref[...] * pwo_ref[...], axis=0, keepdims=True)
        for g, win in enumerate(POOL_WINDOWS):
            gs = slice(GW * g, GW * (g + 1))
            d_ext = jnp.concatenate([d_ref[:, gs], jnp.where(i < last, h_ref[:, gs], 0.0)], axis=0)
            dpw = (d_ext * ps_ref[:, gs]).astype(BF16)
            dpooled = _nt(dpw, pw_ref[g])
            s = jnp.where(t < T, dpooled * (1.0 / jnp.minimum(t + 1, win).astype(F32)), 0.0)
            sh = 1
            while sh < win:
                s = s + pltpu.roll(s, n - sh, 0)
                sh *= 2
            du_ref[:, gs] = (s[:tr] - dpooled[:tr]).astype(BF16)
            gpw_ref[g] += _tn(pooled_ref[:, gs], dpw[:tr])

    tile = pl.BlockSpec((tr, D), lambda i: (i, 0))
    return _call(
        body, [dyp, dyp, pw_out, pooled, pool_w_b, pool_scale, dproj], name="pool_bwd", grid=(T // tr,),
        in_specs=[tile, pl.BlockSpec((16, D), lambda i: (jnp.minimum((i + 1) * nb, T // 16 - 1), 0)), tile, tile,
                  pl.BlockSpec((4, GW, GW), lambda i: (0, 0, 0)), pl.BlockSpec((1, D), lambda i: (0, 0)),
                  pl.BlockSpec(memory_space=pl.ANY)],
        out_specs=[pl.BlockSpec((tr, D), lambda i: (i, OFF_POOL // D)),
                   pl.BlockSpec((4, GW, GW), lambda i: (0, 0, 0)), pl.BlockSpec((8, D), lambda i: (0, 0))],
        out_shape=[jax.ShapeDtypeStruct(dproj.shape, BF16), jax.ShapeDtypeStruct((4, GW, GW), F32),
                   jax.ShapeDtypeStruct((8, D), F32)],
        aliases={6: 0}, sem=("arbitrary",))


def _merge(proj, y_ssd, y_pool):
    T = proj.shape[0]
    tr = _row_tile(T)

    def body(g_ref, a_ref, b_ref, o_ref):
        o_ref[...] = (_sigmoid(g_ref[:, 0:D]) * a_ref[...] + _sigmoid(g_ref[:, D:2 * D]) * b_ref[...]).astype(BF16)

    tile = pl.BlockSpec((tr, D), lambda i: (i, 0))
    return _call(body, [proj, y_ssd, y_pool], name="merge", grid=(T // tr,),
                 in_specs=[pl.BlockSpec((tr, 2 * D), lambda i: (i, OFF_GATE // (2 * D))), tile, tile], out_specs=[tile],
                 out_shape=[jax.ShapeDtypeStruct((T, D), BF16)], sem=("parallel",))[0]


def _merge_bwd(dmerged, proj, y_ssd, y_pool):
    T = proj.shape[0]
    tr = _row_tile(T)

    def body(d_ref, g_ref, a_ref, b_ref, da_ref, db_ref, dg_ref):
        d = d_ref[...]
        ga, gb = _sigmoid(g_ref[:, 0:D]), _sigmoid(g_ref[:, D:2 * D])
        da_ref[...] = (d * ga).astype(BF16)
        db_ref[...] = (d * gb).astype(BF16)
        dg_ref[:, 0:D] = (d * a_ref[...] * ga * (1.0 - ga)).astype(BF16)
        dg_ref[:, D:2 * D] = (d * b_ref[...] * gb * (1.0 - gb)).astype(BF16)

    tile = pl.BlockSpec((tr, D), lambda i: (i, 0))
    gates = pl.BlockSpec((tr, 2 * D), lambda i: (i, OFF_GATE // (2 * D)))
    return _call(body, [dmerged, proj, y_ssd, y_pool], name="merge_bwd", grid=(T // tr,),
                 in_specs=[tile, gates, tile, tile], out_specs=[tile, tile, gates],
                 out_shape=[jax.ShapeDtypeStruct((T, D), BF16), jax.ShapeDtypeStruct((T, D), BF16),
                            jax.ShapeDtypeStruct((T, NP), BF16)], sem=("parallel",))


def _adamw(w, g, m, v, name, carry=None):
    R, C = w.shape
    tr = R if R <= 128 else 128
    assert R % tr == 0

    def body(w_ref, g_ref, m_ref, v_ref, d_ref, mo_ref, vo_ref):
        gv = g_ref[...]
        mn = ADAM_B1 * m_ref[...] + (1.0 - ADAM_B1) * gv
        vn = ADAM_B2 * v_ref[...] + (1.0 - ADAM_B2) * (gv * gv)
        m_hat = mn / (1.0 - ADAM_B1 ** ADAM_STEP)
        v_hat = vn / (1.0 - ADAM_B2 ** ADAM_STEP)
        d_ref[...] = -ADAM_LR * (m_hat / (jnp.sqrt(v_hat) + ADAM_EPS) + ADAM_WD * w_ref[...])
        mo_ref[...] = mn
        vo_ref[...] = vn

    tile = pl.BlockSpec((tr, C), lambda i: (i, 0))
    sds = jax.ShapeDtypeStruct((R, C), F32)
    return _call(body, [w, g, m, v], name=name, grid=(R // tr,), in_specs=[tile] * 4, out_specs=[tile] * 3,
                 out_shape=[sds] * 3, sem=("parallel",), carry=carry)


def _me():
    return lax.axis_index("x"), lax.axis_index("y"), lax.axis_index("c")


def _xor_peer(x, y, c, p):
    return (x ^ ((p >> 2) & 1), y ^ ((p >> 1) & 1), c ^ (p & 1))


def _ada_fwd(c_row, w_ada, b_ada_mine):
    n_cols = w_ada.shape[1]

    def body(c_ref, w_ref, b_ref, mod_ref, c8_ref, csend, mpart, modbuf, send_sems, recv_sems):
        x, y, c = _me()
        me = 4 * x + 2 * y + c
        chip = 2 * x + y
        csend[...] = jnp.broadcast_to(c_ref[...], csend.shape)
        c8_ref[me] = csend[...]

        def c_copy(p):
            return pltpu.make_async_remote_copy(
                src_ref=csend, dst_ref=c8_ref.at[me], send_sem=send_sems.at[p - 1], recv_sem=recv_sems.at[p - 1],
                device_id=_xor_peer(x, y, c, p), device_id_type=MESH)

        for p in range(1, 8):
            c_copy(p).start()
        for p in range(1, 8):
            c_copy(p).wait_recv()
        cs = jnp.concatenate([c8_ref[d][0:1, :] for d in range(8)], axis=0)
        mpart[...] = _nn(cs * _sigmoid(cs), w_ref[...], precision=HIGH) + b_ref[...]
        modbuf[chip] = mpart[...]

        def m_copy(m):
            return pltpu.make_async_remote_copy(
                src_ref=mpart, dst_ref=modbuf.at[chip], send_sem=send_sems.at[6 + m], recv_sem=recv_sems.at[6 + m],
                device_id=_xor_peer(x, y, c, 2 * m), device_id_type=MESH)

        for m in range(1, 4):
            m_copy(m).start()
        for m in range(1, 4):
            m_copy(m).wait_recv()
        mine = lax.broadcasted_iota(jnp.int32, (8, 1), 0) == me
        for k in range(N_CHIPS):
            mod_ref[:, n_cols * k:n_cols * (k + 1)] = jnp.sum(jnp.where(mine, modbuf[k], 0.0), axis=0, keepdims=True)
        for p in range(1, 8):
            c_copy(p).wait_send()
        for m in range(1, 4):
            m_copy(m).wait_send()

    vmem = pl.BlockSpec(memory_space=pltpu.VMEM)
    return _call(
        body, [c_row, w_ada, b_ada_mine], name="ada_fwd", in_specs=[vmem, vmem, vmem], out_specs=[vmem, vmem],
        out_shape=[jax.ShapeDtypeStruct((1, N_CHIPS * n_cols), F32), jax.ShapeDtypeStruct((8, 8, D), F32)],
        scratch_shapes=[pltpu.VMEM((8, D), F32), pltpu.VMEM((8, n_cols), F32), pltpu.VMEM((N_CHIPS, 8, n_cols), F32),
                        pltpu.SemaphoreType.DMA((10,)), pltpu.SemaphoreType.DMA((10,))])


def _gather_small(vec):
    rows = vec.shape[0]

    def body(v_ref, all_ref, tot_ref, dsk_ref, send_sems, recv_sems):
        x, y, c = _me()
        me = 4 * x + 2 * y + c
        all_ref[me] = v_ref[...]

        def copy(p):
            return pltpu.make_async_remote_copy(
                src_ref=v_ref, dst_ref=all_ref.at[me], send_sem=send_sems.at[p - 1], recv_sem=recv_sems.at[p - 1],
                device_id=_xor_peer(x, y, c, p), device_id_type=MESH)

        for p in range(1, 8):
            copy(p).start()
        for p in range(1, 8):
            copy(p).wait_recv()
        tot = all_ref[0]
        for d in range(1, 8):
            tot = tot + all_ref[d]
        tot_ref[...] = tot
        seg = tot[SMALL_OFF["d_skip"] // 128:SMALL_OFF["d_skip"] // 128 + 16, :]
        lane = lax.broadcasted_iota(jnp.int32, (1, 128), 1)
        sa = jnp.sum(jnp.where(lane < HEAD_DIM, seg, 0.0), axis=1, keepdims=True)
        sb = jnp.sum(jnp.where(lane < HEAD_DIM, 0.0, seg), axis=1, keepdims=True)
        dsk_ref[...] = jnp.where(lane == 0, sa, jnp.where(lane == 1, sb, 0.0))
        for p in range(1, 8):
            copy(p).wait_send()

    vmem = pl.BlockSpec(memory_space=pltpu.VMEM)
    return _call(
        body, [vec], name="gather_small", in_specs=[vmem], out_specs=[vmem, vmem, vmem],
        out_shape=[jax.ShapeDtypeStruct((8, rows, 128), F32), jax.ShapeDtypeStruct((rows, 128), F32),
                   jax.ShapeDtypeStruct((16, 128), F32)],
        scratch_shapes=[pltpu.SemaphoreType.DMA((7,)), pltpu.SemaphoreType.DMA((7,))])


def _gather_carry(shards):
    n = len(shards)

    def copies(ins, outs, sems):
        x, y, c = _me()
        chip = 2 * x + y

        def half(w, which):
            h = shards[w].shape[0] // 2
            return pl.ds(which * h, h)

        def first(w, m):
            return pltpu.make_async_remote_copy(
                src_ref=ins[w].at[half(w, c)], dst_ref=outs[w].at[chip, half(w, c)],
                send_sem=sems.send(6 * w + m - 1), recv_sem=sems.recv(6 * w + m - 1),
                device_id=_xor_peer(x, y, c, 2 * m), device_id_type=MESH)

        def landed(w, m):
            return pltpu.make_async_remote_copy(
                src_ref=ins[w].at[half(w, c)], dst_ref=outs[w].at[chip ^ m, half(w, c)],
                send_sem=sems.send(6 * w + m - 1), recv_sem=sems.recv(6 * w + m - 1),
                device_id=_xor_peer(x, y, c, 2 * m), device_id_type=MESH)

        def passed(w, m, which):
            part = outs[w].at[chip ^ m, half(w, which)]
            return pltpu.make_async_remote_copy(
                src_ref=part, dst_ref=part, send_sem=sems.send(6 * w + 2 + m), recv_sem=sems.recv(6 * w + 2 + m),
                device_id=(x, y, 1 - c), device_id_type=MESH)

        return c, first, landed, passed

    pairs = [(w, m) for w in range(n) for m in range(1, 4)]

    def start(ins, outs, sems):
        _, first, _, _ = copies(ins, outs, sems)
        for w, m in pairs:
            first(w, m).start()

    def finish(ins, outs, sems):
        c, first, landed, passed = copies(ins, outs, sems)
        for w, m in pairs:
            landed(w, m).wait_recv()
            passed(w, m, c).start()
        for w, m in pairs:
            passed(w, m, 1 - c).wait_recv()
        for w, m in pairs:
            first(w, m).wait_send()
            passed(w, m, c).wait_send()

    return _Carry(shards, [jax.ShapeDtypeStruct((N_CHIPS,) + s.shape, s.dtype) for s in shards], 6 * n, start, finish)


def _pair_exchange_carry(grads):
    n = len(grads)

    def copy(ins, outs, sems, w):
        x, y, c = _me()
        h = grads[w].shape[1] // 2
        return pltpu.make_async_remote_copy(
            src_ref=ins[w].at[:, pl.ds((1 - c) * h, h)], dst_ref=outs[w],
            send_sem=sems.send(w), recv_sem=sems.recv(w), device_id=(x, y, 1 - c), device_id_type=MESH)

    def start(ins, outs, sems):
        for w in range(n):
            copy(ins, outs, sems, w).start()

    def finish(ins, outs, sems):
        for w in range(n):
            copy(ins, outs, sems, w).wait()

    return _Carry(grads, [jax.ShapeDtypeStruct((N_CHIPS, g.shape[1] // 2, g.shape[2]), g.dtype) for g in grads], n,
                  start, finish)


def _chip_exchange_carry(partials):
    n = len(partials)

    def copier(ins, outs, sems):
        x, y, c = _me()
        chip = 2 * x + y

        def copy(w, m, landed):
            return pltpu.make_async_remote_copy(
                src_ref=ins[w].at[chip ^ m], dst_ref=outs[w].at[(chip ^ m) if landed else chip],
                send_sem=sems.send(3 * w + m - 1), recv_sem=sems.recv(3 * w + m - 1),
                device_id=_xor_peer(x, y, c, 2 * m), device_id_type=MESH)

        return copy

    pairs = [(w, m) for w in range(n) for m in range(1, 4)]

    def start(ins, outs, sems):
        copy = copier(ins, outs, sems)
        for w, m in pairs:
            copy(w, m, False).start()

    def finish(ins, outs, sems):
        copy = copier(ins, outs, sems)
        for w, m in pairs:
            copy(w, m, True).wait_recv()
        for w, m in pairs:
            copy(w, m, False).wait_send()

    return _Carry(partials, [jax.ShapeDtypeStruct(p.shape, p.dtype) for p in partials], 3 * n, start, finish)


def _pair_share_carry(shards):
    n = len(shards)

    def copier(ins, outs, sems):
        x, y, c = _me()

        def copy(w, which):
            h = shards[w].shape[0] // 2
            rows = pl.ds(which * h, h)
            return pltpu.make_async_remote_copy(
                src_ref=ins[w].at[rows], dst_ref=outs[w].at[rows],
                send_sem=sems.send(w), recv_sem=sems.recv(w), device_id=(x, y, 1 - c), device_id_type=MESH)

        return c, copy

    def start(ins, outs, sems):
        c, copy = copier(ins, outs, sems)
        for w in range(n):
            copy(w, c).start()

    def finish(ins, outs, sems):
        c, copy = copier(ins, outs, sems)
        for w in range(n):
            copy(w, 1 - c).wait_recv()
        for w in range(n):
            copy(w, c).wait_send()

    return _Carry(shards, [jax.ShapeDtypeStruct(s.shape, s.dtype) for s in shards], n, start, finish,
                  aliased=[(w, w) for w in range(n)])


def _pair_sum(g, part, idx, name):
    _, h, C = part.shape
    tr = min(128, h)
    nb = h // tr

    def body(idx_ref, g_ref, p_ref, o16_ref, own_ref):
        v = g_ref[...] + p_ref[...]
        o16_ref[...] = v.astype(BF16)

        @pl.when(pl.program_id(1) == idx_ref[1])
        def _():
            own_ref[...] = v

    return pl.pallas_call(
        body, name=name,
        grid_spec=pltpu.PrefetchScalarGridSpec(
            num_scalar_prefetch=1, grid=(nb, N_CHIPS),
            in_specs=[pl.BlockSpec((None, tr, C), lambda i, s, idx_ref: (s, idx_ref[0] * nb + i, 0)),
                      pl.BlockSpec((None, tr, C), lambda i, s, idx_ref: (s, i, 0))],
            out_specs=[pl.BlockSpec((None, tr, C), lambda i, s, idx_ref: (s, i, 0)),
                       pl.BlockSpec((tr, C), lambda i, s, idx_ref: (i, 0))]),
        out_shape=[jax.ShapeDtypeStruct(part.shape, BF16), jax.ShapeDtypeStruct((h, C), F32)],
        compiler_params=pltpu.CompilerParams(dimension_semantics=("arbitrary", "arbitrary"), vmem_limit_bytes=VMEM_LIMIT),
    )(idx, g, part)


def _chip_sum(own, slots, idx, name):
    h, C = own.shape
    tr = min(128, h)
    nb = h // tr

    def body(idx_ref, own_ref, s1_ref, s2_ref, s3_ref, o_ref):
        del idx_ref
        o_ref[...] = ((own_ref[...] + s1_ref[...].astype(F32)) + s2_ref[...].astype(F32)) + s3_ref[...].astype(F32)

    def slot(m):
        return pl.BlockSpec((None, tr, C), lambda i, idx_ref: (idx_ref[1] ^ m, i, 0))

    return pl.pallas_call(
        body, name=name,
        grid_spec=pltpu.PrefetchScalarGridSpec(
            num_scalar_prefetch=1, grid=(nb,),
            in_specs=[pl.BlockSpec((tr, C), lambda i, idx_ref: (i, 0)), slot(1), slot(2), slot(3)],
            out_specs=pl.BlockSpec((tr, C), lambda i, idx_ref: (idx_ref[0] * nb + i, 0))),
        out_shape=jax.ShapeDtypeStruct((2 * h, C), F32),
        compiler_params=pltpu.CompilerParams(dimension_semantics=("parallel",), vmem_limit_bytes=VMEM_LIMIT),
    )(idx, own, slots, slots, slots)


class _Reducer:
    def __init__(self, idx):
        self.idx, self.chips, self.p16, self.own, self.mine, self.final = idx, {}, {}, {}, {}, {}

    def add(self, name, whole, chip_blocks=False):
        self.chips[name] = whole if chip_blocks else _chips_from_whole(name, whole)

    def pair(self, names):
        return _pair_exchange_carry([self.chips[n] for n in names])

    def take_pair(self, names, outs):
        for n, part in zip(names, outs):
            self.p16[n], self.own[n] = _pair_sum(self.chips.pop(n), part, self.idx, "pair_sum_" + n)

    def chip(self, names):
        return _chip_exchange_carry([self.p16[n] for n in names])

    def take_chip(self, names, outs):
        for n, slots in zip(names, outs):
            del self.p16[n]
            self.mine[n] = _chip_sum(self.own.pop(n), slots, self.idx, "chip_sum_" + n)

    def share(self, names):
        return _pair_share_carry([self.mine[n] for n in names])

    def take_share(self, names, outs):
        for n, s in zip(names, outs):
            del self.mine[n]
            self.final[n] = s


def _w_ada_grad(c8, dmod_cols):
    n_cols = dmod_cols.shape[1]
    tn = 512

    def body(c_ref, d_ref, o_ref):
        cv = c_ref[...]
        o_ref[...] = _tn(cv * _sigmoid(cv), d_ref[...], precision=HIGH)

    return _call(body, [c8, dmod_cols], name="w_ada_grad", grid=(n_cols // tn,),
                 in_specs=[pl.BlockSpec((8, D), lambda j: (0, 0)), pl.BlockSpec((8, tn), lambda j: (0, j))],
                 out_specs=[pl.BlockSpec((D, tn), lambda j: (0, j))],
                 out_shape=[jax.ShapeDtypeStruct((D, n_cols), F32)], sem=("parallel",))[0]


_SMALL_SEGS = (("dmod", 6144), ("norm_mix_w", 1024), ("conv_b", 3072), ("ssd_norm_w", 2048), ("pool_scale", 1024),
               ("norm_mlp_w", 1024), ("norm_final_w", 1024), ("conv_w", 4 * XBC), ("d_skip", 2048), ("a_log", 128),
               ("dt_bias", 128), ("loss", 128))
SMALL_OFF = {}
_o = 0
for _n, _s in _SMALL_SEGS:
    SMALL_OFF[_n] = _o
    _o += _s
SMALL_LEN = -(-_o // 1024) * 1024

_FIRST = ("w_in", "conv_w")
_LATER = ("w_branch_ssd", "pool_w", "w_branch_pool", "w_out", "w_up", "w_down")
_SMALL_REPLICATED = ("b_ada", "norm_mix_w", "conv_b", "dt_bias", "a_log", "d_skip", "ssd_norm_w", "pool_scale",
                     "norm_mlp_w", "norm_final_w")
_WEIGHTS = ("w_ada", "b_ada", "norm_mix_w", "w_in", "conv_w", "conv_b", "dt_bias", "a_log", "d_skip", "ssd_norm_w",
            "w_branch_ssd", "pool_w", "pool_scale", "w_branch_pool", "w_out", "norm_mlp_w", "w_up", "w_down",
            "norm_final_w")


def _shard_2d(name, a):
    if name == "conv_w":
        return a.reshape(16, -1)
    return (a.reshape(GW, GW) if name == "pool_w" else a.reshape(a.shape[-2], a.shape[-1])).astype(BF16)


def _whole_from_chips(name, g, own, chip):
    g = lax.dynamic_update_slice(g, own[None], (chip, 0, 0))
    if name == "w_in":
        return _perm_cols(jnp.transpose(g, (1, 0, 2)).reshape(D, IN_COLS))
    if name == "w_up":
        return jnp.transpose(g, (1, 0, 2)).reshape(D, DFF)
    if name == "pool_w":
        return jnp.transpose(g.reshape(N_CHIPS, 4, GW // N_CHIPS, GW), (1, 0, 2, 3)).reshape(4, GW, GW)
    if name == "conv_w":
        return jnp.transpose(g.reshape(N_CHIPS, 4, XBC // N_CHIPS), (1, 0, 2)).reshape(4, XBC)
    return g.reshape(N_CHIPS * g.shape[1], g.shape[2])


def _chips_from_whole(name, g):
    if name.startswith("w_in"):
        return jnp.transpose(_unperm_cols(g).reshape(g.shape[0], N_CHIPS, IN_COLS // N_CHIPS), (1, 0, 2))
    if name == "w_up":
        return jnp.transpose(g.reshape(D, N_CHIPS, DFF // N_CHIPS), (1, 0, 2))
    if name == "pool_w":
        return jnp.transpose(g.reshape(4, N_CHIPS, GW // N_CHIPS, GW), (1, 0, 2, 3)).reshape(N_CHIPS, GW, GW)
    return g.reshape(N_CHIPS, g.shape[0] // N_CHIPS, g.shape[1])


def kernel(x, c, w_ada, b_ada, norm_mix_w, w_in, conv_w, conv_b, dt_bias, a_log, d_skip, ssd_norm_w, w_branch_ssd, pool_w, pool_scale, w_branch_pool, w_out, norm_mlp_w, w_up, w_down, norm_final_w, loss_target, m_w_ada, m_b_ada, m_norm_mix_w, m_w_in, m_conv_w, m_conv_b, m_dt_bias, m_a_log, m_d_skip, m_ssd_norm_w, m_w_branch_ssd, m_pool_w, m_pool_scale, m_w_branch_pool, m_w_out, m_norm_mlp_w, m_w_up, m_w_down, m_norm_final_w, v_w_ada, v_b_ada, v_norm_mix_w, v_w_in, v_conv_w, v_conv_b, v_dt_bias, v_a_log, v_d_skip, v_ssd_norm_w, v_w_branch_ssd, v_pool_w, v_pool_scale, v_w_branch_pool, v_w_out, v_norm_mlp_w, v_w_up, v_w_down, v_norm_final_w):
    args = locals()
    w = {n: args[n] for n in _WEIGHTS}
    m = {n: args["m_" + n] for n in _WEIGHTS}
    v = {n: args["v_" + n] for n in _WEIGHTS}
    xi, yi, ci = _me()
    chip = 2 * xi + yi
    idx = jnp.stack([ci, chip]).astype(jnp.int32)
    ada_cols = w_ada.shape[-1]
    xs, target = x[0], loss_target[0]
    two_d = lambda n, a: a.reshape(GW, GW) if n == "pool_w" else a.reshape(-1, a.shape[-1])
    delta, new_m, new_v, g = {}, {}, {}, {}

    def adamw(n, carry=None):
        res = _adamw(two_d(n, w[n]), two_d(n, g[n]), two_d(n, m[n]), two_d(n, v[n]), "adamw_" + n, carry=carry)
        (delta[n], new_m[n], new_v[n]), extra = res if carry is not None else (res, None)
        return extra

    b_mine = lax.dynamic_slice(b_ada, (0, chip * ada_cols), (1, ada_cols))
    mod, c8 = _ada_fwd(c, w_ada[0], b_mine)
    c8 = c8[:, 0, :]
    shift_m, scale_m, gate_m, shift_f, scale_f, gate_f = [mod[:, D * i:D * (i + 1)] for i in range(6)]
    shards = {n: _shard_2d(n, w[n]) for n in _FIRST + _LATER}
    nf_w = norm_final_w.reshape(1, D)

    h1, first = _norm_mod(xs, norm_mix_w, scale_m, shift_m, "norm_mod_mix",
                          carry=_gather_carry([shards[n] for n in _FIRST]))
    p = {n: _whole_from_chips(n, a, shards[n], chip) for n, a in zip(_FIRST, first)}
    (proj,), later = _matmul(h1, p["w_in"], mode="nn", out_dtypes=[F32], name="mm_proj",
                             carry=_gather_carry([shards[n] for n in _LATER]))
    p.update({n: _whole_from_chips(n, a, shards[n], chip) for n, a in zip(_LATER, later)})
    xbc_a = _conv_fwd(proj, p["conv_w"], conv_b)
    dtb_c, alog_c = dt_bias.reshape(HEADS, 1), a_log.reshape(HEADS, 1)
    dsk_exp = jnp.repeat(d_skip, HEAD_DIM, axis=1)
    y, hin = _ssd_fwd(xbc_a, proj, dt_bias, a_log, dtb_c, alog_c, dsk_exp)
    yn = _gate_norm(y, proj, ssd_norm_w)
    (y_ssd,) = _matmul(yn, p["w_branch_ssd"], mode="nn", out_dtypes=[F32], name="mm_branch_ssd")
    pooled, pw_out, yps = _pool_fwd(proj, p["pool_w"], pool_scale)
    (y_pool,) = _matmul(yps, p["w_branch_pool"], mode="nn", out_dtypes=[F32], name="mm_branch_pool")
    merged = _merge(proj, y_ssd, y_pool)
    resid = lambda acc, r, gt: (r + gt * acc, acc)
    x2, mix = _matmul(merged, p["w_out"], mode="nn", out_dtypes=[F32, F32], name="mm_out",
                      epi=resid, tile_extras=(xs,), row_extras=(gate_m,))
    h2 = _norm_mod(x2, norm_mlp_w, scale_f, shift_f, "norm_mod_mlp")
    relu2 = lambda acc: (acc, jnp.square(jnp.maximum(acc, 0.0)))
    up, act = _matmul(h2, p["w_up"], mode="nn", out_dtypes=[F32, BF16], name="mm_up", epi=relu2)
    x3, down = _matmul(act, p["w_down"], mode="nn", out_dtypes=[F32, F32], name="mm_down",
                       epi=resid, tile_extras=(x2,), row_extras=(gate_f,))

    red = _Reducer(idx)
    dx3, d_down, sums_f = _final_loss_bwd(x3, target, nf_w, down, gate_f)
    drelu2 = lambda acc, u: (acc * (2.0 * jnp.maximum(u, 0.0)),)
    (dup,) = _matmul(d_down, p["w_down"], mode="nt", out_dtypes=[BF16], name="mm_dact",
                     epi=drelu2, tile_extras=(up,))
    red.add("w_down", _matmul(act, d_down, mode="tn", out_dtypes=[F32], name="mm_g_down")[0])
    (dh2,), got = _matmul(dup, p["w_up"], mode="nt", out_dtypes=[F32], name="mm_dh2",
                          carry=red.pair(["w_down"]))
    red.take_pair(["w_down"], got)
    red.add("w_up", _matmul(h2, dup, mode="tn", out_dtypes=[F32], name="mm_g_up", chip_blocks=True)[0], chip_blocks=True)
    dx2, sums_2, dmix = _norm_mod_bwd(x2, dh2, dx3, norm_mlp_w, scale_f, "norm_mod_mlp_bwd", branch=mix, gate=gate_m)
    (dmerged,), got = _matmul(dmix, p["w_out"], mode="nt", out_dtypes=[F32], name="mm_dmerged",
                              carry=red.pair(["w_up"]))
    red.take_pair(["w_up"], got)
    red.add("w_out", _matmul(merged, dmix, mode="tn", out_dtypes=[F32], name="mm_g_out")[0])
    dy_ssd, dy_pool, dproj = _merge_bwd(dmerged, proj, y_ssd, y_pool)
    (dyp,), got = _matmul(dy_pool, p["w_branch_pool"], mode="nt", out_dtypes=[F32], name="mm_dyp",
                          carry=red.pair(["w_out"]))
    red.take_pair(["w_out"], got)
    red.add("w_branch_pool", _matmul(yps, dy_pool, mode="tn", out_dtypes=[F32], name="mm_g_bpool")[0])
    dproj, g_pool_w, sums_pool = _pool_bwd(dyp, pw_out, pooled, p["pool_w"], pool_scale, dproj)
    red.add("pool_w", g_pool_w)
    red.add("w_branch_ssd", _matmul(yn, dy_ssd, mode="tn", out_dtypes=[F32], name="mm_g_bssd")[0])
    mixers = ["w_branch_pool", "pool_w", "w_branch_ssd"]
    (dyn,), got = _matmul(dy_ssd, p["w_branch_ssd"], mode="nt", out_dtypes=[F32], name="mm_dyn",
                          carry=red.pair(mixers))
    red.take_pair(mixers, got)
    dy, dproj, sums_gn = _gate_norm_bwd(dyn, y, proj, ssd_norm_w, dproj)
    six = ["w_down", "w_up", "w_out"] + mixers
    (dxa, dproj, dsk_sum, ssd_small), got = _ssd_bwd(dy, xbc_a, proj, hin, dt_bias, a_log, dtb_c, alog_c, dsk_exp,
                                                     dproj, carry=red.chip(six))
    red.take_chip(six, got)
    dxc, sums_conv = _conv_bwd_a(dxa, proj, p["conv_w"], conv_b)
    dproj = _conv_bwd_b(dxc, p["conv_w"], dproj)
    rows_a = 3 * D // 4
    (g_in_a,), got = _matmul(h1, dproj, mode="tn", out_dtypes=[F32], name="mm_g_in_a", a_cols=(0, rows_a),
                             carry=red.share(six))
    red.take_share(six, got)
    red.add("w_in_a", g_in_a)
    (g_in_b,), got = _matmul(h1, dproj, mode="tn", out_dtypes=[F32], name="mm_g_in_b", a_cols=(rows_a, D - rows_a),
                             carry=red.pair(["w_in_a"]))
    red.take_pair(["w_in_a"], got)
    red.add("w_in_b", g_in_b)
    (dh1,), got = _matmul(dproj, p["w_in"], mode="nt", out_dtypes=[F32], name="mm_dh1",
                          carry=_join(red.chip(["w_in_a"]), red.pair(["w_in_b"])))
    red.take_chip(["w_in_a"], got[:1])
    red.take_pair(["w_in_b"], got[1:])
    (grad_x, sums_1), got = _norm_mod_bwd(xs, dh1, dx2, norm_mix_w, scale_m, "norm_mod_mix_bwd",
                                          carry=_join(red.chip(["w_in_b"]), red.share(["w_in_a"])))
    red.take_chip(["w_in_b"], got[:1])
    red.take_share(["w_in_a"], got[1:])

    dmod = jnp.concatenate([sums_1[0:1], sums_1[1:2], sums_2[3:4], sums_2[0:1], sums_2[1:2], sums_f[1:2]], axis=1)
    pad96 = jnp.zeros((1, 96), F32)
    small = {"dmod": dmod, "norm_mix_w": sums_1[2:3], "conv_b": sums_conv[4:5], "ssd_norm_w": sums_gn[0:1],
             "pool_scale": sums_pool[0:1], "norm_mlp_w": sums_2[2:3], "norm_final_w": sums_f[0:1],
             "conv_w": sums_conv[0:4].reshape(1, 4 * XBC), "d_skip": dsk_sum[0:1],
             "a_log": jnp.concatenate([ssd_small[0:1], pad96], axis=1),
             "dt_bias": jnp.concatenate([ssd_small[1:2], pad96], axis=1), "loss": sums_f[3:4, 0:128]}
    vec = jnp.concatenate([small[n] for n, _ in _SMALL_SEGS], axis=1)
    vec = jnp.pad(vec, ((0, 0), (0, SMALL_LEN - vec.shape[1]))).reshape(SMALL_LEN // 128, 128)
    every, total, dsk = _gather_small(vec)
    total = total.reshape(1, SMALL_LEN)
    seg = lambda n, size: total[:, SMALL_OFF[n]:SMALL_OFF[n] + size]
    g.update({"b_ada": seg("dmod", 6 * D), "norm_mix_w": seg("norm_mix_w", D), "conv_b": seg("conv_b", XBC),
              "dt_bias": seg("dt_bias", HEADS), "a_log": seg("a_log", HEADS), "d_skip": dsk[:, 0:2].reshape(1, HEADS),
              "ssd_norm_w": seg("ssd_norm_w", DI), "pool_scale": seg("pool_scale", D),
              "norm_mlp_w": seg("norm_mlp_w", D), "norm_final_w": seg("norm_final_w", D)})
    loss = total[0, SMALL_OFF["loss"]]
    conv_cols = conv_w.shape[-1]
    g["conv_w"] = lax.dynamic_slice(seg("conv_w", 4 * XBC).reshape(4, XBC), (0, chip * conv_cols), (4, conv_cols))
    dmod8 = every.reshape(8, SMALL_LEN)[:, SMALL_OFF["dmod"]:SMALL_OFF["dmod"] + 6 * D]
    g["w_ada"] = _w_ada_grad(c8, lax.dynamic_slice(dmod8, (0, chip * ada_cols), (8, ada_cols)))

    got = adamw("w_ada", carry=red.share(["w_in_b"]))
    red.take_share(["w_in_b"], got)
    for n in six:
        g[n] = red.final[n]
    g["w_in"] = jnp.concatenate([red.final["w_in_a"], red.final["w_in_b"]], axis=0)
    for n in ["conv_w", "w_in"] + six:
        adamw(n)
    sizes = [w[n].size for n in _SMALL_REPLICATED]
    n_small = -(-sum(sizes) // 1024) * 1024
    pack = lambda d: jnp.pad(jnp.concatenate([d[n].reshape(1, -1) for n in _SMALL_REPLICATED], axis=1),
                             ((0, 0), (0, n_small - sum(sizes)))).reshape(n_small // 128, 128)
    d_, m_, v_ = _adamw(pack(w), pack(g), pack(m), pack(v), "adamw_small")
    off = 0
    for n, s in zip(_SMALL_REPLICATED, sizes):
        for dst, src in ((delta, d_), (new_m, m_), (new_v, v_)):
            dst[n] = src.reshape(1, n_small)[:, off:off + s]
        off += s

    out = [loss, grad_x.reshape(x.shape)]
    for d in (g, delta, new_m, new_v):
        out += [d[n].reshape(w[n].shape) for n in _WEIGHTS]
    return tuple(out)
```

```python
import functools
import operator

import jax
import jax.numpy as jnp
import numpy as np
from jax import lax
from jax.experimental import pallas as pl
from jax.experimental.pallas import tpu as pltpu

F32, BF16 = jnp.float32, jnp.bfloat16
HIGH = lax.Precision.HIGHEST
MESH = pl.DeviceIdType.MESH

D = 1024
DI = 2048
HEADS, HEAD_DIM = 32, 64
GROUPS, STATE = 4, 128
Q = 128
XBC = DI + 2 * GROUPS * STATE
POOL_WINDOWS = (2, 4, 8, 16)
GW = 256
DFF = 4096
EPS = 1e-5
IN_COLS = 8224
OFF_Z, OFF_XBC, OFF_POOL, OFF_GATE, OFF_DT, NP = 0, 2048, 5120, 6144, 8192, 8448
N_CHIPS = 4
ADAM_LR, ADAM_B1, ADAM_B2, ADAM_EPS, ADAM_WD, ADAM_STEP = 0.001, 0.9, 0.999, 1e-08, 0.01, 10
VMEM_LIMIT = 56 * 2 ** 20
NEG = -1e30


def _sigmoid(v):
    return 0.5 * jnp.tanh(0.5 * v) + 0.5


def _softplus(v):
    return jnp.maximum(v, 0.0) + jnp.log1p(jnp.exp(-jnp.abs(v)))


def _dot(a, b, dims, **kw):
    return lax.dot_general(a, b, (dims, ((), ())), preferred_element_type=F32, **kw)


def _nn(a, b, **kw):
    return _dot(a, b, ((1,), (0,)), **kw)


def _nt(a, b, **kw):
    return _dot(a, b, ((1,), (1,)), **kw)


def _tn(a, b, **kw):
    return _dot(a, b, ((0,), (0,)), **kw)


def _perm_cols(w):
    pad = jnp.zeros(w.shape[:-1] + (NP - IN_COLS,), w.dtype)
    return jnp.concatenate([w[..., :5120], w[..., 5152:], w[..., 5120:5152], pad], axis=-1)


def _unperm_cols(g):
    return jnp.concatenate([g[..., :5120], g[..., OFF_DT:OFF_DT + 32], g[..., 5120:OFF_DT]], axis=-1)


class _Sems:
    def __init__(self, send, recv, local, base=0):
        self._send, self._recv, self._local, self._base = send, recv, local, base

    def shift(self, n):
        return _Sems(self._send, self._recv, self._local, self._base + n)

    def send(self, i):
        return self._send.at[self._base + i]

    def recv(self, i):
        return self._recv.at[self._base + i]

    def local(self, i):
        return self._local.at[self._base + i]


class _Carry:
    def __init__(self, ins, out_shapes, n_sems, start, finish, aliased=()):
        self.ins, self.out_shapes, self.n_sems, self.start, self.finish = list(ins), list(out_shapes), n_sems, start, finish
        self.aliased = list(aliased)


def _join(*carries):
    def run(which):
        def fn(ins, outs, sems):
            i = o = s = 0
            for cy in carries:
                getattr(cy, which)(ins[i:i + len(cy.ins)], outs[o:o + len(cy.out_shapes)], sems.shift(s))
                i, o, s = i + len(cy.ins), o + len(cy.out_shapes), s + cy.n_sems
        return fn

    aliased, i, o = [], 0, 0
    for cy in carries:
        aliased += [(i + a, o + b) for a, b in cy.aliased]
        i, o = i + len(cy.ins), o + len(cy.out_shapes)
    return _Carry([a for cy in carries for a in cy.ins], [a for cy in carries for a in cy.out_shapes],
                  sum(cy.n_sems for cy in carries), run("start"), run("finish"), aliased)


def _call(body, args, *, name, grid=(), in_specs, out_specs, out_shape, scratch_shapes=(), sem=None, aliases=None,
          carry=None):
    in_specs, out_specs, out_shape, scratch_shapes = list(in_specs), list(out_specs), list(out_shape), list(scratch_shapes)
    n_in, n_out, n_scr = len(in_specs), len(out_specs), len(scratch_shapes)
    kw = {"vmem_limit_bytes": VMEM_LIMIT}
    if carry is None:
        kernel_fn = functools.partial(body)
        if sem is not None:
            kw["dimension_semantics"] = sem
    else:
        n_ci, n_co = len(carry.ins), len(carry.out_shapes)
        hbm = pl.BlockSpec(memory_space=pl.ANY)
        in_specs += [hbm] * n_ci
        out_specs += [hbm] * n_co
        out_shape += carry.out_shapes
        n_s = max(carry.n_sems, 1)
        scratch_shapes += [pltpu.SemaphoreType.DMA((n_s,))] * 3
        args = list(args) + carry.ins
        aliases = dict(aliases or {})
        aliases.update({n_in + i: n_out + o for i, o in carry.aliased})
        if grid:
            kw["dimension_semantics"] = ("arbitrary",) * len(grid)

        def kernel_fn(*refs):
            a = n_in
            ins, c_ins = refs[:a], refs[a:a + n_ci]
            a += n_ci
            outs, c_outs = refs[a:a + n_out], refs[a + n_out:a + n_out + n_co]
            a += n_out + n_co
            scr, sems = refs[a:a + n_scr], _Sems(*refs[a + n_scr:a + n_scr + 3])
            if grid:
                ids = [pl.program_id(d) for d in range(len(grid))]
                first = functools.reduce(operator.and_, [i == 0 for i in ids])
                last = functools.reduce(operator.and_, [i == g - 1 for i, g in zip(ids, grid)])

                @pl.when(first)
                def _():
                    carry.start(c_ins, c_outs, sems)

                body(*ins, *outs, *scr)

                @pl.when(last)
                def _():
                    carry.finish(c_ins, c_outs, sems)
            else:
                carry.start(c_ins, c_outs, sems)
                body(*ins, *outs, *scr)
                carry.finish(c_ins, c_outs, sems)

    outs = pl.pallas_call(
        kernel_fn, name=name, grid=grid, in_specs=in_specs, out_specs=out_specs, out_shape=out_shape,
        scratch_shapes=scratch_shapes, input_output_aliases=aliases or {},
        compiler_params=pltpu.CompilerParams(**kw),
    )(*args)
    outs = list(outs)
    return outs if carry is None else (outs[:n_out], outs[n_out:])


def _run_carry(carry, name):
    _, outs = _call(lambda: None, [], name=name, in_specs=[], out_specs=[], out_shape=[], carry=carry)
    return outs


_TILES = {
    "mm_proj": (1024, 2816, 1024), "mm_branch_ssd": (1024, 1024, 2048), "mm_branch_pool": (1024, 1024, 1024),
    "mm_out": (1024, 1024, 1024), "mm_up": (1024, 1024, 1024), "mm_down": (512, 1024, 4096),
    "mm_dact": (1024, 1024, 1024), "mm_g_down": (1024, 1024, 2048), "mm_dh2": (1024, 1024, 4096),
    "mm_g_up": (1024, 1024, 2048), "mm_dmerged": (1024, 1024, 1024), "mm_g_out": (1024, 1024, 2048),
    "mm_dyp": (1024, 1024, 1024), "mm_g_bpool": (1024, 1024, 2048), "mm_g_bssd": (1024, 1024, 2048),
    "mm_dyn": (1024, 1024, 1024), "mm_g_in_a": (768, 1408, 2048), "mm_g_in_b": (256, 2816, 2048),
    "mm_dh1": (1024, 1024, 2816),
}


def _matmul(a, b, *, mode, out_dtypes, name, epi=None, tile_extras=(), row_extras=(), carry=None, a_cols=None,
            chip_blocks=False):
    M, K = (a.shape[1], a.shape[0]) if mode == "tn" else a.shape
    N = b.shape[0] if mode == "nt" else b.shape[1]
    a_start, M = a_cols if a_cols is not None else (0, M)
    tm, tn, tk = _TILES[name]
    tm, tn, tk = min(tm, M), min(tn, N), min(tk, K)
    assert M % tm == 0 and N % tn == 0 and K % tk == 0 and a_start % tm == 0, (name, M, N, K, tm, tn, tk)
    a_off = a_start // tm
    if mode == "nn":
        a_spec = pl.BlockSpec((tm, tk), lambda i, j, k: (i, k))
        b_spec = pl.BlockSpec((tk, tn), lambda i, j, k: (k, j))
        dims = ((1,), (0,))
    elif mode == "nt":
        a_spec = pl.BlockSpec((tm, tk), lambda i, j, k: (i, k))
        b_spec = pl.BlockSpec((tn, tk), lambda i, j, k: (j, k))
        dims = ((1,), (1,))
    else:
        a_spec = pl.BlockSpec((tk, tm), lambda i, j, k: (k, i + a_off))
        b_spec = pl.BlockSpec((tk, tn), lambda i, j, k: (k, j))
        dims = ((0,), (0,))
    nk = K // tk
    n_te, n_re, n_out = len(tile_extras), len(row_extras), len(out_dtypes)
    if epi is None:
        epi = lambda acc: (acc,)

    def body(a_ref, b_ref, *rest):
        extras = rest[:n_te + n_re]
        outs = rest[n_te + n_re:n_te + n_re + n_out]
        p = _dot(a_ref[...], b_ref[...], dims)

        def finish(acc):
            vals = epi(acc, *[e[...] for e in extras])
            for o, v in zip(outs, vals):
                o[...] = v.astype(o.dtype)

        if nk == 1:
            finish(p)
        else:
            acc_ref = rest[-1]
            k = pl.program_id(2)

            @pl.when(k == 0)
            def _():
                acc_ref[...] = p

            @pl.when(k > 0)
            def _():
                acc_ref[...] += p

            @pl.when(k == nk - 1)
            def _():
                finish(acc_ref[...])

    tile_spec = pl.BlockSpec((tm, tn), lambda i, j, k: (i, j))
    row_spec = pl.BlockSpec((1, tn), lambda i, j, k: (0, j))
    out_spec, out_dims = tile_spec, (M, N)
    if chip_blocks:
        assert n_te == 0 and tn * N_CHIPS == N
        out_spec, out_dims = pl.BlockSpec((None, tm, tn), lambda i, j, k: (j, i, 0)), (N_CHIPS, M, tn)
    return _call(
        body, [a, b, *tile_extras, *row_extras], name=name, grid=(M // tm, N // tn, nk),
        in_specs=[a_spec, b_spec] + [tile_spec] * n_te + [row_spec] * n_re, out_specs=[out_spec] * n_out,
        out_shape=[jax.ShapeDtypeStruct(out_dims, dt) for dt in out_dtypes],
        scratch_shapes=[pltpu.VMEM((tm, tn), F32)] if nk > 1 else [],
        sem=("parallel", "parallel", "arbitrary"), carry=carry)


def _row_tile(T):
    return min(512, T)


def _norm_mod(x, nw, scale, shift, name, carry=None):
    T = x.shape[0]
    tr = _row_tile(T)

    def body(x_ref, nw_ref, sc_ref, sh_ref, o_ref):
        xv = x_ref[...]
        r = lax.rsqrt(jnp.mean(xv * xv, axis=-1, keepdims=True) + EPS)
        o_ref[...] = ((xv * r) * nw_ref[...] * (1.0 + sc_ref[...]) + sh_ref[...]).astype(BF16)

    tile = pl.BlockSpec((tr, D), lambda i: (i, 0))
    row = pl.BlockSpec((1, D), lambda i: (0, 0))
    res = _call(body, [x, nw, scale, shift], name=name, grid=(T // tr,), in_specs=[tile, row, row, row],
                out_specs=[tile], out_shape=[jax.ShapeDtypeStruct((T, D), BF16)], sem=("parallel",), carry=carry)
    return res[0] if carry is None else (res[0][0], res[1])


def _norm_mod_bwd(x, dh, dres, nw, scale, name, branch=None, gate=None, carry=None):
    T = x.shape[0]
    tr = _row_tile(T)
    with_branch = branch is not None

    def body(x_ref, dh_ref, dr_ref, nw_ref, sc_ref, *rest):
        if with_branch:
            br_ref, g_ref, dx_ref, sums_ref, db_ref = rest
        else:
            dx_ref, sums_ref = rest
        i = pl.program_id(0)

        @pl.when(i == 0)
        def _():
            sums_ref[...] = jnp.zeros_like(sums_ref)

        xv, dhv = x_ref[...], dh_ref[...]
        r = lax.rsqrt(jnp.mean(xv * xv, axis=-1, keepdims=True) + EPS)
        xn = xv * r
        g1 = dhv * (1.0 + sc_ref[...])
        dxn = g1 * nw_ref[...]
        dx = dr_ref[...] + r * (dxn - xn * jnp.mean(dxn * xn, axis=-1, keepdims=True))
        dx_ref[...] = dx
        sums_ref[0:1, :] += jnp.sum(dhv, axis=0, keepdims=True)
        sums_ref[1:2, :] += jnp.sum(dhv * (xn * nw_ref[...]), axis=0, keepdims=True)
        sums_ref[2:3, :] += jnp.sum(g1 * xn, axis=0, keepdims=True)
        if with_branch:
            db_ref[...] = (dx * g_ref[...]).astype(BF16)
            sums_ref[3:4, :] += jnp.sum(dx * br_ref[...], axis=0, keepdims=True)

    tile = pl.BlockSpec((tr, D), lambda i: (i, 0))
    row = pl.BlockSpec((1, D), lambda i: (0, 0))
    sums = pl.BlockSpec((8, D), lambda i: (0, 0))
    ins = [x, dh, dres, nw, scale] + ([branch, gate] if with_branch else [])
    in_specs = [tile, tile, tile, row, row] + ([tile, row] if with_branch else [])
    out_specs = [tile, sums] + ([tile] if with_branch else [])
    out_shape = [jax.ShapeDtypeStruct((T, D), F32), jax.ShapeDtypeStruct((8, D), F32)]
    if with_branch:
        out_shape.append(jax.ShapeDtypeStruct((T, D), BF16))
    return _call(body, ins, name=name, grid=(T // tr,), in_specs=in_specs, out_specs=out_specs, out_shape=out_shape,
                 sem=("arbitrary",), carry=carry)


def _final_loss_bwd(x3, target, wf, down, gate_f):
    T = x3.shape[0]
    tr = _row_tile(T)
    n_steps = T // tr

    def body(x_ref, t_ref, w_ref, dn_ref, g_ref, dx_ref, dd_ref, sums_ref):
        i = pl.program_id(0)

        @pl.when(i == 0)
        def _():
            sums_ref[...] = jnp.zeros_like(sums_ref)

        xv = x_ref[...]
        r = lax.rsqrt(jnp.mean(xv * xv, axis=-1, keepdims=True) + EPS)
        xn = xv * r
        err = xn * w_ref[...] - t_ref[...]
        dy = err * (1.0 / D)
        dxn = dy * w_ref[...]
        dx = r * (dxn - xn * jnp.mean(dxn * xn, axis=-1, keepdims=True))
        dx_ref[...] = dx
        dd_ref[...] = (dx * g_ref[...]).astype(BF16)
        sums_ref[0:1, :] += jnp.sum(dy * xn, axis=0, keepdims=True)
        sums_ref[1:2, :] += jnp.sum(dx * dn_ref[...], axis=0, keepdims=True)
        sums_ref[2:3, :] += jnp.sum(err * err, axis=0, keepdims=True) * (0.5 / D)

        @pl.when(i == n_steps - 1)
        def _():
            sums_ref[3:4, :] = jnp.broadcast_to(jnp.sum(sums_ref[2:3, :], axis=1, keepdims=True), (1, D))

    tile = pl.BlockSpec((tr, D), lambda i: (i, 0))
    row = pl.BlockSpec((1, D), lambda i: (0, 0))
    sums = pl.BlockSpec((8, D), lambda i: (0, 0))
    return _call(body, [x3, target, wf, down, gate_f], name="final_loss_bwd", grid=(n_steps,),
                 in_specs=[tile, tile, row, tile, row], out_specs=[tile, tile, sums],
                 out_shape=[jax.ShapeDtypeStruct((T, D), F32), jax.ShapeDtypeStruct((T, D), BF16),
                            jax.ShapeDtypeStruct((8, D), F32)], sem=("arbitrary",))


CONV_TC = 1024


def _conv_taps(xp, w, b):
    acc = b + w[3:4, :] * xp
    for k in range(3):
        acc = acc + w[k:k + 1, :] * pltpu.roll(xp, 3 - k, 0)
    return acc


def _conv_fwd(proj, conv_w, conv_b):
    T = proj.shape[0]
    tr = _row_tile(T)
    nb, offb = tr // 8, OFF_XBC // CONV_TC

    def body(x_ref, h_ref, w_ref, b_ref, o_ref):
        halo = jnp.where(pl.program_id(0) > 0, h_ref[...], 0.0)
        xp = jnp.concatenate([halo, x_ref[...]], axis=0)
        acc = _conv_taps(xp, w_ref[...], b_ref[...])[8:]
        o_ref[...] = acc * _sigmoid(acc)

    return _call(
        body, [proj, proj, conv_w, conv_b], name="conv_fwd", grid=(T // tr, XBC // CONV_TC),
        in_specs=[pl.BlockSpec((tr, CONV_TC), lambda i, j: (i, j + offb)),
                  pl.BlockSpec((8, CONV_TC), lambda i, j: (jnp.maximum(i * nb - 1, 0), j + offb)),
                  pl.BlockSpec((4, CONV_TC), lambda i, j: (0, j)),
                  pl.BlockSpec((1, CONV_TC), lambda i, j: (0, j))],
        out_specs=[pl.BlockSpec((tr, CONV_TC), lambda i, j: (i, j))],
        out_shape=[jax.ShapeDtypeStruct((T, XBC), F32)], sem=("parallel", "parallel"))[0]


def _conv_bwd_a(dxa, proj, conv_w, conv_b):
    T = proj.shape[0]
    tr = _row_tile(T)
    nb, offb = tr // 8, OFF_XBC // CONV_TC

    def body(d_ref, x_ref, h_ref, w_ref, b_ref, o_ref, sums_ref):
        i = pl.program_id(1)

        @pl.when(i == 0)
        def _():
            sums_ref[...] = jnp.zeros_like(sums_ref)

        halo = jnp.where(i > 0, h_ref[...], 0.0)
        xp = jnp.concatenate([halo, x_ref[...]], axis=0)
        acc = _conv_taps(xp, w_ref[...], b_ref[...])[8:]
        s = _sigmoid(acc)
        dxc = d_ref[...] * (s * (1.0 + acc * (1.0 - s)))
        o_ref[...] = dxc
        sums_ref[3:4, :] += jnp.sum(dxc * x_ref[...], axis=0, keepdims=True)
        for k in range(3):
            sums_ref[k:k + 1, :] += jnp.sum(dxc * pltpu.roll(xp, 3 - k, 0)[8:], axis=0, keepdims=True)
        sums_ref[4:5, :] += jnp.sum(dxc, axis=0, keepdims=True)

    return _call(
        body, [dxa, proj, proj, conv_w, conv_b], name="conv_bwd_a", grid=(XBC // CONV_TC, T // tr),
        in_specs=[pl.BlockSpec((tr, CONV_TC), lambda j, i: (i, j)),
                  pl.BlockSpec((tr, CONV_TC), lambda j, i: (i, j + offb)),
                  pl.BlockSpec((8, CONV_TC), lambda j, i: (jnp.maximum(i * nb - 1, 0), j + offb)),
                  pl.BlockSpec((4, CONV_TC), lambda j, i: (0, j)),
                  pl.BlockSpec((1, CONV_TC), lambda j, i: (0, j))],
        out_specs=[pl.BlockSpec((tr, CONV_TC), lambda j, i: (i, j)), pl.BlockSpec((8, CONV_TC), lambda j, i: (0, j))],
        out_shape=[jax.ShapeDtypeStruct((T, XBC), F32), jax.ShapeDtypeStruct((8, XBC), F32)],
        sem=("parallel", "arbitrary"))


def _conv_bwd_b(dxc, conv_w, dproj):
    T = dxc.shape[0]
    tr = _row_tile(T)
    nb, offb, last = tr // 8, OFF_XBC // CONV_TC, T // tr - 1

    def body(d_ref, h_ref, w_ref, dp_in, o_ref):
        del dp_in
        halo = jnp.where(pl.program_id(0) < last, h_ref[...], 0.0)
        xp = jnp.concatenate([d_ref[...], halo], axis=0)
        n = xp.shape[0]
        w = w_ref[...]
        acc = w[3:4, :] * xp
        for k in range(3):
            acc = acc + w[k:k + 1, :] * pltpu.roll(xp, n - (3 - k), 0)
        o_ref[...] = acc[:tr].astype(BF16)

    return _call(
        body, [dxc, dxc, conv_w, dproj], name="conv_bwd_b", grid=(T // tr, XBC // CONV_TC),
        in_specs=[pl.BlockSpec((tr, CONV_TC), lambda i, j: (i, j)),
                  pl.BlockSpec((8, CONV_TC), lambda i, j: (jnp.minimum((i + 1) * nb, T // 8 - 1), j)),
                  pl.BlockSpec((4, CONV_TC), lambda i, j: (0, j)),
                  pl.BlockSpec(memory_space=pl.ANY)],
        out_specs=[pl.BlockSpec((tr, CONV_TC), lambda i, j: (i, j + offb))],
        out_shape=[jax.ShapeDtypeStruct(dproj.shape, BF16)], aliases={3: 0}, sem=("parallel", "parallel"))[0]


def _spread(v, sel, pieces):
    out = None
    for _ in range(pieces):
        p = v.astype(BF16)
        term = _nn(p, sel)
        out = term if out is None else out + term
        v = v - p.astype(F32)
    return out


def _ssd_selectors():
    g = np.arange(GROUPS)[:, None, None]
    h = np.arange(HEADS)[None, :, None]
    blocks = (h == 8 * g + np.arange(1024)[None, None, :] // 128)
    pairs = (h == 8 * g + np.arange(512)[None, None, :] // HEAD_DIM)
    lane = np.arange(128)[None, None, :]
    block_sum = (lane == 8 * g + np.arange(1024)[None, :, None] // 128)
    pair_sum = (lane == 8 * g + np.arange(512)[None, :, None] // HEAD_DIM)
    return [jnp.asarray(m, BF16) for m in (blocks, pairs, block_sum, pair_sum)]


def _ssd_group(g, cs, csT, dt, s_mat, causal_w, lo, blocks_ref, pairs_ref):
    csb = _spread(cs, blocks_ref[g], 3)
    row = jnp.concatenate([csT[8 * g + hh:8 * g + hh + 1, :] for hh in range(8)], axis=1)
    l_w = jnp.exp(jnp.where(causal_w, csb - row, NEG))
    m_w = jnp.concatenate([s_mat] * 8, axis=1) * l_w
    cs_g = jnp.concatenate([jnp.where(lo, csb[:, 256 * jj:256 * jj + 128], csb[:, 256 * jj + 128:256 * jj + 256])
                            for jj in range(4)], axis=1)
    cs_last = cs_g[Q - 1:Q, :]
    return m_w, l_w, _spread(dt, pairs_ref[g], 2), jnp.exp(cs_g), jnp.exp(cs_last - cs_g), jnp.exp(cs_last)


def _ssd_common(dtp_ref, dtb_r, alog_r, dtb_c, alog_c):
    rows = lax.broadcasted_iota(jnp.int32, (Q, Q), 0)
    cols = lax.broadcasted_iota(jnp.int32, (Q, Q), 1)
    causal = cols <= rows
    tri = causal.astype(F32)
    raw = dtp_ref[:, 0:HEADS] + dtb_r[...]
    dt = _softplus(raw)
    a_r = -jnp.exp(alog_r[...])
    cs = _nn(tri, dt * a_r, precision=HIGH)
    aT = _softplus(dtp_ref[...].T[0:HEADS, :] + dtb_c[...]) * (-jnp.exp(alog_c[...]))
    csT = _nt(aT, tri, precision=HIGH)
    return causal, raw, dt, a_r, cs, csT


def _ssd_fwd(xbc_a, proj, dtb_r, alog_r, dtb_c, alog_c, dsk_exp):
    T = xbc_a.shape[0]
    nc = T // Q

    def body(xbc_ref, dtp_ref, dtb_r_ref, alog_r_ref, dtb_c_ref, alog_c_ref, dsk_ref, blocks_ref, pairs_ref,
             y_ref, hin_ref, h_scr):
        @pl.when(pl.program_id(0) == 0)
        def _():
            h_scr[...] = jnp.zeros_like(h_scr)

        _, _, dt, _, cs, csT = _ssd_common(dtp_ref, dtb_r_ref, alog_r_ref, dtb_c_ref, alog_c_ref)
        lo = lax.broadcasted_iota(jnp.int32, (1, 128), 1) < HEAD_DIM
        hi = jnp.logical_not(lo)
        causal_w = (lax.broadcasted_iota(jnp.int32, (Q, 1024), 1) & (Q - 1)) <= lax.broadcasted_iota(jnp.int32, (Q, 1024), 0)
        for g in range(GROUPS):
            gs = slice(512 * g, 512 * (g + 1))
            hs = slice(128 * g, 128 * (g + 1))
            xs_g = xbc_ref[:, gs]
            b_g = xbc_ref[:, DI + STATE * g:DI + STATE * (g + 1)].astype(BF16)
            c_g = xbc_ref[:, DI + 512 + STATE * g:DI + 512 + STATE * (g + 1)].astype(BF16)
            m_w, _, dt_g, ecs_g, dec_g, cd_g = _ssd_group(g, cs, csT, dt, _nt(c_g, b_g), causal_w, lo, blocks_ref, pairs_ref)
            m_b = m_w.astype(BF16)
            xdt = xs_g * dt_g
            xdt_b = xdt.astype(BF16)
            ys = []
            for jj in range(4):
                xp = xdt_b[:, 128 * jj:128 * (jj + 1)]
                x_ab = jnp.concatenate([jnp.where(lo, xp, jnp.zeros_like(xp)), jnp.where(hi, xp, jnp.zeros_like(xp))], axis=0)
                ys.append(_nn(m_b[:, 256 * jj:256 * (jj + 1)], x_ab))
            h_g = h_scr[hs, :]
            hin_ref[0, hs, :] = h_g
            y_ref[:, gs] = jnp.concatenate(ys, axis=1) + _nn(c_g, h_g.astype(BF16)) * ecs_g + dsk_ref[:, gs] * xs_g
            h_scr[hs, :] = h_g * cd_g + _tn(b_g, (xdt * dec_g).astype(BF16))

    small_r = pl.BlockSpec((1, HEADS), lambda c: (0, 0))
    small_c = pl.BlockSpec((HEADS, 1), lambda c: (0, 0))
    blocks, pairs, _, _ = _ssd_selectors()
    whole = lambda a: pl.BlockSpec(a.shape, lambda c: (0,) * a.ndim)
    return _call(
        body, [xbc_a, proj, dtb_r, alog_r, dtb_c, alog_c, dsk_exp, blocks, pairs], name="ssd_fwd", grid=(nc,),
        in_specs=[pl.BlockSpec((Q, XBC), lambda c: (c, 0)),
                  pl.BlockSpec((Q, 128), lambda c: (c, OFF_DT // 128)),
                  small_r, small_r, small_c, small_c,
                  pl.BlockSpec((1, DI), lambda c: (0, 0)), whole(blocks), whole(pairs)],
        out_specs=[pl.BlockSpec((Q, DI), lambda c: (c, 0)), pl.BlockSpec((1, 512, 512), lambda c: (c, 0, 0))],
        out_shape=[jax.ShapeDtypeStruct((T, DI), F32), jax.ShapeDtypeStruct((nc, 512, 512), F32)],
        scratch_shapes=[pltpu.VMEM((512, 512), F32)], sem=("arbitrary",))


def _ssd_bwd(dy, xbc_a, proj, hin, dtb_r, alog_r, dtb_c, alog_c, dsk_exp, dproj, carry=None):
    T = xbc_a.shape[0]
    nc = T // Q

    def body(dy_ref, xbc_ref, dtp_ref, hin_ref, dtb_r_ref, alog_r_ref, dtb_c_ref, alog_c_ref, dsk_ref, dp_in,
             blocks_ref, pairs_ref, block_sum_ref, pair_sum_ref, dxa_ref, dp_ref, dsk_sum_ref, small_ref, dh_scr):
        del dp_in

        @pl.when(pl.program_id(0) == 0)
        def _():
            dh_scr[...] = jnp.zeros_like(dh_scr)
            dsk_sum_ref[...] = jnp.zeros_like(dsk_sum_ref)
            small_ref[...] = jnp.zeros_like(small_ref)

        _, raw, dt, a_r, cs, csT = _ssd_common(dtp_ref, dtb_r_ref, alog_r_ref, dtb_c_ref, alog_c_ref)
        lo = lax.broadcasted_iota(jnp.int32, (1, 128), 1) < HEAD_DIM
        hi = jnp.logical_not(lo)
        sub32 = lax.broadcasted_iota(jnp.int32, (HEADS, 1), 0)
        causal_w = (lax.broadcasted_iota(jnp.int32, (Q, 1024), 1) & (Q - 1)) <= lax.broadcasted_iota(jnp.int32, (Q, 1024), 0)
        dcs_c = jnp.zeros((Q, 128), F32)
        dcs_r = jnp.zeros((HEADS, Q), F32)
        dcs_l = jnp.zeros((8, 128), F32)
        ddt_x = jnp.zeros((Q, 128), F32)
        for g in range(GROUPS):
            gs = slice(512 * g, 512 * (g + 1))
            hs = slice(128 * g, 128 * (g + 1))
            xs_g, dy_g = xbc_ref[:, gs], dy_ref[:, gs]
            b_g = xbc_ref[:, DI + STATE * g:DI + STATE * (g + 1)].astype(BF16)
            c_g = xbc_ref[:, DI + 512 + STATE * g:DI + 512 + STATE * (g + 1)].astype(BF16)
            m_w, l_w, dt_g, ecs_g, dec_g, cd_g = _ssd_group(g, cs, csT, dt, _nt(c_g, b_g), causal_w, lo, blocks_ref, pairs_ref)
            m_b = m_w.astype(BF16)
            xdt = xs_g * dt_g
            xdt_b, dy_b = xdt.astype(BF16), dy_g.astype(BF16)
            dms, dxs = [], []
            for jj in range(4):
                xp, dyp = xdt_b[:, 128 * jj:128 * (jj + 1)], dy_b[:, 128 * jj:128 * (jj + 1)]
                dy_ab = jnp.concatenate([jnp.where(lo, dyp, jnp.zeros_like(dyp)), jnp.where(hi, dyp, jnp.zeros_like(dyp))], axis=0)
                dm_ab = _nt(dy_ab, xp)
                dms += [dm_ab[:Q], dm_ab[Q:]]
                dx_ab = _tn(m_b[:, 256 * jj:256 * (jj + 1)], dyp)
                dxs.append(jnp.where(lo, dx_ab[:Q], dx_ab[Q:]))
            dm_w = jnp.concatenate(dms, axis=1)
            w_w = dm_w * m_w
            dcs_c = dcs_c + _spread(w_w, block_sum_ref[g], 2)
            w_cols = jnp.sum(w_w, axis=0, keepdims=True)
            for hh in range(8):
                dcs_r = dcs_r + jnp.where(sub32 == 8 * g + hh, w_cols[:, 128 * hh:128 * (hh + 1)], 0.0)
            dl_w = dm_w * l_w
            ds_mat = dl_w[:, 0:128]
            for hh in range(1, 8):
                ds_mat = ds_mat + dl_w[:, 128 * hh:128 * (hh + 1)]
            hin_g = hin_ref[0, hs, :]
            hin_b = hin_g.astype(BF16)
            dh_g = dh_scr[hs, :]
            dh_b = dh_g.astype(BF16)
            g_mat = _nn(b_g, dh_b)
            xdec = xdt * dec_g
            xg = xdec * g_mat
            dxdt = jnp.concatenate(dxs, axis=1) + dec_g * g_mat
            sums = _spread(jnp.concatenate([dy_g * (_nn(c_g, hin_b) * ecs_g) - xg, dxdt * xs_g], axis=0), pair_sum_ref[g], 2)
            dcs_c = dcs_c + sums[:Q]
            ddt_x = ddt_x + sums[Q:]
            last = jnp.sum(xg, axis=0, keepdims=True) + jnp.sum(dh_g * hin_g, axis=0, keepdims=True) * cd_g
            dcs_l = dcs_l + _spread(jnp.broadcast_to(last, (8, 512)), pair_sum_ref[g], 2)
            dz = (dy_g * ecs_g).astype(BF16)
            ds_b = ds_mat.astype(BF16)
            dxa_ref[:, gs] = dxdt * dt_g + dy_g * dsk_ref[:, gs]
            dxa_ref[:, DI + STATE * g:DI + STATE * (g + 1)] = _nt(xdec.astype(BF16), dh_b) + _tn(ds_b, c_g)
            dxa_ref[:, DI + 512 + STATE * g:DI + 512 + STATE * (g + 1)] = _nt(dz, hin_b) + _nn(ds_b, b_g)
            dh_scr[hs, :] = _tn(c_g, dz) + dh_g * cd_g
            dsk_sum_ref[0:1, gs] += jnp.sum(dy_g * xs_g, axis=0, keepdims=True)

        rows = lax.broadcasted_iota(jnp.int32, (Q, Q), 0)
        cols = lax.broadcasted_iota(jnp.int32, (Q, Q), 1)
        tri_t = (cols >= rows).astype(F32)
        last_row = lax.broadcasted_iota(jnp.int32, (Q, 1), 0) == Q - 1
        dcs = (dcs_c + jnp.where(last_row, dcs_l[0:1, :], 0.0))[:, 0:HEADS]
        da = _nn(tri_t, dcs, precision=HIGH) - _nt(tri_t, dcs_r, precision=HIGH)
        ddt_raw = (ddt_x[:, 0:HEADS] + da * a_r) * _sigmoid(raw)
        small_ref[0:1, :] += jnp.sum(da * dt, axis=0, keepdims=True) * a_r
        small_ref[1:2, :] += jnp.sum(ddt_raw, axis=0, keepdims=True)
        dp_ref[...] = jnp.zeros_like(dp_ref)
        dp_ref[:, 0:HEADS] = ddt_raw.astype(BF16)

    rev = lambda c: nc - 1 - c
    small_r = pl.BlockSpec((1, HEADS), lambda c: (0, 0))
    small_c = pl.BlockSpec((HEADS, 1), lambda c: (0, 0))
    selectors = _ssd_selectors()
    whole = lambda a: pl.BlockSpec(a.shape, lambda c: (0,) * a.ndim)
    return _call(
        body, [dy, xbc_a, proj, hin, dtb_r, alog_r, dtb_c, alog_c, dsk_exp, dproj, *selectors], name="ssd_bwd", grid=(nc,),
        in_specs=[pl.BlockSpec((Q, DI), lambda c: (rev(c), 0)),
                  pl.BlockSpec((Q, XBC), lambda c: (rev(c), 0)),
                  pl.BlockSpec((Q, 128), lambda c: (rev(c), OFF_DT // 128)),
                  pl.BlockSpec((1, 512, 512), lambda c: (rev(c), 0, 0)),
                  small_r, small_r, small_c, small_c,
                  pl.BlockSpec((1, DI), lambda c: (0, 0)),
                  pl.BlockSpec(memory_space=pl.ANY)] + [whole(a) for a in selectors],
        out_specs=[pl.BlockSpec((Q, XBC), lambda c: (rev(c), 0)),
                   pl.BlockSpec((Q, 256), lambda c: (rev(c), OFF_DT // 256)),
                   pl.BlockSpec((8, DI), lambda c: (0, 0)),
                   pl.BlockSpec((8, HEADS), lambda c: (0, 0))],
        out_shape=[jax.ShapeDtypeStruct((T, XBC), F32), jax.ShapeDtypeStruct(dproj.shape, BF16),
                   jax.ShapeDtypeStruct((8, DI), F32), jax.ShapeDtypeStruct((8, HEADS), F32)],
        aliases={9: 1}, scratch_shapes=[pltpu.VMEM((512, 512), F32)], sem=("arbitrary",), carry=carry)


def _gate_norm(y, proj, w):
    T = y.shape[0]
    tr = _row_tile(T)

    def body(y_ref, z_ref, w_ref, o_ref):
        for g in range(GROUPS):
            gs = slice(512 * g, 512 * (g + 1))
            z = z_ref[:, gs]
            yg = y_ref[:, gs] * (z * _sigmoid(z))
            r = lax.rsqrt(jnp.mean(yg * yg, axis=-1, keepdims=True) + EPS)
            o_ref[:, gs] = (yg * r * w_ref[:, gs]).astype(BF16)

    tile = pl.BlockSpec((tr, DI), lambda i: (i, 0))
    return _call(body, [y, proj, w], name="gate_norm", grid=(T // tr,),
                 in_specs=[tile, tile, pl.BlockSpec((1, DI), lambda i: (0, 0))], out_specs=[tile],
                 out_shape=[jax.ShapeDtypeStruct((T, DI), BF16)], sem=("parallel",))[0]


def _gate_norm_bwd(dyn, y, proj, w, dproj):
    T = y.shape[0]
    tr = _row_tile(T)

    def body(d_ref, y_ref, z_ref, w_ref, dp_in, dy_ref, dz_ref, sums_ref):
        del dp_in

        @pl.when(pl.program_id(0) == 0)
        def _():
            sums_ref[...] = jnp.zeros_like(sums_ref)

        for g in range(GROUPS):
            gs = slice(512 * g, 512 * (g + 1))
            z, yv, d = z_ref[:, gs], y_ref[:, gs], d_ref[:, gs]
            s = _sigmoid(z)
            silu = z * s
            yg = yv * silu
            r = lax.rsqrt(jnp.mean(yg * yg, axis=-1, keepdims=True) + EPS)
            yn = yg * r
            sums_ref[0:1, gs] += jnp.sum(d * yn, axis=0, keepdims=True)
            dn = d * w_ref[:, gs]
            dyg = r * (dn - yn * jnp.mean(dn * yn, axis=-1, keepdims=True))
            dy_ref[:, gs] = dyg * silu
            dz_ref[:, gs] = (dyg * yv * (s * (1.0 + z * (1.0 - s)))).astype(BF16)

    tile = pl.BlockSpec((tr, DI), lambda i: (i, 0))
    return _call(
        body, [dyn, y, proj, w, dproj], name="gate_norm_bwd", grid=(T // tr,),
        in_specs=[tile, tile, tile, pl.BlockSpec((1, DI), lambda i: (0, 0)), pl.BlockSpec(memory_space=pl.ANY)],
        out_specs=[tile, tile, pl.BlockSpec((8, DI), lambda i: (0, 0))],
        out_shape=[jax.ShapeDtypeStruct((T, DI), F32), jax.ShapeDtypeStruct(dproj.shape, BF16),
                   jax.ShapeDtypeStruct((8, DI), F32)],
        aliases={4: 1}, sem=("arbitrary",))


def _pool_fwd(proj, pool_w_b, pool_scale):
    T = proj.shape[0]
    tr = _row_tile(T)
    nb = tr // 16

    def body(u_ref, h_ref, pw_ref, ps_ref, pooled_ref, pw_out_ref, yps_ref):
        i = pl.program_id(0)
        t = i * tr + lax.broadcasted_iota(jnp.int32, (tr, 1), 0)
        for g, win in enumerate(POOL_WINDOWS):
            gs = slice(GW * g, GW * (g + 1))
            u = u_ref[:, gs]
            s = jnp.concatenate([jnp.where(i > 0, h_ref[:, gs], 0.0), u], axis=0)
            sh = 1
            while sh < win:
                s = s + pltpu.roll(s, sh, 0)
                sh *= 2
            pooled = (s[16:] * (1.0 / jnp.minimum(t + 1, win).astype(F32)) - u).astype(BF16)
            pooled_ref[:, gs] = pooled
            pwv = _nn(pooled, pw_ref[g])
            pw_out_ref[:, gs] = pwv
            yps_ref[:, gs] = (pwv * ps_ref[:, gs]).astype(BF16)

    tile = pl.BlockSpec((tr, D), lambda i: (i, 0))
    return _call(
        body, [proj, proj, pool_w_b, pool_scale], name="pool_fwd", grid=(T // tr,),
        in_specs=[pl.BlockSpec((tr, D), lambda i: (i, OFF_POOL // D)),
                  pl.BlockSpec((16, D), lambda i: (jnp.maximum(i * nb - 1, 0), OFF_POOL // D)),
                  pl.BlockSpec((4, GW, GW), lambda i: (0, 0, 0)),
                  pl.BlockSpec((1, D), lambda i: (0, 0))],
        out_specs=[tile, tile, tile],
        out_shape=[jax.ShapeDtypeStruct((T, D), BF16), jax.ShapeDtypeStruct((T, D), F32),
                   jax.ShapeDtypeStruct((T, D), BF16)], sem=("parallel",))


def _pool_bwd(dyp, pw_out, pooled, pool_w_b, pool_scale, dproj):
    T = dyp.shape[0]
    tr = _row_tile(T)
    nb, last = tr // 16, T // tr - 1

    def body(d_ref, h_ref, pwo_ref, pooled_ref, pw_ref, ps_ref, dp_in, du_ref, gpw_ref, sums_ref):
        del dp_in
        i = pl.program_id(0)

        @pl.when(i == 0)
        def _():
            gpw_ref[...] = jnp.zeros_like(gpw_ref)
            sums_ref[...] = jnp.zeros_like(sums_ref)

        n = tr + 16
        t = i * tr + lax.broadcasted_iota(jnp.int32, (n, 1), 0)
        sums_ref[0:1, :] += jnp.sum(d_ref[...] * pwo_ref[...], axis=0, keepdims=True)
        for g, win in enumerate(POOL_WINDOWS):
            gs = slice(GW * g, GW * (g + 1))
            d_ext = jnp.concatenate([d_ref[:, gs], jnp.where(i < last, h_ref[:, gs], 0.0)], axis=0)
            dpw = (d_ext * ps_ref[:, gs]).astype(BF16)
            dpooled = _nt(dpw, pw_ref[g])
            s = jnp.where(t < T, dpooled * (1.0 / jnp.minimum(t + 1, win).astype(F32)), 0.0)
            sh = 1
            while sh < win:
                s = s + pltpu.roll(s, n - sh, 0)
                sh *= 2
            du_ref[:, gs] = (s[:tr] - dpooled[:tr]).astype(BF16)
            gpw_ref[g] += _tn(pooled_ref[:, gs], dpw[:tr])

    tile = pl.BlockSpec((tr, D), lambda i: (i, 0))
    return _call(
        body, [dyp, dyp, pw_out, pooled, pool_w_b, pool_scale, dproj], name="pool_bwd", grid=(T // tr,),
        in_specs=[tile, pl.BlockSpec((16, D), lambda i: (jnp.minimum((i + 1) * nb, T // 16 - 1), 0)), tile, tile,
                  pl.BlockSpec((4, GW, GW), lambda i: (0, 0, 0)), pl.BlockSpec((1, D), lambda i: (0, 0)),
                  pl.BlockSpec(memory_space=pl.ANY)],
        out_specs=[pl.BlockSpec((tr, D), lambda i: (i, OFF_POOL // D)),
                   pl.BlockSpec((4, GW, GW), lambda i: (0, 0, 0)), pl.BlockSpec((8, D), lambda i: (0, 0))],
        out_shape=[jax.ShapeDtypeStruct(dproj.shape, BF16), jax.ShapeDtypeStruct((4, GW, GW), F32),
                   jax.ShapeDtypeStruct((8, D), F32)],
        aliases={6: 0}, sem=("arbitrary",))


def _merge(proj, y_ssd, y_pool):
    T = proj.shape[0]
    tr = _row_tile(T)

    def body(g_ref, a_ref, b_ref, o_ref):
        o_ref[...] = (_sigmoid(g_ref[:, 0:D]) * a_ref[...] + _sigmoid(g_ref[:, D:2 * D]) * b_ref[...]).astype(BF16)

    tile = pl.BlockSpec((tr, D), lambda i: (i, 0))
    return _call(body, [proj, y_ssd, y_pool], name="merge", grid=(T // tr,),
                 in_specs=[pl.BlockSpec((tr, 2 * D), lambda i: (i, OFF_GATE // (2 * D))), tile, tile], out_specs=[tile],
                 out_shape=[jax.ShapeDtypeStruct((T, D), BF16)], sem=("parallel",))[0]


def _merge_bwd(dmerged, proj, y_ssd, y_pool):
    T = proj.shape[0]
    tr = _row_tile(T)

    def body(d_ref, g_ref, a_ref, b_ref, da_ref, db_ref, dg_ref):
        d = d_ref[...]
        ga, gb = _sigmoid(g_ref[:, 0:D]), _sigmoid(g_ref[:, D:2 * D])
        da_ref[...] = (d * ga).astype(BF16)
        db_ref[...] = (d * gb).astype(BF16)
        dg_ref[:, 0:D] = (d * a_ref[...] * ga * (1.0 - ga)).astype(BF16)
        dg_ref[:, D:2 * D] = (d * b_ref[...] * gb * (1.0 - gb)).astype(BF16)

    tile = pl.BlockSpec((tr, D), lambda i: (i, 0))
    gates = pl.BlockSpec((tr, 2 * D), lambda i: (i, OFF_GATE // (2 * D)))
    return _call(body, [dmerged, proj, y_ssd, y_pool], name="merge_bwd", grid=(T // tr,),
                 in_specs=[tile, gates, tile, tile], out_specs=[tile, tile, gates],
                 out_shape=[jax.ShapeDtypeStruct((T, D), BF16), jax.ShapeDtypeStruct((T, D), BF16),
                            jax.ShapeDtypeStruct((T, NP), BF16)], sem=("parallel",))


def _adamw(w, g, m, v, name, carry=None):
    R, C = w.shape
    tr = R if R <= 128 else 128
    assert R % tr == 0

    def body(w_ref, g_ref, m_ref, v_ref, d_ref, mo_ref, vo_ref):
        gv = g_ref[...]
        mn = ADAM_B1 * m_ref[...] + (1.0 - ADAM_B1) * gv
        vn = ADAM_B2 * v_ref[...] + (1.0 - ADAM_B2) * (gv * gv)
        m_hat = mn / (1.0 - ADAM_B1 ** ADAM_STEP)
        v_hat = vn / (1.0 - ADAM_B2 ** ADAM_STEP)
        d_ref[...] = -ADAM_LR * (m_hat / (jnp.sqrt(v_hat) + ADAM_EPS) + ADAM_WD * w_ref[...])
        mo_ref[...] = mn
        vo_ref[...] = vn

    tile = pl.BlockSpec((tr, C), lambda i: (i, 0))
    sds = jax.ShapeDtypeStruct((R, C), F32)
    return _call(body, [w, g, m, v], name=name, grid=(R // tr,), in_specs=[tile] * 4, out_specs=[tile] * 3,
                 out_shape=[sds] * 3, sem=("parallel",), carry=carry)


def _me():
    return lax.axis_index("x"), lax.axis_index("y"), lax.axis_index("c")


def _xor_peer(x, y, c, p):
    return (x ^ ((p >> 2) & 1), y ^ ((p >> 1) & 1), c ^ (p & 1))


def _ada_fwd(c_row, w_ada, b_ada_mine):
    n_cols = w_ada.shape[1]

    def body(c_ref, w_ref, b_ref, mod_ref, c8_ref, csend, mpart, modbuf, send_sems, recv_sems):
        x, y, c = _me()
        me = 4 * x + 2 * y + c
        chip = 2 * x + y
        csend[...] = jnp.broadcast_to(c_ref[...], csend.shape)
        c8_ref[me] = csend[...]

        def c_copy(p):
            return pltpu.make_async_remote_copy(
                src_ref=csend, dst_ref=c8_ref.at[me], send_sem=send_sems.at[p - 1], recv_sem=recv_sems.at[p - 1],
                device_id=_xor_peer(x, y, c, p), device_id_type=MESH)

        for p in range(1, 8):
            c_copy(p).start()
        for p in range(1, 8):
            c_copy(p).wait_recv()
        cs = jnp.concatenate([c8_ref[d][0:1, :] for d in range(8)], axis=0)
        mpart[...] = _nn(cs * _sigmoid(cs), w_ref[...], precision=HIGH) + b_ref[...]
        modbuf[chip] = mpart[...]

        def m_copy(m):
            return pltpu.make_async_remote_copy(
                src_ref=mpart, dst_ref=modbuf.at[chip], send_sem=send_sems.at[6 + m], recv_sem=recv_sems.at[6 + m],
                device_id=_xor_peer(x, y, c, 2 * m), device_id_type=MESH)

        for m in range(1, 4):
            m_copy(m).start()
        for m in range(1, 4):
            m_copy(m).wait_recv()
        mine = lax.broadcasted_iota(jnp.int32, (8, 1), 0) == me
        for k in range(N_CHIPS):
            mod_ref[:, n_cols * k:n_cols * (k + 1)] = jnp.sum(jnp.where(mine, modbuf[k], 0.0), axis=0, keepdims=True)
        for p in range(1, 8):
            c_copy(p).wait_send()
        for m in range(1, 4):
            m_copy(m).wait_send()

    vmem = pl.BlockSpec(memory_space=pltpu.VMEM)
    return _call(
        body, [c_row, w_ada, b_ada_mine], name="ada_fwd", in_specs=[vmem, vmem, vmem], out_specs=[vmem, vmem],
        out_shape=[jax.ShapeDtypeStruct((1, N_CHIPS * n_cols), F32), jax.ShapeDtypeStruct((8, 8, D), F32)],
        scratch_shapes=[pltpu.VMEM((8, D), F32), pltpu.VMEM((8, n_cols), F32), pltpu.VMEM((N_CHIPS, 8, n_cols), F32),
                        pltpu.SemaphoreType.DMA((10,)), pltpu.SemaphoreType.DMA((10,))])


def _gather_small(vec, carry=None):
    rows = vec.shape[0]

    def body(v_ref, all_ref, tot_ref, dsk_ref, send_sems, recv_sems):
        x, y, c = _me()
        me = 4 * x + 2 * y + c
        all_ref[me] = v_ref[...]

        def copy(p):
            return pltpu.make_async_remote_copy(
                src_ref=v_ref, dst_ref=all_ref.at[me], send_sem=send_sems.at[p - 1], recv_sem=recv_sems.at[p - 1],
                device_id=_xor_peer(x, y, c, p), device_id_type=MESH)

        for p in range(1, 8):
            copy(p).start()
        for p in range(1, 8):
            copy(p).wait_recv()
        tot = all_ref[0]
        for d in range(1, 8):
            tot = tot + all_ref[d]
        tot_ref[...] = tot
        seg = tot[SMALL_OFF["d_skip"] // 128:SMALL_OFF["d_skip"] // 128 + 16, :]
        lane = lax.broadcasted_iota(jnp.int32, (1, 128), 1)
        sa = jnp.sum(jnp.where(lane < HEAD_DIM, seg, 0.0), axis=1, keepdims=True)
        sb = jnp.sum(jnp.where(lane < HEAD_DIM, 0.0, seg), axis=1, keepdims=True)
        dsk_ref[...] = jnp.where(lane == 0, sa, jnp.where(lane == 1, sb, 0.0))
        for p in range(1, 8):
            copy(p).wait_send()

    vmem = pl.BlockSpec(memory_space=pltpu.VMEM)
    return _call(
        body, [vec], name="gather_small", in_specs=[vmem], out_specs=[vmem, vmem, vmem],
        out_shape=[jax.ShapeDtypeStruct((8, rows, 128), F32), jax.ShapeDtypeStruct((rows, 128), F32),
                   jax.ShapeDtypeStruct((16, 128), F32)],
        scratch_shapes=[pltpu.SemaphoreType.DMA((7,)), pltpu.SemaphoreType.DMA((7,))], carry=carry)


def _gather_carry(shards):
    n = len(shards)

    def copies(ins, outs, sems):
        x, y, c = _me()
        chip = 2 * x + y

        def half(w, which):
            h = shards[w].shape[0] // 2
            return pl.ds(which * h, h)

        def first(w, m):
            return pltpu.make_async_remote_copy(
                src_ref=ins[w].at[half(w, c)], dst_ref=outs[w].at[chip, half(w, c)],
                send_sem=sems.send(6 * w + m - 1), recv_sem=sems.recv(6 * w + m - 1),
                device_id=_xor_peer(x, y, c, 2 * m), device_id_type=MESH)

        def landed(w, m):
            return pltpu.make_async_remote_copy(
                src_ref=ins[w].at[half(w, c)], dst_ref=outs[w].at[chip ^ m, half(w, c)],
                send_sem=sems.send(6 * w + m - 1), recv_sem=sems.recv(6 * w + m - 1),
                device_id=_xor_peer(x, y, c, 2 * m), device_id_type=MESH)

        def passed(w, m, which):
            part = outs[w].at[chip ^ m, half(w, which)]
            return pltpu.make_async_remote_copy(
                src_ref=part, dst_ref=part, send_sem=sems.send(6 * w + 2 + m), recv_sem=sems.recv(6 * w + 2 + m),
                device_id=(x, y, 1 - c), device_id_type=MESH)

        return c, first, landed, passed

    pairs = [(w, m) for w in range(n) for m in range(1, 4)]

    def start(ins, outs, sems):
        _, first, _, _ = copies(ins, outs, sems)
        for w, m in pairs:
            first(w, m).start()

    def finish(ins, outs, sems):
        c, first, landed, passed = copies(ins, outs, sems)
        for w, m in pairs:
            landed(w, m).wait_recv()
            passed(w, m, c).start()
        for w, m in pairs:
            passed(w, m, 1 - c).wait_recv()
        for w, m in pairs:
            first(w, m).wait_send()
            passed(w, m, c).wait_send()

    return _Carry(shards, [jax.ShapeDtypeStruct((N_CHIPS,) + s.shape, s.dtype) for s in shards], 6 * n, start, finish)


def _pair_exchange_carry(grads):
    n = len(grads)

    def copy(ins, outs, sems, w):
        x, y, c = _me()
        h = grads[w].shape[1] // 2
        return pltpu.make_async_remote_copy(
            src_ref=ins[w].at[:, pl.ds((1 - c) * h, h)], dst_ref=outs[w],
            send_sem=sems.send(w), recv_sem=sems.recv(w), device_id=(x, y, 1 - c), device_id_type=MESH)

    def start(ins, outs, sems):
        for w in range(n):
            copy(ins, outs, sems, w).start()

    def finish(ins, outs, sems):
        for w in range(n):
            copy(ins, outs, sems, w).wait()

    return _Carry(grads, [jax.ShapeDtypeStruct((N_CHIPS, g.shape[1] // 2, g.shape[2]), g.dtype) for g in grads], n,
                  start, finish)


def _chip_exchange_carry(partials):
    n = len(partials)

    def copier(ins, outs, sems):
        x, y, c = _me()
        chip = 2 * x + y

        def copy(w, m, landed):
            return pltpu.make_async_remote_copy(
                src_ref=ins[w].at[chip ^ m], dst_ref=outs[w].at[(chip ^ m) if landed else chip],
                send_sem=sems.send(3 * w + m - 1), recv_sem=sems.recv(3 * w + m - 1),
                device_id=_xor_peer(x, y, c, 2 * m), device_id_type=MESH)

        return copy

    pairs = [(w, m) for w in range(n) for m in range(1, 4)]

    def start(ins, outs, sems):
        copy = copier(ins, outs, sems)
        for w, m in pairs:
            copy(w, m, False).start()

    def finish(ins, outs, sems):
        copy = copier(ins, outs, sems)
        for w, m in pairs:
            copy(w, m, True).wait_recv()
        for w, m in pairs:
            copy(w, m, False).wait_send()

    return _Carry(partials, [jax.ShapeDtypeStruct(p.shape, p.dtype) for p in partials], 3 * n, start, finish)


def _pair_share_carry(shards):
    n = len(shards)

    def copier(ins, outs, sems):
        x, y, c = _me()

        def copy(w, which):
            h = shards[w].shape[0] // 2
            rows = pl.ds(which * h, h)
            return pltpu.make_async_remote_copy(
                src_ref=ins[w].at[rows], dst_ref=outs[w].at[rows],
                send_sem=sems.send(w), recv_sem=sems.recv(w), device_id=(x, y, 1 - c), device_id_type=MESH)

        return c, copy

    def start(ins, outs, sems):
        c, copy = copier(ins, outs, sems)
        for w in range(n):
            copy(w, c).start()

    def finish(ins, outs, sems):
        c, copy = copier(ins, outs, sems)
        for w in range(n):
            copy(w, 1 - c).wait_recv()
        for w in range(n):
            copy(w, c).wait_send()

    return _Carry(shards, [jax.ShapeDtypeStruct(s.shape, s.dtype) for s in shards], n, start, finish,
                  aliased=[(w, w) for w in range(n)])


def _pair_sum(g, part, idx, name):
    _, h, C = part.shape
    tr = min(128, h)
    nb = h // tr

    def body(idx_ref, g_ref, p_ref, o16_ref, own_ref):
        v = g_ref[...] + p_ref[...]
        o16_ref[...] = v.astype(BF16)

        @pl.when(pl.program_id(1) == idx_ref[1])
        def _():
            own_ref[...] = v

    return pl.pallas_call(
        body, name=name,
        grid_spec=pltpu.PrefetchScalarGridSpec(
            num_scalar_prefetch=1, grid=(nb, N_CHIPS),
            in_specs=[pl.BlockSpec((None, tr, C), lambda i, s, idx_ref: (s, idx_ref[0] * nb + i, 0)),
                      pl.BlockSpec((None, tr, C), lambda i, s, idx_ref: (s, i, 0))],
            out_specs=[pl.BlockSpec((None, tr, C), lambda i, s, idx_ref: (s, i, 0)),
                       pl.BlockSpec((tr, C), lambda i, s, idx_ref: (i, 0))]),
        out_shape=[jax.ShapeDtypeStruct(part.shape, BF16), jax.ShapeDtypeStruct((h, C), F32)],
        compiler_params=pltpu.CompilerParams(dimension_semantics=("arbitrary", "arbitrary"), vmem_limit_bytes=VMEM_LIMIT),
    )(idx, g, part)


def _chip_sum(own, slots, idx, name):
    h, C = own.shape
    tr = min(128, h)
    nb = h // tr

    def body(idx_ref, own_ref, s1_ref, s2_ref, s3_ref, o_ref):
        del idx_ref
        o_ref[...] = ((own_ref[...] + s1_ref[...].astype(F32)) + s2_ref[...].astype(F32)) + s3_ref[...].astype(F32)

    def slot(m):
        return pl.BlockSpec((None, tr, C), lambda i, idx_ref: (idx_ref[1] ^ m, i, 0))

    return pl.pallas_call(
        body, name=name,
        grid_spec=pltpu.PrefetchScalarGridSpec(
            num_scalar_prefetch=1, grid=(nb,),
            in_specs=[pl.BlockSpec((tr, C), lambda i, idx_ref: (i, 0)), slot(1), slot(2), slot(3)],
            out_specs=pl.BlockSpec((tr, C), lambda i, idx_ref: (idx_ref[0] * nb + i, 0))),
        out_shape=jax.ShapeDtypeStruct((2 * h, C), F32),
        compiler_params=pltpu.CompilerParams(dimension_semantics=("parallel",), vmem_limit_bytes=VMEM_LIMIT),
    )(idx, own, slots, slots, slots)


class _Reducer:
    def __init__(self, idx):
        self.idx, self.chips, self.p16, self.own, self.mine, self.final = idx, {}, {}, {}, {}, {}

    def add(self, name, whole, chip_blocks=False):
        self.chips[name] = whole if chip_blocks else _chips_from_whole(name, whole)

    def pair(self, names):
        return _pair_exchange_carry([self.chips[n] for n in names])

    def take_pair(self, names, outs):
        for n, part in zip(names, outs):
            self.p16[n], self.own[n] = _pair_sum(self.chips.pop(n), part, self.idx, "pair_sum_" + n)

    def chip(self, names):
        return _chip_exchange_carry([self.p16[n] for n in names])

    def take_chip(self, names, outs):
        for n, slots in zip(names, outs):
            del self.p16[n]
            self.mine[n] = _chip_sum(self.own.pop(n), slots, self.idx, "chip_sum_" + n)

    def share(self, names):
        return _pair_share_carry([self.mine[n] for n in names])

    def take_share(self, names, outs):
        for n, s in zip(names, outs):
            del self.mine[n]
            self.final[n] = s


def _w_ada_grad(c8, dmod_cols):
    n_cols = dmod_cols.shape[1]
    tn = 512

    def body(c_ref, d_ref, o_ref):
        cv = c_ref[...]
        o_ref[...] = _tn(cv * _sigmoid(cv), d_ref[...], precision=HIGH)

    return _call(body, [c8, dmod_cols], name="w_ada_grad", grid=(n_cols // tn,),
                 in_specs=[pl.BlockSpec((8, D), lambda j: (0, 0)), pl.BlockSpec((8, tn), lambda j: (0, j))],
                 out_specs=[pl.BlockSpec((D, tn), lambda j: (0, j))],
                 out_shape=[jax.ShapeDtypeStruct((D, n_cols), F32)], sem=("parallel",))[0]


_SMALL_SEGS = (("dmod", 6144), ("norm_mix_w", 1024), ("conv_b", 3072), ("ssd_norm_w", 2048), ("pool_scale", 1024),
               ("norm_mlp_w", 1024), ("norm_final_w", 1024), ("conv_w", 4 * XBC), ("d_skip", 2048), ("a_log", 128),
               ("dt_bias", 128), ("loss", 128))
SMALL_OFF = {}
_o = 0
for _n, _s in _SMALL_SEGS:
    SMALL_OFF[_n] = _o
    _o += _s
SMALL_LEN = -(-_o // 1024) * 1024

_FIRST = ("w_in", "conv_w")
_LATER = ("w_branch_ssd", "pool_w", "w_branch_pool", "w_out", "w_up", "w_down")
_SMALL_REPLICATED = ("b_ada", "norm_mix_w", "conv_b", "dt_bias", "a_log", "d_skip", "ssd_norm_w", "pool_scale",
                     "norm_mlp_w", "norm_final_w")
_WEIGHTS = ("w_ada", "b_ada", "norm_mix_w", "w_in", "conv_w", "conv_b", "dt_bias", "a_log", "d_skip", "ssd_norm_w",
            "w_branch_ssd", "pool_w", "pool_scale", "w_branch_pool", "w_out", "norm_mlp_w", "w_up", "w_down",
            "norm_final_w")


def _shard_2d(name, a):
    if name == "conv_w":
        return a.reshape(16, -1)
    return (a.reshape(GW, GW) if name == "pool_w" else a.reshape(a.shape[-2], a.shape[-1])).astype(BF16)


def _whole_from_chips(name, g, own, chip):
    g = lax.dynamic_update_slice(g, own[None], (chip, 0, 0))
    if name == "w_in":
        return _perm_cols(jnp.transpose(g, (1, 0, 2)).reshape(D, IN_COLS))
    if name == "w_up":
        return jnp.transpose(g, (1, 0, 2)).reshape(D, DFF)
    if name == "pool_w":
        return jnp.transpose(g.reshape(N_CHIPS, 4, GW // N_CHIPS, GW), (1, 0, 2, 3)).reshape(4, GW, GW)
    if name == "conv_w":
        return jnp.transpose(g.reshape(N_CHIPS, 4, XBC // N_CHIPS), (1, 0, 2)).reshape(4, XBC)
    return g.reshape(N_CHIPS * g.shape[1], g.shape[2])


def _chips_from_whole(name, g):
    if name.startswith("w_in"):
        return jnp.transpose(_unperm_cols(g).reshape(g.shape[0], N_CHIPS, IN_COLS // N_CHIPS), (1, 0, 2))
    if name == "w_up":
        return jnp.transpose(g.reshape(D, N_CHIPS, DFF // N_CHIPS), (1, 0, 2))
    if name == "pool_w":
        return jnp.transpose(g.reshape(4, N_CHIPS, GW // N_CHIPS, GW), (1, 0, 2, 3)).reshape(N_CHIPS, GW, GW)
    return g.reshape(N_CHIPS, g.shape[0] // N_CHIPS, g.shape[1])


def kernel(x, c, w_ada, b_ada, norm_mix_w, w_in, conv_w, conv_b, dt_bias, a_log, d_skip, ssd_norm_w, w_branch_ssd, pool_w, pool_scale, w_branch_pool, w_out, norm_mlp_w, w_up, w_down, norm_final_w, loss_target, m_w_ada, m_b_ada, m_norm_mix_w, m_w_in, m_conv_w, m_conv_b, m_dt_bias, m_a_log, m_d_skip, m_ssd_norm_w, m_w_branch_ssd, m_pool_w, m_pool_scale, m_w_branch_pool, m_w_out, m_norm_mlp_w, m_w_up, m_w_down, m_norm_final_w, v_w_ada, v_b_ada, v_norm_mix_w, v_w_in, v_conv_w, v_conv_b, v_dt_bias, v_a_log, v_d_skip, v_ssd_norm_w, v_w_branch_ssd, v_pool_w, v_pool_scale, v_w_branch_pool, v_w_out, v_norm_mlp_w, v_w_up, v_w_down, v_norm_final_w):
    args = locals()
    w = {n: args[n] for n in _WEIGHTS}
    m = {n: args["m_" + n] for n in _WEIGHTS}
    v = {n: args["v_" + n] for n in _WEIGHTS}
    xi, yi, ci = _me()
    chip = 2 * xi + yi
    idx = jnp.stack([ci, chip]).astype(jnp.int32)
    ada_cols = w_ada.shape[-1]
    xs, target = x[0], loss_target[0]
    two_d = lambda n, a: a.reshape(GW, GW) if n == "pool_w" else a.reshape(-1, a.shape[-1])
    delta, new_m, new_v, g = {}, {}, {}, {}

    def adamw(n, carry=None):
        res = _adamw(two_d(n, w[n]), two_d(n, g[n]), two_d(n, m[n]), two_d(n, v[n]), "adamw_" + n, carry=carry)
        (delta[n], new_m[n], new_v[n]), extra = res if carry is not None else (res, None)
        return extra

    b_mine = lax.dynamic_slice(b_ada, (0, chip * ada_cols), (1, ada_cols))
    mod, c8 = _ada_fwd(c, w_ada[0], b_mine)
    c8 = c8[:, 0, :]
    shift_m, scale_m, gate_m, shift_f, scale_f, gate_f = [mod[:, D * i:D * (i + 1)] for i in range(6)]
    shards = {n: _shard_2d(n, w[n]) for n in _FIRST + _LATER}
    nf_w = norm_final_w.reshape(1, D)

    h1, first = _norm_mod(xs, norm_mix_w, scale_m, shift_m, "norm_mod_mix",
                          carry=_gather_carry([shards[n] for n in _FIRST]))
    p = {n: _whole_from_chips(n, a, shards[n], chip) for n, a in zip(_FIRST, first)}
    (proj,), later = _matmul(h1, p["w_in"], mode="nn", out_dtypes=[F32], name="mm_proj",
                             carry=_gather_carry([shards[n] for n in _LATER]))
    p.update({n: _whole_from_chips(n, a, shards[n], chip) for n, a in zip(_LATER, later)})
    xbc_a = _conv_fwd(proj, p["conv_w"], conv_b)
    dtb_c, alog_c = dt_bias.reshape(HEADS, 1), a_log.reshape(HEADS, 1)
    dsk_exp = jnp.repeat(d_skip, HEAD_DIM, axis=1)
    y, hin = _ssd_fwd(xbc_a, proj, dt_bias, a_log, dtb_c, alog_c, dsk_exp)
    yn = _gate_norm(y, proj, ssd_norm_w)
    (y_ssd,) = _matmul(yn, p["w_branch_ssd"], mode="nn", out_dtypes=[F32], name="mm_branch_ssd")
    pooled, pw_out, yps = _pool_fwd(proj, p["pool_w"], pool_scale)
    (y_pool,) = _matmul(yps, p["w_branch_pool"], mode="nn", out_dtypes=[F32], name="mm_branch_pool")
    merged = _merge(proj, y_ssd, y_pool)
    resid = lambda acc, r, gt: (r + gt * acc, acc)
    x2, mix = _matmul(merged, p["w_out"], mode="nn", out_dtypes=[F32, BF16], name="mm_out",
                      epi=resid, tile_extras=(xs,), row_extras=(gate_m,))
    h2 = _norm_mod(x2, norm_mlp_w, scale_f, shift_f, "norm_mod_mlp")
    relu2 = lambda acc: (acc, jnp.square(jnp.maximum(acc, 0.0)))
    up, act = _matmul(h2, p["w_up"], mode="nn", out_dtypes=[BF16, BF16], name="mm_up", epi=relu2)
    x3, down = _matmul(act, p["w_down"], mode="nn", out_dtypes=[F32, BF16], name="mm_down",
                       epi=resid, tile_extras=(x2,), row_extras=(gate_f,))

    red = _Reducer(idx)
    dx3, d_down, sums_f = _final_loss_bwd(x3, target, nf_w, down, gate_f)
    drelu2 = lambda acc, u: (acc * (2.0 * jnp.maximum(u.astype(F32), 0.0)),)
    (dup,) = _matmul(d_down, p["w_down"], mode="nt", out_dtypes=[BF16], name="mm_dact",
                     epi=drelu2, tile_extras=(up,))
    red.add("w_down", _matmul(act, d_down, mode="tn", out_dtypes=[F32], name="mm_g_down")[0])
    (dh2,), got = _matmul(dup, p["w_up"], mode="nt", out_dtypes=[F32], name="mm_dh2",
                          carry=red.pair(["w_down"]))
    red.take_pair(["w_down"], got)
    red.add("w_up", _matmul(h2, dup, mode="tn", out_dtypes=[F32], name="mm_g_up", chip_blocks=True)[0], chip_blocks=True)
    dx2, sums_2, dmix = _norm_mod_bwd(x2, dh2, dx3, norm_mlp_w, scale_f, "norm_mod_mlp_bwd", branch=mix, gate=gate_m)
    (dmerged,), got = _matmul(dmix, p["w_out"], mode="nt", out_dtypes=[F32], name="mm_dmerged",
                              carry=red.pair(["w_up"]))
    red.take_pair(["w_up"], got)
    red.add("w_out", _matmul(merged, dmix, mode="tn", out_dtypes=[F32], name="mm_g_out")[0])
    dy_ssd, dy_pool, dproj = _merge_bwd(dmerged, proj, y_ssd, y_pool)
    (dyp,), got = _matmul(dy_pool, p["w_branch_pool"], mode="nt", out_dtypes=[F32], name="mm_dyp",
                          carry=red.pair(["w_out"]))
    red.take_pair(["w_out"], got)
    red.add("w_branch_pool", _matmul(yps, dy_pool, mode="tn", out_dtypes=[F32], name="mm_g_bpool")[0])
    dproj, g_pool_w, sums_pool = _pool_bwd(dyp, pw_out, pooled, p["pool_w"], pool_scale, dproj)
    red.add("pool_w", g_pool_w)
    red.add("w_branch_ssd", _matmul(yn, dy_ssd, mode="tn", out_dtypes=[F32], name="mm_g_bssd")[0])
    mixers = ["w_branch_pool", "pool_w", "w_branch_ssd"]
    (dyn,), got = _matmul(dy_ssd, p["w_branch_ssd"], mode="nt", out_dtypes=[F32], name="mm_dyn",
                          carry=red.pair(mixers))
    red.take_pair(mixers, got)
    dy, dproj, sums_gn = _gate_norm_bwd(dyn, y, proj, ssd_norm_w, dproj)
    six = ["w_down", "w_up", "w_out"] + mixers
    (dxa, dproj, dsk_sum, ssd_small), got = _ssd_bwd(dy, xbc_a, proj, hin, dt_bias, a_log, dtb_c, alog_c, dsk_exp,
                                                     dproj, carry=red.chip(six))
    red.take_chip(six, got)
    dxc, sums_conv = _conv_bwd_a(dxa, proj, p["conv_w"], conv_b)
    dproj = _conv_bwd_b(dxc, p["conv_w"], dproj)
    rows_a = 3 * D // 4
    (g_in_a,), got = _matmul(h1, dproj, mode="tn", out_dtypes=[F32], name="mm_g_in_a", a_cols=(0, rows_a),
                             carry=red.share(six))
    red.take_share(six, got)
    red.add("w_in_a", g_in_a)
    (g_in_b,), got = _matmul(h1, dproj, mode="tn", out_dtypes=[F32], name="mm_g_in_b", a_cols=(rows_a, D - rows_a),
                             carry=red.pair(["w_in_a"]))
    red.take_pair(["w_in_a"], got)
    red.add("w_in_b", g_in_b)
    (dh1,), got = _matmul(dproj, p["w_in"], mode="nt", out_dtypes=[F32], name="mm_dh1",
                          carry=_join(red.chip(["w_in_a"]), red.pair(["w_in_b"])))
    red.take_chip(["w_in_a"], got[:1])
    red.take_pair(["w_in_b"], got[1:])
    grad_x, sums_1 = _norm_mod_bwd(xs, dh1, dx2, norm_mix_w, scale_m, "norm_mod_mix_bwd")

    dmod = jnp.concatenate([sums_1[0:1], sums_1[1:2], sums_2[3:4], sums_2[0:1], sums_2[1:2], sums_f[1:2]], axis=1)
    pad96 = jnp.zeros((1, 96), F32)
    small = {"dmod": dmod, "norm_mix_w": sums_1[2:3], "conv_b": sums_conv[4:5], "ssd_norm_w": sums_gn[0:1],
             "pool_scale": sums_pool[0:1], "norm_mlp_w": sums_2[2:3], "norm_final_w": sums_f[0:1],
             "conv_w": sums_conv[0:4].reshape(1, 4 * XBC), "d_skip": dsk_sum[0:1],
             "a_log": jnp.concatenate([ssd_small[0:1], pad96], axis=1),
             "dt_bias": jnp.concatenate([ssd_small[1:2], pad96], axis=1), "loss": sums_f[3:4, 0:128]}
    vec = jnp.concatenate([small[n] for n, _ in _SMALL_SEGS], axis=1)
    vec = jnp.pad(vec, ((0, 0), (0, SMALL_LEN - vec.shape[1]))).reshape(SMALL_LEN // 128, 128)
    (every, total, dsk), got = _gather_small(vec, carry=_join(red.chip(["w_in_b"]), red.share(["w_in_a"])))
    red.take_chip(["w_in_b"], got[:1])
    red.take_share(["w_in_a"], got[1:])
    total = total.reshape(1, SMALL_LEN)
    seg = lambda n, size: total[:, SMALL_OFF[n]:SMALL_OFF[n] + size]
    g.update({"b_ada": seg("dmod", 6 * D), "norm_mix_w": seg("norm_mix_w", D), "conv_b": seg("conv_b", XBC),
              "dt_bias": seg("dt_bias", HEADS), "a_log": seg("a_log", HEADS), "d_skip": dsk[:, 0:2].reshape(1, HEADS),
              "ssd_norm_w": seg("ssd_norm_w", DI), "pool_scale": seg("pool_scale", D),
              "norm_mlp_w": seg("norm_mlp_w", D), "norm_final_w": seg("norm_final_w", D)})
    loss = total[0, SMALL_OFF["loss"]]
    conv_cols = conv_w.shape[-1]
    g["conv_w"] = lax.dynamic_slice(seg("conv_w", 4 * XBC).reshape(4, XBC), (0, chip * conv_cols), (4, conv_cols))
    dmod8 = every.reshape(8, SMALL_LEN)[:, SMALL_OFF["dmod"]:SMALL_OFF["dmod"] + 6 * D]
    g["w_ada"] = _w_ada_grad(c8, lax.dynamic_slice(dmod8, (0, chip * ada_cols), (8, ada_cols)))

    got = adamw("w_ada", carry=red.share(["w_in_b"]))
    red.take_share(["w_in_b"], got)
    for n in six:
        g[n] = red.final[n]
    g["w_in"] = jnp.concatenate([red.final["w_in_a"], red.final["w_in_b"]], axis=0)
    for n in ["conv_w", "w_in"] + six:
        adamw(n)
    sizes = [w[n].size for n in _SMALL_REPLICATED]
    n_small = -(-sum(sizes) // 1024) * 1024
    pack = lambda d: jnp.pad(jnp.concatenate([d[n].reshape(1, -1) for n in _SMALL_REPLICATED], axis=1),
                             ((0, 0), (0, n_small - sum(sizes)))).reshape(n_small // 128, 128)
    d_, m_, v_ = _adamw(pack(w), pack(g), pack(m), pack(v), "adamw_small")
    off = 0
    for n, s in zip(_SMALL_REPLICATED, sizes):
        for dst, src in ((delta, d_), (new_m, m_), (new_v, v_)):
            dst[n] = src.reshape(1, n_small)[:, off:off + s]
        off += s

    out = [loss, grad_x.reshape(x.shape)]
    for d in (g, delta, new_m, new_v):
        out += [d[n].reshape(w[n].shape) for n in _WEIGHTS]
    return tuple(out)
```

```python
import functools
import operator

import jax
import jax.numpy as jnp
import numpy as np
from jax import lax
from jax.experimental import pallas as pl
from jax.experimental.pallas import tpu as pltpu

F32, BF16 = jnp.float32, jnp.bfloat16
HIGH = lax.Precision.HIGHEST
MESH = pl.DeviceIdType.MESH

D = 1024
DI = 2048
HEADS, HEAD_DIM = 32, 64
GROUPS, STATE = 4, 128
Q = 128
XBC = DI + 2 * GROUPS * STATE
POOL_WINDOWS = (2, 4, 8, 16)
GW = 256
DFF = 4096
EPS = 1e-5
IN_COLS = 8224
OFF_Z, OFF_XBC, OFF_POOL, OFF_GATE, OFF_DT, NP = 0, 2048, 5120, 6144, 8192, 8448
N_CHIPS = 4
ADAM_LR, ADAM_B1, ADAM_B2, ADAM_EPS, ADAM_WD, ADAM_STEP = 0.001, 0.9, 0.999, 1e-08, 0.01, 10
VMEM_LIMIT = 56 * 2 ** 20
NEG = -1e30


def _sigmoid(v):
    return 0.5 * jnp.tanh(0.5 * v) + 0.5


def _softplus(v):
    return jnp.maximum(v, 0.0) + jnp.log1p(jnp.exp(-jnp.abs(v)))


def _dot(a, b, dims, **kw):
    return lax.dot_general(a, b, (dims, ((), ())), preferred_element_type=F32, **kw)


def _nn(a, b, **kw):
    return _dot(a, b, ((1,), (0,)), **kw)


def _nt(a, b, **kw):
    return _dot(a, b, ((1,), (1,)), **kw)


def _tn(a, b, **kw):
    return _dot(a, b, ((0,), (0,)), **kw)


def _perm_cols(w):
    pad = jnp.zeros(w.shape[:-1] + (NP - IN_COLS,), w.dtype)
    return jnp.concatenate([w[..., :5120], w[..., 5152:], w[..., 5120:5152], pad], axis=-1)


def _unperm_cols(g):
    return jnp.concatenate([g[..., :5120], g[..., OFF_DT:OFF_DT + 32], g[..., 5120:OFF_DT]], axis=-1)


class _Sems:
    def __init__(self, send, recv, local, base=0):
        self._send, self._recv, self._local, self._base = send, recv, local, base

    def shift(self, n):
        return _Sems(self._send, self._recv, self._local, self._base + n)

    def send(self, i):
        return self._send.at[self._base + i]

    def recv(self, i):
        return self._recv.at[self._base + i]

    def local(self, i):
        return self._local.at[self._base + i]


class _Carry:
    def __init__(self, ins, out_shapes, n_sems, start, finish, aliased=()):
        self.ins, self.out_shapes, self.n_sems, self.start, self.finish = list(ins), list(out_shapes), n_sems, start, finish
        self.aliased = list(aliased)


def _join(*carries):
    def run(which):
        def fn(ins, outs, sems):
            i = o = s = 0
            for cy in carries:
                getattr(cy, which)(ins[i:i + len(cy.ins)], outs[o:o + len(cy.out_shapes)], sems.shift(s))
                i, o, s = i + len(cy.ins), o + len(cy.out_shapes), s + cy.n_sems
        return fn

    aliased, i, o = [], 0, 0
    for cy in carries:
        aliased += [(i + a, o + b) for a, b in cy.aliased]
        i, o = i + len(cy.ins), o + len(cy.out_shapes)
    return _Carry([a for cy in carries for a in cy.ins], [a for cy in carries for a in cy.out_shapes],
                  sum(cy.n_sems for cy in carries), run("start"), run("finish"), aliased)


def _call(body, args, *, name, grid=(), in_specs, out_specs, out_shape, scratch_shapes=(), sem=None, aliases=None,
          carry=None):
    in_specs, out_specs, out_shape, scratch_shapes = list(in_specs), list(out_specs), list(out_shape), list(scratch_shapes)
    n_in, n_out, n_scr = len(in_specs), len(out_specs), len(scratch_shapes)
    kw = {"vmem_limit_bytes": VMEM_LIMIT}
    if carry is None:
        kernel_fn = functools.partial(body)
        if sem is not None:
            kw["dimension_semantics"] = sem
    else:
        n_ci, n_co = len(carry.ins), len(carry.out_shapes)
        hbm = pl.BlockSpec(memory_space=pl.ANY)
        in_specs += [hbm] * n_ci
        out_specs += [hbm] * n_co
        out_shape += carry.out_shapes
        n_s = max(carry.n_sems, 1)
        scratch_shapes += [pltpu.SemaphoreType.DMA((n_s,))] * 3
        args = list(args) + carry.ins
        aliases = dict(aliases or {})
        aliases.update({n_in + i: n_out + o for i, o in carry.aliased})
        if grid:
            kw["dimension_semantics"] = ("arbitrary",) * len(grid)

        def kernel_fn(*refs):
            a = n_in
            ins, c_ins = refs[:a], refs[a:a + n_ci]
            a += n_ci
            outs, c_outs = refs[a:a + n_out], refs[a + n_out:a + n_out + n_co]
            a += n_out + n_co
            scr, sems = refs[a:a + n_scr], _Sems(*refs[a + n_scr:a + n_scr + 3])
            if grid:
                ids = [pl.program_id(d) for d in range(len(grid))]
                first = functools.reduce(operator.and_, [i == 0 for i in ids])
                last = functools.reduce(operator.and_, [i == g - 1 for i, g in zip(ids, grid)])

                @pl.when(first)
                def _():
                    carry.start(c_ins, c_outs, sems)

                body(*ins, *outs, *scr)

                @pl.when(last)
                def _():
                    carry.finish(c_ins, c_outs, sems)
            else:
                carry.start(c_ins, c_outs, sems)
                body(*ins, *outs, *scr)
                carry.finish(c_ins, c_outs, sems)

    outs = pl.pallas_call(
        kernel_fn, name=name, grid=grid, in_specs=in_specs, out_specs=out_specs, out_shape=out_shape,
        scratch_shapes=scratch_shapes, input_output_aliases=aliases or {},
        compiler_params=pltpu.CompilerParams(**kw),
    )(*args)
    outs = list(outs)
    return outs if carry is None else (outs[:n_out], outs[n_out:])


def _run_carry(carry, name):
    _, outs = _call(lambda: None, [], name=name, in_specs=[], out_specs=[], out_shape=[], carry=carry)
    return outs


_TILES = {
    "mm_proj": (1024, 2816, 1024), "mm_branch_ssd": (1024, 1024, 2048), "mm_branch_pool": (1024, 1024, 1024),
    "mm_out": (1024, 1024, 1024), "mm_up": (1024, 1024, 1024), "mm_down": (512, 1024, 4096),
    "mm_dact": (1024, 1024, 1024), "mm_g_down": (1024, 1024, 2048), "mm_dh2": (1024, 1024, 4096),
    "mm_g_up": (1024, 1024, 2048), "mm_dmerged": (1024, 1024, 1024), "mm_g_out": (1024, 1024, 2048),
    "mm_dyp": (1024, 1024, 1024), "mm_g_bpool": (1024, 1024, 2048), "mm_g_bssd": (1024, 1024, 2048),
    "mm_dyn": (1024, 1024, 1024), "mm_g_in_a": (768, 1408, 2048), "mm_g_in_b": (256, 2816, 2048),
    "mm_dh1": (1024, 1024, 2816),
}


def _matmul(a, b, *, mode, out_dtypes, name, epi=None, tile_extras=(), row_extras=(), carry=None, a_cols=None,
            chip_blocks=False):
    M, K = (a.shape[1], a.shape[0]) if mode == "tn" else a.shape
    N = b.shape[0] if mode == "nt" else b.shape[1]
    a_start, M = a_cols if a_cols is not None else (0, M)
    tm, tn, tk = _TILES[name]
    tm, tn, tk = min(tm, M), min(tn, N), min(tk, K)
    assert M % tm == 0 and N % tn == 0 and K % tk == 0 and a_start % tm == 0, (name, M, N, K, tm, tn, tk)
    a_off = a_start // tm
    if mode == "nn":
        a_spec = pl.BlockSpec((tm, tk), lambda i, j, k: (i, k))
        b_spec = pl.BlockSpec((tk, tn), lambda i, j, k: (k, j))
        dims = ((1,), (0,))
    elif mode == "nt":
        a_spec = pl.BlockSpec((tm, tk), lambda i, j, k: (i, k))
        b_spec = pl.BlockSpec((tn, tk), lambda i, j, k: (j, k))
        dims = ((1,), (1,))
    else:
        a_spec = pl.BlockSpec((tk, tm), lambda i, j, k: (k, i + a_off))
        b_spec = pl.BlockSpec((tk, tn), lambda i, j, k: (k, j))
        dims = ((0,), (0,))
    nk = K // tk
    n_te, n_re, n_out = len(tile_extras), len(row_extras), len(out_dtypes)
    if epi is None:
        epi = lambda acc: (acc,)

    def body(a_ref, b_ref, *rest):
        extras = rest[:n_te + n_re]
        outs = rest[n_te + n_re:n_te + n_re + n_out]
        p = _dot(a_ref[...], b_ref[...], dims)

        def finish(acc):
            vals = epi(acc, *[e[...] for e in extras])
            for o, v in zip(outs, vals):
                o[...] = v.astype(o.dtype)

        if nk == 1:
            finish(p)
        else:
            acc_ref = rest[-1]
            k = pl.program_id(2)

            @pl.when(k == 0)
            def _():
                acc_ref[...] = p

            @pl.when(k > 0)
            def _():
                acc_ref[...] += p

            @pl.when(k == nk - 1)
            def _():
                finish(acc_ref[...])

    tile_spec = pl.BlockSpec((tm, tn), lambda i, j, k: (i, j))
    row_spec = pl.BlockSpec((1, tn), lambda i, j, k: (0, j))
    out_spec, out_dims = tile_spec, (M, N)
    if chip_blocks:
        assert n_te == 0 and tn * N_CHIPS == N
        out_spec, out_dims = pl.BlockSpec((None, tm, tn), lambda i, j, k: (j, i, 0)), (N_CHIPS, M, tn)
    return _call(
        body, [a, b, *tile_extras, *row_extras], name=name, grid=(M // tm, N // tn, nk),
        in_specs=[a_spec, b_spec] + [tile_spec] * n_te + [row_spec] * n_re, out_specs=[out_spec] * n_out,
        out_shape=[jax.ShapeDtypeStruct(out_dims, dt) for dt in out_dtypes],
        scratch_shapes=[pltpu.VMEM((tm, tn), F32)] if nk > 1 else [],
        sem=("parallel", "parallel", "arbitrary"), carry=carry)


def _row_tile(T):
    return min(512, T)


def _norm_mod(x, nw, scale, shift, name, carry=None):
    T = x.shape[0]
    tr = _row_tile(T)

    def body(x_ref, nw_ref, sc_ref, sh_ref, o_ref):
        xv = x_ref[...]
        r = lax.rsqrt(jnp.mean(xv * xv, axis=-1, keepdims=True) + EPS)
        o_ref[...] = ((xv * r) * nw_ref[...] * (1.0 + sc_ref[...]) + sh_ref[...]).astype(BF16)

    tile = pl.BlockSpec((tr, D), lambda i: (i, 0))
    row = pl.BlockSpec((1, D), lambda i: (0, 0))
    res = _call(body, [x, nw, scale, shift], name=name, grid=(T // tr,), in_specs=[tile, row, row, row],
                out_specs=[tile], out_shape=[jax.ShapeDtypeStruct((T, D), BF16)], sem=("parallel",), carry=carry)
    return res[0] if carry is None else (res[0][0], res[1])


def _norm_mod_bwd(x, dh, dres, nw, scale, name, branch=None, gate=None, carry=None):
    T = x.shape[0]
    tr = _row_tile(T)
    with_branch = branch is not None

    def body(x_ref, dh_ref, dr_ref, nw_ref, sc_ref, *rest):
        if with_branch:
            br_ref, g_ref, dx_ref, sums_ref, db_ref = rest
        else:
            dx_ref, sums_ref = rest
        i = pl.program_id(0)

        @pl.when(i == 0)
        def _():
            sums_ref[...] = jnp.zeros_like(sums_ref)

        xv, dhv = x_ref[...], dh_ref[...]
        r = lax.rsqrt(jnp.mean(xv * xv, axis=-1, keepdims=True) + EPS)
        xn = xv * r
        g1 = dhv * (1.0 + sc_ref[...])
        dxn = g1 * nw_ref[...]
        dx = dr_ref[...] + r * (dxn - xn * jnp.mean(dxn * xn, axis=-1, keepdims=True))
        dx_ref[...] = dx
        sums_ref[0:1, :] += jnp.sum(dhv, axis=0, keepdims=True)
        sums_ref[1:2, :] += jnp.sum(dhv * (xn * nw_ref[...]), axis=0, keepdims=True)
        sums_ref[2:3, :] += jnp.sum(g1 * xn, axis=0, keepdims=True)
        if with_branch:
            db_ref[...] = (dx * g_ref[...]).astype(BF16)
            sums_ref[3:4, :] += jnp.sum(dx * br_ref[...], axis=0, keepdims=True)

    tile = pl.BlockSpec((tr, D), lambda i: (i, 0))
    row = pl.BlockSpec((1, D), lambda i: (0, 0))
    sums = pl.BlockSpec((8, D), lambda i: (0, 0))
    ins = [x, dh, dres, nw, scale] + ([branch, gate] if with_branch else [])
    in_specs = [tile, tile, tile, row, row] + ([tile, row] if with_branch else [])
    out_specs = [tile, sums] + ([tile] if with_branch else [])
    out_shape = [jax.ShapeDtypeStruct((T, D), F32), jax.ShapeDtypeStruct((8, D), F32)]
    if with_branch:
        out_shape.append(jax.ShapeDtypeStruct((T, D), BF16))
    return _call(body, ins, name=name, grid=(T // tr,), in_specs=in_specs, out_specs=out_specs, out_shape=out_shape,
                 sem=("arbitrary",), carry=carry)


def _final_loss_bwd(x3, target, wf, down, gate_f):
    T = x3.shape[0]
    tr = _row_tile(T)
    n_steps = T // tr

    def body(x_ref, t_ref, w_ref, dn_ref, g_ref, dx_ref, dd_ref, sums_ref):
        i = pl.program_id(0)

        @pl.when(i == 0)
        def _():
            sums_ref[...] = jnp.zeros_like(sums_ref)

        xv = x_ref[...]
        r = lax.rsqrt(jnp.mean(xv * xv, axis=-1, keepdims=True) + EPS)
        xn = xv * r
        err = xn * w_ref[...] - t_ref[...]
        dy = err * (1.0 / D)
        dxn = dy * w_ref[...]
        dx = r * (dxn - xn * jnp.mean(dxn * xn, axis=-1, keepdims=True))
        dx_ref[...] = dx
        dd_ref[...] = (dx * g_ref[...]).astype(BF16)
        sums_ref[0:1, :] += jnp.sum(dy * xn, axis=0, keepdims=True)
        sums_ref[1:2, :] += jnp.sum(dx * dn_ref[...], axis=0, keepdims=True)
        sums_ref[2:3, :] += jnp.sum(err * err, axis=0, keepdims=True) * (0.5 / D)

        @pl.when(i == n_steps - 1)
        def _():
            sums_ref[3:4, :] = jnp.broadcast_to(jnp.sum(sums_ref[2:3, :], axis=1, keepdims=True), (1, D))

    tile = pl.BlockSpec((tr, D), lambda i: (i, 0))
    row = pl.BlockSpec((1, D), lambda i: (0, 0))
    sums = pl.BlockSpec((8, D), lambda i: (0, 0))
    return _call(body, [x3, target, wf, down, gate_f], name="final_loss_bwd", grid=(n_steps,),
                 in_specs=[tile, tile, row, tile, row], out_specs=[tile, tile, sums],
                 out_shape=[jax.ShapeDtypeStruct((T, D), F32), jax.ShapeDtypeStruct((T, D), BF16),
                            jax.ShapeDtypeStruct((8, D), F32)], sem=("arbitrary",))


CONV_TC = 1024


def _conv_taps(xp, w, b):
    acc = b + w[3:4, :] * xp
    for k in range(3):
        acc = acc + w[k:k + 1, :] * pltpu.roll(xp, 3 - k, 0)
    return acc


def _conv_fwd(proj, conv_w, conv_b):
    T = proj.shape[0]
    tr = _row_tile(T)
    nb, offb = tr // 8, OFF_XBC // CONV_TC

    def body(x_ref, h_ref, w_ref, b_ref, o_ref):
        halo = jnp.where(pl.program_id(0) > 0, h_ref[...], 0.0)
        xp = jnp.concatenate([halo, x_ref[...]], axis=0)
        acc = _conv_taps(xp, w_ref[...], b_ref[...])[8:]
        o_ref[...] = acc * _sigmoid(acc)

    return _call(
        body, [proj, proj, conv_w, conv_b], name="conv_fwd", grid=(T // tr, XBC // CONV_TC),
        in_specs=[pl.BlockSpec((tr, CONV_TC), lambda i, j: (i, j + offb)),
                  pl.BlockSpec((8, CONV_TC), lambda i, j: (jnp.maximum(i * nb - 1, 0), j + offb)),
                  pl.BlockSpec((4, CONV_TC), lambda i, j: (0, j)),
                  pl.BlockSpec((1, CONV_TC), lambda i, j: (0, j))],
        out_specs=[pl.BlockSpec((tr, CONV_TC), lambda i, j: (i, j))],
        out_shape=[jax.ShapeDtypeStruct((T, XBC), F32)], sem=("parallel", "parallel"))[0]


def _conv_bwd_a(dxa, proj, conv_w, conv_b):
    T = proj.shape[0]
    tr = _row_tile(T)
    nb, offb = tr // 8, OFF_XBC // CONV_TC

    def body(d_ref, x_ref, h_ref, w_ref, b_ref, o_ref, sums_ref):
        i = pl.program_id(1)

        @pl.when(i == 0)
        def _():
            sums_ref[...] = jnp.zeros_like(sums_ref)

        halo = jnp.where(i > 0, h_ref[...], 0.0)
        xp = jnp.concatenate([halo, x_ref[...]], axis=0)
        acc = _conv_taps(xp, w_ref[...], b_ref[...])[8:]
        s = _sigmoid(acc)
        dxc = d_ref[...] * (s * (1.0 + acc * (1.0 - s)))
        o_ref[...] = dxc
        sums_ref[3:4, :] += jnp.sum(dxc * x_ref[...], axis=0, keepdims=True)
        for k in range(3):
            sums_ref[k:k + 1, :] += jnp.sum(dxc * pltpu.roll(xp, 3 - k, 0)[8:], axis=0, keepdims=True)
        sums_ref[4:5, :] += jnp.sum(dxc, axis=0, keepdims=True)

    return _call(
        body, [dxa, proj, proj, conv_w, conv_b], name="conv_bwd_a", grid=(XBC // CONV_TC, T // tr),
        in_specs=[pl.BlockSpec((tr, CONV_TC), lambda j, i: (i, j)),
                  pl.BlockSpec((tr, CONV_TC), lambda j, i: (i, j + offb)),
                  pl.BlockSpec((8, CONV_TC), lambda j, i: (jnp.maximum(i * nb - 1, 0), j + offb)),
                  pl.BlockSpec((4, CONV_TC), lambda j, i: (0, j)),
                  pl.BlockSpec((1, CONV_TC), lambda j, i: (0, j))],
        out_specs=[pl.BlockSpec((tr, CONV_TC), lambda j, i: (i, j)), pl.BlockSpec((8, CONV_TC), lambda j, i: (0, j))],
        out_shape=[jax.ShapeDtypeStruct((T, XBC), F32), jax.ShapeDtypeStruct((8, XBC), F32)],
        sem=("parallel", "arbitrary"))


def _conv_bwd_b(dxc, conv_w, dproj):
    T = dxc.shape[0]
    tr = _row_tile(T)
    nb, offb, last = tr // 8, OFF_XBC // CONV_TC, T // tr - 1

    def body(d_ref, h_ref, w_ref, dp_in, o_ref):
        del dp_in
        halo = jnp.where(pl.program_id(0) < last, h_ref[...], 0.0)
        xp = jnp.concatenate([d_ref[...], halo], axis=0)
        n = xp.shape[0]
        w = w_ref[...]
        acc = w[3:4, :] * xp
        for k in range(3):
            acc = acc + w[k:k + 1, :] * pltpu.roll(xp, n - (3 - k), 0)
        o_ref[...] = acc[:tr].astype(BF16)

    return _call(
        body, [dxc, dxc, conv_w, dproj], name="conv_bwd_b", grid=(T // tr, XBC // CONV_TC),
        in_specs=[pl.BlockSpec((tr, CONV_TC), lambda i, j: (i, j)),
                  pl.BlockSpec((8, CONV_TC), lambda i, j: (jnp.minimum((i + 1) * nb, T // 8 - 1), j)),
                  pl.BlockSpec((4, CONV_TC), lambda i, j: (0, j)),
                  pl.BlockSpec(memory_space=pl.ANY)],
        out_specs=[pl.BlockSpec((tr, CONV_TC), lambda i, j: (i, j + offb))],
        out_shape=[jax.ShapeDtypeStruct(dproj.shape, BF16)], aliases={3: 0}, sem=("parallel", "parallel"))[0]


def _spread(v, sel, pieces):
    out = None
    for _ in range(pieces):
        p = v.astype(BF16)
        term = _nn(p, sel)
        out = term if out is None else out + term
        v = v - p.astype(F32)
    return out


def _ssd_selectors():
    g = np.arange(GROUPS)[:, None, None]
    piece = np.arange(128)[None, :, None]
    h = np.where(piece < 3 * HEADS, piece % HEADS, -1)
    blocks = (h == 8 * g + np.arange(1024)[None, None, :] // 128)
    pairs = (h == 8 * g + np.arange(512)[None, None, :] // HEAD_DIM)
    lane = np.arange(128)[None, None, :]
    block_sum = (lane == 8 * g + np.arange(1024)[None, :, None] // 128)
    pair_sum = (lane == 8 * g + np.arange(512)[None, :, None] // HEAD_DIM)
    return [jnp.asarray(m, BF16) for m in (blocks, pairs, block_sum, pair_sum)]


def _pack3(v):
    p0 = v.astype(BF16)
    r1 = v - p0.astype(F32)
    p1 = r1.astype(BF16)
    r2 = r1 - p1.astype(F32)
    return p0 + pltpu.roll(r1, HEADS, 1).astype(BF16) + pltpu.roll(r2, 2 * HEADS, 1).astype(BF16)


def _ssd_group(g, cs_p, csT, dt_p, s_mat, causal_w, lo, blocks_ref, pairs_ref):
    csb = _nn(cs_p, blocks_ref[g])
    row = jnp.concatenate([csT[8 * g + hh:8 * g + hh + 1, :] for hh in range(8)], axis=1)
    l_w = jnp.exp(jnp.where(causal_w, csb - row, NEG))
    m_w = jnp.concatenate([s_mat] * 8, axis=1) * l_w
    cs_g = jnp.concatenate([jnp.where(lo, csb[:, 256 * jj:256 * jj + 128], csb[:, 256 * jj + 128:256 * jj + 256])
                            for jj in range(4)], axis=1)
    cs_last = cs_g[Q - 1:Q, :]
    return m_w, l_w, _nn(dt_p, pairs_ref[g]), jnp.exp(cs_g), jnp.exp(cs_last - cs_g), jnp.exp(cs_last)


def _ssd_common(dtp_ref, dtb_r, alog_r, dtb_c, alog_c):
    rows = lax.broadcasted_iota(jnp.int32, (Q, Q), 0)
    cols = lax.broadcasted_iota(jnp.int32, (Q, Q), 1)
    tri = (cols <= rows).astype(F32)
    heads = lax.broadcasted_iota(jnp.int32, (1, 128), 1) < HEADS
    raw_w = dtp_ref[...] + dtb_r[...]
    dt_w = jnp.where(heads, _softplus(raw_w), 0.0)
    a_w = -jnp.exp(alog_r[...])
    cs_w = _nn(tri, dt_w * a_w, precision=HIGH)
    aT = _softplus(dtp_ref[...].T[0:HEADS, :] + dtb_c[...]) * (-jnp.exp(alog_c[...]))
    csT = _nt(aT, tri, precision=HIGH)
    return raw_w[:, 0:HEADS], dt_w[:, 0:HEADS], a_w[:, 0:HEADS], csT, _pack3(cs_w), _pack3(dt_w)


def _ssd_fwd(xbc_a, proj, dtb_r, alog_r, dtb_c, alog_c, dsk_exp):
    T = xbc_a.shape[0]
    nc = T // Q
    dtb_r, alog_r = [jnp.pad(a, ((0, 0), (0, 128 - HEADS))) for a in (dtb_r, alog_r)]

    def body(xbc_ref, dtp_ref, dtb_r_ref, alog_r_ref, dtb_c_ref, alog_c_ref, dsk_ref, blocks_ref, pairs_ref,
             y_ref, hin_ref, h_scr):
        @pl.when(pl.program_id(0) == 0)
        def _():
            h_scr[...] = jnp.zeros_like(h_scr)

        _, _, _, csT, cs_p, dt_p = _ssd_common(dtp_ref, dtb_r_ref, alog_r_ref, dtb_c_ref, alog_c_ref)
        lo = lax.broadcasted_iota(jnp.int32, (1, 128), 1) < HEAD_DIM
        hi = jnp.logical_not(lo)
        causal_w = (lax.broadcasted_iota(jnp.int32, (Q, 1024), 1) & (Q - 1)) <= lax.broadcasted_iota(jnp.int32, (Q, 1024), 0)
        for g in range(GROUPS):
            gs = slice(512 * g, 512 * (g + 1))
            hs = slice(128 * g, 128 * (g + 1))
            xs_g = xbc_ref[:, gs]
            b_g = xbc_ref[:, DI + STATE * g:DI + STATE * (g + 1)].astype(BF16)
            c_g = xbc_ref[:, DI + 512 + STATE * g:DI + 512 + STATE * (g + 1)].astype(BF16)
            m_w, _, dt_g, ecs_g, dec_g, cd_g = _ssd_group(g, cs_p, csT, dt_p, _nt(c_g, b_g), causal_w, lo, blocks_ref, pairs_ref)
            m_b = m_w.astype(BF16)
            xdt = xs_g * dt_g
            xdt_b = xdt.astype(BF16)
            ys = []
            for jj in range(4):
                xp = xdt_b[:, 128 * jj:128 * (jj + 1)]
                x_ab = jnp.concatenate([jnp.where(lo, xp, jnp.zeros_like(xp)), jnp.where(hi, xp, jnp.zeros_like(xp))], axis=0)
                ys.append(_nn(m_b[:, 256 * jj:256 * (jj + 1)], x_ab))
            h_g = h_scr[hs, :]
            hin_ref[0, hs, :] = h_g
            y_ref[:, gs] = jnp.concatenate(ys, axis=1) + _nn(c_g, h_g.astype(BF16)) * ecs_g + dsk_ref[:, gs] * xs_g
            h_scr[hs, :] = h_g * cd_g + _tn(b_g, (xdt * dec_g).astype(BF16))

    small_r = pl.BlockSpec((1, 128), lambda c: (0, 0))
    small_c = pl.BlockSpec((HEADS, 1), lambda c: (0, 0))
    blocks, pairs, _, _ = _ssd_selectors()
    whole = lambda a: pl.BlockSpec(a.shape, lambda c: (0,) * a.ndim)
    return _call(
        body, [xbc_a, proj, dtb_r, alog_r, dtb_c, alog_c, dsk_exp, blocks, pairs], name="ssd_fwd", grid=(nc,),
        in_specs=[pl.BlockSpec((Q, XBC), lambda c: (c, 0)),
                  pl.BlockSpec((Q, 128), lambda c: (c, OFF_DT // 128)),
                  small_r, small_r, small_c, small_c,
                  pl.BlockSpec((1, DI), lambda c: (0, 0)), whole(blocks), whole(pairs)],
        out_specs=[pl.BlockSpec((Q, DI), lambda c: (c, 0)), pl.BlockSpec((1, 512, 512), lambda c: (c, 0, 0))],
        out_shape=[jax.ShapeDtypeStruct((T, DI), F32), jax.ShapeDtypeStruct((nc, 512, 512), F32)],
        scratch_shapes=[pltpu.VMEM((512, 512), F32)], sem=("arbitrary",))


def _ssd_bwd(dy, xbc_a, proj, hin, dtb_r, alog_r, dtb_c, alog_c, dsk_exp, dproj, carry=None):
    T = xbc_a.shape[0]
    nc = T // Q
    dtb_r, alog_r = [jnp.pad(a, ((0, 0), (0, 128 - HEADS))) for a in (dtb_r, alog_r)]

    def body(dy_ref, xbc_ref, dtp_ref, hin_ref, dtb_r_ref, alog_r_ref, dtb_c_ref, alog_c_ref, dsk_ref, dp_in,
             blocks_ref, pairs_ref, block_sum_ref, pair_sum_ref, dxa_ref, dp_ref, dsk_sum_ref, small_ref, dh_scr):
        del dp_in

        @pl.when(pl.program_id(0) == 0)
        def _():
            dh_scr[...] = jnp.zeros_like(dh_scr)
            dsk_sum_ref[...] = jnp.zeros_like(dsk_sum_ref)
            small_ref[...] = jnp.zeros_like(small_ref)

        raw, dt, a_r, csT, cs_p, dt_p = _ssd_common(dtp_ref, dtb_r_ref, alog_r_ref, dtb_c_ref, alog_c_ref)
        lo = lax.broadcasted_iota(jnp.int32, (1, 128), 1) < HEAD_DIM
        hi = jnp.logical_not(lo)
        sub32 = lax.broadcasted_iota(jnp.int32, (HEADS, 1), 0)
        causal_w = (lax.broadcasted_iota(jnp.int32, (Q, 1024), 1) & (Q - 1)) <= lax.broadcasted_iota(jnp.int32, (Q, 1024), 0)
        dcs_c = jnp.zeros((Q, 128), F32)
        dcs_r = jnp.zeros((HEADS, Q), F32)
        dcs_l = jnp.zeros((8, 128), F32)
        ddt_x = jnp.zeros((Q, 128), F32)
        for g in range(GROUPS):
            gs = slice(512 * g, 512 * (g + 1))
            hs = slice(128 * g, 128 * (g + 1))
            xs_g, dy_g = xbc_ref[:, gs], dy_ref[:, gs]
            b_g = xbc_ref[:, DI + STATE * g:DI + STATE * (g + 1)].astype(BF16)
            c_g = xbc_ref[:, DI + 512 + STATE * g:DI + 512 + STATE * (g + 1)].astype(BF16)
            m_w, l_w, dt_g, ecs_g, dec_g, cd_g = _ssd_group(g, cs_p, csT, dt_p, _nt(c_g, b_g), causal_w, lo, blocks_ref, pairs_ref)
            m_b = m_w.astype(BF16)
            xdt = xs_g * dt_g
            xdt_b, dy_b = xdt.astype(BF16), dy_g.astype(BF16)
            dms, dxs = [], []
            for jj in range(4):
                xp, dyp = xdt_b[:, 128 * jj:128 * (jj + 1)], dy_b[:, 128 * jj:128 * (jj + 1)]
                dy_ab = jnp.concatenate([jnp.where(lo, dyp, jnp.zeros_like(dyp)), jnp.where(hi, dyp, jnp.zeros_like(dyp))], axis=0)
                dm_ab = _nt(dy_ab, xp)
                dms += [dm_ab[:Q], dm_ab[Q:]]
                dx_ab = _tn(m_b[:, 256 * jj:256 * (jj + 1)], dyp)
                dxs.append(jnp.where(lo, dx_ab[:Q], dx_ab[Q:]))
            dm_w = jnp.concatenate(dms, axis=1)
            w_w = dm_w * m_w
            dcs_c = dcs_c + _spread(w_w, block_sum_ref[g], 2)
            w_cols = jnp.sum(w_w, axis=0, keepdims=True)
            for hh in range(8):
                dcs_r = dcs_r + jnp.where(sub32 == 8 * g + hh, w_cols[:, 128 * hh:128 * (hh + 1)], 0.0)
            dl_w = dm_w * l_w
            ds_mat = dl_w[:, 0:128]
            for hh in range(1, 8):
                ds_mat = ds_mat + dl_w[:, 128 * hh:128 * (hh + 1)]
            hin_g = hin_ref[0, hs, :]
            hin_b = hin_g.astype(BF16)
            dh_g = dh_scr[hs, :]
            dh_b = dh_g.astype(BF16)
            g_mat = _nn(b_g, dh_b)
            xdec = xdt * dec_g
            xg = xdec * g_mat
            dxdt = jnp.concatenate(dxs, axis=1) + dec_g * g_mat
            sums = _spread(jnp.concatenate([dy_g * (_nn(c_g, hin_b) * ecs_g) - xg, dxdt * xs_g], axis=0), pair_sum_ref[g], 2)
            dcs_c = dcs_c + sums[:Q]
            ddt_x = ddt_x + sums[Q:]
            last = jnp.sum(xg, axis=0, keepdims=True) + jnp.sum(dh_g * hin_g, axis=0, keepdims=True) * cd_g
            dcs_l = dcs_l + _spread(jnp.broadcast_to(last, (8, 512)), pair_sum_ref[g], 2)
            dz = (dy_g * ecs_g).astype(BF16)
            ds_b = ds_mat.astype(BF16)
            dxa_ref[:, gs] = dxdt * dt_g + dy_g * dsk_ref[:, gs]
            dxa_ref[:, DI + STATE * g:DI + STATE * (g + 1)] = _nt(xdec.astype(BF16), dh_b) + _tn(ds_b, c_g)
            dxa_ref[:, DI + 512 + STATE * g:DI + 512 + STATE * (g + 1)] = _nt(dz, hin_b) + _nn(ds_b, b_g)
            dh_scr[hs, :] = _tn(c_g, dz) + dh_g * cd_g
            dsk_sum_ref[0:1, gs] += jnp.sum(dy_g * xs_g, axis=0, keepdims=True)

        rows = lax.broadcasted_iota(jnp.int32, (Q, Q), 0)
        cols = lax.broadcasted_iota(jnp.int32, (Q, Q), 1)
        tri_t = (cols >= rows).astype(F32)
        last_row = lax.broadcasted_iota(jnp.int32, (Q, 1), 0) == Q - 1
        dcs = (dcs_c + jnp.where(last_row, dcs_l[0:1, :], 0.0))[:, 0:HEADS]
        da = _nn(tri_t, dcs, precision=HIGH) - _nt(tri_t, dcs_r, precision=HIGH)
        ddt_raw = (ddt_x[:, 0:HEADS] + da * a_r) * _sigmoid(raw)
        small_ref[0:1, :] += jnp.sum(da * dt, axis=0, keepdims=True) * a_r
        small_ref[1:2, :] += jnp.sum(ddt_raw, axis=0, keepdims=True)
        dp_ref[...] = jnp.zeros_like(dp_ref)
        dp_ref[:, 0:HEADS] = ddt_raw.astype(BF16)

    rev = lambda c: nc - 1 - c
    small_r = pl.BlockSpec((1, 128), lambda c: (0, 0))
    small_c = pl.BlockSpec((HEADS, 1), lambda c: (0, 0))
    selectors = _ssd_selectors()
    whole = lambda a: pl.BlockSpec(a.shape, lambda c: (0,) * a.ndim)
    return _call(
        body, [dy, xbc_a, proj, hin, dtb_r, alog_r, dtb_c, alog_c, dsk_exp, dproj, *selectors], name="ssd_bwd", grid=(nc,),
        in_specs=[pl.BlockSpec((Q, DI), lambda c: (rev(c), 0)),
                  pl.BlockSpec((Q, XBC), lambda c: (rev(c), 0)),
                  pl.BlockSpec((Q, 128), lambda c: (rev(c), OFF_DT // 128)),
                  pl.BlockSpec((1, 512, 512), lambda c: (rev(c), 0, 0)),
                  small_r, small_r, small_c, small_c,
                  pl.BlockSpec((1, DI), lambda c: (0, 0)),
                  pl.BlockSpec(memory_space=pl.ANY)] + [whole(a) for a in selectors],
        out_specs=[pl.BlockSpec((Q, XBC), lambda c: (rev(c), 0)),
                   pl.BlockSpec((Q, 256), lambda c: (rev(c), OFF_DT // 256)),
                   pl.BlockSpec((8, DI), lambda c: (0, 0)),
                   pl.BlockSpec((8, HEADS), lambda c: (0, 0))],
        out_shape=[jax.ShapeDtypeStruct((T, XBC), F32), jax.ShapeDtypeStruct(dproj.shape, BF16),
                   jax.ShapeDtypeStruct((8, DI), F32), jax.ShapeDtypeStruct((8, HEADS), F32)],
        aliases={9: 1}, scratch_shapes=[pltpu.VMEM((512, 512), F32)], sem=("arbitrary",), carry=carry)


def _gate_norm(y, proj, w):
    T = y.shape[0]
    tr = _row_tile(T)

    def body(y_ref, z_ref, w_ref, o_ref):
        for g in range(GROUPS):
            gs = slice(512 * g, 512 * (g + 1))
            z = z_ref[:, gs]
            yg = y_ref[:, gs] * (z * _sigmoid(z))
            r = lax.rsqrt(jnp.mean(yg * yg, axis=-1, keepdims=True) + EPS)
            o_ref[:, gs] = (yg * r * w_ref[:, gs]).astype(BF16)

    tile = pl.BlockSpec((tr, DI), lambda i: (i, 0))
    return _call(body, [y, proj, w], name="gate_norm", grid=(T // tr,),
                 in_specs=[tile, tile, pl.BlockSpec((1, DI), lambda i: (0, 0))], out_specs=[tile],
                 out_shape=[jax.ShapeDtypeStruct((T, DI), BF16)], sem=("parallel",))[0]


def _gate_norm_bwd(dyn, y, proj, w, dproj):
    T = y.shape[0]
    tr = _row_tile(T)

    def body(d_ref, y_ref, z_ref, w_ref, dp_in, dy_ref, dz_ref, sums_ref):
        del dp_in

        @pl.when(pl.program_id(0) == 0)
        def _():
            sums_ref[...] = jnp.zeros_like(sums_ref)

        for g in range(GROUPS):
            gs = slice(512 * g, 512 * (g + 1))
            z, yv, d = z_ref[:, gs], y_ref[:, gs], d_ref[:, gs]
            s = _sigmoid(z)
            silu = z * s
            yg = yv * silu
            r = lax.rsqrt(jnp.mean(yg * yg, axis=-1, keepdims=True) + EPS)
            yn = yg * r
            sums_ref[0:1, gs] += jnp.sum(d * yn, axis=0, keepdims=True)
            dn = d * w_ref[:, gs]
            dyg = r * (dn - yn * jnp.mean(dn * yn, axis=-1, keepdims=True))
            dy_ref[:, gs] = dyg * silu
            dz_ref[:, gs] = (dyg * yv * (s * (1.0 + z * (1.0 - s)))).astype(BF16)

    tile = pl.BlockSpec((tr, DI), lambda i: (i, 0))
    return _call(
        body, [dyn, y, proj, w, dproj], name="gate_norm_bwd", grid=(T // tr,),
        in_specs=[tile, tile, tile, pl.BlockSpec((1, DI), lambda i: (0, 0)), pl.BlockSpec(memory_space=pl.ANY)],
        out_specs=[tile, tile, pl.BlockSpec((8, DI), lambda i: (0, 0))],
        out_shape=[jax.ShapeDtypeStruct((T, DI), F32), jax.ShapeDtypeStruct(dproj.shape, BF16),
                   jax.ShapeDtypeStruct((8, DI), F32)],
        aliases={4: 1}, sem=("arbitrary",))


def _pool_fwd(proj, pool_w_b, pool_scale):
    T = proj.shape[0]
    tr = _row_tile(T)
    nb = tr // 16

    def body(u_ref, h_ref, pw_ref, ps_ref, pooled_ref, pw_out_ref, yps_ref):
        i = pl.program_id(0)
        t = i * tr + lax.broadcasted_iota(jnp.int32, (tr, 1), 0)
        for g, win in enumerate(POOL_WINDOWS):
            gs = slice(GW * g, GW * (g + 1))
            u = u_ref[:, gs]
            s = jnp.concatenate([jnp.where(i > 0, h_ref[:, gs], 0.0), u], axis=0)
            sh = 1
            while sh < win:
                s = s + pltpu.roll(s, sh, 0)
                sh *= 2
            pooled = (s[16:] * (1.0 / jnp.minimum(t + 1, win).astype(F32)) - u).astype(BF16)
            pooled_ref[:, gs] = pooled
            pwv = _nn(pooled, pw_ref[g])
            pw_out_ref[:, gs] = pwv
            yps_ref[:, gs] = (pwv * ps_ref[:, gs]).astype(BF16)

    tile = pl.BlockSpec((tr, D), lambda i: (i, 0))
    return _call(
        body, [proj, proj, pool_w_b, pool_scale], name="pool_fwd", grid=(T // tr,),
        in_specs=[pl.BlockSpec((tr, D), lambda i: (i, OFF_POOL // D)),
                  pl.BlockSpec((16, D), lambda i: (jnp.maximum(i * nb - 1, 0), OFF_POOL // D)),
                  pl.BlockSpec((4, GW, GW), lambda i: (0, 0, 0)),
                  pl.BlockSpec((1, D), lambda i: (0, 0))],
        out_specs=[tile, tile, tile],
        out_shape=[jax.ShapeDtypeStruct((T, D), BF16), jax.ShapeDtypeStruct((T, D), F32),
                   jax.ShapeDtypeStruct((T, D), BF16)], sem=("parallel",))


def _pool_bwd(dyp, pw_out, pooled, pool_w_b, pool_scale, dproj):
    T = dyp.shape[0]
    tr = _row_tile(T)
    nb, last = tr // 16, T // tr - 1

    def body(d_ref, h_ref, pwo_ref, pooled_ref, pw_ref, ps_ref, dp_in, du_ref, gpw_ref, sums_ref):
        del dp_in
        i = pl.program_id(0)

        @pl.when(i == 0)
        def _():
            gpw_ref[...] = jnp.zeros_like(gpw_ref)
            sums_ref[...] = jnp.zeros_like(sums_ref)

        n = tr + 16
        t = i * tr + lax.broadcasted_iota(jnp.int32, (n, 1), 0)
        sums_ref[0:1, :] += jnp.sum(d_ref[...] * pwo_ref[...], axis=0, keepdims=True)
        for g, win in enumerate(POOL_WINDOWS):
            gs = slice(GW * g, GW * (g + 1))
            d_ext = jnp.concatenate([d_ref[:, gs], jnp.where(i < last, h_ref[:, gs], 0.0)], axis=0)
            dpw = (d_ext * ps_ref[:, gs]).astype(BF16)
            dpooled = _nt(dpw, pw_ref[g])
            s = jnp.where(t < T, dpooled * (1.0 / jnp.minimum(t + 1, win).astype(F32)), 0.0)
            sh = 1
            while sh < win:
                s = s + pltpu.roll(s, n - sh, 0)
                sh *= 2
            du_ref[:, gs] = (s[:tr] - dpooled[:tr]).astype(BF16)
            gpw_ref[g] += _tn(pooled_ref[:, gs], dpw[:tr])

    tile = pl.BlockSpec((tr, D), lambda i: (i, 0))
    return _call(
        body, [dyp, dyp, pw_out, pooled, pool_w_b, pool_scale, dproj], name="pool_bwd", grid=(T // tr,),
        in_specs=[tile, pl.BlockSpec((16, D), lambda i: (jnp.minimum((i + 1) * nb, T // 16 - 1), 0)), tile, tile,
                  pl.BlockSpec((4, GW, GW), lambda i: (0, 0, 0)), pl.BlockSpec((1, D), lambda i: (0, 0)),
                  pl.BlockSpec(memory_space=pl.ANY)],
        out_specs=[pl.BlockSpec((tr, D), lambda i: (i, OFF_POOL // D)),
                   pl.BlockSpec((4, GW, GW), lambda i: (0, 0, 0)), pl.BlockSpec((8, D), lambda i: (0, 0))],
        out_shape=[jax.ShapeDtypeStruct(dproj.shape, BF16), jax.ShapeDtypeStruct((4, GW, GW), F32),
                   jax.ShapeDtypeStruct((8, D), F32)],
        aliases={6: 0}, sem=("arbitrary",))


def _merge(proj, y_ssd, y_pool):
    T = proj.shape[0]
    tr = _row_tile(T)

    def body(g_ref, a_ref, b_ref, o_ref):
        o_ref[...] = (_sigmoid(g_ref[:, 0:D]) * a_ref[...] + _sigmoid(g_ref[:, D:2 * D]) * b_ref[...]).astype(BF16)

    tile = pl.BlockSpec((tr, D), lambda i: (i, 0))
    return _call(body, [proj, y_ssd, y_pool], name="merge", grid=(T // tr,),
                 in_specs=[pl.BlockSpec((tr, 2 * D), lambda i: (i, OFF_GATE // (2 * D))), tile, tile], out_specs=[tile],
                 out_shape=[jax.ShapeDtypeStruct((T, D), BF16)], sem=("parallel",))[0]


def _merge_bwd(dmerged, proj, y_ssd, y_pool):
    T = proj.shape[0]
    tr = _row_tile(T)

    def body(d_ref, g_ref, a_ref, b_ref, da_ref, db_ref, dg_ref):
        d = d_ref[...]
        ga, gb = _sigmoid(g_ref[:, 0:D]), _sigmoid(g_ref[:, D:2 * D])
        da_ref[...] = (d * ga).astype(BF16)
        db_ref[...] = (d * gb).astype(BF16)
        dg_ref[:, 0:D] = (d * a_ref[...] * ga * (1.0 - ga)).astype(BF16)
        dg_ref[:, D:2 * D] = (d * b_ref[...] * gb * (1.0 - gb)).astype(BF16)

    tile = pl.BlockSpec((tr, D), lambda i: (i, 0))
    gates = pl.BlockSpec((tr, 2 * D), lambda i: (i, OFF_GATE // (2 * D)))
    return _call(body, [dmerged, proj, y_ssd, y_pool], name="merge_bwd", grid=(T // tr,),
                 in_specs=[tile, gates, tile, tile], out_specs=[tile, tile, gates],
                 out_shape=[jax.ShapeDtypeStruct((T, D), BF16), jax.ShapeDtypeStruct((T, D), BF16),
                            jax.ShapeDtypeStruct((T, NP), BF16)], sem=("parallel",))


def _adamw(w, g, m, v, name, carry=None):
    R, C = w.shape
    tr = R if R <= 128 else 128
    assert R % tr == 0

    def body(w_ref, g_ref, m_ref, v_ref, d_ref, mo_ref, vo_ref):
        gv = g_ref[...]
        mn = ADAM_B1 * m_ref[...] + (1.0 - ADAM_B1) * gv
        vn = ADAM_B2 * v_ref[...] + (1.0 - ADAM_B2) * (gv * gv)
        m_hat = mn / (1.0 - ADAM_B1 ** ADAM_STEP)
        v_hat = vn / (1.0 - ADAM_B2 ** ADAM_STEP)
        d_ref[...] = -ADAM_LR * (m_hat / (jnp.sqrt(v_hat) + ADAM_EPS) + ADAM_WD * w_ref[...])
        mo_ref[...] = mn
        vo_ref[...] = vn

    tile = pl.BlockSpec((tr, C), lambda i: (i, 0))
    sds = jax.ShapeDtypeStruct((R, C), F32)
    return _call(body, [w, g, m, v], name=name, grid=(R // tr,), in_specs=[tile] * 4, out_specs=[tile] * 3,
                 out_shape=[sds] * 3, sem=("parallel",), carry=carry)


def _me():
    return lax.axis_index("x"), lax.axis_index("y"), lax.axis_index("c")


def _xor_peer(x, y, c, p):
    return (x ^ ((p >> 2) & 1), y ^ ((p >> 1) & 1), c ^ (p & 1))


def _ada_fwd(c_row, w_ada, b_ada_mine):
    n_cols = w_ada.shape[1]

    def body(c_ref, w_ref, b_ref, mod_ref, c8_ref, csend, mpart, modbuf, send_sems, recv_sems):
        x, y, c = _me()
        me = 4 * x + 2 * y + c
        chip = 2 * x + y
        csend[...] = jnp.broadcast_to(c_ref[...], csend.shape)
        c8_ref[me] = csend[...]

        def c_copy(p):
            return pltpu.make_async_remote_copy(
                src_ref=csend, dst_ref=c8_ref.at[me], send_sem=send_sems.at[p - 1], recv_sem=recv_sems.at[p - 1],
                device_id=_xor_peer(x, y, c, p), device_id_type=MESH)

        for p in range(1, 8):
            c_copy(p).start()
        for p in range(1, 8):
            c_copy(p).wait_recv()
        cs = jnp.concatenate([c8_ref[d][0:1, :] for d in range(8)], axis=0)
        mpart[...] = _nn(cs * _sigmoid(cs), w_ref[...], precision=HIGH) + b_ref[...]
        modbuf[chip] = mpart[...]

        def m_copy(m):
            return pltpu.make_async_remote_copy(
                src_ref=mpart, dst_ref=modbuf.at[chip], send_sem=send_sems.at[6 + m], recv_sem=recv_sems.at[6 + m],
                device_id=_xor_peer(x, y, c, 2 * m), device_id_type=MESH)

        for m in range(1, 4):
            m_copy(m).start()
        for m in range(1, 4):
            m_copy(m).wait_recv()
        mine = lax.broadcasted_iota(jnp.int32, (8, 1), 0) == me
        for k in range(N_CHIPS):
            mod_ref[:, n_cols * k:n_cols * (k + 1)] = jnp.sum(jnp.where(mine, modbuf[k], 0.0), axis=0, keepdims=True)
        for p in range(1, 8):
            c_copy(p).wait_send()
        for m in range(1, 4):
            m_copy(m).wait_send()

    vmem = pl.BlockSpec(memory_space=pltpu.VMEM)
    return _call(
        body, [c_row, w_ada, b_ada_mine], name="ada_fwd", in_specs=[vmem, vmem, vmem], out_specs=[vmem, vmem],
        out_shape=[jax.ShapeDtypeStruct((1, N_CHIPS * n_cols), F32), jax.ShapeDtypeStruct((8, 8, D), F32)],
        scratch_shapes=[pltpu.VMEM((8, D), F32), pltpu.VMEM((8, n_cols), F32), pltpu.VMEM((N_CHIPS, 8, n_cols), F32),
                        pltpu.SemaphoreType.DMA((10,)), pltpu.SemaphoreType.DMA((10,))])


def _gather_small(vec, carry=None):
    rows = vec.shape[0]

    def body(v_ref, all_ref, tot_ref, dsk_ref, send_sems, recv_sems):
        x, y, c = _me()
        me = 4 * x + 2 * y + c
        all_ref[me] = v_ref[...]

        def copy(p):
            return pltpu.make_async_remote_copy(
                src_ref=v_ref, dst_ref=all_ref.at[me], send_sem=send_sems.at[p - 1], recv_sem=recv_sems.at[p - 1],
                device_id=_xor_peer(x, y, c, p), device_id_type=MESH)

        for p in range(1, 8):
            copy(p).start()
        for p in range(1, 8):
            copy(p).wait_recv()
        tot = all_ref[0]
        for d in range(1, 8):
            tot = tot + all_ref[d]
        tot_ref[...] = tot
        seg = tot[SMALL_OFF["d_skip"] // 128:SMALL_OFF["d_skip"] // 128 + 16, :]
        lane = lax.broadcasted_iota(jnp.int32, (1, 128), 1)
        sa = jnp.sum(jnp.where(lane < HEAD_DIM, seg, 0.0), axis=1, keepdims=True)
        sb = jnp.sum(jnp.where(lane < HEAD_DIM, 0.0, seg), axis=1, keepdims=True)
        dsk_ref[...] = jnp.where(lane == 0, sa, jnp.where(lane == 1, sb, 0.0))
        for p in range(1, 8):
            copy(p).wait_send()

    vmem = pl.BlockSpec(memory_space=pltpu.VMEM)
    return _call(
        body, [vec], name="gather_small", in_specs=[vmem], out_specs=[vmem, vmem, vmem],
        out_shape=[jax.ShapeDtypeStruct((8, rows, 128), F32), jax.ShapeDtypeStruct((rows, 128), F32),
                   jax.ShapeDtypeStruct((16, 128), F32)],
        scratch_shapes=[pltpu.SemaphoreType.DMA((7,)), pltpu.SemaphoreType.DMA((7,))], carry=carry)


def _gather_carry(shards):
    n = len(shards)

    def copies(ins, outs, sems):
        x, y, c = _me()
        chip = 2 * x + y

        def half(w, which):
            h = shards[w].shape[0] // 2
            return pl.ds(which * h, h)

        def first(w, m):
            return pltpu.make_async_remote_copy(
                src_ref=ins[w].at[half(w, c)], dst_ref=outs[w].at[chip, half(w, c)],
                send_sem=sems.send(6 * w + m - 1), recv_sem=sems.recv(6 * w + m - 1),
                device_id=_xor_peer(x, y, c, 2 * m), device_id_type=MESH)

        def landed(w, m):
            return pltpu.make_async_remote_copy(
                src_ref=ins[w].at[half(w, c)], dst_ref=outs[w].at[chip ^ m, half(w, c)],
                send_sem=sems.send(6 * w + m - 1), recv_sem=sems.recv(6 * w + m - 1),
                device_id=_xor_peer(x, y, c, 2 * m), device_id_type=MESH)

        def passed(w, m, which):
            part = outs[w].at[chip ^ m, half(w, which)]
            return pltpu.make_async_remote_copy(
                src_ref=part, dst_ref=part, send_sem=sems.send(6 * w + 2 + m), recv_sem=sems.recv(6 * w + 2 + m),
                device_id=(x, y, 1 - c), device_id_type=MESH)

        return c, first, landed, passed

    pairs = [(w, m) for w in range(n) for m in range(1, 4)]

    def start(ins, outs, sems):
        _, first, _, _ = copies(ins, outs, sems)
        for w, m in pairs:
            first(w, m).start()

    def finish(ins, outs, sems):
        c, first, landed, passed = copies(ins, outs, sems)
        for w, m in pairs:
            landed(w, m).wait_recv()
            passed(w, m, c).start()
        for w, m in pairs:
            passed(w, m, 1 - c).wait_recv()
        for w, m in pairs:
            first(w, m).wait_send()
            passed(w, m, c).wait_send()

    return _Carry(shards, [jax.ShapeDtypeStruct((N_CHIPS,) + s.shape, s.dtype) for s in shards], 6 * n, start, finish)


def _pair_exchange_carry(grads):
    n = len(grads)

    def copy(ins, outs, sems, w):
        x, y, c = _me()
        h = grads[w].shape[1] // 2
        return pltpu.make_async_remote_copy(
            src_ref=ins[w].at[:, pl.ds((1 - c) * h, h)], dst_ref=outs[w],
            send_sem=sems.send(w), recv_sem=sems.recv(w), device_id=(x, y, 1 - c), device_id_type=MESH)

    def start(ins, outs, sems):
        for w in range(n):
            copy(ins, outs, sems, w).start()

    def finish(ins, outs, sems):
        for w in range(n):
            copy(ins, outs, sems, w).wait()

    return _Carry(grads, [jax.ShapeDtypeStruct((N_CHIPS, g.shape[1] // 2, g.shape[2]), g.dtype) for g in grads], n,
                  start, finish)


def _chip_exchange_carry(partials):
    n = len(partials)

    def copier(ins, outs, sems):
        x, y, c = _me()
        chip = 2 * x + y

        def copy(w, m, landed):
            return pltpu.make_async_remote_copy(
                src_ref=ins[w].at[chip ^ m], dst_ref=outs[w].at[(chip ^ m) if landed else chip],
                send_sem=sems.send(3 * w + m - 1), recv_sem=sems.recv(3 * w + m - 1),
                device_id=_xor_peer(x, y, c, 2 * m), device_id_type=MESH)

        return copy

    pairs = [(w, m) for w in range(n) for m in range(1, 4)]

    def start(ins, outs, sems):
        copy = copier(ins, outs, sems)
        for w, m in pairs:
            copy(w, m, False).start()

    def finish(ins, outs, sems):
        copy = copier(ins, outs, sems)
        for w, m in pairs:
            copy(w, m, True).wait_recv()
        for w, m in pairs:
            copy(w, m, False).wait_send()

    return _Carry(partials, [jax.ShapeDtypeStruct(p.shape, p.dtype) for p in partials], 3 * n, start, finish)


def _pair_share_carry(shards):
    n = len(shards)

    def copier(ins, outs, sems):
        x, y, c = _me()

        def copy(w, which):
            h = shards[w].shape[0] // 2
            rows = pl.ds(which * h, h)
            return pltpu.make_async_remote_copy(
                src_ref=ins[w].at[rows], dst_ref=outs[w].at[rows],
                send_sem=sems.send(w), recv_sem=sems.recv(w), device_id=(x, y, 1 - c), device_id_type=MESH)

        return c, copy

    def start(ins, outs, sems):
        c, copy = copier(ins, outs, sems)
        for w in range(n):
            copy(w, c).start()

    def finish(ins, outs, sems):
        c, copy = copier(ins, outs, sems)
        for w in range(n):
            copy(w, 1 - c).wait_recv()
        for w in range(n):
            copy(w, c).wait_send()

    return _Carry(shards, [jax.ShapeDtypeStruct(s.shape, s.dtype) for s in shards], n, start, finish,
                  aliased=[(w, w) for w in range(n)])


def _pair_sum(g, part, idx, name):
    _, h, C = part.shape
    tr = min(512, h)
    nb = h // tr

    def body(idx_ref, g_ref, p_ref, o16_ref, own_ref):
        v = g_ref[...] + p_ref[...]
        o16_ref[...] = v.astype(BF16)

        @pl.when(pl.program_id(1) == idx_ref[1])
        def _():
            own_ref[...] = v

    return pl.pallas_call(
        body, name=name,
        grid_spec=pltpu.PrefetchScalarGridSpec(
            num_scalar_prefetch=1, grid=(nb, N_CHIPS),
            in_specs=[pl.BlockSpec((None, tr, C), lambda i, s, idx_ref: (s, idx_ref[0] * nb + i, 0)),
                      pl.BlockSpec((None, tr, C), lambda i, s, idx_ref: (s, i, 0))],
            out_specs=[pl.BlockSpec((None, tr, C), lambda i, s, idx_ref: (s, i, 0)),
                       pl.BlockSpec((tr, C), lambda i, s, idx_ref: (i, 0))]),
        out_shape=[jax.ShapeDtypeStruct(part.shape, BF16), jax.ShapeDtypeStruct((h, C), F32)],
        compiler_params=pltpu.CompilerParams(dimension_semantics=("arbitrary", "arbitrary"), vmem_limit_bytes=VMEM_LIMIT),
    )(idx, g, part)


def _chip_sum(own, slots, idx, name):
    h, C = own.shape
    tr = min(512, h)
    nb = h // tr

    def body(idx_ref, own_ref, s1_ref, s2_ref, s3_ref, o_ref):
        del idx_ref
        o_ref[...] = ((own_ref[...] + s1_ref[...].astype(F32)) + s2_ref[...].astype(F32)) + s3_ref[...].astype(F32)

    def slot(m):
        return pl.BlockSpec((None, tr, C), lambda i, idx_ref: (idx_ref[1] ^ m, i, 0))

    return pl.pallas_call(
        body, name=name,
        grid_spec=pltpu.PrefetchScalarGridSpec(
            num_scalar_prefetch=1, grid=(nb,),
            in_specs=[pl.BlockSpec((tr, C), lambda i, idx_ref: (i, 0)), slot(1), slot(2), slot(3)],
            out_specs=pl.BlockSpec((tr, C), lambda i, idx_ref: (idx_ref[0] * nb + i, 0))),
        out_shape=jax.ShapeDtypeStruct((2 * h, C), F32),
        compiler_params=pltpu.CompilerParams(dimension_semantics=("parallel",), vmem_limit_bytes=VMEM_LIMIT),
    )(idx, own, slots, slots, slots)


class _Reducer:
    def __init__(self, idx):
        self.idx, self.chips, self.p16, self.own, self.mine, self.final = idx, {}, {}, {}, {}, {}

    def add(self, name, whole, chip_blocks=False):
        self.chips[name] = whole if chip_blocks else _chips_from_whole(name, whole)

    def pair(self, names):
        return _pair_exchange_carry([self.chips[n] for n in names])

    def take_pair(self, names, outs):
        for n, part in zip(names, outs):
            self.p16[n], self.own[n] = _pair_sum(self.chips.pop(n), part, self.idx, "pair_sum_" + n)

    def chip(self, names):
        return _chip_exchange_carry([self.p16[n] for n in names])

    def take_chip(self, names, outs):
        for n, slots in zip(names, outs):
            del self.p16[n]
            self.mine[n] = _chip_sum(self.own.pop(n), slots, self.idx, "chip_sum_" + n)

    def share(self, names):
        return _pair_share_carry([self.mine[n] for n in names])

    def take_share(self, names, outs):
        for n, s in zip(names, outs):
            del self.mine[n]
            self.final[n] = s


def _w_ada_grad(c8, dmod_cols):
    n_cols = dmod_cols.shape[1]
    tn = 512

    def body(c_ref, d_ref, o_ref):
        cv = c_ref[...]
        o_ref[...] = _tn(cv * _sigmoid(cv), d_ref[...], precision=HIGH)

    return _call(body, [c8, dmod_cols], name="w_ada_grad", grid=(n_cols // tn,),
                 in_specs=[pl.BlockSpec((8, D), lambda j: (0, 0)), pl.BlockSpec((8, tn), lambda j: (0, j))],
                 out_specs=[pl.BlockSpec((D, tn), lambda j: (0, j))],
                 out_shape=[jax.ShapeDtypeStruct((D, n_cols), F32)], sem=("parallel",))[0]


_SMALL_SEGS = (("dmod", 6144), ("norm_mix_w", 1024), ("conv_b", 3072), ("ssd_norm_w", 2048), ("pool_scale", 1024),
               ("norm_mlp_w", 1024), ("norm_final_w", 1024), ("conv_w", 4 * XBC), ("d_skip", 2048), ("a_log", 128),
               ("dt_bias", 128), ("loss", 128))
SMALL_OFF = {}
_o = 0
for _n, _s in _SMALL_SEGS:
    SMALL_OFF[_n] = _o
    _o += _s
SMALL_LEN = -(-_o // 1024) * 1024

_FIRST = ("w_in", "conv_w")
_LATER = ("w_branch_ssd", "pool_w", "w_branch_pool", "w_out", "w_up", "w_down")
_SMALL_REPLICATED = ("b_ada", "norm_mix_w", "conv_b", "dt_bias", "a_log", "d_skip", "ssd_norm_w", "pool_scale",
                     "norm_mlp_w", "norm_final_w")
_WEIGHTS = ("w_ada", "b_ada", "norm_mix_w", "w_in", "conv_w", "conv_b", "dt_bias", "a_log", "d_skip", "ssd_norm_w",
            "w_branch_ssd", "pool_w", "pool_scale", "w_branch_pool", "w_out", "norm_mlp_w", "w_up", "w_down",
            "norm_final_w")


def _shard_2d(name, a):
    if name == "conv_w":
        return a.reshape(16, -1)
    return (a.reshape(GW, GW) if name == "pool_w" else a.reshape(a.shape[-2], a.shape[-1])).astype(BF16)


def _whole_from_chips(name, g, own, chip):
    g = lax.dynamic_update_slice(g, own[None], (chip, 0, 0))
    if name == "w_in":
        return _perm_cols(jnp.transpose(g, (1, 0, 2)).reshape(D, IN_COLS))
    if name == "w_up":
        return jnp.transpose(g, (1, 0, 2)).reshape(D, DFF)
    if name == "pool_w":
        return jnp.transpose(g.reshape(N_CHIPS, 4, GW // N_CHIPS, GW), (1, 0, 2, 3)).reshape(4, GW, GW)
    if name == "conv_w":
        return jnp.transpose(g.reshape(N_CHIPS, 4, XBC // N_CHIPS), (1, 0, 2)).reshape(4, XBC)
    return g.reshape(N_CHIPS * g.shape[1], g.shape[2])


def _chips_from_whole(name, g):
    if name.startswith("w_in"):
        return jnp.transpose(_unperm_cols(g).reshape(g.shape[0], N_CHIPS, IN_COLS // N_CHIPS), (1, 0, 2))
    if name == "w_up":
        return jnp.transpose(g.reshape(D, N_CHIPS, DFF // N_CHIPS), (1, 0, 2))
    if name == "pool_w":
        return jnp.transpose(g.reshape(4, N_CHIPS, GW // N_CHIPS, GW), (1, 0, 2, 3)).reshape(N_CHIPS, GW, GW)
    return g.reshape(N_CHIPS, g.shape[0] // N_CHIPS, g.shape[1])


def kernel(x, c, w_ada, b_ada, norm_mix_w, w_in, conv_w, conv_b, dt_bias, a_log, d_skip, ssd_norm_w, w_branch_ssd, pool_w, pool_scale, w_branch_pool, w_out, norm_mlp_w, w_up, w_down, norm_final_w, loss_target, m_w_ada, m_b_ada, m_norm_mix_w, m_w_in, m_conv_w, m_conv_b, m_dt_bias, m_a_log, m_d_skip, m_ssd_norm_w, m_w_branch_ssd, m_pool_w, m_pool_scale, m_w_branch_pool, m_w_out, m_norm_mlp_w, m_w_up, m_w_down, m_norm_final_w, v_w_ada, v_b_ada, v_norm_mix_w, v_w_in, v_conv_w, v_conv_b, v_dt_bias, v_a_log, v_d_skip, v_ssd_norm_w, v_w_branch_ssd, v_pool_w, v_pool_scale, v_w_branch_pool, v_w_out, v_norm_mlp_w, v_w_up, v_w_down, v_norm_final_w):
    args = locals()
    w = {n: args[n] for n in _WEIGHTS}
    m = {n: args["m_" + n] for n in _WEIGHTS}
    v = {n: args["v_" + n] for n in _WEIGHTS}
    xi, yi, ci = _me()
    chip = 2 * xi + yi
    idx = jnp.stack([ci, chip]).astype(jnp.int32)
    ada_cols = w_ada.shape[-1]
    xs, target = x[0], loss_target[0]
    two_d = lambda n, a: a.reshape(GW, GW) if n == "pool_w" else a.reshape(-1, a.shape[-1])
    delta, new_m, new_v, g = {}, {}, {}, {}

    def adamw(n, carry=None):
        res = _adamw(two_d(n, w[n]), two_d(n, g[n]), two_d(n, m[n]), two_d(n, v[n]), "adamw_" + n, carry=carry)
        (delta[n], new_m[n], new_v[n]), extra = res if carry is not None else (res, None)
        return extra

    b_mine = lax.dynamic_slice(b_ada, (0, chip * ada_cols), (1, ada_cols))
    mod, c8 = _ada_fwd(c, w_ada[0], b_mine)
    c8 = c8[:, 0, :]
    shift_m, scale_m, gate_m, shift_f, scale_f, gate_f = [mod[:, D * i:D * (i + 1)] for i in range(6)]
    shards = {n: _shard_2d(n, w[n]) for n in _FIRST + _LATER}
    nf_w = norm_final_w.reshape(1, D)

    h1, first = _norm_mod(xs, norm_mix_w, scale_m, shift_m, "norm_mod_mix",
                          carry=_gather_carry([shards[n] for n in _FIRST]))
    p = {n: _whole_from_chips(n, a, shards[n], chip) for n, a in zip(_FIRST, first)}
    (proj,), later = _matmul(h1, p["w_in"], mode="nn", out_dtypes=[F32], name="mm_proj",
                             carry=_gather_carry([shards[n] for n in _LATER]))
    p.update({n: _whole_from_chips(n, a, shards[n], chip) for n, a in zip(_LATER, later)})
    xbc_a = _conv_fwd(proj, p["conv_w"], conv_b)
    dtb_c, alog_c = dt_bias.reshape(HEADS, 1), a_log.reshape(HEADS, 1)
    dsk_exp = jnp.repeat(d_skip, HEAD_DIM, axis=1)
    y, hin = _ssd_fwd(xbc_a, proj, dt_bias, a_log, dtb_c, alog_c, dsk_exp)
    yn = _gate_norm(y, proj, ssd_norm_w)
    (y_ssd,) = _matmul(yn, p["w_branch_ssd"], mode="nn", out_dtypes=[F32], name="mm_branch_ssd")
    pooled, pw_out, yps = _pool_fwd(proj, p["pool_w"], pool_scale)
    (y_pool,) = _matmul(yps, p["w_branch_pool"], mode="nn", out_dtypes=[F32], name="mm_branch_pool")
    merged = _merge(proj, y_ssd, y_pool)
    resid = lambda acc, r, gt: (r + gt * acc, acc)
    x2, mix = _matmul(merged, p["w_out"], mode="nn", out_dtypes=[F32, BF16], name="mm_out",
                      epi=resid, tile_extras=(xs,), row_extras=(gate_m,))
    h2 = _norm_mod(x2, norm_mlp_w, scale_f, shift_f, "norm_mod_mlp")
    relu2 = lambda acc: (acc, jnp.square(jnp.maximum(acc, 0.0)))
    up, act = _matmul(h2, p["w_up"], mode="nn", out_dtypes=[BF16, BF16], name="mm_up", epi=relu2)
    x3, down = _matmul(act, p["w_down"], mode="nn", out_dtypes=[F32, BF16], name="mm_down",
                       epi=resid, tile_extras=(x2,), row_extras=(gate_f,))

    red = _Reducer(idx)
    dx3, d_down, sums_f = _final_loss_bwd(x3, target, nf_w, down, gate_f)
    drelu2 = lambda acc, u: (acc * (2.0 * jnp.maximum(u.astype(F32), 0.0)),)
    (dup,) = _matmul(d_down, p["w_down"], mode="nt", out_dtypes=[BF16], name="mm_dact",
                     epi=drelu2, tile_extras=(up,))
    red.add("w_down", _matmul(act, d_down, mode="tn", out_dtypes=[F32], name="mm_g_down")[0])
    (dh2,), got = _matmul(dup, p["w_up"], mode="nt", out_dtypes=[F32], name="mm_dh2",
                          carry=red.pair(["w_down"]))
    red.take_pair(["w_down"], got)
    red.add("w_up", _matmul(h2, dup, mode="tn", out_dtypes=[F32], name="mm_g_up", chip_blocks=True)[0], chip_blocks=True)
    dx2, sums_2, dmix = _norm_mod_bwd(x2, dh2, dx3, norm_mlp_w, scale_f, "norm_mod_mlp_bwd", branch=mix, gate=gate_m)
    (dmerged,), got = _matmul(dmix, p["w_out"], mode="nt", out_dtypes=[F32], name="mm_dmerged",
                              carry=red.pair(["w_up"]))
    red.take_pair(["w_up"], got)
    red.add("w_out", _matmul(merged, dmix, mode="tn", out_dtypes=[F32], name="mm_g_out")[0])
    dy_ssd, dy_pool, dproj = _merge_bwd(dmerged, proj, y_ssd, y_pool)
    (dyp,), got = _matmul(dy_pool, p["w_branch_pool"], mode="nt", out_dtypes=[F32], name="mm_dyp",
                          carry=red.pair(["w_out"]))
    red.take_pair(["w_out"], got)
    red.add("w_branch_pool", _matmul(yps, dy_pool, mode="tn", out_dtypes=[F32], name="mm_g_bpool")[0])
    dproj, g_pool_w, sums_pool = _pool_bwd(dyp, pw_out, pooled, p["pool_w"], pool_scale, dproj)
    red.add("pool_w", g_pool_w)
    red.add("w_branch_ssd", _matmul(yn, dy_ssd, mode="tn", out_dtypes=[F32], name="mm_g_bssd")[0])
    mixers = ["w_branch_pool", "pool_w", "w_branch_ssd"]
    (dyn,), got = _matmul(dy_ssd, p["w_branch_ssd"], mode="nt", out_dtypes=[F32], name="mm_dyn",
                          carry=red.pair(mixers))
    red.take_pair(mixers, got)
    dy, dproj, sums_gn = _gate_norm_bwd(dyn, y, proj, ssd_norm_w, dproj)
    six = ["w_down", "w_up", "w_out"] + mixers
    (dxa, dproj, dsk_sum, ssd_small), got = _ssd_bwd(dy, xbc_a, proj, hin, dt_bias, a_log, dtb_c, alog_c, dsk_exp,
                                                     dproj, carry=red.chip(six))
    red.take_chip(six, got)
    dxc, sums_conv = _conv_bwd_a(dxa, proj, p["conv_w"], conv_b)
    dproj = _conv_bwd_b(dxc, p["conv_w"], dproj)
    rows_a = 3 * D // 4
    (g_in_a,), got = _matmul(h1, dproj, mode="tn", out_dtypes=[F32], name="mm_g_in_a", a_cols=(0, rows_a),
                             carry=red.share(six))
    red.take_share(six, got)
    red.add("w_in_a", g_in_a)
    (g_in_b,), got = _matmul(h1, dproj, mode="tn", out_dtypes=[F32], name="mm_g_in_b", a_cols=(rows_a, D - rows_a),
                             carry=red.pair(["w_in_a"]))
    red.take_pair(["w_in_a"], got)
    red.add("w_in_b", g_in_b)
    (dh1,), got = _matmul(dproj, p["w_in"], mode="nt", out_dtypes=[F32], name="mm_dh1",
                          carry=_join(red.chip(["w_in_a"]), red.pair(["w_in_b"])))
    red.take_chip(["w_in_a"], got[:1])
    red.take_pair(["w_in_b"], got[1:])
    grad_x, sums_1 = _norm_mod_bwd(xs, dh1, dx2, norm_mix_w, scale_m, "norm_mod_mix_bwd")

    dmod = jnp.concatenate([sums_1[0:1], sums_1[1:2], sums_2[3:4], sums_2[0:1], sums_2[1:2], sums_f[1:2]], axis=1)
    pad96 = jnp.zeros((1, 96), F32)
    small = {"dmod": dmod, "norm_mix_w": sums_1[2:3], "conv_b": sums_conv[4:5], "ssd_norm_w": sums_gn[0:1],
             "pool_scale": sums_pool[0:1], "norm_mlp_w": sums_2[2:3], "norm_final_w": sums_f[0:1],
             "conv_w": sums_conv[0:4].reshape(1, 4 * XBC), "d_skip": dsk_sum[0:1],
             "a_log": jnp.concatenate([ssd_small[0:1], pad96], axis=1),
             "dt_bias": jnp.concatenate([ssd_small[1:2], pad96], axis=1), "loss": sums_f[3:4, 0:128]}
    vec = jnp.concatenate([small[n] for n, _ in _SMALL_SEGS], axis=1)
    vec = jnp.pad(vec, ((0, 0), (0, SMALL_LEN - vec.shape[1]))).reshape(SMALL_LEN // 128, 128)
    (every, total, dsk), got = _gather_small(vec, carry=_join(red.chip(["w_in_b"]), red.share(["w_in_a"])))
    red.take_chip(["w_in_b"], got[:1])
    red.take_share(["w_in_a"], got[1:])
    total = total.reshape(1, SMALL_LEN)
    seg = lambda n, size: total[:, SMALL_OFF[n]:SMALL_OFF[n] + size]
    g.update({"b_ada": seg("dmod", 6 * D), "norm_mix_w": seg("norm_mix_w", D), "conv_b": seg("conv_b", XBC),
              "dt_bias": seg("dt_bias", HEADS), "a_log": seg("a_log", HEADS), "d_skip": dsk[:, 0:2].reshape(1, HEADS),
              "ssd_norm_w": seg("ssd_norm_w", DI), "pool_scale": seg("pool_scale", D),
              "norm_mlp_w": seg("norm_mlp_w", D), "norm_final_w": seg("norm_final_w", D)})
    loss = total[0, SMALL_OFF["loss"]]
    conv_cols = conv_w.shape[-1]
    g["conv_w"] = lax.dynamic_slice(seg("conv_w", 4 * XBC).reshape(4, XBC), (0, chip * conv_cols), (4, conv_cols))
    dmod8 = every.reshape(8, SMALL_LEN)[:, SMALL_OFF["dmod"]:SMALL_OFF["dmod"] + 6 * D]
    g["w_ada"] = _w_ada_grad(c8, lax.dynamic_slice(dmod8, (0, chip * ada_cols), (8, ada_cols)))

    got = adamw("w_ada", carry=red.share(["w_in_b"]))
    red.take_share(["w_in_b"], got)
    for n in six:
        g[n] = red.final[n]
    g["w_in"] = jnp.concatenate([red.final["w_in_a"], red.final["w_in_b"]], axis=0)
    for n in ["conv_w", "w_in"] + six:
        adamw(n)
    sizes = [w[n].size for n in _SMALL_REPLICATED]
    n_small = -(-sum(sizes) // 1024) * 1024
    pack = lambda d: jnp.pad(jnp.concatenate([d[n].reshape(1, -1) for n in _SMALL_REPLICATED], axis=1),
                             ((0, 0), (0, n_small - sum(sizes)))).reshape(n_small // 128, 128)
    d_, m_, v_ = _adamw(pack(w), pack(g), pack(m), pack(v), "adamw_small")
    off = 0
    for n, s in zip(_SMALL_REPLICATED, sizes):
        for dst, src in ((delta, d_), (new_m, m_), (new_v, v_)):
            dst[n] = src.reshape(1, n_small)[:, off:off + s]
        off += s

    out = [loss, grad_x.reshape(x.shape)]
    for d in (g, delta, new_m, new_v):
        out += [d[n].reshape(w[n].shape) for n in _WEIGHTS]
    return tuple(out)
```

```python
import functools
import operator

import jax
import jax.numpy as jnp
import numpy as np
from jax import lax
from jax.experimental import pallas as pl
from jax.experimental.pallas import tpu as pltpu

F32, BF16 = jnp.float32, jnp.bfloat16
HIGH = lax.Precision.HIGHEST
MESH = pl.DeviceIdType.MESH

D = 1024
DI = 2048
HEADS, HEAD_DIM = 32, 64
GROUPS, STATE = 4, 128
Q = 128
XBC = DI + 2 * GROUPS * STATE
POOL_WINDOWS = (2, 4, 8, 16)
GW = 256
DFF = 4096
EPS = 1e-5
IN_COLS = 8224
OFF_Z, OFF_XBC, OFF_POOL, OFF_GATE, OFF_DT, NP = 0, 2048, 5120, 6144, 8192, 8448
N_CHIPS = 4
ADAM_LR, ADAM_B1, ADAM_B2, ADAM_EPS, ADAM_WD, ADAM_STEP = 0.001, 0.9, 0.999, 1e-08, 0.01, 10
VMEM_LIMIT = 56 * 2 ** 20
NEG = -1e30


def _sigmoid(v):
    return 0.5 * jnp.tanh(0.5 * v) + 0.5


def _softplus(v):
    return jnp.maximum(v, 0.0) + jnp.log1p(jnp.exp(-jnp.abs(v)))


def _dot(a, b, dims, **kw):
    return lax.dot_general(a, b, (dims, ((), ())), preferred_element_type=F32, **kw)


def _nn(a, b, **kw):
    return _dot(a, b, ((1,), (0,)), **kw)


def _nt(a, b, **kw):
    return _dot(a, b, ((1,), (1,)), **kw)


def _tn(a, b, **kw):
    return _dot(a, b, ((0,), (0,)), **kw)


def _perm_cols(w):
    pad = jnp.zeros(w.shape[:-1] + (NP - IN_COLS,), w.dtype)
    return jnp.concatenate([w[..., :5120], w[..., 5152:], w[..., 5120:5152], pad], axis=-1)


def _unperm_cols(g):
    return jnp.concatenate([g[..., :5120], g[..., OFF_DT:OFF_DT + 32], g[..., 5120:OFF_DT]], axis=-1)


class _Sems:
    def __init__(self, send, recv, local, base=0):
        self._send, self._recv, self._local, self._base = send, recv, local, base

    def shift(self, n):
        return _Sems(self._send, self._recv, self._local, self._base + n)

    def send(self, i):
        return self._send.at[self._base + i]

    def recv(self, i):
        return self._recv.at[self._base + i]

    def local(self, i):
        return self._local.at[self._base + i]


class _Carry:
    def __init__(self, ins, out_shapes, n_sems, start, finish, aliased=()):
        self.ins, self.out_shapes, self.n_sems, self.start, self.finish = list(ins), list(out_shapes), n_sems, start, finish
        self.aliased = list(aliased)


def _join(*carries):
    def run(which):
        def fn(ins, outs, sems):
            i = o = s = 0
            for cy in carries:
                getattr(cy, which)(ins[i:i + len(cy.ins)], outs[o:o + len(cy.out_shapes)], sems.shift(s))
                i, o, s = i + len(cy.ins), o + len(cy.out_shapes), s + cy.n_sems
        return fn

    aliased, i, o = [], 0, 0
    for cy in carries:
        aliased += [(i + a, o + b) for a, b in cy.aliased]
        i, o = i + len(cy.ins), o + len(cy.out_shapes)
    return _Carry([a for cy in carries for a in cy.ins], [a for cy in carries for a in cy.out_shapes],
                  sum(cy.n_sems for cy in carries), run("start"), run("finish"), aliased)


def _call(body, args, *, name, grid=(), in_specs, out_specs, out_shape, scratch_shapes=(), sem=None, aliases=None,
          carry=None):
    in_specs, out_specs, out_shape, scratch_shapes = list(in_specs), list(out_specs), list(out_shape), list(scratch_shapes)
    n_in, n_out, n_scr = len(in_specs), len(out_specs), len(scratch_shapes)
    kw = {"vmem_limit_bytes": VMEM_LIMIT}
    if carry is None:
        kernel_fn = functools.partial(body)
        if sem is not None:
            kw["dimension_semantics"] = sem
    else:
        n_ci, n_co = len(carry.ins), len(carry.out_shapes)
        hbm = pl.BlockSpec(memory_space=pl.ANY)
        in_specs += [hbm] * n_ci
        out_specs += [hbm] * n_co
        out_shape += carry.out_shapes
        n_s = max(carry.n_sems, 1)
        scratch_shapes += [pltpu.SemaphoreType.DMA((n_s,))] * 3
        args = list(args) + carry.ins
        aliases = dict(aliases or {})
        aliases.update({n_in + i: n_out + o for i, o in carry.aliased})
        if grid:
            kw["dimension_semantics"] = ("arbitrary",) * len(grid)

        def kernel_fn(*refs):
            a = n_in
            ins, c_ins = refs[:a], refs[a:a + n_ci]
            a += n_ci
            outs, c_outs = refs[a:a + n_out], refs[a + n_out:a + n_out + n_co]
            a += n_out + n_co
            scr, sems = refs[a:a + n_scr], _Sems(*refs[a + n_scr:a + n_scr + 3])
            if grid:
                ids = [pl.program_id(d) for d in range(len(grid))]
                first = functools.reduce(operator.and_, [i == 0 for i in ids])
                last = functools.reduce(operator.and_, [i == g - 1 for i, g in zip(ids, grid)])

                @pl.when(first)
                def _():
                    carry.start(c_ins, c_outs, sems)

                body(*ins, *outs, *scr)

                @pl.when(last)
                def _():
                    carry.finish(c_ins, c_outs, sems)
            else:
                carry.start(c_ins, c_outs, sems)
                body(*ins, *outs, *scr)
                carry.finish(c_ins, c_outs, sems)

    outs = pl.pallas_call(
        kernel_fn, name=name, grid=grid, in_specs=in_specs, out_specs=out_specs, out_shape=out_shape,
        scratch_shapes=scratch_shapes, input_output_aliases=aliases or {},
        compiler_params=pltpu.CompilerParams(**kw),
    )(*args)
    outs = list(outs)
    return outs if carry is None else (outs[:n_out], outs[n_out:])


def _run_carry(carry, name):
    _, outs = _call(lambda: None, [], name=name, in_specs=[], out_specs=[], out_shape=[], carry=carry)
    return outs


_TILES = {
    "mm_proj": (1024, 2816, 1024), "mm_branch_ssd": (1024, 1024, 2048), "mm_branch_pool": (1024, 1024, 1024),
    "mm_out": (1024, 1024, 1024), "mm_up": (1024, 1024, 1024), "mm_down": (512, 1024, 4096),
    "mm_dact": (1024, 1024, 1024), "mm_g_down": (1024, 1024, 2048), "mm_dh2": (1024, 1024, 4096),
    "mm_g_up": (1024, 1024, 2048), "mm_dmerged": (1024, 1024, 1024), "mm_g_out": (1024, 1024, 2048),
    "mm_dyp": (1024, 1024, 1024), "mm_g_bpool": (1024, 1024, 2048), "mm_g_bssd": (1024, 1024, 2048),
    "mm_dyn": (1024, 1024, 1024), "mm_g_in_a": (768, 1408, 2048), "mm_g_in_b": (256, 2816, 2048),
    "mm_dh1": (1024, 1024, 2816),
}


def _matmul(a, b, *, mode, out_dtypes, name, epi=None, tile_extras=(), row_extras=(), carry=None, a_cols=None,
            chip_blocks=False):
    M, K = (a.shape[1], a.shape[0]) if mode == "tn" else a.shape
    N = b.shape[0] if mode == "nt" else b.shape[1]
    a_start, M = a_cols if a_cols is not None else (0, M)
    tm, tn, tk = _TILES[name]
    tm, tn, tk = min(tm, M), min(tn, N), min(tk, K)
    assert M % tm == 0 and N % tn == 0 and K % tk == 0 and a_start % tm == 0, (name, M, N, K, tm, tn, tk)
    a_off = a_start // tm
    if mode == "nn":
        a_spec = pl.BlockSpec((tm, tk), lambda i, j, k: (i, k))
        b_spec = pl.BlockSpec((tk, tn), lambda i, j, k: (k, j))
        dims = ((1,), (0,))
    elif mode == "nt":
        a_spec = pl.BlockSpec((tm, tk), lambda i, j, k: (i, k))
        b_spec = pl.BlockSpec((tn, tk), lambda i, j, k: (j, k))
        dims = ((1,), (1,))
    else:
        a_spec = pl.BlockSpec((tk, tm), lambda i, j, k: (k, i + a_off))
        b_spec = pl.BlockSpec((tk, tn), lambda i, j, k: (k, j))
        dims = ((0,), (0,))
    nk = K // tk
    n_te, n_re, n_out = len(tile_extras), len(row_extras), len(out_dtypes)
    if epi is None:
        epi = lambda acc: (acc,)

    def body(a_ref, b_ref, *rest):
        extras = rest[:n_te + n_re]
        outs = rest[n_te + n_re:n_te + n_re + n_out]
        p = _dot(a_ref[...], b_ref[...], dims)

        def finish(acc):
            vals = epi(acc, *[e[...] for e in extras])
            for o, v in zip(outs, vals):
                o[...] = v.astype(o.dtype)

        if nk == 1:
            finish(p)
        else:
            acc_ref = rest[-1]
            k = pl.program_id(2)

            @pl.when(k == 0)
            def _():
                acc_ref[...] = p

            @pl.when(k > 0)
            def _():
                acc_ref[...] += p

            @pl.when(k == nk - 1)
            def _():
                finish(acc_ref[...])

    tile_spec = pl.BlockSpec((tm, tn), lambda i, j, k: (i, j))
    row_spec = pl.BlockSpec((1, tn), lambda i, j, k: (0, j))
    out_spec, out_dims = tile_spec, (M, N)
    if chip_blocks:
        assert n_te == 0 and tn * N_CHIPS == N
        out_spec, out_dims = pl.BlockSpec((None, tm, tn), lambda i, j, k: (j, i, 0)), (N_CHIPS, M, tn)
    return _call(
        body, [a, b, *tile_extras, *row_extras], name=name, grid=(M // tm, N // tn, nk),
        in_specs=[a_spec, b_spec] + [tile_spec] * n_te + [row_spec] * n_re, out_specs=[out_spec] * n_out,
        out_shape=[jax.ShapeDtypeStruct(out_dims, dt) for dt in out_dtypes],
        scratch_shapes=[pltpu.VMEM((tm, tn), F32)] if nk > 1 else [],
        sem=("parallel", "parallel", "arbitrary"), carry=carry)


def _row_tile(T):
    return min(512, T)


def _norm_mod(x, nw, scale, shift, name, carry=None):
    T = x.shape[0]
    tr = _row_tile(T)

    def body(x_ref, nw_ref, sc_ref, sh_ref, o_ref):
        xv = x_ref[...]
        r = lax.rsqrt(jnp.mean(xv * xv, axis=-1, keepdims=True) + EPS)
        o_ref[...] = ((xv * r) * nw_ref[...] * (1.0 + sc_ref[...]) + sh_ref[...]).astype(BF16)

    tile = pl.BlockSpec((tr, D), lambda i: (i, 0))
    row = pl.BlockSpec((1, D), lambda i: (0, 0))
    res = _call(body, [x, nw, scale, shift], name=name, grid=(T // tr,), in_specs=[tile, row, row, row],
                out_specs=[tile], out_shape=[jax.ShapeDtypeStruct((T, D), BF16)], sem=("parallel",), carry=carry)
    return res[0] if carry is None else (res[0][0], res[1])


def _norm_mod_bwd(x, dh, dres, nw, scale, name, branch=None, gate=None, carry=None):
    T = x.shape[0]
    tr = _row_tile(T)
    with_branch = branch is not None

    def body(x_ref, dh_ref, dr_ref, nw_ref, sc_ref, *rest):
        if with_branch:
            br_ref, g_ref, dx_ref, sums_ref, db_ref = rest
        else:
            dx_ref, sums_ref = rest
        i = pl.program_id(0)

        @pl.when(i == 0)
        def _():
            sums_ref[...] = jnp.zeros_like(sums_ref)

        xv, dhv = x_ref[...], dh_ref[...]
        r = lax.rsqrt(jnp.mean(xv * xv, axis=-1, keepdims=True) + EPS)
        xn = xv * r
        g1 = dhv * (1.0 + sc_ref[...])
        dxn = g1 * nw_ref[...]
        dx = dr_ref[...] + r * (dxn - xn * jnp.mean(dxn * xn, axis=-1, keepdims=True))
        dx_ref[...] = dx
        sums_ref[0:1, :] += jnp.sum(dhv, axis=0, keepdims=True)
        sums_ref[1:2, :] += jnp.sum(dhv * (xn * nw_ref[...]), axis=0, keepdims=True)
        sums_ref[2:3, :] += jnp.sum(g1 * xn, axis=0, keepdims=True)
        if with_branch:
            db_ref[...] = (dx * g_ref[...]).astype(BF16)
            sums_ref[3:4, :] += jnp.sum(dx * br_ref[...], axis=0, keepdims=True)

    tile = pl.BlockSpec((tr, D), lambda i: (i, 0))
    row = pl.BlockSpec((1, D), lambda i: (0, 0))
    sums = pl.BlockSpec((8, D), lambda i: (0, 0))
    ins = [x, dh, dres, nw, scale] + ([branch, gate] if with_branch else [])
    in_specs = [tile, tile, tile, row, row] + ([tile, row] if with_branch else [])
    out_specs = [tile, sums] + ([tile] if with_branch else [])
    out_shape = [jax.ShapeDtypeStruct((T, D), F32), jax.ShapeDtypeStruct((8, D), F32)]
    if with_branch:
        out_shape.append(jax.ShapeDtypeStruct((T, D), BF16))
    return _call(body, ins, name=name, grid=(T // tr,), in_specs=in_specs, out_specs=out_specs, out_shape=out_shape,
                 sem=("arbitrary",), carry=carry)


def _final_loss_bwd(x3, target, wf, down, gate_f):
    T = x3.shape[0]
    tr = _row_tile(T)
    n_steps = T // tr

    def body(x_ref, t_ref, w_ref, dn_ref, g_ref, dx_ref, dd_ref, sums_ref):
        i = pl.program_id(0)

        @pl.when(i == 0)
        def _():
            sums_ref[...] = jnp.zeros_like(sums_ref)

        xv = x_ref[...]
        r = lax.rsqrt(jnp.mean(xv * xv, axis=-1, keepdims=True) + EPS)
        xn = xv * r
        err = xn * w_ref[...] - t_ref[...]
        dy = err * (1.0 / D)
        dxn = dy * w_ref[...]
        dx = r * (dxn - xn * jnp.mean(dxn * xn, axis=-1, keepdims=True))
        dx_ref[...] = dx
        dd_ref[...] = (dx * g_ref[...]).astype(BF16)
        sums_ref[0:1, :] += jnp.sum(dy * xn, axis=0, keepdims=True)
        sums_ref[1:2, :] += jnp.sum(dx * dn_ref[...], axis=0, keepdims=True)
        sums_ref[2:3, :] += jnp.sum(err * err, axis=0, keepdims=True) * (0.5 / D)

        @pl.when(i == n_steps - 1)
        def _():
            sums_ref[3:4, :] = jnp.broadcast_to(jnp.sum(sums_ref[2:3, :], axis=1, keepdims=True), (1, D))

    tile = pl.BlockSpec((tr, D), lambda i: (i, 0))
    row = pl.BlockSpec((1, D), lambda i: (0, 0))
    sums = pl.BlockSpec((8, D), lambda i: (0, 0))
    return _call(body, [x3, target, wf, down, gate_f], name="final_loss_bwd", grid=(n_steps,),
                 in_specs=[tile, tile, row, tile, row], out_specs=[tile, tile, sums],
                 out_shape=[jax.ShapeDtypeStruct((T, D), F32), jax.ShapeDtypeStruct((T, D), BF16),
                            jax.ShapeDtypeStruct((8, D), F32)], sem=("arbitrary",))


CONV_TC = 1024


def _conv_taps(xp, w, b):
    acc = b + w[3:4, :] * xp
    for k in range(3):
        acc = acc + w[k:k + 1, :] * pltpu.roll(xp, 3 - k, 0)
    return acc


def _conv_fwd(proj, conv_w, conv_b):
    T = proj.shape[0]
    tr = _row_tile(T)
    nb, offb = tr // 8, OFF_XBC // CONV_TC

    def body(x_ref, h_ref, w_ref, b_ref, o_ref):
        halo = jnp.where(pl.program_id(0) > 0, h_ref[...], 0.0)
        xp = jnp.concatenate([halo, x_ref[...]], axis=0)
        acc = _conv_taps(xp, w_ref[...], b_ref[...])[8:]
        o_ref[...] = acc * _sigmoid(acc)

    return _call(
        body, [proj, proj, conv_w, conv_b], name="conv_fwd", grid=(T // tr, XBC // CONV_TC),
        in_specs=[pl.BlockSpec((tr, CONV_TC), lambda i, j: (i, j + offb)),
                  pl.BlockSpec((8, CONV_TC), lambda i, j: (jnp.maximum(i * nb - 1, 0), j + offb)),
                  pl.BlockSpec((4, CONV_TC), lambda i, j: (0, j)),
                  pl.BlockSpec((1, CONV_TC), lambda i, j: (0, j))],
        out_specs=[pl.BlockSpec((tr, CONV_TC), lambda i, j: (i, j))],
        out_shape=[jax.ShapeDtypeStruct((T, XBC), F32)], sem=("parallel", "parallel"))[0]


def _conv_bwd_a(dxa, proj, conv_w, conv_b):
    T = proj.shape[0]
    tr = _row_tile(T)
    nb, offb = tr // 8, OFF_XBC // CONV_TC

    def body(d_ref, x_ref, h_ref, w_ref, b_ref, o_ref, sums_ref):
        i = pl.program_id(1)

        @pl.when(i == 0)
        def _():
            sums_ref[...] = jnp.zeros_like(sums_ref)

        halo = jnp.where(i > 0, h_ref[...], 0.0)
        xp = jnp.concatenate([halo, x_ref[...]], axis=0)
        w = w_ref[...]
        taps = [pltpu.roll(xp, 3 - k, 0)[8:] for k in range(3)] + [x_ref[...]]
        acc = b_ref[...] + w[3:4, :] * taps[3]
        for k in range(3):
            acc = acc + w[k:k + 1, :] * taps[k]
        s = _sigmoid(acc)
        dxc = d_ref[...] * (s * (1.0 + acc * (1.0 - s)))
        o_ref[...] = dxc
        for k in range(4):
            sums_ref[k:k + 1, :] += jnp.sum(dxc * taps[k], axis=0, keepdims=True)
        sums_ref[4:5, :] += jnp.sum(dxc, axis=0, keepdims=True)

    return _call(
        body, [dxa, proj, proj, conv_w, conv_b], name="conv_bwd_a", grid=(XBC // CONV_TC, T // tr),
        in_specs=[pl.BlockSpec((tr, CONV_TC), lambda j, i: (i, j)),
                  pl.BlockSpec((tr, CONV_TC), lambda j, i: (i, j + offb)),
                  pl.BlockSpec((8, CONV_TC), lambda j, i: (jnp.maximum(i * nb - 1, 0), j + offb)),
                  pl.BlockSpec((4, CONV_TC), lambda j, i: (0, j)),
                  pl.BlockSpec((1, CONV_TC), lambda j, i: (0, j))],
        out_specs=[pl.BlockSpec((tr, CONV_TC), lambda j, i: (i, j)), pl.BlockSpec((8, CONV_TC), lambda j, i: (0, j))],
        out_shape=[jax.ShapeDtypeStruct((T, XBC), F32), jax.ShapeDtypeStruct((8, XBC), F32)],
        sem=("parallel", "arbitrary"))


def _conv_bwd_b(dxc, conv_w, dproj):
    T = dxc.shape[0]
    tr = _row_tile(T)
    nb, offb, last = tr // 8, OFF_XBC // CONV_TC, T // tr - 1

    def body(d_ref, h_ref, w_ref, dp_in, o_ref):
        del dp_in
        halo = jnp.where(pl.program_id(0) < last, h_ref[...], 0.0)
        xp = jnp.concatenate([d_ref[...], halo], axis=0)
        n = xp.shape[0]
        w = w_ref[...]
        acc = w[3:4, :] * xp
        for k in range(3):
            acc = acc + w[k:k + 1, :] * pltpu.roll(xp, n - (3 - k), 0)
        o_ref[...] = acc[:tr].astype(BF16)

    return _call(
        body, [dxc, dxc, conv_w, dproj], name="conv_bwd_b", grid=(T // tr, XBC // CONV_TC),
        in_specs=[pl.BlockSpec((tr, CONV_TC), lambda i, j: (i, j)),
                  pl.BlockSpec((8, CONV_TC), lambda i, j: (jnp.minimum((i + 1) * nb, T // 8 - 1), j)),
                  pl.BlockSpec((4, CONV_TC), lambda i, j: (0, j)),
                  pl.BlockSpec(memory_space=pl.ANY)],
        out_specs=[pl.BlockSpec((tr, CONV_TC), lambda i, j: (i, j + offb))],
        out_shape=[jax.ShapeDtypeStruct(dproj.shape, BF16)], aliases={3: 0}, sem=("parallel", "parallel"))[0]


def _spread(v, sel, pieces):
    out = None
    for _ in range(pieces):
        p = v.astype(BF16)
        term = _nn(p, sel)
        out = term if out is None else out + term
        v = v - p.astype(F32)
    return out


def _ssd_selectors():
    g = np.arange(GROUPS)[:, None, None]
    piece = np.arange(128)[None, :, None]
    h = np.where(piece < 3 * HEADS, piece % HEADS, -1)
    blocks = (h == 8 * g + np.arange(1024)[None, None, :] // 128)
    pairs = (h == 8 * g + np.arange(512)[None, None, :] // HEAD_DIM)
    lane = np.arange(128)[None, None, :]
    block_sum = (lane == 8 * g + np.arange(1024)[None, :, None] // 128)
    pair_sum = (lane == 8 * g + np.arange(512)[None, :, None] // HEAD_DIM)
    return [jnp.asarray(m, BF16) for m in (blocks, pairs, block_sum, pair_sum)]


def _pack3(v):
    p0 = v.astype(BF16)
    r1 = v - p0.astype(F32)
    p1 = r1.astype(BF16)
    r2 = r1 - p1.astype(F32)
    return p0 + pltpu.roll(r1, HEADS, 1).astype(BF16) + pltpu.roll(r2, 2 * HEADS, 1).astype(BF16)


def _ssd_group(g, cs_p, csT, dt_p, s_mat, causal_w, lo, blocks_ref, pairs_ref):
    csb = _nn(cs_p, blocks_ref[g])
    row = jnp.concatenate([csT[8 * g + hh:8 * g + hh + 1, :] for hh in range(8)], axis=1)
    l_w = jnp.exp(jnp.where(causal_w, csb - row, NEG))
    m_w = jnp.concatenate([s_mat] * 8, axis=1) * l_w
    cs_g = jnp.concatenate([jnp.where(lo, csb[:, 256 * jj:256 * jj + 128], csb[:, 256 * jj + 128:256 * jj + 256])
                            for jj in range(4)], axis=1)
    cs_last = cs_g[Q - 1:Q, :]
    return m_w, l_w, _nn(dt_p, pairs_ref[g]), jnp.exp(cs_g), jnp.exp(cs_last - cs_g), jnp.exp(cs_last)


def _ssd_common(dtp_ref, dtb_r, alog_r, dtb_c, alog_c):
    rows = lax.broadcasted_iota(jnp.int32, (Q, Q), 0)
    cols = lax.broadcasted_iota(jnp.int32, (Q, Q), 1)
    tri = (cols <= rows).astype(F32)
    heads = lax.broadcasted_iota(jnp.int32, (1, 128), 1) < HEADS
    raw_w = dtp_ref[...] + dtb_r[...]
    dt_w = jnp.where(heads, _softplus(raw_w), 0.0)
    a_w = -jnp.exp(alog_r[...])
    cs_w = _nn(tri, dt_w * a_w, precision=HIGH)
    aT = _softplus(dtp_ref[...].T[0:HEADS, :] + dtb_c[...]) * (-jnp.exp(alog_c[...]))
    csT = _nt(aT, tri, precision=HIGH)
    return raw_w[:, 0:HEADS], dt_w[:, 0:HEADS], a_w[:, 0:HEADS], csT, _pack3(cs_w), _pack3(dt_w)


def _ssd_fwd(xbc_a, proj, dtb_r, alog_r, dtb_c, alog_c, dsk_exp):
    T = xbc_a.shape[0]
    nc = T // Q
    dtb_r, alog_r = [jnp.pad(a, ((0, 0), (0, 128 - HEADS))) for a in (dtb_r, alog_r)]

    def body(xbc_ref, dtp_ref, dtb_r_ref, alog_r_ref, dtb_c_ref, alog_c_ref, dsk_ref, blocks_ref, pairs_ref,
             y_ref, hin_ref, h_scr):
        @pl.when(pl.program_id(0) == 0)
        def _():
            h_scr[...] = jnp.zeros_like(h_scr)

        _, _, _, csT, cs_p, dt_p = _ssd_common(dtp_ref, dtb_r_ref, alog_r_ref, dtb_c_ref, alog_c_ref)
        lo = lax.broadcasted_iota(jnp.int32, (1, 128), 1) < HEAD_DIM
        hi = jnp.logical_not(lo)
        causal_w = (lax.broadcasted_iota(jnp.int32, (Q, 1024), 1) & (Q - 1)) <= lax.broadcasted_iota(jnp.int32, (Q, 1024), 0)
        for g in range(GROUPS):
            gs = slice(512 * g, 512 * (g + 1))
            hs = slice(128 * g, 128 * (g + 1))
            xs_g = xbc_ref[:, gs]
            b_g = xbc_ref[:, DI + STATE * g:DI + STATE * (g + 1)].astype(BF16)
            c_g = xbc_ref[:, DI + 512 + STATE * g:DI + 512 + STATE * (g + 1)].astype(BF16)
            m_w, _, dt_g, ecs_g, dec_g, cd_g = _ssd_group(g, cs_p, csT, dt_p, _nt(c_g, b_g), causal_w, lo, blocks_ref, pairs_ref)
            m_b = m_w.astype(BF16)
            xdt = xs_g * dt_g
            xdt_b = xdt.astype(BF16)
            ys = []
            for jj in range(4):
                xp = xdt_b[:, 128 * jj:128 * (jj + 1)]
                x_ab = jnp.concatenate([jnp.where(lo, xp, jnp.zeros_like(xp)), jnp.where(hi, xp, jnp.zeros_like(xp))], axis=0)
                ys.append(_nn(m_b[:, 256 * jj:256 * (jj + 1)], x_ab))
            h_g = h_scr[hs, :]
            hin_ref[0, hs, :] = h_g
            y_ref[:, gs] = jnp.concatenate(ys, axis=1) + _nn(c_g, h_g.astype(BF16)) * ecs_g + dsk_ref[:, gs] * xs_g
            h_scr[hs, :] = h_g * cd_g + _tn(b_g, (xdt * dec_g).astype(BF16))

    small_r = pl.BlockSpec((1, 128), lambda c: (0, 0))
    small_c = pl.BlockSpec((HEADS, 1), lambda c: (0, 0))
    blocks, pairs, _, _ = _ssd_selectors()
    whole = lambda a: pl.BlockSpec(a.shape, lambda c: (0,) * a.ndim)
    return _call(
        body, [xbc_a, proj, dtb_r, alog_r, dtb_c, alog_c, dsk_exp, blocks, pairs], name="ssd_fwd", grid=(nc,),
        in_specs=[pl.BlockSpec((Q, XBC), lambda c: (c, 0)),
                  pl.BlockSpec((Q, 128), lambda c: (c, OFF_DT // 128)),
                  small_r, small_r, small_c, small_c,
                  pl.BlockSpec((1, DI), lambda c: (0, 0)), whole(blocks), whole(pairs)],
        out_specs=[pl.BlockSpec((Q, DI), lambda c: (c, 0)), pl.BlockSpec((1, 512, 512), lambda c: (c, 0, 0))],
        out_shape=[jax.ShapeDtypeStruct((T, DI), F32), jax.ShapeDtypeStruct((nc, 512, 512), F32)],
        scratch_shapes=[pltpu.VMEM((512, 512), F32)], sem=("arbitrary",))


def _ssd_bwd(dy, xbc_a, proj, hin, dtb_r, alog_r, dtb_c, alog_c, dsk_exp, dproj, carry=None):
    T = xbc_a.shape[0]
    nc = T // Q
    dtb_r, alog_r = [jnp.pad(a, ((0, 0), (0, 128 - HEADS))) for a in (dtb_r, alog_r)]

    def body(dy_ref, xbc_ref, dtp_ref, hin_ref, dtb_r_ref, alog_r_ref, dtb_c_ref, alog_c_ref, dsk_ref, dp_in,
             blocks_ref, pairs_ref, block_sum_ref, pair_sum_ref, dxa_ref, dp_ref, dsk_sum_ref, small_ref, dh_scr):
        del dp_in

        @pl.when(pl.program_id(0) == 0)
        def _():
            dh_scr[...] = jnp.zeros_like(dh_scr)
            dsk_sum_ref[...] = jnp.zeros_like(dsk_sum_ref)
            small_ref[...] = jnp.zeros_like(small_ref)

        raw, dt, a_r, csT, cs_p, dt_p = _ssd_common(dtp_ref, dtb_r_ref, alog_r_ref, dtb_c_ref, alog_c_ref)
        lo = lax.broadcasted_iota(jnp.int32, (1, 128), 1) < HEAD_DIM
        hi = jnp.logical_not(lo)
        sub32 = lax.broadcasted_iota(jnp.int32, (HEADS, 1), 0)
        causal_w = (lax.broadcasted_iota(jnp.int32, (Q, 1024), 1) & (Q - 1)) <= lax.broadcasted_iota(jnp.int32, (Q, 1024), 0)
        dcs_c = jnp.zeros((Q, 128), F32)
        dcs_r = jnp.zeros((HEADS, Q), F32)
        dcs_l = jnp.zeros((8, 128), F32)
        ddt_x = jnp.zeros((Q, 128), F32)
        for g in range(GROUPS):
            gs = slice(512 * g, 512 * (g + 1))
            hs = slice(128 * g, 128 * (g + 1))
            xs_g, dy_g = xbc_ref[:, gs], dy_ref[:, gs]
            b_g = xbc_ref[:, DI + STATE * g:DI + STATE * (g + 1)].astype(BF16)
            c_g = xbc_ref[:, DI + 512 + STATE * g:DI + 512 + STATE * (g + 1)].astype(BF16)
            m_w, l_w, dt_g, ecs_g, dec_g, cd_g = _ssd_group(g, cs_p, csT, dt_p, _nt(c_g, b_g), causal_w, lo, blocks_ref, pairs_ref)
            m_b = m_w.astype(BF16)
            xdt = xs_g * dt_g
            xdt_b, dy_b = xdt.astype(BF16), dy_g.astype(BF16)
            dms, dxs = [], []
            for jj in range(4):
                xp, dyp = xdt_b[:, 128 * jj:128 * (jj + 1)], dy_b[:, 128 * jj:128 * (jj + 1)]
                dy_ab = jnp.concatenate([jnp.where(lo, dyp, jnp.zeros_like(dyp)), jnp.where(hi, dyp, jnp.zeros_like(dyp))], axis=0)
                dm_ab = _nt(dy_ab, xp)
                dms += [dm_ab[:Q], dm_ab[Q:]]
                dx_ab = _tn(m_b[:, 256 * jj:256 * (jj + 1)], dyp)
                dxs.append(jnp.where(lo, dx_ab[:Q], dx_ab[Q:]))
            dm_w = jnp.concatenate(dms, axis=1)
            w_w = dm_w * m_w
            dcs_c = dcs_c + _spread(w_w, block_sum_ref[g], 2)
            w_cols = jnp.sum(w_w, axis=0, keepdims=True)
            for hh in range(8):
                dcs_r = dcs_r + jnp.where(sub32 == 8 * g + hh, w_cols[:, 128 * hh:128 * (hh + 1)], 0.0)
            dl_w = dm_w * l_w
            ds_mat = dl_w[:, 0:128]
            for hh in range(1, 8):
                ds_mat = ds_mat + dl_w[:, 128 * hh:128 * (hh + 1)]
            hin_g = hin_ref[0, hs, :]
            hin_b = hin_g.astype(BF16)
            dh_g = dh_scr[hs, :]
            dh_b = dh_g.astype(BF16)
            g_mat = _nn(b_g, dh_b)
            xdec = xdt * dec_g
            xg = xdec * g_mat
            dxdt = jnp.concatenate(dxs, axis=1) + dec_g * g_mat
            sums = _spread(jnp.concatenate([dy_g * (_nn(c_g, hin_b) * ecs_g) - xg, dxdt * xs_g], axis=0), pair_sum_ref[g], 2)
            dcs_c = dcs_c + sums[:Q]
            ddt_x = ddt_x + sums[Q:]
            last = jnp.sum(xg, axis=0, keepdims=True) + jnp.sum(dh_g * hin_g, axis=0, keepdims=True) * cd_g
            dcs_l = dcs_l + _spread(jnp.broadcast_to(last, (8, 512)), pair_sum_ref[g], 2)
            dz = (dy_g * ecs_g).astype(BF16)
            ds_b = ds_mat.astype(BF16)
            dxa_ref[:, gs] = dxdt * dt_g + dy_g * dsk_ref[:, gs]
            dxa_ref[:, DI + STATE * g:DI + STATE * (g + 1)] = _nt(xdec.astype(BF16), dh_b) + _tn(ds_b, c_g)
            dxa_ref[:, DI + 512 + STATE * g:DI + 512 + STATE * (g + 1)] = _nt(dz, hin_b) + _nn(ds_b, b_g)
            dh_scr[hs, :] = _tn(c_g, dz) + dh_g * cd_g
            dsk_sum_ref[0:1, gs] += jnp.sum(dy_g * xs_g, axis=0, keepdims=True)

        rows = lax.broadcasted_iota(jnp.int32, (Q, Q), 0)
        cols = lax.broadcasted_iota(jnp.int32, (Q, Q), 1)
        tri_t = (cols >= rows).astype(F32)
        last_row = lax.broadcasted_iota(jnp.int32, (Q, 1), 0) == Q - 1
        dcs = (dcs_c + jnp.where(last_row, dcs_l[0:1, :], 0.0))[:, 0:HEADS]
        da = _nn(tri_t, dcs, precision=HIGH) - _nt(tri_t, dcs_r, precision=HIGH)
        ddt_raw = (ddt_x[:, 0:HEADS] + da * a_r) * _sigmoid(raw)
        small_ref[0:1, :] += jnp.sum(da * dt, axis=0, keepdims=True) * a_r
        small_ref[1:2, :] += jnp.sum(ddt_raw, axis=0, keepdims=True)
        dp_ref[...] = jnp.zeros_like(dp_ref)
        dp_ref[:, 0:HEADS] = ddt_raw.astype(BF16)

    rev = lambda c: nc - 1 - c
    small_r = pl.BlockSpec((1, 128), lambda c: (0, 0))
    small_c = pl.BlockSpec((HEADS, 1), lambda c: (0, 0))
    selectors = _ssd_selectors()
    whole = lambda a: pl.BlockSpec(a.shape, lambda c: (0,) * a.ndim)
    return _call(
        body, [dy, xbc_a, proj, hin, dtb_r, alog_r, dtb_c, alog_c, dsk_exp, dproj, *selectors], name="ssd_bwd", grid=(nc,),
        in_specs=[pl.BlockSpec((Q, DI), lambda c: (rev(c), 0)),
                  pl.BlockSpec((Q, XBC), lambda c: (rev(c), 0)),
                  pl.BlockSpec((Q, 128), lambda c: (rev(c), OFF_DT // 128)),
                  pl.BlockSpec((1, 512, 512), lambda c: (rev(c), 0, 0)),
                  small_r, small_r, small_c, small_c,
                  pl.BlockSpec((1, DI), lambda c: (0, 0)),
                  pl.BlockSpec(memory_space=pl.ANY)] + [whole(a) for a in selectors],
        out_specs=[pl.BlockSpec((Q, XBC), lambda c: (rev(c), 0)),
                   pl.BlockSpec((Q, 256), lambda c: (rev(c), OFF_DT // 256)),
                   pl.BlockSpec((8, DI), lambda c: (0, 0)),
                   pl.BlockSpec((8, HEADS), lambda c: (0, 0))],
        out_shape=[jax.ShapeDtypeStruct((T, XBC), F32), jax.ShapeDtypeStruct(dproj.shape, BF16),
                   jax.ShapeDtypeStruct((8, DI), F32), jax.ShapeDtypeStruct((8, HEADS), F32)],
        aliases={9: 1}, scratch_shapes=[pltpu.VMEM((512, 512), F32)], sem=("arbitrary",), carry=carry)


def _gate_norm(y, proj, w):
    T = y.shape[0]
    tr = _row_tile(T)

    def body(y_ref, z_ref, w_ref, o_ref):
        for g in range(GROUPS):
            gs = slice(512 * g, 512 * (g + 1))
            z = z_ref[:, gs]
            yg = y_ref[:, gs] * (z * _sigmoid(z))
            r = lax.rsqrt(jnp.mean(yg * yg, axis=-1, keepdims=True) + EPS)
            o_ref[:, gs] = (yg * r * w_ref[:, gs]).astype(BF16)

    tile = pl.BlockSpec((tr, DI), lambda i: (i, 0))
    return _call(body, [y, proj, w], name="gate_norm", grid=(T // tr,),
                 in_specs=[tile, tile, pl.BlockSpec((1, DI), lambda i: (0, 0))], out_specs=[tile],
                 out_shape=[jax.ShapeDtypeStruct((T, DI), BF16)], sem=("parallel",))[0]


def _gate_norm_bwd(dyn, y, proj, w, dproj):
    T = y.shape[0]
    tr = _row_tile(T)

    def body(d_ref, y_ref, z_ref, w_ref, dp_in, dy_ref, dz_ref, sums_ref):
        del dp_in

        @pl.when(pl.program_id(0) == 0)
        def _():
            sums_ref[...] = jnp.zeros_like(sums_ref)

        for g in range(GROUPS):
            gs = slice(512 * g, 512 * (g + 1))
            z, yv, d = z_ref[:, gs], y_ref[:, gs], d_ref[:, gs]
            s = _sigmoid(z)
            silu = z * s
            yg = yv * silu
            r = lax.rsqrt(jnp.mean(yg * yg, axis=-1, keepdims=True) + EPS)
            yn = yg * r
            sums_ref[0:1, gs] += jnp.sum(d * yn, axis=0, keepdims=True)
            dn = d * w_ref[:, gs]
            dyg = r * (dn - yn * jnp.mean(dn * yn, axis=-1, keepdims=True))
            dy_ref[:, gs] = dyg * silu
            dz_ref[:, gs] = (dyg * yv * (s * (1.0 + z * (1.0 - s)))).astype(BF16)

    tile = pl.BlockSpec((tr, DI), lambda i: (i, 0))
    return _call(
        body, [dyn, y, proj, w, dproj], name="gate_norm_bwd", grid=(T // tr,),
        in_specs=[tile, tile, tile, pl.BlockSpec((1, DI), lambda i: (0, 0)), pl.BlockSpec(memory_space=pl.ANY)],
        out_specs=[tile, tile, pl.BlockSpec((8, DI), lambda i: (0, 0))],
        out_shape=[jax.ShapeDtypeStruct((T, DI), F32), jax.ShapeDtypeStruct(dproj.shape, BF16),
                   jax.ShapeDtypeStruct((8, DI), F32)],
        aliases={4: 1}, sem=("arbitrary",))


def _pool_fwd(proj, pool_w_b, pool_scale):
    T = proj.shape[0]
    tr = _row_tile(T)
    nb = tr // 16

    def body(u_ref, h_ref, pw_ref, ps_ref, pooled_ref, pw_out_ref, yps_ref):
        i = pl.program_id(0)
        t = i * tr + lax.broadcasted_iota(jnp.int32, (tr, 1), 0)
        for g, win in enumerate(POOL_WINDOWS):
            gs = slice(GW * g, GW * (g + 1))
            u = u_ref[:, gs]
            s = jnp.concatenate([jnp.where(i > 0, h_ref[:, gs], 0.0), u], axis=0)
            sh = 1
            while sh < win:
                s = s + pltpu.roll(s, sh, 0)
                sh *= 2
            pooled = (s[16:] * (1.0 / jnp.minimum(t + 1, win).astype(F32)) - u).astype(BF16)
            pooled_ref[:, gs] = pooled
            pwv = _nn(pooled, pw_ref[g])
            pw_out_ref[:, gs] = pwv
            yps_ref[:, gs] = (pwv * ps_ref[:, gs]).astype(BF16)

    tile = pl.BlockSpec((tr, D), lambda i: (i, 0))
    return _call(
        body, [proj, proj, pool_w_b, pool_scale], name="pool_fwd", grid=(T // tr,),
        in_specs=[pl.BlockSpec((tr, D), lambda i: (i, OFF_POOL // D)),
                  pl.BlockSpec((16, D), lambda i: (jnp.maximum(i * nb - 1, 0), OFF_POOL // D)),
                  pl.BlockSpec((4, GW, GW), lambda i: (0, 0, 0)),
                  pl.BlockSpec((1, D), lambda i: (0, 0))],
        out_specs=[tile, tile, tile],
        out_shape=[jax.ShapeDtypeStruct((T, D), BF16), jax.ShapeDtypeStruct((T, D), F32),
                   jax.ShapeDtypeStruct((T, D), BF16)], sem=("parallel",))


def _pool_bwd(dyp, pw_out, pooled, pool_w_b, pool_scale, dproj):
    T = dyp.shape[0]
    tr = _row_tile(T)
    nb, last = tr // 16, T // tr - 1

    def body(d_ref, h_ref, pwo_ref, pooled_ref, pw_ref, ps_ref, dp_in, du_ref, gpw_ref, sums_ref):
        del dp_in
        i = pl.program_id(0)

        @pl.when(i == 0)
        def _():
            gpw_ref[...] = jnp.zeros_like(gpw_ref)
            sums_ref[...] = jnp.zeros_like(sums_ref)

        n = tr + 16
        t = i * tr + lax.broadcasted_iota(jnp.int32, (n, 1), 0)
        sums_ref[0:1, :] += jnp.sum(d_ref[...] * pwo_ref[...], axis=0, keepdims=True)
        for g, win in enumerate(POOL_WINDOWS):
            gs = slice(GW * g, GW * (g + 1))
            d_ext = jnp.concatenate([d_ref[:, gs], jnp.where(i < last, h_ref[:, gs], 0.0)], axis=0)
            dpw = (d_ext * ps_ref[:, gs]).astype(BF16)
            dpooled = _nt(dpw, pw_ref[g])
            s = jnp.where(t < T, dpooled * (1.0 / jnp.minimum(t + 1, win).astype(F32)), 0.0)
            sh = 1
            while sh < win:
                s = s + pltpu.roll(s, n - sh, 0)
                sh *= 2
            du_ref[:, gs] = (s[:tr] - dpooled[:tr]).astype(BF16)
            gpw_ref[g] += _tn(pooled_ref[:, gs], dpw[:tr])

    tile = pl.BlockSpec((tr, D), lambda i: (i, 0))
    return _call(
        body, [dyp, dyp, pw_out, pooled, pool_w_b, pool_scale, dproj], name="pool_bwd", grid=(T // tr,),
        in_specs=[tile, pl.BlockSpec((16, D), lambda i: (jnp.minimum((i + 1) * nb, T // 16 - 1), 0)), tile, tile,
                  pl.BlockSpec((4, GW, GW), lambda i: (0, 0, 0)), pl.BlockSpec((1, D), lambda i: (0, 0)),
                  pl.BlockSpec(memory_space=pl.ANY)],
        out_specs=[pl.BlockSpec((tr, D), lambda i: (i, OFF_POOL // D)),
                   pl.BlockSpec((4, GW, GW), lambda i: (0, 0, 0)), pl.BlockSpec((8, D), lambda i: (0, 0))],
        out_shape=[jax.ShapeDtypeStruct(dproj.shape, BF16), jax.ShapeDtypeStruct((4, GW, GW), F32),
                   jax.ShapeDtypeStruct((8, D), F32)],
        aliases={6: 0}, sem=("arbitrary",))


def _merge(proj, y_ssd, y_pool):
    T = proj.shape[0]
    tr = _row_tile(T)

    def body(g_ref, a_ref, b_ref, o_ref):
        o_ref[...] = (_sigmoid(g_ref[:, 0:D]) * a_ref[...] + _sigmoid(g_ref[:, D:2 * D]) * b_ref[...]).astype(BF16)

    tile = pl.BlockSpec((tr, D), lambda i: (i, 0))
    return _call(body, [proj, y_ssd, y_pool], name="merge", grid=(T // tr,),
                 in_specs=[pl.BlockSpec((tr, 2 * D), lambda i: (i, OFF_GATE // (2 * D))), tile, tile], out_specs=[tile],
                 out_shape=[jax.ShapeDtypeStruct((T, D), BF16)], sem=("parallel",))[0]


def _merge_bwd(dmerged, proj, y_ssd, y_pool):
    T = proj.shape[0]
    tr = _row_tile(T)

    def body(d_ref, g_ref, a_ref, b_ref, da_ref, db_ref, dg_ref):
        d = d_ref[...]
        ga, gb = _sigmoid(g_ref[:, 0:D]), _sigmoid(g_ref[:, D:2 * D])
        da_ref[...] = (d * ga).astype(BF16)
        db_ref[...] = (d * gb).astype(BF16)
        dg_ref[:, 0:D] = (d * a_ref[...] * ga * (1.0 - ga)).astype(BF16)
        dg_ref[:, D:2 * D] = (d * b_ref[...] * gb * (1.0 - gb)).astype(BF16)

    tile = pl.BlockSpec((tr, D), lambda i: (i, 0))
    gates = pl.BlockSpec((tr, 2 * D), lambda i: (i, OFF_GATE // (2 * D)))
    return _call(body, [dmerged, proj, y_ssd, y_pool], name="merge_bwd", grid=(T // tr,),
                 in_specs=[tile, gates, tile, tile], out_specs=[tile, tile, gates],
                 out_shape=[jax.ShapeDtypeStruct((T, D), BF16), jax.ShapeDtypeStruct((T, D), BF16),
                            jax.ShapeDtypeStruct((T, NP), BF16)], sem=("parallel",))


def _adamw(w, g, m, v, name, carry=None):
    R, C = w.shape
    tr = R if R <= 128 else 128
    assert R % tr == 0

    def body(w_ref, g_ref, m_ref, v_ref, d_ref, mo_ref, vo_ref):
        gv = g_ref[...]
        mn = ADAM_B1 * m_ref[...] + (1.0 - ADAM_B1) * gv
        vn = ADAM_B2 * v_ref[...] + (1.0 - ADAM_B2) * (gv * gv)
        m_hat = mn * (1.0 / (1.0 - ADAM_B1 ** ADAM_STEP))
        v_hat = vn * (1.0 / (1.0 - ADAM_B2 ** ADAM_STEP))
        d_ref[...] = -ADAM_LR * (m_hat / (jnp.sqrt(v_hat) + ADAM_EPS) + ADAM_WD * w_ref[...])
        mo_ref[...] = mn
        vo_ref[...] = vn

    tile = pl.BlockSpec((tr, C), lambda i: (i, 0))
    sds = jax.ShapeDtypeStruct((R, C), F32)
    return _call(body, [w, g, m, v], name=name, grid=(R // tr,), in_specs=[tile] * 4, out_specs=[tile] * 3,
                 out_shape=[sds] * 3, sem=("parallel",), carry=carry)


def _me():
    return lax.axis_index("x"), lax.axis_index("y"), lax.axis_index("c")


def _xor_peer(x, y, c, p):
    return (x ^ ((p >> 2) & 1), y ^ ((p >> 1) & 1), c ^ (p & 1))


def _ada_fwd(c_row, w_ada, b_ada_mine, carry=None):
    n_cols = w_ada.shape[1]

    def body(c_ref, w_ref, b_ref, mod_ref, c8_ref, csend, mpart, modbuf, send_sems, recv_sems):
        x, y, c = _me()
        me = 4 * x + 2 * y + c
        chip = 2 * x + y
        csend[...] = jnp.broadcast_to(c_ref[...], csend.shape)
        c8_ref[me] = csend[...]

        def c_copy(p):
            return pltpu.make_async_remote_copy(
                src_ref=csend, dst_ref=c8_ref.at[me], send_sem=send_sems.at[p - 1], recv_sem=recv_sems.at[p - 1],
                device_id=_xor_peer(x, y, c, p), device_id_type=MESH)

        for p in range(1, 8):
            c_copy(p).start()
        for p in range(1, 8):
            c_copy(p).wait_recv()
        cs = jnp.concatenate([c8_ref[d][0:1, :] for d in range(8)], axis=0)
        mpart[...] = _nn(cs * _sigmoid(cs), w_ref[...], precision=HIGH) + b_ref[...]
        modbuf[chip] = mpart[...]

        def m_copy(m):
            return pltpu.make_async_remote_copy(
                src_ref=mpart, dst_ref=modbuf.at[chip], send_sem=send_sems.at[6 + m], recv_sem=recv_sems.at[6 + m],
                device_id=_xor_peer(x, y, c, 2 * m), device_id_type=MESH)

        for m in range(1, 4):
            m_copy(m).start()
        for m in range(1, 4):
            m_copy(m).wait_recv()
        mine = lax.broadcasted_iota(jnp.int32, (8, 1), 0) == me
        for k in range(N_CHIPS):
            mod_ref[:, n_cols * k:n_cols * (k + 1)] = jnp.sum(jnp.where(mine, modbuf[k], 0.0), axis=0, keepdims=True)
        for p in range(1, 8):
            c_copy(p).wait_send()
        for m in range(1, 4):
            m_copy(m).wait_send()

    vmem = pl.BlockSpec(memory_space=pltpu.VMEM)
    return _call(
        body, [c_row, w_ada, b_ada_mine], name="ada_fwd", in_specs=[vmem, vmem, vmem], out_specs=[vmem, vmem],
        out_shape=[jax.ShapeDtypeStruct((1, N_CHIPS * n_cols), F32), jax.ShapeDtypeStruct((8, 8, D), F32)],
        scratch_shapes=[pltpu.VMEM((8, D), F32), pltpu.VMEM((8, n_cols), F32), pltpu.VMEM((N_CHIPS, 8, n_cols), F32),
                        pltpu.SemaphoreType.DMA((10,)), pltpu.SemaphoreType.DMA((10,))], carry=carry)


def _gather_small(vec, carry=None):
    rows = vec.shape[0]

    def body(v_ref, all_ref, tot_ref, dsk_ref, send_sems, recv_sems):
        x, y, c = _me()
        me = 4 * x + 2 * y + c
        all_ref[me] = v_ref[...]

        def copy(p):
            return pltpu.make_async_remote_copy(
                src_ref=v_ref, dst_ref=all_ref.at[me], send_sem=send_sems.at[p - 1], recv_sem=recv_sems.at[p - 1],
                device_id=_xor_peer(x, y, c, p), device_id_type=MESH)

        for p in range(1, 8):
            copy(p).start()
        for p in range(1, 8):
            copy(p).wait_recv()
        tot = all_ref[0]
        for d in range(1, 8):
            tot = tot + all_ref[d]
        tot_ref[...] = tot
        seg = tot[SMALL_OFF["d_skip"] // 128:SMALL_OFF["d_skip"] // 128 + 16, :]
        lane = lax.broadcasted_iota(jnp.int32, (1, 128), 1)
        sa = jnp.sum(jnp.where(lane < HEAD_DIM, seg, 0.0), axis=1, keepdims=True)
        sb = jnp.sum(jnp.where(lane < HEAD_DIM, 0.0, seg), axis=1, keepdims=True)
        dsk_ref[...] = jnp.where(lane == 0, sa, jnp.where(lane == 1, sb, 0.0))
        for p in range(1, 8):
            copy(p).wait_send()

    vmem = pl.BlockSpec(memory_space=pltpu.VMEM)
    return _call(
        body, [vec], name="gather_small", in_specs=[vmem], out_specs=[vmem, vmem, vmem],
        out_shape=[jax.ShapeDtypeStruct((8, rows, 128), F32), jax.ShapeDtypeStruct((rows, 128), F32),
                   jax.ShapeDtypeStruct((16, 128), F32)],
        scratch_shapes=[pltpu.SemaphoreType.DMA((7,)), pltpu.SemaphoreType.DMA((7,))], carry=carry)


def _gather_carry(shards):
    n = len(shards)

    def copies(ins, outs, sems):
        x, y, c = _me()
        chip = 2 * x + y

        def half(w, which):
            h = shards[w].shape[0] // 2
            return pl.ds(which * h, h)

        def first(w, m):
            return pltpu.make_async_remote_copy(
                src_ref=ins[w].at[half(w, c)], dst_ref=outs[w].at[chip, half(w, c)],
                send_sem=sems.send(6 * w + m - 1), recv_sem=sems.recv(6 * w + m - 1),
                device_id=_xor_peer(x, y, c, 2 * m), device_id_type=MESH)

        def landed(w, m):
            return pltpu.make_async_remote_copy(
                src_ref=ins[w].at[half(w, c)], dst_ref=outs[w].at[chip ^ m, half(w, c)],
                send_sem=sems.send(6 * w + m - 1), recv_sem=sems.recv(6 * w + m - 1),
                device_id=_xor_peer(x, y, c, 2 * m), device_id_type=MESH)

        def passed(w, m, which):
            part = outs[w].at[chip ^ m, half(w, which)]
            return pltpu.make_async_remote_copy(
                src_ref=part, dst_ref=part, send_sem=sems.send(6 * w + 2 + m), recv_sem=sems.recv(6 * w + 2 + m),
                device_id=(x, y, 1 - c), device_id_type=MESH)

        return c, first, landed, passed

    pairs = [(w, m) for w in range(n) for m in range(1, 4)]

    def start(ins, outs, sems):
        _, first, _, _ = copies(ins, outs, sems)
        for w, m in pairs:
            first(w, m).start()

    def finish(ins, outs, sems):
        c, first, landed, passed = copies(ins, outs, sems)
        for w, m in pairs:
            landed(w, m).wait_recv()
            passed(w, m, c).start()
        for w, m in pairs:
            passed(w, m, 1 - c).wait_recv()
        for w, m in pairs:
            first(w, m).wait_send()
            passed(w, m, c).wait_send()

    return _Carry(shards, [jax.ShapeDtypeStruct((N_CHIPS,) + s.shape, s.dtype) for s in shards], 6 * n, start, finish)


def _pair_exchange_carry(grads):
    n = len(grads)

    def copy(ins, outs, sems, w):
        x, y, c = _me()
        h = grads[w].shape[1] // 2
        return pltpu.make_async_remote_copy(
            src_ref=ins[w].at[:, pl.ds((1 - c) * h, h)], dst_ref=outs[w],
            send_sem=sems.send(w), recv_sem=sems.recv(w), device_id=(x, y, 1 - c), device_id_type=MESH)

    def start(ins, outs, sems):
        for w in range(n):
            copy(ins, outs, sems, w).start()

    def finish(ins, outs, sems):
        for w in range(n):
            copy(ins, outs, sems, w).wait()

    return _Carry(grads, [jax.ShapeDtypeStruct((N_CHIPS, g.shape[1] // 2, g.shape[2]), g.dtype) for g in grads], n,
                  start, finish)


def _chip_exchange_carry(partials):
    n = len(partials)

    def copier(ins, outs, sems):
        x, y, c = _me()
        chip = 2 * x + y

        def copy(w, m, landed):
            return pltpu.make_async_remote_copy(
                src_ref=ins[w].at[chip ^ m], dst_ref=outs[w].at[(chip ^ m) if landed else chip],
                send_sem=sems.send(3 * w + m - 1), recv_sem=sems.recv(3 * w + m - 1),
                device_id=_xor_peer(x, y, c, 2 * m), device_id_type=MESH)

        return copy

    pairs = [(w, m) for w in range(n) for m in range(1, 4)]

    def start(ins, outs, sems):
        copy = copier(ins, outs, sems)
        for w, m in pairs:
            copy(w, m, False).start()

    def finish(ins, outs, sems):
        copy = copier(ins, outs, sems)
        for w, m in pairs:
            copy(w, m, True).wait_recv()
        for w, m in pairs:
            copy(w, m, False).wait_send()

    return _Carry(partials, [jax.ShapeDtypeStruct(p.shape, p.dtype) for p in partials], 3 * n, start, finish)


def _pair_share_carry(shards):
    n = len(shards)

    def copier(ins, outs, sems):
        x, y, c = _me()

        def copy(w, which):
            h = shards[w].shape[0] // 2
            rows = pl.ds(which * h, h)
            return pltpu.make_async_remote_copy(
                src_ref=ins[w].at[rows], dst_ref=outs[w].at[rows],
                send_sem=sems.send(w), recv_sem=sems.recv(w), device_id=(x, y, 1 - c), device_id_type=MESH)

        return c, copy

    def start(ins, outs, sems):
        c, copy = copier(ins, outs, sems)
        for w in range(n):
            copy(w, c).start()

    def finish(ins, outs, sems):
        c, copy = copier(ins, outs, sems)
        for w in range(n):
            copy(w, 1 - c).wait_recv()
        for w in range(n):
            copy(w, c).wait_send()

    return _Carry(shards, [jax.ShapeDtypeStruct(s.shape, s.dtype) for s in shards], n, start, finish,
                  aliased=[(w, w) for w in range(n)])


def _pair_sum(g, part, idx, name):
    _, h, C = part.shape
    tr = min(512, h)
    nb = h // tr

    def body(idx_ref, g_ref, p_ref, o16_ref, own_ref):
        v = g_ref[...] + p_ref[...]
        o16_ref[...] = v.astype(BF16)

        @pl.when(pl.program_id(1) == idx_ref[1])
        def _():
            own_ref[...] = v

    return pl.pallas_call(
        body, name=name,
        grid_spec=pltpu.PrefetchScalarGridSpec(
            num_scalar_prefetch=1, grid=(nb, N_CHIPS),
            in_specs=[pl.BlockSpec((None, tr, C), lambda i, s, idx_ref: (s, idx_ref[0] * nb + i, 0)),
                      pl.BlockSpec((None, tr, C), lambda i, s, idx_ref: (s, i, 0))],
            out_specs=[pl.BlockSpec((None, tr, C), lambda i, s, idx_ref: (s, i, 0)),
                       pl.BlockSpec((tr, C), lambda i, s, idx_ref: (i, 0))]),
        out_shape=[jax.ShapeDtypeStruct(part.shape, BF16), jax.ShapeDtypeStruct((h, C), F32)],
        compiler_params=pltpu.CompilerParams(dimension_semantics=("arbitrary", "arbitrary"), vmem_limit_bytes=VMEM_LIMIT),
    )(idx, g, part)


def _chip_sum(own, slots, idx, name):
    h, C = own.shape
    tr = min(512, h)
    nb = h // tr

    def body(idx_ref, own_ref, s1_ref, s2_ref, s3_ref, o_ref):
        del idx_ref
        o_ref[...] = ((own_ref[...] + s1_ref[...].astype(F32)) + s2_ref[...].astype(F32)) + s3_ref[...].astype(F32)

    def slot(m):
        return pl.BlockSpec((None, tr, C), lambda i, idx_ref: (idx_ref[1] ^ m, i, 0))

    return pl.pallas_call(
        body, name=name,
        grid_spec=pltpu.PrefetchScalarGridSpec(
            num_scalar_prefetch=1, grid=(nb,),
            in_specs=[pl.BlockSpec((tr, C), lambda i, idx_ref: (i, 0)), slot(1), slot(2), slot(3)],
            out_specs=pl.BlockSpec((tr, C), lambda i, idx_ref: (idx_ref[0] * nb + i, 0))),
        out_shape=jax.ShapeDtypeStruct((2 * h, C), F32),
        compiler_params=pltpu.CompilerParams(dimension_semantics=("parallel",), vmem_limit_bytes=VMEM_LIMIT),
    )(idx, own, slots, slots, slots)


class _Reducer:
    def __init__(self, idx):
        self.idx, self.chips, self.p16, self.own, self.mine, self.final = idx, {}, {}, {}, {}, {}

    def add(self, name, whole, chip_blocks=False):
        self.chips[name] = whole if chip_blocks else _chips_from_whole(name, whole)

    def pair(self, names):
        return _pair_exchange_carry([self.chips[n] for n in names])

    def take_pair(self, names, outs):
        for n, part in zip(names, outs):
            self.p16[n], self.own[n] = _pair_sum(self.chips.pop(n), part, self.idx, "pair_sum_" + n)

    def chip(self, names):
        return _chip_exchange_carry([self.p16[n] for n in names])

    def take_chip(self, names, outs):
        for n, slots in zip(names, outs):
            del self.p16[n]
            self.mine[n] = _chip_sum(self.own.pop(n), slots, self.idx, "chip_sum_" + n)

    def share(self, names):
        return _pair_share_carry([self.mine[n] for n in names])

    def take_share(self, names, outs):
        for n, s in zip(names, outs):
            del self.mine[n]
            self.final[n] = s


def _w_ada_grad(c8, dmod_cols):
    n_cols = dmod_cols.shape[1]
    tn = 512

    def body(c_ref, d_ref, o_ref):
        cv = c_ref[...]
        o_ref[...] = _tn(cv * _sigmoid(cv), d_ref[...], precision=HIGH)

    return _call(body, [c8, dmod_cols], name="w_ada_grad", grid=(n_cols // tn,),
                 in_specs=[pl.BlockSpec((8, D), lambda j: (0, 0)), pl.BlockSpec((8, tn), lambda j: (0, j))],
                 out_specs=[pl.BlockSpec((D, tn), lambda j: (0, j))],
                 out_shape=[jax.ShapeDtypeStruct((D, n_cols), F32)], sem=("parallel",))[0]


_SMALL_SEGS = (("dmod", 6144), ("norm_mix_w", 1024), ("conv_b", 3072), ("ssd_norm_w", 2048), ("pool_scale", 1024),
               ("norm_mlp_w", 1024), ("norm_final_w", 1024), ("conv_w", 4 * XBC), ("d_skip", 2048), ("a_log", 128),
               ("dt_bias", 128), ("loss", 128))
SMALL_OFF = {}
_o = 0
for _n, _s in _SMALL_SEGS:
    SMALL_OFF[_n] = _o
    _o += _s
SMALL_LEN = -(-_o // 1024) * 1024

_FIRST = ("w_in", "conv_w")
_LATER = ("w_branch_ssd", "pool_w", "w_branch_pool", "w_out", "w_up", "w_down")
_SMALL_REPLICATED = ("b_ada", "norm_mix_w", "conv_b", "dt_bias", "a_log", "d_skip", "ssd_norm_w", "pool_scale",
                     "norm_mlp_w", "norm_final_w")
_WEIGHTS = ("w_ada", "b_ada", "norm_mix_w", "w_in", "conv_w", "conv_b", "dt_bias", "a_log", "d_skip", "ssd_norm_w",
            "w_branch_ssd", "pool_w", "pool_scale", "w_branch_pool", "w_out", "norm_mlp_w", "w_up", "w_down",
            "norm_final_w")


def _shard_2d(name, a):
    if name == "conv_w":
        return a.reshape(16, -1)
    return (a.reshape(GW, GW) if name == "pool_w" else a.reshape(a.shape[-2], a.shape[-1])).astype(BF16)


def _whole_from_chips(name, g, own, chip):
    g = lax.dynamic_update_slice(g, own[None], (chip, 0, 0))
    if name == "w_in":
        return _perm_cols(jnp.transpose(g, (1, 0, 2)).reshape(D, IN_COLS))
    if name == "w_up":
        return jnp.transpose(g, (1, 0, 2)).reshape(D, DFF)
    if name == "pool_w":
        return jnp.transpose(g.reshape(N_CHIPS, 4, GW // N_CHIPS, GW), (1, 0, 2, 3)).reshape(4, GW, GW)
    if name == "conv_w":
        return jnp.transpose(g.reshape(N_CHIPS, 4, XBC // N_CHIPS), (1, 0, 2)).reshape(4, XBC)
    return g.reshape(N_CHIPS * g.shape[1], g.shape[2])


def _chips_from_whole(name, g):
    if name.startswith("w_in"):
        return jnp.transpose(_unperm_cols(g).reshape(g.shape[0], N_CHIPS, IN_COLS // N_CHIPS), (1, 0, 2))
    if name == "w_up":
        return jnp.transpose(g.reshape(D, N_CHIPS, DFF // N_CHIPS), (1, 0, 2))
    if name == "pool_w":
        return jnp.transpose(g.reshape(4, N_CHIPS, GW // N_CHIPS, GW), (1, 0, 2, 3)).reshape(N_CHIPS, GW, GW)
    return g.reshape(N_CHIPS, g.shape[0] // N_CHIPS, g.shape[1])


def kernel(x, c, w_ada, b_ada, norm_mix_w, w_in, conv_w, conv_b, dt_bias, a_log, d_skip, ssd_norm_w, w_branch_ssd, pool_w, pool_scale, w_branch_pool, w_out, norm_mlp_w, w_up, w_down, norm_final_w, loss_target, m_w_ada, m_b_ada, m_norm_mix_w, m_w_in, m_conv_w, m_conv_b, m_dt_bias, m_a_log, m_d_skip, m_ssd_norm_w, m_w_branch_ssd, m_pool_w, m_pool_scale, m_w_branch_pool, m_w_out, m_norm_mlp_w, m_w_up, m_w_down, m_norm_final_w, v_w_ada, v_b_ada, v_norm_mix_w, v_w_in, v_conv_w, v_conv_b, v_dt_bias, v_a_log, v_d_skip, v_ssd_norm_w, v_w_branch_ssd, v_pool_w, v_pool_scale, v_w_branch_pool, v_w_out, v_norm_mlp_w, v_w_up, v_w_down, v_norm_final_w):
    args = locals()
    w = {n: args[n] for n in _WEIGHTS}
    m = {n: args["m_" + n] for n in _WEIGHTS}
    v = {n: args["v_" + n] for n in _WEIGHTS}
    xi, yi, ci = _me()
    chip = 2 * xi + yi
    idx = jnp.stack([ci, chip]).astype(jnp.int32)
    ada_cols = w_ada.shape[-1]
    xs, target = x[0], loss_target[0]
    two_d = lambda n, a: a.reshape(GW, GW) if n == "pool_w" else a.reshape(-1, a.shape[-1])
    delta, new_m, new_v, g = {}, {}, {}, {}

    def adamw(n, carry=None):
        res = _adamw(two_d(n, w[n]), two_d(n, g[n]), two_d(n, m[n]), two_d(n, v[n]), "adamw_" + n, carry=carry)
        (delta[n], new_m[n], new_v[n]), extra = res if carry is not None else (res, None)
        return extra

    b_mine = lax.dynamic_slice(b_ada, (0, chip * ada_cols), (1, ada_cols))
    shards = {n: _shard_2d(n, w[n]) for n in _FIRST + _LATER}
    (mod, c8), first = _ada_fwd(c, w_ada[0], b_mine, carry=_gather_carry([shards[n] for n in _FIRST]))
    c8 = c8[:, 0, :]
    shift_m, scale_m, gate_m, shift_f, scale_f, gate_f = [mod[:, D * i:D * (i + 1)] for i in range(6)]
    nf_w = norm_final_w.reshape(1, D)

    h1 = _norm_mod(xs, norm_mix_w, scale_m, shift_m, "norm_mod_mix")
    p = {n: _whole_from_chips(n, a, shards[n], chip) for n, a in zip(_FIRST, first)}
    (proj,), later = _matmul(h1, p["w_in"], mode="nn", out_dtypes=[F32], name="mm_proj",
                             carry=_gather_carry([shards[n] for n in _LATER]))
    p.update({n: _whole_from_chips(n, a, shards[n], chip) for n, a in zip(_LATER, later)})
    xbc_a = _conv_fwd(proj, p["conv_w"], conv_b)
    dtb_c, alog_c = dt_bias.reshape(HEADS, 1), a_log.reshape(HEADS, 1)
    dsk_exp = jnp.repeat(d_skip, HEAD_DIM, axis=1)
    y, hin = _ssd_fwd(xbc_a, proj, dt_bias, a_log, dtb_c, alog_c, dsk_exp)
    yn = _gate_norm(y, proj, ssd_norm_w)
    (y_ssd,) = _matmul(yn, p["w_branch_ssd"], mode="nn", out_dtypes=[F32], name="mm_branch_ssd")
    pooled, pw_out, yps = _pool_fwd(proj, p["pool_w"], pool_scale)
    (y_pool,) = _matmul(yps, p["w_branch_pool"], mode="nn", out_dtypes=[F32], name="mm_branch_pool")
    merged = _merge(proj, y_ssd, y_pool)
    resid = lambda acc, r, gt: (r + gt * acc, acc)
    x2, mix = _matmul(merged, p["w_out"], mode="nn", out_dtypes=[F32, BF16], name="mm_out",
                      epi=resid, tile_extras=(xs,), row_extras=(gate_m,))
    h2 = _norm_mod(x2, norm_mlp_w, scale_f, shift_f, "norm_mod_mlp")
    relu2 = lambda acc: (acc, jnp.square(jnp.maximum(acc, 0.0)))
    up, act = _matmul(h2, p["w_up"], mode="nn", out_dtypes=[BF16, BF16], name="mm_up", epi=relu2)
    x3, down = _matmul(act, p["w_down"], mode="nn", out_dtypes=[F32, BF16], name="mm_down",
                       epi=resid, tile_extras=(x2,), row_extras=(gate_f,))

    red = _Reducer(idx)
    dx3, d_down, sums_f = _final_loss_bwd(x3, target, nf_w, down, gate_f)
    drelu2 = lambda acc, u: (acc * (2.0 * jnp.maximum(u.astype(F32), 0.0)),)
    (dup,) = _matmul(d_down, p["w_down"], mode="nt", out_dtypes=[BF16], name="mm_dact",
                     epi=drelu2, tile_extras=(up,))
    red.add("w_down", _matmul(act, d_down, mode="tn", out_dtypes=[F32], name="mm_g_down")[0])
    (dh2,), got = _matmul(dup, p["w_up"], mode="nt", out_dtypes=[F32], name="mm_dh2",
                          carry=red.pair(["w_down"]))
    red.take_pair(["w_down"], got)
    red.add("w_up", _matmul(h2, dup, mode="tn", out_dtypes=[F32], name="mm_g_up", chip_blocks=True)[0], chip_blocks=True)
    dx2, sums_2, dmix = _norm_mod_bwd(x2, dh2, dx3, norm_mlp_w, scale_f, "norm_mod_mlp_bwd", branch=mix, gate=gate_m)
    (dmerged,), got = _matmul(dmix, p["w_out"], mode="nt", out_dtypes=[F32], name="mm_dmerged",
                              carry=red.pair(["w_up"]))
    red.take_pair(["w_up"], got)
    red.add("w_out", _matmul(merged, dmix, mode="tn", out_dtypes=[F32], name="mm_g_out")[0])
    dy_ssd, dy_pool, dproj = _merge_bwd(dmerged, proj, y_ssd, y_pool)
    (dyp,), got = _matmul(dy_pool, p["w_branch_pool"], mode="nt", out_dtypes=[F32], name="mm_dyp",
                          carry=red.pair(["w_out"]))
    red.take_pair(["w_out"], got)
    red.add("w_branch_pool", _matmul(yps, dy_pool, mode="tn", out_dtypes=[F32], name="mm_g_bpool")[0])
    dproj, g_pool_w, sums_pool = _pool_bwd(dyp, pw_out, pooled, p["pool_w"], pool_scale, dproj)
    red.add("pool_w", g_pool_w)
    red.add("w_branch_ssd", _matmul(yn, dy_ssd, mode="tn", out_dtypes=[F32], name="mm_g_bssd")[0])
    mixers = ["w_branch_pool", "pool_w", "w_branch_ssd"]
    (dyn,), got = _matmul(dy_ssd, p["w_branch_ssd"], mode="nt", out_dtypes=[F32], name="mm_dyn",
                          carry=red.pair(mixers))
    red.take_pair(mixers, got)
    dy, dproj, sums_gn = _gate_norm_bwd(dyn, y, proj, ssd_norm_w, dproj)
    six = ["w_down", "w_up", "w_out"] + mixers
    (dxa, dproj, dsk_sum, ssd_small), got = _ssd_bwd(dy, xbc_a, proj, hin, dt_bias, a_log, dtb_c, alog_c, dsk_exp,
                                                     dproj, carry=red.chip(six))
    red.take_chip(six, got)
    dxc, sums_conv = _conv_bwd_a(dxa, proj, p["conv_w"], conv_b)
    dproj = _conv_bwd_b(dxc, p["conv_w"], dproj)
    rows_a = 3 * D // 4
    (g_in_a,), got = _matmul(h1, dproj, mode="tn", out_dtypes=[F32], name="mm_g_in_a", a_cols=(0, rows_a),
                             carry=red.share(six))
    red.take_share(six, got)
    red.add("w_in_a", g_in_a)
    (g_in_b,), got = _matmul(h1, dproj, mode="tn", out_dtypes=[F32], name="mm_g_in_b", a_cols=(rows_a, D - rows_a),
                             carry=red.pair(["w_in_a"]))
    red.take_pair(["w_in_a"], got)
    red.add("w_in_b", g_in_b)
    (dh1,), got = _matmul(dproj, p["w_in"], mode="nt", out_dtypes=[F32], name="mm_dh1",
                          carry=_join(red.chip(["w_in_a"]), red.pair(["w_in_b"])))
    red.take_chip(["w_in_a"], got[:1])
    red.take_pair(["w_in_b"], got[1:])
    grad_x, sums_1 = _norm_mod_bwd(xs, dh1, dx2, norm_mix_w, scale_m, "norm_mod_mix_bwd")

    dmod = jnp.concatenate([sums_1[0:1], sums_1[1:2], sums_2[3:4], sums_2[0:1], sums_2[1:2], sums_f[1:2]], axis=1)
    pad96 = jnp.zeros((1, 96), F32)
    small = {"dmod": dmod, "norm_mix_w": sums_1[2:3], "conv_b": sums_conv[4:5], "ssd_norm_w": sums_gn[0:1],
             "pool_scale": sums_pool[0:1], "norm_mlp_w": sums_2[2:3], "norm_final_w": sums_f[0:1],
             "conv_w": sums_conv[0:4].reshape(1, 4 * XBC), "d_skip": dsk_sum[0:1],
             "a_log": jnp.concatenate([ssd_small[0:1], pad96], axis=1),
             "dt_bias": jnp.concatenate([ssd_small[1:2], pad96], axis=1), "loss": sums_f[3:4, 0:128]}
    vec = jnp.concatenate([small[n] for n, _ in _SMALL_SEGS], axis=1)
    vec = jnp.pad(vec, ((0, 0), (0, SMALL_LEN - vec.shape[1]))).reshape(SMALL_LEN // 128, 128)
    (every, total, dsk), got = _gather_small(vec, carry=_join(red.chip(["w_in_b"]), red.share(["w_in_a"])))
    red.take_chip(["w_in_b"], got[:1])
    red.take_share(["w_in_a"], got[1:])
    total = total.reshape(1, SMALL_LEN)
    seg = lambda n, size: total[:, SMALL_OFF[n]:SMALL_OFF[n] + size]
    g.update({"b_ada": seg("dmod", 6 * D), "norm_mix_w": seg("norm_mix_w", D), "conv_b": seg("conv_b", XBC),
              "dt_bias": seg("dt_bias", HEADS), "a_log": seg("a_log", HEADS), "d_skip": dsk[:, 0:2].reshape(1, HEADS),
              "ssd_norm_w": seg("ssd_norm_w", DI), "pool_scale": seg("pool_scale", D),
              "norm_mlp_w": seg("norm_mlp_w", D), "norm_final_w": seg("norm_final_w", D)})
    loss = total[0, SMALL_OFF["loss"]]
    conv_cols = conv_w.shape[-1]
    g["conv_w"] = lax.dynamic_slice(seg("conv_w", 4 * XBC).reshape(4, XBC), (0, chip * conv_cols), (4, conv_cols))
    dmod8 = every.reshape(8, SMALL_LEN)[:, SMALL_OFF["dmod"]:SMALL_OFF["dmod"] + 6 * D]
    g["w_ada"] = _w_ada_grad(c8, lax.dynamic_slice(dmod8, (0, chip * ada_cols), (8, ada_cols)))

    got = adamw("w_ada", carry=red.share(["w_in_b"]))
    red.take_share(["w_in_b"], got)
    for n in six:
        g[n] = red.final[n]
    g["w_in"] = jnp.concatenate([red.final["w_in_a"], red.final["w_in_b"]], axis=0)
    for n in ["conv_w", "w_in"] + six:
        adamw(n)
    sizes = [w[n].size for n in _SMALL_REPLICATED]
    n_small = -(-sum(sizes) // 1024) * 1024
    pack = lambda d: jnp.pad(jnp.concatenate([d[n].reshape(1, -1) for n in _SMALL_REPLICATED], axis=1),
                             ((0, 0), (0, n_small - sum(sizes)))).reshape(n_small // 128, 128)
    d_, m_, v_ = _adamw(pack(w), pack(g), pack(m), pack(v), "adamw_small")
    off = 0
    for n, s in zip(_SMALL_REPLICATED, sizes):
        for dst, src in ((delta, d_), (new_m, m_), (new_v, v_)):
            dst[n] = src.reshape(1, n_small)[:, off:off + s]
        off += s

    out = [loss, grad_x.reshape(x.shape)]
    for d in (g, delta, new_m, new_v):
        out += [d[n].reshape(w[n].shape) for n in _WEIGHTS]
    return tuple(out)
```

```python
import functools
import operator

import jax
import jax.numpy as jnp
import numpy as np
from jax import lax
from jax.experimental import pallas as pl
from jax.experimental.pallas import tpu as pltpu

F32, BF16 = jnp.float32, jnp.bfloat16
HIGH = lax.Precision.HIGHEST
MESH = pl.DeviceIdType.MESH

D = 1024
DI = 2048
HEADS, HEAD_DIM = 32, 64
GROUPS, STATE = 4, 128
Q = 128
XBC = DI + 2 * GROUPS * STATE
POOL_WINDOWS = (2, 4, 8, 16)
GW = 256
DFF = 4096
EPS = 1e-5
IN_COLS = 8224
OFF_Z, OFF_XBC, OFF_POOL, OFF_GATE, OFF_DT, NP = 0, 2048, 5120, 6144, 8192, 8448
N_CHIPS = 4
ADAM_LR, ADAM_B1, ADAM_B2, ADAM_EPS, ADAM_WD, ADAM_STEP = 0.001, 0.9, 0.999, 1e-08, 0.01, 10
VMEM_LIMIT = 56 * 2 ** 20
NEG = -1e30


def _sigmoid(v):
    return 0.5 * jnp.tanh(0.5 * v) + 0.5


def _softplus(v):
    return jnp.maximum(v, 0.0) + jnp.log1p(jnp.exp(-jnp.abs(v)))


def _dot(a, b, dims, **kw):
    return lax.dot_general(a, b, (dims, ((), ())), preferred_element_type=F32, **kw)


def _nn(a, b, **kw):
    return _dot(a, b, ((1,), (0,)), **kw)


def _nt(a, b, **kw):
    return _dot(a, b, ((1,), (1,)), **kw)


def _tn(a, b, **kw):
    return _dot(a, b, ((0,), (0,)), **kw)


def _perm_cols(w):
    pad = jnp.zeros(w.shape[:-1] + (NP - IN_COLS,), w.dtype)
    return jnp.concatenate([w[..., :5120], w[..., 5152:], w[..., 5120:5152], pad], axis=-1)


def _unperm_cols(g):
    return jnp.concatenate([g[..., :5120], g[..., OFF_DT:OFF_DT + 32], g[..., 5120:OFF_DT]], axis=-1)


class _Sems:
    def __init__(self, send, recv, local, base=0):
        self._send, self._recv, self._local, self._base = send, recv, local, base

    def shift(self, n):
        return _Sems(self._send, self._recv, self._local, self._base + n)

    def send(self, i):
        return self._send.at[self._base + i]

    def recv(self, i):
        return self._recv.at[self._base + i]

    def local(self, i):
        return self._local.at[self._base + i]


class _Carry:
    def __init__(self, ins, out_shapes, n_sems, start, finish, aliased=()):
        self.ins, self.out_shapes, self.n_sems, self.start, self.finish = list(ins), list(out_shapes), n_sems, start, finish
        self.aliased = list(aliased)


def _join(*carries):
    def run(which):
        def fn(ins, outs, sems):
            i = o = s = 0
            for cy in carries:
                getattr(cy, which)(ins[i:i + len(cy.ins)], outs[o:o + len(cy.out_shapes)], sems.shift(s))
                i, o, s = i + len(cy.ins), o + len(cy.out_shapes), s + cy.n_sems
        return fn

    aliased, i, o = [], 0, 0
    for cy in carries:
        aliased += [(i + a, o + b) for a, b in cy.aliased]
        i, o = i + len(cy.ins), o + len(cy.out_shapes)
    return _Carry([a for cy in carries for a in cy.ins], [a for cy in carries for a in cy.out_shapes],
                  sum(cy.n_sems for cy in carries), run("start"), run("finish"), aliased)


def _call(body, args, *, name, grid=(), in_specs, out_specs, out_shape, scratch_shapes=(), sem=None, aliases=None,
          carry=None):
    in_specs, out_specs, out_shape, scratch_shapes = list(in_specs), list(out_specs), list(out_shape), list(scratch_shapes)
    n_in, n_out, n_scr = len(in_specs), len(out_specs), len(scratch_shapes)
    kw = {"vmem_limit_bytes": VMEM_LIMIT}
    if carry is None:
        kernel_fn = functools.partial(body)
        if sem is not None:
            kw["dimension_semantics"] = sem
    else:
        n_ci, n_co = len(carry.ins), len(carry.out_shapes)
        hbm = pl.BlockSpec(memory_space=pl.ANY)
        in_specs += [hbm] * n_ci
        out_specs += [hbm] * n_co
        out_shape += carry.out_shapes
        n_s = max(carry.n_sems, 1)
        scratch_shapes += [pltpu.SemaphoreType.DMA((n_s,))] * 3
        args = list(args) + carry.ins
        aliases = dict(aliases or {})
        aliases.update({n_in + i: n_out + o for i, o in carry.aliased})
        if grid:
            kw["dimension_semantics"] = ("arbitrary",) * len(grid)

        def kernel_fn(*refs):
            a = n_in
            ins, c_ins = refs[:a], refs[a:a + n_ci]
            a += n_ci
            outs, c_outs = refs[a:a + n_out], refs[a + n_out:a + n_out + n_co]
            a += n_out + n_co
            scr, sems = refs[a:a + n_scr], _Sems(*refs[a + n_scr:a + n_scr + 3])
            if grid:
                ids = [pl.program_id(d) for d in range(len(grid))]
                first = functools.reduce(operator.and_, [i == 0 for i in ids])
                last = functools.reduce(operator.and_, [i == g - 1 for i, g in zip(ids, grid)])

                @pl.when(first)
                def _():
                    carry.start(c_ins, c_outs, sems)

                body(*ins, *outs, *scr)

                @pl.when(last)
                def _():
                    carry.finish(c_ins, c_outs, sems)
            else:
                carry.start(c_ins, c_outs, sems)
                body(*ins, *outs, *scr)
                carry.finish(c_ins, c_outs, sems)

    outs = pl.pallas_call(
        kernel_fn, name=name, grid=grid, in_specs=in_specs, out_specs=out_specs, out_shape=out_shape,
        scratch_shapes=scratch_shapes, input_output_aliases=aliases or {},
        compiler_params=pltpu.CompilerParams(**kw),
    )(*args)
    outs = list(outs)
    return outs if carry is None else (outs[:n_out], outs[n_out:])


def _run_carry(carry, name):
    _, outs = _call(lambda: None, [], name=name, in_specs=[], out_specs=[], out_shape=[], carry=carry)
    return outs


_TILES = {
    "mm_proj": (1024, 2816, 1024), "mm_branch_ssd": (1024, 1024, 2048), "mm_branch_pool": (1024, 1024, 1024),
    "mm_out": (1024, 1024, 1024), "mm_up": (1024, 1024, 1024), "mm_down": (512, 1024, 4096),
    "mm_dact": (1024, 1024, 1024), "mm_g_down": (1024, 1024, 2048), "mm_dh2": (1024, 1024, 4096),
    "mm_g_up": (1024, 1024, 2048), "mm_dmerged": (1024, 1024, 1024), "mm_g_out": (1024, 1024, 2048),
    "mm_dyp": (1024, 1024, 1024), "mm_g_bpool": (1024, 1024, 2048), "mm_g_bssd": (1024, 1024, 2048),
    "mm_dyn": (1024, 1024, 1024), "mm_g_in_a": (768, 1408, 2048), "mm_g_in_b": (256, 2816, 2048),
    "mm_dh1": (1024, 1024, 2816),
}


def _matmul(a, b, *, mode, out_dtypes, name, epi=None, tile_extras=(), row_extras=(), carry=None, a_cols=None,
            chip_blocks=False):
    M, K = (a.shape[1], a.shape[0]) if mode == "tn" else a.shape
    N = b.shape[0] if mode == "nt" else b.shape[1]
    a_start, M = a_cols if a_cols is not None else (0, M)
    tm, tn, tk = _TILES[name]
    tm, tn, tk = min(tm, M), min(tn, N), min(tk, K)
    assert M % tm == 0 and N % tn == 0 and K % tk == 0 and a_start % tm == 0, (name, M, N, K, tm, tn, tk)
    a_off = a_start // tm
    if mode == "nn":
        a_spec = pl.BlockSpec((tm, tk), lambda i, j, k: (i, k))
        b_spec = pl.BlockSpec((tk, tn), lambda i, j, k: (k, j))
        dims = ((1,), (0,))
    elif mode == "nt":
        a_spec = pl.BlockSpec((tm, tk), lambda i, j, k: (i, k))
        b_spec = pl.BlockSpec((tn, tk), lambda i, j, k: (j, k))
        dims = ((1,), (1,))
    else:
        a_spec = pl.BlockSpec((tk, tm), lambda i, j, k: (k, i + a_off))
        b_spec = pl.BlockSpec((tk, tn), lambda i, j, k: (k, j))
        dims = ((0,), (0,))
    nk = K // tk
    n_te, n_re, n_out = len(tile_extras), len(row_extras), len(out_dtypes)
    if epi is None:
        epi = lambda acc: (acc,)

    def body(a_ref, b_ref, *rest):
        extras = rest[:n_te + n_re]
        outs = rest[n_te + n_re:n_te + n_re + n_out]
        p = _dot(a_ref[...], b_ref[...], dims)

        def finish(acc):
            vals = epi(acc, *[e[...] for e in extras])
            for o, v in zip(outs, vals):
                o[...] = v.astype(o.dtype)

        if nk == 1:
            finish(p)
        else:
            acc_ref = rest[-1]
            k = pl.program_id(2)

            @pl.when(k == 0)
            def _():
                acc_ref[...] = p

            @pl.when(k > 0)
            def _():
                acc_ref[...] += p

            @pl.when(k == nk - 1)
            def _():
                finish(acc_ref[...])

    tile_spec = pl.BlockSpec((tm, tn), lambda i, j, k: (i, j))
    row_spec = pl.BlockSpec((1, tn), lambda i, j, k: (0, j))
    out_spec, out_dims = tile_spec, (M, N)
    if chip_blocks:
        assert n_te == 0 and tn * N_CHIPS == N
        out_spec, out_dims = pl.BlockSpec((None, tm, tn), lambda i, j, k: (j, i, 0)), (N_CHIPS, M, tn)
    return _call(
        body, [a, b, *tile_extras, *row_extras], name=name, grid=(M // tm, N // tn, nk),
        in_specs=[a_spec, b_spec] + [tile_spec] * n_te + [row_spec] * n_re, out_specs=[out_spec] * n_out,
        out_shape=[jax.ShapeDtypeStruct(out_dims, dt) for dt in out_dtypes],
        scratch_shapes=[pltpu.VMEM((tm, tn), F32)] if nk > 1 else [],
        sem=("parallel", "parallel", "arbitrary"), carry=carry)


def _row_tile(T):
    return min(512, T)


def _norm_mod(x, nw, scale, shift, name, carry=None):
    T = x.shape[0]
    tr = _row_tile(T)

    def body(x_ref, nw_ref, sc_ref, sh_ref, o_ref):
        xv = x_ref[...]
        r = lax.rsqrt(jnp.mean(xv * xv, axis=-1, keepdims=True) + EPS)
        o_ref[...] = ((xv * r) * nw_ref[...] * (1.0 + sc_ref[...]) + sh_ref[...]).astype(BF16)

    tile = pl.BlockSpec((tr, D), lambda i: (i, 0))
    row = pl.BlockSpec((1, D), lambda i: (0, 0))
    res = _call(body, [x, nw, scale, shift], name=name, grid=(T // tr,), in_specs=[tile, row, row, row],
                out_specs=[tile], out_shape=[jax.ShapeDtypeStruct((T, D), BF16)], sem=("parallel",), carry=carry)
    return res[0] if carry is None else (res[0][0], res[1])


def _norm_mod_bwd(x, dh, dres, nw, scale, name, branch=None, gate=None, carry=None):
    T = x.shape[0]
    tr = _row_tile(T)
    with_branch = branch is not None

    def body(x_ref, dh_ref, dr_ref, nw_ref, sc_ref, *rest):
        if with_branch:
            br_ref, g_ref, dx_ref, sums_ref, db_ref = rest
        else:
            dx_ref, sums_ref = rest
        i = pl.program_id(0)

        @pl.when(i == 0)
        def _():
            sums_ref[...] = jnp.zeros_like(sums_ref)

        xv, dhv = x_ref[...], dh_ref[...]
        r = lax.rsqrt(jnp.mean(xv * xv, axis=-1, keepdims=True) + EPS)
        xn = xv * r
        g1 = dhv * (1.0 + sc_ref[...])
        dxn = g1 * nw_ref[...]
        dx = dr_ref[...] + r * (dxn - xn * jnp.mean(dxn * xn, axis=-1, keepdims=True))
        dx_ref[...] = dx
        sums_ref[0:1, :] += jnp.sum(dhv, axis=0, keepdims=True)
        sums_ref[1:2, :] += jnp.sum(dhv * (xn * nw_ref[...]), axis=0, keepdims=True)
        sums_ref[2:3, :] += jnp.sum(g1 * xn, axis=0, keepdims=True)
        if with_branch:
            db_ref[...] = (dx * g_ref[...]).astype(BF16)
            sums_ref[3:4, :] += jnp.sum(dx * br_ref[...], axis=0, keepdims=True)

    tile = pl.BlockSpec((tr, D), lambda i: (i, 0))
    row = pl.BlockSpec((1, D), lambda i: (0, 0))
    sums = pl.BlockSpec((8, D), lambda i: (0, 0))
    ins = [x, dh, dres, nw, scale] + ([branch, gate] if with_branch else [])
    in_specs = [tile, tile, tile, row, row] + ([tile, row] if with_branch else [])
    out_specs = [tile, sums] + ([tile] if with_branch else [])
    out_shape = [jax.ShapeDtypeStruct((T, D), F32), jax.ShapeDtypeStruct((8, D), F32)]
    if with_branch:
        out_shape.append(jax.ShapeDtypeStruct((T, D), BF16))
    return _call(body, ins, name=name, grid=(T // tr,), in_specs=in_specs, out_specs=out_specs, out_shape=out_shape,
                 sem=("arbitrary",), carry=carry)


def _final_loss_bwd(x3, target, wf, down, gate_f):
    T = x3.shape[0]
    tr = _row_tile(T)
    n_steps = T // tr

    def body(x_ref, t_ref, w_ref, dn_ref, g_ref, dx_ref, dd_ref, sums_ref):
        i = pl.program_id(0)

        @pl.when(i == 0)
        def _():
            sums_ref[...] = jnp.zeros_like(sums_ref)

        xv = x_ref[...]
        r = lax.rsqrt(jnp.mean(xv * xv, axis=-1, keepdims=True) + EPS)
        xn = xv * r
        err = xn * w_ref[...] - t_ref[...]
        dy = err * (1.0 / D)
        dxn = dy * w_ref[...]
        dx = r * (dxn - xn * jnp.mean(dxn * xn, axis=-1, keepdims=True))
        dx_ref[...] = dx
        dd_ref[...] = (dx * g_ref[...]).astype(BF16)
        sums_ref[0:1, :] += jnp.sum(dy * xn, axis=0, keepdims=True)
        sums_ref[1:2, :] += jnp.sum(dx * dn_ref[...], axis=0, keepdims=True)
        sums_ref[2:3, :] += jnp.sum(err * err, axis=0, keepdims=True) * (0.5 / D)

        @pl.when(i == n_steps - 1)
        def _():
            sums_ref[3:4, :] = jnp.broadcast_to(jnp.sum(sums_ref[2:3, :], axis=1, keepdims=True), (1, D))

    tile = pl.BlockSpec((tr, D), lambda i: (i, 0))
    row = pl.BlockSpec((1, D), lambda i: (0, 0))
    sums = pl.BlockSpec((8, D), lambda i: (0, 0))
    return _call(body, [x3, target, wf, down, gate_f], name="final_loss_bwd", grid=(n_steps,),
                 in_specs=[tile, tile, row, tile, row], out_specs=[tile, tile, sums],
                 out_shape=[jax.ShapeDtypeStruct((T, D), F32), jax.ShapeDtypeStruct((T, D), BF16),
                            jax.ShapeDtypeStruct((8, D), F32)], sem=("arbitrary",))


CONV_TC = 1024


def _conv_taps(xp, w, b):
    acc = b + w[3:4, :] * xp
    for k in range(3):
        acc = acc + w[k:k + 1, :] * pltpu.roll(xp, 3 - k, 0)
    return acc


def _conv_fwd(proj, conv_w, conv_b):
    T = proj.shape[0]
    tr = _row_tile(T)
    nb, offb = tr // 8, OFF_XBC // CONV_TC

    def body(x_ref, h_ref, w_ref, b_ref, o_ref):
        halo = jnp.where(pl.program_id(0) > 0, h_ref[...], 0.0)
        xp = jnp.concatenate([halo, x_ref[...]], axis=0)
        acc = _conv_taps(xp, w_ref[...], b_ref[...])[8:]
        o_ref[...] = acc * _sigmoid(acc)

    return _call(
        body, [proj, proj, conv_w, conv_b], name="conv_fwd", grid=(T // tr, XBC // CONV_TC),
        in_specs=[pl.BlockSpec((tr, CONV_TC), lambda i, j: (i, j + offb)),
                  pl.BlockSpec((8, CONV_TC), lambda i, j: (jnp.maximum(i * nb - 1, 0), j + offb)),
                  pl.BlockSpec((4, CONV_TC), lambda i, j: (0, j)),
                  pl.BlockSpec((1, CONV_TC), lambda i, j: (0, j))],
        out_specs=[pl.BlockSpec((tr, CONV_TC), lambda i, j: (i, j))],
        out_shape=[jax.ShapeDtypeStruct((T, XBC), F32)], sem=("parallel", "parallel"))[0]


def _conv_bwd_a(dxa, proj, conv_w, conv_b):
    T = proj.shape[0]
    tr = _row_tile(T)
    nb, offb = tr // 8, OFF_XBC // CONV_TC

    def body(d_ref, x_ref, h_ref, w_ref, b_ref, o_ref, sums_ref):
        i = pl.program_id(1)

        @pl.when(i == 0)
        def _():
            sums_ref[...] = jnp.zeros_like(sums_ref)

        halo = jnp.where(i > 0, h_ref[...], 0.0)
        xp = jnp.concatenate([halo, x_ref[...]], axis=0)
        w = w_ref[...]
        taps = [pltpu.roll(xp, 3 - k, 0)[8:] for k in range(3)] + [x_ref[...]]
        acc = b_ref[...] + w[3:4, :] * taps[3]
        for k in range(3):
            acc = acc + w[k:k + 1, :] * taps[k]
        s = _sigmoid(acc)
        dxc = d_ref[...] * (s * (1.0 + acc * (1.0 - s)))
        o_ref[...] = dxc
        for k in range(4):
            sums_ref[k:k + 1, :] += jnp.sum(dxc * taps[k], axis=0, keepdims=True)
        sums_ref[4:5, :] += jnp.sum(dxc, axis=0, keepdims=True)

    return _call(
        body, [dxa, proj, proj, conv_w, conv_b], name="conv_bwd_a", grid=(XBC // CONV_TC, T // tr),
        in_specs=[pl.BlockSpec((tr, CONV_TC), lambda j, i: (i, j)),
                  pl.BlockSpec((tr, CONV_TC), lambda j, i: (i, j + offb)),
                  pl.BlockSpec((8, CONV_TC), lambda j, i: (jnp.maximum(i * nb - 1, 0), j + offb)),
                  pl.BlockSpec((4, CONV_TC), lambda j, i: (0, j)),
                  pl.BlockSpec((1, CONV_TC), lambda j, i: (0, j))],
        out_specs=[pl.BlockSpec((tr, CONV_TC), lambda j, i: (i, j)), pl.BlockSpec((8, CONV_TC), lambda j, i: (0, j))],
        out_shape=[jax.ShapeDtypeStruct((T, XBC), F32), jax.ShapeDtypeStruct((8, XBC), F32)],
        sem=("parallel", "arbitrary"))


def _conv_bwd_b(dxc, conv_w, dproj):
    T = dxc.shape[0]
    tr = _row_tile(T)
    nb, offb, last = tr // 8, OFF_XBC // CONV_TC, T // tr - 1

    def body(d_ref, h_ref, w_ref, dp_in, o_ref):
        del dp_in
        halo = jnp.where(pl.program_id(0) < last, h_ref[...], 0.0)
        xp = jnp.concatenate([d_ref[...], halo], axis=0)
        n = xp.shape[0]
        w = w_ref[...]
        acc = w[3:4, :] * xp
        for k in range(3):
            acc = acc + w[k:k + 1, :] * pltpu.roll(xp, n - (3 - k), 0)
        o_ref[...] = acc[:tr].astype(BF16)

    return _call(
        body, [dxc, dxc, conv_w, dproj], name="conv_bwd_b", grid=(T // tr, XBC // CONV_TC),
        in_specs=[pl.BlockSpec((tr, CONV_TC), lambda i, j: (i, j)),
                  pl.BlockSpec((8, CONV_TC), lambda i, j: (jnp.minimum((i + 1) * nb, T // 8 - 1), j)),
                  pl.BlockSpec((4, CONV_TC), lambda i, j: (0, j)),
                  pl.BlockSpec(memory_space=pl.ANY)],
        out_specs=[pl.BlockSpec((tr, CONV_TC), lambda i, j: (i, j + offb))],
        out_shape=[jax.ShapeDtypeStruct(dproj.shape, BF16)], aliases={3: 0}, sem=("parallel", "parallel"))[0]


def _spread(v, sel, pieces):
    out = None
    for _ in range(pieces):
        p = v.astype(BF16)
        term = _nn(p, sel)
        out = term if out is None else out + term
        v = v - p.astype(F32)
    return out


def _ssd_selectors():
    g = np.arange(GROUPS)[:, None, None]
    piece = np.arange(128)[None, :, None]
    h = np.where(piece < 3 * HEADS, piece % HEADS, -1)
    blocks = (h == 8 * g + np.arange(1024)[None, None, :] // 128)
    pairs = (h == 8 * g + np.arange(512)[None, None, :] // HEAD_DIM)
    lane = np.arange(128)[None, None, :]
    block_sum = (lane == 8 * g + np.arange(1024)[None, :, None] // 128)
    pair_sum = (lane == 8 * g + np.arange(512)[None, :, None] // HEAD_DIM)
    return [jnp.asarray(m, BF16) for m in (blocks, pairs, block_sum, pair_sum)]


def _pack3(v):
    p0 = v.astype(BF16)
    r1 = v - p0.astype(F32)
    p1 = r1.astype(BF16)
    r2 = r1 - p1.astype(F32)
    return p0 + pltpu.roll(r1, HEADS, 1).astype(BF16) + pltpu.roll(r2, 2 * HEADS, 1).astype(BF16)


def _ssd_group(g, cs_p, csT, dt_p, s_mat, causal_w, lo, blocks_ref, pairs_ref):
    csb = _nn(cs_p, blocks_ref[g])
    row = jnp.concatenate([csT[8 * g + hh:8 * g + hh + 1, :] for hh in range(8)], axis=1)
    l_w = jnp.exp(jnp.where(causal_w, csb - row, NEG))
    m_w = jnp.concatenate([s_mat] * 8, axis=1) * l_w
    cs_g = jnp.concatenate([jnp.where(lo, csb[:, 256 * jj:256 * jj + 128], csb[:, 256 * jj + 128:256 * jj + 256])
                            for jj in range(4)], axis=1)
    cs_last = cs_g[Q - 1:Q, :]
    return m_w, l_w, _nn(dt_p, pairs_ref[g]), jnp.exp(cs_g), jnp.exp(cs_last - cs_g), jnp.exp(cs_last)


def _ssd_common(dtp_ref, dtb_r, alog_r, dtb_c, alog_c):
    rows = lax.broadcasted_iota(jnp.int32, (Q, Q), 0)
    cols = lax.broadcasted_iota(jnp.int32, (Q, Q), 1)
    tri = (cols <= rows).astype(F32)
    heads = lax.broadcasted_iota(jnp.int32, (1, 128), 1) < HEADS
    raw_w = dtp_ref[...] + dtb_r[...]
    dt_w = jnp.where(heads, _softplus(raw_w), 0.0)
    a_w = -jnp.exp(alog_r[...])
    cs_w = _nn(tri, dt_w * a_w, precision=HIGH)
    aT = _softplus(dtp_ref[...].T[0:HEADS, :] + dtb_c[...]) * (-jnp.exp(alog_c[...]))
    csT = _nt(aT, tri, precision=HIGH)
    return raw_w[:, 0:HEADS], dt_w[:, 0:HEADS], a_w[:, 0:HEADS], csT, _pack3(cs_w), _pack3(dt_w)


def _ssd_fwd(xbc_a, proj, dtb_r, alog_r, dtb_c, alog_c, dsk_exp):
    T = xbc_a.shape[0]
    nc = T // Q
    dtb_r, alog_r = [jnp.pad(a, ((0, 0), (0, 128 - HEADS))) for a in (dtb_r, alog_r)]

    def body(xbc_ref, dtp_ref, dtb_r_ref, alog_r_ref, dtb_c_ref, alog_c_ref, dsk_ref, blocks_ref, pairs_ref,
             y_ref, hin_ref, h_scr):
        @pl.when(pl.program_id(0) == 0)
        def _():
            h_scr[...] = jnp.zeros_like(h_scr)

        _, _, _, csT, cs_p, dt_p = _ssd_common(dtp_ref, dtb_r_ref, alog_r_ref, dtb_c_ref, alog_c_ref)
        lo = lax.broadcasted_iota(jnp.int32, (1, 128), 1) < HEAD_DIM
        hi = jnp.logical_not(lo)
        causal_w = (lax.broadcasted_iota(jnp.int32, (Q, 1024), 1) & (Q - 1)) <= lax.broadcasted_iota(jnp.int32, (Q, 1024), 0)
        for g in range(GROUPS):
            gs = slice(512 * g, 512 * (g + 1))
            hs = slice(128 * g, 128 * (g + 1))
            xs_g = xbc_ref[:, gs]
            b_g = xbc_ref[:, DI + STATE * g:DI + STATE * (g + 1)].astype(BF16)
            c_g = xbc_ref[:, DI + 512 + STATE * g:DI + 512 + STATE * (g + 1)].astype(BF16)
            m_w, _, dt_g, ecs_g, dec_g, cd_g = _ssd_group(g, cs_p, csT, dt_p, _nt(c_g, b_g), causal_w, lo, blocks_ref, pairs_ref)
            m_b = m_w.astype(BF16)
            xdt = xs_g * dt_g
            xdt_b = xdt.astype(BF16)
            ys = []
            for jj in range(4):
                xp = xdt_b[:, 128 * jj:128 * (jj + 1)]
                x_ab = jnp.concatenate([jnp.where(lo, xp, jnp.zeros_like(xp)), jnp.where(hi, xp, jnp.zeros_like(xp))], axis=0)
                ys.append(_nn(m_b[:, 256 * jj:256 * (jj + 1)], x_ab))
            h_g = h_scr[hs, :]
            hin_ref[0, hs, :] = h_g
            y_ref[:, gs] = jnp.concatenate(ys, axis=1) + _nn(c_g, h_g.astype(BF16)) * ecs_g + dsk_ref[:, gs] * xs_g
            h_scr[hs, :] = h_g * cd_g + _tn(b_g, (xdt * dec_g).astype(BF16))

    small_r = pl.BlockSpec((1, 128), lambda c: (0, 0))
    small_c = pl.BlockSpec((HEADS, 1), lambda c: (0, 0))
    blocks, pairs, _, _ = _ssd_selectors()
    whole = lambda a: pl.BlockSpec(a.shape, lambda c: (0,) * a.ndim)
    return _call(
        body, [xbc_a, proj, dtb_r, alog_r, dtb_c, alog_c, dsk_exp, blocks, pairs], name="ssd_fwd", grid=(nc,),
        in_specs=[pl.BlockSpec((Q, XBC), lambda c: (c, 0)),
                  pl.BlockSpec((Q, 128), lambda c: (c, OFF_DT // 128)),
                  small_r, small_r, small_c, small_c,
                  pl.BlockSpec((1, DI), lambda c: (0, 0)), whole(blocks), whole(pairs)],
        out_specs=[pl.BlockSpec((Q, DI), lambda c: (c, 0)), pl.BlockSpec((1, 512, 512), lambda c: (c, 0, 0))],
        out_shape=[jax.ShapeDtypeStruct((T, DI), F32), jax.ShapeDtypeStruct((nc, 512, 512), F32)],
        scratch_shapes=[pltpu.VMEM((512, 512), F32)], sem=("arbitrary",))


def _ssd_bwd(dy, xbc_a, proj, hin, dtb_r, alog_r, dtb_c, alog_c, dsk_exp, dproj, carry=None):
    T = xbc_a.shape[0]
    nc = T // Q
    dtb_r, alog_r = [jnp.pad(a, ((0, 0), (0, 128 - HEADS))) for a in (dtb_r, alog_r)]

    def body(dy_ref, xbc_ref, dtp_ref, hin_ref, dtb_r_ref, alog_r_ref, dtb_c_ref, alog_c_ref, dsk_ref, dp_in,
             blocks_ref, pairs_ref, block_sum_ref, pair_sum_ref, dxa_ref, dp_ref, dsk_sum_ref, small_ref, dh_scr):
        del dp_in

        @pl.when(pl.program_id(0) == 0)
        def _():
            dh_scr[...] = jnp.zeros_like(dh_scr)
            dsk_sum_ref[...] = jnp.zeros_like(dsk_sum_ref)
            small_ref[...] = jnp.zeros_like(small_ref)

        raw, dt, a_r, csT, cs_p, dt_p = _ssd_common(dtp_ref, dtb_r_ref, alog_r_ref, dtb_c_ref, alog_c_ref)
        lo = lax.broadcasted_iota(jnp.int32, (1, 128), 1) < HEAD_DIM
        hi = jnp.logical_not(lo)
        sub32 = lax.broadcasted_iota(jnp.int32, (HEADS, 1), 0)
        causal_w = (lax.broadcasted_iota(jnp.int32, (Q, 1024), 1) & (Q - 1)) <= lax.broadcasted_iota(jnp.int32, (Q, 1024), 0)
        dcs_c = jnp.zeros((Q, 128), F32)
        dcs_r = jnp.zeros((HEADS, Q), F32)
        dcs_l = jnp.zeros((8, 128), F32)
        ddt_x = jnp.zeros((Q, 128), F32)
        for g in range(GROUPS):
            gs = slice(512 * g, 512 * (g + 1))
            hs = slice(128 * g, 128 * (g + 1))
            xs_g, dy_g = xbc_ref[:, gs], dy_ref[:, gs]
            b_g = xbc_ref[:, DI + STATE * g:DI + STATE * (g + 1)].astype(BF16)
            c_g = xbc_ref[:, DI + 512 + STATE * g:DI + 512 + STATE * (g + 1)].astype(BF16)
            m_w, l_w, dt_g, ecs_g, dec_g, cd_g = _ssd_group(g, cs_p, csT, dt_p, _nt(c_g, b_g), causal_w, lo, blocks_ref, pairs_ref)
            m_b = m_w.astype(BF16)
            xdt = xs_g * dt_g
            xdt_b, dy_b = xdt.astype(BF16), dy_g.astype(BF16)
            dms, dxs = [], []
            for jj in range(4):
                xp, dyp = xdt_b[:, 128 * jj:128 * (jj + 1)], dy_b[:, 128 * jj:128 * (jj + 1)]
                dy_ab = jnp.concatenate([jnp.where(lo, dyp, jnp.zeros_like(dyp)), jnp.where(hi, dyp, jnp.zeros_like(dyp))], axis=0)
                dm_ab = _nt(dy_ab, xp)
                dms += [dm_ab[:Q], dm_ab[Q:]]
                dx_ab = _tn(m_b[:, 256 * jj:256 * (jj + 1)], dyp)
                dxs.append(jnp.where(lo, dx_ab[:Q], dx_ab[Q:]))
            dm_w = jnp.concatenate(dms, axis=1)
            w_w = dm_w * m_w
            dcs_c = dcs_c + _spread(w_w, block_sum_ref[g], 2)
            w_cols = jnp.sum(w_w, axis=0, keepdims=True)
            for hh in range(8):
                dcs_r = dcs_r + jnp.where(sub32 == 8 * g + hh, w_cols[:, 128 * hh:128 * (hh + 1)], 0.0)
            dl_w = dm_w * l_w
            ds_mat = dl_w[:, 0:128]
            for hh in range(1, 8):
                ds_mat = ds_mat + dl_w[:, 128 * hh:128 * (hh + 1)]
            hin_g = hin_ref[0, hs, :]
            hin_b = hin_g.astype(BF16)
            dh_g = dh_scr[hs, :]
            dh_b = dh_g.astype(BF16)
            g_mat = _nn(b_g, dh_b)
            xdec = xdt * dec_g
            xg = xdec * g_mat
            dxdt = jnp.concatenate(dxs, axis=1) + dec_g * g_mat
            sums = _spread(jnp.concatenate([dy_g * (_nn(c_g, hin_b) * ecs_g) - xg, dxdt * xs_g], axis=0), pair_sum_ref[g], 2)
            dcs_c = dcs_c + sums[:Q]
            ddt_x = ddt_x + sums[Q:]
            last = jnp.sum(xg, axis=0, keepdims=True) + jnp.sum(dh_g * hin_g, axis=0, keepdims=True) * cd_g
            dcs_l = dcs_l + _spread(jnp.broadcast_to(last, (8, 512)), pair_sum_ref[g], 2)
            dz = (dy_g * ecs_g).astype(BF16)
            ds_b = ds_mat.astype(BF16)
            dxa_ref[:, gs] = dxdt * dt_g + dy_g * dsk_ref[:, gs]
            dxa_ref[:, DI + STATE * g:DI + STATE * (g + 1)] = _nt(xdec.astype(BF16), dh_b) + _tn(ds_b, c_g)
            dxa_ref[:, DI + 512 + STATE * g:DI + 512 + STATE * (g + 1)] = _nt(dz, hin_b) + _nn(ds_b, b_g)
            dh_scr[hs, :] = _tn(c_g, dz) + dh_g * cd_g
            dsk_sum_ref[0:1, gs] += jnp.sum(dy_g * xs_g, axis=0, keepdims=True)

        rows = lax.broadcasted_iota(jnp.int32, (Q, Q), 0)
        cols = lax.broadcasted_iota(jnp.int32, (Q, Q), 1)
        tri_t = (cols >= rows).astype(F32)
        last_row = lax.broadcasted_iota(jnp.int32, (Q, 1), 0) == Q - 1
        dcs = (dcs_c + jnp.where(last_row, dcs_l[0:1, :], 0.0))[:, 0:HEADS]
        da = _nn(tri_t, dcs, precision=HIGH) - _nt(tri_t, dcs_r, precision=HIGH)
        ddt_raw = (ddt_x[:, 0:HEADS] + da * a_r) * _sigmoid(raw)
        small_ref[0:1, :] += jnp.sum(da * dt, axis=0, keepdims=True) * a_r
        small_ref[1:2, :] += jnp.sum(ddt_raw, axis=0, keepdims=True)
        dp_ref[...] = jnp.zeros_like(dp_ref)
        dp_ref[:, 0:HEADS] = ddt_raw.astype(BF16)

    rev = lambda c: nc - 1 - c
    small_r = pl.BlockSpec((1, 128), lambda c: (0, 0))
    small_c = pl.BlockSpec((HEADS, 1), lambda c: (0, 0))
    selectors = _ssd_selectors()
    whole = lambda a: pl.BlockSpec(a.shape, lambda c: (0,) * a.ndim)
    return _call(
        body, [dy, xbc_a, proj, hin, dtb_r, alog_r, dtb_c, alog_c, dsk_exp, dproj, *selectors], name="ssd_bwd", grid=(nc,),
        in_specs=[pl.BlockSpec((Q, DI), lambda c: (rev(c), 0)),
                  pl.BlockSpec((Q, XBC), lambda c: (rev(c), 0)),
                  pl.BlockSpec((Q, 128), lambda c: (rev(c), OFF_DT // 128)),
                  pl.BlockSpec((1, 512, 512), lambda c: (rev(c), 0, 0)),
                  small_r, small_r, small_c, small_c,
                  pl.BlockSpec((1, DI), lambda c: (0, 0)),
                  pl.BlockSpec(memory_space=pl.ANY)] + [whole(a) for a in selectors],
        out_specs=[pl.BlockSpec((Q, XBC), lambda c: (rev(c), 0)),
                   pl.BlockSpec((Q, 256), lambda c: (rev(c), OFF_DT // 256)),
                   pl.BlockSpec((8, DI), lambda c: (0, 0)),
                   pl.BlockSpec((8, HEADS), lambda c: (0, 0))],
        out_shape=[jax.ShapeDtypeStruct((T, XBC), F32), jax.ShapeDtypeStruct(dproj.shape, BF16),
                   jax.ShapeDtypeStruct((8, DI), F32), jax.ShapeDtypeStruct((8, HEADS), F32)],
        aliases={9: 1}, scratch_shapes=[pltpu.VMEM((512, 512), F32)], sem=("arbitrary",), carry=carry)


def _gate_norm(y, proj, w):
    T = y.shape[0]
    tr = _row_tile(T)

    def body(y_ref, z_ref, w_ref, o_ref):
        for g in range(GROUPS):
            gs = slice(512 * g, 512 * (g + 1))
            z = z_ref[:, gs]
            yg = y_ref[:, gs] * (z * _sigmoid(z))
            r = lax.rsqrt(jnp.mean(yg * yg, axis=-1, keepdims=True) + EPS)
            o_ref[:, gs] = (yg * r * w_ref[:, gs]).astype(BF16)

    tile = pl.BlockSpec((tr, DI), lambda i: (i, 0))
    return _call(body, [y, proj, w], name="gate_norm", grid=(T // tr,),
                 in_specs=[tile, tile, pl.BlockSpec((1, DI), lambda i: (0, 0))], out_specs=[tile],
                 out_shape=[jax.ShapeDtypeStruct((T, DI), BF16)], sem=("parallel",))[0]


def _gate_norm_bwd(dyn, y, proj, w, dproj):
    T = y.shape[0]
    tr = _row_tile(T)

    def body(d_ref, y_ref, z_ref, w_ref, dp_in, dy_ref, dz_ref, sums_ref):
        del dp_in

        @pl.when(pl.program_id(0) == 0)
        def _():
            sums_ref[...] = jnp.zeros_like(sums_ref)

        for g in range(GROUPS):
            gs = slice(512 * g, 512 * (g + 1))
            z, yv, d = z_ref[:, gs], y_ref[:, gs], d_ref[:, gs]
            s = _sigmoid(z)
            silu = z * s
            yg = yv * silu
            r = lax.rsqrt(jnp.mean(yg * yg, axis=-1, keepdims=True) + EPS)
            yn = yg * r
            sums_ref[0:1, gs] += jnp.sum(d * yn, axis=0, keepdims=True)
            dn = d * w_ref[:, gs]
            dyg = r * (dn - yn * jnp.mean(dn * yn, axis=-1, keepdims=True))
            dy_ref[:, gs] = dyg * silu
            dz_ref[:, gs] = (dyg * yv * (s * (1.0 + z * (1.0 - s)))).astype(BF16)

    tile = pl.BlockSpec((tr, DI), lambda i: (i, 0))
    return _call(
        body, [dyn, y, proj, w, dproj], name="gate_norm_bwd", grid=(T // tr,),
        in_specs=[tile, tile, tile, pl.BlockSpec((1, DI), lambda i: (0, 0)), pl.BlockSpec(memory_space=pl.ANY)],
        out_specs=[tile, tile, pl.BlockSpec((8, DI), lambda i: (0, 0))],
        out_shape=[jax.ShapeDtypeStruct((T, DI), F32), jax.ShapeDtypeStruct(dproj.shape, BF16),
                   jax.ShapeDtypeStruct((8, DI), F32)],
        aliases={4: 1}, sem=("arbitrary",))


def _pool_fwd(proj, pool_w_b, pool_scale):
    T = proj.shape[0]
    tr = _row_tile(T)
    nb = tr // 16

    def body(u_ref, h_ref, pw_ref, ps_ref, pooled_ref, pw_out_ref, yps_ref):
        i = pl.program_id(0)
        t = i * tr + lax.broadcasted_iota(jnp.int32, (tr, 1), 0)
        for g, win in enumerate(POOL_WINDOWS):
            gs = slice(GW * g, GW * (g + 1))
            u = u_ref[:, gs]
            s = jnp.concatenate([jnp.where(i > 0, h_ref[:, gs], 0.0), u], axis=0)
            sh = 1
            while sh < win:
                s = s + pltpu.roll(s, sh, 0)
                sh *= 2
            pooled = (s[16:] * (1.0 / jnp.minimum(t + 1, win).astype(F32)) - u).astype(BF16)
            pooled_ref[:, gs] = pooled
            pwv = _nn(pooled, pw_ref[g])
            pw_out_ref[:, gs] = pwv
            yps_ref[:, gs] = (pwv * ps_ref[:, gs]).astype(BF16)

    tile = pl.BlockSpec((tr, D), lambda i: (i, 0))
    return _call(
        body, [proj, proj, pool_w_b, pool_scale], name="pool_fwd", grid=(T // tr,),
        in_specs=[pl.BlockSpec((tr, D), lambda i: (i, OFF_POOL // D)),
                  pl.BlockSpec((16, D), lambda i: (jnp.maximum(i * nb - 1, 0), OFF_POOL // D)),
                  pl.BlockSpec((4, GW, GW), lambda i: (0, 0, 0)),
                  pl.BlockSpec((1, D), lambda i: (0, 0))],
        out_specs=[tile, tile, tile],
        out_shape=[jax.ShapeDtypeStruct((T, D), BF16), jax.ShapeDtypeStruct((T, D), F32),
                   jax.ShapeDtypeStruct((T, D), BF16)], sem=("parallel",))


def _pool_bwd(dyp, pw_out, pooled, pool_w_b, pool_scale, dproj):
    T = dyp.shape[0]
    tr = _row_tile(T)
    nb, last = tr // 16, T // tr - 1

    def body(d_ref, h_ref, pwo_ref, pooled_ref, pw_ref, ps_ref, dp_in, du_ref, gpw_ref, sums_ref):
        del dp_in
        i = pl.program_id(0)

        @pl.when(i == 0)
        def _():
            gpw_ref[...] = jnp.zeros_like(gpw_ref)
            sums_ref[...] = jnp.zeros_like(sums_ref)

        n = tr + 16
        t = i * tr + lax.broadcasted_iota(jnp.int32, (n, 1), 0)
        sums_ref[0:1, :] += jnp.sum(d_ref[...] * pwo_ref[...], axis=0, keepdims=True)
        for g, win in enumerate(POOL_WINDOWS):
            gs = slice(GW * g, GW * (g + 1))
            d_ext = jnp.concatenate([d_ref[:, gs], jnp.where(i < last, h_ref[:, gs], 0.0)], axis=0)
            dpw = (d_ext * ps_ref[:, gs]).astype(BF16)
            dpooled = _nt(dpw, pw_ref[g])
            s = jnp.where(t < T, dpooled * (1.0 / jnp.minimum(t + 1, win).astype(F32)), 0.0)
            sh = 1
            while sh < win:
                s = s + pltpu.roll(s, n - sh, 0)
                sh *= 2
            du_ref[:, gs] = (s[:tr] - dpooled[:tr]).astype(BF16)
            gpw_ref[g] += _tn(pooled_ref[:, gs], dpw[:tr])

    tile = pl.BlockSpec((tr, D), lambda i: (i, 0))
    return _call(
        body, [dyp, dyp, pw_out, pooled, pool_w_b, pool_scale, dproj], name="pool_bwd", grid=(T // tr,),
        in_specs=[tile, pl.BlockSpec((16, D), lambda i: (jnp.minimum((i + 1) * nb, T // 16 - 1), 0)), tile, tile,
                  pl.BlockSpec((4, GW, GW), lambda i: (0, 0, 0)), pl.BlockSpec((1, D), lambda i: (0, 0)),
                  pl.BlockSpec(memory_space=pl.ANY)],
        out_specs=[pl.BlockSpec((tr, D), lambda i: (i, OFF_POOL // D)),
                   pl.BlockSpec((4, GW, GW), lambda i: (0, 0, 0)), pl.BlockSpec((8, D), lambda i: (0, 0))],
        out_shape=[jax.ShapeDtypeStruct(dproj.shape, BF16), jax.ShapeDtypeStruct((4, GW, GW), F32),
                   jax.ShapeDtypeStruct((8, D), F32)],
        aliases={6: 0}, sem=("arbitrary",))


def _merge(proj, y_ssd, y_pool):
    T = proj.shape[0]
    tr = _row_tile(T)

    def body(g_ref, a_ref, b_ref, o_ref):
        o_ref[...] = (_sigmoid(g_ref[:, 0:D]) * a_ref[...] + _sigmoid(g_ref[:, D:2 * D]) * b_ref[...]).astype(BF16)

    tile = pl.BlockSpec((tr, D), lambda i: (i, 0))
    return _call(body, [proj, y_ssd, y_pool], name="merge", grid=(T // tr,),
                 in_specs=[pl.BlockSpec((tr, 2 * D), lambda i: (i, OFF_GATE // (2 * D))), tile, tile], out_specs=[tile],
                 out_shape=[jax.ShapeDtypeStruct((T, D), BF16)], sem=("parallel",))[0]


def _merge_bwd(dmerged, proj, y_ssd, y_pool):
    T = proj.shape[0]
    tr = _row_tile(T)

    def body(d_ref, g_ref, a_ref, b_ref, da_ref, db_ref, dg_ref):
        d = d_ref[...]
        ga, gb = _sigmoid(g_ref[:, 0:D]), _sigmoid(g_ref[:, D:2 * D])
        da_ref[...] = (d * ga).astype(BF16)
        db_ref[...] = (d * gb).astype(BF16)
        dg_ref[:, 0:D] = (d * a_ref[...] * ga * (1.0 - ga)).astype(BF16)
        dg_ref[:, D:2 * D] = (d * b_ref[...] * gb * (1.0 - gb)).astype(BF16)

    tile = pl.BlockSpec((tr, D), lambda i: (i, 0))
    gates = pl.BlockSpec((tr, 2 * D), lambda i: (i, OFF_GATE // (2 * D)))
    return _call(body, [dmerged, proj, y_ssd, y_pool], name="merge_bwd", grid=(T // tr,),
                 in_specs=[tile, gates, tile, tile], out_specs=[tile, tile, gates],
                 out_shape=[jax.ShapeDtypeStruct((T, D), BF16), jax.ShapeDtypeStruct((T, D), BF16),
                            jax.ShapeDtypeStruct((T, NP), BF16)], sem=("parallel",))


def _adamw(w, g, m, v, name, carry=None):
    R, C = w.shape
    tr = R if R <= 128 else 128
    assert R % tr == 0

    def body(w_ref, g_ref, m_ref, v_ref, d_ref, mo_ref, vo_ref):
        gv = g_ref[...]
        mn = ADAM_B1 * m_ref[...] + (1.0 - ADAM_B1) * gv
        vn = ADAM_B2 * v_ref[...] + (1.0 - ADAM_B2) * (gv * gv)
        m_hat = mn * (1.0 / (1.0 - ADAM_B1 ** ADAM_STEP))
        v_hat = vn * (1.0 / (1.0 - ADAM_B2 ** ADAM_STEP))
        d_ref[...] = -ADAM_LR * (m_hat / (jnp.sqrt(v_hat) + ADAM_EPS) + ADAM_WD * w_ref[...])
        mo_ref[...] = mn
        vo_ref[...] = vn

    tile = pl.BlockSpec((tr, C), lambda i: (i, 0))
    sds = jax.ShapeDtypeStruct((R, C), F32)
    return _call(body, [w, g, m, v], name=name, grid=(R // tr,), in_specs=[tile] * 4, out_specs=[tile] * 3,
                 out_shape=[sds] * 3, sem=("parallel",), carry=carry)


def _me():
    return lax.axis_index("x"), lax.axis_index("y"), lax.axis_index("c")


def _xor_peer(x, y, c, p):
    return (x ^ ((p >> 2) & 1), y ^ ((p >> 1) & 1), c ^ (p & 1))


def _ada_fwd(c_row, w_ada, b_ada_mine, carry=None):
    n_cols = w_ada.shape[1]

    def body(c_ref, w_ref, b_ref, mod_ref, c8_ref, csend, mpart, modbuf, send_sems, recv_sems):
        x, y, c = _me()
        me = 4 * x + 2 * y + c
        chip = 2 * x + y
        csend[...] = jnp.broadcast_to(c_ref[...], csend.shape)
        c8_ref[me] = csend[...]

        def c_copy(p):
            return pltpu.make_async_remote_copy(
                src_ref=csend, dst_ref=c8_ref.at[me], send_sem=send_sems.at[p - 1], recv_sem=recv_sems.at[p - 1],
                device_id=_xor_peer(x, y, c, p), device_id_type=MESH)

        for p in range(1, 8):
            c_copy(p).start()
        for p in range(1, 8):
            c_copy(p).wait_recv()
        cs = jnp.concatenate([c8_ref[d][0:1, :] for d in range(8)], axis=0)
        mpart[...] = _nn(cs * _sigmoid(cs), w_ref[...], precision=HIGH) + b_ref[...]
        modbuf[chip] = mpart[...]

        def m_copy(m):
            return pltpu.make_async_remote_copy(
                src_ref=mpart, dst_ref=modbuf.at[chip], send_sem=send_sems.at[6 + m], recv_sem=recv_sems.at[6 + m],
                device_id=_xor_peer(x, y, c, 2 * m), device_id_type=MESH)

        for m in range(1, 4):
            m_copy(m).start()
        for m in range(1, 4):
            m_copy(m).wait_recv()
        mine = lax.broadcasted_iota(jnp.int32, (8, 1), 0) == me
        for k in range(N_CHIPS):
            mod_ref[:, n_cols * k:n_cols * (k + 1)] = jnp.sum(jnp.where(mine, modbuf[k], 0.0), axis=0, keepdims=True)
        for p in range(1, 8):
            c_copy(p).wait_send()
        for m in range(1, 4):
            m_copy(m).wait_send()

    vmem = pl.BlockSpec(memory_space=pltpu.VMEM)
    return _call(
        body, [c_row, w_ada, b_ada_mine], name="ada_fwd", in_specs=[vmem, vmem, vmem], out_specs=[vmem, vmem],
        out_shape=[jax.ShapeDtypeStruct((1, N_CHIPS * n_cols), F32), jax.ShapeDtypeStruct((8, 8, D), F32)],
        scratch_shapes=[pltpu.VMEM((8, D), F32), pltpu.VMEM((8, n_cols), F32), pltpu.VMEM((N_CHIPS, 8, n_cols), F32),
                        pltpu.SemaphoreType.DMA((10,)), pltpu.SemaphoreType.DMA((10,))], carry=carry)


def _gather_small(vec, carry=None):
    rows = vec.shape[0]

    def body(v_ref, all_ref, tot_ref, dsk_ref, send_sems, recv_sems):
        x, y, c = _me()
        me = 4 * x + 2 * y + c
        all_ref[me] = v_ref[...]

        def copy(p):
            return pltpu.make_async_remote_copy(
                src_ref=v_ref, dst_ref=all_ref.at[me], send_sem=send_sems.at[p - 1], recv_sem=recv_sems.at[p - 1],
                device_id=_xor_peer(x, y, c, p), device_id_type=MESH)

        for p in range(1, 8):
            copy(p).start()
        for p in range(1, 8):
            copy(p).wait_recv()
        tot = all_ref[0]
        for d in range(1, 8):
            tot = tot + all_ref[d]
        tot_ref[...] = tot
        seg = tot[SMALL_OFF["d_skip"] // 128:SMALL_OFF["d_skip"] // 128 + 16, :]
        lane = lax.broadcasted_iota(jnp.int32, (1, 128), 1)
        sa = jnp.sum(jnp.where(lane < HEAD_DIM, seg, 0.0), axis=1, keepdims=True)
        sb = jnp.sum(jnp.where(lane < HEAD_DIM, 0.0, seg), axis=1, keepdims=True)
        dsk_ref[...] = jnp.where(lane == 0, sa, jnp.where(lane == 1, sb, 0.0))
        for p in range(1, 8):
            copy(p).wait_send()

    vmem = pl.BlockSpec(memory_space=pltpu.VMEM)
    return _call(
        body, [vec], name="gather_small", in_specs=[vmem], out_specs=[vmem, vmem, vmem],
        out_shape=[jax.ShapeDtypeStruct((8, rows, 128), F32), jax.ShapeDtypeStruct((rows, 128), F32),
                   jax.ShapeDtypeStruct((16, 128), F32)],
        scratch_shapes=[pltpu.SemaphoreType.DMA((7,)), pltpu.SemaphoreType.DMA((7,))], carry=carry)


def _gather_carry(shards):
    n = len(shards)

    def copies(ins, outs, sems):
        x, y, c = _me()
        chip = 2 * x + y

        def half(w, which):
            h = shards[w].shape[0] // 2
            return pl.ds(which * h, h)

        def first(w, m):
            return pltpu.make_async_remote_copy(
                src_ref=ins[w].at[half(w, c)], dst_ref=outs[w].at[chip, half(w, c)],
                send_sem=sems.send(6 * w + m - 1), recv_sem=sems.recv(6 * w + m - 1),
                device_id=_xor_peer(x, y, c, 2 * m), device_id_type=MESH)

        def landed(w, m):
            return pltpu.make_async_remote_copy(
                src_ref=ins[w].at[half(w, c)], dst_ref=outs[w].at[chip ^ m, half(w, c)],
                send_sem=sems.send(6 * w + m - 1), recv_sem=sems.recv(6 * w + m - 1),
                device_id=_xor_peer(x, y, c, 2 * m), device_id_type=MESH)

        def passed(w, m, which):
            part = outs[w].at[chip ^ m, half(w, which)]
            return pltpu.make_async_remote_copy(
                src_ref=part, dst_ref=part, send_sem=sems.send(6 * w + 2 + m), recv_sem=sems.recv(6 * w + 2 + m),
                device_id=(x, y, 1 - c), device_id_type=MESH)

        return c, first, landed, passed

    pairs = [(w, m) for w in range(n) for m in range(1, 4)]

    def start(ins, outs, sems):
        _, first, _, _ = copies(ins, outs, sems)
        for w, m in pairs:
            first(w, m).start()

    def finish(ins, outs, sems):
        c, first, landed, passed = copies(ins, outs, sems)
        for w, m in pairs:
            landed(w, m).wait_recv()
            passed(w, m, c).start()
        for w, m in pairs:
            passed(w, m, 1 - c).wait_recv()
        for w, m in pairs:
            first(w, m).wait_send()
            passed(w, m, c).wait_send()

    return _Carry(shards, [jax.ShapeDtypeStruct((N_CHIPS,) + s.shape, s.dtype) for s in shards], 6 * n, start, finish)


def _pair_exchange_carry(grads):
    n = len(grads)

    def copy(ins, outs, sems, w):
        x, y, c = _me()
        h = grads[w].shape[1] // 2
        return pltpu.make_async_remote_copy(
            src_ref=ins[w].at[:, pl.ds((1 - c) * h, h)], dst_ref=outs[w],
            send_sem=sems.send(w), recv_sem=sems.recv(w), device_id=(x, y, 1 - c), device_id_type=MESH)

    def start(ins, outs, sems):
        for w in range(n):
            copy(ins, outs, sems, w).start()

    def finish(ins, outs, sems):
        for w in range(n):
            copy(ins, outs, sems, w).wait()

    return _Carry(grads, [jax.ShapeDtypeStruct((N_CHIPS, g.shape[1] // 2, g.shape[2]), g.dtype) for g in grads], n,
                  start, finish)


def _chip_exchange_carry(partials):
    n = len(partials)

    def copier(ins, outs, sems):
        x, y, c = _me()
        chip = 2 * x + y

        def copy(w, m, landed):
            return pltpu.make_async_remote_copy(
                src_ref=ins[w].at[chip ^ m], dst_ref=outs[w].at[(chip ^ m) if landed else chip],
                send_sem=sems.send(3 * w + m - 1), recv_sem=sems.recv(3 * w + m - 1),
                device_id=_xor_peer(x, y, c, 2 * m), device_id_type=MESH)

        return copy

    pairs = [(w, m) for w in range(n) for m in range(1, 4)]

    def start(ins, outs, sems):
        copy = copier(ins, outs, sems)
        for w, m in pairs:
            copy(w, m, False).start()

    def finish(ins, outs, sems):
        copy = copier(ins, outs, sems)
        for w, m in pairs:
            copy(w, m, True).wait_recv()
        for w, m in pairs:
            copy(w, m, False).wait_send()

    return _Carry(partials, [jax.ShapeDtypeStruct(p.shape, p.dtype) for p in partials], 3 * n, start, finish)


def _pair_share_carry(shards):
    n = len(shards)

    def copier(ins, outs, sems):
        x, y, c = _me()

        def copy(w, which):
            h = shards[w].shape[0] // 2
            rows = pl.ds(which * h, h)
            return pltpu.make_async_remote_copy(
                src_ref=ins[w].at[rows], dst_ref=outs[w].at[rows],
                send_sem=sems.send(w), recv_sem=sems.recv(w), device_id=(x, y, 1 - c), device_id_type=MESH)

        return c, copy

    def start(ins, outs, sems):
        c, copy = copier(ins, outs, sems)
        for w in range(n):
            copy(w, c).start()

    def finish(ins, outs, sems):
        c, copy = copier(ins, outs, sems)
        for w in range(n):
            copy(w, 1 - c).wait_recv()
        for w in range(n):
            copy(w, c).wait_send()

    return _Carry(shards, [jax.ShapeDtypeStruct(s.shape, s.dtype) for s in shards], n, start, finish,
                  aliased=[(w, w) for w in range(n)])


def _pair_sum(g, part, idx, name):
    _, h, C = part.shape
    tr = min(512, h)
    nb = h // tr

    def body(idx_ref, g_ref, p_ref, o16_ref, own_ref):
        v = g_ref[...].astype(F32) + p_ref[...].astype(F32)
        o16_ref[...] = v.astype(BF16)

        @pl.when(pl.program_id(1) == idx_ref[1])
        def _():
            own_ref[...] = v

    return pl.pallas_call(
        body, name=name,
        grid_spec=pltpu.PrefetchScalarGridSpec(
            num_scalar_prefetch=1, grid=(nb, N_CHIPS),
            in_specs=[pl.BlockSpec((None, tr, C), lambda i, s, idx_ref: (s, idx_ref[0] * nb + i, 0)),
                      pl.BlockSpec((None, tr, C), lambda i, s, idx_ref: (s, i, 0))],
            out_specs=[pl.BlockSpec((None, tr, C), lambda i, s, idx_ref: (s, i, 0)),
                       pl.BlockSpec((tr, C), lambda i, s, idx_ref: (i, 0))]),
        out_shape=[jax.ShapeDtypeStruct(part.shape, BF16), jax.ShapeDtypeStruct((h, C), F32)],
        compiler_params=pltpu.CompilerParams(dimension_semantics=("arbitrary", "arbitrary"), vmem_limit_bytes=VMEM_LIMIT),
    )(idx, g, part)


def _chip_sum(own, slots, idx, name):
    h, C = own.shape
    tr = min(512, h)
    nb = h // tr

    def body(idx_ref, own_ref, s1_ref, s2_ref, s3_ref, o_ref):
        del idx_ref
        o_ref[...] = ((own_ref[...] + s1_ref[...].astype(F32)) + s2_ref[...].astype(F32)) + s3_ref[...].astype(F32)

    def slot(m):
        return pl.BlockSpec((None, tr, C), lambda i, idx_ref: (idx_ref[1] ^ m, i, 0))

    return pl.pallas_call(
        body, name=name,
        grid_spec=pltpu.PrefetchScalarGridSpec(
            num_scalar_prefetch=1, grid=(nb,),
            in_specs=[pl.BlockSpec((tr, C), lambda i, idx_ref: (i, 0)), slot(1), slot(2), slot(3)],
            out_specs=pl.BlockSpec((tr, C), lambda i, idx_ref: (idx_ref[0] * nb + i, 0))),
        out_shape=jax.ShapeDtypeStruct((2 * h, C), F32),
        compiler_params=pltpu.CompilerParams(dimension_semantics=("parallel",), vmem_limit_bytes=VMEM_LIMIT),
    )(idx, own, slots, slots, slots)


class _Reducer:
    def __init__(self, idx):
        self.idx, self.chips, self.p16, self.own, self.mine, self.final = idx, {}, {}, {}, {}, {}

    def add(self, name, whole, chip_blocks=False):
        self.chips[name] = whole if chip_blocks else _chips_from_whole(name, whole)

    def pair(self, names):
        return _pair_exchange_carry([self.chips[n] for n in names])

    def take_pair(self, names, outs):
        for n, part in zip(names, outs):
            self.p16[n], self.own[n] = _pair_sum(self.chips.pop(n), part, self.idx, "pair_sum_" + n)

    def chip(self, names):
        return _chip_exchange_carry([self.p16[n] for n in names])

    def take_chip(self, names, outs):
        for n, slots in zip(names, outs):
            del self.p16[n]
            self.mine[n] = _chip_sum(self.own.pop(n), slots, self.idx, "chip_sum_" + n)

    def share(self, names):
        return _pair_share_carry([self.mine[n] for n in names])

    def take_share(self, names, outs):
        for n, s in zip(names, outs):
            del self.mine[n]
            self.final[n] = s


def _w_ada_grad(c8, dmod_cols):
    n_cols = dmod_cols.shape[1]
    tn = 512

    def body(c_ref, d_ref, o_ref):
        cv = c_ref[...]
        o_ref[...] = _tn(cv * _sigmoid(cv), d_ref[...], precision=HIGH)

    return _call(body, [c8, dmod_cols], name="w_ada_grad", grid=(n_cols // tn,),
                 in_specs=[pl.BlockSpec((8, D), lambda j: (0, 0)), pl.BlockSpec((8, tn), lambda j: (0, j))],
                 out_specs=[pl.BlockSpec((D, tn), lambda j: (0, j))],
                 out_shape=[jax.ShapeDtypeStruct((D, n_cols), F32)], sem=("parallel",))[0]


_SMALL_SEGS = (("dmod", 6144), ("norm_mix_w", 1024), ("conv_b", 3072), ("ssd_norm_w", 2048), ("pool_scale", 1024),
               ("norm_mlp_w", 1024), ("norm_final_w", 1024), ("conv_w", 4 * XBC), ("d_skip", 2048), ("a_log", 128),
               ("dt_bias", 128), ("loss", 128))
SMALL_OFF = {}
_o = 0
for _n, _s in _SMALL_SEGS:
    SMALL_OFF[_n] = _o
    _o += _s
SMALL_LEN = -(-_o // 1024) * 1024

_FIRST = ("w_in", "conv_w")
_LATER = ("w_branch_ssd", "pool_w", "w_branch_pool", "w_out", "w_up", "w_down")
_SMALL_REPLICATED = ("b_ada", "norm_mix_w", "conv_b", "dt_bias", "a_log", "d_skip", "ssd_norm_w", "pool_scale",
                     "norm_mlp_w", "norm_final_w")
_WEIGHTS = ("w_ada", "b_ada", "norm_mix_w", "w_in", "conv_w", "conv_b", "dt_bias", "a_log", "d_skip", "ssd_norm_w",
            "w_branch_ssd", "pool_w", "pool_scale", "w_branch_pool", "w_out", "norm_mlp_w", "w_up", "w_down",
            "norm_final_w")


def _shard_2d(name, a):
    if name == "conv_w":
        return a.reshape(16, -1)
    return (a.reshape(GW, GW) if name == "pool_w" else a.reshape(a.shape[-2], a.shape[-1])).astype(BF16)


def _whole_from_chips(name, g, own, chip):
    g = lax.dynamic_update_slice(g, own[None], (chip, 0, 0))
    if name == "w_in":
        return _perm_cols(jnp.transpose(g, (1, 0, 2)).reshape(D, IN_COLS))
    if name == "w_up":
        return jnp.transpose(g, (1, 0, 2)).reshape(D, DFF)
    if name == "pool_w":
        return jnp.transpose(g.reshape(N_CHIPS, 4, GW // N_CHIPS, GW), (1, 0, 2, 3)).reshape(4, GW, GW)
    if name == "conv_w":
        return jnp.transpose(g.reshape(N_CHIPS, 4, XBC // N_CHIPS), (1, 0, 2)).reshape(4, XBC)
    return g.reshape(N_CHIPS * g.shape[1], g.shape[2])


def _chips_from_whole(name, g):
    if name.startswith("w_in"):
        return jnp.transpose(_unperm_cols(g).reshape(g.shape[0], N_CHIPS, IN_COLS // N_CHIPS), (1, 0, 2))
    if name == "w_up":
        return jnp.transpose(g.reshape(D, N_CHIPS, DFF // N_CHIPS), (1, 0, 2))
    if name == "pool_w":
        return jnp.transpose(g.reshape(4, N_CHIPS, GW // N_CHIPS, GW), (1, 0, 2, 3)).reshape(N_CHIPS, GW, GW)
    return g.reshape(N_CHIPS, g.shape[0] // N_CHIPS, g.shape[1])


def kernel(x, c, w_ada, b_ada, norm_mix_w, w_in, conv_w, conv_b, dt_bias, a_log, d_skip, ssd_norm_w, w_branch_ssd, pool_w, pool_scale, w_branch_pool, w_out, norm_mlp_w, w_up, w_down, norm_final_w, loss_target, m_w_ada, m_b_ada, m_norm_mix_w, m_w_in, m_conv_w, m_conv_b, m_dt_bias, m_a_log, m_d_skip, m_ssd_norm_w, m_w_branch_ssd, m_pool_w, m_pool_scale, m_w_branch_pool, m_w_out, m_norm_mlp_w, m_w_up, m_w_down, m_norm_final_w, v_w_ada, v_b_ada, v_norm_mix_w, v_w_in, v_conv_w, v_conv_b, v_dt_bias, v_a_log, v_d_skip, v_ssd_norm_w, v_w_branch_ssd, v_pool_w, v_pool_scale, v_w_branch_pool, v_w_out, v_norm_mlp_w, v_w_up, v_w_down, v_norm_final_w):
    args = locals()
    w = {n: args[n] for n in _WEIGHTS}
    m = {n: args["m_" + n] for n in _WEIGHTS}
    v = {n: args["v_" + n] for n in _WEIGHTS}
    xi, yi, ci = _me()
    chip = 2 * xi + yi
    idx = jnp.stack([ci, chip]).astype(jnp.int32)
    ada_cols = w_ada.shape[-1]
    xs, target = x[0], loss_target[0]
    two_d = lambda n, a: a.reshape(GW, GW) if n == "pool_w" else a.reshape(-1, a.shape[-1])
    delta, new_m, new_v, g = {}, {}, {}, {}

    def adamw(n, carry=None):
        res = _adamw(two_d(n, w[n]), two_d(n, g[n]), two_d(n, m[n]), two_d(n, v[n]), "adamw_" + n, carry=carry)
        (delta[n], new_m[n], new_v[n]), extra = res if carry is not None else (res, None)
        return extra

    b_mine = lax.dynamic_slice(b_ada, (0, chip * ada_cols), (1, ada_cols))
    shards = {n: _shard_2d(n, w[n]) for n in _FIRST + _LATER}
    mod, c8 = _ada_fwd(c, w_ada[0], b_mine)
    c8 = c8[:, 0, :]
    shift_m, scale_m, gate_m, shift_f, scale_f, gate_f = [mod[:, D * i:D * (i + 1)] for i in range(6)]
    nf_w = norm_final_w.reshape(1, D)

    h1, first = _norm_mod(xs, norm_mix_w, scale_m, shift_m, "norm_mod_mix",
                          carry=_gather_carry([shards[n] for n in _FIRST]))
    p ={n: _whole_from_chips(n, a, shards[n], chip) for n, a in zip(_FIRST, first)}
    (proj,), later = _matmul(h1, p["w_in"], mode="nn", out_dtypes=[F32], name="mm_proj",
                             carry=_gather_carry([shards[n] for n in _LATER]))
    p.update({n: _whole_from_chips(n, a, shards[n], chip) for n, a in zip(_LATER, later)})
    xbc_a = _conv_fwd(proj, p["conv_w"], conv_b)
    dtb_c, alog_c = dt_bias.reshape(HEADS, 1), a_log.reshape(HEADS, 1)
    dsk_exp = jnp.repeat(d_skip, HEAD_DIM, axis=1)
    y, hin = _ssd_fwd(xbc_a, proj, dt_bias, a_log, dtb_c, alog_c, dsk_exp)
    yn = _gate_norm(y, proj, ssd_norm_w)
    (y_ssd,) = _matmul(yn, p["w_branch_ssd"], mode="nn", out_dtypes=[F32], name="mm_branch_ssd")
    pooled, pw_out, yps = _pool_fwd(proj, p["pool_w"], pool_scale)
    (y_pool,) = _matmul(yps, p["w_branch_pool"], mode="nn", out_dtypes=[F32], name="mm_branch_pool")
    merged = _merge(proj, y_ssd, y_pool)
    resid = lambda acc, r, gt: (r + gt * acc, acc)
    x2, mix = _matmul(merged, p["w_out"], mode="nn", out_dtypes=[F32, BF16], name="mm_out",
                      epi=resid, tile_extras=(xs,), row_extras=(gate_m,))
    h2 = _norm_mod(x2, norm_mlp_w, scale_f, shift_f, "norm_mod_mlp")
    relu2 = lambda acc: (acc, jnp.square(jnp.maximum(acc, 0.0)))
    up, act = _matmul(h2, p["w_up"], mode="nn", out_dtypes=[BF16, BF16], name="mm_up", epi=relu2)
    x3, down = _matmul(act, p["w_down"], mode="nn", out_dtypes=[F32, BF16], name="mm_down",
                       epi=resid, tile_extras=(x2,), row_extras=(gate_f,))

    red = _Reducer(idx)
    dx3, d_down, sums_f = _final_loss_bwd(x3, target, nf_w, down, gate_f)
    drelu2 = lambda acc, u: (acc * (2.0 * jnp.maximum(u.astype(F32), 0.0)),)
    (dup,) = _matmul(d_down, p["w_down"], mode="nt", out_dtypes=[BF16], name="mm_dact",
                     epi=drelu2, tile_extras=(up,))
    red.add("w_down", _matmul(act, d_down, mode="tn", out_dtypes=[BF16], name="mm_g_down")[0])
    (dh2,), got = _matmul(dup, p["w_up"], mode="nt", out_dtypes=[F32], name="mm_dh2",
                          carry=red.pair(["w_down"]))
    red.take_pair(["w_down"], got)
    red.add("w_up", _matmul(h2, dup, mode="tn", out_dtypes=[BF16], name="mm_g_up", chip_blocks=True)[0], chip_blocks=True)
    dx2, sums_2, dmix = _norm_mod_bwd(x2, dh2, dx3, norm_mlp_w, scale_f, "norm_mod_mlp_bwd", branch=mix, gate=gate_m)
    (dmerged,), got = _matmul(dmix, p["w_out"], mode="nt", out_dtypes=[F32], name="mm_dmerged",
                              carry=red.pair(["w_up"]))
    red.take_pair(["w_up"], got)
    red.add("w_out", _matmul(merged, dmix, mode="tn", out_dtypes=[BF16], name="mm_g_out")[0])
    dy_ssd, dy_pool, dproj = _merge_bwd(dmerged, proj, y_ssd, y_pool)
    (dyp,), got = _matmul(dy_pool, p["w_branch_pool"], mode="nt", out_dtypes=[F32], name="mm_dyp",
                          carry=red.pair(["w_out"]))
    red.take_pair(["w_out"], got)
    red.add("w_branch_pool", _matmul(yps, dy_pool, mode="tn", out_dtypes=[BF16], name="mm_g_bpool")[0])
    dproj, g_pool_w, sums_pool = _pool_bwd(dyp, pw_out, pooled, p["pool_w"], pool_scale, dproj)
    red.add("pool_w", g_pool_w.astype(BF16))
    red.add("w_branch_ssd", _matmul(yn, dy_ssd, mode="tn", out_dtypes=[BF16], name="mm_g_bssd")[0])
    mixers = ["w_branch_pool", "pool_w", "w_branch_ssd"]
    (dyn,), got = _matmul(dy_ssd, p["w_branch_ssd"], mode="nt", out_dtypes=[F32], name="mm_dyn",
                          carry=red.pair(mixers))
    red.take_pair(mixers, got)
    dy, dproj, sums_gn = _gate_norm_bwd(dyn, y, proj, ssd_norm_w, dproj)
    six = ["w_down", "w_up", "w_out"] + mixers
    (dxa, dproj, dsk_sum, ssd_small), got = _ssd_bwd(dy, xbc_a, proj, hin, dt_bias, a_log, dtb_c, alog_c, dsk_exp,
                                                     dproj, carry=red.chip(six))
    red.take_chip(six, got)
    dxc, sums_conv = _conv_bwd_a(dxa, proj, p["conv_w"], conv_b)
    dproj = _conv_bwd_b(dxc, p["conv_w"], dproj)
    rows_a = 3 * D // 4
    (g_in_a,), got = _matmul(h1, dproj, mode="tn", out_dtypes=[BF16], name="mm_g_in_a", a_cols=(0, rows_a),
                             carry=red.share(six))
    red.take_share(six, got)
    red.add("w_in_a", g_in_a)
    (g_in_b,), got = _matmul(h1, dproj, mode="tn", out_dtypes=[BF16], name="mm_g_in_b", a_cols=(rows_a, D - rows_a),
                             carry=red.pair(["w_in_a"]))
    red.take_pair(["w_in_a"], got)
    red.add("w_in_b", g_in_b)
    (dh1,), got = _matmul(dproj, p["w_in"], mode="nt", out_dtypes=[F32], name="mm_dh1",
                          carry=_join(red.chip(["w_in_a"]), red.pair(["w_in_b"])))
    red.take_chip(["w_in_a"], got[:1])
    red.take_pair(["w_in_b"], got[1:])
    grad_x, sums_1 = _norm_mod_bwd(xs, dh1, dx2, norm_mix_w, scale_m, "norm_mod_mix_bwd")

    dmod = jnp.concatenate([sums_1[0:1], sums_1[1:2], sums_2[3:4], sums_2[0:1], sums_2[1:2], sums_f[1:2]], axis=1)
    pad96 = jnp.zeros((1, 96), F32)
    small = {"dmod": dmod, "norm_mix_w": sums_1[2:3], "conv_b": sums_conv[4:5], "ssd_norm_w": sums_gn[0:1],
             "pool_scale": sums_pool[0:1], "norm_mlp_w": sums_2[2:3], "norm_final_w": sums_f[0:1],
             "conv_w": sums_conv[0:4].reshape(1, 4 * XBC), "d_skip": dsk_sum[0:1],
             "a_log": jnp.concatenate([ssd_small[0:1], pad96], axis=1),
             "dt_bias": jnp.concatenate([ssd_small[1:2], pad96], axis=1), "loss": sums_f[3:4, 0:128]}
    vec = jnp.concatenate([small[n] for n, _ in _SMALL_SEGS], axis=1)
    vec = jnp.pad(vec, ((0, 0), (0, SMALL_LEN - vec.shape[1]))).reshape(SMALL_LEN // 128, 128)
    (every, total, dsk), got = _gather_small(vec, carry=_join(red.chip(["w_in_b"]), red.share(["w_in_a"])))
    red.take_chip(["w_in_b"], got[:1])
    red.take_share(["w_in_a"], got[1:])
    total = total.reshape(1, SMALL_LEN)
    seg = lambda n, size: total[:, SMALL_OFF[n]:SMALL_OFF[n] + size]
    g.update({"b_ada": seg("dmod", 6 * D), "norm_mix_w": seg("norm_mix_w", D), "conv_b": seg("conv_b", XBC),
              "dt_bias": seg("dt_bias", HEADS), "a_log": seg("a_log", HEADS), "d_skip": dsk[:, 0:2].reshape(1, HEADS),
              "ssd_norm_w": seg("ssd_norm_w", DI), "pool_scale": seg("pool_scale", D),
              "norm_mlp_w": seg("norm_mlp_w", D), "norm_final_w": seg("norm_final_w", D)})
    loss = total[0, SMALL_OFF["loss"]]
    conv_cols = conv_w.shape[-1]
    g["conv_w"] = lax.dynamic_slice(seg("conv_w", 4 * XBC).reshape(4, XBC), (0, chip * conv_cols), (4, conv_cols))
    dmod8 = every.reshape(8, SMALL_LEN)[:, SMALL_OFF["dmod"]:SMALL_OFF["dmod"] + 6 * D]
    g["w_ada"] = _w_ada_grad(c8, lax.dynamic_slice(dmod8, (0, chip * ada_cols), (8, ada_cols)))

    got = adamw("w_ada", carry=red.share(["w_in_b"]))
    red.take_share(["w_in_b"], got)
    for n in six:
        g[n] = red.final[n]
    g["w_in"] = jnp.concatenate([red.final["w_in_a"], red.final["w_in_b"]], axis=0)
    for n in ["conv_w", "w_in"] + six:
        adamw(n)
    sizes = [w[n].size for n in _SMALL_REPLICATED]
    n_small = -(-sum(sizes) // 1024) * 1024
    pack = lambda d: jnp.pad(jnp.concatenate([d[n].reshape(1, -1) for n in _SMALL_REPLICATED], axis=1),
                             ((0, 0), (0, n_small - sum(sizes)))).reshape(n_small // 128, 128)
    d_, m_, v_ = _adamw(pack(w), pack(g), pack(m), pack(v), "adamw_small")
    off = 0
    for n, s in zip(_SMALL_REPLICATED, sizes):
        for dst, src in ((delta, d_), (new_m, m_), (new_v, v_)):
            dst[n] = src.reshape(1, n_small)[:, off:off + s]
        off += s

    out = [loss, grad_x.reshape(x.shape)]
    for d in (g, delta, new_m, new_v):
        out += [d[n].reshape(w[n].shape) for n in _WEIGHTS]
    return tuple(out)
```

```python
import functools
import operator

import jax
import jax.numpy as jnp
import numpy as np
from jax import lax
from jax.experimental import pallas as pl
from jax.experimental.pallas import tpu as pltpu

F32, BF16 = jnp.float32, jnp.bfloat16
HIGH = lax.Precision.HIGHEST
MESH = pl.DeviceIdType.MESH

D = 1024
DI = 2048
HEADS, HEAD_DIM = 32, 64
GROUPS, STATE = 4, 128
Q = 128
XBC = DI + 2 * GROUPS * STATE
POOL_WINDOWS = (2, 4, 8, 16)
GW = 256
DFF = 4096
EPS = 1e-5
IN_COLS = 8224
OFF_Z, OFF_XBC, OFF_POOL, OFF_GATE, OFF_DT, NP = 0, 2048, 5120, 6144, 8192, 8448
N_CHIPS = 4
ADAM_LR, ADAM_B1, ADAM_B2, ADAM_EPS, ADAM_WD, ADAM_STEP = 0.001, 0.9, 0.999, 1e-08, 0.01, 10
VMEM_LIMIT = 56 * 2 ** 20
NEG = -1e30


def _sigmoid(v):
    return 0.5 * jnp.tanh(0.5 * v) + 0.5


def _softplus(v):
    return jnp.maximum(v, 0.0) + jnp.log1p(jnp.exp(-jnp.abs(v)))


def _dot(a, b, dims, **kw):
    return lax.dot_general(a, b, (dims, ((), ())), preferred_element_type=F32, **kw)


def _nn(a, b, **kw):
    return _dot(a, b, ((1,), (0,)), **kw)


def _nt(a, b, **kw):
    return _dot(a, b, ((1,), (1,)), **kw)


def _tn(a, b, **kw):
    return _dot(a, b, ((0,), (0,)), **kw)


_DT_IN_CHIP2 = 5120 - 2 * (IN_COLS // 4)


class _Sems:
    def __init__(self, send, recv, local, base=0):
        self._send, self._recv, self._local, self._base = send, recv, local, base

    def shift(self, n):
        return _Sems(self._send, self._recv, self._local, self._base + n)

    def send(self, i):
        return self._send.at[self._base + i]

    def recv(self, i):
        return self._recv.at[self._base + i]

    def local(self, i):
        return self._local.at[self._base + i]


class _Carry:
    def __init__(self, ins, out_shapes, n_sems, start, finish, aliased=()):
        self.ins, self.out_shapes, self.n_sems, self.start, self.finish = list(ins), list(out_shapes), n_sems, start, finish
        self.aliased = list(aliased)


def _join(*carries):
    def run(which):
        def fn(ins, outs, sems):
            i = o = s = 0
            for cy in carries:
                getattr(cy, which)(ins[i:i + len(cy.ins)], outs[o:o + len(cy.out_shapes)], sems.shift(s))
                i, o, s = i + len(cy.ins), o + len(cy.out_shapes), s + cy.n_sems
        return fn

    aliased, i, o = [], 0, 0
    for cy in carries:
        aliased += [(i + a, o + b) for a, b in cy.aliased]
        i, o = i + len(cy.ins), o + len(cy.out_shapes)
    return _Carry([a for cy in carries for a in cy.ins], [a for cy in carries for a in cy.out_shapes],
                  sum(cy.n_sems for cy in carries), run("start"), run("finish"), aliased)


def _call(body, args, *, name, grid=(), in_specs, out_specs, out_shape, scratch_shapes=(), sem=None, aliases=None,
          carry=None):
    in_specs, out_specs, out_shape, scratch_shapes = list(in_specs), list(out_specs), list(out_shape), list(scratch_shapes)
    n_in, n_out, n_scr = len(in_specs), len(out_specs), len(scratch_shapes)
    kw = {"vmem_limit_bytes": VMEM_LIMIT}
    if carry is None:
        kernel_fn = functools.partial(body)
        if sem is not None:
            kw["dimension_semantics"] = sem
    else:
        n_ci, n_co = len(carry.ins), len(carry.out_shapes)
        hbm = pl.BlockSpec(memory_space=pl.ANY)
        in_specs += [hbm] * n_ci
        out_specs += [hbm] * n_co
        out_shape += carry.out_shapes
        n_s = max(carry.n_sems, 1)
        scratch_shapes += [pltpu.SemaphoreType.DMA((n_s,))] * 3
        args = list(args) + carry.ins
        aliases = dict(aliases or {})
        aliases.update({n_in + i: n_out + o for i, o in carry.aliased})
        if grid:
            kw["dimension_semantics"] = ("arbitrary",) * len(grid)

        def kernel_fn(*refs):
            a = n_in
            ins, c_ins = refs[:a], refs[a:a + n_ci]
            a += n_ci
            outs, c_outs = refs[a:a + n_out], refs[a + n_out:a + n_out + n_co]
            a += n_out + n_co
            scr, sems = refs[a:a + n_scr], _Sems(*refs[a + n_scr:a + n_scr + 3])
            if grid:
                ids = [pl.program_id(d) for d in range(len(grid))]
                first = functools.reduce(operator.and_, [i == 0 for i in ids])
                last = functools.reduce(operator.and_, [i == g - 1 for i, g in zip(ids, grid)])

                @pl.when(first)
                def _():
                    carry.start(c_ins, c_outs, sems)

                body(*ins, *outs, *scr)

                @pl.when(last)
                def _():
                    carry.finish(c_ins, c_outs, sems)
            else:
                carry.start(c_ins, c_outs, sems)
                body(*ins, *outs, *scr)
                carry.finish(c_ins, c_outs, sems)

    outs = pl.pallas_call(
        kernel_fn, name=name, grid=grid, in_specs=in_specs, out_specs=out_specs, out_shape=out_shape,
        scratch_shapes=scratch_shapes, input_output_aliases=aliases or {},
        compiler_params=pltpu.CompilerParams(**kw),
    )(*args)
    outs = list(outs)
    return outs if carry is None else (outs[:n_out], outs[n_out:])


def _run_carry(carry, name):
    _, outs = _call(lambda: None, [], name=name, in_specs=[], out_specs=[], out_shape=[], carry=carry)
    return outs


_TILES = {
    "mm_proj": (1024, 2816, 1024), "mm_branch_ssd": (1024, 1024, 2048), "mm_branch_pool": (1024, 1024, 1024),
    "mm_out": (1024, 1024, 1024), "mm_up": (1024, 1024, 1024), "mm_down": (512, 1024, 4096),
    "mm_dact": (1024, 1024, 1024), "mm_g_down": (1024, 1024, 2048), "mm_dh2": (1024, 1024, 4096),
    "mm_g_up": (1024, 1024, 2048), "mm_dmerged": (1024, 1024, 1024), "mm_g_out": (1024, 1024, 2048),
    "mm_dyp": (1024, 1024, 1024), "mm_g_bpool": (1024, 1024, 2048), "mm_g_bssd": (1024, 1024, 2048),
    "mm_dyn": (1024, 1024, 1024), "mm_g_in_a": (768, 1408, 2048), "mm_g_in_b": (256, 2816, 2048),
    "mm_dh1": (1024, 1024, 2816),
}


def _matmul(a, b, *, mode, out_dtypes, name, epi=None, tile_extras=(), row_extras=(), carry=None, a_cols=None,
            chip_blocks=False):
    M, K = (a.shape[1], a.shape[0]) if mode == "tn" else a.shape
    N = b.shape[0] if mode == "nt" else b.shape[1]
    a_start, M = a_cols if a_cols is not None else (0, M)
    tm, tn, tk = _TILES[name]
    tm, tn, tk = min(tm, M), min(tn, N), min(tk, K)
    assert M % tm == 0 and N % tn == 0 and K % tk == 0 and a_start % tm == 0, (name, M, N, K, tm, tn, tk)
    a_off = a_start // tm
    if mode == "nn":
        a_spec = pl.BlockSpec((tm, tk), lambda i, j, k: (i, k))
        b_spec = pl.BlockSpec((tk, tn), lambda i, j, k: (k, j))
        dims = ((1,), (0,))
    elif mode == "nt":
        a_spec = pl.BlockSpec((tm, tk), lambda i, j, k: (i, k))
        b_spec = pl.BlockSpec((tn, tk), lambda i, j, k: (j, k))
        dims = ((1,), (1,))
    else:
        a_spec = pl.BlockSpec((tk, tm), lambda i, j, k: (k, i + a_off))
        b_spec = pl.BlockSpec((tk, tn), lambda i, j, k: (k, j))
        dims = ((0,), (0,))
    nk = K // tk
    n_te, n_re, n_out = len(tile_extras), len(row_extras), len(out_dtypes)
    if epi is None:
        epi = lambda acc: (acc,)

    def body(a_ref, b_ref, *rest):
        extras = rest[:n_te + n_re]
        outs = rest[n_te + n_re:n_te + n_re + n_out]
        p = _dot(a_ref[...], b_ref[...], dims)

        def finish(acc):
            vals = epi(acc, *[e[...] for e in extras])
            for o, v in zip(outs, vals):
                o[...] = v.astype(o.dtype)

        if nk == 1:
            finish(p)
        else:
            acc_ref = rest[-1]
            k = pl.program_id(2)

            @pl.when(k == 0)
            def _():
                acc_ref[...] = p

            @pl.when(k > 0)
            def _():
                acc_ref[...] += p

            @pl.when(k == nk - 1)
            def _():
                finish(acc_ref[...])

    tile_spec = pl.BlockSpec((tm, tn), lambda i, j, k: (i, j))
    row_spec = pl.BlockSpec((1, tn), lambda i, j, k: (0, j))
    out_spec, out_dims = tile_spec, (M, N)
    if chip_blocks:
        assert n_te == 0 and tn * N_CHIPS == N
        out_spec, out_dims = pl.BlockSpec((None, tm, tn), lambda i, j, k: (j, i, 0)), (N_CHIPS, M, tn)
    return _call(
        body, [a, b, *tile_extras, *row_extras], name=name, grid=(M // tm, N // tn, nk),
        in_specs=[a_spec, b_spec] + [tile_spec] * n_te + [row_spec] * n_re, out_specs=[out_spec] * n_out,
        out_shape=[jax.ShapeDtypeStruct(out_dims, dt) for dt in out_dtypes],
        scratch_shapes=[pltpu.VMEM((tm, tn), F32)] if nk > 1 else [],
        sem=("parallel", "parallel", "arbitrary"), carry=carry)


def _row_tile(T):
    return min(512, T)


def _norm_mod(x, nw, scale, shift, name, carry=None):
    T = x.shape[0]
    tr = _row_tile(T)

    def body(x_ref, nw_ref, sc_ref, sh_ref, o_ref):
        xv = x_ref[...]
        r = lax.rsqrt(jnp.mean(xv * xv, axis=-1, keepdims=True) + EPS)
        o_ref[...] = ((xv * r) * nw_ref[...] * (1.0 + sc_ref[...]) + sh_ref[...]).astype(BF16)

    tile = pl.BlockSpec((tr, D), lambda i: (i, 0))
    row = pl.BlockSpec((1, D), lambda i: (0, 0))
    res = _call(body, [x, nw, scale, shift], name=name, grid=(T // tr,), in_specs=[tile, row, row, row],
                out_specs=[tile], out_shape=[jax.ShapeDtypeStruct((T, D), BF16)], sem=("parallel",), carry=carry)
    return res[0] if carry is None else (res[0][0], res[1])


def _norm_mod_bwd(x, dh, dres, nw, scale, name, branch=None, gate=None, carry=None):
    T = x.shape[0]
    tr = _row_tile(T)
    with_branch = branch is not None

    def body(x_ref, dh_ref, dr_ref, nw_ref, sc_ref, *rest):
        if with_branch:
            br_ref, g_ref, dx_ref, sums_ref, db_ref = rest
        else:
            dx_ref, sums_ref = rest
        i = pl.program_id(0)

        @pl.when(i == 0)
        def _():
            sums_ref[...] = jnp.zeros_like(sums_ref)

        xv, dhv = x_ref[...], dh_ref[...]
        r = lax.rsqrt(jnp.mean(xv * xv, axis=-1, keepdims=True) + EPS)
        xn = xv * r
        g1 = dhv * (1.0 + sc_ref[...])
        dxn = g1 * nw_ref[...]
        dx = dr_ref[...] + r * (dxn - xn * jnp.mean(dxn * xn, axis=-1, keepdims=True))
        dx_ref[...] = dx
        sums_ref[0:1, :] += jnp.sum(dhv, axis=0, keepdims=True)
        sums_ref[1:2, :] += jnp.sum(dhv * (xn * nw_ref[...]), axis=0, keepdims=True)
        sums_ref[2:3, :] += jnp.sum(g1 * xn, axis=0, keepdims=True)
        if with_branch:
            db_ref[...] = (dx * g_ref[...]).astype(BF16)
            sums_ref[3:4, :] += jnp.sum(dx * br_ref[...], axis=0, keepdims=True)

    tile = pl.BlockSpec((tr, D), lambda i: (i, 0))
    row = pl.BlockSpec((1, D), lambda i: (0, 0))
    sums = pl.BlockSpec((8, D), lambda i: (0, 0))
    ins = [x, dh, dres, nw, scale] + ([branch, gate] if with_branch else [])
    in_specs = [tile, tile, tile, row, row] + ([tile, row] if with_branch else [])
    out_specs = [tile, sums] + ([tile] if with_branch else [])
    out_shape = [jax.ShapeDtypeStruct((T, D), F32), jax.ShapeDtypeStruct((8, D), F32)]
    if with_branch:
        out_shape.append(jax.ShapeDtypeStruct((T, D), BF16))
    return _call(body, ins, name=name, grid=(T // tr,), in_specs=in_specs, out_specs=out_specs, out_shape=out_shape,
                 sem=("arbitrary",), carry=carry)


def _final_loss_bwd(x3, target, wf, down, gate_f):
    T = x3.shape[0]
    tr = _row_tile(T)
    n_steps = T // tr

    def body(x_ref, t_ref, w_ref, dn_ref, g_ref, dx_ref, dd_ref, sums_ref):
        i = pl.program_id(0)

        @pl.when(i == 0)
        def _():
            sums_ref[...] = jnp.zeros_like(sums_ref)

        xv = x_ref[...]
        r = lax.rsqrt(jnp.mean(xv * xv, axis=-1, keepdims=True) + EPS)
        xn = xv * r
        err = xn * w_ref[...] - t_ref[...]
        dy = err * (1.0 / D)
        dxn = dy * w_ref[...]
        dx = r * (dxn - xn * jnp.mean(dxn * xn, axis=-1, keepdims=True))
        dx_ref[...] = dx
        dd_ref[...] = (dx * g_ref[...]).astype(BF16)
        sums_ref[0:1, :] += jnp.sum(dy * xn, axis=0, keepdims=True)
        sums_ref[1:2, :] += jnp.sum(dx * dn_ref[...], axis=0, keepdims=True)
        sums_ref[2:3, :] += jnp.sum(err * err, axis=0, keepdims=True) * (0.5 / D)

        @pl.when(i == n_steps - 1)
        def _():
            sums_ref[3:4, :] = jnp.broadcast_to(jnp.sum(sums_ref[2:3, :], axis=1, keepdims=True), (1, D))

    tile = pl.BlockSpec((tr, D), lambda i: (i, 0))
    row = pl.BlockSpec((1, D), lambda i: (0, 0))
    sums = pl.BlockSpec((8, D), lambda i: (0, 0))
    return _call(body, [x3, target, wf, down, gate_f], name="final_loss_bwd", grid=(n_steps,),
                 in_specs=[tile, tile, row, tile, row], out_specs=[tile, tile, sums],
                 out_shape=[jax.ShapeDtypeStruct((T, D), F32), jax.ShapeDtypeStruct((T, D), BF16),
                            jax.ShapeDtypeStruct((8, D), F32)], sem=("arbitrary",))


CONV_TC = 1024


def _conv_taps(xp, w, b):
    acc = b + w[3:4, :] * xp
    for k in range(3):
        acc = acc + w[k:k + 1, :] * pltpu.roll(xp, 3 - k, 0)
    return acc


def _conv_fwd(proj, conv_w, conv_b):
    T = proj.shape[0]
    tr = _row_tile(T)
    nb, offb = tr // 8, OFF_XBC // CONV_TC

    def body(x_ref, h_ref, w_ref, b_ref, o_ref):
        halo = jnp.where(pl.program_id(0) > 0, h_ref[...], 0.0)
        xp = jnp.concatenate([halo, x_ref[...]], axis=0)
        acc = _conv_taps(xp, w_ref[...], b_ref[...])[8:]
        o_ref[...] = acc * _sigmoid(acc)

    return _call(
        body, [proj, proj, conv_w, conv_b], name="conv_fwd", grid=(T // tr, XBC // CONV_TC),
        in_specs=[pl.BlockSpec((tr, CONV_TC), lambda i, j: (i, j + offb)),
                  pl.BlockSpec((8, CONV_TC), lambda i, j: (jnp.maximum(i * nb - 1, 0), j + offb)),
                  pl.BlockSpec((4, CONV_TC), lambda i, j: (0, j)),
                  pl.BlockSpec((1, CONV_TC), lambda i, j: (0, j))],
        out_specs=[pl.BlockSpec((tr, CONV_TC), lambda i, j: (i, j))],
        out_shape=[jax.ShapeDtypeStruct((T, XBC), F32)], sem=("parallel", "parallel"))[0]


def _conv_bwd_a(dxa, proj, conv_w, conv_b):
    T = proj.shape[0]
    tr = _row_tile(T)
    nb, offb = tr // 8, OFF_XBC // CONV_TC

    def body(d_ref, x_ref, h_ref, w_ref, b_ref, o_ref, sums_ref):
        i = pl.program_id(1)

        @pl.when(i == 0)
        def _():
            sums_ref[...] = jnp.zeros_like(sums_ref)

        halo = jnp.where(i > 0, h_ref[...], 0.0)
        xp = jnp.concatenate([halo, x_ref[...]], axis=0)
        w = w_ref[...]
        taps = [pltpu.roll(xp, 3 - k, 0)[8:] for k in range(3)] + [x_ref[...]]
        acc = b_ref[...] + w[3:4, :] * taps[3]
        for k in range(3):
            acc = acc + w[k:k + 1, :] * taps[k]
        s = _sigmoid(acc)
        dxc = d_ref[...] * (s * (1.0 + acc * (1.0 - s)))
        o_ref[...] = dxc
        for k in range(4):
            sums_ref[k:k + 1, :] += jnp.sum(dxc * taps[k], axis=0, keepdims=True)
        sums_ref[4:5, :] += jnp.sum(dxc, axis=0, keepdims=True)

    return _call(
        body, [dxa, proj, proj, conv_w, conv_b], name="conv_bwd_a", grid=(XBC // CONV_TC, T // tr),
        in_specs=[pl.BlockSpec((tr, CONV_TC), lambda j, i: (i, j)),
                  pl.BlockSpec((tr, CONV_TC), lambda j, i: (i, j + offb)),
                  pl.BlockSpec((8, CONV_TC), lambda j, i: (jnp.maximum(i * nb - 1, 0), j + offb)),
                  pl.BlockSpec((4, CONV_TC), lambda j, i: (0, j)),
                  pl.BlockSpec((1, CONV_TC), lambda j, i: (0, j))],
        out_specs=[pl.BlockSpec((tr, CONV_TC), lambda j, i: (i, j)), pl.BlockSpec((8, CONV_TC), lambda j, i: (0, j))],
        out_shape=[jax.ShapeDtypeStruct((T, XBC), F32), jax.ShapeDtypeStruct((8, XBC), F32)],
        sem=("parallel", "arbitrary"))


def _conv_bwd_b(dxc, conv_w, dproj):
    T = dxc.shape[0]
    tr = _row_tile(T)
    nb, offb, last = tr // 8, OFF_XBC // CONV_TC, T // tr - 1

    def body(d_ref, h_ref, w_ref, dp_in, o_ref):
        del dp_in
        halo = jnp.where(pl.program_id(0) < last, h_ref[...], 0.0)
        xp = jnp.concatenate([d_ref[...], halo], axis=0)
        n = xp.shape[0]
        w = w_ref[...]
        acc = w[3:4, :] * xp
        for k in range(3):
            acc = acc + w[k:k + 1, :] * pltpu.roll(xp, n - (3 - k), 0)
        o_ref[...] = acc[:tr].astype(BF16)

    return _call(
        body, [dxc, dxc, conv_w, dproj], name="conv_bwd_b", grid=(T // tr, XBC // CONV_TC),
        in_specs=[pl.BlockSpec((tr, CONV_TC), lambda i, j: (i, j)),
                  pl.BlockSpec((8, CONV_TC), lambda i, j: (jnp.minimum((i + 1) * nb, T // 8 - 1), j)),
                  pl.BlockSpec((4, CONV_TC), lambda i, j: (0, j)),
                  pl.BlockSpec(memory_space=pl.ANY)],
        out_specs=[pl.BlockSpec((tr, CONV_TC), lambda i, j: (i, j + offb))],
        out_shape=[jax.ShapeDtypeStruct(dproj.shape, BF16)], aliases={3: 0}, sem=("parallel", "parallel"))[0]


def _spread(v, sel, pieces):
    out = None
    for _ in range(pieces):
        p = v.astype(BF16)
        term = _nn(p, sel)
        out = term if out is None else out + term
        v = v - p.astype(F32)
    return out


def _ssd_selectors():
    g = np.arange(GROUPS)[:, None, None]
    piece = np.arange(128)[None, :, None]
    h = np.where(piece < 3 * HEADS, piece % HEADS, -1)
    blocks = (h == 8 * g + np.arange(1024)[None, None, :] // 128)
    pairs = (h == 8 * g + np.arange(512)[None, None, :] // HEAD_DIM)
    lane = np.arange(128)[None, None, :]
    block_sum = (lane == 8 * g + np.arange(1024)[None, :, None] // 128)
    pair_sum = (lane == 8 * g + np.arange(512)[None, :, None] // HEAD_DIM)
    return [jnp.asarray(m, BF16) for m in (blocks, pairs, block_sum, pair_sum)]


def _pack3(v):
    p0 = v.astype(BF16)
    r1 = v - p0.astype(F32)
    p1 = r1.astype(BF16)
    r2 = r1 - p1.astype(F32)
    return p0 + pltpu.roll(r1, HEADS, 1).astype(BF16) + pltpu.roll(r2, 2 * HEADS, 1).astype(BF16)


def _ssd_group(g, cs_p, csT, dt_p, s_mat, causal_w, lo, blocks_ref, pairs_ref):
    csb = _nn(cs_p, blocks_ref[g])
    row = jnp.concatenate([csT[8 * g + hh:8 * g + hh + 1, :] for hh in range(8)], axis=1)
    l_w = jnp.exp(jnp.where(causal_w, csb - row, NEG))
    m_w = jnp.concatenate([s_mat] * 8, axis=1) * l_w
    cs_g = jnp.concatenate([jnp.where(lo, csb[:, 256 * jj:256 * jj + 128], csb[:, 256 * jj + 128:256 * jj + 256])
                            for jj in range(4)], axis=1)
    cs_last = cs_g[Q - 1:Q, :]
    return m_w, l_w, _nn(dt_p, pairs_ref[g]), jnp.exp(cs_g), jnp.exp(cs_last - cs_g), jnp.exp(cs_last)


def _ssd_common(dtp_ref, dtb_r, alog_r, dtb_c, alog_c):
    rows = lax.broadcasted_iota(jnp.int32, (Q, Q), 0)
    cols = lax.broadcasted_iota(jnp.int32, (Q, Q), 1)
    tri = (cols <= rows).astype(F32)
    heads = lax.broadcasted_iota(jnp.int32, (1, 128), 1) < HEADS
    raw_w = dtp_ref[...] + dtb_r[...]
    dt_w = jnp.where(heads, _softplus(raw_w), 0.0)
    a_w = -jnp.exp(alog_r[...])
    cs_w = _nn(tri, dt_w * a_w, precision=HIGH)
    aT = _softplus(dtp_ref[...].T[0:HEADS, :] + dtb_c[...]) * (-jnp.exp(alog_c[...]))
    csT = _nt(aT, tri, precision=HIGH)
    return raw_w[:, 0:HEADS], dt_w[:, 0:HEADS], a_w[:, 0:HEADS], csT, _pack3(cs_w), _pack3(dt_w)


def _ssd_fwd(xbc_a, proj, dtb_r, alog_r, dtb_c, alog_c, dsk_exp):
    T = xbc_a.shape[0]
    nc = T // Q
    dtb_r, alog_r = [jnp.pad(a, ((0, 0), (0, 128 - HEADS))) for a in (dtb_r, alog_r)]

    def body(xbc_ref, dtp_ref, dtb_r_ref, alog_r_ref, dtb_c_ref, alog_c_ref, dsk_ref, blocks_ref, pairs_ref,
             y_ref, hin_ref, h_scr):
        @pl.when(pl.program_id(0) == 0)
        def _():
            h_scr[...] = jnp.zeros_like(h_scr)

        _, _, _, csT, cs_p, dt_p = _ssd_common(dtp_ref, dtb_r_ref, alog_r_ref, dtb_c_ref, alog_c_ref)
        lo = lax.broadcasted_iota(jnp.int32, (1, 128), 1) < HEAD_DIM
        hi = jnp.logical_not(lo)
        causal_w = (lax.broadcasted_iota(jnp.int32, (Q, 1024), 1) & (Q - 1)) <= lax.broadcasted_iota(jnp.int32, (Q, 1024), 0)
        for g in range(GROUPS):
            gs = slice(512 * g, 512 * (g + 1))
            hs = slice(128 * g, 128 * (g + 1))
            xs_g = xbc_ref[:, gs]
            b_g = xbc_ref[:, DI + STATE * g:DI + STATE * (g + 1)].astype(BF16)
            c_g = xbc_ref[:, DI + 512 + STATE * g:DI + 512 + STATE * (g + 1)].astype(BF16)
            m_w, _, dt_g, ecs_g, dec_g, cd_g = _ssd_group(g, cs_p, csT, dt_p, _nt(c_g, b_g), causal_w, lo, blocks_ref, pairs_ref)
            m_b = m_w.astype(BF16)
            xdt = xs_g * dt_g
            xdt_b = xdt.astype(BF16)
            ys = []
            for jj in range(4):
                xp = xdt_b[:, 128 * jj:128 * (jj + 1)]
                x_ab = jnp.concatenate([jnp.where(lo, xp, jnp.zeros_like(xp)), jnp.where(hi, xp, jnp.zeros_like(xp))], axis=0)
                ys.append(_nn(m_b[:, 256 * jj:256 * (jj + 1)], x_ab))
            h_g = h_scr[hs, :]
            hin_ref[0, hs, :] = h_g
            y_ref[:, gs] = jnp.concatenate(ys, axis=1) + _nn(c_g, h_g.astype(BF16)) * ecs_g + dsk_ref[:, gs] * xs_g
            h_scr[hs, :] = h_g * cd_g + _tn(b_g, (xdt * dec_g).astype(BF16))

    small_r = pl.BlockSpec((1, 128), lambda c: (0, 0))
    small_c = pl.BlockSpec((HEADS, 1), lambda c: (0, 0))
    blocks, pairs, _, _ = _ssd_selectors()
    whole = lambda a: pl.BlockSpec(a.shape, lambda c: (0,) * a.ndim)
    return _call(
        body, [xbc_a, proj, dtb_r, alog_r, dtb_c, alog_c, dsk_exp, blocks, pairs], name="ssd_fwd", grid=(nc,),
        in_specs=[pl.BlockSpec((Q, XBC), lambda c: (c, 0)),
                  pl.BlockSpec((Q, 128), lambda c: (c, OFF_DT // 128)),
                  small_r, small_r, small_c, small_c,
                  pl.BlockSpec((1, DI), lambda c: (0, 0)), whole(blocks), whole(pairs)],
        out_specs=[pl.BlockSpec((Q, DI), lambda c: (c, 0)), pl.BlockSpec((1, 512, 512), lambda c: (c, 0, 0))],
        out_shape=[jax.ShapeDtypeStruct((T, DI), F32), jax.ShapeDtypeStruct((nc, 512, 512), F32)],
        scratch_shapes=[pltpu.VMEM((512, 512), F32)], sem=("arbitrary",))


def _ssd_bwd(dy, xbc_a, proj, hin, dtb_r, alog_r, dtb_c, alog_c, dsk_exp, dproj, carry=None):
    T = xbc_a.shape[0]
    nc = T // Q
    dtb_r, alog_r = [jnp.pad(a, ((0, 0), (0, 128 - HEADS))) for a in (dtb_r, alog_r)]

    def body(dy_ref, xbc_ref, dtp_ref, hin_ref, dtb_r_ref, alog_r_ref, dtb_c_ref, alog_c_ref, dsk_ref, dp_in,
             blocks_ref, pairs_ref, block_sum_ref, pair_sum_ref, dxa_ref, dp_ref, dsk_sum_ref, small_ref, dh_scr):
        del dp_in

        @pl.when(pl.program_id(0) == 0)
        def _():
            dh_scr[...] = jnp.zeros_like(dh_scr)
            dsk_sum_ref[...] = jnp.zeros_like(dsk_sum_ref)
            small_ref[...] = jnp.zeros_like(small_ref)

        raw, dt, a_r, csT, cs_p, dt_p = _ssd_common(dtp_ref, dtb_r_ref, alog_r_ref, dtb_c_ref, alog_c_ref)
        lo = lax.broadcasted_iota(jnp.int32, (1, 128), 1) < HEAD_DIM
        hi = jnp.logical_not(lo)
        sub32 = lax.broadcasted_iota(jnp.int32, (HEADS, 1), 0)
        causal_w = (lax.broadcasted_iota(jnp.int32, (Q, 1024), 1) & (Q - 1)) <= lax.broadcasted_iota(jnp.int32, (Q, 1024), 0)
        dcs_c = jnp.zeros((Q, 128), F32)
        dcs_r = jnp.zeros((HEADS, Q), F32)
        dcs_l = jnp.zeros((8, 128), F32)
        ddt_x = jnp.zeros((Q, 128), F32)
        for g in range(GROUPS):
            gs = slice(512 * g, 512 * (g + 1))
            hs = slice(128 * g, 128 * (g + 1))
            xs_g, dy_g = xbc_ref[:, gs], dy_ref[:, gs]
            b_g = xbc_ref[:, DI + STATE * g:DI + STATE * (g + 1)].astype(BF16)
            c_g = xbc_ref[:, DI + 512 + STATE * g:DI + 512 + STATE * (g + 1)].astype(BF16)
            m_w, l_w, dt_g, ecs_g, dec_g, cd_g = _ssd_group(g, cs_p, csT, dt_p, _nt(c_g, b_g), causal_w, lo, blocks_ref, pairs_ref)
            m_b = m_w.astype(BF16)
            xdt = xs_g * dt_g
            xdt_b, dy_b = xdt.astype(BF16), dy_g.astype(BF16)
            dms, dxs = [], []
            for jj in range(4):
                xp, dyp = xdt_b[:, 128 * jj:128 * (jj + 1)], dy_b[:, 128 * jj:128 * (jj + 1)]
                dy_ab = jnp.concatenate([jnp.where(lo, dyp, jnp.zeros_like(dyp)), jnp.where(hi, dyp, jnp.zeros_like(dyp))], axis=0)
                dm_ab = _nt(dy_ab, xp)
                dms += [dm_ab[:Q], dm_ab[Q:]]
                dx_ab = _tn(m_b[:, 256 * jj:256 * (jj + 1)], dyp)
                dxs.append(jnp.where(lo, dx_ab[:Q], dx_ab[Q:]))
            dm_w = jnp.concatenate(dms, axis=1)
            w_w = dm_w * m_w
            dcs_c = dcs_c + _spread(w_w, block_sum_ref[g], 2)
            w_cols = jnp.sum(w_w, axis=0, keepdims=True)
            for hh in range(8):
                dcs_r = dcs_r + jnp.where(sub32 == 8 * g + hh, w_cols[:, 128 * hh:128 * (hh + 1)], 0.0)
            dl_w = dm_w * l_w
            ds_mat = dl_w[:, 0:128]
            for hh in range(1, 8):
                ds_mat = ds_mat + dl_w[:, 128 * hh:128 * (hh + 1)]
            hin_g = hin_ref[0, hs, :]
            hin_b = hin_g.astype(BF16)
            dh_g = dh_scr[hs, :]
            dh_b = dh_g.astype(BF16)
            g_mat = _nn(b_g, dh_b)
            xdec = xdt * dec_g
            xg = xdec * g_mat
            dxdt = jnp.concatenate(dxs, axis=1) + dec_g * g_mat
            sums = _spread(jnp.concatenate([dy_g * (_nn(c_g, hin_b) * ecs_g) - xg, dxdt * xs_g], axis=0), pair_sum_ref[g], 2)
            dcs_c = dcs_c + sums[:Q]
            ddt_x = ddt_x + sums[Q:]
            last = jnp.sum(xg, axis=0, keepdims=True) + jnp.sum(dh_g * hin_g, axis=0, keepdims=True) * cd_g
            dcs_l = dcs_l + _spread(jnp.broadcast_to(last, (8, 512)), pair_sum_ref[g], 2)
            dz = (dy_g * ecs_g).astype(BF16)
            ds_b = ds_mat.astype(BF16)
            dxa_ref[:, gs] = dxdt * dt_g + dy_g * dsk_ref[:, gs]
            dxa_ref[:, DI + STATE * g:DI + STATE * (g + 1)] = _nt(xdec.astype(BF16), dh_b) + _tn(ds_b, c_g)
            dxa_ref[:, DI + 512 + STATE * g:DI + 512 + STATE * (g + 1)] = _nt(dz, hin_b) + _nn(ds_b, b_g)
            dh_scr[hs, :] = _tn(c_g, dz) + dh_g * cd_g
            dsk_sum_ref[0:1, gs] += jnp.sum(dy_g * xs_g, axis=0, keepdims=True)

        rows = lax.broadcasted_iota(jnp.int32, (Q, Q), 0)
        cols = lax.broadcasted_iota(jnp.int32, (Q, Q), 1)
        tri_t = (cols >= rows).astype(F32)
        last_row = lax.broadcasted_iota(jnp.int32, (Q, 1), 0) == Q - 1
        dcs = (dcs_c + jnp.where(last_row, dcs_l[0:1, :], 0.0))[:, 0:HEADS]
        da = _nn(tri_t, dcs, precision=HIGH) - _nt(tri_t, dcs_r, precision=HIGH)
        ddt_raw = (ddt_x[:, 0:HEADS] + da * a_r) * _sigmoid(raw)
        small_ref[0:1, :] += jnp.sum(da * dt, axis=0, keepdims=True) * a_r
        small_ref[1:2, :] += jnp.sum(ddt_raw, axis=0, keepdims=True)
        dp_ref[...] = jnp.zeros_like(dp_ref)
        dp_ref[:, 0:HEADS] = ddt_raw.astype(BF16)

    rev = lambda c: nc - 1 - c
    small_r = pl.BlockSpec((1, 128), lambda c: (0, 0))
    small_c = pl.BlockSpec((HEADS, 1), lambda c: (0, 0))
    selectors = _ssd_selectors()
    whole = lambda a: pl.BlockSpec(a.shape, lambda c: (0,) * a.ndim)
    return _call(
        body, [dy, xbc_a, proj, hin, dtb_r, alog_r, dtb_c, alog_c, dsk_exp, dproj, *selectors], name="ssd_bwd", grid=(nc,),
        in_specs=[pl.BlockSpec((Q, DI), lambda c: (rev(c), 0)),
                  pl.BlockSpec((Q, XBC), lambda c: (rev(c), 0)),
                  pl.BlockSpec((Q, 128), lambda c: (rev(c), OFF_DT // 128)),
                  pl.BlockSpec((1, 512, 512), lambda c: (rev(c), 0, 0)),
                  small_r, small_r, small_c, small_c,
                  pl.BlockSpec((1, DI), lambda c: (0, 0)),
                  pl.BlockSpec(memory_space=pl.ANY)] + [whole(a) for a in selectors],
        out_specs=[pl.BlockSpec((Q, XBC), lambda c: (rev(c), 0)),
                   pl.BlockSpec((Q, 256), lambda c: (rev(c), OFF_DT // 256)),
                   pl.BlockSpec((8, DI), lambda c: (0, 0)),
                   pl.BlockSpec((8, HEADS), lambda c: (0, 0))],
        out_shape=[jax.ShapeDtypeStruct((T, XBC), F32), jax.ShapeDtypeStruct(dproj.shape, BF16),
                   jax.ShapeDtypeStruct((8, DI), F32), jax.ShapeDtypeStruct((8, HEADS), F32)],
        aliases={9: 1}, scratch_shapes=[pltpu.VMEM((512, 512), F32)], sem=("arbitrary",), carry=carry)


def _gate_norm(y, proj, w):
    T = y.shape[0]
    tr = _row_tile(T)

    def body(y_ref, z_ref, w_ref, o_ref):
        for g in range(GROUPS):
            gs = slice(512 * g, 512 * (g + 1))
            z = z_ref[:, gs]
            yg = y_ref[:, gs] * (z * _sigmoid(z))
            r = lax.rsqrt(jnp.mean(yg * yg, axis=-1, keepdims=True) + EPS)
            o_ref[:, gs] = (yg * r * w_ref[:, gs]).astype(BF16)

    tile = pl.BlockSpec((tr, DI), lambda i: (i, 0))
    return _call(body, [y, proj, w], name="gate_norm", grid=(T // tr,),
                 in_specs=[tile, tile, pl.BlockSpec((1, DI), lambda i: (0, 0))], out_specs=[tile],
                 out_shape=[jax.ShapeDtypeStruct((T, DI), BF16)], sem=("parallel",))[0]


def _gate_norm_bwd(dyn, y, proj, w, dproj):
    T = y.shape[0]
    tr = _row_tile(T)

    def body(d_ref, y_ref, z_ref, w_ref, dp_in, dy_ref, dz_ref, sums_ref):
        del dp_in

        @pl.when(pl.program_id(0) == 0)
        def _():
            sums_ref[...] = jnp.zeros_like(sums_ref)

        for g in range(GROUPS):
            gs = slice(512 * g, 512 * (g + 1))
            z, yv, d = z_ref[:, gs], y_ref[:, gs], d_ref[:, gs]
            s = _sigmoid(z)
            silu = z * s
            yg = yv * silu
            r = lax.rsqrt(jnp.mean(yg * yg, axis=-1, keepdims=True) + EPS)
            yn = yg * r
            sums_ref[0:1, gs] += jnp.sum(d * yn, axis=0, keepdims=True)
            dn = d * w_ref[:, gs]
            dyg = r * (dn - yn * jnp.mean(dn * yn, axis=-1, keepdims=True))
            dy_ref[:, gs] = dyg * silu
            dz_ref[:, gs] = (dyg * yv * (s * (1.0 + z * (1.0 - s)))).astype(BF16)

    tile = pl.BlockSpec((tr, DI), lambda i: (i, 0))
    return _call(
        body, [dyn, y, proj, w, dproj], name="gate_norm_bwd", grid=(T // tr,),
        in_specs=[tile, tile, tile, pl.BlockSpec((1, DI), lambda i: (0, 0)), pl.BlockSpec(memory_space=pl.ANY)],
        out_specs=[tile, tile, pl.BlockSpec((8, DI), lambda i: (0, 0))],
        out_shape=[jax.ShapeDtypeStruct((T, DI), F32), jax.ShapeDtypeStruct(dproj.shape, BF16),
                   jax.ShapeDtypeStruct((8, DI), F32)],
        aliases={4: 1}, sem=("arbitrary",))


def _pool_fwd(proj, pool_w_b, pool_scale):
    T = proj.shape[0]
    tr = _row_tile(T)
    nb = tr // 16

    def body(u_ref, h_ref, pw_ref, ps_ref, pooled_ref, pw_out_ref, yps_ref):
        i = pl.program_id(0)
        t = i * tr + lax.broadcasted_iota(jnp.int32, (tr, 1), 0)
        for g, win in enumerate(POOL_WINDOWS):
            gs = slice(GW * g, GW * (g + 1))
            u = u_ref[:, gs]
            s = jnp.concatenate([jnp.where(i > 0, h_ref[:, gs], 0.0), u], axis=0)
            sh = 1
            while sh < win:
                s = s + pltpu.roll(s, sh, 0)
                sh *= 2
            pooled = (s[16:] * (1.0 / jnp.minimum(t + 1, win).astype(F32)) - u).astype(BF16)
            pooled_ref[:, gs] = pooled
            pwv = _nn(pooled, pw_ref[g])
            pw_out_ref[:, gs] = pwv
            yps_ref[:, gs] = (pwv * ps_ref[:, gs]).astype(BF16)

    tile = pl.BlockSpec((tr, D), lambda i: (i, 0))
    return _call(
        body, [proj, proj, pool_w_b, pool_scale], name="pool_fwd", grid=(T // tr,),
        in_specs=[pl.BlockSpec((tr, D), lambda i: (i, OFF_POOL // D)),
                  pl.BlockSpec((16, D), lambda i: (jnp.maximum(i * nb - 1, 0), OFF_POOL // D)),
                  pl.BlockSpec((4, GW, GW), lambda i: (0, 0, 0)),
                  pl.BlockSpec((1, D), lambda i: (0, 0))],
        out_specs=[tile, tile, tile],
        out_shape=[jax.ShapeDtypeStruct((T, D), BF16), jax.ShapeDtypeStruct((T, D), F32),
                   jax.ShapeDtypeStruct((T, D), BF16)], sem=("parallel",))


def _pool_bwd(dyp, pw_out, pooled, pool_w_b, pool_scale, dproj):
    T = dyp.shape[0]
    tr = _row_tile(T)
    nb, last = tr // 16, T // tr - 1

    def body(d_ref, h_ref, pwo_ref, pooled_ref, pw_ref, ps_ref, dp_in, du_ref, gpw_ref, sums_ref):
        del dp_in
        i = pl.program_id(0)

        @pl.when(i == 0)
        def _():
            gpw_ref[...] = jnp.zeros_like(gpw_ref)
            sums_ref[...] = jnp.zeros_like(sums_ref)

        n = tr + 16
        t = i * tr + lax.broadcasted_iota(jnp.int32, (n, 1), 0)
        sums_ref[0:1, :] += jnp.sum(d_ref[...] * pwo_ref[...], axis=0, keepdims=True)
        for g, win in enumerate(POOL_WINDOWS):
            gs = slice(GW * g, GW * (g + 1))
            d_ext = jnp.concatenate([d_ref[:, gs], jnp.where(i < last, h_ref[:, gs], 0.0)], axis=0)
            dpw = (d_ext * ps_ref[:, gs]).astype(BF16)
            dpooled = _nt(dpw, pw_ref[g])
            s = jnp.where(t < T, dpooled * (1.0 / jnp.minimum(t + 1, win).astype(F32)), 0.0)
            sh = 1
            while sh < win:
                s = s + pltpu.roll(s, n - sh, 0)
                sh *= 2
            du_ref[:, gs] = (s[:tr] - dpooled[:tr]).astype(BF16)
            gpw_ref[g] += _tn(pooled_ref[:, gs], dpw[:tr])

    tile = pl.BlockSpec((tr, D), lambda i: (i, 0))
    return _call(
        body, [dyp, dyp, pw_out, pooled, pool_w_b, pool_scale, dproj], name="pool_bwd", grid=(T // tr,),
        in_specs=[tile, pl.BlockSpec((16, D), lambda i: (jnp.minimum((i + 1) * nb, T // 16 - 1), 0)), tile, tile,
                  pl.BlockSpec((4, GW, GW), lambda i: (0, 0, 0)), pl.BlockSpec((1, D), lambda i: (0, 0)),
                  pl.BlockSpec(memory_space=pl.ANY)],
        out_specs=[pl.BlockSpec((tr, D), lambda i: (i, OFF_POOL // D)),
                   pl.BlockSpec((4, GW, GW), lambda i: (0, 0, 0)), pl.BlockSpec((8, D), lambda i: (0, 0))],
        out_shape=[jax.ShapeDtypeStruct(dproj.shape, BF16), jax.ShapeDtypeStruct((4, GW, GW), F32),
                   jax.ShapeDtypeStruct((8, D), F32)],
        aliases={6: 0}, sem=("arbitrary",))


def _merge(proj, y_ssd, y_pool):
    T = proj.shape[0]
    tr = _row_tile(T)

    def body(g_ref, a_ref, b_ref, o_ref):
        o_ref[...] = (_sigmoid(g_ref[:, 0:D]) * a_ref[...] + _sigmoid(g_ref[:, D:2 * D]) * b_ref[...]).astype(BF16)

    tile = pl.BlockSpec((tr, D), lambda i: (i, 0))
    return _call(body, [proj, y_ssd, y_pool], name="merge", grid=(T // tr,),
                 in_specs=[pl.BlockSpec((tr, 2 * D), lambda i: (i, OFF_GATE // (2 * D))), tile, tile], out_specs=[tile],
                 out_shape=[jax.ShapeDtypeStruct((T, D), BF16)], sem=("parallel",))[0]


def _merge_bwd(dmerged, proj, y_ssd, y_pool):
    T = proj.shape[0]
    tr = _row_tile(T)

    def body(d_ref, g_ref, a_ref, b_ref, da_ref, db_ref, dg_ref):
        d = d_ref[...]
        ga, gb = _sigmoid(g_ref[:, 0:D]), _sigmoid(g_ref[:, D:2 * D])
        da_ref[...] = (d * ga).astype(BF16)
        db_ref[...] = (d * gb).astype(BF16)
        dg_ref[:, 0:D] = (d * a_ref[...] * ga * (1.0 - ga)).astype(BF16)
        dg_ref[:, D:2 * D] = (d * b_ref[...] * gb * (1.0 - gb)).astype(BF16)

    tile = pl.BlockSpec((tr, D), lambda i: (i, 0))
    gates = pl.BlockSpec((tr, 2 * D), lambda i: (i, OFF_GATE // (2 * D)))
    return _call(body, [dmerged, proj, y_ssd, y_pool], name="merge_bwd", grid=(T // tr,),
                 in_specs=[tile, gates, tile, tile], out_specs=[tile, tile, gates],
                 out_shape=[jax.ShapeDtypeStruct((T, D), BF16), jax.ShapeDtypeStruct((T, D), BF16),
                            jax.ShapeDtypeStruct((T, NP), BF16)], sem=("parallel",))


def _adamw(w, g, m, v, name, carry=None):
    R, C = w.shape
    tr = R if R <= 128 else 128
    assert R % tr == 0

    def body(w_ref, g_ref, m_ref, v_ref, d_ref, mo_ref, vo_ref):
        gv = g_ref[...]
        mn = ADAM_B1 * m_ref[...] + (1.0 - ADAM_B1) * gv
        vn = ADAM_B2 * v_ref[...] + (1.0 - ADAM_B2) * (gv * gv)
        m_hat = mn * (1.0 / (1.0 - ADAM_B1 ** ADAM_STEP))
        v_hat = vn * (1.0 / (1.0 - ADAM_B2 ** ADAM_STEP))
        d_ref[...] = -ADAM_LR * (m_hat / (jnp.sqrt(v_hat) + ADAM_EPS) + ADAM_WD * w_ref[...])
        mo_ref[...] = mn
        vo_ref[...] = vn

    tile = pl.BlockSpec((tr, C), lambda i: (i, 0))
    sds = jax.ShapeDtypeStruct((R, C), F32)
    return _call(body, [w, g, m, v], name=name, grid=(R // tr,), in_specs=[tile] * 4, out_specs=[tile] * 3,
                 out_shape=[sds] * 3, sem=("parallel",), carry=carry)


def _me():
    return lax.axis_index("x"), lax.axis_index("y"), lax.axis_index("c")


def _xor_peer(x, y, c, p):
    return (x ^ ((p >> 2) & 1), y ^ ((p >> 1) & 1), c ^ (p & 1))


def _ada_fwd(c_row, w_ada, b_ada_mine, carry=None):
    n_cols = w_ada.shape[1]

    def body(c_ref, w_ref, b_ref, mod_ref, c8_ref, csend, mpart, modbuf, send_sems, recv_sems):
        x, y, c = _me()
        me = 4 * x + 2 * y + c
        chip = 2 * x + y
        csend[...] = jnp.broadcast_to(c_ref[...], csend.shape)
        c8_ref[me] = csend[...]

        def c_copy(p):
            return pltpu.make_async_remote_copy(
                src_ref=csend, dst_ref=c8_ref.at[me], send_sem=send_sems.at[p - 1], recv_sem=recv_sems.at[p - 1],
                device_id=_xor_peer(x, y, c, p), device_id_type=MESH)

        for p in range(1, 8):
            c_copy(p).start()
        for p in range(1, 8):
            c_copy(p).wait_recv()
        cs = jnp.concatenate([c8_ref[d][0:1, :] for d in range(8)], axis=0)
        mpart[...] = _nn(cs * _sigmoid(cs), w_ref[...], precision=HIGH) + b_ref[...]
        modbuf[chip] = mpart[...]

        def m_copy(m):
            return pltpu.make_async_remote_copy(
                src_ref=mpart, dst_ref=modbuf.at[chip], send_sem=send_sems.at[6 + m], recv_sem=recv_sems.at[6 + m],
                device_id=_xor_peer(x, y, c, 2 * m), device_id_type=MESH)

        for m in range(1, 4):
            m_copy(m).start()
        for m in range(1, 4):
            m_copy(m).wait_recv()
        mine = lax.broadcasted_iota(jnp.int32, (8, 1), 0) == me
        for k in range(N_CHIPS):
            mod_ref[:, n_cols * k:n_cols * (k + 1)] = jnp.sum(jnp.where(mine, modbuf[k], 0.0), axis=0, keepdims=True)
        for p in range(1, 8):
            c_copy(p).wait_send()
        for m in range(1, 4):
            m_copy(m).wait_send()

    vmem = pl.BlockSpec(memory_space=pltpu.VMEM)
    return _call(
        body, [c_row, w_ada, b_ada_mine], name="ada_fwd", in_specs=[vmem, vmem, vmem], out_specs=[vmem, vmem],
        out_shape=[jax.ShapeDtypeStruct((1, N_CHIPS * n_cols), F32), jax.ShapeDtypeStruct((8, 8, D), F32)],
        scratch_shapes=[pltpu.VMEM((8, D), F32), pltpu.VMEM((8, n_cols), F32), pltpu.VMEM((N_CHIPS, 8, n_cols), F32),
                        pltpu.SemaphoreType.DMA((10,)), pltpu.SemaphoreType.DMA((10,))], carry=carry)


def _gather_small(vec, carry=None):
    rows = vec.shape[0]

    def body(v_ref, all_ref, tot_ref, dsk_ref, send_sems, recv_sems):
        x, y, c = _me()
        me = 4 * x + 2 * y + c
        all_ref[me] = v_ref[...]

        def copy(p):
            return pltpu.make_async_remote_copy(
                src_ref=v_ref, dst_ref=all_ref.at[me], send_sem=send_sems.at[p - 1], recv_sem=recv_sems.at[p - 1],
                device_id=_xor_peer(x, y, c, p), device_id_type=MESH)

        for p in range(1, 8):
            copy(p).start()
        for p in range(1, 8):
            copy(p).wait_recv()
        tot = all_ref[0]
        for d in range(1, 8):
            tot = tot + all_ref[d]
        tot_ref[...] = tot
        seg = tot[SMALL_OFF["d_skip"] // 128:SMALL_OFF["d_skip"] // 128 + 16, :]
        lane = lax.broadcasted_iota(jnp.int32, (1, 128), 1)
        sa = jnp.sum(jnp.where(lane < HEAD_DIM, seg, 0.0), axis=1, keepdims=True)
        sb = jnp.sum(jnp.where(lane < HEAD_DIM, 0.0, seg), axis=1, keepdims=True)
        dsk_ref[...] = jnp.where(lane == 0, sa, jnp.where(lane == 1, sb, 0.0))
        for p in range(1, 8):
            copy(p).wait_send()

    vmem = pl.BlockSpec(memory_space=pltpu.VMEM)
    return _call(
        body, [vec], name="gather_small", in_specs=[vmem], out_specs=[vmem, vmem, vmem],
        out_shape=[jax.ShapeDtypeStruct((8, rows, 128), F32), jax.ShapeDtypeStruct((rows, 128), F32),
                   jax.ShapeDtypeStruct((16, 128), F32)],
        scratch_shapes=[pltpu.SemaphoreType.DMA((7,)), pltpu.SemaphoreType.DMA((7,))], carry=carry)


def _gather_carry(shards):
    n = len(shards)

    def copies(ins, outs, sems):
        x, y, c = _me()
        chip = 2 * x + y

        def half(w, which):
            h = shards[w].shape[0] // 2
            return pl.ds(which * h, h)

        def first(w, m):
            return pltpu.make_async_remote_copy(
                src_ref=ins[w].at[half(w, c)], dst_ref=outs[w].at[chip, half(w, c)],
                send_sem=sems.send(6 * w + m - 1), recv_sem=sems.recv(6 * w + m - 1),
                device_id=_xor_peer(x, y, c, 2 * m), device_id_type=MESH)

        def landed(w, m):
            return pltpu.make_async_remote_copy(
                src_ref=ins[w].at[half(w, c)], dst_ref=outs[w].at[chip ^ m, half(w, c)],
                send_sem=sems.send(6 * w + m - 1), recv_sem=sems.recv(6 * w + m - 1),
                device_id=_xor_peer(x, y, c, 2 * m), device_id_type=MESH)

        def passed(w, m, which):
            part = outs[w].at[chip ^ m, half(w, which)]
            return pltpu.make_async_remote_copy(
                src_ref=part, dst_ref=part, send_sem=sems.send(6 * w + 2 + m), recv_sem=sems.recv(6 * w + 2 + m),
                device_id=(x, y, 1 - c), device_id_type=MESH)

        return c, first, landed, passed

    pairs = [(w, m) for w in range(n) for m in range(1, 4)]

    def start(ins, outs, sems):
        _, first, _, _ = copies(ins, outs, sems)
        for w, m in pairs:
            first(w, m).start()

    def finish(ins, outs, sems):
        c, first, landed, passed = copies(ins, outs, sems)
        for w, m in pairs:
            landed(w, m).wait_recv()
            passed(w, m, c).start()
        for w, m in pairs:
            passed(w, m, 1 - c).wait_recv()
        for w, m in pairs:
            first(w, m).wait_send()
            passed(w, m, c).wait_send()

    return _Carry(shards, [jax.ShapeDtypeStruct((N_CHIPS,) + s.shape, s.dtype) for s in shards], 6 * n, start, finish)


def _pair_exchange_carry(grads):
    n = len(grads)

    def copy(ins, outs, sems, w):
        x, y, c = _me()
        h = grads[w].shape[1] // 2
        return pltpu.make_async_remote_copy(
            src_ref=ins[w].at[:, pl.ds((1 - c) * h, h)], dst_ref=outs[w],
            send_sem=sems.send(w), recv_sem=sems.recv(w), device_id=(x, y, 1 - c), device_id_type=MESH)

    def start(ins, outs, sems):
        for w in range(n):
            copy(ins, outs, sems, w).start()

    def finish(ins, outs, sems):
        for w in range(n):
            copy(ins, outs, sems, w).wait()

    return _Carry(grads, [jax.ShapeDtypeStruct((N_CHIPS, g.shape[1] // 2, g.shape[2]), g.dtype) for g in grads], n,
                  start, finish)


def _chip_exchange_carry(partials):
    n = len(partials)

    def copier(ins, outs, sems):
        x, y, c = _me()
        chip = 2 * x + y

        def copy(w, m, landed):
            return pltpu.make_async_remote_copy(
                src_ref=ins[w].at[chip ^ m], dst_ref=outs[w].at[(chip ^ m) if landed else chip],
                send_sem=sems.send(3 * w + m - 1), recv_sem=sems.recv(3 * w + m - 1),
                device_id=_xor_peer(x, y, c, 2 * m), device_id_type=MESH)

        return copy

    pairs = [(w, m) for w in range(n) for m in range(1, 4)]

    def start(ins, outs, sems):
        copy = copier(ins, outs, sems)
        for w, m in pairs:
            copy(w, m, False).start()

    def finish(ins, outs, sems):
        copy = copier(ins, outs, sems)
        for w, m in pairs:
            copy(w, m, True).wait_recv()
        for w, m in pairs:
            copy(w, m, False).wait_send()

    return _Carry(partials, [jax.ShapeDtypeStruct(p.shape, p.dtype) for p in partials], 3 * n, start, finish)


def _pair_share_carry(shards):
    n = len(shards)

    def copier(ins, outs, sems):
        x, y, c = _me()

        def copy(w, which):
            h = shards[w].shape[0] // 2
            rows = pl.ds(which * h, h)
            return pltpu.make_async_remote_copy(
                src_ref=ins[w].at[rows], dst_ref=outs[w].at[rows],
                send_sem=sems.send(w), recv_sem=sems.recv(w), device_id=(x, y, 1 - c), device_id_type=MESH)

        return c, copy

    def start(ins, outs, sems):
        c, copy = copier(ins, outs, sems)
        for w in range(n):
            copy(w, c).start()

    def finish(ins, outs, sems):
        c, copy = copier(ins, outs, sems)
        for w in range(n):
            copy(w, 1 - c).wait_recv()
        for w in range(n):
            copy(w, c).wait_send()

    return _Carry(shards, [jax.ShapeDtypeStruct(s.shape, s.dtype) for s in shards], n, start, finish,
                  aliased=[(w, w) for w in range(n)])


def _pair_sum(g, part, idx, name):
    _, h, C = part.shape
    tr = min(512, h)
    nb = h // tr

    def body(idx_ref, g_ref, p_ref, o16_ref, own_ref):
        v = g_ref[...].astype(F32) + p_ref[...].astype(F32)
        o16_ref[...] = v.astype(BF16)

        @pl.when(pl.program_id(1) == idx_ref[1])
        def _():
            own_ref[...] = v

    return pl.pallas_call(
        body, name=name,
        grid_spec=pltpu.PrefetchScalarGridSpec(
            num_scalar_prefetch=1, grid=(nb, N_CHIPS),
            in_specs=[pl.BlockSpec((None, tr, C), lambda i, s, idx_ref: (s, idx_ref[0] * nb + i, 0)),
                      pl.BlockSpec((None, tr, C), lambda i, s, idx_ref: (s, i, 0))],
            out_specs=[pl.BlockSpec((None, tr, C), lambda i, s, idx_ref: (s, i, 0)),
                       pl.BlockSpec((tr, C), lambda i, s, idx_ref: (i, 0))]),
        out_shape=[jax.ShapeDtypeStruct(part.shape, BF16), jax.ShapeDtypeStruct((h, C), F32)],
        compiler_params=pltpu.CompilerParams(dimension_semantics=("arbitrary", "arbitrary"), vmem_limit_bytes=VMEM_LIMIT),
    )(idx, g, part)


def _chip_sum(own, slots, idx, name):
    h, C = own.shape
    tr = min(512, h)
    nb = h // tr

    def body(idx_ref, own_ref, s1_ref, s2_ref, s3_ref, o_ref):
        del idx_ref
        o_ref[...] = ((own_ref[...] + s1_ref[...].astype(F32)) + s2_ref[...].astype(F32)) + s3_ref[...].astype(F32)

    def slot(m):
        return pl.BlockSpec((None, tr, C), lambda i, idx_ref: (idx_ref[1] ^ m, i, 0))

    return pl.pallas_call(
        body, name=name,
        grid_spec=pltpu.PrefetchScalarGridSpec(
            num_scalar_prefetch=1, grid=(nb,),
            in_specs=[pl.BlockSpec((tr, C), lambda i, idx_ref: (i, 0)), slot(1), slot(2), slot(3)],
            out_specs=pl.BlockSpec((tr, C), lambda i, idx_ref: (idx_ref[0] * nb + i, 0))),
        out_shape=jax.ShapeDtypeStruct((2 * h, C), F32),
        compiler_params=pltpu.CompilerParams(dimension_semantics=("parallel",), vmem_limit_bytes=VMEM_LIMIT),
    )(idx, own, slots, slots, slots)


class _Reducer:
    def __init__(self, idx):
        self.idx, self.chips, self.p16, self.own, self.mine, self.final = idx, {}, {}, {}, {}, {}

    def add(self, name, whole, chip_blocks=False):
        self.chips[name] = whole if chip_blocks else _chips_from_whole(name, whole)

    def pair(self, names):
        return _pair_exchange_carry([self.chips[n] for n in names])

    def take_pair(self, names, outs):
        for n, part in zip(names, outs):
            self.p16[n], self.own[n] = _pair_sum(self.chips.pop(n), part, self.idx, "pair_sum_" + n)

    def chip(self, names):
        return _chip_exchange_carry([self.p16[n] for n in names])

    def take_chip(self, names, outs):
        for n, slots in zip(names, outs):
            del self.p16[n]
            self.mine[n] = _chip_sum(self.own.pop(n), slots, self.idx, "chip_sum_" + n)

    def share(self, names):
        return _pair_share_carry([self.mine[n] for n in names])

    def take_share(self, names, outs):
        for n, s in zip(names, outs):
            del self.mine[n]
            self.final[n] = s


def _w_ada_grad(c8, dmod_cols):
    n_cols = dmod_cols.shape[1]
    tn = 512

    def body(c_ref, d_ref, o_ref):
        cv = c_ref[...]
        o_ref[...] = _tn(cv * _sigmoid(cv), d_ref[...], precision=HIGH)

    return _call(body, [c8, dmod_cols], name="w_ada_grad", grid=(n_cols // tn,),
                 in_specs=[pl.BlockSpec((8, D), lambda j: (0, 0)), pl.BlockSpec((8, tn), lambda j: (0, j))],
                 out_specs=[pl.BlockSpec((D, tn), lambda j: (0, j))],
                 out_shape=[jax.ShapeDtypeStruct((D, n_cols), F32)], sem=("parallel",))[0]


_SMALL_SEGS = (("dmod", 6144), ("norm_mix_w", 1024), ("conv_b", 3072), ("ssd_norm_w", 2048), ("pool_scale", 1024),
               ("norm_mlp_w", 1024), ("norm_final_w", 1024), ("conv_w", 4 * XBC), ("d_skip", 2048), ("a_log", 128),
               ("dt_bias", 128), ("loss", 128))
SMALL_OFF = {}
_o = 0
for _n, _s in _SMALL_SEGS:
    SMALL_OFF[_n] = _o
    _o += _s
SMALL_LEN = -(-_o // 1024) * 1024

_FIRST = ("w_in", "conv_w")
_LATER = ("w_branch_ssd", "pool_w", "w_branch_pool", "w_out", "w_up", "w_down")
_SMALL_REPLICATED = ("b_ada", "norm_mix_w", "conv_b", "dt_bias", "a_log", "d_skip", "ssd_norm_w", "pool_scale",
                     "norm_mlp_w", "norm_final_w")
_WEIGHTS = ("w_ada", "b_ada", "norm_mix_w", "w_in", "conv_w", "conv_b", "dt_bias", "a_log", "d_skip", "ssd_norm_w",
            "w_branch_ssd", "pool_w", "pool_scale", "w_branch_pool", "w_out", "norm_mlp_w", "w_up", "w_down",
            "norm_final_w")


def _shard_2d(name, a):
    if name == "conv_w":
        return a.reshape(16, -1)
    return (a.reshape(GW, GW) if name == "pool_w" else a.reshape(a.shape[-2], a.shape[-1])).astype(BF16)


def _whole_from_chips(name, g, own, chip):
    g = lax.dynamic_update_slice(g, own[None], (chip, 0, 0))
    if name == "w_in":
        a, b = _DT_IN_CHIP2, _DT_IN_CHIP2 + HEADS
        pad = jnp.zeros((D, NP - IN_COLS), g.dtype)
        return jnp.concatenate([g[0], g[1], g[2][:, :a], g[2][:, b:], g[3], g[2][:, a:b], pad], axis=1)
    if name == "w_up":
        return jnp.concatenate([g[k] for k in range(N_CHIPS)], axis=1)
    if name == "pool_w":
        return jnp.transpose(g.reshape(N_CHIPS, 4, GW // N_CHIPS, GW), (1, 0, 2, 3)).reshape(4, GW, GW)
    if name == "conv_w":
        return jnp.transpose(g.reshape(N_CHIPS, 4, XBC // N_CHIPS), (1, 0, 2)).reshape(4, XBC)
    return g.reshape(N_CHIPS * g.shape[1], g.shape[2])


def _chips_from_whole(name, g):
    if name.startswith("w_in"):
        cw, a = IN_COLS // N_CHIPS, _DT_IN_CHIP2
        chip2 = jnp.concatenate([g[:, 2 * cw:2 * cw + a], g[:, OFF_DT:OFF_DT + HEADS], g[:, 5120:3 * cw - HEADS]], axis=1)
        return jnp.stack([g[:, :cw], g[:, cw:2 * cw], chip2, g[:, 3 * cw - HEADS:OFF_DT]])
    if name == "w_up":
        return jnp.transpose(g.reshape(D, N_CHIPS, DFF // N_CHIPS), (1, 0, 2))
    if name == "pool_w":
        return jnp.transpose(g.reshape(4, N_CHIPS, GW // N_CHIPS, GW), (1, 0, 2, 3)).reshape(N_CHIPS, GW, GW)
    return g.reshape(N_CHIPS, g.shape[0] // N_CHIPS, g.shape[1])


def kernel(x, c, w_ada, b_ada, norm_mix_w, w_in, conv_w, conv_b, dt_bias, a_log, d_skip, ssd_norm_w, w_branch_ssd, pool_w, pool_scale, w_branch_pool, w_out, norm_mlp_w, w_up, w_down, norm_final_w, loss_target, m_w_ada, m_b_ada, m_norm_mix_w, m_w_in, m_conv_w, m_conv_b, m_dt_bias, m_a_log, m_d_skip, m_ssd_norm_w, m_w_branch_ssd, m_pool_w, m_pool_scale, m_w_branch_pool, m_w_out, m_norm_mlp_w, m_w_up, m_w_down, m_norm_final_w, v_w_ada, v_b_ada, v_norm_mix_w, v_w_in, v_conv_w, v_conv_b, v_dt_bias, v_a_log, v_d_skip, v_ssd_norm_w, v_w_branch_ssd, v_pool_w, v_pool_scale, v_w_branch_pool, v_w_out, v_norm_mlp_w, v_w_up, v_w_down, v_norm_final_w):
    args = locals()
    w = {n: args[n] for n in _WEIGHTS}
    m = {n: args["m_" + n] for n in _WEIGHTS}
    v = {n: args["v_" + n] for n in _WEIGHTS}
    xi, yi, ci = _me()
    chip = 2 * xi + yi
    idx = jnp.stack([ci, chip]).astype(jnp.int32)
    ada_cols = w_ada.shape[-1]
    xs, target = x[0], loss_target[0]
    two_d = lambda n, a: a.reshape(GW, GW) if n == "pool_w" else a.reshape(-1, a.shape[-1])
    delta, new_m, new_v, g = {}, {}, {}, {}

    def adamw(n, carry=None):
        res = _adamw(two_d(n, w[n]), two_d(n, g[n]), two_d(n, m[n]), two_d(n, v[n]), "adamw_" + n, carry=carry)
        (delta[n], new_m[n], new_v[n]), extra = res if carry is not None else (res, None)
        return extra

    b_mine = lax.dynamic_slice(b_ada, (0, chip * ada_cols), (1, ada_cols))
    shards = {n: _shard_2d(n, w[n]) for n in _FIRST + _LATER}
    mod, c8 = _ada_fwd(c, w_ada[0], b_mine)
    c8 = c8[:, 0, :]
    shift_m, scale_m, gate_m, shift_f, scale_f, gate_f = [mod[:, D * i:D * (i + 1)] for i in range(6)]
    nf_w = norm_final_w.reshape(1, D)

    h1, first = _norm_mod(xs, norm_mix_w, scale_m, shift_m, "norm_mod_mix",
                          carry=_gather_carry([shards[n] for n in _FIRST]))
    p ={n: _whole_from_chips(n, a, shards[n], chip) for n, a in zip(_FIRST, first)}
    (proj,), later = _matmul(h1, p["w_in"], mode="nn", out_dtypes=[F32], name="mm_proj",
                             carry=_gather_carry([shards[n] for n in _LATER]))
    p.update({n: _whole_from_chips(n, a, shards[n], chip) for n, a in zip(_LATER, later)})
    xbc_a = _conv_fwd(proj, p["conv_w"], conv_b)
    dtb_c, alog_c = dt_bias.reshape(HEADS, 1), a_log.reshape(HEADS, 1)
    dsk_exp = jnp.repeat(d_skip, HEAD_DIM, axis=1)
    y, hin = _ssd_fwd(xbc_a, proj, dt_bias, a_log, dtb_c, alog_c, dsk_exp)
    yn = _gate_norm(y, proj, ssd_norm_w)
    (y_ssd,) = _matmul(yn, p["w_branch_ssd"], mode="nn", out_dtypes=[F32], name="mm_branch_ssd")
    pooled, pw_out, yps = _pool_fwd(proj, p["pool_w"], pool_scale)
    (y_pool,) = _matmul(yps, p["w_branch_pool"], mode="nn", out_dtypes=[F32], name="mm_branch_pool")
    merged = _merge(proj, y_ssd, y_pool)
    resid = lambda acc, r, gt: (r + gt * acc, acc)
    x2, mix = _matmul(merged, p["w_out"], mode="nn", out_dtypes=[F32, BF16], name="mm_out",
                      epi=resid, tile_extras=(xs,), row_extras=(gate_m,))
    h2 = _norm_mod(x2, norm_mlp_w, scale_f, shift_f, "norm_mod_mlp")
    relu2 = lambda acc: (acc, jnp.square(jnp.maximum(acc, 0.0)))
    up, act = _matmul(h2, p["w_up"], mode="nn", out_dtypes=[BF16, BF16], name="mm_up", epi=relu2)
    x3, down = _matmul(act, p["w_down"], mode="nn", out_dtypes=[F32, BF16], name="mm_down",
                       epi=resid, tile_extras=(x2,), row_extras=(gate_f,))

    red = _Reducer(idx)
    dx3, d_down, sums_f = _final_loss_bwd(x3, target, nf_w, down, gate_f)
    drelu2 = lambda acc, u: (acc * (2.0 * jnp.maximum(u.astype(F32), 0.0)),)
    (dup,) = _matmul(d_down, p["w_down"], mode="nt", out_dtypes=[BF16], name="mm_dact",
                     epi=drelu2, tile_extras=(up,))
    red.add("w_down", _matmul(act, d_down, mode="tn", out_dtypes=[BF16], name="mm_g_down")[0])
    (dh2,), got = _matmul(dup, p["w_up"], mode="nt", out_dtypes=[F32], name="mm_dh2",
                          carry=red.pair(["w_down"]))
    red.take_pair(["w_down"], got)
    red.add("w_up", _matmul(h2, dup, mode="tn", out_dtypes=[BF16], name="mm_g_up", chip_blocks=True)[0], chip_blocks=True)
    dx2, sums_2, dmix = _norm_mod_bwd(x2, dh2, dx3, norm_mlp_w, scale_f, "norm_mod_mlp_bwd", branch=mix, gate=gate_m)
    (dmerged,), got = _matmul(dmix, p["w_out"], mode="nt", out_dtypes=[F32], name="mm_dmerged",
                              carry=red.pair(["w_up"]))
    red.take_pair(["w_up"], got)
    red.add("w_out", _matmul(merged, dmix, mode="tn", out_dtypes=[BF16], name="mm_g_out")[0])
    dy_ssd, dy_pool, dproj = _merge_bwd(dmerged, proj, y_ssd, y_pool)
    (dyp,), got = _matmul(dy_pool, p["w_branch_pool"], mode="nt", out_dtypes=[F32], name="mm_dyp",
                          carry=red.pair(["w_out"]))
    red.take_pair(["w_out"], got)
    red.add("w_branch_pool", _matmul(yps, dy_pool, mode="tn", out_dtypes=[BF16], name="mm_g_bpool")[0])
    dproj, g_pool_w, sums_pool = _pool_bwd(dyp, pw_out, pooled, p["pool_w"], pool_scale, dproj)
    red.add("pool_w", g_pool_w.astype(BF16))
    red.add("w_branch_ssd", _matmul(yn, dy_ssd, mode="tn", out_dtypes=[BF16], name="mm_g_bssd")[0])
    mixers = ["w_branch_pool", "pool_w", "w_branch_ssd"]
    (dyn,), got = _matmul(dy_ssd, p["w_branch_ssd"], mode="nt", out_dtypes=[F32], name="mm_dyn",
                          carry=red.pair(mixers))
    red.take_pair(mixers, got)
    dy, dproj, sums_gn = _gate_norm_bwd(dyn, y, proj, ssd_norm_w, dproj)
    six = ["w_down", "w_up", "w_out"] + mixers
    (dxa, dproj, dsk_sum, ssd_small), got = _ssd_bwd(dy, xbc_a, proj, hin, dt_bias, a_log, dtb_c, alog_c, dsk_exp,
                                                     dproj, carry=red.chip(six))
    red.take_chip(six, got)
    dxc, sums_conv = _conv_bwd_a(dxa, proj, p["conv_w"], conv_b)
    dproj = _conv_bwd_b(dxc, p["conv_w"], dproj)
    rows_a = 3 * D // 4
    (g_in_a,), got = _matmul(h1, dproj, mode="tn", out_dtypes=[BF16], name="mm_g_in_a", a_cols=(0, rows_a),
                             carry=red.share(six))
    red.take_share(six, got)
    red.add("w_in_a", g_in_a)
    (g_in_b,), got = _matmul(h1, dproj, mode="tn", out_dtypes=[BF16], name="mm_g_in_b", a_cols=(rows_a, D - rows_a),
                             carry=red.pair(["w_in_a"]))
    red.take_pair(["w_in_a"], got)
    red.add("w_in_b", g_in_b)
    (dh1,), got = _matmul(dproj, p["w_in"], mode="nt", out_dtypes=[F32], name="mm_dh1",
                          carry=_join(red.chip(["w_in_a"]), red.pair(["w_in_b"])))
    red.take_chip(["w_in_a"], got[:1])
    red.take_pair(["w_in_b"], got[1:])
    grad_x, sums_1 = _norm_mod_bwd(xs, dh1, dx2, norm_mix_w, scale_m, "norm_mod_mix_bwd")

    dmod = jnp.concatenate([sums_1[0:1], sums_1[1:2], sums_2[3:4], sums_2[0:1], sums_2[1:2], sums_f[1:2]], axis=1)
    pad96 = jnp.zeros((1, 96), F32)
    small = {"dmod": dmod, "norm_mix_w": sums_1[2:3], "conv_b": sums_conv[4:5], "ssd_norm_w": sums_gn[0:1],
             "pool_scale": sums_pool[0:1], "norm_mlp_w": sums_2[2:3], "norm_final_w": sums_f[0:1],
             "conv_w": sums_conv[0:4].reshape(1, 4 * XBC), "d_skip": dsk_sum[0:1],
             "a_log": jnp.concatenate([ssd_small[0:1], pad96], axis=1),
             "dt_bias": jnp.concatenate([ssd_small[1:2], pad96], axis=1), "loss": sums_f[3:4, 0:128]}
    vec = jnp.concatenate([small[n] for n, _ in _SMALL_SEGS], axis=1)
    vec = jnp.pad(vec, ((0, 0), (0, SMALL_LEN - vec.shape[1]))).reshape(SMALL_LEN // 128, 128)
    (every, total, dsk), got = _gather_small(vec, carry=_join(red.chip(["w_in_b"]), red.share(["w_in_a"])))
    red.take_chip(["w_in_b"], got[:1])
    red.take_share(["w_in_a"], got[1:])
    total = total.reshape(1, SMALL_LEN)
    seg = lambda n, size: total[:, SMALL_OFF[n]:SMALL_OFF[n] + size]
    g.update({"b_ada": seg("dmod", 6 * D), "norm_mix_w": seg("norm_mix_w", D), "conv_b": seg("conv_b", XBC),
              "dt_bias": seg("dt_bias", HEADS), "a_log": seg("a_log", HEADS), "d_skip": dsk[:, 0:2].reshape(1, HEADS),
              "ssd_norm_w": seg("ssd_norm_w", DI), "pool_scale": seg("pool_scale", D),
              "norm_mlp_w": seg("norm_mlp_w", D), "norm_final_w": seg("norm_final_w", D)})
    loss = total[0, SMALL_OFF["loss"]]
    conv_cols = conv_w.shape[-1]
    g["conv_w"] = lax.dynamic_slice(seg("conv_w", 4 * XBC).reshape(4, XBC), (0, chip * conv_cols), (4, conv_cols))
    dmod8 = every.reshape(8, SMALL_LEN)[:, SMALL_OFF["dmod"]:SMALL_OFF["dmod"] + 6 * D]
    g["w_ada"] = _w_ada_grad(c8, lax.dynamic_slice(dmod8, (0, chip * ada_cols), (8, ada_cols)))

    got = adamw("w_ada", carry=red.share(["w_in_b"]))
    red.take_share(["w_in_b"], got)
    for n in six:
        g[n] = red.final[n]
    g["w_in"] = jnp.concatenate([red.final["w_in_a"], red.final["w_in_b"]], axis=0)
    for n in ["conv_w", "w_in"] + six:
        adamw(n)
    sizes = [w[n].size for n in _SMALL_REPLICATED]
    n_small = -(-sum(sizes) // 1024) * 1024
    pack = lambda d: jnp.pad(jnp.concatenate([d[n].reshape(1, -1) for n in _SMALL_REPLICATED], axis=1),
                             ((0, 0), (0, n_small - sum(sizes)))).reshape(n_small // 128, 128)
    d_, m_, v_ = _adamw(pack(w), pack(g), pack(m), pack(v), "adamw_small")
    off = 0
    for n, s in zip(_SMALL_REPLICATED, sizes):
        for dst, src in ((delta, d_), (new_m, m_), (new_v, v_)):
            dst[n] = src.reshape(1, n_small)[:, off:off + s]
        off += s

    out = [loss, grad_x.reshape(x.shape)]
    for d in (g, delta, new_m, new_v):
        out += [d[n].reshape(w[n].shape) for n in _WEIGHTS]
    return tuple(out)
```

```python
import functools
import operator

import jax
import jax.numpy as jnp
import numpy as np
from jax import lax
from jax.experimental import pallas as pl
from jax.experimental.pallas import tpu as pltpu

F32, BF16 = jnp.float32, jnp.bfloat16
HIGH = lax.Precision.HIGHEST
MESH = pl.DeviceIdType.MESH

D = 1024
DI = 2048
HEADS, HEAD_DIM = 32, 64
GROUPS, STATE = 4, 128
Q = 128
XBC = DI + 2 * GROUPS * STATE
POOL_WINDOWS = (2, 4, 8, 16)
GW = 256
DFF = 4096
EPS = 1e-5
IN_COLS = 8224
OFF_Z, OFF_XBC, OFF_POOL, OFF_GATE, OFF_DT, NP = 0, 2048, 5120, 6144, 8192, 8448
N_CHIPS = 4
ADAM_LR, ADAM_B1, ADAM_B2, ADAM_EPS, ADAM_WD, ADAM_STEP = 0.001, 0.9, 0.999, 1e-08, 0.01, 10
VMEM_LIMIT = 56 * 2 ** 20
NEG = -1e30


def _sigmoid(v):
    return 0.5 * jnp.tanh(0.5 * v) + 0.5


def _softplus(v):
    return jnp.maximum(v, 0.0) + jnp.log1p(jnp.exp(-jnp.abs(v)))


def _dot(a, b, dims, **kw):
    return lax.dot_general(a, b, (dims, ((), ())), preferred_element_type=F32, **kw)


def _nn(a, b, **kw):
    return _dot(a, b, ((1,), (0,)), **kw)


def _nt(a, b, **kw):
    return _dot(a, b, ((1,), (1,)), **kw)


def _tn(a, b, **kw):
    return _dot(a, b, ((0,), (0,)), **kw)


_DT_IN_CHIP2 = 5120 - 2 * (IN_COLS // 4)


class _Sems:
    def __init__(self, send, recv, local, base=0):
        self._send, self._recv, self._local, self._base = send, recv, local, base

    def shift(self, n):
        return _Sems(self._send, self._recv, self._local, self._base + n)

    def send(self, i):
        return self._send.at[self._base + i]

    def recv(self, i):
        return self._recv.at[self._base + i]

    def local(self, i):
        return self._local.at[self._base + i]


class _Carry:
    def __init__(self, ins, out_shapes, n_sems, start, finish, aliased=()):
        self.ins, self.out_shapes, self.n_sems, self.start, self.finish = list(ins), list(out_shapes), n_sems, start, finish
        self.aliased = list(aliased)


def _join(*carries):
    def run(which):
        def fn(ins, outs, sems):
            i = o = s = 0
            for cy in carries:
                getattr(cy, which)(ins[i:i + len(cy.ins)], outs[o:o + len(cy.out_shapes)], sems.shift(s))
                i, o, s = i + len(cy.ins), o + len(cy.out_shapes), s + cy.n_sems
        return fn

    aliased, i, o = [], 0, 0
    for cy in carries:
        aliased += [(i + a, o + b) for a, b in cy.aliased]
        i, o = i + len(cy.ins), o + len(cy.out_shapes)
    return _Carry([a for cy in carries for a in cy.ins], [a for cy in carries for a in cy.out_shapes],
                  sum(cy.n_sems for cy in carries), run("start"), run("finish"), aliased)


def _call(body, args, *, name, grid=(), in_specs, out_specs, out_shape, scratch_shapes=(), sem=None, aliases=None,
          carry=None):
    in_specs, out_specs, out_shape, scratch_shapes = list(in_specs), list(out_specs), list(out_shape), list(scratch_shapes)
    n_in, n_out, n_scr = len(in_specs), len(out_specs), len(scratch_shapes)
    kw = {"vmem_limit_bytes": VMEM_LIMIT}
    if carry is None:
        kernel_fn = functools.partial(body)
        if sem is not None:
            kw["dimension_semantics"] = sem
    else:
        n_ci, n_co = len(carry.ins), len(carry.out_shapes)
        hbm = pl.BlockSpec(memory_space=pl.ANY)
        in_specs += [hbm] * n_ci
        out_specs += [hbm] * n_co
        out_shape += carry.out_shapes
        n_s = max(carry.n_sems, 1)
        scratch_shapes += [pltpu.SemaphoreType.DMA((n_s,))] * 3
        args = list(args) + carry.ins
        aliases = dict(aliases or {})
        aliases.update({n_in + i: n_out + o for i, o in carry.aliased})
        if grid:
            kw["dimension_semantics"] = ("arbitrary",) * len(grid)

        def kernel_fn(*refs):
            a = n_in
            ins, c_ins = refs[:a], refs[a:a + n_ci]
            a += n_ci
            outs, c_outs = refs[a:a + n_out], refs[a + n_out:a + n_out + n_co]
            a += n_out + n_co
            scr, sems = refs[a:a + n_scr], _Sems(*refs[a + n_scr:a + n_scr + 3])
            if grid:
                ids = [pl.program_id(d) for d in range(len(grid))]
                first = functools.reduce(operator.and_, [i == 0 for i in ids])
                last = functools.reduce(operator.and_, [i == g - 1 for i, g in zip(ids, grid)])

                @pl.when(first)
                def _():
                    carry.start(c_ins, c_outs, sems)

                body(*ins, *outs, *scr)

                @pl.when(last)
                def _():
                    carry.finish(c_ins, c_outs, sems)
            else:
                carry.start(c_ins, c_outs, sems)
                body(*ins, *outs, *scr)
                carry.finish(c_ins, c_outs, sems)

    outs = pl.pallas_call(
        kernel_fn, name=name, grid=grid, in_specs=in_specs, out_specs=out_specs, out_shape=out_shape,
        scratch_shapes=scratch_shapes, input_output_aliases=aliases or {},
        compiler_params=pltpu.CompilerParams(**kw),
    )(*args)
    outs = list(outs)
    return outs if carry is None else (outs[:n_out], outs[n_out:])


def _run_carry(carry, name):
    _, outs = _call(lambda: None, [], name=name, in_specs=[], out_specs=[], out_shape=[], carry=carry)
    return outs


_TILES = {
    "mm_proj": (1024, 2816, 1024), "mm_branch_ssd": (1024, 1024, 2048), "mm_branch_pool": (1024, 1024, 1024),
    "mm_out": (1024, 1024, 1024), "mm_up": (2048, 1024, 1024), "mm_down": (512, 1024, 4096),
    "mm_dact": (2048, 1024, 1024), "mm_g_down": (1024, 1024, 2048), "mm_dh2": (1024, 1024, 4096),
    "mm_g_up": (1024, 1024, 2048), "mm_dmerged": (1024, 1024, 1024), "mm_g_out": (1024, 1024, 2048),
    "mm_dyp": (1024, 1024, 1024), "mm_g_bpool": (1024, 1024, 2048), "mm_g_bssd": (1024, 1024, 2048),
    "mm_dyn": (1024, 1024, 1024), "mm_g_in_a": (768, 1408, 2048), "mm_g_in_b": (256, 2816, 2048),
    "mm_dh1": (1024, 1024, 2816),
}


def _matmul(a, b, *, mode, out_dtypes, name, epi=None, tile_extras=(), row_extras=(), carry=None, a_cols=None,
            chip_blocks=False):
    M, K = (a.shape[1], a.shape[0]) if mode == "tn" else a.shape
    N = b.shape[0] if mode == "nt" else b.shape[1]
    a_start, M = a_cols if a_cols is not None else (0, M)
    tm, tn, tk = _TILES[name]
    tm, tn, tk = min(tm, M), min(tn, N), min(tk, K)
    assert M % tm == 0 and N % tn == 0 and K % tk == 0 and a_start % tm == 0, (name, M, N, K, tm, tn, tk)
    a_off = a_start // tm
    if mode == "nn":
        a_spec = pl.BlockSpec((tm, tk), lambda i, j, k: (i, k))
        b_spec = pl.BlockSpec((tk, tn), lambda i, j, k: (k, j))
        dims = ((1,), (0,))
    elif mode == "nt":
        a_spec = pl.BlockSpec((tm, tk), lambda i, j, k: (i, k))
        b_spec = pl.BlockSpec((tn, tk), lambda i, j, k: (j, k))
        dims = ((1,), (1,))
    else:
        a_spec = pl.BlockSpec((tk, tm), lambda i, j, k: (k, i + a_off))
        b_spec = pl.BlockSpec((tk, tn), lambda i, j, k: (k, j))
        dims = ((0,), (0,))
    nk = K // tk
    n_te, n_re, n_out = len(tile_extras), len(row_extras), len(out_dtypes)
    if epi is None:
        epi = lambda acc: (acc,)

    def body(a_ref, b_ref, *rest):
        extras = rest[:n_te + n_re]
        outs = rest[n_te + n_re:n_te + n_re + n_out]
        p = _dot(a_ref[...], b_ref[...], dims)

        def finish(acc):
            vals = epi(acc, *[e[...] for e in extras])
            for o, v in zip(outs, vals):
                o[...] = v.astype(o.dtype)

        if nk == 1:
            finish(p)
        else:
            acc_ref = rest[-1]
            k = pl.program_id(2)

            @pl.when(k == 0)
            def _():
                acc_ref[...] = p

            @pl.when(k > 0)
            def _():
                acc_ref[...] += p

            @pl.when(k == nk - 1)
            def _():
                finish(acc_ref[...])

    tile_spec = pl.BlockSpec((tm, tn), lambda i, j, k: (i, j))
    row_spec = pl.BlockSpec((1, tn), lambda i, j, k: (0, j))
    out_spec, out_dims = tile_spec, (M, N)
    if chip_blocks:
        assert n_te == 0 and tn * N_CHIPS == N
        out_spec, out_dims = pl.BlockSpec((None, tm, tn), lambda i, j, k: (j, i, 0)), (N_CHIPS, M, tn)
    return _call(
        body, [a, b, *tile_extras, *row_extras], name=name, grid=(M // tm, N // tn, nk),
        in_specs=[a_spec, b_spec] + [tile_spec] * n_te + [row_spec] * n_re, out_specs=[out_spec] * n_out,
        out_shape=[jax.ShapeDtypeStruct(out_dims, dt) for dt in out_dtypes],
        scratch_shapes=[pltpu.VMEM((tm, tn), F32)] if nk > 1 else [],
        sem=("parallel", "parallel", "arbitrary"), carry=carry)


def _row_tile(T):
    return min(512, T)


def _norm_mod(x, nw, scale, shift, name, carry=None):
    T = x.shape[0]
    tr = _row_tile(T)

    def body(x_ref, nw_ref, sc_ref, sh_ref, o_ref):
        xv = x_ref[...]
        r = lax.rsqrt(jnp.mean(xv * xv, axis=-1, keepdims=True) + EPS)
        o_ref[...] = ((xv * r) * nw_ref[...] * (1.0 + sc_ref[...]) + sh_ref[...]).astype(BF16)

    tile = pl.BlockSpec((tr, D), lambda i: (i, 0))
    row = pl.BlockSpec((1, D), lambda i: (0, 0))
    res = _call(body, [x, nw, scale, shift], name=name, grid=(T // tr,), in_specs=[tile, row, row, row],
                out_specs=[tile], out_shape=[jax.ShapeDtypeStruct((T, D), BF16)], sem=("parallel",), carry=carry)
    return res[0] if carry is None else (res[0][0], res[1])


def _norm_mod_bwd(x, dh, dres, nw, scale, name, branch=None, gate=None, carry=None):
    T = x.shape[0]
    tr = _row_tile(T)
    with_branch = branch is not None

    def body(x_ref, dh_ref, dr_ref, nw_ref, sc_ref, *rest):
        if with_branch:
            br_ref, g_ref, dx_ref, sums_ref, db_ref = rest
        else:
            dx_ref, sums_ref = rest
        i = pl.program_id(0)

        @pl.when(i == 0)
        def _():
            sums_ref[...] = jnp.zeros_like(sums_ref)

        xv, dhv = x_ref[...], dh_ref[...]
        r = lax.rsqrt(jnp.mean(xv * xv, axis=-1, keepdims=True) + EPS)
        xn = xv * r
        g1 = dhv * (1.0 + sc_ref[...])
        dxn = g1 * nw_ref[...]
        dx = dr_ref[...] + r * (dxn - xn * jnp.mean(dxn * xn, axis=-1, keepdims=True))
        dx_ref[...] = dx
        sums_ref[0:1, :] += jnp.sum(dhv, axis=0, keepdims=True)
        sums_ref[1:2, :] += jnp.sum(dhv * (xn * nw_ref[...]), axis=0, keepdims=True)
        sums_ref[2:3, :] += jnp.sum(g1 * xn, axis=0, keepdims=True)
        if with_branch:
            db_ref[...] = (dx * g_ref[...]).astype(BF16)
            sums_ref[3:4, :] += jnp.sum(dx * br_ref[...], axis=0, keepdims=True)

    tile = pl.BlockSpec((tr, D), lambda i: (i, 0))
    row = pl.BlockSpec((1, D), lambda i: (0, 0))
    sums = pl.BlockSpec((8, D), lambda i: (0, 0))
    ins = [x, dh, dres, nw, scale] + ([branch, gate] if with_branch else [])
    in_specs = [tile, tile, tile, row, row] + ([tile, row] if with_branch else [])
    out_specs = [tile, sums] + ([tile] if with_branch else [])
    out_shape = [jax.ShapeDtypeStruct((T, D), F32), jax.ShapeDtypeStruct((8, D), F32)]
    if with_branch:
        out_shape.append(jax.ShapeDtypeStruct((T, D), BF16))
    return _call(body, ins, name=name, grid=(T // tr,), in_specs=in_specs, out_specs=out_specs, out_shape=out_shape,
                 sem=("arbitrary",), carry=carry)


def _final_loss_bwd(x3, target, wf, down, gate_f):
    T = x3.shape[0]
    tr = _row_tile(T)
    n_steps = T // tr

    def body(x_ref, t_ref, w_ref, dn_ref, g_ref, dx_ref, dd_ref, sums_ref):
        i = pl.program_id(0)

        @pl.when(i == 0)
        def _():
            sums_ref[...] = jnp.zeros_like(sums_ref)

        xv = x_ref[...]
        r = lax.rsqrt(jnp.mean(xv * xv, axis=-1, keepdims=True) + EPS)
        xn = xv * r
        err = xn * w_ref[...] - t_ref[...]
        dy = err * (1.0 / D)
        dxn = dy * w_ref[...]
        dx = r * (dxn - xn * jnp.mean(dxn * xn, axis=-1, keepdims=True))
        dx_ref[...] = dx
        dd_ref[...] = (dx * g_ref[...]).astype(BF16)
        sums_ref[0:1, :] += jnp.sum(dy * xn, axis=0, keepdims=True)
        sums_ref[1:2, :] += jnp.sum(dx * dn_ref[...], axis=0, keepdims=True)
        sums_ref[2:3, :] += jnp.sum(err * err, axis=0, keepdims=True) * (0.5 / D)

        @pl.when(i == n_steps - 1)
        def _():
            sums_ref[3:4, :] = jnp.broadcast_to(jnp.sum(sums_ref[2:3, :], axis=1, keepdims=True), (1, D))

    tile = pl.BlockSpec((tr, D), lambda i: (i, 0))
    row = pl.BlockSpec((1, D), lambda i: (0, 0))
    sums = pl.BlockSpec((8, D), lambda i: (0, 0))
    return _call(body, [x3, target, wf, down, gate_f], name="final_loss_bwd", grid=(n_steps,),
                 in_specs=[tile, tile, row, tile, row], out_specs=[tile, tile, sums],
                 out_shape=[jax.ShapeDtypeStruct((T, D), F32), jax.ShapeDtypeStruct((T, D), BF16),
                            jax.ShapeDtypeStruct((8, D), F32)], sem=("arbitrary",))


CONV_TC = 1024


def _conv_taps(xp, w, b):
    acc = b + w[3:4, :] * xp
    for k in range(3):
        acc = acc + w[k:k + 1, :] * pltpu.roll(xp, 3 - k, 0)
    return acc


def _conv_fwd(proj, conv_w, conv_b):
    T = proj.shape[0]
    tr = _row_tile(T)
    nb, offb = tr // 8, OFF_XBC // CONV_TC

    def body(x_ref, h_ref, w_ref, b_ref, o_ref):
        halo = jnp.where(pl.program_id(0) > 0, h_ref[...], 0.0)
        xp = jnp.concatenate([halo, x_ref[...]], axis=0)
        acc = _conv_taps(xp, w_ref[...], b_ref[...])[8:]
        o_ref[...] = acc * _sigmoid(acc)

    return _call(
        body, [proj, proj, conv_w, conv_b], name="conv_fwd", grid=(T // tr, XBC // CONV_TC),
        in_specs=[pl.BlockSpec((tr, CONV_TC), lambda i, j: (i, j + offb)),
                  pl.BlockSpec((8, CONV_TC), lambda i, j: (jnp.maximum(i * nb - 1, 0), j + offb)),
                  pl.BlockSpec((4, CONV_TC), lambda i, j: (0, j)),
                  pl.BlockSpec((1, CONV_TC), lambda i, j: (0, j))],
        out_specs=[pl.BlockSpec((tr, CONV_TC), lambda i, j: (i, j))],
        out_shape=[jax.ShapeDtypeStruct((T, XBC), F32)], sem=("parallel", "parallel"))[0]


def _conv_bwd_a(dxa, proj, conv_w, conv_b):
    T = proj.shape[0]
    tr = _row_tile(T)
    nb, offb = tr // 8, OFF_XBC // CONV_TC

    def body(d_ref, x_ref, h_ref, w_ref, b_ref, o_ref, sums_ref):
        i = pl.program_id(1)

        @pl.when(i == 0)
        def _():
            sums_ref[...] = jnp.zeros_like(sums_ref)

        halo = jnp.where(i > 0, h_ref[...], 0.0)
        xp = jnp.concatenate([halo, x_ref[...]], axis=0)
        w = w_ref[...]
        taps = [pltpu.roll(xp, 3 - k, 0)[8:] for k in range(3)] + [x_ref[...]]
        acc = b_ref[...] + w[3:4, :] * taps[3]
        for k in range(3):
            acc = acc + w[k:k + 1, :] * taps[k]
        s = _sigmoid(acc)
        dxc = d_ref[...] * (s * (1.0 + acc * (1.0 - s)))
        o_ref[...] = dxc
        for k in range(4):
            sums_ref[k:k + 1, :] += jnp.sum(dxc * taps[k], axis=0, keepdims=True)
        sums_ref[4:5, :] += jnp.sum(dxc, axis=0, keepdims=True)

    return _call(
        body, [dxa, proj, proj, conv_w, conv_b], name="conv_bwd_a", grid=(XBC // CONV_TC, T // tr),
        in_specs=[pl.BlockSpec((tr, CONV_TC), lambda j, i: (i, j)),
                  pl.BlockSpec((tr, CONV_TC), lambda j, i: (i, j + offb)),
                  pl.BlockSpec((8, CONV_TC), lambda j, i: (jnp.maximum(i * nb - 1, 0), j + offb)),
                  pl.BlockSpec((4, CONV_TC), lambda j, i: (0, j)),
                  pl.BlockSpec((1, CONV_TC), lambda j, i: (0, j))],
        out_specs=[pl.BlockSpec((tr, CONV_TC), lambda j, i: (i, j)), pl.BlockSpec((8, CONV_TC), lambda j, i: (0, j))],
        out_shape=[jax.ShapeDtypeStruct((T, XBC), F32), jax.ShapeDtypeStruct((8, XBC), F32)],
        sem=("parallel", "arbitrary"))


def _conv_bwd_b(dxc, conv_w, dproj):
    T = dxc.shape[0]
    tr = _row_tile(T)
    nb, offb, last = tr // 8, OFF_XBC // CONV_TC, T // tr - 1

    def body(d_ref, h_ref, w_ref, dp_in, o_ref):
        del dp_in
        halo = jnp.where(pl.program_id(0) < last, h_ref[...], 0.0)
        xp = jnp.concatenate([d_ref[...], halo], axis=0)
        n = xp.shape[0]
        w = w_ref[...]
        acc = w[3:4, :] * xp
        for k in range(3):
            acc = acc + w[k:k + 1, :] * pltpu.roll(xp, n - (3 - k), 0)
        o_ref[...] = acc[:tr].astype(BF16)

    return _call(
        body, [dxc, dxc, conv_w, dproj], name="conv_bwd_b", grid=(T // tr, XBC // CONV_TC),
        in_specs=[pl.BlockSpec((tr, CONV_TC), lambda i, j: (i, j)),
                  pl.BlockSpec((8, CONV_TC), lambda i, j: (jnp.minimum((i + 1) * nb, T // 8 - 1), j)),
                  pl.BlockSpec((4, CONV_TC), lambda i, j: (0, j)),
                  pl.BlockSpec(memory_space=pl.ANY)],
        out_specs=[pl.BlockSpec((tr, CONV_TC), lambda i, j: (i, j + offb))],
        out_shape=[jax.ShapeDtypeStruct(dproj.shape, BF16)], aliases={3: 0}, sem=("parallel", "parallel"))[0]


def _spread(v, sel, pieces):
    out = None
    for _ in range(pieces):
        p = v.astype(BF16)
        term = _nn(p, sel)
        out = term if out is None else out + term
        v = v - p.astype(F32)
    return out


def _ssd_selectors():
    g = np.arange(GROUPS)[:, None, None]
    piece = np.arange(128)[None, :, None]
    h = np.where(piece < 3 * HEADS, piece % HEADS, -1)
    blocks = (h == 8 * g + np.arange(1024)[None, None, :] // 128)
    pairs = (h == 8 * g + np.arange(512)[None, None, :] // HEAD_DIM)
    lane = np.arange(128)[None, None, :]
    block_sum = (lane == 8 * g + np.arange(1024)[None, :, None] // 128)
    pair_sum = (lane == 8 * g + np.arange(512)[None, :, None] // HEAD_DIM)
    return [jnp.asarray(m, BF16) for m in (blocks, pairs, block_sum, pair_sum)]


def _pack3(v):
    p0 = v.astype(BF16)
    r1 = v - p0.astype(F32)
    p1 = r1.astype(BF16)
    r2 = r1 - p1.astype(F32)
    return p0 + pltpu.roll(r1, HEADS, 1).astype(BF16) + pltpu.roll(r2, 2 * HEADS, 1).astype(BF16)


def _ssd_group(g, cs_p, csT, dt_p, s_mat, causal_w, lo, blocks_ref, pairs_ref):
    csb = _nn(cs_p, blocks_ref[g])
    row = jnp.concatenate([csT[8 * g + hh:8 * g + hh + 1, :] for hh in range(8)], axis=1)
    l_w = jnp.exp(jnp.where(causal_w, csb - row, NEG))
    m_w = jnp.concatenate([s_mat] * 8, axis=1) * l_w
    cs_g = jnp.concatenate([jnp.where(lo, csb[:, 256 * jj:256 * jj + 128], csb[:, 256 * jj + 128:256 * jj + 256])
                            for jj in range(4)], axis=1)
    cs_last = cs_g[Q - 1:Q, :]
    return m_w, l_w, _nn(dt_p, pairs_ref[g]), jnp.exp(cs_g), jnp.exp(cs_last - cs_g), jnp.exp(cs_last)


def _ssd_common(dtp_ref, dtb_r, alog_r, dtb_c, alog_c):
    rows = lax.broadcasted_iota(jnp.int32, (Q, Q), 0)
    cols = lax.broadcasted_iota(jnp.int32, (Q, Q), 1)
    tri = (cols <= rows).astype(F32)
    heads = lax.broadcasted_iota(jnp.int32, (1, 128), 1) < HEADS
    raw_w = dtp_ref[...] + dtb_r[...]
    dt_w = jnp.where(heads, _softplus(raw_w), 0.0)
    a_w = -jnp.exp(alog_r[...])
    cs_w = _nn(tri, dt_w * a_w, precision=HIGH)
    aT = _softplus(dtp_ref[...].T[0:HEADS, :] + dtb_c[...]) * (-jnp.exp(alog_c[...]))
    csT = _nt(aT, tri, precision=HIGH)
    return raw_w[:, 0:HEADS], dt_w[:, 0:HEADS], a_w[:, 0:HEADS], csT, _pack3(cs_w), _pack3(dt_w)


def _ssd_fwd(xbc_a, proj, dtb_r, alog_r, dtb_c, alog_c, dsk_exp):
    T = xbc_a.shape[0]
    nc = T // Q
    dtb_r, alog_r = [jnp.pad(a, ((0, 0), (0, 128 - HEADS))) for a in (dtb_r, alog_r)]

    def body(xbc_ref, dtp_ref, dtb_r_ref, alog_r_ref, dtb_c_ref, alog_c_ref, dsk_ref, blocks_ref, pairs_ref,
             y_ref, hin_ref, h_scr):
        @pl.when(pl.program_id(0) == 0)
        def _():
            h_scr[...] = jnp.zeros_like(h_scr)

        _, _, _, csT, cs_p, dt_p = _ssd_common(dtp_ref, dtb_r_ref, alog_r_ref, dtb_c_ref, alog_c_ref)
        lo = lax.broadcasted_iota(jnp.int32, (1, 128), 1) < HEAD_DIM
        hi = jnp.logical_not(lo)
        causal_w = (lax.broadcasted_iota(jnp.int32, (Q, 1024), 1) & (Q - 1)) <= lax.broadcasted_iota(jnp.int32, (Q, 1024), 0)
        for g in range(GROUPS):
            gs = slice(512 * g, 512 * (g + 1))
            hs = slice(128 * g, 128 * (g + 1))
            xs_g = xbc_ref[:, gs]
            b_g = xbc_ref[:, DI + STATE * g:DI + STATE * (g + 1)].astype(BF16)
            c_g = xbc_ref[:, DI + 512 + STATE * g:DI + 512 + STATE * (g + 1)].astype(BF16)
            m_w, _, dt_g, ecs_g, dec_g, cd_g = _ssd_group(g, cs_p, csT, dt_p, _nt(c_g, b_g), causal_w, lo, blocks_ref, pairs_ref)
            m_b = m_w.astype(BF16)
            xdt = xs_g * dt_g
            xdt_b = xdt.astype(BF16)
            ys = []
            for jj in range(4):
                xp = xdt_b[:, 128 * jj:128 * (jj + 1)]
                x_ab = jnp.concatenate([jnp.where(lo, xp, jnp.zeros_like(xp)), jnp.where(hi, xp, jnp.zeros_like(xp))], axis=0)
                ys.append(_nn(m_b[:, 256 * jj:256 * (jj + 1)], x_ab))
            h_g = h_scr[hs, :]
            hin_ref[0, hs, :] = h_g
            y_ref[:, gs] = jnp.concatenate(ys, axis=1) + _nn(c_g, h_g.astype(BF16)) * ecs_g + dsk_ref[:, gs] * xs_g
            h_scr[hs, :] = h_g * cd_g + _tn(b_g, (xdt * dec_g).astype(BF16))

    small_r = pl.BlockSpec((1, 128), lambda c: (0, 0))
    small_c = pl.BlockSpec((HEADS, 1), lambda c: (0, 0))
    blocks, pairs, _, _ = _ssd_selectors()
    whole = lambda a: pl.BlockSpec(a.shape, lambda c: (0,) * a.ndim)
    return _call(
        body, [xbc_a, proj, dtb_r, alog_r, dtb_c, alog_c, dsk_exp, blocks, pairs], name="ssd_fwd", grid=(nc,),
        in_specs=[pl.BlockSpec((Q, XBC), lambda c: (c, 0)),
                  pl.BlockSpec((Q, 128), lambda c: (c, OFF_DT // 128)),
                  small_r, small_r, small_c, small_c,
                  pl.BlockSpec((1, DI), lambda c: (0, 0)), whole(blocks), whole(pairs)],
        out_specs=[pl.BlockSpec((Q, DI), lambda c: (c, 0)), pl.BlockSpec((1, 512, 512), lambda c: (c, 0, 0))],
        out_shape=[jax.ShapeDtypeStruct((T, DI), F32), jax.ShapeDtypeStruct((nc, 512, 512), F32)],
        scratch_shapes=[pltpu.VMEM((512, 512), F32)], sem=("arbitrary",))


def _ssd_bwd(dy, xbc_a, proj, hin, dtb_r, alog_r, dtb_c, alog_c, dsk_exp, dproj, carry=None):
    T = xbc_a.shape[0]
    nc = T // Q
    dtb_r, alog_r = [jnp.pad(a, ((0, 0), (0, 128 - HEADS))) for a in (dtb_r, alog_r)]

    def body(dy_ref, xbc_ref, dtp_ref, hin_ref, dtb_r_ref, alog_r_ref, dtb_c_ref, alog_c_ref, dsk_ref, dp_in,
             blocks_ref, pairs_ref, block_sum_ref, pair_sum_ref, dxa_ref, dp_ref, dsk_sum_ref, small_ref, dh_scr):
        del dp_in

        @pl.when(pl.program_id(0) == 0)
        def _():
            dh_scr[...] = jnp.zeros_like(dh_scr)
            dsk_sum_ref[...] = jnp.zeros_like(dsk_sum_ref)
            small_ref[...] = jnp.zeros_like(small_ref)

        raw, dt, a_r, csT, cs_p, dt_p = _ssd_common(dtp_ref, dtb_r_ref, alog_r_ref, dtb_c_ref, alog_c_ref)
        lo = lax.broadcasted_iota(jnp.int32, (1, 128), 1) < HEAD_DIM
        hi = jnp.logical_not(lo)
        sub32 = lax.broadcasted_iota(jnp.int32, (HEADS, 1), 0)
        causal_w = (lax.broadcasted_iota(jnp.int32, (Q, 1024), 1) & (Q - 1)) <= lax.broadcasted_iota(jnp.int32, (Q, 1024), 0)
        dcs_c = jnp.zeros((Q, 128), F32)
        dcs_r = jnp.zeros((HEADS, Q), F32)
        dcs_l = jnp.zeros((8, 128), F32)
        ddt_x = jnp.zeros((Q, 128), F32)
        for g in range(GROUPS):
            gs = slice(512 * g, 512 * (g + 1))
            hs = slice(128 * g, 128 * (g + 1))
            xs_g, dy_g = xbc_ref[:, gs], dy_ref[:, gs]
            b_g = xbc_ref[:, DI + STATE * g:DI + STATE * (g + 1)].astype(BF16)
            c_g = xbc_ref[:, DI + 512 + STATE * g:DI + 512 + STATE * (g + 1)].astype(BF16)
            m_w, l_w, dt_g, ecs_g, dec_g, cd_g = _ssd_group(g, cs_p, csT, dt_p, _nt(c_g, b_g), causal_w, lo, blocks_ref, pairs_ref)
            m_b = m_w.astype(BF16)
            xdt = xs_g * dt_g
            xdt_b, dy_b = xdt.astype(BF16), dy_g.astype(BF16)
            dms, dxs = [], []
            for jj in range(4):
                xp, dyp = xdt_b[:, 128 * jj:128 * (jj + 1)], dy_b[:, 128 * jj:128 * (jj + 1)]
                dy_ab = jnp.concatenate([jnp.where(lo, dyp, jnp.zeros_like(dyp)), jnp.where(hi, dyp, jnp.zeros_like(dyp))], axis=0)
                dm_ab = _nt(dy_ab, xp)
                dms += [dm_ab[:Q], dm_ab[Q:]]
                dx_ab = _tn(m_b[:, 256 * jj:256 * (jj + 1)], dyp)
                dxs.append(jnp.where(lo, dx_ab[:Q], dx_ab[Q:]))
            dm_w = jnp.concatenate(dms, axis=1)
            w_w = dm_w * m_w
            dcs_c = dcs_c + _spread(w_w, block_sum_ref[g], 2)
            w_cols = jnp.sum(w_w, axis=0, keepdims=True)
            for hh in range(8):
                dcs_r = dcs_r + jnp.where(sub32 == 8 * g + hh, w_cols[:, 128 * hh:128 * (hh + 1)], 0.0)
            dl_w = dm_w * l_w
            ds_mat = dl_w[:, 0:128]
            for hh in range(1, 8):
                ds_mat = ds_mat + dl_w[:, 128 * hh:128 * (hh + 1)]
            hin_g = hin_ref[0, hs, :]
            hin_b = hin_g.astype(BF16)
            dh_g = dh_scr[hs, :]
            dh_b = dh_g.astype(BF16)
            g_mat = _nn(b_g, dh_b)
            xdec = xdt * dec_g
            xg = xdec * g_mat
            dxdt = jnp.concatenate(dxs, axis=1) + dec_g * g_mat
            sums = _spread(jnp.concatenate([dy_g * (_nn(c_g, hin_b) * ecs_g) - xg, dxdt * xs_g], axis=0), pair_sum_ref[g], 2)
            dcs_c = dcs_c + sums[:Q]
            ddt_x = ddt_x + sums[Q:]
            last = jnp.sum(xg, axis=0, keepdims=True) + jnp.sum(dh_g * hin_g, axis=0, keepdims=True) * cd_g
            dcs_l = dcs_l + _spread(jnp.broadcast_to(last, (8, 512)), pair_sum_ref[g], 2)
            dz = (dy_g * ecs_g).astype(BF16)
            ds_b = ds_mat.astype(BF16)
            dxa_ref[:, gs] = dxdt * dt_g + dy_g * dsk_ref[:, gs]
            dxa_ref[:, DI + STATE * g:DI + STATE * (g + 1)] = _nt(xdec.astype(BF16), dh_b) + _tn(ds_b, c_g)
            dxa_ref[:, DI + 512 + STATE * g:DI + 512 + STATE * (g + 1)] = _nt(dz, hin_b) + _nn(ds_b, b_g)
            dh_scr[hs, :] = _tn(c_g, dz) + dh_g * cd_g
            dsk_sum_ref[0:1, gs] += jnp.sum(dy_g * xs_g, axis=0, keepdims=True)

        rows = lax.broadcasted_iota(jnp.int32, (Q, Q), 0)
        cols = lax.broadcasted_iota(jnp.int32, (Q, Q), 1)
        tri_t = (cols >= rows).astype(F32)
        last_row = lax.broadcasted_iota(jnp.int32, (Q, 1), 0) == Q - 1
        dcs = (dcs_c + jnp.where(last_row, dcs_l[0:1, :], 0.0))[:, 0:HEADS]
        da = _nn(tri_t, dcs, precision=HIGH) - _nt(tri_t, dcs_r, precision=HIGH)
        ddt_raw = (ddt_x[:, 0:HEADS] + da * a_r) * _sigmoid(raw)
        small_ref[0:1, :] += jnp.sum(da * dt, axis=0, keepdims=True) * a_r
        small_ref[1:2, :] += jnp.sum(ddt_raw, axis=0, keepdims=True)
        dp_ref[...] = jnp.zeros_like(dp_ref)
        dp_ref[:, 0:HEADS] = ddt_raw.astype(BF16)

    rev = lambda c: nc - 1 - c
    small_r = pl.BlockSpec((1, 128), lambda c: (0, 0))
    small_c = pl.BlockSpec((HEADS, 1), lambda c: (0, 0))
    selectors = _ssd_selectors()
    whole = lambda a: pl.BlockSpec(a.shape, lambda c: (0,) * a.ndim)
    return _call(
        body, [dy, xbc_a, proj, hin, dtb_r, alog_r, dtb_c, alog_c, dsk_exp, dproj, *selectors], name="ssd_bwd", grid=(nc,),
        in_specs=[pl.BlockSpec((Q, DI), lambda c: (rev(c), 0)),
                  pl.BlockSpec((Q, XBC), lambda c: (rev(c), 0)),
                  pl.BlockSpec((Q, 128), lambda c: (rev(c), OFF_DT // 128)),
                  pl.BlockSpec((1, 512, 512), lambda c: (rev(c), 0, 0)),
                  small_r, small_r, small_c, small_c,
                  pl.BlockSpec((1, DI), lambda c: (0, 0)),
                  pl.BlockSpec(memory_space=pl.ANY)] + [whole(a) for a in selectors],
        out_specs=[pl.BlockSpec((Q, XBC), lambda c: (rev(c), 0)),
                   pl.BlockSpec((Q, 256), lambda c: (rev(c), OFF_DT // 256)),
                   pl.BlockSpec((8, DI), lambda c: (0, 0)),
                   pl.BlockSpec((8, HEADS), lambda c: (0, 0))],
        out_shape=[jax.ShapeDtypeStruct((T, XBC), F32), jax.ShapeDtypeStruct(dproj.shape, BF16),
                   jax.ShapeDtypeStruct((8, DI), F32), jax.ShapeDtypeStruct((8, HEADS), F32)],
        aliases={9: 1}, scratch_shapes=[pltpu.VMEM((512, 512), F32)], sem=("arbitrary",), carry=carry)


def _gate_norm(y, proj, w):
    T = y.shape[0]
    tr = _row_tile(T)

    def body(y_ref, z_ref, w_ref, o_ref):
        for g in range(GROUPS):
            gs = slice(512 * g, 512 * (g + 1))
            z = z_ref[:, gs]
            yg = y_ref[:, gs] * (z * _sigmoid(z))
            r = lax.rsqrt(jnp.mean(yg * yg, axis=-1, keepdims=True) + EPS)
            o_ref[:, gs] = (yg * r * w_ref[:, gs]).astype(BF16)

    tile = pl.BlockSpec((tr, DI), lambda i: (i, 0))
    return _call(body, [y, proj, w], name="gate_norm", grid=(T // tr,),
                 in_specs=[tile, tile, pl.BlockSpec((1, DI), lambda i: (0, 0))], out_specs=[tile],
                 out_shape=[jax.ShapeDtypeStruct((T, DI), BF16)], sem=("parallel",))[0]


def _gate_norm_bwd(dyn, y, proj, w, dproj):
    T = y.shape[0]
    tr = _row_tile(T)

    def body(d_ref, y_ref, z_ref, w_ref, dp_in, dy_ref, dz_ref, sums_ref):
        del dp_in

        @pl.when(pl.program_id(0) == 0)
        def _():
            sums_ref[...] = jnp.zeros_like(sums_ref)

        for g in range(GROUPS):
            gs = slice(512 * g, 512 * (g + 1))
            z, yv, d = z_ref[:, gs], y_ref[:, gs], d_ref[:, gs]
            s = _sigmoid(z)
            silu = z * s
            yg = yv * silu
            r = lax.rsqrt(jnp.mean(yg * yg, axis=-1, keepdims=True) + EPS)
            yn = yg * r
            sums_ref[0:1, gs] += jnp.sum(d * yn, axis=0, keepdims=True)
            dn = d * w_ref[:, gs]
            dyg = r * (dn - yn * jnp.mean(dn * yn, axis=-1, keepdims=True))
            dy_ref[:, gs] = dyg * silu
            dz_ref[:, gs] = (dyg * yv * (s * (1.0 + z * (1.0 - s)))).astype(BF16)

    tile = pl.BlockSpec((tr, DI), lambda i: (i, 0))
    return _call(
        body, [dyn, y, proj, w, dproj], name="gate_norm_bwd", grid=(T // tr,),
        in_specs=[tile, tile, tile, pl.BlockSpec((1, DI), lambda i: (0, 0)), pl.BlockSpec(memory_space=pl.ANY)],
        out_specs=[tile, tile, pl.BlockSpec((8, DI), lambda i: (0, 0))],
        out_shape=[jax.ShapeDtypeStruct((T, DI), F32), jax.ShapeDtypeStruct(dproj.shape, BF16),
                   jax.ShapeDtypeStruct((8, DI), F32)],
        aliases={4: 1}, sem=("arbitrary",))


def _pool_fwd(proj, pool_w_b, pool_scale):
    T = proj.shape[0]
    tr = _row_tile(T)
    nb = tr // 16

    def body(u_ref, h_ref, pw_ref, ps_ref, pooled_ref, pw_out_ref, yps_ref):
        i = pl.program_id(0)
        t = i * tr + lax.broadcasted_iota(jnp.int32, (tr, 1), 0)
        for g, win in enumerate(POOL_WINDOWS):
            gs = slice(GW * g, GW * (g + 1))
            u = u_ref[:, gs]
            s = jnp.concatenate([jnp.where(i > 0, h_ref[:, gs], 0.0), u], axis=0)
            sh = 1
            while sh < win:
                s = s + pltpu.roll(s, sh, 0)
                sh *= 2
            pooled = (s[16:] * (1.0 / jnp.minimum(t + 1, win).astype(F32)) - u).astype(BF16)
            pooled_ref[:, gs] = pooled
            pwv = _nn(pooled, pw_ref[g])
            pw_out_ref[:, gs] = pwv
            yps_ref[:, gs] = (pwv * ps_ref[:, gs]).astype(BF16)

    tile = pl.BlockSpec((tr, D), lambda i: (i, 0))
    return _call(
        body, [proj, proj, pool_w_b, pool_scale], name="pool_fwd", grid=(T // tr,),
        in_specs=[pl.BlockSpec((tr, D), lambda i: (i, OFF_POOL // D)),
                  pl.BlockSpec((16, D), lambda i: (jnp.maximum(i * nb - 1, 0), OFF_POOL // D)),
                  pl.BlockSpec((4, GW, GW), lambda i: (0, 0, 0)),
                  pl.BlockSpec((1, D), lambda i: (0, 0))],
        out_specs=[tile, tile, tile],
        out_shape=[jax.ShapeDtypeStruct((T, D), BF16), jax.ShapeDtypeStruct((T, D), F32),
                   jax.ShapeDtypeStruct((T, D), BF16)], sem=("parallel",))


def _pool_bwd(dyp, pw_out, pooled, pool_w_b, pool_scale, dproj):
    T = dyp.shape[0]
    tr = _row_tile(T)
    nb, last = tr // 16, T // tr - 1

    def body(d_ref, h_ref, pwo_ref, pooled_ref, pw_ref, ps_ref, dp_in, du_ref, gpw_ref, sums_ref):
        del dp_in
        i = pl.program_id(0)

        @pl.when(i == 0)
        def _():
            gpw_ref[...] = jnp.zeros_like(gpw_ref)
            sums_ref[...] = jnp.zeros_like(sums_ref)

        n = tr + 16
        t = i * tr + lax.broadcasted_iota(jnp.int32, (n, 1), 0)
        sums_ref[0:1, :] += jnp.sum(d_ref[...] * pwo_ref[...], axis=0, keepdims=True)
        for g, win in enumerate(POOL_WINDOWS):
            gs = slice(GW * g, GW * (g + 1))
            d_ext = jnp.concatenate([d_ref[:, gs], jnp.where(i < last, h_ref[:, gs], 0.0)], axis=0)
            dpw = (d_ext * ps_ref[:, gs]).astype(BF16)
            dpooled = _nt(dpw, pw_ref[g])
            s = jnp.where(t < T, dpooled * (1.0 / jnp.minimum(t + 1, win).astype(F32)), 0.0)
            sh = 1
            while sh < win:
                s = s + pltpu.roll(s, n - sh, 0)
                sh *= 2
            du_ref[:, gs] = (s[:tr] - dpooled[:tr]).astype(BF16)
            gpw_ref[g] += _tn(pooled_ref[:, gs], dpw[:tr])

    tile = pl.BlockSpec((tr, D), lambda i: (i, 0))
    return _call(
        body, [dyp, dyp, pw_out, pooled, pool_w_b, pool_scale, dproj], name="pool_bwd", grid=(T // tr,),
        in_specs=[tile, pl.BlockSpec((16, D), lambda i: (jnp.minimum((i + 1) * nb, T // 16 - 1), 0)), tile, tile,
                  pl.BlockSpec((4, GW, GW), lambda i: (0, 0, 0)), pl.BlockSpec((1, D), lambda i: (0, 0)),
                  pl.BlockSpec(memory_space=pl.ANY)],
        out_specs=[pl.BlockSpec((tr, D), lambda i: (i, OFF_POOL // D)),
                   pl.BlockSpec((4, GW, GW), lambda i: (0, 0, 0)), pl.BlockSpec((8, D), lambda i: (0, 0))],
        out_shape=[jax.ShapeDtypeStruct(dproj.shape, BF16), jax.ShapeDtypeStruct((4, GW, GW), F32),
                   jax.ShapeDtypeStruct((8, D), F32)],
        aliases={6: 0}, sem=("arbitrary",))


def _merge(proj, y_ssd, y_pool):
    T = proj.shape[0]
    tr = _row_tile(T)

    def body(g_ref, a_ref, b_ref, o_ref):
        o_ref[...] = (_sigmoid(g_ref[:, 0:D]) * a_ref[...] + _sigmoid(g_ref[:, D:2 * D]) * b_ref[...]).astype(BF16)

    tile = pl.BlockSpec((tr, D), lambda i: (i, 0))
    return _call(body, [proj, y_ssd, y_pool], name="merge", grid=(T // tr,),
                 in_specs=[pl.BlockSpec((tr, 2 * D), lambda i: (i, OFF_GATE // (2 * D))), tile, tile], out_specs=[tile],
                 out_shape=[jax.ShapeDtypeStruct((T, D), BF16)], sem=("parallel",))[0]


def _merge_bwd(dmerged, proj, y_ssd, y_pool):
    T = proj.shape[0]
    tr = _row_tile(T)

    def body(d_ref, g_ref, a_ref, b_ref, da_ref, db_ref, dg_ref):
        d = d_ref[...]
        ga, gb = _sigmoid(g_ref[:, 0:D]), _sigmoid(g_ref[:, D:2 * D])
        da_ref[...] = (d * ga).astype(BF16)
        db_ref[...] = (d * gb).astype(BF16)
        dg_ref[:, 0:D] = (d * a_ref[...] * ga * (1.0 - ga)).astype(BF16)
        dg_ref[:, D:2 * D] = (d * b_ref[...] * gb * (1.0 - gb)).astype(BF16)

    tile = pl.BlockSpec((tr, D), lambda i: (i, 0))
    gates = pl.BlockSpec((tr, 2 * D), lambda i: (i, OFF_GATE // (2 * D)))
    return _call(body, [dmerged, proj, y_ssd, y_pool], name="merge_bwd", grid=(T // tr,),
                 in_specs=[tile, gates, tile, tile], out_specs=[tile, tile, gates],
                 out_shape=[jax.ShapeDtypeStruct((T, D), BF16), jax.ShapeDtypeStruct((T, D), BF16),
                            jax.ShapeDtypeStruct((T, NP), BF16)], sem=("parallel",))


def _adamw(w, g, m, v, name, carry=None):
    R, C = w.shape
    tr = R if R <= 128 else 128
    assert R % tr == 0

    def body(w_ref, g_ref, m_ref, v_ref, d_ref, mo_ref, vo_ref):
        gv = g_ref[...]
        mn = ADAM_B1 * m_ref[...] + (1.0 - ADAM_B1) * gv
        vn = ADAM_B2 * v_ref[...] + (1.0 - ADAM_B2) * (gv * gv)
        m_hat = mn * (1.0 / (1.0 - ADAM_B1 ** ADAM_STEP))
        v_hat = vn * (1.0 / (1.0 - ADAM_B2 ** ADAM_STEP))
        d_ref[...] = -ADAM_LR * (m_hat / (jnp.sqrt(v_hat) + ADAM_EPS) + ADAM_WD * w_ref[...])
        mo_ref[...] = mn
        vo_ref[...] = vn

    tile = pl.BlockSpec((tr, C), lambda i: (i, 0))
    sds = jax.ShapeDtypeStruct((R, C), F32)
    return _call(body, [w, g, m, v], name=name, grid=(R // tr,), in_specs=[tile] * 4, out_specs=[tile] * 3,
                 out_shape=[sds] * 3, sem=("parallel",), carry=carry)


def _me():
    return lax.axis_index("x"), lax.axis_index("y"), lax.axis_index("c")


def _xor_peer(x, y, c, p):
    return (x ^ ((p >> 2) & 1), y ^ ((p >> 1) & 1), c ^ (p & 1))


def _ada_fwd(c_row, w_ada, b_ada_mine, carry=None):
    n_cols = w_ada.shape[1]

    def body(c_ref, w_ref, b_ref, mod_ref, c8_ref, csend, mpart, modbuf, send_sems, recv_sems):
        x, y, c = _me()
        me = 4 * x + 2 * y + c
        chip = 2 * x + y
        csend[...] = jnp.broadcast_to(c_ref[...], csend.shape)
        c8_ref[me] = csend[...]

        def c_copy(p):
            return pltpu.make_async_remote_copy(
                src_ref=csend, dst_ref=c8_ref.at[me], send_sem=send_sems.at[p - 1], recv_sem=recv_sems.at[p - 1],
                device_id=_xor_peer(x, y, c, p), device_id_type=MESH)

        for p in range(1, 8):
            c_copy(p).start()
        for p in range(1, 8):
            c_copy(p).wait_recv()
        cs = jnp.concatenate([c8_ref[d][0:1, :] for d in range(8)], axis=0)
        mpart[...] = _nn(cs * _sigmoid(cs), w_ref[...], precision=HIGH) + b_ref[...]
        modbuf[chip] = mpart[...]

        def m_copy(m):
            return pltpu.make_async_remote_copy(
                src_ref=mpart, dst_ref=modbuf.at[chip], send_sem=send_sems.at[6 + m], recv_sem=recv_sems.at[6 + m],
                device_id=_xor_peer(x, y, c, 2 * m), device_id_type=MESH)

        for m in range(1, 4):
            m_copy(m).start()
        for m in range(1, 4):
            m_copy(m).wait_recv()
        mine = lax.broadcasted_iota(jnp.int32, (8, 1), 0) == me
        for k in range(N_CHIPS):
            mod_ref[:, n_cols * k:n_cols * (k + 1)] = jnp.sum(jnp.where(mine, modbuf[k], 0.0), axis=0, keepdims=True)
        for p in range(1, 8):
            c_copy(p).wait_send()
        for m in range(1, 4):
            m_copy(m).wait_send()

    vmem = pl.BlockSpec(memory_space=pltpu.VMEM)
    return _call(
        body, [c_row, w_ada, b_ada_mine], name="ada_fwd", in_specs=[vmem, vmem, vmem], out_specs=[vmem, vmem],
        out_shape=[jax.ShapeDtypeStruct((1, N_CHIPS * n_cols), F32), jax.ShapeDtypeStruct((8, 8, D), F32)],
        scratch_shapes=[pltpu.VMEM((8, D), F32), pltpu.VMEM((8, n_cols), F32), pltpu.VMEM((N_CHIPS, 8, n_cols), F32),
                        pltpu.SemaphoreType.DMA((10,)), pltpu.SemaphoreType.DMA((10,))], carry=carry)


def _gather_small(vec, carry=None):
    rows = vec.shape[0]

    def body(v_ref, all_ref, tot_ref, dsk_ref, send_sems, recv_sems):
        x, y, c = _me()
        me = 4 * x + 2 * y + c
        all_ref[me] = v_ref[...]

        def copy(p):
            return pltpu.make_async_remote_copy(
                src_ref=v_ref, dst_ref=all_ref.at[me], send_sem=send_sems.at[p - 1], recv_sem=recv_sems.at[p - 1],
                device_id=_xor_peer(x, y, c, p), device_id_type=MESH)

        for p in range(1, 8):
            copy(p).start()
        for p in range(1, 8):
            copy(p).wait_recv()
        tot = all_ref[0]
        for d in range(1, 8):
            tot = tot + all_ref[d]
        tot_ref[...] = tot
        seg = tot[SMALL_OFF["d_skip"] // 128:SMALL_OFF["d_skip"] // 128 + 16, :]
        lane = lax.broadcasted_iota(jnp.int32, (1, 128), 1)
        sa = jnp.sum(jnp.where(lane < HEAD_DIM, seg, 0.0), axis=1, keepdims=True)
        sb = jnp.sum(jnp.where(lane < HEAD_DIM, 0.0, seg), axis=1, keepdims=True)
        dsk_ref[...] = jnp.where(lane == 0, sa, jnp.where(lane == 1, sb, 0.0))
        for p in range(1, 8):
            copy(p).wait_send()

    vmem = pl.BlockSpec(memory_space=pltpu.VMEM)
    return _call(
        body, [vec], name="gather_small", in_specs=[vmem], out_specs=[vmem, vmem, vmem],
        out_shape=[jax.ShapeDtypeStruct((8, rows, 128), F32), jax.ShapeDtypeStruct((rows, 128), F32),
                   jax.ShapeDtypeStruct((16, 128), F32)],
        scratch_shapes=[pltpu.SemaphoreType.DMA((7,)), pltpu.SemaphoreType.DMA((7,))], carry=carry)


def _gather_carry(shards):
    n = len(shards)

    def copies(ins, outs, sems):
        x, y, c = _me()
        chip = 2 * x + y

        def half(w, which):
            h = shards[w].shape[0] // 2
            return pl.ds(which * h, h)

        def first(w, m):
            return pltpu.make_async_remote_copy(
                src_ref=ins[w].at[half(w, c)], dst_ref=outs[w].at[chip, half(w, c)],
                send_sem=sems.send(6 * w + m - 1), recv_sem=sems.recv(6 * w + m - 1),
                device_id=_xor_peer(x, y, c, 2 * m), device_id_type=MESH)

        def landed(w, m):
            return pltpu.make_async_remote_copy(
                src_ref=ins[w].at[half(w, c)], dst_ref=outs[w].at[chip ^ m, half(w, c)],
                send_sem=sems.send(6 * w + m - 1), recv_sem=sems.recv(6 * w + m - 1),
                device_id=_xor_peer(x, y, c, 2 * m), device_id_type=MESH)

        def passed(w, m, which):
            part = outs[w].at[chip ^ m, half(w, which)]
            return pltpu.make_async_remote_copy(
                src_ref=part, dst_ref=part, send_sem=sems.send(6 * w + 2 + m), recv_sem=sems.recv(6 * w + 2 + m),
                device_id=(x, y, 1 - c), device_id_type=MESH)

        return c, first, landed, passed

    pairs = [(w, m) for w in range(n) for m in range(1, 4)]

    def start(ins, outs, sems):
        _, first, _, _ = copies(ins, outs, sems)
        for w, m in pairs:
            first(w, m).start()

    def finish(ins, outs, sems):
        c, first, landed, passed = copies(ins, outs, sems)
        for w, m in pairs:
            landed(w, m).wait_recv()
            passed(w, m, c).start()
        for w, m in pairs:
            passed(w, m, 1 - c).wait_recv()
        for w, m in pairs:
            first(w, m).wait_send()
            passed(w, m, c).wait_send()

    return _Carry(shards, [jax.ShapeDtypeStruct((N_CHIPS,) + s.shape, s.dtype) for s in shards], 6 * n, start, finish)


def _pair_exchange_carry(grads):
    n = len(grads)

    def copy(ins, outs, sems, w):
        x, y, c = _me()
        h = grads[w].shape[1] // 2
        return pltpu.make_async_remote_copy(
            src_ref=ins[w].at[:, pl.ds((1 - c) * h, h)], dst_ref=outs[w],
            send_sem=sems.send(w), recv_sem=sems.recv(w), device_id=(x, y, 1 - c), device_id_type=MESH)

    def start(ins, outs, sems):
        for w in range(n):
            copy(ins, outs, sems, w).start()

    def finish(ins, outs, sems):
        for w in range(n):
            copy(ins, outs, sems, w).wait()

    return _Carry(grads, [jax.ShapeDtypeStruct((N_CHIPS, g.shape[1] // 2, g.shape[2]), g.dtype) for g in grads], n,
                  start, finish)


def _chip_exchange_carry(partials):
    n = len(partials)

    def copier(ins, outs, sems):
        x, y, c = _me()
        chip = 2 * x + y

        def copy(w, m, landed):
            return pltpu.make_async_remote_copy(
                src_ref=ins[w].at[chip ^ m], dst_ref=outs[w].at[(chip ^ m) if landed else chip],
                send_sem=sems.send(3 * w + m - 1), recv_sem=sems.recv(3 * w + m - 1),
                device_id=_xor_peer(x, y, c, 2 * m), device_id_type=MESH)

        return copy

    pairs = [(w, m) for w in range(n) for m in range(1, 4)]

    def start(ins, outs, sems):
        copy = copier(ins, outs, sems)
        for w, m in pairs:
            copy(w, m, False).start()

    def finish(ins, outs, sems):
        copy = copier(ins, outs, sems)
        for w, m in pairs:
            copy(w, m, True).wait_recv()
        for w, m in pairs:
            copy(w, m, False).wait_send()

    return _Carry(partials, [jax.ShapeDtypeStruct(p.shape, p.dtype) for p in partials], 3 * n, start, finish)


def _pair_share_carry(shards):
    n = len(shards)

    def copier(ins, outs, sems):
        x, y, c = _me()

        def copy(w, which):
            h = shards[w].shape[0] // 2
            rows = pl.ds(which * h, h)
            return pltpu.make_async_remote_copy(
                src_ref=ins[w].at[rows], dst_ref=outs[w].at[rows],
                send_sem=sems.send(w), recv_sem=sems.recv(w), device_id=(x, y, 1 - c), device_id_type=MESH)

        return c, copy

    def start(ins, outs, sems):
        c, copy = copier(ins, outs, sems)
        for w in range(n):
            copy(w, c).start()

    def finish(ins, outs, sems):
        c, copy = copier(ins, outs, sems)
        for w in range(n):
            copy(w, 1 - c).wait_recv()
        for w in range(n):
            copy(w, c).wait_send()

    return _Carry(shards, [jax.ShapeDtypeStruct(s.shape, s.dtype) for s in shards], n, start, finish,
                  aliased=[(w, w) for w in range(n)])


def _pair_sum(g, part, idx, name):
    _, h, C = part.shape
    tr = min(512, h)
    nb = h // tr

    def body(idx_ref, g_ref, p_ref, o16_ref, own_ref):
        v = g_ref[...].astype(F32) + p_ref[...].astype(F32)
        o16_ref[...] = v.astype(BF16)

        @pl.when(pl.program_id(1) == idx_ref[1])
        def _():
            own_ref[...] = v

    return pl.pallas_call(
        body, name=name,
        grid_spec=pltpu.PrefetchScalarGridSpec(
            num_scalar_prefetch=1, grid=(nb, N_CHIPS),
            in_specs=[pl.BlockSpec((None, tr, C), lambda i, s, idx_ref: (s, idx_ref[0] * nb + i, 0)),
                      pl.BlockSpec((None, tr, C), lambda i, s, idx_ref: (s, i, 0))],
            out_specs=[pl.BlockSpec((None, tr, C), lambda i, s, idx_ref: (s, i, 0)),
                       pl.BlockSpec((tr, C), lambda i, s, idx_ref: (i, 0))]),
        out_shape=[jax.ShapeDtypeStruct(part.shape, BF16), jax.ShapeDtypeStruct((h, C), F32)],
        compiler_params=pltpu.CompilerParams(dimension_semantics=("arbitrary", "arbitrary"), vmem_limit_bytes=VMEM_LIMIT),
    )(idx, g, part)


def _chip_sum(own, slots, idx, name):
    h, C = own.shape
    tr = min(512, h)
    nb = h // tr

    def body(idx_ref, own_ref, s1_ref, s2_ref, s3_ref, o_ref):
        del idx_ref
        o_ref[...] = ((own_ref[...] + s1_ref[...].astype(F32)) + s2_ref[...].astype(F32)) + s3_ref[...].astype(F32)

    def slot(m):
        return pl.BlockSpec((None, tr, C), lambda i, idx_ref: (idx_ref[1] ^ m, i, 0))

    return pl.pallas_call(
        body, name=name,
        grid_spec=pltpu.PrefetchScalarGridSpec(
            num_scalar_prefetch=1, grid=(nb,),
            in_specs=[pl.BlockSpec((tr, C), lambda i, idx_ref: (i, 0)), slot(1), slot(2), slot(3)],
            out_specs=pl.BlockSpec((tr, C), lambda i, idx_ref: (idx_ref[0] * nb + i, 0))),
        out_shape=jax.ShapeDtypeStruct((2 * h, C), F32),
        compiler_params=pltpu.CompilerParams(dimension_semantics=("parallel",), vmem_limit_bytes=VMEM_LIMIT),
    )(idx, own, slots, slots, slots)


class _Reducer:
    def __init__(self, idx):
        self.idx, self.chips, self.p16, self.own, self.mine, self.final = idx, {}, {}, {}, {}, {}

    def add(self, name, whole, chip_blocks=False):
        self.chips[name] = whole if chip_blocks else _chips_from_whole(name, whole)

    def pair(self, names):
        return _pair_exchange_carry([self.chips[n] for n in names])

    def take_pair(self, names, outs):
        for n, part in zip(names, outs):
            self.p16[n], self.own[n] = _pair_sum(self.chips.pop(n), part, self.idx, "pair_sum_" + n)

    def chip(self, names):
        return _chip_exchange_carry([self.p16[n] for n in names])

    def take_chip(self, names, outs):
        for n, slots in zip(names, outs):
            del self.p16[n]
            self.mine[n] = _chip_sum(self.own.pop(n), slots, self.idx, "chip_sum_" + n)

    def share(self, names):
        return _pair_share_carry([self.mine[n] for n in names])

    def take_share(self, names, outs):
        for n, s in zip(names, outs):
            del self.mine[n]
            self.final[n] = s


def _w_ada_grad(c8, dmod_cols):
    n_cols = dmod_cols.shape[1]
    tn = 512

    def body(c_ref, d_ref, o_ref):
        cv = c_ref[...]
        o_ref[...] = _tn(cv * _sigmoid(cv), d_ref[...], precision=HIGH)

    return _call(body, [c8, dmod_cols], name="w_ada_grad", grid=(n_cols // tn,),
                 in_specs=[pl.BlockSpec((8, D), lambda j: (0, 0)), pl.BlockSpec((8, tn), lambda j: (0, j))],
                 out_specs=[pl.BlockSpec((D, tn), lambda j: (0, j))],
                 out_shape=[jax.ShapeDtypeStruct((D, n_cols), F32)], sem=("parallel",))[0]


_SMALL_SEGS = (("dmod", 6144), ("norm_mix_w", 1024), ("conv_b", 3072), ("ssd_norm_w", 2048), ("pool_scale", 1024),
               ("norm_mlp_w", 1024), ("norm_final_w", 1024), ("conv_w", 4 * XBC), ("d_skip", 2048), ("a_log", 128),
               ("dt_bias", 128), ("loss", 128))
SMALL_OFF = {}
_o = 0
for _n, _s in _SMALL_SEGS:
    SMALL_OFF[_n] = _o
    _o += _s
SMALL_LEN = -(-_o // 1024) * 1024

_FIRST = ("w_in", "conv_w")
_LATER = ("w_branch_ssd", "pool_w", "w_branch_pool", "w_out", "w_up", "w_down")
_SMALL_REPLICATED = ("b_ada", "norm_mix_w", "conv_b", "dt_bias", "a_log", "d_skip", "ssd_norm_w", "pool_scale",
                     "norm_mlp_w", "norm_final_w")
_WEIGHTS = ("w_ada", "b_ada", "norm_mix_w", "w_in", "conv_w", "conv_b", "dt_bias", "a_log", "d_skip", "ssd_norm_w",
            "w_branch_ssd", "pool_w", "pool_scale", "w_branch_pool", "w_out", "norm_mlp_w", "w_up", "w_down",
            "norm_final_w")


def _shard_2d(name, a):
    if name == "conv_w":
        return a.reshape(16, -1)
    return (a.reshape(GW, GW) if name == "pool_w" else a.reshape(a.shape[-2], a.shape[-1])).astype(BF16)


def _whole_from_chips(name, g, own, chip):
    g = lax.dynamic_update_slice(g, own[None], (chip, 0, 0))
    if name == "w_in":
        a, b = _DT_IN_CHIP2, _DT_IN_CHIP2 + HEADS
        pad = jnp.zeros((D, NP - IN_COLS), g.dtype)
        return jnp.concatenate([g[0], g[1], g[2][:, :a], g[2][:, b:], g[3], g[2][:, a:b], pad], axis=1)
    if name == "w_up":
        return jnp.concatenate([g[k] for k in range(N_CHIPS)], axis=1)
    if name == "pool_w":
        return jnp.transpose(g.reshape(N_CHIPS, 4, GW // N_CHIPS, GW), (1, 0, 2, 3)).reshape(4, GW, GW)
    if name == "conv_w":
        return jnp.transpose(g.reshape(N_CHIPS, 4, XBC // N_CHIPS), (1, 0, 2)).reshape(4, XBC)
    return g.reshape(N_CHIPS * g.shape[1], g.shape[2])


def _chips_from_whole(name, g):
    if name.startswith("w_in"):
        cw, a = IN_COLS // N_CHIPS, _DT_IN_CHIP2
        chip2 = jnp.concatenate([g[:, 2 * cw:2 * cw + a], g[:, OFF_DT:OFF_DT + HEADS], g[:, 5120:3 * cw - HEADS]], axis=1)
        return jnp.stack([g[:, :cw], g[:, cw:2 * cw], chip2, g[:, 3 * cw - HEADS:OFF_DT]])
    if name == "w_up":
        return jnp.transpose(g.reshape(D, N_CHIPS, DFF // N_CHIPS), (1, 0, 2))
    if name == "pool_w":
        return jnp.transpose(g.reshape(4, N_CHIPS, GW // N_CHIPS, GW), (1, 0, 2, 3)).reshape(N_CHIPS, GW, GW)
    return g.reshape(N_CHIPS, g.shape[0] // N_CHIPS, g.shape[1])


def kernel(x, c, w_ada, b_ada, norm_mix_w, w_in, conv_w, conv_b, dt_bias, a_log, d_skip, ssd_norm_w, w_branch_ssd, pool_w, pool_scale, w_branch_pool, w_out, norm_mlp_w, w_up, w_down, norm_final_w, loss_target, m_w_ada, m_b_ada, m_norm_mix_w, m_w_in, m_conv_w, m_conv_b, m_dt_bias, m_a_log, m_d_skip, m_ssd_norm_w, m_w_branch_ssd, m_pool_w, m_pool_scale, m_w_branch_pool, m_w_out, m_norm_mlp_w, m_w_up, m_w_down, m_norm_final_w, v_w_ada, v_b_ada, v_norm_mix_w, v_w_in, v_conv_w, v_conv_b, v_dt_bias, v_a_log, v_d_skip, v_ssd_norm_w, v_w_branch_ssd, v_pool_w, v_pool_scale, v_w_branch_pool, v_w_out, v_norm_mlp_w, v_w_up, v_w_down, v_norm_final_w):
    args = locals()
    w = {n: args[n] for n in _WEIGHTS}
    m = {n: args["m_" + n] for n in _WEIGHTS}
    v = {n: args["v_" + n] for n in _WEIGHTS}
    xi, yi, ci = _me()
    chip = 2 * xi + yi
    idx = jnp.stack([ci, chip]).astype(jnp.int32)
    ada_cols = w_ada.shape[-1]
    xs, target = x[0], loss_target[0]
    two_d = lambda n, a: a.reshape(GW, GW) if n == "pool_w" else a.reshape(-1, a.shape[-1])
    delta, new_m, new_v, g = {}, {}, {}, {}

    def adamw(n, carry=None):
        res = _adamw(two_d(n, w[n]), two_d(n, g[n]), two_d(n, m[n]), two_d(n, v[n]), "adamw_" + n, carry=carry)
        (delta[n], new_m[n], new_v[n]), extra = res if carry is not None else (res, None)
        return extra

    b_mine = lax.dynamic_slice(b_ada, (0, chip * ada_cols), (1, ada_cols))
    shards = {n: _shard_2d(n, w[n]) for n in _FIRST + _LATER}
    mod, c8 = _ada_fwd(c, w_ada[0], b_mine)
    c8 = c8[:, 0, :]
    shift_m, scale_m, gate_m, shift_f, scale_f, gate_f = [mod[:, D * i:D * (i + 1)] for i in range(6)]
    nf_w = norm_final_w.reshape(1, D)

    h1, first = _norm_mod(xs, norm_mix_w, scale_m, shift_m, "norm_mod_mix",
                          carry=_gather_carry([shards[n] for n in _FIRST]))
    p ={n: _whole_from_chips(n, a, shards[n], chip) for n, a in zip(_FIRST, first)}
    (proj,), later = _matmul(h1, p["w_in"], mode="nn", out_dtypes=[F32], name="mm_proj",
                             carry=_gather_carry([shards[n] for n in _LATER]))
    p.update({n: _whole_from_chips(n, a, shards[n], chip) for n, a in zip(_LATER, later)})
    xbc_a = _conv_fwd(proj, p["conv_w"], conv_b)
    dtb_c, alog_c = dt_bias.reshape(HEADS, 1), a_log.reshape(HEADS, 1)
    dsk_exp = jnp.repeat(d_skip, HEAD_DIM, axis=1)
    y, hin = _ssd_fwd(xbc_a, proj, dt_bias, a_log, dtb_c, alog_c, dsk_exp)
    yn = _gate_norm(y, proj, ssd_norm_w)
    (y_ssd,) = _matmul(yn, p["w_branch_ssd"], mode="nn", out_dtypes=[F32], name="mm_branch_ssd")
    pooled, pw_out, yps = _pool_fwd(proj, p["pool_w"], pool_scale)
    (y_pool,) = _matmul(yps, p["w_branch_pool"], mode="nn", out_dtypes=[F32], name="mm_branch_pool")
    merged = _merge(proj, y_ssd, y_pool)
    resid = lambda acc, r, gt: (r + gt * acc, acc)
    x2, mix = _matmul(merged, p["w_out"], mode="nn", out_dtypes=[F32, BF16], name="mm_out",
                      epi=resid, tile_extras=(xs,), row_extras=(gate_m,))
    h2 = _norm_mod(x2, norm_mlp_w, scale_f, shift_f, "norm_mod_mlp")
    relu2 = lambda acc: (jnp.square(jnp.maximum(acc, 0.0)),)
    (act,) = _matmul(h2, p["w_up"], mode="nn", out_dtypes=[BF16], name="mm_up", epi=relu2)
    x3, down = _matmul(act, p["w_down"], mode="nn", out_dtypes=[F32, BF16], name="mm_down",
                       epi=resid, tile_extras=(x2,), row_extras=(gate_f,))

    red = _Reducer(idx)
    dx3, d_down, sums_f = _final_loss_bwd(x3, target, nf_w, down, gate_f)
    drelu2 = lambda acc, a: (acc * (2.0 * jnp.sqrt(a.astype(F32))),)
    (dup,) = _matmul(d_down, p["w_down"], mode="nt", out_dtypes=[BF16], name="mm_dact",
                     epi=drelu2, tile_extras=(act,))
    red.add("w_down", _matmul(act, d_down, mode="tn", out_dtypes=[BF16], name="mm_g_down")[0])
    (dh2,), got = _matmul(dup, p["w_up"], mode="nt", out_dtypes=[F32], name="mm_dh2",
                          carry=red.pair(["w_down"]))
    red.take_pair(["w_down"], got)
    red.add("w_up", _matmul(h2, dup, mode="tn", out_dtypes=[BF16], name="mm_g_up", chip_blocks=True)[0], chip_blocks=True)
    dx2, sums_2, dmix = _norm_mod_bwd(x2, dh2, dx3, norm_mlp_w, scale_f, "norm_mod_mlp_bwd", branch=mix, gate=gate_m)
    (dmerged,), got = _matmul(dmix, p["w_out"], mode="nt", out_dtypes=[F32], name="mm_dmerged",
                              carry=red.pair(["w_up"]))
    red.take_pair(["w_up"], got)
    red.add("w_out", _matmul(merged, dmix, mode="tn", out_dtypes=[BF16], name="mm_g_out")[0])
    dy_ssd, dy_pool, dproj = _merge_bwd(dmerged, proj, y_ssd, y_pool)
    (dyp,), got = _matmul(dy_pool, p["w_branch_pool"], mode="nt", out_dtypes=[F32], name="mm_dyp",
                          carry=red.pair(["w_out"]))
    red.take_pair(["w_out"], got)
    red.add("w_branch_pool", _matmul(yps, dy_pool, mode="tn", out_dtypes=[BF16], name="mm_g_bpool")[0])
    dproj, g_pool_w, sums_pool = _pool_bwd(dyp, pw_out, pooled, p["pool_w"], pool_scale, dproj)
    red.add("pool_w", g_pool_w.astype(BF16))
    red.add("w_branch_ssd", _matmul(yn, dy_ssd, mode="tn", out_dtypes=[BF16], name="mm_g_bssd")[0])
    mixers = ["w_branch_pool", "pool_w", "w_branch_ssd"]
    (dyn,), got = _matmul(dy_ssd, p["w_branch_ssd"], mode="nt", out_dtypes=[F32], name="mm_dyn",
                          carry=red.pair(mixers))
    red.take_pair(mixers, got)
    dy, dproj, sums_gn = _gate_norm_bwd(dyn, y, proj, ssd_norm_w, dproj)
    six = ["w_down", "w_up", "w_out"] + mixers
    (dxa, dproj, dsk_sum, ssd_small), got = _ssd_bwd(dy, xbc_a, proj, hin, dt_bias, a_log, dtb_c, alog_c, dsk_exp,
                                                     dproj, carry=red.chip(six))
    red.take_chip(six, got)
    dxc, sums_conv = _conv_bwd_a(dxa, proj, p["conv_w"], conv_b)
    dproj = _conv_bwd_b(dxc, p["conv_w"], dproj)
    rows_a = 3 * D // 4
    (g_in_a,), got = _matmul(h1, dproj, mode="tn", out_dtypes=[BF16], name="mm_g_in_a", a_cols=(0, rows_a),
                             carry=red.share(six))
    red.take_share(six, got)
    red.add("w_in_a", g_in_a)
    (g_in_b,), got = _matmul(h1, dproj, mode="tn", out_dtypes=[BF16], name="mm_g_in_b", a_cols=(rows_a, D - rows_a),
                             carry=red.pair(["w_in_a"]))
    red.take_pair(["w_in_a"], got)
    red.add("w_in_b", g_in_b)
    (dh1,), got = _matmul(dproj, p["w_in"], mode="nt", out_dtypes=[F32], name="mm_dh1",
                          carry=_join(red.chip(["w_in_a"]), red.pair(["w_in_b"])))
    red.take_chip(["w_in_a"], got[:1])
    red.take_pair(["w_in_b"], got[1:])
    grad_x, sums_1 = _norm_mod_bwd(xs, dh1, dx2, norm_mix_w, scale_m, "norm_mod_mix_bwd")

    dmod = jnp.concatenate([sums_1[0:1], sums_1[1:2], sums_2[3:4], sums_2[0:1], sums_2[1:2], sums_f[1:2]], axis=1)
    pad96 = jnp.zeros((1, 96), F32)
    small = {"dmod": dmod, "norm_mix_w": sums_1[2:3], "conv_b": sums_conv[4:5], "ssd_norm_w": sums_gn[0:1],
             "pool_scale": sums_pool[0:1], "norm_mlp_w": sums_2[2:3], "norm_final_w": sums_f[0:1],
             "conv_w": sums_conv[0:4].reshape(1, 4 * XBC), "d_skip": dsk_sum[0:1],
             "a_log": jnp.concatenate([ssd_small[0:1], pad96], axis=1),
             "dt_bias": jnp.concatenate([ssd_small[1:2], pad96], axis=1), "loss": sums_f[3:4, 0:128]}
    vec = jnp.concatenate([small[n] for n, _ in _SMALL_SEGS], axis=1)
    vec = jnp.pad(vec, ((0, 0), (0, SMALL_LEN - vec.shape[1]))).reshape(SMALL_LEN // 128, 128)
    (every, total, dsk), got = _gather_small(vec, carry=_join(red.chip(["w_in_b"]), red.share(["w_in_a"])))
    red.take_chip(["w_in_b"], got[:1])
    red.take_share(["w_in_a"], got[1:])
    total = total.reshape(1, SMALL_LEN)
    seg = lambda n, size: total[:, SMALL_OFF[n]:SMALL_OFF[n] + size]
    g.update({"b_ada": seg("dmod", 6 * D), "norm_mix_w": seg("norm_mix_w", D), "conv_b": seg("conv_b", XBC),
              "dt_bias": seg("dt_bias", HEADS), "a_log": seg("a_log", HEADS), "d_skip": dsk[:, 0:2].reshape(1, HEADS),
              "ssd_norm_w": seg("ssd_norm_w", DI), "pool_scale": seg("pool_scale", D),
              "norm_mlp_w": seg("norm_mlp_w", D), "norm_final_w": seg("norm_final_w", D)})
    loss = total[0, SMALL_OFF["loss"]]
    conv_cols = conv_w.shape[-1]
    g["conv_w"] = lax.dynamic_slice(seg("conv_w", 4 * XBC).reshape(4, XBC), (0, chip * conv_cols), (4, conv_cols))
    dmod8 = every.reshape(8, SMALL_LEN)[:, SMALL_OFF["dmod"]:SMALL_OFF["dmod"] + 6 * D]
    g["w_ada"] = _w_ada_grad(c8, lax.dynamic_slice(dmod8, (0, chip * ada_cols), (8, ada_cols)))

    got = adamw("w_ada", carry=red.share(["w_in_b"]))
    red.take_share(["w_in_b"], got)
    for n in six:
        g[n] = red.final[n]
    g["w_in"] = jnp.concatenate([red.final["w_in_a"], red.final["w_in_b"]], axis=0)
    for n in ["conv_w", "w_in"] + six:
        adamw(n)
    sizes = [w[n].size for n in _SMALL_REPLICATED]
    n_small = -(-sum(sizes) // 1024) * 1024
    pack = lambda d: jnp.pad(jnp.concatenate([d[n].reshape(1, -1) for n in _SMALL_REPLICATED], axis=1),
                             ((0, 0), (0, n_small - sum(sizes)))).reshape(n_small // 128, 128)
    d_, m_, v_ = _adamw(pack(w), pack(g), pack(m), pack(v), "adamw_small")
    off = 0
    for n, s in zip(_SMALL_REPLICATED, sizes):
        for dst, src in ((delta, d_), (new_m, m_), (new_v, v_)):
            dst[n] = src.reshape(1, n_small)[:, off:off + s]
        off += s

    out = [loss, grad_x.reshape(x.shape)]
    for d in (g, delta, new_m, new_v):
        out += [d[n].reshape(w[n].shape) for n in _WEIGHTS]
    return tuple(out)
```

```python
import functools
import operator

import jax
import jax.numpy as jnp
import numpy as np
from jax import lax
from jax.experimental import pallas as pl
from jax.experimental.pallas import tpu as pltpu

F32, BF16 = jnp.float32, jnp.bfloat16
HIGH = lax.Precision.HIGHEST
MESH = pl.DeviceIdType.MESH

D = 1024
DI = 2048
HEADS, HEAD_DIM = 32, 64
GROUPS, STATE = 4, 128
Q = 128
XBC = DI + 2 * GROUPS * STATE
POOL_WINDOWS = (2, 4, 8, 16)
GW = 256
DFF = 4096
EPS = 1e-5
IN_COLS = 8224
OFF_Z, OFF_XBC, OFF_POOL, OFF_GATE, OFF_DT, NP = 0, 2048, 5120, 6144, 8192, 8448
N_CHIPS = 4
ADAM_LR, ADAM_B1, ADAM_B2, ADAM_EPS, ADAM_WD, ADAM_STEP = 0.001, 0.9, 0.999, 1e-08, 0.01, 10
VMEM_LIMIT = 56 * 2 ** 20
NEG = -1e30


def _sigmoid(v):
    return 0.5 * jnp.tanh(0.5 * v) + 0.5


def _softplus(v):
    return jnp.maximum(v, 0.0) + jnp.log1p(jnp.exp(-jnp.abs(v)))


def _dot(a, b, dims, **kw):
    return lax.dot_general(a, b, (dims, ((), ())), preferred_element_type=F32, **kw)


def _nn(a, b, **kw):
    return _dot(a, b, ((1,), (0,)), **kw)


def _nt(a, b, **kw):
    return _dot(a, b, ((1,), (1,)), **kw)


def _tn(a, b, **kw):
    return _dot(a, b, ((0,), (0,)), **kw)


_DT_IN_CHIP2 = 5120 - 2 * (IN_COLS // 4)


class _Sems:
    def __init__(self, send, recv, local, base=0):
        self._send, self._recv, self._local, self._base = send, recv, local, base

    def shift(self, n):
        return _Sems(self._send, self._recv, self._local, self._base + n)

    def send(self, i):
        return self._send.at[self._base + i]

    def recv(self, i):
        return self._recv.at[self._base + i]

    def local(self, i):
        return self._local.at[self._base + i]


class _Carry:
    def __init__(self, ins, out_shapes, n_sems, start, finish, aliased=()):
        self.ins, self.out_shapes, self.n_sems, self.start, self.finish = list(ins), list(out_shapes), n_sems, start, finish
        self.aliased = list(aliased)


def _join(*carries):
    def run(which):
        def fn(ins, outs, sems):
            i = o = s = 0
            for cy in carries:
                getattr(cy, which)(ins[i:i + len(cy.ins)], outs[o:o + len(cy.out_shapes)], sems.shift(s))
                i, o, s = i + len(cy.ins), o + len(cy.out_shapes), s + cy.n_sems
        return fn

    aliased, i, o = [], 0, 0
    for cy in carries:
        aliased += [(i + a, o + b) for a, b in cy.aliased]
        i, o = i + len(cy.ins), o + len(cy.out_shapes)
    return _Carry([a for cy in carries for a in cy.ins], [a for cy in carries for a in cy.out_shapes],
                  sum(cy.n_sems for cy in carries), run("start"), run("finish"), aliased)


def _call(body, args, *, name, grid=(), in_specs, out_specs, out_shape, scratch_shapes=(), sem=None, aliases=None,
          carry=None):
    in_specs, out_specs, out_shape, scratch_shapes = list(in_specs), list(out_specs), list(out_shape), list(scratch_shapes)
    n_in, n_out, n_scr = len(in_specs), len(out_specs), len(scratch_shapes)
    kw = {"vmem_limit_bytes": VMEM_LIMIT}
    if carry is None:
        kernel_fn = functools.partial(body)
        if sem is not None:
            kw["dimension_semantics"] = sem
    else:
        n_ci, n_co = len(carry.ins), len(carry.out_shapes)
        hbm = pl.BlockSpec(memory_space=pl.ANY)
        in_specs += [hbm] * n_ci
        out_specs += [hbm] * n_co
        out_shape += carry.out_shapes
        n_s = max(carry.n_sems, 1)
        scratch_shapes += [pltpu.SemaphoreType.DMA((n_s,))] * 3
        args = list(args) + carry.ins
        aliases = dict(aliases or {})
        aliases.update({n_in + i: n_out + o for i, o in carry.aliased})
        if grid:
            kw["dimension_semantics"] = ("arbitrary",) * len(grid)

        def kernel_fn(*refs):
            a = n_in
            ins, c_ins = refs[:a], refs[a:a + n_ci]
            a += n_ci
            outs, c_outs = refs[a:a + n_out], refs[a + n_out:a + n_out + n_co]
            a += n_out + n_co
            scr, sems = refs[a:a + n_scr], _Sems(*refs[a + n_scr:a + n_scr + 3])
            if grid:
                ids = [pl.program_id(d) for d in range(len(grid))]
                first = functools.reduce(operator.and_, [i == 0 for i in ids])
                last = functools.reduce(operator.and_, [i == g - 1 for i, g in zip(ids, grid)])

                @pl.when(first)
                def _():
                    carry.start(c_ins, c_outs, sems)

                body(*ins, *outs, *scr)

                @pl.when(last)
                def _():
                    carry.finish(c_ins, c_outs, sems)
            else:
                carry.start(c_ins, c_outs, sems)
                body(*ins, *outs, *scr)
                carry.finish(c_ins, c_outs, sems)

    outs = pl.pallas_call(
        kernel_fn, name=name, grid=grid, in_specs=in_specs, out_specs=out_specs, out_shape=out_shape,
        scratch_shapes=scratch_shapes, input_output_aliases=aliases or {},
        compiler_params=pltpu.CompilerParams(**kw),
    )(*args)
    outs = list(outs)
    return outs if carry is None else (outs[:n_out], outs[n_out:])


def _run_carry(carry, name):
    _, outs = _call(lambda: None, [], name=name, in_specs=[], out_specs=[], out_shape=[], carry=carry)
    return outs


_TILES = {
    "mm_proj": (1024, 2816, 1024), "mm_branch_ssd": (1024, 1024, 2048), "mm_branch_pool": (1024, 1024, 1024),
    "mm_out": (1024, 1024, 1024), "mm_up": (2048, 1024, 1024), "mm_down": (512, 1024, 4096),
    "mm_dact": (1024, 1024, 1024), "mm_g_down": (1024, 1024, 2048), "mm_dh2": (1024, 1024, 4096),
    "mm_g_up": (1024, 1024, 2048), "mm_dmerged": (1024, 1024, 1024), "mm_g_out": (1024, 1024, 2048),
    "mm_dyp": (1024, 1024, 1024), "mm_g_bpool": (1024, 1024, 2048), "mm_g_bssd": (1024, 1024, 2048),
    "mm_dyn": (1024, 1024, 1024), "mm_g_in_a": (768, 1408, 2048), "mm_g_in_b": (256, 2816, 2048),
    "mm_dh1": (1024, 1024, 2816),
}


def _matmul(a, b, *, mode, out_dtypes, name, epi=None, tile_extras=(), row_extras=(), carry=None, a_cols=None,
            chip_blocks=False, cols_outer=False):
    M, K = (a.shape[1], a.shape[0]) if mode == "tn" else a.shape
    N = b.shape[0] if mode == "nt" else b.shape[1]
    a_start, M = a_cols if a_cols is not None else (0, M)
    tm, tn, tk = _TILES[name]
    tm, tn, tk = min(tm, M), min(tn, N), min(tk, K)
    assert M % tm == 0 and N % tn == 0 and K % tk == 0 and a_start % tm == 0, (name, M, N, K, tm, tn, tk)
    a_off = a_start // tm
    if mode == "nn":
        a_spec = pl.BlockSpec((tm, tk), lambda i, j, k: (i, k))
        b_spec = pl.BlockSpec((tk, tn), lambda i, j, k: (k, j))
        dims = ((1,), (0,))
    elif mode == "nt":
        a_spec = pl.BlockSpec((tm, tk), lambda i, j, k: (i, k))
        b_spec = pl.BlockSpec((tn, tk), lambda i, j, k: (j, k))
        dims = ((1,), (1,))
    else:
        a_spec = pl.BlockSpec((tk, tm), lambda i, j, k: (k, i + a_off))
        b_spec = pl.BlockSpec((tk, tn), lambda i, j, k: (k, j))
        dims = ((0,), (0,))
    nk = K // tk
    n_te, n_re, n_out = len(tile_extras), len(row_extras), len(out_dtypes)
    if epi is None:
        epi = lambda acc: (acc,)

    def body(a_ref, b_ref, *rest):
        extras = rest[:n_te + n_re]
        outs = rest[n_te + n_re:n_te + n_re + n_out]
        p = _dot(a_ref[...], b_ref[...], dims)

        def finish(acc):
            vals = epi(acc, *[e[...] for e in extras])
            for o, v in zip(outs, vals):
                o[...] = v.astype(o.dtype)

        if nk == 1:
            finish(p)
        else:
            acc_ref = rest[-1]
            k = pl.program_id(2)

            @pl.when(k == 0)
            def _():
                acc_ref[...] = p

            @pl.when(k > 0)
            def _():
                acc_ref[...] += p

            @pl.when(k == nk - 1)
            def _():
                finish(acc_ref[...])

    tile_spec = pl.BlockSpec((tm, tn), lambda i, j, k: (i, j))
    row_spec = pl.BlockSpec((1, tn), lambda i, j, k: (0, j))
    out_spec, out_dims = tile_spec, (M, N)
    if chip_blocks:
        assert n_te == 0 and tn * N_CHIPS == N
        out_spec, out_dims = pl.BlockSpec((None, tm, tn), lambda i, j, k: (j, i, 0)), (N_CHIPS, M, tn)
    in_specs, grid = [a_spec, b_spec] + [tile_spec] * n_te + [row_spec] * n_re, (M // tm, N // tn, nk)
    if cols_outer:
        swap = lambda s: pl.BlockSpec(s.block_shape, lambda g0, g1, k, f=s.index_map: f(g1, g0, k))
        in_specs, out_spec, grid = [swap(s) for s in in_specs], swap(out_spec), (N // tn, M // tm, nk)
    return _call(
        body, [a, b, *tile_extras, *row_extras], name=name, grid=grid,
        in_specs=in_specs, out_specs=[out_spec] * n_out,
        out_shape=[jax.ShapeDtypeStruct(out_dims, dt) for dt in out_dtypes],
        scratch_shapes=[pltpu.VMEM((tm, tn), F32)] if nk > 1 else [],
        sem=("parallel", "parallel", "arbitrary"), carry=carry)


def _row_tile(T):
    return min(512, T)


def _norm_mod(x, nw, scale, shift, name, carry=None):
    T = x.shape[0]
    tr = _row_tile(T)

    def body(x_ref, nw_ref, sc_ref, sh_ref, o_ref):
        xv = x_ref[...]
        r = lax.rsqrt(jnp.mean(xv * xv, axis=-1, keepdims=True) + EPS)
        o_ref[...] = ((xv * r) * nw_ref[...] * (1.0 + sc_ref[...]) + sh_ref[...]).astype(BF16)

    tile = pl.BlockSpec((tr, D), lambda i: (i, 0))
    row = pl.BlockSpec((1, D), lambda i: (0, 0))
    res = _call(body, [x, nw, scale, shift], name=name, grid=(T // tr,), in_specs=[tile, row, row, row],
                out_specs=[tile], out_shape=[jax.ShapeDtypeStruct((T, D), BF16)], sem=("parallel",), carry=carry)
    return res[0] if carry is None else (res[0][0], res[1])


def _norm_mod_bwd(x, dh, dres, nw, scale, name, branch=None, gate=None, carry=None):
    T = x.shape[0]
    tr = _row_tile(T)
    with_branch = branch is not None

    def body(x_ref, dh_ref, dr_ref, nw_ref, sc_ref, *rest):
        if with_branch:
            br_ref, g_ref, dx_ref, sums_ref, db_ref = rest
        else:
            dx_ref, sums_ref = rest
        i = pl.program_id(0)

        @pl.when(i == 0)
        def _():
            sums_ref[...] = jnp.zeros_like(sums_ref)

        xv, dhv = x_ref[...], dh_ref[...]
        r = lax.rsqrt(jnp.mean(xv * xv, axis=-1, keepdims=True) + EPS)
        xn = xv * r
        g1 = dhv * (1.0 + sc_ref[...])
        dxn = g1 * nw_ref[...]
        dx = dr_ref[...] + r * (dxn - xn * jnp.mean(dxn * xn, axis=-1, keepdims=True))
        dx_ref[...] = dx
        sums_ref[0:1, :] += jnp.sum(dhv, axis=0, keepdims=True)
        sums_ref[1:2, :] += jnp.sum(dhv * (xn * nw_ref[...]), axis=0, keepdims=True)
        sums_ref[2:3, :] += jnp.sum(g1 * xn, axis=0, keepdims=True)
        if with_branch:
            db_ref[...] = (dx * g_ref[...]).astype(BF16)
            sums_ref[3:4, :] += jnp.sum(dx * br_ref[...], axis=0, keepdims=True)

    tile = pl.BlockSpec((tr, D), lambda i: (i, 0))
    row = pl.BlockSpec((1, D), lambda i: (0, 0))
    sums = pl.BlockSpec((8, D), lambda i: (0, 0))
    ins = [x, dh, dres, nw, scale] + ([branch, gate] if with_branch else [])
    in_specs = [tile, tile, tile, row, row] + ([tile, row] if with_branch else [])
    out_specs = [tile, sums] + ([tile] if with_branch else [])
    out_shape = [jax.ShapeDtypeStruct((T, D), F32), jax.ShapeDtypeStruct((8, D), F32)]
    if with_branch:
        out_shape.append(jax.ShapeDtypeStruct((T, D), BF16))
    return _call(body, ins, name=name, grid=(T // tr,), in_specs=in_specs, out_specs=out_specs, out_shape=out_shape,
                 sem=("arbitrary",), carry=carry)


def _final_loss_bwd(x3, target, wf, down, gate_f):
    T = x3.shape[0]
    tr = _row_tile(T)
    n_steps = T // tr

    def body(x_ref, t_ref, w_ref, dn_ref, g_ref, dx_ref, dd_ref, sums_ref):
        i = pl.program_id(0)

        @pl.when(i == 0)
        def _():
            sums_ref[...] = jnp.zeros_like(sums_ref)

        xv = x_ref[...]
        r = lax.rsqrt(jnp.mean(xv * xv, axis=-1, keepdims=True) + EPS)
        xn = xv * r
        err = xn * w_ref[...] - t_ref[...]
        dy = err * (1.0 / D)
        dxn = dy * w_ref[...]
        dx = r * (dxn - xn * jnp.mean(dxn * xn, axis=-1, keepdims=True))
        dx_ref[...] = dx
        dd_ref[...] = (dx * g_ref[...]).astype(BF16)
        sums_ref[0:1, :] += jnp.sum(dy * xn, axis=0, keepdims=True)
        sums_ref[1:2, :] += jnp.sum(dx * dn_ref[...], axis=0, keepdims=True)
        sums_ref[2:3, :] += jnp.sum(err * err, axis=0, keepdims=True) * (0.5 / D)

        @pl.when(i == n_steps - 1)
        def _():
            sums_ref[3:4, :] = jnp.broadcast_to(jnp.sum(sums_ref[2:3, :], axis=1, keepdims=True), (1, D))

    tile = pl.BlockSpec((tr, D), lambda i: (i, 0))
    row = pl.BlockSpec((1, D), lambda i: (0, 0))
    sums = pl.BlockSpec((8, D), lambda i: (0, 0))
    return _call(body, [x3, target, wf, down, gate_f], name="final_loss_bwd", grid=(n_steps,),
                 in_specs=[tile, tile, row, tile, row], out_specs=[tile, tile, sums],
                 out_shape=[jax.ShapeDtypeStruct((T, D), F32), jax.ShapeDtypeStruct((T, D), BF16),
                            jax.ShapeDtypeStruct((8, D), F32)], sem=("arbitrary",))


CONV_TC = 1024


def _conv_taps(xp, w, b):
    acc = b + w[3:4, :] * xp
    for k in range(3):
        acc = acc + w[k:k + 1, :] * pltpu.roll(xp, 3 - k, 0)
    return acc


def _conv_fwd(proj, conv_w, conv_b):
    T = proj.shape[0]
    tr = _row_tile(T)
    nb, offb = tr // 8, OFF_XBC // CONV_TC

    def body(x_ref, h_ref, w_ref, b_ref, o_ref):
        halo = jnp.where(pl.program_id(0) > 0, h_ref[...], 0.0)
        xp = jnp.concatenate([halo, x_ref[...]], axis=0)
        acc = _conv_taps(xp, w_ref[...], b_ref[...])[8:]
        o_ref[...] = acc * _sigmoid(acc)

    return _call(
        body, [proj, proj, conv_w, conv_b], name="conv_fwd", grid=(T // tr, XBC // CONV_TC),
        in_specs=[pl.BlockSpec((tr, CONV_TC), lambda i, j: (i, j + offb)),
                  pl.BlockSpec((8, CONV_TC), lambda i, j: (jnp.maximum(i * nb - 1, 0), j + offb)),
                  pl.BlockSpec((4, CONV_TC), lambda i, j: (0, j)),
                  pl.BlockSpec((1, CONV_TC), lambda i, j: (0, j))],
        out_specs=[pl.BlockSpec((tr, CONV_TC), lambda i, j: (i, j))],
        out_shape=[jax.ShapeDtypeStruct((T, XBC), F32)], sem=("parallel", "parallel"))[0]


def _conv_bwd_a(dxa, proj, conv_w, conv_b):
    T = proj.shape[0]
    tr = _row_tile(T)
    nb, offb = tr // 8, OFF_XBC // CONV_TC

    def body(d_ref, x_ref, h_ref, w_ref, b_ref, o_ref, sums_ref):
        i = pl.program_id(1)

        @pl.when(i == 0)
        def _():
            sums_ref[...] = jnp.zeros_like(sums_ref)

        halo = jnp.where(i > 0, h_ref[...], 0.0)
        xp = jnp.concatenate([halo, x_ref[...]], axis=0)
        w = w_ref[...]
        taps = [pltpu.roll(xp, 3 - k, 0)[8:] for k in range(3)] + [x_ref[...]]
        acc = b_ref[...] + w[3:4, :] * taps[3]
        for k in range(3):
            acc = acc + w[k:k + 1, :] * taps[k]
        s = _sigmoid(acc)
        dxc = d_ref[...] * (s * (1.0 + acc * (1.0 - s)))
        o_ref[...] = dxc
        for k in range(4):
            sums_ref[k:k + 1, :] += jnp.sum(dxc * taps[k], axis=0, keepdims=True)
        sums_ref[4:5, :] += jnp.sum(dxc, axis=0, keepdims=True)

    return _call(
        body, [dxa, proj, proj, conv_w, conv_b], name="conv_bwd_a", grid=(XBC // CONV_TC, T // tr),
        in_specs=[pl.BlockSpec((tr, CONV_TC), lambda j, i: (i, j)),
                  pl.BlockSpec((tr, CONV_TC), lambda j, i: (i, j + offb)),
                  pl.BlockSpec((8, CONV_TC), lambda j, i: (jnp.maximum(i * nb - 1, 0), j + offb)),
                  pl.BlockSpec((4, CONV_TC), lambda j, i: (0, j)),
                  pl.BlockSpec((1, CONV_TC), lambda j, i: (0, j))],
        out_specs=[pl.BlockSpec((tr, CONV_TC), lambda j, i: (i, j)), pl.BlockSpec((8, CONV_TC), lambda j, i: (0, j))],
        out_shape=[jax.ShapeDtypeStruct((T, XBC), F32), jax.ShapeDtypeStruct((8, XBC), F32)],
        sem=("parallel", "arbitrary"))


def _conv_bwd_b(dxc, conv_w, dproj):
    T = dxc.shape[0]
    tr = _row_tile(T)
    nb, offb, last = tr // 8, OFF_XBC // CONV_TC, T // tr - 1

    def body(d_ref, h_ref, w_ref, dp_in, o_ref):
        del dp_in
        halo = jnp.where(pl.program_id(0) < last, h_ref[...], 0.0)
        xp = jnp.concatenate([d_ref[...], halo], axis=0)
        n = xp.shape[0]
        w = w_ref[...]
        acc = w[3:4, :] * xp
        for k in range(3):
            acc = acc + w[k:k + 1, :] * pltpu.roll(xp, n - (3 - k), 0)
        o_ref[...] = acc[:tr].astype(BF16)

    return _call(
        body, [dxc, dxc, conv_w, dproj], name="conv_bwd_b", grid=(T // tr, XBC // CONV_TC),
        in_specs=[pl.BlockSpec((tr, CONV_TC), lambda i, j: (i, j)),
                  pl.BlockSpec((8, CONV_TC), lambda i, j: (jnp.minimum((i + 1) * nb, T // 8 - 1), j)),
                  pl.BlockSpec((4, CONV_TC), lambda i, j: (0, j)),
                  pl.BlockSpec(memory_space=pl.ANY)],
        out_specs=[pl.BlockSpec((tr, CONV_TC), lambda i, j: (i, j + offb))],
        out_shape=[jax.ShapeDtypeStruct(dproj.shape, BF16)], aliases={3: 0}, sem=("parallel", "parallel"))[0]


def _spread(v, sel, pieces):
    out = None
    for _ in range(pieces):
        p = v.astype(BF16)
        term = _nn(p, sel)
        out = term if out is None else out + term
        v = v - p.astype(F32)
    return out


def _ssd_selectors():
    g = np.arange(GROUPS)[:, None, None]
    piece = np.arange(128)[None, :, None]
    h = np.where(piece < 3 * HEADS, piece % HEADS, -1)
    blocks = (h == 8 * g + np.arange(1024)[None, None, :] // 128)
    pairs = (h == 8 * g + np.arange(512)[None, None, :] // HEAD_DIM)
    lane = np.arange(128)[None, None, :]
    block_sum = (lane == 8 * g + np.arange(1024)[None, :, None] // 128)
    pair_sum = (lane == 8 * g + np.arange(512)[None, :, None] // HEAD_DIM)
    return [jnp.asarray(m, BF16) for m in (blocks, pairs, block_sum, pair_sum)]


def _pack3(v):
    p0 = v.astype(BF16)
    r1 = v - p0.astype(F32)
    p1 = r1.astype(BF16)
    r2 = r1 - p1.astype(F32)
    return p0 + pltpu.roll(r1, HEADS, 1).astype(BF16) + pltpu.roll(r2, 2 * HEADS, 1).astype(BF16)


def _ssd_group(g, cs_p, csT, dt_p, s_mat, causal_w, lo, blocks_ref, pairs_ref):
    csb = _nn(cs_p, blocks_ref[g])
    row = jnp.concatenate([csT[8 * g + hh:8 * g + hh + 1, :] for hh in range(8)], axis=1)
    l_w = jnp.exp(jnp.where(causal_w, csb - row, NEG))
    m_w = jnp.concatenate([s_mat] * 8, axis=1) * l_w
    cs_g = jnp.concatenate([jnp.where(lo, csb[:, 256 * jj:256 * jj + 128], csb[:, 256 * jj + 128:256 * jj + 256])
                            for jj in range(4)], axis=1)
    cs_last = cs_g[Q - 1:Q, :]
    return m_w, l_w, _nn(dt_p, pairs_ref[g]), jnp.exp(cs_g), jnp.exp(cs_last - cs_g), jnp.exp(cs_last)


def _ssd_common(dtp_ref, dtb_r, alog_r, dtb_c, alog_c):
    rows = lax.broadcasted_iota(jnp.int32, (Q, Q), 0)
    cols = lax.broadcasted_iota(jnp.int32, (Q, Q), 1)
    tri = (cols <= rows).astype(F32)
    heads = lax.broadcasted_iota(jnp.int32, (1, 128), 1) < HEADS
    raw_w = dtp_ref[...] + dtb_r[...]
    dt_w = jnp.where(heads, _softplus(raw_w), 0.0)
    a_w = -jnp.exp(alog_r[...])
    cs_w = _nn(tri, dt_w * a_w, precision=HIGH)
    aT = _softplus(dtp_ref[...].T[0:HEADS, :] + dtb_c[...]) * (-jnp.exp(alog_c[...]))
    csT = _nt(aT, tri, precision=HIGH)
    return raw_w[:, 0:HEADS], dt_w[:, 0:HEADS], a_w[:, 0:HEADS], csT, _pack3(cs_w), _pack3(dt_w)


def _ssd_fwd(xbc_a, proj, dtb_r, alog_r, dtb_c, alog_c, dsk_exp):
    T = xbc_a.shape[0]
    nc = T // Q
    dtb_r, alog_r = [jnp.pad(a, ((0, 0), (0, 128 - HEADS))) for a in (dtb_r, alog_r)]

    def body(xbc_ref, dtp_ref, dtb_r_ref, alog_r_ref, dtb_c_ref, alog_c_ref, dsk_ref, blocks_ref, pairs_ref,
             y_ref, hin_ref, h_scr):
        @pl.when(pl.program_id(0) == 0)
        def _():
            h_scr[...] = jnp.zeros_like(h_scr)

        _, _, _, csT, cs_p, dt_p = _ssd_common(dtp_ref, dtb_r_ref, alog_r_ref, dtb_c_ref, alog_c_ref)
        lo = lax.broadcasted_iota(jnp.int32, (1, 128), 1) < HEAD_DIM
        hi = jnp.logical_not(lo)
        causal_w = (lax.broadcasted_iota(jnp.int32, (Q, 1024), 1) & (Q - 1)) <= lax.broadcasted_iota(jnp.int32, (Q, 1024), 0)
        for g in range(GROUPS):
            gs = slice(512 * g, 512 * (g + 1))
            hs = slice(128 * g, 128 * (g + 1))
            xs_g = xbc_ref[:, gs]
            b_g = xbc_ref[:, DI + STATE * g:DI + STATE * (g + 1)].astype(BF16)
            c_g = xbc_ref[:, DI + 512 + STATE * g:DI + 512 + STATE * (g + 1)].astype(BF16)
            m_w, _, dt_g, ecs_g, dec_g, cd_g = _ssd_group(g, cs_p, csT, dt_p, _nt(c_g, b_g), causal_w, lo, blocks_ref, pairs_ref)
            m_b = m_w.astype(BF16)
            xdt = xs_g * dt_g
            xdt_b = xdt.astype(BF16)
            ys = []
            for jj in range(4):
                xp = xdt_b[:, 128 * jj:128 * (jj + 1)]
                x_ab = jnp.concatenate([jnp.where(lo, xp, jnp.zeros_like(xp)), jnp.where(hi, xp, jnp.zeros_like(xp))], axis=0)
                ys.append(_nn(m_b[:, 256 * jj:256 * (jj + 1)], x_ab))
            h_g = h_scr[hs, :]
            hin_ref[0, hs, :] = h_g
            y_ref[:, gs] = jnp.concatenate(ys, axis=1) + _nn(c_g, h_g.astype(BF16)) * ecs_g + dsk_ref[:, gs] * xs_g
            h_scr[hs, :] = h_g * cd_g + _tn(b_g, (xdt * dec_g).astype(BF16))

    small_r = pl.BlockSpec((1, 128), lambda c: (0, 0))
    small_c = pl.BlockSpec((HEADS, 1), lambda c: (0, 0))
    blocks, pairs, _, _ = _ssd_selectors()
    whole = lambda a: pl.BlockSpec(a.shape, lambda c: (0,) * a.ndim)
    return _call(
        body, [xbc_a, proj, dtb_r, alog_r, dtb_c, alog_c, dsk_exp, blocks, pairs], name="ssd_fwd", grid=(nc,),
        in_specs=[pl.BlockSpec((Q, XBC), lambda c: (c, 0)),
                  pl.BlockSpec((Q, 128), lambda c: (c, OFF_DT // 128)),
                  small_r, small_r, small_c, small_c,
                  pl.BlockSpec((1, DI), lambda c: (0, 0)), whole(blocks), whole(pairs)],
        out_specs=[pl.BlockSpec((Q, DI), lambda c: (c, 0)), pl.BlockSpec((1, 512, 512), lambda c: (c, 0, 0))],
        out_shape=[jax.ShapeDtypeStruct((T, DI), F32), jax.ShapeDtypeStruct((nc, 512, 512), F32)],
        scratch_shapes=[pltpu.VMEM((512, 512), F32)], sem=("arbitrary",))


def _ssd_bwd(dy, xbc_a, proj, hin, dtb_r, alog_r, dtb_c, alog_c, dsk_exp, dproj, carry=None):
    T = xbc_a.shape[0]
    nc = T // Q
    dtb_r, alog_r = [jnp.pad(a, ((0, 0), (0, 128 - HEADS))) for a in (dtb_r, alog_r)]

    def body(dy_ref, xbc_ref, dtp_ref, hin_ref, dtb_r_ref, alog_r_ref, dtb_c_ref, alog_c_ref, dsk_ref, dp_in,
             blocks_ref, pairs_ref, block_sum_ref, pair_sum_ref, dxa_ref, dp_ref, dsk_sum_ref, small_ref, dh_scr):
        del dp_in

        @pl.when(pl.program_id(0) == 0)
        def _():
            dh_scr[...] = jnp.zeros_like(dh_scr)
            dsk_sum_ref[...] = jnp.zeros_like(dsk_sum_ref)
            small_ref[...] = jnp.zeros_like(small_ref)

        raw, dt, a_r, csT, cs_p, dt_p = _ssd_common(dtp_ref, dtb_r_ref, alog_r_ref, dtb_c_ref, alog_c_ref)
        lo = lax.broadcasted_iota(jnp.int32, (1, 128), 1) < HEAD_DIM
        hi = jnp.logical_not(lo)
        sub32 = lax.broadcasted_iota(jnp.int32, (HEADS, 1), 0)
        causal_w = (lax.broadcasted_iota(jnp.int32, (Q, 1024), 1) & (Q - 1)) <= lax.broadcasted_iota(jnp.int32, (Q, 1024), 0)
        dcs_c = jnp.zeros((Q, 128), F32)
        dcs_r = jnp.zeros((HEADS, Q), F32)
        dcs_l = jnp.zeros((8, 128), F32)
        ddt_x = jnp.zeros((Q, 128), F32)
        for g in range(GROUPS):
            gs = slice(512 * g, 512 * (g + 1))
            hs = slice(128 * g, 128 * (g + 1))
            xs_g, dy_g = xbc_ref[:, gs], dy_ref[:, gs]
            b_g = xbc_ref[:, DI + STATE * g:DI + STATE * (g + 1)].astype(BF16)
            c_g = xbc_ref[:, DI + 512 + STATE * g:DI + 512 + STATE * (g + 1)].astype(BF16)
            m_w, l_w, dt_g, ecs_g, dec_g, cd_g = _ssd_group(g, cs_p, csT, dt_p, _nt(c_g, b_g), causal_w, lo, blocks_ref, pairs_ref)
            m_b = m_w.astype(BF16)
            xdt = xs_g * dt_g
            xdt_b, dy_b = xdt.astype(BF16), dy_g.astype(BF16)
            dms, dxs = [], []
            for jj in range(4):
                xp, dyp = xdt_b[:, 128 * jj:128 * (jj + 1)], dy_b[:, 128 * jj:128 * (jj + 1)]
                dy_ab = jnp.concatenate([jnp.where(lo, dyp, jnp.zeros_like(dyp)), jnp.where(hi, dyp, jnp.zeros_like(dyp))], axis=0)
                dm_ab = _nt(dy_ab, xp)
                dms += [dm_ab[:Q], dm_ab[Q:]]
                dx_ab = _tn(m_b[:, 256 * jj:256 * (jj + 1)], dyp)
                dxs.append(jnp.where(lo, dx_ab[:Q], dx_ab[Q:]))
            dm_w = jnp.concatenate(dms, axis=1)
            w_w = dm_w * m_w
            dcs_c = dcs_c + _spread(w_w, block_sum_ref[g], 2)
            w_cols = jnp.sum(w_w, axis=0, keepdims=True)
            for hh in range(8):
                dcs_r = dcs_r + jnp.where(sub32 == 8 * g + hh, w_cols[:, 128 * hh:128 * (hh + 1)], 0.0)
            dl_w = dm_w * l_w
            ds_mat = dl_w[:, 0:128]
            for hh in range(1, 8):
                ds_mat = ds_mat + dl_w[:, 128 * hh:128 * (hh + 1)]
            hin_g = hin_ref[0, hs, :]
            hin_b = hin_g.astype(BF16)
            dh_g = dh_scr[hs, :]
            dh_b = dh_g.astype(BF16)
            g_mat = _nn(b_g, dh_b)
            xdec = xdt * dec_g
            xg = xdec * g_mat
            dxdt = jnp.concatenate(dxs, axis=1) + dec_g * g_mat
            sums = _spread(jnp.concatenate([dy_g * (_nn(c_g, hin_b) * ecs_g) - xg, dxdt * xs_g], axis=0), pair_sum_ref[g], 2)
            dcs_c = dcs_c + sums[:Q]
            ddt_x = ddt_x + sums[Q:]
            last = jnp.sum(xg, axis=0, keepdims=True) + jnp.sum(dh_g * hin_g, axis=0, keepdims=True) * cd_g
            dcs_l = dcs_l + _spread(jnp.broadcast_to(last, (8, 512)), pair_sum_ref[g], 2)
            dz = (dy_g * ecs_g).astype(BF16)
            ds_b = ds_mat.astype(BF16)
            dxa_ref[:, gs] = dxdt * dt_g + dy_g * dsk_ref[:, gs]
            dxa_ref[:, DI + STATE * g:DI + STATE * (g + 1)] = _nt(xdec.astype(BF16), dh_b) + _tn(ds_b, c_g)
            dxa_ref[:, DI + 512 + STATE * g:DI + 512 + STATE * (g + 1)] = _nt(dz, hin_b) + _nn(ds_b, b_g)
            dh_scr[hs, :] = _tn(c_g, dz) + dh_g * cd_g
            dsk_sum_ref[0:1, gs] += jnp.sum(dy_g * xs_g, axis=0, keepdims=True)

        rows = lax.broadcasted_iota(jnp.int32, (Q, Q), 0)
        cols = lax.broadcasted_iota(jnp.int32, (Q, Q), 1)
        tri_t = (cols >= rows).astype(F32)
        last_row = lax.broadcasted_iota(jnp.int32, (Q, 1), 0) == Q - 1
        dcs = (dcs_c + jnp.where(last_row, dcs_l[0:1, :], 0.0))[:, 0:HEADS]
        da = _nn(tri_t, dcs, precision=HIGH) - _nt(tri_t, dcs_r, precision=HIGH)
        ddt_raw = (ddt_x[:, 0:HEADS] + da * a_r) * _sigmoid(raw)
        small_ref[0:1, :] += jnp.sum(da * dt, axis=0, keepdims=True) * a_r
        small_ref[1:2, :] += jnp.sum(ddt_raw, axis=0, keepdims=True)
        dp_ref[...] = jnp.zeros_like(dp_ref)
        dp_ref[:, 0:HEADS] = ddt_raw.astype(BF16)

    rev = lambda c: nc - 1 - c
    small_r = pl.BlockSpec((1, 128), lambda c: (0, 0))
    small_c = pl.BlockSpec((HEADS, 1), lambda c: (0, 0))
    selectors = _ssd_selectors()
    whole = lambda a: pl.BlockSpec(a.shape, lambda c: (0,) * a.ndim)
    return _call(
        body, [dy, xbc_a, proj, hin, dtb_r, alog_r, dtb_c, alog_c, dsk_exp, dproj, *selectors], name="ssd_bwd", grid=(nc,),
        in_specs=[pl.BlockSpec((Q, DI), lambda c: (rev(c), 0)),
                  pl.BlockSpec((Q, XBC), lambda c: (rev(c), 0)),
                  pl.BlockSpec((Q, 128), lambda c: (rev(c), OFF_DT // 128)),
                  pl.BlockSpec((1, 512, 512), lambda c: (rev(c), 0, 0)),
                  small_r, small_r, small_c, small_c,
                  pl.BlockSpec((1, DI), lambda c: (0, 0)),
                  pl.BlockSpec(memory_space=pl.ANY)] + [whole(a) for a in selectors],
        out_specs=[pl.BlockSpec((Q, XBC), lambda c: (rev(c), 0)),
                   pl.BlockSpec((Q, 256), lambda c: (rev(c), OFF_DT // 256)),
                   pl.BlockSpec((8, DI), lambda c: (0, 0)),
                   pl.BlockSpec((8, HEADS), lambda c: (0, 0))],
        out_shape=[jax.ShapeDtypeStruct((T, XBC), F32), jax.ShapeDtypeStruct(dproj.shape, BF16),
                   jax.ShapeDtypeStruct((8, DI), F32), jax.ShapeDtypeStruct((8, HEADS), F32)],
        aliases={9: 1}, scratch_shapes=[pltpu.VMEM((512, 512), F32)], sem=("arbitrary",), carry=carry)


def _gate_norm(y, proj, w):
    T = y.shape[0]
    tr = _row_tile(T)

    def body(y_ref, z_ref, w_ref, o_ref):
        for g in range(GROUPS):
            gs = slice(512 * g, 512 * (g + 1))
            z = z_ref[:, gs]
            yg = y_ref[:, gs] * (z * _sigmoid(z))
            r = lax.rsqrt(jnp.mean(yg * yg, axis=-1, keepdims=True) + EPS)
            o_ref[:, gs] = (yg * r * w_ref[:, gs]).astype(BF16)

    tile = pl.BlockSpec((tr, DI), lambda i: (i, 0))
    return _call(body, [y, proj, w], name="gate_norm", grid=(T // tr,),
                 in_specs=[tile, tile, pl.BlockSpec((1, DI), lambda i: (0, 0))], out_specs=[tile],
                 out_shape=[jax.ShapeDtypeStruct((T, DI), BF16)], sem=("parallel",))[0]


def _gate_norm_bwd(dyn, y, proj, w, dproj):
    T = y.shape[0]
    tr = _row_tile(T)

    def body(d_ref, y_ref, z_ref, w_ref, dp_in, dy_ref, dz_ref, sums_ref):
        del dp_in

        @pl.when(pl.program_id(0) == 0)
        def _():
            sums_ref[...] = jnp.zeros_like(sums_ref)

        for g in range(GROUPS):
            gs = slice(512 * g, 512 * (g + 1))
            z, yv, d = z_ref[:, gs], y_ref[:, gs], d_ref[:, gs]
            s = _sigmoid(z)
            silu = z * s
            yg = yv * silu
            r = lax.rsqrt(jnp.mean(yg * yg, axis=-1, keepdims=True) + EPS)
            yn = yg * r
            sums_ref[0:1, gs] += jnp.sum(d * yn, axis=0, keepdims=True)
            dn = d * w_ref[:, gs]
            dyg = r * (dn - yn * jnp.mean(dn * yn, axis=-1, keepdims=True))
            dy_ref[:, gs] = dyg * silu
            dz_ref[:, gs] = (dyg * yv * (s * (1.0 + z * (1.0 - s)))).astype(BF16)

    tile = pl.BlockSpec((tr, DI), lambda i: (i, 0))
    return _call(
        body, [dyn, y, proj, w, dproj], name="gate_norm_bwd", grid=(T // tr,),
        in_specs=[tile, tile, tile, pl.BlockSpec((1, DI), lambda i: (0, 0)), pl.BlockSpec(memory_space=pl.ANY)],
        out_specs=[tile, tile, pl.BlockSpec((8, DI), lambda i: (0, 0))],
        out_shape=[jax.ShapeDtypeStruct((T, DI), F32), jax.ShapeDtypeStruct(dproj.shape, BF16),
                   jax.ShapeDtypeStruct((8, DI), F32)],
        aliases={4: 1}, sem=("arbitrary",))


def _pool_fwd(proj, pool_w_b, pool_scale):
    T = proj.shape[0]
    tr = _row_tile(T)
    nb = tr // 16

    def body(u_ref, h_ref, pw_ref, ps_ref, pooled_ref, pw_out_ref, yps_ref):
        i = pl.program_id(0)
        t = i * tr + lax.broadcasted_iota(jnp.int32, (tr, 1), 0)
        for g, win in enumerate(POOL_WINDOWS):
            gs = slice(GW * g, GW * (g + 1))
            u = u_ref[:, gs]
            s = jnp.concatenate([jnp.where(i > 0, h_ref[:, gs], 0.0), u], axis=0)
            sh = 1
            while sh < win:
                s = s + pltpu.roll(s, sh, 0)
                sh *= 2
            pooled = (s[16:] * (1.0 / jnp.minimum(t + 1, win).astype(F32)) - u).astype(BF16)
            pooled_ref[:, gs] = pooled
            pwv = _nn(pooled, pw_ref[g])
            pw_out_ref[:, gs] = pwv
            yps_ref[:, gs] = (pwv * ps_ref[:, gs]).astype(BF16)

    tile = pl.BlockSpec((tr, D), lambda i: (i, 0))
    return _call(
        body, [proj, proj, pool_w_b, pool_scale], name="pool_fwd", grid=(T // tr,),
        in_specs=[pl.BlockSpec((tr, D), lambda i: (i, OFF_POOL // D)),
                  pl.BlockSpec((16, D), lambda i: (jnp.maximum(i * nb - 1, 0), OFF_POOL // D)),
                  pl.BlockSpec((4, GW, GW), lambda i: (0, 0, 0)),
                  pl.BlockSpec((1, D), lambda i: (0, 0))],
        out_specs=[tile, tile, tile],
        out_shape=[jax.ShapeDtypeStruct((T, D), BF16), jax.ShapeDtypeStruct((T, D), F32),
                   jax.ShapeDtypeStruct((T, D), BF16)], sem=("parallel",))


def _pool_bwd(dyp, pw_out, pooled, pool_w_b, pool_scale, dproj):
    T = dyp.shape[0]
    tr = _row_tile(T)
    nb, last = tr // 16, T // tr - 1

    def body(d_ref, h_ref, pwo_ref, pooled_ref, pw_ref, ps_ref, dp_in, du_ref, gpw_ref, sums_ref):
        del dp_in
        i = pl.program_id(0)

        @pl.when(i == 0)
        def _():
            gpw_ref[...] = jnp.zeros_like(gpw_ref)
            sums_ref[...] = jnp.zeros_like(sums_ref)

        n = tr + 16
        t = i * tr + lax.broadcasted_iota(jnp.int32, (n, 1), 0)
        sums_ref[0:1, :] += jnp.sum(d_ref[...] * pwo_ref[...], axis=0, keepdims=True)
        for g, win in enumerate(POOL_WINDOWS):
            gs = slice(GW * g, GW * (g + 1))
            d_ext = jnp.concatenate([d_ref[:, gs], jnp.where(i < last, h_ref[:, gs], 0.0)], axis=0)
            dpw = (d_ext * ps_ref[:, gs]).astype(BF16)
            dpooled = _nt(dpw, pw_ref[g])
            s = jnp.where(t < T, dpooled * (1.0 / jnp.minimum(t + 1, win).astype(F32)), 0.0)
            sh = 1
            while sh < win:
                s = s + pltpu.roll(s, n - sh, 0)
                sh *= 2
            du_ref[:, gs] = (s[:tr] - dpooled[:tr]).astype(BF16)
            gpw_ref[g] += _tn(pooled_ref[:, gs], dpw[:tr])

    tile = pl.BlockSpec((tr, D), lambda i: (i, 0))
    return _call(
        body, [dyp, dyp, pw_out, pooled, pool_w_b, pool_scale, dproj], name="pool_bwd", grid=(T // tr,),
        in_specs=[tile, pl.BlockSpec((16, D), lambda i: (jnp.minimum((i + 1) * nb, T // 16 - 1), 0)), tile, tile,
                  pl.BlockSpec((4, GW, GW), lambda i: (0, 0, 0)), pl.BlockSpec((1, D), lambda i: (0, 0)),
                  pl.BlockSpec(memory_space=pl.ANY)],
        out_specs=[pl.BlockSpec((tr, D), lambda i: (i, OFF_POOL // D)),
                   pl.BlockSpec((4, GW, GW), lambda i: (0, 0, 0)), pl.BlockSpec((8, D), lambda i: (0, 0))],
        out_shape=[jax.ShapeDtypeStruct(dproj.shape, BF16), jax.ShapeDtypeStruct((4, GW, GW), F32),
                   jax.ShapeDtypeStruct((8, D), F32)],
        aliases={6: 0}, sem=("arbitrary",))


def _merge(proj, y_ssd, y_pool):
    T = proj.shape[0]
    tr = _row_tile(T)

    def body(g_ref, a_ref, b_ref, o_ref):
        o_ref[...] = (_sigmoid(g_ref[:, 0:D]) * a_ref[...] + _sigmoid(g_ref[:, D:2 * D]) * b_ref[...]).astype(BF16)

    tile = pl.BlockSpec((tr, D), lambda i: (i, 0))
    return _call(body, [proj, y_ssd, y_pool], name="merge", grid=(T // tr,),
                 in_specs=[pl.BlockSpec((tr, 2 * D), lambda i: (i, OFF_GATE // (2 * D))), tile, tile], out_specs=[tile],
                 out_shape=[jax.ShapeDtypeStruct((T, D), BF16)], sem=("parallel",))[0]


def _merge_bwd(dmerged, proj, y_ssd, y_pool):
    T = proj.shape[0]
    tr = _row_tile(T)

    def body(d_ref, g_ref, a_ref, b_ref, da_ref, db_ref, dg_ref):
        d = d_ref[...]
        ga, gb = _sigmoid(g_ref[:, 0:D]), _sigmoid(g_ref[:, D:2 * D])
        da_ref[...] = (d * ga).astype(BF16)
        db_ref[...] = (d * gb).astype(BF16)
        dg_ref[:, 0:D] = (d * a_ref[...] * ga * (1.0 - ga)).astype(BF16)
        dg_ref[:, D:2 * D] = (d * b_ref[...] * gb * (1.0 - gb)).astype(BF16)

    tile = pl.BlockSpec((tr, D), lambda i: (i, 0))
    gates = pl.BlockSpec((tr, 2 * D), lambda i: (i, OFF_GATE // (2 * D)))
    return _call(body, [dmerged, proj, y_ssd, y_pool], name="merge_bwd", grid=(T // tr,),
                 in_specs=[tile, gates, tile, tile], out_specs=[tile, tile, gates],
                 out_shape=[jax.ShapeDtypeStruct((T, D), BF16), jax.ShapeDtypeStruct((T, D), BF16),
                            jax.ShapeDtypeStruct((T, NP), BF16)], sem=("parallel",))


def _adamw(w, g, m, v, name, carry=None):
    R, C = w.shape
    tr = R if R <= 128 else 128
    assert R % tr == 0

    def body(w_ref, g_ref, m_ref, v_ref, d_ref, mo_ref, vo_ref):
        gv = g_ref[...]
        mn = ADAM_B1 * m_ref[...] + (1.0 - ADAM_B1) * gv
        vn = ADAM_B2 * v_ref[...] + (1.0 - ADAM_B2) * (gv * gv)
        m_hat = mn * (1.0 / (1.0 - ADAM_B1 ** ADAM_STEP))
        v_hat = vn * (1.0 / (1.0 - ADAM_B2 ** ADAM_STEP))
        d_ref[...] = -ADAM_LR * (m_hat / (jnp.sqrt(v_hat) + ADAM_EPS) + ADAM_WD * w_ref[...])
        mo_ref[...] = mn
        vo_ref[...] = vn

    tile = pl.BlockSpec((tr, C), lambda i: (i, 0))
    sds = jax.ShapeDtypeStruct((R, C), F32)
    return _call(body, [w, g, m, v], name=name, grid=(R // tr,), in_specs=[tile] * 4, out_specs=[tile] * 3,
                 out_shape=[sds] * 3, sem=("parallel",), carry=carry)


def _me():
    return lax.axis_index("x"), lax.axis_index("y"), lax.axis_index("c")


def _xor_peer(x, y, c, p):
    return (x ^ ((p >> 2) & 1), y ^ ((p >> 1) & 1), c ^ (p & 1))


def _ada_fwd(c_row, w_ada, b_ada_mine, carry=None):
    n_cols = w_ada.shape[1]

    def body(c_ref, w_ref, b_ref, mod_ref, c8_ref, csend, mpart, modbuf, send_sems, recv_sems):
        x, y, c = _me()
        me = 4 * x + 2 * y + c
        chip = 2 * x + y
        csend[...] = jnp.broadcast_to(c_ref[...], csend.shape)
        c8_ref[me] = csend[...]

        def c_copy(p):
            return pltpu.make_async_remote_copy(
                src_ref=csend, dst_ref=c8_ref.at[me], send_sem=send_sems.at[p - 1], recv_sem=recv_sems.at[p - 1],
                device_id=_xor_peer(x, y, c, p), device_id_type=MESH)

        for p in range(1, 8):
            c_copy(p).start()
        for p in range(1, 8):
            c_copy(p).wait_recv()
        cs = jnp.concatenate([c8_ref[d][0:1, :] for d in range(8)], axis=0)
        mpart[...] = _nn(cs * _sigmoid(cs), w_ref[...], precision=HIGH) + b_ref[...]
        modbuf[chip] = mpart[...]

        def m_copy(m):
            return pltpu.make_async_remote_copy(
                src_ref=mpart, dst_ref=modbuf.at[chip], send_sem=send_sems.at[6 + m], recv_sem=recv_sems.at[6 + m],
                device_id=_xor_peer(x, y, c, 2 * m), device_id_type=MESH)

        for m in range(1, 4):
            m_copy(m).start()
        for m in range(1, 4):
            m_copy(m).wait_recv()
        mine = lax.broadcasted_iota(jnp.int32, (8, 1), 0) == me
        for k in range(N_CHIPS):
            mod_ref[:, n_cols * k:n_cols * (k + 1)] = jnp.sum(jnp.where(mine, modbuf[k], 0.0), axis=0, keepdims=True)
        for p in range(1, 8):
            c_copy(p).wait_send()
        for m in range(1, 4):
            m_copy(m).wait_send()

    vmem = pl.BlockSpec(memory_space=pltpu.VMEM)
    return _call(
        body, [c_row, w_ada, b_ada_mine], name="ada_fwd", in_specs=[vmem, vmem, vmem], out_specs=[vmem, vmem],
        out_shape=[jax.ShapeDtypeStruct((1, N_CHIPS * n_cols), F32), jax.ShapeDtypeStruct((8, 8, D), F32)],
        scratch_shapes=[pltpu.VMEM((8, D), F32), pltpu.VMEM((8, n_cols), F32), pltpu.VMEM((N_CHIPS, 8, n_cols), F32),
                        pltpu.SemaphoreType.DMA((10,)), pltpu.SemaphoreType.DMA((10,))], carry=carry)


def _gather_small(vec, carry=None):
    rows = vec.shape[0]

    def body(v_ref, all_ref, tot_ref, dsk_ref, send_sems, recv_sems):
        x, y, c = _me()
        me = 4 * x + 2 * y + c
        all_ref[me] = v_ref[...]

        def copy(p):
            return pltpu.make_async_remote_copy(
                src_ref=v_ref, dst_ref=all_ref.at[me], send_sem=send_sems.at[p - 1], recv_sem=recv_sems.at[p - 1],
                device_id=_xor_peer(x, y, c, p), device_id_type=MESH)

        for p in range(1, 8):
            copy(p).start()
        for p in range(1, 8):
            copy(p).wait_recv()
        tot = all_ref[0]
        for d in range(1, 8):
            tot = tot + all_ref[d]
        tot_ref[...] = tot
        seg = tot[SMALL_OFF["d_skip"] // 128:SMALL_OFF["d_skip"] // 128 + 16, :]
        lane = lax.broadcasted_iota(jnp.int32, (1, 128), 1)
        sa = jnp.sum(jnp.where(lane < HEAD_DIM, seg, 0.0), axis=1, keepdims=True)
        sb = jnp.sum(jnp.where(lane < HEAD_DIM, 0.0, seg), axis=1, keepdims=True)
        dsk_ref[...] = jnp.where(lane == 0, sa, jnp.where(lane == 1, sb, 0.0))
        for p in range(1, 8):
            copy(p).wait_send()

    vmem = pl.BlockSpec(memory_space=pltpu.VMEM)
    return _call(
        body, [vec], name="gather_small", in_specs=[vmem], out_specs=[vmem, vmem, vmem],
        out_shape=[jax.ShapeDtypeStruct((8, rows, 128), F32), jax.ShapeDtypeStruct((rows, 128), F32),
                   jax.ShapeDtypeStruct((16, 128), F32)],
        scratch_shapes=[pltpu.SemaphoreType.DMA((7,)), pltpu.SemaphoreType.DMA((7,))], carry=carry)


def _gather_carry(shards):
    n = len(shards)

    def copies(ins, outs, sems):
        x, y, c = _me()
        chip = 2 * x + y

        def half(w, which):
            h = shards[w].shape[0] // 2
            return pl.ds(which * h, h)

        def first(w, m):
            return pltpu.make_async_remote_copy(
                src_ref=ins[w].at[half(w, c)], dst_ref=outs[w].at[chip, half(w, c)],
                send_sem=sems.send(6 * w + m - 1), recv_sem=sems.recv(6 * w + m - 1),
                device_id=_xor_peer(x, y, c, 2 * m), device_id_type=MESH)

        def landed(w, m):
            return pltpu.make_async_remote_copy(
                src_ref=ins[w].at[half(w, c)], dst_ref=outs[w].at[chip ^ m, half(w, c)],
                send_sem=sems.send(6 * w + m - 1), recv_sem=sems.recv(6 * w + m - 1),
                device_id=_xor_peer(x, y, c, 2 * m), device_id_type=MESH)

        def passed(w, m, which):
            part = outs[w].at[chip ^ m, half(w, which)]
            return pltpu.make_async_remote_copy(
                src_ref=part, dst_ref=part, send_sem=sems.send(6 * w + 2 + m), recv_sem=sems.recv(6 * w + 2 + m),
                device_id=(x, y, 1 - c), device_id_type=MESH)

        return c, first, landed, passed

    pairs = [(w, m) for w in range(n) for m in range(1, 4)]

    def start(ins, outs, sems):
        _, first, _, _ = copies(ins, outs, sems)
        for w, m in pairs:
            first(w, m).start()

    def finish(ins, outs, sems):
        c, first, landed, passed = copies(ins, outs, sems)
        for w, m in pairs:
            landed(w, m).wait_recv()
            passed(w, m, c).start()
        for w, m in pairs:
            passed(w, m, 1 - c).wait_recv()
        for w, m in pairs:
            first(w, m).wait_send()
            passed(w, m, c).wait_send()

    return _Carry(shards, [jax.ShapeDtypeStruct((N_CHIPS,) + s.shape, s.dtype) for s in shards], 6 * n, start, finish)


def _pair_exchange_carry(grads):
    n = len(grads)

    def copy(ins, outs, sems, w):
        x, y, c = _me()
        h = grads[w].shape[1] // 2
        return pltpu.make_async_remote_copy(
            src_ref=ins[w].at[:, pl.ds((1 - c) * h, h)], dst_ref=outs[w],
            send_sem=sems.send(w), recv_sem=sems.recv(w), device_id=(x, y, 1 - c), device_id_type=MESH)

    def start(ins, outs, sems):
        for w in range(n):
            copy(ins, outs, sems, w).start()

    def finish(ins, outs, sems):
        for w in range(n):
            copy(ins, outs, sems, w).wait()

    return _Carry(grads, [jax.ShapeDtypeStruct((N_CHIPS, g.shape[1] // 2, g.shape[2]), g.dtype) for g in grads], n,
                  start, finish)


def _chip_exchange_carry(partials):
    n = len(partials)

    def copier(ins, outs, sems):
        x, y, c = _me()
        chip = 2 * x + y

        def copy(w, m, landed):
            return pltpu.make_async_remote_copy(
                src_ref=ins[w].at[chip ^ m], dst_ref=outs[w].at[(chip ^ m) if landed else chip],
                send_sem=sems.send(3 * w + m - 1), recv_sem=sems.recv(3 * w + m - 1),
                device_id=_xor_peer(x, y, c, 2 * m), device_id_type=MESH)

        return copy

    pairs = [(w, m) for w in range(n) for m in range(1, 4)]

    def start(ins, outs, sems):
        copy = copier(ins, outs, sems)
        for w, m in pairs:
            copy(w, m, False).start()

    def finish(ins, outs, sems):
        copy = copier(ins, outs, sems)
        for w, m in pairs:
            copy(w, m, True).wait_recv()
        for w, m in pairs:
            copy(w, m, False).wait_send()

    return _Carry(partials, [jax.ShapeDtypeStruct(p.shape, p.dtype) for p in partials], 3 * n, start, finish)


def _pair_share_carry(shards):
    n = len(shards)

    def copier(ins, outs, sems):
        x, y, c = _me()

        def copy(w, which):
            h = shards[w].shape[0] // 2
            rows = pl.ds(which * h, h)
            return pltpu.make_async_remote_copy(
                src_ref=ins[w].at[rows], dst_ref=outs[w].at[rows],
                send_sem=sems.send(w), recv_sem=sems.recv(w), device_id=(x, y, 1 - c), device_id_type=MESH)

        return c, copy

    def start(ins, outs, sems):
        c, copy = copier(ins, outs, sems)
        for w in range(n):
            copy(w, c).start()

    def finish(ins, outs, sems):
        c, copy = copier(ins, outs, sems)
        for w in range(n):
            copy(w, 1 - c).wait_recv()
        for w in range(n):
            copy(w, c).wait_send()

    return _Carry(shards, [jax.ShapeDtypeStruct(s.shape, s.dtype) for s in shards], n, start, finish,
                  aliased=[(w, w) for w in range(n)])


def _pair_sum(g, part, idx, name):
    _, h, C = part.shape
    tr = min(512, h)
    nb = h // tr

    def body(idx_ref, g_ref, p_ref, o16_ref, own_ref):
        v = g_ref[...].astype(F32) + p_ref[...].astype(F32)
        o16_ref[...] = v.astype(BF16)

        @pl.when(pl.program_id(1) == idx_ref[1])
        def _():
            own_ref[...] = v

    return pl.pallas_call(
        body, name=name,
        grid_spec=pltpu.PrefetchScalarGridSpec(
            num_scalar_prefetch=1, grid=(nb, N_CHIPS),
            in_specs=[pl.BlockSpec((None, tr, C), lambda i, s, idx_ref: (s, idx_ref[0] * nb + i, 0)),
                      pl.BlockSpec((None, tr, C), lambda i, s, idx_ref: (s, i, 0))],
            out_specs=[pl.BlockSpec((None, tr, C), lambda i, s, idx_ref: (s, i, 0)),
                       pl.BlockSpec((tr, C), lambda i, s, idx_ref: (i, 0))]),
        out_shape=[jax.ShapeDtypeStruct(part.shape, BF16), jax.ShapeDtypeStruct((h, C), F32)],
        compiler_params=pltpu.CompilerParams(dimension_semantics=("arbitrary", "arbitrary"), vmem_limit_bytes=VMEM_LIMIT),
    )(idx, g, part)


def _chip_sum(own, slots, idx, name):
    h, C = own.shape
    tr = min(512, h)
    nb = h // tr

    def body(idx_ref, own_ref, s1_ref, s2_ref, s3_ref, o_ref):
        del idx_ref
        o_ref[...] = ((own_ref[...] + s1_ref[...].astype(F32)) + s2_ref[...].astype(F32)) + s3_ref[...].astype(F32)

    def slot(m):
        return pl.BlockSpec((None, tr, C), lambda i, idx_ref: (idx_ref[1] ^ m, i, 0))

    return pl.pallas_call(
        body, name=name,
        grid_spec=pltpu.PrefetchScalarGridSpec(
            num_scalar_prefetch=1, grid=(nb,),
            in_specs=[pl.BlockSpec((tr, C), lambda i, idx_ref: (i, 0)), slot(1), slot(2), slot(3)],
            out_specs=pl.BlockSpec((tr, C), lambda i, idx_ref: (idx_ref[0] * nb + i, 0))),
        out_shape=jax.ShapeDtypeStruct((2 * h, C), F32),
        compiler_params=pltpu.CompilerParams(dimension_semantics=("parallel",), vmem_limit_bytes=VMEM_LIMIT),
    )(idx, own, slots, slots, slots)


class _Reducer:
    def __init__(self, idx):
        self.idx, self.chips, self.p16, self.own, self.mine, self.final = idx, {}, {}, {}, {}, {}

    def add(self, name, whole, chip_blocks=False):
        self.chips[name] = whole if chip_blocks else _chips_from_whole(name, whole)

    def pair(self, names):
        return _pair_exchange_carry([self.chips[n] for n in names])

    def take_pair(self, names, outs):
        for n, part in zip(names, outs):
            self.p16[n], self.own[n] = _pair_sum(self.chips.pop(n), part, self.idx, "pair_sum_" + n)

    def chip(self, names):
        return _chip_exchange_carry([self.p16[n] for n in names])

    def take_chip(self, names, outs):
        for n, slots in zip(names, outs):
            del self.p16[n]
            self.mine[n] = _chip_sum(self.own.pop(n), slots, self.idx, "chip_sum_" + n)

    def share(self, names):
        return _pair_share_carry([self.mine[n] for n in names])

    def take_share(self, names, outs):
        for n, s in zip(names, outs):
            del self.mine[n]
            self.final[n] = s


def _w_ada_grad(c8, dmod_cols):
    n_cols = dmod_cols.shape[1]
    tn = 512

    def body(c_ref, d_ref, o_ref):
        cv = c_ref[...]
        o_ref[...] = _tn(cv * _sigmoid(cv), d_ref[...], precision=HIGH)

    return _call(body, [c8, dmod_cols], name="w_ada_grad", grid=(n_cols // tn,),
                 in_specs=[pl.BlockSpec((8, D), lambda j: (0, 0)), pl.BlockSpec((8, tn), lambda j: (0, j))],
                 out_specs=[pl.BlockSpec((D, tn), lambda j: (0, j))],
                 out_shape=[jax.ShapeDtypeStruct((D, n_cols), F32)], sem=("parallel",))[0]


_SMALL_SEGS = (("dmod", 6144), ("norm_mix_w", 1024), ("conv_b", 3072), ("ssd_norm_w", 2048), ("pool_scale", 1024),
               ("norm_mlp_w", 1024), ("norm_final_w", 1024), ("conv_w", 4 * XBC), ("d_skip", 2048), ("a_log", 128),
               ("dt_bias", 128), ("loss", 128))
SMALL_OFF = {}
_o = 0
for _n, _s in _SMALL_SEGS:
    SMALL_OFF[_n] = _o
    _o += _s
SMALL_LEN = -(-_o // 1024) * 1024

_FIRST = ("w_in", "conv_w")
_LATER = ("w_branch_ssd", "pool_w", "w_branch_pool", "w_out", "w_up", "w_down")
_SMALL_REPLICATED = ("b_ada", "norm_mix_w", "conv_b", "dt_bias", "a_log", "d_skip", "ssd_norm_w", "pool_scale",
                     "norm_mlp_w", "norm_final_w")
_WEIGHTS = ("w_ada", "b_ada", "norm_mix_w", "w_in", "conv_w", "conv_b", "dt_bias", "a_log", "d_skip", "ssd_norm_w",
            "w_branch_ssd", "pool_w", "pool_scale", "w_branch_pool", "w_out", "norm_mlp_w", "w_up", "w_down",
            "norm_final_w")


def _shard_2d(name, a):
    if name == "conv_w":
        return a.reshape(16, -1)
    return (a.reshape(GW, GW) if name == "pool_w" else a.reshape(a.shape[-2], a.shape[-1])).astype(BF16)


def _whole_from_chips(name, g, own, chip):
    g = lax.dynamic_update_slice(g, own[None], (chip, 0, 0))
    if name == "w_in":
        a, b = _DT_IN_CHIP2, _DT_IN_CHIP2 + HEADS
        pad = jnp.zeros((D, NP - IN_COLS), g.dtype)
        return jnp.concatenate([g[0], g[1], g[2][:, :a], g[2][:, b:], g[3], g[2][:, a:b], pad], axis=1)
    if name == "w_up":
        return jnp.concatenate([g[k] for k in range(N_CHIPS)], axis=1)
    if name == "pool_w":
        return jnp.transpose(g.reshape(N_CHIPS, 4, GW // N_CHIPS, GW), (1, 0, 2, 3)).reshape(4, GW, GW)
    if name == "conv_w":
        return jnp.transpose(g.reshape(N_CHIPS, 4, XBC // N_CHIPS), (1, 0, 2)).reshape(4, XBC)
    return g.reshape(N_CHIPS * g.shape[1], g.shape[2])


def _chips_from_whole(name, g):
    if name.startswith("w_in"):
        cw, a = IN_COLS // N_CHIPS, _DT_IN_CHIP2
        chip2 = jnp.concatenate([g[:, 2 * cw:2 * cw + a], g[:, OFF_DT:OFF_DT + HEADS], g[:, 5120:3 * cw - HEADS]], axis=1)
        return jnp.stack([g[:, :cw], g[:, cw:2 * cw], chip2, g[:, 3 * cw - HEADS:OFF_DT]])
    if name == "w_up":
        return jnp.transpose(g.reshape(D, N_CHIPS, DFF // N_CHIPS), (1, 0, 2))
    if name == "pool_w":
        return jnp.transpose(g.reshape(4, N_CHIPS, GW // N_CHIPS, GW), (1, 0, 2, 3)).reshape(N_CHIPS, GW, GW)
    return g.reshape(N_CHIPS, g.shape[0] // N_CHIPS, g.shape[1])


def kernel(x, c, w_ada, b_ada, norm_mix_w, w_in, conv_w, conv_b, dt_bias, a_log, d_skip, ssd_norm_w, w_branch_ssd, pool_w, pool_scale, w_branch_pool, w_out, norm_mlp_w, w_up, w_down, norm_final_w, loss_target, m_w_ada, m_b_ada, m_norm_mix_w, m_w_in, m_conv_w, m_conv_b, m_dt_bias, m_a_log, m_d_skip, m_ssd_norm_w, m_w_branch_ssd, m_pool_w, m_pool_scale, m_w_branch_pool, m_w_out, m_norm_mlp_w, m_w_up, m_w_down, m_norm_final_w, v_w_ada, v_b_ada, v_norm_mix_w, v_w_in, v_conv_w, v_conv_b, v_dt_bias, v_a_log, v_d_skip, v_ssd_norm_w, v_w_branch_ssd, v_pool_w, v_pool_scale, v_w_branch_pool, v_w_out, v_norm_mlp_w, v_w_up, v_w_down, v_norm_final_w):
    args = locals()
    w = {n: args[n] for n in _WEIGHTS}
    m = {n: args["m_" + n] for n in _WEIGHTS}
    v = {n: args["v_" + n] for n in _WEIGHTS}
    xi, yi, ci = _me()
    chip = 2 * xi + yi
    idx = jnp.stack([ci, chip]).astype(jnp.int32)
    ada_cols = w_ada.shape[-1]
    xs, target = x[0], loss_target[0]
    two_d = lambda n, a: a.reshape(GW, GW) if n == "pool_w" else a.reshape(-1, a.shape[-1])
    delta, new_m, new_v, g = {}, {}, {}, {}

    def adamw(n, carry=None):
        res = _adamw(two_d(n, w[n]), two_d(n, g[n]), two_d(n, m[n]), two_d(n, v[n]), "adamw_" + n, carry=carry)
        (delta[n], new_m[n], new_v[n]), extra = res if carry is not None else (res, None)
        return extra

    b_mine = lax.dynamic_slice(b_ada, (0, chip * ada_cols), (1, ada_cols))
    shards = {n: _shard_2d(n, w[n]) for n in _FIRST + _LATER}
    mod, c8 = _ada_fwd(c, w_ada[0], b_mine)
    c8 = c8[:, 0, :]
    shift_m, scale_m, gate_m, shift_f, scale_f, gate_f = [mod[:, D * i:D * (i + 1)] for i in range(6)]
    nf_w = norm_final_w.reshape(1, D)

    h1, first = _norm_mod(xs, norm_mix_w, scale_m, shift_m, "norm_mod_mix",
                          carry=_gather_carry([shards[n] for n in _FIRST]))
    p ={n: _whole_from_chips(n, a, shards[n], chip) for n, a in zip(_FIRST, first)}
    (proj,), later = _matmul(h1, p["w_in"], mode="nn", out_dtypes=[F32], name="mm_proj", cols_outer=True,
                             carry=_gather_carry([shards[n] for n in _LATER]))
    p.update({n: _whole_from_chips(n, a, shards[n], chip) for n, a in zip(_LATER, later)})
    xbc_a = _conv_fwd(proj, p["conv_w"], conv_b)
    dtb_c, alog_c = dt_bias.reshape(HEADS, 1), a_log.reshape(HEADS, 1)
    dsk_exp = jnp.repeat(d_skip, HEAD_DIM, axis=1)
    y, hin = _ssd_fwd(xbc_a, proj, dt_bias, a_log, dtb_c, alog_c, dsk_exp)
    yn = _gate_norm(y, proj, ssd_norm_w)
    (y_ssd,) = _matmul(yn, p["w_branch_ssd"], mode="nn", out_dtypes=[F32], name="mm_branch_ssd")
    pooled, pw_out, yps = _pool_fwd(proj, p["pool_w"], pool_scale)
    (y_pool,) = _matmul(yps, p["w_branch_pool"], mode="nn", out_dtypes=[F32], name="mm_branch_pool")
    merged = _merge(proj, y_ssd, y_pool)
    resid = lambda acc, r, gt: (r + gt * acc, acc)
    x2, mix = _matmul(merged, p["w_out"], mode="nn", out_dtypes=[F32, BF16], name="mm_out",
                      epi=resid, tile_extras=(xs,), row_extras=(gate_m,))
    h2 = _norm_mod(x2, norm_mlp_w, scale_f, shift_f, "norm_mod_mlp")
    relu2 = lambda acc: (jnp.square(jnp.maximum(acc, 0.0)),)
    (act,) = _matmul(h2, p["w_up"], mode="nn", out_dtypes=[BF16], name="mm_up", epi=relu2)
    x3, down = _matmul(act, p["w_down"], mode="nn", out_dtypes=[F32, BF16], name="mm_down",
                       epi=resid, tile_extras=(x2,), row_extras=(gate_f,))

    red = _Reducer(idx)
    dx3, d_down, sums_f = _final_loss_bwd(x3, target, nf_w, down, gate_f)
    drelu2 = lambda acc, a: (acc * (2.0 * jnp.sqrt(a)).astype(F32),)
    (dup,) = _matmul(d_down, p["w_down"], mode="nt", out_dtypes=[BF16], name="mm_dact",
                     epi=drelu2, tile_extras=(act,))
    red.add("w_down", _matmul(act, d_down, mode="tn", out_dtypes=[BF16], name="mm_g_down")[0])
    (dh2,), got = _matmul(dup, p["w_up"], mode="nt", out_dtypes=[F32], name="mm_dh2",
                          carry=red.pair(["w_down"]))
    red.take_pair(["w_down"], got)
    red.add("w_up", _matmul(h2, dup, mode="tn", out_dtypes=[BF16], name="mm_g_up", chip_blocks=True)[0], chip_blocks=True)
    dx2, sums_2, dmix = _norm_mod_bwd(x2, dh2, dx3, norm_mlp_w, scale_f, "norm_mod_mlp_bwd", branch=mix, gate=gate_m)
    (dmerged,), got = _matmul(dmix, p["w_out"], mode="nt", out_dtypes=[F32], name="mm_dmerged",
                              carry=red.pair(["w_up"]))
    red.take_pair(["w_up"], got)
    red.add("w_out", _matmul(merged, dmix, mode="tn", out_dtypes=[BF16], name="mm_g_out")[0])
    dy_ssd, dy_pool, dproj = _merge_bwd(dmerged, proj, y_ssd, y_pool)
    (dyp,), got = _matmul(dy_pool, p["w_branch_pool"], mode="nt", out_dtypes=[F32], name="mm_dyp",
                          carry=red.pair(["w_out"]))
    red.take_pair(["w_out"], got)
    red.add("w_branch_pool", _matmul(yps, dy_pool, mode="tn", out_dtypes=[BF16], name="mm_g_bpool")[0])
    dproj, g_pool_w, sums_pool = _pool_bwd(dyp, pw_out, pooled, p["pool_w"], pool_scale, dproj)
    red.add("pool_w", g_pool_w.astype(BF16))
    red.add("w_branch_ssd", _matmul(yn, dy_ssd, mode="tn", out_dtypes=[BF16], name="mm_g_bssd")[0])
    mixers = ["w_branch_pool", "pool_w", "w_branch_ssd"]
    (dyn,), got = _matmul(dy_ssd, p["w_branch_ssd"], mode="nt", out_dtypes=[F32], name="mm_dyn",
                          carry=red.pair(mixers))
    red.take_pair(mixers, got)
    dy, dproj, sums_gn = _gate_norm_bwd(dyn, y, proj, ssd_norm_w, dproj)
    six = ["w_down", "w_up", "w_out"] + mixers
    (dxa, dproj, dsk_sum, ssd_small), got = _ssd_bwd(dy, xbc_a, proj, hin, dt_bias, a_log, dtb_c, alog_c, dsk_exp,
                                                     dproj, carry=red.chip(six))
    red.take_chip(six, got)
    dxc, sums_conv = _conv_bwd_a(dxa, proj, p["conv_w"], conv_b)
    dproj = _conv_bwd_b(dxc, p["conv_w"], dproj)
    rows_a = 3 * D // 4
    (g_in_a,), got = _matmul(h1, dproj, mode="tn", out_dtypes=[BF16], name="mm_g_in_a", a_cols=(0, rows_a),
                             carry=red.share(six))
    red.take_share(six, got)
    red.add("w_in_a", g_in_a)
    (g_in_b,), got = _matmul(h1, dproj, mode="tn", out_dtypes=[BF16], name="mm_g_in_b", a_cols=(rows_a, D - rows_a),
                             carry=red.pair(["w_in_a"]))
    red.take_pair(["w_in_a"], got)
    red.add("w_in_b", g_in_b)
    (dh1,), got = _matmul(dproj, p["w_in"], mode="nt", out_dtypes=[F32], name="mm_dh1",
                          carry=_join(red.chip(["w_in_a"]), red.pair(["w_in_b"])))
    red.take_chip(["w_in_a"], got[:1])
    red.take_pair(["w_in_b"], got[1:])
    grad_x, sums_1 = _norm_mod_bwd(xs, dh1, dx2, norm_mix_w, scale_m, "norm_mod_mix_bwd")

    dmod = jnp.concatenate([sums_1[0:1], sums_1[1:2], sums_2[3:4], sums_2[0:1], sums_2[1:2], sums_f[1:2]], axis=1)
    pad96 = jnp.zeros((1, 96), F32)
    small = {"dmod": dmod, "norm_mix_w": sums_1[2:3], "conv_b": sums_conv[4:5], "ssd_norm_w": sums_gn[0:1],
             "pool_scale": sums_pool[0:1], "norm_mlp_w": sums_2[2:3], "norm_final_w": sums_f[0:1],
             "conv_w": sums_conv[0:4].reshape(1, 4 * XBC), "d_skip": dsk_sum[0:1],
             "a_log": jnp.concatenate([ssd_small[0:1], pad96], axis=1),
             "dt_bias": jnp.concatenate([ssd_small[1:2], pad96], axis=1), "loss": sums_f[3:4, 0:128]}
    vec = jnp.concatenate([small[n] for n, _ in _SMALL_SEGS], axis=1)
    vec = jnp.pad(vec, ((0, 0), (0, SMALL_LEN - vec.shape[1]))).reshape(SMALL_LEN // 128, 128)
    (every, total, dsk), got = _gather_small(vec, carry=_join(red.chip(["w_in_b"]), red.share(["w_in_a"])))
    red.take_chip(["w_in_b"], got[:1])
    red.take_share(["w_in_a"], got[1:])
    total = total.reshape(1, SMALL_LEN)
    seg = lambda n, size: total[:, SMALL_OFF[n]:SMALL_OFF[n] + size]
    g.update({"b_ada": seg("dmod", 6 * D), "norm_mix_w": seg("norm_mix_w", D), "conv_b": seg("conv_b", XBC),
              "dt_bias": seg("dt_bias", HEADS), "a_log": seg("a_log", HEADS), "d_skip": dsk[:, 0:2].reshape(1, HEADS),
              "ssd_norm_w": seg("ssd_norm_w", DI), "pool_scale": seg("pool_scale", D),
              "norm_mlp_w": seg("norm_mlp_w", D), "norm_final_w": seg("norm_final_w", D)})
    loss = total[0, SMALL_OFF["loss"]]
    conv_cols = conv_w.shape[-1]
    g["conv_w"] = lax.dynamic_slice(seg("conv_w", 4 * XBC).reshape(4, XBC), (0, chip * conv_cols), (4, conv_cols))
    dmod8 = every.reshape(8, SMALL_LEN)[:, SMALL_OFF["dmod"]:SMALL_OFF["dmod"] + 6 * D]
    g["w_ada"] = _w_ada_grad(c8, lax.dynamic_slice(dmod8, (0, chip * ada_cols), (8, ada_cols)))

    got = adamw("w_ada", carry=red.share(["w_in_b"]))
    red.take_share(["w_in_b"], got)
    for n in six:
        g[n] = red.final[n]
    g["w_in"] = jnp.concatenate([red.final["w_in_a"], red.final["w_in_b"]], axis=0)
    for n in ["conv_w", "w_in"] + six:
        adamw(n)
    sizes = [w[n].size for n in _SMALL_REPLICATED]
    n_small = -(-sum(sizes) // 1024) * 1024
    pack = lambda d: jnp.pad(jnp.concatenate([d[n].reshape(1, -1) for n in _SMALL_REPLICATED], axis=1),
                             ((0, 0), (0, n_small - sum(sizes)))).reshape(n_small // 128, 128)
    d_, m_, v_ = _adamw(pack(w), pack(g), pack(m), pack(v), "adamw_small")
    off = 0
    for n, s in zip(_SMALL_REPLICATED, sizes):
        for dst, src in ((delta, d_), (new_m, m_), (new_v, v_)):
            dst[n] = src.reshape(1, n_small)[:, off:off + s]
        off += s

    out = [loss, grad_x.reshape(x.shape)]
    for d in (g, delta, new_m, new_v):
        out += [d[n].reshape(w[n].shape) for n in _WEIGHTS]
    return tuple(out)
```

```python
import functools
import operator

import jax
import jax.numpy as jnp
import numpy as np
from jax import lax
from jax.experimental import pallas as pl
from jax.experimental.pallas import tpu as pltpu

F32, BF16 = jnp.float32, jnp.bfloat16
HIGH = lax.Precision.HIGHEST
MESH = pl.DeviceIdType.MESH

D = 1024
DI = 2048
HEADS, HEAD_DIM = 32, 64
GROUPS, STATE = 4, 128
Q = 128
XBC = DI + 2 * GROUPS * STATE
POOL_WINDOWS = (2, 4, 8, 16)
GW = 256
DFF = 4096
EPS = 1e-5
IN_COLS = 8224
OFF_Z, OFF_XBC, OFF_POOL, OFF_GATE, OFF_DT, NP = 0, 2048, 5120, 6144, 8192, 8448
N_CHIPS = 4
ADAM_LR, ADAM_B1, ADAM_B2, ADAM_EPS, ADAM_WD, ADAM_STEP = 0.001, 0.9, 0.999, 1e-08, 0.01, 10
VMEM_LIMIT = 56 * 2 ** 20
NEG = -1e30


def _sigmoid(v):
    return 0.5 * jnp.tanh(0.5 * v) + 0.5


def _softplus(v):
    return jnp.maximum(v, 0.0) + jnp.log1p(jnp.exp(-jnp.abs(v)))


def _dot(a, b, dims, **kw):
    return lax.dot_general(a, b, (dims, ((), ())), preferred_element_type=F32, **kw)


def _nn(a, b, **kw):
    return _dot(a, b, ((1,), (0,)), **kw)


def _nt(a, b, **kw):
    return _dot(a, b, ((1,), (1,)), **kw)


def _tn(a, b, **kw):
    return _dot(a, b, ((0,), (0,)), **kw)


_DT_IN_CHIP2 = 5120 - 2 * (IN_COLS // 4)


class _Sems:
    def __init__(self, send, recv, local, base=0):
        self._send, self._recv, self._local, self._base = send, recv, local, base

    def shift(self, n):
        return _Sems(self._send, self._recv, self._local, self._base + n)

    def send(self, i):
        return self._send.at[self._base + i]

    def recv(self, i):
        return self._recv.at[self._base + i]

    def local(self, i):
        return self._local.at[self._base + i]


class _Carry:
    def __init__(self, ins, out_shapes, n_sems, start, finish, aliased=()):
        self.ins, self.out_shapes, self.n_sems, self.start, self.finish = list(ins), list(out_shapes), n_sems, start, finish
        self.aliased = list(aliased)


def _join(*carries):
    def run(which):
        def fn(ins, outs, sems):
            i = o = s = 0
            for cy in carries:
                getattr(cy, which)(ins[i:i + len(cy.ins)], outs[o:o + len(cy.out_shapes)], sems.shift(s))
                i, o, s = i + len(cy.ins), o + len(cy.out_shapes), s + cy.n_sems
        return fn

    aliased, i, o = [], 0, 0
    for cy in carries:
        aliased += [(i + a, o + b) for a, b in cy.aliased]
        i, o = i + len(cy.ins), o + len(cy.out_shapes)
    return _Carry([a for cy in carries for a in cy.ins], [a for cy in carries for a in cy.out_shapes],
                  sum(cy.n_sems for cy in carries), run("start"), run("finish"), aliased)


def _call(body, args, *, name, grid=(), in_specs, out_specs, out_shape, scratch_shapes=(), sem=None, aliases=None,
          carry=None):
    in_specs, out_specs, out_shape, scratch_shapes = list(in_specs), list(out_specs), list(out_shape), list(scratch_shapes)
    n_in, n_out, n_scr = len(in_specs), len(out_specs), len(scratch_shapes)
    kw = {"vmem_limit_bytes": VMEM_LIMIT}
    if carry is None:
        kernel_fn = functools.partial(body)
        if sem is not None:
            kw["dimension_semantics"] = sem
    else:
        n_ci, n_co = len(carry.ins), len(carry.out_shapes)
        hbm = pl.BlockSpec(memory_space=pl.ANY)
        in_specs += [hbm] * n_ci
        out_specs += [hbm] * n_co
        out_shape += carry.out_shapes
        n_s = max(carry.n_sems, 1)
        scratch_shapes += [pltpu.SemaphoreType.DMA((n_s,))] * 3
        args = list(args) + carry.ins
        aliases = dict(aliases or {})
        aliases.update({n_in + i: n_out + o for i, o in carry.aliased})
        if grid:
            kw["dimension_semantics"] = ("arbitrary",) * len(grid)

        def kernel_fn(*refs):
            a = n_in
            ins, c_ins = refs[:a], refs[a:a + n_ci]
            a += n_ci
            outs, c_outs = refs[a:a + n_out], refs[a + n_out:a + n_out + n_co]
            a += n_out + n_co
            scr, sems = refs[a:a + n_scr], _Sems(*refs[a + n_scr:a + n_scr + 3])
            if grid:
                ids = [pl.program_id(d) for d in range(len(grid))]
                first = functools.reduce(operator.and_, [i == 0 for i in ids])
                last = functools.reduce(operator.and_, [i == g - 1 for i, g in zip(ids, grid)])

                @pl.when(first)
                def _():
                    carry.start(c_ins, c_outs, sems)

                body(*ins, *outs, *scr)

                @pl.when(last)
                def _():
                    carry.finish(c_ins, c_outs, sems)
            else:
                carry.start(c_ins, c_outs, sems)
                body(*ins, *outs, *scr)
                carry.finish(c_ins, c_outs, sems)

    outs = pl.pallas_call(
        kernel_fn, name=name, grid=grid, in_specs=in_specs, out_specs=out_specs, out_shape=out_shape,
        scratch_shapes=scratch_shapes, input_output_aliases=aliases or {},
        compiler_params=pltpu.CompilerParams(**kw),
    )(*args)
    outs = list(outs)
    return outs if carry is None else (outs[:n_out], outs[n_out:])


def _run_carry(carry, name):
    _, outs = _call(lambda: None, [], name=name, in_specs=[], out_specs=[], out_shape=[], carry=carry)
    return outs


_TILES = {
    "mm_proj": (1024, 2816, 1024), "mm_branch_ssd": (1024, 1024, 2048), "mm_branch_pool": (1024, 1024, 1024),
    "mm_out": (1024, 1024, 1024), "mm_up": (2048, 1024, 1024), "mm_down": (512, 1024, 4096),
    "mm_dact": (1024, 1024, 1024), "mm_g_down": (1024, 1024, 4096), "mm_dh2": (1024, 1024, 4096),
    "mm_g_up": (1024, 1024, 4096), "mm_dmerged": (1024, 1024, 1024), "mm_g_out": (1024, 1024, 4096),
    "mm_dyp": (1024, 1024, 1024), "mm_g_bpool": (1024, 1024, 4096), "mm_g_bssd": (1024, 1024, 4096),
    "mm_dyn": (1024, 1024, 1024), "mm_g_in_a": (768, 1408, 4096), "mm_g_in_b": (256, 2816, 2048),
    "mm_dh1": (1024, 1024, 4224),
}


def _matmul(a, b, *, mode, out_dtypes, name, epi=None, tile_extras=(), row_extras=(), carry=None, a_cols=None,
            chip_blocks=False, cols_outer=False):
    M, K = (a.shape[1], a.shape[0]) if mode == "tn" else a.shape
    N = b.shape[0] if mode == "nt" else b.shape[1]
    a_start, M = a_cols if a_cols is not None else (0, M)
    tm, tn, tk = _TILES[name]
    tm, tn, tk = min(tm, M), min(tn, N), min(tk, K)
    assert M % tm == 0 and N % tn == 0 and K % tk == 0 and a_start % tm == 0, (name, M, N, K, tm, tn, tk)
    a_off = a_start // tm
    if mode == "nn":
        a_spec = pl.BlockSpec((tm, tk), lambda i, j, k: (i, k))
        b_spec = pl.BlockSpec((tk, tn), lambda i, j, k: (k, j))
        dims = ((1,), (0,))
    elif mode == "nt":
        a_spec = pl.BlockSpec((tm, tk), lambda i, j, k: (i, k))
        b_spec = pl.BlockSpec((tn, tk), lambda i, j, k: (j, k))
        dims = ((1,), (1,))
    else:
        a_spec = pl.BlockSpec((tk, tm), lambda i, j, k: (k, i + a_off))
        b_spec = pl.BlockSpec((tk, tn), lambda i, j, k: (k, j))
        dims = ((0,), (0,))
    nk = K // tk
    n_te, n_re, n_out = len(tile_extras), len(row_extras), len(out_dtypes)
    if epi is None:
        epi = lambda acc: (acc,)

    def body(a_ref, b_ref, *rest):
        extras = rest[:n_te + n_re]
        outs = rest[n_te + n_re:n_te + n_re + n_out]
        p = _dot(a_ref[...], b_ref[...], dims)

        def finish(acc):
            vals = epi(acc, *[e[...] for e in extras])
            for o, v in zip(outs, vals):
                o[...] = v.astype(o.dtype)

        if nk == 1:
            finish(p)
        else:
            acc_ref = rest[-1]
            k = pl.program_id(2)

            @pl.when(k == 0)
            def _():
                acc_ref[...] = p

            @pl.when(k > 0)
            def _():
                acc_ref[...] += p

            @pl.when(k == nk - 1)
            def _():
                finish(acc_ref[...])

    tile_spec = pl.BlockSpec((tm, tn), lambda i, j, k: (i, j))
    row_spec = pl.BlockSpec((1, tn), lambda i, j, k: (0, j))
    out_spec, out_dims = tile_spec, (M, N)
    if chip_blocks:
        assert n_te == 0 and tn * N_CHIPS == N
        out_spec, out_dims = pl.BlockSpec((None, tm, tn), lambda i, j, k: (j, i, 0)), (N_CHIPS, M, tn)
    in_specs, grid = [a_spec, b_spec] + [tile_spec] * n_te + [row_spec] * n_re, (M // tm, N // tn, nk)
    if cols_outer:
        swap = lambda s: pl.BlockSpec(s.block_shape, lambda g0, g1, k, f=s.index_map: f(g1, g0, k))
        in_specs, out_spec, grid = [swap(s) for s in in_specs], swap(out_spec), (N // tn, M // tm, nk)
    return _call(
        body, [a, b, *tile_extras, *row_extras], name=name, grid=grid,
        in_specs=in_specs, out_specs=[out_spec] * n_out,
        out_shape=[jax.ShapeDtypeStruct(out_dims, dt) for dt in out_dtypes],
        scratch_shapes=[pltpu.VMEM((tm, tn), F32)] if nk > 1 else [],
        sem=("parallel", "parallel", "arbitrary"), carry=carry)


def _row_tile(T):
    return min(512, T)


def _norm_mod(x, nw, scale, shift, name, carry=None):
    T = x.shape[0]
    tr = _row_tile(T)

    def body(x_ref, nw_ref, sc_ref, sh_ref, o_ref):
        xv = x_ref[...]
        r = lax.rsqrt(jnp.mean(xv * xv, axis=-1, keepdims=True) + EPS)
        o_ref[...] = ((xv * r) * nw_ref[...] * (1.0 + sc_ref[...]) + sh_ref[...]).astype(BF16)

    tile = pl.BlockSpec((tr, D), lambda i: (i, 0))
    row = pl.BlockSpec((1, D), lambda i: (0, 0))
    res = _call(body, [x, nw, scale, shift], name=name, grid=(T // tr,), in_specs=[tile, row, row, row],
                out_specs=[tile], out_shape=[jax.ShapeDtypeStruct((T, D), BF16)], sem=("parallel",), carry=carry)
    return res[0] if carry is None else (res[0][0], res[1])


def _norm_mod_bwd(x, dh, dres, nw, scale, name, branch=None, gate=None, carry=None):
    T = x.shape[0]
    tr = _row_tile(T)
    with_branch = branch is not None

    def body(x_ref, dh_ref, dr_ref, nw_ref, sc_ref, *rest):
        if with_branch:
            br_ref, g_ref, dx_ref, sums_ref, db_ref = rest
        else:
            dx_ref, sums_ref = rest
        i = pl.program_id(0)

        @pl.when(i == 0)
        def _():
            sums_ref[...] = jnp.zeros_like(sums_ref)

        xv, dhv = x_ref[...], dh_ref[...]
        r = lax.rsqrt(jnp.mean(xv * xv, axis=-1, keepdims=True) + EPS)
        xn = xv * r
        g1 = dhv * (1.0 + sc_ref[...])
        dxn = g1 * nw_ref[...]
        dx = dr_ref[...] + r * (dxn - xn * jnp.mean(dxn * xn, axis=-1, keepdims=True))
        dx_ref[...] = dx
        sums_ref[0:1, :] += jnp.sum(dhv, axis=0, keepdims=True)
        sums_ref[1:2, :] += jnp.sum(dhv * (xn * nw_ref[...]), axis=0, keepdims=True)
        sums_ref[2:3, :] += jnp.sum(g1 * xn, axis=0, keepdims=True)
        if with_branch:
            db_ref[...] = (dx * g_ref[...]).astype(BF16)
            sums_ref[3:4, :] += jnp.sum(dx * br_ref[...], axis=0, keepdims=True)

    tile = pl.BlockSpec((tr, D), lambda i: (i, 0))
    row = pl.BlockSpec((1, D), lambda i: (0, 0))
    sums = pl.BlockSpec((8, D), lambda i: (0, 0))
    ins = [x, dh, dres, nw, scale] + ([branch, gate] if with_branch else [])
    in_specs = [tile, tile, tile, row, row] + ([tile, row] if with_branch else [])
    out_specs = [tile, sums] + ([tile] if with_branch else [])
    out_shape = [jax.ShapeDtypeStruct((T, D), F32), jax.ShapeDtypeStruct((8, D), F32)]
    if with_branch:
        out_shape.append(jax.ShapeDtypeStruct((T, D), BF16))
    return _call(body, ins, name=name, grid=(T // tr,), in_specs=in_specs, out_specs=out_specs, out_shape=out_shape,
                 sem=("arbitrary",), carry=carry)


def _final_loss_bwd(x3, target, wf, down, gate_f):
    T = x3.shape[0]
    tr = _row_tile(T)
    n_steps = T // tr

    def body(x_ref, t_ref, w_ref, dn_ref, g_ref, dx_ref, dd_ref, sums_ref):
        i = pl.program_id(0)

        @pl.when(i == 0)
        def _():
            sums_ref[...] = jnp.zeros_like(sums_ref)

        xv = x_ref[...]
        r = lax.rsqrt(jnp.mean(xv * xv, axis=-1, keepdims=True) + EPS)
        xn = xv * r
        err = xn * w_ref[...] - t_ref[...]
        dy = err * (1.0 / D)
        dxn = dy * w_ref[...]
        dx = r * (dxn - xn * jnp.mean(dxn * xn, axis=-1, keepdims=True))
        dx_ref[...] = dx
        dd_ref[...] = (dx * g_ref[...]).astype(BF16)
        sums_ref[0:1, :] += jnp.sum(dy * xn, axis=0, keepdims=True)
        sums_ref[1:2, :] += jnp.sum(dx * dn_ref[...], axis=0, keepdims=True)
        sums_ref[2:3, :] += jnp.sum(err * err, axis=0, keepdims=True) * (0.5 / D)

        @pl.when(i == n_steps - 1)
        def _():
            sums_ref[3:4, :] = jnp.broadcast_to(jnp.sum(sums_ref[2:3, :], axis=1, keepdims=True), (1, D))

    tile = pl.BlockSpec((tr, D), lambda i: (i, 0))
    row = pl.BlockSpec((1, D), lambda i: (0, 0))
    sums = pl.BlockSpec((8, D), lambda i: (0, 0))
    return _call(body, [x3, target, wf, down, gate_f], name="final_loss_bwd", grid=(n_steps,),
                 in_specs=[tile, tile, row, tile, row], out_specs=[tile, tile, sums],
                 out_shape=[jax.ShapeDtypeStruct((T, D), F32), jax.ShapeDtypeStruct((T, D), BF16),
                            jax.ShapeDtypeStruct((8, D), F32)], sem=("arbitrary",))


CONV_TC = 1024


def _conv_taps(xp, w, b):
    acc = b + w[3:4, :] * xp
    for k in range(3):
        acc = acc + w[k:k + 1, :] * pltpu.roll(xp, 3 - k, 0)
    return acc


def _conv_fwd(proj, conv_w, conv_b):
    T = proj.shape[0]
    tr = _row_tile(T)
    nb, offb = tr // 8, OFF_XBC // CONV_TC

    def body(x_ref, h_ref, w_ref, b_ref, o_ref):
        halo = jnp.where(pl.program_id(0) > 0, h_ref[...], 0.0)
        xp = jnp.concatenate([halo, x_ref[...]], axis=0)
        acc = _conv_taps(xp, w_ref[...], b_ref[...])[8:]
        o_ref[...] = acc * _sigmoid(acc)

    return _call(
        body, [proj, proj, conv_w, conv_b], name="conv_fwd", grid=(T // tr, XBC // CONV_TC),
        in_specs=[pl.BlockSpec((tr, CONV_TC), lambda i, j: (i, j + offb)),
                  pl.BlockSpec((8, CONV_TC), lambda i, j: (jnp.maximum(i * nb - 1, 0), j + offb)),
                  pl.BlockSpec((4, CONV_TC), lambda i, j: (0, j)),
                  pl.BlockSpec((1, CONV_TC), lambda i, j: (0, j))],
        out_specs=[pl.BlockSpec((tr, CONV_TC), lambda i, j: (i, j))],
        out_shape=[jax.ShapeDtypeStruct((T, XBC), F32)], sem=("parallel", "parallel"))[0]


def _conv_bwd_a(dxa, proj, conv_w, conv_b):
    T = proj.shape[0]
    tr = _row_tile(T)
    nb, offb = tr // 8, OFF_XBC // CONV_TC

    def body(d_ref, x_ref, h_ref, w_ref, b_ref, o_ref, sums_ref):
        i = pl.program_id(1)

        @pl.when(i == 0)
        def _():
            sums_ref[...] = jnp.zeros_like(sums_ref)

        halo = jnp.where(i > 0, h_ref[...], 0.0)
        xp = jnp.concatenate([halo, x_ref[...]], axis=0)
        w = w_ref[...]
        taps = [pltpu.roll(xp, 3 - k, 0)[8:] for k in range(3)] + [x_ref[...]]
        acc = b_ref[...] + w[3:4, :] * taps[3]
        for k in range(3):
            acc = acc + w[k:k + 1, :] * taps[k]
        s = _sigmoid(acc)
        dxc = d_ref[...] * (s * (1.0 + acc * (1.0 - s)))
        o_ref[...] = dxc
        for k in range(4):
            sums_ref[k:k + 1, :] += jnp.sum(dxc * taps[k], axis=0, keepdims=True)
        sums_ref[4:5, :] += jnp.sum(dxc, axis=0, keepdims=True)

    return _call(
        body, [dxa, proj, proj, conv_w, conv_b], name="conv_bwd_a", grid=(XBC // CONV_TC, T // tr),
        in_specs=[pl.BlockSpec((tr, CONV_TC), lambda j, i: (i, j)),
                  pl.BlockSpec((tr, CONV_TC), lambda j, i: (i, j + offb)),
                  pl.BlockSpec((8, CONV_TC), lambda j, i: (jnp.maximum(i * nb - 1, 0), j + offb)),
                  pl.BlockSpec((4, CONV_TC), lambda j, i: (0, j)),
                  pl.BlockSpec((1, CONV_TC), lambda j, i: (0, j))],
        out_specs=[pl.BlockSpec((tr, CONV_TC), lambda j, i: (i, j)), pl.BlockSpec((8, CONV_TC), lambda j, i: (0, j))],
        out_shape=[jax.ShapeDtypeStruct((T, XBC), F32), jax.ShapeDtypeStruct((8, XBC), F32)],
        sem=("parallel", "arbitrary"))


def _conv_bwd_b(dxc, conv_w, dproj):
    T = dxc.shape[0]
    tr = _row_tile(T)
    nb, offb, last = tr // 8, OFF_XBC // CONV_TC, T // tr - 1

    def body(d_ref, h_ref, w_ref, dp_in, o_ref):
        del dp_in
        halo = jnp.where(pl.program_id(0) < last, h_ref[...], 0.0)
        xp = jnp.concatenate([d_ref[...], halo], axis=0)
        n = xp.shape[0]
        w = w_ref[...]
        acc = w[3:4, :] * xp
        for k in range(3):
            acc = acc + w[k:k + 1, :] * pltpu.roll(xp, n - (3 - k), 0)
        o_ref[...] = acc[:tr].astype(BF16)

    return _call(
        body, [dxc, dxc, conv_w, dproj], name="conv_bwd_b", grid=(T // tr, XBC // CONV_TC),
        in_specs=[pl.BlockSpec((tr, CONV_TC), lambda i, j: (i, j)),
                  pl.BlockSpec((8, CONV_TC), lambda i, j: (jnp.minimum((i + 1) * nb, T // 8 - 1), j)),
                  pl.BlockSpec((4, CONV_TC), lambda i, j: (0, j)),
                  pl.BlockSpec(memory_space=pl.ANY)],
        out_specs=[pl.BlockSpec((tr, CONV_TC), lambda i, j: (i, j + offb))],
        out_shape=[jax.ShapeDtypeStruct(dproj.shape, BF16)], aliases={3: 0}, sem=("parallel", "parallel"))[0]


def _spread(v, sel, pieces):
    out = None
    for _ in range(pieces):
        p = v.astype(BF16)
        term = _nn(p, sel)
        out = term if out is None else out + term
        v = v - p.astype(F32)
    return out


def _ssd_selectors():
    g = np.arange(GROUPS)[:, None, None]
    piece = np.arange(128)[None, :, None]
    h = np.where(piece < 3 * HEADS, piece % HEADS, -1)
    blocks = (h == 8 * g + np.arange(1024)[None, None, :] // 128)
    pairs = (h == 8 * g + np.arange(512)[None, None, :] // HEAD_DIM)
    lane = np.arange(128)[None, None, :]
    block_sum = (lane == 8 * g + np.arange(1024)[None, :, None] // 128)
    pair_sum = (lane == 8 * g + np.arange(512)[None, :, None] // HEAD_DIM)
    return [jnp.asarray(m, BF16) for m in (blocks, pairs, block_sum, pair_sum)]


def _pack3(v):
    p0 = v.astype(BF16)
    r1 = v - p0.astype(F32)
    p1 = r1.astype(BF16)
    r2 = r1 - p1.astype(F32)
    return p0 + pltpu.roll(r1, HEADS, 1).astype(BF16) + pltpu.roll(r2, 2 * HEADS, 1).astype(BF16)


def _ssd_group(g, cs_p, csT, dt_p, s_mat, causal_w, lo, blocks_ref, pairs_ref):
    csb = _nn(cs_p, blocks_ref[g])
    row = jnp.concatenate([csT[8 * g + hh:8 * g + hh + 1, :] for hh in range(8)], axis=1)
    l_w = jnp.exp(jnp.where(causal_w, csb - row, NEG))
    m_w = jnp.concatenate([s_mat] * 8, axis=1) * l_w
    cs_g = jnp.concatenate([jnp.where(lo, csb[:, 256 * jj:256 * jj + 128], csb[:, 256 * jj + 128:256 * jj + 256])
                            for jj in range(4)], axis=1)
    cs_last = cs_g[Q - 1:Q, :]
    return m_w, l_w, _nn(dt_p, pairs_ref[g]), jnp.exp(cs_g), jnp.exp(cs_last - cs_g), jnp.exp(cs_last)


def _ssd_common(dtp_ref, dtb_r, alog_r, dtb_c, alog_c):
    rows = lax.broadcasted_iota(jnp.int32, (Q, Q), 0)
    cols = lax.broadcasted_iota(jnp.int32, (Q, Q), 1)
    tri = (cols <= rows).astype(F32)
    heads = lax.broadcasted_iota(jnp.int32, (1, 128), 1) < HEADS
    raw_w = dtp_ref[...] + dtb_r[...]
    dt_w = jnp.where(heads, _softplus(raw_w), 0.0)
    a_w = -jnp.exp(alog_r[...])
    cs_w = _nn(tri, dt_w * a_w, precision=HIGH)
    aT = _softplus(dtp_ref[...].T[0:HEADS, :] + dtb_c[...]) * (-jnp.exp(alog_c[...]))
    csT = _nt(aT, tri, precision=HIGH)
    return raw_w[:, 0:HEADS], dt_w[:, 0:HEADS], a_w[:, 0:HEADS], csT, _pack3(cs_w), _pack3(dt_w)


def _ssd_fwd(xbc_a, proj, dtb_r, alog_r, dtb_c, alog_c, dsk_exp):
    T = xbc_a.shape[0]
    nc = T // Q
    dtb_r, alog_r = [jnp.pad(a, ((0, 0), (0, 128 - HEADS))) for a in (dtb_r, alog_r)]

    def body(xbc_ref, dtp_ref, dtb_r_ref, alog_r_ref, dtb_c_ref, alog_c_ref, dsk_ref, blocks_ref, pairs_ref,
             y_ref, hin_ref, h_scr):
        @pl.when(pl.program_id(0) == 0)
        def _():
            h_scr[...] = jnp.zeros_like(h_scr)

        _, _, _, csT, cs_p, dt_p = _ssd_common(dtp_ref, dtb_r_ref, alog_r_ref, dtb_c_ref, alog_c_ref)
        lo = lax.broadcasted_iota(jnp.int32, (1, 128), 1) < HEAD_DIM
        hi = jnp.logical_not(lo)
        causal_w = (lax.broadcasted_iota(jnp.int32, (Q, 1024), 1) & (Q - 1)) <= lax.broadcasted_iota(jnp.int32, (Q, 1024), 0)
        for g in range(GROUPS):
            gs = slice(512 * g, 512 * (g + 1))
            hs = slice(128 * g, 128 * (g + 1))
            xs_g = xbc_ref[:, gs]
            b_g = xbc_ref[:, DI + STATE * g:DI + STATE * (g + 1)].astype(BF16)
            c_g = xbc_ref[:, DI + 512 + STATE * g:DI + 512 + STATE * (g + 1)].astype(BF16)
            m_w, _, dt_g, ecs_g, dec_g, cd_g = _ssd_group(g, cs_p, csT, dt_p, _nt(c_g, b_g), causal_w, lo, blocks_ref, pairs_ref)
            m_b = m_w.astype(BF16)
            xdt = xs_g * dt_g
            xdt_b = xdt.astype(BF16)
            ys = []
            for jj in range(4):
                xp = xdt_b[:, 128 * jj:128 * (jj + 1)]
                x_ab = jnp.concatenate([jnp.where(lo, xp, jnp.zeros_like(xp)), jnp.where(hi, xp, jnp.zeros_like(xp))], axis=0)
                ys.append(_nn(m_b[:, 256 * jj:256 * (jj + 1)], x_ab))
            h_g = h_scr[hs, :]
            hin_ref[0, hs, :] = h_g
            y_ref[:, gs] = jnp.concatenate(ys, axis=1) + _nn(c_g, h_g.astype(BF16)) * ecs_g + dsk_ref[:, gs] * xs_g
            h_scr[hs, :] = h_g * cd_g + _tn(b_g, (xdt * dec_g).astype(BF16))

    small_r = pl.BlockSpec((1, 128), lambda c: (0, 0))
    small_c = pl.BlockSpec((HEADS, 1), lambda c: (0, 0))
    blocks, pairs, _, _ = _ssd_selectors()
    whole = lambda a: pl.BlockSpec(a.shape, lambda c: (0,) * a.ndim)
    return _call(
        body, [xbc_a, proj, dtb_r, alog_r, dtb_c, alog_c, dsk_exp, blocks, pairs], name="ssd_fwd", grid=(nc,),
        in_specs=[pl.BlockSpec((Q, XBC), lambda c: (c, 0)),
                  pl.BlockSpec((Q, 128), lambda c: (c, OFF_DT // 128)),
                  small_r, small_r, small_c, small_c,
                  pl.BlockSpec((1, DI), lambda c: (0, 0)), whole(blocks), whole(pairs)],
        out_specs=[pl.BlockSpec((Q, DI), lambda c: (c, 0)), pl.BlockSpec((1, 512, 512), lambda c: (c, 0, 0))],
        out_shape=[jax.ShapeDtypeStruct((T, DI), F32), jax.ShapeDtypeStruct((nc, 512, 512), F32)],
        scratch_shapes=[pltpu.VMEM((512, 512), F32)], sem=("arbitrary",))


def _ssd_bwd(dy, xbc_a, proj, hin, dtb_r, alog_r, dtb_c, alog_c, dsk_exp, dproj, carry=None):
    T = xbc_a.shape[0]
    nc = T // Q
    dtb_r, alog_r = [jnp.pad(a, ((0, 0), (0, 128 - HEADS))) for a in (dtb_r, alog_r)]

    def body(dy_ref, xbc_ref, dtp_ref, hin_ref, dtb_r_ref, alog_r_ref, dtb_c_ref, alog_c_ref, dsk_ref, dp_in,
             blocks_ref, pairs_ref, block_sum_ref, pair_sum_ref, dxa_ref, dp_ref, dsk_sum_ref, small_ref, dh_scr):
        del dp_in

        @pl.when(pl.program_id(0) == 0)
        def _():
            dh_scr[...] = jnp.zeros_like(dh_scr)
            dsk_sum_ref[...] = jnp.zeros_like(dsk_sum_ref)
            small_ref[...] = jnp.zeros_like(small_ref)

        raw, dt, a_r, csT, cs_p, dt_p = _ssd_common(dtp_ref, dtb_r_ref, alog_r_ref, dtb_c_ref, alog_c_ref)
        lo = lax.broadcasted_iota(jnp.int32, (1, 128), 1) < HEAD_DIM
        hi = jnp.logical_not(lo)
        sub32 = lax.broadcasted_iota(jnp.int32, (HEADS, 1), 0)
        causal_w = (lax.broadcasted_iota(jnp.int32, (Q, 1024), 1) & (Q - 1)) <= lax.broadcasted_iota(jnp.int32, (Q, 1024), 0)
        dcs_c = jnp.zeros((Q, 128), F32)
        dcs_r = jnp.zeros((HEADS, Q), F32)
        dcs_l = jnp.zeros((8, 128), F32)
        ddt_x = jnp.zeros((Q, 128), F32)
        for g in range(GROUPS):
            gs = slice(512 * g, 512 * (g + 1))
            hs = slice(128 * g, 128 * (g + 1))
            xs_g, dy_g = xbc_ref[:, gs], dy_ref[:, gs]
            b_g = xbc_ref[:, DI + STATE * g:DI + STATE * (g + 1)].astype(BF16)
            c_g = xbc_ref[:, DI + 512 + STATE * g:DI + 512 + STATE * (g + 1)].astype(BF16)
            m_w, l_w, dt_g, ecs_g, dec_g, cd_g = _ssd_group(g, cs_p, csT, dt_p, _nt(c_g, b_g), causal_w, lo, blocks_ref, pairs_ref)
            m_b = m_w.astype(BF16)
            xdt = xs_g * dt_g
            xdt_b, dy_b = xdt.astype(BF16), dy_g.astype(BF16)
            dms, dxs = [], []
            for jj in range(4):
                xp, dyp = xdt_b[:, 128 * jj:128 * (jj + 1)], dy_b[:, 128 * jj:128 * (jj + 1)]
                dy_ab = jnp.concatenate([jnp.where(lo, dyp, jnp.zeros_like(dyp)), jnp.where(hi, dyp, jnp.zeros_like(dyp))], axis=0)
                dm_ab = _nt(dy_ab, xp)
                dms += [dm_ab[:Q], dm_ab[Q:]]
                dx_ab = _tn(m_b[:, 256 * jj:256 * (jj + 1)], dyp)
                dxs.append(jnp.where(lo, dx_ab[:Q], dx_ab[Q:]))
            dm_w = jnp.concatenate(dms, axis=1)
            w_w = dm_w * m_w
            dcs_c = dcs_c + _spread(w_w, block_sum_ref[g], 2)
            w_cols = jnp.sum(w_w, axis=0, keepdims=True)
            for hh in range(8):
                dcs_r = dcs_r + jnp.where(sub32 == 8 * g + hh, w_cols[:, 128 * hh:128 * (hh + 1)], 0.0)
            dl_w = dm_w * l_w
            ds_mat = dl_w[:, 0:128]
            for hh in range(1, 8):
                ds_mat = ds_mat + dl_w[:, 128 * hh:128 * (hh + 1)]
            hin_g = hin_ref[0, hs, :]
            hin_b = hin_g.astype(BF16)
            dh_g = dh_scr[hs, :]
            dh_b = dh_g.astype(BF16)
            g_mat = _nn(b_g, dh_b)
            xdec = xdt * dec_g
            xg = xdec * g_mat
            dxdt = jnp.concatenate(dxs, axis=1) + dec_g * g_mat
            sums = _spread(jnp.concatenate([dy_g * (_nn(c_g, hin_b) * ecs_g) - xg, dxdt * xs_g], axis=0), pair_sum_ref[g], 2)
            dcs_c = dcs_c + sums[:Q]
            ddt_x = ddt_x + sums[Q:]
            last = jnp.sum(xg, axis=0, keepdims=True) + jnp.sum(dh_g * hin_g, axis=0, keepdims=True) * cd_g
            dcs_l = dcs_l + _spread(jnp.broadcast_to(last, (8, 512)), pair_sum_ref[g], 2)
            dz = (dy_g * ecs_g).astype(BF16)
            ds_b = ds_mat.astype(BF16)
            dxa_ref[:, gs] = dxdt * dt_g + dy_g * dsk_ref[:, gs]
            dxa_ref[:, DI + STATE * g:DI + STATE * (g + 1)] = _nt(xdec.astype(BF16), dh_b) + _tn(ds_b, c_g)
            dxa_ref[:, DI + 512 + STATE * g:DI + 512 + STATE * (g + 1)] = _nt(dz, hin_b) + _nn(ds_b, b_g)
            dh_scr[hs, :] = _tn(c_g, dz) + dh_g * cd_g
            dsk_sum_ref[0:1, gs] += jnp.sum(dy_g * xs_g, axis=0, keepdims=True)

        rows = lax.broadcasted_iota(jnp.int32, (Q, Q), 0)
        cols = lax.broadcasted_iota(jnp.int32, (Q, Q), 1)
        tri_t = (cols >= rows).astype(F32)
        last_row = lax.broadcasted_iota(jnp.int32, (Q, 1), 0) == Q - 1
        dcs = (dcs_c + jnp.where(last_row, dcs_l[0:1, :], 0.0))[:, 0:HEADS]
        da = _nn(tri_t, dcs, precision=HIGH) - _nt(tri_t, dcs_r, precision=HIGH)
        ddt_raw = (ddt_x[:, 0:HEADS] + da * a_r) * _sigmoid(raw)
        small_ref[0:1, :] += jnp.sum(da * dt, axis=0, keepdims=True) * a_r
        small_ref[1:2, :] += jnp.sum(ddt_raw, axis=0, keepdims=True)
        dp_ref[...] = jnp.zeros_like(dp_ref)
        dp_ref[:, 0:HEADS] = ddt_raw.astype(BF16)

    rev = lambda c: nc - 1 - c
    small_r = pl.BlockSpec((1, 128), lambda c: (0, 0))
    small_c = pl.BlockSpec((HEADS, 1), lambda c: (0, 0))
    selectors = _ssd_selectors()
    whole = lambda a: pl.BlockSpec(a.shape, lambda c: (0,) * a.ndim)
    return _call(
        body, [dy, xbc_a, proj, hin, dtb_r, alog_r, dtb_c, alog_c, dsk_exp, dproj, *selectors], name="ssd_bwd", grid=(nc,),
        in_specs=[pl.BlockSpec((Q, DI), lambda c: (rev(c), 0)),
                  pl.BlockSpec((Q, XBC), lambda c: (rev(c), 0)),
                  pl.BlockSpec((Q, 128), lambda c: (rev(c), OFF_DT // 128)),
                  pl.BlockSpec((1, 512, 512), lambda c: (rev(c), 0, 0)),
                  small_r, small_r, small_c, small_c,
                  pl.BlockSpec((1, DI), lambda c: (0, 0)),
                  pl.BlockSpec(memory_space=pl.ANY)] + [whole(a) for a in selectors],
        out_specs=[pl.BlockSpec((Q, XBC), lambda c: (rev(c), 0)),
                   pl.BlockSpec((Q, 256), lambda c: (rev(c), OFF_DT // 256)),
                   pl.BlockSpec((8, DI), lambda c: (0, 0)),
                   pl.BlockSpec((8, HEADS), lambda c: (0, 0))],
        out_shape=[jax.ShapeDtypeStruct((T, XBC), F32), jax.ShapeDtypeStruct(dproj.shape, BF16),
                   jax.ShapeDtypeStruct((8, DI), F32), jax.ShapeDtypeStruct((8, HEADS), F32)],
        aliases={9: 1}, scratch_shapes=[pltpu.VMEM((512, 512), F32)], sem=("arbitrary",), carry=carry)


def _gate_norm(y, proj, w):
    T = y.shape[0]
    tr = _row_tile(T)

    def body(y_ref, z_ref, w_ref, o_ref):
        for g in range(GROUPS):
            gs = slice(512 * g, 512 * (g + 1))
            z = z_ref[:, gs]
            yg = y_ref[:, gs] * (z * _sigmoid(z))
            r = lax.rsqrt(jnp.mean(yg * yg, axis=-1, keepdims=True) + EPS)
            o_ref[:, gs] = (yg * r * w_ref[:, gs]).astype(BF16)

    tile = pl.BlockSpec((tr, DI), lambda i: (i, 0))
    return _call(body, [y, proj, w], name="gate_norm", grid=(T // tr,),
                 in_specs=[tile, tile, pl.BlockSpec((1, DI), lambda i: (0, 0))], out_specs=[tile],
                 out_shape=[jax.ShapeDtypeStruct((T, DI), BF16)], sem=("parallel",))[0]


def _gate_norm_bwd(dyn, y, proj, w, dproj):
    T = y.shape[0]
    tr = _row_tile(T)

    def body(d_ref, y_ref, z_ref, w_ref, dp_in, dy_ref, dz_ref, sums_ref):
        del dp_in

        @pl.when(pl.program_id(0) == 0)
        def _():
            sums_ref[...] = jnp.zeros_like(sums_ref)

        for g in range(GROUPS):
            gs = slice(512 * g, 512 * (g + 1))
            z, yv, d = z_ref[:, gs], y_ref[:, gs], d_ref[:, gs]
            s = _sigmoid(z)
            silu = z * s
            yg = yv * silu
            r = lax.rsqrt(jnp.mean(yg * yg, axis=-1, keepdims=True) + EPS)
            yn = yg * r
            sums_ref[0:1, gs] += jnp.sum(d * yn, axis=0, keepdims=True)
            dn = d * w_ref[:, gs]
            dyg = r * (dn - yn * jnp.mean(dn * yn, axis=-1, keepdims=True))
            dy_ref[:, gs] = dyg * silu
            dz_ref[:, gs] = (dyg * yv * (s * (1.0 + z * (1.0 - s)))).astype(BF16)

    tile = pl.BlockSpec((tr, DI), lambda i: (i, 0))
    return _call(
        body, [dyn, y, proj, w, dproj], name="gate_norm_bwd", grid=(T // tr,),
        in_specs=[tile, tile, tile, pl.BlockSpec((1, DI), lambda i: (0, 0)), pl.BlockSpec(memory_space=pl.ANY)],
        out_specs=[tile, tile, pl.BlockSpec((8, DI), lambda i: (0, 0))],
        out_shape=[jax.ShapeDtypeStruct((T, DI), F32), jax.ShapeDtypeStruct(dproj.shape, BF16),
                   jax.ShapeDtypeStruct((8, DI), F32)],
        aliases={4: 1}, sem=("arbitrary",))


def _pool_fwd(proj, pool_w_b, pool_scale):
    T = proj.shape[0]
    tr = _row_tile(T)
    nb = tr // 16

    def body(u_ref, h_ref, pw_ref, ps_ref, pooled_ref, pw_out_ref, yps_ref):
        i = pl.program_id(0)
        t = i * tr + lax.broadcasted_iota(jnp.int32, (tr, 1), 0)
        for g, win in enumerate(POOL_WINDOWS):
            gs = slice(GW * g, GW * (g + 1))
            u = u_ref[:, gs]
            s = jnp.concatenate([jnp.where(i > 0, h_ref[:, gs], 0.0), u], axis=0)
            sh = 1
            while sh < win:
                s = s + pltpu.roll(s, sh, 0)
                sh *= 2
            pooled = (s[16:] * (1.0 / jnp.minimum(t + 1, win).astype(F32)) - u).astype(BF16)
            pooled_ref[:, gs] = pooled
            pwv = _nn(pooled, pw_ref[g])
            pw_out_ref[:, gs] = pwv
            yps_ref[:, gs] = (pwv * ps_ref[:, gs]).astype(BF16)

    tile = pl.BlockSpec((tr, D), lambda i: (i, 0))
    return _call(
        body, [proj, proj, pool_w_b, pool_scale], name="pool_fwd", grid=(T // tr,),
        in_specs=[pl.BlockSpec((tr, D), lambda i: (i, OFF_POOL // D)),
                  pl.BlockSpec((16, D), lambda i: (jnp.maximum(i * nb - 1, 0), OFF_POOL // D)),
                  pl.BlockSpec((4, GW, GW), lambda i: (0, 0, 0)),
                  pl.BlockSpec((1, D), lambda i: (0, 0))],
        out_specs=[tile, tile, tile],
        out_shape=[jax.ShapeDtypeStruct((T, D), BF16), jax.ShapeDtypeStruct((T, D), F32),
                   jax.ShapeDtypeStruct((T, D), BF16)], sem=("parallel",))


def _pool_bwd(dyp, pw_out, pooled, pool_w_b, pool_scale, dproj):
    T = dyp.shape[0]
    tr = _row_tile(T)
    nb, last = tr // 16, T // tr - 1

    def body(d_ref, h_ref, pwo_ref, pooled_ref, pw_ref, ps_ref, dp_in, du_ref, gpw_ref, sums_ref):
        del dp_in
        i = pl.program_id(0)

        @pl.when(i == 0)
        def _():
            gpw_ref[...] = jnp.zeros_like(gpw_ref)
            sums_ref[...] = jnp.zeros_like(sums_ref)

        n = tr + 16
        t = i * tr + lax.broadcasted_iota(jnp.int32, (n, 1), 0)
        sums_ref[0:1, :] += jnp.sum(d_ref[...] * pwo_ref[...], axis=0, keepdims=True)
        for g, win in enumerate(POOL_WINDOWS):
            gs = slice(GW * g, GW * (g + 1))
            d_ext = jnp.concatenate([d_ref[:, gs], jnp.where(i < last, h_ref[:, gs], 0.0)], axis=0)
            dpw = (d_ext * ps_ref[:, gs]).astype(BF16)
            dpooled = _nt(dpw, pw_ref[g])
            s = jnp.where(t < T, dpooled * (1.0 / jnp.minimum(t + 1, win).astype(F32)), 0.0)
            sh = 1
            while sh < win:
                s = s + pltpu.roll(s, n - sh, 0)
                sh *= 2
            du_ref[:, gs] = (s[:tr] - dpooled[:tr]).astype(BF16)
            gpw_ref[g] += _tn(pooled_ref[:, gs], dpw[:tr])

    tile = pl.BlockSpec((tr, D), lambda i: (i, 0))
    return _call(
        body, [dyp, dyp, pw_out, pooled, pool_w_b, pool_scale, dproj], name="pool_bwd", grid=(T // tr,),
        in_specs=[tile, pl.BlockSpec((16, D), lambda i: (jnp.minimum((i + 1) * nb, T // 16 - 1), 0)), tile, tile,
                  pl.BlockSpec((4, GW, GW), lambda i: (0, 0, 0)), pl.BlockSpec((1, D), lambda i: (0, 0)),
                  pl.BlockSpec(memory_space=pl.ANY)],
        out_specs=[pl.BlockSpec((tr, D), lambda i: (i, OFF_POOL // D)),
                   pl.BlockSpec((4, GW, GW), lambda i: (0, 0, 0)), pl.BlockSpec((8, D), lambda i: (0, 0))],
        out_shape=[jax.ShapeDtypeStruct(dproj.shape, BF16), jax.ShapeDtypeStruct((4, GW, GW), F32),
                   jax.ShapeDtypeStruct((8, D), F32)],
        aliases={6: 0}, sem=("arbitrary",))


def _merge(proj, y_ssd, y_pool):
    T = proj.shape[0]
    tr = _row_tile(T)

    def body(g_ref, a_ref, b_ref, o_ref):
        o_ref[...] = (_sigmoid(g_ref[:, 0:D]) * a_ref[...] + _sigmoid(g_ref[:, D:2 * D]) * b_ref[...]).astype(BF16)

    tile = pl.BlockSpec((tr, D), lambda i: (i, 0))
    return _call(body, [proj, y_ssd, y_pool], name="merge", grid=(T // tr,),
                 in_specs=[pl.BlockSpec((tr, 2 * D), lambda i: (i, OFF_GATE // (2 * D))), tile, tile], out_specs=[tile],
                 out_shape=[jax.ShapeDtypeStruct((T, D), BF16)], sem=("parallel",))[0]


def _merge_bwd(dmerged, proj, y_ssd, y_pool):
    T = proj.shape[0]
    tr = _row_tile(T)

    def body(d_ref, g_ref, a_ref, b_ref, da_ref, db_ref, dg_ref):
        d = d_ref[...]
        ga, gb = _sigmoid(g_ref[:, 0:D]), _sigmoid(g_ref[:, D:2 * D])
        da_ref[...] = (d * ga).astype(BF16)
        db_ref[...] = (d * gb).astype(BF16)
        dg_ref[:, 0:D] = (d * a_ref[...] * ga * (1.0 - ga)).astype(BF16)
        dg_ref[:, D:2 * D] = (d * b_ref[...] * gb * (1.0 - gb)).astype(BF16)

    tile = pl.BlockSpec((tr, D), lambda i: (i, 0))
    gates = pl.BlockSpec((tr, 2 * D), lambda i: (i, OFF_GATE // (2 * D)))
    return _call(body, [dmerged, proj, y_ssd, y_pool], name="merge_bwd", grid=(T // tr,),
                 in_specs=[tile, gates, tile, tile], out_specs=[tile, tile, gates],
                 out_shape=[jax.ShapeDtypeStruct((T, D), BF16), jax.ShapeDtypeStruct((T, D), BF16),
                            jax.ShapeDtypeStruct((T, NP), BF16)], sem=("parallel",))


def _adamw(w, g, m, v, name, carry=None):
    R, C = w.shape
    tr = R if R <= 128 else 128
    assert R % tr == 0

    def body(w_ref, g_ref, m_ref, v_ref, d_ref, mo_ref, vo_ref):
        gv = g_ref[...]
        mn = ADAM_B1 * m_ref[...] + (1.0 - ADAM_B1) * gv
        vn = ADAM_B2 * v_ref[...] + (1.0 - ADAM_B2) * (gv * gv)
        m_hat = mn * (1.0 / (1.0 - ADAM_B1 ** ADAM_STEP))
        v_hat = vn * (1.0 / (1.0 - ADAM_B2 ** ADAM_STEP))
        d_ref[...] = -ADAM_LR * (m_hat / (jnp.sqrt(v_hat) + ADAM_EPS) + ADAM_WD * w_ref[...])
        mo_ref[...] = mn
        vo_ref[...] = vn

    tile = pl.BlockSpec((tr, C), lambda i: (i, 0))
    sds = jax.ShapeDtypeStruct((R, C), F32)
    return _call(body, [w, g, m, v], name=name, grid=(R // tr,), in_specs=[tile] * 4, out_specs=[tile] * 3,
                 out_shape=[sds] * 3, sem=("parallel",), carry=carry)


def _me():
    return lax.axis_index("x"), lax.axis_index("y"), lax.axis_index("c")


def _xor_peer(x, y, c, p):
    return (x ^ ((p >> 2) & 1), y ^ ((p >> 1) & 1), c ^ (p & 1))


def _ada_fwd(c_row, w_ada, b_ada_mine, carry=None):
    n_cols = w_ada.shape[1]

    def body(c_ref, w_ref, b_ref, mod_ref, c8_ref, csend, mpart, modbuf, send_sems, recv_sems):
        x, y, c = _me()
        me = 4 * x + 2 * y + c
        chip = 2 * x + y
        csend[...] = jnp.broadcast_to(c_ref[...], csend.shape)
        c8_ref[me] = csend[...]

        def c_copy(p):
            return pltpu.make_async_remote_copy(
                src_ref=csend, dst_ref=c8_ref.at[me], send_sem=send_sems.at[p - 1], recv_sem=recv_sems.at[p - 1],
                device_id=_xor_peer(x, y, c, p), device_id_type=MESH)

        for p in range(1, 8):
            c_copy(p).start()
        for p in range(1, 8):
            c_copy(p).wait_recv()
        cs = jnp.concatenate([c8_ref[d][0:1, :] for d in range(8)], axis=0)
        mpart[...] = _nn(cs * _sigmoid(cs), w_ref[...], precision=HIGH) + b_ref[...]
        modbuf[chip] = mpart[...]

        def m_copy(m):
            return pltpu.make_async_remote_copy(
                src_ref=mpart, dst_ref=modbuf.at[chip], send_sem=send_sems.at[6 + m], recv_sem=recv_sems.at[6 + m],
                device_id=_xor_peer(x, y, c, 2 * m), device_id_type=MESH)

        for m in range(1, 4):
            m_copy(m).start()
        for m in range(1, 4):
            m_copy(m).wait_recv()
        mine = lax.broadcasted_iota(jnp.int32, (8, 1), 0) == me
        for k in range(N_CHIPS):
            mod_ref[:, n_cols * k:n_cols * (k + 1)] = jnp.sum(jnp.where(mine, modbuf[k], 0.0), axis=0, keepdims=True)
        for p in range(1, 8):
            c_copy(p).wait_send()
        for m in range(1, 4):
            m_copy(m).wait_send()

    vmem = pl.BlockSpec(memory_space=pltpu.VMEM)
    return _call(
        body, [c_row, w_ada, b_ada_mine], name="ada_fwd", in_specs=[vmem, vmem, vmem], out_specs=[vmem, vmem],
        out_shape=[jax.ShapeDtypeStruct((1, N_CHIPS * n_cols), F32), jax.ShapeDtypeStruct((8, 8, D), F32)],
        scratch_shapes=[pltpu.VMEM((8, D), F32), pltpu.VMEM((8, n_cols), F32), pltpu.VMEM((N_CHIPS, 8, n_cols), F32),
                        pltpu.SemaphoreType.DMA((10,)), pltpu.SemaphoreType.DMA((10,))], carry=carry)


def _gather_small(vec, carry=None):
    rows = vec.shape[0]

    def body(v_ref, all_ref, tot_ref, dsk_ref, send_sems, recv_sems):
        x, y, c = _me()
        me = 4 * x + 2 * y + c
        all_ref[me] = v_ref[...]

        def copy(p):
            return pltpu.make_async_remote_copy(
                src_ref=v_ref, dst_ref=all_ref.at[me], send_sem=send_sems.at[p - 1], recv_sem=recv_sems.at[p - 1],
                device_id=_xor_peer(x, y, c, p), device_id_type=MESH)

        for p in range(1, 8):
            copy(p).start()
        for p in range(1, 8):
            copy(p).wait_recv()
        tot = all_ref[0]
        for d in range(1, 8):
            tot = tot + all_ref[d]
        tot_ref[...] = tot
        seg = tot[SMALL_OFF["d_skip"] // 128:SMALL_OFF["d_skip"] // 128 + 16, :]
        lane = lax.broadcasted_iota(jnp.int32, (1, 128), 1)
        sa = jnp.sum(jnp.where(lane < HEAD_DIM, seg, 0.0), axis=1, keepdims=True)
        sb = jnp.sum(jnp.where(lane < HEAD_DIM, 0.0, seg), axis=1, keepdims=True)
        dsk_ref[...] = jnp.where(lane == 0, sa, jnp.where(lane == 1, sb, 0.0))
        for p in range(1, 8):
            copy(p).wait_send()

    vmem = pl.BlockSpec(memory_space=pltpu.VMEM)
    return _call(
        body, [vec], name="gather_small", in_specs=[vmem], out_specs=[vmem, vmem, vmem],
        out_shape=[jax.ShapeDtypeStruct((8, rows, 128), F32), jax.ShapeDtypeStruct((rows, 128), F32),
                   jax.ShapeDtypeStruct((16, 128), F32)],
        scratch_shapes=[pltpu.SemaphoreType.DMA((7,)), pltpu.SemaphoreType.DMA((7,))], carry=carry)


def _gather_carry(shards):
    n = len(shards)

    def copies(ins, outs, sems):
        x, y, c = _me()
        chip = 2 * x + y

        def half(w, which):
            h = shards[w].shape[0] // 2
            return pl.ds(which * h, h)

        def first(w, m):
            return pltpu.make_async_remote_copy(
                src_ref=ins[w].at[half(w, c)], dst_ref=outs[w].at[chip, half(w, c)],
                send_sem=sems.send(6 * w + m - 1), recv_sem=sems.recv(6 * w + m - 1),
                device_id=_xor_peer(x, y, c, 2 * m), device_id_type=MESH)

        def landed(w, m):
            return pltpu.make_async_remote_copy(
                src_ref=ins[w].at[half(w, c)], dst_ref=outs[w].at[chip ^ m, half(w, c)],
                send_sem=sems.send(6 * w + m - 1), recv_sem=sems.recv(6 * w + m - 1),
                device_id=_xor_peer(x, y, c, 2 * m), device_id_type=MESH)

        def passed(w, m, which):
            part = outs[w].at[chip ^ m, half(w, which)]
            return pltpu.make_async_remote_copy(
                src_ref=part, dst_ref=part, send_sem=sems.send(6 * w + 2 + m), recv_sem=sems.recv(6 * w + 2 + m),
                device_id=(x, y, 1 - c), device_id_type=MESH)

        return c, first, landed, passed

    pairs = [(w, m) for w in range(n) for m in range(1, 4)]

    def start(ins, outs, sems):
        _, first, _, _ = copies(ins, outs, sems)
        for w, m in pairs:
            first(w, m).start()

    def finish(ins, outs, sems):
        c, first, landed, passed = copies(ins, outs, sems)
        for w, m in pairs:
            landed(w, m).wait_recv()
            passed(w, m, c).start()
        for w, m in pairs:
            passed(w, m, 1 - c).wait_recv()
        for w, m in pairs:
            first(w, m).wait_send()
            passed(w, m, c).wait_send()

    return _Carry(shards, [jax.ShapeDtypeStruct((N_CHIPS,) + s.shape, s.dtype) for s in shards], 6 * n, start, finish)


def _pair_exchange_carry(grads):
    n = len(grads)

    def copy(ins, outs, sems, w):
        x, y, c = _me()
        h = grads[w].shape[1] // 2
        return pltpu.make_async_remote_copy(
            src_ref=ins[w].at[:, pl.ds((1 - c) * h, h)], dst_ref=outs[w],
            send_sem=sems.send(w), recv_sem=sems.recv(w), device_id=(x, y, 1 - c), device_id_type=MESH)

    def start(ins, outs, sems):
        for w in range(n):
            copy(ins, outs, sems, w).start()

    def finish(ins, outs, sems):
        for w in range(n):
            copy(ins, outs, sems, w).wait()

    return _Carry(grads, [jax.ShapeDtypeStruct((N_CHIPS, g.shape[1] // 2, g.shape[2]), g.dtype) for g in grads], n,
                  start, finish)


def _chip_exchange_carry(partials):
    n = len(partials)

    def copier(ins, outs, sems):
        x, y, c = _me()
        chip = 2 * x + y

        def copy(w, m, landed):
            return pltpu.make_async_remote_copy(
                src_ref=ins[w].at[chip ^ m], dst_ref=outs[w].at[(chip ^ m) if landed else chip],
                send_sem=sems.send(3 * w + m - 1), recv_sem=sems.recv(3 * w + m - 1),
                device_id=_xor_peer(x, y, c, 2 * m), device_id_type=MESH)

        return copy

    pairs = [(w, m) for w in range(n) for m in range(1, 4)]

    def start(ins, outs, sems):
        copy = copier(ins, outs, sems)
        for w, m in pairs:
            copy(w, m, False).start()

    def finish(ins, outs, sems):
        copy = copier(ins, outs, sems)
        for w, m in pairs:
            copy(w, m, True).wait_recv()
        for w, m in pairs:
            copy(w, m, False).wait_send()

    return _Carry(partials, [jax.ShapeDtypeStruct(p.shape, p.dtype) for p in partials], 3 * n, start, finish)


def _pair_share_carry(shards):
    n = len(shards)

    def copier(ins, outs, sems):
        x, y, c = _me()

        def copy(w, which):
            h = shards[w].shape[0] // 2
            rows = pl.ds(which * h, h)
            return pltpu.make_async_remote_copy(
                src_ref=ins[w].at[rows], dst_ref=outs[w].at[rows],
                send_sem=sems.send(w), recv_sem=sems.recv(w), device_id=(x, y, 1 - c), device_id_type=MESH)

        return c, copy

    def start(ins, outs, sems):
        c, copy = copier(ins, outs, sems)
        for w in range(n):
            copy(w, c).start()

    def finish(ins, outs, sems):
        c, copy = copier(ins, outs, sems)
        for w in range(n):
            copy(w, 1 - c).wait_recv()
        for w in range(n):
            copy(w, c).wait_send()

    return _Carry(shards, [jax.ShapeDtypeStruct(s.shape, s.dtype) for s in shards], n, start, finish,
                  aliased=[(w, w) for w in range(n)])


def _pair_sum(g, part, idx, name):
    _, h, C = part.shape
    tr = min(512, h)
    nb = h // tr

    def body(idx_ref, g_ref, p_ref, o16_ref, own_ref):
        v = g_ref[...].astype(F32) + p_ref[...].astype(F32)
        o16_ref[...] = v.astype(BF16)

        @pl.when(pl.program_id(1) == idx_ref[1])
        def _():
            own_ref[...] = v

    return pl.pallas_call(
        body, name=name,
        grid_spec=pltpu.PrefetchScalarGridSpec(
            num_scalar_prefetch=1, grid=(nb, N_CHIPS),
            in_specs=[pl.BlockSpec((None, tr, C), lambda i, s, idx_ref: (s, idx_ref[0] * nb + i, 0)),
                      pl.BlockSpec((None, tr, C), lambda i, s, idx_ref: (s, i, 0))],
            out_specs=[pl.BlockSpec((None, tr, C), lambda i, s, idx_ref: (s, i, 0)),
                       pl.BlockSpec((tr, C), lambda i, s, idx_ref: (i, 0))]),
        out_shape=[jax.ShapeDtypeStruct(part.shape, BF16), jax.ShapeDtypeStruct((h, C), F32)],
        compiler_params=pltpu.CompilerParams(dimension_semantics=("arbitrary", "arbitrary"), vmem_limit_bytes=VMEM_LIMIT),
    )(idx, g, part)


def _chip_sum(own, slots, idx, name):
    h, C = own.shape
    tr = min(512, h)
    nb = h // tr

    def body(idx_ref, own_ref, s1_ref, s2_ref, s3_ref, o_ref):
        del idx_ref
        o_ref[...] = ((own_ref[...] + s1_ref[...].astype(F32)) + s2_ref[...].astype(F32)) + s3_ref[...].astype(F32)

    def slot(m):
        return pl.BlockSpec((None, tr, C), lambda i, idx_ref: (idx_ref[1] ^ m, i, 0))

    return pl.pallas_call(
        body, name=name,
        grid_spec=pltpu.PrefetchScalarGridSpec(
            num_scalar_prefetch=1, grid=(nb,),
            in_specs=[pl.BlockSpec((tr, C), lambda i, idx_ref: (i, 0)), slot(1), slot(2), slot(3)],
            out_specs=pl.BlockSpec((tr, C), lambda i, idx_ref: (idx_ref[0] * nb + i, 0))),
        out_shape=jax.ShapeDtypeStruct((2 * h, C), F32),
        compiler_params=pltpu.CompilerParams(dimension_semantics=("parallel",), vmem_limit_bytes=VMEM_LIMIT),
    )(idx, own, slots, slots, slots)


class _Reducer:
    def __init__(self, idx):
        self.idx, self.chips, self.p16, self.own, self.mine, self.final = idx, {}, {}, {}, {}, {}

    def add(self, name, whole, chip_blocks=False):
        self.chips[name] = whole if chip_blocks else _chips_from_whole(name, whole)

    def pair(self, names):
        return _pair_exchange_carry([self.chips[n] for n in names])

    def take_pair(self, names, outs):
        for n, part in zip(names, outs):
            self.p16[n], self.own[n] = _pair_sum(self.chips.pop(n), part, self.idx, "pair_sum_" + n)

    def chip(self, names):
        return _chip_exchange_carry([self.p16[n] for n in names])

    def take_chip(self, names, outs):
        for n, slots in zip(names, outs):
            del self.p16[n]
            self.mine[n] = _chip_sum(self.own.pop(n), slots, self.idx, "chip_sum_" + n)

    def share(self, names):
        return _pair_share_carry([self.mine[n] for n in names])

    def take_share(self, names, outs):
        for n, s in zip(names, outs):
            del self.mine[n]
            self.final[n] = s


def _w_ada_grad(c8, dmod_cols):
    n_cols = dmod_cols.shape[1]
    tn = 512

    def body(c_ref, d_ref, o_ref):
        cv = c_ref[...]
        o_ref[...] = _tn(cv * _sigmoid(cv), d_ref[...], precision=HIGH)

    return _call(body, [c8, dmod_cols], name="w_ada_grad", grid=(n_cols // tn,),
                 in_specs=[pl.BlockSpec((8, D), lambda j: (0, 0)), pl.BlockSpec((8, tn), lambda j: (0, j))],
                 out_specs=[pl.BlockSpec((D, tn), lambda j: (0, j))],
                 out_shape=[jax.ShapeDtypeStruct((D, n_cols), F32)], sem=("parallel",))[0]


_SMALL_SEGS = (("dmod", 6144), ("norm_mix_w", 1024), ("conv_b", 3072), ("ssd_norm_w", 2048), ("pool_scale", 1024),
               ("norm_mlp_w", 1024), ("norm_final_w", 1024), ("conv_w", 4 * XBC), ("d_skip", 2048), ("a_log", 128),
               ("dt_bias", 128), ("loss", 128))
SMALL_OFF = {}
_o = 0
for _n, _s in _SMALL_SEGS:
    SMALL_OFF[_n] = _o
    _o += _s
SMALL_LEN = -(-_o // 1024) * 1024

_FIRST = ("w_in", "conv_w")
_LATER = ("w_branch_ssd", "pool_w", "w_branch_pool", "w_out", "w_up", "w_down")
_SMALL_REPLICATED = ("b_ada", "norm_mix_w", "conv_b", "dt_bias", "a_log", "d_skip", "ssd_norm_w", "pool_scale",
                     "norm_mlp_w", "norm_final_w")
_WEIGHTS = ("w_ada", "b_ada", "norm_mix_w", "w_in", "conv_w", "conv_b", "dt_bias", "a_log", "d_skip", "ssd_norm_w",
            "w_branch_ssd", "pool_w", "pool_scale", "w_branch_pool", "w_out", "norm_mlp_w", "w_up", "w_down",
            "norm_final_w")


def _shard_2d(name, a):
    if name == "conv_w":
        return a.reshape(16, -1)
    return (a.reshape(GW, GW) if name == "pool_w" else a.reshape(a.shape[-2], a.shape[-1])).astype(BF16)


def _whole_from_chips(name, g, own, chip):
    g = lax.dynamic_update_slice(g, own[None], (chip, 0, 0))
    if name == "w_in":
        a, b = _DT_IN_CHIP2, _DT_IN_CHIP2 + HEADS
        pad = jnp.zeros((D, NP - IN_COLS), g.dtype)
        return jnp.concatenate([g[0], g[1], g[2][:, :a], g[2][:, b:], g[3], g[2][:, a:b], pad], axis=1)
    if name == "w_up":
        return jnp.concatenate([g[k] for k in range(N_CHIPS)], axis=1)
    if name == "pool_w":
        return jnp.transpose(g.reshape(N_CHIPS, 4, GW // N_CHIPS, GW), (1, 0, 2, 3)).reshape(4, GW, GW)
    if name == "conv_w":
        return jnp.transpose(g.reshape(N_CHIPS, 4, XBC // N_CHIPS), (1, 0, 2)).reshape(4, XBC)
    return g.reshape(N_CHIPS * g.shape[1], g.shape[2])


def _chips_from_whole(name, g):
    if name.startswith("w_in"):
        cw, a = IN_COLS // N_CHIPS, _DT_IN_CHIP2
        chip2 = jnp.concatenate([g[:, 2 * cw:2 * cw + a], g[:, OFF_DT:OFF_DT + HEADS], g[:, 5120:3 * cw - HEADS]], axis=1)
        return jnp.stack([g[:, :cw], g[:, cw:2 * cw], chip2, g[:, 3 * cw - HEADS:OFF_DT]])
    if name == "w_up":
        return jnp.transpose(g.reshape(D, N_CHIPS, DFF // N_CHIPS), (1, 0, 2))
    if name == "pool_w":
        return jnp.transpose(g.reshape(4, N_CHIPS, GW // N_CHIPS, GW), (1, 0, 2, 3)).reshape(N_CHIPS, GW, GW)
    return g.reshape(N_CHIPS, g.shape[0] // N_CHIPS, g.shape[1])


def kernel(x, c, w_ada, b_ada, norm_mix_w, w_in, conv_w, conv_b, dt_bias, a_log, d_skip, ssd_norm_w, w_branch_ssd, pool_w, pool_scale, w_branch_pool, w_out, norm_mlp_w, w_up, w_down, norm_final_w, loss_target, m_w_ada, m_b_ada, m_norm_mix_w, m_w_in, m_conv_w, m_conv_b, m_dt_bias, m_a_log, m_d_skip, m_ssd_norm_w, m_w_branch_ssd, m_pool_w, m_pool_scale, m_w_branch_pool, m_w_out, m_norm_mlp_w, m_w_up, m_w_down, m_norm_final_w, v_w_ada, v_b_ada, v_norm_mix_w, v_w_in, v_conv_w, v_conv_b, v_dt_bias, v_a_log, v_d_skip, v_ssd_norm_w, v_w_branch_ssd, v_pool_w, v_pool_scale, v_w_branch_pool, v_w_out, v_norm_mlp_w, v_w_up, v_w_down, v_norm_final_w):
    args = locals()
    w = {n: args[n] for n in _WEIGHTS}
    m = {n: args["m_" + n] for n in _WEIGHTS}
    v = {n: args["v_" + n] for n in _WEIGHTS}
    xi, yi, ci = _me()
    chip = 2 * xi + yi
    idx = jnp.stack([ci, chip]).astype(jnp.int32)
    ada_cols = w_ada.shape[-1]
    xs, target = x[0], loss_target[0]
    two_d = lambda n, a: a.reshape(GW, GW) if n == "pool_w" else a.reshape(-1, a.shape[-1])
    delta, new_m, new_v, g = {}, {}, {}, {}

    def adamw(n, carry=None):
        res = _adamw(two_d(n, w[n]), two_d(n, g[n]), two_d(n, m[n]), two_d(n, v[n]), "adamw_" + n, carry=carry)
        (delta[n], new_m[n], new_v[n]), extra = res if carry is not None else (res, None)
        return extra

    b_mine = lax.dynamic_slice(b_ada, (0, chip * ada_cols), (1, ada_cols))
    shards = {n: _shard_2d(n, w[n]) for n in _FIRST + _LATER}
    mod, c8 = _ada_fwd(c, w_ada[0], b_mine)
    c8 = c8[:, 0, :]
    shift_m, scale_m, gate_m, shift_f, scale_f, gate_f = [mod[:, D * i:D * (i + 1)] for i in range(6)]
    nf_w = norm_final_w.reshape(1, D)

    h1, first = _norm_mod(xs, norm_mix_w, scale_m, shift_m, "norm_mod_mix",
                          carry=_gather_carry([shards[n] for n in _FIRST]))
    p ={n: _whole_from_chips(n, a, shards[n], chip) for n, a in zip(_FIRST, first)}
    (proj,), later = _matmul(h1, p["w_in"], mode="nn", out_dtypes=[F32], name="mm_proj", cols_outer=True,
                             carry=_gather_carry([shards[n] for n in _LATER]))
    p.update({n: _whole_from_chips(n, a, shards[n], chip) for n, a in zip(_LATER, later)})
    xbc_a = _conv_fwd(proj, p["conv_w"], conv_b)
    dtb_c, alog_c = dt_bias.reshape(HEADS, 1), a_log.reshape(HEADS, 1)
    dsk_exp = jnp.repeat(d_skip, HEAD_DIM, axis=1)
    y, hin = _ssd_fwd(xbc_a, proj, dt_bias, a_log, dtb_c, alog_c, dsk_exp)
    yn = _gate_norm(y, proj, ssd_norm_w)
    (y_ssd,) = _matmul(yn, p["w_branch_ssd"], mode="nn", out_dtypes=[F32], name="mm_branch_ssd")
    pooled, pw_out, yps = _pool_fwd(proj, p["pool_w"], pool_scale)
    (y_pool,) = _matmul(yps, p["w_branch_pool"], mode="nn", out_dtypes=[F32], name="mm_branch_pool")
    merged = _merge(proj, y_ssd, y_pool)
    resid = lambda acc, r, gt: (r + gt * acc, acc)
    x2, mix = _matmul(merged, p["w_out"], mode="nn", out_dtypes=[F32, BF16], name="mm_out",
                      epi=resid, tile_extras=(xs,), row_extras=(gate_m,))
    h2 = _norm_mod(x2, norm_mlp_w, scale_f, shift_f, "norm_mod_mlp")
    relu2 = lambda acc: (jnp.square(jnp.maximum(acc, 0.0)),)
    (act,) = _matmul(h2, p["w_up"], mode="nn", out_dtypes=[BF16], name="mm_up", epi=relu2)
    x3, down = _matmul(act, p["w_down"], mode="nn", out_dtypes=[F32, BF16], name="mm_down",
                       epi=resid, tile_extras=(x2,), row_extras=(gate_f,))

    red = _Reducer(idx)
    dx3, d_down, sums_f = _final_loss_bwd(x3, target, nf_w, down, gate_f)
    drelu2 = lambda acc, a: (acc * (2.0 * jnp.sqrt(a)).astype(F32),)
    (dup,) = _matmul(d_down, p["w_down"], mode="nt", out_dtypes=[BF16], name="mm_dact",
                     epi=drelu2, tile_extras=(act,))
    red.add("w_down", _matmul(act, d_down, mode="tn", out_dtypes=[BF16], name="mm_g_down")[0])
    (dh2,), got = _matmul(dup, p["w_up"], mode="nt", out_dtypes=[F32], name="mm_dh2",
                          carry=red.pair(["w_down"]))
    red.take_pair(["w_down"], got)
    red.add("w_up", _matmul(h2, dup, mode="tn", out_dtypes=[BF16], name="mm_g_up", chip_blocks=True)[0], chip_blocks=True)
    dx2, sums_2, dmix = _norm_mod_bwd(x2, dh2, dx3, norm_mlp_w, scale_f, "norm_mod_mlp_bwd", branch=mix, gate=gate_m)
    (dmerged,), got = _matmul(dmix, p["w_out"], mode="nt", out_dtypes=[F32], name="mm_dmerged",
                              carry=red.pair(["w_up"]))
    red.take_pair(["w_up"], got)
    red.add("w_out", _matmul(merged, dmix, mode="tn", out_dtypes=[BF16], name="mm_g_out")[0])
    dy_ssd, dy_pool, dproj = _merge_bwd(dmerged, proj, y_ssd, y_pool)
    (dyp,), got = _matmul(dy_pool, p["w_branch_pool"], mode="nt", out_dtypes=[F32], name="mm_dyp",
                          carry=red.pair(["w_out"]))
    red.take_pair(["w_out"], got)
    red.add("w_branch_pool", _matmul(yps, dy_pool, mode="tn", out_dtypes=[BF16], name="mm_g_bpool")[0])
    dproj, g_pool_w, sums_pool = _pool_bwd(dyp, pw_out, pooled, p["pool_w"], pool_scale, dproj)
    red.add("pool_w", g_pool_w.astype(BF16))
    red.add("w_branch_ssd", _matmul(yn, dy_ssd, mode="tn", out_dtypes=[BF16], name="mm_g_bssd")[0])
    mixers = ["w_branch_pool", "pool_w", "w_branch_ssd"]
    (dyn,), got = _matmul(dy_ssd, p["w_branch_ssd"], mode="nt", out_dtypes=[F32], name="mm_dyn",
                          carry=red.pair(mixers))
    red.take_pair(mixers, got)
    dy, dproj, sums_gn = _gate_norm_bwd(dyn, y, proj, ssd_norm_w, dproj)
    six = ["w_down", "w_up", "w_out"] + mixers
    (dxa, dproj, dsk_sum, ssd_small), got = _ssd_bwd(dy, xbc_a, proj, hin, dt_bias, a_log, dtb_c, alog_c, dsk_exp,
                                                     dproj, carry=red.chip(six))
    red.take_chip(six, got)
    dxc, sums_conv = _conv_bwd_a(dxa, proj, p["conv_w"], conv_b)
    dproj = _conv_bwd_b(dxc, p["conv_w"], dproj)
    rows_a = 3 * D // 4
    (g_in_a,), got = _matmul(h1, dproj, mode="tn", out_dtypes=[BF16], name="mm_g_in_a", a_cols=(0, rows_a),
                             carry=red.share(six))
    red.take_share(six, got)
    red.add("w_in_a", g_in_a)
    (g_in_b,), got = _matmul(h1, dproj, mode="tn", out_dtypes=[BF16], name="mm_g_in_b", a_cols=(rows_a, D - rows_a),
                             carry=red.pair(["w_in_a"]))
    red.take_pair(["w_in_a"], got)
    red.add("w_in_b", g_in_b)
    (dh1,), got = _matmul(dproj, p["w_in"], mode="nt", out_dtypes=[F32], name="mm_dh1",
                          carry=_join(red.chip(["w_in_a"]), red.pair(["w_in_b"])))
    red.take_chip(["w_in_a"], got[:1])
    red.take_pair(["w_in_b"], got[1:])
    grad_x, sums_1 = _norm_mod_bwd(xs, dh1, dx2, norm_mix_w, scale_m, "norm_mod_mix_bwd")

    dmod = jnp.concatenate([sums_1[0:1], sums_1[1:2], sums_2[3:4], sums_2[0:1], sums_2[1:2], sums_f[1:2]], axis=1)
    pad96 = jnp.zeros((1, 96), F32)
    small = {"dmod": dmod, "norm_mix_w": sums_1[2:3], "conv_b": sums_conv[4:5], "ssd_norm_w": sums_gn[0:1],
             "pool_scale": sums_pool[0:1], "norm_mlp_w": sums_2[2:3], "norm_final_w": sums_f[0:1],
             "conv_w": sums_conv[0:4].reshape(1, 4 * XBC), "d_skip": dsk_sum[0:1],
             "a_log": jnp.concatenate([ssd_small[0:1], pad96], axis=1),
             "dt_bias": jnp.concatenate([ssd_small[1:2], pad96], axis=1), "loss": sums_f[3:4, 0:128]}
    vec = jnp.concatenate([small[n] for n, _ in _SMALL_SEGS], axis=1)
    vec = jnp.pad(vec, ((0, 0), (0, SMALL_LEN - vec.shape[1]))).reshape(SMALL_LEN // 128, 128)
    (every, total, dsk), got = _gather_small(vec, carry=_join(red.chip(["w_in_b"]), red.share(["w_in_a"])))
    red.take_chip(["w_in_b"], got[:1])
    red.take_share(["w_in_a"], got[1:])
    total = total.reshape(1, SMALL_LEN)
    seg = lambda n, size: total[:, SMALL_OFF[n]:SMALL_OFF[n] + size]
    g.update({"b_ada": seg("dmod", 6 * D), "norm_mix_w": seg("norm_mix_w", D), "conv_b": seg("conv_b", XBC),
              "dt_bias": seg("dt_bias", HEADS), "a_log": seg("a_log", HEADS), "d_skip": dsk[:, 0:2].reshape(1, HEADS),
              "ssd_norm_w": seg("ssd_norm_w", DI), "pool_scale": seg("pool_scale", D),
              "norm_mlp_w": seg("norm_mlp_w", D), "norm_final_w": seg("norm_final_w", D)})
    loss = total[0, SMALL_OFF["loss"]]
    conv_cols = conv_w.shape[-1]
    g["conv_w"] = lax.dynamic_slice(seg("conv_w", 4 * XBC).reshape(4, XBC), (0, chip * conv_cols), (4, conv_cols))
    dmod8 = every.reshape(8, SMALL_LEN)[:, SMALL_OFF["dmod"]:SMALL_OFF["dmod"] + 6 * D]
    g["w_ada"] = _w_ada_grad(c8, lax.dynamic_slice(dmod8, (0, chip * ada_cols), (8, ada_cols)))

    got = adamw("w_ada", carry=red.share(["w_in_b"]))
    red.take_share(["w_in_b"], got)
    for n in six:
        g[n] = red.final[n]
    g["w_in"] = jnp.concatenate([red.final["w_in_a"], red.final["w_in_b"]], axis=0)
    for n in ["conv_w", "w_in"] + six:
        adamw(n)
    sizes = [w[n].size for n in _SMALL_REPLICATED]
    n_small = -(-sum(sizes) // 1024) * 1024
    pack = lambda d: jnp.pad(jnp.concatenate([d[n].reshape(1, -1) for n in _SMALL_REPLICATED], axis=1),
                             ((0, 0), (0, n_small - sum(sizes)))).reshape(n_small // 128, 128)
    d_, m_, v_ = _adamw(pack(w), pack(g), pack(m), pack(v), "adamw_small")
    off = 0
    for n, s in zip(_SMALL_REPLICATED, sizes):
        for dst, src in ((delta, d_), (new_m, m_), (new_v, v_)):
            dst[n] = src.reshape(1, n_small)[:, off:off + s]
        off += s

    out = [loss, grad_x.reshape(x.shape)]
    for d in (g, delta, new_m, new_v):
        out += [d[n].reshape(w[n].shape) for n in _WEIGHTS]
    return tuple(out)
```

```python
import functools
import operator

import jax
import jax.numpy as jnp
import numpy as np
from jax import lax
from jax.experimental import pallas as pl
from jax.experimental.pallas import tpu as pltpu

F32, BF16 = jnp.float32, jnp.bfloat16
HIGH = lax.Precision.HIGHEST
MESH = pl.DeviceIdType.MESH

D = 1024
DI = 2048
HEADS, HEAD_DIM = 32, 64
GROUPS, STATE = 4, 128
Q = 128
XBC = DI + 2 * GROUPS * STATE
POOL_WINDOWS = (2, 4, 8, 16)
GW = 256
DFF = 4096
EPS = 1e-5
IN_COLS = 8224
OFF_Z, OFF_XBC, OFF_POOL, OFF_GATE, OFF_DT, NP = 0, 2048, 5120, 6144, 8192, 8448
N_CHIPS = 4
ADAM_LR, ADAM_B1, ADAM_B2, ADAM_EPS, ADAM_WD, ADAM_STEP = 0.001, 0.9, 0.999, 1e-08, 0.01, 10
VMEM_LIMIT = 56 * 2 ** 20
NEG = -1e30


def _sigmoid(v):
    return 0.5 * jnp.tanh(0.5 * v) + 0.5


def _softplus(v):
    return jnp.maximum(v, 0.0) + jnp.log1p(jnp.exp(-jnp.abs(v)))


def _dot(a, b, dims, **kw):
    return lax.dot_general(a, b, (dims, ((), ())), preferred_element_type=F32, **kw)


def _nn(a, b, **kw):
    return _dot(a, b, ((1,), (0,)), **kw)


def _nt(a, b, **kw):
    return _dot(a, b, ((1,), (1,)), **kw)


def _tn(a, b, **kw):
    return _dot(a, b, ((0,), (0,)), **kw)


_DT_IN_CHIP2 = 5120 - 2 * (IN_COLS // 4)


class _Sems:
    def __init__(self, send, recv, local, base=0):
        self._send, self._recv, self._local, self._base = send, recv, local, base

    def shift(self, n):
        return _Sems(self._send, self._recv, self._local, self._base + n)

    def send(self, i):
        return self._send.at[self._base + i]

    def recv(self, i):
        return self._recv.at[self._base + i]

    def local(self, i):
        return self._local.at[self._base + i]


class _Carry:
    def __init__(self, ins, out_shapes, n_sems, start, finish, aliased=()):
        self.ins, self.out_shapes, self.n_sems, self.start, self.finish = list(ins), list(out_shapes), n_sems, start, finish
        self.aliased = list(aliased)


def _join(*carries):
    def run(which):
        def fn(ins, outs, sems):
            i = o = s = 0
            for cy in carries:
                getattr(cy, which)(ins[i:i + len(cy.ins)], outs[o:o + len(cy.out_shapes)], sems.shift(s))
                i, o, s = i + len(cy.ins), o + len(cy.out_shapes), s + cy.n_sems
        return fn

    aliased, i, o = [], 0, 0
    for cy in carries:
        aliased += [(i + a, o + b) for a, b in cy.aliased]
        i, o = i + len(cy.ins), o + len(cy.out_shapes)
    return _Carry([a for cy in carries for a in cy.ins], [a for cy in carries for a in cy.out_shapes],
                  sum(cy.n_sems for cy in carries), run("start"), run("finish"), aliased)


def _call(body, args, *, name, grid=(), in_specs, out_specs, out_shape, scratch_shapes=(), sem=None, aliases=None,
          carry=None):
    in_specs, out_specs, out_shape, scratch_shapes = list(in_specs), list(out_specs), list(out_shape), list(scratch_shapes)
    n_in, n_out, n_scr = len(in_specs), len(out_specs), len(scratch_shapes)
    kw = {"vmem_limit_bytes": VMEM_LIMIT}
    if carry is None:
        kernel_fn = functools.partial(body)
        if sem is not None:
            kw["dimension_semantics"] = sem
    else:
        n_ci, n_co = len(carry.ins), len(carry.out_shapes)
        hbm = pl.BlockSpec(memory_space=pl.ANY)
        in_specs += [hbm] * n_ci
        out_specs += [hbm] * n_co
        out_shape += carry.out_shapes
        n_s = max(carry.n_sems, 1)
        scratch_shapes += [pltpu.SemaphoreType.DMA((n_s,))] * 3
        args = list(args) + carry.ins
        aliases = dict(aliases or {})
        aliases.update({n_in + i: n_out + o for i, o in carry.aliased})
        if grid:
            kw["dimension_semantics"] = ("arbitrary",) * len(grid)

        def kernel_fn(*refs):
            a = n_in
            ins, c_ins = refs[:a], refs[a:a + n_ci]
            a += n_ci
            outs, c_outs = refs[a:a + n_out], refs[a + n_out:a + n_out + n_co]
            a += n_out + n_co
            scr, sems = refs[a:a + n_scr], _Sems(*refs[a + n_scr:a + n_scr + 3])
            if grid:
                ids = [pl.program_id(d) for d in range(len(grid))]
                first = functools.reduce(operator.and_, [i == 0 for i in ids])
                last = functools.reduce(operator.and_, [i == g - 1 for i, g in zip(ids, grid)])

                @pl.when(first)
                def _():
                    carry.start(c_ins, c_outs, sems)

                body(*ins, *outs, *scr)

                @pl.when(last)
                def _():
                    carry.finish(c_ins, c_outs, sems)
            else:
                carry.start(c_ins, c_outs, sems)
                body(*ins, *outs, *scr)
                carry.finish(c_ins, c_outs, sems)

    outs = pl.pallas_call(
        kernel_fn, name=name, grid=grid, in_specs=in_specs, out_specs=out_specs, out_shape=out_shape,
        scratch_shapes=scratch_shapes, input_output_aliases=aliases or {},
        compiler_params=pltpu.CompilerParams(**kw),
    )(*args)
    outs = list(outs)
    return outs if carry is None else (outs[:n_out], outs[n_out:])


def _run_carry(carry, name):
    _, outs = _call(lambda: None, [], name=name, in_specs=[], out_specs=[], out_shape=[], carry=carry)
    return outs


_TILES = {
    "mm_proj": (1024, 2816, 1024), "mm_branch_ssd": (1024, 1024, 2048), "mm_branch_pool": (1024, 1024, 1024),
    "mm_out": (1024, 1024, 1024), "mm_up": (2048, 1024, 1024), "mm_down": (512, 1024, 4096),
    "mm_dact": (1024, 1024, 1024), "mm_g_down": (1024, 1024, 4096), "mm_dh2": (1024, 1024, 4096),
    "mm_g_up": (1024, 1024, 4096), "mm_dmerged": (1024, 1024, 1024), "mm_g_out": (1024, 1024, 2048),
    "mm_dyp": (1024, 1024, 1024), "mm_g_bpool": (1024, 1024, 2048), "mm_g_bssd": (1024, 1024, 4096),
    "mm_dyn": (1024, 1024, 1024), "mm_g_in_a": (768, 1408, 4096), "mm_g_in_b": (256, 2816, 2048),
    "mm_dh1": (1024, 1024, 4224),
}


def _matmul(a, b, *, mode, out_dtypes, name, epi=None, tile_extras=(), row_extras=(), carry=None, a_cols=None,
            chip_blocks=False, cols_outer=False):
    M, K = (a.shape[1], a.shape[0]) if mode == "tn" else a.shape
    N = b.shape[0] if mode == "nt" else b.shape[1]
    a_start, M = a_cols if a_cols is not None else (0, M)
    tm, tn, tk = _TILES[name]
    tm, tn, tk = min(tm, M), min(tn, N), min(tk, K)
    assert M % tm == 0 and N % tn == 0 and K % tk == 0 and a_start % tm == 0, (name, M, N, K, tm, tn, tk)
    a_off = a_start // tm
    if mode == "nn":
        a_spec = pl.BlockSpec((tm, tk), lambda i, j, k: (i, k))
        b_spec = pl.BlockSpec((tk, tn), lambda i, j, k: (k, j))
        dims = ((1,), (0,))
    elif mode == "nt":
        a_spec = pl.BlockSpec((tm, tk), lambda i, j, k: (i, k))
        b_spec = pl.BlockSpec((tn, tk), lambda i, j, k: (j, k))
        dims = ((1,), (1,))
    else:
        a_spec = pl.BlockSpec((tk, tm), lambda i, j, k: (k, i + a_off))
        b_spec = pl.BlockSpec((tk, tn), lambda i, j, k: (k, j))
        dims = ((0,), (0,))
    nk = K // tk
    n_te, n_re, n_out = len(tile_extras), len(row_extras), len(out_dtypes)
    if epi is None:
        epi = lambda acc: (acc,)

    def body(a_ref, b_ref, *rest):
        extras = rest[:n_te + n_re]
        outs = rest[n_te + n_re:n_te + n_re + n_out]
        p = _dot(a_ref[...], b_ref[...], dims)

        def finish(acc):
            vals = epi(acc, *[e[...] for e in extras])
            for o, v in zip(outs, vals):
                o[...] = v.astype(o.dtype)

        if nk == 1:
            finish(p)
        else:
            acc_ref = rest[-1]
            k = pl.program_id(2)

            @pl.when(k == 0)
            def _():
                acc_ref[...] = p

            @pl.when(k > 0)
            def _():
                acc_ref[...] += p

            @pl.when(k == nk - 1)
            def _():
                finish(acc_ref[...])

    tile_spec = pl.BlockSpec((tm, tn), lambda i, j, k: (i, j))
    row_spec = pl.BlockSpec((1, tn), lambda i, j, k: (0, j))
    out_spec, out_dims = tile_spec, (M, N)
    if chip_blocks:
        assert n_te == 0 and tn * N_CHIPS == N
        out_spec, out_dims = pl.BlockSpec((None, tm, tn), lambda i, j, k: (j, i, 0)), (N_CHIPS, M, tn)
    in_specs, grid = [a_spec, b_spec] + [tile_spec] * n_te + [row_spec] * n_re, (M // tm, N // tn, nk)
    if cols_outer:
        swap = lambda s: pl.BlockSpec(s.block_shape, lambda g0, g1, k, f=s.index_map: f(g1, g0, k))
        in_specs, out_spec, grid = [swap(s) for s in in_specs], swap(out_spec), (N // tn, M // tm, nk)
    return _call(
        body, [a, b, *tile_extras, *row_extras], name=name, grid=grid,
        in_specs=in_specs, out_specs=[out_spec] * n_out,
        out_shape=[jax.ShapeDtypeStruct(out_dims, dt) for dt in out_dtypes],
        scratch_shapes=[pltpu.VMEM((tm, tn), F32)] if nk > 1 else [],
        sem=("parallel", "parallel", "arbitrary"), carry=carry)


def _row_tile(T):
    return min(512, T)


def _norm_mod(x, nw, scale, shift, name, carry=None):
    T = x.shape[0]
    tr = _row_tile(T)

    def body(x_ref, nw_ref, sc_ref, sh_ref, o_ref):
        xv = x_ref[...]
        r = lax.rsqrt(jnp.mean(xv * xv, axis=-1, keepdims=True) + EPS)
        o_ref[...] = ((xv * r) * nw_ref[...] * (1.0 + sc_ref[...]) + sh_ref[...]).astype(BF16)

    tile = pl.BlockSpec((tr, D), lambda i: (i, 0))
    row = pl.BlockSpec((1, D), lambda i: (0, 0))
    res = _call(body, [x, nw, scale, shift], name=name, grid=(T // tr,), in_specs=[tile, row, row, row],
                out_specs=[tile], out_shape=[jax.ShapeDtypeStruct((T, D), BF16)], sem=("parallel",), carry=carry)
    return res[0] if carry is None else (res[0][0], res[1])


def _norm_mod_bwd(x, dh, dres, nw, scale, name, branch=None, gate=None, carry=None):
    T = x.shape[0]
    tr = _row_tile(T)
    with_branch = branch is not None

    def body(x_ref, dh_ref, dr_ref, nw_ref, sc_ref, *rest):
        if with_branch:
            br_ref, g_ref, dx_ref, sums_ref, db_ref = rest
        else:
            dx_ref, sums_ref = rest
        i = pl.program_id(0)

        @pl.when(i == 0)
        def _():
            sums_ref[...] = jnp.zeros_like(sums_ref)

        xv, dhv = x_ref[...], dh_ref[...]
        r = lax.rsqrt(jnp.mean(xv * xv, axis=-1, keepdims=True) + EPS)
        xn = xv * r
        g1 = dhv * (1.0 + sc_ref[...])
        dxn = g1 * nw_ref[...]
        dx = dr_ref[...] + r * (dxn - xn * jnp.mean(dxn * xn, axis=-1, keepdims=True))
        dx_ref[...] = dx
        sums_ref[0:1, :] += jnp.sum(dhv, axis=0, keepdims=True)
        sums_ref[1:2, :] += jnp.sum(dhv * (xn * nw_ref[...]), axis=0, keepdims=True)
        sums_ref[2:3, :] += jnp.sum(g1 * xn, axis=0, keepdims=True)
        if with_branch:
            db_ref[...] = (dx * g_ref[...]).astype(BF16)
            sums_ref[3:4, :] += jnp.sum(dx * br_ref[...], axis=0, keepdims=True)

    tile = pl.BlockSpec((tr, D), lambda i: (i, 0))
    row = pl.BlockSpec((1, D), lambda i: (0, 0))
    sums = pl.BlockSpec((8, D), lambda i: (0, 0))
    ins = [x, dh, dres, nw, scale] + ([branch, gate] if with_branch else [])
    in_specs = [tile, tile, tile, row, row] + ([tile, row] if with_branch else [])
    out_specs = [tile, sums] + ([tile] if with_branch else [])
    out_shape = [jax.ShapeDtypeStruct((T, D), F32), jax.ShapeDtypeStruct((8, D), F32)]
    if with_branch:
        out_shape.append(jax.ShapeDtypeStruct((T, D), BF16))
    return _call(body, ins, name=name, grid=(T // tr,), in_specs=in_specs, out_specs=out_specs, out_shape=out_shape,
                 sem=("arbitrary",), carry=carry)


def _final_loss_bwd(x3, target, wf, down, gate_f):
    T = x3.shape[0]
    tr = _row_tile(T)
    n_steps = T // tr

    def body(x_ref, t_ref, w_ref, dn_ref, g_ref, dx_ref, dd_ref, sums_ref):
        i = pl.program_id(0)

        @pl.when(i == 0)
        def _():
            sums_ref[...] = jnp.zeros_like(sums_ref)

        xv = x_ref[...]
        r = lax.rsqrt(jnp.mean(xv * xv, axis=-1, keepdims=True) + EPS)
        xn = xv * r
        err = xn * w_ref[...] - t_ref[...]
        dy = err * (1.0 / D)
        dxn = dy * w_ref[...]
        dx = r * (dxn - xn * jnp.mean(dxn * xn, axis=-1, keepdims=True))
        dx_ref[...] = dx
        dd_ref[...] = (dx * g_ref[...]).astype(BF16)
        sums_ref[0:1, :] += jnp.sum(dy * xn, axis=0, keepdims=True)
        sums_ref[1:2, :] += jnp.sum(dx * dn_ref[...], axis=0, keepdims=True)
        sums_ref[2:3, :] += jnp.sum(err * err, axis=0, keepdims=True) * (0.5 / D)

        @pl.when(i == n_steps - 1)
        def _():
            sums_ref[3:4, :] = jnp.broadcast_to(jnp.sum(sums_ref[2:3, :], axis=1, keepdims=True), (1, D))

    tile = pl.BlockSpec((tr, D), lambda i: (i, 0))
    row = pl.BlockSpec((1, D), lambda i: (0, 0))
    sums = pl.BlockSpec((8, D), lambda i: (0, 0))
    return _call(body, [x3, target, wf, down, gate_f], name="final_loss_bwd", grid=(n_steps,),
                 in_specs=[tile, tile, row, tile, row], out_specs=[tile, tile, sums],
                 out_shape=[jax.ShapeDtypeStruct((T, D), F32), jax.ShapeDtypeStruct((T, D), BF16),
                            jax.ShapeDtypeStruct((8, D), F32)], sem=("arbitrary",))


CONV_TC = 1024


def _conv_taps(xp, w, b):
    acc = b + w[3:4, :] * xp
    for k in range(3):
        acc = acc + w[k:k + 1, :] * pltpu.roll(xp, 3 - k, 0)
    return acc


def _conv_fwd(proj, conv_w, conv_b):
    T = proj.shape[0]
    tr = _row_tile(T)
    nb, offb = tr // 8, OFF_XBC // CONV_TC

    def body(x_ref, h_ref, w_ref, b_ref, o_ref):
        halo = jnp.where(pl.program_id(0) > 0, h_ref[...], 0.0)
        xp = jnp.concatenate([halo, x_ref[...]], axis=0)
        acc = _conv_taps(xp, w_ref[...], b_ref[...])[8:]
        o_ref[...] = acc * _sigmoid(acc)

    return _call(
        body, [proj, proj, conv_w, conv_b], name="conv_fwd", grid=(T // tr, XBC // CONV_TC),
        in_specs=[pl.BlockSpec((tr, CONV_TC), lambda i, j: (i, j + offb)),
                  pl.BlockSpec((8, CONV_TC), lambda i, j: (jnp.maximum(i * nb - 1, 0), j + offb)),
                  pl.BlockSpec((4, CONV_TC), lambda i, j: (0, j)),
                  pl.BlockSpec((1, CONV_TC), lambda i, j: (0, j))],
        out_specs=[pl.BlockSpec((tr, CONV_TC), lambda i, j: (i, j))],
        out_shape=[jax.ShapeDtypeStruct((T, XBC), F32)], sem=("parallel", "parallel"))[0]


def _conv_bwd_a(dxa, proj, conv_w, conv_b):
    T = proj.shape[0]
    tr = _row_tile(T)
    nb, offb = tr // 8, OFF_XBC // CONV_TC

    def body(d_ref, x_ref, h_ref, w_ref, b_ref, o_ref, sums_ref):
        i = pl.program_id(1)

        @pl.when(i == 0)
        def _():
            sums_ref[...] = jnp.zeros_like(sums_ref)

        halo = jnp.where(i > 0, h_ref[...], 0.0)
        xp = jnp.concatenate([halo, x_ref[...]], axis=0)
        w = w_ref[...]
        taps = [pltpu.roll(xp, 3 - k, 0)[8:] for k in range(3)] + [x_ref[...]]
        acc = b_ref[...] + w[3:4, :] * taps[3]
        for k in range(3):
            acc = acc + w[k:k + 1, :] * taps[k]
        s = _sigmoid(acc)
        dxc = d_ref[...] * (s * (1.0 + acc * (1.0 - s)))
        o_ref[...] = dxc
        for k in range(4):
            sums_ref[k:k + 1, :] += jnp.sum(dxc * taps[k], axis=0, keepdims=True)
        sums_ref[4:5, :] += jnp.sum(dxc, axis=0, keepdims=True)

    return _call(
        body, [dxa, proj, proj, conv_w, conv_b], name="conv_bwd_a", grid=(XBC // CONV_TC, T // tr),
        in_specs=[pl.BlockSpec((tr, CONV_TC), lambda j, i: (i, j)),
                  pl.BlockSpec((tr, CONV_TC), lambda j, i: (i, j + offb)),
                  pl.BlockSpec((8, CONV_TC), lambda j, i: (jnp.maximum(i * nb - 1, 0), j + offb)),
                  pl.BlockSpec((4, CONV_TC), lambda j, i: (0, j)),
                  pl.BlockSpec((1, CONV_TC), lambda j, i: (0, j))],
        out_specs=[pl.BlockSpec((tr, CONV_TC), lambda j, i: (i, j)), pl.BlockSpec((8, CONV_TC), lambda j, i: (0, j))],
        out_shape=[jax.ShapeDtypeStruct((T, XBC), F32), jax.ShapeDtypeStruct((8, XBC), F32)],
        sem=("parallel", "arbitrary"))


def _conv_bwd_b(dxc, conv_w, dproj):
    T = dxc.shape[0]
    tr = _row_tile(T)
    nb, offb, last = tr // 8, OFF_XBC // CONV_TC, T // tr - 1

    def body(d_ref, h_ref, w_ref, dp_in, o_ref):
        del dp_in
        halo = jnp.where(pl.program_id(0) < last, h_ref[...], 0.0)
        xp = jnp.concatenate([d_ref[...], halo], axis=0)
        n = xp.shape[0]
        w = w_ref[...]
        acc = w[3:4, :] * xp
        for k in range(3):
            acc = acc + w[k:k + 1, :] * pltpu.roll(xp, n - (3 - k), 0)
        o_ref[...] = acc[:tr].astype(BF16)

    return _call(
        body, [dxc, dxc, conv_w, dproj], name="conv_bwd_b", grid=(T // tr, XBC // CONV_TC),
        in_specs=[pl.BlockSpec((tr, CONV_TC), lambda i, j: (i, j)),
                  pl.BlockSpec((8, CONV_TC), lambda i, j: (jnp.minimum((i + 1) * nb, T // 8 - 1), j)),
                  pl.BlockSpec((4, CONV_TC), lambda i, j: (0, j)),
                  pl.BlockSpec(memory_space=pl.ANY)],
        out_specs=[pl.BlockSpec((tr, CONV_TC), lambda i, j: (i, j + offb))],
        out_shape=[jax.ShapeDtypeStruct(dproj.shape, BF16)], aliases={3: 0}, sem=("parallel", "parallel"))[0]


def _spread(v, sel, pieces):
    out = None
    for _ in range(pieces):
        p = v.astype(BF16)
        term = _nn(p, sel)
        out = term if out is None else out + term
        v = v - p.astype(F32)
    return out


def _ssd_selectors():
    g = np.arange(GROUPS)[:, None, None]
    piece = np.arange(128)[None, :, None]
    h = np.where(piece < 3 * HEADS, piece % HEADS, -1)
    blocks = (h == 8 * g + np.arange(1024)[None, None, :] // 128)
    pairs = (h == 8 * g + np.arange(512)[None, None, :] // HEAD_DIM)
    lane = np.arange(128)[None, None, :]
    block_sum = (lane == 8 * g + np.arange(1024)[None, :, None] // 128)
    pair_sum = (lane == 8 * g + np.arange(512)[None, :, None] // HEAD_DIM)
    return [jnp.asarray(m, BF16) for m in (blocks, pairs, block_sum, pair_sum)]


def _pack3(v):
    p0 = v.astype(BF16)
    r1 = v - p0.astype(F32)
    p1 = r1.astype(BF16)
    r2 = r1 - p1.astype(F32)
    return p0 + pltpu.roll(r1, HEADS, 1).astype(BF16) + pltpu.roll(r2, 2 * HEADS, 1).astype(BF16)


def _ssd_group(g, cs_p, csT, dt_p, s_mat, causal_w, lo, blocks_ref, pairs_ref):
    csb = _nn(cs_p, blocks_ref[g])
    row = jnp.concatenate([csT[8 * g + hh:8 * g + hh + 1, :] for hh in range(8)], axis=1)
    l_w = jnp.exp(jnp.where(causal_w, csb - row, NEG))
    m_w = jnp.concatenate([s_mat] * 8, axis=1) * l_w
    cs_g = jnp.concatenate([jnp.where(lo, csb[:, 256 * jj:256 * jj + 128], csb[:, 256 * jj + 128:256 * jj + 256])
                            for jj in range(4)], axis=1)
    cs_last = cs_g[Q - 1:Q, :]
    return m_w, l_w, _nn(dt_p, pairs_ref[g]), jnp.exp(cs_g), jnp.exp(cs_last - cs_g), jnp.exp(cs_last)


def _ssd_common(dtp_ref, dtb_r, alog_r, dtb_c, alog_c):
    rows = lax.broadcasted_iota(jnp.int32, (Q, Q), 0)
    cols = lax.broadcasted_iota(jnp.int32, (Q, Q), 1)
    tri = (cols <= rows).astype(F32)
    heads = lax.broadcasted_iota(jnp.int32, (1, 128), 1) < HEADS
    raw_w = dtp_ref[...] + dtb_r[...]
    dt_w = jnp.where(heads, _softplus(raw_w), 0.0)
    a_w = -jnp.exp(alog_r[...])
    cs_w = _nn(tri, dt_w * a_w, precision=HIGH)
    aT = _softplus(dtp_ref[...].T[0:HEADS, :] + dtb_c[...]) * (-jnp.exp(alog_c[...]))
    csT = _nt(aT, tri, precision=HIGH)
    return raw_w[:, 0:HEADS], dt_w[:, 0:HEADS], a_w[:, 0:HEADS], csT, _pack3(cs_w), _pack3(dt_w)


def _ssd_fwd(xbc_a, proj, dtb_r, alog_r, dtb_c, alog_c, dsk_exp):
    T = xbc_a.shape[0]
    nc = T // Q
    dtb_r, alog_r = [jnp.pad(a, ((0, 0), (0, 128 - HEADS))) for a in (dtb_r, alog_r)]

    def body(xbc_ref, dtp_ref, dtb_r_ref, alog_r_ref, dtb_c_ref, alog_c_ref, dsk_ref, blocks_ref, pairs_ref,
             y_ref, hin_ref, h_scr):
        @pl.when(pl.program_id(0) == 0)
        def _():
            h_scr[...] = jnp.zeros_like(h_scr)

        _, _, _, csT, cs_p, dt_p = _ssd_common(dtp_ref, dtb_r_ref, alog_r_ref, dtb_c_ref, alog_c_ref)
        lo = lax.broadcasted_iota(jnp.int32, (1, 128), 1) < HEAD_DIM
        hi = jnp.logical_not(lo)
        causal_w = (lax.broadcasted_iota(jnp.int32, (Q, 1024), 1) & (Q - 1)) <= lax.broadcasted_iota(jnp.int32, (Q, 1024), 0)
        for g in range(GROUPS):
            gs = slice(512 * g, 512 * (g + 1))
            hs = slice(128 * g, 128 * (g + 1))
            xs_g = xbc_ref[:, gs]
            b_g = xbc_ref[:, DI + STATE * g:DI + STATE * (g + 1)].astype(BF16)
            c_g = xbc_ref[:, DI + 512 + STATE * g:DI + 512 + STATE * (g + 1)].astype(BF16)
            m_w, _, dt_g, ecs_g, dec_g, cd_g = _ssd_group(g, cs_p, csT, dt_p, _nt(c_g, b_g), causal_w, lo, blocks_ref, pairs_ref)
            m_b = m_w.astype(BF16)
            xdt = xs_g * dt_g
            xdt_b = xdt.astype(BF16)
            ys = []
            for jj in range(4):
                xp = xdt_b[:, 128 * jj:128 * (jj + 1)]
                x_ab = jnp.concatenate([jnp.where(lo, xp, jnp.zeros_like(xp)), jnp.where(hi, xp, jnp.zeros_like(xp))], axis=0)
                ys.append(_nn(m_b[:, 256 * jj:256 * (jj + 1)], x_ab))
            h_g = h_scr[hs, :]
            hin_ref[0, hs, :] = h_g
            y_ref[:, gs] = jnp.concatenate(ys, axis=1) + _nn(c_g, h_g.astype(BF16)) * ecs_g + dsk_ref[:, gs] * xs_g
            h_scr[hs, :] = h_g * cd_g + _tn(b_g, (xdt * dec_g).astype(BF16))

    small_r = pl.BlockSpec((1, 128), lambda c: (0, 0))
    small_c = pl.BlockSpec((HEADS, 1), lambda c: (0, 0))
    blocks, pairs, _, _ = _ssd_selectors()
    whole = lambda a: pl.BlockSpec(a.shape, lambda c: (0,) * a.ndim)
    return _call(
        body, [xbc_a, proj, dtb_r, alog_r, dtb_c, alog_c, dsk_exp, blocks, pairs], name="ssd_fwd", grid=(nc,),
        in_specs=[pl.BlockSpec((Q, XBC), lambda c: (c, 0)),
                  pl.BlockSpec((Q, 128), lambda c: (c, OFF_DT // 128)),
                  small_r, small_r, small_c, small_c,
                  pl.BlockSpec((1, DI), lambda c: (0, 0)), whole(blocks), whole(pairs)],
        out_specs=[pl.BlockSpec((Q, DI), lambda c: (c, 0)), pl.BlockSpec((1, 512, 512), lambda c: (c, 0, 0))],
        out_shape=[jax.ShapeDtypeStruct((T, DI), F32), jax.ShapeDtypeStruct((nc, 512, 512), F32)],
        scratch_shapes=[pltpu.VMEM((512, 512), F32)], sem=("arbitrary",))


def _ssd_bwd(dy, xbc_a, proj, hin, dtb_r, alog_r, dtb_c, alog_c, dsk_exp, dproj, carry=None):
    T = xbc_a.shape[0]
    nc = T // Q
    dtb_r, alog_r = [jnp.pad(a, ((0, 0), (0, 128 - HEADS))) for a in (dtb_r, alog_r)]

    def body(dy_ref, xbc_ref, dtp_ref, hin_ref, dtb_r_ref, alog_r_ref, dtb_c_ref, alog_c_ref, dsk_ref, dp_in,
             blocks_ref, pairs_ref, block_sum_ref, pair_sum_ref, dxa_ref, dp_ref, dsk_sum_ref, small_ref, dh_scr):
        del dp_in

        @pl.when(pl.program_id(0) == 0)
        def _():
            dh_scr[...] = jnp.zeros_like(dh_scr)
            dsk_sum_ref[...] = jnp.zeros_like(dsk_sum_ref)
            small_ref[...] = jnp.zeros_like(small_ref)

        raw, dt, a_r, csT, cs_p, dt_p = _ssd_common(dtp_ref, dtb_r_ref, alog_r_ref, dtb_c_ref, alog_c_ref)
        lo = lax.broadcasted_iota(jnp.int32, (1, 128), 1) < HEAD_DIM
        hi = jnp.logical_not(lo)
        sub32 = lax.broadcasted_iota(jnp.int32, (HEADS, 1), 0)
        causal_w = (lax.broadcasted_iota(jnp.int32, (Q, 1024), 1) & (Q - 1)) <= lax.broadcasted_iota(jnp.int32, (Q, 1024), 0)
        dcs_c = jnp.zeros((Q, 128), F32)
        dcs_r = jnp.zeros((HEADS, Q), F32)
        dcs_l = jnp.zeros((8, 128), F32)
        ddt_x = jnp.zeros((Q, 128), F32)
        for g in range(GROUPS):
            gs = slice(512 * g, 512 * (g + 1))
            hs = slice(128 * g, 128 * (g + 1))
            xs_g, dy_g = xbc_ref[:, gs], dy_ref[:, gs]
            b_g = xbc_ref[:, DI + STATE * g:DI + STATE * (g + 1)].astype(BF16)
            c_g = xbc_ref[:, DI + 512 + STATE * g:DI + 512 + STATE * (g + 1)].astype(BF16)
            m_w, l_w, dt_g, ecs_g, dec_g, cd_g = _ssd_group(g, cs_p, csT, dt_p, _nt(c_g, b_g), causal_w, lo, blocks_ref, pairs_ref)
            m_b = m_w.astype(BF16)
            xdt = xs_g * dt_g
            xdt_b, dy_b = xdt.astype(BF16), dy_g.astype(BF16)
            dms, dxs = [], []
            for jj in range(4):
                xp, dyp = xdt_b[:, 128 * jj:128 * (jj + 1)], dy_b[:, 128 * jj:128 * (jj + 1)]
                dy_ab = jnp.concatenate([jnp.where(lo, dyp, jnp.zeros_like(dyp)), jnp.where(hi, dyp, jnp.zeros_like(dyp))], axis=0)
                dm_ab = _nt(dy_ab, xp)
                dms += [dm_ab[:Q], dm_ab[Q:]]
                dx_ab = _tn(m_b[:, 256 * jj:256 * (jj + 1)], dyp)
                dxs.append(jnp.where(lo, dx_ab[:Q], dx_ab[Q:]))
            dm_w = jnp.concatenate(dms, axis=1)
            w_w = dm_w * m_w
            dcs_c = dcs_c + _spread(w_w, block_sum_ref[g], 2)
            w_cols = jnp.sum(w_w, axis=0, keepdims=True)
            for hh in range(8):
                dcs_r = dcs_r + jnp.where(sub32 == 8 * g + hh, w_cols[:, 128 * hh:128 * (hh + 1)], 0.0)
            dl_w = dm_w * l_w
            ds_mat = dl_w[:, 0:128]
            for hh in range(1, 8):
                ds_mat = ds_mat + dl_w[:, 128 * hh:128 * (hh + 1)]
            hin_g = hin_ref[0, hs, :]
            hin_b = hin_g.astype(BF16)
            dh_g = dh_scr[hs, :]
            dh_b = dh_g.astype(BF16)
            g_mat = _nn(b_g, dh_b)
            xdec = xdt * dec_g
            xg = xdec * g_mat
            dxdt = jnp.concatenate(dxs, axis=1) + dec_g * g_mat
            sums = _spread(jnp.concatenate([dy_g * (_nn(c_g, hin_b) * ecs_g) - xg, dxdt * xs_g], axis=0), pair_sum_ref[g], 2)
            dcs_c = dcs_c + sums[:Q]
            ddt_x = ddt_x + sums[Q:]
            last = jnp.sum(xg, axis=0, keepdims=True) + jnp.sum(dh_g * hin_g, axis=0, keepdims=True) * cd_g
            dcs_l = dcs_l + _spread(jnp.broadcast_to(last, (8, 512)), pair_sum_ref[g], 2)
            dz = (dy_g * ecs_g).astype(BF16)
            ds_b = ds_mat.astype(BF16)
            dxa_ref[:, gs] = dxdt * dt_g + dy_g * dsk_ref[:, gs]
            dxa_ref[:, DI + STATE * g:DI + STATE * (g + 1)] = _nt(xdec.astype(BF16), dh_b) + _tn(ds_b, c_g)
            dxa_ref[:, DI + 512 + STATE * g:DI + 512 + STATE * (g + 1)] = _nt(dz, hin_b) + _nn(ds_b, b_g)
            dh_scr[hs, :] = _tn(c_g, dz) + dh_g * cd_g
            dsk_sum_ref[0:1, gs] += jnp.sum(dy_g * xs_g, axis=0, keepdims=True)

        rows = lax.broadcasted_iota(jnp.int32, (Q, Q), 0)
        cols = lax.broadcasted_iota(jnp.int32, (Q, Q), 1)
        tri_t = (cols >= rows).astype(F32)
        last_row = lax.broadcasted_iota(jnp.int32, (Q, 1), 0) == Q - 1
        dcs = (dcs_c + jnp.where(last_row, dcs_l[0:1, :], 0.0))[:, 0:HEADS]
        da = _nn(tri_t, dcs, precision=HIGH) - _nt(tri_t, dcs_r, precision=HIGH)
        ddt_raw = (ddt_x[:, 0:HEADS] + da * a_r) * _sigmoid(raw)
        small_ref[0:1, :] += jnp.sum(da * dt, axis=0, keepdims=True) * a_r
        small_ref[1:2, :] += jnp.sum(ddt_raw, axis=0, keepdims=True)
        dp_ref[...] = jnp.zeros_like(dp_ref)
        dp_ref[:, 0:HEADS] = ddt_raw.astype(BF16)

    rev = lambda c: nc - 1 - c
    small_r = pl.BlockSpec((1, 128), lambda c: (0, 0))
    small_c = pl.BlockSpec((HEADS, 1), lambda c: (0, 0))
    selectors = _ssd_selectors()
    whole = lambda a: pl.BlockSpec(a.shape, lambda c: (0,) * a.ndim)
    return _call(
        body, [dy, xbc_a, proj, hin, dtb_r, alog_r, dtb_c, alog_c, dsk_exp, dproj, *selectors], name="ssd_bwd", grid=(nc,),
        in_specs=[pl.BlockSpec((Q, DI), lambda c: (rev(c), 0)),
                  pl.BlockSpec((Q, XBC), lambda c: (rev(c), 0)),
                  pl.BlockSpec((Q, 128), lambda c: (rev(c), OFF_DT // 128)),
                  pl.BlockSpec((1, 512, 512), lambda c: (rev(c), 0, 0)),
                  small_r, small_r, small_c, small_c,
                  pl.BlockSpec((1, DI), lambda c: (0, 0)),
                  pl.BlockSpec(memory_space=pl.ANY)] + [whole(a) for a in selectors],
        out_specs=[pl.BlockSpec((Q, XBC), lambda c: (rev(c), 0)),
                   pl.BlockSpec((Q, 256), lambda c: (rev(c), OFF_DT // 256)),
                   pl.BlockSpec((8, DI), lambda c: (0, 0)),
                   pl.BlockSpec((8, HEADS), lambda c: (0, 0))],
        out_shape=[jax.ShapeDtypeStruct((T, XBC), F32), jax.ShapeDtypeStruct(dproj.shape, BF16),
                   jax.ShapeDtypeStruct((8, DI), F32), jax.ShapeDtypeStruct((8, HEADS), F32)],
        aliases={9: 1}, scratch_shapes=[pltpu.VMEM((512, 512), F32)], sem=("arbitrary",), carry=carry)


def _gate_norm(y, proj, w):
    T = y.shape[0]
    tr = _row_tile(T)

    def body(y_ref, z_ref, w_ref, o_ref):
        for g in range(GROUPS):
            gs = slice(512 * g, 512 * (g + 1))
            z = z_ref[:, gs]
            yg = y_ref[:, gs] * (z * _sigmoid(z))
            r = lax.rsqrt(jnp.mean(yg * yg, axis=-1, keepdims=True) + EPS)
            o_ref[:, gs] = (yg * r * w_ref[:, gs]).astype(BF16)

    tile = pl.BlockSpec((tr, DI), lambda i: (i, 0))
    return _call(body, [y, proj, w], name="gate_norm", grid=(T // tr,),
                 in_specs=[tile, tile, pl.BlockSpec((1, DI), lambda i: (0, 0))], out_specs=[tile],
                 out_shape=[jax.ShapeDtypeStruct((T, DI), BF16)], sem=("parallel",))[0]


def _gate_norm_bwd(dyn, y, proj, w, dproj):
    T = y.shape[0]
    tr = _row_tile(T)

    def body(d_ref, y_ref, z_ref, w_ref, dp_in, dy_ref, dz_ref, sums_ref):
        del dp_in

        @pl.when(pl.program_id(0) == 0)
        def _():
            sums_ref[...] = jnp.zeros_like(sums_ref)

        for g in range(GROUPS):
            gs = slice(512 * g, 512 * (g + 1))
            z, yv, d = z_ref[:, gs], y_ref[:, gs], d_ref[:, gs]
            s = _sigmoid(z)
            silu = z * s
            yg = yv * silu
            r = lax.rsqrt(jnp.mean(yg * yg, axis=-1, keepdims=True) + EPS)
            yn = yg * r
            sums_ref[0:1, gs] += jnp.sum(d * yn, axis=0, keepdims=True)
            dn = d * w_ref[:, gs]
            dyg = r * (dn - yn * jnp.mean(dn * yn, axis=-1, keepdims=True))
            dy_ref[:, gs] = dyg * silu
            dz_ref[:, gs] = (dyg * yv * (s * (1.0 + z * (1.0 - s)))).astype(BF16)

    tile = pl.BlockSpec((tr, DI), lambda i: (i, 0))
    return _call(
        body, [dyn, y, proj, w, dproj], name="gate_norm_bwd", grid=(T // tr,),
        in_specs=[tile, tile, tile, pl.BlockSpec((1, DI), lambda i: (0, 0)), pl.BlockSpec(memory_space=pl.ANY)],
        out_specs=[tile, tile, pl.BlockSpec((8, DI), lambda i: (0, 0))],
        out_shape=[jax.ShapeDtypeStruct((T, DI), F32), jax.ShapeDtypeStruct(dproj.shape, BF16),
                   jax.ShapeDtypeStruct((8, DI), F32)],
        aliases={4: 1}, sem=("arbitrary",))


def _pool_fwd(proj, pool_w_b, pool_scale):
    T = proj.shape[0]
    tr = _row_tile(T)
    nb = tr // 16

    def body(u_ref, h_ref, pw_ref, ps_ref, pooled_ref, pw_out_ref, yps_ref):
        i = pl.program_id(0)
        t = i * tr + lax.broadcasted_iota(jnp.int32, (tr, 1), 0)
        for g, win in enumerate(POOL_WINDOWS):
            gs = slice(GW * g, GW * (g + 1))
            u = u_ref[:, gs]
            s = jnp.concatenate([jnp.where(i > 0, h_ref[:, gs], 0.0), u], axis=0)
            sh = 1
            while sh < win:
                s = s + pltpu.roll(s, sh, 0)
                sh *= 2
            pooled = (s[16:] * (1.0 / jnp.minimum(t + 1, win).astype(F32)) - u).astype(BF16)
            pooled_ref[:, gs] = pooled
            pwv = _nn(pooled, pw_ref[g])
            pw_out_ref[:, gs] = pwv
            yps_ref[:, gs] = (pwv * ps_ref[:, gs]).astype(BF16)

    tile = pl.BlockSpec((tr, D), lambda i: (i, 0))
    return _call(
        body, [proj, proj, pool_w_b, pool_scale], name="pool_fwd", grid=(T // tr,),
        in_specs=[pl.BlockSpec((tr, D), lambda i: (i, OFF_POOL // D)),
                  pl.BlockSpec((16, D), lambda i: (jnp.maximum(i * nb - 1, 0), OFF_POOL // D)),
                  pl.BlockSpec((4, GW, GW), lambda i: (0, 0, 0)),
                  pl.BlockSpec((1, D), lambda i: (0, 0))],
        out_specs=[tile, tile, tile],
        out_shape=[jax.ShapeDtypeStruct((T, D), BF16), jax.ShapeDtypeStruct((T, D), F32),
                   jax.ShapeDtypeStruct((T, D), BF16)], sem=("parallel",))


def _pool_bwd(dyp, pw_out, pooled, pool_w_b, pool_scale, dproj):
    T = dyp.shape[0]
    tr = _row_tile(T)
    nb, last = tr // 16, T // tr - 1

    def body(d_ref, h_ref, pwo_ref, pooled_ref, pw_ref, ps_ref, dp_in, du_ref, gpw_ref, sums_ref):
        del dp_in
        i = pl.program_id(0)

        @pl.when(i == 0)
        def _():
            gpw_ref[...] = jnp.zeros_like(gpw_ref)
            sums_ref[...] = jnp.zeros_like(sums_ref)

        n = tr + 16
        t = i * tr + lax.broadcasted_iota(jnp.int32, (n, 1), 0)
        sums_ref[0:1, :] += jnp.sum(d_ref[...] * pwo_ref[...], axis=0, keepdims=True)
        for g, win in enumerate(POOL_WINDOWS):
            gs = slice(GW * g, GW * (g + 1))
            d_ext = jnp.concatenate([d_ref[:, gs], jnp.where(i < last, h_ref[:, gs], 0.0)], axis=0)
            dpw = (d_ext * ps_ref[:, gs]).astype(BF16)
            dpooled = _nt(dpw, pw_ref[g])
            s = jnp.where(t < T, dpooled * (1.0 / jnp.minimum(t + 1, win).astype(F32)), 0.0)
            sh = 1
            while sh < win:
                s = s + pltpu.roll(s, n - sh, 0)
                sh *= 2
            du_ref[:, gs] = (s[:tr] - dpooled[:tr]).astype(BF16)
            gpw_ref[g] += _tn(pooled_ref[:, gs], dpw[:tr])

    tile = pl.BlockSpec((tr, D), lambda i: (i, 0))
    return _call(
        body, [dyp, dyp, pw_out, pooled, pool_w_b, pool_scale, dproj], name="pool_bwd", grid=(T // tr,),
        in_specs=[tile, pl.BlockSpec((16, D), lambda i: (jnp.minimum((i + 1) * nb, T // 16 - 1), 0)), tile, tile,
                  pl.BlockSpec((4, GW, GW), lambda i: (0, 0, 0)), pl.BlockSpec((1, D), lambda i: (0, 0)),
                  pl.BlockSpec(memory_space=pl.ANY)],
        out_specs=[pl.BlockSpec((tr, D), lambda i: (i, OFF_POOL // D)),
                   pl.BlockSpec((4, GW, GW), lambda i: (0, 0, 0)), pl.BlockSpec((8, D), lambda i: (0, 0))],
        out_shape=[jax.ShapeDtypeStruct(dproj.shape, BF16), jax.ShapeDtypeStruct((4, GW, GW), F32),
                   jax.ShapeDtypeStruct((8, D), F32)],
        aliases={6: 0}, sem=("arbitrary",))


def _merge(proj, y_ssd, y_pool):
    T = proj.shape[0]
    tr = _row_tile(T)

    def body(g_ref, a_ref, b_ref, o_ref):
        o_ref[...] = (_sigmoid(g_ref[:, 0:D]) * a_ref[...] + _sigmoid(g_ref[:, D:2 * D]) * b_ref[...]).astype(BF16)

    tile = pl.BlockSpec((tr, D), lambda i: (i, 0))
    return _call(body, [proj, y_ssd, y_pool], name="merge", grid=(T // tr,),
                 in_specs=[pl.BlockSpec((tr, 2 * D), lambda i: (i, OFF_GATE // (2 * D))), tile, tile], out_specs=[tile],
                 out_shape=[jax.ShapeDtypeStruct((T, D), BF16)], sem=("parallel",))[0]


def _merge_bwd(dmerged, proj, y_ssd, y_pool):
    T = proj.shape[0]
    tr = _row_tile(T)

    def body(d_ref, g_ref, a_ref, b_ref, da_ref, db_ref, dg_ref):
        d = d_ref[...]
        ga, gb = _sigmoid(g_ref[:, 0:D]), _sigmoid(g_ref[:, D:2 * D])
        da_ref[...] = (d * ga).astype(BF16)
        db_ref[...] = (d * gb).astype(BF16)
        dg_ref[:, 0:D] = (d * a_ref[...] * ga * (1.0 - ga)).astype(BF16)
        dg_ref[:, D:2 * D] = (d * b_ref[...] * gb * (1.0 - gb)).astype(BF16)

    tile = pl.BlockSpec((tr, D), lambda i: (i, 0))
    gates = pl.BlockSpec((tr, 2 * D), lambda i: (i, OFF_GATE // (2 * D)))
    return _call(body, [dmerged, proj, y_ssd, y_pool], name="merge_bwd", grid=(T // tr,),
                 in_specs=[tile, gates, tile, tile], out_specs=[tile, tile, gates],
                 out_shape=[jax.ShapeDtypeStruct((T, D), BF16), jax.ShapeDtypeStruct((T, D), BF16),
                            jax.ShapeDtypeStruct((T, NP), BF16)], sem=("parallel",))


def _adamw(w, g, m, v, name, carry=None):
    R, C = w.shape
    tr = R if R <= 128 else 128
    assert R % tr == 0

    def body(w_ref, g_ref, m_ref, v_ref, d_ref, mo_ref, vo_ref):
        gv = g_ref[...]
        mn = ADAM_B1 * m_ref[...] + (1.0 - ADAM_B1) * gv
        vn = ADAM_B2 * v_ref[...] + (1.0 - ADAM_B2) * (gv * gv)
        m_hat = mn * (1.0 / (1.0 - ADAM_B1 ** ADAM_STEP))
        v_hat = vn * (1.0 / (1.0 - ADAM_B2 ** ADAM_STEP))
        d_ref[...] = -ADAM_LR * (m_hat / (jnp.sqrt(v_hat) + ADAM_EPS) + ADAM_WD * w_ref[...])
        mo_ref[...] = mn
        vo_ref[...] = vn

    tile = pl.BlockSpec((tr, C), lambda i: (i, 0))
    sds = jax.ShapeDtypeStruct((R, C), F32)
    return _call(body, [w, g, m, v], name=name, grid=(R // tr,), in_specs=[tile] * 4, out_specs=[tile] * 3,
                 out_shape=[sds] * 3, sem=("parallel",), carry=carry)


def _me():
    return lax.axis_index("x"), lax.axis_index("y"), lax.axis_index("c")


def _xor_peer(x, y, c, p):
    return (x ^ ((p >> 2) & 1), y ^ ((p >> 1) & 1), c ^ (p & 1))


def _ada_fwd(c_row, w_ada, b_ada_mine, carry=None):
    n_cols = w_ada.shape[1]

    def body(c_ref, w_ref, b_ref, mod_ref, c8_ref, csend, mpart, modbuf, send_sems, recv_sems):
        x, y, c = _me()
        me = 4 * x + 2 * y + c
        chip = 2 * x + y
        csend[...] = jnp.broadcast_to(c_ref[...], csend.shape)
        c8_ref[me] = csend[...]

        def c_copy(p):
            return pltpu.make_async_remote_copy(
                src_ref=csend, dst_ref=c8_ref.at[me], send_sem=send_sems.at[p - 1], recv_sem=recv_sems.at[p - 1],
                device_id=_xor_peer(x, y, c, p), device_id_type=MESH)

        for p in range(1, 8):
            c_copy(p).start()
        for p in range(1, 8):
            c_copy(p).wait_recv()
        cs = jnp.concatenate([c8_ref[d][0:1, :] for d in range(8)], axis=0)
        mpart[...] = _nn(cs * _sigmoid(cs), w_ref[...], precision=HIGH) + b_ref[...]
        modbuf[chip] = mpart[...]

        def m_copy(m):
            return pltpu.make_async_remote_copy(
                src_ref=mpart, dst_ref=modbuf.at[chip], send_sem=send_sems.at[6 + m], recv_sem=recv_sems.at[6 + m],
                device_id=_xor_peer(x, y, c, 2 * m), device_id_type=MESH)

        for m in range(1, 4):
            m_copy(m).start()
        for m in range(1, 4):
            m_copy(m).wait_recv()
        mine = lax.broadcasted_iota(jnp.int32, (8, 1), 0) == me
        for k in range(N_CHIPS):
            mod_ref[:, n_cols * k:n_cols * (k + 1)] = jnp.sum(jnp.where(mine, modbuf[k], 0.0), axis=0, keepdims=True)
        for p in range(1, 8):
            c_copy(p).wait_send()
        for m in range(1, 4):
            m_copy(m).wait_send()

    vmem = pl.BlockSpec(memory_space=pltpu.VMEM)
    return _call(
        body, [c_row, w_ada, b_ada_mine], name="ada_fwd", in_specs=[vmem, vmem, vmem], out_specs=[vmem, vmem],
        out_shape=[jax.ShapeDtypeStruct((1, N_CHIPS * n_cols), F32), jax.ShapeDtypeStruct((8, 8, D), F32)],
        scratch_shapes=[pltpu.VMEM((8, D), F32), pltpu.VMEM((8, n_cols), F32), pltpu.VMEM((N_CHIPS, 8, n_cols), F32),
                        pltpu.SemaphoreType.DMA((10,)), pltpu.SemaphoreType.DMA((10,))], carry=carry)


def _gather_small(vec, carry=None):
    rows = vec.shape[0]

    def body(v_ref, all_ref, tot_ref, dsk_ref, send_sems, recv_sems):
        x, y, c = _me()
        me = 4 * x + 2 * y + c
        all_ref[me] = v_ref[...]

        def copy(p):
            return pltpu.make_async_remote_copy(
                src_ref=v_ref, dst_ref=all_ref.at[me], send_sem=send_sems.at[p - 1], recv_sem=recv_sems.at[p - 1],
                device_id=_xor_peer(x, y, c, p), device_id_type=MESH)

        for p in range(1, 8):
            copy(p).start()
        for p in range(1, 8):
            copy(p).wait_recv()
        tot = all_ref[0]
        for d in range(1, 8):
            tot = tot + all_ref[d]
        tot_ref[...] = tot
        seg = tot[SMALL_OFF["d_skip"] // 128:SMALL_OFF["d_skip"] // 128 + 16, :]
        lane = lax.broadcasted_iota(jnp.int32, (1, 128), 1)
        sa = jnp.sum(jnp.where(lane < HEAD_DIM, seg, 0.0), axis=1, keepdims=True)
        sb = jnp.sum(jnp.where(lane < HEAD_DIM, 0.0, seg), axis=1, keepdims=True)
        dsk_ref[...] = jnp.where(lane == 0, sa, jnp.where(lane == 1, sb, 0.0))
        for p in range(1, 8):
            copy(p).wait_send()

    vmem = pl.BlockSpec(memory_space=pltpu.VMEM)
    return _call(
        body, [vec], name="gather_small", in_specs=[vmem], out_specs=[vmem, vmem, vmem],
        out_shape=[jax.ShapeDtypeStruct((8, rows, 128), F32), jax.ShapeDtypeStruct((rows, 128), F32),
                   jax.ShapeDtypeStruct((16, 128), F32)],
        scratch_shapes=[pltpu.SemaphoreType.DMA((7,)), pltpu.SemaphoreType.DMA((7,))], carry=carry)


def _gather_carry(shards):
    n = len(shards)

    def copies(ins, outs, sems):
        x, y, c = _me()
        chip = 2 * x + y

        def half(w, which):
            h = shards[w].shape[0] // 2
            return pl.ds(which * h, h)

        def first(w, m):
            return pltpu.make_async_remote_copy(
                src_ref=ins[w].at[half(w, c)], dst_ref=outs[w].at[chip, half(w, c)],
                send_sem=sems.send(6 * w + m - 1), recv_sem=sems.recv(6 * w + m - 1),
                device_id=_xor_peer(x, y, c, 2 * m), device_id_type=MESH)

        def landed(w, m):
            return pltpu.make_async_remote_copy(
                src_ref=ins[w].at[half(w, c)], dst_ref=outs[w].at[chip ^ m, half(w, c)],
                send_sem=sems.send(6 * w + m - 1), recv_sem=sems.recv(6 * w + m - 1),
                device_id=_xor_peer(x, y, c, 2 * m), device_id_type=MESH)

        def passed(w, m, which):
            part = outs[w].at[chip ^ m, half(w, which)]
            return pltpu.make_async_remote_copy(
                src_ref=part, dst_ref=part, send_sem=sems.send(6 * w + 2 + m), recv_sem=sems.recv(6 * w + 2 + m),
                device_id=(x, y, 1 - c), device_id_type=MESH)

        return c, first, landed, passed

    pairs = [(w, m) for w in range(n) for m in range(1, 4)]

    def start(ins, outs, sems):
        _, first, _, _ = copies(ins, outs, sems)
        for w, m in pairs:
            first(w, m).start()

    def finish(ins, outs, sems):
        c, first, landed, passed = copies(ins, outs, sems)
        for w, m in pairs:
            landed(w, m).wait_recv()
            passed(w, m, c).start()
        for w, m in pairs:
            passed(w, m, 1 - c).wait_recv()
        for w, m in pairs:
            first(w, m).wait_send()
            passed(w, m, c).wait_send()

    return _Carry(shards, [jax.ShapeDtypeStruct((N_CHIPS,) + s.shape, s.dtype) for s in shards], 6 * n, start, finish)


def _pair_exchange_carry(grads):
    n = len(grads)

    def copy(ins, outs, sems, w):
        x, y, c = _me()
        h = grads[w].shape[1] // 2
        return pltpu.make_async_remote_copy(
            src_ref=ins[w].at[:, pl.ds((1 - c) * h, h)], dst_ref=outs[w],
            send_sem=sems.send(w), recv_sem=sems.recv(w), device_id=(x, y, 1 - c), device_id_type=MESH)

    def start(ins, outs, sems):
        for w in range(n):
            copy(ins, outs, sems, w).start()

    def finish(ins, outs, sems):
        for w in range(n):
            copy(ins, outs, sems, w).wait()

    return _Carry(grads, [jax.ShapeDtypeStruct((N_CHIPS, g.shape[1] // 2, g.shape[2]), g.dtype) for g in grads], n,
                  start, finish)


def _chip_exchange_carry(partials):
    n = len(partials)

    def copier(ins, outs, sems):
        x, y, c = _me()
        chip = 2 * x + y

        def copy(w, m, landed):
            return pltpu.make_async_remote_copy(
                src_ref=ins[w].at[chip ^ m], dst_ref=outs[w].at[(chip ^ m) if landed else chip],
                send_sem=sems.send(3 * w + m - 1), recv_sem=sems.recv(3 * w + m - 1),
                device_id=_xor_peer(x, y, c, 2 * m), device_id_type=MESH)

        return copy

    pairs = [(w, m) for w in range(n) for m in range(1, 4)]

    def start(ins, outs, sems):
        copy = copier(ins, outs, sems)
        for w, m in pairs:
            copy(w, m, False).start()

    def finish(ins, outs, sems):
        copy = copier(ins, outs, sems)
        for w, m in pairs:
            copy(w, m, True).wait_recv()
        for w, m in pairs:
            copy(w, m, False).wait_send()

    return _Carry(partials, [jax.ShapeDtypeStruct(p.shape, p.dtype) for p in partials], 3 * n, start, finish)


def _pair_share_carry(shards):
    n = len(shards)

    def copier(ins, outs, sems):
        x, y, c = _me()

        def copy(w, which):
            h = shards[w].shape[0] // 2
            rows = pl.ds(which * h, h)
            return pltpu.make_async_remote_copy(
                src_ref=ins[w].at[rows], dst_ref=outs[w].at[rows],
                send_sem=sems.send(w), recv_sem=sems.recv(w), device_id=(x, y, 1 - c), device_id_type=MESH)

        return c, copy

    def start(ins, outs, sems):
        c, copy = copier(ins, outs, sems)
        for w in range(n):
            copy(w, c).start()

    def finish(ins, outs, sems):
        c, copy = copier(ins, outs, sems)
        for w in range(n):
            copy(w, 1 - c).wait_recv()
        for w in range(n):
            copy(w, c).wait_send()

    return _Carry(shards, [jax.ShapeDtypeStruct(s.shape, s.dtype) for s in shards], n, start, finish,
                  aliased=[(w, w) for w in range(n)])


def _pair_sum(g, part, idx, name):
    _, h, C = part.shape
    tr = min(512, h)
    nb = h // tr

    def body(idx_ref, g_ref, p_ref, o16_ref, own_ref):
        v = g_ref[...].astype(F32) + p_ref[...].astype(F32)
        o16_ref[...] = v.astype(BF16)

        @pl.when(pl.program_id(1) == idx_ref[1])
        def _():
            own_ref[...] = v

    return pl.pallas_call(
        body, name=name,
        grid_spec=pltpu.PrefetchScalarGridSpec(
            num_scalar_prefetch=1, grid=(nb, N_CHIPS),
            in_specs=[pl.BlockSpec((None, tr, C), lambda i, s, idx_ref: (s, idx_ref[0] * nb + i, 0)),
                      pl.BlockSpec((None, tr, C), lambda i, s, idx_ref: (s, i, 0))],
            out_specs=[pl.BlockSpec((None, tr, C), lambda i, s, idx_ref: (s, i, 0)),
                       pl.BlockSpec((tr, C), lambda i, s, idx_ref: (i, 0))]),
        out_shape=[jax.ShapeDtypeStruct(part.shape, BF16), jax.ShapeDtypeStruct((h, C), F32)],
        compiler_params=pltpu.CompilerParams(dimension_semantics=("arbitrary", "arbitrary"), vmem_limit_bytes=VMEM_LIMIT),
    )(idx, g, part)


def _chip_sum(own, slots, idx, name):
    h, C = own.shape
    tr = min(512, h)
    nb = h // tr

    def body(idx_ref, own_ref, s1_ref, s2_ref, s3_ref, o_ref):
        del idx_ref
        o_ref[...] = ((own_ref[...] + s1_ref[...].astype(F32)) + s2_ref[...].astype(F32)) + s3_ref[...].astype(F32)

    def slot(m):
        return pl.BlockSpec((None, tr, C), lambda i, idx_ref: (idx_ref[1] ^ m, i, 0))

    return pl.pallas_call(
        body, name=name,
        grid_spec=pltpu.PrefetchScalarGridSpec(
            num_scalar_prefetch=1, grid=(nb,),
            in_specs=[pl.BlockSpec((tr, C), lambda i, idx_ref: (i, 0)), slot(1), slot(2), slot(3)],
            out_specs=pl.BlockSpec((tr, C), lambda i, idx_ref: (idx_ref[0] * nb + i, 0))),
        out_shape=jax.ShapeDtypeStruct((2 * h, C), F32),
        compiler_params=pltpu.CompilerParams(dimension_semantics=("parallel",), vmem_limit_bytes=VMEM_LIMIT),
    )(idx, own, slots, slots, slots)


class _Reducer:
    def __init__(self, idx):
        self.idx, self.chips, self.p16, self.own, self.mine, self.final = idx, {}, {}, {}, {}, {}

    def add(self, name, whole, chip_blocks=False):
        self.chips[name] = whole if chip_blocks else _chips_from_whole(name, whole)

    def pair(self, names):
        return _pair_exchange_carry([self.chips[n] for n in names])

    def take_pair(self, names, outs):
        for n, part in zip(names, outs):
            self.p16[n], self.own[n] = _pair_sum(self.chips.pop(n), part, self.idx, "pair_sum_" + n)

    def chip(self, names):
        return _chip_exchange_carry([self.p16[n] for n in names])

    def take_chip(self, names, outs):
        for n, slots in zip(names, outs):
            del self.p16[n]
            self.mine[n] = _chip_sum(self.own.pop(n), slots, self.idx, "chip_sum_" + n)

    def share(self, names):
        return _pair_share_carry([self.mine[n] for n in names])

    def take_share(self, names, outs):
        for n, s in zip(names, outs):
            del self.mine[n]
            self.final[n] = s


def _w_ada_grad(c8, dmod_cols):
    n_cols = dmod_cols.shape[1]
    tn = 512

    def body(c_ref, d_ref, o_ref):
        cv = c_ref[...]
        o_ref[...] = _tn(cv * _sigmoid(cv), d_ref[...], precision=HIGH)

    return _call(body, [c8, dmod_cols], name="w_ada_grad", grid=(n_cols // tn,),
                 in_specs=[pl.BlockSpec((8, D), lambda j: (0, 0)), pl.BlockSpec((8, tn), lambda j: (0, j))],
                 out_specs=[pl.BlockSpec((D, tn), lambda j: (0, j))],
                 out_shape=[jax.ShapeDtypeStruct((D, n_cols), F32)], sem=("parallel",))[0]


_SMALL_SEGS = (("dmod", 6144), ("norm_mix_w", 1024), ("conv_b", 3072), ("ssd_norm_w", 2048), ("pool_scale", 1024),
               ("norm_mlp_w", 1024), ("norm_final_w", 1024), ("conv_w", 4 * XBC), ("d_skip", 2048), ("a_log", 128),
               ("dt_bias", 128), ("loss", 128))
SMALL_OFF = {}
_o = 0
for _n, _s in _SMALL_SEGS:
    SMALL_OFF[_n] = _o
    _o += _s
SMALL_LEN = -(-_o // 1024) * 1024

_FIRST = ("w_in", "conv_w")
_LATER = ("w_branch_ssd", "pool_w", "w_branch_pool", "w_out", "w_up", "w_down")
_SMALL_REPLICATED = ("b_ada", "norm_mix_w", "conv_b", "dt_bias", "a_log", "d_skip", "ssd_norm_w", "pool_scale",
                     "norm_mlp_w", "norm_final_w")
_WEIGHTS = ("w_ada", "b_ada", "norm_mix_w", "w_in", "conv_w", "conv_b", "dt_bias", "a_log", "d_skip", "ssd_norm_w",
            "w_branch_ssd", "pool_w", "pool_scale", "w_branch_pool", "w_out", "norm_mlp_w", "w_up", "w_down",
            "norm_final_w")


def _shard_2d(name, a):
    if name == "conv_w":
        return a.reshape(16, -1)
    return (a.reshape(GW, GW) if name == "pool_w" else a.reshape(a.shape[-2], a.shape[-1])).astype(BF16)


def _whole_from_chips(name, g, own, chip):
    g = lax.dynamic_update_slice(g, own[None], (chip, 0, 0))
    if name == "w_in":
        a, b = _DT_IN_CHIP2, _DT_IN_CHIP2 + HEADS
        pad = jnp.zeros((D, NP - IN_COLS), g.dtype)
        return jnp.concatenate([g[0], g[1], g[2][:, :a], g[2][:, b:], g[3], g[2][:, a:b], pad], axis=1)
    if name == "w_up":
        return jnp.concatenate([g[k] for k in range(N_CHIPS)], axis=1)
    if name == "pool_w":
        return jnp.transpose(g.reshape(N_CHIPS, 4, GW // N_CHIPS, GW), (1, 0, 2, 3)).reshape(4, GW, GW)
    if name == "conv_w":
        return jnp.transpose(g.reshape(N_CHIPS, 4, XBC // N_CHIPS), (1, 0, 2)).reshape(4, XBC)
    return g.reshape(N_CHIPS * g.shape[1], g.shape[2])


def _chips_from_whole(name, g):
    if name.startswith("w_in"):
        cw, a = IN_COLS // N_CHIPS, _DT_IN_CHIP2
        chip2 = jnp.concatenate([g[:, 2 * cw:2 * cw + a], g[:, OFF_DT:OFF_DT + HEADS], g[:, 5120:3 * cw - HEADS]], axis=1)
        return jnp.stack([g[:, :cw], g[:, cw:2 * cw], chip2, g[:, 3 * cw - HEADS:OFF_DT]])
    if name == "w_up":
        return jnp.transpose(g.reshape(D, N_CHIPS, DFF // N_CHIPS), (1, 0, 2))
    if name == "pool_w":
        return jnp.transpose(g.reshape(4, N_CHIPS, GW // N_CHIPS, GW), (1, 0, 2, 3)).reshape(N_CHIPS, GW, GW)
    return g.reshape(N_CHIPS, g.shape[0] // N_CHIPS, g.shape[1])


def kernel(x, c, w_ada, b_ada, norm_mix_w, w_in, conv_w, conv_b, dt_bias, a_log, d_skip, ssd_norm_w, w_branch_ssd, pool_w, pool_scale, w_branch_pool, w_out, norm_mlp_w, w_up, w_down, norm_final_w, loss_target, m_w_ada, m_b_ada, m_norm_mix_w, m_w_in, m_conv_w, m_conv_b, m_dt_bias, m_a_log, m_d_skip, m_ssd_norm_w, m_w_branch_ssd, m_pool_w, m_pool_scale, m_w_branch_pool, m_w_out, m_norm_mlp_w, m_w_up, m_w_down, m_norm_final_w, v_w_ada, v_b_ada, v_norm_mix_w, v_w_in, v_conv_w, v_conv_b, v_dt_bias, v_a_log, v_d_skip, v_ssd_norm_w, v_w_branch_ssd, v_pool_w, v_pool_scale, v_w_branch_pool, v_w_out, v_norm_mlp_w, v_w_up, v_w_down, v_norm_final_w):
    args = locals()
    w = {n: args[n] for n in _WEIGHTS}
    m = {n: args["m_" + n] for n in _WEIGHTS}
    v = {n: args["v_" + n] for n in _WEIGHTS}
    xi, yi, ci = _me()
    chip = 2 * xi + yi
    idx = jnp.stack([ci, chip]).astype(jnp.int32)
    ada_cols = w_ada.shape[-1]
    xs, target = x[0], loss_target[0]
    two_d = lambda n, a: a.reshape(GW, GW) if n == "pool_w" else a.reshape(-1, a.shape[-1])
    delta, new_m, new_v, g = {}, {}, {}, {}

    def adamw(n, carry=None):
        res = _adamw(two_d(n, w[n]), two_d(n, g[n]), two_d(n, m[n]), two_d(n, v[n]), "adamw_" + n, carry=carry)
        (delta[n], new_m[n], new_v[n]), extra = res if carry is not None else (res, None)
        return extra

    b_mine = lax.dynamic_slice(b_ada, (0, chip * ada_cols), (1, ada_cols))
    shards = {n: _shard_2d(n, w[n]) for n in _FIRST + _LATER}
    mod, c8 = _ada_fwd(c, w_ada[0], b_mine)
    c8 = c8[:, 0, :]
    shift_m, scale_m, gate_m, shift_f, scale_f, gate_f = [mod[:, D * i:D * (i + 1)] for i in range(6)]
    nf_w = norm_final_w.reshape(1, D)

    h1, first = _norm_mod(xs, norm_mix_w, scale_m, shift_m, "norm_mod_mix",
                          carry=_gather_carry([shards[n] for n in _FIRST]))
    p ={n: _whole_from_chips(n, a, shards[n], chip) for n, a in zip(_FIRST, first)}
    (proj,), later = _matmul(h1, p["w_in"], mode="nn", out_dtypes=[F32], name="mm_proj", cols_outer=True,
                             carry=_gather_carry([shards[n] for n in _LATER]))
    p.update({n: _whole_from_chips(n, a, shards[n], chip) for n, a in zip(_LATER, later)})
    xbc_a = _conv_fwd(proj, p["conv_w"], conv_b)
    dtb_c, alog_c = dt_bias.reshape(HEADS, 1), a_log.reshape(HEADS, 1)
    dsk_exp = jnp.repeat(d_skip, HEAD_DIM, axis=1)
    y, hin = _ssd_fwd(xbc_a, proj, dt_bias, a_log, dtb_c, alog_c, dsk_exp)
    yn = _gate_norm(y, proj, ssd_norm_w)
    (y_ssd,) = _matmul(yn, p["w_branch_ssd"], mode="nn", out_dtypes=[F32], name="mm_branch_ssd")
    pooled, pw_out, yps = _pool_fwd(proj, p["pool_w"], pool_scale)
    (y_pool,) = _matmul(yps, p["w_branch_pool"], mode="nn", out_dtypes=[F32], name="mm_branch_pool")
    merged = _merge(proj, y_ssd, y_pool)
    resid = lambda acc, r, gt: (r + gt * acc, acc)
    x2, mix = _matmul(merged, p["w_out"], mode="nn", out_dtypes=[F32, BF16], name="mm_out",
                      epi=resid, tile_extras=(xs,), row_extras=(gate_m,))
    h2 = _norm_mod(x2, norm_mlp_w, scale_f, shift_f, "norm_mod_mlp")
    relu2 = lambda acc: (jnp.square(jnp.maximum(acc, 0.0)),)
    (act,) = _matmul(h2, p["w_up"], mode="nn", out_dtypes=[BF16], name="mm_up", epi=relu2)
    x3, down = _matmul(act, p["w_down"], mode="nn", out_dtypes=[F32, BF16], name="mm_down",
                       epi=resid, tile_extras=(x2,), row_extras=(gate_f,))

    red = _Reducer(idx)
    dx3, d_down, sums_f = _final_loss_bwd(x3, target, nf_w, down, gate_f)
    drelu2 = lambda acc, a: (acc * (2.0 * jnp.sqrt(a)).astype(F32),)
    (dup,) = _matmul(d_down, p["w_down"], mode="nt", out_dtypes=[BF16], name="mm_dact",
                     epi=drelu2, tile_extras=(act,))
    red.add("w_down", _matmul(act, d_down, mode="tn", out_dtypes=[BF16], name="mm_g_down")[0])
    (dh2,), got = _matmul(dup, p["w_up"], mode="nt", out_dtypes=[F32], name="mm_dh2",
                          carry=red.pair(["w_down"]))
    red.take_pair(["w_down"], got)
    red.add("w_up", _matmul(h2, dup, mode="tn", out_dtypes=[BF16], name="mm_g_up", chip_blocks=True)[0], chip_blocks=True)
    dx2, sums_2, dmix = _norm_mod_bwd(x2, dh2, dx3, norm_mlp_w, scale_f, "norm_mod_mlp_bwd", branch=mix, gate=gate_m)
    (dmerged,), got = _matmul(dmix, p["w_out"], mode="nt", out_dtypes=[F32], name="mm_dmerged",
                              carry=red.pair(["w_up"]))
    red.take_pair(["w_up"], got)
    red.add("w_out", _matmul(merged, dmix, mode="tn", out_dtypes=[BF16], name="mm_g_out")[0])
    dy_ssd, dy_pool, dproj = _merge_bwd(dmerged, proj, y_ssd, y_pool)
    (dyp,), got = _matmul(dy_pool, p["w_branch_pool"], mode="nt", out_dtypes=[F32], name="mm_dyp",
                          carry=red.pair(["w_out"]))
    red.take_pair(["w_out"], got)
    red.add("w_branch_pool", _matmul(yps, dy_pool, mode="tn", out_dtypes=[BF16], name="mm_g_bpool")[0])
    dproj, g_pool_w, sums_pool = _pool_bwd(dyp, pw_out, pooled, p["pool_w"], pool_scale, dproj)
    red.add("pool_w", g_pool_w.astype(BF16))
    red.add("w_branch_ssd", _matmul(yn, dy_ssd, mode="tn", out_dtypes=[BF16], name="mm_g_bssd")[0])
    mixers = ["w_branch_pool", "pool_w", "w_branch_ssd"]
    (dyn,), got = _matmul(dy_ssd, p["w_branch_ssd"], mode="nt", out_dtypes=[F32], name="mm_dyn",
                          carry=red.pair(mixers))
    red.take_pair(mixers, got)
    dy, dproj, sums_gn = _gate_norm_bwd(dyn, y, proj, ssd_norm_w, dproj)
    six = ["w_down", "w_up", "w_out"] + mixers
    (dxa, dproj, dsk_sum, ssd_small), got = _ssd_bwd(dy, xbc_a, proj, hin, dt_bias, a_log, dtb_c, alog_c, dsk_exp,
                                                     dproj, carry=red.chip(six))
    red.take_chip(six, got)
    dxc, sums_conv = _conv_bwd_a(dxa, proj, p["conv_w"], conv_b)
    dproj = _conv_bwd_b(dxc, p["conv_w"], dproj)
    rows_a = 3 * D // 4
    (g_in_a,), got = _matmul(h1, dproj, mode="tn", out_dtypes=[BF16], name="mm_g_in_a", a_cols=(0, rows_a),
                             carry=red.share(six))
    red.take_share(six, got)
    red.add("w_in_a", g_in_a)
    (g_in_b,), got = _matmul(h1, dproj, mode="tn", out_dtypes=[BF16], name="mm_g_in_b", a_cols=(rows_a, D - rows_a),
                             carry=red.pair(["w_in_a"]))
    red.take_pair(["w_in_a"], got)
    red.add("w_in_b", g_in_b)
    (dh1,), got = _matmul(dproj, p["w_in"], mode="nt", out_dtypes=[F32], name="mm_dh1",
                          carry=_join(red.chip(["w_in_a"]), red.pair(["w_in_b"])))
    red.take_chip(["w_in_a"], got[:1])
    red.take_pair(["w_in_b"], got[1:])
    grad_x, sums_1 = _norm_mod_bwd(xs, dh1, dx2, norm_mix_w, scale_m, "norm_mod_mix_bwd")

    dmod = jnp.concatenate([sums_1[0:1], sums_1[1:2], sums_2[3:4], sums_2[0:1], sums_2[1:2], sums_f[1:2]], axis=1)
    pad96 = jnp.zeros((1, 96), F32)
    small = {"dmod": dmod, "norm_mix_w": sums_1[2:3], "conv_b": sums_conv[4:5], "ssd_norm_w": sums_gn[0:1],
             "pool_scale": sums_pool[0:1], "norm_mlp_w": sums_2[2:3], "norm_final_w": sums_f[0:1],
             "conv_w": sums_conv[0:4].reshape(1, 4 * XBC), "d_skip": dsk_sum[0:1],
             "a_log": jnp.concatenate([ssd_small[0:1], pad96], axis=1),
             "dt_bias": jnp.concatenate([ssd_small[1:2], pad96], axis=1), "loss": sums_f[3:4, 0:128]}
    vec = jnp.concatenate([small[n] for n, _ in _SMALL_SEGS], axis=1)
    vec = jnp.pad(vec, ((0, 0), (0, SMALL_LEN - vec.shape[1]))).reshape(SMALL_LEN // 128, 128)
    (every, total, dsk), got = _gather_small(vec, carry=_join(red.chip(["w_in_b"]), red.share(["w_in_a"])))
    red.take_chip(["w_in_b"], got[:1])
    red.take_share(["w_in_a"], got[1:])
    total = total.reshape(1, SMALL_LEN)
    seg = lambda n, size: total[:, SMALL_OFF[n]:SMALL_OFF[n] + size]
    g.update({"b_ada": seg("dmod", 6 * D), "norm_mix_w": seg("norm_mix_w", D), "conv_b": seg("conv_b", XBC),
              "dt_bias": seg("dt_bias", HEADS), "a_log": seg("a_log", HEADS), "d_skip": dsk[:, 0:2].reshape(1, HEADS),
              "ssd_norm_w": seg("ssd_norm_w", DI), "pool_scale": seg("pool_scale", D),
              "norm_mlp_w": seg("norm_mlp_w", D), "norm_final_w": seg("norm_final_w", D)})
    loss = total[0, SMALL_OFF["loss"]]
    conv_cols = conv_w.shape[-1]
    g["conv_w"] = lax.dynamic_slice(seg("conv_w", 4 * XBC).reshape(4, XBC), (0, chip * conv_cols), (4, conv_cols))
    dmod8 = every.reshape(8, SMALL_LEN)[:, SMALL_OFF["dmod"]:SMALL_OFF["dmod"] + 6 * D]
    g["w_ada"] = _w_ada_grad(c8, lax.dynamic_slice(dmod8, (0, chip * ada_cols), (8, ada_cols)))

    got = adamw("w_ada", carry=red.share(["w_in_b"]))
    red.take_share(["w_in_b"], got)
    for n in six:
        g[n] = red.final[n]
    g["w_in"] = jnp.concatenate([red.final["w_in_a"], red.final["w_in_b"]], axis=0)
    for n in ["conv_w", "w_in"] + six:
        adamw(n)
    sizes = [w[n].size for n in _SMALL_REPLICATED]
    n_small = -(-sum(sizes) // 1024) * 1024
    pack = lambda d: jnp.pad(jnp.concatenate([d[n].reshape(1, -1) for n in _SMALL_REPLICATED], axis=1),
                             ((0, 0), (0, n_small - sum(sizes)))).reshape(n_small // 128, 128)
    d_, m_, v_ = _adamw(pack(w), pack(g), pack(m), pack(v), "adamw_small")
    off = 0
    for n, s in zip(_SMALL_REPLICATED, sizes):
        for dst, src in ((delta, d_), (new_m, m_), (new_v, v_)):
            dst[n] = src.reshape(1, n_small)[:, off:off + s]
        off += s

    out = [loss, grad_x.reshape(x.shape)]
    for d in (g, delta, new_m, new_v):
        out += [d[n].reshape(w[n].shape) for n in _WEIGHTS]
    return tuple(out)
```

```python
import functools
import operator

import jax
import jax.numpy as jnp
import numpy as np
from jax import lax
from jax.experimental import pallas as pl
from jax.experimental.pallas import tpu as pltpu

F32, BF16 = jnp.float32, jnp.bfloat16
HIGH = lax.Precision.HIGHEST
MESH = pl.DeviceIdType.MESH

D = 1024
DI = 2048
HEADS, HEAD_DIM = 32, 64
GROUPS, STATE = 4, 128
Q = 128
XBC = DI + 2 * GROUPS * STATE
POOL_WINDOWS = (2, 4, 8, 16)
GW = 256
DFF = 4096
EPS = 1e-5
IN_COLS = 8224
OFF_Z, OFF_XBC, OFF_POOL, OFF_GATE, OFF_DT, NP = 0, 2048, 5120, 6144, 8192, 8448
N_CHIPS = 4
ADAM_LR, ADAM_B1, ADAM_B2, ADAM_EPS, ADAM_WD, ADAM_STEP = 0.001, 0.9, 0.999, 1e-08, 0.01, 10
VMEM_LIMIT = 56 * 2 ** 20
NEG = -1e30


def _sigmoid(v):
    return 0.5 * jnp.tanh(0.5 * v) + 0.5


def _softplus(v):
    return jnp.maximum(v, 0.0) + jnp.log1p(jnp.exp(-jnp.abs(v)))


def _dot(a, b, dims, **kw):
    return lax.dot_general(a, b, (dims, ((), ())), preferred_element_type=F32, **kw)


def _nn(a, b, **kw):
    return _dot(a, b, ((1,), (0,)), **kw)


def _nt(a, b, **kw):
    return _dot(a, b, ((1,), (1,)), **kw)


def _tn(a, b, **kw):
    return _dot(a, b, ((0,), (0,)), **kw)


_DT_IN_CHIP2 = 5120 - 2 * (IN_COLS // 4)


class _Sems:
    def __init__(self, send, recv, local, base=0):
        self._send, self._recv, self._local, self._base = send, recv, local, base

    def shift(self, n):
        return _Sems(self._send, self._recv, self._local, self._base + n)

    def send(self, i):
        return self._send.at[self._base + i]

    def recv(self, i):
        return self._recv.at[self._base + i]

    def local(self, i):
        return self._local.at[self._base + i]


class _Carry:
    def __init__(self, ins, out_shapes, n_sems, start, finish, aliased=()):
        self.ins, self.out_shapes, self.n_sems, self.start, self.finish = list(ins), list(out_shapes), n_sems, start, finish
        self.aliased = list(aliased)


def _join(*carries):
    def run(which):
        def fn(ins, outs, sems):
            i = o = s = 0
            for cy in carries:
                getattr(cy, which)(ins[i:i + len(cy.ins)], outs[o:o + len(cy.out_shapes)], sems.shift(s))
                i, o, s = i + len(cy.ins), o + len(cy.out_shapes), s + cy.n_sems
        return fn

    aliased, i, o = [], 0, 0
    for cy in carries:
        aliased += [(i + a, o + b) for a, b in cy.aliased]
        i, o = i + len(cy.ins), o + len(cy.out_shapes)
    return _Carry([a for cy in carries for a in cy.ins], [a for cy in carries for a in cy.out_shapes],
                  sum(cy.n_sems for cy in carries), run("start"), run("finish"), aliased)


def _call(body, args, *, name, grid=(), in_specs, out_specs, out_shape, scratch_shapes=(), sem=None, aliases=None,
          carry=None):
    in_specs, out_specs, out_shape, scratch_shapes = list(in_specs), list(out_specs), list(out_shape), list(scratch_shapes)
    n_in, n_out, n_scr = len(in_specs), len(out_specs), len(scratch_shapes)
    kw = {"vmem_limit_bytes": VMEM_LIMIT}
    if carry is None:
        kernel_fn = functools.partial(body)
        if sem is not None:
            kw["dimension_semantics"] = sem
    else:
        n_ci, n_co = len(carry.ins), len(carry.out_shapes)
        hbm = pl.BlockSpec(memory_space=pl.ANY)
        in_specs += [hbm] * n_ci
        out_specs += [hbm] * n_co
        out_shape += carry.out_shapes
        n_s = max(carry.n_sems, 1)
        scratch_shapes += [pltpu.SemaphoreType.DMA((n_s,))] * 3
        args = list(args) + carry.ins
        aliases = dict(aliases or {})
        aliases.update({n_in + i: n_out + o for i, o in carry.aliased})
        if grid:
            kw["dimension_semantics"] = ("arbitrary",) * len(grid)

        def kernel_fn(*refs):
            a = n_in
            ins, c_ins = refs[:a], refs[a:a + n_ci]
            a += n_ci
            outs, c_outs = refs[a:a + n_out], refs[a + n_out:a + n_out + n_co]
            a += n_out + n_co
            scr, sems = refs[a:a + n_scr], _Sems(*refs[a + n_scr:a + n_scr + 3])
            if grid:
                ids = [pl.program_id(d) for d in range(len(grid))]
                first = functools.reduce(operator.and_, [i == 0 for i in ids])
                last = functools.reduce(operator.and_, [i == g - 1 for i, g in zip(ids, grid)])

                @pl.when(first)
                def _():
                    carry.start(c_ins, c_outs, sems)

                body(*ins, *outs, *scr)

                @pl.when(last)
                def _():
                    carry.finish(c_ins, c_outs, sems)
            else:
                carry.start(c_ins, c_outs, sems)
                body(*ins, *outs, *scr)
                carry.finish(c_ins, c_outs, sems)

    outs = pl.pallas_call(
        kernel_fn, name=name, grid=grid, in_specs=in_specs, out_specs=out_specs, out_shape=out_shape,
        scratch_shapes=scratch_shapes, input_output_aliases=aliases or {},
        compiler_params=pltpu.CompilerParams(**kw),
    )(*args)
    outs = list(outs)
    return outs if carry is None else (outs[:n_out], outs[n_out:])


def _run_carry(carry, name):
    _, outs = _call(lambda: None, [], name=name, in_specs=[], out_specs=[], out_shape=[], carry=carry)
    return outs


_TILES = {
    "mm_proj": (1024, 2816, 1024), "mm_branch_ssd": (1024, 1024, 2048), "mm_branch_pool": (1024, 1024, 1024),
    "mm_out": (1024, 1024, 1024), "mm_up": (2048, 1024, 1024), "mm_down": (512, 1024, 4096),
    "mm_dact": (1024, 1024, 1024), "mm_g_down": (1024, 1024, 4096), "mm_dh2": (1024, 1024, 4096),
    "mm_g_up": (1024, 1024, 4096), "mm_dmerged": (1024, 1024, 1024), "mm_g_out": (1024, 1024, 2048),
    "mm_dyp": (1024, 1024, 1024), "mm_g_bpool": (1024, 1024, 2048), "mm_g_bssd": (1024, 1024, 4096),
    "mm_dyn": (1024, 1024, 1024), "mm_g_in_a": (768, 1408, 4096), "mm_g_in_b": (256, 2816, 2048),
    "mm_dh1": (1024, 1024, 4224),
}


def _matmul(a, b, *, mode, out_dtypes, name, epi=None, tile_extras=(), row_extras=(), carry=None, a_cols=None,
            chip_blocks=False, cols_outer=False):
    M, K = (a.shape[1], a.shape[0]) if mode == "tn" else a.shape
    N = b.shape[0] if mode == "nt" else b.shape[1]
    a_start, M = a_cols if a_cols is not None else (0, M)
    tm, tn, tk = _TILES[name]
    tm, tn, tk = min(tm, M), min(tn, N), min(tk, K)
    assert M % tm == 0 and N % tn == 0 and K % tk == 0 and a_start % tm == 0, (name, M, N, K, tm, tn, tk)
    a_off = a_start // tm
    if mode == "nn":
        a_spec = pl.BlockSpec((tm, tk), lambda i, j, k: (i, k))
        b_spec = pl.BlockSpec((tk, tn), lambda i, j, k: (k, j))
        dims = ((1,), (0,))
    elif mode == "nt":
        a_spec = pl.BlockSpec((tm, tk), lambda i, j, k: (i, k))
        b_spec = pl.BlockSpec((tn, tk), lambda i, j, k: (j, k))
        dims = ((1,), (1,))
    else:
        a_spec = pl.BlockSpec((tk, tm), lambda i, j, k: (k, i + a_off))
        b_spec = pl.BlockSpec((tk, tn), lambda i, j, k: (k, j))
        dims = ((0,), (0,))
    nk = K // tk
    n_te, n_re, n_out = len(tile_extras), len(row_extras), len(out_dtypes)
    if epi is None:
        epi = lambda acc: (acc,)

    def body(a_ref, b_ref, *rest):
        extras = rest[:n_te + n_re]
        outs = rest[n_te + n_re:n_te + n_re + n_out]
        p = _dot(a_ref[...], b_ref[...], dims)

        def finish(acc):
            vals = epi(acc, *[e[...] for e in extras])
            for o, v in zip(outs, vals):
                o[...] = v.astype(o.dtype)

        if nk == 1:
            finish(p)
        else:
            acc_ref = rest[-1]
            k = pl.program_id(2)

            @pl.when(k == 0)
            def _():
                acc_ref[...] = p

            @pl.when(k > 0)
            def _():
                acc_ref[...] += p

            @pl.when(k == nk - 1)
            def _():
                finish(acc_ref[...])

    tile_spec = pl.BlockSpec((tm, tn), lambda i, j, k: (i, j))
    row_spec = pl.BlockSpec((1, tn), lambda i, j, k: (0, j))
    out_spec, out_dims = tile_spec, (M, N)
    if chip_blocks:
        assert n_te == 0 and tn * N_CHIPS == N
        out_spec, out_dims = pl.BlockSpec((None, tm, tn), lambda i, j, k: (j, i, 0)), (N_CHIPS, M, tn)
    in_specs, grid = [a_spec, b_spec] + [tile_spec] * n_te + [row_spec] * n_re, (M // tm, N // tn, nk)
    if cols_outer:
        swap = lambda s: pl.BlockSpec(s.block_shape, lambda g0, g1, k, f=s.index_map: f(g1, g0, k))
        in_specs, out_spec, grid = [swap(s) for s in in_specs], swap(out_spec), (N // tn, M // tm, nk)
    return _call(
        body, [a, b, *tile_extras, *row_extras], name=name, grid=grid,
        in_specs=in_specs, out_specs=[out_spec] * n_out,
        out_shape=[jax.ShapeDtypeStruct(out_dims, dt) for dt in out_dtypes],
        scratch_shapes=[pltpu.VMEM((tm, tn), F32)] if nk > 1 else [],
        sem=("parallel", "parallel", "arbitrary"), carry=carry)


def _row_tile(T):
    return min(512, T)


def _norm_mod(x, nw, scale, shift, name, carry=None):
    T = x.shape[0]
    tr = _row_tile(T)

    def body(x_ref, nw_ref, sc_ref, sh_ref, o_ref):
        xv = x_ref[...]
        r = lax.rsqrt(jnp.mean(xv * xv, axis=-1, keepdims=True) + EPS)
        o_ref[...] = ((xv * r) * nw_ref[...] * (1.0 + sc_ref[...]) + sh_ref[...]).astype(BF16)

    tile = pl.BlockSpec((tr, D), lambda i: (i, 0))
    row = pl.BlockSpec((1, D), lambda i: (0, 0))
    res = _call(body, [x, nw, scale, shift], name=name, grid=(T // tr,), in_specs=[tile, row, row, row],
                out_specs=[tile], out_shape=[jax.ShapeDtypeStruct((T, D), BF16)], sem=("parallel",), carry=carry)
    return res[0] if carry is None else (res[0][0], res[1])


def _norm_mod_bwd(x, dh, dres, nw, scale, name, branch=None, gate=None, carry=None):
    T = x.shape[0]
    tr = _row_tile(T)
    with_branch = branch is not None

    def body(x_ref, dh_ref, dr_ref, nw_ref, sc_ref, *rest):
        if with_branch:
            br_ref, g_ref, dx_ref, sums_ref, db_ref = rest
        else:
            dx_ref, sums_ref = rest
        i = pl.program_id(0)

        @pl.when(i == 0)
        def _():
            sums_ref[...] = jnp.zeros_like(sums_ref)

        xv, dhv = x_ref[...], dh_ref[...]
        r = lax.rsqrt(jnp.mean(xv * xv, axis=-1, keepdims=True) + EPS)
        xn = xv * r
        g1 = dhv * (1.0 + sc_ref[...])
        dxn = g1 * nw_ref[...]
        dx = dr_ref[...] + r * (dxn - xn * jnp.mean(dxn * xn, axis=-1, keepdims=True))
        dx_ref[...] = dx
        sums_ref[0:1, :] += jnp.sum(dhv, axis=0, keepdims=True)
        sums_ref[1:2, :] += jnp.sum(dhv * (xn * nw_ref[...]), axis=0, keepdims=True)
        sums_ref[2:3, :] += jnp.sum(g1 * xn, axis=0, keepdims=True)
        if with_branch:
            db_ref[...] = (dx * g_ref[...]).astype(BF16)
            sums_ref[3:4, :] += jnp.sum(dx * br_ref[...], axis=0, keepdims=True)

    tile = pl.BlockSpec((tr, D), lambda i: (i, 0))
    row = pl.BlockSpec((1, D), lambda i: (0, 0))
    sums = pl.BlockSpec((8, D), lambda i: (0, 0))
    ins = [x, dh, dres, nw, scale] + ([branch, gate] if with_branch else [])
    in_specs = [tile, tile, tile, row, row] + ([tile, row] if with_branch else [])
    out_specs = [tile, sums] + ([tile] if with_branch else [])
    out_shape = [jax.ShapeDtypeStruct((T, D), F32), jax.ShapeDtypeStruct((8, D), F32)]
    if with_branch:
        out_shape.append(jax.ShapeDtypeStruct((T, D), BF16))
    return _call(body, ins, name=name, grid=(T // tr,), in_specs=in_specs, out_specs=out_specs, out_shape=out_shape,
                 sem=("arbitrary",), carry=carry)


def _final_loss_bwd(x3, target, wf, down, gate_f):
    T = x3.shape[0]
    tr = _row_tile(T)
    n_steps = T // tr

    def body(x_ref, t_ref, w_ref, dn_ref, g_ref, dx_ref, dd_ref, sums_ref):
        i = pl.program_id(0)

        @pl.when(i == 0)
        def _():
            sums_ref[...] = jnp.zeros_like(sums_ref)

        xv = x_ref[...]
        r = lax.rsqrt(jnp.mean(xv * xv, axis=-1, keepdims=True) + EPS)
        xn = xv * r
        err = xn * w_ref[...] - t_ref[...]
        dy = err * (1.0 / D)
        dxn = dy * w_ref[...]
        dx = r * (dxn - xn * jnp.mean(dxn * xn, axis=-1, keepdims=True))
        dx_ref[...] = dx
        dd_ref[...] = (dx * g_ref[...]).astype(BF16)
        sums_ref[0:1, :] += jnp.sum(dy * xn, axis=0, keepdims=True)
        sums_ref[1:2, :] += jnp.sum(dx * dn_ref[...], axis=0, keepdims=True)
        sums_ref[2:3, :] += jnp.sum(err * err, axis=0, keepdims=True) * (0.5 / D)

        @pl.when(i == n_steps - 1)
        def _():
            sums_ref[3:4, :] = jnp.broadcast_to(jnp.sum(sums_ref[2:3, :], axis=1, keepdims=True), (1, D))

    tile = pl.BlockSpec((tr, D), lambda i: (i, 0))
    row = pl.BlockSpec((1, D), lambda i: (0, 0))
    sums = pl.BlockSpec((8, D), lambda i: (0, 0))
    return _call(body, [x3, target, wf, down, gate_f], name="final_loss_bwd", grid=(n_steps,),
                 in_specs=[tile, tile, row, tile, row], out_specs=[tile, tile, sums],
                 out_shape=[jax.ShapeDtypeStruct((T, D), F32), jax.ShapeDtypeStruct((T, D), BF16),
                            jax.ShapeDtypeStruct((8, D), F32)], sem=("arbitrary",))


CONV_TC = 1024


def _conv_taps(xp, w, b):
    acc = b + w[3:4, :] * xp
    for k in range(3):
        acc = acc + w[k:k + 1, :] * pltpu.roll(xp, 3 - k, 0)
    return acc


def _conv_fwd(proj, conv_w, conv_b):
    T = proj.shape[0]
    tr = _row_tile(T)
    nb, offb = tr // 8, OFF_XBC // CONV_TC

    def body(x_ref, h_ref, w_ref, b_ref, o_ref):
        halo = jnp.where(pl.program_id(0) > 0, h_ref[...], 0.0)
        xp = jnp.concatenate([halo, x_ref[...]], axis=0)
        acc = _conv_taps(xp, w_ref[...], b_ref[...])[8:]
        o_ref[...] = acc * _sigmoid(acc)

    return _call(
        body, [proj, proj, conv_w, conv_b], name="conv_fwd", grid=(T // tr, XBC // CONV_TC),
        in_specs=[pl.BlockSpec((tr, CONV_TC), lambda i, j: (i, j + offb)),
                  pl.BlockSpec((8, CONV_TC), lambda i, j: (jnp.maximum(i * nb - 1, 0), j + offb)),
                  pl.BlockSpec((4, CONV_TC), lambda i, j: (0, j)),
                  pl.BlockSpec((1, CONV_TC), lambda i, j: (0, j))],
        out_specs=[pl.BlockSpec((tr, CONV_TC), lambda i, j: (i, j))],
        out_shape=[jax.ShapeDtypeStruct((T, XBC), F32)], sem=("parallel", "parallel"))[0]


def _conv_bwd(dxa, proj, conv_w, conv_b, dproj):
    T = proj.shape[0]
    tr = _row_tile(T)
    nb, offb, last = tr // 8, OFF_XBC // CONV_TC, T // tr - 1
    prev8 = lambda i: jnp.maximum(i * nb - 1, 0)
    next8 = lambda i: jnp.minimum((i + 1) * nb, T // 8 - 1)

    def body(d_ref, dn_ref, x_ref, xp_ref, xn_ref, w_ref, b_ref, dp_in, o_ref, sums_ref):
        del dp_in
        i = pl.program_id(1)

        @pl.when(i == 0)
        def _():
            sums_ref[...] = jnp.zeros_like(sums_ref)

        x = jnp.concatenate([jnp.where(i > 0, xp_ref[...], 0.0), x_ref[...], jnp.where(i < last, xn_ref[...], 0.0)], axis=0)
        d = jnp.concatenate([d_ref[...], jnp.where(i < last, dn_ref[...], 0.0)], axis=0)
        w = w_ref[...]
        taps = [pltpu.roll(x, 3 - k, 0)[8:] for k in range(3)] + [x[8:]]
        acc = b_ref[...] + w[3:4, :] * taps[3]
        for k in range(3):
            acc = acc + w[k:k + 1, :] * taps[k]
        s = _sigmoid(acc)
        dxc = d * (s * (1.0 + acc * (1.0 - s)))
        n = tr + 8
        dx = w[3:4, :] * dxc
        for k in range(3):
            dx = dx + w[k:k + 1, :] * pltpu.roll(dxc, n - (3 - k), 0)
        o_ref[...] = dx[:tr].astype(BF16)
        own = dxc[:tr]
        for k in range(4):
            sums_ref[k:k + 1, :] += jnp.sum(own * taps[k][:tr], axis=0, keepdims=True)
        sums_ref[4:5, :] += jnp.sum(own, axis=0, keepdims=True)

    return _call(
        body, [dxa, dxa, proj, proj, proj, conv_w, conv_b, dproj], name="conv_bwd", grid=(XBC // CONV_TC, T // tr),
        in_specs=[pl.BlockSpec((tr, CONV_TC), lambda j, i: (i, j)),
                  pl.BlockSpec((8, CONV_TC), lambda j, i: (next8(i), j)),
                  pl.BlockSpec((tr, CONV_TC), lambda j, i: (i, j + offb)),
                  pl.BlockSpec((8, CONV_TC), lambda j, i: (prev8(i), j + offb)),
                  pl.BlockSpec((8, CONV_TC), lambda j, i: (next8(i), j + offb)),
                  pl.BlockSpec((4, CONV_TC), lambda j, i: (0, j)),
                  pl.BlockSpec((1, CONV_TC), lambda j, i: (0, j)),
                  pl.BlockSpec(memory_space=pl.ANY)],
        out_specs=[pl.BlockSpec((tr, CONV_TC), lambda j, i: (i, j + offb)), pl.BlockSpec((8, CONV_TC), lambda j, i: (0, j))],
        out_shape=[jax.ShapeDtypeStruct(dproj.shape, BF16), jax.ShapeDtypeStruct((8, XBC), F32)],
        aliases={7: 0}, sem=("parallel", "arbitrary"))


def _spread(v, sel, pieces):
    out = None
    for _ in range(pieces):
        p = v.astype(BF16)
        term = _nn(p, sel)
        out = term if out is None else out + term
        v = v - p.astype(F32)
    return out


def _ssd_selectors():
    g = np.arange(GROUPS)[:, None, None]
    piece = np.arange(128)[None, :, None]
    h = np.where(piece < 3 * HEADS, piece % HEADS, -1)
    blocks = (h == 8 * g + np.arange(1024)[None, None, :] // 128)
    pairs = (h == 8 * g + np.arange(512)[None, None, :] // HEAD_DIM)
    lane = np.arange(128)[None, None, :]
    block_sum = (lane == 8 * g + np.arange(1024)[None, :, None] // 128)
    pair_sum = (lane == 8 * g + np.arange(512)[None, :, None] // HEAD_DIM)
    return [jnp.asarray(m, BF16) for m in (blocks, pairs, block_sum, pair_sum)]


def _pack3(v):
    p0 = v.astype(BF16)
    r1 = v - p0.astype(F32)
    p1 = r1.astype(BF16)
    r2 = r1 - p1.astype(F32)
    return p0 + pltpu.roll(r1, HEADS, 1).astype(BF16) + pltpu.roll(r2, 2 * HEADS, 1).astype(BF16)


def _ssd_group(g, cs_p, csT, dt_p, s_mat, causal_w, lo, blocks_ref, pairs_ref):
    csb = _nn(cs_p, blocks_ref[g])
    row = jnp.concatenate([csT[8 * g + hh:8 * g + hh + 1, :] for hh in range(8)], axis=1)
    l_w = jnp.exp(jnp.where(causal_w, csb - row, NEG))
    m_w = jnp.concatenate([s_mat] * 8, axis=1) * l_w
    cs_g = jnp.concatenate([jnp.where(lo, csb[:, 256 * jj:256 * jj + 128], csb[:, 256 * jj + 128:256 * jj + 256])
                            for jj in range(4)], axis=1)
    cs_last = cs_g[Q - 1:Q, :]
    return m_w, l_w, _nn(dt_p, pairs_ref[g]), jnp.exp(cs_g), jnp.exp(cs_last - cs_g), jnp.exp(cs_last)


def _ssd_common(dtp_ref, dtb_r, alog_r, dtb_c, alog_c):
    rows = lax.broadcasted_iota(jnp.int32, (Q, Q), 0)
    cols = lax.broadcasted_iota(jnp.int32, (Q, Q), 1)
    tri = (cols <= rows).astype(F32)
    heads = lax.broadcasted_iota(jnp.int32, (1, 128), 1) < HEADS
    raw_w = dtp_ref[...] + dtb_r[...]
    dt_w = jnp.where(heads, _softplus(raw_w), 0.0)
    a_w = -jnp.exp(alog_r[...])
    cs_w = _nn(tri, dt_w * a_w, precision=HIGH)
    aT = _softplus(dtp_ref[...].T[0:HEADS, :] + dtb_c[...]) * (-jnp.exp(alog_c[...]))
    csT = _nt(aT, tri, precision=HIGH)
    return raw_w[:, 0:HEADS], dt_w[:, 0:HEADS], a_w[:, 0:HEADS], csT, _pack3(cs_w), _pack3(dt_w)


def _ssd_fwd(xbc_a, proj, dtb_r, alog_r, dtb_c, alog_c, dsk_exp):
    T = xbc_a.shape[0]
    nc = T // Q
    dtb_r, alog_r = [jnp.pad(a, ((0, 0), (0, 128 - HEADS))) for a in (dtb_r, alog_r)]

    def body(xbc_ref, dtp_ref, dtb_r_ref, alog_r_ref, dtb_c_ref, alog_c_ref, dsk_ref, blocks_ref, pairs_ref,
             y_ref, hin_ref, h_scr):
        @pl.when(pl.program_id(0) == 0)
        def _():
            h_scr[...] = jnp.zeros_like(h_scr)

        _, _, _, csT, cs_p, dt_p = _ssd_common(dtp_ref, dtb_r_ref, alog_r_ref, dtb_c_ref, alog_c_ref)
        lo = lax.broadcasted_iota(jnp.int32, (1, 128), 1) < HEAD_DIM
        hi = jnp.logical_not(lo)
        causal_w = (lax.broadcasted_iota(jnp.int32, (Q, 1024), 1) & (Q - 1)) <= lax.broadcasted_iota(jnp.int32, (Q, 1024), 0)
        for g in range(GROUPS):
            gs = slice(512 * g, 512 * (g + 1))
            hs = slice(128 * g, 128 * (g + 1))
            xs_g = xbc_ref[:, gs]
            b_g = xbc_ref[:, DI + STATE * g:DI + STATE * (g + 1)].astype(BF16)
            c_g = xbc_ref[:, DI + 512 + STATE * g:DI + 512 + STATE * (g + 1)].astype(BF16)
            m_w, _, dt_g, ecs_g, dec_g, cd_g = _ssd_group(g, cs_p, csT, dt_p, _nt(c_g, b_g), causal_w, lo, blocks_ref, pairs_ref)
            m_b = m_w.astype(BF16)
            xdt = xs_g * dt_g
            xdt_b = xdt.astype(BF16)
            ys = []
            for jj in range(4):
                xp = xdt_b[:, 128 * jj:128 * (jj + 1)]
                x_ab = jnp.concatenate([jnp.where(lo, xp, jnp.zeros_like(xp)), jnp.where(hi, xp, jnp.zeros_like(xp))], axis=0)
                ys.append(_nn(m_b[:, 256 * jj:256 * (jj + 1)], x_ab))
            h_g = h_scr[hs, :]
            hin_ref[0, hs, :] = h_g
            y_ref[:, gs] = jnp.concatenate(ys, axis=1) + _nn(c_g, h_g.astype(BF16)) * ecs_g + dsk_ref[:, gs] * xs_g
            h_scr[hs, :] = h_g * cd_g + _tn(b_g, (xdt * dec_g).astype(BF16))

    small_r = pl.BlockSpec((1, 128), lambda c: (0, 0))
    small_c = pl.BlockSpec((HEADS, 1), lambda c: (0, 0))
    blocks, pairs, _, _ = _ssd_selectors()
    whole = lambda a: pl.BlockSpec(a.shape, lambda c: (0,) * a.ndim)
    return _call(
        body, [xbc_a, proj, dtb_r, alog_r, dtb_c, alog_c, dsk_exp, blocks, pairs], name="ssd_fwd", grid=(nc,),
        in_specs=[pl.BlockSpec((Q, XBC), lambda c: (c, 0)),
                  pl.BlockSpec((Q, 128), lambda c: (c, OFF_DT // 128)),
                  small_r, small_r, small_c, small_c,
                  pl.BlockSpec((1, DI), lambda c: (0, 0)), whole(blocks), whole(pairs)],
        out_specs=[pl.BlockSpec((Q, DI), lambda c: (c, 0)), pl.BlockSpec((1, 512, 512), lambda c: (c, 0, 0))],
        out_shape=[jax.ShapeDtypeStruct((T, DI), F32), jax.ShapeDtypeStruct((nc, 512, 512), F32)],
        scratch_shapes=[pltpu.VMEM((512, 512), F32)], sem=("arbitrary",))


def _ssd_bwd(dy, xbc_a, proj, hin, dtb_r, alog_r, dtb_c, alog_c, dsk_exp, dproj, carry=None):
    T = xbc_a.shape[0]
    nc = T // Q
    dtb_r, alog_r = [jnp.pad(a, ((0, 0), (0, 128 - HEADS))) for a in (dtb_r, alog_r)]

    def body(dy_ref, xbc_ref, dtp_ref, hin_ref, dtb_r_ref, alog_r_ref, dtb_c_ref, alog_c_ref, dsk_ref, dp_in,
             blocks_ref, pairs_ref, block_sum_ref, pair_sum_ref, dxa_ref, dp_ref, dsk_sum_ref, small_ref, dh_scr):
        del dp_in

        @pl.when(pl.program_id(0) == 0)
        def _():
            dh_scr[...] = jnp.zeros_like(dh_scr)
            dsk_sum_ref[...] = jnp.zeros_like(dsk_sum_ref)
            small_ref[...] = jnp.zeros_like(small_ref)

        raw, dt, a_r, csT, cs_p, dt_p = _ssd_common(dtp_ref, dtb_r_ref, alog_r_ref, dtb_c_ref, alog_c_ref)
        lo = lax.broadcasted_iota(jnp.int32, (1, 128), 1) < HEAD_DIM
        hi = jnp.logical_not(lo)
        sub32 = lax.broadcasted_iota(jnp.int32, (HEADS, 1), 0)
        causal_w = (lax.broadcasted_iota(jnp.int32, (Q, 1024), 1) & (Q - 1)) <= lax.broadcasted_iota(jnp.int32, (Q, 1024), 0)
        dcs_c = jnp.zeros((Q, 128), F32)
        dcs_r = jnp.zeros((HEADS, Q), F32)
        dcs_l = jnp.zeros((8, 128), F32)
        ddt_x = jnp.zeros((Q, 128), F32)
        for g in range(GROUPS):
            gs = slice(512 * g, 512 * (g + 1))
            hs = slice(128 * g, 128 * (g + 1))
            xs_g, dy_g = xbc_ref[:, gs], dy_ref[:, gs]
            b_g = xbc_ref[:, DI + STATE * g:DI + STATE * (g + 1)].astype(BF16)
            c_g = xbc_ref[:, DI + 512 + STATE * g:DI + 512 + STATE * (g + 1)].astype(BF16)
            m_w, l_w, dt_g, ecs_g, dec_g, cd_g = _ssd_group(g, cs_p, csT, dt_p, _nt(c_g, b_g), causal_w, lo, blocks_ref, pairs_ref)
            m_b = m_w.astype(BF16)
            xdt = xs_g * dt_g
            xdt_b, dy_b = xdt.astype(BF16), dy_g.astype(BF16)
            dms, dxs = [], []
            for jj in range(4):
                xp, dyp = xdt_b[:, 128 * jj:128 * (jj + 1)], dy_b[:, 128 * jj:128 * (jj + 1)]
                dy_ab = jnp.concatenate([jnp.where(lo, dyp, jnp.zeros_like(dyp)), jnp.where(hi, dyp, jnp.zeros_like(dyp))], axis=0)
                dm_ab = _nt(dy_ab, xp)
                dms += [dm_ab[:Q], dm_ab[Q:]]
                dx_ab = _tn(m_b[:, 256 * jj:256 * (jj + 1)], dyp)
                dxs.append(jnp.where(lo, dx_ab[:Q], dx_ab[Q:]))
            dm_w = jnp.concatenate(dms, axis=1)
            w_w = dm_w * m_w
            dcs_c = dcs_c + _spread(w_w, block_sum_ref[g], 2)
            w_cols = jnp.sum(w_w, axis=0, keepdims=True)
            for hh in range(8):
                dcs_r = dcs_r + jnp.where(sub32 == 8 * g + hh, w_cols[:, 128 * hh:128 * (hh + 1)], 0.0)
            dl_w = dm_w * l_w
            ds_mat = dl_w[:, 0:128]
            for hh in range(1, 8):
                ds_mat = ds_mat + dl_w[:, 128 * hh:128 * (hh + 1)]
            hin_g = hin_ref[0, hs, :]
            hin_b = hin_g.astype(BF16)
            dh_g = dh_scr[hs, :]
            dh_b = dh_g.astype(BF16)
            g_mat = _nn(b_g, dh_b)
            xdec = xdt * dec_g
            xg = xdec * g_mat
            dxdt = jnp.concatenate(dxs, axis=1) + dec_g * g_mat
            sums = _spread(jnp.concatenate([dy_g * (_nn(c_g, hin_b) * ecs_g) - xg, dxdt * xs_g], axis=0), pair_sum_ref[g], 2)
            dcs_c = dcs_c + sums[:Q]
            ddt_x = ddt_x + sums[Q:]
            last = jnp.sum(xg, axis=0, keepdims=True) + jnp.sum(dh_g * hin_g, axis=0, keepdims=True) * cd_g
            dcs_l = dcs_l + _spread(jnp.broadcast_to(last, (8, 512)), pair_sum_ref[g], 2)
            dz = (dy_g * ecs_g).astype(BF16)
            ds_b = ds_mat.astype(BF16)
            dxa_ref[:, gs] = dxdt * dt_g + dy_g * dsk_ref[:, gs]
            dxa_ref[:, DI + STATE * g:DI + STATE * (g + 1)] = _nt(xdec.astype(BF16), dh_b) + _tn(ds_b, c_g)
            dxa_ref[:, DI + 512 + STATE * g:DI + 512 + STATE * (g + 1)] = _nt(dz, hin_b) + _nn(ds_b, b_g)
            dh_scr[hs, :] = _tn(c_g, dz) + dh_g * cd_g
            dsk_sum_ref[0:1, gs] += jnp.sum(dy_g * xs_g, axis=0, keepdims=True)

        rows = lax.broadcasted_iota(jnp.int32, (Q, Q), 0)
        cols = lax.broadcasted_iota(jnp.int32, (Q, Q), 1)
        tri_t = (cols >= rows).astype(F32)
        last_row = lax.broadcasted_iota(jnp.int32, (Q, 1), 0) == Q - 1
        dcs = (dcs_c + jnp.where(last_row, dcs_l[0:1, :], 0.0))[:, 0:HEADS]
        da = _nn(tri_t, dcs, precision=HIGH) - _nt(tri_t, dcs_r, precision=HIGH)
        ddt_raw = (ddt_x[:, 0:HEADS] + da * a_r) * _sigmoid(raw)
        small_ref[0:1, :] += jnp.sum(da * dt, axis=0, keepdims=True) * a_r
        small_ref[1:2, :] += jnp.sum(ddt_raw, axis=0, keepdims=True)
        dp_ref[...] = jnp.zeros_like(dp_ref)
        dp_ref[:, 0:HEADS] = ddt_raw.astype(BF16)

    rev = lambda c: nc - 1 - c
    small_r = pl.BlockSpec((1, 128), lambda c: (0, 0))
    small_c = pl.BlockSpec((HEADS, 1), lambda c: (0, 0))
    selectors = _ssd_selectors()
    whole = lambda a: pl.BlockSpec(a.shape, lambda c: (0,) * a.ndim)
    return _call(
        body, [dy, xbc_a, proj, hin, dtb_r, alog_r, dtb_c, alog_c, dsk_exp, dproj, *selectors], name="ssd_bwd", grid=(nc,),
        in_specs=[pl.BlockSpec((Q, DI), lambda c: (rev(c), 0)),
                  pl.BlockSpec((Q, XBC), lambda c: (rev(c), 0)),
                  pl.BlockSpec((Q, 128), lambda c: (rev(c), OFF_DT // 128)),
                  pl.BlockSpec((1, 512, 512), lambda c: (rev(c), 0, 0)),
                  small_r, small_r, small_c, small_c,
                  pl.BlockSpec((1, DI), lambda c: (0, 0)),
                  pl.BlockSpec(memory_space=pl.ANY)] + [whole(a) for a in selectors],
        out_specs=[pl.BlockSpec((Q, XBC), lambda c: (rev(c), 0)),
                   pl.BlockSpec((Q, 256), lambda c: (rev(c), OFF_DT // 256)),
                   pl.BlockSpec((8, DI), lambda c: (0, 0)),
                   pl.BlockSpec((8, HEADS), lambda c: (0, 0))],
        out_shape=[jax.ShapeDtypeStruct((T, XBC), F32), jax.ShapeDtypeStruct(dproj.shape, BF16),
                   jax.ShapeDtypeStruct((8, DI), F32), jax.ShapeDtypeStruct((8, HEADS), F32)],
        aliases={9: 1}, scratch_shapes=[pltpu.VMEM((512, 512), F32)], sem=("arbitrary",), carry=carry)


def _gate_norm(y, proj, w):
    T = y.shape[0]
    tr = _row_tile(T)

    def body(y_ref, z_ref, w_ref, o_ref):
        for g in range(GROUPS):
            gs = slice(512 * g, 512 * (g + 1))
            z = z_ref[:, gs]
            yg = y_ref[:, gs] * (z * _sigmoid(z))
            r = lax.rsqrt(jnp.mean(yg * yg, axis=-1, keepdims=True) + EPS)
            o_ref[:, gs] = (yg * r * w_ref[:, gs]).astype(BF16)

    tile = pl.BlockSpec((tr, DI), lambda i: (i, 0))
    return _call(body, [y, proj, w], name="gate_norm", grid=(T // tr,),
                 in_specs=[tile, tile, pl.BlockSpec((1, DI), lambda i: (0, 0))], out_specs=[tile],
                 out_shape=[jax.ShapeDtypeStruct((T, DI), BF16)], sem=("parallel",))[0]


def _gate_norm_bwd(dyn, y, proj, w, dproj):
    T = y.shape[0]
    tr = _row_tile(T)

    def body(d_ref, y_ref, z_ref, w_ref, dp_in, dy_ref, dz_ref, sums_ref):
        del dp_in

        @pl.when(pl.program_id(0) == 0)
        def _():
            sums_ref[...] = jnp.zeros_like(sums_ref)

        for g in range(GROUPS):
            gs = slice(512 * g, 512 * (g + 1))
            z, yv, d = z_ref[:, gs], y_ref[:, gs], d_ref[:, gs]
            s = _sigmoid(z)
            silu = z * s
            yg = yv * silu
            r = lax.rsqrt(jnp.mean(yg * yg, axis=-1, keepdims=True) + EPS)
            yn = yg * r
            sums_ref[0:1, gs] += jnp.sum(d * yn, axis=0, keepdims=True)
            dn = d * w_ref[:, gs]
            dyg = r * (dn - yn * jnp.mean(dn * yn, axis=-1, keepdims=True))
            dy_ref[:, gs] = dyg * silu
            dz_ref[:, gs] = (dyg * yv * (s * (1.0 + z * (1.0 - s)))).astype(BF16)

    tile = pl.BlockSpec((tr, DI), lambda i: (i, 0))
    return _call(
        body, [dyn, y, proj, w, dproj], name="gate_norm_bwd", grid=(T // tr,),
        in_specs=[tile, tile, tile, pl.BlockSpec((1, DI), lambda i: (0, 0)), pl.BlockSpec(memory_space=pl.ANY)],
        out_specs=[tile, tile, pl.BlockSpec((8, DI), lambda i: (0, 0))],
        out_shape=[jax.ShapeDtypeStruct((T, DI), F32), jax.ShapeDtypeStruct(dproj.shape, BF16),
                   jax.ShapeDtypeStruct((8, DI), F32)],
        aliases={4: 1}, sem=("arbitrary",))


def _pool_fwd(proj, pool_w_b, pool_scale):
    T = proj.shape[0]
    tr = _row_tile(T)
    nb = tr // 16

    def body(u_ref, h_ref, pw_ref, ps_ref, pooled_ref, pw_out_ref, yps_ref):
        i = pl.program_id(0)
        t = i * tr + lax.broadcasted_iota(jnp.int32, (tr, 1), 0)
        for g, win in enumerate(POOL_WINDOWS):
            gs = slice(GW * g, GW * (g + 1))
            u = u_ref[:, gs]
            s = jnp.concatenate([jnp.where(i > 0, h_ref[:, gs], 0.0), u], axis=0)
            sh = 1
            while sh < win:
                s = s + pltpu.roll(s, sh, 0)
                sh *= 2
            pooled = (s[16:] * (1.0 / jnp.minimum(t + 1, win).astype(F32)) - u).astype(BF16)
            pooled_ref[:, gs] = pooled
            pwv = _nn(pooled, pw_ref[g])
            pw_out_ref[:, gs] = pwv
            yps_ref[:, gs] = (pwv * ps_ref[:, gs]).astype(BF16)

    tile = pl.BlockSpec((tr, D), lambda i: (i, 0))
    return _call(
        body, [proj, proj, pool_w_b, pool_scale], name="pool_fwd", grid=(T // tr,),
        in_specs=[pl.BlockSpec((tr, D), lambda i: (i, OFF_POOL // D)),
                  pl.BlockSpec((16, D), lambda i: (jnp.maximum(i * nb - 1, 0), OFF_POOL // D)),
                  pl.BlockSpec((4, GW, GW), lambda i: (0, 0, 0)),
                  pl.BlockSpec((1, D), lambda i: (0, 0))],
        out_specs=[tile, tile, tile],
        out_shape=[jax.ShapeDtypeStruct((T, D), BF16), jax.ShapeDtypeStruct((T, D), F32),
                   jax.ShapeDtypeStruct((T, D), BF16)], sem=("parallel",))


def _pool_bwd(dyp, pw_out, pooled, pool_w_b, pool_scale, dproj):
    T = dyp.shape[0]
    tr = _row_tile(T)
    nb, last = tr // 16, T // tr - 1

    def body(d_ref, h_ref, pwo_ref, pooled_ref, pw_ref, ps_ref, dp_in, du_ref, gpw_ref, sums_ref):
        del dp_in
        i = pl.program_id(0)

        @pl.when(i == 0)
        def _():
            gpw_ref[...] = jnp.zeros_like(gpw_ref)
            sums_ref[...] = jnp.zeros_like(sums_ref)

        n = tr + 16
        t = i * tr + lax.broadcasted_iota(jnp.int32, (n, 1), 0)
        sums_ref[0:1, :] += jnp.sum(d_ref[...] * pwo_ref[...], axis=0, keepdims=True)
        for g, win in enumerate(POOL_WINDOWS):
            gs = slice(GW * g, GW * (g + 1))
            d_ext = jnp.concatenate([d_ref[:, gs], jnp.where(i < last, h_ref[:, gs], 0.0)], axis=0)
            dpw = (d_ext * ps_ref[:, gs]).astype(BF16)
            dpooled = _nt(dpw, pw_ref[g])
            s = jnp.where(t < T, dpooled * (1.0 / jnp.minimum(t + 1, win).astype(F32)), 0.0)
            sh = 1
            while sh < win:
                s = s + pltpu.roll(s, n - sh, 0)
                sh *= 2
            du_ref[:, gs] = (s[:tr] - dpooled[:tr]).astype(BF16)
            gpw_ref[g] += _tn(pooled_ref[:, gs], dpw[:tr])

    tile = pl.BlockSpec((tr, D), lambda i: (i, 0))
    return _call(
        body, [dyp, dyp, pw_out, pooled, pool_w_b, pool_scale, dproj], name="pool_bwd", grid=(T // tr,),
        in_specs=[tile, pl.BlockSpec((16, D), lambda i: (jnp.minimum((i + 1) * nb, T // 16 - 1), 0)), tile, tile,
                  pl.BlockSpec((4, GW, GW), lambda i: (0, 0, 0)), pl.BlockSpec((1, D), lambda i: (0, 0)),
                  pl.BlockSpec(memory_space=pl.ANY)],
        out_specs=[pl.BlockSpec((tr, D), lambda i: (i, OFF_POOL // D)),
                   pl.BlockSpec((4, GW, GW), lambda i: (0, 0, 0)), pl.BlockSpec((8, D), lambda i: (0, 0))],
        out_shape=[jax.ShapeDtypeStruct(dproj.shape, BF16), jax.ShapeDtypeStruct((4, GW, GW), F32),
                   jax.ShapeDtypeStruct((8, D), F32)],
        aliases={6: 0}, sem=("arbitrary",))


def _merge(proj, y_ssd, y_pool):
    T = proj.shape[0]
    tr = _row_tile(T)

    def body(g_ref, a_ref, b_ref, o_ref):
        o_ref[...] = (_sigmoid(g_ref[:, 0:D]) * a_ref[...] + _sigmoid(g_ref[:, D:2 * D]) * b_ref[...]).astype(BF16)

    tile = pl.BlockSpec((tr, D), lambda i: (i, 0))
    return _call(body, [proj, y_ssd, y_pool], name="merge", grid=(T // tr,),
                 in_specs=[pl.BlockSpec((tr, 2 * D), lambda i: (i, OFF_GATE // (2 * D))), tile, tile], out_specs=[tile],
                 out_shape=[jax.ShapeDtypeStruct((T, D), BF16)], sem=("parallel",))[0]


def _merge_bwd(dmerged, proj, y_ssd, y_pool):
    T = proj.shape[0]
    tr = _row_tile(T)

    def body(d_ref, g_ref, a_ref, b_ref, da_ref, db_ref, dg_ref):
        d = d_ref[...]
        ga, gb = _sigmoid(g_ref[:, 0:D]), _sigmoid(g_ref[:, D:2 * D])
        da_ref[...] = (d * ga).astype(BF16)
        db_ref[...] = (d * gb).astype(BF16)
        dg_ref[:, 0:D] = (d * a_ref[...] * ga * (1.0 - ga)).astype(BF16)
        dg_ref[:, D:2 * D] = (d * b_ref[...] * gb * (1.0 - gb)).astype(BF16)

    tile = pl.BlockSpec((tr, D), lambda i: (i, 0))
    gates = pl.BlockSpec((tr, 2 * D), lambda i: (i, OFF_GATE // (2 * D)))
    return _call(body, [dmerged, proj, y_ssd, y_pool], name="merge_bwd", grid=(T // tr,),
                 in_specs=[tile, gates, tile, tile], out_specs=[tile, tile, gates],
                 out_shape=[jax.ShapeDtypeStruct((T, D), BF16), jax.ShapeDtypeStruct((T, D), BF16),
                            jax.ShapeDtypeStruct((T, NP), BF16)], sem=("parallel",))


def _adamw(w, g, m, v, name, carry=None):
    R, C = w.shape
    tr = R if R <= 128 else 128
    assert R % tr == 0

    def body(w_ref, g_ref, m_ref, v_ref, d_ref, mo_ref, vo_ref):
        gv = g_ref[...]
        mn = ADAM_B1 * m_ref[...] + (1.0 - ADAM_B1) * gv
        vn = ADAM_B2 * v_ref[...] + (1.0 - ADAM_B2) * (gv * gv)
        m_hat = mn * (1.0 / (1.0 - ADAM_B1 ** ADAM_STEP))
        v_hat = vn * (1.0 / (1.0 - ADAM_B2 ** ADAM_STEP))
        d_ref[...] = -ADAM_LR * (m_hat / (jnp.sqrt(v_hat) + ADAM_EPS) + ADAM_WD * w_ref[...])
        mo_ref[...] = mn
        vo_ref[...] = vn

    tile = pl.BlockSpec((tr, C), lambda i: (i, 0))
    sds = jax.ShapeDtypeStruct((R, C), F32)
    return _call(body, [w, g, m, v], name=name, grid=(R // tr,), in_specs=[tile] * 4, out_specs=[tile] * 3,
                 out_shape=[sds] * 3, sem=("parallel",), carry=carry)


def _me():
    return lax.axis_index("x"), lax.axis_index("y"), lax.axis_index("c")


def _xor_peer(x, y, c, p):
    return (x ^ ((p >> 2) & 1), y ^ ((p >> 1) & 1), c ^ (p & 1))


def _ada_fwd(c_row, w_ada, b_ada_mine, carry=None):
    n_cols = w_ada.shape[1]

    def body(c_ref, w_ref, b_ref, mod_ref, c8_ref, csend, mpart, modbuf, send_sems, recv_sems):
        x, y, c = _me()
        me = 4 * x + 2 * y + c
        chip = 2 * x + y
        csend[...] = jnp.broadcast_to(c_ref[...], csend.shape)
        c8_ref[me] = csend[...]

        def c_copy(p):
            return pltpu.make_async_remote_copy(
                src_ref=csend, dst_ref=c8_ref.at[me], send_sem=send_sems.at[p - 1], recv_sem=recv_sems.at[p - 1],
                device_id=_xor_peer(x, y, c, p), device_id_type=MESH)

        for p in range(1, 8):
            c_copy(p).start()
        for p in range(1, 8):
            c_copy(p).wait_recv()
        cs = jnp.concatenate([c8_ref[d][0:1, :] for d in range(8)], axis=0)
        mpart[...] = _nn(cs * _sigmoid(cs), w_ref[...], precision=HIGH) + b_ref[...]
        modbuf[chip] = mpart[...]

        def m_copy(m):
            return pltpu.make_async_remote_copy(
                src_ref=mpart, dst_ref=modbuf.at[chip], send_sem=send_sems.at[6 + m], recv_sem=recv_sems.at[6 + m],
                device_id=_xor_peer(x, y, c, 2 * m), device_id_type=MESH)

        for m in range(1, 4):
            m_copy(m).start()
        for m in range(1, 4):
            m_copy(m).wait_recv()
        mine = lax.broadcasted_iota(jnp.int32, (8, 1), 0) == me
        for k in range(N_CHIPS):
            mod_ref[:, n_cols * k:n_cols * (k + 1)] = jnp.sum(jnp.where(mine, modbuf[k], 0.0), axis=0, keepdims=True)
        for p in range(1, 8):
            c_copy(p).wait_send()
        for m in range(1, 4):
            m_copy(m).wait_send()

    vmem = pl.BlockSpec(memory_space=pltpu.VMEM)
    return _call(
        body, [c_row, w_ada, b_ada_mine], name="ada_fwd", in_specs=[vmem, vmem, vmem], out_specs=[vmem, vmem],
        out_shape=[jax.ShapeDtypeStruct((1, N_CHIPS * n_cols), F32), jax.ShapeDtypeStruct((8, 8, D), F32)],
        scratch_shapes=[pltpu.VMEM((8, D), F32), pltpu.VMEM((8, n_cols), F32), pltpu.VMEM((N_CHIPS, 8, n_cols), F32),
                        pltpu.SemaphoreType.DMA((10,)), pltpu.SemaphoreType.DMA((10,))], carry=carry)


def _gather_small(vec, carry=None):
    rows = vec.shape[0]

    def body(v_ref, all_ref, tot_ref, dsk_ref, send_sems, recv_sems):
        x, y, c = _me()
        me = 4 * x + 2 * y + c
        all_ref[me] = v_ref[...]

        def copy(p):
            return pltpu.make_async_remote_copy(
                src_ref=v_ref, dst_ref=all_ref.at[me], send_sem=send_sems.at[p - 1], recv_sem=recv_sems.at[p - 1],
                device_id=_xor_peer(x, y, c, p), device_id_type=MESH)

        for p in range(1, 8):
            copy(p).start()
        for p in range(1, 8):
            copy(p).wait_recv()
        tot = all_ref[0]
        for d in range(1, 8):
            tot = tot + all_ref[d]
        tot_ref[...] = tot
        seg = tot[SMALL_OFF["d_skip"] // 128:SMALL_OFF["d_skip"] // 128 + 16, :]
        lane = lax.broadcasted_iota(jnp.int32, (1, 128), 1)
        sa = jnp.sum(jnp.where(lane < HEAD_DIM, seg, 0.0), axis=1, keepdims=True)
        sb = jnp.sum(jnp.where(lane < HEAD_DIM, 0.0, seg), axis=1, keepdims=True)
        dsk_ref[...] = jnp.where(lane == 0, sa, jnp.where(lane == 1, sb, 0.0))
        for p in range(1, 8):
            copy(p).wait_send()

    vmem = pl.BlockSpec(memory_space=pltpu.VMEM)
    return _call(
        body, [vec], name="gather_small", in_specs=[vmem], out_specs=[vmem, vmem, vmem],
        out_shape=[jax.ShapeDtypeStruct((8, rows, 128), F32), jax.ShapeDtypeStruct((rows, 128), F32),
                   jax.ShapeDtypeStruct((16, 128), F32)],
        scratch_shapes=[pltpu.SemaphoreType.DMA((7,)), pltpu.SemaphoreType.DMA((7,))], carry=carry)


def _gather_carry(shards):
    n = len(shards)

    def copies(ins, outs, sems):
        x, y, c = _me()
        chip = 2 * x + y

        def half(w, which):
            h = shards[w].shape[0] // 2
            return pl.ds(which * h, h)

        def first(w, m):
            return pltpu.make_async_remote_copy(
                src_ref=ins[w].at[half(w, c)], dst_ref=outs[w].at[chip, half(w, c)],
                send_sem=sems.send(6 * w + m - 1), recv_sem=sems.recv(6 * w + m - 1),
                device_id=_xor_peer(x, y, c, 2 * m), device_id_type=MESH)

        def landed(w, m):
            return pltpu.make_async_remote_copy(
                src_ref=ins[w].at[half(w, c)], dst_ref=outs[w].at[chip ^ m, half(w, c)],
                send_sem=sems.send(6 * w + m - 1), recv_sem=sems.recv(6 * w + m - 1),
                device_id=_xor_peer(x, y, c, 2 * m), device_id_type=MESH)

        def passed(w, m, which):
            part = outs[w].at[chip ^ m, half(w, which)]
            return pltpu.make_async_remote_copy(
                src_ref=part, dst_ref=part, send_sem=sems.send(6 * w + 2 + m), recv_sem=sems.recv(6 * w + 2 + m),
                device_id=(x, y, 1 - c), device_id_type=MESH)

        return c, first, landed, passed

    pairs = [(w, m) for w in range(n) for m in range(1, 4)]

    def start(ins, outs, sems):
        _, first, _, _ = copies(ins, outs, sems)
        for w, m in pairs:
            first(w, m).start()

    def finish(ins, outs, sems):
        c, first, landed, passed = copies(ins, outs, sems)
        for w, m in pairs:
            landed(w, m).wait_recv()
            passed(w, m, c).start()
        for w, m in pairs:
            passed(w, m, 1 - c).wait_recv()
        for w, m in pairs:
            first(w, m).wait_send()
            passed(w, m, c).wait_send()

    return _Carry(shards, [jax.ShapeDtypeStruct((N_CHIPS,) + s.shape, s.dtype) for s in shards], 6 * n, start, finish)


def _pair_exchange_carry(grads):
    n = len(grads)

    def copy(ins, outs, sems, w):
        x, y, c = _me()
        h = grads[w].shape[1] // 2
        return pltpu.make_async_remote_copy(
            src_ref=ins[w].at[:, pl.ds((1 - c) * h, h)], dst_ref=outs[w],
            send_sem=sems.send(w), recv_sem=sems.recv(w), device_id=(x, y, 1 - c), device_id_type=MESH)

    def start(ins, outs, sems):
        for w in range(n):
            copy(ins, outs, sems, w).start()

    def finish(ins, outs, sems):
        for w in range(n):
            copy(ins, outs, sems, w).wait()

    return _Carry(grads, [jax.ShapeDtypeStruct((N_CHIPS, g.shape[1] // 2, g.shape[2]), g.dtype) for g in grads], n,
                  start, finish)


def _chip_exchange_carry(partials):
    n = len(partials)

    def copier(ins, outs, sems):
        x, y, c = _me()
        chip = 2 * x + y

        def copy(w, m, landed):
            return pltpu.make_async_remote_copy(
                src_ref=ins[w].at[chip ^ m], dst_ref=outs[w].at[(chip ^ m) if landed else chip],
                send_sem=sems.send(3 * w + m - 1), recv_sem=sems.recv(3 * w + m - 1),
                device_id=_xor_peer(x, y, c, 2 * m), device_id_type=MESH)

        return copy

    pairs = [(w, m) for w in range(n) for m in range(1, 4)]

    def start(ins, outs, sems):
        copy = copier(ins, outs, sems)
        for w, m in pairs:
            copy(w, m, False).start()

    def finish(ins, outs, sems):
        copy = copier(ins, outs, sems)
        for w, m in pairs:
            copy(w, m, True).wait_recv()
        for w, m in pairs:
            copy(w, m, False).wait_send()

    return _Carry(partials, [jax.ShapeDtypeStruct(p.shape, p.dtype) for p in partials], 3 * n, start, finish)


def _pair_share_carry(shards):
    n = len(shards)

    def copier(ins, outs, sems):
        x, y, c = _me()

        def copy(w, which):
            h = shards[w].shape[0] // 2
            rows = pl.ds(which * h, h)
            return pltpu.make_async_remote_copy(
                src_ref=ins[w].at[rows], dst_ref=outs[w].at[rows],
                send_sem=sems.send(w), recv_sem=sems.recv(w), device_id=(x, y, 1 - c), device_id_type=MESH)

        return c, copy

    def start(ins, outs, sems):
        c, copy = copier(ins, outs, sems)
        for w in range(n):
            copy(w, c).start()

    def finish(ins, outs, sems):
        c, copy = copier(ins, outs, sems)
        for w in range(n):
            copy(w, 1 - c).wait_recv()
        for w in range(n):
            copy(w, c).wait_send()

    return _Carry(shards, [jax.ShapeDtypeStruct(s.shape, s.dtype) for s in shards], n, start, finish,
                  aliased=[(w, w) for w in range(n)])


def _pair_sum(g, part, idx, name):
    _, h, C = part.shape
    tr = min(512, h)
    nb = h // tr

    def body(idx_ref, g_ref, p_ref, o16_ref, own_ref):
        v = g_ref[...].astype(F32) + p_ref[...].astype(F32)
        o16_ref[...] = v.astype(BF16)

        @pl.when(pl.program_id(1) == idx_ref[1])
        def _():
            own_ref[...] = v

    return pl.pallas_call(
        body, name=name,
        grid_spec=pltpu.PrefetchScalarGridSpec(
            num_scalar_prefetch=1, grid=(nb, N_CHIPS),
            in_specs=[pl.BlockSpec((None, tr, C), lambda i, s, idx_ref: (s, idx_ref[0] * nb + i, 0)),
                      pl.BlockSpec((None, tr, C), lambda i, s, idx_ref: (s, i, 0))],
            out_specs=[pl.BlockSpec((None, tr, C), lambda i, s, idx_ref: (s, i, 0)),
                       pl.BlockSpec((tr, C), lambda i, s, idx_ref: (i, 0))]),
        out_shape=[jax.ShapeDtypeStruct(part.shape, BF16), jax.ShapeDtypeStruct((h, C), F32)],
        compiler_params=pltpu.CompilerParams(dimension_semantics=("arbitrary", "arbitrary"), vmem_limit_bytes=VMEM_LIMIT),
    )(idx, g, part)


def _chip_sum(own, slots, idx, name):
    h, C = own.shape
    tr = min(512, h)
    nb = h // tr

    def body(idx_ref, own_ref, s1_ref, s2_ref, s3_ref, o_ref):
        del idx_ref
        o_ref[...] = ((own_ref[...] + s1_ref[...].astype(F32)) + s2_ref[...].astype(F32)) + s3_ref[...].astype(F32)

    def slot(m):
        return pl.BlockSpec((None, tr, C), lambda i, idx_ref: (idx_ref[1] ^ m, i, 0))

    return pl.pallas_call(
        body, name=name,
        grid_spec=pltpu.PrefetchScalarGridSpec(
            num_scalar_prefetch=1, grid=(nb,),
            in_specs=[pl.BlockSpec((tr, C), lambda i, idx_ref: (i, 0)), slot(1), slot(2), slot(3)],
            out_specs=pl.BlockSpec((tr, C), lambda i, idx_ref: (idx_ref[0] * nb + i, 0))),
        out_shape=jax.ShapeDtypeStruct((2 * h, C), F32),
        compiler_params=pltpu.CompilerParams(dimension_semantics=("parallel",), vmem_limit_bytes=VMEM_LIMIT),
    )(idx, own, slots, slots, slots)


class _Reducer:
    def __init__(self, idx):
        self.idx, self.chips, self.p16, self.own, self.mine, self.final = idx, {}, {}, {}, {}, {}

    def add(self, name, whole, chip_blocks=False):
        self.chips[name] = whole if chip_blocks else _chips_from_whole(name, whole)

    def pair(self, names):
        return _pair_exchange_carry([self.chips[n] for n in names])

    def take_pair(self, names, outs):
        for n, part in zip(names, outs):
            self.p16[n], self.own[n] = _pair_sum(self.chips.pop(n), part, self.idx, "pair_sum_" + n)

    def chip(self, names):
        return _chip_exchange_carry([self.p16[n] for n in names])

    def take_chip(self, names, outs):
        for n, slots in zip(names, outs):
            del self.p16[n]
            self.mine[n] = _chip_sum(self.own.pop(n), slots, self.idx, "chip_sum_" + n)

    def share(self, names):
        return _pair_share_carry([self.mine[n] for n in names])

    def take_share(self, names, outs):
        for n, s in zip(names, outs):
            del self.mine[n]
            self.final[n] = s


def _w_ada_grad(c8, dmod_cols):
    n_cols = dmod_cols.shape[1]
    tn = 512

    def body(c_ref, d_ref, o_ref):
        cv = c_ref[...]
        o_ref[...] = _tn(cv * _sigmoid(cv), d_ref[...], precision=HIGH)

    return _call(body, [c8, dmod_cols], name="w_ada_grad", grid=(n_cols // tn,),
                 in_specs=[pl.BlockSpec((8, D), lambda j: (0, 0)), pl.BlockSpec((8, tn), lambda j: (0, j))],
                 out_specs=[pl.BlockSpec((D, tn), lambda j: (0, j))],
                 out_shape=[jax.ShapeDtypeStruct((D, n_cols), F32)], sem=("parallel",))[0]


_SMALL_SEGS = (("dmod", 6144), ("norm_mix_w", 1024), ("conv_b", 3072), ("ssd_norm_w", 2048), ("pool_scale", 1024),
               ("norm_mlp_w", 1024), ("norm_final_w", 1024), ("conv_w", 4 * XBC), ("d_skip", 2048), ("a_log", 128),
               ("dt_bias", 128), ("loss", 128))
SMALL_OFF = {}
_o = 0
for _n, _s in _SMALL_SEGS:
    SMALL_OFF[_n] = _o
    _o += _s
SMALL_LEN = -(-_o // 1024) * 1024

_FIRST = ("w_in", "conv_w")
_LATER = ("w_branch_ssd", "pool_w", "w_branch_pool", "w_out", "w_up", "w_down")
_SMALL_REPLICATED = ("b_ada", "norm_mix_w", "conv_b", "dt_bias", "a_log", "d_skip", "ssd_norm_w", "pool_scale",
                     "norm_mlp_w", "norm_final_w")
_WEIGHTS = ("w_ada", "b_ada", "norm_mix_w", "w_in", "conv_w", "conv_b", "dt_bias", "a_log", "d_skip", "ssd_norm_w",
            "w_branch_ssd", "pool_w", "pool_scale", "w_branch_pool", "w_out", "norm_mlp_w", "w_up", "w_down",
            "norm_final_w")


def _shard_2d(name, a):
    if name == "conv_w":
        return a.reshape(16, -1)
    return (a.reshape(GW, GW) if name == "pool_w" else a.reshape(a.shape[-2], a.shape[-1])).astype(BF16)


def _whole_from_chips(name, g, own, chip):
    g = lax.dynamic_update_slice(g, own[None], (chip, 0, 0))
    if name == "w_in":
        a, b = _DT_IN_CHIP2, _DT_IN_CHIP2 + HEADS
        pad = jnp.zeros((D, NP - IN_COLS), g.dtype)
        return jnp.concatenate([g[0], g[1], g[2][:, :a], g[2][:, b:], g[3], g[2][:, a:b], pad], axis=1)
    if name == "w_up":
        return jnp.concatenate([g[k] for k in range(N_CHIPS)], axis=1)
    if name == "pool_w":
        return jnp.transpose(g.reshape(N_CHIPS, 4, GW // N_CHIPS, GW), (1, 0, 2, 3)).reshape(4, GW, GW)
    if name == "conv_w":
        return jnp.transpose(g.reshape(N_CHIPS, 4, XBC // N_CHIPS), (1, 0, 2)).reshape(4, XBC)
    return g.reshape(N_CHIPS * g.shape[1], g.shape[2])


def _chips_from_whole(name, g):
    if name.startswith("w_in"):
        cw, a = IN_COLS // N_CHIPS, _DT_IN_CHIP2
        chip2 = jnp.concatenate([g[:, 2 * cw:2 * cw + a], g[:, OFF_DT:OFF_DT + HEADS], g[:, 5120:3 * cw - HEADS]], axis=1)
        return jnp.stack([g[:, :cw], g[:, cw:2 * cw], chip2, g[:, 3 * cw - HEADS:OFF_DT]])
    if name == "w_up":
        return jnp.transpose(g.reshape(D, N_CHIPS, DFF // N_CHIPS), (1, 0, 2))
    if name == "pool_w":
        return jnp.transpose(g.reshape(4, N_CHIPS, GW // N_CHIPS, GW), (1, 0, 2, 3)).reshape(N_CHIPS, GW, GW)
    return g.reshape(N_CHIPS, g.shape[0] // N_CHIPS, g.shape[1])


def kernel(x, c, w_ada, b_ada, norm_mix_w, w_in, conv_w, conv_b, dt_bias, a_log, d_skip, ssd_norm_w, w_branch_ssd, pool_w, pool_scale, w_branch_pool, w_out, norm_mlp_w, w_up, w_down, norm_final_w, loss_target, m_w_ada, m_b_ada, m_norm_mix_w, m_w_in, m_conv_w, m_conv_b, m_dt_bias, m_a_log, m_d_skip, m_ssd_norm_w, m_w_branch_ssd, m_pool_w, m_pool_scale, m_w_branch_pool, m_w_out, m_norm_mlp_w, m_w_up, m_w_down, m_norm_final_w, v_w_ada, v_b_ada, v_norm_mix_w, v_w_in, v_conv_w, v_conv_b, v_dt_bias, v_a_log, v_d_skip, v_ssd_norm_w, v_w_branch_ssd, v_pool_w, v_pool_scale, v_w_branch_pool, v_w_out, v_norm_mlp_w, v_w_up, v_w_down, v_norm_final_w):
    args = locals()
    w = {n: args[n] for n in _WEIGHTS}
    m = {n: args["m_" + n] for n in _WEIGHTS}
    v = {n: args["v_" + n] for n in _WEIGHTS}
    xi, yi, ci = _me()
    chip = 2 * xi + yi
    idx = jnp.stack([ci, chip]).astype(jnp.int32)
    ada_cols = w_ada.shape[-1]
    xs, target = x[0], loss_target[0]
    two_d = lambda n, a: a.reshape(GW, GW) if n == "pool_w" else a.reshape(-1, a.shape[-1])
    delta, new_m, new_v, g = {}, {}, {}, {}

    def adamw(n, carry=None):
        res = _adamw(two_d(n, w[n]), two_d(n, g[n]), two_d(n, m[n]), two_d(n, v[n]), "adamw_" + n, carry=carry)
        (delta[n], new_m[n], new_v[n]), extra = res if carry is not None else (res, None)
        return extra

    b_mine = lax.dynamic_slice(b_ada, (0, chip * ada_cols), (1, ada_cols))
    shards = {n: _shard_2d(n, w[n]) for n in _FIRST + _LATER}
    mod, c8 = _ada_fwd(c, w_ada[0], b_mine)
    c8 = c8[:, 0, :]
    shift_m, scale_m, gate_m, shift_f, scale_f, gate_f = [mod[:, D * i:D * (i + 1)] for i in range(6)]
    nf_w = norm_final_w.reshape(1, D)

    h1, first = _norm_mod(xs, norm_mix_w, scale_m, shift_m, "norm_mod_mix",
                          carry=_gather_carry([shards[n] for n in _FIRST]))
    p ={n: _whole_from_chips(n, a, shards[n], chip) for n, a in zip(_FIRST, first)}
    (proj,), later = _matmul(h1, p["w_in"], mode="nn", out_dtypes=[F32], name="mm_proj", cols_outer=True,
                             carry=_gather_carry([shards[n] for n in _LATER]))
    p.update({n: _whole_from_chips(n, a, shards[n], chip) for n, a in zip(_LATER, later)})
    xbc_a = _conv_fwd(proj, p["conv_w"], conv_b)
    dtb_c, alog_c = dt_bias.reshape(HEADS, 1), a_log.reshape(HEADS, 1)
    dsk_exp = jnp.repeat(d_skip, HEAD_DIM, axis=1)
    y, hin = _ssd_fwd(xbc_a, proj, dt_bias, a_log, dtb_c, alog_c, dsk_exp)
    yn = _gate_norm(y, proj, ssd_norm_w)
    (y_ssd,) = _matmul(yn, p["w_branch_ssd"], mode="nn", out_dtypes=[F32], name="mm_branch_ssd")
    pooled, pw_out, yps = _pool_fwd(proj, p["pool_w"], pool_scale)
    (y_pool,) = _matmul(yps, p["w_branch_pool"], mode="nn", out_dtypes=[F32], name="mm_branch_pool")
    merged = _merge(proj, y_ssd, y_pool)
    resid = lambda acc, r, gt: (r + gt * acc, acc)
    x2, mix = _matmul(merged, p["w_out"], mode="nn", out_dtypes=[F32, BF16], name="mm_out",
                      epi=resid, tile_extras=(xs,), row_extras=(gate_m,))
    h2 = _norm_mod(x2, norm_mlp_w, scale_f, shift_f, "norm_mod_mlp")
    relu2 = lambda acc: (jnp.square(jnp.maximum(acc, 0.0)),)
    (act,) = _matmul(h2, p["w_up"], mode="nn", out_dtypes=[BF16], name="mm_up", epi=relu2)
    x3, down = _matmul(act, p["w_down"], mode="nn", out_dtypes=[F32, BF16], name="mm_down",
                       epi=resid, tile_extras=(x2,), row_extras=(gate_f,))

    red = _Reducer(idx)
    dx3, d_down, sums_f = _final_loss_bwd(x3, target, nf_w, down, gate_f)
    drelu2 = lambda acc, a: (acc * (2.0 * jnp.sqrt(a)).astype(F32),)
    (dup,) = _matmul(d_down, p["w_down"], mode="nt", out_dtypes=[BF16], name="mm_dact",
                     epi=drelu2, tile_extras=(act,))
    red.add("w_down", _matmul(act, d_down, mode="tn", out_dtypes=[BF16], name="mm_g_down")[0])
    (dh2,), got = _matmul(dup, p["w_up"], mode="nt", out_dtypes=[F32], name="mm_dh2",
                          carry=red.pair(["w_down"]))
    red.take_pair(["w_down"], got)
    red.add("w_up", _matmul(h2, dup, mode="tn", out_dtypes=[BF16], name="mm_g_up", chip_blocks=True)[0], chip_blocks=True)
    dx2, sums_2, dmix = _norm_mod_bwd(x2, dh2, dx3, norm_mlp_w, scale_f, "norm_mod_mlp_bwd", branch=mix, gate=gate_m)
    (dmerged,), got = _matmul(dmix, p["w_out"], mode="nt", out_dtypes=[F32], name="mm_dmerged",
                              carry=red.pair(["w_up"]))
    red.take_pair(["w_up"], got)
    red.add("w_out", _matmul(merged, dmix, mode="tn", out_dtypes=[BF16], name="mm_g_out")[0])
    dy_ssd, dy_pool, dproj = _merge_bwd(dmerged, proj, y_ssd, y_pool)
    (dyp,), got = _matmul(dy_pool, p["w_branch_pool"], mode="nt", out_dtypes=[F32], name="mm_dyp",
                          carry=red.pair(["w_out"]))
    red.take_pair(["w_out"], got)
    red.add("w_branch_pool", _matmul(yps, dy_pool, mode="tn", out_dtypes=[BF16], name="mm_g_bpool")[0])
    dproj, g_pool_w, sums_pool = _pool_bwd(dyp, pw_out, pooled, p["pool_w"], pool_scale, dproj)
    red.add("pool_w", g_pool_w.astype(BF16))
    red.add("w_branch_ssd", _matmul(yn, dy_ssd, mode="tn", out_dtypes=[BF16], name="mm_g_bssd")[0])
    mixers = ["w_branch_pool", "pool_w", "w_branch_ssd"]
    (dyn,), got = _matmul(dy_ssd, p["w_branch_ssd"], mode="nt", out_dtypes=[F32], name="mm_dyn",
                          carry=red.pair(mixers))
    red.take_pair(mixers, got)
    dy, dproj, sums_gn = _gate_norm_bwd(dyn, y, proj, ssd_norm_w, dproj)
    six = ["w_down", "w_up", "w_out"] + mixers
    (dxa, dproj, dsk_sum, ssd_small), got = _ssd_bwd(dy, xbc_a, proj, hin, dt_bias, a_log, dtb_c, alog_c, dsk_exp,
                                                     dproj, carry=red.chip(six))
    red.take_chip(six, got)
    dproj, sums_conv = _conv_bwd(dxa, proj, p["conv_w"], conv_b, dproj)
    rows_a = 3 * D // 4
    (g_in_a,), got = _matmul(h1, dproj, mode="tn", out_dtypes=[BF16], name="mm_g_in_a", a_cols=(0, rows_a),
                             carry=red.share(six))
    red.take_share(six, got)
    red.add("w_in_a", g_in_a)
    (g_in_b,), got = _matmul(h1, dproj, mode="tn", out_dtypes=[BF16], name="mm_g_in_b", a_cols=(rows_a, D - rows_a),
                             carry=red.pair(["w_in_a"]))
    red.take_pair(["w_in_a"], got)
    red.add("w_in_b", g_in_b)
    (dh1,), got = _matmul(dproj, p["w_in"], mode="nt", out_dtypes=[F32], name="mm_dh1",
                          carry=_join(red.chip(["w_in_a"]), red.pair(["w_in_b"])))
    red.take_chip(["w_in_a"], got[:1])
    red.take_pair(["w_in_b"], got[1:])
    grad_x, sums_1 = _norm_mod_bwd(xs, dh1, dx2, norm_mix_w, scale_m, "norm_mod_mix_bwd")

    dmod = jnp.concatenate([sums_1[0:1], sums_1[1:2], sums_2[3:4], sums_2[0:1], sums_2[1:2], sums_f[1:2]], axis=1)
    pad96 = jnp.zeros((1, 96), F32)
    small = {"dmod": dmod, "norm_mix_w": sums_1[2:3], "conv_b": sums_conv[4:5], "ssd_norm_w": sums_gn[0:1],
             "pool_scale": sums_pool[0:1], "norm_mlp_w": sums_2[2:3], "norm_final_w": sums_f[0:1],
             "conv_w": sums_conv[0:4].reshape(1, 4 * XBC), "d_skip": dsk_sum[0:1],
             "a_log": jnp.concatenate([ssd_small[0:1], pad96], axis=1),
             "dt_bias": jnp.concatenate([ssd_small[1:2], pad96], axis=1), "loss": sums_f[3:4, 0:128]}
    vec = jnp.concatenate([small[n] for n, _ in _SMALL_SEGS], axis=1)
    vec = jnp.pad(vec, ((0, 0), (0, SMALL_LEN - vec.shape[1]))).reshape(SMALL_LEN // 128, 128)
    (every, total, dsk), got = _gather_small(vec, carry=_join(red.chip(["w_in_b"]), red.share(["w_in_a"])))
    red.take_chip(["w_in_b"], got[:1])
    red.take_share(["w_in_a"], got[1:])
    total = total.reshape(1, SMALL_LEN)
    seg = lambda n, size: total[:, SMALL_OFF[n]:SMALL_OFF[n] + size]
    g.update({"b_ada": seg("dmod", 6 * D), "norm_mix_w": seg("norm_mix_w", D), "conv_b": seg("conv_b", XBC),
              "dt_bias": seg("dt_bias", HEADS), "a_log": seg("a_log", HEADS), "d_skip": dsk[:, 0:2].reshape(1, HEADS),
              "ssd_norm_w": seg("ssd_norm_w", DI), "pool_scale": seg("pool_scale", D),
              "norm_mlp_w": seg("norm_mlp_w", D), "norm_final_w": seg("norm_final_w", D)})
    loss = total[0, SMALL_OFF["loss"]]
    conv_cols = conv_w.shape[-1]
    g["conv_w"] = lax.dynamic_slice(seg("conv_w", 4 * XBC).reshape(4, XBC), (0, chip * conv_cols), (4, conv_cols))
    dmod8 = every.reshape(8, SMALL_LEN)[:, SMALL_OFF["dmod"]:SMALL_OFF["dmod"] + 6 * D]
    g["w_ada"] = _w_ada_grad(c8, lax.dynamic_slice(dmod8, (0, chip * ada_cols), (8, ada_cols)))

    got = adamw("w_ada", carry=red.share(["w_in_b"]))
    red.take_share(["w_in_b"], got)
    for n in six:
        g[n] = red.final[n]
    g["w_in"] = jnp.concatenate([red.final["w_in_a"], red.final["w_in_b"]], axis=0)
    for n in ["conv_w", "w_in"] + six:
        adamw(n)
    sizes = [w[n].size for n in _SMALL_REPLICATED]
    n_small = -(-sum(sizes) // 1024) * 1024
    pack = lambda d: jnp.pad(jnp.concatenate([d[n].reshape(1, -1) for n in _SMALL_REPLICATED], axis=1),
                             ((0, 0), (0, n_small - sum(sizes)))).reshape(n_small // 128, 128)
    d_, m_, v_ = _adamw(pack(w), pack(g), pack(m), pack(v), "adamw_small")
    off = 0
    for n, s in zip(_SMALL_REPLICATED, sizes):
        for dst, src in ((delta, d_), (new_m, m_), (new_v, v_)):
            dst[n] = src.reshape(1, n_small)[:, off:off + s]
        off += s

    out = [loss, grad_x.reshape(x.shape)]
    for d in (g, delta, new_m, new_v):
        out += [d[n].reshape(w[n].shape) for n in _WEIGHTS]
    return tuple(out)
```

```python
import functools
import operator

import jax
import jax.numpy as jnp
import numpy as np
from jax import lax
from jax.experimental import pallas as pl
from jax.experimental.pallas import tpu as pltpu

F32, BF16 = jnp.float32, jnp.bfloat16
HIGH = lax.Precision.HIGHEST
MESH = pl.DeviceIdType.MESH

D = 1024
DI = 2048
HEADS, HEAD_DIM = 32, 64
GROUPS, STATE = 4, 128
Q = 128
XBC = DI + 2 * GROUPS * STATE
POOL_WINDOWS = (2, 4, 8, 16)
GW = 256
DFF = 4096
EPS = 1e-5
IN_COLS = 8224
OFF_Z, OFF_XBC, OFF_POOL, OFF_GATE, OFF_DT, NP = 0, 2048, 5120, 6144, 8192, 8448
N_CHIPS = 4
ADAM_LR, ADAM_B1, ADAM_B2, ADAM_EPS, ADAM_WD, ADAM_STEP = 0.001, 0.9, 0.999, 1e-08, 0.01, 10
VMEM_LIMIT = 56 * 2 ** 20
NEG = -1e30


def _sigmoid(v):
    return 0.5 * jnp.tanh(0.5 * v) + 0.5


def _softplus(v):
    return jnp.maximum(v, 0.0) + jnp.log1p(jnp.exp(-jnp.abs(v)))


def _dot(a, b, dims, **kw):
    return lax.dot_general(a, b, (dims, ((), ())), preferred_element_type=F32, **kw)


def _nn(a, b, **kw):
    return _dot(a, b, ((1,), (0,)), **kw)


def _nt(a, b, **kw):
    return _dot(a, b, ((1,), (1,)), **kw)


def _tn(a, b, **kw):
    return _dot(a, b, ((0,), (0,)), **kw)


_DT_IN_CHIP2 = 5120 - 2 * (IN_COLS // 4)


class _Sems:
    def __init__(self, send, recv, local, base=0):
        self._send, self._recv, self._local, self._base = send, recv, local, base

    def shift(self, n):
        return _Sems(self._send, self._recv, self._local, self._base + n)

    def send(self, i):
        return self._send.at[self._base + i]

    def recv(self, i):
        return self._recv.at[self._base + i]

    def local(self, i):
        return self._local.at[self._base + i]


class _Carry:
    def __init__(self, ins, out_shapes, n_sems, start, finish, aliased=()):
        self.ins, self.out_shapes, self.n_sems, self.start, self.finish = list(ins), list(out_shapes), n_sems, start, finish
        self.aliased = list(aliased)


def _join(*carries):
    def run(which):
        def fn(ins, outs, sems):
            i = o = s = 0
            for cy in carries:
                getattr(cy, which)(ins[i:i + len(cy.ins)], outs[o:o + len(cy.out_shapes)], sems.shift(s))
                i, o, s = i + len(cy.ins), o + len(cy.out_shapes), s + cy.n_sems
        return fn

    aliased, i, o = [], 0, 0
    for cy in carries:
        aliased += [(i + a, o + b) for a, b in cy.aliased]
        i, o = i + len(cy.ins), o + len(cy.out_shapes)
    return _Carry([a for cy in carries for a in cy.ins], [a for cy in carries for a in cy.out_shapes],
                  sum(cy.n_sems for cy in carries), run("start"), run("finish"), aliased)


def _call(body, args, *, name, grid=(), in_specs, out_specs, out_shape, scratch_shapes=(), sem=None, aliases=None,
          carry=None):
    in_specs, out_specs, out_shape, scratch_shapes = list(in_specs), list(out_specs), list(out_shape), list(scratch_shapes)
    n_in, n_out, n_scr = len(in_specs), len(out_specs), len(scratch_shapes)
    kw = {"vmem_limit_bytes": VMEM_LIMIT}
    if carry is None:
        kernel_fn = functools.partial(body)
        if sem is not None:
            kw["dimension_semantics"] = sem
    else:
        n_ci, n_co = len(carry.ins), len(carry.out_shapes)
        hbm = pl.BlockSpec(memory_space=pl.ANY)
        in_specs += [hbm] * n_ci
        out_specs += [hbm] * n_co
        out_shape += carry.out_shapes
        n_s = max(carry.n_sems, 1)
        scratch_shapes += [pltpu.SemaphoreType.DMA((n_s,))] * 3
        args = list(args) + carry.ins
        aliases = dict(aliases or {})
        aliases.update({n_in + i: n_out + o for i, o in carry.aliased})
        if grid:
            kw["dimension_semantics"] = ("arbitrary",) * len(grid)

        def kernel_fn(*refs):
            a = n_in
            ins, c_ins = refs[:a], refs[a:a + n_ci]
            a += n_ci
            outs, c_outs = refs[a:a + n_out], refs[a + n_out:a + n_out + n_co]
            a += n_out + n_co
            scr, sems = refs[a:a + n_scr], _Sems(*refs[a + n_scr:a + n_scr + 3])
            if grid:
                ids = [pl.program_id(d) for d in range(len(grid))]
                first = functools.reduce(operator.and_, [i == 0 for i in ids])
                last = functools.reduce(operator.and_, [i == g - 1 for i, g in zip(ids, grid)])

                @pl.when(first)
                def _():
                    carry.start(c_ins, c_outs, sems)

                body(*ins, *outs, *scr)

                @pl.when(last)
                def _():
                    carry.finish(c_ins, c_outs, sems)
            else:
                carry.start(c_ins, c_outs, sems)
                body(*ins, *outs, *scr)
                carry.finish(c_ins, c_outs, sems)

    outs = pl.pallas_call(
        kernel_fn, name=name, grid=grid, in_specs=in_specs, out_specs=out_specs, out_shape=out_shape,
        scratch_shapes=scratch_shapes, input_output_aliases=aliases or {},
        compiler_params=pltpu.CompilerParams(**kw),
    )(*args)
    outs = list(outs)
    return outs if carry is None else (outs[:n_out], outs[n_out:])


def _run_carry(carry, name):
    _, outs = _call(lambda: None, [], name=name, in_specs=[], out_specs=[], out_shape=[], carry=carry)
    return outs


_TILES = {
    "mm_proj": (1024, 2816, 1024), "mm_branch_ssd": (1024, 1024, 2048), "mm_branch_pool": (1024, 1024, 1024),
    "mm_out": (1024, 1024, 1024), "mm_up": (2048, 1024, 1024), "mm_down": (512, 1024, 4096),
    "mm_dact": (1024, 1024, 1024), "mm_g_down": (1024, 1024, 4096), "mm_dh2": (1024, 1024, 4096),
    "mm_g_up": (1024, 1024, 4096), "mm_dmerged": (1024, 1024, 1024), "mm_g_out": (1024, 1024, 2048),
    "mm_dyp": (1024, 1024, 1024), "mm_g_bpool": (1024, 1024, 2048), "mm_g_bssd": (1024, 1024, 4096),
    "mm_dyn": (1024, 1024, 1024), "mm_g_in_a": (768, 1408, 4096), "mm_g_in_b": (256, 2816, 2048),
    "mm_dh1": (1024, 1024, 4224),
}


def _matmul(a, b, *, mode, out_dtypes, name, epi=None, tile_extras=(), row_extras=(), carry=None, a_cols=None,
            chip_blocks=False, cols_outer=False):
    M, K = (a.shape[1], a.shape[0]) if mode == "tn" else a.shape
    N = b.shape[0] if mode == "nt" else b.shape[1]
    a_start, M = a_cols if a_cols is not None else (0, M)
    tm, tn, tk = _TILES[name]
    tm, tn, tk = min(tm, M), min(tn, N), min(tk, K)
    assert M % tm == 0 and N % tn == 0 and K % tk == 0 and a_start % tm == 0, (name, M, N, K, tm, tn, tk)
    a_off = a_start // tm
    if mode == "nn":
        a_spec = pl.BlockSpec((tm, tk), lambda i, j, k: (i, k))
        b_spec = pl.BlockSpec((tk, tn), lambda i, j, k: (k, j))
        dims = ((1,), (0,))
    elif mode == "nt":
        a_spec = pl.BlockSpec((tm, tk), lambda i, j, k: (i, k))
        b_spec = pl.BlockSpec((tn, tk), lambda i, j, k: (j, k))
        dims = ((1,), (1,))
    else:
        a_spec = pl.BlockSpec((tk, tm), lambda i, j, k: (k, i + a_off))
        b_spec = pl.BlockSpec((tk, tn), lambda i, j, k: (k, j))
        dims = ((0,), (0,))
    nk = K // tk
    n_te, n_re, n_out = len(tile_extras), len(row_extras), len(out_dtypes)
    if epi is None:
        epi = lambda acc: (acc,)

    def body(a_ref, b_ref, *rest):
        extras = rest[:n_te + n_re]
        outs = rest[n_te + n_re:n_te + n_re + n_out]
        p = _dot(a_ref[...], b_ref[...], dims)

        def finish(acc):
            vals = epi(acc, *[e[...] for e in extras])
            for o, v in zip(outs, vals):
                o[...] = v.astype(o.dtype)

        if nk == 1:
            finish(p)
        else:
            acc_ref = rest[-1]
            k = pl.program_id(2)

            @pl.when(k == 0)
            def _():
                acc_ref[...] = p

            @pl.when(k > 0)
            def _():
                acc_ref[...] += p

            @pl.when(k == nk - 1)
            def _():
                finish(acc_ref[...])

    tile_spec = pl.BlockSpec((tm, tn), lambda i, j, k: (i, j))
    row_spec = pl.BlockSpec((1, tn), lambda i, j, k: (0, j))
    out_spec, out_dims = tile_spec, (M, N)
    if chip_blocks:
        assert n_te == 0 and tn * N_CHIPS == N
        out_spec, out_dims = pl.BlockSpec((None, tm, tn), lambda i, j, k: (j, i, 0)), (N_CHIPS, M, tn)
    in_specs, grid = [a_spec, b_spec] + [tile_spec] * n_te + [row_spec] * n_re, (M // tm, N // tn, nk)
    if cols_outer:
        swap = lambda s: pl.BlockSpec(s.block_shape, lambda g0, g1, k, f=s.index_map: f(g1, g0, k))
        in_specs, out_spec, grid = [swap(s) for s in in_specs], swap(out_spec), (N // tn, M // tm, nk)
    return _call(
        body, [a, b, *tile_extras, *row_extras], name=name, grid=grid,
        in_specs=in_specs, out_specs=[out_spec] * n_out,
        out_shape=[jax.ShapeDtypeStruct(out_dims, dt) for dt in out_dtypes],
        scratch_shapes=[pltpu.VMEM((tm, tn), F32)] if nk > 1 else [],
        sem=("parallel", "parallel", "arbitrary"), carry=carry)


def _row_tile(T):
    return min(512, T)


def _norm_mod(x, nw, scale, shift, name, carry=None):
    T = x.shape[0]
    tr = _row_tile(T)

    def body(x_ref, nw_ref, sc_ref, sh_ref, o_ref):
        xv = x_ref[...]
        r = lax.rsqrt(jnp.mean(xv * xv, axis=-1, keepdims=True) + EPS)
        o_ref[...] = ((xv * r) * nw_ref[...] * (1.0 + sc_ref[...]) + sh_ref[...]).astype(BF16)

    tile = pl.BlockSpec((tr, D), lambda i: (i, 0))
    row = pl.BlockSpec((1, D), lambda i: (0, 0))
    res = _call(body, [x, nw, scale, shift], name=name, grid=(T // tr,), in_specs=[tile, row, row, row],
                out_specs=[tile], out_shape=[jax.ShapeDtypeStruct((T, D), BF16)], sem=("parallel",), carry=carry)
    return res[0] if carry is None else (res[0][0], res[1])


def _norm_mod_bwd(x, dh, dres, nw, scale, name, branch=None, gate=None, carry=None):
    T = x.shape[0]
    tr = _row_tile(T)
    with_branch = branch is not None

    def body(x_ref, dh_ref, dr_ref, nw_ref, sc_ref, *rest):
        if with_branch:
            br_ref, g_ref, dx_ref, sums_ref, db_ref = rest
        else:
            dx_ref, sums_ref = rest
        i = pl.program_id(0)

        @pl.when(i == 0)
        def _():
            sums_ref[...] = jnp.zeros_like(sums_ref)

        xv, dhv = x_ref[...], dh_ref[...]
        r = lax.rsqrt(jnp.mean(xv * xv, axis=-1, keepdims=True) + EPS)
        xn = xv * r
        g1 = dhv * (1.0 + sc_ref[...])
        dxn = g1 * nw_ref[...]
        dx = dr_ref[...] + r * (dxn - xn * jnp.mean(dxn * xn, axis=-1, keepdims=True))
        dx_ref[...] = dx
        sums_ref[0:1, :] += jnp.sum(dhv, axis=0, keepdims=True)
        sums_ref[1:2, :] += jnp.sum(dhv * (xn * nw_ref[...]), axis=0, keepdims=True)
        sums_ref[2:3, :] += jnp.sum(g1 * xn, axis=0, keepdims=True)
        if with_branch:
            db_ref[...] = (dx * g_ref[...]).astype(BF16)
            sums_ref[3:4, :] += jnp.sum(dx * br_ref[...], axis=0, keepdims=True)

    tile = pl.BlockSpec((tr, D), lambda i: (i, 0))
    row = pl.BlockSpec((1, D), lambda i: (0, 0))
    sums = pl.BlockSpec((8, D), lambda i: (0, 0))
    ins = [x, dh, dres, nw, scale] + ([branch, gate] if with_branch else [])
    in_specs = [tile, tile, tile, row, row] + ([tile, row] if with_branch else [])
    out_specs = [tile, sums] + ([tile] if with_branch else [])
    out_shape = [jax.ShapeDtypeStruct((T, D), F32), jax.ShapeDtypeStruct((8, D), F32)]
    if with_branch:
        out_shape.append(jax.ShapeDtypeStruct((T, D), BF16))
    return _call(body, ins, name=name, grid=(T // tr,), in_specs=in_specs, out_specs=out_specs, out_shape=out_shape,
                 sem=("arbitrary",), carry=carry)


def _final_loss_bwd(x3, target, wf, down, gate_f):
    T = x3.shape[0]
    tr = _row_tile(T)
    n_steps = T // tr

    def body(x_ref, t_ref, w_ref, dn_ref, g_ref, dx_ref, dd_ref, sums_ref):
        i = pl.program_id(0)

        @pl.when(i == 0)
        def _():
            sums_ref[...] = jnp.zeros_like(sums_ref)

        xv = x_ref[...]
        r = lax.rsqrt(jnp.mean(xv * xv, axis=-1, keepdims=True) + EPS)
        xn = xv * r
        err = xn * w_ref[...] - t_ref[...]
        dy = err * (1.0 / D)
        dxn = dy * w_ref[...]
        dx = r * (dxn - xn * jnp.mean(dxn * xn, axis=-1, keepdims=True))
        dx_ref[...] = dx
        dd_ref[...] = (dx * g_ref[...]).astype(BF16)
        sums_ref[0:1, :] += jnp.sum(dy * xn, axis=0, keepdims=True)
        sums_ref[1:2, :] += jnp.sum(dx * dn_ref[...], axis=0, keepdims=True)
        sums_ref[2:3, :] += jnp.sum(err * err, axis=0, keepdims=True) * (0.5 / D)

        @pl.when(i == n_steps - 1)
        def _():
            sums_ref[3:4, :] = jnp.broadcast_to(jnp.sum(sums_ref[2:3, :], axis=1, keepdims=True), (1, D))

    tile = pl.BlockSpec((tr, D), lambda i: (i, 0))
    row = pl.BlockSpec((1, D), lambda i: (0, 0))
    sums = pl.BlockSpec((8, D), lambda i: (0, 0))
    return _call(body, [x3, target, wf, down, gate_f], name="final_loss_bwd", grid=(n_steps,),
                 in_specs=[tile, tile, row, tile, row], out_specs=[tile, tile, sums],
                 out_shape=[jax.ShapeDtypeStruct((T, D), F32), jax.ShapeDtypeStruct((T, D), BF16),
                            jax.ShapeDtypeStruct((8, D), F32)], sem=("arbitrary",))


CONV_TC = 1024


def _conv_taps(xp, w, b):
    acc = b + w[3:4, :] * xp
    for k in range(3):
        acc = acc + w[k:k + 1, :] * pltpu.roll(xp, 3 - k, 0)
    return acc


def _conv_fwd(proj, conv_w, conv_b):
    T = proj.shape[0]
    tr = _row_tile(T)
    nb, offb = tr // 8, OFF_XBC // CONV_TC

    def body(x_ref, h_ref, w_ref, b_ref, o_ref):
        halo = jnp.where(pl.program_id(0) > 0, h_ref[...], 0.0)
        xp = jnp.concatenate([halo, x_ref[...]], axis=0)
        acc = _conv_taps(xp, w_ref[...], b_ref[...])[8:]
        o_ref[...] = acc * _sigmoid(acc)

    return _call(
        body, [proj, proj, conv_w, conv_b], name="conv_fwd", grid=(T // tr, XBC // CONV_TC),
        in_specs=[pl.BlockSpec((tr, CONV_TC), lambda i, j: (i, j + offb)),
                  pl.BlockSpec((8, CONV_TC), lambda i, j: (jnp.maximum(i * nb - 1, 0), j + offb)),
                  pl.BlockSpec((4, CONV_TC), lambda i, j: (0, j)),
                  pl.BlockSpec((1, CONV_TC), lambda i, j: (0, j))],
        out_specs=[pl.BlockSpec((tr, CONV_TC), lambda i, j: (i, j))],
        out_shape=[jax.ShapeDtypeStruct((T, XBC), F32)], sem=("parallel", "parallel"))[0]


def _conv_bwd(dxa, proj, conv_w, conv_b, dproj):
    T = proj.shape[0]
    tr = _row_tile(T)
    nb, offb, last = tr // 8, OFF_XBC // CONV_TC, T // tr - 1
    prev8 = lambda i: jnp.maximum(i * nb - 1, 0)
    next8 = lambda i: jnp.minimum((i + 1) * nb, T // 8 - 1)

    def body(d_ref, dn_ref, x_ref, xp_ref, xn_ref, w_ref, b_ref, dp_in, o_ref, sums_ref):
        del dp_in
        i = pl.program_id(1)

        @pl.when(i == 0)
        def _():
            sums_ref[...] = jnp.zeros_like(sums_ref)

        x = jnp.concatenate([jnp.where(i > 0, xp_ref[...], 0.0), x_ref[...], jnp.where(i < last, xn_ref[...], 0.0)], axis=0)
        d = jnp.concatenate([d_ref[...], jnp.where(i < last, dn_ref[...], 0.0)], axis=0)
        w = w_ref[...]
        taps = [pltpu.roll(x, 3 - k, 0)[8:] for k in range(3)] + [x[8:]]
        acc = b_ref[...] + w[3:4, :] * taps[3]
        for k in range(3):
            acc = acc + w[k:k + 1, :] * taps[k]
        s = _sigmoid(acc)
        dxc = d * (s * (1.0 + acc * (1.0 - s)))
        n = tr + 8
        dx = w[3:4, :] * dxc
        for k in range(3):
            dx = dx + w[k:k + 1, :] * pltpu.roll(dxc, n - (3 - k), 0)
        o_ref[...] = dx[:tr].astype(BF16)
        own = dxc[:tr]
        for k in range(4):
            sums_ref[k:k + 1, :] += jnp.sum(own * taps[k][:tr], axis=0, keepdims=True)
        sums_ref[4:5, :] += jnp.sum(own, axis=0, keepdims=True)

    return _call(
        body, [dxa, dxa, proj, proj, proj, conv_w, conv_b, dproj], name="conv_bwd", grid=(XBC // CONV_TC, T // tr),
        in_specs=[pl.BlockSpec((tr, CONV_TC), lambda j, i: (i, j)),
                  pl.BlockSpec((8, CONV_TC), lambda j, i: (next8(i), j)),
                  pl.BlockSpec((tr, CONV_TC), lambda j, i: (i, j + offb)),
                  pl.BlockSpec((8, CONV_TC), lambda j, i: (prev8(i), j + offb)),
                  pl.BlockSpec((8, CONV_TC), lambda j, i: (next8(i), j + offb)),
                  pl.BlockSpec((4, CONV_TC), lambda j, i: (0, j)),
                  pl.BlockSpec((1, CONV_TC), lambda j, i: (0, j)),
                  pl.BlockSpec(memory_space=pl.ANY)],
        out_specs=[pl.BlockSpec((tr, CONV_TC), lambda j, i: (i, j + offb)), pl.BlockSpec((8, CONV_TC), lambda j, i: (0, j))],
        out_shape=[jax.ShapeDtypeStruct(dproj.shape, BF16), jax.ShapeDtypeStruct((8, XBC), F32)],
        aliases={7: 0}, sem=("parallel", "arbitrary"))


def _spread(v, sel, pieces):
    out = None
    for _ in range(pieces):
        p = v.astype(BF16)
        term = _nn(p, sel)
        out = term if out is None else out + term
        v = v - p.astype(F32)
    return out


def _ssd_selectors():
    g = np.arange(GROUPS)[:, None, None]
    piece = np.arange(128)[None, :, None]
    h = np.where(piece < 3 * HEADS, piece % HEADS, -1)
    blocks = (h == 8 * g + np.arange(1024)[None, None, :] // 128)
    pairs = (h == 8 * g + np.arange(512)[None, None, :] // HEAD_DIM)
    lane = np.arange(128)[None, None, :]
    block_sum = (lane == 8 * g + np.arange(1024)[None, :, None] // 128)
    pair_sum = (lane == 8 * g + np.arange(512)[None, :, None] // HEAD_DIM)
    return [jnp.asarray(m, BF16) for m in (blocks, pairs, block_sum, pair_sum)]


def _pack3(v):
    p0 = v.astype(BF16)
    r1 = v - p0.astype(F32)
    p1 = r1.astype(BF16)
    r2 = r1 - p1.astype(F32)
    return p0 + pltpu.roll(r1, HEADS, 1).astype(BF16) + pltpu.roll(r2, 2 * HEADS, 1).astype(BF16)


def _ssd_group(g, cs_p, csT, dt_p, s_mat, causal_w, lo, blocks_ref, pairs_ref):
    csb = _nn(cs_p, blocks_ref[g])
    row = jnp.concatenate([csT[8 * g + hh:8 * g + hh + 1, :] for hh in range(8)], axis=1)
    l_w = jnp.exp(jnp.where(causal_w, csb - row, NEG))
    m_w = jnp.concatenate([s_mat] * 8, axis=1) * l_w
    cs_g = jnp.concatenate([jnp.where(lo, csb[:, 256 * jj:256 * jj + 128], csb[:, 256 * jj + 128:256 * jj + 256])
                            for jj in range(4)], axis=1)
    cs_last = cs_g[Q - 1:Q, :]
    return m_w, l_w, _nn(dt_p, pairs_ref[g]), jnp.exp(cs_g), jnp.exp(cs_last - cs_g), jnp.exp(cs_last)


def _ssd_common(dtp_ref, dtb_r, alog_r, dtb_c, alog_c):
    rows = lax.broadcasted_iota(jnp.int32, (Q, Q), 0)
    cols = lax.broadcasted_iota(jnp.int32, (Q, Q), 1)
    tri = (cols <= rows).astype(F32)
    heads = lax.broadcasted_iota(jnp.int32, (1, 128), 1) < HEADS
    raw_w = dtp_ref[...] + dtb_r[...]
    dt_w = jnp.where(heads, _softplus(raw_w), 0.0)
    a_w = -jnp.exp(alog_r[...])
    cs_w = _nn(tri, dt_w * a_w, precision=HIGH)
    aT = _softplus(dtp_ref[...].T[0:HEADS, :] + dtb_c[...]) * (-jnp.exp(alog_c[...]))
    csT = _nt(aT, tri, precision=HIGH)
    return raw_w[:, 0:HEADS], dt_w[:, 0:HEADS], a_w[:, 0:HEADS], csT, _pack3(cs_w), _pack3(dt_w)


def _ssd_fwd(xbc_a, proj, dtb_r, alog_r, dtb_c, alog_c, dsk_exp, norm_w):
    T = xbc_a.shape[0]
    nc = T // Q
    dtb_r, alog_r = [jnp.pad(a, ((0, 0), (0, 128 - HEADS))) for a in (dtb_r, alog_r)]

    def body(xbc_ref, dtp_ref, z_ref, dtb_r_ref, alog_r_ref, dtb_c_ref, alog_c_ref, dsk_ref, nw_ref, blocks_ref,
             pairs_ref, y_ref, hin_ref, yn_ref, h_scr):
        @pl.when(pl.program_id(0) == 0)
        def _():
            h_scr[...] = jnp.zeros_like(h_scr)

        _, _, _, csT, cs_p, dt_p = _ssd_common(dtp_ref, dtb_r_ref, alog_r_ref, dtb_c_ref, alog_c_ref)
        lo = lax.broadcasted_iota(jnp.int32, (1, 128), 1) < HEAD_DIM
        hi = jnp.logical_not(lo)
        causal_w = (lax.broadcasted_iota(jnp.int32, (Q, 1024), 1) & (Q - 1)) <= lax.broadcasted_iota(jnp.int32, (Q, 1024), 0)
        for g in range(GROUPS):
            gs = slice(512 * g, 512 * (g + 1))
            hs = slice(128 * g, 128 * (g + 1))
            xs_g = xbc_ref[:, gs]
            b_g = xbc_ref[:, DI + STATE * g:DI + STATE * (g + 1)].astype(BF16)
            c_g = xbc_ref[:, DI + 512 + STATE * g:DI + 512 + STATE * (g + 1)].astype(BF16)
            m_w, _, dt_g, ecs_g, dec_g, cd_g = _ssd_group(g, cs_p, csT, dt_p, _nt(c_g, b_g), causal_w, lo, blocks_ref, pairs_ref)
            m_b = m_w.astype(BF16)
            xdt = xs_g * dt_g
            xdt_b = xdt.astype(BF16)
            ys = []
            for jj in range(4):
                xp = xdt_b[:, 128 * jj:128 * (jj + 1)]
                x_ab = jnp.concatenate([jnp.where(lo, xp, jnp.zeros_like(xp)), jnp.where(hi, xp, jnp.zeros_like(xp))], axis=0)
                ys.append(_nn(m_b[:, 256 * jj:256 * (jj + 1)], x_ab))
            h_g = h_scr[hs, :]
            hin_ref[0, hs, :] = h_g
            y_g = jnp.concatenate(ys, axis=1) + _nn(c_g, h_g.astype(BF16)) * ecs_g + dsk_ref[:, gs] * xs_g
            y_ref[:, gs] = y_g
            z = z_ref[:, gs]
            yg = y_g * (z * _sigmoid(z))
            r = lax.rsqrt(jnp.mean(yg * yg, axis=-1, keepdims=True) + EPS)
            yn_ref[:, gs] = (yg * r * nw_ref[:, gs]).astype(BF16)
            h_scr[hs, :] = h_g * cd_g + _tn(b_g, (xdt * dec_g).astype(BF16))

    small_r = pl.BlockSpec((1, 128), lambda c: (0, 0))
    small_c = pl.BlockSpec((HEADS, 1), lambda c: (0, 0))
    blocks, pairs, _, _ = _ssd_selectors()
    whole = lambda a: pl.BlockSpec(a.shape, lambda c: (0,) * a.ndim)
    wide, row = pl.BlockSpec((Q, DI), lambda c: (c, 0)), pl.BlockSpec((1, DI), lambda c: (0, 0))
    return _call(
        body, [xbc_a, proj, proj, dtb_r, alog_r, dtb_c, alog_c, dsk_exp, norm_w, blocks, pairs], name="ssd_fwd", grid=(nc,),
        in_specs=[pl.BlockSpec((Q, XBC), lambda c: (c, 0)),
                  pl.BlockSpec((Q, 128), lambda c: (c, OFF_DT // 128)), wide,
                  small_r, small_r, small_c, small_c, row, row, whole(blocks), whole(pairs)],
        out_specs=[wide, pl.BlockSpec((1, 512, 512), lambda c: (c, 0, 0)), wide],
        out_shape=[jax.ShapeDtypeStruct((T, DI), F32), jax.ShapeDtypeStruct((nc, 512, 512), F32),
                   jax.ShapeDtypeStruct((T, DI), BF16)],
        scratch_shapes=[pltpu.VMEM((512, 512), F32)], sem=("arbitrary",))


def _ssd_bwd(dy, xbc_a, proj, hin, dtb_r, alog_r, dtb_c, alog_c, dsk_exp, dproj, carry=None):
    T = xbc_a.shape[0]
    nc = T // Q
    dtb_r, alog_r = [jnp.pad(a, ((0, 0), (0, 128 - HEADS))) for a in (dtb_r, alog_r)]

    def body(dy_ref, xbc_ref, dtp_ref, hin_ref, dtb_r_ref, alog_r_ref, dtb_c_ref, alog_c_ref, dsk_ref, dp_in,
             blocks_ref, pairs_ref, block_sum_ref, pair_sum_ref, dxa_ref, dp_ref, dsk_sum_ref, small_ref, dh_scr):
        del dp_in

        @pl.when(pl.program_id(0) == 0)
        def _():
            dh_scr[...] = jnp.zeros_like(dh_scr)
            dsk_sum_ref[...] = jnp.zeros_like(dsk_sum_ref)
            small_ref[...] = jnp.zeros_like(small_ref)

        raw, dt, a_r, csT, cs_p, dt_p = _ssd_common(dtp_ref, dtb_r_ref, alog_r_ref, dtb_c_ref, alog_c_ref)
        lo = lax.broadcasted_iota(jnp.int32, (1, 128), 1) < HEAD_DIM
        hi = jnp.logical_not(lo)
        sub32 = lax.broadcasted_iota(jnp.int32, (HEADS, 1), 0)
        causal_w = (lax.broadcasted_iota(jnp.int32, (Q, 1024), 1) & (Q - 1)) <= lax.broadcasted_iota(jnp.int32, (Q, 1024), 0)
        dcs_c = jnp.zeros((Q, 128), F32)
        dcs_r = jnp.zeros((HEADS, Q), F32)
        dcs_l = jnp.zeros((8, 128), F32)
        ddt_x = jnp.zeros((Q, 128), F32)
        for g in range(GROUPS):
            gs = slice(512 * g, 512 * (g + 1))
            hs = slice(128 * g, 128 * (g + 1))
            xs_g, dy_g = xbc_ref[:, gs], dy_ref[:, gs]
            b_g = xbc_ref[:, DI + STATE * g:DI + STATE * (g + 1)].astype(BF16)
            c_g = xbc_ref[:, DI + 512 + STATE * g:DI + 512 + STATE * (g + 1)].astype(BF16)
            m_w, l_w, dt_g, ecs_g, dec_g, cd_g = _ssd_group(g, cs_p, csT, dt_p, _nt(c_g, b_g), causal_w, lo, blocks_ref, pairs_ref)
            m_b = m_w.astype(BF16)
            xdt = xs_g * dt_g
            xdt_b, dy_b = xdt.astype(BF16), dy_g.astype(BF16)
            dms, dxs = [], []
            for jj in range(4):
                xp, dyp = xdt_b[:, 128 * jj:128 * (jj + 1)], dy_b[:, 128 * jj:128 * (jj + 1)]
                dy_ab = jnp.concatenate([jnp.where(lo, dyp, jnp.zeros_like(dyp)), jnp.where(hi, dyp, jnp.zeros_like(dyp))], axis=0)
                dm_ab = _nt(dy_ab, xp)
                dms += [dm_ab[:Q], dm_ab[Q:]]
                dx_ab = _tn(m_b[:, 256 * jj:256 * (jj + 1)], dyp)
                dxs.append(jnp.where(lo, dx_ab[:Q], dx_ab[Q:]))
            dm_w = jnp.concatenate(dms, axis=1)
            w_w = dm_w * m_w
            dcs_c = dcs_c + _spread(w_w, block_sum_ref[g], 2)
            w_cols = jnp.sum(w_w, axis=0, keepdims=True)
            for hh in range(8):
                dcs_r = dcs_r + jnp.where(sub32 == 8 * g + hh, w_cols[:, 128 * hh:128 * (hh + 1)], 0.0)
            dl_w = dm_w * l_w
            ds_mat = dl_w[:, 0:128]
            for hh in range(1, 8):
                ds_mat = ds_mat + dl_w[:, 128 * hh:128 * (hh + 1)]
            hin_g = hin_ref[0, hs, :]
            hin_b = hin_g.astype(BF16)
            dh_g = dh_scr[hs, :]
            dh_b = dh_g.astype(BF16)
            g_mat = _nn(b_g, dh_b)
            xdec = xdt * dec_g
            xg = xdec * g_mat
            dxdt = jnp.concatenate(dxs, axis=1) + dec_g * g_mat
            sums = _spread(jnp.concatenate([dy_g * (_nn(c_g, hin_b) * ecs_g) - xg, dxdt * xs_g], axis=0), pair_sum_ref[g], 2)
            dcs_c = dcs_c + sums[:Q]
            ddt_x = ddt_x + sums[Q:]
            last = jnp.sum(xg, axis=0, keepdims=True) + jnp.sum(dh_g * hin_g, axis=0, keepdims=True) * cd_g
            dcs_l = dcs_l + _spread(jnp.broadcast_to(last, (8, 512)), pair_sum_ref[g], 2)
            dz = (dy_g * ecs_g).astype(BF16)
            ds_b = ds_mat.astype(BF16)
            dxa_ref[:, gs] = dxdt * dt_g + dy_g * dsk_ref[:, gs]
            dxa_ref[:, DI + STATE * g:DI + STATE * (g + 1)] = _nt(xdec.astype(BF16), dh_b) + _tn(ds_b, c_g)
            dxa_ref[:, DI + 512 + STATE * g:DI + 512 + STATE * (g + 1)] = _nt(dz, hin_b) + _nn(ds_b, b_g)
            dh_scr[hs, :] = _tn(c_g, dz) + dh_g * cd_g
            dsk_sum_ref[0:1, gs] += jnp.sum(dy_g * xs_g, axis=0, keepdims=True)

        rows = lax.broadcasted_iota(jnp.int32, (Q, Q), 0)
        cols = lax.broadcasted_iota(jnp.int32, (Q, Q), 1)
        tri_t = (cols >= rows).astype(F32)
        last_row = lax.broadcasted_iota(jnp.int32, (Q, 1), 0) == Q - 1
        dcs = (dcs_c + jnp.where(last_row, dcs_l[0:1, :], 0.0))[:, 0:HEADS]
        da = _nn(tri_t, dcs, precision=HIGH) - _nt(tri_t, dcs_r, precision=HIGH)
        ddt_raw = (ddt_x[:, 0:HEADS] + da * a_r) * _sigmoid(raw)
        small_ref[0:1, :] += jnp.sum(da * dt, axis=0, keepdims=True) * a_r
        small_ref[1:2, :] += jnp.sum(ddt_raw, axis=0, keepdims=True)
        dp_ref[...] = jnp.zeros_like(dp_ref)
        dp_ref[:, 0:HEADS] = ddt_raw.astype(BF16)

    rev = lambda c: nc - 1 - c
    small_r = pl.BlockSpec((1, 128), lambda c: (0, 0))
    small_c = pl.BlockSpec((HEADS, 1), lambda c: (0, 0))
    selectors = _ssd_selectors()
    whole = lambda a: pl.BlockSpec(a.shape, lambda c: (0,) * a.ndim)
    return _call(
        body, [dy, xbc_a, proj, hin, dtb_r, alog_r, dtb_c, alog_c, dsk_exp, dproj, *selectors], name="ssd_bwd", grid=(nc,),
        in_specs=[pl.BlockSpec((Q, DI), lambda c: (rev(c), 0)),
                  pl.BlockSpec((Q, XBC), lambda c: (rev(c), 0)),
                  pl.BlockSpec((Q, 128), lambda c: (rev(c), OFF_DT // 128)),
                  pl.BlockSpec((1, 512, 512), lambda c: (rev(c), 0, 0)),
                  small_r, small_r, small_c, small_c,
                  pl.BlockSpec((1, DI), lambda c: (0, 0)),
                  pl.BlockSpec(memory_space=pl.ANY)] + [whole(a) for a in selectors],
        out_specs=[pl.BlockSpec((Q, XBC), lambda c: (rev(c), 0)),
                   pl.BlockSpec((Q, 256), lambda c: (rev(c), OFF_DT // 256)),
                   pl.BlockSpec((8, DI), lambda c: (0, 0)),
                   pl.BlockSpec((8, HEADS), lambda c: (0, 0))],
        out_shape=[jax.ShapeDtypeStruct((T, XBC), F32), jax.ShapeDtypeStruct(dproj.shape, BF16),
                   jax.ShapeDtypeStruct((8, DI), F32), jax.ShapeDtypeStruct((8, HEADS), F32)],
        aliases={9: 1}, scratch_shapes=[pltpu.VMEM((512, 512), F32)], sem=("arbitrary",), carry=carry)


def _gate_norm_bwd(dyn, y, proj, w, dproj):
    T = y.shape[0]
    tr = _row_tile(T)

    def body(d_ref, y_ref, z_ref, w_ref, dp_in, dy_ref, dz_ref, sums_ref):
        del dp_in

        @pl.when(pl.program_id(0) == 0)
        def _():
            sums_ref[...] = jnp.zeros_like(sums_ref)

        for g in range(GROUPS):
            gs = slice(512 * g, 512 * (g + 1))
            z, yv, d = z_ref[:, gs], y_ref[:, gs], d_ref[:, gs]
            s = _sigmoid(z)
            silu = z * s
            yg = yv * silu
            r = lax.rsqrt(jnp.mean(yg * yg, axis=-1, keepdims=True) + EPS)
            yn = yg * r
            sums_ref[0:1, gs] += jnp.sum(d * yn, axis=0, keepdims=True)
            dn = d * w_ref[:, gs]
            dyg = r * (dn - yn * jnp.mean(dn * yn, axis=-1, keepdims=True))
            dy_ref[:, gs] = dyg * silu
            dz_ref[:, gs] = (dyg * yv * (s * (1.0 + z * (1.0 - s)))).astype(BF16)

    tile = pl.BlockSpec((tr, DI), lambda i: (i, 0))
    return _call(
        body, [dyn, y, proj, w, dproj], name="gate_norm_bwd", grid=(T // tr,),
        in_specs=[tile, tile, tile, pl.BlockSpec((1, DI), lambda i: (0, 0)), pl.BlockSpec(memory_space=pl.ANY)],
        out_specs=[tile, tile, pl.BlockSpec((8, DI), lambda i: (0, 0))],
        out_shape=[jax.ShapeDtypeStruct((T, DI), F32), jax.ShapeDtypeStruct(dproj.shape, BF16),
                   jax.ShapeDtypeStruct((8, DI), F32)],
        aliases={4: 1}, sem=("arbitrary",))


def _pool_fwd(proj, pool_w_b, pool_scale):
    T = proj.shape[0]
    tr = _row_tile(T)
    nb = tr // 16

    def body(u_ref, h_ref, pw_ref, ps_ref, pooled_ref, pw_out_ref, yps_ref):
        i = pl.program_id(0)
        t = i * tr + lax.broadcasted_iota(jnp.int32, (tr, 1), 0)
        for g, win in enumerate(POOL_WINDOWS):
            gs = slice(GW * g, GW * (g + 1))
            u = u_ref[:, gs]
            s = jnp.concatenate([jnp.where(i > 0, h_ref[:, gs], 0.0), u], axis=0)
            sh = 1
            while sh < win:
                s = s + pltpu.roll(s, sh, 0)
                sh *= 2
            pooled = (s[16:] * (1.0 / jnp.minimum(t + 1, win).astype(F32)) - u).astype(BF16)
            pooled_ref[:, gs] = pooled
            pwv = _nn(pooled, pw_ref[g])
            pw_out_ref[:, gs] = pwv
            yps_ref[:, gs] = (pwv * ps_ref[:, gs]).astype(BF16)

    tile = pl.BlockSpec((tr, D), lambda i: (i, 0))
    return _call(
        body, [proj, proj, pool_w_b, pool_scale], name="pool_fwd", grid=(T // tr,),
        in_specs=[pl.BlockSpec((tr, D), lambda i: (i, OFF_POOL // D)),
                  pl.BlockSpec((16, D), lambda i: (jnp.maximum(i * nb - 1, 0), OFF_POOL // D)),
                  pl.BlockSpec((4, GW, GW), lambda i: (0, 0, 0)),
                  pl.BlockSpec((1, D), lambda i: (0, 0))],
        out_specs=[tile, tile, tile],
        out_shape=[jax.ShapeDtypeStruct((T, D), BF16), jax.ShapeDtypeStruct((T, D), F32),
                   jax.ShapeDtypeStruct((T, D), BF16)], sem=("parallel",))


def _pool_bwd(dyp, pw_out, pooled, pool_w_b, pool_scale, dproj):
    T = dyp.shape[0]
    tr = _row_tile(T)
    nb, last = tr // 16, T // tr - 1

    def body(d_ref, h_ref, pwo_ref, pooled_ref, pw_ref, ps_ref, dp_in, du_ref, gpw_ref, sums_ref):
        del dp_in
        i = pl.program_id(0)

        @pl.when(i == 0)
        def _():
            gpw_ref[...] = jnp.zeros_like(gpw_ref)
            sums_ref[...] = jnp.zeros_like(sums_ref)

        n = tr + 16
        t = i * tr + lax.broadcasted_iota(jnp.int32, (n, 1), 0)
        sums_ref[0:1, :] += jnp.sum(d_ref[...] * pwo_ref[...], axis=0, keepdims=True)
        for g, win in enumerate(POOL_WINDOWS):
            gs = slice(GW * g, GW * (g + 1))
            d_ext = jnp.concatenate([d_ref[:, gs], jnp.where(i < last, h_ref[:, gs], 0.0)], axis=0)
            dpw = (d_ext * ps_ref[:, gs]).astype(BF16)
            dpooled = _nt(dpw, pw_ref[g])
            s = jnp.where(t < T, dpooled * (1.0 / jnp.minimum(t + 1, win).astype(F32)), 0.0)
            sh = 1
            while sh < win:
                s = s + pltpu.roll(s, n - sh, 0)
                sh *= 2
            du_ref[:, gs] = (s[:tr] - dpooled[:tr]).astype(BF16)
            gpw_ref[g] += _tn(pooled_ref[:, gs], dpw[:tr])

    tile = pl.BlockSpec((tr, D), lambda i: (i, 0))
    return _call(
        body, [dyp, dyp, pw_out, pooled, pool_w_b, pool_scale, dproj], name="pool_bwd", grid=(T // tr,),
        in_specs=[tile, pl.BlockSpec((16, D), lambda i: (jnp.minimum((i + 1) * nb, T // 16 - 1), 0)), tile, tile,
                  pl.BlockSpec((4, GW, GW), lambda i: (0, 0, 0)), pl.BlockSpec((1, D), lambda i: (0, 0)),
                  pl.BlockSpec(memory_space=pl.ANY)],
        out_specs=[pl.BlockSpec((tr, D), lambda i: (i, OFF_POOL // D)),
                   pl.BlockSpec((4, GW, GW), lambda i: (0, 0, 0)), pl.BlockSpec((8, D), lambda i: (0, 0))],
        out_shape=[jax.ShapeDtypeStruct(dproj.shape, BF16), jax.ShapeDtypeStruct((4, GW, GW), F32),
                   jax.ShapeDtypeStruct((8, D), F32)],
        aliases={6: 0}, sem=("arbitrary",))


def _merge(proj, y_ssd, y_pool):
    T = proj.shape[0]
    tr = _row_tile(T)

    def body(g_ref, a_ref, b_ref, o_ref):
        o_ref[...] = (_sigmoid(g_ref[:, 0:D]) * a_ref[...] + _sigmoid(g_ref[:, D:2 * D]) * b_ref[...]).astype(BF16)

    tile = pl.BlockSpec((tr, D), lambda i: (i, 0))
    return _call(body, [proj, y_ssd, y_pool], name="merge", grid=(T // tr,),
                 in_specs=[pl.BlockSpec((tr, 2 * D), lambda i: (i, OFF_GATE // (2 * D))), tile, tile], out_specs=[tile],
                 out_shape=[jax.ShapeDtypeStruct((T, D), BF16)], sem=("parallel",))[0]


def _merge_bwd(dmerged, proj, y_ssd, y_pool):
    T = proj.shape[0]
    tr = _row_tile(T)

    def body(d_ref, g_ref, a_ref, b_ref, da_ref, db_ref, dg_ref):
        d = d_ref[...]
        ga, gb = _sigmoid(g_ref[:, 0:D]), _sigmoid(g_ref[:, D:2 * D])
        da_ref[...] = (d * ga).astype(BF16)
        db_ref[...] = (d * gb).astype(BF16)
        dg_ref[:, 0:D] = (d * a_ref[...] * ga * (1.0 - ga)).astype(BF16)
        dg_ref[:, D:2 * D] = (d * b_ref[...] * gb * (1.0 - gb)).astype(BF16)

    tile = pl.BlockSpec((tr, D), lambda i: (i, 0))
    gates = pl.BlockSpec((tr, 2 * D), lambda i: (i, OFF_GATE // (2 * D)))
    return _call(body, [dmerged, proj, y_ssd, y_pool], name="merge_bwd", grid=(T // tr,),
                 in_specs=[tile, gates, tile, tile], out_specs=[tile, tile, gates],
                 out_shape=[jax.ShapeDtypeStruct((T, D), BF16), jax.ShapeDtypeStruct((T, D), BF16),
                            jax.ShapeDtypeStruct((T, NP), BF16)], sem=("parallel",))


def _adamw(w, g, m, v, name, carry=None):
    R, C = w.shape
    tr = R if R <= 128 else 128
    assert R % tr == 0

    def body(w_ref, g_ref, m_ref, v_ref, d_ref, mo_ref, vo_ref):
        gv = g_ref[...]
        mn = ADAM_B1 * m_ref[...] + (1.0 - ADAM_B1) * gv
        vn = ADAM_B2 * v_ref[...] + (1.0 - ADAM_B2) * (gv * gv)
        m_hat = mn * (1.0 / (1.0 - ADAM_B1 ** ADAM_STEP))
        v_hat = vn * (1.0 / (1.0 - ADAM_B2 ** ADAM_STEP))
        d_ref[...] = -ADAM_LR * (m_hat / (jnp.sqrt(v_hat) + ADAM_EPS) + ADAM_WD * w_ref[...])
        mo_ref[...] = mn
        vo_ref[...] = vn

    tile = pl.BlockSpec((tr, C), lambda i: (i, 0))
    sds = jax.ShapeDtypeStruct((R, C), F32)
    return _call(body, [w, g, m, v], name=name, grid=(R // tr,), in_specs=[tile] * 4, out_specs=[tile] * 3,
                 out_shape=[sds] * 3, sem=("parallel",), carry=carry)


def _me():
    return lax.axis_index("x"), lax.axis_index("y"), lax.axis_index("c")


def _xor_peer(x, y, c, p):
    return (x ^ ((p >> 2) & 1), y ^ ((p >> 1) & 1), c ^ (p & 1))


def _ada_fwd(c_row, w_ada, b_ada_mine, carry=None):
    n_cols = w_ada.shape[1]

    def body(c_ref, w_ref, b_ref, mod_ref, c8_ref, csend, mpart, modbuf, send_sems, recv_sems):
        x, y, c = _me()
        me = 4 * x + 2 * y + c
        chip = 2 * x + y
        csend[...] = jnp.broadcast_to(c_ref[...], csend.shape)
        c8_ref[me] = csend[...]

        def c_copy(p):
            return pltpu.make_async_remote_copy(
                src_ref=csend, dst_ref=c8_ref.at[me], send_sem=send_sems.at[p - 1], recv_sem=recv_sems.at[p - 1],
                device_id=_xor_peer(x, y, c, p), device_id_type=MESH)

        for p in range(1, 8):
            c_copy(p).start()
        for p in range(1, 8):
            c_copy(p).wait_recv()
        cs = jnp.concatenate([c8_ref[d][0:1, :] for d in range(8)], axis=0)
        mpart[...] = _nn(cs * _sigmoid(cs), w_ref[...], precision=HIGH) + b_ref[...]
        modbuf[chip] = mpart[...]

        def m_copy(m):
            return pltpu.make_async_remote_copy(
                src_ref=mpart, dst_ref=modbuf.at[chip], send_sem=send_sems.at[6 + m], recv_sem=recv_sems.at[6 + m],
                device_id=_xor_peer(x, y, c, 2 * m), device_id_type=MESH)

        for m in range(1, 4):
            m_copy(m).start()
        for m in range(1, 4):
            m_copy(m).wait_recv()
        mine = lax.broadcasted_iota(jnp.int32, (8, 1), 0) == me
        for k in range(N_CHIPS):
            mod_ref[:, n_cols * k:n_cols * (k + 1)] = jnp.sum(jnp.where(mine, modbuf[k], 0.0), axis=0, keepdims=True)
        for p in range(1, 8):
            c_copy(p).wait_send()
        for m in range(1, 4):
            m_copy(m).wait_send()

    vmem = pl.BlockSpec(memory_space=pltpu.VMEM)
    return _call(
        body, [c_row, w_ada, b_ada_mine], name="ada_fwd", in_specs=[vmem, vmem, vmem], out_specs=[vmem, vmem],
        out_shape=[jax.ShapeDtypeStruct((1, N_CHIPS * n_cols), F32), jax.ShapeDtypeStruct((8, 8, D), F32)],
        scratch_shapes=[pltpu.VMEM((8, D), F32), pltpu.VMEM((8, n_cols), F32), pltpu.VMEM((N_CHIPS, 8, n_cols), F32),
                        pltpu.SemaphoreType.DMA((10,)), pltpu.SemaphoreType.DMA((10,))], carry=carry)


def _gather_small(vec, carry=None):
    rows = vec.shape[0]

    def body(v_ref, all_ref, tot_ref, dsk_ref, send_sems, recv_sems):
        x, y, c = _me()
        me = 4 * x + 2 * y + c
        all_ref[me] = v_ref[...]

        def copy(p):
            return pltpu.make_async_remote_copy(
                src_ref=v_ref, dst_ref=all_ref.at[me], send_sem=send_sems.at[p - 1], recv_sem=recv_sems.at[p - 1],
                device_id=_xor_peer(x, y, c, p), device_id_type=MESH)

        for p in range(1, 8):
            copy(p).start()
        for p in range(1, 8):
            copy(p).wait_recv()
        tot = all_ref[0]
        for d in range(1, 8):
            tot = tot + all_ref[d]
        tot_ref[...] = tot
        seg = tot[SMALL_OFF["d_skip"] // 128:SMALL_OFF["d_skip"] // 128 + 16, :]
        lane = lax.broadcasted_iota(jnp.int32, (1, 128), 1)
        sa = jnp.sum(jnp.where(lane < HEAD_DIM, seg, 0.0), axis=1, keepdims=True)
        sb = jnp.sum(jnp.where(lane < HEAD_DIM, 0.0, seg), axis=1, keepdims=True)
        dsk_ref[...] = jnp.where(lane == 0, sa, jnp.where(lane == 1, sb, 0.0))
        for p in range(1, 8):
            copy(p).wait_send()

    vmem = pl.BlockSpec(memory_space=pltpu.VMEM)
    return _call(
        body, [vec], name="gather_small", in_specs=[vmem], out_specs=[vmem, vmem, vmem],
        out_shape=[jax.ShapeDtypeStruct((8, rows, 128), F32), jax.ShapeDtypeStruct((rows, 128), F32),
                   jax.ShapeDtypeStruct((16, 128), F32)],
        scratch_shapes=[pltpu.SemaphoreType.DMA((7,)), pltpu.SemaphoreType.DMA((7,))], carry=carry)


def _gather_carry(shards):
    n = len(shards)

    def copies(ins, outs, sems):
        x, y, c = _me()
        chip = 2 * x + y

        def half(w, which):
            h = shards[w].shape[0] // 2
            return pl.ds(which * h, h)

        def first(w, m):
            return pltpu.make_async_remote_copy(
                src_ref=ins[w].at[half(w, c)], dst_ref=outs[w].at[chip, half(w, c)],
                send_sem=sems.send(6 * w + m - 1), recv_sem=sems.recv(6 * w + m - 1),
                device_id=_xor_peer(x, y, c, 2 * m), device_id_type=MESH)

        def landed(w, m):
            return pltpu.make_async_remote_copy(
                src_ref=ins[w].at[half(w, c)], dst_ref=outs[w].at[chip ^ m, half(w, c)],
                send_sem=sems.send(6 * w + m - 1), recv_sem=sems.recv(6 * w + m - 1),
                device_id=_xor_peer(x, y, c, 2 * m), device_id_type=MESH)

        def passed(w, m, which):
            part = outs[w].at[chip ^ m, half(w, which)]
            return pltpu.make_async_remote_copy(
                src_ref=part, dst_ref=part, send_sem=sems.send(6 * w + 2 + m), recv_sem=sems.recv(6 * w + 2 + m),
                device_id=(x, y, 1 - c), device_id_type=MESH)

        return c, first, landed, passed

    pairs = [(w, m) for w in range(n) for m in range(1, 4)]

    def start(ins, outs, sems):
        _, first, _, _ = copies(ins, outs, sems)
        for w, m in pairs:
            first(w, m).start()

    def finish(ins, outs, sems):
        c, first, landed, passed = copies(ins, outs, sems)
        for w, m in pairs:
            landed(w, m).wait_recv()
            passed(w, m, c).start()
        for w, m in pairs:
            passed(w, m, 1 - c).wait_recv()
        for w, m in pairs:
            first(w, m).wait_send()
            passed(w, m, c).wait_send()

    return _Carry(shards, [jax.ShapeDtypeStruct((N_CHIPS,) + s.shape, s.dtype) for s in shards], 6 * n, start, finish)


def _pair_exchange_carry(grads):
    n = len(grads)

    def copy(ins, outs, sems, w):
        x, y, c = _me()
        h = grads[w].shape[1] // 2
        return pltpu.make_async_remote_copy(
            src_ref=ins[w].at[:, pl.ds((1 - c) * h, h)], dst_ref=outs[w],
            send_sem=sems.send(w), recv_sem=sems.recv(w), device_id=(x, y, 1 - c), device_id_type=MESH)

    def start(ins, outs, sems):
        for w in range(n):
            copy(ins, outs, sems, w).start()

    def finish(ins, outs, sems):
        for w in range(n):
            copy(ins, outs, sems, w).wait()

    return _Carry(grads, [jax.ShapeDtypeStruct((N_CHIPS, g.shape[1] // 2, g.shape[2]), g.dtype) for g in grads], n,
                  start, finish)


def _chip_exchange_carry(partials):
    n = len(partials)

    def copier(ins, outs, sems):
        x, y, c = _me()
        chip = 2 * x + y

        def copy(w, m, landed):
            return pltpu.make_async_remote_copy(
                src_ref=ins[w].at[chip ^ m], dst_ref=outs[w].at[(chip ^ m) if landed else chip],
                send_sem=sems.send(3 * w + m - 1), recv_sem=sems.recv(3 * w + m - 1),
                device_id=_xor_peer(x, y, c, 2 * m), device_id_type=MESH)

        return copy

    pairs = [(w, m) for w in range(n) for m in range(1, 4)]

    def start(ins, outs, sems):
        copy = copier(ins, outs, sems)
        for w, m in pairs:
            copy(w, m, False).start()

    def finish(ins, outs, sems):
        copy = copier(ins, outs, sems)
        for w, m in pairs:
            copy(w, m, True).wait_recv()
        for w, m in pairs:
            copy(w, m, False).wait_send()

    return _Carry(partials, [jax.ShapeDtypeStruct(p.shape, p.dtype) for p in partials], 3 * n, start, finish)


def _pair_share_carry(shards):
    n = len(shards)

    def copier(ins, outs, sems):
        x, y, c = _me()

        def copy(w, which):
            h = shards[w].shape[0] // 2
            rows = pl.ds(which * h, h)
            return pltpu.make_async_remote_copy(
                src_ref=ins[w].at[rows], dst_ref=outs[w].at[rows],
                send_sem=sems.send(w), recv_sem=sems.recv(w), device_id=(x, y, 1 - c), device_id_type=MESH)

        return c, copy

    def start(ins, outs, sems):
        c, copy = copier(ins, outs, sems)
        for w in range(n):
            copy(w, c).start()

    def finish(ins, outs, sems):
        c, copy = copier(ins, outs, sems)
        for w in range(n):
            copy(w, 1 - c).wait_recv()
        for w in range(n):
            copy(w, c).wait_send()

    return _Carry(shards, [jax.ShapeDtypeStruct(s.shape, s.dtype) for s in shards], n, start, finish,
                  aliased=[(w, w) for w in range(n)])


def _pair_sum(g, part, idx, name):
    _, h, C = part.shape
    tr = min(512, h)
    nb = h // tr

    def body(idx_ref, g_ref, p_ref, o16_ref, own_ref):
        v = g_ref[...].astype(F32) + p_ref[...].astype(F32)
        o16_ref[...] = v.astype(BF16)

        @pl.when(pl.program_id(1) == idx_ref[1])
        def _():
            own_ref[...] = v

    return pl.pallas_call(
        body, name=name,
        grid_spec=pltpu.PrefetchScalarGridSpec(
            num_scalar_prefetch=1, grid=(nb, N_CHIPS),
            in_specs=[pl.BlockSpec((None, tr, C), lambda i, s, idx_ref: (s, idx_ref[0] * nb + i, 0)),
                      pl.BlockSpec((None, tr, C), lambda i, s, idx_ref: (s, i, 0))],
            out_specs=[pl.BlockSpec((None, tr, C), lambda i, s, idx_ref: (s, i, 0)),
                       pl.BlockSpec((tr, C), lambda i, s, idx_ref: (i, 0))]),
        out_shape=[jax.ShapeDtypeStruct(part.shape, BF16), jax.ShapeDtypeStruct((h, C), F32)],
        compiler_params=pltpu.CompilerParams(dimension_semantics=("arbitrary", "arbitrary"), vmem_limit_bytes=VMEM_LIMIT),
    )(idx, g, part)


def _chip_sum(own, slots, idx, name):
    h, C = own.shape
    tr = min(512, h)
    nb = h // tr

    def body(idx_ref, own_ref, s1_ref, s2_ref, s3_ref, o_ref):
        del idx_ref
        o_ref[...] = ((own_ref[...] + s1_ref[...].astype(F32)) + s2_ref[...].astype(F32)) + s3_ref[...].astype(F32)

    def slot(m):
        return pl.BlockSpec((None, tr, C), lambda i, idx_ref: (idx_ref[1] ^ m, i, 0))

    return pl.pallas_call(
        body, name=name,
        grid_spec=pltpu.PrefetchScalarGridSpec(
            num_scalar_prefetch=1, grid=(nb,),
            in_specs=[pl.BlockSpec((tr, C), lambda i, idx_ref: (i, 0)), slot(1), slot(2), slot(3)],
            out_specs=pl.BlockSpec((tr, C), lambda i, idx_ref: (idx_ref[0] * nb + i, 0))),
        out_shape=jax.ShapeDtypeStruct((2 * h, C), F32),
        compiler_params=pltpu.CompilerParams(dimension_semantics=("parallel",), vmem_limit_bytes=VMEM_LIMIT),
    )(idx, own, slots, slots, slots)


class _Reducer:
    def __init__(self, idx):
        self.idx, self.chips, self.p16, self.own, self.mine, self.final = idx, {}, {}, {}, {}, {}

    def add(self, name, whole, chip_blocks=False):
        self.chips[name] = whole if chip_blocks else _chips_from_whole(name, whole)

    def pair(self, names):
        return _pair_exchange_carry([self.chips[n] for n in names])

    def take_pair(self, names, outs):
        for n, part in zip(names, outs):
            self.p16[n], self.own[n] = _pair_sum(self.chips.pop(n), part, self.idx, "pair_sum_" + n)

    def chip(self, names):
        return _chip_exchange_carry([self.p16[n] for n in names])

    def take_chip(self, names, outs):
        for n, slots in zip(names, outs):
            del self.p16[n]
            self.mine[n] = _chip_sum(self.own.pop(n), slots, self.idx, "chip_sum_" + n)

    def share(self, names):
        return _pair_share_carry([self.mine[n] for n in names])

    def take_share(self, names, outs):
        for n, s in zip(names, outs):
            del self.mine[n]
            self.final[n] = s


def _w_ada_grad(c8, dmod_cols):
    n_cols = dmod_cols.shape[1]
    tn = 512

    def body(c_ref, d_ref, o_ref):
        cv = c_ref[...]
        o_ref[...] = _tn(cv * _sigmoid(cv), d_ref[...], precision=HIGH)

    return _call(body, [c8, dmod_cols], name="w_ada_grad", grid=(n_cols // tn,),
                 in_specs=[pl.BlockSpec((8, D), lambda j: (0, 0)), pl.BlockSpec((8, tn), lambda j: (0, j))],
                 out_specs=[pl.BlockSpec((D, tn), lambda j: (0, j))],
                 out_shape=[jax.ShapeDtypeStruct((D, n_cols), F32)], sem=("parallel",))[0]


_SMALL_SEGS = (("dmod", 6144), ("norm_mix_w", 1024), ("conv_b", 3072), ("ssd_norm_w", 2048), ("pool_scale", 1024),
               ("norm_mlp_w", 1024), ("norm_final_w", 1024), ("conv_w", 4 * XBC), ("d_skip", 2048), ("a_log", 128),
               ("dt_bias", 128), ("loss", 128))
SMALL_OFF = {}
_o = 0
for _n, _s in _SMALL_SEGS:
    SMALL_OFF[_n] = _o
    _o += _s
SMALL_LEN = -(-_o // 1024) * 1024

_FIRST = ("w_in", "conv_w")
_LATER = ("w_branch_ssd", "pool_w", "w_branch_pool", "w_out", "w_up", "w_down")
_SMALL_REPLICATED = ("b_ada", "norm_mix_w", "conv_b", "dt_bias", "a_log", "d_skip", "ssd_norm_w", "pool_scale",
                     "norm_mlp_w", "norm_final_w")
_WEIGHTS = ("w_ada", "b_ada", "norm_mix_w", "w_in", "conv_w", "conv_b", "dt_bias", "a_log", "d_skip", "ssd_norm_w",
            "w_branch_ssd", "pool_w", "pool_scale", "w_branch_pool", "w_out", "norm_mlp_w", "w_up", "w_down",
            "norm_final_w")


def _shard_2d(name, a):
    if name == "conv_w":
        return a.reshape(16, -1)
    return (a.reshape(GW, GW) if name == "pool_w" else a.reshape(a.shape[-2], a.shape[-1])).astype(BF16)


def _whole_from_chips(name, g, own, chip):
    g = lax.dynamic_update_slice(g, own[None], (chip, 0, 0))
    if name == "w_in":
        a, b = _DT_IN_CHIP2, _DT_IN_CHIP2 + HEADS
        pad = jnp.zeros((D, NP - IN_COLS), g.dtype)
        return jnp.concatenate([g[0], g[1], g[2][:, :a], g[2][:, b:], g[3], g[2][:, a:b], pad], axis=1)
    if name == "w_up":
        return jnp.concatenate([g[k] for k in range(N_CHIPS)], axis=1)
    if name == "pool_w":
        return jnp.transpose(g.reshape(N_CHIPS, 4, GW // N_CHIPS, GW), (1, 0, 2, 3)).reshape(4, GW, GW)
    if name == "conv_w":
        return jnp.transpose(g.reshape(N_CHIPS, 4, XBC // N_CHIPS), (1, 0, 2)).reshape(4, XBC)
    return g.reshape(N_CHIPS * g.shape[1], g.shape[2])


def _chips_from_whole(name, g):
    if name.startswith("w_in"):
        cw, a = IN_COLS // N_CHIPS, _DT_IN_CHIP2
        chip2 = jnp.concatenate([g[:, 2 * cw:2 * cw + a], g[:, OFF_DT:OFF_DT + HEADS], g[:, 5120:3 * cw - HEADS]], axis=1)
        return jnp.stack([g[:, :cw], g[:, cw:2 * cw], chip2, g[:, 3 * cw - HEADS:OFF_DT]])
    if name == "w_up":
        return jnp.transpose(g.reshape(D, N_CHIPS, DFF // N_CHIPS), (1, 0, 2))
    if name == "pool_w":
        return jnp.transpose(g.reshape(4, N_CHIPS, GW // N_CHIPS, GW), (1, 0, 2, 3)).reshape(N_CHIPS, GW, GW)
    return g.reshape(N_CHIPS, g.shape[0] // N_CHIPS, g.shape[1])


def kernel(x, c, w_ada, b_ada, norm_mix_w, w_in, conv_w, conv_b, dt_bias, a_log, d_skip, ssd_norm_w, w_branch_ssd, pool_w, pool_scale, w_branch_pool, w_out, norm_mlp_w, w_up, w_down, norm_final_w, loss_target, m_w_ada, m_b_ada, m_norm_mix_w, m_w_in, m_conv_w, m_conv_b, m_dt_bias, m_a_log, m_d_skip, m_ssd_norm_w, m_w_branch_ssd, m_pool_w, m_pool_scale, m_w_branch_pool, m_w_out, m_norm_mlp_w, m_w_up, m_w_down, m_norm_final_w, v_w_ada, v_b_ada, v_norm_mix_w, v_w_in, v_conv_w, v_conv_b, v_dt_bias, v_a_log, v_d_skip, v_ssd_norm_w, v_w_branch_ssd, v_pool_w, v_pool_scale, v_w_branch_pool, v_w_out, v_norm_mlp_w, v_w_up, v_w_down, v_norm_final_w):
    args = locals()
    w = {n: args[n] for n in _WEIGHTS}
    m = {n: args["m_" + n] for n in _WEIGHTS}
    v = {n: args["v_" + n] for n in _WEIGHTS}
    xi, yi, ci = _me()
    chip = 2 * xi + yi
    idx = jnp.stack([ci, chip]).astype(jnp.int32)
    ada_cols = w_ada.shape[-1]
    xs, target = x[0], loss_target[0]
    two_d = lambda n, a: a.reshape(GW, GW) if n == "pool_w" else a.reshape(-1, a.shape[-1])
    delta, new_m, new_v, g = {}, {}, {}, {}

    def adamw(n, carry=None):
        res = _adamw(two_d(n, w[n]), two_d(n, g[n]), two_d(n, m[n]), two_d(n, v[n]), "adamw_" + n, carry=carry)
        (delta[n], new_m[n], new_v[n]), extra = res if carry is not None else (res, None)
        return extra

    b_mine = lax.dynamic_slice(b_ada, (0, chip * ada_cols), (1, ada_cols))
    shards = {n: _shard_2d(n, w[n]) for n in _FIRST + _LATER}
    mod, c8 = _ada_fwd(c, w_ada[0], b_mine)
    c8 = c8[:, 0, :]
    shift_m, scale_m, gate_m, shift_f, scale_f, gate_f = [mod[:, D * i:D * (i + 1)] for i in range(6)]
    nf_w = norm_final_w.reshape(1, D)

    h1, first = _norm_mod(xs, norm_mix_w, scale_m, shift_m, "norm_mod_mix",
                          carry=_gather_carry([shards[n] for n in _FIRST]))
    p ={n: _whole_from_chips(n, a, shards[n], chip) for n, a in zip(_FIRST, first)}
    (proj,), later = _matmul(h1, p["w_in"], mode="nn", out_dtypes=[F32], name="mm_proj", cols_outer=True,
                             carry=_gather_carry([shards[n] for n in _LATER]))
    p.update({n: _whole_from_chips(n, a, shards[n], chip) for n, a in zip(_LATER, later)})
    xbc_a = _conv_fwd(proj, p["conv_w"], conv_b)
    dtb_c, alog_c = dt_bias.reshape(HEADS, 1), a_log.reshape(HEADS, 1)
    dsk_exp = jnp.repeat(d_skip, HEAD_DIM, axis=1)
    y, hin, yn = _ssd_fwd(xbc_a, proj, dt_bias, a_log, dtb_c, alog_c, dsk_exp, ssd_norm_w)
    (y_ssd,) = _matmul(yn, p["w_branch_ssd"], mode="nn", out_dtypes=[F32], name="mm_branch_ssd")
    pooled, pw_out, yps = _pool_fwd(proj, p["pool_w"], pool_scale)
    (y_pool,) = _matmul(yps, p["w_branch_pool"], mode="nn", out_dtypes=[F32], name="mm_branch_pool")
    merged = _merge(proj, y_ssd, y_pool)
    resid = lambda acc, r, gt: (r + gt * acc, acc)
    x2, mix = _matmul(merged, p["w_out"], mode="nn", out_dtypes=[F32, BF16], name="mm_out",
                      epi=resid, tile_extras=(xs,), row_extras=(gate_m,))
    h2 = _norm_mod(x2, norm_mlp_w, scale_f, shift_f, "norm_mod_mlp")
    relu2 = lambda acc: (jnp.square(jnp.maximum(acc, 0.0)),)
    (act,) = _matmul(h2, p["w_up"], mode="nn", out_dtypes=[BF16], name="mm_up", epi=relu2)
    x3, down = _matmul(act, p["w_down"], mode="nn", out_dtypes=[F32, BF16], name="mm_down",
                       epi=resid, tile_extras=(x2,), row_extras=(gate_f,))

    red = _Reducer(idx)
    dx3, d_down, sums_f = _final_loss_bwd(x3, target, nf_w, down, gate_f)
    drelu2 = lambda acc, a: (acc * (2.0 * jnp.sqrt(a)).astype(F32),)
    (dup,) = _matmul(d_down, p["w_down"], mode="nt", out_dtypes=[BF16], name="mm_dact",
                     epi=drelu2, tile_extras=(act,))
    red.add("w_down", _matmul(act, d_down, mode="tn", out_dtypes=[BF16], name="mm_g_down")[0])
    (dh2,), got = _matmul(dup, p["w_up"], mode="nt", out_dtypes=[F32], name="mm_dh2",
                          carry=red.pair(["w_down"]))
    red.take_pair(["w_down"], got)
    red.add("w_up", _matmul(h2, dup, mode="tn", out_dtypes=[BF16], name="mm_g_up", chip_blocks=True)[0], chip_blocks=True)
    dx2, sums_2, dmix = _norm_mod_bwd(x2, dh2, dx3, norm_mlp_w, scale_f, "norm_mod_mlp_bwd", branch=mix, gate=gate_m)
    (dmerged,), got = _matmul(dmix, p["w_out"], mode="nt", out_dtypes=[F32], name="mm_dmerged",
                              carry=red.pair(["w_up"]))
    red.take_pair(["w_up"], got)
    red.add("w_out", _matmul(merged, dmix, mode="tn", out_dtypes=[BF16], name="mm_g_out")[0])
    dy_ssd, dy_pool, dproj = _merge_bwd(dmerged, proj, y_ssd, y_pool)
    (dyp,), got = _matmul(dy_pool, p["w_branch_pool"], mode="nt", out_dtypes=[F32], name="mm_dyp",
                          carry=red.pair(["w_out"]))
    red.take_pair(["w_out"], got)
    red.add("w_branch_pool", _matmul(yps, dy_pool, mode="tn", out_dtypes=[BF16], name="mm_g_bpool")[0])
    dproj, g_pool_w, sums_pool = _pool_bwd(dyp, pw_out, pooled, p["pool_w"], pool_scale, dproj)
    red.add("pool_w", g_pool_w.astype(BF16))
    red.add("w_branch_ssd", _matmul(yn, dy_ssd, mode="tn", out_dtypes=[BF16], name="mm_g_bssd")[0])
    mixers = ["w_branch_pool", "pool_w", "w_branch_ssd"]
    (dyn,), got = _matmul(dy_ssd, p["w_branch_ssd"], mode="nt", out_dtypes=[F32], name="mm_dyn",
                          carry=red.pair(mixers))
    red.take_pair(mixers, got)
    dy, dproj, sums_gn = _gate_norm_bwd(dyn, y, proj, ssd_norm_w, dproj)
    six = ["w_down", "w_up", "w_out"] + mixers
    (dxa, dproj, dsk_sum, ssd_small), got = _ssd_bwd(dy, xbc_a, proj, hin, dt_bias, a_log, dtb_c, alog_c, dsk_exp,
                                                     dproj, carry=red.chip(six))
    red.take_chip(six, got)
    dproj, sums_conv = _conv_bwd(dxa, proj, p["conv_w"], conv_b, dproj)
    rows_a = 3 * D // 4
    (g_in_a,), got = _matmul(h1, dproj, mode="tn", out_dtypes=[BF16], name="mm_g_in_a", a_cols=(0, rows_a),
                             carry=red.share(six))
    red.take_share(six, got)
    red.add("w_in_a", g_in_a)
    (g_in_b,), got = _matmul(h1, dproj, mode="tn", out_dtypes=[BF16], name="mm_g_in_b", a_cols=(rows_a, D - rows_a),
                             carry=red.pair(["w_in_a"]))
    red.take_pair(["w_in_a"], got)
    red.add("w_in_b", g_in_b)
    (dh1,), got = _matmul(dproj, p["w_in"], mode="nt", out_dtypes=[F32], name="mm_dh1",
                          carry=_join(red.chip(["w_in_a"]), red.pair(["w_in_b"])))
    red.take_chip(["w_in_a"], got[:1])
    red.take_pair(["w_in_b"], got[1:])
    grad_x, sums_1 = _norm_mod_bwd(xs, dh1, dx2, norm_mix_w, scale_m, "norm_mod_mix_bwd")

    dmod = jnp.concatenate([sums_1[0:1], sums_1[1:2], sums_2[3:4], sums_2[0:1], sums_2[1:2], sums_f[1:2]], axis=1)
    pad96 = jnp.zeros((1, 96), F32)
    small = {"dmod": dmod, "norm_mix_w": sums_1[2:3], "conv_b": sums_conv[4:5], "ssd_norm_w": sums_gn[0:1],
             "pool_scale": sums_pool[0:1], "norm_mlp_w": sums_2[2:3], "norm_final_w": sums_f[0:1],
             "conv_w": sums_conv[0:4].reshape(1, 4 * XBC), "d_skip": dsk_sum[0:1],
             "a_log": jnp.concatenate([ssd_small[0:1], pad96], axis=1),
             "dt_bias": jnp.concatenate([ssd_small[1:2], pad96], axis=1), "loss": sums_f[3:4, 0:128]}
    vec = jnp.concatenate([small[n] for n, _ in _SMALL_SEGS], axis=1)
    vec = jnp.pad(vec, ((0, 0), (0, SMALL_LEN - vec.shape[1]))).reshape(SMALL_LEN // 128, 128)
    (every, total, dsk), got = _gather_small(vec, carry=_join(red.chip(["w_in_b"]), red.share(["w_in_a"])))
    red.take_chip(["w_in_b"], got[:1])
    red.take_share(["w_in_a"], got[1:])
    total = total.reshape(1, SMALL_LEN)
    seg = lambda n, size: total[:, SMALL_OFF[n]:SMALL_OFF[n] + size]
    g.update({"b_ada": seg("dmod", 6 * D), "norm_mix_w": seg("norm_mix_w", D), "conv_b": seg("conv_b", XBC),
              "dt_bias": seg("dt_bias", HEADS), "a_log": seg("a_log", HEADS), "d_skip": dsk[:, 0:2].reshape(1, HEADS),
              "ssd_norm_w": seg("ssd_norm_w", DI), "pool_scale": seg("pool_scale", D),
              "norm_mlp_w": seg("norm_mlp_w", D), "norm_final_w": seg("norm_final_w", D)})
    loss = total[0, SMALL_OFF["loss"]]
    conv_cols = conv_w.shape[-1]
    g["conv_w"] = lax.dynamic_slice(seg("conv_w", 4 * XBC).reshape(4, XBC), (0, chip * conv_cols), (4, conv_cols))
    dmod8 = every.reshape(8, SMALL_LEN)[:, SMALL_OFF["dmod"]:SMALL_OFF["dmod"] + 6 * D]
    g["w_ada"] = _w_ada_grad(c8, lax.dynamic_slice(dmod8, (0, chip * ada_cols), (8, ada_cols)))

    got = adamw("w_ada", carry=red.share(["w_in_b"]))
    red.take_share(["w_in_b"], got)
    for n in six:
        g[n] = red.final[n]
    g["w_in"] = jnp.concatenate([red.final["w_in_a"], red.final["w_in_b"]], axis=0)
    for n in ["conv_w", "w_in"] + six:
        adamw(n)
    sizes = [w[n].size for n in _SMALL_REPLICATED]
    n_small = -(-sum(sizes) // 1024) * 1024
    pack = lambda d: jnp.pad(jnp.concatenate([d[n].reshape(1, -1) for n in _SMALL_REPLICATED], axis=1),
                             ((0, 0), (0, n_small - sum(sizes)))).reshape(n_small // 128, 128)
    d_, m_, v_ = _adamw(pack(w), pack(g), pack(m), pack(v), "adamw_small")
    off = 0
    for n, s in zip(_SMALL_REPLICATED, sizes):
        for dst, src in ((delta, d_), (new_m, m_), (new_v, v_)):
            dst[n] = src.reshape(1, n_small)[:, off:off + s]
        off += s

    out = [loss, grad_x.reshape(x.shape)]
    for d in (g, delta, new_m, new_v):
        out += [d[n].reshape(w[n].shape) for n in _WEIGHTS]
    return tuple(out)
```

```python
import functools
import operator

import jax
import jax.numpy as jnp
import numpy as np
from jax import lax
from jax.experimental import pallas as pl
from jax.experimental.pallas import tpu as pltpu

F32, BF16 = jnp.float32, jnp.bfloat16
HIGH = lax.Precision.HIGHEST
MESH = pl.DeviceIdType.MESH

D = 1024
DI = 2048
HEADS, HEAD_DIM = 32, 64
GROUPS, STATE = 4, 128
Q = 128
XBC = DI + 2 * GROUPS * STATE
POOL_WINDOWS = (2, 4, 8, 16)
GW = 256
DFF = 4096
EPS = 1e-5
IN_COLS = 8224
OFF_Z, OFF_XBC, OFF_POOL, OFF_GATE, OFF_DT, NP = 0, 2048, 5120, 6144, 8192, 8448
N_CHIPS = 4
ADAM_LR, ADAM_B1, ADAM_B2, ADAM_EPS, ADAM_WD, ADAM_STEP = 0.001, 0.9, 0.999, 1e-08, 0.01, 10
VMEM_LIMIT = 56 * 2 ** 20
NEG = -1e30


def _sigmoid(v):
    return 0.5 * jnp.tanh(0.5 * v) + 0.5


def _softplus(v):
    return jnp.maximum(v, 0.0) + jnp.log1p(jnp.exp(-jnp.abs(v)))


def _dot(a, b, dims, **kw):
    return lax.dot_general(a, b, (dims, ((), ())), preferred_element_type=F32, **kw)


def _nn(a, b, **kw):
    return _dot(a, b, ((1,), (0,)), **kw)


def _nt(a, b, **kw):
    return _dot(a, b, ((1,), (1,)), **kw)


def _tn(a, b, **kw):
    return _dot(a, b, ((0,), (0,)), **kw)


_DT_IN_CHIP2 = 5120 - 2 * (IN_COLS // 4)


class _Sems:
    def __init__(self, send, recv, local, base=0):
        self._send, self._recv, self._local, self._base = send, recv, local, base

    def shift(self, n):
        return _Sems(self._send, self._recv, self._local, self._base + n)

    def send(self, i):
        return self._send.at[self._base + i]

    def recv(self, i):
        return self._recv.at[self._base + i]

    def local(self, i):
        return self._local.at[self._base + i]


class _Carry:
    def __init__(self, ins, out_shapes, n_sems, start, finish, aliased=()):
        self.ins, self.out_shapes, self.n_sems, self.start, self.finish = list(ins), list(out_shapes), n_sems, start, finish
        self.aliased = list(aliased)


def _join(*carries):
    def run(which):
        def fn(ins, outs, sems):
            i = o = s = 0
            for cy in carries:
                getattr(cy, which)(ins[i:i + len(cy.ins)], outs[o:o + len(cy.out_shapes)], sems.shift(s))
                i, o, s = i + len(cy.ins), o + len(cy.out_shapes), s + cy.n_sems
        return fn

    aliased, i, o = [], 0, 0
    for cy in carries:
        aliased += [(i + a, o + b) for a, b in cy.aliased]
        i, o = i + len(cy.ins), o + len(cy.out_shapes)
    return _Carry([a for cy in carries for a in cy.ins], [a for cy in carries for a in cy.out_shapes],
                  sum(cy.n_sems for cy in carries), run("start"), run("finish"), aliased)


def _call(body, args, *, name, grid=(), in_specs, out_specs, out_shape, scratch_shapes=(), sem=None, aliases=None,
          carry=None):
    in_specs, out_specs, out_shape, scratch_shapes = list(in_specs), list(out_specs), list(out_shape), list(scratch_shapes)
    n_in, n_out, n_scr = len(in_specs), len(out_specs), len(scratch_shapes)
    kw = {"vmem_limit_bytes": VMEM_LIMIT}
    if carry is None:
        kernel_fn = functools.partial(body)
        if sem is not None:
            kw["dimension_semantics"] = sem
    else:
        n_ci, n_co = len(carry.ins), len(carry.out_shapes)
        hbm = pl.BlockSpec(memory_space=pl.ANY)
        in_specs += [hbm] * n_ci
        out_specs += [hbm] * n_co
        out_shape += carry.out_shapes
        n_s = max(carry.n_sems, 1)
        scratch_shapes += [pltpu.SemaphoreType.DMA((n_s,))] * 3
        args = list(args) + carry.ins
        aliases = dict(aliases or {})
        aliases.update({n_in + i: n_out + o for i, o in carry.aliased})
        if grid:
            kw["dimension_semantics"] = ("arbitrary",) * len(grid)

        def kernel_fn(*refs):
            a = n_in
            ins, c_ins = refs[:a], refs[a:a + n_ci]
            a += n_ci
            outs, c_outs = refs[a:a + n_out], refs[a + n_out:a + n_out + n_co]
            a += n_out + n_co
            scr, sems = refs[a:a + n_scr], _Sems(*refs[a + n_scr:a + n_scr + 3])
            if grid:
                ids = [pl.program_id(d) for d in range(len(grid))]
                first = functools.reduce(operator.and_, [i == 0 for i in ids])
                last = functools.reduce(operator.and_, [i == g - 1 for i, g in zip(ids, grid)])

                @pl.when(first)
                def _():
                    carry.start(c_ins, c_outs, sems)

                body(*ins, *outs, *scr)

                @pl.when(last)
                def _():
                    carry.finish(c_ins, c_outs, sems)
            else:
                carry.start(c_ins, c_outs, sems)
                body(*ins, *outs, *scr)
                carry.finish(c_ins, c_outs, sems)

    outs = pl.pallas_call(
        kernel_fn, name=name, grid=grid, in_specs=in_specs, out_specs=out_specs, out_shape=out_shape,
        scratch_shapes=scratch_shapes, input_output_aliases=aliases or {},
        compiler_params=pltpu.CompilerParams(**kw),
    )(*args)
    outs = list(outs)
    return outs if carry is None else (outs[:n_out], outs[n_out:])


def _run_carry(carry, name):
    _, outs = _call(lambda: None, [], name=name, in_specs=[], out_specs=[], out_shape=[], carry=carry)
    return outs


_TILES = {
    "mm_proj": (1024, 2816, 1024), "mm_branch_ssd": (1024, 1024, 2048), "mm_branch_pool": (1024, 1024, 1024),
    "mm_out": (1024, 1024, 1024), "mm_up": (2048, 1024, 1024), "mm_down": (512, 1024, 4096),
    "mm_dact": (1024, 1024, 1024), "mm_g_down": (1024, 1024, 4096), "mm_dh2": (1024, 1024, 4096),
    "mm_g_up": (1024, 1024, 4096), "mm_dmerged": (1024, 1024, 1024), "mm_g_out": (1024, 1024, 2048),
    "mm_dyp": (1024, 1024, 1024), "mm_g_bpool": (1024, 1024, 2048), "mm_g_bssd": (1024, 1024, 4096),
    "mm_dyn": (1024, 1024, 1024), "mm_g_in_a": (768, 1408, 4096), "mm_g_in_b": (256, 2816, 2048),
    "mm_dh1": (1024, 1024, 4224),
}


def _matmul(a, b, *, mode, out_dtypes, name, epi=None, tile_extras=(), row_extras=(), carry=None, a_cols=None,
            chip_blocks=False, cols_outer=False):
    M, K = (a.shape[1], a.shape[0]) if mode == "tn" else a.shape
    N = b.shape[0] if mode == "nt" else b.shape[1]
    a_start, M = a_cols if a_cols is not None else (0, M)
    tm, tn, tk = _TILES[name]
    tm, tn, tk = min(tm, M), min(tn, N), min(tk, K)
    assert M % tm == 0 and N % tn == 0 and K % tk == 0 and a_start % tm == 0, (name, M, N, K, tm, tn, tk)
    a_off = a_start // tm
    if mode == "nn":
        a_spec = pl.BlockSpec((tm, tk), lambda i, j, k: (i, k))
        b_spec = pl.BlockSpec((tk, tn), lambda i, j, k: (k, j))
        dims = ((1,), (0,))
    elif mode == "nt":
        a_spec = pl.BlockSpec((tm, tk), lambda i, j, k: (i, k))
        b_spec = pl.BlockSpec((tn, tk), lambda i, j, k: (j, k))
        dims = ((1,), (1,))
    else:
        a_spec = pl.BlockSpec((tk, tm), lambda i, j, k: (k, i + a_off))
        b_spec = pl.BlockSpec((tk, tn), lambda i, j, k: (k, j))
        dims = ((0,), (0,))
    nk = K // tk
    n_te, n_re, n_out = len(tile_extras), len(row_extras), len(out_dtypes)
    if epi is None:
        epi = lambda acc: (acc,)

    def body(a_ref, b_ref, *rest):
        extras = rest[:n_te + n_re]
        outs = rest[n_te + n_re:n_te + n_re + n_out]
        p = _dot(a_ref[...], b_ref[...], dims)

        def finish(acc):
            vals = epi(acc, *[e[...] for e in extras])
            for o, v in zip(outs, vals):
                o[...] = v.astype(o.dtype)

        if nk == 1:
            finish(p)
        else:
            acc_ref = rest[-1]
            k = pl.program_id(2)

            @pl.when(k == 0)
            def _():
                acc_ref[...] = p

            @pl.when(k > 0)
            def _():
                acc_ref[...] += p

            @pl.when(k == nk - 1)
            def _():
                finish(acc_ref[...])

    tile_spec = pl.BlockSpec((tm, tn), lambda i, j, k: (i, j))
    row_spec = pl.BlockSpec((1, tn), lambda i, j, k: (0, j))
    out_spec, out_dims = tile_spec, (M, N)
    if chip_blocks:
        assert n_te == 0 and tn * N_CHIPS == N
        out_spec, out_dims = pl.BlockSpec((None, tm, tn), lambda i, j, k: (j, i, 0)), (N_CHIPS, M, tn)
    in_specs, grid = [a_spec, b_spec] + [tile_spec] * n_te + [row_spec] * n_re, (M // tm, N // tn, nk)
    if cols_outer:
        swap = lambda s: pl.BlockSpec(s.block_shape, lambda g0, g1, k, f=s.index_map: f(g1, g0, k))
        in_specs, out_spec, grid = [swap(s) for s in in_specs], swap(out_spec), (N // tn, M // tm, nk)
    return _call(
        body, [a, b, *tile_extras, *row_extras], name=name, grid=grid,
        in_specs=in_specs, out_specs=[out_spec] * n_out,
        out_shape=[jax.ShapeDtypeStruct(out_dims, dt) for dt in out_dtypes],
        scratch_shapes=[pltpu.VMEM((tm, tn), F32)] if nk > 1 else [],
        sem=("parallel", "parallel", "arbitrary"), carry=carry)


def _row_tile(T):
    return min(512, T)


def _norm_mod(x, nw, scale, shift, name, carry=None):
    T = x.shape[0]
    tr = _row_tile(T)

    def body(x_ref, nw_ref, sc_ref, sh_ref, o_ref):
        xv = x_ref[...]
        r = lax.rsqrt(jnp.mean(xv * xv, axis=-1, keepdims=True) + EPS)
        o_ref[...] = ((xv * r) * nw_ref[...] * (1.0 + sc_ref[...]) + sh_ref[...]).astype(BF16)

    tile = pl.BlockSpec((tr, D), lambda i: (i, 0))
    row = pl.BlockSpec((1, D), lambda i: (0, 0))
    res = _call(body, [x, nw, scale, shift], name=name, grid=(T // tr,), in_specs=[tile, row, row, row],
                out_specs=[tile], out_shape=[jax.ShapeDtypeStruct((T, D), BF16)], sem=("parallel",), carry=carry)
    return res[0] if carry is None else (res[0][0], res[1])


def _norm_mod_bwd(x, dh, dres, nw, scale, name, branch=None, gate=None, carry=None):
    T = x.shape[0]
    tr = _row_tile(T)
    with_branch = branch is not None

    def body(x_ref, dh_ref, dr_ref, nw_ref, sc_ref, *rest):
        if with_branch:
            br_ref, g_ref, dx_ref, sums_ref, db_ref = rest
        else:
            dx_ref, sums_ref = rest
        i = pl.program_id(0)

        @pl.when(i == 0)
        def _():
            sums_ref[...] = jnp.zeros_like(sums_ref)

        xv, dhv = x_ref[...], dh_ref[...]
        r = lax.rsqrt(jnp.mean(xv * xv, axis=-1, keepdims=True) + EPS)
        xn = xv * r
        g1 = dhv * (1.0 + sc_ref[...])
        dxn = g1 * nw_ref[...]
        dx = dr_ref[...] + r * (dxn - xn * jnp.mean(dxn * xn, axis=-1, keepdims=True))
        dx_ref[...] = dx
        sums_ref[0:1, :] += jnp.sum(dhv, axis=0, keepdims=True)
        sums_ref[1:2, :] += jnp.sum(dhv * (xn * nw_ref[...]), axis=0, keepdims=True)
        sums_ref[2:3, :] += jnp.sum(g1 * xn, axis=0, keepdims=True)
        if with_branch:
            db_ref[...] = (dx * g_ref[...]).astype(BF16)
            sums_ref[3:4, :] += jnp.sum(dx * br_ref[...], axis=0, keepdims=True)

    tile = pl.BlockSpec((tr, D), lambda i: (i, 0))
    row = pl.BlockSpec((1, D), lambda i: (0, 0))
    sums = pl.BlockSpec((8, D), lambda i: (0, 0))
    ins = [x, dh, dres, nw, scale] + ([branch, gate] if with_branch else [])
    in_specs = [tile, tile, tile, row, row] + ([tile, row] if with_branch else [])
    out_specs = [tile, sums] + ([tile] if with_branch else [])
    out_shape = [jax.ShapeDtypeStruct((T, D), F32), jax.ShapeDtypeStruct((8, D), F32)]
    if with_branch:
        out_shape.append(jax.ShapeDtypeStruct((T, D), BF16))
    return _call(body, ins, name=name, grid=(T // tr,), in_specs=in_specs, out_specs=out_specs, out_shape=out_shape,
                 sem=("arbitrary",), carry=carry)


def _final_loss_bwd(x3, target, wf, down, gate_f):
    T = x3.shape[0]
    tr = _row_tile(T)
    n_steps = T // tr

    def body(x_ref, t_ref, w_ref, dn_ref, g_ref, dx_ref, dd_ref, sums_ref):
        i = pl.program_id(0)

        @pl.when(i == 0)
        def _():
            sums_ref[...] = jnp.zeros_like(sums_ref)

        xv = x_ref[...]
        r = lax.rsqrt(jnp.mean(xv * xv, axis=-1, keepdims=True) + EPS)
        xn = xv * r
        err = xn * w_ref[...] - t_ref[...]
        dy = err * (1.0 / D)
        dxn = dy * w_ref[...]
        dx = r * (dxn - xn * jnp.mean(dxn * xn, axis=-1, keepdims=True))
        dx_ref[...] = dx
        dd_ref[...] = (dx * g_ref[...]).astype(BF16)
        sums_ref[0:1, :] += jnp.sum(dy * xn, axis=0, keepdims=True)
        sums_ref[1:2, :] += jnp.sum(dx * dn_ref[...], axis=0, keepdims=True)
        sums_ref[2:3, :] += jnp.sum(err * err, axis=0, keepdims=True) * (0.5 / D)

        @pl.when(i == n_steps - 1)
        def _():
            sums_ref[3:4, :] = jnp.broadcast_to(jnp.sum(sums_ref[2:3, :], axis=1, keepdims=True), (1, D))

    tile = pl.BlockSpec((tr, D), lambda i: (i, 0))
    row = pl.BlockSpec((1, D), lambda i: (0, 0))
    sums = pl.BlockSpec((8, D), lambda i: (0, 0))
    return _call(body, [x3, target, wf, down, gate_f], name="final_loss_bwd", grid=(n_steps,),
                 in_specs=[tile, tile, row, tile, row], out_specs=[tile, tile, sums],
                 out_shape=[jax.ShapeDtypeStruct((T, D), F32), jax.ShapeDtypeStruct((T, D), BF16),
                            jax.ShapeDtypeStruct((8, D), F32)], sem=("arbitrary",))


CONV_TC = 1024


def _conv_taps(xp, w, b):
    acc = b + w[3:4, :] * xp
    for k in range(3):
        acc = acc + w[k:k + 1, :] * pltpu.roll(xp, 3 - k, 0)
    return acc


def _conv_fwd(proj, conv_w, conv_b):
    T = proj.shape[0]
    tr = _row_tile(T)
    nb, offb = tr // 8, OFF_XBC // CONV_TC

    def body(x_ref, h_ref, w_ref, b_ref, o_ref):
        halo = jnp.where(pl.program_id(0) > 0, h_ref[...], 0.0)
        xp = jnp.concatenate([halo, x_ref[...]], axis=0)
        acc = _conv_taps(xp, w_ref[...], b_ref[...])[8:]
        o_ref[...] = acc * _sigmoid(acc)

    return _call(
        body, [proj, proj, conv_w, conv_b], name="conv_fwd", grid=(T // tr, XBC // CONV_TC),
        in_specs=[pl.BlockSpec((tr, CONV_TC), lambda i, j: (i, j + offb)),
                  pl.BlockSpec((8, CONV_TC), lambda i, j: (jnp.maximum(i * nb - 1, 0), j + offb)),
                  pl.BlockSpec((4, CONV_TC), lambda i, j: (0, j)),
                  pl.BlockSpec((1, CONV_TC), lambda i, j: (0, j))],
        out_specs=[pl.BlockSpec((tr, CONV_TC), lambda i, j: (i, j))],
        out_shape=[jax.ShapeDtypeStruct((T, XBC), F32)], sem=("parallel", "parallel"))[0]


def _conv_bwd(dxa, proj, conv_w, conv_b, dproj):
    T = proj.shape[0]
    tr = _row_tile(T)
    nb, offb, last = tr // 8, OFF_XBC // CONV_TC, T // tr - 1
    prev8 = lambda i: jnp.maximum(i * nb - 1, 0)
    next8 = lambda i: jnp.minimum((i + 1) * nb, T // 8 - 1)

    def body(d_ref, dn_ref, x_ref, xp_ref, xn_ref, w_ref, b_ref, dp_in, o_ref, sums_ref):
        del dp_in
        i = pl.program_id(1)

        @pl.when(i == 0)
        def _():
            sums_ref[...] = jnp.zeros_like(sums_ref)

        x = jnp.concatenate([jnp.where(i > 0, xp_ref[...], 0.0), x_ref[...], jnp.where(i < last, xn_ref[...], 0.0)], axis=0)
        d = jnp.concatenate([d_ref[...], jnp.where(i < last, dn_ref[...], 0.0)], axis=0)
        w = w_ref[...]
        taps = [pltpu.roll(x, 3 - k, 0)[8:] for k in range(3)] + [x[8:]]
        acc = b_ref[...] + w[3:4, :] * taps[3]
        for k in range(3):
            acc = acc + w[k:k + 1, :] * taps[k]
        s = _sigmoid(acc)
        dxc = d * (s * (1.0 + acc * (1.0 - s)))
        n = tr + 8
        dx = w[3:4, :] * dxc
        for k in range(3):
            dx = dx + w[k:k + 1, :] * pltpu.roll(dxc, n - (3 - k), 0)
        o_ref[...] = dx[:tr].astype(BF16)
        own = dxc[:tr]
        for k in range(4):
            sums_ref[k:k + 1, :] += jnp.sum(own * taps[k][:tr], axis=0, keepdims=True)
        sums_ref[4:5, :] += jnp.sum(own, axis=0, keepdims=True)

    return _call(
        body, [dxa, dxa, proj, proj, proj, conv_w, conv_b, dproj], name="conv_bwd", grid=(XBC // CONV_TC, T // tr),
        in_specs=[pl.BlockSpec((tr, CONV_TC), lambda j, i: (i, j)),
                  pl.BlockSpec((8, CONV_TC), lambda j, i: (next8(i), j)),
                  pl.BlockSpec((tr, CONV_TC), lambda j, i: (i, j + offb)),
                  pl.BlockSpec((8, CONV_TC), lambda j, i: (prev8(i), j + offb)),
                  pl.BlockSpec((8, CONV_TC), lambda j, i: (next8(i), j + offb)),
                  pl.BlockSpec((4, CONV_TC), lambda j, i: (0, j)),
                  pl.BlockSpec((1, CONV_TC), lambda j, i: (0, j)),
                  pl.BlockSpec(memory_space=pl.ANY)],
        out_specs=[pl.BlockSpec((tr, CONV_TC), lambda j, i: (i, j + offb)), pl.BlockSpec((8, CONV_TC), lambda j, i: (0, j))],
        out_shape=[jax.ShapeDtypeStruct(dproj.shape, BF16), jax.ShapeDtypeStruct((8, XBC), F32)],
        aliases={7: 0}, sem=("parallel", "arbitrary"))


def _spread(v, sel, pieces):
    out = None
    for _ in range(pieces):
        p = v.astype(BF16)
        term = _nn(p, sel)
        out = term if out is None else out + term
        v = v - p.astype(F32)
    return out


def _ssd_selectors():
    g = np.arange(GROUPS)[:, None, None]
    piece = np.arange(128)[None, :, None]
    h = np.where(piece < 3 * HEADS, piece % HEADS, -1)
    blocks = (h == 8 * g + np.arange(1024)[None, None, :] // 128)
    pairs = (h == 8 * g + np.arange(512)[None, None, :] // HEAD_DIM)
    lane = np.arange(128)[None, None, :]
    block_sum = (lane == 8 * g + np.arange(1024)[None, :, None] // 128)
    pair_sum = (lane == 8 * g + np.arange(512)[None, :, None] // HEAD_DIM)
    return [jnp.asarray(m, BF16) for m in (blocks, pairs, block_sum, pair_sum)]


def _pack3(v):
    p0 = v.astype(BF16)
    r1 = v - p0.astype(F32)
    p1 = r1.astype(BF16)
    r2 = r1 - p1.astype(F32)
    return p0 + pltpu.roll(r1, HEADS, 1).astype(BF16) + pltpu.roll(r2, 2 * HEADS, 1).astype(BF16)


def _ssd_group(g, cs_p, csT, dt_p, s_mat, causal_w, lo, blocks_ref, pairs_ref):
    csb = _nn(cs_p, blocks_ref[g])
    row = jnp.concatenate([csT[8 * g + hh:8 * g + hh + 1, :] for hh in range(8)], axis=1)
    l_w = jnp.exp(jnp.where(causal_w, csb - row, NEG))
    m_w = jnp.concatenate([s_mat] * 8, axis=1) * l_w
    cs_g = jnp.concatenate([jnp.where(lo, csb[:, 256 * jj:256 * jj + 128], csb[:, 256 * jj + 128:256 * jj + 256])
                            for jj in range(4)], axis=1)
    cs_last = cs_g[Q - 1:Q, :]
    return m_w, l_w, _nn(dt_p, pairs_ref[g]), jnp.exp(cs_g), jnp.exp(cs_last - cs_g), jnp.exp(cs_last)


def _ssd_common(dtp_ref, dtb_r, alog_r, dtb_c, alog_c):
    rows = lax.broadcasted_iota(jnp.int32, (Q, Q), 0)
    cols = lax.broadcasted_iota(jnp.int32, (Q, Q), 1)
    tri = (cols <= rows).astype(F32)
    heads = lax.broadcasted_iota(jnp.int32, (1, 128), 1) < HEADS
    raw_w = dtp_ref[...] + dtb_r[...]
    dt_w = jnp.where(heads, _softplus(raw_w), 0.0)
    a_w = -jnp.exp(alog_r[...])
    cs_w = _nn(tri, dt_w * a_w, precision=HIGH)
    aT = _softplus(dtp_ref[...].T[0:HEADS, :] + dtb_c[...]) * (-jnp.exp(alog_c[...]))
    csT = _nt(aT, tri, precision=HIGH)
    return raw_w[:, 0:HEADS], dt_w[:, 0:HEADS], a_w[:, 0:HEADS], csT, _pack3(cs_w), _pack3(dt_w)


def _ssd_fwd(xbc_a, proj, dtb_r, alog_r, dtb_c, alog_c, dsk_exp, norm_w):
    T = xbc_a.shape[0]
    nc = T // Q
    dtb_r, alog_r = [jnp.pad(a, ((0, 0), (0, 128 - HEADS))) for a in (dtb_r, alog_r)]

    def body(xbc_ref, dtp_ref, z_ref, dtb_r_ref, alog_r_ref, dtb_c_ref, alog_c_ref, dsk_ref, nw_ref, blocks_ref,
             pairs_ref, y_ref, hin_ref, yn_ref, h_scr):
        @pl.when(pl.program_id(0) == 0)
        def _():
            h_scr[...] = jnp.zeros_like(h_scr)

        _, _, _, csT, cs_p, dt_p = _ssd_common(dtp_ref, dtb_r_ref, alog_r_ref, dtb_c_ref, alog_c_ref)
        lo = lax.broadcasted_iota(jnp.int32, (1, 128), 1) < HEAD_DIM
        hi = jnp.logical_not(lo)
        causal_w = (lax.broadcasted_iota(jnp.int32, (Q, 1024), 1) & (Q - 1)) <= lax.broadcasted_iota(jnp.int32, (Q, 1024), 0)
        for g in range(GROUPS):
            gs = slice(512 * g, 512 * (g + 1))
            hs = slice(128 * g, 128 * (g + 1))
            xs_g = xbc_ref[:, gs]
            b_g = xbc_ref[:, DI + STATE * g:DI + STATE * (g + 1)].astype(BF16)
            c_g = xbc_ref[:, DI + 512 + STATE * g:DI + 512 + STATE * (g + 1)].astype(BF16)
            m_w, _, dt_g, ecs_g, dec_g, cd_g = _ssd_group(g, cs_p, csT, dt_p, _nt(c_g, b_g), causal_w, lo, blocks_ref, pairs_ref)
            m_b = m_w.astype(BF16)
            xdt = xs_g * dt_g
            xdt_b = xdt.astype(BF16)
            ys = []
            for jj in range(4):
                xp = xdt_b[:, 128 * jj:128 * (jj + 1)]
                x_ab = jnp.concatenate([jnp.where(lo, xp, jnp.zeros_like(xp)), jnp.where(hi, xp, jnp.zeros_like(xp))], axis=0)
                ys.append(_nn(m_b[:, 256 * jj:256 * (jj + 1)], x_ab))
            h_g = h_scr[hs, :]
            hin_ref[0, hs, :] = h_g
            y_g = jnp.concatenate(ys, axis=1) + _nn(c_g, h_g.astype(BF16)) * ecs_g + dsk_ref[:, gs] * xs_g
            y_ref[:, gs] = y_g
            z = z_ref[:, gs]
            yg = y_g * (z * _sigmoid(z))
            r = lax.rsqrt(jnp.mean(yg * yg, axis=-1, keepdims=True) + EPS)
            yn_ref[:, gs] = (yg * r * nw_ref[:, gs]).astype(BF16)
            h_scr[hs, :] = h_g * cd_g + _tn(b_g, (xdt * dec_g).astype(BF16))

    small_r = pl.BlockSpec((1, 128), lambda c: (0, 0))
    small_c = pl.BlockSpec((HEADS, 1), lambda c: (0, 0))
    blocks, pairs, _, _ = _ssd_selectors()
    whole = lambda a: pl.BlockSpec(a.shape, lambda c: (0,) * a.ndim)
    wide, row = pl.BlockSpec((Q, DI), lambda c: (c, 0)), pl.BlockSpec((1, DI), lambda c: (0, 0))
    return _call(
        body, [xbc_a, proj, proj, dtb_r, alog_r, dtb_c, alog_c, dsk_exp, norm_w, blocks, pairs], name="ssd_fwd", grid=(nc,),
        in_specs=[pl.BlockSpec((Q, XBC), lambda c: (c, 0)),
                  pl.BlockSpec((Q, 128), lambda c: (c, OFF_DT // 128)), wide,
                  small_r, small_r, small_c, small_c, row, row, whole(blocks), whole(pairs)],
        out_specs=[wide, pl.BlockSpec((1, 512, 512), lambda c: (c, 0, 0)), wide],
        out_shape=[jax.ShapeDtypeStruct((T, DI), F32), jax.ShapeDtypeStruct((nc, 512, 512), F32),
                   jax.ShapeDtypeStruct((T, DI), BF16)],
        scratch_shapes=[pltpu.VMEM((512, 512), F32)], sem=("arbitrary",))


def _ssd_bwd(dyn, y, xbc_a, proj, hin, dtb_r, alog_r, dtb_c, alog_c, dsk_exp, norm_w, dproj, carry=None):
    T = xbc_a.shape[0]
    nc = T // Q
    dtb_r, alog_r = [jnp.pad(a, ((0, 0), (0, 128 - HEADS))) for a in (dtb_r, alog_r)]

    def body(dyn_ref, y_ref, z_ref, xbc_ref, dtp_ref, hin_ref, dtb_r_ref, alog_r_ref, dtb_c_ref, alog_c_ref, dsk_ref,
             nw_ref, dp_in, blocks_ref, pairs_ref, block_sum_ref, pair_sum_ref,
             dxa_ref, dp_ref, dsk_sum_ref, small_ref, dz_ref, gn_ref, dh_scr):
        del dp_in

        @pl.when(pl.program_id(0) == 0)
        def _():
            dh_scr[...] = jnp.zeros_like(dh_scr)
            dsk_sum_ref[...] = jnp.zeros_like(dsk_sum_ref)
            small_ref[...] = jnp.zeros_like(small_ref)
            gn_ref[...] = jnp.zeros_like(gn_ref)

        raw, dt, a_r, csT, cs_p, dt_p = _ssd_common(dtp_ref, dtb_r_ref, alog_r_ref, dtb_c_ref, alog_c_ref)
        lo = lax.broadcasted_iota(jnp.int32, (1, 128), 1) < HEAD_DIM
        hi = jnp.logical_not(lo)
        sub32 = lax.broadcasted_iota(jnp.int32, (HEADS, 1), 0)
        causal_w = (lax.broadcasted_iota(jnp.int32, (Q, 1024), 1) & (Q - 1)) <= lax.broadcasted_iota(jnp.int32, (Q, 1024), 0)
        dcs_c = jnp.zeros((Q, 128), F32)
        dcs_r = jnp.zeros((HEADS, Q), F32)
        dcs_l = jnp.zeros((8, 128), F32)
        ddt_x = jnp.zeros((Q, 128), F32)
        for g in range(GROUPS):
            gs = slice(512 * g, 512 * (g + 1))
            hs = slice(128 * g, 128 * (g + 1))
            xs_g = xbc_ref[:, gs]
            z, yv, d = z_ref[:, gs], y_ref[:, gs], dyn_ref[:, gs]
            s = _sigmoid(z)
            silu = z * s
            yg = yv * silu
            r = lax.rsqrt(jnp.mean(yg * yg, axis=-1, keepdims=True) + EPS)
            yn = yg * r
            gn_ref[0:1, gs] += jnp.sum(d * yn, axis=0, keepdims=True)
            dn = d * nw_ref[:, gs]
            dyg = r * (dn - yn * jnp.mean(dn * yn, axis=-1, keepdims=True))
            dy_g = dyg * silu
            dz_ref[:, gs] = (dyg * yv * (s * (1.0 + z * (1.0 - s)))).astype(BF16)
            b_g = xbc_ref[:, DI + STATE * g:DI + STATE * (g + 1)].astype(BF16)
            c_g = xbc_ref[:, DI + 512 + STATE * g:DI + 512 + STATE * (g + 1)].astype(BF16)
            m_w, l_w, dt_g, ecs_g, dec_g, cd_g = _ssd_group(g, cs_p, csT, dt_p, _nt(c_g, b_g), causal_w, lo, blocks_ref, pairs_ref)
            m_b = m_w.astype(BF16)
            xdt = xs_g * dt_g
            xdt_b, dy_b = xdt.astype(BF16), dy_g.astype(BF16)
            dms, dxs = [], []
            for jj in range(4):
                xp, dyp = xdt_b[:, 128 * jj:128 * (jj + 1)], dy_b[:, 128 * jj:128 * (jj + 1)]
                dy_ab = jnp.concatenate([jnp.where(lo, dyp, jnp.zeros_like(dyp)), jnp.where(hi, dyp, jnp.zeros_like(dyp))], axis=0)
                dm_ab = _nt(dy_ab, xp)
                dms += [dm_ab[:Q], dm_ab[Q:]]
                dx_ab = _tn(m_b[:, 256 * jj:256 * (jj + 1)], dyp)
                dxs.append(jnp.where(lo, dx_ab[:Q], dx_ab[Q:]))
            dm_w = jnp.concatenate(dms, axis=1)
            w_w = dm_w * m_w
            dcs_c = dcs_c + _spread(w_w, block_sum_ref[g], 2)
            w_cols = jnp.sum(w_w, axis=0, keepdims=True)
            for hh in range(8):
                dcs_r = dcs_r + jnp.where(sub32 == 8 * g + hh, w_cols[:, 128 * hh:128 * (hh + 1)], 0.0)
            dl_w = dm_w * l_w
            ds_mat = dl_w[:, 0:128]
            for hh in range(1, 8):
                ds_mat = ds_mat + dl_w[:, 128 * hh:128 * (hh + 1)]
            hin_g = hin_ref[0, hs, :]
            hin_b = hin_g.astype(BF16)
            dh_g = dh_scr[hs, :]
            dh_b = dh_g.astype(BF16)
            g_mat = _nn(b_g, dh_b)
            xdec = xdt * dec_g
            xg = xdec * g_mat
            dxdt = jnp.concatenate(dxs, axis=1) + dec_g * g_mat
            sums = _spread(jnp.concatenate([dy_g * (_nn(c_g, hin_b) * ecs_g) - xg, dxdt * xs_g], axis=0), pair_sum_ref[g], 2)
            dcs_c = dcs_c + sums[:Q]
            ddt_x = ddt_x + sums[Q:]
            last = jnp.sum(xg, axis=0, keepdims=True) + jnp.sum(dh_g * hin_g, axis=0, keepdims=True) * cd_g
            dcs_l = dcs_l + _spread(jnp.broadcast_to(last, (8, 512)), pair_sum_ref[g], 2)
            dz = (dy_g * ecs_g).astype(BF16)
            ds_b = ds_mat.astype(BF16)
            dxa_ref[:, gs] = dxdt * dt_g + dy_g * dsk_ref[:, gs]
            dxa_ref[:, DI + STATE * g:DI + STATE * (g + 1)] = _nt(xdec.astype(BF16), dh_b) + _tn(ds_b, c_g)
            dxa_ref[:, DI + 512 + STATE * g:DI + 512 + STATE * (g + 1)] = _nt(dz, hin_b) + _nn(ds_b, b_g)
            dh_scr[hs, :] = _tn(c_g, dz) + dh_g * cd_g
            dsk_sum_ref[0:1, gs] += jnp.sum(dy_g * xs_g, axis=0, keepdims=True)

        rows = lax.broadcasted_iota(jnp.int32, (Q, Q), 0)
        cols = lax.broadcasted_iota(jnp.int32, (Q, Q), 1)
        tri_t = (cols >= rows).astype(F32)
        last_row = lax.broadcasted_iota(jnp.int32, (Q, 1), 0) == Q - 1
        dcs = (dcs_c + jnp.where(last_row, dcs_l[0:1, :], 0.0))[:, 0:HEADS]
        da = _nn(tri_t, dcs, precision=HIGH) - _nt(tri_t, dcs_r, precision=HIGH)
        ddt_raw = (ddt_x[:, 0:HEADS] + da * a_r) * _sigmoid(raw)
        small_ref[0:1, :] += jnp.sum(da * dt, axis=0, keepdims=True) * a_r
        small_ref[1:2, :] += jnp.sum(ddt_raw, axis=0, keepdims=True)
        dp_ref[...] = jnp.zeros_like(dp_ref)
        dp_ref[:, 0:HEADS] = ddt_raw.astype(BF16)

    rev = lambda c: nc - 1 - c
    small_r = pl.BlockSpec((1, 128), lambda c: (0, 0))
    small_c = pl.BlockSpec((HEADS, 1), lambda c: (0, 0))
    selectors = _ssd_selectors()
    whole = lambda a: pl.BlockSpec(a.shape, lambda c: (0,) * a.ndim)
    wide, row = pl.BlockSpec((Q, DI), lambda c: (rev(c), 0)), pl.BlockSpec((1, DI), lambda c: (0, 0))
    sums = pl.BlockSpec((8, DI), lambda c: (0, 0))
    return _call(
        body, [dyn, y, proj, xbc_a, proj, hin, dtb_r, alog_r, dtb_c, alog_c, dsk_exp, norm_w, dproj, *selectors],
        name="ssd_bwd", grid=(nc,),
        in_specs=[wide, wide, wide,
                  pl.BlockSpec((Q, XBC), lambda c: (rev(c), 0)),
                  pl.BlockSpec((Q, 128), lambda c: (rev(c), OFF_DT // 128)),
                  pl.BlockSpec((1, 512, 512), lambda c: (rev(c), 0, 0)),
                  small_r, small_r, small_c, small_c, row, row,
                  pl.BlockSpec(memory_space=pl.ANY)] + [whole(a) for a in selectors],
        out_specs=[pl.BlockSpec((Q, XBC), lambda c: (rev(c), 0)),
                   pl.BlockSpec((Q, 256), lambda c: (rev(c), OFF_DT // 256)),
                   sums, pl.BlockSpec((8, HEADS), lambda c: (0, 0)), wide, sums],
        out_shape=[jax.ShapeDtypeStruct((T, XBC), F32), jax.ShapeDtypeStruct(dproj.shape, BF16),
                   jax.ShapeDtypeStruct((8, DI), F32), jax.ShapeDtypeStruct((8, HEADS), F32),
                   jax.ShapeDtypeStruct((T, DI), BF16), jax.ShapeDtypeStruct((8, DI), F32)],
        aliases={12: 1}, scratch_shapes=[pltpu.VMEM((512, 512), F32)], sem=("arbitrary",), carry=carry)


def _pool_fwd(proj, pool_w_b, pool_scale):
    T = proj.shape[0]
    tr = _row_tile(T)
    nb = tr // 16

    def body(u_ref, h_ref, pw_ref, ps_ref, pooled_ref, pw_out_ref, yps_ref):
        i = pl.program_id(0)
        t = i * tr + lax.broadcasted_iota(jnp.int32, (tr, 1), 0)
        for g, win in enumerate(POOL_WINDOWS):
            gs = slice(GW * g, GW * (g + 1))
            u = u_ref[:, gs]
            s = jnp.concatenate([jnp.where(i > 0, h_ref[:, gs], 0.0), u], axis=0)
            sh = 1
            while sh < win:
                s = s + pltpu.roll(s, sh, 0)
                sh *= 2
            pooled = (s[16:] * (1.0 / jnp.minimum(t + 1, win).astype(F32)) - u).astype(BF16)
            pooled_ref[:, gs] = pooled
            pwv = _nn(pooled, pw_ref[g])
            pw_out_ref[:, gs] = pwv
            yps_ref[:, gs] = (pwv * ps_ref[:, gs]).astype(BF16)

    tile = pl.BlockSpec((tr, D), lambda i: (i, 0))
    return _call(
        body, [proj, proj, pool_w_b, pool_scale], name="pool_fwd", grid=(T // tr,),
        in_specs=[pl.BlockSpec((tr, D), lambda i: (i, OFF_POOL // D)),
                  pl.BlockSpec((16, D), lambda i: (jnp.maximum(i * nb - 1, 0), OFF_POOL // D)),
                  pl.BlockSpec((4, GW, GW), lambda i: (0, 0, 0)),
                  pl.BlockSpec((1, D), lambda i: (0, 0))],
        out_specs=[tile, tile, tile],
        out_shape=[jax.ShapeDtypeStruct((T, D), BF16), jax.ShapeDtypeStruct((T, D), F32),
                   jax.ShapeDtypeStruct((T, D), BF16)], sem=("parallel",))


def _pool_bwd(dyp, pw_out, pooled, pool_w_b, pool_scale, dproj):
    T = dyp.shape[0]
    tr = _row_tile(T)
    nb, last = tr // 16, T // tr - 1

    def body(d_ref, h_ref, pwo_ref, pooled_ref, pw_ref, ps_ref, dp_in, du_ref, gpw_ref, sums_ref):
        del dp_in
        i = pl.program_id(0)

        @pl.when(i == 0)
        def _():
            gpw_ref[...] = jnp.zeros_like(gpw_ref)
            sums_ref[...] = jnp.zeros_like(sums_ref)

        n = tr + 16
        t = i * tr + lax.broadcasted_iota(jnp.int32, (n, 1), 0)
        sums_ref[0:1, :] += jnp.sum(d_ref[...] * pwo_ref[...], axis=0, keepdims=True)
        for g, win in enumerate(POOL_WINDOWS):
            gs = slice(GW * g, GW * (g + 1))
            d_ext = jnp.concatenate([d_ref[:, gs], jnp.where(i < last, h_ref[:, gs], 0.0)], axis=0)
            dpw = (d_ext * ps_ref[:, gs]).astype(BF16)
            dpooled = _nt(dpw, pw_ref[g])
            s = jnp.where(t < T, dpooled * (1.0 / jnp.minimum(t + 1, win).astype(F32)), 0.0)
            sh = 1
            while sh < win:
                s = s + pltpu.roll(s, n - sh, 0)
                sh *= 2
            du_ref[:, gs] = (s[:tr] - dpooled[:tr]).astype(BF16)
            gpw_ref[g] += _tn(pooled_ref[:, gs], dpw[:tr])

    tile = pl.BlockSpec((tr, D), lambda i: (i, 0))
    return _call(
        body, [dyp, dyp, pw_out, pooled, pool_w_b, pool_scale, dproj], name="pool_bwd", grid=(T // tr,),
        in_specs=[tile, pl.BlockSpec((16, D), lambda i: (jnp.minimum((i + 1) * nb, T // 16 - 1), 0)), tile, tile,
                  pl.BlockSpec((4, GW, GW), lambda i: (0, 0, 0)), pl.BlockSpec((1, D), lambda i: (0, 0)),
                  pl.BlockSpec(memory_space=pl.ANY)],
        out_specs=[pl.BlockSpec((tr, D), lambda i: (i, OFF_POOL // D)),
                   pl.BlockSpec((4, GW, GW), lambda i: (0, 0, 0)), pl.BlockSpec((8, D), lambda i: (0, 0))],
        out_shape=[jax.ShapeDtypeStruct(dproj.shape, BF16), jax.ShapeDtypeStruct((4, GW, GW), F32),
                   jax.ShapeDtypeStruct((8, D), F32)],
        aliases={6: 0}, sem=("arbitrary",))


def _merge(proj, y_ssd, y_pool):
    T = proj.shape[0]
    tr = _row_tile(T)

    def body(g_ref, a_ref, b_ref, o_ref):
        o_ref[...] = (_sigmoid(g_ref[:, 0:D]) * a_ref[...] + _sigmoid(g_ref[:, D:2 * D]) * b_ref[...]).astype(BF16)

    tile = pl.BlockSpec((tr, D), lambda i: (i, 0))
    return _call(body, [proj, y_ssd, y_pool], name="merge", grid=(T // tr,),
                 in_specs=[pl.BlockSpec((tr, 2 * D), lambda i: (i, OFF_GATE // (2 * D))), tile, tile], out_specs=[tile],
                 out_shape=[jax.ShapeDtypeStruct((T, D), BF16)], sem=("parallel",))[0]


def _merge_bwd(dmerged, proj, y_ssd, y_pool):
    T = proj.shape[0]
    tr = _row_tile(T)

    def body(d_ref, g_ref, a_ref, b_ref, da_ref, db_ref, dg_ref):
        d = d_ref[...]
        ga, gb = _sigmoid(g_ref[:, 0:D]), _sigmoid(g_ref[:, D:2 * D])
        da_ref[...] = (d * ga).astype(BF16)
        db_ref[...] = (d * gb).astype(BF16)
        dg_ref[:, 0:D] = (d * a_ref[...] * ga * (1.0 - ga)).astype(BF16)
        dg_ref[:, D:2 * D] = (d * b_ref[...] * gb * (1.0 - gb)).astype(BF16)

    tile = pl.BlockSpec((tr, D), lambda i: (i, 0))
    gates = pl.BlockSpec((tr, 2 * D), lambda i: (i, OFF_GATE // (2 * D)))
    return _call(body, [dmerged, proj, y_ssd, y_pool], name="merge_bwd", grid=(T // tr,),
                 in_specs=[tile, gates, tile, tile], out_specs=[tile, tile, gates],
                 out_shape=[jax.ShapeDtypeStruct((T, D), BF16), jax.ShapeDtypeStruct((T, D), BF16),
                            jax.ShapeDtypeStruct((T, NP), BF16)], sem=("parallel",))


def _adamw(w, g, m, v, name, carry=None):
    R, C = w.shape
    tr = R if R <= 128 else 128
    assert R % tr == 0

    def body(w_ref, g_ref, m_ref, v_ref, d_ref, mo_ref, vo_ref):
        gv = g_ref[...]
        mn = ADAM_B1 * m_ref[...] + (1.0 - ADAM_B1) * gv
        vn = ADAM_B2 * v_ref[...] + (1.0 - ADAM_B2) * (gv * gv)
        m_hat = mn * (1.0 / (1.0 - ADAM_B1 ** ADAM_STEP))
        v_hat = vn * (1.0 / (1.0 - ADAM_B2 ** ADAM_STEP))
        d_ref[...] = -ADAM_LR * (m_hat / (jnp.sqrt(v_hat) + ADAM_EPS) + ADAM_WD * w_ref[...])
        mo_ref[...] = mn
        vo_ref[...] = vn

    tile = pl.BlockSpec((tr, C), lambda i: (i, 0))
    sds = jax.ShapeDtypeStruct((R, C), F32)
    return _call(body, [w, g, m, v], name=name, grid=(R // tr,), in_specs=[tile] * 4, out_specs=[tile] * 3,
                 out_shape=[sds] * 3, sem=("parallel",), carry=carry)


def _me():
    return lax.axis_index("x"), lax.axis_index("y"), lax.axis_index("c")


def _xor_peer(x, y, c, p):
    return (x ^ ((p >> 2) & 1), y ^ ((p >> 1) & 1), c ^ (p & 1))


def _ada_fwd(c_row, w_ada, b_ada_mine, carry=None):
    n_cols = w_ada.shape[1]

    def body(c_ref, w_ref, b_ref, mod_ref, c8_ref, csend, mpart, modbuf, send_sems, recv_sems):
        x, y, c = _me()
        me = 4 * x + 2 * y + c
        chip = 2 * x + y
        csend[...] = jnp.broadcast_to(c_ref[...], csend.shape)
        c8_ref[me] = csend[...]

        def c_copy(p):
            return pltpu.make_async_remote_copy(
                src_ref=csend, dst_ref=c8_ref.at[me], send_sem=send_sems.at[p - 1], recv_sem=recv_sems.at[p - 1],
                device_id=_xor_peer(x, y, c, p), device_id_type=MESH)

        for p in range(1, 8):
            c_copy(p).start()
        for p in range(1, 8):
            c_copy(p).wait_recv()
        cs = jnp.concatenate([c8_ref[d][0:1, :] for d in range(8)], axis=0)
        mpart[...] = _nn(cs * _sigmoid(cs), w_ref[...], precision=HIGH) + b_ref[...]
        modbuf[chip] = mpart[...]

        def m_copy(m):
            return pltpu.make_async_remote_copy(
                src_ref=mpart, dst_ref=modbuf.at[chip], send_sem=send_sems.at[6 + m], recv_sem=recv_sems.at[6 + m],
                device_id=_xor_peer(x, y, c, 2 * m), device_id_type=MESH)

        for m in range(1, 4):
            m_copy(m).start()
        for m in range(1, 4):
            m_copy(m).wait_recv()
        mine = lax.broadcasted_iota(jnp.int32, (8, 1), 0) == me
        for k in range(N_CHIPS):
            mod_ref[:, n_cols * k:n_cols * (k + 1)] = jnp.sum(jnp.where(mine, modbuf[k], 0.0), axis=0, keepdims=True)
        for p in range(1, 8):
            c_copy(p).wait_send()
        for m in range(1, 4):
            m_copy(m).wait_send()

    vmem = pl.BlockSpec(memory_space=pltpu.VMEM)
    return _call(
        body, [c_row, w_ada, b_ada_mine], name="ada_fwd", in_specs=[vmem, vmem, vmem], out_specs=[vmem, vmem],
        out_shape=[jax.ShapeDtypeStruct((1, N_CHIPS * n_cols), F32), jax.ShapeDtypeStruct((8, 8, D), F32)],
        scratch_shapes=[pltpu.VMEM((8, D), F32), pltpu.VMEM((8, n_cols), F32), pltpu.VMEM((N_CHIPS, 8, n_cols), F32),
                        pltpu.SemaphoreType.DMA((10,)), pltpu.SemaphoreType.DMA((10,))], carry=carry)


def _gather_small(vec, carry=None):
    rows = vec.shape[0]

    def body(v_ref, all_ref, tot_ref, dsk_ref, send_sems, recv_sems):
        x, y, c = _me()
        me = 4 * x + 2 * y + c
        all_ref[me] = v_ref[...]

        def copy(p):
            return pltpu.make_async_remote_copy(
                src_ref=v_ref, dst_ref=all_ref.at[me], send_sem=send_sems.at[p - 1], recv_sem=recv_sems.at[p - 1],
                device_id=_xor_peer(x, y, c, p), device_id_type=MESH)

        for p in range(1, 8):
            copy(p).start()
        for p in range(1, 8):
            copy(p).wait_recv()
        tot = all_ref[0]
        for d in range(1, 8):
            tot = tot + all_ref[d]
        tot_ref[...] = tot
        seg = tot[SMALL_OFF["d_skip"] // 128:SMALL_OFF["d_skip"] // 128 + 16, :]
        lane = lax.broadcasted_iota(jnp.int32, (1, 128), 1)
        sa = jnp.sum(jnp.where(lane < HEAD_DIM, seg, 0.0), axis=1, keepdims=True)
        sb = jnp.sum(jnp.where(lane < HEAD_DIM, 0.0, seg), axis=1, keepdims=True)
        dsk_ref[...] = jnp.where(lane == 0, sa, jnp.where(lane == 1, sb, 0.0))
        for p in range(1, 8):
            copy(p).wait_send()

    vmem = pl.BlockSpec(memory_space=pltpu.VMEM)
    return _call(
        body, [vec], name="gather_small", in_specs=[vmem], out_specs=[vmem, vmem, vmem],
        out_shape=[jax.ShapeDtypeStruct((8, rows, 128), F32), jax.ShapeDtypeStruct((rows, 128), F32),
                   jax.ShapeDtypeStruct((16, 128), F32)],
        scratch_shapes=[pltpu.SemaphoreType.DMA((7,)), pltpu.SemaphoreType.DMA((7,))], carry=carry)


def _gather_carry(shards):
    n = len(shards)

    def copies(ins, outs, sems):
        x, y, c = _me()
        chip = 2 * x + y

        def half(w, which):
            h = shards[w].shape[0] // 2
            return pl.ds(which * h, h)

        def first(w, m):
            return pltpu.make_async_remote_copy(
                src_ref=ins[w].at[half(w, c)], dst_ref=outs[w].at[chip, half(w, c)],
                send_sem=sems.send(6 * w + m - 1), recv_sem=sems.recv(6 * w + m - 1),
                device_id=_xor_peer(x, y, c, 2 * m), device_id_type=MESH)

        def landed(w, m):
            return pltpu.make_async_remote_copy(
                src_ref=ins[w].at[half(w, c)], dst_ref=outs[w].at[chip ^ m, half(w, c)],
                send_sem=sems.send(6 * w + m - 1), recv_sem=sems.recv(6 * w + m - 1),
                device_id=_xor_peer(x, y, c, 2 * m), device_id_type=MESH)

        def passed(w, m, which):
            part = outs[w].at[chip ^ m, half(w, which)]
            return pltpu.make_async_remote_copy(
                src_ref=part, dst_ref=part, send_sem=sems.send(6 * w + 2 + m), recv_sem=sems.recv(6 * w + 2 + m),
                device_id=(x, y, 1 - c), device_id_type=MESH)

        return c, first, landed, passed

    pairs = [(w, m) for w in range(n) for m in range(1, 4)]

    def start(ins, outs, sems):
        _, first, _, _ = copies(ins, outs, sems)
        for w, m in pairs:
            first(w, m).start()

    def finish(ins, outs, sems):
        c, first, landed, passed = copies(ins, outs, sems)
        for w, m in pairs:
            landed(w, m).wait_recv()
            passed(w, m, c).start()
        for w, m in pairs:
            passed(w, m, 1 - c).wait_recv()
        for w, m in pairs:
            first(w, m).wait_send()
            passed(w, m, c).wait_send()

    return _Carry(shards, [jax.ShapeDtypeStruct((N_CHIPS,) + s.shape, s.dtype) for s in shards], 6 * n, start, finish)


def _pair_exchange_carry(grads):
    n = len(grads)

    def copy(ins, outs, sems, w):
        x, y, c = _me()
        h = grads[w].shape[1] // 2
        return pltpu.make_async_remote_copy(
            src_ref=ins[w].at[:, pl.ds((1 - c) * h, h)], dst_ref=outs[w],
            send_sem=sems.send(w), recv_sem=sems.recv(w), device_id=(x, y, 1 - c), device_id_type=MESH)

    def start(ins, outs, sems):
        for w in range(n):
            copy(ins, outs, sems, w).start()

    def finish(ins, outs, sems):
        for w in range(n):
            copy(ins, outs, sems, w).wait()

    return _Carry(grads, [jax.ShapeDtypeStruct((N_CHIPS, g.shape[1] // 2, g.shape[2]), g.dtype) for g in grads], n,
                  start, finish)


def _chip_exchange_carry(partials):
    n = len(partials)

    def copier(ins, outs, sems):
        x, y, c = _me()
        chip = 2 * x + y

        def copy(w, m, landed):
            return pltpu.make_async_remote_copy(
                src_ref=ins[w].at[chip ^ m], dst_ref=outs[w].at[(chip ^ m) if landed else chip],
                send_sem=sems.send(3 * w + m - 1), recv_sem=sems.recv(3 * w + m - 1),
                device_id=_xor_peer(x, y, c, 2 * m), device_id_type=MESH)

        return copy

    pairs = [(w, m) for w in range(n) for m in range(1, 4)]

    def start(ins, outs, sems):
        copy = copier(ins, outs, sems)
        for w, m in pairs:
            copy(w, m, False).start()

    def finish(ins, outs, sems):
        copy = copier(ins, outs, sems)
        for w, m in pairs:
            copy(w, m, True).wait_recv()
        for w, m in pairs:
            copy(w, m, False).wait_send()

    return _Carry(partials, [jax.ShapeDtypeStruct(p.shape, p.dtype) for p in partials], 3 * n, start, finish)


def _pair_share_carry(shards):
    n = len(shards)

    def copier(ins, outs, sems):
        x, y, c = _me()

        def copy(w, which):
            h = shards[w].shape[0] // 2
            rows = pl.ds(which * h, h)
            return pltpu.make_async_remote_copy(
                src_ref=ins[w].at[rows], dst_ref=outs[w].at[rows],
                send_sem=sems.send(w), recv_sem=sems.recv(w), device_id=(x, y, 1 - c), device_id_type=MESH)

        return c, copy

    def start(ins, outs, sems):
        c, copy = copier(ins, outs, sems)
        for w in range(n):
            copy(w, c).start()

    def finish(ins, outs, sems):
        c, copy = copier(ins, outs, sems)
        for w in range(n):
            copy(w, 1 - c).wait_recv()
        for w in range(n):
            copy(w, c).wait_send()

    return _Carry(shards, [jax.ShapeDtypeStruct(s.shape, s.dtype) for s in shards], n, start, finish,
                  aliased=[(w, w) for w in range(n)])


def _pair_sum(g, part, idx, name):
    _, h, C = part.shape
    tr = min(512, h)
    nb = h // tr

    def body(idx_ref, g_ref, p_ref, o16_ref, own_ref):
        v = g_ref[...].astype(F32) + p_ref[...].astype(F32)
        o16_ref[...] = v.astype(BF16)

        @pl.when(pl.program_id(1) == idx_ref[1])
        def _():
            own_ref[...] = v

    return pl.pallas_call(
        body, name=name,
        grid_spec=pltpu.PrefetchScalarGridSpec(
            num_scalar_prefetch=1, grid=(nb, N_CHIPS),
            in_specs=[pl.BlockSpec((None, tr, C), lambda i, s, idx_ref: (s, idx_ref[0] * nb + i, 0)),
                      pl.BlockSpec((None, tr, C), lambda i, s, idx_ref: (s, i, 0))],
            out_specs=[pl.BlockSpec((None, tr, C), lambda i, s, idx_ref: (s, i, 0)),
                       pl.BlockSpec((tr, C), lambda i, s, idx_ref: (i, 0))]),
        out_shape=[jax.ShapeDtypeStruct(part.shape, BF16), jax.ShapeDtypeStruct((h, C), F32)],
        compiler_params=pltpu.CompilerParams(dimension_semantics=("arbitrary", "arbitrary"), vmem_limit_bytes=VMEM_LIMIT),
    )(idx, g, part)


def _chip_sum(own, slots, idx, name):
    h, C = own.shape
    tr = min(512, h)
    nb = h // tr

    def body(idx_ref, own_ref, s1_ref, s2_ref, s3_ref, o_ref):
        del idx_ref
        o_ref[...] = ((own_ref[...] + s1_ref[...].astype(F32)) + s2_ref[...].astype(F32)) + s3_ref[...].astype(F32)

    def slot(m):
        return pl.BlockSpec((None, tr, C), lambda i, idx_ref: (idx_ref[1] ^ m, i, 0))

    return pl.pallas_call(
        body, name=name,
        grid_spec=pltpu.PrefetchScalarGridSpec(
            num_scalar_prefetch=1, grid=(nb,),
            in_specs=[pl.BlockSpec((tr, C), lambda i, idx_ref: (i, 0)), slot(1), slot(2), slot(3)],
            out_specs=pl.BlockSpec((tr, C), lambda i, idx_ref: (idx_ref[0] * nb + i, 0))),
        out_shape=jax.ShapeDtypeStruct((2 * h, C), F32),
        compiler_params=pltpu.CompilerParams(dimension_semantics=("parallel",), vmem_limit_bytes=VMEM_LIMIT),
    )(idx, own, slots, slots, slots)


class _Reducer:
    def __init__(self, idx):
        self.idx, self.chips, self.p16, self.own, self.mine, self.final = idx, {}, {}, {}, {}, {}

    def add(self, name, whole, chip_blocks=False):
        self.chips[name] = whole if chip_blocks else _chips_from_whole(name, whole)

    def pair(self, names):
        return _pair_exchange_carry([self.chips[n] for n in names])

    def take_pair(self, names, outs):
        for n, part in zip(names, outs):
            self.p16[n], self.own[n] = _pair_sum(self.chips.pop(n), part, self.idx, "pair_sum_" + n)

    def chip(self, names):
        return _chip_exchange_carry([self.p16[n] for n in names])

    def take_chip(self, names, outs):
        for n, slots in zip(names, outs):
            del self.p16[n]
            self.mine[n] = _chip_sum(self.own.pop(n), slots, self.idx, "chip_sum_" + n)

    def share(self, names):
        return _pair_share_carry([self.mine[n] for n in names])

    def take_share(self, names, outs):
        for n, s in zip(names, outs):
            del self.mine[n]
            self.final[n] = s


def _w_ada_grad(c8, dmod_cols):
    n_cols = dmod_cols.shape[1]
    tn = 512

    def body(c_ref, d_ref, o_ref):
        cv = c_ref[...]
        o_ref[...] = _tn(cv * _sigmoid(cv), d_ref[...], precision=HIGH)

    return _call(body, [c8, dmod_cols], name="w_ada_grad", grid=(n_cols // tn,),
                 in_specs=[pl.BlockSpec((8, D), lambda j: (0, 0)), pl.BlockSpec((8, tn), lambda j: (0, j))],
                 out_specs=[pl.BlockSpec((D, tn), lambda j: (0, j))],
                 out_shape=[jax.ShapeDtypeStruct((D, n_cols), F32)], sem=("parallel",))[0]


_SMALL_SEGS = (("dmod", 6144), ("norm_mix_w", 1024), ("conv_b", 3072), ("ssd_norm_w", 2048), ("pool_scale", 1024),
               ("norm_mlp_w", 1024), ("norm_final_w", 1024), ("conv_w", 4 * XBC), ("d_skip", 2048), ("a_log", 128),
               ("dt_bias", 128), ("loss", 128))
SMALL_OFF = {}
_o = 0
for _n, _s in _SMALL_SEGS:
    SMALL_OFF[_n] = _o
    _o += _s
SMALL_LEN = -(-_o // 1024) * 1024

_FIRST = ("w_in", "conv_w")
_LATER = ("w_branch_ssd", "pool_w", "w_branch_pool", "w_out", "w_up", "w_down")
_SMALL_REPLICATED = ("b_ada", "norm_mix_w", "conv_b", "dt_bias", "a_log", "d_skip", "ssd_norm_w", "pool_scale",
                     "norm_mlp_w", "norm_final_w")
_WEIGHTS = ("w_ada", "b_ada", "norm_mix_w", "w_in", "conv_w", "conv_b", "dt_bias", "a_log", "d_skip", "ssd_norm_w",
            "w_branch_ssd", "pool_w", "pool_scale", "w_branch_pool", "w_out", "norm_mlp_w", "w_up", "w_down",
            "norm_final_w")


def _shard_2d(name, a):
    if name == "conv_w":
        return a.reshape(16, -1)
    return (a.reshape(GW, GW) if name == "pool_w" else a.reshape(a.shape[-2], a.shape[-1])).astype(BF16)


def _whole_from_chips(name, g, own, chip):
    g = lax.dynamic_update_slice(g, own[None], (chip, 0, 0))
    if name == "w_in":
        a, b = _DT_IN_CHIP2, _DT_IN_CHIP2 + HEADS
        pad = jnp.zeros((D, NP - IN_COLS), g.dtype)
        return jnp.concatenate([g[0], g[1], g[2][:, :a], g[2][:, b:], g[3], g[2][:, a:b], pad], axis=1)
    if name == "w_up":
        return jnp.concatenate([g[k] for k in range(N_CHIPS)], axis=1)
    if name == "pool_w":
        return jnp.transpose(g.reshape(N_CHIPS, 4, GW // N_CHIPS, GW), (1, 0, 2, 3)).reshape(4, GW, GW)
    if name == "conv_w":
        return jnp.transpose(g.reshape(N_CHIPS, 4, XBC // N_CHIPS), (1, 0, 2)).reshape(4, XBC)
    return g.reshape(N_CHIPS * g.shape[1], g.shape[2])


def _chips_from_whole(name, g):
    if name.startswith("w_in"):
        cw, a = IN_COLS // N_CHIPS, _DT_IN_CHIP2
        chip2 = jnp.concatenate([g[:, 2 * cw:2 * cw + a], g[:, OFF_DT:OFF_DT + HEADS], g[:, 5120:3 * cw - HEADS]], axis=1)
        return jnp.stack([g[:, :cw], g[:, cw:2 * cw], chip2, g[:, 3 * cw - HEADS:OFF_DT]])
    if name == "w_up":
        return jnp.transpose(g.reshape(D, N_CHIPS, DFF // N_CHIPS), (1, 0, 2))
    if name == "pool_w":
        return jnp.transpose(g.reshape(4, N_CHIPS, GW // N_CHIPS, GW), (1, 0, 2, 3)).reshape(N_CHIPS, GW, GW)
    return g.reshape(N_CHIPS, g.shape[0] // N_CHIPS, g.shape[1])


def kernel(x, c, w_ada, b_ada, norm_mix_w, w_in, conv_w, conv_b, dt_bias, a_log, d_skip, ssd_norm_w, w_branch_ssd, pool_w, pool_scale, w_branch_pool, w_out, norm_mlp_w, w_up, w_down, norm_final_w, loss_target, m_w_ada, m_b_ada, m_norm_mix_w, m_w_in, m_conv_w, m_conv_b, m_dt_bias, m_a_log, m_d_skip, m_ssd_norm_w, m_w_branch_ssd, m_pool_w, m_pool_scale, m_w_branch_pool, m_w_out, m_norm_mlp_w, m_w_up, m_w_down, m_norm_final_w, v_w_ada, v_b_ada, v_norm_mix_w, v_w_in, v_conv_w, v_conv_b, v_dt_bias, v_a_log, v_d_skip, v_ssd_norm_w, v_w_branch_ssd, v_pool_w, v_pool_scale, v_w_branch_pool, v_w_out, v_norm_mlp_w, v_w_up, v_w_down, v_norm_final_w):
    args = locals()
    w = {n: args[n] for n in _WEIGHTS}
    m = {n: args["m_" + n] for n in _WEIGHTS}
    v = {n: args["v_" + n] for n in _WEIGHTS}
    xi, yi, ci = _me()
    chip = 2 * xi + yi
    idx = jnp.stack([ci, chip]).astype(jnp.int32)
    ada_cols = w_ada.shape[-1]
    xs, target = x[0], loss_target[0]
    two_d = lambda n, a: a.reshape(GW, GW) if n == "pool_w" else a.reshape(-1, a.shape[-1])
    delta, new_m, new_v, g = {}, {}, {}, {}

    def adamw(n, carry=None):
        res = _adamw(two_d(n, w[n]), two_d(n, g[n]), two_d(n, m[n]), two_d(n, v[n]), "adamw_" + n, carry=carry)
        (delta[n], new_m[n], new_v[n]), extra = res if carry is not None else (res, None)
        return extra

    b_mine = lax.dynamic_slice(b_ada, (0, chip * ada_cols), (1, ada_cols))
    shards = {n: _shard_2d(n, w[n]) for n in _FIRST + _LATER}
    mod, c8 = _ada_fwd(c, w_ada[0], b_mine)
    c8 = c8[:, 0, :]
    shift_m, scale_m, gate_m, shift_f, scale_f, gate_f = [mod[:, D * i:D * (i + 1)] for i in range(6)]
    nf_w = norm_final_w.reshape(1, D)

    h1, first = _norm_mod(xs, norm_mix_w, scale_m, shift_m, "norm_mod_mix",
                          carry=_gather_carry([shards[n] for n in _FIRST]))
    p ={n: _whole_from_chips(n, a, shards[n], chip) for n, a in zip(_FIRST, first)}
    (proj,), later = _matmul(h1, p["w_in"], mode="nn", out_dtypes=[F32], name="mm_proj", cols_outer=True,
                             carry=_gather_carry([shards[n] for n in _LATER]))
    p.update({n: _whole_from_chips(n, a, shards[n], chip) for n, a in zip(_LATER, later)})
    xbc_a = _conv_fwd(proj, p["conv_w"], conv_b)
    dtb_c, alog_c = dt_bias.reshape(HEADS, 1), a_log.reshape(HEADS, 1)
    dsk_exp = jnp.repeat(d_skip, HEAD_DIM, axis=1)
    y, hin, yn = _ssd_fwd(xbc_a, proj, dt_bias, a_log, dtb_c, alog_c, dsk_exp, ssd_norm_w)
    (y_ssd,) = _matmul(yn, p["w_branch_ssd"], mode="nn", out_dtypes=[F32], name="mm_branch_ssd")
    pooled, pw_out, yps = _pool_fwd(proj, p["pool_w"], pool_scale)
    (y_pool,) = _matmul(yps, p["w_branch_pool"], mode="nn", out_dtypes=[F32], name="mm_branch_pool")
    merged = _merge(proj, y_ssd, y_pool)
    resid = lambda acc, r, gt: (r + gt * acc, acc)
    x2, mix = _matmul(merged, p["w_out"], mode="nn", out_dtypes=[F32, BF16], name="mm_out",
                      epi=resid, tile_extras=(xs,), row_extras=(gate_m,))
    h2 = _norm_mod(x2, norm_mlp_w, scale_f, shift_f, "norm_mod_mlp")
    relu2 = lambda acc: (jnp.square(jnp.maximum(acc, 0.0)),)
    (act,) = _matmul(h2, p["w_up"], mode="nn", out_dtypes=[BF16], name="mm_up", epi=relu2)
    x3, down = _matmul(act, p["w_down"], mode="nn", out_dtypes=[F32, BF16], name="mm_down",
                       epi=resid, tile_extras=(x2,), row_extras=(gate_f,))

    red = _Reducer(idx)
    dx3, d_down, sums_f = _final_loss_bwd(x3, target, nf_w, down, gate_f)
    drelu2 = lambda acc, a: (acc * (2.0 * jnp.sqrt(a)).astype(F32),)
    (dup,) = _matmul(d_down, p["w_down"], mode="nt", out_dtypes=[BF16], name="mm_dact",
                     epi=drelu2, tile_extras=(act,))
    red.add("w_down", _matmul(act, d_down, mode="tn", out_dtypes=[BF16], name="mm_g_down")[0])
    (dh2,), got = _matmul(dup, p["w_up"], mode="nt", out_dtypes=[F32], name="mm_dh2",
                          carry=red.pair(["w_down"]))
    red.take_pair(["w_down"], got)
    red.add("w_up", _matmul(h2, dup, mode="tn", out_dtypes=[BF16], name="mm_g_up", chip_blocks=True)[0], chip_blocks=True)
    dx2, sums_2, dmix = _norm_mod_bwd(x2, dh2, dx3, norm_mlp_w, scale_f, "norm_mod_mlp_bwd", branch=mix, gate=gate_m)
    (dmerged,), got = _matmul(dmix, p["w_out"], mode="nt", out_dtypes=[F32], name="mm_dmerged",
                              carry=red.pair(["w_up"]))
    red.take_pair(["w_up"], got)
    red.add("w_out", _matmul(merged, dmix, mode="tn", out_dtypes=[BF16], name="mm_g_out")[0])
    dy_ssd, dy_pool, dproj = _merge_bwd(dmerged, proj, y_ssd, y_pool)
    (dyp,), got = _matmul(dy_pool, p["w_branch_pool"], mode="nt", out_dtypes=[F32], name="mm_dyp",
                          carry=red.pair(["w_out"]))
    red.take_pair(["w_out"], got)
    red.add("w_branch_pool", _matmul(yps, dy_pool, mode="tn", out_dtypes=[BF16], name="mm_g_bpool")[0])
    dproj, g_pool_w, sums_pool = _pool_bwd(dyp, pw_out, pooled, p["pool_w"], pool_scale, dproj)
    red.add("pool_w", g_pool_w.astype(BF16))
    red.add("w_branch_ssd", _matmul(yn, dy_ssd, mode="tn", out_dtypes=[BF16], name="mm_g_bssd")[0])
    mixers = ["w_branch_pool", "pool_w", "w_branch_ssd"]
    (dyn,), got = _matmul(dy_ssd, p["w_branch_ssd"], mode="nt", out_dtypes=[F32], name="mm_dyn",
                          carry=red.pair(mixers))
    red.take_pair(mixers, got)
    six = ["w_down", "w_up", "w_out"] + mixers
    (dxa, dproj, dsk_sum, ssd_small, dz, sums_gn), got = _ssd_bwd(
        dyn, y, xbc_a, proj, hin, dt_bias, a_log, dtb_c, alog_c, dsk_exp, ssd_norm_w, dproj, carry=red.chip(six))
    red.take_chip(six, got)
    dproj = lax.dynamic_update_slice(dproj, dz, (0, OFF_Z))
    dproj, sums_conv = _conv_bwd(dxa, proj, p["conv_w"], conv_b, dproj)
    rows_a = 3 * D // 4
    (g_in_a,), got = _matmul(h1, dproj, mode="tn", out_dtypes=[BF16], name="mm_g_in_a", a_cols=(0, rows_a),
                             carry=red.share(six))
    red.take_share(six, got)
    red.add("w_in_a", g_in_a)
    (g_in_b,), got = _matmul(h1, dproj, mode="tn", out_dtypes=[BF16], name="mm_g_in_b", a_cols=(rows_a, D - rows_a),
                             carry=red.pair(["w_in_a"]))
    red.take_pair(["w_in_a"], got)
    red.add("w_in_b", g_in_b)
    (dh1,), got = _matmul(dproj, p["w_in"], mode="nt", out_dtypes=[F32], name="mm_dh1",
                          carry=_join(red.chip(["w_in_a"]), red.pair(["w_in_b"])))
    red.take_chip(["w_in_a"], got[:1])
    red.take_pair(["w_in_b"], got[1:])
    grad_x, sums_1 = _norm_mod_bwd(xs, dh1, dx2, norm_mix_w, scale_m, "norm_mod_mix_bwd")

    dmod = jnp.concatenate([sums_1[0:1], sums_1[1:2], sums_2[3:4], sums_2[0:1], sums_2[1:2], sums_f[1:2]], axis=1)
    pad96 = jnp.zeros((1, 96), F32)
    small = {"dmod": dmod, "norm_mix_w": sums_1[2:3], "conv_b": sums_conv[4:5], "ssd_norm_w": sums_gn[0:1],
             "pool_scale": sums_pool[0:1], "norm_mlp_w": sums_2[2:3], "norm_final_w": sums_f[0:1],
             "conv_w": sums_conv[0:4].reshape(1, 4 * XBC), "d_skip": dsk_sum[0:1],
             "a_log": jnp.concatenate([ssd_small[0:1], pad96], axis=1),
             "dt_bias": jnp.concatenate([ssd_small[1:2], pad96], axis=1), "loss": sums_f[3:4, 0:128]}
    vec = jnp.concatenate([small[n] for n, _ in _SMALL_SEGS], axis=1)
    vec = jnp.pad(vec, ((0, 0), (0, SMALL_LEN - vec.shape[1]))).reshape(SMALL_LEN // 128, 128)
    (every, total, dsk), got = _gather_small(vec, carry=_join(red.chip(["w_in_b"]), red.share(["w_in_a"])))
    red.take_chip(["w_in_b"], got[:1])
    red.take_share(["w_in_a"], got[1:])
    total = total.reshape(1, SMALL_LEN)
    seg = lambda n, size: total[:, SMALL_OFF[n]:SMALL_OFF[n] + size]
    g.update({"b_ada": seg("dmod", 6 * D), "norm_mix_w": seg("norm_mix_w", D), "conv_b": seg("conv_b", XBC),
              "dt_bias": seg("dt_bias", HEADS), "a_log": seg("a_log", HEADS), "d_skip": dsk[:, 0:2].reshape(1, HEADS),
              "ssd_norm_w": seg("ssd_norm_w", DI), "pool_scale": seg("pool_scale", D),
              "norm_mlp_w": seg("norm_mlp_w", D), "norm_final_w": seg("norm_final_w", D)})
    loss = total[0, SMALL_OFF["loss"]]
    conv_cols = conv_w.shape[-1]
    g["conv_w"] = lax.dynamic_slice(seg("conv_w", 4 * XBC).reshape(4, XBC), (0, chip * conv_cols), (4, conv_cols))
    dmod8 = every.reshape(8, SMALL_LEN)[:, SMALL_OFF["dmod"]:SMALL_OFF["dmod"] + 6 * D]
    g["w_ada"] = _w_ada_grad(c8, lax.dynamic_slice(dmod8, (0, chip * ada_cols), (8, ada_cols)))

    got = adamw("w_ada", carry=red.share(["w_in_b"]))
    red.take_share(["w_in_b"], got)
    for n in six:
        g[n] = red.final[n]
    g["w_in"] = jnp.concatenate([red.final["w_in_a"], red.final["w_in_b"]], axis=0)
    for n in ["conv_w", "w_in"] + six:
        adamw(n)
    sizes = [w[n].size for n in _SMALL_REPLICATED]
    n_small = -(-sum(sizes) // 1024) * 1024
    pack = lambda d: jnp.pad(jnp.concatenate([d[n].reshape(1, -1) for n in _SMALL_REPLICATED], axis=1),
                             ((0, 0), (0, n_small - sum(sizes)))).reshape(n_small // 128, 128)
    d_, m_, v_ = _adamw(pack(w), pack(g), pack(m), pack(v), "adamw_small")
    off = 0
    for n, s in zip(_SMALL_REPLICATED, sizes):
        for dst, src in ((delta, d_), (new_m, m_), (new_v, v_)):
            dst[n] = src.reshape(1, n_small)[:, off:off + s]
        off += s

    out = [loss, grad_x.reshape(x.shape)]
    for d in (g, delta, new_m, new_v):
        out += [d[n].reshape(w[n].shape) for n in _WEIGHTS]
    return tuple(out)
```

```python
import functools
import operator

import jax
import jax.numpy as jnp
import numpy as np
from jax import lax
from jax.experimental import pallas as pl
from jax.experimental.pallas import tpu as pltpu

F32, BF16 = jnp.float32, jnp.bfloat16
HIGH = lax.Precision.HIGHEST
MESH = pl.DeviceIdType.MESH

D = 1024
DI = 2048
HEADS, HEAD_DIM = 32, 64
GROUPS, STATE = 4, 128
Q = 128
XBC = DI + 2 * GROUPS * STATE
POOL_WINDOWS = (2, 4, 8, 16)
GW = 256
DFF = 4096
EPS = 1e-5
IN_COLS = 8224
OFF_Z, OFF_XBC, OFF_POOL, OFF_GATE, OFF_DT, NP = 0, 2048, 5120, 6144, 8192, 8448
N_CHIPS = 4
ADAM_LR, ADAM_B1, ADAM_B2, ADAM_EPS, ADAM_WD, ADAM_STEP = 0.001, 0.9, 0.999, 1e-08, 0.01, 10
VMEM_LIMIT = 56 * 2 ** 20
NEG = -1e30


def _sigmoid(v):
    return 0.5 * jnp.tanh(0.5 * v) + 0.5


def _softplus(v):
    return jnp.maximum(v, 0.0) + jnp.log1p(jnp.exp(-jnp.abs(v)))


def _dot(a, b, dims, **kw):
    return lax.dot_general(a, b, (dims, ((), ())), preferred_element_type=F32, **kw)


def _nn(a, b, **kw):
    return _dot(a, b, ((1,), (0,)), **kw)


def _nt(a, b, **kw):
    return _dot(a, b, ((1,), (1,)), **kw)


def _tn(a, b, **kw):
    return _dot(a, b, ((0,), (0,)), **kw)


_DT_IN_CHIP2 = 5120 - 2 * (IN_COLS // 4)


class _Sems:
    def __init__(self, send, recv, local, base=0):
        self._send, self._recv, self._local, self._base = send, recv, local, base

    def shift(self, n):
        return _Sems(self._send, self._recv, self._local, self._base + n)

    def send(self, i):
        return self._send.at[self._base + i]

    def recv(self, i):
        return self._recv.at[self._base + i]

    def local(self, i):
        return self._local.at[self._base + i]


class _Carry:
    def __init__(self, ins, out_shapes, n_sems, start, finish, aliased=()):
        self.ins, self.out_shapes, self.n_sems, self.start, self.finish = list(ins), list(out_shapes), n_sems, start, finish
        self.aliased = list(aliased)


def _join(*carries):
    def run(which):
        def fn(ins, outs, sems):
            i = o = s = 0
            for cy in carries:
                getattr(cy, which)(ins[i:i + len(cy.ins)], outs[o:o + len(cy.out_shapes)], sems.shift(s))
                i, o, s = i + len(cy.ins), o + len(cy.out_shapes), s + cy.n_sems
        return fn

    aliased, i, o = [], 0, 0
    for cy in carries:
        aliased += [(i + a, o + b) for a, b in cy.aliased]
        i, o = i + len(cy.ins), o + len(cy.out_shapes)
    return _Carry([a for cy in carries for a in cy.ins], [a for cy in carries for a in cy.out_shapes],
                  sum(cy.n_sems for cy in carries), run("start"), run("finish"), aliased)


def _call(body, args, *, name, grid=(), in_specs, out_specs, out_shape, scratch_shapes=(), sem=None, aliases=None,
          carry=None):
    in_specs, out_specs, out_shape, scratch_shapes = list(in_specs), list(out_specs), list(out_shape), list(scratch_shapes)
    n_in, n_out, n_scr = len(in_specs), len(out_specs), len(scratch_shapes)
    kw = {"vmem_limit_bytes": VMEM_LIMIT}
    if carry is None:
        kernel_fn = functools.partial(body)
        if sem is not None:
            kw["dimension_semantics"] = sem
    else:
        n_ci, n_co = len(carry.ins), len(carry.out_shapes)
        hbm = pl.BlockSpec(memory_space=pl.ANY)
        in_specs += [hbm] * n_ci
        out_specs += [hbm] * n_co
        out_shape += carry.out_shapes
        n_s = max(carry.n_sems, 1)
        scratch_shapes += [pltpu.SemaphoreType.DMA((n_s,))] * 3
        args = list(args) + carry.ins
        aliases = dict(aliases or {})
        aliases.update({n_in + i: n_out + o for i, o in carry.aliased})
        if grid:
            kw["dimension_semantics"] = ("arbitrary",) * len(grid)

        def kernel_fn(*refs):
            a = n_in
            ins, c_ins = refs[:a], refs[a:a + n_ci]
            a += n_ci
            outs, c_outs = refs[a:a + n_out], refs[a + n_out:a + n_out + n_co]
            a += n_out + n_co
            scr, sems = refs[a:a + n_scr], _Sems(*refs[a + n_scr:a + n_scr + 3])
            if grid:
                ids = [pl.program_id(d) for d in range(len(grid))]
                first = functools.reduce(operator.and_, [i == 0 for i in ids])
                last = functools.reduce(operator.and_, [i == g - 1 for i, g in zip(ids, grid)])

                @pl.when(first)
                def _():
                    carry.start(c_ins, c_outs, sems)

                body(*ins, *outs, *scr)

                @pl.when(last)
                def _():
                    carry.finish(c_ins, c_outs, sems)
            else:
                carry.start(c_ins, c_outs, sems)
                body(*ins, *outs, *scr)
                carry.finish(c_ins, c_outs, sems)

    outs = pl.pallas_call(
        kernel_fn, name=name, grid=grid, in_specs=in_specs, out_specs=out_specs, out_shape=out_shape,
        scratch_shapes=scratch_shapes, input_output_aliases=aliases or {},
        compiler_params=pltpu.CompilerParams(**kw),
    )(*args)
    outs = list(outs)
    return outs if carry is None else (outs[:n_out], outs[n_out:])


def _run_carry(carry, name):
    _, outs = _call(lambda: None, [], name=name, in_specs=[], out_specs=[], out_shape=[], carry=carry)
    return outs


_TILES = {
    "mm_proj": (1024, 2816, 1024), "mm_branch_ssd": (1024, 1024, 2048), "mm_branch_pool": (1024, 1024, 1024),
    "mm_out": (1024, 1024, 1024), "mm_up": (2048, 1024, 1024), "mm_down": (512, 1024, 4096),
    "mm_dact": (1024, 1024, 1024), "mm_g_down": (1024, 1024, 4096), "mm_dh2": (1024, 1024, 4096),
    "mm_g_up": (1024, 1024, 4096), "mm_dmerged": (1024, 1024, 1024), "mm_g_out": (1024, 1024, 2048),
    "mm_dyp": (1024, 1024, 1024), "mm_g_bpool": (1024, 1024, 2048), "mm_g_bssd": (1024, 1024, 4096),
    "mm_dyn": (1024, 1024, 1024), "mm_g_in_a": (768, 1408, 4096), "mm_g_in_b": (256, 2816, 2048),
    "mm_dh1": (1024, 1024, 4224),
}


def _matmul(a, b, *, mode, out_dtypes, name, epi=None, tile_extras=(), row_extras=(), carry=None, a_cols=None,
            chip_blocks=False, cols_outer=False):
    M, K = (a.shape[1], a.shape[0]) if mode == "tn" else a.shape
    N = b.shape[0] if mode == "nt" else b.shape[1]
    a_start, M = a_cols if a_cols is not None else (0, M)
    tm, tn, tk = _TILES[name]
    tm, tn, tk = min(tm, M), min(tn, N), min(tk, K)
    assert M % tm == 0 and N % tn == 0 and K % tk == 0 and a_start % tm == 0, (name, M, N, K, tm, tn, tk)
    a_off = a_start // tm
    if mode == "nn":
        a_spec = pl.BlockSpec((tm, tk), lambda i, j, k: (i, k))
        b_spec = pl.BlockSpec((tk, tn), lambda i, j, k: (k, j))
        dims = ((1,), (0,))
    elif mode == "nt":
        a_spec = pl.BlockSpec((tm, tk), lambda i, j, k: (i, k))
        b_spec = pl.BlockSpec((tn, tk), lambda i, j, k: (j, k))
        dims = ((1,), (1,))
    else:
        a_spec = pl.BlockSpec((tk, tm), lambda i, j, k: (k, i + a_off))
        b_spec = pl.BlockSpec((tk, tn), lambda i, j, k: (k, j))
        dims = ((0,), (0,))
    nk = K // tk
    n_te, n_re, n_out = len(tile_extras), len(row_extras), len(out_dtypes)
    if epi is None:
        epi = lambda acc: (acc,)

    def body(a_ref, b_ref, *rest):
        extras = rest[:n_te + n_re]
        outs = rest[n_te + n_re:n_te + n_re + n_out]
        p = _dot(a_ref[...], b_ref[...], dims)

        def finish(acc):
            vals = epi(acc, *[e[...] for e in extras])
            for o, v in zip(outs, vals):
                o[...] = v.astype(o.dtype)

        if nk == 1:
            finish(p)
        else:
            acc_ref = rest[-1]
            k = pl.program_id(2)

            @pl.when(k == 0)
            def _():
                acc_ref[...] = p

            @pl.when(k > 0)
            def _():
                acc_ref[...] += p

            @pl.when(k == nk - 1)
            def _():
                finish(acc_ref[...])

    tile_spec = pl.BlockSpec((tm, tn), lambda i, j, k: (i, j))
    row_spec = pl.BlockSpec((1, tn), lambda i, j, k: (0, j))
    out_spec, out_dims = tile_spec, (M, N)
    if chip_blocks:
        assert n_te == 0 and tn * N_CHIPS == N
        out_spec, out_dims = pl.BlockSpec((None, tm, tn), lambda i, j, k: (j, i, 0)), (N_CHIPS, M, tn)
    in_specs, grid = [a_spec, b_spec] + [tile_spec] * n_te + [row_spec] * n_re, (M // tm, N // tn, nk)
    if cols_outer:
        swap = lambda s: pl.BlockSpec(s.block_shape, lambda g0, g1, k, f=s.index_map: f(g1, g0, k))
        in_specs, out_spec, grid = [swap(s) for s in in_specs], swap(out_spec), (N // tn, M // tm, nk)
    return _call(
        body, [a, b, *tile_extras, *row_extras], name=name, grid=grid,
        in_specs=in_specs, out_specs=[out_spec] * n_out,
        out_shape=[jax.ShapeDtypeStruct(out_dims, dt) for dt in out_dtypes],
        scratch_shapes=[pltpu.VMEM((tm, tn), F32)] if nk > 1 else [],
        sem=("parallel", "parallel", "arbitrary"), carry=carry)


def _row_tile(T):
    return min(512, T)


def _norm_mod(x, nw, scale, shift, name, carry=None):
    T = x.shape[0]
    tr = _row_tile(T)

    def body(x_ref, nw_ref, sc_ref, sh_ref, o_ref):
        xv = x_ref[...]
        r = lax.rsqrt(jnp.mean(xv * xv, axis=-1, keepdims=True) + EPS)
        o_ref[...] = ((xv * r) * nw_ref[...] * (1.0 + sc_ref[...]) + sh_ref[...]).astype(BF16)

    tile = pl.BlockSpec((tr, D), lambda i: (i, 0))
    row = pl.BlockSpec((1, D), lambda i: (0, 0))
    res = _call(body, [x, nw, scale, shift], name=name, grid=(T // tr,), in_specs=[tile, row, row, row],
                out_specs=[tile], out_shape=[jax.ShapeDtypeStruct((T, D), BF16)], sem=("parallel",), carry=carry)
    return res[0] if carry is None else (res[0][0], res[1])


def _norm_mod_bwd(x, dh, dres, nw, scale, name, branch=None, gate=None, carry=None):
    T = x.shape[0]
    tr = _row_tile(T)
    with_branch = branch is not None

    def body(x_ref, dh_ref, dr_ref, nw_ref, sc_ref, *rest):
        if with_branch:
            br_ref, g_ref, dx_ref, sums_ref, db_ref = rest
        else:
            dx_ref, sums_ref = rest
        i = pl.program_id(0)

        @pl.when(i == 0)
        def _():
            sums_ref[...] = jnp.zeros_like(sums_ref)

        xv, dhv = x_ref[...], dh_ref[...]
        r = lax.rsqrt(jnp.mean(xv * xv, axis=-1, keepdims=True) + EPS)
        xn = xv * r
        g1 = dhv * (1.0 + sc_ref[...])
        dxn = g1 * nw_ref[...]
        dx = dr_ref[...] + r * (dxn - xn * jnp.mean(dxn * xn, axis=-1, keepdims=True))
        dx_ref[...] = dx
        sums_ref[0:1, :] += jnp.sum(dhv, axis=0, keepdims=True)
        sums_ref[1:2, :] += jnp.sum(dhv * (xn * nw_ref[...]), axis=0, keepdims=True)
        sums_ref[2:3, :] += jnp.sum(g1 * xn, axis=0, keepdims=True)
        if with_branch:
            db_ref[...] = (dx * g_ref[...]).astype(BF16)
            sums_ref[3:4, :] += jnp.sum(dx * br_ref[...], axis=0, keepdims=True)

    tile = pl.BlockSpec((tr, D), lambda i: (i, 0))
    row = pl.BlockSpec((1, D), lambda i: (0, 0))
    sums = pl.BlockSpec((8, D), lambda i: (0, 0))
    ins = [x, dh, dres, nw, scale] + ([branch, gate] if with_branch else [])
    in_specs = [tile, tile, tile, row, row] + ([tile, row] if with_branch else [])
    out_specs = [tile, sums] + ([tile] if with_branch else [])
    out_shape = [jax.ShapeDtypeStruct((T, D), F32), jax.ShapeDtypeStruct((8, D), F32)]
    if with_branch:
        out_shape.append(jax.ShapeDtypeStruct((T, D), BF16))
    return _call(body, ins, name=name, grid=(T // tr,), in_specs=in_specs, out_specs=out_specs, out_shape=out_shape,
                 sem=("arbitrary",), carry=carry)


def _final_loss_bwd(x3, target, wf, down, gate_f):
    T = x3.shape[0]
    tr = _row_tile(T)
    n_steps = T // tr

    def body(x_ref, t_ref, w_ref, dn_ref, g_ref, dx_ref, dd_ref, sums_ref):
        i = pl.program_id(0)

        @pl.when(i == 0)
        def _():
            sums_ref[...] = jnp.zeros_like(sums_ref)

        xv = x_ref[...]
        r = lax.rsqrt(jnp.mean(xv * xv, axis=-1, keepdims=True) + EPS)
        xn = xv * r
        err = xn * w_ref[...] - t_ref[...]
        dy = err * (1.0 / D)
        dxn = dy * w_ref[...]
        dx = r * (dxn - xn * jnp.mean(dxn * xn, axis=-1, keepdims=True))
        dx_ref[...] = dx
        dd_ref[...] = (dx * g_ref[...]).astype(BF16)
        sums_ref[0:1, :] += jnp.sum(dy * xn, axis=0, keepdims=True)
        sums_ref[1:2, :] += jnp.sum(dx * dn_ref[...], axis=0, keepdims=True)
        sums_ref[2:3, :] += jnp.sum(err * err, axis=0, keepdims=True) * (0.5 / D)

        @pl.when(i == n_steps - 1)
        def _():
            sums_ref[3:4, :] = jnp.broadcast_to(jnp.sum(sums_ref[2:3, :], axis=1, keepdims=True), (1, D))

    tile = pl.BlockSpec((tr, D), lambda i: (i, 0))
    row = pl.BlockSpec((1, D), lambda i: (0, 0))
    sums = pl.BlockSpec((8, D), lambda i: (0, 0))
    return _call(body, [x3, target, wf, down, gate_f], name="final_loss_bwd", grid=(n_steps,),
                 in_specs=[tile, tile, row, tile, row], out_specs=[tile, tile, sums],
                 out_shape=[jax.ShapeDtypeStruct((T, D), F32), jax.ShapeDtypeStruct((T, D), BF16),
                            jax.ShapeDtypeStruct((8, D), F32)], sem=("arbitrary",))


CONV_TC = 1024


def _conv_taps(xp, w, b):
    acc = b + w[3:4, :] * xp
    for k in range(3):
        acc = acc + w[k:k + 1, :] * pltpu.roll(xp, 3 - k, 0)
    return acc


def _conv_fwd(proj, conv_w, conv_b):
    T = proj.shape[0]
    tr = _row_tile(T)
    nb, offb = tr // 8, OFF_XBC // CONV_TC

    def body(x_ref, h_ref, w_ref, b_ref, o_ref):
        halo = jnp.where(pl.program_id(0) > 0, h_ref[...], 0.0)
        xp = jnp.concatenate([halo, x_ref[...]], axis=0)
        acc = _conv_taps(xp, w_ref[...], b_ref[...])[8:]
        o_ref[...] = acc * _sigmoid(acc)

    return _call(
        body, [proj, proj, conv_w, conv_b], name="conv_fwd", grid=(T // tr, XBC // CONV_TC),
        in_specs=[pl.BlockSpec((tr, CONV_TC), lambda i, j: (i, j + offb)),
                  pl.BlockSpec((8, CONV_TC), lambda i, j: (jnp.maximum(i * nb - 1, 0), j + offb)),
                  pl.BlockSpec((4, CONV_TC), lambda i, j: (0, j)),
                  pl.BlockSpec((1, CONV_TC), lambda i, j: (0, j))],
        out_specs=[pl.BlockSpec((tr, CONV_TC), lambda i, j: (i, j))],
        out_shape=[jax.ShapeDtypeStruct((T, XBC), F32)], sem=("parallel", "parallel"))[0]


def _conv_bwd(dxa, proj, conv_w, conv_b, dproj):
    T = proj.shape[0]
    tr = _row_tile(T)
    nb, offb, last = tr // 8, OFF_XBC // CONV_TC, T // tr - 1
    prev8 = lambda i: jnp.maximum(i * nb - 1, 0)
    next8 = lambda i: jnp.minimum((i + 1) * nb, T // 8 - 1)

    def body(d_ref, dn_ref, x_ref, xp_ref, xn_ref, w_ref, b_ref, dp_in, o_ref, sums_ref):
        del dp_in
        i = pl.program_id(1)

        @pl.when(i == 0)
        def _():
            sums_ref[...] = jnp.zeros_like(sums_ref)

        x = jnp.concatenate([jnp.where(i > 0, xp_ref[...], 0.0), x_ref[...], jnp.where(i < last, xn_ref[...], 0.0)], axis=0)
        d = jnp.concatenate([d_ref[...], jnp.where(i < last, dn_ref[...], 0.0)], axis=0)
        w = w_ref[...]
        taps = [pltpu.roll(x, 3 - k, 0)[8:] for k in range(3)] + [x[8:]]
        acc = b_ref[...] + w[3:4, :] * taps[3]
        for k in range(3):
            acc = acc + w[k:k + 1, :] * taps[k]
        s = _sigmoid(acc)
        dxc = d * (s * (1.0 + acc * (1.0 - s)))
        n = tr + 8
        dx = w[3:4, :] * dxc
        for k in range(3):
            dx = dx + w[k:k + 1, :] * pltpu.roll(dxc, n - (3 - k), 0)
        o_ref[...] = dx[:tr].astype(BF16)
        own = dxc[:tr]
        for k in range(4):
            sums_ref[k:k + 1, :] += jnp.sum(own * taps[k][:tr], axis=0, keepdims=True)
        sums_ref[4:5, :] += jnp.sum(own, axis=0, keepdims=True)

    return _call(
        body, [dxa, dxa, proj, proj, proj, conv_w, conv_b, dproj], name="conv_bwd", grid=(XBC // CONV_TC, T // tr),
        in_specs=[pl.BlockSpec((tr, CONV_TC), lambda j, i: (i, j)),
                  pl.BlockSpec((8, CONV_TC), lambda j, i: (next8(i), j)),
                  pl.BlockSpec((tr, CONV_TC), lambda j, i: (i, j + offb)),
                  pl.BlockSpec((8, CONV_TC), lambda j, i: (prev8(i), j + offb)),
                  pl.BlockSpec((8, CONV_TC), lambda j, i: (next8(i), j + offb)),
                  pl.BlockSpec((4, CONV_TC), lambda j, i: (0, j)),
                  pl.BlockSpec((1, CONV_TC), lambda j, i: (0, j)),
                  pl.BlockSpec(memory_space=pl.ANY)],
        out_specs=[pl.BlockSpec((tr, CONV_TC), lambda j, i: (i, j + offb)), pl.BlockSpec((8, CONV_TC), lambda j, i: (0, j))],
        out_shape=[jax.ShapeDtypeStruct(dproj.shape, BF16), jax.ShapeDtypeStruct((8, XBC), F32)],
        aliases={7: 0}, sem=("parallel", "arbitrary"))


def _spread(v, sel, pieces):
    out = None
    for _ in range(pieces):
        p = v.astype(BF16)
        term = _nn(p, sel)
        out = term if out is None else out + term
        v = v - p.astype(F32)
    return out


def _ssd_selectors():
    g = np.arange(GROUPS)[:, None, None]
    piece = np.arange(128)[None, :, None]
    h = np.where(piece < 3 * HEADS, piece % HEADS, -1)
    blocks = (h == 8 * g + np.arange(1024)[None, None, :] // 128)
    pairs = (h == 8 * g + np.arange(512)[None, None, :] // HEAD_DIM)
    lane = np.arange(128)[None, None, :]
    block_sum = (lane == 8 * g + np.arange(1024)[None, :, None] // 128)
    pair_sum = (lane == 8 * g + np.arange(512)[None, :, None] // HEAD_DIM)
    return [jnp.asarray(m, BF16) for m in (blocks, pairs, block_sum, pair_sum)]


def _pack3(v):
    p0 = v.astype(BF16)
    r1 = v - p0.astype(F32)
    p1 = r1.astype(BF16)
    r2 = r1 - p1.astype(F32)
    return p0 + pltpu.roll(r1, HEADS, 1).astype(BF16) + pltpu.roll(r2, 2 * HEADS, 1).astype(BF16)


def _ssd_group(g, cs_p, csT, dt_p, s_mat, causal_w, lo, blocks_ref, pairs_ref):
    csb = _nn(cs_p, blocks_ref[g])
    row = jnp.concatenate([csT[8 * g + hh:8 * g + hh + 1, :] for hh in range(8)], axis=1)
    l_w = jnp.exp(jnp.where(causal_w, csb - row, NEG))
    m_w = jnp.concatenate([s_mat] * 8, axis=1) * l_w
    cs_g = jnp.concatenate([jnp.where(lo, csb[:, 256 * jj:256 * jj + 128], csb[:, 256 * jj + 128:256 * jj + 256])
                            for jj in range(4)], axis=1)
    cs_last = cs_g[Q - 1:Q, :]
    return m_w, l_w, _nn(dt_p, pairs_ref[g]), jnp.exp(cs_g), jnp.exp(cs_last - cs_g), jnp.exp(cs_last)


def _ssd_common(dtp_ref, dtb_r, alog_r, dtb_c, alog_c):
    rows = lax.broadcasted_iota(jnp.int32, (Q, Q), 0)
    cols = lax.broadcasted_iota(jnp.int32, (Q, Q), 1)
    tri = (cols <= rows).astype(F32)
    heads = lax.broadcasted_iota(jnp.int32, (1, 128), 1) < HEADS
    raw_w = dtp_ref[...] + dtb_r[...]
    dt_w = jnp.where(heads, _softplus(raw_w), 0.0)
    a_w = -jnp.exp(alog_r[...])
    cs_w = _nn(tri, dt_w * a_w, precision=HIGH)
    aT = _softplus(dtp_ref[...].T[0:HEADS, :] + dtb_c[...]) * (-jnp.exp(alog_c[...]))
    csT = _nt(aT, tri, precision=HIGH)
    return raw_w[:, 0:HEADS], dt_w[:, 0:HEADS], a_w[:, 0:HEADS], csT, _pack3(cs_w), _pack3(dt_w)


def _ssd_fwd(xbc_a, proj, dtb_r, alog_r, dtb_c, alog_c, dsk_exp, norm_w):
    T = xbc_a.shape[0]
    nc = T // Q
    dtb_r, alog_r = [jnp.pad(a, ((0, 0), (0, 128 - HEADS))) for a in (dtb_r, alog_r)]

    def body(xbc_ref, dtp_ref, z_ref, dtb_r_ref, alog_r_ref, dtb_c_ref, alog_c_ref, dsk_ref, nw_ref, blocks_ref,
             pairs_ref, y_ref, hin_ref, yn_ref, h_scr):
        @pl.when(pl.program_id(0) == 0)
        def _():
            h_scr[...] = jnp.zeros_like(h_scr)

        _, _, _, csT, cs_p, dt_p = _ssd_common(dtp_ref, dtb_r_ref, alog_r_ref, dtb_c_ref, alog_c_ref)
        lo = lax.broadcasted_iota(jnp.int32, (1, 128), 1) < HEAD_DIM
        hi = jnp.logical_not(lo)
        causal_w = (lax.broadcasted_iota(jnp.int32, (Q, 1024), 1) & (Q - 1)) <= lax.broadcasted_iota(jnp.int32, (Q, 1024), 0)
        for g in range(GROUPS):
            gs = slice(512 * g, 512 * (g + 1))
            hs = slice(128 * g, 128 * (g + 1))
            xs_g = xbc_ref[:, gs]
            b_g = xbc_ref[:, DI + STATE * g:DI + STATE * (g + 1)].astype(BF16)
            c_g = xbc_ref[:, DI + 512 + STATE * g:DI + 512 + STATE * (g + 1)].astype(BF16)
            m_w, _, dt_g, ecs_g, dec_g, cd_g = _ssd_group(g, cs_p, csT, dt_p, _nt(c_g, b_g), causal_w, lo, blocks_ref, pairs_ref)
            m_b = m_w.astype(BF16)
            xdt = xs_g * dt_g
            xdt_b = xdt.astype(BF16)
            ys = []
            for jj in range(4):
                xp = xdt_b[:, 128 * jj:128 * (jj + 1)]
                x_ab = jnp.concatenate([jnp.where(lo, xp, jnp.zeros_like(xp)), jnp.where(hi, xp, jnp.zeros_like(xp))], axis=0)
                ys.append(_nn(m_b[:, 256 * jj:256 * (jj + 1)], x_ab))
            h_g = h_scr[hs, :]
            hin_ref[0, hs, :] = h_g
            y_g = jnp.concatenate(ys, axis=1) + _nn(c_g, h_g.astype(BF16)) * ecs_g + dsk_ref[:, gs] * xs_g
            y_ref[:, gs] = y_g
            z = z_ref[:, gs]
            yg = y_g * (z * _sigmoid(z))
            r = lax.rsqrt(jnp.mean(yg * yg, axis=-1, keepdims=True) + EPS)
            yn_ref[:, gs] = (yg * r * nw_ref[:, gs]).astype(BF16)
            h_scr[hs, :] = h_g * cd_g + _tn(b_g, (xdt * dec_g).astype(BF16))

    small_r = pl.BlockSpec((1, 128), lambda c: (0, 0))
    small_c = pl.BlockSpec((HEADS, 1), lambda c: (0, 0))
    blocks, pairs, _, _ = _ssd_selectors()
    whole = lambda a: pl.BlockSpec(a.shape, lambda c: (0,) * a.ndim)
    wide, row = pl.BlockSpec((Q, DI), lambda c: (c, 0)), pl.BlockSpec((1, DI), lambda c: (0, 0))
    return _call(
        body, [xbc_a, proj, proj, dtb_r, alog_r, dtb_c, alog_c, dsk_exp, norm_w, blocks, pairs], name="ssd_fwd", grid=(nc,),
        in_specs=[pl.BlockSpec((Q, XBC), lambda c: (c, 0)),
                  pl.BlockSpec((Q, 128), lambda c: (c, OFF_DT // 128)), wide,
                  small_r, small_r, small_c, small_c, row, row, whole(blocks), whole(pairs)],
        out_specs=[wide, pl.BlockSpec((1, 512, 512), lambda c: (c, 0, 0)), wide],
        out_shape=[jax.ShapeDtypeStruct((T, DI), F32), jax.ShapeDtypeStruct((nc, 512, 512), F32),
                   jax.ShapeDtypeStruct((T, DI), BF16)],
        scratch_shapes=[pltpu.VMEM((512, 512), F32)], sem=("arbitrary",))


def _ssd_bwd(dyn, y, xbc_a, proj, hin, dtb_r, alog_r, dtb_c, alog_c, dsk_exp, norm_w, dproj, carry=None):
    T = xbc_a.shape[0]
    nc = T // Q
    dtb_r, alog_r = [jnp.pad(a, ((0, 0), (0, 128 - HEADS))) for a in (dtb_r, alog_r)]

    def body(dyn_ref, y_ref, z_ref, xbc_ref, dtp_ref, hin_ref, dtb_r_ref, alog_r_ref, dtb_c_ref, alog_c_ref, dsk_ref,
             nw_ref, dp_in, blocks_ref, pairs_ref, block_sum_ref, pair_sum_ref,
             dxa_ref, dz_ref, dsk_sum_ref, small_ref, dp_ref, gn_ref, dh_scr):
        del dp_in

        @pl.when(pl.program_id(0) == 0)
        def _():
            dh_scr[...] = jnp.zeros_like(dh_scr)
            dsk_sum_ref[...] = jnp.zeros_like(dsk_sum_ref)
            small_ref[...] = jnp.zeros_like(small_ref)
            gn_ref[...] = jnp.zeros_like(gn_ref)

        raw, dt, a_r, csT, cs_p, dt_p = _ssd_common(dtp_ref, dtb_r_ref, alog_r_ref, dtb_c_ref, alog_c_ref)
        lo = lax.broadcasted_iota(jnp.int32, (1, 128), 1) < HEAD_DIM
        hi = jnp.logical_not(lo)
        sub32 = lax.broadcasted_iota(jnp.int32, (HEADS, 1), 0)
        causal_w = (lax.broadcasted_iota(jnp.int32, (Q, 1024), 1) & (Q - 1)) <= lax.broadcasted_iota(jnp.int32, (Q, 1024), 0)
        dcs_c = jnp.zeros((Q, 128), F32)
        dcs_r = jnp.zeros((HEADS, Q), F32)
        dcs_l = jnp.zeros((8, 128), F32)
        ddt_x = jnp.zeros((Q, 128), F32)
        for g in range(GROUPS):
            gs = slice(512 * g, 512 * (g + 1))
            hs = slice(128 * g, 128 * (g + 1))
            xs_g = xbc_ref[:, gs]
            z, yv, d = z_ref[:, gs], y_ref[:, gs], dyn_ref[:, gs]
            s = _sigmoid(z)
            silu = z * s
            yg = yv * silu
            r = lax.rsqrt(jnp.mean(yg * yg, axis=-1, keepdims=True) + EPS)
            yn = yg * r
            gn_ref[0:1, gs] += jnp.sum(d * yn, axis=0, keepdims=True)
            dn = d * nw_ref[:, gs]
            dyg = r * (dn - yn * jnp.mean(dn * yn, axis=-1, keepdims=True))
            dy_g = dyg * silu
            dz_ref[:, gs] = (dyg * yv * (s * (1.0 + z * (1.0 - s)))).astype(BF16)
            b_g = xbc_ref[:, DI + STATE * g:DI + STATE * (g + 1)].astype(BF16)
            c_g = xbc_ref[:, DI + 512 + STATE * g:DI + 512 + STATE * (g + 1)].astype(BF16)
            m_w, l_w, dt_g, ecs_g, dec_g, cd_g = _ssd_group(g, cs_p, csT, dt_p, _nt(c_g, b_g), causal_w, lo, blocks_ref, pairs_ref)
            m_b = m_w.astype(BF16)
            xdt = xs_g * dt_g
            xdt_b, dy_b = xdt.astype(BF16), dy_g.astype(BF16)
            dms, dxs = [], []
            for jj in range(4):
                xp, dyp = xdt_b[:, 128 * jj:128 * (jj + 1)], dy_b[:, 128 * jj:128 * (jj + 1)]
                dy_ab = jnp.concatenate([jnp.where(lo, dyp, jnp.zeros_like(dyp)), jnp.where(hi, dyp, jnp.zeros_like(dyp))], axis=0)
                dm_ab = _nt(dy_ab, xp)
                dms += [dm_ab[:Q], dm_ab[Q:]]
                dx_ab = _tn(m_b[:, 256 * jj:256 * (jj + 1)], dyp)
                dxs.append(jnp.where(lo, dx_ab[:Q], dx_ab[Q:]))
            dm_w = jnp.concatenate(dms, axis=1)
            w_w = dm_w * m_w
            dcs_c = dcs_c + _spread(w_w, block_sum_ref[g], 2)
            w_cols = jnp.sum(w_w, axis=0, keepdims=True)
            for hh in range(8):
                dcs_r = dcs_r + jnp.where(sub32 == 8 * g + hh, w_cols[:, 128 * hh:128 * (hh + 1)], 0.0)
            dl_w = dm_w * l_w
            ds_mat = dl_w[:, 0:128]
            for hh in range(1, 8):
                ds_mat = ds_mat + dl_w[:, 128 * hh:128 * (hh + 1)]
            hin_g = hin_ref[0, hs, :]
            hin_b = hin_g.astype(BF16)
            dh_g = dh_scr[hs, :]
            dh_b = dh_g.astype(BF16)
            g_mat = _nn(b_g, dh_b)
            xdec = xdt * dec_g
            xg = xdec * g_mat
            dxdt = jnp.concatenate(dxs, axis=1) + dec_g * g_mat
            sums = _spread(jnp.concatenate([dy_g * (_nn(c_g, hin_b) * ecs_g) - xg, dxdt * xs_g], axis=0), pair_sum_ref[g], 2)
            dcs_c = dcs_c + sums[:Q]
            ddt_x = ddt_x + sums[Q:]
            last = jnp.sum(xg, axis=0, keepdims=True) + jnp.sum(dh_g * hin_g, axis=0, keepdims=True) * cd_g
            dcs_l = dcs_l + _spread(jnp.broadcast_to(last, (8, 512)), pair_sum_ref[g], 2)
            dz = (dy_g * ecs_g).astype(BF16)
            ds_b = ds_mat.astype(BF16)
            dxa_ref[:, gs] = dxdt * dt_g + dy_g * dsk_ref[:, gs]
            dxa_ref[:, DI + STATE * g:DI + STATE * (g + 1)] = _nt(xdec.astype(BF16), dh_b) + _tn(ds_b, c_g)
            dxa_ref[:, DI + 512 + STATE * g:DI + 512 + STATE * (g + 1)] = _nt(dz, hin_b) + _nn(ds_b, b_g)
            dh_scr[hs, :] = _tn(c_g, dz) + dh_g * cd_g
            dsk_sum_ref[0:1, gs] += jnp.sum(dy_g * xs_g, axis=0, keepdims=True)

        rows = lax.broadcasted_iota(jnp.int32, (Q, Q), 0)
        cols = lax.broadcasted_iota(jnp.int32, (Q, Q), 1)
        tri_t = (cols >= rows).astype(F32)
        last_row = lax.broadcasted_iota(jnp.int32, (Q, 1), 0) == Q - 1
        dcs = (dcs_c + jnp.where(last_row, dcs_l[0:1, :], 0.0))[:, 0:HEADS]
        da = _nn(tri_t, dcs, precision=HIGH) - _nt(tri_t, dcs_r, precision=HIGH)
        ddt_raw = (ddt_x[:, 0:HEADS] + da * a_r) * _sigmoid(raw)
        small_ref[0:1, :] += jnp.sum(da * dt, axis=0, keepdims=True) * a_r
        small_ref[1:2, :] += jnp.sum(ddt_raw, axis=0, keepdims=True)
        dp_ref[...] = jnp.zeros_like(dp_ref)
        dp_ref[:, 0:HEADS] = ddt_raw.astype(BF16)

    rev = lambda c: nc - 1 - c
    small_r = pl.BlockSpec((1, 128), lambda c: (0, 0))
    small_c = pl.BlockSpec((HEADS, 1), lambda c: (0, 0))
    selectors = _ssd_selectors()
    whole = lambda a: pl.BlockSpec(a.shape, lambda c: (0,) * a.ndim)
    wide, row = pl.BlockSpec((Q, DI), lambda c: (rev(c), 0)), pl.BlockSpec((1, DI), lambda c: (0, 0))
    sums = pl.BlockSpec((8, DI), lambda c: (0, 0))
    return _call(
        body, [dyn, y, proj, xbc_a, proj, hin, dtb_r, alog_r, dtb_c, alog_c, dsk_exp, norm_w, dproj, *selectors],
        name="ssd_bwd", grid=(nc,),
        in_specs=[wide, wide, wide,
                  pl.BlockSpec((Q, XBC), lambda c: (rev(c), 0)),
                  pl.BlockSpec((Q, 128), lambda c: (rev(c), OFF_DT // 128)),
                  pl.BlockSpec((1, 512, 512), lambda c: (rev(c), 0, 0)),
                  small_r, small_r, small_c, small_c, row, row,
                  pl.BlockSpec(memory_space=pl.ANY)] + [whole(a) for a in selectors],
        out_specs=[pl.BlockSpec((Q, XBC), lambda c: (rev(c), 0)),
                   pl.BlockSpec((Q, DI), lambda c: (rev(c), OFF_Z // DI)),
                   sums, pl.BlockSpec((8, HEADS), lambda c: (0, 0)), pl.BlockSpec((Q, 256), lambda c: (rev(c), 0)), sums],
        out_shape=[jax.ShapeDtypeStruct((T, XBC), F32), jax.ShapeDtypeStruct(dproj.shape, BF16),
                   jax.ShapeDtypeStruct((8, DI), F32), jax.ShapeDtypeStruct((8, HEADS), F32),
                   jax.ShapeDtypeStruct((T, 256), BF16), jax.ShapeDtypeStruct((8, DI), F32)],
        aliases={12: 1}, scratch_shapes=[pltpu.VMEM((512, 512), F32)], sem=("arbitrary",), carry=carry)


def _pool_fwd(proj, pool_w_b, pool_scale):
    T = proj.shape[0]
    tr = _row_tile(T)
    nb = tr // 16

    def body(u_ref, h_ref, pw_ref, ps_ref, pooled_ref, pw_out_ref, yps_ref):
        i = pl.program_id(0)
        t = i * tr + lax.broadcasted_iota(jnp.int32, (tr, 1), 0)
        for g, win in enumerate(POOL_WINDOWS):
            gs = slice(GW * g, GW * (g + 1))
            u = u_ref[:, gs]
            s = jnp.concatenate([jnp.where(i > 0, h_ref[:, gs], 0.0), u], axis=0)
            sh = 1
            while sh < win:
                s = s + pltpu.roll(s, sh, 0)
                sh *= 2
            pooled = (s[16:] * (1.0 / jnp.minimum(t + 1, win).astype(F32)) - u).astype(BF16)
            pooled_ref[:, gs] = pooled
            pwv = _nn(pooled, pw_ref[g])
            pw_out_ref[:, gs] = pwv
            yps_ref[:, gs] = (pwv * ps_ref[:, gs]).astype(BF16)

    tile = pl.BlockSpec((tr, D), lambda i: (i, 0))
    return _call(
        body, [proj, proj, pool_w_b, pool_scale], name="pool_fwd", grid=(T // tr,),
        in_specs=[pl.BlockSpec((tr, D), lambda i: (i, OFF_POOL // D)),
                  pl.BlockSpec((16, D), lambda i: (jnp.maximum(i * nb - 1, 0), OFF_POOL // D)),
                  pl.BlockSpec((4, GW, GW), lambda i: (0, 0, 0)),
                  pl.BlockSpec((1, D), lambda i: (0, 0))],
        out_specs=[tile, tile, tile],
        out_shape=[jax.ShapeDtypeStruct((T, D), BF16), jax.ShapeDtypeStruct((T, D), F32),
                   jax.ShapeDtypeStruct((T, D), BF16)], sem=("parallel",))


def _pool_bwd(dyp, pw_out, pooled, pool_w_b, pool_scale, dproj):
    T = dyp.shape[0]
    tr = _row_tile(T)
    nb, last = tr // 16, T // tr - 1

    def body(d_ref, h_ref, pwo_ref, pooled_ref, pw_ref, ps_ref, dp_in, du_ref, gpw_ref, sums_ref):
        del dp_in
        i = pl.program_id(0)

        @pl.when(i == 0)
        def _():
            gpw_ref[...] = jnp.zeros_like(gpw_ref)
            sums_ref[...] = jnp.zeros_like(sums_ref)

        n = tr + 16
        t = i * tr + lax.broadcasted_iota(jnp.int32, (n, 1), 0)
        sums_ref[0:1, :] += jnp.sum(d_ref[...] * pwo_ref[...], axis=0, keepdims=True)
        for g, win in enumerate(POOL_WINDOWS):
            gs = slice(GW * g, GW * (g + 1))
            d_ext = jnp.concatenate([d_ref[:, gs], jnp.where(i < last, h_ref[:, gs], 0.0)], axis=0)
            dpw = (d_ext * ps_ref[:, gs]).astype(BF16)
            dpooled = _nt(dpw, pw_ref[g])
            s = jnp.where(t < T, dpooled * (1.0 / jnp.minimum(t + 1, win).astype(F32)), 0.0)
            sh = 1
            while sh < win:
                s = s + pltpu.roll(s, n - sh, 0)
                sh *= 2
            du_ref[:, gs] = (s[:tr] - dpooled[:tr]).astype(BF16)
            gpw_ref[g] += _tn(pooled_ref[:, gs], dpw[:tr])

    tile = pl.BlockSpec((tr, D), lambda i: (i, 0))
    return _call(
        body, [dyp, dyp, pw_out, pooled, pool_w_b, pool_scale, dproj], name="pool_bwd", grid=(T // tr,),
        in_specs=[tile, pl.BlockSpec((16, D), lambda i: (jnp.minimum((i + 1) * nb, T // 16 - 1), 0)), tile, tile,
                  pl.BlockSpec((4, GW, GW), lambda i: (0, 0, 0)), pl.BlockSpec((1, D), lambda i: (0, 0)),
                  pl.BlockSpec(memory_space=pl.ANY)],
        out_specs=[pl.BlockSpec((tr, D), lambda i: (i, OFF_POOL // D)),
                   pl.BlockSpec((4, GW, GW), lambda i: (0, 0, 0)), pl.BlockSpec((8, D), lambda i: (0, 0))],
        out_shape=[jax.ShapeDtypeStruct(dproj.shape, BF16), jax.ShapeDtypeStruct((4, GW, GW), F32),
                   jax.ShapeDtypeStruct((8, D), F32)],
        aliases={6: 0}, sem=("arbitrary",))


def _merge(proj, y_ssd, y_pool):
    T = proj.shape[0]
    tr = _row_tile(T)

    def body(g_ref, a_ref, b_ref, o_ref):
        o_ref[...] = (_sigmoid(g_ref[:, 0:D]) * a_ref[...] + _sigmoid(g_ref[:, D:2 * D]) * b_ref[...]).astype(BF16)

    tile = pl.BlockSpec((tr, D), lambda i: (i, 0))
    return _call(body, [proj, y_ssd, y_pool], name="merge", grid=(T // tr,),
                 in_specs=[pl.BlockSpec((tr, 2 * D), lambda i: (i, OFF_GATE // (2 * D))), tile, tile], out_specs=[tile],
                 out_shape=[jax.ShapeDtypeStruct((T, D), BF16)], sem=("parallel",))[0]


def _merge_bwd(dmerged, proj, y_ssd, y_pool):
    T = proj.shape[0]
    tr = _row_tile(T)

    def body(d_ref, g_ref, a_ref, b_ref, da_ref, db_ref, dg_ref):
        d = d_ref[...]
        ga, gb = _sigmoid(g_ref[:, 0:D]), _sigmoid(g_ref[:, D:2 * D])
        da_ref[...] = (d * ga).astype(BF16)
        db_ref[...] = (d * gb).astype(BF16)
        dg_ref[:, 0:D] = (d * a_ref[...] * ga * (1.0 - ga)).astype(BF16)
        dg_ref[:, D:2 * D] = (d * b_ref[...] * gb * (1.0 - gb)).astype(BF16)

    tile = pl.BlockSpec((tr, D), lambda i: (i, 0))
    gates = pl.BlockSpec((tr, 2 * D), lambda i: (i, OFF_GATE // (2 * D)))
    return _call(body, [dmerged, proj, y_ssd, y_pool], name="merge_bwd", grid=(T // tr,),
                 in_specs=[tile, gates, tile, tile], out_specs=[tile, tile, gates],
                 out_shape=[jax.ShapeDtypeStruct((T, D), BF16), jax.ShapeDtypeStruct((T, D), BF16),
                            jax.ShapeDtypeStruct((T, NP), BF16)], sem=("parallel",))


def _adamw(w, g, m, v, name, carry=None):
    R, C = w.shape
    tr = R if R <= 128 else 128
    assert R % tr == 0

    def body(w_ref, g_ref, m_ref, v_ref, d_ref, mo_ref, vo_ref):
        gv = g_ref[...]
        mn = ADAM_B1 * m_ref[...] + (1.0 - ADAM_B1) * gv
        vn = ADAM_B2 * v_ref[...] + (1.0 - ADAM_B2) * (gv * gv)
        m_hat = mn * (1.0 / (1.0 - ADAM_B1 ** ADAM_STEP))
        v_hat = vn * (1.0 / (1.0 - ADAM_B2 ** ADAM_STEP))
        d_ref[...] = -ADAM_LR * (m_hat / (jnp.sqrt(v_hat) + ADAM_EPS) + ADAM_WD * w_ref[...])
        mo_ref[...] = mn
        vo_ref[...] = vn

    tile = pl.BlockSpec((tr, C), lambda i: (i, 0))
    sds = jax.ShapeDtypeStruct((R, C), F32)
    return _call(body, [w, g, m, v], name=name, grid=(R // tr,), in_specs=[tile] * 4, out_specs=[tile] * 3,
                 out_shape=[sds] * 3, sem=("parallel",), carry=carry)


def _me():
    return lax.axis_index("x"), lax.axis_index("y"), lax.axis_index("c")


def _xor_peer(x, y, c, p):
    return (x ^ ((p >> 2) & 1), y ^ ((p >> 1) & 1), c ^ (p & 1))


def _ada_fwd(c_row, w_ada, b_ada_mine, carry=None):
    n_cols = w_ada.shape[1]

    def body(c_ref, w_ref, b_ref, mod_ref, c8_ref, csend, mpart, modbuf, send_sems, recv_sems):
        x, y, c = _me()
        me = 4 * x + 2 * y + c
        chip = 2 * x + y
        csend[...] = jnp.broadcast_to(c_ref[...], csend.shape)
        c8_ref[me] = csend[...]

        def c_copy(p):
            return pltpu.make_async_remote_copy(
                src_ref=csend, dst_ref=c8_ref.at[me], send_sem=send_sems.at[p - 1], recv_sem=recv_sems.at[p - 1],
                device_id=_xor_peer(x, y, c, p), device_id_type=MESH)

        for p in range(1, 8):
            c_copy(p).start()
        for p in range(1, 8):
            c_copy(p).wait_recv()
        cs = jnp.concatenate([c8_ref[d][0:1, :] for d in range(8)], axis=0)
        mpart[...] = _nn(cs * _sigmoid(cs), w_ref[...], precision=HIGH) + b_ref[...]
        modbuf[chip] = mpart[...]

        def m_copy(m):
            return pltpu.make_async_remote_copy(
                src_ref=mpart, dst_ref=modbuf.at[chip], send_sem=send_sems.at[6 + m], recv_sem=recv_sems.at[6 + m],
                device_id=_xor_peer(x, y, c, 2 * m), device_id_type=MESH)

        for m in range(1, 4):
            m_copy(m).start()
        for m in range(1, 4):
            m_copy(m).wait_recv()
        mine = lax.broadcasted_iota(jnp.int32, (8, 1), 0) == me
        for k in range(N_CHIPS):
            mod_ref[:, n_cols * k:n_cols * (k + 1)] = jnp.sum(jnp.where(mine, modbuf[k], 0.0), axis=0, keepdims=True)
        for p in range(1, 8):
            c_copy(p).wait_send()
        for m in range(1, 4):
            m_copy(m).wait_send()

    vmem = pl.BlockSpec(memory_space=pltpu.VMEM)
    return _call(
        body, [c_row, w_ada, b_ada_mine], name="ada_fwd", in_specs=[vmem, vmem, vmem], out_specs=[vmem, vmem],
        out_shape=[jax.ShapeDtypeStruct((1, N_CHIPS * n_cols), F32), jax.ShapeDtypeStruct((8, 8, D), F32)],
        scratch_shapes=[pltpu.VMEM((8, D), F32), pltpu.VMEM((8, n_cols), F32), pltpu.VMEM((N_CHIPS, 8, n_cols), F32),
                        pltpu.SemaphoreType.DMA((10,)), pltpu.SemaphoreType.DMA((10,))], carry=carry)


def _gather_small(vec, carry=None):
    rows = vec.shape[0]

    def body(v_ref, all_ref, tot_ref, dsk_ref, send_sems, recv_sems):
        x, y, c = _me()
        me = 4 * x + 2 * y + c
        all_ref[me] = v_ref[...]

        def copy(p):
            return pltpu.make_async_remote_copy(
                src_ref=v_ref, dst_ref=all_ref.at[me], send_sem=send_sems.at[p - 1], recv_sem=recv_sems.at[p - 1],
                device_id=_xor_peer(x, y, c, p), device_id_type=MESH)

        for p in range(1, 8):
            copy(p).start()
        for p in range(1, 8):
            copy(p).wait_recv()
        tot = all_ref[0]
        for d in range(1, 8):
            tot = tot + all_ref[d]
        tot_ref[...] = tot
        seg = tot[SMALL_OFF["d_skip"] // 128:SMALL_OFF["d_skip"] // 128 + 16, :]
        lane = lax.broadcasted_iota(jnp.int32, (1, 128), 1)
        sa = jnp.sum(jnp.where(lane < HEAD_DIM, seg, 0.0), axis=1, keepdims=True)
        sb = jnp.sum(jnp.where(lane < HEAD_DIM, 0.0, seg), axis=1, keepdims=True)
        dsk_ref[...] = jnp.where(lane == 0, sa, jnp.where(lane == 1, sb, 0.0))
        for p in range(1, 8):
            copy(p).wait_send()

    vmem = pl.BlockSpec(memory_space=pltpu.VMEM)
    return _call(
        body, [vec], name="gather_small", in_specs=[vmem], out_specs=[vmem, vmem, vmem],
        out_shape=[jax.ShapeDtypeStruct((8, rows, 128), F32), jax.ShapeDtypeStruct((rows, 128), F32),
                   jax.ShapeDtypeStruct((16, 128), F32)],
        scratch_shapes=[pltpu.SemaphoreType.DMA((7,)), pltpu.SemaphoreType.DMA((7,))], carry=carry)


def _gather_carry(shards):
    n = len(shards)

    def copies(ins, outs, sems):
        x, y, c = _me()
        chip = 2 * x + y

        def half(w, which):
            h = shards[w].shape[0] // 2
            return pl.ds(which * h, h)

        def first(w, m):
            return pltpu.make_async_remote_copy(
                src_ref=ins[w].at[half(w, c)], dst_ref=outs[w].at[chip, half(w, c)],
                send_sem=sems.send(6 * w + m - 1), recv_sem=sems.recv(6 * w + m - 1),
                device_id=_xor_peer(x, y, c, 2 * m), device_id_type=MESH)

        def landed(w, m):
            return pltpu.make_async_remote_copy(
                src_ref=ins[w].at[half(w, c)], dst_ref=outs[w].at[chip ^ m, half(w, c)],
                send_sem=sems.send(6 * w + m - 1), recv_sem=sems.recv(6 * w + m - 1),
                device_id=_xor_peer(x, y, c, 2 * m), device_id_type=MESH)

        def passed(w, m, which):
            part = outs[w].at[chip ^ m, half(w, which)]
            return pltpu.make_async_remote_copy(
                src_ref=part, dst_ref=part, send_sem=sems.send(6 * w + 2 + m), recv_sem=sems.recv(6 * w + 2 + m),
                device_id=(x, y, 1 - c), device_id_type=MESH)

        return c, first, landed, passed

    pairs = [(w, m) for w in range(n) for m in range(1, 4)]

    def start(ins, outs, sems):
        _, first, _, _ = copies(ins, outs, sems)
        for w, m in pairs:
            first(w, m).start()

    def finish(ins, outs, sems):
        c, first, landed, passed = copies(ins, outs, sems)
        for w, m in pairs:
            landed(w, m).wait_recv()
            passed(w, m, c).start()
        for w, m in pairs:
            passed(w, m, 1 - c).wait_recv()
        for w, m in pairs:
            first(w, m).wait_send()
            passed(w, m, c).wait_send()

    return _Carry(shards, [jax.ShapeDtypeStruct((N_CHIPS,) + s.shape, s.dtype) for s in shards], 6 * n, start, finish)


def _pair_exchange_carry(grads):
    n = len(grads)

    def copy(ins, outs, sems, w):
        x, y, c = _me()
        h = grads[w].shape[1] // 2
        return pltpu.make_async_remote_copy(
            src_ref=ins[w].at[:, pl.ds((1 - c) * h, h)], dst_ref=outs[w],
            send_sem=sems.send(w), recv_sem=sems.recv(w), device_id=(x, y, 1 - c), device_id_type=MESH)

    def start(ins, outs, sems):
        for w in range(n):
            copy(ins, outs, sems, w).start()

    def finish(ins, outs, sems):
        for w in range(n):
            copy(ins, outs, sems, w).wait()

    return _Carry(grads, [jax.ShapeDtypeStruct((N_CHIPS, g.shape[1] // 2, g.shape[2]), g.dtype) for g in grads], n,
                  start, finish)


def _chip_exchange_carry(partials):
    n = len(partials)

    def copier(ins, outs, sems):
        x, y, c = _me()
        chip = 2 * x + y

        def copy(w, m, landed):
            return pltpu.make_async_remote_copy(
                src_ref=ins[w].at[chip ^ m], dst_ref=outs[w].at[(chip ^ m) if landed else chip],
                send_sem=sems.send(3 * w + m - 1), recv_sem=sems.recv(3 * w + m - 1),
                device_id=_xor_peer(x, y, c, 2 * m), device_id_type=MESH)

        return copy

    pairs = [(w, m) for w in range(n) for m in range(1, 4)]

    def start(ins, outs, sems):
        copy = copier(ins, outs, sems)
        for w, m in pairs:
            copy(w, m, False).start()

    def finish(ins, outs, sems):
        copy = copier(ins, outs, sems)
        for w, m in pairs:
            copy(w, m, True).wait_recv()
        for w, m in pairs:
            copy(w, m, False).wait_send()

    return _Carry(partials, [jax.ShapeDtypeStruct(p.shape, p.dtype) for p in partials], 3 * n, start, finish)


def _pair_share_carry(shards):
    n = len(shards)

    def copier(ins, outs, sems):
        x, y, c = _me()

        def copy(w, which):
            h = shards[w].shape[0] // 2
            rows = pl.ds(which * h, h)
            return pltpu.make_async_remote_copy(
                src_ref=ins[w].at[rows], dst_ref=outs[w].at[rows],
                send_sem=sems.send(w), recv_sem=sems.recv(w), device_id=(x, y, 1 - c), device_id_type=MESH)

        return c, copy

    def start(ins, outs, sems):
        c, copy = copier(ins, outs, sems)
        for w in range(n):
            copy(w, c).start()

    def finish(ins, outs, sems):
        c, copy = copier(ins, outs, sems)
        for w in range(n):
            copy(w, 1 - c).wait_recv()
        for w in range(n):
            copy(w, c).wait_send()

    return _Carry(shards, [jax.ShapeDtypeStruct(s.shape, s.dtype) for s in shards], n, start, finish,
                  aliased=[(w, w) for w in range(n)])


def _pair_sum(g, part, idx, name):
    _, h, C = part.shape
    tr = min(512, h)
    nb = h // tr

    def body(idx_ref, g_ref, p_ref, o16_ref, own_ref):
        v = g_ref[...].astype(F32) + p_ref[...].astype(F32)
        o16_ref[...] = v.astype(BF16)

        @pl.when(pl.program_id(1) == idx_ref[1])
        def _():
            own_ref[...] = v

    return pl.pallas_call(
        body, name=name,
        grid_spec=pltpu.PrefetchScalarGridSpec(
            num_scalar_prefetch=1, grid=(nb, N_CHIPS),
            in_specs=[pl.BlockSpec((None, tr, C), lambda i, s, idx_ref: (s, idx_ref[0] * nb + i, 0)),
                      pl.BlockSpec((None, tr, C), lambda i, s, idx_ref: (s, i, 0))],
            out_specs=[pl.BlockSpec((None, tr, C), lambda i, s, idx_ref: (s, i, 0)),
                       pl.BlockSpec((tr, C), lambda i, s, idx_ref: (i, 0))]),
        out_shape=[jax.ShapeDtypeStruct(part.shape, BF16), jax.ShapeDtypeStruct((h, C), F32)],
        compiler_params=pltpu.CompilerParams(dimension_semantics=("arbitrary", "arbitrary"), vmem_limit_bytes=VMEM_LIMIT),
    )(idx, g, part)


def _chip_sum(own, slots, idx, name):
    h, C = own.shape
    tr = min(512, h)
    nb = h // tr

    def body(idx_ref, own_ref, s1_ref, s2_ref, s3_ref, o_ref):
        del idx_ref
        o_ref[...] = ((own_ref[...] + s1_ref[...].astype(F32)) + s2_ref[...].astype(F32)) + s3_ref[...].astype(F32)

    def slot(m):
        return pl.BlockSpec((None, tr, C), lambda i, idx_ref: (idx_ref[1] ^ m, i, 0))

    return pl.pallas_call(
        body, name=name,
        grid_spec=pltpu.PrefetchScalarGridSpec(
            num_scalar_prefetch=1, grid=(nb,),
            in_specs=[pl.BlockSpec((tr, C), lambda i, idx_ref: (i, 0)), slot(1), slot(2), slot(3)],
            out_specs=pl.BlockSpec((tr, C), lambda i, idx_ref: (idx_ref[0] * nb + i, 0))),
        out_shape=jax.ShapeDtypeStruct((2 * h, C), F32),
        compiler_params=pltpu.CompilerParams(dimension_semantics=("parallel",), vmem_limit_bytes=VMEM_LIMIT),
    )(idx, own, slots, slots, slots)


class _Reducer:
    def __init__(self, idx):
        self.idx, self.chips, self.p16, self.own, self.mine, self.final = idx, {}, {}, {}, {}, {}

    def add(self, name, whole, chip_blocks=False):
        self.chips[name] = whole if chip_blocks else _chips_from_whole(name, whole)

    def pair(self, names):
        return _pair_exchange_carry([self.chips[n] for n in names])

    def take_pair(self, names, outs):
        for n, part in zip(names, outs):
            self.p16[n], self.own[n] = _pair_sum(self.chips.pop(n), part, self.idx, "pair_sum_" + n)

    def chip(self, names):
        return _chip_exchange_carry([self.p16[n] for n in names])

    def take_chip(self, names, outs):
        for n, slots in zip(names, outs):
            del self.p16[n]
            self.mine[n] = _chip_sum(self.own.pop(n), slots, self.idx, "chip_sum_" + n)

    def share(self, names):
        return _pair_share_carry([self.mine[n] for n in names])

    def take_share(self, names, outs):
        for n, s in zip(names, outs):
            del self.mine[n]
            self.final[n] = s


def _w_ada_grad(c8, dmod_cols):
    n_cols = dmod_cols.shape[1]
    tn = 512

    def body(c_ref, d_ref, o_ref):
        cv = c_ref[...]
        o_ref[...] = _tn(cv * _sigmoid(cv), d_ref[...], precision=HIGH)

    return _call(body, [c8, dmod_cols], name="w_ada_grad", grid=(n_cols // tn,),
                 in_specs=[pl.BlockSpec((8, D), lambda j: (0, 0)), pl.BlockSpec((8, tn), lambda j: (0, j))],
                 out_specs=[pl.BlockSpec((D, tn), lambda j: (0, j))],
                 out_shape=[jax.ShapeDtypeStruct((D, n_cols), F32)], sem=("parallel",))[0]


_SMALL_SEGS = (("dmod", 6144), ("norm_mix_w", 1024), ("conv_b", 3072), ("ssd_norm_w", 2048), ("pool_scale", 1024),
               ("norm_mlp_w", 1024), ("norm_final_w", 1024), ("conv_w", 4 * XBC), ("d_skip", 2048), ("a_log", 128),
               ("dt_bias", 128), ("loss", 128))
SMALL_OFF = {}
_o = 0
for _n, _s in _SMALL_SEGS:
    SMALL_OFF[_n] = _o
    _o += _s
SMALL_LEN = -(-_o // 1024) * 1024

_FIRST = ("w_in", "conv_w")
_LATER = ("w_branch_ssd", "pool_w", "w_branch_pool", "w_out", "w_up", "w_down")
_SMALL_REPLICATED = ("b_ada", "norm_mix_w", "conv_b", "dt_bias", "a_log", "d_skip", "ssd_norm_w", "pool_scale",
                     "norm_mlp_w", "norm_final_w")
_WEIGHTS = ("w_ada", "b_ada", "norm_mix_w", "w_in", "conv_w", "conv_b", "dt_bias", "a_log", "d_skip", "ssd_norm_w",
            "w_branch_ssd", "pool_w", "pool_scale", "w_branch_pool", "w_out", "norm_mlp_w", "w_up", "w_down",
            "norm_final_w")


def _shard_2d(name, a):
    if name == "conv_w":
        return a.reshape(16, -1)
    return (a.reshape(GW, GW) if name == "pool_w" else a.reshape(a.shape[-2], a.shape[-1])).astype(BF16)


def _whole_from_chips(name, g, own, chip):
    g = lax.dynamic_update_slice(g, own[None], (chip, 0, 0))
    if name == "w_in":
        a, b = _DT_IN_CHIP2, _DT_IN_CHIP2 + HEADS
        pad = jnp.zeros((D, NP - IN_COLS), g.dtype)
        return jnp.concatenate([g[0], g[1], g[2][:, :a], g[2][:, b:], g[3], g[2][:, a:b], pad], axis=1)
    if name == "w_up":
        return jnp.concatenate([g[k] for k in range(N_CHIPS)], axis=1)
    if name == "pool_w":
        return jnp.transpose(g.reshape(N_CHIPS, 4, GW // N_CHIPS, GW), (1, 0, 2, 3)).reshape(4, GW, GW)
    if name == "conv_w":
        return jnp.transpose(g.reshape(N_CHIPS, 4, XBC // N_CHIPS), (1, 0, 2)).reshape(4, XBC)
    return g.reshape(N_CHIPS * g.shape[1], g.shape[2])


def _chips_from_whole(name, g):
    if name.startswith("w_in"):
        cw, a = IN_COLS // N_CHIPS, _DT_IN_CHIP2
        chip2 = jnp.concatenate([g[:, 2 * cw:2 * cw + a], g[:, OFF_DT:OFF_DT + HEADS], g[:, 5120:3 * cw - HEADS]], axis=1)
        return jnp.stack([g[:, :cw], g[:, cw:2 * cw], chip2, g[:, 3 * cw - HEADS:OFF_DT]])
    if name == "w_up":
        return jnp.transpose(g.reshape(D, N_CHIPS, DFF // N_CHIPS), (1, 0, 2))
    if name == "pool_w":
        return jnp.transpose(g.reshape(4, N_CHIPS, GW // N_CHIPS, GW), (1, 0, 2, 3)).reshape(N_CHIPS, GW, GW)
    return g.reshape(N_CHIPS, g.shape[0] // N_CHIPS, g.shape[1])


def kernel(x, c, w_ada, b_ada, norm_mix_w, w_in, conv_w, conv_b, dt_bias, a_log, d_skip, ssd_norm_w, w_branch_ssd, pool_w, pool_scale, w_branch_pool, w_out, norm_mlp_w, w_up, w_down, norm_final_w, loss_target, m_w_ada, m_b_ada, m_norm_mix_w, m_w_in, m_conv_w, m_conv_b, m_dt_bias, m_a_log, m_d_skip, m_ssd_norm_w, m_w_branch_ssd, m_pool_w, m_pool_scale, m_w_branch_pool, m_w_out, m_norm_mlp_w, m_w_up, m_w_down, m_norm_final_w, v_w_ada, v_b_ada, v_norm_mix_w, v_w_in, v_conv_w, v_conv_b, v_dt_bias, v_a_log, v_d_skip, v_ssd_norm_w, v_w_branch_ssd, v_pool_w, v_pool_scale, v_w_branch_pool, v_w_out, v_norm_mlp_w, v_w_up, v_w_down, v_norm_final_w):
    args = locals()
    w = {n: args[n] for n in _WEIGHTS}
    m = {n: args["m_" + n] for n in _WEIGHTS}
    v = {n: args["v_" + n] for n in _WEIGHTS}
    xi, yi, ci = _me()
    chip = 2 * xi + yi
    idx = jnp.stack([ci, chip]).astype(jnp.int32)
    ada_cols = w_ada.shape[-1]
    xs, target = x[0], loss_target[0]
    two_d = lambda n, a: a.reshape(GW, GW) if n == "pool_w" else a.reshape(-1, a.shape[-1])
    delta, new_m, new_v, g = {}, {}, {}, {}

    def adamw(n, carry=None):
        res = _adamw(two_d(n, w[n]), two_d(n, g[n]), two_d(n, m[n]), two_d(n, v[n]), "adamw_" + n, carry=carry)
        (delta[n], new_m[n], new_v[n]), extra = res if carry is not None else (res, None)
        return extra

    b_mine = lax.dynamic_slice(b_ada, (0, chip * ada_cols), (1, ada_cols))
    shards = {n: _shard_2d(n, w[n]) for n in _FIRST + _LATER}
    mod, c8 = _ada_fwd(c, w_ada[0], b_mine)
    c8 = c8[:, 0, :]
    shift_m, scale_m, gate_m, shift_f, scale_f, gate_f = [mod[:, D * i:D * (i + 1)] for i in range(6)]
    nf_w = norm_final_w.reshape(1, D)

    h1, first = _norm_mod(xs, norm_mix_w, scale_m, shift_m, "norm_mod_mix",
                          carry=_gather_carry([shards[n] for n in _FIRST]))
    p ={n: _whole_from_chips(n, a, shards[n], chip) for n, a in zip(_FIRST, first)}
    (proj,), later = _matmul(h1, p["w_in"], mode="nn", out_dtypes=[F32], name="mm_proj", cols_outer=True,
                             carry=_gather_carry([shards[n] for n in _LATER]))
    p.update({n: _whole_from_chips(n, a, shards[n], chip) for n, a in zip(_LATER, later)})
    xbc_a = _conv_fwd(proj, p["conv_w"], conv_b)
    dtb_c, alog_c = dt_bias.reshape(HEADS, 1), a_log.reshape(HEADS, 1)
    dsk_exp = jnp.repeat(d_skip, HEAD_DIM, axis=1)
    y, hin, yn = _ssd_fwd(xbc_a, proj, dt_bias, a_log, dtb_c, alog_c, dsk_exp, ssd_norm_w)
    (y_ssd,) = _matmul(yn, p["w_branch_ssd"], mode="nn", out_dtypes=[F32], name="mm_branch_ssd")
    pooled, pw_out, yps = _pool_fwd(proj, p["pool_w"], pool_scale)
    (y_pool,) = _matmul(yps, p["w_branch_pool"], mode="nn", out_dtypes=[F32], name="mm_branch_pool")
    merged = _merge(proj, y_ssd, y_pool)
    resid = lambda acc, r, gt: (r + gt * acc, acc)
    x2, mix = _matmul(merged, p["w_out"], mode="nn", out_dtypes=[F32, BF16], name="mm_out",
                      epi=resid, tile_extras=(xs,), row_extras=(gate_m,))
    h2 = _norm_mod(x2, norm_mlp_w, scale_f, shift_f, "norm_mod_mlp")
    relu2 = lambda acc: (jnp.square(jnp.maximum(acc, 0.0)),)
    (act,) = _matmul(h2, p["w_up"], mode="nn", out_dtypes=[BF16], name="mm_up", epi=relu2)
    x3, down = _matmul(act, p["w_down"], mode="nn", out_dtypes=[F32, BF16], name="mm_down",
                       epi=resid, tile_extras=(x2,), row_extras=(gate_f,))

    red = _Reducer(idx)
    dx3, d_down, sums_f = _final_loss_bwd(x3, target, nf_w, down, gate_f)
    drelu2 = lambda acc, a: (acc * (2.0 * jnp.sqrt(a)).astype(F32),)
    (dup,) = _matmul(d_down, p["w_down"], mode="nt", out_dtypes=[BF16], name="mm_dact",
                     epi=drelu2, tile_extras=(act,))
    red.add("w_down", _matmul(act, d_down, mode="tn", out_dtypes=[BF16], name="mm_g_down")[0])
    (dh2,), got = _matmul(dup, p["w_up"], mode="nt", out_dtypes=[F32], name="mm_dh2",
                          carry=red.pair(["w_down"]))
    red.take_pair(["w_down"], got)
    red.add("w_up", _matmul(h2, dup, mode="tn", out_dtypes=[BF16], name="mm_g_up", chip_blocks=True)[0], chip_blocks=True)
    dx2, sums_2, dmix = _norm_mod_bwd(x2, dh2, dx3, norm_mlp_w, scale_f, "norm_mod_mlp_bwd", branch=mix, gate=gate_m)
    (dmerged,), got = _matmul(dmix, p["w_out"], mode="nt", out_dtypes=[F32], name="mm_dmerged",
                              carry=red.pair(["w_up"]))
    red.take_pair(["w_up"], got)
    red.add("w_out", _matmul(merged, dmix, mode="tn", out_dtypes=[BF16], name="mm_g_out")[0])
    dy_ssd, dy_pool, dproj = _merge_bwd(dmerged, proj, y_ssd, y_pool)
    (dyp,), got = _matmul(dy_pool, p["w_branch_pool"], mode="nt", out_dtypes=[F32], name="mm_dyp",
                          carry=red.pair(["w_out"]))
    red.take_pair(["w_out"], got)
    red.add("w_branch_pool", _matmul(yps, dy_pool, mode="tn", out_dtypes=[BF16], name="mm_g_bpool")[0])
    dproj, g_pool_w, sums_pool = _pool_bwd(dyp, pw_out, pooled, p["pool_w"], pool_scale, dproj)
    red.add("pool_w", g_pool_w.astype(BF16))
    red.add("w_branch_ssd", _matmul(yn, dy_ssd, mode="tn", out_dtypes=[BF16], name="mm_g_bssd")[0])
    mixers = ["w_branch_pool", "pool_w", "w_branch_ssd"]
    (dyn,), got = _matmul(dy_ssd, p["w_branch_ssd"], mode="nt", out_dtypes=[F32], name="mm_dyn",
                          carry=red.pair(mixers))
    red.take_pair(mixers, got)
    six = ["w_down", "w_up", "w_out"] + mixers
    (dxa, dproj, dsk_sum, ssd_small, ddt, sums_gn), got = _ssd_bwd(
        dyn, y, xbc_a, proj, hin, dt_bias, a_log, dtb_c, alog_c, dsk_exp, ssd_norm_w, dproj, carry=red.chip(six))
    red.take_chip(six, got)
    dproj = lax.dynamic_update_slice(dproj, ddt, (0, OFF_DT))
    dproj, sums_conv = _conv_bwd(dxa, proj, p["conv_w"], conv_b, dproj)
    rows_a = 3 * D // 4
    (g_in_a,), got = _matmul(h1, dproj, mode="tn", out_dtypes=[BF16], name="mm_g_in_a", a_cols=(0, rows_a),
                             carry=red.share(six))
    red.take_share(six, got)
    red.add("w_in_a", g_in_a)
    (g_in_b,), got = _matmul(h1, dproj, mode="tn", out_dtypes=[BF16], name="mm_g_in_b", a_cols=(rows_a, D - rows_a),
                             carry=red.pair(["w_in_a"]))
    red.take_pair(["w_in_a"], got)
    red.add("w_in_b", g_in_b)
    (dh1,), got = _matmul(dproj, p["w_in"], mode="nt", out_dtypes=[F32], name="mm_dh1",
                          carry=_join(red.chip(["w_in_a"]), red.pair(["w_in_b"])))
    red.take_chip(["w_in_a"], got[:1])
    red.take_pair(["w_in_b"], got[1:])
    grad_x, sums_1 = _norm_mod_bwd(xs, dh1, dx2, norm_mix_w, scale_m, "norm_mod_mix_bwd")

    dmod = jnp.concatenate([sums_1[0:1], sums_1[1:2], sums_2[3:4], sums_2[0:1], sums_2[1:2], sums_f[1:2]], axis=1)
    pad96 = jnp.zeros((1, 96), F32)
    small = {"dmod": dmod, "norm_mix_w": sums_1[2:3], "conv_b": sums_conv[4:5], "ssd_norm_w": sums_gn[0:1],
             "pool_scale": sums_pool[0:1], "norm_mlp_w": sums_2[2:3], "norm_final_w": sums_f[0:1],
             "conv_w": sums_conv[0:4].reshape(1, 4 * XBC), "d_skip": dsk_sum[0:1],
             "a_log": jnp.concatenate([ssd_small[0:1], pad96], axis=1),
             "dt_bias": jnp.concatenate([ssd_small[1:2], pad96], axis=1), "loss": sums_f[3:4, 0:128]}
    vec = jnp.concatenate([small[n] for n, _ in _SMALL_SEGS], axis=1)
    vec = jnp.pad(vec, ((0, 0), (0, SMALL_LEN - vec.shape[1]))).reshape(SMALL_LEN // 128, 128)
    (every, total, dsk), got = _gather_small(vec, carry=_join(red.chip(["w_in_b"]), red.share(["w_in_a"])))
    red.take_chip(["w_in_b"], got[:1])
    red.take_share(["w_in_a"], got[1:])
    total = total.reshape(1, SMALL_LEN)
    seg = lambda n, size: total[:, SMALL_OFF[n]:SMALL_OFF[n] + size]
    g.update({"b_ada": seg("dmod", 6 * D), "norm_mix_w": seg("norm_mix_w", D), "conv_b": seg("conv_b", XBC),
              "dt_bias": seg("dt_bias", HEADS), "a_log": seg("a_log", HEADS), "d_skip": dsk[:, 0:2].reshape(1, HEADS),
              "ssd_norm_w": seg("ssd_norm_w", DI), "pool_scale": seg("pool_scale", D),
              "norm_mlp_w": seg("norm_mlp_w", D), "norm_final_w": seg("norm_final_w", D)})
    loss = total[0, SMALL_OFF["loss"]]
    conv_cols = conv_w.shape[-1]
    g["conv_w"] = lax.dynamic_slice(seg("conv_w", 4 * XBC).reshape(4, XBC), (0, chip * conv_cols), (4, conv_cols))
    dmod8 = every.reshape(8, SMALL_LEN)[:, SMALL_OFF["dmod"]:SMALL_OFF["dmod"] + 6 * D]
    g["w_ada"] = _w_ada_grad(c8, lax.dynamic_slice(dmod8, (0, chip * ada_cols), (8, ada_cols)))

    got = adamw("w_ada", carry=red.share(["w_in_b"]))
    red.take_share(["w_in_b"], got)
    for n in six:
        g[n] = red.final[n]
    g["w_in"] = jnp.concatenate([red.final["w_in_a"], red.final["w_in_b"]], axis=0)
    for n in ["conv_w", "w_in"] + six:
        adamw(n)
    sizes = [w[n].size for n in _SMALL_REPLICATED]
    n_small = -(-sum(sizes) // 1024) * 1024
    pack = lambda d: jnp.pad(jnp.concatenate([d[n].reshape(1, -1) for n in _SMALL_REPLICATED], axis=1),
                             ((0, 0), (0, n_small - sum(sizes)))).reshape(n_small // 128, 128)
    d_, m_, v_ = _adamw(pack(w), pack(g), pack(m), pack(v), "adamw_small")
    off = 0
    for n, s in zip(_SMALL_REPLICATED, sizes):
        for dst, src in ((delta, d_), (new_m, m_), (new_v, v_)):
            dst[n] = src.reshape(1, n_small)[:, off:off + s]
        off += s

    out = [loss, grad_x.reshape(x.shape)]
    for d in (g, delta, new_m, new_v):
        out += [d[n].reshape(w[n].shape) for n in _WEIGHTS]
    return tuple(out)
```

```python
import functools
import operator

import jax
import jax.numpy as jnp
import numpy as np
from jax import lax
from jax.experimental import pallas as pl
from jax.experimental.pallas import tpu as pltpu

F32, BF16 = jnp.float32, jnp.bfloat16
HIGH = lax.Precision.HIGHEST
MESH = pl.DeviceIdType.MESH

D = 1024
DI = 2048
HEADS, HEAD_DIM = 32, 64
GROUPS, STATE = 4, 128
Q = 128
XBC = DI + 2 * GROUPS * STATE
POOL_WINDOWS = (2, 4, 8, 16)
GW = 256
DFF = 4096
EPS = 1e-5
IN_COLS = 8224
OFF_Z, OFF_XBC, OFF_POOL, OFF_GATE, OFF_DT, NP = 0, 2048, 5120, 6144, 8192, 8448
N_CHIPS = 4
ADAM_LR, ADAM_B1, ADAM_B2, ADAM_EPS, ADAM_WD, ADAM_STEP = 0.001, 0.9, 0.999, 1e-08, 0.01, 10
VMEM_LIMIT = 56 * 2 ** 20
NEG = -1e30


def _sigmoid(v):
    return 0.5 * jnp.tanh(0.5 * v) + 0.5


def _softplus(v):
    return jnp.maximum(v, 0.0) + jnp.log1p(jnp.exp(-jnp.abs(v)))


def _dot(a, b, dims, **kw):
    return lax.dot_general(a, b, (dims, ((), ())), preferred_element_type=F32, **kw)


def _nn(a, b, **kw):
    return _dot(a, b, ((1,), (0,)), **kw)


def _nt(a, b, **kw):
    return _dot(a, b, ((1,), (1,)), **kw)


def _tn(a, b, **kw):
    return _dot(a, b, ((0,), (0,)), **kw)


_DT_IN_CHIP2 = 5120 - 2 * (IN_COLS // 4)


class _Sems:
    def __init__(self, send, recv, local, base=0):
        self._send, self._recv, self._local, self._base = send, recv, local, base

    def shift(self, n):
        return _Sems(self._send, self._recv, self._local, self._base + n)

    def send(self, i):
        return self._send.at[self._base + i]

    def recv(self, i):
        return self._recv.at[self._base + i]

    def local(self, i):
        return self._local.at[self._base + i]


class _Carry:
    def __init__(self, ins, out_shapes, n_sems, start, finish, aliased=()):
        self.ins, self.out_shapes, self.n_sems, self.start, self.finish = list(ins), list(out_shapes), n_sems, start, finish
        self.aliased = list(aliased)


def _join(*carries):
    def run(which):
        def fn(ins, outs, sems):
            i = o = s = 0
            for cy in carries:
                getattr(cy, which)(ins[i:i + len(cy.ins)], outs[o:o + len(cy.out_shapes)], sems.shift(s))
                i, o, s = i + len(cy.ins), o + len(cy.out_shapes), s + cy.n_sems
        return fn

    aliased, i, o = [], 0, 0
    for cy in carries:
        aliased += [(i + a, o + b) for a, b in cy.aliased]
        i, o = i + len(cy.ins), o + len(cy.out_shapes)
    return _Carry([a for cy in carries for a in cy.ins], [a for cy in carries for a in cy.out_shapes],
                  sum(cy.n_sems for cy in carries), run("start"), run("finish"), aliased)


def _call(body, args, *, name, grid=(), in_specs, out_specs, out_shape, scratch_shapes=(), sem=None, aliases=None,
          carry=None):
    in_specs, out_specs, out_shape, scratch_shapes = list(in_specs), list(out_specs), list(out_shape), list(scratch_shapes)
    n_in, n_out, n_scr = len(in_specs), len(out_specs), len(scratch_shapes)
    kw = {"vmem_limit_bytes": VMEM_LIMIT}
    if carry is None:
        kernel_fn = functools.partial(body)
        if sem is not None:
            kw["dimension_semantics"] = sem
    else:
        n_ci, n_co = len(carry.ins), len(carry.out_shapes)
        hbm = pl.BlockSpec(memory_space=pl.ANY)
        in_specs += [hbm] * n_ci
        out_specs += [hbm] * n_co
        out_shape += carry.out_shapes
        n_s = max(carry.n_sems, 1)
        scratch_shapes += [pltpu.SemaphoreType.DMA((n_s,))] * 3
        args = list(args) + carry.ins
        aliases = dict(aliases or {})
        aliases.update({n_in + i: n_out + o for i, o in carry.aliased})
        if grid:
            kw["dimension_semantics"] = ("arbitrary",) * len(grid)

        def kernel_fn(*refs):
            a = n_in
            ins, c_ins = refs[:a], refs[a:a + n_ci]
            a += n_ci
            outs, c_outs = refs[a:a + n_out], refs[a + n_out:a + n_out + n_co]
            a += n_out + n_co
            scr, sems = refs[a:a + n_scr], _Sems(*refs[a + n_scr:a + n_scr + 3])
            if grid:
                ids = [pl.program_id(d) for d in range(len(grid))]
                first = functools.reduce(operator.and_, [i == 0 for i in ids])
                last = functools.reduce(operator.and_, [i == g - 1 for i, g in zip(ids, grid)])

                @pl.when(first)
                def _():
                    carry.start(c_ins, c_outs, sems)

                body(*ins, *outs, *scr)

                @pl.when(last)
                def _():
                    carry.finish(c_ins, c_outs, sems)
            else:
                carry.start(c_ins, c_outs, sems)
                body(*ins, *outs, *scr)
                carry.finish(c_ins, c_outs, sems)

    outs = pl.pallas_call(
        kernel_fn, name=name, grid=grid, in_specs=in_specs, out_specs=out_specs, out_shape=out_shape,
        scratch_shapes=scratch_shapes, input_output_aliases=aliases or {},
        compiler_params=pltpu.CompilerParams(**kw),
    )(*args)
    outs = list(outs)
    return outs if carry is None else (outs[:n_out], outs[n_out:])


def _run_carry(carry, name):
    _, outs = _call(lambda: None, [], name=name, in_specs=[], out_specs=[], out_shape=[], carry=carry)
    return outs


_TILES = {
    "mm_proj": (1024, 2816, 1024), "mm_branch_ssd": (1024, 1024, 2048), "mm_branch_pool": (1024, 1024, 1024),
    "mm_out": (1024, 1024, 1024), "mm_up": (2048, 1024, 1024), "mm_down": (512, 1024, 4096),
    "mm_dact": (1024, 1024, 1024), "mm_g_down": (1024, 1024, 4096), "mm_dh2": (1024, 1024, 4096),
    "mm_g_up": (1024, 1024, 4096), "mm_dmerged": (1024, 1024, 1024), "mm_g_out": (1024, 1024, 2048),
    "mm_dyp": (1024, 1024, 1024), "mm_g_bpool": (1024, 1024, 2048), "mm_g_bssd": (1024, 1024, 4096),
    "mm_dyn": (1024, 1024, 1024), "mm_g_in_a": (768, 1408, 4096), "mm_g_in_b": (256, 2816, 2048),
    "mm_dh1": (1024, 1024, 4224),
}


def _matmul(a, b, *, mode, out_dtypes, name, epi=None, tile_extras=(), row_extras=(), carry=None, a_cols=None,
            chip_blocks=False, cols_outer=False):
    M, K = (a.shape[1], a.shape[0]) if mode == "tn" else a.shape
    N = b.shape[0] if mode == "nt" else b.shape[1]
    a_start, M = a_cols if a_cols is not None else (0, M)
    tm, tn, tk = _TILES[name]
    tm, tn, tk = min(tm, M), min(tn, N), min(tk, K)
    assert M % tm == 0 and N % tn == 0 and K % tk == 0 and a_start % tm == 0, (name, M, N, K, tm, tn, tk)
    a_off = a_start // tm
    if mode == "nn":
        a_spec = pl.BlockSpec((tm, tk), lambda i, j, k: (i, k))
        b_spec = pl.BlockSpec((tk, tn), lambda i, j, k: (k, j))
        dims = ((1,), (0,))
    elif mode == "nt":
        a_spec = pl.BlockSpec((tm, tk), lambda i, j, k: (i, k))
        b_spec = pl.BlockSpec((tn, tk), lambda i, j, k: (j, k))
        dims = ((1,), (1,))
    else:
        a_spec = pl.BlockSpec((tk, tm), lambda i, j, k: (k, i + a_off))
        b_spec = pl.BlockSpec((tk, tn), lambda i, j, k: (k, j))
        dims = ((0,), (0,))
    nk = K // tk
    n_te, n_re, n_out = len(tile_extras), len(row_extras), len(out_dtypes)
    if epi is None:
        epi = lambda acc: (acc,)

    def body(a_ref, b_ref, *rest):
        extras = rest[:n_te + n_re]
        outs = rest[n_te + n_re:n_te + n_re + n_out]
        p = _dot(a_ref[...], b_ref[...], dims)

        def finish(acc):
            vals = epi(acc, *[e[...] for e in extras])
            for o, v in zip(outs, vals):
                o[...] = v.astype(o.dtype)

        if nk == 1:
            finish(p)
        else:
            acc_ref = rest[-1]
            k = pl.program_id(2)

            @pl.when(k == 0)
            def _():
                acc_ref[...] = p

            @pl.when(k > 0)
            def _():
                acc_ref[...] += p

            @pl.when(k == nk - 1)
            def _():
                finish(acc_ref[...])

    tile_spec = pl.BlockSpec((tm, tn), lambda i, j, k: (i, j))
    row_spec = pl.BlockSpec((1, tn), lambda i, j, k: (0, j))
    out_spec, out_dims = tile_spec, (M, N)
    if chip_blocks:
        assert n_te == 0 and tn * N_CHIPS == N
        out_spec, out_dims = pl.BlockSpec((None, tm, tn), lambda i, j, k: (j, i, 0)), (N_CHIPS, M, tn)
    in_specs, grid = [a_spec, b_spec] + [tile_spec] * n_te + [row_spec] * n_re, (M // tm, N // tn, nk)
    if cols_outer:
        swap = lambda s: pl.BlockSpec(s.block_shape, lambda g0, g1, k, f=s.index_map: f(g1, g0, k))
        in_specs, out_spec, grid = [swap(s) for s in in_specs], swap(out_spec), (N // tn, M // tm, nk)
    return _call(
        body, [a, b, *tile_extras, *row_extras], name=name, grid=grid,
        in_specs=in_specs, out_specs=[out_spec] * n_out,
        out_shape=[jax.ShapeDtypeStruct(out_dims, dt) for dt in out_dtypes],
        scratch_shapes=[pltpu.VMEM((tm, tn), F32)] if nk > 1 else [],
        sem=("parallel", "parallel", "arbitrary"), carry=carry)


def _row_tile(T):
    return min(512, T)


def _norm_mod(x, nw, scale, shift, name, carry=None):
    T = x.shape[0]
    tr = _row_tile(T)

    def body(x_ref, nw_ref, sc_ref, sh_ref, o_ref):
        xv = x_ref[...]
        r = lax.rsqrt(jnp.mean(xv * xv, axis=-1, keepdims=True) + EPS)
        o_ref[...] = ((xv * r) * nw_ref[...] * (1.0 + sc_ref[...]) + sh_ref[...]).astype(BF16)

    tile = pl.BlockSpec((tr, D), lambda i: (i, 0))
    row = pl.BlockSpec((1, D), lambda i: (0, 0))
    res = _call(body, [x, nw, scale, shift], name=name, grid=(T // tr,), in_specs=[tile, row, row, row],
                out_specs=[tile], out_shape=[jax.ShapeDtypeStruct((T, D), BF16)], sem=("parallel",), carry=carry)
    return res[0] if carry is None else (res[0][0], res[1])


def _norm_mod_bwd(x, dh, dres, nw, scale, name, branch=None, gate=None, carry=None):
    T = x.shape[0]
    tr = _row_tile(T)
    with_branch = branch is not None

    def body(x_ref, dh_ref, dr_ref, nw_ref, sc_ref, *rest):
        if with_branch:
            br_ref, g_ref, dx_ref, sums_ref, db_ref = rest
        else:
            dx_ref, sums_ref = rest
        i = pl.program_id(0)

        @pl.when(i == 0)
        def _():
            sums_ref[...] = jnp.zeros_like(sums_ref)

        xv, dhv = x_ref[...], dh_ref[...]
        r = lax.rsqrt(jnp.mean(xv * xv, axis=-1, keepdims=True) + EPS)
        xn = xv * r
        g1 = dhv * (1.0 + sc_ref[...])
        dxn = g1 * nw_ref[...]
        dx = dr_ref[...] + r * (dxn - xn * jnp.mean(dxn * xn, axis=-1, keepdims=True))
        dx_ref[...] = dx
        sums_ref[0:1, :] += jnp.sum(dhv, axis=0, keepdims=True)
        sums_ref[1:2, :] += jnp.sum(dhv * (xn * nw_ref[...]), axis=0, keepdims=True)
        sums_ref[2:3, :] += jnp.sum(g1 * xn, axis=0, keepdims=True)
        if with_branch:
            db_ref[...] = (dx * g_ref[...]).astype(BF16)
            sums_ref[3:4, :] += jnp.sum(dx * br_ref[...], axis=0, keepdims=True)

    tile = pl.BlockSpec((tr, D), lambda i: (i, 0))
    row = pl.BlockSpec((1, D), lambda i: (0, 0))
    sums = pl.BlockSpec((8, D), lambda i: (0, 0))
    ins = [x, dh, dres, nw, scale] + ([branch, gate] if with_branch else [])
    in_specs = [tile, tile, tile, row, row] + ([tile, row] if with_branch else [])
    out_specs = [tile, sums] + ([tile] if with_branch else [])
    out_shape = [jax.ShapeDtypeStruct((T, D), F32), jax.ShapeDtypeStruct((8, D), F32)]
    if with_branch:
        out_shape.append(jax.ShapeDtypeStruct((T, D), BF16))
    return _call(body, ins, name=name, grid=(T // tr,), in_specs=in_specs, out_specs=out_specs, out_shape=out_shape,
                 sem=("arbitrary",), carry=carry)


def _final_loss_bwd(x3, target, wf, down, gate_f):
    T = x3.shape[0]
    tr = _row_tile(T)
    n_steps = T // tr

    def body(x_ref, t_ref, w_ref, dn_ref, g_ref, dx_ref, dd_ref, sums_ref):
        i = pl.program_id(0)

        @pl.when(i == 0)
        def _():
            sums_ref[...] = jnp.zeros_like(sums_ref)

        xv = x_ref[...]
        r = lax.rsqrt(jnp.mean(xv * xv, axis=-1, keepdims=True) + EPS)
        xn = xv * r
        err = xn * w_ref[...] - t_ref[...]
        dy = err * (1.0 / D)
        dxn = dy * w_ref[...]
        dx = r * (dxn - xn * jnp.mean(dxn * xn, axis=-1, keepdims=True))
        dx_ref[...] = dx
        dd_ref[...] = (dx * g_ref[...]).astype(BF16)
        sums_ref[0:1, :] += jnp.sum(dy * xn, axis=0, keepdims=True)
        sums_ref[1:2, :] += jnp.sum(dx * dn_ref[...], axis=0, keepdims=True)
        sums_ref[2:3, :] += jnp.sum(err * err, axis=0, keepdims=True) * (0.5 / D)

        @pl.when(i == n_steps - 1)
        def _():
            sums_ref[3:4, :] = jnp.broadcast_to(jnp.sum(sums_ref[2:3, :], axis=1, keepdims=True), (1, D))

    tile = pl.BlockSpec((tr, D), lambda i: (i, 0))
    row = pl.BlockSpec((1, D), lambda i: (0, 0))
    sums = pl.BlockSpec((8, D), lambda i: (0, 0))
    return _call(body, [x3, target, wf, down, gate_f], name="final_loss_bwd", grid=(n_steps,),
                 in_specs=[tile, tile, row, tile, row], out_specs=[tile, tile, sums],
                 out_shape=[jax.ShapeDtypeStruct((T, D), F32), jax.ShapeDtypeStruct((T, D), BF16),
                            jax.ShapeDtypeStruct((8, D), F32)], sem=("arbitrary",))


CONV_TC = 1024


def _conv_taps(xp, w, b):
    acc = b + w[3:4, :] * xp
    for k in range(3):
        acc = acc + w[k:k + 1, :] * pltpu.roll(xp, 3 - k, 0)
    return acc


def _conv_bwd(dxa, proj, conv_w, conv_b, dproj):
    T = proj.shape[0]
    tr = _row_tile(T)
    nb, offb, last = tr // 8, OFF_XBC // CONV_TC, T // tr - 1
    prev8 = lambda i: jnp.maximum(i * nb - 1, 0)
    next8 = lambda i: jnp.minimum((i + 1) * nb, T // 8 - 1)

    def body(d_ref, dn_ref, x_ref, xp_ref, xn_ref, w_ref, b_ref, dp_in, o_ref, sums_ref):
        del dp_in
        i = pl.program_id(1)

        @pl.when(i == 0)
        def _():
            sums_ref[...] = jnp.zeros_like(sums_ref)

        x = jnp.concatenate([jnp.where(i > 0, xp_ref[...], 0.0), x_ref[...], jnp.where(i < last, xn_ref[...], 0.0)], axis=0)
        d = jnp.concatenate([d_ref[...], jnp.where(i < last, dn_ref[...], 0.0)], axis=0)
        w = w_ref[...]
        taps = [pltpu.roll(x, 3 - k, 0)[8:] for k in range(3)] + [x[8:]]
        acc = b_ref[...] + w[3:4, :] * taps[3]
        for k in range(3):
            acc = acc + w[k:k + 1, :] * taps[k]
        s = _sigmoid(acc)
        dxc = d * (s * (1.0 + acc * (1.0 - s)))
        n = tr + 8
        dx = w[3:4, :] * dxc
        for k in range(3):
            dx = dx + w[k:k + 1, :] * pltpu.roll(dxc, n - (3 - k), 0)
        o_ref[...] = dx[:tr].astype(BF16)
        own = dxc[:tr]
        for k in range(4):
            sums_ref[k:k + 1, :] += jnp.sum(own * taps[k][:tr], axis=0, keepdims=True)
        sums_ref[4:5, :] += jnp.sum(own, axis=0, keepdims=True)

    return _call(
        body, [dxa, dxa, proj, proj, proj, conv_w, conv_b, dproj], name="conv_bwd", grid=(XBC // CONV_TC, T // tr),
        in_specs=[pl.BlockSpec((tr, CONV_TC), lambda j, i: (i, j)),
                  pl.BlockSpec((8, CONV_TC), lambda j, i: (next8(i), j)),
                  pl.BlockSpec((tr, CONV_TC), lambda j, i: (i, j + offb)),
                  pl.BlockSpec((8, CONV_TC), lambda j, i: (prev8(i), j + offb)),
                  pl.BlockSpec((8, CONV_TC), lambda j, i: (next8(i), j + offb)),
                  pl.BlockSpec((4, CONV_TC), lambda j, i: (0, j)),
                  pl.BlockSpec((1, CONV_TC), lambda j, i: (0, j)),
                  pl.BlockSpec(memory_space=pl.ANY)],
        out_specs=[pl.BlockSpec((tr, CONV_TC), lambda j, i: (i, j + offb)), pl.BlockSpec((8, CONV_TC), lambda j, i: (0, j))],
        out_shape=[jax.ShapeDtypeStruct(dproj.shape, BF16), jax.ShapeDtypeStruct((8, XBC), F32)],
        aliases={7: 0}, sem=("parallel", "arbitrary"))


def _spread(v, sel, pieces):
    out = None
    for _ in range(pieces):
        p = v.astype(BF16)
        term = _nn(p, sel)
        out = term if out is None else out + term
        v = v - p.astype(F32)
    return out


def _ssd_selectors():
    g = np.arange(GROUPS)[:, None, None]
    piece = np.arange(128)[None, :, None]
    h = np.where(piece < 3 * HEADS, piece % HEADS, -1)
    blocks = (h == 8 * g + np.arange(1024)[None, None, :] // 128)
    pairs = (h == 8 * g + np.arange(512)[None, None, :] // HEAD_DIM)
    lane = np.arange(128)[None, None, :]
    block_sum = (lane == 8 * g + np.arange(1024)[None, :, None] // 128)
    pair_sum = (lane == 8 * g + np.arange(512)[None, :, None] // HEAD_DIM)
    return [jnp.asarray(m, BF16) for m in (blocks, pairs, block_sum, pair_sum)]


def _pack3(v):
    p0 = v.astype(BF16)
    r1 = v - p0.astype(F32)
    p1 = r1.astype(BF16)
    r2 = r1 - p1.astype(F32)
    return p0 + pltpu.roll(r1, HEADS, 1).astype(BF16) + pltpu.roll(r2, 2 * HEADS, 1).astype(BF16)


def _ssd_group(g, cs_p, csT, dt_p, s_mat, causal_w, lo, blocks_ref, pairs_ref):
    csb = _nn(cs_p, blocks_ref[g])
    row = jnp.concatenate([csT[8 * g + hh:8 * g + hh + 1, :] for hh in range(8)], axis=1)
    l_w = jnp.exp(jnp.where(causal_w, csb - row, NEG))
    m_w = jnp.concatenate([s_mat] * 8, axis=1) * l_w
    cs_g = jnp.concatenate([jnp.where(lo, csb[:, 256 * jj:256 * jj + 128], csb[:, 256 * jj + 128:256 * jj + 256])
                            for jj in range(4)], axis=1)
    cs_last = cs_g[Q - 1:Q, :]
    return m_w, l_w, _nn(dt_p, pairs_ref[g]), jnp.exp(cs_g), jnp.exp(cs_last - cs_g), jnp.exp(cs_last)


def _ssd_common(dtp_ref, dtb_r, alog_r, dtb_c, alog_c):
    rows = lax.broadcasted_iota(jnp.int32, (Q, Q), 0)
    cols = lax.broadcasted_iota(jnp.int32, (Q, Q), 1)
    tri = (cols <= rows).astype(F32)
    heads = lax.broadcasted_iota(jnp.int32, (1, 128), 1) < HEADS
    raw_w = dtp_ref[...] + dtb_r[...]
    dt_w = jnp.where(heads, _softplus(raw_w), 0.0)
    a_w = -jnp.exp(alog_r[...])
    cs_w = _nn(tri, dt_w * a_w, precision=HIGH)
    aT = _softplus(dtp_ref[...].T[0:HEADS, :] + dtb_c[...]) * (-jnp.exp(alog_c[...]))
    csT = _nt(aT, tri, precision=HIGH)
    return raw_w[:, 0:HEADS], dt_w[:, 0:HEADS], a_w[:, 0:HEADS], csT, _pack3(cs_w), _pack3(dt_w)


def _ssd_fwd(proj, conv_w, conv_b, dtb_r, alog_r, dtb_c, alog_c, dsk_exp, norm_w):
    T = proj.shape[0]
    nc = T // Q
    dtb_r, alog_r = [jnp.pad(a, ((0, 0), (0, 128 - HEADS))) for a in (dtb_r, alog_r)]

    def body(x0_ref, x1_ref, x2_ref, h0_ref, h1_ref, h2_ref, cw_ref, cb_ref, dtp_ref, z_ref, dtb_r_ref, alog_r_ref,
             dtb_c_ref, alog_c_ref, dsk_ref, nw_ref, blocks_ref, pairs_ref, y_ref, hin_ref, yn_ref, xbc_ref, h_scr):
        first = pl.program_id(0) == 0

        @pl.when(first)
        def _():
            h_scr[...] = jnp.zeros_like(h_scr)

        x_refs, halo_refs = (x0_ref, x1_ref, x2_ref), (h0_ref, h1_ref, h2_ref)

        def conv(lo, hi):
            j, a, b = lo // 1024, lo % 1024, (hi - 1) % 1024 + 1
            halo = jnp.where(first, 0.0, halo_refs[j][:, a:b])
            acc = _conv_taps(jnp.concatenate([halo, x_refs[j][:, a:b]], axis=0), cw_ref[:, lo:hi], cb_ref[:, lo:hi])[8:]
            out = acc * _sigmoid(acc)
            xbc_ref[:, lo:hi] = out
            return out

        _, _, _, csT, cs_p, dt_p = _ssd_common(dtp_ref, dtb_r_ref, alog_r_ref, dtb_c_ref, alog_c_ref)
        lo = lax.broadcasted_iota(jnp.int32, (1, 128), 1) < HEAD_DIM
        hi = jnp.logical_not(lo)
        causal_w = (lax.broadcasted_iota(jnp.int32, (Q, 1024), 1) & (Q - 1)) <= lax.broadcasted_iota(jnp.int32, (Q, 1024), 0)
        for g in range(GROUPS):
            gs = slice(512 * g, 512 * (g + 1))
            hs = slice(128 * g, 128 * (g + 1))
            xs_g = conv(512 * g, 512 * (g + 1))
            b_g = conv(DI + STATE * g, DI + STATE * (g + 1)).astype(BF16)
            c_g = conv(DI + 512 + STATE * g, DI + 512 + STATE * (g + 1)).astype(BF16)
            m_w, _, dt_g, ecs_g, dec_g, cd_g = _ssd_group(g, cs_p, csT, dt_p, _nt(c_g, b_g), causal_w, lo, blocks_ref, pairs_ref)
            m_b = m_w.astype(BF16)
            xdt = xs_g * dt_g
            xdt_b = xdt.astype(BF16)
            ys = []
            for jj in range(4):
                xp = xdt_b[:, 128 * jj:128 * (jj + 1)]
                x_ab = jnp.concatenate([jnp.where(lo, xp, jnp.zeros_like(xp)), jnp.where(hi, xp, jnp.zeros_like(xp))], axis=0)
                ys.append(_nn(m_b[:, 256 * jj:256 * (jj + 1)], x_ab))
            h_g = h_scr[hs, :]
            hin_ref[0, hs, :] = h_g
            y_g = jnp.concatenate(ys, axis=1) + _nn(c_g, h_g.astype(BF16)) * ecs_g + dsk_ref[:, gs] * xs_g
            y_ref[:, gs] = y_g
            z = z_ref[:, gs]
            yg = y_g * (z * _sigmoid(z))
            r = lax.rsqrt(jnp.mean(yg * yg, axis=-1, keepdims=True) + EPS)
            yn_ref[:, gs] = (yg * r * nw_ref[:, gs]).astype(BF16)
            h_scr[hs, :] = h_g * cd_g + _tn(b_g, (xdt * dec_g).astype(BF16))

    small_r = pl.BlockSpec((1, 128), lambda c: (0, 0))
    small_c = pl.BlockSpec((HEADS, 1), lambda c: (0, 0))
    blocks, pairs, _, _ = _ssd_selectors()
    whole = lambda a: pl.BlockSpec(a.shape, lambda c: (0,) * a.ndim)
    wide, row = pl.BlockSpec((Q, DI), lambda c: (c, 0)), pl.BlockSpec((1, DI), lambda c: (0, 0))
    xb = OFF_XBC // 1024
    x_specs = [pl.BlockSpec((Q, 1024), lambda c, j=j: (c, xb + j)) for j in range(3)]
    halo_specs = [pl.BlockSpec((8, 1024), lambda c, j=j: (jnp.maximum(c * (Q // 8) - 1, 0), xb + j)) for j in range(3)]
    return _call(
        body, [proj] * 6 + [conv_w, conv_b, proj, proj, dtb_r, alog_r, dtb_c, alog_c, dsk_exp, norm_w, blocks, pairs],
        name="ssd_fwd", grid=(nc,),
        in_specs=x_specs + halo_specs + [whole(conv_w), whole(conv_b),
                  pl.BlockSpec((Q, 128), lambda c: (c, OFF_DT // 128)), wide,
                  small_r, small_r, small_c, small_c, row, row, whole(blocks), whole(pairs)],
        out_specs=[wide, pl.BlockSpec((1, 512, 512), lambda c: (c, 0, 0)), wide, pl.BlockSpec((Q, XBC), lambda c: (c, 0))],
        out_shape=[jax.ShapeDtypeStruct((T, DI), F32), jax.ShapeDtypeStruct((nc, 512, 512), F32),
                   jax.ShapeDtypeStruct((T, DI), BF16), jax.ShapeDtypeStruct((T, XBC), F32)],
        scratch_shapes=[pltpu.VMEM((512, 512), F32)], sem=("arbitrary",))


def _ssd_bwd(dyn, y, xbc_a, proj, hin, dtb_r, alog_r, dtb_c, alog_c, dsk_exp, norm_w, dproj, carry=None):
    T = xbc_a.shape[0]
    nc = T // Q
    dtb_r, alog_r = [jnp.pad(a, ((0, 0), (0, 128 - HEADS))) for a in (dtb_r, alog_r)]

    def body(dyn_ref, y_ref, z_ref, xbc_ref, dtp_ref, hin_ref, dtb_r_ref, alog_r_ref, dtb_c_ref, alog_c_ref, dsk_ref,
             nw_ref, dp_in, blocks_ref, pairs_ref, block_sum_ref, pair_sum_ref,
             dxa_ref, dz_ref, dsk_sum_ref, small_ref, dp_ref, gn_ref, dh_scr):
        del dp_in

        @pl.when(pl.program_id(0) == 0)
        def _():
            dh_scr[...] = jnp.zeros_like(dh_scr)
            dsk_sum_ref[...] = jnp.zeros_like(dsk_sum_ref)
            small_ref[...] = jnp.zeros_like(small_ref)
            gn_ref[...] = jnp.zeros_like(gn_ref)

        raw, dt, a_r, csT, cs_p, dt_p = _ssd_common(dtp_ref, dtb_r_ref, alog_r_ref, dtb_c_ref, alog_c_ref)
        lo = lax.broadcasted_iota(jnp.int32, (1, 128), 1) < HEAD_DIM
        hi = jnp.logical_not(lo)
        sub32 = lax.broadcasted_iota(jnp.int32, (HEADS, 1), 0)
        causal_w = (lax.broadcasted_iota(jnp.int32, (Q, 1024), 1) & (Q - 1)) <= lax.broadcasted_iota(jnp.int32, (Q, 1024), 0)
        dcs_c = jnp.zeros((Q, 128), F32)
        dcs_r = jnp.zeros((HEADS, Q), F32)
        dcs_l = jnp.zeros((8, 128), F32)
        ddt_x = jnp.zeros((Q, 128), F32)
        for g in range(GROUPS):
            gs = slice(512 * g, 512 * (g + 1))
            hs = slice(128 * g, 128 * (g + 1))
            xs_g = xbc_ref[:, gs]
            z, yv, d = z_ref[:, gs], y_ref[:, gs], dyn_ref[:, gs]
            s = _sigmoid(z)
            silu = z * s
            yg = yv * silu
            r = lax.rsqrt(jnp.mean(yg * yg, axis=-1, keepdims=True) + EPS)
            yn = yg * r
            gn_ref[0:1, gs] += jnp.sum(d * yn, axis=0, keepdims=True)
            dn = d * nw_ref[:, gs]
            dyg = r * (dn - yn * jnp.mean(dn * yn, axis=-1, keepdims=True))
            dy_g = dyg * silu
            dz_ref[:, gs] = (dyg * yv * (s * (1.0 + z * (1.0 - s)))).astype(BF16)
            b_g = xbc_ref[:, DI + STATE * g:DI + STATE * (g + 1)].astype(BF16)
            c_g = xbc_ref[:, DI + 512 + STATE * g:DI + 512 + STATE * (g + 1)].astype(BF16)
            m_w, l_w, dt_g, ecs_g, dec_g, cd_g = _ssd_group(g, cs_p, csT, dt_p, _nt(c_g, b_g), causal_w, lo, blocks_ref, pairs_ref)
            m_b = m_w.astype(BF16)
            xdt = xs_g * dt_g
            xdt_b, dy_b = xdt.astype(BF16), dy_g.astype(BF16)
            dms, dxs = [], []
            for jj in range(4):
                xp, dyp = xdt_b[:, 128 * jj:128 * (jj + 1)], dy_b[:, 128 * jj:128 * (jj + 1)]
                dy_ab = jnp.concatenate([jnp.where(lo, dyp, jnp.zeros_like(dyp)), jnp.where(hi, dyp, jnp.zeros_like(dyp))], axis=0)
                dm_ab = _nt(dy_ab, xp)
                dms += [dm_ab[:Q], dm_ab[Q:]]
                dx_ab = _tn(m_b[:, 256 * jj:256 * (jj + 1)], dyp)
                dxs.append(jnp.where(lo, dx_ab[:Q], dx_ab[Q:]))
            dm_w = jnp.concatenate(dms, axis=1)
            w_w = dm_w * m_w
            dcs_c = dcs_c + _spread(w_w, block_sum_ref[g], 2)
            w_cols = jnp.sum(w_w, axis=0, keepdims=True)
            for hh in range(8):
                dcs_r = dcs_r + jnp.where(sub32 == 8 * g + hh, w_cols[:, 128 * hh:128 * (hh + 1)], 0.0)
            dl_w = dm_w * l_w
            ds_mat = dl_w[:, 0:128]
            for hh in range(1, 8):
                ds_mat = ds_mat + dl_w[:, 128 * hh:128 * (hh + 1)]
            hin_g = hin_ref[0, hs, :]
            hin_b = hin_g.astype(BF16)
            dh_g = dh_scr[hs, :]
            dh_b = dh_g.astype(BF16)
            g_mat = _nn(b_g, dh_b)
            xdec = xdt * dec_g
            xg = xdec * g_mat
            dxdt = jnp.concatenate(dxs, axis=1) + dec_g * g_mat
            sums = _spread(jnp.concatenate([dy_g * (_nn(c_g, hin_b) * ecs_g) - xg, dxdt * xs_g], axis=0), pair_sum_ref[g], 2)
            dcs_c = dcs_c + sums[:Q]
            ddt_x = ddt_x + sums[Q:]
            last = jnp.sum(xg, axis=0, keepdims=True) + jnp.sum(dh_g * hin_g, axis=0, keepdims=True) * cd_g
            dcs_l = dcs_l + _spread(jnp.broadcast_to(last, (8, 512)), pair_sum_ref[g], 2)
            dz = (dy_g * ecs_g).astype(BF16)
            ds_b = ds_mat.astype(BF16)
            dxa_ref[:, gs] = dxdt * dt_g + dy_g * dsk_ref[:, gs]
            dxa_ref[:, DI + STATE * g:DI + STATE * (g + 1)] = _nt(xdec.astype(BF16), dh_b) + _tn(ds_b, c_g)
            dxa_ref[:, DI + 512 + STATE * g:DI + 512 + STATE * (g + 1)] = _nt(dz, hin_b) + _nn(ds_b, b_g)
            dh_scr[hs, :] = _tn(c_g, dz) + dh_g * cd_g
            dsk_sum_ref[0:1, gs] += jnp.sum(dy_g * xs_g, axis=0, keepdims=True)

        rows = lax.broadcasted_iota(jnp.int32, (Q, Q), 0)
        cols = lax.broadcasted_iota(jnp.int32, (Q, Q), 1)
        tri_t = (cols >= rows).astype(F32)
        last_row = lax.broadcasted_iota(jnp.int32, (Q, 1), 0) == Q - 1
        dcs = (dcs_c + jnp.where(last_row, dcs_l[0:1, :], 0.0))[:, 0:HEADS]
        da = _nn(tri_t, dcs, precision=HIGH) - _nt(tri_t, dcs_r, precision=HIGH)
        ddt_raw = (ddt_x[:, 0:HEADS] + da * a_r) * _sigmoid(raw)
        small_ref[0:1, :] += jnp.sum(da * dt, axis=0, keepdims=True) * a_r
        small_ref[1:2, :] += jnp.sum(ddt_raw, axis=0, keepdims=True)
        dp_ref[...] = jnp.zeros_like(dp_ref)
        dp_ref[:, 0:HEADS] = ddt_raw.astype(BF16)

    rev = lambda c: nc - 1 - c
    small_r = pl.BlockSpec((1, 128), lambda c: (0, 0))
    small_c = pl.BlockSpec((HEADS, 1), lambda c: (0, 0))
    selectors = _ssd_selectors()
    whole = lambda a: pl.BlockSpec(a.shape, lambda c: (0,) * a.ndim)
    wide, row = pl.BlockSpec((Q, DI), lambda c: (rev(c), 0)), pl.BlockSpec((1, DI), lambda c: (0, 0))
    sums = pl.BlockSpec((8, DI), lambda c: (0, 0))
    return _call(
        body, [dyn, y, proj, xbc_a, proj, hin, dtb_r, alog_r, dtb_c, alog_c, dsk_exp, norm_w, dproj, *selectors],
        name="ssd_bwd", grid=(nc,),
        in_specs=[wide, wide, wide,
                  pl.BlockSpec((Q, XBC), lambda c: (rev(c), 0)),
                  pl.BlockSpec((Q, 128), lambda c: (rev(c), OFF_DT // 128)),
                  pl.BlockSpec((1, 512, 512), lambda c: (rev(c), 0, 0)),
                  small_r, small_r, small_c, small_c, row, row,
                  pl.BlockSpec(memory_space=pl.ANY)] + [whole(a) for a in selectors],
        out_specs=[pl.BlockSpec((Q, XBC), lambda c: (rev(c), 0)),
                   pl.BlockSpec((Q, DI), lambda c: (rev(c), OFF_Z // DI)),
                   sums, pl.BlockSpec((8, HEADS), lambda c: (0, 0)), pl.BlockSpec((Q, 256), lambda c: (rev(c), 0)), sums],
        out_shape=[jax.ShapeDtypeStruct((T, XBC), F32), jax.ShapeDtypeStruct(dproj.shape, BF16),
                   jax.ShapeDtypeStruct((8, DI), F32), jax.ShapeDtypeStruct((8, HEADS), F32),
                   jax.ShapeDtypeStruct((T, 256), BF16), jax.ShapeDtypeStruct((8, DI), F32)],
        aliases={12: 1}, scratch_shapes=[pltpu.VMEM((512, 512), F32)], sem=("arbitrary",), carry=carry)


def _pool_fwd(proj, pool_w_b, pool_scale):
    T = proj.shape[0]
    tr = _row_tile(T)
    nb = tr // 16

    def body(u_ref, h_ref, pw_ref, ps_ref, pooled_ref, pw_out_ref, yps_ref):
        i = pl.program_id(0)
        t = i * tr + lax.broadcasted_iota(jnp.int32, (tr, 1), 0)
        for g, win in enumerate(POOL_WINDOWS):
            gs = slice(GW * g, GW * (g + 1))
            u = u_ref[:, gs]
            s = jnp.concatenate([jnp.where(i > 0, h_ref[:, gs], 0.0), u], axis=0)
            sh = 1
            while sh < win:
                s = s + pltpu.roll(s, sh, 0)
                sh *= 2
            pooled = (s[16:] * (1.0 / jnp.minimum(t + 1, win).astype(F32)) - u).astype(BF16)
            pooled_ref[:, gs] = pooled
            pwv = _nn(pooled, pw_ref[g])
            pw_out_ref[:, gs] = pwv
            yps_ref[:, gs] = (pwv * ps_ref[:, gs]).astype(BF16)

    tile = pl.BlockSpec((tr, D), lambda i: (i, 0))
    return _call(
        body, [proj, proj, pool_w_b, pool_scale], name="pool_fwd", grid=(T // tr,),
        in_specs=[pl.BlockSpec((tr, D), lambda i: (i, OFF_POOL // D)),
                  pl.BlockSpec((16, D), lambda i: (jnp.maximum(i * nb - 1, 0), OFF_POOL // D)),
                  pl.BlockSpec((4, GW, GW), lambda i: (0, 0, 0)),
                  pl.BlockSpec((1, D), lambda i: (0, 0))],
        out_specs=[tile, tile, tile],
        out_shape=[jax.ShapeDtypeStruct((T, D), BF16), jax.ShapeDtypeStruct((T, D), F32),
                   jax.ShapeDtypeStruct((T, D), BF16)], sem=("parallel",))


def _pool_bwd(dyp, pw_out, pooled, pool_w_b, pool_scale, dproj):
    T = dyp.shape[0]
    tr = _row_tile(T)
    nb, last = tr // 16, T // tr - 1

    def body(d_ref, h_ref, pwo_ref, pooled_ref, pw_ref, ps_ref, dp_in, du_ref, gpw_ref, sums_ref):
        del dp_in
        i = pl.program_id(0)

        @pl.when(i == 0)
        def _():
            gpw_ref[...] = jnp.zeros_like(gpw_ref)
            sums_ref[...] = jnp.zeros_like(sums_ref)

        n = tr + 16
        t = i * tr + lax.broadcasted_iota(jnp.int32, (n, 1), 0)
        sums_ref[0:1, :] += jnp.sum(d_ref[...] * pwo_ref[...], axis=0, keepdims=True)
        for g, win in enumerate(POOL_WINDOWS):
            gs = slice(GW * g, GW * (g + 1))
            d_ext = jnp.concatenate([d_ref[:, gs], jnp.where(i < last, h_ref[:, gs], 0.0)], axis=0)
            dpw = (d_ext * ps_ref[:, gs]).astype(BF16)
            dpooled = _nt(dpw, pw_ref[g])
            s = jnp.where(t < T, dpooled * (1.0 / jnp.minimum(t + 1, win).astype(F32)), 0.0)
            sh = 1
            while sh < win:
                s = s + pltpu.roll(s, n - sh, 0)
                sh *= 2
            du_ref[:, gs] = (s[:tr] - dpooled[:tr]).astype(BF16)
            gpw_ref[g] += _tn(pooled_ref[:, gs], dpw[:tr])

    tile = pl.BlockSpec((tr, D), lambda i: (i, 0))
    return _call(
        body, [dyp, dyp, pw_out, pooled, pool_w_b, pool_scale, dproj], name="pool_bwd", grid=(T // tr,),
        in_specs=[tile, pl.BlockSpec((16, D), lambda i: (jnp.minimum((i + 1) * nb, T // 16 - 1), 0)), tile, tile,
                  pl.BlockSpec((4, GW, GW), lambda i: (0, 0, 0)), pl.BlockSpec((1, D), lambda i: (0, 0)),
                  pl.BlockSpec(memory_space=pl.ANY)],
        out_specs=[pl.BlockSpec((tr, D), lambda i: (i, OFF_POOL // D)),
                   pl.BlockSpec((4, GW, GW), lambda i: (0, 0, 0)), pl.BlockSpec((8, D), lambda i: (0, 0))],
        out_shape=[jax.ShapeDtypeStruct(dproj.shape, BF16), jax.ShapeDtypeStruct((4, GW, GW), F32),
                   jax.ShapeDtypeStruct((8, D), F32)],
        aliases={6: 0}, sem=("arbitrary",))


def _merge(proj, y_ssd, y_pool):
    T = proj.shape[0]
    tr = _row_tile(T)

    def body(g_ref, a_ref, b_ref, o_ref):
        o_ref[...] = (_sigmoid(g_ref[:, 0:D]) * a_ref[...] + _sigmoid(g_ref[:, D:2 * D]) * b_ref[...]).astype(BF16)

    tile = pl.BlockSpec((tr, D), lambda i: (i, 0))
    return _call(body, [proj, y_ssd, y_pool], name="merge", grid=(T // tr,),
                 in_specs=[pl.BlockSpec((tr, 2 * D), lambda i: (i, OFF_GATE // (2 * D))), tile, tile], out_specs=[tile],
                 out_shape=[jax.ShapeDtypeStruct((T, D), BF16)], sem=("parallel",))[0]


def _merge_bwd(dmerged, proj, y_ssd, y_pool):
    T = proj.shape[0]
    tr = _row_tile(T)

    def body(d_ref, g_ref, a_ref, b_ref, da_ref, db_ref, dg_ref):
        d = d_ref[...]
        ga, gb = _sigmoid(g_ref[:, 0:D]), _sigmoid(g_ref[:, D:2 * D])
        da_ref[...] = (d * ga).astype(BF16)
        db_ref[...] = (d * gb).astype(BF16)
        dg_ref[:, 0:D] = (d * a_ref[...] * ga * (1.0 - ga)).astype(BF16)
        dg_ref[:, D:2 * D] = (d * b_ref[...] * gb * (1.0 - gb)).astype(BF16)

    tile = pl.BlockSpec((tr, D), lambda i: (i, 0))
    gates = pl.BlockSpec((tr, 2 * D), lambda i: (i, OFF_GATE // (2 * D)))
    return _call(body, [dmerged, proj, y_ssd, y_pool], name="merge_bwd", grid=(T // tr,),
                 in_specs=[tile, gates, tile, tile], out_specs=[tile, tile, gates],
                 out_shape=[jax.ShapeDtypeStruct((T, D), BF16), jax.ShapeDtypeStruct((T, D), BF16),
                            jax.ShapeDtypeStruct((T, NP), BF16)], sem=("parallel",))


def _adamw(w, g, m, v, name, carry=None):
    R, C = w.shape
    tr = R if R <= 128 else 128
    assert R % tr == 0

    def body(w_ref, g_ref, m_ref, v_ref, d_ref, mo_ref, vo_ref):
        gv = g_ref[...]
        mn = ADAM_B1 * m_ref[...] + (1.0 - ADAM_B1) * gv
        vn = ADAM_B2 * v_ref[...] + (1.0 - ADAM_B2) * (gv * gv)
        m_hat = mn * (1.0 / (1.0 - ADAM_B1 ** ADAM_STEP))
        v_hat = vn * (1.0 / (1.0 - ADAM_B2 ** ADAM_STEP))
        d_ref[...] = -ADAM_LR * (m_hat / (jnp.sqrt(v_hat) + ADAM_EPS) + ADAM_WD * w_ref[...])
        mo_ref[...] = mn
        vo_ref[...] = vn

    tile = pl.BlockSpec((tr, C), lambda i: (i, 0))
    sds = jax.ShapeDtypeStruct((R, C), F32)
    return _call(body, [w, g, m, v], name=name, grid=(R // tr,), in_specs=[tile] * 4, out_specs=[tile] * 3,
                 out_shape=[sds] * 3, sem=("parallel",), carry=carry)


def _me():
    return lax.axis_index("x"), lax.axis_index("y"), lax.axis_index("c")


def _xor_peer(x, y, c, p):
    return (x ^ ((p >> 2) & 1), y ^ ((p >> 1) & 1), c ^ (p & 1))


def _ada_fwd(c_row, w_ada, b_ada_mine, carry=None):
    n_cols = w_ada.shape[1]

    def body(c_ref, w_ref, b_ref, mod_ref, c8_ref, csend, mpart, modbuf, send_sems, recv_sems):
        x, y, c = _me()
        me = 4 * x + 2 * y + c
        chip = 2 * x + y
        csend[...] = jnp.broadcast_to(c_ref[...], csend.shape)
        c8_ref[me] = csend[...]

        def c_copy(p):
            return pltpu.make_async_remote_copy(
                src_ref=csend, dst_ref=c8_ref.at[me], send_sem=send_sems.at[p - 1], recv_sem=recv_sems.at[p - 1],
                device_id=_xor_peer(x, y, c, p), device_id_type=MESH)

        for p in range(1, 8):
            c_copy(p).start()
        for p in range(1, 8):
            c_copy(p).wait_recv()
        cs = jnp.concatenate([c8_ref[d][0:1, :] for d in range(8)], axis=0)
        mpart[...] = _nn(cs * _sigmoid(cs), w_ref[...], precision=HIGH) + b_ref[...]
        modbuf[chip] = mpart[...]

        def m_copy(m):
            return pltpu.make_async_remote_copy(
                src_ref=mpart, dst_ref=modbuf.at[chip], send_sem=send_sems.at[6 + m], recv_sem=recv_sems.at[6 + m],
                device_id=_xor_peer(x, y, c, 2 * m), device_id_type=MESH)

        for m in range(1, 4):
            m_copy(m).start()
        for m in range(1, 4):
            m_copy(m).wait_recv()
        mine = lax.broadcasted_iota(jnp.int32, (8, 1), 0) == me
        for k in range(N_CHIPS):
            mod_ref[:, n_cols * k:n_cols * (k + 1)] = jnp.sum(jnp.where(mine, modbuf[k], 0.0), axis=0, keepdims=True)
        for p in range(1, 8):
            c_copy(p).wait_send()
        for m in range(1, 4):
            m_copy(m).wait_send()

    vmem = pl.BlockSpec(memory_space=pltpu.VMEM)
    return _call(
        body, [c_row, w_ada, b_ada_mine], name="ada_fwd", in_specs=[vmem, vmem, vmem], out_specs=[vmem, vmem],
        out_shape=[jax.ShapeDtypeStruct((1, N_CHIPS * n_cols), F32), jax.ShapeDtypeStruct((8, 8, D), F32)],
        scratch_shapes=[pltpu.VMEM((8, D), F32), pltpu.VMEM((8, n_cols), F32), pltpu.VMEM((N_CHIPS, 8, n_cols), F32),
                        pltpu.SemaphoreType.DMA((10,)), pltpu.SemaphoreType.DMA((10,))], carry=carry)


def _gather_small(vec, carry=None):
    rows = vec.shape[0]

    def body(v_ref, all_ref, tot_ref, dsk_ref, send_sems, recv_sems):
        x, y, c = _me()
        me = 4 * x + 2 * y + c
        all_ref[me] = v_ref[...]

        def copy(p):
            return pltpu.make_async_remote_copy(
                src_ref=v_ref, dst_ref=all_ref.at[me], send_sem=send_sems.at[p - 1], recv_sem=recv_sems.at[p - 1],
                device_id=_xor_peer(x, y, c, p), device_id_type=MESH)

        for p in range(1, 8):
            copy(p).start()
        for p in range(1, 8):
            copy(p).wait_recv()
        tot = all_ref[0]
        for d in range(1, 8):
            tot = tot + all_ref[d]
        tot_ref[...] = tot
        seg = tot[SMALL_OFF["d_skip"] // 128:SMALL_OFF["d_skip"] // 128 + 16, :]
        lane = lax.broadcasted_iota(jnp.int32, (1, 128), 1)
        sa = jnp.sum(jnp.where(lane < HEAD_DIM, seg, 0.0), axis=1, keepdims=True)
        sb = jnp.sum(jnp.where(lane < HEAD_DIM, 0.0, seg), axis=1, keepdims=True)
        dsk_ref[...] = jnp.where(lane == 0, sa, jnp.where(lane == 1, sb, 0.0))
        for p in range(1, 8):
            copy(p).wait_send()

    vmem = pl.BlockSpec(memory_space=pltpu.VMEM)
    return _call(
        body, [vec], name="gather_small", in_specs=[vmem], out_specs=[vmem, vmem, vmem],
        out_shape=[jax.ShapeDtypeStruct((8, rows, 128), F32), jax.ShapeDtypeStruct((rows, 128), F32),
                   jax.ShapeDtypeStruct((16, 128), F32)],
        scratch_shapes=[pltpu.SemaphoreType.DMA((7,)), pltpu.SemaphoreType.DMA((7,))], carry=carry)


def _gather_carry(shards):
    n = len(shards)

    def copies(ins, outs, sems):
        x, y, c = _me()
        chip = 2 * x + y

        def half(w, which):
            h = shards[w].shape[0] // 2
            return pl.ds(which * h, h)

        def first(w, m):
            return pltpu.make_async_remote_copy(
                src_ref=ins[w].at[half(w, c)], dst_ref=outs[w].at[chip, half(w, c)],
                send_sem=sems.send(6 * w + m - 1), recv_sem=sems.recv(6 * w + m - 1),
                device_id=_xor_peer(x, y, c, 2 * m), device_id_type=MESH)

        def landed(w, m):
            return pltpu.make_async_remote_copy(
                src_ref=ins[w].at[half(w, c)], dst_ref=outs[w].at[chip ^ m, half(w, c)],
                send_sem=sems.send(6 * w + m - 1), recv_sem=sems.recv(6 * w + m - 1),
                device_id=_xor_peer(x, y, c, 2 * m), device_id_type=MESH)

        def passed(w, m, which):
            part = outs[w].at[chip ^ m, half(w, which)]
            return pltpu.make_async_remote_copy(
                src_ref=part, dst_ref=part, send_sem=sems.send(6 * w + 2 + m), recv_sem=sems.recv(6 * w + 2 + m),
                device_id=(x, y, 1 - c), device_id_type=MESH)

        return c, first, landed, passed

    pairs = [(w, m) for w in range(n) for m in range(1, 4)]

    def start(ins, outs, sems):
        _, first, _, _ = copies(ins, outs, sems)
        for w, m in pairs:
            first(w, m).start()

    def finish(ins, outs, sems):
        c, first, landed, passed = copies(ins, outs, sems)
        for w, m in pairs:
            landed(w, m).wait_recv()
            passed(w, m, c).start()
        for w, m in pairs:
            passed(w, m, 1 - c).wait_recv()
        for w, m in pairs:
            first(w, m).wait_send()
            passed(w, m, c).wait_send()

    return _Carry(shards, [jax.ShapeDtypeStruct((N_CHIPS,) + s.shape, s.dtype) for s in shards], 6 * n, start, finish)


def _pair_exchange_carry(grads):
    n = len(grads)

    def copy(ins, outs, sems, w):
        x, y, c = _me()
        h = grads[w].shape[1] // 2
        return pltpu.make_async_remote_copy(
            src_ref=ins[w].at[:, pl.ds((1 - c) * h, h)], dst_ref=outs[w],
            send_sem=sems.send(w), recv_sem=sems.recv(w), device_id=(x, y, 1 - c), device_id_type=MESH)

    def start(ins, outs, sems):
        for w in range(n):
            copy(ins, outs, sems, w).start()

    def finish(ins, outs, sems):
        for w in range(n):
            copy(ins, outs, sems, w).wait()

    return _Carry(grads, [jax.ShapeDtypeStruct((N_CHIPS, g.shape[1] // 2, g.shape[2]), g.dtype) for g in grads], n,
                  start, finish)


def _chip_exchange_carry(partials):
    n = len(partials)

    def copier(ins, outs, sems):
        x, y, c = _me()
        chip = 2 * x + y

        def copy(w, m, landed):
            return pltpu.make_async_remote_copy(
                src_ref=ins[w].at[chip ^ m], dst_ref=outs[w].at[(chip ^ m) if landed else chip],
                send_sem=sems.send(3 * w + m - 1), recv_sem=sems.recv(3 * w + m - 1),
                device_id=_xor_peer(x, y, c, 2 * m), device_id_type=MESH)

        return copy

    pairs = [(w, m) for w in range(n) for m in range(1, 4)]

    def start(ins, outs, sems):
        copy = copier(ins, outs, sems)
        for w, m in pairs:
            copy(w, m, False).start()

    def finish(ins, outs, sems):
        copy = copier(ins, outs, sems)
        for w, m in pairs:
            copy(w, m, True).wait_recv()
        for w, m in pairs:
            copy(w, m, False).wait_send()

    return _Carry(partials, [jax.ShapeDtypeStruct(p.shape, p.dtype) for p in partials], 3 * n, start, finish)


def _pair_share_carry(shards):
    n = len(shards)

    def copier(ins, outs, sems):
        x, y, c = _me()

        def copy(w, which):
            h = shards[w].shape[0] // 2
            rows = pl.ds(which * h, h)
            return pltpu.make_async_remote_copy(
                src_ref=ins[w].at[rows], dst_ref=outs[w].at[rows],
                send_sem=sems.send(w), recv_sem=sems.recv(w), device_id=(x, y, 1 - c), device_id_type=MESH)

        return c, copy

    def start(ins, outs, sems):
        c, copy = copier(ins, outs, sems)
        for w in range(n):
            copy(w, c).start()

    def finish(ins, outs, sems):
        c, copy = copier(ins, outs, sems)
        for w in range(n):
            copy(w, 1 - c).wait_recv()
        for w in range(n):
            copy(w, c).wait_send()

    return _Carry(shards, [jax.ShapeDtypeStruct(s.shape, s.dtype) for s in shards], n, start, finish,
                  aliased=[(w, w) for w in range(n)])


def _pair_sum(g, part, idx, name):
    _, h, C = part.shape
    tr = min(512, h)
    nb = h // tr

    def body(idx_ref, g_ref, p_ref, o16_ref, own_ref):
        v = g_ref[...].astype(F32) + p_ref[...].astype(F32)
        o16_ref[...] = v.astype(BF16)

        @pl.when(pl.program_id(1) == idx_ref[1])
        def _():
            own_ref[...] = v

    return pl.pallas_call(
        body, name=name,
        grid_spec=pltpu.PrefetchScalarGridSpec(
            num_scalar_prefetch=1, grid=(nb, N_CHIPS),
            in_specs=[pl.BlockSpec((None, tr, C), lambda i, s, idx_ref: (s, idx_ref[0] * nb + i, 0)),
                      pl.BlockSpec((None, tr, C), lambda i, s, idx_ref: (s, i, 0))],
            out_specs=[pl.BlockSpec((None, tr, C), lambda i, s, idx_ref: (s, i, 0)),
                       pl.BlockSpec((tr, C), lambda i, s, idx_ref: (i, 0))]),
        out_shape=[jax.ShapeDtypeStruct(part.shape, BF16), jax.ShapeDtypeStruct((h, C), F32)],
        compiler_params=pltpu.CompilerParams(dimension_semantics=("arbitrary", "arbitrary"), vmem_limit_bytes=VMEM_LIMIT),
    )(idx, g, part)


def _chip_sum(own, slots, idx, name):
    h, C = own.shape
    tr = min(512, h)
    nb = h // tr

    def body(idx_ref, own_ref, s1_ref, s2_ref, s3_ref, o_ref):
        del idx_ref
        o_ref[...] = ((own_ref[...] + s1_ref[...].astype(F32)) + s2_ref[...].astype(F32)) + s3_ref[...].astype(F32)

    def slot(m):
        return pl.BlockSpec((None, tr, C), lambda i, idx_ref: (idx_ref[1] ^ m, i, 0))

    return pl.pallas_call(
        body, name=name,
        grid_spec=pltpu.PrefetchScalarGridSpec(
            num_scalar_prefetch=1, grid=(nb,),
            in_specs=[pl.BlockSpec((tr, C), lambda i, idx_ref: (i, 0)), slot(1), slot(2), slot(3)],
            out_specs=pl.BlockSpec((tr, C), lambda i, idx_ref: (idx_ref[0] * nb + i, 0))),
        out_shape=jax.ShapeDtypeStruct((2 * h, C), F32),
        compiler_params=pltpu.CompilerParams(dimension_semantics=("parallel",), vmem_limit_bytes=VMEM_LIMIT),
    )(idx, own, slots, slots, slots)


class _Reducer:
    def __init__(self, idx):
        self.idx, self.chips, self.p16, self.own, self.mine, self.final = idx, {}, {}, {}, {}, {}

    def add(self, name, whole, chip_blocks=False):
        self.chips[name] = whole if chip_blocks else _chips_from_whole(name, whole)

    def pair(self, names):
        return _pair_exchange_carry([self.chips[n] for n in names])

    def take_pair(self, names, outs):
        for n, part in zip(names, outs):
            self.p16[n], self.own[n] = _pair_sum(self.chips.pop(n), part, self.idx, "pair_sum_" + n)

    def chip(self, names):
        return _chip_exchange_carry([self.p16[n] for n in names])

    def take_chip(self, names, outs):
        for n, slots in zip(names, outs):
            del self.p16[n]
            self.mine[n] = _chip_sum(self.own.pop(n), slots, self.idx, "chip_sum_" + n)

    def share(self, names):
        return _pair_share_carry([self.mine[n] for n in names])

    def take_share(self, names, outs):
        for n, s in zip(names, outs):
            del self.mine[n]
            self.final[n] = s


def _w_ada_grad(c8, dmod_cols):
    n_cols = dmod_cols.shape[1]
    tn = 512

    def body(c_ref, d_ref, o_ref):
        cv = c_ref[...]
        o_ref[...] = _tn(cv * _sigmoid(cv), d_ref[...], precision=HIGH)

    return _call(body, [c8, dmod_cols], name="w_ada_grad", grid=(n_cols // tn,),
                 in_specs=[pl.BlockSpec((8, D), lambda j: (0, 0)), pl.BlockSpec((8, tn), lambda j: (0, j))],
                 out_specs=[pl.BlockSpec((D, tn), lambda j: (0, j))],
                 out_shape=[jax.ShapeDtypeStruct((D, n_cols), F32)], sem=("parallel",))[0]


_SMALL_SEGS = (("dmod", 6144), ("norm_mix_w", 1024), ("conv_b", 3072), ("ssd_norm_w", 2048), ("pool_scale", 1024),
               ("norm_mlp_w", 1024), ("norm_final_w", 1024), ("conv_w", 4 * XBC), ("d_skip", 2048), ("a_log", 128),
               ("dt_bias", 128), ("loss", 128))
SMALL_OFF = {}
_o = 0
for _n, _s in _SMALL_SEGS:
    SMALL_OFF[_n] = _o
    _o += _s
SMALL_LEN = -(-_o // 1024) * 1024

_FIRST = ("w_in", "conv_w")
_LATER = ("w_branch_ssd", "pool_w", "w_branch_pool", "w_out", "w_up", "w_down")
_SMALL_REPLICATED = ("b_ada", "norm_mix_w", "conv_b", "dt_bias", "a_log", "d_skip", "ssd_norm_w", "pool_scale",
                     "norm_mlp_w", "norm_final_w")
_WEIGHTS = ("w_ada", "b_ada", "norm_mix_w", "w_in", "conv_w", "conv_b", "dt_bias", "a_log", "d_skip", "ssd_norm_w",
            "w_branch_ssd", "pool_w", "pool_scale", "w_branch_pool", "w_out", "norm_mlp_w", "w_up", "w_down",
            "norm_final_w")


def _shard_2d(name, a):
    if name == "conv_w":
        return a.reshape(16, -1)
    return (a.reshape(GW, GW) if name == "pool_w" else a.reshape(a.shape[-2], a.shape[-1])).astype(BF16)


def _whole_from_chips(name, g, own, chip):
    g = lax.dynamic_update_slice(g, own[None], (chip, 0, 0))
    if name == "w_in":
        a, b = _DT_IN_CHIP2, _DT_IN_CHIP2 + HEADS
        pad = jnp.zeros((D, NP - IN_COLS), g.dtype)
        return jnp.concatenate([g[0], g[1], g[2][:, :a], g[2][:, b:], g[3], g[2][:, a:b], pad], axis=1)
    if name == "w_up":
        return jnp.concatenate([g[k] for k in range(N_CHIPS)], axis=1)
    if name == "pool_w":
        return jnp.transpose(g.reshape(N_CHIPS, 4, GW // N_CHIPS, GW), (1, 0, 2, 3)).reshape(4, GW, GW)
    if name == "conv_w":
        return jnp.transpose(g.reshape(N_CHIPS, 4, XBC // N_CHIPS), (1, 0, 2)).reshape(4, XBC)
    return g.reshape(N_CHIPS * g.shape[1], g.shape[2])


def _chips_from_whole(name, g):
    if name.startswith("w_in"):
        cw, a = IN_COLS // N_CHIPS, _DT_IN_CHIP2
        chip2 = jnp.concatenate([g[:, 2 * cw:2 * cw + a], g[:, OFF_DT:OFF_DT + HEADS], g[:, 5120:3 * cw - HEADS]], axis=1)
        return jnp.stack([g[:, :cw], g[:, cw:2 * cw], chip2, g[:, 3 * cw - HEADS:OFF_DT]])
    if name == "w_up":
        return jnp.transpose(g.reshape(D, N_CHIPS, DFF // N_CHIPS), (1, 0, 2))
    if name == "pool_w":
        return jnp.transpose(g.reshape(4, N_CHIPS, GW // N_CHIPS, GW), (1, 0, 2, 3)).reshape(N_CHIPS, GW, GW)
    return g.reshape(N_CHIPS, g.shape[0] // N_CHIPS, g.shape[1])


def kernel(x, c, w_ada, b_ada, norm_mix_w, w_in, conv_w, conv_b, dt_bias, a_log, d_skip, ssd_norm_w, w_branch_ssd, pool_w, pool_scale, w_branch_pool, w_out, norm_mlp_w, w_up, w_down, norm_final_w, loss_target, m_w_ada, m_b_ada, m_norm_mix_w, m_w_in, m_conv_w, m_conv_b, m_dt_bias, m_a_log, m_d_skip, m_ssd_norm_w, m_w_branch_ssd, m_pool_w, m_pool_scale, m_w_branch_pool, m_w_out, m_norm_mlp_w, m_w_up, m_w_down, m_norm_final_w, v_w_ada, v_b_ada, v_norm_mix_w, v_w_in, v_conv_w, v_conv_b, v_dt_bias, v_a_log, v_d_skip, v_ssd_norm_w, v_w_branch_ssd, v_pool_w, v_pool_scale, v_w_branch_pool, v_w_out, v_norm_mlp_w, v_w_up, v_w_down, v_norm_final_w):
    args = locals()
    w = {n: args[n] for n in _WEIGHTS}
    m = {n: args["m_" + n] for n in _WEIGHTS}
    v = {n: args["v_" + n] for n in _WEIGHTS}
    xi, yi, ci = _me()
    chip = 2 * xi + yi
    idx = jnp.stack([ci, chip]).astype(jnp.int32)
    ada_cols = w_ada.shape[-1]
    xs, target = x[0], loss_target[0]
    two_d = lambda n, a: a.reshape(GW, GW) if n == "pool_w" else a.reshape(-1, a.shape[-1])
    delta, new_m, new_v, g = {}, {}, {}, {}

    def adamw(n, carry=None):
        res = _adamw(two_d(n, w[n]), two_d(n, g[n]), two_d(n, m[n]), two_d(n, v[n]), "adamw_" + n, carry=carry)
        (delta[n], new_m[n], new_v[n]), extra = res if carry is not None else (res, None)
        return extra

    b_mine = lax.dynamic_slice(b_ada, (0, chip * ada_cols), (1, ada_cols))
    shards = {n: _shard_2d(n, w[n]) for n in _FIRST + _LATER}
    mod, c8 = _ada_fwd(c, w_ada[0], b_mine)
    c8 = c8[:, 0, :]
    shift_m, scale_m, gate_m, shift_f, scale_f, gate_f = [mod[:, D * i:D * (i + 1)] for i in range(6)]
    nf_w = norm_final_w.reshape(1, D)

    h1, first = _norm_mod(xs, norm_mix_w, scale_m, shift_m, "norm_mod_mix",
                          carry=_gather_carry([shards[n] for n in _FIRST]))
    p ={n: _whole_from_chips(n, a, shards[n], chip) for n, a in zip(_FIRST, first)}
    (proj,), later = _matmul(h1, p["w_in"], mode="nn", out_dtypes=[F32], name="mm_proj", cols_outer=True,
                             carry=_gather_carry([shards[n] for n in _LATER]))
    p.update({n: _whole_from_chips(n, a, shards[n], chip) for n, a in zip(_LATER, later)})
    dtb_c, alog_c = dt_bias.reshape(HEADS, 1), a_log.reshape(HEADS, 1)
    dsk_exp = jnp.repeat(d_skip, HEAD_DIM, axis=1)
    y, hin, yn, xbc_a = _ssd_fwd(proj, p["conv_w"], conv_b, dt_bias, a_log, dtb_c, alog_c, dsk_exp, ssd_norm_w)
    (y_ssd,) = _matmul(yn, p["w_branch_ssd"], mode="nn", out_dtypes=[F32], name="mm_branch_ssd")
    pooled, pw_out, yps = _pool_fwd(proj, p["pool_w"], pool_scale)
    (y_pool,) = _matmul(yps, p["w_branch_pool"], mode="nn", out_dtypes=[F32], name="mm_branch_pool")
    merged = _merge(proj, y_ssd, y_pool)
    resid = lambda acc, r, gt: (r + gt * acc, acc)
    x2, mix = _matmul(merged, p["w_out"], mode="nn", out_dtypes=[F32, BF16], name="mm_out",
                      epi=resid, tile_extras=(xs,), row_extras=(gate_m,))
    h2 = _norm_mod(x2, norm_mlp_w, scale_f, shift_f, "norm_mod_mlp")
    relu2 = lambda acc: (jnp.square(jnp.maximum(acc, 0.0)),)
    (act,) = _matmul(h2, p["w_up"], mode="nn", out_dtypes=[BF16], name="mm_up", epi=relu2)
    x3, down = _matmul(act, p["w_down"], mode="nn", out_dtypes=[F32, BF16], name="mm_down",
                       epi=resid, tile_extras=(x2,), row_extras=(gate_f,))

    red = _Reducer(idx)
    dx3, d_down, sums_f = _final_loss_bwd(x3, target, nf_w, down, gate_f)
    drelu2 = lambda acc, a: (acc * (2.0 * jnp.sqrt(a)).astype(F32),)
    (dup,) = _matmul(d_down, p["w_down"], mode="nt", out_dtypes=[BF16], name="mm_dact",
                     epi=drelu2, tile_extras=(act,))
    red.add("w_down", _matmul(act, d_down, mode="tn", out_dtypes=[BF16], name="mm_g_down")[0])
    (dh2,), got = _matmul(dup, p["w_up"], mode="nt", out_dtypes=[F32], name="mm_dh2",
                          carry=red.pair(["w_down"]))
    red.take_pair(["w_down"], got)
    red.add("w_up", _matmul(h2, dup, mode="tn", out_dtypes=[BF16], name="mm_g_up", chip_blocks=True)[0], chip_blocks=True)
    dx2, sums_2, dmix = _norm_mod_bwd(x2, dh2, dx3, norm_mlp_w, scale_f, "norm_mod_mlp_bwd", branch=mix, gate=gate_m)
    (dmerged,), got = _matmul(dmix, p["w_out"], mode="nt", out_dtypes=[F32], name="mm_dmerged",
                              carry=red.pair(["w_up"]))
    red.take_pair(["w_up"], got)
    red.add("w_out", _matmul(merged, dmix, mode="tn", out_dtypes=[BF16], name="mm_g_out")[0])
    dy_ssd, dy_pool, dproj = _merge_bwd(dmerged, proj, y_ssd, y_pool)
    (dyp,), got = _matmul(dy_pool, p["w_branch_pool"], mode="nt", out_dtypes=[F32], name="mm_dyp",
                          carry=red.pair(["w_out"]))
    red.take_pair(["w_out"], got)
    red.add("w_branch_pool", _matmul(yps, dy_pool, mode="tn", out_dtypes=[BF16], name="mm_g_bpool")[0])
    dproj, g_pool_w, sums_pool = _pool_bwd(dyp, pw_out, pooled, p["pool_w"], pool_scale, dproj)
    red.add("pool_w", g_pool_w.astype(BF16))
    red.add("w_branch_ssd", _matmul(yn, dy_ssd, mode="tn", out_dtypes=[BF16], name="mm_g_bssd")[0])
    mixers = ["w_branch_pool", "pool_w", "w_branch_ssd"]
    (dyn,), got = _matmul(dy_ssd, p["w_branch_ssd"], mode="nt", out_dtypes=[F32], name="mm_dyn",
                          carry=red.pair(mixers))
    red.take_pair(mixers, got)
    six = ["w_down", "w_up", "w_out"] + mixers
    (dxa, dproj, dsk_sum, ssd_small, ddt, sums_gn), got = _ssd_bwd(
        dyn, y, xbc_a, proj, hin, dt_bias, a_log, dtb_c, alog_c, dsk_exp, ssd_norm_w, dproj, carry=red.chip(six))
    red.take_chip(six, got)
    dproj = lax.dynamic_update_slice(dproj, ddt, (0, OFF_DT))
    dproj, sums_conv = _conv_bwd(dxa, proj, p["conv_w"], conv_b, dproj)
    rows_a = 3 * D // 4
    (g_in_a,), got = _matmul(h1, dproj, mode="tn", out_dtypes=[BF16], name="mm_g_in_a", a_cols=(0, rows_a),
                             carry=red.share(six))
    red.take_share(six, got)
    red.add("w_in_a", g_in_a)
    (g_in_b,), got = _matmul(h1, dproj, mode="tn", out_dtypes=[BF16], name="mm_g_in_b", a_cols=(rows_a, D - rows_a),
                             carry=red.pair(["w_in_a"]))
    red.take_pair(["w_in_a"], got)
    red.add("w_in_b", g_in_b)
    (dh1,), got = _matmul(dproj, p["w_in"], mode="nt", out_dtypes=[F32], name="mm_dh1",
                          carry=_join(red.chip(["w_in_a"]), red.pair(["w_in_b"])))
    red.take_chip(["w_in_a"], got[:1])
    red.take_pair(["w_in_b"], got[1:])
    grad_x, sums_1 = _norm_mod_bwd(xs, dh1, dx2, norm_mix_w, scale_m, "norm_mod_mix_bwd")

    dmod = jnp.concatenate([sums_1[0:1], sums_1[1:2], sums_2[3:4], sums_2[0:1], sums_2[1:2], sums_f[1:2]], axis=1)
    pad96 = jnp.zeros((1, 96), F32)
    small = {"dmod": dmod, "norm_mix_w": sums_1[2:3], "conv_b": sums_conv[4:5], "ssd_norm_w": sums_gn[0:1],
             "pool_scale": sums_pool[0:1], "norm_mlp_w": sums_2[2:3], "norm_final_w": sums_f[0:1],
             "conv_w": sums_conv[0:4].reshape(1, 4 * XBC), "d_skip": dsk_sum[0:1],
             "a_log": jnp.concatenate([ssd_small[0:1], pad96], axis=1),
             "dt_bias": jnp.concatenate([ssd_small[1:2], pad96], axis=1), "loss": sums_f[3:4, 0:128]}
    vec = jnp.concatenate([small[n] for n, _ in _SMALL_SEGS], axis=1)
    vec = jnp.pad(vec, ((0, 0), (0, SMALL_LEN - vec.shape[1]))).reshape(SMALL_LEN // 128, 128)
    (every, total, dsk), got = _gather_small(vec, carry=_join(red.chip(["w_in_b"]), red.share(["w_in_a"])))
    red.take_chip(["w_in_b"], got[:1])
    red.take_share(["w_in_a"], got[1:])
    total = total.reshape(1, SMALL_LEN)
    seg = lambda n, size: total[:, SMALL_OFF[n]:SMALL_OFF[n] + size]
    g.update({"b_ada": seg("dmod", 6 * D), "norm_mix_w": seg("norm_mix_w", D), "conv_b": seg("conv_b", XBC),
              "dt_bias": seg("dt_bias", HEADS), "a_log": seg("a_log", HEADS), "d_skip": dsk[:, 0:2].reshape(1, HEADS),
              "ssd_norm_w": seg("ssd_norm_w", DI), "pool_scale": seg("pool_scale", D),
              "norm_mlp_w": seg("norm_mlp_w", D), "norm_final_w": seg("norm_final_w", D)})
    loss = total[0, SMALL_OFF["loss"]]
    conv_cols = conv_w.shape[-1]
    g["conv_w"] = lax.dynamic_slice(seg("conv_w", 4 * XBC).reshape(4, XBC), (0, chip * conv_cols), (4, conv_cols))
    dmod8 = every.reshape(8, SMALL_LEN)[:, SMALL_OFF["dmod"]:SMALL_OFF["dmod"] + 6 * D]
    g["w_ada"] = _w_ada_grad(c8, lax.dynamic_slice(dmod8, (0, chip * ada_cols), (8, ada_cols)))

    got = adamw("w_ada", carry=red.share(["w_in_b"]))
    red.take_share(["w_in_b"], got)
    for n in six:
        g[n] = red.final[n]
    g["w_in"] = jnp.concatenate([red.final["w_in_a"], red.final["w_in_b"]], axis=0)
    for n in ["conv_w", "w_in"] + six:
        adamw(n)
    sizes = [w[n].size for n in _SMALL_REPLICATED]
    n_small = -(-sum(sizes) // 1024) * 1024
    pack = lambda d: jnp.pad(jnp.concatenate([d[n].reshape(1, -1) for n in _SMALL_REPLICATED], axis=1),
                             ((0, 0), (0, n_small - sum(sizes)))).reshape(n_small // 128, 128)
    d_, m_, v_ = _adamw(pack(w), pack(g), pack(m), pack(v), "adamw_small")
    off = 0
    for n, s in zip(_SMALL_REPLICATED, sizes):
        for dst, src in ((delta, d_), (new_m, m_), (new_v, v_)):
            dst[n] = src.reshape(1, n_small)[:, off:off + s]
        off += s

    out = [loss, grad_x.reshape(x.shape)]
    for d in (g, delta, new_m, new_v):
        out += [d[n].reshape(w[n].shape) for n in _WEIGHTS]
    return tuple(out)
```

```python
import functools
import operator

import jax
import jax.numpy as jnp
import numpy as np
from jax import lax
from jax.experimental import pallas as pl
from jax.experimental.pallas import tpu as pltpu

F32, BF16 = jnp.float32, jnp.bfloat16
HIGH = lax.Precision.HIGHEST
MESH = pl.DeviceIdType.MESH

D = 1024
DI = 2048
HEADS, HEAD_DIM = 32, 64
GROUPS, STATE = 4, 128
Q = 128
XBC = DI + 2 * GROUPS * STATE
POOL_WINDOWS = (2, 4, 8, 16)
GW = 256
DFF = 4096
EPS = 1e-5
IN_COLS = 8224
OFF_Z, OFF_XBC, OFF_POOL, OFF_GATE, OFF_DT, NP = 0, 2048, 5120, 6144, 8192, 8448
N_CHIPS = 4
ADAM_LR, ADAM_B1, ADAM_B2, ADAM_EPS, ADAM_WD, ADAM_STEP = 0.001, 0.9, 0.999, 1e-08, 0.01, 10
VMEM_LIMIT = 56 * 2 ** 20
NEG = -1e30


def _sigmoid(v):
    return 0.5 * jnp.tanh(0.5 * v) + 0.5


def _softplus(v):
    return jnp.maximum(v, 0.0) + jnp.log1p(jnp.exp(-jnp.abs(v)))


def _dot(a, b, dims, **kw):
    return lax.dot_general(a, b, (dims, ((), ())), preferred_element_type=F32, **kw)


def _nn(a, b, **kw):
    return _dot(a, b, ((1,), (0,)), **kw)


def _nt(a, b, **kw):
    return _dot(a, b, ((1,), (1,)), **kw)


def _tn(a, b, **kw):
    return _dot(a, b, ((0,), (0,)), **kw)


_DT_IN_CHIP2 = 5120 - 2 * (IN_COLS // 4)


class _Sems:
    def __init__(self, send, recv, local, base=0):
        self._send, self._recv, self._local, self._base = send, recv, local, base

    def shift(self, n):
        return _Sems(self._send, self._recv, self._local, self._base + n)

    def send(self, i):
        return self._send.at[self._base + i]

    def recv(self, i):
        return self._recv.at[self._base + i]

    def local(self, i):
        return self._local.at[self._base + i]


class _Carry:
    def __init__(self, ins, out_shapes, n_sems, start, finish, aliased=()):
        self.ins, self.out_shapes, self.n_sems, self.start, self.finish = list(ins), list(out_shapes), n_sems, start, finish
        self.aliased = list(aliased)


def _join(*carries):
    def run(which):
        def fn(ins, outs, sems):
            i = o = s = 0
            for cy in carries:
                getattr(cy, which)(ins[i:i + len(cy.ins)], outs[o:o + len(cy.out_shapes)], sems.shift(s))
                i, o, s = i + len(cy.ins), o + len(cy.out_shapes), s + cy.n_sems
        return fn

    aliased, i, o = [], 0, 0
    for cy in carries:
        aliased += [(i + a, o + b) for a, b in cy.aliased]
        i, o = i + len(cy.ins), o + len(cy.out_shapes)
    return _Carry([a for cy in carries for a in cy.ins], [a for cy in carries for a in cy.out_shapes],
                  sum(cy.n_sems for cy in carries), run("start"), run("finish"), aliased)


def _call(body, args, *, name, grid=(), in_specs, out_specs, out_shape, scratch_shapes=(), sem=None, aliases=None,
          carry=None):
    in_specs, out_specs, out_shape, scratch_shapes = list(in_specs), list(out_specs), list(out_shape), list(scratch_shapes)
    n_in, n_out, n_scr = len(in_specs), len(out_specs), len(scratch_shapes)
    kw = {"vmem_limit_bytes": VMEM_LIMIT}
    if carry is None:
        kernel_fn = functools.partial(body)
        if sem is not None:
            kw["dimension_semantics"] = sem
    else:
        n_ci, n_co = len(carry.ins), len(carry.out_shapes)
        hbm = pl.BlockSpec(memory_space=pl.ANY)
        in_specs += [hbm] * n_ci
        out_specs += [hbm] * n_co
        out_shape += carry.out_shapes
        n_s = max(carry.n_sems, 1)
        scratch_shapes += [pltpu.SemaphoreType.DMA((n_s,))] * 3
        args = list(args) + carry.ins
        aliases = dict(aliases or {})
        aliases.update({n_in + i: n_out + o for i, o in carry.aliased})
        if grid:
            kw["dimension_semantics"] = ("arbitrary",) * len(grid)

        def kernel_fn(*refs):
            a = n_in
            ins, c_ins = refs[:a], refs[a:a + n_ci]
            a += n_ci
            outs, c_outs = refs[a:a + n_out], refs[a + n_out:a + n_out + n_co]
            a += n_out + n_co
            scr, sems = refs[a:a + n_scr], _Sems(*refs[a + n_scr:a + n_scr + 3])
            if grid:
                ids = [pl.program_id(d) for d in range(len(grid))]
                first = functools.reduce(operator.and_, [i == 0 for i in ids])
                last = functools.reduce(operator.and_, [i == g - 1 for i, g in zip(ids, grid)])

                @pl.when(first)
                def _():
                    carry.start(c_ins, c_outs, sems)

                body(*ins, *outs, *scr)

                @pl.when(last)
                def _():
                    carry.finish(c_ins, c_outs, sems)
            else:
                carry.start(c_ins, c_outs, sems)
                body(*ins, *outs, *scr)
                carry.finish(c_ins, c_outs, sems)

    outs = pl.pallas_call(
        kernel_fn, name=name, grid=grid, in_specs=in_specs, out_specs=out_specs, out_shape=out_shape,
        scratch_shapes=scratch_shapes, input_output_aliases=aliases or {},
        compiler_params=pltpu.CompilerParams(**kw),
    )(*args)
    outs = list(outs)
    return outs if carry is None else (outs[:n_out], outs[n_out:])


def _run_carry(carry, name):
    _, outs = _call(lambda: None, [], name=name, in_specs=[], out_specs=[], out_shape=[], carry=carry)
    return outs


_TILES = {
    "mm_proj": (1024, 2816, 1024), "mm_branch_ssd": (1024, 1024, 2048), "mm_branch_pool": (1024, 1024, 1024),
    "mm_out": (1024, 1024, 1024), "mm_up": (2048, 1024, 1024), "mm_down": (512, 1024, 4096),
    "mm_dact": (1024, 1024, 1024), "mm_g_down": (1024, 1024, 4096), "mm_dh2": (1024, 1024, 4096),
    "mm_g_up": (1024, 1024, 4096), "mm_dmerged": (1024, 1024, 1024), "mm_g_out": (1024, 1024, 2048),
    "mm_dyp": (1024, 1024, 1024), "mm_g_bpool": (1024, 1024, 2048), "mm_g_bssd": (1024, 1024, 4096),
    "mm_dyn": (1024, 1024, 1024), "mm_g_in_a": (768, 1408, 4096), "mm_g_in_b": (256, 2816, 2048),
    "mm_dh1": (1024, 1024, 4224),
}


def _matmul(a, b, *, mode, out_dtypes, name, epi=None, tile_extras=(), row_extras=(), carry=None, a_cols=None,
            chip_blocks=False, cols_outer=False):
    M, K = (a.shape[1], a.shape[0]) if mode == "tn" else a.shape
    N = b.shape[0] if mode == "nt" else b.shape[1]
    a_start, M = a_cols if a_cols is not None else (0, M)
    tm, tn, tk = _TILES[name]
    tm, tn, tk = min(tm, M), min(tn, N), min(tk, K)
    assert M % tm == 0 and N % tn == 0 and K % tk == 0 and a_start % tm == 0, (name, M, N, K, tm, tn, tk)
    a_off = a_start // tm
    if mode == "nn":
        a_spec = pl.BlockSpec((tm, tk), lambda i, j, k: (i, k))
        b_spec = pl.BlockSpec((tk, tn), lambda i, j, k: (k, j))
        dims = ((1,), (0,))
    elif mode == "nt":
        a_spec = pl.BlockSpec((tm, tk), lambda i, j, k: (i, k))
        b_spec = pl.BlockSpec((tn, tk), lambda i, j, k: (j, k))
        dims = ((1,), (1,))
    else:
        a_spec = pl.BlockSpec((tk, tm), lambda i, j, k: (k, i + a_off))
        b_spec = pl.BlockSpec((tk, tn), lambda i, j, k: (k, j))
        dims = ((0,), (0,))
    nk = K // tk
    n_te, n_re, n_out = len(tile_extras), len(row_extras), len(out_dtypes)
    if epi is None:
        epi = lambda acc: (acc,)

    def body(a_ref, b_ref, *rest):
        extras = rest[:n_te + n_re]
        outs = rest[n_te + n_re:n_te + n_re + n_out]
        p = _dot(a_ref[...], b_ref[...], dims)

        def finish(acc):
            vals = epi(acc, *[e[...] for e in extras])
            for o, v in zip(outs, vals):
                o[...] = v.astype(o.dtype)

        if nk == 1:
            finish(p)
        else:
            acc_ref = rest[-1]
            k = pl.program_id(2)

            @pl.when(k == 0)
            def _():
                acc_ref[...] = p

            @pl.when(k > 0)
            def _():
                acc_ref[...] += p

            @pl.when(k == nk - 1)
            def _():
                finish(acc_ref[...])

    tile_spec = pl.BlockSpec((tm, tn), lambda i, j, k: (i, j))
    row_spec = pl.BlockSpec((1, tn), lambda i, j, k: (0, j))
    out_spec, out_dims = tile_spec, (M, N)
    if chip_blocks:
        assert n_te == 0 and tn * N_CHIPS == N
        out_spec, out_dims = pl.BlockSpec((None, tm, tn), lambda i, j, k: (j, i, 0)), (N_CHIPS, M, tn)
    in_specs, grid = [a_spec, b_spec] + [tile_spec] * n_te + [row_spec] * n_re, (M // tm, N // tn, nk)
    if cols_outer:
        swap = lambda s: pl.BlockSpec(s.block_shape, lambda g0, g1, k, f=s.index_map: f(g1, g0, k))
        in_specs, out_spec, grid = [swap(s) for s in in_specs], swap(out_spec), (N // tn, M // tm, nk)
    return _call(
        body, [a, b, *tile_extras, *row_extras], name=name, grid=grid,
        in_specs=in_specs, out_specs=[out_spec] * n_out,
        out_shape=[jax.ShapeDtypeStruct(out_dims, dt) for dt in out_dtypes],
        scratch_shapes=[pltpu.VMEM((tm, tn), F32)] if nk > 1 else [],
        sem=("parallel", "parallel", "arbitrary"), carry=carry)


def _row_tile(T):
    return min(512, T)


def _norm_mod(x, nw, scale, shift, name, carry=None):
    T = x.shape[0]
    tr = _row_tile(T)

    def body(x_ref, nw_ref, sc_ref, sh_ref, o_ref):
        xv = x_ref[...]
        r = lax.rsqrt(jnp.mean(xv * xv, axis=-1, keepdims=True) + EPS)
        o_ref[...] = ((xv * r) * nw_ref[...] * (1.0 + sc_ref[...]) + sh_ref[...]).astype(BF16)

    tile = pl.BlockSpec((tr, D), lambda i: (i, 0))
    row = pl.BlockSpec((1, D), lambda i: (0, 0))
    res = _call(body, [x, nw, scale, shift], name=name, grid=(T // tr,), in_specs=[tile, row, row, row],
                out_specs=[tile], out_shape=[jax.ShapeDtypeStruct((T, D), BF16)], sem=("parallel",), carry=carry)
    return res[0] if carry is None else (res[0][0], res[1])


def _norm_mod_bwd(x, dh, dres, nw, scale, name, branch=None, gate=None, carry=None):
    T = x.shape[0]
    tr = _row_tile(T)
    with_branch = branch is not None

    def body(x_ref, dh_ref, dr_ref, nw_ref, sc_ref, *rest):
        if with_branch:
            br_ref, g_ref, dx_ref, sums_ref, db_ref = rest
        else:
            dx_ref, sums_ref = rest
        i = pl.program_id(0)

        @pl.when(i == 0)
        def _():
            sums_ref[...] = jnp.zeros_like(sums_ref)

        xv, dhv = x_ref[...], dh_ref[...]
        r = lax.rsqrt(jnp.mean(xv * xv, axis=-1, keepdims=True) + EPS)
        xn = xv * r
        g1 = dhv * (1.0 + sc_ref[...])
        dxn = g1 * nw_ref[...]
        dx = dr_ref[...] + r * (dxn - xn * jnp.mean(dxn * xn, axis=-1, keepdims=True))
        dx_ref[...] = dx
        sums_ref[0:1, :] += jnp.sum(dhv, axis=0, keepdims=True)
        sums_ref[1:2, :] += jnp.sum(dhv * (xn * nw_ref[...]), axis=0, keepdims=True)
        sums_ref[2:3, :] += jnp.sum(g1 * xn, axis=0, keepdims=True)
        if with_branch:
            db_ref[...] = (dx * g_ref[...]).astype(BF16)
            sums_ref[3:4, :] += jnp.sum(dx * br_ref[...], axis=0, keepdims=True)

    tile = pl.BlockSpec((tr, D), lambda i: (i, 0))
    row = pl.BlockSpec((1, D), lambda i: (0, 0))
    sums = pl.BlockSpec((8, D), lambda i: (0, 0))
    ins = [x, dh, dres, nw, scale] + ([branch, gate] if with_branch else [])
    in_specs = [tile, tile, tile, row, row] + ([tile, row] if with_branch else [])
    out_specs = [tile, sums] + ([tile] if with_branch else [])
    out_shape = [jax.ShapeDtypeStruct((T, D), F32), jax.ShapeDtypeStruct((8, D), F32)]
    if with_branch:
        out_shape.append(jax.ShapeDtypeStruct((T, D), BF16))
    return _call(body, ins, name=name, grid=(T // tr,), in_specs=in_specs, out_specs=out_specs, out_shape=out_shape,
                 sem=("arbitrary",), carry=carry)


def _final_loss_bwd(x3, target, wf, down, gate_f):
    T = x3.shape[0]
    tr = _row_tile(T)
    n_steps = T // tr

    def body(x_ref, t_ref, w_ref, dn_ref, g_ref, dx_ref, dd_ref, sums_ref):
        i = pl.program_id(0)

        @pl.when(i == 0)
        def _():
            sums_ref[...] = jnp.zeros_like(sums_ref)

        xv = x_ref[...]
        r = lax.rsqrt(jnp.mean(xv * xv, axis=-1, keepdims=True) + EPS)
        xn = xv * r
        err = xn * w_ref[...] - t_ref[...]
        dy = err * (1.0 / D)
        dxn = dy * w_ref[...]
        dx = r * (dxn - xn * jnp.mean(dxn * xn, axis=-1, keepdims=True))
        dx_ref[...] = dx
        dd_ref[...] = (dx * g_ref[...]).astype(BF16)
        sums_ref[0:1, :] += jnp.sum(dy * xn, axis=0, keepdims=True)
        sums_ref[1:2, :] += jnp.sum(dx * dn_ref[...], axis=0, keepdims=True)
        sums_ref[2:3, :] += jnp.sum(err * err, axis=0, keepdims=True) * (0.5 / D)

        @pl.when(i == n_steps - 1)
        def _():
            sums_ref[3:4, :] = jnp.broadcast_to(jnp.sum(sums_ref[2:3, :], axis=1, keepdims=True), (1, D))

    tile = pl.BlockSpec((tr, D), lambda i: (i, 0))
    row = pl.BlockSpec((1, D), lambda i: (0, 0))
    sums = pl.BlockSpec((8, D), lambda i: (0, 0))
    return _call(body, [x3, target, wf, down, gate_f], name="final_loss_bwd", grid=(n_steps,),
                 in_specs=[tile, tile, row, tile, row], out_specs=[tile, tile, sums],
                 out_shape=[jax.ShapeDtypeStruct((T, D), F32), jax.ShapeDtypeStruct((T, D), BF16),
                            jax.ShapeDtypeStruct((8, D), F32)], sem=("arbitrary",))


CONV_TC = 1024


def _conv_taps(xp, w, b):
    acc = b + w[3:4, :] * xp
    for k in range(3):
        acc = acc + w[k:k + 1, :] * pltpu.roll(xp, 3 - k, 0)
    return acc


def _conv_bwd(dxa, proj, conv_w, conv_b, dproj):
    T = proj.shape[0]
    tr = _row_tile(T)
    nb, offb, last = tr // 8, OFF_XBC // CONV_TC, T // tr - 1
    prev8 = lambda i: jnp.maximum(i * nb - 1, 0)
    next8 = lambda i: jnp.minimum((i + 1) * nb, T // 8 - 1)

    def body(d_ref, dn_ref, x_ref, xp_ref, xn_ref, w_ref, b_ref, dp_in, o_ref, sums_ref):
        del dp_in
        i = pl.program_id(1)

        @pl.when(i == 0)
        def _():
            sums_ref[...] = jnp.zeros_like(sums_ref)

        x = jnp.concatenate([jnp.where(i > 0, xp_ref[...], 0.0), x_ref[...], jnp.where(i < last, xn_ref[...], 0.0)], axis=0)
        d = jnp.concatenate([d_ref[...], jnp.where(i < last, dn_ref[...], 0.0)], axis=0)
        w = w_ref[...]
        taps = [pltpu.roll(x, 3 - k, 0)[8:] for k in range(3)] + [x[8:]]
        acc = b_ref[...] + w[3:4, :] * taps[3]
        for k in range(3):
            acc = acc + w[k:k + 1, :] * taps[k]
        s = _sigmoid(acc)
        dxc = d * (s * (1.0 + acc * (1.0 - s)))
        n = tr + 8
        dx = w[3:4, :] * dxc
        for k in range(3):
            dx = dx + w[k:k + 1, :] * pltpu.roll(dxc, n - (3 - k), 0)
        o_ref[...] = dx[:tr].astype(BF16)
        own = dxc[:tr]
        for k in range(4):
            sums_ref[k:k + 1, :] += jnp.sum(own * taps[k][:tr], axis=0, keepdims=True)
        sums_ref[4:5, :] += jnp.sum(own, axis=0, keepdims=True)

    return _call(
        body, [dxa, dxa, proj, proj, proj, conv_w, conv_b, dproj], name="conv_bwd", grid=(XBC // CONV_TC, T // tr),
        in_specs=[pl.BlockSpec((tr, CONV_TC), lambda j, i: (i, j)),
                  pl.BlockSpec((8, CONV_TC), lambda j, i: (next8(i), j)),
                  pl.BlockSpec((tr, CONV_TC), lambda j, i: (i, j + offb)),
                  pl.BlockSpec((8, CONV_TC), lambda j, i: (prev8(i), j + offb)),
                  pl.BlockSpec((8, CONV_TC), lambda j, i: (next8(i), j + offb)),
                  pl.BlockSpec((4, CONV_TC), lambda j, i: (0, j)),
                  pl.BlockSpec((1, CONV_TC), lambda j, i: (0, j)),
                  pl.BlockSpec(memory_space=pl.ANY)],
        out_specs=[pl.BlockSpec((tr, CONV_TC), lambda j, i: (i, j + offb)), pl.BlockSpec((8, CONV_TC), lambda j, i: (0, j))],
        out_shape=[jax.ShapeDtypeStruct(dproj.shape, BF16), jax.ShapeDtypeStruct((8, XBC), F32)],
        aliases={7: 0}, sem=("parallel", "arbitrary"))


def _spread(v, sel, pieces):
    out = None
    for _ in range(pieces):
        p = v.astype(BF16)
        term = _nn(p, sel)
        out = term if out is None else out + term
        v = v - p.astype(F32)
    return out


def _ssd_selectors():
    g = np.arange(GROUPS)[:, None, None]
    piece = np.arange(128)[None, :, None]
    h = np.where(piece < 3 * HEADS, piece % HEADS, -1)
    blocks = (h == 8 * g + np.arange(1024)[None, None, :] // 128)
    pairs = (h == 8 * g + np.arange(512)[None, None, :] // HEAD_DIM)
    lane = np.arange(128)[None, None, :]
    block_sum = (lane == 8 * g + np.arange(1024)[None, :, None] // 128)
    pair_sum = (lane == 8 * g + np.arange(512)[None, :, None] // HEAD_DIM)
    return [jnp.asarray(m, BF16) for m in (blocks, pairs, block_sum, pair_sum)]


def _pack3(v):
    p0 = v.astype(BF16)
    r1 = v - p0.astype(F32)
    p1 = r1.astype(BF16)
    r2 = r1 - p1.astype(F32)
    return p0 + pltpu.roll(r1, HEADS, 1).astype(BF16) + pltpu.roll(r2, 2 * HEADS, 1).astype(BF16)


def _ssd_group(g, cs_p, csT, dt_p, s_mat, causal_w, lo, blocks_ref, pairs_ref):
    csb = _nn(cs_p, blocks_ref[g])
    row = jnp.concatenate([csT[8 * g + hh:8 * g + hh + 1, :] for hh in range(8)], axis=1)
    l_w = jnp.exp(jnp.where(causal_w, csb - row, NEG))
    m_w = jnp.concatenate([s_mat] * 8, axis=1) * l_w
    cs_g = jnp.concatenate([jnp.where(lo, csb[:, 256 * jj:256 * jj + 128], csb[:, 256 * jj + 128:256 * jj + 256])
                            for jj in range(4)], axis=1)
    cs_last = cs_g[Q - 1:Q, :]
    return m_w, l_w, _nn(dt_p, pairs_ref[g]), jnp.exp(cs_g), jnp.exp(cs_last - cs_g), jnp.exp(cs_last)


def _ssd_common(dtp_ref, dtb_r, alog_r, dtb_c, alog_c):
    rows = lax.broadcasted_iota(jnp.int32, (Q, Q), 0)
    cols = lax.broadcasted_iota(jnp.int32, (Q, Q), 1)
    tri = (cols <= rows).astype(F32)
    heads = lax.broadcasted_iota(jnp.int32, (1, 128), 1) < HEADS
    raw_w = dtp_ref[...] + dtb_r[...]
    dt_w = jnp.where(heads, _softplus(raw_w), 0.0)
    a_w = -jnp.exp(alog_r[...])
    cs_w = _nn(tri, dt_w * a_w, precision=HIGH)
    aT = _softplus(dtp_ref[...].T[0:HEADS, :] + dtb_c[...]) * (-jnp.exp(alog_c[...]))
    csT = _nt(aT, tri, precision=HIGH)
    return raw_w[:, 0:HEADS], dt_w[:, 0:HEADS], a_w[:, 0:HEADS], csT, _pack3(cs_w), _pack3(dt_w)


def _ssd_fwd(proj, conv_w, conv_b, dtb_r, alog_r, dtb_c, alog_c, dsk_exp, norm_w, pool_w_b, pool_scale):
    T = proj.shape[0]
    nc = T // Q
    dtb_r, alog_r = [jnp.pad(a, ((0, 0), (0, 128 - HEADS))) for a in (dtb_r, alog_r)]

    def body(x0_ref, x1_ref, x2_ref, h0_ref, h1_ref, h2_ref, cw_ref, cb_ref, dtp_ref, z_ref, dtb_r_ref, alog_r_ref,
             dtb_c_ref, alog_c_ref, dsk_ref, nw_ref, blocks_ref, pairs_ref, u_ref, uh_ref, pw_ref, ps_ref,
             y_ref, hin_ref, yn_ref, xbc_ref, pooled_ref, pw_out_ref, yps_ref, h_scr):
        first = pl.program_id(0) == 0
        t_row = pl.program_id(0) * Q + lax.broadcasted_iota(jnp.int32, (Q, 1), 0)

        def pool(g):
            win, cols = POOL_WINDOWS[g], slice(GW * g, GW * (g + 1))
            u = u_ref[:, cols]
            s = jnp.concatenate([jnp.where(first, 0.0, uh_ref[:, cols]), u], axis=0)
            sh = 1
            while sh < win:
                s = s + pltpu.roll(s, sh, 0)
                sh *= 2
            pooled = (s[16:] * (1.0 / jnp.minimum(t_row + 1, win).astype(F32)) - u).astype(BF16)
            pooled_ref[:, cols] = pooled
            pwv = _nn(pooled, pw_ref[g])
            pw_out_ref[:, cols] = pwv
            yps_ref[:, cols] = (pwv * ps_ref[:, cols]).astype(BF16)

        @pl.when(first)
        def _():
            h_scr[...] = jnp.zeros_like(h_scr)

        x_refs, halo_refs = (x0_ref, x1_ref, x2_ref), (h0_ref, h1_ref, h2_ref)

        def conv(lo, hi):
            j, a, b = lo // 1024, lo % 1024, (hi - 1) % 1024 + 1
            halo = jnp.where(first, 0.0, halo_refs[j][:, a:b])
            acc = _conv_taps(jnp.concatenate([halo, x_refs[j][:, a:b]], axis=0), cw_ref[:, lo:hi], cb_ref[:, lo:hi])[8:]
            out = acc * _sigmoid(acc)
            xbc_ref[:, lo:hi] = out
            return out

        _, _, _, csT, cs_p, dt_p = _ssd_common(dtp_ref, dtb_r_ref, alog_r_ref, dtb_c_ref, alog_c_ref)
        lo = lax.broadcasted_iota(jnp.int32, (1, 128), 1) < HEAD_DIM
        hi = jnp.logical_not(lo)
        causal_w = (lax.broadcasted_iota(jnp.int32, (Q, 1024), 1) & (Q - 1)) <= lax.broadcasted_iota(jnp.int32, (Q, 1024), 0)
        for g in range(GROUPS):
            gs = slice(512 * g, 512 * (g + 1))
            hs = slice(128 * g, 128 * (g + 1))
            pool(g)
            xs_g = conv(512 * g, 512 * (g + 1))
            b_g = conv(DI + STATE * g, DI + STATE * (g + 1)).astype(BF16)
            c_g = conv(DI + 512 + STATE * g, DI + 512 + STATE * (g + 1)).astype(BF16)
            m_w, _, dt_g, ecs_g, dec_g, cd_g = _ssd_group(g, cs_p, csT, dt_p, _nt(c_g, b_g), causal_w, lo, blocks_ref, pairs_ref)
            m_b = m_w.astype(BF16)
            xdt = xs_g * dt_g
            xdt_b = xdt.astype(BF16)
            ys = []
            for jj in range(4):
                xp = xdt_b[:, 128 * jj:128 * (jj + 1)]
                x_ab = jnp.concatenate([jnp.where(lo, xp, jnp.zeros_like(xp)), jnp.where(hi, xp, jnp.zeros_like(xp))], axis=0)
                ys.append(_nn(m_b[:, 256 * jj:256 * (jj + 1)], x_ab))
            h_g = h_scr[hs, :]
            hin_ref[0, hs, :] = h_g
            y_g = jnp.concatenate(ys, axis=1) + _nn(c_g, h_g.astype(BF16)) * ecs_g + dsk_ref[:, gs] * xs_g
            y_ref[:, gs] = y_g
            z = z_ref[:, gs]
            yg = y_g * (z * _sigmoid(z))
            r = lax.rsqrt(jnp.mean(yg * yg, axis=-1, keepdims=True) + EPS)
            yn_ref[:, gs] = (yg * r * nw_ref[:, gs]).astype(BF16)
            h_scr[hs, :] = h_g * cd_g + _tn(b_g, (xdt * dec_g).astype(BF16))

    small_r = pl.BlockSpec((1, 128), lambda c: (0, 0))
    small_c = pl.BlockSpec((HEADS, 1), lambda c: (0, 0))
    blocks, pairs, _, _ = _ssd_selectors()
    whole = lambda a: pl.BlockSpec(a.shape, lambda c: (0,) * a.ndim)
    wide, row = pl.BlockSpec((Q, DI), lambda c: (c, 0)), pl.BlockSpec((1, DI), lambda c: (0, 0))
    xb = OFF_XBC // 1024
    x_specs = [pl.BlockSpec((Q, 1024), lambda c, j=j: (c, xb + j)) for j in range(3)]
    halo_specs = [pl.BlockSpec((8, 1024), lambda c, j=j: (jnp.maximum(c * (Q // 8) - 1, 0), xb + j)) for j in range(3)]
    return _call(
        body, [proj] * 6 + [conv_w, conv_b, proj, proj, dtb_r, alog_r, dtb_c, alog_c, dsk_exp, norm_w, blocks, pairs,
                            proj, proj, pool_w_b, pool_scale],
        name="ssd_fwd", grid=(nc,),
        in_specs=x_specs + halo_specs + [whole(conv_w), whole(conv_b),
                  pl.BlockSpec((Q, 128), lambda c: (c, OFF_DT // 128)), wide,
                  small_r, small_r, small_c, small_c, row, row, whole(blocks), whole(pairs),
                  pl.BlockSpec((Q, D), lambda c: (c, OFF_POOL // D)),
                  pl.BlockSpec((16, D), lambda c: (jnp.maximum(c * (Q // 16) - 1, 0), OFF_POOL // D)),
                  whole(pool_w_b), pl.BlockSpec((1, D), lambda c: (0, 0))],
        out_specs=[wide, pl.BlockSpec((1, 512, 512), lambda c: (c, 0, 0)), wide, pl.BlockSpec((Q, XBC), lambda c: (c, 0))]
        + [pl.BlockSpec((Q, D), lambda c: (c, 0))] * 3,
        out_shape=[jax.ShapeDtypeStruct((T, DI), F32), jax.ShapeDtypeStruct((nc, 512, 512), F32),
                   jax.ShapeDtypeStruct((T, DI), BF16), jax.ShapeDtypeStruct((T, XBC), F32),
                   jax.ShapeDtypeStruct((T, D), BF16), jax.ShapeDtypeStruct((T, D), F32), jax.ShapeDtypeStruct((T, D), BF16)],
        scratch_shapes=[pltpu.VMEM((512, 512), F32)], sem=("arbitrary",))


def _ssd_bwd(dyn, y, xbc_a, proj, hin, dtb_r, alog_r, dtb_c, alog_c, dsk_exp, norm_w, dproj, carry=None):
    T = xbc_a.shape[0]
    nc = T // Q
    dtb_r, alog_r = [jnp.pad(a, ((0, 0), (0, 128 - HEADS))) for a in (dtb_r, alog_r)]

    def body(dyn_ref, y_ref, z_ref, xbc_ref, dtp_ref, hin_ref, dtb_r_ref, alog_r_ref, dtb_c_ref, alog_c_ref, dsk_ref,
             nw_ref, dp_in, blocks_ref, pairs_ref, block_sum_ref, pair_sum_ref,
             dxa_ref, dz_ref, dsk_sum_ref, small_ref, dp_ref, gn_ref, dh_scr):
        del dp_in

        @pl.when(pl.program_id(0) == 0)
        def _():
            dh_scr[...] = jnp.zeros_like(dh_scr)
            dsk_sum_ref[...] = jnp.zeros_like(dsk_sum_ref)
            small_ref[...] = jnp.zeros_like(small_ref)
            gn_ref[...] = jnp.zeros_like(gn_ref)

        raw, dt, a_r, csT, cs_p, dt_p = _ssd_common(dtp_ref, dtb_r_ref, alog_r_ref, dtb_c_ref, alog_c_ref)
        lo = lax.broadcasted_iota(jnp.int32, (1, 128), 1) < HEAD_DIM
        hi = jnp.logical_not(lo)
        sub32 = lax.broadcasted_iota(jnp.int32, (HEADS, 1), 0)
        causal_w = (lax.broadcasted_iota(jnp.int32, (Q, 1024), 1) & (Q - 1)) <= lax.broadcasted_iota(jnp.int32, (Q, 1024), 0)
        dcs_c = jnp.zeros((Q, 128), F32)
        dcs_r = jnp.zeros((HEADS, Q), F32)
        dcs_l = jnp.zeros((8, 128), F32)
        ddt_x = jnp.zeros((Q, 128), F32)
        for g in range(GROUPS):
            gs = slice(512 * g, 512 * (g + 1))
            hs = slice(128 * g, 128 * (g + 1))
            xs_g = xbc_ref[:, gs]
            z, yv, d = z_ref[:, gs], y_ref[:, gs], dyn_ref[:, gs]
            s = _sigmoid(z)
            silu = z * s
            yg = yv * silu
            r = lax.rsqrt(jnp.mean(yg * yg, axis=-1, keepdims=True) + EPS)
            yn = yg * r
            gn_ref[0:1, gs] += jnp.sum(d * yn, axis=0, keepdims=True)
            dn = d * nw_ref[:, gs]
            dyg = r * (dn - yn * jnp.mean(dn * yn, axis=-1, keepdims=True))
            dy_g = dyg * silu
            dz_ref[:, gs] = (dyg * yv * (s * (1.0 + z * (1.0 - s)))).astype(BF16)
            b_g = xbc_ref[:, DI + STATE * g:DI + STATE * (g + 1)].astype(BF16)
            c_g = xbc_ref[:, DI + 512 + STATE * g:DI + 512 + STATE * (g + 1)].astype(BF16)
            m_w, l_w, dt_g, ecs_g, dec_g, cd_g = _ssd_group(g, cs_p, csT, dt_p, _nt(c_g, b_g), causal_w, lo, blocks_ref, pairs_ref)
            m_b = m_w.astype(BF16)
            xdt = xs_g * dt_g
            xdt_b, dy_b = xdt.astype(BF16), dy_g.astype(BF16)
            dms, dxs = [], []
            for jj in range(4):
                xp, dyp = xdt_b[:, 128 * jj:128 * (jj + 1)], dy_b[:, 128 * jj:128 * (jj + 1)]
                dy_ab = jnp.concatenate([jnp.where(lo, dyp, jnp.zeros_like(dyp)), jnp.where(hi, dyp, jnp.zeros_like(dyp))], axis=0)
                dm_ab = _nt(dy_ab, xp)
                dms += [dm_ab[:Q], dm_ab[Q:]]
                dx_ab = _tn(m_b[:, 256 * jj:256 * (jj + 1)], dyp)
                dxs.append(jnp.where(lo, dx_ab[:Q], dx_ab[Q:]))
            dm_w = jnp.concatenate(dms, axis=1)
            w_w = dm_w * m_w
            dcs_c = dcs_c + _spread(w_w, block_sum_ref[g], 2)
            w_cols = jnp.sum(w_w, axis=0, keepdims=True)
            for hh in range(8):
                dcs_r = dcs_r + jnp.where(sub32 == 8 * g + hh, w_cols[:, 128 * hh:128 * (hh + 1)], 0.0)
            dl_w = dm_w * l_w
            ds_mat = dl_w[:, 0:128]
            for hh in range(1, 8):
                ds_mat = ds_mat + dl_w[:, 128 * hh:128 * (hh + 1)]
            hin_g = hin_ref[0, hs, :]
            hin_b = hin_g.astype(BF16)
            dh_g = dh_scr[hs, :]
            dh_b = dh_g.astype(BF16)
            g_mat = _nn(b_g, dh_b)
            xdec = xdt * dec_g
            xg = xdec * g_mat
            dxdt = jnp.concatenate(dxs, axis=1) + dec_g * g_mat
            sums = _spread(jnp.concatenate([dy_g * (_nn(c_g, hin_b) * ecs_g) - xg, dxdt * xs_g], axis=0), pair_sum_ref[g], 2)
            dcs_c = dcs_c + sums[:Q]
            ddt_x = ddt_x + sums[Q:]
            last = jnp.sum(xg, axis=0, keepdims=True) + jnp.sum(dh_g * hin_g, axis=0, keepdims=True) * cd_g
            dcs_l = dcs_l + _spread(jnp.broadcast_to(last, (8, 512)), pair_sum_ref[g], 2)
            dz = (dy_g * ecs_g).astype(BF16)
            ds_b = ds_mat.astype(BF16)
            dxa_ref[:, gs] = dxdt * dt_g + dy_g * dsk_ref[:, gs]
            dxa_ref[:, DI + STATE * g:DI + STATE * (g + 1)] = _nt(xdec.astype(BF16), dh_b) + _tn(ds_b, c_g)
            dxa_ref[:, DI + 512 + STATE * g:DI + 512 + STATE * (g + 1)] = _nt(dz, hin_b) + _nn(ds_b, b_g)
            dh_scr[hs, :] = _tn(c_g, dz) + dh_g * cd_g
            dsk_sum_ref[0:1, gs] += jnp.sum(dy_g * xs_g, axis=0, keepdims=True)

        rows = lax.broadcasted_iota(jnp.int32, (Q, Q), 0)
        cols = lax.broadcasted_iota(jnp.int32, (Q, Q), 1)
        tri_t = (cols >= rows).astype(F32)
        last_row = lax.broadcasted_iota(jnp.int32, (Q, 1), 0) == Q - 1
        dcs = (dcs_c + jnp.where(last_row, dcs_l[0:1, :], 0.0))[:, 0:HEADS]
        da = _nn(tri_t, dcs, precision=HIGH) - _nt(tri_t, dcs_r, precision=HIGH)
        ddt_raw = (ddt_x[:, 0:HEADS] + da * a_r) * _sigmoid(raw)
        small_ref[0:1, :] += jnp.sum(da * dt, axis=0, keepdims=True) * a_r
        small_ref[1:2, :] += jnp.sum(ddt_raw, axis=0, keepdims=True)
        dp_ref[...] = jnp.zeros_like(dp_ref)
        dp_ref[:, 0:HEADS] = ddt_raw.astype(BF16)

    rev = lambda c: nc - 1 - c
    small_r = pl.BlockSpec((1, 128), lambda c: (0, 0))
    small_c = pl.BlockSpec((HEADS, 1), lambda c: (0, 0))
    selectors = _ssd_selectors()
    whole = lambda a: pl.BlockSpec(a.shape, lambda c: (0,) * a.ndim)
    wide, row = pl.BlockSpec((Q, DI), lambda c: (rev(c), 0)), pl.BlockSpec((1, DI), lambda c: (0, 0))
    sums = pl.BlockSpec((8, DI), lambda c: (0, 0))
    return _call(
        body, [dyn, y, proj, xbc_a, proj, hin, dtb_r, alog_r, dtb_c, alog_c, dsk_exp, norm_w, dproj, *selectors],
        name="ssd_bwd", grid=(nc,),
        in_specs=[wide, wide, wide,
                  pl.BlockSpec((Q, XBC), lambda c: (rev(c), 0)),
                  pl.BlockSpec((Q, 128), lambda c: (rev(c), OFF_DT // 128)),
                  pl.BlockSpec((1, 512, 512), lambda c: (rev(c), 0, 0)),
                  small_r, small_r, small_c, small_c, row, row,
                  pl.BlockSpec(memory_space=pl.ANY)] + [whole(a) for a in selectors],
        out_specs=[pl.BlockSpec((Q, XBC), lambda c: (rev(c), 0)),
                   pl.BlockSpec((Q, DI), lambda c: (rev(c), OFF_Z // DI)),
                   sums, pl.BlockSpec((8, HEADS), lambda c: (0, 0)), pl.BlockSpec((Q, 256), lambda c: (rev(c), 0)), sums],
        out_shape=[jax.ShapeDtypeStruct((T, XBC), F32), jax.ShapeDtypeStruct(dproj.shape, BF16),
                   jax.ShapeDtypeStruct((8, DI), F32), jax.ShapeDtypeStruct((8, HEADS), F32),
                   jax.ShapeDtypeStruct((T, 256), BF16), jax.ShapeDtypeStruct((8, DI), F32)],
        aliases={12: 1}, scratch_shapes=[pltpu.VMEM((512, 512), F32)], sem=("arbitrary",), carry=carry)


def _pool_fwd(proj, pool_w_b, pool_scale):
    T = proj.shape[0]
    tr = _row_tile(T)
    nb = tr // 16

    def body(u_ref, h_ref, pw_ref, ps_ref, pooled_ref, pw_out_ref, yps_ref):
        i = pl.program_id(0)
        t = i * tr + lax.broadcasted_iota(jnp.int32, (tr, 1), 0)
        for g, win in enumerate(POOL_WINDOWS):
            gs = slice(GW * g, GW * (g + 1))
            u = u_ref[:, gs]
            s = jnp.concatenate([jnp.where(i > 0, h_ref[:, gs], 0.0), u], axis=0)
            sh = 1
            while sh < win:
                s = s + pltpu.roll(s, sh, 0)
                sh *= 2
            pooled = (s[16:] * (1.0 / jnp.minimum(t + 1, win).astype(F32)) - u).astype(BF16)
            pooled_ref[:, gs] = pooled
            pwv = _nn(pooled, pw_ref[g])
            pw_out_ref[:, gs] = pwv
            yps_ref[:, gs] = (pwv * ps_ref[:, gs]).astype(BF16)

    tile = pl.BlockSpec((tr, D), lambda i: (i, 0))
    return _call(
        body, [proj, proj, pool_w_b, pool_scale], name="pool_fwd", grid=(T // tr,),
        in_specs=[pl.BlockSpec((tr, D), lambda i: (i, OFF_POOL // D)),
                  pl.BlockSpec((16, D), lambda i: (jnp.maximum(i * nb - 1, 0), OFF_POOL // D)),
                  pl.BlockSpec((4, GW, GW), lambda i: (0, 0, 0)),
                  pl.BlockSpec((1, D), lambda i: (0, 0))],
        out_specs=[tile, tile, tile],
        out_shape=[jax.ShapeDtypeStruct((T, D), BF16), jax.ShapeDtypeStruct((T, D), F32),
                   jax.ShapeDtypeStruct((T, D), BF16)], sem=("parallel",))


def _pool_bwd(dyp, pw_out, pooled, pool_w_b, pool_scale, dproj):
    T = dyp.shape[0]
    tr = _row_tile(T)
    nb, last = tr // 16, T // tr - 1

    def body(d_ref, h_ref, pwo_ref, pooled_ref, pw_ref, ps_ref, dp_in, du_ref, gpw_ref, sums_ref):
        del dp_in
        i = pl.program_id(0)

        @pl.when(i == 0)
        def _():
            gpw_ref[...] = jnp.zeros_like(gpw_ref)
            sums_ref[...] = jnp.zeros_like(sums_ref)

        n = tr + 16
        t = i * tr + lax.broadcasted_iota(jnp.int32, (n, 1), 0)
        sums_ref[0:1, :] += jnp.sum(d_ref[...] * pwo_ref[...], axis=0, keepdims=True)
        for g, win in enumerate(POOL_WINDOWS):
            gs = slice(GW * g, GW * (g + 1))
            d_ext = jnp.concatenate([d_ref[:, gs], jnp.where(i < last, h_ref[:, gs], 0.0)], axis=0)
            dpw = (d_ext * ps_ref[:, gs]).astype(BF16)
            dpooled = _nt(dpw, pw_ref[g])
            s = jnp.where(t < T, dpooled * (1.0 / jnp.minimum(t + 1, win).astype(F32)), 0.0)
            sh = 1
            while sh < win:
                s = s + pltpu.roll(s, n - sh, 0)
                sh *= 2
            du_ref[:, gs] = (s[:tr] - dpooled[:tr]).astype(BF16)
            gpw_ref[g] += _tn(pooled_ref[:, gs], dpw[:tr])

    tile = pl.BlockSpec((tr, D), lambda i: (i, 0))
    return _call(
        body, [dyp, dyp, pw_out, pooled, pool_w_b, pool_scale, dproj], name="pool_bwd", grid=(T // tr,),
        in_specs=[tile, pl.BlockSpec((16, D), lambda i: (jnp.minimum((i + 1) * nb, T // 16 - 1), 0)), tile, tile,
                  pl.BlockSpec((4, GW, GW), lambda i: (0, 0, 0)), pl.BlockSpec((1, D), lambda i: (0, 0)),
                  pl.BlockSpec(memory_space=pl.ANY)],
        out_specs=[pl.BlockSpec((tr, D), lambda i: (i, OFF_POOL // D)),
                   pl.BlockSpec((4, GW, GW), lambda i: (0, 0, 0)), pl.BlockSpec((8, D), lambda i: (0, 0))],
        out_shape=[jax.ShapeDtypeStruct(dproj.shape, BF16), jax.ShapeDtypeStruct((4, GW, GW), F32),
                   jax.ShapeDtypeStruct((8, D), F32)],
        aliases={6: 0}, sem=("arbitrary",))


def _merge(proj, y_ssd, y_pool):
    T = proj.shape[0]
    tr = _row_tile(T)

    def body(g_ref, a_ref, b_ref, o_ref):
        o_ref[...] = (_sigmoid(g_ref[:, 0:D]) * a_ref[...] + _sigmoid(g_ref[:, D:2 * D]) * b_ref[...]).astype(BF16)

    tile = pl.BlockSpec((tr, D), lambda i: (i, 0))
    return _call(body, [proj, y_ssd, y_pool], name="merge", grid=(T // tr,),
                 in_specs=[pl.BlockSpec((tr, 2 * D), lambda i: (i, OFF_GATE // (2 * D))), tile, tile], out_specs=[tile],
                 out_shape=[jax.ShapeDtypeStruct((T, D), BF16)], sem=("parallel",))[0]


def _merge_bwd(dmerged, proj, y_ssd, y_pool):
    T = proj.shape[0]
    tr = _row_tile(T)

    def body(d_ref, g_ref, a_ref, b_ref, da_ref, db_ref, dg_ref):
        d = d_ref[...]
        ga, gb = _sigmoid(g_ref[:, 0:D]), _sigmoid(g_ref[:, D:2 * D])
        da_ref[...] = (d * ga).astype(BF16)
        db_ref[...] = (d * gb).astype(BF16)
        dg_ref[:, 0:D] = (d * a_ref[...] * ga * (1.0 - ga)).astype(BF16)
        dg_ref[:, D:2 * D] = (d * b_ref[...] * gb * (1.0 - gb)).astype(BF16)

    tile = pl.BlockSpec((tr, D), lambda i: (i, 0))
    gates = pl.BlockSpec((tr, 2 * D), lambda i: (i, OFF_GATE // (2 * D)))
    return _call(body, [dmerged, proj, y_ssd, y_pool], name="merge_bwd", grid=(T // tr,),
                 in_specs=[tile, gates, tile, tile], out_specs=[tile, tile, gates],
                 out_shape=[jax.ShapeDtypeStruct((T, D), BF16), jax.ShapeDtypeStruct((T, D), BF16),
                            jax.ShapeDtypeStruct((T, NP), BF16)], sem=("parallel",))


def _adamw(w, g, m, v, name, carry=None):
    R, C = w.shape
    tr = R if R <= 128 else 128
    assert R % tr == 0

    def body(w_ref, g_ref, m_ref, v_ref, d_ref, mo_ref, vo_ref):
        gv = g_ref[...]
        mn = ADAM_B1 * m_ref[...] + (1.0 - ADAM_B1) * gv
        vn = ADAM_B2 * v_ref[...] + (1.0 - ADAM_B2) * (gv * gv)
        m_hat = mn * (1.0 / (1.0 - ADAM_B1 ** ADAM_STEP))
        v_hat = vn * (1.0 / (1.0 - ADAM_B2 ** ADAM_STEP))
        d_ref[...] = -ADAM_LR * (m_hat / (jnp.sqrt(v_hat) + ADAM_EPS) + ADAM_WD * w_ref[...])
        mo_ref[...] = mn
        vo_ref[...] = vn

    tile = pl.BlockSpec((tr, C), lambda i: (i, 0))
    sds = jax.ShapeDtypeStruct((R, C), F32)
    return _call(body, [w, g, m, v], name=name, grid=(R // tr,), in_specs=[tile] * 4, out_specs=[tile] * 3,
                 out_shape=[sds] * 3, sem=("parallel",), carry=carry)


def _me():
    return lax.axis_index("x"), lax.axis_index("y"), lax.axis_index("c")


def _xor_peer(x, y, c, p):
    return (x ^ ((p >> 2) & 1), y ^ ((p >> 1) & 1), c ^ (p & 1))


def _ada_fwd(c_row, w_ada, b_ada_mine, carry=None):
    n_cols = w_ada.shape[1]

    def body(c_ref, w_ref, b_ref, mod_ref, c8_ref, csend, mpart, modbuf, send_sems, recv_sems):
        x, y, c = _me()
        me = 4 * x + 2 * y + c
        chip = 2 * x + y
        csend[...] = jnp.broadcast_to(c_ref[...], csend.shape)
        c8_ref[me] = csend[...]

        def c_copy(p):
            return pltpu.make_async_remote_copy(
                src_ref=csend, dst_ref=c8_ref.at[me], send_sem=send_sems.at[p - 1], recv_sem=recv_sems.at[p - 1],
                device_id=_xor_peer(x, y, c, p), device_id_type=MESH)

        for p in range(1, 8):
            c_copy(p).start()
        for p in range(1, 8):
            c_copy(p).wait_recv()
        cs = jnp.concatenate([c8_ref[d][0:1, :] for d in range(8)], axis=0)
        mpart[...] = _nn(cs * _sigmoid(cs), w_ref[...], precision=HIGH) + b_ref[...]
        modbuf[chip] = mpart[...]

        def m_copy(m):
            return pltpu.make_async_remote_copy(
                src_ref=mpart, dst_ref=modbuf.at[chip], send_sem=send_sems.at[6 + m], recv_sem=recv_sems.at[6 + m],
                device_id=_xor_peer(x, y, c, 2 * m), device_id_type=MESH)

        for m in range(1, 4):
            m_copy(m).start()
        for m in range(1, 4):
            m_copy(m).wait_recv()
        mine = lax.broadcasted_iota(jnp.int32, (8, 1), 0) == me
        for k in range(N_CHIPS):
            mod_ref[:, n_cols * k:n_cols * (k + 1)] = jnp.sum(jnp.where(mine, modbuf[k], 0.0), axis=0, keepdims=True)
        for p in range(1, 8):
            c_copy(p).wait_send()
        for m in range(1, 4):
            m_copy(m).wait_send()

    vmem = pl.BlockSpec(memory_space=pltpu.VMEM)
    return _call(
        body, [c_row, w_ada, b_ada_mine], name="ada_fwd", in_specs=[vmem, vmem, vmem], out_specs=[vmem, vmem],
        out_shape=[jax.ShapeDtypeStruct((1, N_CHIPS * n_cols), F32), jax.ShapeDtypeStruct((8, 8, D), F32)],
        scratch_shapes=[pltpu.VMEM((8, D), F32), pltpu.VMEM((8, n_cols), F32), pltpu.VMEM((N_CHIPS, 8, n_cols), F32),
                        pltpu.SemaphoreType.DMA((10,)), pltpu.SemaphoreType.DMA((10,))], carry=carry)


def _gather_small(vec, carry=None):
    rows = vec.shape[0]

    def body(v_ref, all_ref, tot_ref, dsk_ref, send_sems, recv_sems):
        x, y, c = _me()
        me = 4 * x + 2 * y + c
        all_ref[me] = v_ref[...]

        def copy(p):
            return pltpu.make_async_remote_copy(
                src_ref=v_ref, dst_ref=all_ref.at[me], send_sem=send_sems.at[p - 1], recv_sem=recv_sems.at[p - 1],
                device_id=_xor_peer(x, y, c, p), device_id_type=MESH)

        for p in range(1, 8):
            copy(p).start()
        for p in range(1, 8):
            copy(p).wait_recv()
        tot = all_ref[0]
        for d in range(1, 8):
            tot = tot + all_ref[d]
        tot_ref[...] = tot
        seg = tot[SMALL_OFF["d_skip"] // 128:SMALL_OFF["d_skip"] // 128 + 16, :]
        lane = lax.broadcasted_iota(jnp.int32, (1, 128), 1)
        sa = jnp.sum(jnp.where(lane < HEAD_DIM, seg, 0.0), axis=1, keepdims=True)
        sb = jnp.sum(jnp.where(lane < HEAD_DIM, 0.0, seg), axis=1, keepdims=True)
        dsk_ref[...] = jnp.where(lane == 0, sa, jnp.where(lane == 1, sb, 0.0))
        for p in range(1, 8):
            copy(p).wait_send()

    vmem = pl.BlockSpec(memory_space=pltpu.VMEM)
    return _call(
        body, [vec], name="gather_small", in_specs=[vmem], out_specs=[vmem, vmem, vmem],
        out_shape=[jax.ShapeDtypeStruct((8, rows, 128), F32), jax.ShapeDtypeStruct((rows, 128), F32),
                   jax.ShapeDtypeStruct((16, 128), F32)],
        scratch_shapes=[pltpu.SemaphoreType.DMA((7,)), pltpu.SemaphoreType.DMA((7,))], carry=carry)


def _gather_carry(shards):
    n = len(shards)

    def copies(ins, outs, sems):
        x, y, c = _me()
        chip = 2 * x + y

        def half(w, which):
            h = shards[w].shape[0] // 2
            return pl.ds(which * h, h)

        def first(w, m):
            return pltpu.make_async_remote_copy(
                src_ref=ins[w].at[half(w, c)], dst_ref=outs[w].at[chip, half(w, c)],
                send_sem=sems.send(6 * w + m - 1), recv_sem=sems.recv(6 * w + m - 1),
                device_id=_xor_peer(x, y, c, 2 * m), device_id_type=MESH)

        def landed(w, m):
            return pltpu.make_async_remote_copy(
                src_ref=ins[w].at[half(w, c)], dst_ref=outs[w].at[chip ^ m, half(w, c)],
                send_sem=sems.send(6 * w + m - 1), recv_sem=sems.recv(6 * w + m - 1),
                device_id=_xor_peer(x, y, c, 2 * m), device_id_type=MESH)

        def passed(w, m, which):
            part = outs[w].at[chip ^ m, half(w, which)]
            return pltpu.make_async_remote_copy(
                src_ref=part, dst_ref=part, send_sem=sems.send(6 * w + 2 + m), recv_sem=sems.recv(6 * w + 2 + m),
                device_id=(x, y, 1 - c), device_id_type=MESH)

        return c, first, landed, passed

    pairs = [(w, m) for w in range(n) for m in range(1, 4)]

    def start(ins, outs, sems):
        _, first, _, _ = copies(ins, outs, sems)
        for w, m in pairs:
            first(w, m).start()

    def finish(ins, outs, sems):
        c, first, landed, passed = copies(ins, outs, sems)
        for w, m in pairs:
            landed(w, m).wait_recv()
            passed(w, m, c).start()
        for w, m in pairs:
            passed(w, m, 1 - c).wait_recv()
        for w, m in pairs:
            first(w, m).wait_send()
            passed(w, m, c).wait_send()

    return _Carry(shards, [jax.ShapeDtypeStruct((N_CHIPS,) + s.shape, s.dtype) for s in shards], 6 * n, start, finish)


def _pair_exchange_carry(grads):
    n = len(grads)

    def copy(ins, outs, sems, w):
        x, y, c = _me()
        h = grads[w].shape[1] // 2
        return pltpu.make_async_remote_copy(
            src_ref=ins[w].at[:, pl.ds((1 - c) * h, h)], dst_ref=outs[w],
            send_sem=sems.send(w), recv_sem=sems.recv(w), device_id=(x, y, 1 - c), device_id_type=MESH)

    def start(ins, outs, sems):
        for w in range(n):
            copy(ins, outs, sems, w).start()

    def finish(ins, outs, sems):
        for w in range(n):
            copy(ins, outs, sems, w).wait()

    return _Carry(grads, [jax.ShapeDtypeStruct((N_CHIPS, g.shape[1] // 2, g.shape[2]), g.dtype) for g in grads], n,
                  start, finish)


def _chip_exchange_carry(partials):
    n = len(partials)

    def copier(ins, outs, sems):
        x, y, c = _me()
        chip = 2 * x + y

        def copy(w, m, landed):
            return pltpu.make_async_remote_copy(
                src_ref=ins[w].at[chip ^ m], dst_ref=outs[w].at[(chip ^ m) if landed else chip],
                send_sem=sems.send(3 * w + m - 1), recv_sem=sems.recv(3 * w + m - 1),
                device_id=_xor_peer(x, y, c, 2 * m), device_id_type=MESH)

        return copy

    pairs = [(w, m) for w in range(n) for m in range(1, 4)]

    def start(ins, outs, sems):
        copy = copier(ins, outs, sems)
        for w, m in pairs:
            copy(w, m, False).start()

    def finish(ins, outs, sems):
        copy = copier(ins, outs, sems)
        for w, m in pairs:
            copy(w, m, True).wait_recv()
        for w, m in pairs:
            copy(w, m, False).wait_send()

    return _Carry(partials, [jax.ShapeDtypeStruct(p.shape, p.dtype) for p in partials], 3 * n, start, finish)


def _pair_share_carry(shards):
    n = len(shards)

    def copier(ins, outs, sems):
        x, y, c = _me()

        def copy(w, which):
            h = shards[w].shape[0] // 2
            rows = pl.ds(which * h, h)
            return pltpu.make_async_remote_copy(
                src_ref=ins[w].at[rows], dst_ref=outs[w].at[rows],
                send_sem=sems.send(w), recv_sem=sems.recv(w), device_id=(x, y, 1 - c), device_id_type=MESH)

        return c, copy

    def start(ins, outs, sems):
        c, copy = copier(ins, outs, sems)
        for w in range(n):
            copy(w, c).start()

    def finish(ins, outs, sems):
        c, copy = copier(ins, outs, sems)
        for w in range(n):
            copy(w, 1 - c).wait_recv()
        for w in range(n):
            copy(w, c).wait_send()

    return _Carry(shards, [jax.ShapeDtypeStruct(s.shape, s.dtype) for s in shards], n, start, finish,
                  aliased=[(w, w) for w in range(n)])


def _pair_sum(g, part, idx, name):
    _, h, C = part.shape
    tr = min(512, h)
    nb = h // tr

    def body(idx_ref, g_ref, p_ref, o16_ref, own_ref):
        v = g_ref[...].astype(F32) + p_ref[...].astype(F32)
        o16_ref[...] = v.astype(BF16)

        @pl.when(pl.program_id(1) == idx_ref[1])
        def _():
            own_ref[...] = v

    return pl.pallas_call(
        body, name=name,
        grid_spec=pltpu.PrefetchScalarGridSpec(
            num_scalar_prefetch=1, grid=(nb, N_CHIPS),
            in_specs=[pl.BlockSpec((None, tr, C), lambda i, s, idx_ref: (s, idx_ref[0] * nb + i, 0)),
                      pl.BlockSpec((None, tr, C), lambda i, s, idx_ref: (s, i, 0))],
            out_specs=[pl.BlockSpec((None, tr, C), lambda i, s, idx_ref: (s, i, 0)),
                       pl.BlockSpec((tr, C), lambda i, s, idx_ref: (i, 0))]),
        out_shape=[jax.ShapeDtypeStruct(part.shape, BF16), jax.ShapeDtypeStruct((h, C), F32)],
        compiler_params=pltpu.CompilerParams(dimension_semantics=("arbitrary", "arbitrary"), vmem_limit_bytes=VMEM_LIMIT),
    )(idx, g, part)


def _chip_sum(own, slots, idx, name):
    h, C = own.shape
    tr = min(512, h)
    nb = h // tr

    def body(idx_ref, own_ref, s1_ref, s2_ref, s3_ref, o_ref):
        del idx_ref
        o_ref[...] = ((own_ref[...] + s1_ref[...].astype(F32)) + s2_ref[...].astype(F32)) + s3_ref[...].astype(F32)

    def slot(m):
        return pl.BlockSpec((None, tr, C), lambda i, idx_ref: (idx_ref[1] ^ m, i, 0))

    return pl.pallas_call(
        body, name=name,
        grid_spec=pltpu.PrefetchScalarGridSpec(
            num_scalar_prefetch=1, grid=(nb,),
            in_specs=[pl.BlockSpec((tr, C), lambda i, idx_ref: (i, 0)), slot(1), slot(2), slot(3)],
            out_specs=pl.BlockSpec((tr, C), lambda i, idx_ref: (idx_ref[0] * nb + i, 0))),
        out_shape=jax.ShapeDtypeStruct((2 * h, C), F32),
        compiler_params=pltpu.CompilerParams(dimension_semantics=("parallel",), vmem_limit_bytes=VMEM_LIMIT),
    )(idx, own, slots, slots, slots)


class _Reducer:
    def __init__(self, idx):
        self.idx, self.chips, self.p16, self.own, self.mine, self.final = idx, {}, {}, {}, {}, {}

    def add(self, name, whole, chip_blocks=False):
        self.chips[name] = whole if chip_blocks else _chips_from_whole(name, whole)

    def pair(self, names):
        return _pair_exchange_carry([self.chips[n] for n in names])

    def take_pair(self, names, outs):
        for n, part in zip(names, outs):
            self.p16[n], self.own[n] = _pair_sum(self.chips.pop(n), part, self.idx, "pair_sum_" + n)

    def chip(self, names):
        return _chip_exchange_carry([self.p16[n] for n in names])

    def take_chip(self, names, outs):
        for n, slots in zip(names, outs):
            del self.p16[n]
            self.mine[n] = _chip_sum(self.own.pop(n), slots, self.idx, "chip_sum_" + n)

    def share(self, names):
        return _pair_share_carry([self.mine[n] for n in names])

    def take_share(self, names, outs):
        for n, s in zip(names, outs):
            del self.mine[n]
            self.final[n] = s


def _w_ada_grad(c8, dmod_cols):
    n_cols = dmod_cols.shape[1]
    tn = 512

    def body(c_ref, d_ref, o_ref):
        cv = c_ref[...]
        o_ref[...] = _tn(cv * _sigmoid(cv), d_ref[...], precision=HIGH)

    return _call(body, [c8, dmod_cols], name="w_ada_grad", grid=(n_cols // tn,),
                 in_specs=[pl.BlockSpec((8, D), lambda j: (0, 0)), pl.BlockSpec((8, tn), lambda j: (0, j))],
                 out_specs=[pl.BlockSpec((D, tn), lambda j: (0, j))],
                 out_shape=[jax.ShapeDtypeStruct((D, n_cols), F32)], sem=("parallel",))[0]


_SMALL_SEGS = (("dmod", 6144), ("norm_mix_w", 1024), ("conv_b", 3072), ("ssd_norm_w", 2048), ("pool_scale", 1024),
               ("norm_mlp_w", 1024), ("norm_final_w", 1024), ("conv_w", 4 * XBC), ("d_skip", 2048), ("a_log", 128),
               ("dt_bias", 128), ("loss", 128))
SMALL_OFF = {}
_o = 0
for _n, _s in _SMALL_SEGS:
    SMALL_OFF[_n] = _o
    _o += _s
SMALL_LEN = -(-_o // 1024) * 1024

_FIRST = ("w_in", "conv_w")
_LATER = ("w_branch_ssd", "pool_w", "w_branch_pool", "w_out", "w_up", "w_down")
_SMALL_REPLICATED = ("b_ada", "norm_mix_w", "conv_b", "dt_bias", "a_log", "d_skip", "ssd_norm_w", "pool_scale",
                     "norm_mlp_w", "norm_final_w")
_WEIGHTS = ("w_ada", "b_ada", "norm_mix_w", "w_in", "conv_w", "conv_b", "dt_bias", "a_log", "d_skip", "ssd_norm_w",
            "w_branch_ssd", "pool_w", "pool_scale", "w_branch_pool", "w_out", "norm_mlp_w", "w_up", "w_down",
            "norm_final_w")


def _shard_2d(name, a):
    if name == "conv_w":
        return a.reshape(16, -1)
    return (a.reshape(GW, GW) if name == "pool_w" else a.reshape(a.shape[-2], a.shape[-1])).astype(BF16)


def _whole_from_chips(name, g, own, chip):
    g = lax.dynamic_update_slice(g, own[None], (chip, 0, 0))
    if name == "w_in":
        a, b = _DT_IN_CHIP2, _DT_IN_CHIP2 + HEADS
        pad = jnp.zeros((D, NP - IN_COLS), g.dtype)
        return jnp.concatenate([g[0], g[1], g[2][:, :a], g[2][:, b:], g[3], g[2][:, a:b], pad], axis=1)
    if name == "w_up":
        return jnp.concatenate([g[k] for k in range(N_CHIPS)], axis=1)
    if name == "pool_w":
        return jnp.transpose(g.reshape(N_CHIPS, 4, GW // N_CHIPS, GW), (1, 0, 2, 3)).reshape(4, GW, GW)
    if name == "conv_w":
        return jnp.transpose(g.reshape(N_CHIPS, 4, XBC // N_CHIPS), (1, 0, 2)).reshape(4, XBC)
    return g.reshape(N_CHIPS * g.shape[1], g.shape[2])


def _chips_from_whole(name, g):
    if name.startswith("w_in"):
        cw, a = IN_COLS // N_CHIPS, _DT_IN_CHIP2
        chip2 = jnp.concatenate([g[:, 2 * cw:2 * cw + a], g[:, OFF_DT:OFF_DT + HEADS], g[:, 5120:3 * cw - HEADS]], axis=1)
        return jnp.stack([g[:, :cw], g[:, cw:2 * cw], chip2, g[:, 3 * cw - HEADS:OFF_DT]])
    if name == "w_up":
        return jnp.transpose(g.reshape(D, N_CHIPS, DFF // N_CHIPS), (1, 0, 2))
    if name == "pool_w":
        return jnp.transpose(g.reshape(4, N_CHIPS, GW // N_CHIPS, GW), (1, 0, 2, 3)).reshape(N_CHIPS, GW, GW)
    return g.reshape(N_CHIPS, g.shape[0] // N_CHIPS, g.shape[1])


def kernel(x, c, w_ada, b_ada, norm_mix_w, w_in, conv_w, conv_b, dt_bias, a_log, d_skip, ssd_norm_w, w_branch_ssd, pool_w, pool_scale, w_branch_pool, w_out, norm_mlp_w, w_up, w_down, norm_final_w, loss_target, m_w_ada, m_b_ada, m_norm_mix_w, m_w_in, m_conv_w, m_conv_b, m_dt_bias, m_a_log, m_d_skip, m_ssd_norm_w, m_w_branch_ssd, m_pool_w, m_pool_scale, m_w_branch_pool, m_w_out, m_norm_mlp_w, m_w_up, m_w_down, m_norm_final_w, v_w_ada, v_b_ada, v_norm_mix_w, v_w_in, v_conv_w, v_conv_b, v_dt_bias, v_a_log, v_d_skip, v_ssd_norm_w, v_w_branch_ssd, v_pool_w, v_pool_scale, v_w_branch_pool, v_w_out, v_norm_mlp_w, v_w_up, v_w_down, v_norm_final_w):
    args = locals()
    w = {n: args[n] for n in _WEIGHTS}
    m = {n: args["m_" + n] for n in _WEIGHTS}
    v = {n: args["v_" + n] for n in _WEIGHTS}
    xi, yi, ci = _me()
    chip = 2 * xi + yi
    idx = jnp.stack([ci, chip]).astype(jnp.int32)
    ada_cols = w_ada.shape[-1]
    xs, target = x[0], loss_target[0]
    two_d = lambda n, a: a.reshape(GW, GW) if n == "pool_w" else a.reshape(-1, a.shape[-1])
    delta, new_m, new_v, g = {}, {}, {}, {}

    def adamw(n, carry=None):
        res = _adamw(two_d(n, w[n]), two_d(n, g[n]), two_d(n, m[n]), two_d(n, v[n]), "adamw_" + n, carry=carry)
        (delta[n], new_m[n], new_v[n]), extra = res if carry is not None else (res, None)
        return extra

    b_mine = lax.dynamic_slice(b_ada, (0, chip * ada_cols), (1, ada_cols))
    shards = {n: _shard_2d(n, w[n]) for n in _FIRST + _LATER}
    mod, c8 = _ada_fwd(c, w_ada[0], b_mine)
    c8 = c8[:, 0, :]
    shift_m, scale_m, gate_m, shift_f, scale_f, gate_f = [mod[:, D * i:D * (i + 1)] for i in range(6)]
    nf_w = norm_final_w.reshape(1, D)

    h1, first = _norm_mod(xs, norm_mix_w, scale_m, shift_m, "norm_mod_mix",
                          carry=_gather_carry([shards[n] for n in _FIRST]))
    p ={n: _whole_from_chips(n, a, shards[n], chip) for n, a in zip(_FIRST, first)}
    (proj,), later = _matmul(h1, p["w_in"], mode="nn", out_dtypes=[F32], name="mm_proj", cols_outer=True,
                             carry=_gather_carry([shards[n] for n in _LATER]))
    p.update({n: _whole_from_chips(n, a, shards[n], chip) for n, a in zip(_LATER, later)})
    dtb_c, alog_c = dt_bias.reshape(HEADS, 1), a_log.reshape(HEADS, 1)
    dsk_exp = jnp.repeat(d_skip, HEAD_DIM, axis=1)
    y, hin, yn, xbc_a, pooled, pw_out, yps = _ssd_fwd(proj, p["conv_w"], conv_b, dt_bias, a_log, dtb_c, alog_c, dsk_exp,
                                                    ssd_norm_w, p["pool_w"], pool_scale)
    (y_ssd,) = _matmul(yn, p["w_branch_ssd"], mode="nn", out_dtypes=[F32], name="mm_branch_ssd")
    (y_pool,) = _matmul(yps, p["w_branch_pool"], mode="nn", out_dtypes=[F32], name="mm_branch_pool")
    merged = _merge(proj, y_ssd, y_pool)
    resid = lambda acc, r, gt: (r + gt * acc, acc)
    x2, mix = _matmul(merged, p["w_out"], mode="nn", out_dtypes=[F32, BF16], name="mm_out",
                      epi=resid, tile_extras=(xs,), row_extras=(gate_m,))
    h2 = _norm_mod(x2, norm_mlp_w, scale_f, shift_f, "norm_mod_mlp")
    relu2 = lambda acc: (jnp.square(jnp.maximum(acc, 0.0)),)
    (act,) = _matmul(h2, p["w_up"], mode="nn", out_dtypes=[BF16], name="mm_up", epi=relu2)
    x3, down = _matmul(act, p["w_down"], mode="nn", out_dtypes=[F32, BF16], name="mm_down",
                       epi=resid, tile_extras=(x2,), row_extras=(gate_f,))

    red = _Reducer(idx)
    dx3, d_down, sums_f = _final_loss_bwd(x3, target, nf_w, down, gate_f)
    drelu2 = lambda acc, a: (acc * (2.0 * jnp.sqrt(a)).astype(F32),)
    (dup,) = _matmul(d_down, p["w_down"], mode="nt", out_dtypes=[BF16], name="mm_dact",
                     epi=drelu2, tile_extras=(act,))
    red.add("w_down", _matmul(act, d_down, mode="tn", out_dtypes=[BF16], name="mm_g_down")[0])
    (dh2,), got = _matmul(dup, p["w_up"], mode="nt", out_dtypes=[F32], name="mm_dh2",
                          carry=red.pair(["w_down"]))
    red.take_pair(["w_down"], got)
    red.add("w_up", _matmul(h2, dup, mode="tn", out_dtypes=[BF16], name="mm_g_up", chip_blocks=True)[0], chip_blocks=True)
    dx2, sums_2, dmix = _norm_mod_bwd(x2, dh2, dx3, norm_mlp_w, scale_f, "norm_mod_mlp_bwd", branch=mix, gate=gate_m)
    (dmerged,), got = _matmul(dmix, p["w_out"], mode="nt", out_dtypes=[F32], name="mm_dmerged",
                              carry=red.pair(["w_up"]))
    red.take_pair(["w_up"], got)
    red.add("w_out", _matmul(merged, dmix, mode="tn", out_dtypes=[BF16], name="mm_g_out")[0])
    dy_ssd, dy_pool, dproj = _merge_bwd(dmerged, proj, y_ssd, y_pool)
    (dyp,), got = _matmul(dy_pool, p["w_branch_pool"], mode="nt", out_dtypes=[F32], name="mm_dyp",
                          carry=red.pair(["w_out"]))
    red.take_pair(["w_out"], got)
    red.add("w_branch_pool", _matmul(yps, dy_pool, mode="tn", out_dtypes=[BF16], name="mm_g_bpool")[0])
    dproj, g_pool_w, sums_pool = _pool_bwd(dyp, pw_out, pooled, p["pool_w"], pool_scale, dproj)
    red.add("pool_w", g_pool_w.astype(BF16))
    red.add("w_branch_ssd", _matmul(yn, dy_ssd, mode="tn", out_dtypes=[BF16], name="mm_g_bssd")[0])
    mixers = ["w_branch_pool", "pool_w", "w_branch_ssd"]
    (dyn,), got = _matmul(dy_ssd, p["w_branch_ssd"], mode="nt", out_dtypes=[F32], name="mm_dyn",
                          carry=red.pair(mixers))
    red.take_pair(mixers, got)
    six = ["w_down", "w_up", "w_out"] + mixers
    (dxa, dproj, dsk_sum, ssd_small, ddt, sums_gn), got = _ssd_bwd(
        dyn, y, xbc_a, proj, hin, dt_bias, a_log, dtb_c, alog_c, dsk_exp, ssd_norm_w, dproj, carry=red.chip(six))
    red.take_chip(six, got)
    dproj = lax.dynamic_update_slice(dproj, ddt, (0, OFF_DT))
    dproj, sums_conv = _conv_bwd(dxa, proj, p["conv_w"], conv_b, dproj)
    rows_a = 3 * D // 4
    (g_in_a,), got = _matmul(h1, dproj, mode="tn", out_dtypes=[BF16], name="mm_g_in_a", a_cols=(0, rows_a),
                             carry=red.share(six))
    red.take_share(six, got)
    red.add("w_in_a", g_in_a)
    (g_in_b,), got = _matmul(h1, dproj, mode="tn", out_dtypes=[BF16], name="mm_g_in_b", a_cols=(rows_a, D - rows_a),
                             carry=red.pair(["w_in_a"]))
    red.take_pair(["w_in_a"], got)
    red.add("w_in_b", g_in_b)
    (dh1,), got = _matmul(dproj, p["w_in"], mode="nt", out_dtypes=[F32], name="mm_dh1",
                          carry=_join(red.chip(["w_in_a"]), red.pair(["w_in_b"])))
    red.take_chip(["w_in_a"], got[:1])
    red.take_pair(["w_in_b"], got[1:])
    grad_x, sums_1 = _norm_mod_bwd(xs, dh1, dx2, norm_mix_w, scale_m, "norm_mod_mix_bwd")

    dmod = jnp.concatenate([sums_1[0:1], sums_1[1:2], sums_2[3:4], sums_2[0:1], sums_2[1:2], sums_f[1:2]], axis=1)
    pad96 = jnp.zeros((1, 96), F32)
    small = {"dmod": dmod, "norm_mix_w": sums_1[2:3], "conv_b": sums_conv[4:5], "ssd_norm_w": sums_gn[0:1],
             "pool_scale": sums_pool[0:1], "norm_mlp_w": sums_2[2:3], "norm_final_w": sums_f[0:1],
             "conv_w": sums_conv[0:4].reshape(1, 4 * XBC), "d_skip": dsk_sum[0:1],
             "a_log": jnp.concatenate([ssd_small[0:1], pad96], axis=1),
             "dt_bias": jnp.concatenate([ssd_small[1:2], pad96], axis=1), "loss": sums_f[3:4, 0:128]}
    vec = jnp.concatenate([small[n] for n, _ in _SMALL_SEGS], axis=1)
    vec = jnp.pad(vec, ((0, 0), (0, SMALL_LEN - vec.shape[1]))).reshape(SMALL_LEN // 128, 128)
    (every, total, dsk), got = _gather_small(vec, carry=_join(red.chip(["w_in_b"]), red.share(["w_in_a"])))
    red.take_chip(["w_in_b"], got[:1])
    red.take_share(["w_in_a"], got[1:])
    total = total.reshape(1, SMALL_LEN)
    seg = lambda n, size: total[:, SMALL_OFF[n]:SMALL_OFF[n] + size]
    g.update({"b_ada": seg("dmod", 6 * D), "norm_mix_w": seg("norm_mix_w", D), "conv_b": seg("conv_b", XBC),
              "dt_bias": seg("dt_bias", HEADS), "a_log": seg("a_log", HEADS), "d_skip": dsk[:, 0:2].reshape(1, HEADS),
              "ssd_norm_w": seg("ssd_norm_w", DI), "pool_scale": seg("pool_scale", D),
              "norm_mlp_w": seg("norm_mlp_w", D), "norm_final_w": seg("norm_final_w", D)})
    loss = total[0, SMALL_OFF["loss"]]
    conv_cols = conv_w.shape[-1]
    g["conv_w"] = lax.dynamic_slice(seg("conv_w", 4 * XBC).reshape(4, XBC), (0, chip * conv_cols), (4, conv_cols))
    dmod8 = every.reshape(8, SMALL_LEN)[:, SMALL_OFF["dmod"]:SMALL_OFF["dmod"] + 6 * D]
    g["w_ada"] = _w_ada_grad(c8, lax.dynamic_slice(dmod8, (0, chip * ada_cols), (8, ada_cols)))

    got = adamw("w_ada", carry=red.share(["w_in_b"]))
    red.take_share(["w_in_b"], got)
    for n in six:
        g[n] = red.final[n]
    g["w_in"] = jnp.concatenate([red.final["w_in_a"], red.final["w_in_b"]], axis=0)
    for n in ["conv_w", "w_in"] + six:
        adamw(n)
    sizes = [w[n].size for n in _SMALL_REPLICATED]
    n_small = -(-sum(sizes) // 1024) * 1024
    pack = lambda d: jnp.pad(jnp.concatenate([d[n].reshape(1, -1) for n in _SMALL_REPLICATED], axis=1),
                             ((0, 0), (0, n_small - sum(sizes)))).reshape(n_small // 128, 128)
    d_, m_, v_ = _adamw(pack(w), pack(g), pack(m), pack(v), "adamw_small")
    off = 0
    for n, s in zip(_SMALL_REPLICATED, sizes):
        for dst, src in ((delta, d_), (new_m, m_), (new_v, v_)):
            dst[n] = src.reshape(1, n_small)[:, off:off + s]
        off += s

    out = [loss, grad_x.reshape(x.shape)]
    for d in (g, delta, new_m, new_v):
        out += [d[n].reshape(w[n].shape) for n in _WEIGHTS]
    return tuple(out)
```

```python
import functools
import operator

import jax
import jax.numpy as jnp
import numpy as np
from jax import lax
from jax.experimental import pallas as pl
from jax.experimental.pallas import tpu as pltpu

F32, BF16 = jnp.float32, jnp.bfloat16
HIGH = lax.Precision.HIGHEST
MESH = pl.DeviceIdType.MESH

D = 1024
DI = 2048
HEADS, HEAD_DIM = 32, 64
GROUPS, STATE = 4, 128
Q = 128
XBC = DI + 2 * GROUPS * STATE
POOL_WINDOWS = (2, 4, 8, 16)
GW = 256
DFF = 4096
EPS = 1e-5
IN_COLS = 8224
OFF_Z, OFF_XBC, OFF_POOL, OFF_GATE, OFF_DT, NP = 0, 2048, 5120, 6144, 8192, 8448
N_CHIPS = 4
ADAM_LR, ADAM_B1, ADAM_B2, ADAM_EPS, ADAM_WD, ADAM_STEP = 0.001, 0.9, 0.999, 1e-08, 0.01, 10
VMEM_LIMIT = 56 * 2 ** 20
NEG = -1e30


def _sigmoid(v):
    return 0.5 * jnp.tanh(0.5 * v) + 0.5


def _softplus(v):
    return jnp.maximum(v, 0.0) + jnp.log1p(jnp.exp(-jnp.abs(v)))


def _dot(a, b, dims, **kw):
    return lax.dot_general(a, b, (dims, ((), ())), preferred_element_type=F32, **kw)


def _nn(a, b, **kw):
    return _dot(a, b, ((1,), (0,)), **kw)


def _nt(a, b, **kw):
    return _dot(a, b, ((1,), (1,)), **kw)


def _tn(a, b, **kw):
    return _dot(a, b, ((0,), (0,)), **kw)


_DT_IN_CHIP2 = 5120 - 2 * (IN_COLS // 4)


class _Sems:
    def __init__(self, send, recv, local, base=0):
        self._send, self._recv, self._local, self._base = send, recv, local, base

    def shift(self, n):
        return _Sems(self._send, self._recv, self._local, self._base + n)

    def send(self, i):
        return self._send.at[self._base + i]

    def recv(self, i):
        return self._recv.at[self._base + i]

    def local(self, i):
        return self._local.at[self._base + i]


class _Carry:
    def __init__(self, ins, out_shapes, n_sems, start, finish, aliased=()):
        self.ins, self.out_shapes, self.n_sems, self.start, self.finish = list(ins), list(out_shapes), n_sems, start, finish
        self.aliased = list(aliased)


def _join(*carries):
    def run(which):
        def fn(ins, outs, sems):
            i = o = s = 0
            for cy in carries:
                getattr(cy, which)(ins[i:i + len(cy.ins)], outs[o:o + len(cy.out_shapes)], sems.shift(s))
                i, o, s = i + len(cy.ins), o + len(cy.out_shapes), s + cy.n_sems
        return fn

    aliased, i, o = [], 0, 0
    for cy in carries:
        aliased += [(i + a, o + b) for a, b in cy.aliased]
        i, o = i + len(cy.ins), o + len(cy.out_shapes)
    return _Carry([a for cy in carries for a in cy.ins], [a for cy in carries for a in cy.out_shapes],
                  sum(cy.n_sems for cy in carries), run("start"), run("finish"), aliased)


def _call(body, args, *, name, grid=(), in_specs, out_specs, out_shape, scratch_shapes=(), sem=None, aliases=None,
          carry=None):
    in_specs, out_specs, out_shape, scratch_shapes = list(in_specs), list(out_specs), list(out_shape), list(scratch_shapes)
    n_in, n_out, n_scr = len(in_specs), len(out_specs), len(scratch_shapes)
    kw = {"vmem_limit_bytes": VMEM_LIMIT}
    if carry is None:
        kernel_fn = functools.partial(body)
        if sem is not None:
            kw["dimension_semantics"] = sem
    else:
        n_ci, n_co = len(carry.ins), len(carry.out_shapes)
        hbm = pl.BlockSpec(memory_space=pl.ANY)
        in_specs += [hbm] * n_ci
        out_specs += [hbm] * n_co
        out_shape += carry.out_shapes
        n_s = max(carry.n_sems, 1)
        scratch_shapes += [pltpu.SemaphoreType.DMA((n_s,))] * 3
        args = list(args) + carry.ins
        aliases = dict(aliases or {})
        aliases.update({n_in + i: n_out + o for i, o in carry.aliased})
        if grid:
            kw["dimension_semantics"] = ("arbitrary",) * len(grid)

        def kernel_fn(*refs):
            a = n_in
            ins, c_ins = refs[:a], refs[a:a + n_ci]
            a += n_ci
            outs, c_outs = refs[a:a + n_out], refs[a + n_out:a + n_out + n_co]
            a += n_out + n_co
            scr, sems = refs[a:a + n_scr], _Sems(*refs[a + n_scr:a + n_scr + 3])
            if grid:
                ids = [pl.program_id(d) for d in range(len(grid))]
                first = functools.reduce(operator.and_, [i == 0 for i in ids])
                last = functools.reduce(operator.and_, [i == g - 1 for i, g in zip(ids, grid)])

                @pl.when(first)
                def _():
                    carry.start(c_ins, c_outs, sems)

                body(*ins, *outs, *scr)

                @pl.when(last)
                def _():
                    carry.finish(c_ins, c_outs, sems)
            else:
                carry.start(c_ins, c_outs, sems)
                body(*ins, *outs, *scr)
                carry.finish(c_ins, c_outs, sems)

    outs = pl.pallas_call(
        kernel_fn, name=name, grid=grid, in_specs=in_specs, out_specs=out_specs, out_shape=out_shape,
        scratch_shapes=scratch_shapes, input_output_aliases=aliases or {},
        compiler_params=pltpu.CompilerParams(**kw),
    )(*args)
    outs = list(outs)
    return outs if carry is None else (outs[:n_out], outs[n_out:])


def _run_carry(carry, name):
    _, outs = _call(lambda: None, [], name=name, in_specs=[], out_specs=[], out_shape=[], carry=carry)
    return outs


_TILES = {
    "mm_proj": (1024, 2816, 1024), "mm_branch_ssd": (1024, 1024, 2048), "mm_branch_pool": (1024, 1024, 1024),
    "mm_out": (1024, 1024, 1024), "mm_up": (2048, 1024, 1024), "mm_down": (512, 1024, 4096),
    "mm_dact": (1024, 1024, 1024), "mm_g_down": (1024, 1024, 4096), "mm_dh2": (1024, 1024, 4096),
    "mm_g_up": (1024, 1024, 4096), "mm_dmerged": (1024, 1024, 1024), "mm_g_out": (1024, 1024, 2048),
    "mm_dyp": (1024, 1024, 1024), "mm_g_bpool": (1024, 1024, 2048), "mm_g_bssd": (1024, 1024, 4096),
    "mm_dyn": (1024, 1024, 1024), "mm_g_in_a": (768, 1408, 4096), "mm_g_in_b": (256, 2816, 2048),
    "mm_dh1": (1024, 1024, 4224),
}


def _matmul(a, b, *, mode, out_dtypes, name, epi=None, tile_extras=(), row_extras=(), carry=None, a_cols=None,
            chip_blocks=False, cols_outer=False):
    M, K = (a.shape[1], a.shape[0]) if mode == "tn" else a.shape
    N = b.shape[0] if mode == "nt" else b.shape[1]
    a_start, M = a_cols if a_cols is not None else (0, M)
    tm, tn, tk = _TILES[name]
    tm, tn, tk = min(tm, M), min(tn, N), min(tk, K)
    assert M % tm == 0 and N % tn == 0 and K % tk == 0 and a_start % tm == 0, (name, M, N, K, tm, tn, tk)
    a_off = a_start // tm
    if mode == "nn":
        a_spec = pl.BlockSpec((tm, tk), lambda i, j, k: (i, k))
        b_spec = pl.BlockSpec((tk, tn), lambda i, j, k: (k, j))
        dims = ((1,), (0,))
    elif mode == "nt":
        a_spec = pl.BlockSpec((tm, tk), lambda i, j, k: (i, k))
        b_spec = pl.BlockSpec((tn, tk), lambda i, j, k: (j, k))
        dims = ((1,), (1,))
    else:
        a_spec = pl.BlockSpec((tk, tm), lambda i, j, k: (k, i + a_off))
        b_spec = pl.BlockSpec((tk, tn), lambda i, j, k: (k, j))
        dims = ((0,), (0,))
    nk = K // tk
    n_te, n_re, n_out = len(tile_extras), len(row_extras), len(out_dtypes)
    if epi is None:
        epi = lambda acc: (acc,)

    def body(a_ref, b_ref, *rest):
        extras = rest[:n_te + n_re]
        outs = rest[n_te + n_re:n_te + n_re + n_out]
        p = _dot(a_ref[...], b_ref[...], dims)

        def finish(acc):
            vals = epi(acc, *[e[...] for e in extras])
            for o, v in zip(outs, vals):
                o[...] = v.astype(o.dtype)

        if nk == 1:
            finish(p)
        else:
            acc_ref = rest[-1]
            k = pl.program_id(2)

            @pl.when(k == 0)
            def _():
                acc_ref[...] = p

            @pl.when(k > 0)
            def _():
                acc_ref[...] += p

            @pl.when(k == nk - 1)
            def _():
                finish(acc_ref[...])

    tile_spec = pl.BlockSpec((tm, tn), lambda i, j, k: (i, j))
    row_spec = pl.BlockSpec((1, tn), lambda i, j, k: (0, j))
    out_spec, out_dims = tile_spec, (M, N)
    if chip_blocks:
        assert n_te == 0 and tn * N_CHIPS == N
        out_spec, out_dims = pl.BlockSpec((None, tm, tn), lambda i, j, k: (j, i, 0)), (N_CHIPS, M, tn)
    in_specs, grid = [a_spec, b_spec] + [tile_spec] * n_te + [row_spec] * n_re, (M // tm, N // tn, nk)
    if cols_outer:
        swap = lambda s: pl.BlockSpec(s.block_shape, lambda g0, g1, k, f=s.index_map: f(g1, g0, k))
        in_specs, out_spec, grid = [swap(s) for s in in_specs], swap(out_spec), (N // tn, M // tm, nk)
    return _call(
        body, [a, b, *tile_extras, *row_extras], name=name, grid=grid,
        in_specs=in_specs, out_specs=[out_spec] * n_out,
        out_shape=[jax.ShapeDtypeStruct(out_dims, dt) for dt in out_dtypes],
        scratch_shapes=[pltpu.VMEM((tm, tn), F32)] if nk > 1 else [],
        sem=("parallel", "parallel", "arbitrary"), carry=carry)


def _row_tile(T):
    return min(512, T)


def _norm_mod(x, nw, scale, shift, name, carry=None):
    T = x.shape[0]
    tr = _row_tile(T)

    def body(x_ref, nw_ref, sc_ref, sh_ref, o_ref):
        xv = x_ref[...]
        r = lax.rsqrt(jnp.mean(xv * xv, axis=-1, keepdims=True) + EPS)
        o_ref[...] = ((xv * r) * nw_ref[...] * (1.0 + sc_ref[...]) + sh_ref[...]).astype(BF16)

    tile = pl.BlockSpec((tr, D), lambda i: (i, 0))
    row = pl.BlockSpec((1, D), lambda i: (0, 0))
    res = _call(body, [x, nw, scale, shift], name=name, grid=(T // tr,), in_specs=[tile, row, row, row],
                out_specs=[tile], out_shape=[jax.ShapeDtypeStruct((T, D), BF16)], sem=("parallel",), carry=carry)
    return res[0] if carry is None else (res[0][0], res[1])


def _norm_mod_bwd(x, dh, dres, nw, scale, name, branch=None, gate=None, carry=None):
    T = x.shape[0]
    tr = _row_tile(T)
    with_branch = branch is not None

    def body(x_ref, dh_ref, dr_ref, nw_ref, sc_ref, *rest):
        if with_branch:
            br_ref, g_ref, dx_ref, sums_ref, db_ref = rest
        else:
            dx_ref, sums_ref = rest
        i = pl.program_id(0)

        @pl.when(i == 0)
        def _():
            sums_ref[...] = jnp.zeros_like(sums_ref)

        xv, dhv = x_ref[...], dh_ref[...]
        r = lax.rsqrt(jnp.mean(xv * xv, axis=-1, keepdims=True) + EPS)
        xn = xv * r
        g1 = dhv * (1.0 + sc_ref[...])
        dxn = g1 * nw_ref[...]
        dx = dr_ref[...] + r * (dxn - xn * jnp.mean(dxn * xn, axis=-1, keepdims=True))
        dx_ref[...] = dx
        sums_ref[0:1, :] += jnp.sum(dhv, axis=0, keepdims=True)
        sums_ref[1:2, :] += jnp.sum(dhv * (xn * nw_ref[...]), axis=0, keepdims=True)
        sums_ref[2:3, :] += jnp.sum(g1 * xn, axis=0, keepdims=True)
        if with_branch:
            db_ref[...] = (dx * g_ref[...]).astype(BF16)
            sums_ref[3:4, :] += jnp.sum(dx * br_ref[...], axis=0, keepdims=True)

    tile = pl.BlockSpec((tr, D), lambda i: (i, 0))
    row = pl.BlockSpec((1, D), lambda i: (0, 0))
    sums = pl.BlockSpec((8, D), lambda i: (0, 0))
    ins = [x, dh, dres, nw, scale] + ([branch, gate] if with_branch else [])
    in_specs = [tile, tile, tile, row, row] + ([tile, row] if with_branch else [])
    out_specs = [tile, sums] + ([tile] if with_branch else [])
    out_shape = [jax.ShapeDtypeStruct((T, D), F32), jax.ShapeDtypeStruct((8, D), F32)]
    if with_branch:
        out_shape.append(jax.ShapeDtypeStruct((T, D), BF16))
    return _call(body, ins, name=name, grid=(T // tr,), in_specs=in_specs, out_specs=out_specs, out_shape=out_shape,
                 sem=("arbitrary",), carry=carry)


def _final_loss_bwd(x3, target, wf, down, gate_f):
    T = x3.shape[0]
    tr = _row_tile(T)
    n_steps = T // tr

    def body(x_ref, t_ref, w_ref, dn_ref, g_ref, dx_ref, dd_ref, sums_ref):
        i = pl.program_id(0)

        @pl.when(i == 0)
        def _():
            sums_ref[...] = jnp.zeros_like(sums_ref)

        xv = x_ref[...]
        r = lax.rsqrt(jnp.mean(xv * xv, axis=-1, keepdims=True) + EPS)
        xn = xv * r
        err = xn * w_ref[...] - t_ref[...]
        dy = err * (1.0 / D)
        dxn = dy * w_ref[...]
        dx = r * (dxn - xn * jnp.mean(dxn * xn, axis=-1, keepdims=True))
        dx_ref[...] = dx
        dd_ref[...] = (dx * g_ref[...]).astype(BF16)
        sums_ref[0:1, :] += jnp.sum(dy * xn, axis=0, keepdims=True)
        sums_ref[1:2, :] += jnp.sum(dx * dn_ref[...], axis=0, keepdims=True)
        sums_ref[2:3, :] += jnp.sum(err * err, axis=0, keepdims=True) * (0.5 / D)

        @pl.when(i == n_steps - 1)
        def _():
            sums_ref[3:4, :] = jnp.broadcast_to(jnp.sum(sums_ref[2:3, :], axis=1, keepdims=True), (1, D))

    tile = pl.BlockSpec((tr, D), lambda i: (i, 0))
    row = pl.BlockSpec((1, D), lambda i: (0, 0))
    sums = pl.BlockSpec((8, D), lambda i: (0, 0))
    return _call(body, [x3, target, wf, down, gate_f], name="final_loss_bwd", grid=(n_steps,),
                 in_specs=[tile, tile, row, tile, row], out_specs=[tile, tile, sums],
                 out_shape=[jax.ShapeDtypeStruct((T, D), F32), jax.ShapeDtypeStruct((T, D), BF16),
                            jax.ShapeDtypeStruct((8, D), F32)], sem=("arbitrary",))


CONV_TC = 1024


def _conv_taps(xp, w, b):
    acc = b + w[3:4, :] * xp
    for k in range(3):
        acc = acc + w[k:k + 1, :] * pltpu.roll(xp, 3 - k, 0)
    return acc


def _conv_bwd(dxa, proj, conv_w, conv_b, dproj):
    T = proj.shape[0]
    tr = _row_tile(T)
    nb, offb, last = tr // 8, OFF_XBC // CONV_TC, T // tr - 1
    prev8 = lambda i: jnp.maximum(i * nb - 1, 0)
    next8 = lambda i: jnp.minimum((i + 1) * nb, T // 8 - 1)

    def body(d_ref, dn_ref, x_ref, xp_ref, xn_ref, w_ref, b_ref, dp_in, o_ref, sums_ref):
        del dp_in
        i = pl.program_id(1)

        @pl.when(i == 0)
        def _():
            sums_ref[...] = jnp.zeros_like(sums_ref)

        x = jnp.concatenate([jnp.where(i > 0, xp_ref[...], 0.0), x_ref[...], jnp.where(i < last, xn_ref[...], 0.0)], axis=0)
        d = jnp.concatenate([d_ref[...], jnp.where(i < last, dn_ref[...], 0.0)], axis=0)
        w = w_ref[...]
        taps = [pltpu.roll(x, 3 - k, 0)[8:] for k in range(3)] + [x[8:]]
        acc = b_ref[...] + w[3:4, :] * taps[3]
        for k in range(3):
            acc = acc + w[k:k + 1, :] * taps[k]
        s = _sigmoid(acc)
        dxc = d * (s * (1.0 + acc * (1.0 - s)))
        n = tr + 8
        dx = w[3:4, :] * dxc
        for k in range(3):
            dx = dx + w[k:k + 1, :] * pltpu.roll(dxc, n - (3 - k), 0)
        o_ref[...] = dx[:tr].astype(BF16)
        own = dxc[:tr]
        for k in range(4):
            sums_ref[k:k + 1, :] += jnp.sum(own * taps[k][:tr], axis=0, keepdims=True)
        sums_ref[4:5, :] += jnp.sum(own, axis=0, keepdims=True)

    return _call(
        body, [dxa, dxa, proj, proj, proj, conv_w, conv_b, dproj], name="conv_bwd", grid=(XBC // CONV_TC, T // tr),
        in_specs=[pl.BlockSpec((tr, CONV_TC), lambda j, i: (i, j)),
                  pl.BlockSpec((8, CONV_TC), lambda j, i: (next8(i), j)),
                  pl.BlockSpec((tr, CONV_TC), lambda j, i: (i, j + offb)),
                  pl.BlockSpec((8, CONV_TC), lambda j, i: (prev8(i), j + offb)),
                  pl.BlockSpec((8, CONV_TC), lambda j, i: (next8(i), j + offb)),
                  pl.BlockSpec((4, CONV_TC), lambda j, i: (0, j)),
                  pl.BlockSpec((1, CONV_TC), lambda j, i: (0, j)),
                  pl.BlockSpec(memory_space=pl.ANY)],
        out_specs=[pl.BlockSpec((tr, CONV_TC), lambda j, i: (i, j + offb)), pl.BlockSpec((8, CONV_TC), lambda j, i: (0, j))],
        out_shape=[jax.ShapeDtypeStruct(dproj.shape, BF16), jax.ShapeDtypeStruct((8, XBC), F32)],
        aliases={7: 0}, sem=("parallel", "arbitrary"))


def _spread(v, sel, pieces):
    out = None
    for _ in range(pieces):
        p = v.astype(BF16)
        term = _nn(p, sel)
        out = term if out is None else out + term
        v = v - p.astype(F32)
    return out


def _ssd_selectors():
    g = np.arange(GROUPS)[:, None, None]
    piece = np.arange(128)[None, :, None]
    h = np.where(piece < 3 * HEADS, piece % HEADS, -1)
    blocks = (h == 8 * g + np.arange(1024)[None, None, :] // 128)
    pairs = (h == 8 * g + np.arange(512)[None, None, :] // HEAD_DIM)
    lane = np.arange(128)[None, None, :]
    block_sum = (lane == 8 * g + np.arange(1024)[None, :, None] // 128)
    pair_sum = (lane == 8 * g + np.arange(512)[None, :, None] // HEAD_DIM)
    return [jnp.asarray(m, BF16) for m in (blocks, pairs, block_sum, pair_sum)]


def _pack3(v):
    p0 = v.astype(BF16)
    r1 = v - p0.astype(F32)
    p1 = r1.astype(BF16)
    r2 = r1 - p1.astype(F32)
    return p0 + pltpu.roll(r1, HEADS, 1).astype(BF16) + pltpu.roll(r2, 2 * HEADS, 1).astype(BF16)


def _ssd_group(g, cs_p, csT, dt_p, s_mat, causal_w, lo, blocks_ref, pairs_ref):
    csb = _nn(cs_p, blocks_ref[g])
    row = jnp.concatenate([csT[8 * g + hh:8 * g + hh + 1, :] for hh in range(8)], axis=1)
    l_w = jnp.exp(jnp.where(causal_w, csb - row, NEG))
    m_w = jnp.concatenate([s_mat] * 8, axis=1) * l_w
    cs_g = jnp.concatenate([jnp.where(lo, csb[:, 256 * jj:256 * jj + 128], csb[:, 256 * jj + 128:256 * jj + 256])
                            for jj in range(4)], axis=1)
    cs_last = cs_g[Q - 1:Q, :]
    return m_w, l_w, _nn(dt_p, pairs_ref[g]), jnp.exp(cs_g), jnp.exp(cs_last - cs_g), jnp.exp(cs_last)


def _ssd_common(dtp_ref, dtb_r, alog_r, dtb_c, alog_c):
    rows = lax.broadcasted_iota(jnp.int32, (Q, Q), 0)
    cols = lax.broadcasted_iota(jnp.int32, (Q, Q), 1)
    tri = (cols <= rows).astype(F32)
    heads = lax.broadcasted_iota(jnp.int32, (1, 128), 1) < HEADS
    raw_w = dtp_ref[...] + dtb_r[...]
    dt_w = jnp.where(heads, _softplus(raw_w), 0.0)
    a_w = -jnp.exp(alog_r[...])
    cs_w = _nn(tri, dt_w * a_w, precision=HIGH)
    aT = _softplus(dtp_ref[...].T[0:HEADS, :] + dtb_c[...]) * (-jnp.exp(alog_c[...]))
    csT = _nt(aT, tri, precision=HIGH)
    return raw_w[:, 0:HEADS], dt_w[:, 0:HEADS], a_w[:, 0:HEADS], csT, _pack3(cs_w), _pack3(dt_w)


def _ssd_fwd(proj, conv_w, conv_b, dtb_r, alog_r, dtb_c, alog_c, dsk_exp, norm_w):
    T = proj.shape[0]
    nc = T // Q
    dtb_r, alog_r = [jnp.pad(a, ((0, 0), (0, 128 - HEADS))) for a in (dtb_r, alog_r)]

    def body(x0_ref, x1_ref, x2_ref, h0_ref, h1_ref, h2_ref, cw_ref, cb_ref, dtp_ref, z_ref, dtb_r_ref, alog_r_ref,
             dtb_c_ref, alog_c_ref, dsk_ref, nw_ref, blocks_ref, pairs_ref, y_ref, hin_ref, yn_ref, xbc_ref, h_scr):
        first = pl.program_id(0) == 0

        @pl.when(first)
        def _():
            h_scr[...] = jnp.zeros_like(h_scr)

        x_refs, halo_refs = (x0_ref, x1_ref, x2_ref), (h0_ref, h1_ref, h2_ref)

        def conv(lo, hi):
            j, a, b = lo // 1024, lo % 1024, (hi - 1) % 1024 + 1
            halo = jnp.where(first, 0.0, halo_refs[j][:, a:b])
            acc = _conv_taps(jnp.concatenate([halo, x_refs[j][:, a:b]], axis=0), cw_ref[:, lo:hi], cb_ref[:, lo:hi])[8:]
            out = acc * _sigmoid(acc)
            xbc_ref[:, lo:hi] = out
            return out

        _, _, _, csT, cs_p, dt_p = _ssd_common(dtp_ref, dtb_r_ref, alog_r_ref, dtb_c_ref, alog_c_ref)
        lo = lax.broadcasted_iota(jnp.int32, (1, 128), 1) < HEAD_DIM
        hi = jnp.logical_not(lo)
        causal_w = (lax.broadcasted_iota(jnp.int32, (Q, 1024), 1) & (Q - 1)) <= lax.broadcasted_iota(jnp.int32, (Q, 1024), 0)
        for g in range(GROUPS):
            gs = slice(512 * g, 512 * (g + 1))
            hs = slice(128 * g, 128 * (g + 1))
            xs_g = conv(512 * g, 512 * (g + 1))
            b_g = conv(DI + STATE * g, DI + STATE * (g + 1)).astype(BF16)
            c_g = conv(DI + 512 + STATE * g, DI + 512 + STATE * (g + 1)).astype(BF16)
            m_w, _, dt_g, ecs_g, dec_g, cd_g = _ssd_group(g, cs_p, csT, dt_p, _nt(c_g, b_g), causal_w, lo, blocks_ref, pairs_ref)
            m_b = m_w.astype(BF16)
            xdt = xs_g * dt_g
            xdt_b = xdt.astype(BF16)
            ys = []
            for jj in range(4):
                xp = xdt_b[:, 128 * jj:128 * (jj + 1)]
                x_ab = jnp.concatenate([jnp.where(lo, xp, jnp.zeros_like(xp)), jnp.where(hi, xp, jnp.zeros_like(xp))], axis=0)
                ys.append(_nn(m_b[:, 256 * jj:256 * (jj + 1)], x_ab))
            h_g = h_scr[hs, :]
            hin_ref[0, hs, :] = h_g
            y_g = jnp.concatenate(ys, axis=1) + _nn(c_g, h_g.astype(BF16)) * ecs_g + dsk_ref[:, gs] * xs_g
            y_ref[:, gs] = y_g
            z = z_ref[:, gs]
            yg = y_g * (z * _sigmoid(z))
            r = lax.rsqrt(jnp.mean(yg * yg, axis=-1, keepdims=True) + EPS)
            yn_ref[:, gs] = (yg * r * nw_ref[:, gs]).astype(BF16)
            h_scr[hs, :] = h_g * cd_g + _tn(b_g, (xdt * dec_g).astype(BF16))

    small_r = pl.BlockSpec((1, 128), lambda c: (0, 0))
    small_c = pl.BlockSpec((HEADS, 1), lambda c: (0, 0))
    blocks, pairs, _, _ = _ssd_selectors()
    whole = lambda a: pl.BlockSpec(a.shape, lambda c: (0,) * a.ndim)
    wide, row = pl.BlockSpec((Q, DI), lambda c: (c, 0)), pl.BlockSpec((1, DI), lambda c: (0, 0))
    xb = OFF_XBC // 1024
    x_specs = [pl.BlockSpec((Q, 1024), lambda c, j=j: (c, xb + j)) for j in range(3)]
    halo_specs = [pl.BlockSpec((8, 1024), lambda c, j=j: (jnp.maximum(c * (Q // 8) - 1, 0), xb + j)) for j in range(3)]
    return _call(
        body, [proj] * 6 + [conv_w, conv_b, proj, proj, dtb_r, alog_r, dtb_c, alog_c, dsk_exp, norm_w, blocks, pairs],
        name="ssd_fwd", grid=(nc,),
        in_specs=x_specs + halo_specs + [whole(conv_w), whole(conv_b),
                  pl.BlockSpec((Q, 128), lambda c: (c, OFF_DT // 128)), wide,
                  small_r, small_r, small_c, small_c, row, row, whole(blocks), whole(pairs)],
        out_specs=[wide, pl.BlockSpec((1, 512, 512), lambda c: (c, 0, 0)), wide, pl.BlockSpec((Q, XBC), lambda c: (c, 0))],
        out_shape=[jax.ShapeDtypeStruct((T, DI), F32), jax.ShapeDtypeStruct((nc, 512, 512), F32),
                   jax.ShapeDtypeStruct((T, DI), BF16), jax.ShapeDtypeStruct((T, XBC), F32)],
        scratch_shapes=[pltpu.VMEM((512, 512), F32)], sem=("arbitrary",))


def _ssd_bwd(dyn, y, xbc_a, proj, hin, dtb_r, alog_r, dtb_c, alog_c, dsk_exp, norm_w, dproj, carry=None):
    T = xbc_a.shape[0]
    nc = T // Q
    dtb_r, alog_r = [jnp.pad(a, ((0, 0), (0, 128 - HEADS))) for a in (dtb_r, alog_r)]

    def body(dyn_ref, y_ref, z_ref, xbc_ref, dtp_ref, hin_ref, dtb_r_ref, alog_r_ref, dtb_c_ref, alog_c_ref, dsk_ref,
             nw_ref, dp_in, blocks_ref, pairs_ref, block_sum_ref, pair_sum_ref,
             dxa_ref, dz_ref, dsk_sum_ref, small_ref, dp_ref, gn_ref, dh_scr):
        del dp_in

        @pl.when(pl.program_id(0) == 0)
        def _():
            dh_scr[...] = jnp.zeros_like(dh_scr)
            dsk_sum_ref[...] = jnp.zeros_like(dsk_sum_ref)
            small_ref[...] = jnp.zeros_like(small_ref)
            gn_ref[...] = jnp.zeros_like(gn_ref)

        raw, dt, a_r, csT, cs_p, dt_p = _ssd_common(dtp_ref, dtb_r_ref, alog_r_ref, dtb_c_ref, alog_c_ref)
        lo = lax.broadcasted_iota(jnp.int32, (1, 128), 1) < HEAD_DIM
        hi = jnp.logical_not(lo)
        sub32 = lax.broadcasted_iota(jnp.int32, (HEADS, 1), 0)
        causal_w = (lax.broadcasted_iota(jnp.int32, (Q, 1024), 1) & (Q - 1)) <= lax.broadcasted_iota(jnp.int32, (Q, 1024), 0)
        dcs_c = jnp.zeros((Q, 128), F32)
        dcs_r = jnp.zeros((HEADS, Q), F32)
        dcs_l = jnp.zeros((8, 128), F32)
        ddt_x = jnp.zeros((Q, 128), F32)
        for g in range(GROUPS):
            gs = slice(512 * g, 512 * (g + 1))
            hs = slice(128 * g, 128 * (g + 1))
            xs_g = xbc_ref[:, gs]
            z, yv, d = z_ref[:, gs], y_ref[:, gs], dyn_ref[:, gs]
            s = _sigmoid(z)
            silu = z * s
            yg = yv * silu
            r = lax.rsqrt(jnp.mean(yg * yg, axis=-1, keepdims=True) + EPS)
            yn = yg * r
            gn_ref[0:1, gs] += jnp.sum(d * yn, axis=0, keepdims=True)
            dn = d * nw_ref[:, gs]
            dyg = r * (dn - yn * jnp.mean(dn * yn, axis=-1, keepdims=True))
            dy_g = dyg * silu
            dz_ref[:, gs] = (dyg * yv * (s * (1.0 + z * (1.0 - s)))).astype(BF16)
            b_g = xbc_ref[:, DI + STATE * g:DI + STATE * (g + 1)].astype(BF16)
            c_g = xbc_ref[:, DI + 512 + STATE * g:DI + 512 + STATE * (g + 1)].astype(BF16)
            m_w, l_w, dt_g, ecs_g, dec_g, cd_g = _ssd_group(g, cs_p, csT, dt_p, _nt(c_g, b_g), causal_w, lo, blocks_ref, pairs_ref)
            m_b = m_w.astype(BF16)
            xdt = xs_g * dt_g
            xdt_b, dy_b = xdt.astype(BF16), dy_g.astype(BF16)
            dms, dxs = [], []
            for jj in range(4):
                xp, dyp = xdt_b[:, 128 * jj:128 * (jj + 1)], dy_b[:, 128 * jj:128 * (jj + 1)]
                dy_ab = jnp.concatenate([jnp.where(lo, dyp, jnp.zeros_like(dyp)), jnp.where(hi, dyp, jnp.zeros_like(dyp))], axis=0)
                dm_ab = _nt(dy_ab, xp)
                dms += [dm_ab[:Q], dm_ab[Q:]]
                dx_ab = _tn(m_b[:, 256 * jj:256 * (jj + 1)], dyp)
                dxs.append(jnp.where(lo, dx_ab[:Q], dx_ab[Q:]))
            ds_mat = jnp.zeros((Q, Q), F32)
            for hf in range(2):
                sl = slice(512 * hf, 512 * (hf + 1))
                dm_h = jnp.concatenate(dms[4 * hf:4 * hf + 4], axis=1)
                w_h = dm_h * m_w[:, sl]
                dcs_c = dcs_c + _spread(w_h, block_sum_ref[g, 512 * hf:512 * (hf + 1), :], 2)
                w_cols = jnp.sum(w_h, axis=0, keepdims=True)
                dl_h = dm_h * l_w[:, sl]
                for hh in range(4):
                    dcs_r = dcs_r + jnp.where(sub32 == 8 * g + 4 * hf + hh, w_cols[:, 128 * hh:128 * (hh + 1)], 0.0)
                    ds_mat = ds_mat + dl_h[:, 128 * hh:128 * (hh + 1)]
            hin_g = hin_ref[0, hs, :]
            hin_b = hin_g.astype(BF16)
            dh_g = dh_scr[hs, :]
            dh_b = dh_g.astype(BF16)
            g_mat = _nn(b_g, dh_b)
            xdec = xdt * dec_g
            xg = xdec * g_mat
            dxdt = jnp.concatenate(dxs, axis=1) + dec_g * g_mat
            sums = _spread(jnp.concatenate([dy_g * (_nn(c_g, hin_b) * ecs_g) - xg, dxdt * xs_g], axis=0), pair_sum_ref[g], 2)
            dcs_c = dcs_c + sums[:Q]
            ddt_x = ddt_x + sums[Q:]
            last = jnp.sum(xg, axis=0, keepdims=True) + jnp.sum(dh_g * hin_g, axis=0, keepdims=True) * cd_g
            dcs_l = dcs_l + _spread(jnp.broadcast_to(last, (8, 512)), pair_sum_ref[g], 2)
            dz = (dy_g * ecs_g).astype(BF16)
            ds_b = ds_mat.astype(BF16)
            dxa_ref[:, gs] = dxdt * dt_g + dy_g * dsk_ref[:, gs]
            dxa_ref[:, DI + STATE * g:DI + STATE * (g + 1)] = _nt(xdec.astype(BF16), dh_b) + _tn(ds_b, c_g)
            dxa_ref[:, DI + 512 + STATE * g:DI + 512 + STATE * (g + 1)] = _nt(dz, hin_b) + _nn(ds_b, b_g)
            dh_scr[hs, :] = _tn(c_g, dz) + dh_g * cd_g
            dsk_sum_ref[0:1, gs] += jnp.sum(dy_g * xs_g, axis=0, keepdims=True)

        rows = lax.broadcasted_iota(jnp.int32, (Q, Q), 0)
        cols = lax.broadcasted_iota(jnp.int32, (Q, Q), 1)
        tri_t = (cols >= rows).astype(F32)
        last_row = lax.broadcasted_iota(jnp.int32, (Q, 1), 0) == Q - 1
        dcs = (dcs_c + jnp.where(last_row, dcs_l[0:1, :], 0.0))[:, 0:HEADS]
        da = _nn(tri_t, dcs, precision=HIGH) - _nt(tri_t, dcs_r, precision=HIGH)
        ddt_raw = (ddt_x[:, 0:HEADS] + da * a_r) * _sigmoid(raw)
        small_ref[0:1, :] += jnp.sum(da * dt, axis=0, keepdims=True) * a_r
        small_ref[1:2, :] += jnp.sum(ddt_raw, axis=0, keepdims=True)
        dp_ref[...] = jnp.zeros_like(dp_ref)
        dp_ref[:, 0:HEADS] = ddt_raw.astype(BF16)

    rev = lambda c: nc - 1 - c
    small_r = pl.BlockSpec((1, 128), lambda c: (0, 0))
    small_c = pl.BlockSpec((HEADS, 1), lambda c: (0, 0))
    selectors = _ssd_selectors()
    whole = lambda a: pl.BlockSpec(a.shape, lambda c: (0,) * a.ndim)
    wide, row = pl.BlockSpec((Q, DI), lambda c: (rev(c), 0)), pl.BlockSpec((1, DI), lambda c: (0, 0))
    sums = pl.BlockSpec((8, DI), lambda c: (0, 0))
    return _call(
        body, [dyn, y, proj, xbc_a, proj, hin, dtb_r, alog_r, dtb_c, alog_c, dsk_exp, norm_w, dproj, *selectors],
        name="ssd_bwd", grid=(nc,),
        in_specs=[wide, wide, wide,
                  pl.BlockSpec((Q, XBC), lambda c: (rev(c), 0)),
                  pl.BlockSpec((Q, 128), lambda c: (rev(c), OFF_DT // 128)),
                  pl.BlockSpec((1, 512, 512), lambda c: (rev(c), 0, 0)),
                  small_r, small_r, small_c, small_c, row, row,
                  pl.BlockSpec(memory_space=pl.ANY)] + [whole(a) for a in selectors],
        out_specs=[pl.BlockSpec((Q, XBC), lambda c: (rev(c), 0)),
                   pl.BlockSpec((Q, DI), lambda c: (rev(c), OFF_Z // DI)),
                   sums, pl.BlockSpec((8, HEADS), lambda c: (0, 0)), pl.BlockSpec((Q, 256), lambda c: (rev(c), 0)), sums],
        out_shape=[jax.ShapeDtypeStruct((T, XBC), F32), jax.ShapeDtypeStruct(dproj.shape, BF16),
                   jax.ShapeDtypeStruct((8, DI), F32), jax.ShapeDtypeStruct((8, HEADS), F32),
                   jax.ShapeDtypeStruct((T, 256), BF16), jax.ShapeDtypeStruct((8, DI), F32)],
        aliases={12: 1}, scratch_shapes=[pltpu.VMEM((512, 512), F32)], sem=("arbitrary",), carry=carry)


def _pool_fwd(proj, pool_w_b, pool_scale):
    T = proj.shape[0]
    tr = _row_tile(T)
    nb = tr // 16

    def body(u_ref, h_ref, pw_ref, ps_ref, pooled_ref, pw_out_ref, yps_ref):
        i = pl.program_id(0)
        t = i * tr + lax.broadcasted_iota(jnp.int32, (tr, 1), 0)
        for g, win in enumerate(POOL_WINDOWS):
            gs = slice(GW * g, GW * (g + 1))
            u = u_ref[:, gs]
            s = jnp.concatenate([jnp.where(i > 0, h_ref[:, gs], 0.0), u], axis=0)
            sh = 1
            while sh < win:
                s = s + pltpu.roll(s, sh, 0)
                sh *= 2
            pooled = (s[16:] * (1.0 / jnp.minimum(t + 1, win).astype(F32)) - u).astype(BF16)
            pooled_ref[:, gs] = pooled
            pwv = _nn(pooled, pw_ref[g])
            pw_out_ref[:, gs] = pwv
            yps_ref[:, gs] = (pwv * ps_ref[:, gs]).astype(BF16)

    tile = pl.BlockSpec((tr, D), lambda i: (i, 0))
    return _call(
        body, [proj, proj, pool_w_b, pool_scale], name="pool_fwd", grid=(T // tr,),
        in_specs=[pl.BlockSpec((tr, D), lambda i: (i, OFF_POOL // D)),
                  pl.BlockSpec((16, D), lambda i: (jnp.maximum(i * nb - 1, 0), OFF_POOL // D)),
                  pl.BlockSpec((4, GW, GW), lambda i: (0, 0, 0)),
                  pl.BlockSpec((1, D), lambda i: (0, 0))],
        out_specs=[tile, tile, tile],
        out_shape=[jax.ShapeDtypeStruct((T, D), BF16), jax.ShapeDtypeStruct((T, D), F32),
                   jax.ShapeDtypeStruct((T, D), BF16)], sem=("parallel",))


def _pool_bwd(dyp, pw_out, pooled, pool_w_b, pool_scale, dproj):
    T = dyp.shape[0]
    tr = _row_tile(T)
    nb, last = tr // 16, T // tr - 1

    def body(d_ref, h_ref, pwo_ref, pooled_ref, pw_ref, ps_ref, dp_in, du_ref, gpw_ref, sums_ref):
        del dp_in
        i = pl.program_id(0)

        @pl.when(i == 0)
        def _():
            gpw_ref[...] = jnp.zeros_like(gpw_ref)
            sums_ref[...] = jnp.zeros_like(sums_ref)

        n = tr + 16
        t = i * tr + lax.broadcasted_iota(jnp.int32, (n, 1), 0)
        sums_ref[0:1, :] += jnp.sum(d_ref[...] * pwo_ref[...], axis=0, keepdims=True)
        for g, win in enumerate(POOL_WINDOWS):
            gs = slice(GW * g, GW * (g + 1))
            d_ext = jnp.concatenate([d_ref[:, gs], jnp.where(i < last, h_ref[:, gs], 0.0)], axis=0)
            dpw = (d_ext * ps_ref[:, gs]).astype(BF16)
            dpooled = _nt(dpw, pw_ref[g])
            s = jnp.where(t < T, dpooled * (1.0 / jnp.minimum(t + 1, win).astype(F32)), 0.0)
            sh = 1
            while sh < win:
                s = s + pltpu.roll(s, n - sh, 0)
                sh *= 2
            du_ref[:, gs] = (s[:tr] - dpooled[:tr]).astype(BF16)
            gpw_ref[g] += _tn(pooled_ref[:, gs], dpw[:tr])

    tile = pl.BlockSpec((tr, D), lambda i: (i, 0))
    return _call(
        body, [dyp, dyp, pw_out, pooled, pool_w_b, pool_scale, dproj], name="pool_bwd", grid=(T // tr,),
        in_specs=[tile, pl.BlockSpec((16, D), lambda i: (jnp.minimum((i + 1) * nb, T // 16 - 1), 0)), tile, tile,
                  pl.BlockSpec((4, GW, GW), lambda i: (0, 0, 0)), pl.BlockSpec((1, D), lambda i: (0, 0)),
                  pl.BlockSpec(memory_space=pl.ANY)],
        out_specs=[pl.BlockSpec((tr, D), lambda i: (i, OFF_POOL // D)),
                   pl.BlockSpec((4, GW, GW), lambda i: (0, 0, 0)), pl.BlockSpec((8, D), lambda i: (0, 0))],
        out_shape=[jax.ShapeDtypeStruct(dproj.shape, BF16), jax.ShapeDtypeStruct((4, GW, GW), F32),
                   jax.ShapeDtypeStruct((8, D), F32)],
        aliases={6: 0}, sem=("arbitrary",))


def _merge(proj, y_ssd, y_pool):
    T = proj.shape[0]
    tr = _row_tile(T)

    def body(g_ref, a_ref, b_ref, o_ref):
        o_ref[...] = (_sigmoid(g_ref[:, 0:D]) * a_ref[...] + _sigmoid(g_ref[:, D:2 * D]) * b_ref[...]).astype(BF16)

    tile = pl.BlockSpec((tr, D), lambda i: (i, 0))
    return _call(body, [proj, y_ssd, y_pool], name="merge", grid=(T // tr,),
                 in_specs=[pl.BlockSpec((tr, 2 * D), lambda i: (i, OFF_GATE // (2 * D))), tile, tile], out_specs=[tile],
                 out_shape=[jax.ShapeDtypeStruct((T, D), BF16)], sem=("parallel",))[0]


def _merge_bwd(dmerged, proj, y_ssd, y_pool):
    T = proj.shape[0]
    tr = _row_tile(T)

    def body(d_ref, g_ref, a_ref, b_ref, da_ref, db_ref, dg_ref):
        d = d_ref[...]
        ga, gb = _sigmoid(g_ref[:, 0:D]), _sigmoid(g_ref[:, D:2 * D])
        da_ref[...] = (d * ga).astype(BF16)
        db_ref[...] = (d * gb).astype(BF16)
        dg_ref[:, 0:D] = (d * a_ref[...] * ga * (1.0 - ga)).astype(BF16)
        dg_ref[:, D:2 * D] = (d * b_ref[...] * gb * (1.0 - gb)).astype(BF16)

    tile = pl.BlockSpec((tr, D), lambda i: (i, 0))
    gates = pl.BlockSpec((tr, 2 * D), lambda i: (i, OFF_GATE // (2 * D)))
    return _call(body, [dmerged, proj, y_ssd, y_pool], name="merge_bwd", grid=(T // tr,),
                 in_specs=[tile, gates, tile, tile], out_specs=[tile, tile, gates],
                 out_shape=[jax.ShapeDtypeStruct((T, D), BF16), jax.ShapeDtypeStruct((T, D), BF16),
                            jax.ShapeDtypeStruct((T, NP), BF16)], sem=("parallel",))


def _adamw(w, g, m, v, name, carry=None):
    R, C = w.shape
    tr = R if R <= 128 else 128
    assert R % tr == 0

    def body(w_ref, g_ref, m_ref, v_ref, d_ref, mo_ref, vo_ref):
        gv = g_ref[...]
        mn = ADAM_B1 * m_ref[...] + (1.0 - ADAM_B1) * gv
        vn = ADAM_B2 * v_ref[...] + (1.0 - ADAM_B2) * (gv * gv)
        m_hat = mn * (1.0 / (1.0 - ADAM_B1 ** ADAM_STEP))
        v_hat = vn * (1.0 / (1.0 - ADAM_B2 ** ADAM_STEP))
        d_ref[...] = -ADAM_LR * (m_hat / (jnp.sqrt(v_hat) + ADAM_EPS) + ADAM_WD * w_ref[...])
        mo_ref[...] = mn
        vo_ref[...] = vn

    tile = pl.BlockSpec((tr, C), lambda i: (i, 0))
    sds = jax.ShapeDtypeStruct((R, C), F32)
    return _call(body, [w, g, m, v], name=name, grid=(R // tr,), in_specs=[tile] * 4, out_specs=[tile] * 3,
                 out_shape=[sds] * 3, sem=("parallel",), carry=carry)


def _me():
    return lax.axis_index("x"), lax.axis_index("y"), lax.axis_index("c")


def _xor_peer(x, y, c, p):
    return (x ^ ((p >> 2) & 1), y ^ ((p >> 1) & 1), c ^ (p & 1))


def _ada_fwd(c_row, w_ada, b_ada_mine, carry=None):
    n_cols = w_ada.shape[1]

    def body(c_ref, w_ref, b_ref, mod_ref, c8_ref, csend, mpart, modbuf, send_sems, recv_sems):
        x, y, c = _me()
        me = 4 * x + 2 * y + c
        chip = 2 * x + y
        csend[...] = jnp.broadcast_to(c_ref[...], csend.shape)
        c8_ref[me] = csend[...]

        def c_copy(p):
            return pltpu.make_async_remote_copy(
                src_ref=csend, dst_ref=c8_ref.at[me], send_sem=send_sems.at[p - 1], recv_sem=recv_sems.at[p - 1],
                device_id=_xor_peer(x, y, c, p), device_id_type=MESH)

        for p in range(1, 8):
            c_copy(p).start()
        for p in range(1, 8):
            c_copy(p).wait_recv()
        cs = jnp.concatenate([c8_ref[d][0:1, :] for d in range(8)], axis=0)
        mpart[...] = _nn(cs * _sigmoid(cs), w_ref[...], precision=HIGH) + b_ref[...]
        modbuf[chip] = mpart[...]

        def m_copy(m):
            return pltpu.make_async_remote_copy(
                src_ref=mpart, dst_ref=modbuf.at[chip], send_sem=send_sems.at[6 + m], recv_sem=recv_sems.at[6 + m],
                device_id=_xor_peer(x, y, c, 2 * m), device_id_type=MESH)

        for m in range(1, 4):
            m_copy(m).start()
        for m in range(1, 4):
            m_copy(m).wait_recv()
        mine = lax.broadcasted_iota(jnp.int32, (8, 1), 0) == me
        for k in range(N_CHIPS):
            mod_ref[:, n_cols * k:n_cols * (k + 1)] = jnp.sum(jnp.where(mine, modbuf[k], 0.0), axis=0, keepdims=True)
        for p in range(1, 8):
            c_copy(p).wait_send()
        for m in range(1, 4):
            m_copy(m).wait_send()

    vmem = pl.BlockSpec(memory_space=pltpu.VMEM)
    return _call(
        body, [c_row, w_ada, b_ada_mine], name="ada_fwd", in_specs=[vmem, vmem, vmem], out_specs=[vmem, vmem],
        out_shape=[jax.ShapeDtypeStruct((1, N_CHIPS * n_cols), F32), jax.ShapeDtypeStruct((8, 8, D), F32)],
        scratch_shapes=[pltpu.VMEM((8, D), F32), pltpu.VMEM((8, n_cols), F32), pltpu.VMEM((N_CHIPS, 8, n_cols), F32),
                        pltpu.SemaphoreType.DMA((10,)), pltpu.SemaphoreType.DMA((10,))], carry=carry)


def _gather_small(vec, carry=None):
    rows = vec.shape[0]

    def body(v_ref, all_ref, tot_ref, dsk_ref, send_sems, recv_sems):
        x, y, c = _me()
        me = 4 * x + 2 * y + c
        all_ref[me] = v_ref[...]

        def copy(p):
            return pltpu.make_async_remote_copy(
                src_ref=v_ref, dst_ref=all_ref.at[me], send_sem=send_sems.at[p - 1], recv_sem=recv_sems.at[p - 1],
                device_id=_xor_peer(x, y, c, p), device_id_type=MESH)

        for p in range(1, 8):
            copy(p).start()
        for p in range(1, 8):
            copy(p).wait_recv()
        tot = all_ref[0]
        for d in range(1, 8):
            tot = tot + all_ref[d]
        tot_ref[...] = tot
        seg = tot[SMALL_OFF["d_skip"] // 128:SMALL_OFF["d_skip"] // 128 + 16, :]
        lane = lax.broadcasted_iota(jnp.int32, (1, 128), 1)
        sa = jnp.sum(jnp.where(lane < HEAD_DIM, seg, 0.0), axis=1, keepdims=True)
        sb = jnp.sum(jnp.where(lane < HEAD_DIM, 0.0, seg), axis=1, keepdims=True)
        dsk_ref[...] = jnp.where(lane == 0, sa, jnp.where(lane == 1, sb, 0.0))
        for p in range(1, 8):
            copy(p).wait_send()

    vmem = pl.BlockSpec(memory_space=pltpu.VMEM)
    return _call(
        body, [vec], name="gather_small", in_specs=[vmem], out_specs=[vmem, vmem, vmem],
        out_shape=[jax.ShapeDtypeStruct((8, rows, 128), F32), jax.ShapeDtypeStruct((rows, 128), F32),
                   jax.ShapeDtypeStruct((16, 128), F32)],
        scratch_shapes=[pltpu.SemaphoreType.DMA((7,)), pltpu.SemaphoreType.DMA((7,))], carry=carry)


def _gather_carry(shards):
    n = len(shards)

    def copies(ins, outs, sems):
        x, y, c = _me()
        chip = 2 * x + y

        def half(w, which):
            h = shards[w].shape[0] // 2
            return pl.ds(which * h, h)

        def first(w, m):
            return pltpu.make_async_remote_copy(
                src_ref=ins[w].at[half(w, c)], dst_ref=outs[w].at[chip, half(w, c)],
                send_sem=sems.send(6 * w + m - 1), recv_sem=sems.recv(6 * w + m - 1),
                device_id=_xor_peer(x, y, c, 2 * m), device_id_type=MESH)

        def landed(w, m):
            return pltpu.make_async_remote_copy(
                src_ref=ins[w].at[half(w, c)], dst_ref=outs[w].at[chip ^ m, half(w, c)],
                send_sem=sems.send(6 * w + m - 1), recv_sem=sems.recv(6 * w + m - 1),
                device_id=_xor_peer(x, y, c, 2 * m), device_id_type=MESH)

        def passed(w, m, which):
            part = outs[w].at[chip ^ m, half(w, which)]
            return pltpu.make_async_remote_copy(
                src_ref=part, dst_ref=part, send_sem=sems.send(6 * w + 2 + m), recv_sem=sems.recv(6 * w + 2 + m),
                device_id=(x, y, 1 - c), device_id_type=MESH)

        return c, first, landed, passed

    pairs = [(w, m) for w in range(n) for m in range(1, 4)]

    def start(ins, outs, sems):
        _, first, _, _ = copies(ins, outs, sems)
        for w, m in pairs:
            first(w, m).start()

    def finish(ins, outs, sems):
        c, first, landed, passed = copies(ins, outs, sems)
        for w, m in pairs:
            landed(w, m).wait_recv()
            passed(w, m, c).start()
        for w, m in pairs:
            passed(w, m, 1 - c).wait_recv()
        for w, m in pairs:
            first(w, m).wait_send()
            passed(w, m, c).wait_send()

    return _Carry(shards, [jax.ShapeDtypeStruct((N_CHIPS,) + s.shape, s.dtype) for s in shards], 6 * n, start, finish)


def _pair_exchange_carry(grads):
    n = len(grads)

    def copy(ins, outs, sems, w):
        x, y, c = _me()
        h = grads[w].shape[1] // 2
        return pltpu.make_async_remote_copy(
            src_ref=ins[w].at[:, pl.ds((1 - c) * h, h)], dst_ref=outs[w],
            send_sem=sems.send(w), recv_sem=sems.recv(w), device_id=(x, y, 1 - c), device_id_type=MESH)

    def start(ins, outs, sems):
        for w in range(n):
            copy(ins, outs, sems, w).start()

    def finish(ins, outs, sems):
        for w in range(n):
            copy(ins, outs, sems, w).wait()

    return _Carry(grads, [jax.ShapeDtypeStruct((N_CHIPS, g.shape[1] // 2, g.shape[2]), g.dtype) for g in grads], n,
                  start, finish)


def _chip_exchange_carry(partials):
    n = len(partials)

    def copier(ins, outs, sems):
        x, y, c = _me()
        chip = 2 * x + y

        def copy(w, m, landed):
            return pltpu.make_async_remote_copy(
                src_ref=ins[w].at[chip ^ m], dst_ref=outs[w].at[(chip ^ m) if landed else chip],
                send_sem=sems.send(3 * w + m - 1), recv_sem=sems.recv(3 * w + m - 1),
                device_id=_xor_peer(x, y, c, 2 * m), device_id_type=MESH)

        return copy

    pairs = [(w, m) for w in range(n) for m in range(1, 4)]

    def start(ins, outs, sems):
        copy = copier(ins, outs, sems)
        for w, m in pairs:
            copy(w, m, False).start()

    def finish(ins, outs, sems):
        copy = copier(ins, outs, sems)
        for w, m in pairs:
            copy(w, m, True).wait_recv()
        for w, m in pairs:
            copy(w, m, False).wait_send()

    return _Carry(partials, [jax.ShapeDtypeStruct(p.shape, p.dtype) for p in partials], 3 * n, start, finish)


def _pair_share_carry(shards):
    n = len(shards)

    def copier(ins, outs, sems):
        x, y, c = _me()

        def copy(w, which):
            h = shards[w].shape[0] // 2
            rows = pl.ds(which * h, h)
            return pltpu.make_async_remote_copy(
                src_ref=ins[w].at[rows], dst_ref=outs[w].at[rows],
                send_sem=sems.send(w), recv_sem=sems.recv(w), device_id=(x, y, 1 - c), device_id_type=MESH)

        return c, copy

    def start(ins, outs, sems):
        c, copy = copier(ins, outs, sems)
        for w in range(n):
            copy(w, c).start()

    def finish(ins, outs, sems):
        c, copy = copier(ins, outs, sems)
        for w in range(n):
            copy(w, 1 - c).wait_recv()
        for w in range(n):
            copy(w, c).wait_send()

    return _Carry(shards, [jax.ShapeDtypeStruct(s.shape, s.dtype) for s in shards], n, start, finish,
                  aliased=[(w, w) for w in range(n)])


def _pair_sum(g, part, idx, name):
    _, h, C = part.shape
    tr = min(512, h)
    nb = h // tr

    def body(idx_ref, g_ref, p_ref, o16_ref, own_ref):
        v = g_ref[...].astype(F32) + p_ref[...].astype(F32)
        o16_ref[...] = v.astype(BF16)

        @pl.when(pl.program_id(1) == idx_ref[1])
        def _():
            own_ref[...] = v

    return pl.pallas_call(
        body, name=name,
        grid_spec=pltpu.PrefetchScalarGridSpec(
            num_scalar_prefetch=1, grid=(nb, N_CHIPS),
            in_specs=[pl.BlockSpec((None, tr, C), lambda i, s, idx_ref: (s, idx_ref[0] * nb + i, 0)),
                      pl.BlockSpec((None, tr, C), lambda i, s, idx_ref: (s, i, 0))],
            out_specs=[pl.BlockSpec((None, tr, C), lambda i, s, idx_ref: (s, i, 0)),
                       pl.BlockSpec((tr, C), lambda i, s, idx_ref: (i, 0))]),
        out_shape=[jax.ShapeDtypeStruct(part.shape, BF16), jax.ShapeDtypeStruct((h, C), F32)],
        compiler_params=pltpu.CompilerParams(dimension_semantics=("arbitrary", "arbitrary"), vmem_limit_bytes=VMEM_LIMIT),
    )(idx, g, part)


def _chip_sum(own, slots, idx, name):
    h, C = own.shape
    tr = min(512, h)
    nb = h // tr

    def body(idx_ref, own_ref, s1_ref, s2_ref, s3_ref, o_ref):
        del idx_ref
        o_ref[...] = ((own_ref[...] + s1_ref[...].astype(F32)) + s2_ref[...].astype(F32)) + s3_ref[...].astype(F32)

    def slot(m):
        return pl.BlockSpec((None, tr, C), lambda i, idx_ref: (idx_ref[1] ^ m, i, 0))

    return pl.pallas_call(
        body, name=name,
        grid_spec=pltpu.PrefetchScalarGridSpec(
            num_scalar_prefetch=1, grid=(nb,),
            in_specs=[pl.BlockSpec((tr, C), lambda i, idx_ref: (i, 0)), slot(1), slot(2), slot(3)],
            out_specs=pl.BlockSpec((tr, C), lambda i, idx_ref: (idx_ref[0] * nb + i, 0))),
        out_shape=jax.ShapeDtypeStruct((2 * h, C), F32),
        compiler_params=pltpu.CompilerParams(dimension_semantics=("parallel",), vmem_limit_bytes=VMEM_LIMIT),
    )(idx, own, slots, slots, slots)


class _Reducer:
    def __init__(self, idx):
        self.idx, self.chips, self.p16, self.own, self.mine, self.final = idx, {}, {}, {}, {}, {}

    def add(self, name, whole, chip_blocks=False):
        self.chips[name] = whole if chip_blocks else _chips_from_whole(name, whole)

    def pair(self, names):
        return _pair_exchange_carry([self.chips[n] for n in names])

    def take_pair(self, names, outs):
        for n, part in zip(names, outs):
            self.p16[n], self.own[n] = _pair_sum(self.chips.pop(n), part, self.idx, "pair_sum_" + n)

    def chip(self, names):
        return _chip_exchange_carry([self.p16[n] for n in names])

    def take_chip(self, names, outs):
        for n, slots in zip(names, outs):
            del self.p16[n]
            self.mine[n] = _chip_sum(self.own.pop(n), slots, self.idx, "chip_sum_" + n)

    def share(self, names):
        return _pair_share_carry([self.mine[n] for n in names])

    def take_share(self, names, outs):
        for n, s in zip(names, outs):
            del self.mine[n]
            self.final[n] = s


def _w_ada_grad(c8, dmod_cols):
    n_cols = dmod_cols.shape[1]
    tn = 512

    def body(c_ref, d_ref, o_ref):
        cv = c_ref[...]
        o_ref[...] = _tn(cv * _sigmoid(cv), d_ref[...], precision=HIGH)

    return _call(body, [c8, dmod_cols], name="w_ada_grad", grid=(n_cols // tn,),
                 in_specs=[pl.BlockSpec((8, D), lambda j: (0, 0)), pl.BlockSpec((8, tn), lambda j: (0, j))],
                 out_specs=[pl.BlockSpec((D, tn), lambda j: (0, j))],
                 out_shape=[jax.ShapeDtypeStruct((D, n_cols), F32)], sem=("parallel",))[0]


_SMALL_SEGS = (("dmod", 6144), ("norm_mix_w", 1024), ("conv_b", 3072), ("ssd_norm_w", 2048), ("pool_scale", 1024),
               ("norm_mlp_w", 1024), ("norm_final_w", 1024), ("conv_w", 4 * XBC), ("d_skip", 2048), ("a_log", 128),
               ("dt_bias", 128), ("loss", 128))
SMALL_OFF = {}
_o = 0
for _n, _s in _SMALL_SEGS:
    SMALL_OFF[_n] = _o
    _o += _s
SMALL_LEN = -(-_o // 1024) * 1024

_FIRST = ("w_in", "conv_w")
_LATER = ("w_branch_ssd", "pool_w", "w_branch_pool", "w_out", "w_up", "w_down")
_SMALL_REPLICATED = ("b_ada", "norm_mix_w", "conv_b", "dt_bias", "a_log", "d_skip", "ssd_norm_w", "pool_scale",
                     "norm_mlp_w", "norm_final_w")
_WEIGHTS = ("w_ada", "b_ada", "norm_mix_w", "w_in", "conv_w", "conv_b", "dt_bias", "a_log", "d_skip", "ssd_norm_w",
            "w_branch_ssd", "pool_w", "pool_scale", "w_branch_pool", "w_out", "norm_mlp_w", "w_up", "w_down",
            "norm_final_w")


def _shard_2d(name, a):
    if name == "conv_w":
        return a.reshape(16, -1)
    return (a.reshape(GW, GW) if name == "pool_w" else a.reshape(a.shape[-2], a.shape[-1])).astype(BF16)


def _whole_from_chips(name, g, own, chip):
    g = lax.dynamic_update_slice(g, own[None], (chip, 0, 0))
    if name == "w_in":
        a, b = _DT_IN_CHIP2, _DT_IN_CHIP2 + HEADS
        pad = jnp.zeros((D, NP - IN_COLS), g.dtype)
        return jnp.concatenate([g[0], g[1], g[2][:, :a], g[2][:, b:], g[3], g[2][:, a:b], pad], axis=1)
    if name == "w_up":
        return jnp.concatenate([g[k] for k in range(N_CHIPS)], axis=1)
    if name == "pool_w":
        return jnp.transpose(g.reshape(N_CHIPS, 4, GW // N_CHIPS, GW), (1, 0, 2, 3)).reshape(4, GW, GW)
    if name == "conv_w":
        return jnp.transpose(g.reshape(N_CHIPS, 4, XBC // N_CHIPS), (1, 0, 2)).reshape(4, XBC)
    return g.reshape(N_CHIPS * g.shape[1], g.shape[2])


def _chips_from_whole(name, g):
    if name.startswith("w_in"):
        cw, a = IN_COLS // N_CHIPS, _DT_IN_CHIP2
        chip2 = jnp.concatenate([g[:, 2 * cw:2 * cw + a], g[:, OFF_DT:OFF_DT + HEADS], g[:, 5120:3 * cw - HEADS]], axis=1)
        return jnp.stack([g[:, :cw], g[:, cw:2 * cw], chip2, g[:, 3 * cw - HEADS:OFF_DT]])
    if name == "w_up":
        return jnp.transpose(g.reshape(D, N_CHIPS, DFF // N_CHIPS), (1, 0, 2))
    if name == "pool_w":
        return jnp.transpose(g.reshape(4, N_CHIPS, GW // N_CHIPS, GW), (1, 0, 2, 3)).reshape(N_CHIPS, GW, GW)
    return g.reshape(N_CHIPS, g.shape[0] // N_CHIPS, g.shape[1])


def kernel(x, c, w_ada, b_ada, norm_mix_w, w_in, conv_w, conv_b, dt_bias, a_log, d_skip, ssd_norm_w, w_branch_ssd, pool_w, pool_scale, w_branch_pool, w_out, norm_mlp_w, w_up, w_down, norm_final_w, loss_target, m_w_ada, m_b_ada, m_norm_mix_w, m_w_in, m_conv_w, m_conv_b, m_dt_bias, m_a_log, m_d_skip, m_ssd_norm_w, m_w_branch_ssd, m_pool_w, m_pool_scale, m_w_branch_pool, m_w_out, m_norm_mlp_w, m_w_up, m_w_down, m_norm_final_w, v_w_ada, v_b_ada, v_norm_mix_w, v_w_in, v_conv_w, v_conv_b, v_dt_bias, v_a_log, v_d_skip, v_ssd_norm_w, v_w_branch_ssd, v_pool_w, v_pool_scale, v_w_branch_pool, v_w_out, v_norm_mlp_w, v_w_up, v_w_down, v_norm_final_w):
    args = locals()
    w = {n: args[n] for n in _WEIGHTS}
    m = {n: args["m_" + n] for n in _WEIGHTS}
    v = {n: args["v_" + n] for n in _WEIGHTS}
    xi, yi, ci = _me()
    chip = 2 * xi + yi
    idx = jnp.stack([ci, chip]).astype(jnp.int32)
    ada_cols = w_ada.shape[-1]
    xs, target = x[0], loss_target[0]
    two_d = lambda n, a: a.reshape(GW, GW) if n == "pool_w" else a.reshape(-1, a.shape[-1])
    delta, new_m, new_v, g = {}, {}, {}, {}

    def adamw(n, carry=None):
        res = _adamw(two_d(n, w[n]), two_d(n, g[n]), two_d(n, m[n]), two_d(n, v[n]), "adamw_" + n, carry=carry)
        (delta[n], new_m[n], new_v[n]), extra = res if carry is not None else (res, None)
        return extra

    b_mine = lax.dynamic_slice(b_ada, (0, chip * ada_cols), (1, ada_cols))
    shards = {n: _shard_2d(n, w[n]) for n in _FIRST + _LATER}
    mod, c8 = _ada_fwd(c, w_ada[0], b_mine)
    c8 = c8[:, 0, :]
    shift_m, scale_m, gate_m, shift_f, scale_f, gate_f = [mod[:, D * i:D * (i + 1)] for i in range(6)]
    nf_w = norm_final_w.reshape(1, D)

    h1, first = _norm_mod(xs, norm_mix_w, scale_m, shift_m, "norm_mod_mix",
                          carry=_gather_carry([shards[n] for n in _FIRST]))
    p ={n: _whole_from_chips(n, a, shards[n], chip) for n, a in zip(_FIRST, first)}
    (proj,), later = _matmul(h1, p["w_in"], mode="nn", out_dtypes=[F32], name="mm_proj", cols_outer=True,
                             carry=_gather_carry([shards[n] for n in _LATER]))
    p.update({n: _whole_from_chips(n, a, shards[n], chip) for n, a in zip(_LATER, later)})
    dtb_c, alog_c = dt_bias.reshape(HEADS, 1), a_log.reshape(HEADS, 1)
    dsk_exp = jnp.repeat(d_skip, HEAD_DIM, axis=1)
    y, hin, yn, xbc_a = _ssd_fwd(proj, p["conv_w"], conv_b, dt_bias, a_log, dtb_c, alog_c, dsk_exp, ssd_norm_w)
    (y_ssd,) = _matmul(yn, p["w_branch_ssd"], mode="nn", out_dtypes=[F32], name="mm_branch_ssd")
    pooled, pw_out, yps = _pool_fwd(proj, p["pool_w"], pool_scale)
    (y_pool,) = _matmul(yps, p["w_branch_pool"], mode="nn", out_dtypes=[F32], name="mm_branch_pool")
    merged = _merge(proj, y_ssd, y_pool)
    resid = lambda acc, r, gt: (r + gt * acc, acc)
    x2, mix = _matmul(merged, p["w_out"], mode="nn", out_dtypes=[F32, BF16], name="mm_out",
                      epi=resid, tile_extras=(xs,), row_extras=(gate_m,))
    h2 = _norm_mod(x2, norm_mlp_w, scale_f, shift_f, "norm_mod_mlp")
    relu2 = lambda acc: (jnp.square(jnp.maximum(acc, 0.0)),)
    (act,) = _matmul(h2, p["w_up"], mode="nn", out_dtypes=[BF16], name="mm_up", epi=relu2)
    x3, down = _matmul(act, p["w_down"], mode="nn", out_dtypes=[F32, BF16], name="mm_down",
                       epi=resid, tile_extras=(x2,), row_extras=(gate_f,))

    red = _Reducer(idx)
    dx3, d_down, sums_f = _final_loss_bwd(x3, target, nf_w, down, gate_f)
    drelu2 = lambda acc, a: (acc * (2.0 * jnp.sqrt(a)).astype(F32),)
    (dup,) = _matmul(d_down, p["w_down"], mode="nt", out_dtypes=[BF16], name="mm_dact",
                     epi=drelu2, tile_extras=(act,))
    red.add("w_down", _matmul(act, d_down, mode="tn", out_dtypes=[BF16], name="mm_g_down")[0])
    (dh2,), got = _matmul(dup, p["w_up"], mode="nt", out_dtypes=[F32], name="mm_dh2",
                          carry=red.pair(["w_down"]))
    red.take_pair(["w_down"], got)
    red.add("w_up", _matmul(h2, dup, mode="tn", out_dtypes=[BF16], name="mm_g_up", chip_blocks=True)[0], chip_blocks=True)
    dx2, sums_2, dmix = _norm_mod_bwd(x2, dh2, dx3, norm_mlp_w, scale_f, "norm_mod_mlp_bwd", branch=mix, gate=gate_m)
    (dmerged,), got = _matmul(dmix, p["w_out"], mode="nt", out_dtypes=[F32], name="mm_dmerged",
                              carry=red.pair(["w_up"]))
    red.take_pair(["w_up"], got)
    red.add("w_out", _matmul(merged, dmix, mode="tn", out_dtypes=[BF16], name="mm_g_out")[0])
    dy_ssd, dy_pool, dproj = _merge_bwd(dmerged, proj, y_ssd, y_pool)
    (dyp,), got = _matmul(dy_pool, p["w_branch_pool"], mode="nt", out_dtypes=[F32], name="mm_dyp",
                          carry=red.pair(["w_out"]))
    red.take_pair(["w_out"], got)
    red.add("w_branch_pool", _matmul(yps, dy_pool, mode="tn", out_dtypes=[BF16], name="mm_g_bpool")[0])
    dproj, g_pool_w, sums_pool = _pool_bwd(dyp, pw_out, pooled, p["pool_w"], pool_scale, dproj)
    red.add("pool_w", g_pool_w.astype(BF16))
    red.add("w_branch_ssd", _matmul(yn, dy_ssd, mode="tn", out_dtypes=[BF16], name="mm_g_bssd")[0])
    mixers = ["w_branch_pool", "pool_w", "w_branch_ssd"]
    (dyn,), got = _matmul(dy_ssd, p["w_branch_ssd"], mode="nt", out_dtypes=[F32], name="mm_dyn",
                          carry=red.pair(mixers))
    red.take_pair(mixers, got)
    six = ["w_down", "w_up", "w_out"] + mixers
    (dxa, dproj, dsk_sum, ssd_small, ddt, sums_gn), got = _ssd_bwd(
        dyn, y, xbc_a, proj, hin, dt_bias, a_log, dtb_c, alog_c, dsk_exp, ssd_norm_w, dproj, carry=red.chip(six))
    red.take_chip(six, got)
    dproj = lax.dynamic_update_slice(dproj, ddt, (0, OFF_DT))
    dproj, sums_conv = _conv_bwd(dxa, proj, p["conv_w"], conv_b, dproj)
    rows_a = 3 * D // 4
    (g_in_a,), got = _matmul(h1, dproj, mode="tn", out_dtypes=[BF16], name="mm_g_in_a", a_cols=(0, rows_a),
                             carry=red.share(six))
    red.take_share(six, got)
    red.add("w_in_a", g_in_a)
    (g_in_b,), got = _matmul(h1, dproj, mode="tn", out_dtypes=[BF16], name="mm_g_in_b", a_cols=(rows_a, D - rows_a),
                             carry=red.pair(["w_in_a"]))
    red.take_pair(["w_in_a"], got)
    red.add("w_in_b", g_in_b)
    (dh1,), got = _matmul(dproj, p["w_in"], mode="nt", out_dtypes=[F32], name="mm_dh1",
                          carry=_join(red.chip(["w_in_a"]), red.pair(["w_in_b"])))
    red.take_chip(["w_in_a"], got[:1])
    red.take_pair(["w_in_b"], got[1:])
    grad_x, sums_1 = _norm_mod_bwd(xs, dh1, dx2, norm_mix_w, scale_m, "norm_mod_mix_bwd")

    dmod = jnp.concatenate([sums_1[0:1], sums_1[1:2], sums_2[3:4], sums_2[0:1], sums_2[1:2], sums_f[1:2]], axis=1)
    pad96 = jnp.zeros((1, 96), F32)
    small = {"dmod": dmod, "norm_mix_w": sums_1[2:3], "conv_b": sums_conv[4:5], "ssd_norm_w": sums_gn[0:1],
             "pool_scale": sums_pool[0:1], "norm_mlp_w": sums_2[2:3], "norm_final_w": sums_f[0:1],
             "conv_w": sums_conv[0:4].reshape(1, 4 * XBC), "d_skip": dsk_sum[0:1],
             "a_log": jnp.concatenate([ssd_small[0:1], pad96], axis=1),
             "dt_bias": jnp.concatenate([ssd_small[1:2], pad96], axis=1), "loss": sums_f[3:4, 0:128]}
    vec = jnp.concatenate([small[n] for n, _ in _SMALL_SEGS], axis=1)
    vec = jnp.pad(vec, ((0, 0), (0, SMALL_LEN - vec.shape[1]))).reshape(SMALL_LEN // 128, 128)
    (every, total, dsk), got = _gather_small(vec, carry=_join(red.chip(["w_in_b"]), red.share(["w_in_a"])))
    red.take_chip(["w_in_b"], got[:1])
    red.take_share(["w_in_a"], got[1:])
    total = total.reshape(1, SMALL_LEN)
    seg = lambda n, size: total[:, SMALL_OFF[n]:SMALL_OFF[n] + size]
    g.update({"b_ada": seg("dmod", 6 * D), "norm_mix_w": seg("norm_mix_w", D), "conv_b": seg("conv_b", XBC),
              "dt_bias": seg("dt_bias", HEADS), "a_log": seg("a_log", HEADS), "d_skip": dsk[:, 0:2].reshape(1, HEADS),
              "ssd_norm_w": seg("ssd_norm_w", DI), "pool_scale": seg("pool_scale", D),
              "norm_mlp_w": seg("norm_mlp_w", D), "norm_final_w": seg("norm_final_w", D)})
    loss = total[0, SMALL_OFF["loss"]]
    conv_cols = conv_w.shape[-1]
    g["conv_w"] = lax.dynamic_slice(seg("conv_w", 4 * XBC).reshape(4, XBC), (0, chip * conv_cols), (4, conv_cols))
    dmod8 = every.reshape(8, SMALL_LEN)[:, SMALL_OFF["dmod"]:SMALL_OFF["dmod"] + 6 * D]
    g["w_ada"] = _w_ada_grad(c8, lax.dynamic_slice(dmod8, (0, chip * ada_cols), (8, ada_cols)))

    got = adamw("w_ada", carry=red.share(["w_in_b"]))
    red.take_share(["w_in_b"], got)
    for n in six:
        g[n] = red.final[n]
    g["w_in"] = jnp.concatenate([red.final["w_in_a"], red.final["w_in_b"]], axis=0)
    for n in ["conv_w", "w_in"] + six:
        adamw(n)
    sizes = [w[n].size for n in _SMALL_REPLICATED]
    n_small = -(-sum(sizes) // 1024) * 1024
    pack = lambda d: jnp.pad(jnp.concatenate([d[n].reshape(1, -1) for n in _SMALL_REPLICATED], axis=1),
                             ((0, 0), (0, n_small - sum(sizes)))).reshape(n_small // 128, 128)
    d_, m_, v_ = _adamw(pack(w), pack(g), pack(m), pack(v), "adamw_small")
    off = 0
    for n, s in zip(_SMALL_REPLICATED, sizes):
        for dst, src in ((delta, d_), (new_m, m_), (new_v, v_)):
            dst[n] = src.reshape(1, n_small)[:, off:off + s]
        off += s

    out = [loss, grad_x.reshape(x.shape)]
    for d in (g, delta, new_m, new_v):
        out += [d[n].reshape(w[n].shape) for n in _WEIGHTS]
    return tuple(out)
```
